```python
import jax, jax.numpy as jnp
from jax import lax
import numpy as np

D_MODEL = 2048
BATCH = 8
SEQ = 2048
DEPTH = 1

W_CONV = D_MODEL // 2
CONV_GROUPS = 8
W_MLSTM = D_MODEL - W_CONV
N_HEADS_M = 4
HEAD_DIM_M = W_MLSTM // N_HEADS_M
CONV_WIDTH = 3
CHUNK = 64
D_FF = 5632
LN_EPS = 1e-5
HEAD_NORM_EPS = 1e-6
N_IN = 3 * W_CONV + 4 * W_MLSTM + 2 * N_HEADS_M

kernel_name = "hybrid_shortconv_mlstm_convffn_deepnorm"


def causal_dwconv(u, w, b=None):
    s = u.shape[1]
    up = jnp.pad(u, ((0, 0), (CONV_WIDTH - 1, 0), (0, 0)))
    y = w[0] * up[:, 0:s]
    for j in range(1, CONV_WIDTH):
        y = y + w[j] * up[:, j:j + s]
    return y if b is None else y + b


def layer_norm(x, g, b):
    xf = x.astype(jnp.float32)
    mu = jnp.mean(xf, axis=-1, keepdims=True)
    var = jnp.mean(jnp.square(xf - mu), axis=-1, keepdims=True)
    y = (xf - mu) * lax.rsqrt(var + LN_EPS) * g.astype(jnp.float32) + b.astype(jnp.float32)
    return y.astype(x.dtype)


def head_layer_norm(h, g):
    mu = jnp.mean(h, axis=-1, keepdims=True)
    var = jnp.mean(jnp.square(h - mu), axis=-1, keepdims=True)
    return (h - mu) * lax.rsqrt(var + HEAD_NORM_EPS) * g.astype(jnp.float32)


def mlstm_chunkwise(q, k, v, i_pre, f_pre):
    f32 = jnp.float32
    bsz, s, nh, dh = q.shape
    nc = s // CHUNK

    def to_chunks(t):
        return t.astype(f32).reshape(bsz, nc, CHUNK, nh, dh).transpose(1, 0, 3, 2, 4)

    def gate_chunks(t):
        return t.astype(f32).reshape(bsz, nc, CHUNK, nh).transpose(1, 0, 3, 2)

    qc = to_chunks(q) * (dh ** -0.5)
    kc = to_chunks(k)
    vc = to_chunks(v)
    ic = gate_chunks(i_pre)
    lfc = gate_chunks(jax.nn.log_sigmoid(f_pre.astype(f32)))
    causal = jnp.tril(jnp.ones((CHUNK, CHUNK), dtype=bool))

    def step(carry, xs):
        c_state, n_state, m_state = carry
        qb, kb, vb, ib, lfb = xs
        bcum = jnp.cumsum(lfb, axis=-1)
        g_tot = bcum[..., -1]
        log_d = bcum[..., :, None] - bcum[..., None, :] + ib[..., None, :]
        log_d = jnp.where(causal, log_d, -jnp.inf)
        inter_log = bcum + m_state[..., None]
        m_t = jnp.maximum(inter_log, jnp.max(log_d, axis=-1))
        d_w = jnp.exp(log_d - m_t[..., None])
        scores = jnp.einsum('bhtd,bhsd->bhts', qb, kb) * d_w
        inter_w = jnp.exp(inter_log - m_t)
        num = (jnp.einsum('bhts,bhse->bhte', scores, vb)
               + inter_w[..., None] * jnp.einsum('bhtd,bhde->bhte', qb, c_state))
        den = jnp.sum(scores, axis=-1) + inter_w * jnp.einsum('bhtd,bhd->bht', qb, n_state)
        h = num / jnp.maximum(jnp.abs(den), jnp.exp(-m_t))[..., None]
        w_log = g_tot[..., None] - bcum + ib
        m_new = jnp.maximum(g_tot + m_state, jnp.max(w_log, axis=-1))
        w = jnp.exp(w_log - m_new[..., None])
        decay = jnp.exp(g_tot + m_state - m_new)
        c_new = decay[..., None, None] * c_state + jnp.einsum('bhs,bhsd,bhse->bhde', w, kb, vb)
        n_new = decay[..., None] * n_state + jnp.einsum('bhs,bhsd->bhd', w, kb)
        return (c_new, n_new, m_new), h

    init = (jnp.zeros((bsz, nh, dh, dh), f32),
            jnp.zeros((bsz, nh, dh), f32),
            jnp.zeros((bsz, nh), f32))
    _, h = lax.scan(step, init, (qc, kc, vc, ic, lfc))
    return h.transpose(1, 0, 3, 2, 4).reshape(bsz, s, nh, dh)


def hybrid_mixer(x, w_in, b_gates, w_sc_conv, mh_gain, w_out):
    bsz, s, _ = x.shape
    proj = jnp.einsum('bsd,dn->bsn', x, w_in)
    splits = [W_CONV, 2 * W_CONV, 3 * W_CONV,
              3 * W_CONV + W_MLSTM, 3 * W_CONV + 2 * W_MLSTM,
              3 * W_CONV + 3 * W_MLSTM, 3 * W_CONV + 4 * W_MLSTM]
    c_b, c_c, c_h, q, k, v, o, gates = jnp.split(proj, splits, axis=-1)
    y_conv = c_b * causal_dwconv(c_c * c_h, w_sc_conv)
    gates = gates + b_gates
    i_pre, f_pre = gates[..., :N_HEADS_M], gates[..., N_HEADS_M:]
    hd = (bsz, s, N_HEADS_M, HEAD_DIM_M)
    h = mlstm_chunkwise(q.reshape(hd), k.reshape(hd), v.reshape(hd), i_pre, f_pre)
    h = head_layer_norm(h, mh_gain.reshape(N_HEADS_M, HEAD_DIM_M))
    h = jax.nn.sigmoid(o.astype(jnp.float32)).reshape(hd) * h
    y_mlstm = h.reshape(bsz, s, W_MLSTM).astype(x.dtype)
    y = jnp.concatenate([y_conv, y_mlstm], axis=-1)
    return jnp.einsum('bsd,de->bse', y, w_out)


def conv_ffn(x, w_up, w_ffn_conv, b_ffn_conv, w_down):
    hid = jnp.einsum('bsd,df->bsf', x, w_up)
    hid = causal_dwconv(hid, w_ffn_conv, b_ffn_conv)
    val, gate = hid[..., :D_FF], hid[..., D_FF:]
    return jnp.einsum('bsf,fd->bsd', jax.nn.silu(gate) * val, w_down)


def _fwd_setup_inputs(seed: int = 0) -> dict:
    key = jax.random.key(seed)
    ks = jax.random.split(key, 16)
    beta = (8 * DEPTH) ** -0.25
    nrm = jax.random.normal
    x = nrm(ks[0], (BATCH, SEQ, D_MODEL), jnp.float32)
    col_scale = jnp.ones((N_IN,), jnp.float32)
    col_scale = col_scale.at[2 * W_CONV:3 * W_CONV].set(beta)
    col_scale = col_scale.at[3 * W_CONV + 2 * W_MLSTM:3 * W_CONV + 3 * W_MLSTM].set(beta)
    w_in = nrm(ks[1], (DEPTH, D_MODEL, N_IN), jnp.float32) * (D_MODEL ** -0.5) * col_scale
    b_i = 0.1 * nrm(ks[2], (DEPTH, N_HEADS_M), jnp.float32)
    b_f = jnp.linspace(3.0, 6.0, N_HEADS_M, dtype=jnp.float32) + 0.1 * nrm(ks[3], (DEPTH, N_HEADS_M), jnp.float32)
    b_gates = jnp.concatenate([b_i, b_f], axis=-1)
    w_sc_conv = nrm(ks[4], (DEPTH, CONV_WIDTH, W_CONV), jnp.float32) * (CONV_WIDTH ** -0.5)
    mh_gain = 1.0 + 0.02 * nrm(ks[5], (DEPTH, W_MLSTM), jnp.float32)
    w_out = nrm(ks[6], (DEPTH, D_MODEL, D_MODEL), jnp.float32) * (D_MODEL ** -0.5) * beta
    ln1_g = 1.0 + 0.02 * nrm(ks[7], (DEPTH, D_MODEL), jnp.float32)
    ln1_b = 0.02 * nrm(ks[8], (DEPTH, D_MODEL), jnp.float32)
    w_up = nrm(ks[9], (DEPTH, D_MODEL, 2 * D_FF), jnp.float32) * (D_MODEL ** -0.5)
    w_ffn_conv = nrm(ks[10], (DEPTH, CONV_WIDTH, 2 * D_FF), jnp.float32) * (CONV_WIDTH ** -0.5)
    b_ffn_conv = 0.02 * nrm(ks[11], (DEPTH, 2 * D_FF), jnp.float32)
    w_down = nrm(ks[12], (DEPTH, D_FF, D_MODEL), jnp.float32) * (D_FF ** -0.5) * beta
    ln2_g = 1.0 + 0.02 * nrm(ks[13], (DEPTH, D_MODEL), jnp.float32)
    ln2_b = 0.02 * nrm(ks[14], (DEPTH, D_MODEL), jnp.float32)
    return {"x": x, "w_in": w_in, "b_gates": b_gates, "w_sc_conv": w_sc_conv,
            "mh_gain": mh_gain, "w_out": w_out, "ln1_g": ln1_g, "ln1_b": ln1_b,
            "w_up": w_up, "w_ffn_conv": w_ffn_conv, "b_ffn_conv": b_ffn_conv,
            "w_down": w_down, "ln2_g": ln2_g, "ln2_b": ln2_b}


def _fwd_reference(x, w_in, b_gates, w_sc_conv, mh_gain, w_out, ln1_g, ln1_b,
              w_up, w_ffn_conv, b_ffn_conv, w_down, ln2_g, ln2_b):
    alpha = (2 * DEPTH) ** 0.25
    for l in range(DEPTH):
        mix = hybrid_mixer(x, w_in[l], b_gates[l], w_sc_conv[l], mh_gain[l], w_out[l])
        x = layer_norm(alpha * x + mix, ln1_g[l], ln1_b[l])
        ff = conv_ffn(x, w_up[l], w_ffn_conv[l], b_ffn_conv[l], w_down[l])
        x = layer_norm(alpha * x + ff, ln2_g[l], ln2_b[l])
    return x


import jax as _jax
import jax.numpy as _jnp

TWIN_FORMAT = 'train_step'
FWD_PARAMS = ['x', 'w_in', 'b_gates', 'w_sc_conv', 'mh_gain', 'w_out', 'ln1_g', 'ln1_b', 'w_up', 'w_ffn_conv', 'b_ffn_conv', 'w_down', 'ln2_g', 'ln2_b']
TWIN_WEIGHTS = ['w_in', 'b_gates', 'w_sc_conv', 'mh_gain', 'w_out', 'ln1_g', 'ln1_b', 'w_up', 'w_ffn_conv', 'b_ffn_conv', 'w_down', 'ln2_g', 'ln2_b']
TWIN_DIFF_INPUT = 'x'
TWIN_INPUTS = ['x', 'w_in', 'b_gates', 'w_sc_conv', 'mh_gain', 'w_out', 'ln1_g', 'ln1_b', 'w_up', 'w_ffn_conv', 'b_ffn_conv', 'w_down', 'ln2_g', 'ln2_b', 'loss_target', 'm_w_in', 'm_b_gates', 'm_w_sc_conv', 'm_mh_gain', 'm_w_out', 'm_ln1_g', 'm_ln1_b', 'm_w_up', 'm_w_ffn_conv', 'm_b_ffn_conv', 'm_w_down', 'm_ln2_g', 'm_ln2_b', 'v_w_in', 'v_b_gates', 'v_w_sc_conv', 'v_mh_gain', 'v_w_out', 'v_ln1_g', 'v_ln1_b', 'v_w_up', 'v_w_ffn_conv', 'v_b_ffn_conv', 'v_w_down', 'v_ln2_g', 'v_ln2_b']
TWIN_OUTPUTS = ['loss', 'grad_x', 'grad_w_in', 'grad_b_gates', 'grad_w_sc_conv', 'grad_mh_gain', 'grad_w_out', 'grad_ln1_g', 'grad_ln1_b', 'grad_w_up', 'grad_w_ffn_conv', 'grad_b_ffn_conv', 'grad_w_down', 'grad_ln2_g', 'grad_ln2_b', 'delta_w_in', 'delta_b_gates', 'delta_w_sc_conv', 'delta_mh_gain', 'delta_w_out', 'delta_ln1_g', 'delta_ln1_b', 'delta_w_up', 'delta_w_ffn_conv', 'delta_b_ffn_conv', 'delta_w_down', 'delta_ln2_g', 'delta_ln2_b', 'new_m_w_in', 'new_m_b_gates', 'new_m_w_sc_conv', 'new_m_mh_gain', 'new_m_w_out', 'new_m_ln1_g', 'new_m_ln1_b', 'new_m_w_up', 'new_m_w_ffn_conv', 'new_m_b_ffn_conv', 'new_m_w_down', 'new_m_ln2_g', 'new_m_ln2_b', 'new_v_w_in', 'new_v_b_gates', 'new_v_w_sc_conv', 'new_v_mh_gain', 'new_v_w_out', 'new_v_ln1_g', 'new_v_ln1_b', 'new_v_w_up', 'new_v_w_ffn_conv', 'new_v_b_ffn_conv', 'new_v_w_down', 'new_v_ln2_g', 'new_v_ln2_b']
TWIN_LEAF_KINDS = {'loss': 'loss', 'grad_x': 'grad_x', 'grad_w_in': 'grad_w', 'grad_b_gates': 'grad_w', 'grad_w_sc_conv': 'grad_w', 'grad_mh_gain': 'grad_w', 'grad_w_out': 'grad_w', 'grad_ln1_g': 'grad_w', 'grad_ln1_b': 'grad_w', 'grad_w_up': 'grad_w', 'grad_w_ffn_conv': 'grad_w', 'grad_b_ffn_conv': 'grad_w', 'grad_w_down': 'grad_w', 'grad_ln2_g': 'grad_w', 'grad_ln2_b': 'grad_w', 'delta_w_in': 'delta_w', 'delta_b_gates': 'delta_w', 'delta_w_sc_conv': 'delta_w', 'delta_mh_gain': 'delta_w', 'delta_w_out': 'delta_w', 'delta_ln1_g': 'delta_w', 'delta_ln1_b': 'delta_w', 'delta_w_up': 'delta_w', 'delta_w_ffn_conv': 'delta_w', 'delta_b_ffn_conv': 'delta_w', 'delta_w_down': 'delta_w', 'delta_ln2_g': 'delta_w', 'delta_ln2_b': 'delta_w', 'new_m_w_in': 'new_m', 'new_m_b_gates': 'new_m', 'new_m_w_sc_conv': 'new_m', 'new_m_mh_gain': 'new_m', 'new_m_w_out': 'new_m', 'new_m_ln1_g': 'new_m', 'new_m_ln1_b': 'new_m', 'new_m_w_up': 'new_m', 'new_m_w_ffn_conv': 'new_m', 'new_m_b_ffn_conv': 'new_m', 'new_m_w_down': 'new_m', 'new_m_ln2_g': 'new_m', 'new_m_ln2_b': 'new_m', 'new_v_w_in': 'new_v', 'new_v_b_gates': 'new_v', 'new_v_w_sc_conv': 'new_v', 'new_v_mh_gain': 'new_v', 'new_v_w_out': 'new_v', 'new_v_ln1_g': 'new_v', 'new_v_ln1_b': 'new_v', 'new_v_w_up': 'new_v', 'new_v_w_ffn_conv': 'new_v', 'new_v_b_ffn_conv': 'new_v', 'new_v_w_down': 'new_v', 'new_v_ln2_g': 'new_v', 'new_v_ln2_b': 'new_v'}


def _forward(args):
    return _fwd_reference(*[args[k] for k in FWD_PARAMS])


def _output_shape():
    out = _jax.eval_shape(lambda: _forward(_fwd_setup_inputs(0)))
    return out.shape, out.dtype

N_MICROBATCH = 1
ADAM_LR = 0.001
ADAM_B1 = 0.9
ADAM_B2 = 0.999
ADAM_EPS = 1e-08
ADAM_WD = 0.01
ADAM_STEP = 10
PER_EXAMPLE_BATCH_AXIS = {'x': 0, 'loss_target': 0}
SHARED_INPUTS = []
_WEIGHT_DTYPES = {'w_in': _jnp.float32, 'b_gates': _jnp.float32, 'w_sc_conv': _jnp.float32, 'mh_gain': _jnp.float32, 'w_out': _jnp.float32, 'ln1_g': _jnp.float32, 'ln1_b': _jnp.float32, 'w_up': _jnp.float32, 'w_ffn_conv': _jnp.float32, 'b_ffn_conv': _jnp.float32, 'w_down': _jnp.float32, 'ln2_g': _jnp.float32, 'ln2_b': _jnp.float32}
MOMENT_SCALE = {'w_in': 2.239999e-02, 'b_gates': 8.141281e-02, 'w_sc_conv': 1.946363e-02, 'mh_gain': 1.667971e-02, 'w_out': 3.023642e-02, 'ln1_g': 2.630088e-01, 'ln1_b': 1.422252e-01, 'w_up': 1.132334e-02, 'w_ffn_conv': 1.119952e-02, 'b_ffn_conv': 1.226610e-02, 'w_down': 3.114399e-02, 'ln2_g': 8.000159e+00, 'ln2_b': 1.916235e-01}


def _to_microbatches(a, axis):
    t = _jnp.moveaxis(a, axis, 0)
    t = t.reshape((N_MICROBATCH, t.shape[0] // N_MICROBATCH) + t.shape[1:])
    return _jnp.moveaxis(t, 1, axis + 1)


def setup_inputs(seed: int = 0) -> dict:
    inp = _fwd_setup_inputs(seed)
    key = _jax.random.fold_in(_jax.random.key(seed), 7919)
    shape, _ = _output_shape()
    out = dict(inp)
    out["loss_target"] = _jax.random.normal(_jax.random.fold_in(key, 0), shape, _jnp.float32)
    for i, name in enumerate(TWIN_WEIGHTS):
        w = inp[name].astype(_jnp.float32)
        if MOMENT_SCALE is None:
            s = _jnp.sqrt(_jnp.mean(_jnp.square(w)) + 1e-30)
        else:
            s = MOMENT_SCALE[name]
        km, kv = _jax.random.split(_jax.random.fold_in(key, i + 1))
        out[name] = w
        out["m_" + name] = s * _jax.random.normal(km, w.shape, _jnp.float32)
        out["v_" + name] = (s * s) * _jax.random.uniform(kv, w.shape, _jnp.float32, 0.5, 1.5)
    if N_MICROBATCH > 1:
        for name, axis in PER_EXAMPLE_BATCH_AXIS.items():
            out[name] = _to_microbatches(out[name], axis)
    return {'x': out['x'], 'w_in': out['w_in'], 'b_gates': out['b_gates'], 'w_sc_conv': out['w_sc_conv'], 'mh_gain': out['mh_gain'], 'w_out': out['w_out'], 'ln1_g': out['ln1_g'], 'ln1_b': out['ln1_b'], 'w_up': out['w_up'], 'w_ffn_conv': out['w_ffn_conv'], 'b_ffn_conv': out['b_ffn_conv'], 'w_down': out['w_down'], 'ln2_g': out['ln2_g'], 'ln2_b': out['ln2_b'], 'loss_target': out['loss_target'], 'm_w_in': out['m_w_in'], 'm_b_gates': out['m_b_gates'], 'm_w_sc_conv': out['m_w_sc_conv'], 'm_mh_gain': out['m_mh_gain'], 'm_w_out': out['m_w_out'], 'm_ln1_g': out['m_ln1_g'], 'm_ln1_b': out['m_ln1_b'], 'm_w_up': out['m_w_up'], 'm_w_ffn_conv': out['m_w_ffn_conv'], 'm_b_ffn_conv': out['m_b_ffn_conv'], 'm_w_down': out['m_w_down'], 'm_ln2_g': out['m_ln2_g'], 'm_ln2_b': out['m_ln2_b'], 'v_w_in': out['v_w_in'], 'v_b_gates': out['v_b_gates'], 'v_w_sc_conv': out['v_w_sc_conv'], 'v_mh_gain': out['v_mh_gain'], 'v_w_out': out['v_w_out'], 'v_ln1_g': out['v_ln1_g'], 'v_ln1_b': out['v_ln1_b'], 'v_w_up': out['v_w_up'], 'v_w_ffn_conv': out['v_w_ffn_conv'], 'v_b_ffn_conv': out['v_b_ffn_conv'], 'v_w_down': out['v_w_down'], 'v_ln2_g': out['v_ln2_g'], 'v_ln2_b': out['v_ln2_b']}


def _loss(weights, diff, rest, loss_target):
    with _jax.named_scope("forward"):
        args = {**rest, TWIN_DIFF_INPUT: diff, **{k: w.astype(_WEIGHT_DTYPES[k]) for k, w in weights.items()}}
        y = _forward(args)
    with _jax.named_scope("loss_head"):
        err = _jnp.square(y.astype(_jnp.float32) - loss_target)
        return 0.5 * _jnp.sum(_jnp.mean(err, axis=-1)) if err.ndim else 0.5 * err


def _adamw(w, g, m, v):
    m = ADAM_B1 * m + (1.0 - ADAM_B1) * g
    v = ADAM_B2 * v + (1.0 - ADAM_B2) * _jnp.square(g)
    m_hat = m / (1.0 - ADAM_B1 ** ADAM_STEP)
    v_hat = v / (1.0 - ADAM_B2 ** ADAM_STEP)
    delta = -ADAM_LR * (m_hat / (_jnp.sqrt(v_hat) + ADAM_EPS) + ADAM_WD * w)
    return delta, m, v


def reference(x, w_in, b_gates, w_sc_conv, mh_gain, w_out, ln1_g, ln1_b, w_up, w_ffn_conv, b_ffn_conv, w_down, ln2_g, ln2_b, loss_target, m_w_in, m_b_gates, m_w_sc_conv, m_mh_gain, m_w_out, m_ln1_g, m_ln1_b, m_w_up, m_w_ffn_conv, m_b_ffn_conv, m_w_down, m_ln2_g, m_ln2_b, v_w_in, v_b_gates, v_w_sc_conv, v_mh_gain, v_w_out, v_ln1_g, v_ln1_b, v_w_up, v_w_ffn_conv, v_b_ffn_conv, v_w_down, v_ln2_g, v_ln2_b):
    given = dict(x=x, w_in=w_in, b_gates=b_gates, w_sc_conv=w_sc_conv, mh_gain=mh_gain, w_out=w_out, ln1_g=ln1_g, ln1_b=ln1_b, w_up=w_up, w_ffn_conv=w_ffn_conv, b_ffn_conv=b_ffn_conv, w_down=w_down, ln2_g=ln2_g, ln2_b=ln2_b, loss_target=loss_target, m_w_in=m_w_in, m_b_gates=m_b_gates, m_w_sc_conv=m_w_sc_conv, m_mh_gain=m_mh_gain, m_w_out=m_w_out, m_ln1_g=m_ln1_g, m_ln1_b=m_ln1_b, m_w_up=m_w_up, m_w_ffn_conv=m_w_ffn_conv, m_b_ffn_conv=m_b_ffn_conv, m_w_down=m_w_down, m_ln2_g=m_ln2_g, m_ln2_b=m_ln2_b, v_w_in=v_w_in, v_b_gates=v_b_gates, v_w_sc_conv=v_w_sc_conv, v_mh_gain=v_mh_gain, v_w_out=v_w_out, v_ln1_g=v_ln1_g, v_ln1_b=v_ln1_b, v_w_up=v_w_up, v_w_ffn_conv=v_w_ffn_conv, v_b_ffn_conv=v_b_ffn_conv, v_w_down=v_w_down, v_ln2_g=v_ln2_g, v_ln2_b=v_ln2_b)
    weights = {n: given[n] for n in TWIN_WEIGHTS}
    shared = {n: given[n] for n in SHARED_INPUTS}
    per_example = {n: given[n] for n in ['x']}
    grad_fn = _jax.value_and_grad(_loss, argnums=(0, 1))

    def one_microbatch(ex, loss_target):
        ex = dict(ex)
        diff = ex.pop(TWIN_DIFF_INPUT)
        return grad_fn(weights, diff, {**shared, **ex}, loss_target)

    if N_MICROBATCH == 1:
        loss, (grad_w, grad_x) = one_microbatch(per_example, given["loss_target"])
    else:
        def body(carry, xs):
            loss_sum, grad_sum = carry
            l_k, (gw_k, gx_k) = one_microbatch(xs[0], xs[1])
            with _jax.named_scope("update"):
                return (loss_sum + l_k, _jax.tree.map(_jnp.add, grad_sum, gw_k)), gx_k

        init = (_jnp.zeros((), _jnp.float32), _jax.tree.map(_jnp.zeros_like, weights))
        (loss, grad_w), grad_x = _jax.lax.scan(body, init, (per_example, given["loss_target"]))
    with _jax.named_scope("update"):
        delta_w, new_m, new_v = {}, {}, {}
        for n in TWIN_WEIGHTS:
            delta_w[n], new_m[n], new_v[n] = _adamw(weights[n], grad_w[n], given["m_" + n], given["v_" + n])
    return (loss, grad_x, *[grad_w[n] for n in TWIN_WEIGHTS], *[delta_w[n] for n in TWIN_WEIGHTS],
            *[new_m[n] for n in TWIN_WEIGHTS], *[new_v[n] for n in TWIN_WEIGHTS])
```

```python
import functools

import jax
import jax.numpy as jnp
from jax import lax
from jax.experimental import pallas as pl
from jax.experimental.pallas import tpu as pltpu

F32 = jnp.float32
BF16 = jnp.bfloat16
MESH = pl.DeviceIdType.MESH

N_DEV = 8
NH = 4
CHUNK = 64
LN_EPS = 1e-5
HN_EPS = 1e-6
ALPHA = 2.0 ** 0.25
LANE = 128
IN_SLAB = 7 * LANE
VMEM_LIMIT = 56 * 1024 * 1024
ADAM_LR, ADAM_B1, ADAM_B2, ADAM_EPS, ADAM_WD, ADAM_STEP = 0.001, 0.9, 0.999, 1e-08, 0.01, 10

_NN = (((1,), (0,)), ((), ()))
_NT = (((1,), (1,)), ((), ()))
_TN = (((0,), (0,)), ((), ()))


def _dot(a, b, dn=_NN):
    return lax.dot_general(a, b, dn, preferred_element_type=F32)


def _params(*sem):
    return pltpu.CompilerParams(dimension_semantics=sem if sem else None, vmem_limit_bytes=VMEM_LIMIT)


def _iota(shape, axis):
    return lax.broadcasted_iota(jnp.int32, shape, axis)


def _fit(n, want):
    if n <= want:
        return n
    t = want - want % LANE
    while n % t:
        t -= LANE
    return t


def _matmul(a, b, mode, out_dtype, name, tm=1024, tn=512, tk=1024, add=None, add_scale=1.0,
            b_blocked=False, o_width=None):
    if mode == "tn":
        kd, m = a.shape
    else:
        m, kd = a.shape
    if b_blocked:
        nb, rows, w = b.shape
        n = nb * w if mode == "nn" else rows
        assert (nb * w if mode == "nt" else rows) == kd, (name, b.shape, kd)
    else:
        n = b.shape[0] if mode == "nt" else b.shape[1]
    tm, tn, tk = _fit(m, tm), _fit(n, tn), _fit(kd, tk)
    if b_blocked and mode == "nn":
        tn = _fit(w, tn)
    if b_blocked and mode == "nt":
        tk = _fit(w, tk)
    if o_width is not None:
        tn = _fit(o_width, tn)
    assert m % tm == 0 and n % tn == 0 and kd % tk == 0, (name, m, n, kd, tm, tn, tk)
    nk = kd // tk
    dn = {"nn": _NN, "nt": _NT, "tn": _TN}[mode]
    a_spec = (pl.BlockSpec((tk, tm), lambda i, j, k: (k, i)) if mode == "tn"
              else pl.BlockSpec((tm, tk), lambda i, j, k: (i, k)))
    if b_blocked and mode == "nn":
        per = w // tn
        b_spec = pl.BlockSpec((None, tk, tn), lambda i, j, k: (j // per, k, j % per))
    elif b_blocked:
        per = w // tk
        b_spec = pl.BlockSpec((None, tn, tk), lambda i, j, k: (k // per, j, k % per))
    elif mode == "nt":
        b_spec = pl.BlockSpec((tn, tk), lambda i, j, k: (j, k))
    else:
        b_spec = pl.BlockSpec((tk, tn), lambda i, j, k: (k, j))
    if o_width is None:
        o_spec = pl.BlockSpec((tm, tn), lambda i, j, k: (i, j))
        o_shape = (m, n)
    else:
        oper = o_width // tn
        o_spec = pl.BlockSpec((None, tm, tn), lambda i, j, k: (j // oper, i, j % oper))
        o_shape = (n // o_width, m, o_width)
    has_add = add is not None

    def body(*refs):
        if has_add:
            a_ref, b_ref, add_ref, o_ref = refs[:4]
        else:
            a_ref, b_ref, o_ref = refs[:3]

        def finish(r):
            if has_add:
                r = r + add_scale * add_ref[...]
            o_ref[...] = r.astype(out_dtype)

        if nk == 1:
            finish(_dot(a_ref[...], b_ref[...], dn))
        else:
            acc = refs[-1]
            k = pl.program_id(2)

            @pl.when(k == 0)
            def _():
                acc[...] = jnp.zeros_like(acc)

            acc[...] += _dot(a_ref[...], b_ref[...], dn)

            @pl.when(k == nk - 1)
            def _():
                finish(acc[...])

    in_specs = [a_spec, b_spec] + ([pl.BlockSpec((tm, tn), lambda i, j, k: (i, j))] if has_add else [])
    args = (a, b) + ((add,) if has_add else ())
    return pl.pallas_call(
        body, name=name, grid=(m // tm, n // tn, nk),
        in_specs=in_specs, out_specs=o_spec,
        out_shape=jax.ShapeDtypeStruct(o_shape, out_dtype),
        scratch_shapes=[pltpu.VMEM((tm, tn), F32)] if nk > 1 else [],
        compiler_params=_params("parallel", "parallel", "arbitrary"),
    )(*args)


def _shift_down(u, s):
    return jnp.where(_iota(u.shape, 0) >= s, pltpu.roll(u, s, 0), 0.0)


def _shift_up(u, s):
    t = u.shape[0]
    return jnp.where(_iota(u.shape, 0) < t - s, pltpu.roll(u, t - s, 0), 0.0)


def _conv(u, w):
    return w[0:1] * _shift_down(u, 2) + w[1:2] * _shift_down(u, 1) + w[2:3] * u


def _conv_t(dy, w):
    return w[2:3] * dy + w[1:2] * _shift_up(dy, 1) + w[0:1] * _shift_up(dy, 2)


def _conv_dw(dy, u):
    d0 = jnp.sum(dy * _shift_down(u, 2), axis=0, keepdims=True)
    d1 = jnp.sum(dy * _shift_down(u, 1), axis=0, keepdims=True)
    d2 = jnp.sum(dy * u, axis=0, keepdims=True)
    r = _iota((3, dy.shape[1]), 0)
    return jnp.where(r == 0, d0, jnp.where(r == 1, d1, d2))


def _sigmoid(x):
    return 1.0 / (1.0 + jnp.exp(-x))


def _sconv_fwd(proj, w_sc, t, wc):
    nb = wc // LANE

    def body(cb_ref, cc_ref, ch_ref, w_ref, y_ref):
        u = cc_ref[...] * ch_ref[...]
        y_ref[...] = (cb_ref[...] * _conv(u, w_ref[...])).astype(BF16)

    col = lambda off: pl.BlockSpec((t, LANE), lambda j: (0, j + off))
    return pl.pallas_call(
        body, name="sconv_fwd", grid=(nb,),
        in_specs=[col(0), col(nb), col(2 * nb), pl.BlockSpec((3, LANE), lambda j: (0, j))],
        out_specs=pl.BlockSpec((t, LANE), lambda j: (0, j)),
        out_shape=jax.ShapeDtypeStruct((t, wc), BF16),
        compiler_params=_params("parallel"),
    )(proj, proj, proj, w_sc)


def _sconv_bwd(dy, proj, w_sc, t, wc):
    nb = wc // LANE

    def body(dy_ref, cb_ref, cc_ref, ch_ref, w_ref, dcb_ref, dcc_ref, dch_ref, dw_ref):
        cc, ch, w, d = cc_ref[...], ch_ref[...], w_ref[...], dy_ref[...]
        u = cc * ch
        dcb_ref[...] = (d * _conv(u, w)).astype(BF16)
        dcu = d * cb_ref[...]
        dw_ref[...] = _conv_dw(dcu, u)
        du = _conv_t(dcu, w)
        dcc_ref[...] = (du * ch).astype(BF16)
        dch_ref[...] = (du * cc).astype(BF16)

    col = lambda off: pl.BlockSpec((t, LANE), lambda j: (0, j + off))
    act = jax.ShapeDtypeStruct((t, wc), BF16)
    return pl.pallas_call(
        body, name="sconv_bwd", grid=(nb,),
        in_specs=[col(0), col(0), col(nb), col(2 * nb), pl.BlockSpec((3, LANE), lambda j: (0, j))],
        out_specs=[col(0), col(0), col(0), pl.BlockSpec((3, LANE), lambda j: (0, j))],
        out_shape=[act, act, act, jax.ShapeDtypeStruct((3, wc), F32)],
        compiler_params=_params("parallel"),
    )(dy, proj, proj, proj, w_sc)


def _gates_prep(proj, bias_tile, t, gate_tile):
    def body(g_ref, b_ref, o_ref):
        g = g_ref[...] + b_ref[...]
        lane = _iota(g.shape, 1)
        is_f = (lane >= NH) & (lane < 2 * NH)
        lf = jnp.minimum(g, 0.0) - jnp.log(1.0 + jnp.exp(-jnp.abs(g)))
        c = jnp.where(is_f, lf, 0.0)
        r = _iota(g.shape, 0) % CHUNK
        s = 1
        while s < CHUNK:
            c = c + jnp.where(r >= s, pltpu.roll(c, s, 0), 0.0)
            s *= 2
        o_ref[...] = jnp.where(is_f, c, jnp.where(lane < NH, g, 0.0))

    return pl.pallas_call(
        body, name="gates_prep", grid=(1,),
        in_specs=[pl.BlockSpec((t, LANE), lambda i: (0, gate_tile)), pl.BlockSpec((1, LANE), lambda i: (0, 0))],
        out_specs=pl.BlockSpec((t, LANE), lambda i: (0, 0)),
        out_shape=jax.ShapeDtypeStruct((t, LANE), F32),
        compiler_params=_params("arbitrary"),
    )(proj, bias_tile)


def _gates_bwd(dgate, proj, bias_tile, t, gate_tile):
    def body(dg_ref, g_ref, b_ref, o_ref, s_ref):
        g = g_ref[...] + b_ref[...]
        lane = _iota(g.shape, 1)
        r = _iota(g.shape, 0) % CHUNK
        dsig = 1.0 - _sigmoid(g)
        out = jnp.zeros(g.shape, F32)
        for h in range(NH):
            d = dg_ref[h]
            c = d
            s = 1
            while s < CHUNK:
                c = c + jnp.where(r + s < CHUNK, pltpu.roll(c, t - s, 0), 0.0)
                s *= 2
            di = jnp.broadcast_to(d[:, 0:1], g.shape)
            db = jnp.broadcast_to(c[:, 1:2], g.shape)
            out = out + jnp.where(lane == h, di, 0.0) + jnp.where(lane == NH + h, db * dsig, 0.0)
        o_ref[...] = out.astype(BF16)
        s_ref[...] = jnp.sum(out, axis=0, keepdims=True)

    return pl.pallas_call(
        body, name="gates_bwd", grid=(1,),
        in_specs=[pl.BlockSpec((NH, t, LANE), lambda i: (0, 0, 0)),
                  pl.BlockSpec((t, LANE), lambda i: (0, gate_tile)), pl.BlockSpec((1, LANE), lambda i: (0, 0))],
        out_specs=[pl.BlockSpec((t, LANE), lambda i: (0, 0)), pl.BlockSpec((1, LANE), lambda i: (0, 0))],
        out_shape=[jax.ShapeDtypeStruct((t, LANE), BF16), jax.ShapeDtypeStruct((1, LANE), F32)],
        compiler_params=_params("arbitrary"),
    )(dgate, proj, bias_tile)


def _chunk_gates(gc, gr, h, mprev):
    L = CHUNK
    lane = _iota(gc.shape, 1)
    sub = _iota(gr.shape, 0)
    icol = jnp.sum(jnp.where(lane == h, gc, 0.0), axis=1, keepdims=True)
    bcol = jnp.sum(jnp.where(lane == h + NH, gc, 0.0), axis=1, keepdims=True)
    irow = jnp.sum(jnp.where(sub == h, gr, 0.0), axis=0, keepdims=True)
    brow = jnp.sum(jnp.where(sub == h + NH, gr, 0.0), axis=0, keepdims=True)
    tri = _iota((L, L), 0) >= _iota((L, L), 1)
    log_d = jnp.where(tri, bcol - brow + irow, -jnp.inf)
    inter = bcol + mprev
    mt = jnp.maximum(inter, jnp.max(log_d, axis=1, keepdims=True))
    dw = jnp.exp(log_d - mt)
    iw = jnp.exp(inter - mt)
    g = jnp.sum(jnp.where(_iota((L, 1), 0) == L - 1, bcol, 0.0), axis=0, keepdims=True)
    wlog_col = g - bcol + icol
    wlog_row = g - brow + irow
    mnew = jnp.maximum(g + mprev, jnp.max(wlog_row, axis=1, keepdims=True))
    wcol = jnp.exp(wlog_col - mnew)
    decay = jnp.exp(g + mprev - mnew)
    return dw, iw, mt, wcol, decay, mnew


def _mlstm_fwd(proj, gcol, grow, t, wc, dh):
    nc = t // CHUNK
    qoff = 3 * wc // dh
    scale = dh ** -0.5
    wm = NH * dh

    def body(q_ref, k_ref, v_ref, gc_ref, gr_ref, h_ref, cs_ref, ns_ref, c_s, n_s, m_s):
        h = pl.program_id(0)
        c = pl.program_id(1)

        @pl.when(c == 0)
        def _():
            c_s[...] = jnp.zeros_like(c_s)
            n_s[...] = jnp.zeros_like(n_s)
            m_s[...] = jnp.zeros_like(m_s)

        mprev = m_s[0:1, 0:1]
        cprev = c_s[...]
        n8 = n_s[...]
        nprev = n8[0:1]
        cs_ref[...] = cprev
        ns_ref[...] = jnp.where(_iota(n8.shape, 0) == 1, mprev, n8)

        dw, iw, mt, wcol, decay, mnew = _chunk_gates(gc_ref[...], gr_ref[0], h, mprev)
        qs = q_ref[...] * scale
        k = k_ref[...]
        qs_b, k_b, v_b = qs.astype(BF16), k.astype(BF16), v_ref[...].astype(BF16)
        s = _dot(qs_b, k_b, _NT) * dw
        num = _dot(s.astype(BF16), v_b) + iw * _dot(qs_b, cprev.astype(BF16))
        den = jnp.sum(s, axis=1, keepdims=True) + iw * jnp.sum(qs * nprev, axis=1, keepdims=True)
        h_ref[...] = num / jnp.maximum(jnp.abs(den), jnp.exp(-mt))

        wk = wcol * k
        c_s[...] = decay * cprev + _dot(wk.astype(BF16), v_b, _TN)
        n_s[...] = decay * n8 + jnp.sum(wk, axis=0, keepdims=True)
        m_s[...] = jnp.broadcast_to(mnew, m_s.shape)

    hd = lambda off: pl.BlockSpec((CHUNK, dh), lambda h, c: (c, qoff + off * NH + h))
    return pl.pallas_call(
        body, name="mlstm_fwd", grid=(NH, nc),
        in_specs=[hd(0), hd(1), hd(2),
                  pl.BlockSpec((CHUNK, LANE), lambda h, c: (c, 0)),
                  pl.BlockSpec((1, 8, CHUNK), lambda h, c: (c, 0, 0))],
        out_specs=[pl.BlockSpec((CHUNK, dh), lambda h, c: (c, h)),
                   pl.BlockSpec((None, None, dh, dh), lambda h, c: (h, c, 0, 0)),
                   pl.BlockSpec((None, None, 8, dh), lambda h, c: (h, c, 0, 0))],
        out_shape=[jax.ShapeDtypeStruct((t, wm), F32),
                   jax.ShapeDtypeStruct((NH, nc, dh, dh), F32),
                   jax.ShapeDtypeStruct((NH, nc, 8, dh), F32)],
        scratch_shapes=[pltpu.VMEM((dh, dh), F32), pltpu.VMEM((8, dh), F32), pltpu.VMEM((8, LANE), F32)],
        compiler_params=_params("parallel", "arbitrary"),
    )(proj, proj, proj, gcol, grow)


def _mlstm_bwd(proj, gcol, grow, hval, dh_in, cs, ns, t, wc, dh):
    nc = t // CHUNK
    qoff = 3 * wc // dh
    scale = dh ** -0.5
    wm = NH * dh
    L = CHUNK

    def body(q_ref, k_ref, v_ref, gc_ref, gr_ref, h_ref, dh_ref, cs_ref, ns_ref,
             dq_ref, dk_ref, dv_ref, dg_ref, dc_s, dn_s):
        h = pl.program_id(0)
        step = pl.program_id(1)

        @pl.when(step == 0)
        def _():
            dc_s[...] = jnp.zeros_like(dc_s)
            dn_s[...] = jnp.zeros_like(dn_s)

        ns8 = ns_ref[...]
        nprev = ns8[0:1]
        mprev = ns8[1:2, 0:1]
        cprev = cs_ref[...]
        dcn = dc_s[...]
        dn8 = dn_s[...]
        dnn = dn8[0:1]

        dw, iw, mt, wcol, decay, _ = _chunk_gates(gc_ref[...], gr_ref[0], h, mprev)
        qs = q_ref[...] * scale
        k = k_ref[...]
        qs_b, k_b, v_b = qs.astype(BF16), k.astype(BF16), v_ref[...].astype(BF16)
        qk = _dot(qs_b, k_b, _NT)
        s = qk * dw
        den = jnp.sum(s, axis=1, keepdims=True) + iw * jnp.sum(qs * nprev, axis=1, keepdims=True)
        emt = jnp.exp(-mt)
        r = 1.0 / jnp.maximum(jnp.abs(den), emt)
        dout = dh_ref[...]
        dnum = dout * r
        dden = (-jnp.sum(dout * h_ref[...], axis=1, keepdims=True) * r
                * jnp.where(jnp.abs(den) > emt, jnp.sign(den), 0.0))
        dnum_b = dnum.astype(BF16)
        cprev_b = cprev.astype(BF16)
        dcn_b = dcn.astype(BF16)

        gd = (_dot(dnum_b, v_b, _NT) + dden) * dw
        gd_b = gd.astype(BF16)
        dqs_inter = iw * (_dot(dnum_b, cprev_b, _NT) + dden * nprev)
        dqs = _dot(gd_b, k_b) + dqs_inter
        dk_inter = wcol * (_dot(v_b, dcn_b, _NT) + dnn)
        dk = _dot(gd_b, qs_b, _TN) + dk_inter
        wk = wcol * k
        dv = _dot(s.astype(BF16), dnum_b, _TN) + _dot(wk.astype(BF16), dcn_b)

        e = gd * qk
        eye = _iota((L, L), 0) == _iota((L, L), 1)
        e_cols = jnp.sum(jnp.where(eye, jnp.sum(e, axis=0, keepdims=True), 0.0), axis=1, keepdims=True)
        k_inter = jnp.sum(k * dk_inter, axis=1, keepdims=True)
        rq = jnp.sum(e, axis=1, keepdims=True) + jnp.sum(qs * dqs_inter, axis=1, keepdims=True)
        rk = e_cols + k_inter
        hsum = jnp.sum(k_inter, axis=0, keepdims=True)
        jdec = decay * (jnp.sum(jnp.sum(dcn * cprev, axis=1, keepdims=True), axis=0, keepdims=True)
                        + jnp.sum(dnn * nprev, axis=1, keepdims=True))
        db = rq - rk + jnp.where(_iota((L, 1), 0) == L - 1, hsum + jdec, 0.0)
        lane = _iota((L, LANE), 1)
        dg_ref[...] = jnp.where(lane == 0, rk, jnp.where(lane == 1, db, 0.0))

        dq_ref[...] = (dqs * scale).astype(BF16)
        dk_ref[...] = dk.astype(BF16)
        dv_ref[...] = dv.astype(BF16)

        iq = iw * qs
        dc_s[...] = decay * dcn + _dot(iq.astype(BF16), dnum_b, _TN)
        dn_s[...] = decay * dn8 + jnp.sum(iq * dden, axis=0, keepdims=True)

    rc = lambda c: nc - 1 - c
    hd = lambda off: pl.BlockSpec((L, dh), lambda h, c: (rc(c), qoff + off * NH + h))
    hm = pl.BlockSpec((L, dh), lambda h, c: (rc(c), h))
    act = jax.ShapeDtypeStruct((t, wm), BF16)
    return pl.pallas_call(
        body, name="mlstm_bwd", grid=(NH, nc),
        in_specs=[hd(0), hd(1), hd(2),
                  pl.BlockSpec((L, LANE), lambda h, c: (rc(c), 0)),
                  pl.BlockSpec((1, 8, L), lambda h, c: (rc(c), 0, 0)),
                  hm, hm,
                  pl.BlockSpec((None, None, dh, dh), lambda h, c: (h, rc(c), 0, 0)),
                  pl.BlockSpec((None, None, 8, dh), lambda h, c: (h, rc(c), 0, 0))],
        out_specs=[hm, hm, hm, pl.BlockSpec((None, L, LANE), lambda h, c: (h, rc(c), 0))],
        out_shape=[act, act, act, jax.ShapeDtypeStruct((NH, t, LANE), F32)],
        scratch_shapes=[pltpu.VMEM((dh, dh), F32), pltpu.VMEM((8, dh), F32)],
        compiler_params=_params("parallel", "arbitrary"),
    )(proj, proj, proj, gcol, grow, hval, dh_in, cs, ns)


def _head_norm(hv):
    mu = jnp.mean(hv, axis=1, keepdims=True)
    hc = hv - mu
    rstd = lax.rsqrt(jnp.mean(hc * hc, axis=1, keepdims=True) + HN_EPS)
    return hc * rstd, rstd


def _hnorm_fwd(hval, proj, gain, t, wc, dh, tr=256):
    ooff = 3 * wc // dh + 3 * NH

    def body(h_ref, o_ref, g_ref, y_ref):
        hhat, _ = _head_norm(h_ref[...])
        y_ref[...] = (_sigmoid(o_ref[...]) * hhat * g_ref[...]).astype(BF16)

    return pl.pallas_call(
        body, name="hnorm_fwd", grid=(t // tr, NH),
        in_specs=[pl.BlockSpec((tr, dh), lambda i, h: (i, h)),
                  pl.BlockSpec((tr, dh), lambda i, h: (i, ooff + h)),
                  pl.BlockSpec((1, dh), lambda i, h: (0, h))],
        out_specs=pl.BlockSpec((tr, dh), lambda i, h: (i, h)),
        out_shape=jax.ShapeDtypeStruct((t, NH * dh), BF16),
        compiler_params=_params("parallel", "parallel"),
    )(hval, proj, gain)


def _hnorm_bwd(dy, hval, proj, gain, t, wc, dh, tr=256):
    ooff = 3 * wc // dh + 3 * NH
    yoff = wc // dh

    def body(dy_ref, h_ref, o_ref, g_ref, do_ref, dh_ref, dg_ref):
        i = pl.program_id(1)
        hhat, rstd = _head_norm(h_ref[...])
        gain_v = g_ref[...]
        sig = _sigmoid(o_ref[...])
        d = dy_ref[...]
        do_ref[...] = (d * hhat * gain_v * sig * (1.0 - sig)).astype(BF16)
        dhn = d * sig
        part = jnp.sum(dhn * hhat, axis=0, keepdims=True)

        @pl.when(i == 0)
        def _():
            dg_ref[...] = part

        @pl.when(i > 0)
        def _():
            dg_ref[...] += part

        dhat = dhn * gain_v
        dh_ref[...] = rstd * (dhat - jnp.mean(dhat, axis=1, keepdims=True)
                              - hhat * jnp.mean(dhat * hhat, axis=1, keepdims=True))

    blk = lambda off: pl.BlockSpec((tr, dh), lambda h, i: (i, off + h))
    return pl.pallas_call(
        body, name="hnorm_bwd", grid=(NH, t // tr),
        in_specs=[blk(yoff), blk(0), blk(ooff), pl.BlockSpec((1, dh), lambda h, i: (0, h))],
        out_specs=[blk(0), blk(0), pl.BlockSpec((1, dh), lambda h, i: (0, h))],
        out_shape=[jax.ShapeDtypeStruct((t, NH * dh), BF16), jax.ShapeDtypeStruct((t, NH * dh), F32),
                   jax.ShapeDtypeStruct((1, NH * dh), F32)],
        compiler_params=_params("parallel", "arbitrary"),
    )(dy, hval, proj, gain)


def _ln_stats(z):
    mu = jnp.mean(z, axis=1, keepdims=True)
    zc = z - mu
    rstd = lax.rsqrt(jnp.mean(zc * zc, axis=1, keepdims=True) + LN_EPS)
    return zc * rstd, rstd


def _ln_bwd(dy, xhat, rstd, g):
    dxh = dy * g
    return rstd * (dxh - jnp.mean(dxh, axis=1, keepdims=True) - xhat * jnp.mean(dxh * xhat, axis=1, keepdims=True))


def _accum(ref, i, part):
    @pl.when(i == 0)
    def _():
        ref[...] = part

    @pl.when(i > 0)
    def _():
        ref[...] += part


def _ln1_fwd(x, mix, g, b, tr=256):
    t, d = x.shape

    def body(x_ref, m_ref, g_ref, b_ref, xh_ref, rs_ref, xb_ref):
        xhat, rstd = _ln_stats(ALPHA * x_ref[...] + m_ref[...])
        xh_ref[...] = xhat
        rs_ref[...] = rstd
        xb_ref[...] = (xhat * g_ref[...] + b_ref[...]).astype(BF16)

    row = pl.BlockSpec((tr, d), lambda i: (i, 0))
    vec = pl.BlockSpec((1, d), lambda i: (0, 0))
    return pl.pallas_call(
        body, name="ln1_fwd", grid=(t // tr,),
        in_specs=[row, row, vec, vec],
        out_specs=[row, pl.BlockSpec((tr, 1), lambda i: (i, 0)), row],
        out_shape=[jax.ShapeDtypeStruct((t, d), F32), jax.ShapeDtypeStruct((t, 1), F32),
                   jax.ShapeDtypeStruct((t, d), BF16)],
        compiler_params=_params("parallel"),
    )(x, mix, g, b)


def _ln2_loss(xhat1, g1, b1, ff, target, g2, b2, tr=256):
    t, d = ff.shape

    def body(xh_ref, g1_ref, b1_ref, f_ref, t_ref, g_ref, b_ref, dz_ref, dzb_ref, dg_ref, db_ref, l_ref):
        i = pl.program_id(0)
        x1 = xh_ref[...] * g1_ref[...] + b1_ref[...]
        xhat, rstd = _ln_stats(ALPHA * x1 + f_ref[...])
        gv = g_ref[...]
        e = xhat * gv + b_ref[...] - t_ref[...]
        lsum = jnp.sum(jnp.sum(e * e, axis=1, keepdims=True), axis=0, keepdims=True) * (0.5 / d)
        dy = e * (1.0 / d)
        _accum(dg_ref, i, jnp.sum(dy * xhat, axis=0, keepdims=True))
        _accum(db_ref, i, jnp.sum(dy, axis=0, keepdims=True))
        _accum(l_ref, i, jnp.broadcast_to(lsum, l_ref.shape))
        dz = _ln_bwd(dy, xhat, rstd, gv)
        dz_ref[...] = dz
        dzb_ref[...] = dz.astype(BF16)

    row = pl.BlockSpec((tr, d), lambda i: (i, 0))
    vec = pl.BlockSpec((1, d), lambda i: (0, 0))
    return pl.pallas_call(
        body, name="ln2_loss", grid=(t // tr,),
        in_specs=[row, vec, vec, row, row, vec, vec],
        out_specs=[row, row, vec, vec, pl.BlockSpec((8, LANE), lambda i: (0, 0))],
        out_shape=[jax.ShapeDtypeStruct((t, d), F32), jax.ShapeDtypeStruct((t, d), BF16),
                   jax.ShapeDtypeStruct((1, d), F32), jax.ShapeDtypeStruct((1, d), F32),
                   jax.ShapeDtypeStruct((8, LANE), F32)],
        compiler_params=_params("arbitrary"),
    )(xhat1, g1, b1, ff, target, g2, b2)


def _ln1_bwd(dz2, dffn, xhat1, rstd1, g1, tr=256):
    t, d = dz2.shape

    def body(a_ref, f_ref, xh_ref, rs_ref, g_ref, dz_ref, dzb_ref, dg_ref, db_ref):
        i = pl.program_id(0)
        dy = ALPHA * a_ref[...] + f_ref[...]
        xhat = xh_ref[...]
        _accum(dg_ref, i, jnp.sum(dy * xhat, axis=0, keepdims=True))
        _accum(db_ref, i, jnp.sum(dy, axis=0, keepdims=True))
        dz = _ln_bwd(dy, xhat, rs_ref[...], g_ref[...])
        dz_ref[...] = dz
        dzb_ref[...] = dz.astype(BF16)

    row = pl.BlockSpec((tr, d), lambda i: (i, 0))
    vec = pl.BlockSpec((1, d), lambda i: (0, 0))
    return pl.pallas_call(
        body, name="ln1_bwd", grid=(t // tr,),
        in_specs=[row, row, row, pl.BlockSpec((tr, 1), lambda i: (i, 0)), vec],
        out_specs=[row, row, vec, vec],
        out_shape=[jax.ShapeDtypeStruct((t, d), F32), jax.ShapeDtypeStruct((t, d), BF16),
                   jax.ShapeDtypeStruct((1, d), F32), jax.ShapeDtypeStruct((1, d), F32)],
        compiler_params=_params("arbitrary"),
    )(dz2, dffn, xhat1, rstd1, g1)


def _ffn_act_fwd(hid0, w_fc, b_fc, t, dff):
    nb = dff // LANE

    def body(hv_ref, hg_ref, wv_ref, wg_ref, bv_ref, bg_ref, a_ref):
        val = _conv(hv_ref[...], wv_ref[...]) + bv_ref[...]
        gate = _conv(hg_ref[...], wg_ref[...]) + bg_ref[...]
        a_ref[...] = (gate * _sigmoid(gate) * val).astype(BF16)

    col = lambda off: pl.BlockSpec((t, LANE), lambda j: (0, j + off))
    w3 = lambda off: pl.BlockSpec((3, LANE), lambda j: (0, j + off))
    w1 = lambda off: pl.BlockSpec((1, LANE), lambda j: (0, j + off))
    return pl.pallas_call(
        body, name="ffn_act_fwd", grid=(nb,),
        in_specs=[col(0), col(nb), w3(0), w3(nb), w1(0), w1(nb)],
        out_specs=col(0),
        out_shape=jax.ShapeDtypeStruct((t, dff), BF16),
        compiler_params=_params("parallel"),
    )(hid0, hid0, w_fc, w_fc, b_fc, b_fc)


def _ffn_act_bwd(da, hid0, w_fc, b_fc, t, dff):
    nb = dff // LANE

    def body(da_ref, hv_ref, hg_ref, wv_ref, wg_ref, bv_ref, bg_ref,
             dhv_ref, dhg_ref, dwv_ref, dwg_ref, dbv_ref, dbg_ref):
        hv, hg, wv, wg = hv_ref[...], hg_ref[...], wv_ref[...], wg_ref[...]
        val = _conv(hv, wv) + bv_ref[...]
        gate = _conv(hg, wg) + bg_ref[...]
        sig = _sigmoid(gate)
        d = da_ref[...]
        dval = d * gate * sig
        dgate = d * val * sig * (1.0 + gate * (1.0 - sig))
        dhv_ref[...] = _conv_t(dval, wv).astype(BF16)
        dhg_ref[...] = _conv_t(dgate, wg).astype(BF16)
        dwv_ref[...] = _conv_dw(dval, hv)
        dwg_ref[...] = _conv_dw(dgate, hg)
        dbv_ref[...] = jnp.sum(dval, axis=0, keepdims=True)
        dbg_ref[...] = jnp.sum(dgate, axis=0, keepdims=True)

    col = lambda off: pl.BlockSpec((t, LANE), lambda j: (0, j + off))
    w3 = lambda off: pl.BlockSpec((3, LANE), lambda j: (0, j + off))
    w1 = lambda off: pl.BlockSpec((1, LANE), lambda j: (0, j + off))
    act = jax.ShapeDtypeStruct((t, dff), BF16)
    s3 = jax.ShapeDtypeStruct((3, dff), F32)
    s1 = jax.ShapeDtypeStruct((1, dff), F32)
    return pl.pallas_call(
        body, name="ffn_act_bwd", grid=(nb,),
        in_specs=[col(0), col(0), col(nb), w3(0), w3(nb), w1(0), w1(nb)],
        out_specs=[col(0), col(0), w3(0), w3(0), w1(0), w1(0)],
        out_shape=[act, act, s3, s3, s1, s1],
        compiler_params=_params("parallel"),
    )(da, hid0, hid0, w_fc, w_fc, b_fc, b_fc)


def _local_step(x, target, w_in, b_gates, w_sc, gain, w_out, ln1_g, ln1_b, w_up, w_fc, b_fc, w_down, ln2_g, ln2_b):
    t, d = x.shape
    wc = d // 2
    dh = (d - wc) // NH
    wm = NH * dh
    dff = w_down.shape[0]
    ninp = w_in.shape[1]
    nin = 3 * wc + 4 * wm
    gate_tile = nin // LANE
    nc = t // CHUNK
    bias_tile = jnp.pad(b_gates, ((0, 0), (0, LANE - 2 * NH)))

    x_b = x.astype(BF16)
    proj = _matmul(x_b, w_in, "nn", F32, "proj", tn=1152)
    y_conv = _sconv_fwd(proj, w_sc, t, wc)
    gcol = _gates_prep(proj, bias_tile, t, gate_tile)
    grow = gcol[:, :8].T.reshape(8, nc, CHUNK).transpose(1, 0, 2)
    hval, cs, ns = _mlstm_fwd(proj, gcol, grow, t, wc, dh)
    y_m = _hnorm_fwd(hval, proj, gain, t, wc, dh)
    y = jnp.concatenate([y_conv, y_m], axis=1)
    mix = _matmul(y, w_out, "nn", F32, "out_proj")
    xhat1, rstd1, x1_b = _ln1_fwd(x, mix, ln1_g, ln1_b)
    wsl = w_up.shape[2]
    hid0 = _matmul(x1_b, w_up, "nn", F32, "ffn_up", tn=wsl, b_blocked=True)
    act = _ffn_act_fwd(hid0, w_fc, b_fc, t, dff)
    ff = _matmul(act, w_down, "nn", F32, "ffn_down")
    dz2, dz2_b, d_ln2_g, d_ln2_b, loss = _ln2_loss(xhat1, ln1_g, ln1_b, ff, target, ln2_g, ln2_b)

    d_act = _matmul(dz2_b, w_down, "nt", F32, "ffn_down_dx")
    d_w_down = _matmul(act, dz2_b, "tn", BF16, "ffn_down_dw", tm=512, tn=1024, tk=t)
    dhv, dhg, dwv, dwg, dbv, dbg = _ffn_act_bwd(d_act, hid0, w_fc, b_fc, t, dff)
    d_hid0 = jnp.concatenate([dhv, dhg], axis=1)
    d_w_fc = jnp.concatenate([dwv, dwg], axis=1)
    d_b_fc = jnp.concatenate([dbv, dbg], axis=1)
    d_x1_ffn = _matmul(d_hid0, w_up, "nt", F32, "ffn_up_dx", tk=wsl, b_blocked=True)
    d_w_up = _matmul(x1_b, d_hid0, "tn", BF16, "ffn_up_dw", tm=512, tn=wsl, tk=t, o_width=wsl)
    dz1, dz1_b, d_ln1_g, d_ln1_b = _ln1_bwd(dz2, d_x1_ffn, xhat1, rstd1, ln1_g)

    dy = _matmul(dz1_b, w_out, "nt", F32, "out_proj_dx")
    d_w_out = _matmul(y, dz1_b, "tn", BF16, "out_proj_dw", tm=512, tn=1024, tk=t)
    dcb, dcc, dch, d_w_sc = _sconv_bwd(dy, proj, w_sc, t, wc)
    d_o, d_hval, d_gain = _hnorm_bwd(dy, hval, proj, gain, t, wc, dh)
    dq, dk, dv, dgate = _mlstm_bwd(proj, gcol, grow, hval, d_hval, cs, ns, t, wc, dh)
    dgt, d_b_gates = _gates_bwd(dgate, proj, bias_tile, t, gate_tile)
    pad = jnp.zeros((t, ninp - nin - LANE), BF16)
    d_proj = jnp.concatenate([dcb, dcc, dch, dq, dk, dv, d_o, dgt, pad], axis=1)
    grad_x = _matmul(d_proj, w_in, "nt", F32, "proj_dx", tk=1152, add=dz1, add_scale=ALPHA)
    d_w_in = _matmul(x_b, d_proj, "tn", BF16, "proj_dw", tm=512, tn=IN_SLAB, tk=t, o_width=IN_SLAB)

    small = dict(b_gates=d_b_gates[:, :2 * NH], w_sc_conv=d_w_sc, mh_gain=d_gain, ln1_g=d_ln1_g, ln1_b=d_ln1_b,
                 w_ffn_conv=d_w_fc, b_ffn_conv=d_b_fc, ln2_g=d_ln2_g, ln2_b=d_ln2_b)
    return loss, grad_x, d_w_in, d_w_out, d_w_up, d_w_down, small


HBM = pl.BlockSpec(memory_space=pltpu.HBM)


def _place():
    return lax.axis_index("x"), lax.axis_index("y"), lax.axis_index("c")


def _index(p):
    return 4 * p[0] + 2 * p[1] + p[2]


def _all_gather(arrs, name):
    n = len(arrs)

    def body(*refs):
        ins, outs = refs[:n], refs[n:2 * n]
        send_sems, recv_sems, local_sems = refs[2 * n:]
        x, y, c = _place()
        me, sibling = (x, y, c), (x, y, 1 - c)
        chips = [(1 - x, y), (x, 1 - y), (1 - x, 1 - y)]

        def copy(a, k, block, to, own=False):
            dst = outs[a].at[_index(block)]
            return pltpu.make_async_remote_copy(
                src_ref=ins[a] if own else dst, dst_ref=dst,
                send_sem=send_sems.at[k * n + a], recv_sem=recv_sems.at[k * n + a],
                device_id=to, device_id_type=MESH)

        mine = [pltpu.make_async_copy(ins[a], outs[a].at[_index(me)], local_sems.at[a]) for a in range(n)]
        for cp in mine:
            cp.start()
        first = []
        for a in range(n):
            first.append(copy(a, 0, me, sibling, own=True))
            first += [copy(a, 1 + j, me, (*chip, c), own=True) for j, chip in enumerate(chips)]
        for cp in first:
            cp.start()
        passed = []
        for j, chip in enumerate(chips):
            for a in range(n):
                copy(a, 1 + j, (*chip, c), me).wait_recv()
                cp = copy(a, 4 + j, (*chip, c), sibling)
                cp.start()
                passed.append(cp)
        for a in range(n):
            copy(a, 0, sibling, me).wait_recv()
            for j, chip in enumerate(chips):
                copy(a, 4 + j, (*chip, 1 - c), me).wait_recv()
        for cp in first + passed:
            cp.wait_send()
        for cp in mine:
            cp.wait()

    return pl.pallas_call(
        body, name=name, in_specs=[HBM] * n, out_specs=[HBM] * n,
        out_shape=[jax.ShapeDtypeStruct((N_DEV,) + a.shape, a.dtype) for a in arrs],
        scratch_shapes=[pltpu.SemaphoreType.DMA((7 * n,)), pltpu.SemaphoreType.DMA((7 * n,)),
                        pltpu.SemaphoreType.DMA((n,))],
    )(*arrs)


def _exchange_grads(grads, tails, name):
    n = len(grads)
    out_shapes = [jax.ShapeDtypeStruct((N_DEV, g.shape[1], g.shape[2] + (LANE if tl else 0)), g.dtype)
                  for g, tl in zip(grads, tails)]

    def body(*refs):
        ins, outs = refs[:n], refs[n:2 * n]
        send_sems, recv_sems, local_sems = refs[2 * n:]
        pieces = []
        for a in range(n):
            if tails[a]:
                w = grads[a].shape[2]
                pieces.append((lambda i, a=a: ins[a].at[i], lambda s, a=a, w=w: outs[a].at[s, :, pl.ds(0, w)]))
                pieces.append((lambda i, a=a: ins[a].at[i + 1, :, pl.ds(0, LANE)],
                               lambda s, a=a, w=w: outs[a].at[s, :, pl.ds(w, LANE)]))
            else:
                pieces.append((lambda i, a=a: ins[a].at[i], lambda s, a=a: outs[a].at[s]))
        x, y, c = _place()
        me = _index((x, y, c))
        peers = [(1 - x if j & 4 else x, 1 - y if j & 2 else y, 1 - c if j & 1 else c) for j in range(1, N_DEV)]

        local = [pltpu.make_async_copy(src(me), dst(me), local_sems.at[p]) for p, (src, dst) in enumerate(pieces)]
        for cp in local:
            cp.start()
        sends = []
        for j, peer in enumerate(peers):
            for p, (src, dst) in enumerate(pieces):
                k = j * len(pieces) + p
                cp = pltpu.make_async_remote_copy(src_ref=src(_index(peer)), dst_ref=dst(me), send_sem=send_sems.at[k],
                                                  recv_sem=recv_sems.at[k], device_id=peer, device_id_type=MESH)
                cp.start()
                sends.append(cp)
        for j, peer in enumerate(peers):
            for p, (src, dst) in enumerate(pieces):
                k = j * len(pieces) + p
                landed = dst(_index(peer))
                pltpu.make_async_remote_copy(src_ref=landed, dst_ref=landed, send_sem=send_sems.at[k],
                                             recv_sem=recv_sems.at[k], device_id=peer, device_id_type=MESH).wait_recv()
        for cp in sends:
            cp.wait_send()
        for cp in local:
            cp.wait()

    n_pieces = n + sum(tails)
    return pl.pallas_call(
        body, name=name, in_specs=[HBM] * n, out_specs=[HBM] * n, out_shape=out_shapes,
        scratch_shapes=[pltpu.SemaphoreType.DMA((7 * n_pieces,)), pltpu.SemaphoreType.DMA((7 * n_pieces,)),
                        pltpu.SemaphoreType.DMA((n_pieces,))],
    )(*grads)


def _assemble_w_in(g, ninp):
    _, d, pw = g.shape
    per = IN_SLAB // LANE
    last = N_DEV * per

    def body(a_ref, b_ref, o_ref):
        t = pl.program_id(0)
        main = jnp.where(t <= last, a_ref[...], jnp.zeros_like(a_ref))
        carry = jnp.where((t % per == 0) & (t > 0) & (t < last), b_ref[...], jnp.zeros_like(b_ref))
        o_ref[...] = main + carry

    def main_map(t):
        k = jnp.minimum(t // per, N_DEV - 1)
        return k, 0, jnp.minimum(t - per * k, per)

    def carry_map(t):
        return jnp.maximum(jnp.minimum(t // per, N_DEV - 1), 1) - 1, 0, per

    return pl.pallas_call(
        body, name="assemble_w_in", grid=(ninp // LANE,),
        in_specs=[pl.BlockSpec((None, d, LANE), main_map), pl.BlockSpec((None, d, LANE), carry_map)],
        out_specs=pl.BlockSpec((d, LANE), lambda t: (0, t)),
        out_shape=jax.ShapeDtypeStruct((d, ninp), g.dtype),
        compiler_params=_params("parallel"),
    )(g, g)


def _rows(n, want):
    t = min(n, want)
    t -= t % 16
    while n % t:
        t -= 16
    return t


def _adam_math(w, g, m, v):
    m2 = ADAM_B1 * m + (1.0 - ADAM_B1) * g
    v2 = ADAM_B2 * v + (1.0 - ADAM_B2) * (g * g)
    m_hat = m2 / (1.0 - ADAM_B1 ** ADAM_STEP)
    v_hat = v2 / (1.0 - ADAM_B2 ** ADAM_STEP)
    return -ADAM_LR * (m_hat / (jnp.sqrt(v_hat) + ADAM_EPS) + ADAM_WD * w), m2, v2


def _slot_sum(r_ref):
    acc = r_ref[0].astype(F32)
    for i in range(1, N_DEV):
        acc = acc + r_ref[i].astype(F32)
    return acc


def _sum_slots(r, name, tr=128):
    _, rows, cols = r.shape
    tr = _rows(rows, tr)

    def body(r_ref, g_ref):
        g_ref[...] = _slot_sum(r_ref)

    return pl.pallas_call(
        body, name=name, grid=(rows // tr,),
        in_specs=[pl.BlockSpec((N_DEV, tr, cols), lambda i: (0, i, 0))],
        out_specs=pl.BlockSpec((tr, cols), lambda i: (i, 0)),
        out_shape=jax.ShapeDtypeStruct((rows, cols), F32),
        compiler_params=_params("parallel"),
    )(r)


def _adamw(w, g, m, v, name, tr=256):
    rows, cols = w.shape
    tr = _rows(rows, tr)

    def body(w_ref, g_ref, m_ref, v_ref, d_ref, m2_ref, v2_ref):
        d_ref[...], m2_ref[...], v2_ref[...] = _adam_math(w_ref[...], g_ref[...], m_ref[...], v_ref[...])

    blk = pl.BlockSpec((tr, cols), lambda i: (i, 0))
    out = jax.ShapeDtypeStruct((rows, cols), F32)
    return pl.pallas_call(
        body, name=name, grid=(rows // tr,), in_specs=[blk] * 4, out_specs=[blk] * 3, out_shape=[out] * 3,
        compiler_params=_params("parallel"),
    )(w, g, m, v)


def _sum_adamw(r, w, m, v, name, tr=128):
    rows, cols = w.shape
    tr = _rows(rows, tr)

    def body(r_ref, w_ref, m_ref, v_ref, g_ref, d_ref, m2_ref, v2_ref):
        g = _slot_sum(r_ref)
        g_ref[...] = g
        d_ref[...], m2_ref[...], v2_ref[...] = _adam_math(w_ref[...], g, m_ref[...], v_ref[...])

    blk = pl.BlockSpec((tr, cols), lambda i: (i, 0))
    out = jax.ShapeDtypeStruct((rows, cols), F32)
    return pl.pallas_call(
        body, name=name, grid=(rows // tr,),
        in_specs=[pl.BlockSpec((N_DEV, tr, cols), lambda i: (0, i, 0)), blk, blk, blk],
        out_specs=[blk] * 4, out_shape=[out] * 4,
        compiler_params=_params("parallel"),
    )(r, w, m, v)


def _pack(pieces, sizes):
    flat = [jnp.pad(p.reshape(-1).astype(F32), (0, s - p.size)) for p, s in zip(pieces, sizes)]
    total = sum(sizes)
    padded = -(-total // (16 * LANE)) * (16 * LANE)
    return jnp.pad(jnp.concatenate(flat), (0, padded - total)).reshape(-1, LANE)


def _unpack(packed, shapes, sizes):
    flat = packed.reshape(-1)
    out, off = [], 0
    for shp, s in zip(shapes, sizes):
        n = 1
        for k in shp:
            n *= k
        out.append(flat[off:off + n].reshape(shp))
        off += s
    return out


def _lanes(n):
    return -(-n // LANE) * LANE


WEIGHTS = ("w_in", "b_gates", "w_sc_conv", "mh_gain", "w_out", "ln1_g", "ln1_b", "w_up", "w_ffn_conv", "b_ffn_conv",
           "w_down", "ln2_g", "ln2_b")
BIG = ("w_in", "w_out", "w_up", "w_down")
SMALL = tuple(n for n in WEIGHTS if n not in BIG)


def kernel(x, w_in, b_gates, w_sc_conv, mh_gain, w_out, ln1_g, ln1_b, w_up, w_ffn_conv, b_ffn_conv, w_down, ln2_g, ln2_b, loss_target, m_w_in, m_b_gates, m_w_sc_conv, m_mh_gain, m_w_out, m_ln1_g, m_ln1_b, m_w_up, m_w_ffn_conv, m_b_ffn_conv, m_w_down, m_ln2_g, m_ln2_b, v_w_in, v_b_gates, v_w_sc_conv, v_mh_gain, v_w_out, v_ln1_g, v_ln1_b, v_w_up, v_w_ffn_conv, v_b_ffn_conv, v_w_down, v_ln2_g, v_ln2_b):
    w = dict(zip(WEIGHTS, (w_in, b_gates, w_sc_conv, mh_gain, w_out, ln1_g, ln1_b, w_up, w_ffn_conv, b_ffn_conv,
                           w_down, ln2_g, ln2_b)))
    m = dict(zip(WEIGHTS, (m_w_in, m_b_gates, m_w_sc_conv, m_mh_gain, m_w_out, m_ln1_g, m_ln1_b, m_w_up,
                           m_w_ffn_conv, m_b_ffn_conv, m_w_down, m_ln2_g, m_ln2_b)))
    v = dict(zip(WEIGHTS, (v_w_in, v_b_gates, v_w_sc_conv, v_mh_gain, v_w_out, v_ln1_g, v_ln1_b, v_w_up,
                           v_w_ffn_conv, v_b_ffn_conv, v_w_down, v_ln2_g, v_ln2_b)))
    me = _index(_place())
    d = x.shape[2]
    ws_in = w_in.shape[2]
    assert ws_in == IN_SLAB + 1 and N_DEV <= LANE, w_in.shape
    ninp = (N_DEV + 1) * IN_SLAB
    ws_sc, ws_fc = w_sc_conv.shape[2], w_ffn_conv.shape[2]

    w_in_shift = lax.dynamic_update_slice(jnp.zeros((d, IN_SLAB + LANE), BF16), w_in[0].astype(BF16), (0, me))
    taps8 = lambda a: jnp.pad(a[0], ((0, 5), (0, 0)))
    g_in, g_out, g_up, g_down, g_sc, g_fc = _all_gather(
        [w_in_shift, w_out[0].astype(BF16), w_up[0].astype(BF16), w_down[0].astype(BF16),
         taps8(w_sc_conv), taps8(w_ffn_conv)], "gather_weights")
    w_in_full = _assemble_w_in(g_in, ninp)
    w_sc_full = g_sc[:, :3].transpose(1, 0, 2).reshape(3, N_DEV * ws_sc)
    w_fc_full = g_fc[:, :3].transpose(1, 0, 2).reshape(3, N_DEV * ws_fc)
    dff = N_DEV * w_down.shape[1]

    loss_t, grad_x, d_w_in, d_w_out, d_w_up, d_w_down, small = _local_step(
        x[0], loss_target[0], w_in_full, b_gates, w_sc_full, mh_gain, g_out.reshape(d, d), ln1_g, ln1_b, g_up,
        w_fc_full, b_ffn_conv, g_down.reshape(dff, d), ln2_g, ln2_b)

    r_in, r_out, r_up, r_down = _exchange_grads(
        [d_w_in, d_w_out.reshape(N_DEV, d // N_DEV, d), d_w_up, d_w_down.reshape(N_DEV, dff // N_DEV, d)],
        [True, False, False, False], "scatter_grads")
    grads, deltas, new_m, new_v = {}, {}, {}, {}
    g_shift = _sum_slots(r_in, "sum_w_in")
    grads["w_in"] = lax.dynamic_slice(g_shift, (0, me), (d, ws_in))
    deltas["w_in"], new_m["w_in"], new_v["w_in"] = _adamw(w_in[0], grads["w_in"], m_w_in[0], v_w_in[0], "adamw_w_in")
    for name, r in (("w_out", r_out), ("w_up", r_up), ("w_down", r_down)):
        grads[name], deltas[name], new_m[name], new_v[name] = _sum_adamw(r, w[name][0], m[name][0], v[name][0],
                                                                         "adamw_" + name)

    names = ("loss",) + SMALL
    parts = dict(small, loss=loss_t[0, :1])
    sizes = [_lanes(parts[n].size) for n in names]
    (g_small,) = _all_gather([_pack([parts[n] for n in names], sizes)], "gather_small")
    summed = _unpack(_sum_slots(g_small, "sum_small", tr=g_small.shape[1]), [parts[n].shape for n in names], sizes)
    full = dict(zip(names, summed))
    full["w_sc_conv"] = lax.dynamic_slice(full["w_sc_conv"], (0, me * ws_sc), (3, ws_sc))
    full["w_ffn_conv"] = lax.dynamic_slice(full["w_ffn_conv"], (0, me * ws_fc), (3, ws_fc))
    for n in SMALL:
        grads[n] = full[n].reshape(w[n].shape)
    sizes = [_lanes(w[n].size) for n in SMALL]
    shapes = [w[n].shape for n in SMALL]
    packed = [_pack([t[n] for n in SMALL], sizes) for t in (w, grads, m, v)]
    for res, t in zip(_adamw(*packed, "adamw_small"), (deltas, new_m, new_v)):
        t.update(zip(SMALL, _unpack(res, shapes, sizes)))

    big = lambda t: {n: (t[n].reshape(w[n].shape) if n in BIG else t[n]) for n in WEIGHTS}
    grads, deltas, new_m, new_v = big(grads), big(deltas), big(new_m), big(new_v)
    return (full["loss"].reshape(()), grad_x[None], *[grads[n] for n in WEIGHTS], *[deltas[n] for n in WEIGHTS],
            *[new_m[n] for n in WEIGHTS], *[new_v[n] for n in WEIGHTS])
```

```python
import functools

import jax
import jax.numpy as jnp
from jax import lax
from jax.experimental import pallas as pl
from jax.experimental.pallas import tpu as pltpu

F32 = jnp.float32
BF16 = jnp.bfloat16
MESH = pl.DeviceIdType.MESH

N_DEV = 8
NH = 4
CHUNK = 64
LN_EPS = 1e-5
HN_EPS = 1e-6
ALPHA = 2.0 ** 0.25
LANE = 128
IN_SLAB = 7 * LANE
VMEM_LIMIT = 56 * 1024 * 1024
ADAM_LR, ADAM_B1, ADAM_B2, ADAM_EPS, ADAM_WD, ADAM_STEP = 0.001, 0.9, 0.999, 1e-08, 0.01, 10

_NN = (((1,), (0,)), ((), ()))
_NT = (((1,), (1,)), ((), ()))
_TN = (((0,), (0,)), ((), ()))


def _dot(a, b, dn=_NN):
    return lax.dot_general(a, b, dn, preferred_element_type=F32)


def _params(*sem):
    return pltpu.CompilerParams(dimension_semantics=sem if sem else None, vmem_limit_bytes=VMEM_LIMIT)


def _iota(shape, axis):
    return lax.broadcasted_iota(jnp.int32, shape, axis)


def _fit(n, want):
    if n <= want:
        return n
    t = want - want % LANE
    while n % t:
        t -= LANE
    return t


def _matmul(a, b, mode, out_dtype, name, tm=1024, tn=512, tk=1024, add=None, add_scale=1.0,
            b_blocked=False, o_width=None, after=None):
    if mode == "tn":
        kd, m = a.shape
    else:
        m, kd = a.shape
    if b_blocked:
        nb, rows, w = b.shape
        n = nb * w if mode == "nn" else rows
        assert (nb * w if mode == "nt" else rows) == kd, (name, b.shape, kd)
    else:
        n = b.shape[0] if mode == "nt" else b.shape[1]
    tm, tn, tk = _fit(m, tm), _fit(n, tn), _fit(kd, tk)
    if b_blocked and mode == "nn":
        tn = _fit(w, tn)
    if b_blocked and mode == "nt":
        tk = _fit(w, tk)
    if o_width is not None:
        tn = _fit(o_width, tn)
    assert m % tm == 0 and n % tn == 0 and kd % tk == 0, (name, m, n, kd, tm, tn, tk)
    nk = kd // tk
    dn = {"nn": _NN, "nt": _NT, "tn": _TN}[mode]
    a_spec = (pl.BlockSpec((tk, tm), lambda i, j, k: (k, i)) if mode == "tn"
              else pl.BlockSpec((tm, tk), lambda i, j, k: (i, k)))
    if b_blocked and mode == "nn":
        per = w // tn
        b_spec = pl.BlockSpec((None, tk, tn), lambda i, j, k: (j // per, k, j % per))
    elif b_blocked:
        per = w // tk
        b_spec = pl.BlockSpec((None, tn, tk), lambda i, j, k: (k // per, j, k % per))
    elif mode == "nt":
        b_spec = pl.BlockSpec((tn, tk), lambda i, j, k: (j, k))
    else:
        b_spec = pl.BlockSpec((tk, tn), lambda i, j, k: (k, j))
    if o_width is None:
        o_spec = pl.BlockSpec((tm, tn), lambda i, j, k: (i, j))
        o_shape = (m, n)
    else:
        oper = o_width // tn
        o_spec = pl.BlockSpec((None, tm, tn), lambda i, j, k: (j // oper, i, j % oper))
        o_shape = (n // o_width, m, o_width)
    has_add = add is not None
    n_in = 2 + has_add + (after is not None)

    def body(*refs):
        a_ref, b_ref = refs[:2]
        add_ref = refs[2] if has_add else None
        o_ref = refs[n_in]

        def finish(r):
            if has_add:
                r = r + add_scale * add_ref[...]
            o_ref[...] = r.astype(out_dtype)

        if nk == 1:
            finish(_dot(a_ref[...], b_ref[...], dn))
        else:
            acc = refs[-1]
            k = pl.program_id(2)

            @pl.when(k == 0)
            def _():
                acc[...] = jnp.zeros_like(acc)

            acc[...] += _dot(a_ref[...], b_ref[...], dn)

            @pl.when(k == nk - 1)
            def _():
                finish(acc[...])

    in_specs = [a_spec, b_spec] + ([pl.BlockSpec((tm, tn), lambda i, j, k: (i, j))] if has_add else [])
    args = (a, b) + ((add,) if has_add else ())
    if after is not None:
        in_specs.append(pl.BlockSpec(memory_space=pl.ANY))
        args += (after,)
    return pl.pallas_call(
        body, name=name, grid=(m // tm, n // tn, nk),
        in_specs=in_specs, out_specs=o_spec,
        out_shape=jax.ShapeDtypeStruct(o_shape, out_dtype),
        scratch_shapes=[pltpu.VMEM((tm, tn), F32)] if nk > 1 else [],
        compiler_params=_params("parallel", "parallel", "arbitrary"),
    )(*args)


def _shift_down(u, s):
    return jnp.where(_iota(u.shape, 0) >= s, pltpu.roll(u, s, 0), 0.0)


def _shift_up(u, s):
    t = u.shape[0]
    return jnp.where(_iota(u.shape, 0) < t - s, pltpu.roll(u, t - s, 0), 0.0)


def _conv(u, w):
    return w[0:1] * _shift_down(u, 2) + w[1:2] * _shift_down(u, 1) + w[2:3] * u


def _conv_t(dy, w):
    return w[2:3] * dy + w[1:2] * _shift_up(dy, 1) + w[0:1] * _shift_up(dy, 2)


def _conv_dw(dy, u):
    d0 = jnp.sum(dy * _shift_down(u, 2), axis=0, keepdims=True)
    d1 = jnp.sum(dy * _shift_down(u, 1), axis=0, keepdims=True)
    d2 = jnp.sum(dy * u, axis=0, keepdims=True)
    r = _iota((3, dy.shape[1]), 0)
    return jnp.where(r == 0, d0, jnp.where(r == 1, d1, d2))


def _sigmoid(x):
    return 1.0 / (1.0 + jnp.exp(-x))


def _sconv_fwd(proj, w_sc, t, wc):
    nb = wc // LANE

    def body(cb_ref, cc_ref, ch_ref, w_ref, y_ref):
        u = cc_ref[...] * ch_ref[...]
        y_ref[...] = (cb_ref[...] * _conv(u, w_ref[...])).astype(BF16)

    col = lambda off: pl.BlockSpec((t, LANE), lambda j: (0, j + off))
    return pl.pallas_call(
        body, name="sconv_fwd", grid=(nb,),
        in_specs=[col(0), col(nb), col(2 * nb), pl.BlockSpec((3, LANE), lambda j: (0, j))],
        out_specs=pl.BlockSpec((t, LANE), lambda j: (0, j)),
        out_shape=jax.ShapeDtypeStruct((t, wc), BF16),
        compiler_params=_params("parallel"),
    )(proj, proj, proj, w_sc)


def _sconv_bwd(dy, proj, w_sc, t, wc):
    nb = wc // LANE

    def body(dy_ref, cb_ref, cc_ref, ch_ref, w_ref, dcb_ref, dcc_ref, dch_ref, dw_ref):
        cc, ch, w, d = cc_ref[...], ch_ref[...], w_ref[...], dy_ref[...]
        u = cc * ch
        dcb_ref[...] = (d * _conv(u, w)).astype(BF16)
        dcu = d * cb_ref[...]
        dw_ref[...] = _conv_dw(dcu, u)
        du = _conv_t(dcu, w)
        dcc_ref[...] = (du * ch).astype(BF16)
        dch_ref[...] = (du * cc).astype(BF16)

    col = lambda off: pl.BlockSpec((t, LANE), lambda j: (0, j + off))
    act = jax.ShapeDtypeStruct((t, wc), BF16)
    return pl.pallas_call(
        body, name="sconv_bwd", grid=(nb,),
        in_specs=[col(0), col(0), col(nb), col(2 * nb), pl.BlockSpec((3, LANE), lambda j: (0, j))],
        out_specs=[col(0), col(0), col(0), pl.BlockSpec((3, LANE), lambda j: (0, j))],
        out_shape=[act, act, act, jax.ShapeDtypeStruct((3, wc), F32)],
        compiler_params=_params("parallel"),
    )(dy, proj, proj, proj, w_sc)


def _gates_prep(proj, bias_tile, t, gate_tile):
    def body(g_ref, b_ref, o_ref):
        g = g_ref[...] + b_ref[...]
        lane = _iota(g.shape, 1)
        is_f = (lane >= NH) & (lane < 2 * NH)
        lf = jnp.minimum(g, 0.0) - jnp.log(1.0 + jnp.exp(-jnp.abs(g)))
        c = jnp.where(is_f, lf, 0.0)
        r = _iota(g.shape, 0) % CHUNK
        s = 1
        while s < CHUNK:
            c = c + jnp.where(r >= s, pltpu.roll(c, s, 0), 0.0)
            s *= 2
        o_ref[...] = jnp.where(is_f, c, jnp.where(lane < NH, g, 0.0))

    return pl.pallas_call(
        body, name="gates_prep", grid=(1,),
        in_specs=[pl.BlockSpec((t, LANE), lambda i: (0, gate_tile)), pl.BlockSpec((1, LANE), lambda i: (0, 0))],
        out_specs=pl.BlockSpec((t, LANE), lambda i: (0, 0)),
        out_shape=jax.ShapeDtypeStruct((t, LANE), F32),
        compiler_params=_params("arbitrary"),
    )(proj, bias_tile)


def _gates_bwd(dgate, proj, bias_tile, t, gate_tile):
    def body(dg_ref, g_ref, b_ref, o_ref, s_ref):
        g = g_ref[...] + b_ref[...]
        lane = _iota(g.shape, 1)
        r = _iota(g.shape, 0) % CHUNK
        dsig = 1.0 - _sigmoid(g)
        out = jnp.zeros(g.shape, F32)
        for h in range(NH):
            d = dg_ref[h]
            c = d
            s = 1
            while s < CHUNK:
                c = c + jnp.where(r + s < CHUNK, pltpu.roll(c, t - s, 0), 0.0)
                s *= 2
            di = jnp.broadcast_to(d[:, 0:1], g.shape)
            db = jnp.broadcast_to(c[:, 1:2], g.shape)
            out = out + jnp.where(lane == h, di, 0.0) + jnp.where(lane == NH + h, db * dsig, 0.0)
        o_ref[...] = out.astype(BF16)
        s_ref[...] = jnp.sum(out, axis=0, keepdims=True)

    return pl.pallas_call(
        body, name="gates_bwd", grid=(1,),
        in_specs=[pl.BlockSpec((NH, t, LANE), lambda i: (0, 0, 0)),
                  pl.BlockSpec((t, LANE), lambda i: (0, gate_tile)), pl.BlockSpec((1, LANE), lambda i: (0, 0))],
        out_specs=[pl.BlockSpec((t, LANE), lambda i: (0, 0)), pl.BlockSpec((1, LANE), lambda i: (0, 0))],
        out_shape=[jax.ShapeDtypeStruct((t, LANE), BF16), jax.ShapeDtypeStruct((1, LANE), F32)],
        compiler_params=_params("arbitrary"),
    )(dgate, proj, bias_tile)


def _chunk_gates(gc, gr, h, mprev):
    L = CHUNK
    lane = _iota(gc.shape, 1)
    sub = _iota(gr.shape, 0)
    icol = jnp.sum(jnp.where(lane == h, gc, 0.0), axis=1, keepdims=True)
    bcol = jnp.sum(jnp.where(lane == h + NH, gc, 0.0), axis=1, keepdims=True)
    irow = jnp.sum(jnp.where(sub == h, gr, 0.0), axis=0, keepdims=True)
    brow = jnp.sum(jnp.where(sub == h + NH, gr, 0.0), axis=0, keepdims=True)
    tri = _iota((L, L), 0) >= _iota((L, L), 1)
    log_d = jnp.where(tri, bcol - brow + irow, -jnp.inf)
    inter = bcol + mprev
    mt = jnp.maximum(inter, jnp.max(log_d, axis=1, keepdims=True))
    dw = jnp.exp(log_d - mt)
    iw = jnp.exp(inter - mt)
    g = jnp.sum(jnp.where(_iota((L, 1), 0) == L - 1, bcol, 0.0), axis=0, keepdims=True)
    wlog_col = g - bcol + icol
    wlog_row = g - brow + irow
    mnew = jnp.maximum(g + mprev, jnp.max(wlog_row, axis=1, keepdims=True))
    wcol = jnp.exp(wlog_col - mnew)
    decay = jnp.exp(g + mprev - mnew)
    return dw, iw, mt, wcol, decay, mnew


def _mlstm_fwd(proj, gcol, grow, t, wc, dh):
    nc = t // CHUNK
    qoff = 3 * wc // dh
    scale = dh ** -0.5
    wm = NH * dh

    def body(q_ref, k_ref, v_ref, gc_ref, gr_ref, h_ref, cs_ref, ns_ref, c_s, n_s, m_s):
        h = pl.program_id(0)
        c = pl.program_id(1)

        @pl.when(c == 0)
        def _():
            c_s[...] = jnp.zeros_like(c_s)
            n_s[...] = jnp.zeros_like(n_s)
            m_s[...] = jnp.zeros_like(m_s)

        mprev = m_s[0:1, 0:1]
        cprev = c_s[...]
        n8 = n_s[...]
        nprev = n8[0:1]
        cs_ref[...] = cprev
        ns_ref[...] = jnp.where(_iota(n8.shape, 0) == 1, mprev, n8)

        dw, iw, mt, wcol, decay, mnew = _chunk_gates(gc_ref[...], gr_ref[0], h, mprev)
        qs = q_ref[...] * scale
        k = k_ref[...]
        qs_b, k_b, v_b = qs.astype(BF16), k.astype(BF16), v_ref[...].astype(BF16)
        s = _dot(qs_b, k_b, _NT) * dw
        num = _dot(s.astype(BF16), v_b) + iw * _dot(qs_b, cprev.astype(BF16))
        den = jnp.sum(s, axis=1, keepdims=True) + iw * jnp.sum(qs * nprev, axis=1, keepdims=True)
        h_ref[...] = num / jnp.maximum(jnp.abs(den), jnp.exp(-mt))

        wk = wcol * k
        c_s[...] = decay * cprev + _dot(wk.astype(BF16), v_b, _TN)
        n_s[...] = decay * n8 + jnp.sum(wk, axis=0, keepdims=True)
        m_s[...] = jnp.broadcast_to(mnew, m_s.shape)

    hd = lambda off: pl.BlockSpec((CHUNK, dh), lambda h, c: (c, qoff + off * NH + h))
    return pl.pallas_call(
        body, name="mlstm_fwd", grid=(NH, nc),
        in_specs=[hd(0), hd(1), hd(2),
                  pl.BlockSpec((CHUNK, LANE), lambda h, c: (c, 0)),
                  pl.BlockSpec((1, 8, CHUNK), lambda h, c: (c, 0, 0))],
        out_specs=[pl.BlockSpec((CHUNK, dh), lambda h, c: (c, h)),
                   pl.BlockSpec((None, None, dh, dh), lambda h, c: (h, c, 0, 0)),
                   pl.BlockSpec((None, None, 8, dh), lambda h, c: (h, c, 0, 0))],
        out_shape=[jax.ShapeDtypeStruct((t, wm), F32),
                   jax.ShapeDtypeStruct((NH, nc, dh, dh), F32),
                   jax.ShapeDtypeStruct((NH, nc, 8, dh), F32)],
        scratch_shapes=[pltpu.VMEM((dh, dh), F32), pltpu.VMEM((8, dh), F32), pltpu.VMEM((8, LANE), F32)],
        compiler_params=_params("parallel", "arbitrary"),
    )(proj, proj, proj, gcol, grow)


def _mlstm_bwd(proj, gcol, grow, hval, dh_in, cs, ns, t, wc, dh):
    nc = t // CHUNK
    qoff = 3 * wc // dh
    scale = dh ** -0.5
    wm = NH * dh
    L = CHUNK

    def body(q_ref, k_ref, v_ref, gc_ref, gr_ref, h_ref, dh_ref, cs_ref, ns_ref,
             dq_ref, dk_ref, dv_ref, dg_ref, dc_s, dn_s):
        h = pl.program_id(0)
        step = pl.program_id(1)

        @pl.when(step == 0)
        def _():
            dc_s[...] = jnp.zeros_like(dc_s)
            dn_s[...] = jnp.zeros_like(dn_s)

        ns8 = ns_ref[...]
        nprev = ns8[0:1]
        mprev = ns8[1:2, 0:1]
        cprev = cs_ref[...]
        dcn = dc_s[...]
        dn8 = dn_s[...]
        dnn = dn8[0:1]

        dw, iw, mt, wcol, decay, _ = _chunk_gates(gc_ref[...], gr_ref[0], h, mprev)
        qs = q_ref[...] * scale
        k = k_ref[...]
        qs_b, k_b, v_b = qs.astype(BF16), k.astype(BF16), v_ref[...].astype(BF16)
        qk = _dot(qs_b, k_b, _NT)
        s = qk * dw
        den = jnp.sum(s, axis=1, keepdims=True) + iw * jnp.sum(qs * nprev, axis=1, keepdims=True)
        emt = jnp.exp(-mt)
        r = 1.0 / jnp.maximum(jnp.abs(den), emt)
        dout = dh_ref[...]
        dnum = dout * r
        dden = (-jnp.sum(dout * h_ref[...], axis=1, keepdims=True) * r
                * jnp.where(jnp.abs(den) > emt, jnp.sign(den), 0.0))
        dnum_b = dnum.astype(BF16)
        cprev_b = cprev.astype(BF16)
        dcn_b = dcn.astype(BF16)

        gd = (_dot(dnum_b, v_b, _NT) + dden) * dw
        gd_b = gd.astype(BF16)
        dqs_inter = iw * (_dot(dnum_b, cprev_b, _NT) + dden * nprev)
        dqs = _dot(gd_b, k_b) + dqs_inter
        dk_inter = wcol * (_dot(v_b, dcn_b, _NT) + dnn)
        dk = _dot(gd_b, qs_b, _TN) + dk_inter
        wk = wcol * k
        dv = _dot(s.astype(BF16), dnum_b, _TN) + _dot(wk.astype(BF16), dcn_b)

        e = gd * qk
        eye = _iota((L, L), 0) == _iota((L, L), 1)
        e_cols = jnp.sum(jnp.where(eye, jnp.sum(e, axis=0, keepdims=True), 0.0), axis=1, keepdims=True)
        k_inter = jnp.sum(k * dk_inter, axis=1, keepdims=True)
        rq = jnp.sum(e, axis=1, keepdims=True) + jnp.sum(qs * dqs_inter, axis=1, keepdims=True)
        rk = e_cols + k_inter
        hsum = jnp.sum(k_inter, axis=0, keepdims=True)
        jdec = decay * (jnp.sum(jnp.sum(dcn * cprev, axis=1, keepdims=True), axis=0, keepdims=True)
                        + jnp.sum(dnn * nprev, axis=1, keepdims=True))
        db = rq - rk + jnp.where(_iota((L, 1), 0) == L - 1, hsum + jdec, 0.0)
        lane = _iota((L, LANE), 1)
        dg_ref[...] = jnp.where(lane == 0, rk, jnp.where(lane == 1, db, 0.0))

        dq_ref[...] = (dqs * scale).astype(BF16)
        dk_ref[...] = dk.astype(BF16)
        dv_ref[...] = dv.astype(BF16)

        iq = iw * qs
        dc_s[...] = decay * dcn + _dot(iq.astype(BF16), dnum_b, _TN)
        dn_s[...] = decay * dn8 + jnp.sum(iq * dden, axis=0, keepdims=True)

    rc = lambda c: nc - 1 - c
    hd = lambda off: pl.BlockSpec((L, dh), lambda h, c: (rc(c), qoff + off * NH + h))
    hm = pl.BlockSpec((L, dh), lambda h, c: (rc(c), h))
    act = jax.ShapeDtypeStruct((t, wm), BF16)
    return pl.pallas_call(
        body, name="mlstm_bwd", grid=(NH, nc),
        in_specs=[hd(0), hd(1), hd(2),
                  pl.BlockSpec((L, LANE), lambda h, c: (rc(c), 0)),
                  pl.BlockSpec((1, 8, L), lambda h, c: (rc(c), 0, 0)),
                  hm, hm,
                  pl.BlockSpec((None, None, dh, dh), lambda h, c: (h, rc(c), 0, 0)),
                  pl.BlockSpec((None, None, 8, dh), lambda h, c: (h, rc(c), 0, 0))],
        out_specs=[hm, hm, hm, pl.BlockSpec((None, L, LANE), lambda h, c: (h, rc(c), 0))],
        out_shape=[act, act, act, jax.ShapeDtypeStruct((NH, t, LANE), F32)],
        scratch_shapes=[pltpu.VMEM((dh, dh), F32), pltpu.VMEM((8, dh), F32)],
        compiler_params=_params("parallel", "arbitrary"),
    )(proj, proj, proj, gcol, grow, hval, dh_in, cs, ns)


def _head_norm(hv):
    mu = jnp.mean(hv, axis=1, keepdims=True)
    hc = hv - mu
    rstd = lax.rsqrt(jnp.mean(hc * hc, axis=1, keepdims=True) + HN_EPS)
    return hc * rstd, rstd


def _hnorm_fwd(hval, proj, gain, t, wc, dh, tr=256):
    ooff = 3 * wc // dh + 3 * NH

    def body(h_ref, o_ref, g_ref, y_ref):
        hhat, _ = _head_norm(h_ref[...])
        y_ref[...] = (_sigmoid(o_ref[...]) * hhat * g_ref[...]).astype(BF16)

    return pl.pallas_call(
        body, name="hnorm_fwd", grid=(t // tr, NH),
        in_specs=[pl.BlockSpec((tr, dh), lambda i, h: (i, h)),
                  pl.BlockSpec((tr, dh), lambda i, h: (i, ooff + h)),
                  pl.BlockSpec((1, dh), lambda i, h: (0, h))],
        out_specs=pl.BlockSpec((tr, dh), lambda i, h: (i, h)),
        out_shape=jax.ShapeDtypeStruct((t, NH * dh), BF16),
        compiler_params=_params("parallel", "parallel"),
    )(hval, proj, gain)


def _hnorm_bwd(dy, hval, proj, gain, t, wc, dh, tr=256):
    ooff = 3 * wc // dh + 3 * NH
    yoff = wc // dh

    def body(dy_ref, h_ref, o_ref, g_ref, do_ref, dh_ref, dg_ref):
        i = pl.program_id(1)
        hhat, rstd = _head_norm(h_ref[...])
        gain_v = g_ref[...]
        sig = _sigmoid(o_ref[...])
        d = dy_ref[...]
        do_ref[...] = (d * hhat * gain_v * sig * (1.0 - sig)).astype(BF16)
        dhn = d * sig
        part = jnp.sum(dhn * hhat, axis=0, keepdims=True)

        @pl.when(i == 0)
        def _():
            dg_ref[...] = part

        @pl.when(i > 0)
        def _():
            dg_ref[...] += part

        dhat = dhn * gain_v
        dh_ref[...] = rstd * (dhat - jnp.mean(dhat, axis=1, keepdims=True)
                              - hhat * jnp.mean(dhat * hhat, axis=1, keepdims=True))

    blk = lambda off: pl.BlockSpec((tr, dh), lambda h, i: (i, off + h))
    return pl.pallas_call(
        body, name="hnorm_bwd", grid=(NH, t // tr),
        in_specs=[blk(yoff), blk(0), blk(ooff), pl.BlockSpec((1, dh), lambda h, i: (0, h))],
        out_specs=[blk(0), blk(0), pl.BlockSpec((1, dh), lambda h, i: (0, h))],
        out_shape=[jax.ShapeDtypeStruct((t, NH * dh), BF16), jax.ShapeDtypeStruct((t, NH * dh), F32),
                   jax.ShapeDtypeStruct((1, NH * dh), F32)],
        compiler_params=_params("parallel", "arbitrary"),
    )(dy, hval, proj, gain)


def _ln_stats(z):
    mu = jnp.mean(z, axis=1, keepdims=True)
    zc = z - mu
    rstd = lax.rsqrt(jnp.mean(zc * zc, axis=1, keepdims=True) + LN_EPS)
    return zc * rstd, rstd


def _ln_bwd(dy, xhat, rstd, g):
    dxh = dy * g
    return rstd * (dxh - jnp.mean(dxh, axis=1, keepdims=True) - xhat * jnp.mean(dxh * xhat, axis=1, keepdims=True))


def _accum(ref, i, part):
    @pl.when(i == 0)
    def _():
        ref[...] = part

    @pl.when(i > 0)
    def _():
        ref[...] += part


def _ln1_fwd(x, mix, g, b, tr=256):
    t, d = x.shape

    def body(x_ref, m_ref, g_ref, b_ref, xh_ref, rs_ref, xb_ref):
        xhat, rstd = _ln_stats(ALPHA * x_ref[...] + m_ref[...])
        xh_ref[...] = xhat
        rs_ref[...] = rstd
        xb_ref[...] = (xhat * g_ref[...] + b_ref[...]).astype(BF16)

    row = pl.BlockSpec((tr, d), lambda i: (i, 0))
    vec = pl.BlockSpec((1, d), lambda i: (0, 0))
    return pl.pallas_call(
        body, name="ln1_fwd", grid=(t // tr,),
        in_specs=[row, row, vec, vec],
        out_specs=[row, pl.BlockSpec((tr, 1), lambda i: (i, 0)), row],
        out_shape=[jax.ShapeDtypeStruct((t, d), F32), jax.ShapeDtypeStruct((t, 1), F32),
                   jax.ShapeDtypeStruct((t, d), BF16)],
        compiler_params=_params("parallel"),
    )(x, mix, g, b)


def _ln2_loss(xhat1, g1, b1, ff, target, g2, b2, tr=256):
    t, d = ff.shape

    def body(xh_ref, g1_ref, b1_ref, f_ref, t_ref, g_ref, b_ref, dz_ref, dzb_ref, dg_ref, db_ref, l_ref):
        i = pl.program_id(0)
        x1 = xh_ref[...] * g1_ref[...] + b1_ref[...]
        xhat, rstd = _ln_stats(ALPHA * x1 + f_ref[...])
        gv = g_ref[...]
        e = xhat * gv + b_ref[...] - t_ref[...]
        lsum = jnp.sum(jnp.sum(e * e, axis=1, keepdims=True), axis=0, keepdims=True) * (0.5 / d)
        dy = e * (1.0 / d)
        _accum(dg_ref, i, jnp.sum(dy * xhat, axis=0, keepdims=True))
        _accum(db_ref, i, jnp.sum(dy, axis=0, keepdims=True))
        _accum(l_ref, i, jnp.broadcast_to(lsum, l_ref.shape))
        dz = _ln_bwd(dy, xhat, rstd, gv)
        dz_ref[...] = dz
        dzb_ref[...] = dz.astype(BF16)

    row = pl.BlockSpec((tr, d), lambda i: (i, 0))
    vec = pl.BlockSpec((1, d), lambda i: (0, 0))
    return pl.pallas_call(
        body, name="ln2_loss", grid=(t // tr,),
        in_specs=[row, vec, vec, row, row, vec, vec],
        out_specs=[row, row, vec, vec, pl.BlockSpec((8, LANE), lambda i: (0, 0))],
        out_shape=[jax.ShapeDtypeStruct((t, d), F32), jax.ShapeDtypeStruct((t, d), BF16),
                   jax.ShapeDtypeStruct((1, d), F32), jax.ShapeDtypeStruct((1, d), F32),
                   jax.ShapeDtypeStruct((8, LANE), F32)],
        compiler_params=_params("arbitrary"),
    )(xhat1, g1, b1, ff, target, g2, b2)


def _ln1_bwd(dz2, dffn, xhat1, rstd1, g1, tr=256):
    t, d = dz2.shape

    def body(a_ref, f_ref, xh_ref, rs_ref, g_ref, dz_ref, dzb_ref, dg_ref, db_ref):
        i = pl.program_id(0)
        dy = ALPHA * a_ref[...] + f_ref[...]
        xhat = xh_ref[...]
        _accum(dg_ref, i, jnp.sum(dy * xhat, axis=0, keepdims=True))
        _accum(db_ref, i, jnp.sum(dy, axis=0, keepdims=True))
        dz = _ln_bwd(dy, xhat, rs_ref[...], g_ref[...])
        dz_ref[...] = dz
        dzb_ref[...] = dz.astype(BF16)

    row = pl.BlockSpec((tr, d), lambda i: (i, 0))
    vec = pl.BlockSpec((1, d), lambda i: (0, 0))
    return pl.pallas_call(
        body, name="ln1_bwd", grid=(t // tr,),
        in_specs=[row, row, row, pl.BlockSpec((tr, 1), lambda i: (i, 0)), vec],
        out_specs=[row, row, vec, vec],
        out_shape=[jax.ShapeDtypeStruct((t, d), F32), jax.ShapeDtypeStruct((t, d), BF16),
                   jax.ShapeDtypeStruct((1, d), F32), jax.ShapeDtypeStruct((1, d), F32)],
        compiler_params=_params("arbitrary"),
    )(dz2, dffn, xhat1, rstd1, g1)


def _ffn_act_fwd(hid0, w_fc, b_fc, t, dff):
    nb = dff // LANE

    def body(hv_ref, hg_ref, wv_ref, wg_ref, bv_ref, bg_ref, a_ref):
        val = _conv(hv_ref[...], wv_ref[...]) + bv_ref[...]
        gate = _conv(hg_ref[...], wg_ref[...]) + bg_ref[...]
        a_ref[...] = (gate * _sigmoid(gate) * val).astype(BF16)

    col = lambda off: pl.BlockSpec((t, LANE), lambda j: (0, j + off))
    w3 = lambda off: pl.BlockSpec((3, LANE), lambda j: (0, j + off))
    w1 = lambda off: pl.BlockSpec((1, LANE), lambda j: (0, j + off))
    return pl.pallas_call(
        body, name="ffn_act_fwd", grid=(nb,),
        in_specs=[col(0), col(nb), w3(0), w3(nb), w1(0), w1(nb)],
        out_specs=col(0),
        out_shape=jax.ShapeDtypeStruct((t, dff), BF16),
        compiler_params=_params("parallel"),
    )(hid0, hid0, w_fc, w_fc, b_fc, b_fc)


def _ffn_act_bwd(da, hid0, w_fc, b_fc, t, dff):
    nb = dff // LANE

    def body(da_ref, hv_ref, hg_ref, wv_ref, wg_ref, bv_ref, bg_ref,
             dhv_ref, dhg_ref, dwv_ref, dwg_ref, dbv_ref, dbg_ref):
        hv, hg, wv, wg = hv_ref[...], hg_ref[...], wv_ref[...], wg_ref[...]
        val = _conv(hv, wv) + bv_ref[...]
        gate = _conv(hg, wg) + bg_ref[...]
        sig = _sigmoid(gate)
        d = da_ref[...]
        dval = d * gate * sig
        dgate = d * val * sig * (1.0 + gate * (1.0 - sig))
        dhv_ref[...] = _conv_t(dval, wv).astype(BF16)
        dhg_ref[...] = _conv_t(dgate, wg).astype(BF16)
        dwv_ref[...] = _conv_dw(dval, hv)
        dwg_ref[...] = _conv_dw(dgate, hg)
        dbv_ref[...] = jnp.sum(dval, axis=0, keepdims=True)
        dbg_ref[...] = jnp.sum(dgate, axis=0, keepdims=True)

    col = lambda off: pl.BlockSpec((t, LANE), lambda j: (0, j + off))
    w3 = lambda off: pl.BlockSpec((3, LANE), lambda j: (0, j + off))
    w1 = lambda off: pl.BlockSpec((1, LANE), lambda j: (0, j + off))
    act = jax.ShapeDtypeStruct((t, dff), BF16)
    s3 = jax.ShapeDtypeStruct((3, dff), F32)
    s1 = jax.ShapeDtypeStruct((1, dff), F32)
    return pl.pallas_call(
        body, name="ffn_act_bwd", grid=(nb,),
        in_specs=[col(0), col(0), col(nb), w3(0), w3(nb), w1(0), w1(nb)],
        out_specs=[col(0), col(0), w3(0), w3(0), w1(0), w1(0)],
        out_shape=[act, act, s3, s3, s1, s1],
        compiler_params=_params("parallel"),
    )(da, hid0, hid0, w_fc, w_fc, b_fc, b_fc)


def _local_step(x, target, w_in, b_gates, w_sc, gain, w_out, ln1_g, ln1_b, w_up, w_fc, b_fc, w_down, ln2_g, ln2_b,
                on_grad=None):
    t, d = x.shape
    wc = d // 2
    dh = (d - wc) // NH
    wm = NH * dh
    dff = w_down.shape[0]
    ninp = w_in.shape[1]
    nin = 3 * wc + 4 * wm
    gate_tile = nin // LANE
    nc = t // CHUNK
    bias_tile = jnp.pad(b_gates, ((0, 0), (0, LANE - 2 * NH)))

    x_b = x.astype(BF16)
    proj = _matmul(x_b, w_in, "nn", F32, "proj", tn=1152)
    y_conv = _sconv_fwd(proj, w_sc, t, wc)
    gcol = _gates_prep(proj, bias_tile, t, gate_tile)
    grow = gcol[:, :8].T.reshape(8, nc, CHUNK).transpose(1, 0, 2)
    hval, cs, ns = _mlstm_fwd(proj, gcol, grow, t, wc, dh)
    y_m = _hnorm_fwd(hval, proj, gain, t, wc, dh)
    y = jnp.concatenate([y_conv, y_m], axis=1)
    mix = _matmul(y, w_out, "nn", F32, "out_proj")
    xhat1, rstd1, x1_b = _ln1_fwd(x, mix, ln1_g, ln1_b)
    wsl = w_up.shape[2]
    hid0 = _matmul(x1_b, w_up, "nn", F32, "ffn_up", tn=wsl, b_blocked=True)
    act = _ffn_act_fwd(hid0, w_fc, b_fc, t, dff)
    ff = _matmul(act, w_down, "nn", F32, "ffn_down")
    dz2, dz2_b, d_ln2_g, d_ln2_b, loss = _ln2_loss(xhat1, ln1_g, ln1_b, ff, target, ln2_g, ln2_b)

    emit = on_grad if on_grad is not None else (lambda name, g: None)
    d_w_down = _matmul(act, dz2_b, "tn", BF16, "ffn_down_dw", tm=512, tn=1024, tk=t)
    d_act = _matmul(dz2_b, w_down, "nt", F32, "ffn_down_dx", after=emit("w_down", d_w_down))
    dhv, dhg, dwv, dwg, dbv, dbg = _ffn_act_bwd(d_act, hid0, w_fc, b_fc, t, dff)
    d_hid0 = jnp.concatenate([dhv, dhg], axis=1)
    d_w_fc = jnp.concatenate([dwv, dwg], axis=1)
    d_b_fc = jnp.concatenate([dbv, dbg], axis=1)
    d_w_up = _matmul(x1_b, d_hid0, "tn", BF16, "ffn_up_dw", tm=512, tn=wsl, tk=t, o_width=wsl)
    d_x1_ffn = _matmul(d_hid0, w_up, "nt", F32, "ffn_up_dx", tk=wsl, b_blocked=True, after=emit("w_up", d_w_up))
    dz1, dz1_b, d_ln1_g, d_ln1_b = _ln1_bwd(dz2, d_x1_ffn, xhat1, rstd1, ln1_g)

    d_w_out = _matmul(y, dz1_b, "tn", BF16, "out_proj_dw", tm=512, tn=1024, tk=t)
    dy = _matmul(dz1_b, w_out, "nt", F32, "out_proj_dx", after=emit("w_out", d_w_out))
    dcb, dcc, dch, d_w_sc = _sconv_bwd(dy, proj, w_sc, t, wc)
    d_o, d_hval, d_gain = _hnorm_bwd(dy, hval, proj, gain, t, wc, dh)
    dq, dk, dv, dgate = _mlstm_bwd(proj, gcol, grow, hval, d_hval, cs, ns, t, wc, dh)
    dgt, d_b_gates = _gates_bwd(dgate, proj, bias_tile, t, gate_tile)
    pad = jnp.zeros((t, ninp - nin - LANE), BF16)
    d_proj = jnp.concatenate([dcb, dcc, dch, dq, dk, dv, d_o, dgt, pad], axis=1)
    d_w_in = _matmul(x_b, d_proj, "tn", BF16, "proj_dw", tm=512, tn=IN_SLAB, tk=t, o_width=IN_SLAB)
    grad_x = _matmul(d_proj, w_in, "nt", F32, "proj_dx", tk=1152, add=dz1, add_scale=ALPHA,
                     after=emit("w_in", d_w_in))

    small = dict(b_gates=d_b_gates[:, :2 * NH], w_sc_conv=d_w_sc, mh_gain=d_gain, ln1_g=d_ln1_g, ln1_b=d_ln1_b,
                 w_ffn_conv=d_w_fc, b_ffn_conv=d_b_fc, ln2_g=d_ln2_g, ln2_b=d_ln2_b)
    return loss, grad_x, d_w_in, d_w_out, d_w_up, d_w_down, small


HBM = pl.BlockSpec(memory_space=pltpu.HBM)


def _place():
    return lax.axis_index("x"), lax.axis_index("y"), lax.axis_index("c")


def _index(p):
    return 4 * p[0] + 2 * p[1] + p[2]


def _all_gather(arrs, name):
    n = len(arrs)

    def body(*refs):
        ins, outs = refs[:n], refs[n:2 * n]
        send_sems, recv_sems, local_sems = refs[2 * n:]
        x, y, c = _place()
        me, sibling = (x, y, c), (x, y, 1 - c)
        chips = [(1 - x, y), (x, 1 - y), (1 - x, 1 - y)]

        def copy(a, k, block, to, own=False):
            dst = outs[a].at[_index(block)]
            return pltpu.make_async_remote_copy(
                src_ref=ins[a] if own else dst, dst_ref=dst,
                send_sem=send_sems.at[k * n + a], recv_sem=recv_sems.at[k * n + a],
                device_id=to, device_id_type=MESH)

        mine = [pltpu.make_async_copy(ins[a], outs[a].at[_index(me)], local_sems.at[a]) for a in range(n)]
        for cp in mine:
            cp.start()
        first = []
        for a in range(n):
            first.append(copy(a, 0, me, sibling, own=True))
            first += [copy(a, 1 + j, me, (*chip, c), own=True) for j, chip in enumerate(chips)]
        for cp in first:
            cp.start()
        passed = []
        for j, chip in enumerate(chips):
            for a in range(n):
                copy(a, 1 + j, (*chip, c), me).wait_recv()
                cp = copy(a, 4 + j, (*chip, c), sibling)
                cp.start()
                passed.append(cp)
        for a in range(n):
            copy(a, 0, sibling, me).wait_recv()
            for j, chip in enumerate(chips):
                copy(a, 4 + j, (*chip, 1 - c), me).wait_recv()
        for cp in first + passed:
            cp.wait_send()
        for cp in mine:
            cp.wait()

    return pl.pallas_call(
        body, name=name, in_specs=[HBM] * n, out_specs=[HBM] * n,
        out_shape=[jax.ShapeDtypeStruct((N_DEV,) + a.shape, a.dtype) for a in arrs],
        scratch_shapes=[pltpu.SemaphoreType.DMA((7 * n,)), pltpu.SemaphoreType.DMA((7 * n,)),
                        pltpu.SemaphoreType.DMA((n,))],
    )(*arrs)


SEM = pl.BlockSpec(memory_space=pltpu.SEMAPHORE)
EFFECT = pltpu.SideEffectType.DATAFLOW_SIDE_EFFECTING


def _peers(x, y, c):
    return [(1 - x if j & 4 else x, 1 - y if j & 2 else y, 1 - c if j & 1 else c) for j in range(1, N_DEV)]


def _exchange_pieces(g_ref, land_ref, width, tail):
    if not tail:
        return [(lambda i: g_ref.at[i], lambda s: land_ref.at[s])]
    return [(lambda i: g_ref.at[i], lambda s: land_ref.at[s, :, pl.ds(0, width)]),
            (lambda i: g_ref.at[i + 1, :, pl.ds(0, LANE)], lambda s: land_ref.at[s, :, pl.ds(width, LANE)])]


def _exchange_start(grad, tail, name):
    width = grad.shape[2]
    n_p = 2 if tail else 1
    land_shape = (N_DEV, grad.shape[1], width + (LANE if tail else 0))

    def body(g_ref, land_ref, send_sems, recv_sems, g_thru, land_thru, token):
        x, y, c = _place()
        me = _index((x, y, c))
        for j, peer in enumerate(_peers(x, y, c)):
            for p, (src, dst) in enumerate(_exchange_pieces(g_ref, land_ref, width, tail)):
                pltpu.make_async_remote_copy(src_ref=src(_index(peer)), dst_ref=dst(me), send_sem=send_sems.at[j * n_p + p],
                                             recv_sem=recv_sems.at[j * n_p + p], device_id=peer,
                                             device_id_type=MESH).start()
        token[...] = jnp.zeros_like(token)

    return pl.pallas_call(
        body, name=name,
        out_shape=(pltpu.SemaphoreType.DMA((7 * n_p,)), pltpu.SemaphoreType.DMA((7 * n_p,)),
                   pltpu.HBM(grad.shape, grad.dtype), pltpu.HBM(land_shape, grad.dtype),
                   jax.ShapeDtypeStruct((8, LANE), F32)),
        in_specs=(HBM, HBM), out_specs=(SEM, SEM, HBM, HBM, pl.BlockSpec(memory_space=pltpu.VMEM)),
        input_output_aliases={0: 2, 1: 3},
        compiler_params=pltpu.CompilerParams(has_side_effects=EFFECT),
    )(pltpu.with_memory_space_constraint(grad, pltpu.HBM),
      pltpu.with_memory_space_constraint(lax.empty(land_shape, grad.dtype), pltpu.HBM))


def _exchange_wait(send_sems, recv_sems, g_thru, land_thru, after, tail, name):
    width = g_thru.shape[2]
    n_p = 2 if tail else 1

    def body(g_ref, land_ref, send_sems, recv_sems, after_ref, g_dead, got_ref):
        x, y, c = _place()
        for j, peer in enumerate(_peers(x, y, c)):
            for p, (src, dst) in enumerate(_exchange_pieces(g_ref, land_ref, width, tail)):
                cp = pltpu.make_async_remote_copy(src_ref=src(_index(peer)), dst_ref=dst(_index(peer)),
                                                  send_sem=send_sems.at[j * n_p + p], recv_sem=recv_sems.at[j * n_p + p],
                                                  device_id=peer, device_id_type=MESH)
                cp.wait_send()
                cp.wait_recv()

    return pl.pallas_call(
        body, name=name,
        out_shape=(pltpu.HBM(g_thru.shape, g_thru.dtype), pltpu.HBM(land_thru.shape, land_thru.dtype)),
        in_specs=(HBM, HBM, SEM, SEM, pl.BlockSpec(memory_space=pl.ANY)), out_specs=(HBM, HBM),
        input_output_aliases={0: 0, 1: 1},
        compiler_params=pltpu.CompilerParams(has_side_effects=EFFECT),
    )(g_thru, land_thru, send_sems, recv_sems, after)


def _own_slab(grad, land, me, tail):
    own = lax.dynamic_index_in_dim(grad, me, 0, keepdims=True)
    if tail:
        nxt = lax.dynamic_index_in_dim(grad, me + 1, 0, keepdims=True)[:, :, :LANE]
        own = jnp.concatenate([own, nxt], axis=2)
    return lax.dynamic_update_index_in_dim(land, own, me, 0)


def _assemble_w_in(g, ninp):
    _, d, pw = g.shape
    per = IN_SLAB // LANE
    last = N_DEV * per

    def body(a_ref, b_ref, o_ref):
        t = pl.program_id(0)
        main = jnp.where(t <= last, a_ref[...], jnp.zeros_like(a_ref))
        carry = jnp.where((t % per == 0) & (t > 0) & (t < last), b_ref[...], jnp.zeros_like(b_ref))
        o_ref[...] = main + carry

    def main_map(t):
        k = jnp.minimum(t // per, N_DEV - 1)
        return k, 0, jnp.minimum(t - per * k, per)

    def carry_map(t):
        return jnp.maximum(jnp.minimum(t // per, N_DEV - 1), 1) - 1, 0, per

    return pl.pallas_call(
        body, name="assemble_w_in", grid=(ninp // LANE,),
        in_specs=[pl.BlockSpec((None, d, LANE), main_map), pl.BlockSpec((None, d, LANE), carry_map)],
        out_specs=pl.BlockSpec((d, LANE), lambda t: (0, t)),
        out_shape=jax.ShapeDtypeStruct((d, ninp), g.dtype),
        compiler_params=_params("parallel"),
    )(g, g)


def _rows(n, want):
    t = min(n, want)
    t -= t % 16
    while n % t:
        t -= 16
    return t


def _adam_math(w, g, m, v):
    m2 = ADAM_B1 * m + (1.0 - ADAM_B1) * g
    v2 = ADAM_B2 * v + (1.0 - ADAM_B2) * (g * g)
    m_hat = m2 / (1.0 - ADAM_B1 ** ADAM_STEP)
    v_hat = v2 / (1.0 - ADAM_B2 ** ADAM_STEP)
    return -ADAM_LR * (m_hat / (jnp.sqrt(v_hat) + ADAM_EPS) + ADAM_WD * w), m2, v2


def _slot_sum(r_ref):
    acc = r_ref[0].astype(F32)
    for i in range(1, N_DEV):
        acc = acc + r_ref[i].astype(F32)
    return acc


def _sum_slots(r, name, tr=128):
    _, rows, cols = r.shape
    tr = _rows(rows, tr)

    def body(r_ref, g_ref):
        g_ref[...] = _slot_sum(r_ref)

    return pl.pallas_call(
        body, name=name, grid=(rows // tr,),
        in_specs=[pl.BlockSpec((N_DEV, tr, cols), lambda i: (0, i, 0))],
        out_specs=pl.BlockSpec((tr, cols), lambda i: (i, 0)),
        out_shape=jax.ShapeDtypeStruct((rows, cols), F32),
        compiler_params=_params("parallel"),
    )(r)


def _adamw(w, g, m, v, name, tr=256):
    rows, cols = w.shape
    tr = _rows(rows, tr)

    def body(w_ref, g_ref, m_ref, v_ref, d_ref, m2_ref, v2_ref):
        d_ref[...], m2_ref[...], v2_ref[...] = _adam_math(w_ref[...], g_ref[...], m_ref[...], v_ref[...])

    blk = pl.BlockSpec((tr, cols), lambda i: (i, 0))
    out = jax.ShapeDtypeStruct((rows, cols), F32)
    return pl.pallas_call(
        body, name=name, grid=(rows // tr,), in_specs=[blk] * 4, out_specs=[blk] * 3, out_shape=[out] * 3,
        compiler_params=_params("parallel"),
    )(w, g, m, v)


def _sum_adamw(r, w, m, v, name, tr=128):
    rows, cols = w.shape
    tr = _rows(rows, tr)

    def body(r_ref, w_ref, m_ref, v_ref, g_ref, d_ref, m2_ref, v2_ref):
        g = _slot_sum(r_ref)
        g_ref[...] = g
        d_ref[...], m2_ref[...], v2_ref[...] = _adam_math(w_ref[...], g, m_ref[...], v_ref[...])

    blk = pl.BlockSpec((tr, cols), lambda i: (i, 0))
    out = jax.ShapeDtypeStruct((rows, cols), F32)
    return pl.pallas_call(
        body, name=name, grid=(rows // tr,),
        in_specs=[pl.BlockSpec((N_DEV, tr, cols), lambda i: (0, i, 0)), blk, blk, blk],
        out_specs=[blk] * 4, out_shape=[out] * 4,
        compiler_params=_params("parallel"),
    )(r, w, m, v)


def _pack(pieces, sizes):
    flat = [jnp.pad(p.reshape(-1).astype(F32), (0, s - p.size)) for p, s in zip(pieces, sizes)]
    total = sum(sizes)
    padded = -(-total // (16 * LANE)) * (16 * LANE)
    return jnp.pad(jnp.concatenate(flat), (0, padded - total)).reshape(-1, LANE)


def _unpack(packed, shapes, sizes):
    flat = packed.reshape(-1)
    out, off = [], 0
    for shp, s in zip(shapes, sizes):
        n = 1
        for k in shp:
            n *= k
        out.append(flat[off:off + n].reshape(shp))
        off += s
    return out


def _lanes(n):
    return -(-n // LANE) * LANE


WEIGHTS = ("w_in", "b_gates", "w_sc_conv", "mh_gain", "w_out", "ln1_g", "ln1_b", "w_up", "w_ffn_conv", "b_ffn_conv",
           "w_down", "ln2_g", "ln2_b")
BIG = ("w_in", "w_out", "w_up", "w_down")
SMALL = tuple(n for n in WEIGHTS if n not in BIG)


def kernel(x, w_in, b_gates, w_sc_conv, mh_gain, w_out, ln1_g, ln1_b, w_up, w_ffn_conv, b_ffn_conv, w_down, ln2_g, ln2_b, loss_target, m_w_in, m_b_gates, m_w_sc_conv, m_mh_gain, m_w_out, m_ln1_g, m_ln1_b, m_w_up, m_w_ffn_conv, m_b_ffn_conv, m_w_down, m_ln2_g, m_ln2_b, v_w_in, v_b_gates, v_w_sc_conv, v_mh_gain, v_w_out, v_ln1_g, v_ln1_b, v_w_up, v_w_ffn_conv, v_b_ffn_conv, v_w_down, v_ln2_g, v_ln2_b):
    w = dict(zip(WEIGHTS, (w_in, b_gates, w_sc_conv, mh_gain, w_out, ln1_g, ln1_b, w_up, w_ffn_conv, b_ffn_conv,
                           w_down, ln2_g, ln2_b)))
    m = dict(zip(WEIGHTS, (m_w_in, m_b_gates, m_w_sc_conv, m_mh_gain, m_w_out, m_ln1_g, m_ln1_b, m_w_up,
                           m_w_ffn_conv, m_b_ffn_conv, m_w_down, m_ln2_g, m_ln2_b)))
    v = dict(zip(WEIGHTS, (v_w_in, v_b_gates, v_w_sc_conv, v_mh_gain, v_w_out, v_ln1_g, v_ln1_b, v_w_up,
                           v_w_ffn_conv, v_b_ffn_conv, v_w_down, v_ln2_g, v_ln2_b)))
    me = _index(_place())
    d = x.shape[2]
    ws_in = w_in.shape[2]
    assert ws_in == IN_SLAB + 1 and N_DEV <= LANE, w_in.shape
    ninp = (N_DEV + 1) * IN_SLAB
    ws_sc, ws_fc = w_sc_conv.shape[2], w_ffn_conv.shape[2]

    w_in_shift = lax.dynamic_update_slice(jnp.zeros((d, IN_SLAB + LANE), BF16), w_in[0].astype(BF16), (0, me))
    taps8 = lambda a: jnp.pad(a[0], ((0, 5), (0, 0)))
    g_in, g_out, g_up, g_down, g_sc, g_fc = _all_gather(
        [w_in_shift, w_out[0].astype(BF16), w_up[0].astype(BF16), w_down[0].astype(BF16),
         taps8(w_sc_conv), taps8(w_ffn_conv)], "gather_weights")
    w_in_full = _assemble_w_in(g_in, ninp)
    w_sc_full = g_sc[:, :3].transpose(1, 0, 2).reshape(3, N_DEV * ws_sc)
    w_fc_full = g_fc[:, :3].transpose(1, 0, 2).reshape(3, N_DEV * ws_fc)
    dff = N_DEV * w_down.shape[1]

    in_flight = {}

    def send(name, g):
        g = g if g.ndim == 3 else g.reshape(N_DEV, g.shape[0] // N_DEV, g.shape[1])
        *in_flight[name], token = _exchange_start(g, name == "w_in", "send_" + name)
        return token

    loss_t, grad_x, _, _, _, _, small = _local_step(
        x[0], loss_target[0], w_in_full, b_gates, w_sc_full, mh_gain, g_out.reshape(d, d), ln1_g, ln1_b, g_up,
        w_fc_full, b_ffn_conv, g_down.reshape(dff, d), ln2_g, ln2_b, on_grad=send)

    landed = {}
    for name in ("w_down", "w_up", "w_out", "w_in"):
        sent, got = _exchange_wait(*in_flight[name], grad_x, name == "w_in", "recv_" + name)
        landed[name] = _own_slab(sent, got, me, name == "w_in")
    r_in, r_out, r_up, r_down = landed["w_in"], landed["w_out"], landed["w_up"], landed["w_down"]
    grads, deltas, new_m, new_v = {}, {}, {}, {}
    g_shift = _sum_slots(r_in, "sum_w_in")
    grads["w_in"] = lax.dynamic_slice(g_shift, (0, me), (d, ws_in))
    deltas["w_in"], new_m["w_in"], new_v["w_in"] = _adamw(w_in[0], grads["w_in"], m_w_in[0], v_w_in[0], "adamw_w_in")
    for name, r in (("w_out", r_out), ("w_up", r_up), ("w_down", r_down)):
        grads[name], deltas[name], new_m[name], new_v[name] = _sum_adamw(r, w[name][0], m[name][0], v[name][0],
                                                                         "adamw_" + name)

    names = ("loss",) + SMALL
    parts = dict(small, loss=loss_t[0, :1])
    sizes = [_lanes(parts[n].size) for n in names]
    (g_small,) = _all_gather([_pack([parts[n] for n in names], sizes)], "gather_small")
    summed = _unpack(_sum_slots(g_small, "sum_small", tr=g_small.shape[1]), [parts[n].shape for n in names], sizes)
    full = dict(zip(names, summed))
    full["w_sc_conv"] = lax.dynamic_slice(full["w_sc_conv"], (0, me * ws_sc), (3, ws_sc))
    full["w_ffn_conv"] = lax.dynamic_slice(full["w_ffn_conv"], (0, me * ws_fc), (3, ws_fc))
    for n in SMALL:
        grads[n] = full[n].reshape(w[n].shape)
    sizes = [_lanes(w[n].size) for n in SMALL]
    shapes = [w[n].shape for n in SMALL]
    packed = [_pack([t[n] for n in SMALL], sizes) for t in (w, grads, m, v)]
    for res, t in zip(_adamw(*packed, "adamw_small"), (deltas, new_m, new_v)):
        t.update(zip(SMALL, _unpack(res, shapes, sizes)))

    big = lambda t: {n: (t[n].reshape(w[n].shape) if n in BIG else t[n]) for n in WEIGHTS}
    grads, deltas, new_m, new_v = big(grads), big(deltas), big(new_m), big(new_v)
    return (full["loss"].reshape(()), grad_x[None], *[grads[n] for n in WEIGHTS], *[deltas[n] for n in WEIGHTS],
            *[new_m[n] for n in WEIGHTS], *[new_v[n] for n in WEIGHTS])
```

```python
import functools

import jax
import jax.numpy as jnp
from jax import lax
from jax.experimental import pallas as pl
from jax.experimental.pallas import tpu as pltpu

F32 = jnp.float32
BF16 = jnp.bfloat16
MESH = pl.DeviceIdType.MESH

N_DEV = 8
NH = 4
CHUNK = 64
LN_EPS = 1e-5
HN_EPS = 1e-6
ALPHA = 2.0 ** 0.25
LANE = 128
IN_SLAB = 7 * LANE
VMEM_LIMIT = 56 * 1024 * 1024
ADAM_LR, ADAM_B1, ADAM_B2, ADAM_EPS, ADAM_WD, ADAM_STEP = 0.001, 0.9, 0.999, 1e-08, 0.01, 10

_NN = (((1,), (0,)), ((), ()))
_NT = (((1,), (1,)), ((), ()))
_TN = (((0,), (0,)), ((), ()))


def _dot(a, b, dn=_NN):
    return lax.dot_general(a, b, dn, preferred_element_type=F32)


def _params(*sem):
    return pltpu.CompilerParams(dimension_semantics=sem if sem else None, vmem_limit_bytes=VMEM_LIMIT)


def _iota(shape, axis):
    return lax.broadcasted_iota(jnp.int32, shape, axis)


def _fit(n, want):
    if n <= want:
        return n
    t = want - want % LANE
    while n % t:
        t -= LANE
    return t


def _matmul(a, b, mode, out_dtype, name, tm=1024, tn=512, tk=1024, add=None, add_scale=1.0,
            b_blocked=False, o_width=None, after=None):
    if mode == "tn":
        kd, m = a.shape
    else:
        m, kd = a.shape
    if b_blocked:
        nb, rows, w = b.shape
        n = nb * w if mode == "nn" else rows
        assert (nb * w if mode == "nt" else rows) == kd, (name, b.shape, kd)
    else:
        n = b.shape[0] if mode == "nt" else b.shape[1]
    tm, tn, tk = _fit(m, tm), _fit(n, tn), _fit(kd, tk)
    if b_blocked and mode == "nn":
        tn = _fit(w, tn)
    if b_blocked and mode == "nt":
        tk = _fit(w, tk)
    if o_width is not None:
        tn = _fit(o_width, tn)
    assert m % tm == 0 and n % tn == 0 and kd % tk == 0, (name, m, n, kd, tm, tn, tk)
    nk = kd // tk
    dn = {"nn": _NN, "nt": _NT, "tn": _TN}[mode]
    a_spec = (pl.BlockSpec((tk, tm), lambda i, j, k: (k, i)) if mode == "tn"
              else pl.BlockSpec((tm, tk), lambda i, j, k: (i, k)))
    if b_blocked and mode == "nn":
        per = w // tn
        b_spec = pl.BlockSpec((None, tk, tn), lambda i, j, k: (j // per, k, j % per))
    elif b_blocked:
        per = w // tk
        b_spec = pl.BlockSpec((None, tn, tk), lambda i, j, k: (k // per, j, k % per))
    elif mode == "nt":
        b_spec = pl.BlockSpec((tn, tk), lambda i, j, k: (j, k))
    else:
        b_spec = pl.BlockSpec((tk, tn), lambda i, j, k: (k, j))
    if o_width is None:
        o_spec = pl.BlockSpec((tm, tn), lambda i, j, k: (i, j))
        o_shape = (m, n)
    else:
        oper = o_width // tn
        o_spec = pl.BlockSpec((None, tm, tn), lambda i, j, k: (j // oper, i, j % oper))
        o_shape = (n // o_width, m, o_width)
    has_add = add is not None
    n_in = 2 + has_add + (after is not None)

    def body(*refs):
        a_ref, b_ref = refs[:2]
        add_ref = refs[2] if has_add else None
        o_ref = refs[n_in]

        def finish(r):
            if has_add:
                r = r + add_scale * add_ref[...]
            o_ref[...] = r.astype(out_dtype)

        if nk == 1:
            finish(_dot(a_ref[...], b_ref[...], dn))
        else:
            acc = refs[-1]
            k = pl.program_id(2)

            @pl.when(k == 0)
            def _():
                acc[...] = jnp.zeros_like(acc)

            acc[...] += _dot(a_ref[...], b_ref[...], dn)

            @pl.when(k == nk - 1)
            def _():
                finish(acc[...])

    in_specs = [a_spec, b_spec] + ([pl.BlockSpec((tm, tn), lambda i, j, k: (i, j))] if has_add else [])
    args = (a, b) + ((add,) if has_add else ())
    if after is not None:
        in_specs.append(pl.BlockSpec(memory_space=pl.ANY))
        args += (after,)
    return pl.pallas_call(
        body, name=name, grid=(m // tm, n // tn, nk),
        in_specs=in_specs, out_specs=o_spec,
        out_shape=jax.ShapeDtypeStruct(o_shape, out_dtype),
        scratch_shapes=[pltpu.VMEM((tm, tn), F32)] if nk > 1 else [],
        compiler_params=_params("parallel", "parallel", "arbitrary"),
    )(*args)


def _shift_down(u, s):
    return jnp.where(_iota(u.shape, 0) >= s, pltpu.roll(u, s, 0), 0.0)


def _shift_up(u, s):
    t = u.shape[0]
    return jnp.where(_iota(u.shape, 0) < t - s, pltpu.roll(u, t - s, 0), 0.0)


def _conv(u, w):
    return w[0:1] * _shift_down(u, 2) + w[1:2] * _shift_down(u, 1) + w[2:3] * u


def _conv_t(dy, w):
    return w[2:3] * dy + w[1:2] * _shift_up(dy, 1) + w[0:1] * _shift_up(dy, 2)


def _conv_dw(dy, u):
    d0 = jnp.sum(dy * _shift_down(u, 2), axis=0, keepdims=True)
    d1 = jnp.sum(dy * _shift_down(u, 1), axis=0, keepdims=True)
    d2 = jnp.sum(dy * u, axis=0, keepdims=True)
    r = _iota((3, dy.shape[1]), 0)
    return jnp.where(r == 0, d0, jnp.where(r == 1, d1, d2))


def _sigmoid(x):
    return 1.0 / (1.0 + jnp.exp(-x))


def _sconv_fwd(proj, w_sc, t, wc):
    nb = wc // LANE

    def body(cb_ref, cc_ref, ch_ref, w_ref, y_ref):
        u = cc_ref[...] * ch_ref[...]
        y_ref[...] = (cb_ref[...] * _conv(u, w_ref[...])).astype(BF16)

    col = lambda off: pl.BlockSpec((t, LANE), lambda j: (0, j + off))
    return pl.pallas_call(
        body, name="sconv_fwd", grid=(nb,),
        in_specs=[col(0), col(nb), col(2 * nb), pl.BlockSpec((3, LANE), lambda j: (0, j))],
        out_specs=pl.BlockSpec((t, LANE), lambda j: (0, j)),
        out_shape=jax.ShapeDtypeStruct((t, wc), BF16),
        compiler_params=_params("parallel"),
    )(proj, proj, proj, w_sc)


def _sconv_bwd(dy, proj, w_sc, t, wc):
    nb = wc // LANE

    def body(dy_ref, cb_ref, cc_ref, ch_ref, w_ref, dcb_ref, dcc_ref, dch_ref, dw_ref):
        cc, ch, w, d = cc_ref[...], ch_ref[...], w_ref[...], dy_ref[...]
        u = cc * ch
        dcb_ref[...] = (d * _conv(u, w)).astype(BF16)
        dcu = d * cb_ref[...]
        dw_ref[...] = _conv_dw(dcu, u)
        du = _conv_t(dcu, w)
        dcc_ref[...] = (du * ch).astype(BF16)
        dch_ref[...] = (du * cc).astype(BF16)

    col = lambda off: pl.BlockSpec((t, LANE), lambda j: (0, j + off))
    act = jax.ShapeDtypeStruct((t, wc), BF16)
    return pl.pallas_call(
        body, name="sconv_bwd", grid=(nb,),
        in_specs=[col(0), col(0), col(nb), col(2 * nb), pl.BlockSpec((3, LANE), lambda j: (0, j))],
        out_specs=[col(0), col(0), col(0), pl.BlockSpec((3, LANE), lambda j: (0, j))],
        out_shape=[act, act, act, jax.ShapeDtypeStruct((3, wc), F32)],
        compiler_params=_params("parallel"),
    )(dy, proj, proj, proj, w_sc)


def _gates_prep(proj, bias_tile, t, gate_tile):
    def body(g_ref, b_ref, o_ref):
        g = g_ref[...] + b_ref[...]
        lane = _iota(g.shape, 1)
        is_f = (lane >= NH) & (lane < 2 * NH)
        lf = jnp.minimum(g, 0.0) - jnp.log(1.0 + jnp.exp(-jnp.abs(g)))
        c = jnp.where(is_f, lf, 0.0)
        r = _iota(g.shape, 0) % CHUNK
        s = 1
        while s < CHUNK:
            c = c + jnp.where(r >= s, pltpu.roll(c, s, 0), 0.0)
            s *= 2
        o_ref[...] = jnp.where(is_f, c, jnp.where(lane < NH, g, 0.0))

    return pl.pallas_call(
        body, name="gates_prep", grid=(1,),
        in_specs=[pl.BlockSpec((t, LANE), lambda i: (0, gate_tile)), pl.BlockSpec((1, LANE), lambda i: (0, 0))],
        out_specs=pl.BlockSpec((t, LANE), lambda i: (0, 0)),
        out_shape=jax.ShapeDtypeStruct((t, LANE), F32),
        compiler_params=_params("arbitrary"),
    )(proj, bias_tile)


def _gates_bwd(dgate, proj, bias_tile, t, gate_tile):
    def body(dg_ref, g_ref, b_ref, o_ref, s_ref):
        g = g_ref[...] + b_ref[...]
        lane = _iota(g.shape, 1)
        r = _iota(g.shape, 0) % CHUNK
        dsig = 1.0 - _sigmoid(g)
        out = jnp.zeros(g.shape, F32)
        for h in range(NH):
            d = dg_ref[h]
            c = d
            s = 1
            while s < CHUNK:
                c = c + jnp.where(r + s < CHUNK, pltpu.roll(c, t - s, 0), 0.0)
                s *= 2
            di = jnp.broadcast_to(d[:, 0:1], g.shape)
            db = jnp.broadcast_to(c[:, 1:2], g.shape)
            out = out + jnp.where(lane == h, di, 0.0) + jnp.where(lane == NH + h, db * dsig, 0.0)
        o_ref[...] = out.astype(BF16)
        s_ref[...] = jnp.sum(out, axis=0, keepdims=True)

    return pl.pallas_call(
        body, name="gates_bwd", grid=(1,),
        in_specs=[pl.BlockSpec((NH, t, LANE), lambda i: (0, 0, 0)),
                  pl.BlockSpec((t, LANE), lambda i: (0, gate_tile)), pl.BlockSpec((1, LANE), lambda i: (0, 0))],
        out_specs=[pl.BlockSpec((t, LANE), lambda i: (0, 0)), pl.BlockSpec((1, LANE), lambda i: (0, 0))],
        out_shape=[jax.ShapeDtypeStruct((t, LANE), BF16), jax.ShapeDtypeStruct((1, LANE), F32)],
        compiler_params=_params("arbitrary"),
    )(dgate, proj, bias_tile)


def _chunk_gates(gc, gr, h, mprev):
    L = CHUNK
    lane = _iota(gc.shape, 1)
    sub = _iota(gr.shape, 0)
    icol = jnp.sum(jnp.where(lane == h, gc, 0.0), axis=1, keepdims=True)
    bcol = jnp.sum(jnp.where(lane == h + NH, gc, 0.0), axis=1, keepdims=True)
    irow = jnp.sum(jnp.where(sub == h, gr, 0.0), axis=0, keepdims=True)
    brow = jnp.sum(jnp.where(sub == h + NH, gr, 0.0), axis=0, keepdims=True)
    tri = _iota((L, L), 0) >= _iota((L, L), 1)
    log_d = jnp.where(tri, bcol - brow + irow, -jnp.inf)
    inter = bcol + mprev
    mt = jnp.maximum(inter, jnp.max(log_d, axis=1, keepdims=True))
    dw = jnp.exp(log_d - mt)
    iw = jnp.exp(inter - mt)
    g = jnp.sum(jnp.where(_iota((L, 1), 0) == L - 1, bcol, 0.0), axis=0, keepdims=True)
    wlog_col = g - bcol + icol
    wlog_row = g - brow + irow
    mnew = jnp.maximum(g + mprev, jnp.max(wlog_row, axis=1, keepdims=True))
    wcol = jnp.exp(wlog_col - mnew)
    decay = jnp.exp(g + mprev - mnew)
    return dw, iw, mt, wcol, decay, mnew


def _mlstm_fwd(proj, gcol, grow, t, wc, dh):
    nc = t // CHUNK
    qoff = 3 * wc // dh
    scale = dh ** -0.5
    wm = NH * dh

    def body(q_ref, k_ref, v_ref, gc_ref, gr_ref, h_ref, cs_ref, ns_ref, c_s, n_s, m_s):
        h = pl.program_id(0)
        c = pl.program_id(1)

        @pl.when(c == 0)
        def _():
            c_s[...] = jnp.zeros_like(c_s)
            n_s[...] = jnp.zeros_like(n_s)
            m_s[...] = jnp.zeros_like(m_s)

        mprev = m_s[0:1, 0:1]
        cprev = c_s[...]
        n8 = n_s[...]
        nprev = n8[0:1]
        cs_ref[...] = cprev
        ns_ref[...] = jnp.where(_iota(n8.shape, 0) == 1, mprev, n8)

        dw, iw, mt, wcol, decay, mnew = _chunk_gates(gc_ref[...], gr_ref[0], h, mprev)
        qs = q_ref[...] * scale
        k = k_ref[...]
        qs_b, k_b, v_b = qs.astype(BF16), k.astype(BF16), v_ref[...].astype(BF16)
        s = _dot(qs_b, k_b, _NT) * dw
        num = _dot(s.astype(BF16), v_b) + iw * _dot(qs_b, cprev.astype(BF16))
        den = jnp.sum(s, axis=1, keepdims=True) + iw * jnp.sum(qs * nprev, axis=1, keepdims=True)
        h_ref[...] = num / jnp.maximum(jnp.abs(den), jnp.exp(-mt))

        wk = wcol * k
        c_s[...] = decay * cprev + _dot(wk.astype(BF16), v_b, _TN)
        n_s[...] = decay * n8 + jnp.sum(wk, axis=0, keepdims=True)
        m_s[...] = jnp.broadcast_to(mnew, m_s.shape)

    hd = lambda off: pl.BlockSpec((CHUNK, dh), lambda h, c: (c, qoff + off * NH + h))
    return pl.pallas_call(
        body, name="mlstm_fwd", grid=(NH, nc),
        in_specs=[hd(0), hd(1), hd(2),
                  pl.BlockSpec((CHUNK, LANE), lambda h, c: (c, 0)),
                  pl.BlockSpec((1, 8, CHUNK), lambda h, c: (c, 0, 0))],
        out_specs=[pl.BlockSpec((CHUNK, dh), lambda h, c: (c, h)),
                   pl.BlockSpec((None, None, dh, dh), lambda h, c: (h, c, 0, 0)),
                   pl.BlockSpec((None, None, 8, dh), lambda h, c: (h, c, 0, 0))],
        out_shape=[jax.ShapeDtypeStruct((t, wm), F32),
                   jax.ShapeDtypeStruct((NH, nc, dh, dh), F32),
                   jax.ShapeDtypeStruct((NH, nc, 8, dh), F32)],
        scratch_shapes=[pltpu.VMEM((dh, dh), F32), pltpu.VMEM((8, dh), F32), pltpu.VMEM((8, LANE), F32)],
        compiler_params=_params("parallel", "arbitrary"),
    )(proj, proj, proj, gcol, grow)


def _mlstm_bwd(proj, gcol, grow, hval, dh_in, cs, ns, t, wc, dh):
    nc = t // CHUNK
    qoff = 3 * wc // dh
    scale = dh ** -0.5
    wm = NH * dh
    L = CHUNK

    def body(q_ref, k_ref, v_ref, gc_ref, gr_ref, h_ref, dh_ref, cs_ref, ns_ref,
             dq_ref, dk_ref, dv_ref, dg_ref, dc_s, dn_s):
        h = pl.program_id(0)
        step = pl.program_id(1)

        @pl.when(step == 0)
        def _():
            dc_s[...] = jnp.zeros_like(dc_s)
            dn_s[...] = jnp.zeros_like(dn_s)

        ns8 = ns_ref[...]
        nprev = ns8[0:1]
        mprev = ns8[1:2, 0:1]
        cprev = cs_ref[...]
        dcn = dc_s[...]
        dn8 = dn_s[...]
        dnn = dn8[0:1]

        dw, iw, mt, wcol, decay, _ = _chunk_gates(gc_ref[...], gr_ref[0], h, mprev)
        qs = q_ref[...] * scale
        k = k_ref[...]
        qs_b, k_b, v_b = qs.astype(BF16), k.astype(BF16), v_ref[...].astype(BF16)
        qk = _dot(qs_b, k_b, _NT)
        s = qk * dw
        den = jnp.sum(s, axis=1, keepdims=True) + iw * jnp.sum(qs * nprev, axis=1, keepdims=True)
        emt = jnp.exp(-mt)
        r = 1.0 / jnp.maximum(jnp.abs(den), emt)
        dout = dh_ref[...]
        dnum = dout * r
        dden = (-jnp.sum(dout * h_ref[...], axis=1, keepdims=True) * r
                * jnp.where(jnp.abs(den) > emt, jnp.sign(den), 0.0))
        dnum_b = dnum.astype(BF16)
        cprev_b = cprev.astype(BF16)
        dcn_b = dcn.astype(BF16)

        gd = (_dot(dnum_b, v_b, _NT) + dden) * dw
        gd_b = gd.astype(BF16)
        dqs_inter = iw * (_dot(dnum_b, cprev_b, _NT) + dden * nprev)
        dqs = _dot(gd_b, k_b) + dqs_inter
        dk_inter = wcol * (_dot(v_b, dcn_b, _NT) + dnn)
        dk = _dot(gd_b, qs_b, _TN) + dk_inter
        wk = wcol * k
        dv = _dot(s.astype(BF16), dnum_b, _TN) + _dot(wk.astype(BF16), dcn_b)

        e = gd * qk
        eye = _iota((L, L), 0) == _iota((L, L), 1)
        e_cols = jnp.sum(jnp.where(eye, jnp.sum(e, axis=0, keepdims=True), 0.0), axis=1, keepdims=True)
        k_inter = jnp.sum(k * dk_inter, axis=1, keepdims=True)
        rq = jnp.sum(e, axis=1, keepdims=True) + jnp.sum(qs * dqs_inter, axis=1, keepdims=True)
        rk = e_cols + k_inter
        hsum = jnp.sum(k_inter, axis=0, keepdims=True)
        jdec = decay * (jnp.sum(jnp.sum(dcn * cprev, axis=1, keepdims=True), axis=0, keepdims=True)
                        + jnp.sum(dnn * nprev, axis=1, keepdims=True))
        db = rq - rk + jnp.where(_iota((L, 1), 0) == L - 1, hsum + jdec, 0.0)
        lane = _iota((L, LANE), 1)
        dg_ref[...] = jnp.where(lane == 0, rk, jnp.where(lane == 1, db, 0.0))

        dq_ref[...] = (dqs * scale).astype(BF16)
        dk_ref[...] = dk.astype(BF16)
        dv_ref[...] = dv.astype(BF16)

        iq = iw * qs
        dc_s[...] = decay * dcn + _dot(iq.astype(BF16), dnum_b, _TN)
        dn_s[...] = decay * dn8 + jnp.sum(iq * dden, axis=0, keepdims=True)

    rc = lambda c: nc - 1 - c
    hd = lambda off: pl.BlockSpec((L, dh), lambda h, c: (rc(c), qoff + off * NH + h))
    hm = pl.BlockSpec((L, dh), lambda h, c: (rc(c), h))
    act = jax.ShapeDtypeStruct((t, wm), BF16)
    return pl.pallas_call(
        body, name="mlstm_bwd", grid=(NH, nc),
        in_specs=[hd(0), hd(1), hd(2),
                  pl.BlockSpec((L, LANE), lambda h, c: (rc(c), 0)),
                  pl.BlockSpec((1, 8, L), lambda h, c: (rc(c), 0, 0)),
                  hm, hm,
                  pl.BlockSpec((None, None, dh, dh), lambda h, c: (h, rc(c), 0, 0)),
                  pl.BlockSpec((None, None, 8, dh), lambda h, c: (h, rc(c), 0, 0))],
        out_specs=[hm, hm, hm, pl.BlockSpec((None, L, LANE), lambda h, c: (h, rc(c), 0))],
        out_shape=[act, act, act, jax.ShapeDtypeStruct((NH, t, LANE), F32)],
        scratch_shapes=[pltpu.VMEM((dh, dh), F32), pltpu.VMEM((8, dh), F32)],
        compiler_params=_params("parallel", "arbitrary"),
    )(proj, proj, proj, gcol, grow, hval, dh_in, cs, ns)


def _head_norm(hv):
    mu = jnp.mean(hv, axis=1, keepdims=True)
    hc = hv - mu
    rstd = lax.rsqrt(jnp.mean(hc * hc, axis=1, keepdims=True) + HN_EPS)
    return hc * rstd, rstd


def _hnorm_fwd(hval, proj, gain, t, wc, dh, tr=256):
    ooff = 3 * wc // dh + 3 * NH

    def body(h_ref, o_ref, g_ref, y_ref):
        hhat, _ = _head_norm(h_ref[...])
        y_ref[...] = (_sigmoid(o_ref[...]) * hhat * g_ref[...]).astype(BF16)

    return pl.pallas_call(
        body, name="hnorm_fwd", grid=(t // tr, NH),
        in_specs=[pl.BlockSpec((tr, dh), lambda i, h: (i, h)),
                  pl.BlockSpec((tr, dh), lambda i, h: (i, ooff + h)),
                  pl.BlockSpec((1, dh), lambda i, h: (0, h))],
        out_specs=pl.BlockSpec((tr, dh), lambda i, h: (i, h)),
        out_shape=jax.ShapeDtypeStruct((t, NH * dh), BF16),
        compiler_params=_params("parallel", "parallel"),
    )(hval, proj, gain)


def _hnorm_bwd(dy, hval, proj, gain, t, wc, dh, tr=256):
    ooff = 3 * wc // dh + 3 * NH
    yoff = wc // dh

    def body(dy_ref, h_ref, o_ref, g_ref, do_ref, dh_ref, dg_ref):
        i = pl.program_id(1)
        hhat, rstd = _head_norm(h_ref[...])
        gain_v = g_ref[...]
        sig = _sigmoid(o_ref[...])
        d = dy_ref[...]
        do_ref[...] = (d * hhat * gain_v * sig * (1.0 - sig)).astype(BF16)
        dhn = d * sig
        part = jnp.sum(dhn * hhat, axis=0, keepdims=True)

        @pl.when(i == 0)
        def _():
            dg_ref[...] = part

        @pl.when(i > 0)
        def _():
            dg_ref[...] += part

        dhat = dhn * gain_v
        dh_ref[...] = rstd * (dhat - jnp.mean(dhat, axis=1, keepdims=True)
                              - hhat * jnp.mean(dhat * hhat, axis=1, keepdims=True))

    blk = lambda off: pl.BlockSpec((tr, dh), lambda h, i: (i, off + h))
    return pl.pallas_call(
        body, name="hnorm_bwd", grid=(NH, t // tr),
        in_specs=[blk(yoff), blk(0), blk(ooff), pl.BlockSpec((1, dh), lambda h, i: (0, h))],
        out_specs=[blk(0), blk(0), pl.BlockSpec((1, dh), lambda h, i: (0, h))],
        out_shape=[jax.ShapeDtypeStruct((t, NH * dh), BF16), jax.ShapeDtypeStruct((t, NH * dh), F32),
                   jax.ShapeDtypeStruct((1, NH * dh), F32)],
        compiler_params=_params("parallel", "arbitrary"),
    )(dy, hval, proj, gain)


def _ln_stats(z):
    mu = jnp.mean(z, axis=1, keepdims=True)
    zc = z - mu
    rstd = lax.rsqrt(jnp.mean(zc * zc, axis=1, keepdims=True) + LN_EPS)
    return zc * rstd, rstd


def _ln_bwd(dy, xhat, rstd, g):
    dxh = dy * g
    return rstd * (dxh - jnp.mean(dxh, axis=1, keepdims=True) - xhat * jnp.mean(dxh * xhat, axis=1, keepdims=True))


def _accum(ref, i, part):
    @pl.when(i == 0)
    def _():
        ref[...] = part

    @pl.when(i > 0)
    def _():
        ref[...] += part


def _ln1_fwd(x, mix, g, b, tr=256):
    t, d = x.shape

    def body(x_ref, m_ref, g_ref, b_ref, xh_ref, rs_ref, xb_ref):
        xhat, rstd = _ln_stats(ALPHA * x_ref[...] + m_ref[...])
        xh_ref[...] = xhat
        rs_ref[...] = rstd
        xb_ref[...] = (xhat * g_ref[...] + b_ref[...]).astype(BF16)

    row = pl.BlockSpec((tr, d), lambda i: (i, 0))
    vec = pl.BlockSpec((1, d), lambda i: (0, 0))
    return pl.pallas_call(
        body, name="ln1_fwd", grid=(t // tr,),
        in_specs=[row, row, vec, vec],
        out_specs=[row, pl.BlockSpec((tr, 1), lambda i: (i, 0)), row],
        out_shape=[jax.ShapeDtypeStruct((t, d), F32), jax.ShapeDtypeStruct((t, 1), F32),
                   jax.ShapeDtypeStruct((t, d), BF16)],
        compiler_params=_params("parallel"),
    )(x, mix, g, b)


def _ln2_loss(xhat1, g1, b1, ff, target, g2, b2, tr=256):
    t, d = ff.shape

    def body(xh_ref, g1_ref, b1_ref, f_ref, t_ref, g_ref, b_ref, dz_ref, dzb_ref, dg_ref, db_ref, l_ref):
        i = pl.program_id(0)
        x1 = xh_ref[...] * g1_ref[...] + b1_ref[...]
        xhat, rstd = _ln_stats(ALPHA * x1 + f_ref[...])
        gv = g_ref[...]
        e = xhat * gv + b_ref[...] - t_ref[...]
        lsum = jnp.sum(jnp.sum(e * e, axis=1, keepdims=True), axis=0, keepdims=True) * (0.5 / d)
        dy = e * (1.0 / d)
        _accum(dg_ref, i, jnp.sum(dy * xhat, axis=0, keepdims=True))
        _accum(db_ref, i, jnp.sum(dy, axis=0, keepdims=True))
        _accum(l_ref, i, jnp.broadcast_to(lsum, l_ref.shape))
        dz = _ln_bwd(dy, xhat, rstd, gv)
        dz_ref[...] = dz
        dzb_ref[...] = dz.astype(BF16)

    row = pl.BlockSpec((tr, d), lambda i: (i, 0))
    vec = pl.BlockSpec((1, d), lambda i: (0, 0))
    return pl.pallas_call(
        body, name="ln2_loss", grid=(t // tr,),
        in_specs=[row, vec, vec, row, row, vec, vec],
        out_specs=[row, row, vec, vec, pl.BlockSpec((8, LANE), lambda i: (0, 0))],
        out_shape=[jax.ShapeDtypeStruct((t, d), F32), jax.ShapeDtypeStruct((t, d), BF16),
                   jax.ShapeDtypeStruct((1, d), F32), jax.ShapeDtypeStruct((1, d), F32),
                   jax.ShapeDtypeStruct((8, LANE), F32)],
        compiler_params=_params("arbitrary"),
    )(xhat1, g1, b1, ff, target, g2, b2)


def _ln1_bwd(dz2, dffn, xhat1, rstd1, g1, tr=256):
    t, d = dz2.shape

    def body(a_ref, f_ref, xh_ref, rs_ref, g_ref, dz_ref, dzb_ref, dg_ref, db_ref):
        i = pl.program_id(0)
        dy = ALPHA * a_ref[...] + f_ref[...]
        xhat = xh_ref[...]
        _accum(dg_ref, i, jnp.sum(dy * xhat, axis=0, keepdims=True))
        _accum(db_ref, i, jnp.sum(dy, axis=0, keepdims=True))
        dz = _ln_bwd(dy, xhat, rs_ref[...], g_ref[...])
        dz_ref[...] = dz
        dzb_ref[...] = dz.astype(BF16)

    row = pl.BlockSpec((tr, d), lambda i: (i, 0))
    vec = pl.BlockSpec((1, d), lambda i: (0, 0))
    return pl.pallas_call(
        body, name="ln1_bwd", grid=(t // tr,),
        in_specs=[row, row, row, pl.BlockSpec((tr, 1), lambda i: (i, 0)), vec],
        out_specs=[row, row, vec, vec],
        out_shape=[jax.ShapeDtypeStruct((t, d), F32), jax.ShapeDtypeStruct((t, d), BF16),
                   jax.ShapeDtypeStruct((1, d), F32), jax.ShapeDtypeStruct((1, d), F32)],
        compiler_params=_params("arbitrary"),
    )(dz2, dffn, xhat1, rstd1, g1)


def _ffn_act_fwd(hid0, w_fc, b_fc, t, dff):
    nb = dff // LANE

    def body(hv_ref, hg_ref, wv_ref, wg_ref, bv_ref, bg_ref, a_ref):
        val = _conv(hv_ref[...], wv_ref[...]) + bv_ref[...]
        gate = _conv(hg_ref[...], wg_ref[...]) + bg_ref[...]
        a_ref[...] = (gate * _sigmoid(gate) * val).astype(BF16)

    col = lambda off: pl.BlockSpec((t, LANE), lambda j: (0, j + off))
    w3 = lambda off: pl.BlockSpec((3, LANE), lambda j: (0, j + off))
    w1 = lambda off: pl.BlockSpec((1, LANE), lambda j: (0, j + off))
    return pl.pallas_call(
        body, name="ffn_act_fwd", grid=(nb,),
        in_specs=[col(0), col(nb), w3(0), w3(nb), w1(0), w1(nb)],
        out_specs=col(0),
        out_shape=jax.ShapeDtypeStruct((t, dff), BF16),
        compiler_params=_params("parallel"),
    )(hid0, hid0, w_fc, w_fc, b_fc, b_fc)


def _ffn_act_bwd(da, hid0, w_fc, b_fc, t, dff):
    nb = dff // LANE

    def body(da_ref, hv_ref, hg_ref, wv_ref, wg_ref, bv_ref, bg_ref,
             dhv_ref, dhg_ref, dwv_ref, dwg_ref, dbv_ref, dbg_ref):
        hv, hg, wv, wg = hv_ref[...], hg_ref[...], wv_ref[...], wg_ref[...]
        val = _conv(hv, wv) + bv_ref[...]
        gate = _conv(hg, wg) + bg_ref[...]
        sig = _sigmoid(gate)
        d = da_ref[...]
        dval = d * gate * sig
        dgate = d * val * sig * (1.0 + gate * (1.0 - sig))
        dhv_ref[...] = _conv_t(dval, wv).astype(BF16)
        dhg_ref[...] = _conv_t(dgate, wg).astype(BF16)
        dwv_ref[...] = _conv_dw(dval, hv)
        dwg_ref[...] = _conv_dw(dgate, hg)
        dbv_ref[...] = jnp.sum(dval, axis=0, keepdims=True)
        dbg_ref[...] = jnp.sum(dgate, axis=0, keepdims=True)

    col = lambda off: pl.BlockSpec((t, LANE), lambda j: (0, j + off))
    w3 = lambda off: pl.BlockSpec((3, LANE), lambda j: (0, j + off))
    w1 = lambda off: pl.BlockSpec((1, LANE), lambda j: (0, j + off))
    act = jax.ShapeDtypeStruct((t, dff), BF16)
    s3 = jax.ShapeDtypeStruct((3, dff), F32)
    s1 = jax.ShapeDtypeStruct((1, dff), F32)
    return pl.pallas_call(
        body, name="ffn_act_bwd", grid=(nb,),
        in_specs=[col(0), col(0), col(nb), w3(0), w3(nb), w1(0), w1(nb)],
        out_specs=[col(0), col(0), w3(0), w3(0), w1(0), w1(0)],
        out_shape=[act, act, s3, s3, s1, s1],
        compiler_params=_params("parallel"),
    )(da, hid0, hid0, w_fc, w_fc, b_fc, b_fc)


class _Ready:
    def __init__(self, **weights):
        self.weights = weights

    def begin(self, after):
        return None

    def forward(self, name, after):
        return None

    def get(self, name, after):
        return self.weights[name]


def _local_step(x, target, w_in, b_gates, w_sc, gain, w_out, ln1_g, ln1_b, w_up, w_fc, b_fc, w_down, ln2_g, ln2_b,
                on_grad=None, wx=None):
    t, d = x.shape
    wc = d // 2
    dh = (d - wc) // NH
    wm = NH * dh
    dff = w_fc.shape[1] // 2
    if wx is None:
        wx = _Ready(w_out=w_out, w_up=w_up, w_down=w_down)
    ninp = w_in.shape[1]
    nin = 3 * wc + 4 * wm
    gate_tile = nin // LANE
    nc = t // CHUNK
    bias_tile = jnp.pad(b_gates, ((0, 0), (0, LANE - 2 * NH)))

    x_b = x.astype(BF16)
    proj = _matmul(x_b, w_in, "nn", F32, "proj", tn=1152, after=wx.begin(w_in))
    y_conv = _sconv_fwd(proj, w_sc, t, wc)
    gcol = _gates_prep(proj, bias_tile, t, gate_tile)
    grow = gcol[:, :8].T.reshape(8, nc, CHUNK).transpose(1, 0, 2)
    hval, cs, ns = _mlstm_fwd(proj, gcol, grow, t, wc, dh)
    y_m = _hnorm_fwd(hval, proj, gain, t, wc, dh)
    y = jnp.concatenate([y_conv, y_m], axis=1)
    tok = wx.forward("w_up", wx.forward("w_out", y))
    w_out = wx.get("w_out", tok)
    mix = _matmul(y, w_out, "nn", F32, "out_proj", after=tok)
    xhat1, rstd1, x1_b = _ln1_fwd(x, mix, ln1_g, ln1_b)
    tok = wx.forward("w_down", x1_b)
    w_up = wx.get("w_up", tok)
    wsl = w_up.shape[2]
    hid0 = _matmul(x1_b, w_up, "nn", F32, "ffn_up", tn=wsl, b_blocked=True, after=tok)
    act = _ffn_act_fwd(hid0, w_fc, b_fc, t, dff)
    w_down = wx.get("w_down", act)
    ff = _matmul(act, w_down, "nn", F32, "ffn_down")
    dz2, dz2_b, d_ln2_g, d_ln2_b, loss = _ln2_loss(xhat1, ln1_g, ln1_b, ff, target, ln2_g, ln2_b)

    emit = on_grad if on_grad is not None else (lambda name, g: None)
    d_w_down = _matmul(act, dz2_b, "tn", BF16, "ffn_down_dw", tm=512, tn=1024, tk=t)
    d_act = _matmul(dz2_b, w_down, "nt", F32, "ffn_down_dx", after=emit("w_down", d_w_down))
    dhv, dhg, dwv, dwg, dbv, dbg = _ffn_act_bwd(d_act, hid0, w_fc, b_fc, t, dff)
    d_hid0 = jnp.concatenate([dhv, dhg], axis=1)
    d_w_fc = jnp.concatenate([dwv, dwg], axis=1)
    d_b_fc = jnp.concatenate([dbv, dbg], axis=1)
    d_w_up = _matmul(x1_b, d_hid0, "tn", BF16, "ffn_up_dw", tm=512, tn=wsl, tk=t, o_width=wsl)
    d_x1_ffn = _matmul(d_hid0, w_up, "nt", F32, "ffn_up_dx", tk=wsl, b_blocked=True, after=emit("w_up", d_w_up))
    dz1, dz1_b, d_ln1_g, d_ln1_b = _ln1_bwd(dz2, d_x1_ffn, xhat1, rstd1, ln1_g)

    d_w_out = _matmul(y, dz1_b, "tn", BF16, "out_proj_dw", tm=512, tn=1024, tk=t)
    dy = _matmul(dz1_b, w_out, "nt", F32, "out_proj_dx", after=emit("w_out", d_w_out))
    dcb, dcc, dch, d_w_sc = _sconv_bwd(dy, proj, w_sc, t, wc)
    d_o, d_hval, d_gain = _hnorm_bwd(dy, hval, proj, gain, t, wc, dh)
    dq, dk, dv, dgate = _mlstm_bwd(proj, gcol, grow, hval, d_hval, cs, ns, t, wc, dh)
    dgt, d_b_gates = _gates_bwd(dgate, proj, bias_tile, t, gate_tile)
    pad = jnp.zeros((t, ninp - nin - LANE), BF16)
    d_proj = jnp.concatenate([dcb, dcc, dch, dq, dk, dv, d_o, dgt, pad], axis=1)
    d_w_in = _matmul(x_b, d_proj, "tn", BF16, "proj_dw", tm=512, tn=IN_SLAB, tk=t, o_width=IN_SLAB)
    grad_x = _matmul(d_proj, w_in, "nt", F32, "proj_dx", tk=1152, add=dz1, add_scale=ALPHA,
                     after=emit("w_in", d_w_in))

    small = dict(b_gates=d_b_gates[:, :2 * NH], w_sc_conv=d_w_sc, mh_gain=d_gain, ln1_g=d_ln1_g, ln1_b=d_ln1_b,
                 w_ffn_conv=d_w_fc, b_ffn_conv=d_b_fc, ln2_g=d_ln2_g, ln2_b=d_ln2_b)
    return loss, grad_x, d_w_in, d_w_out, d_w_up, d_w_down, small


HBM = pl.BlockSpec(memory_space=pltpu.HBM)


def _place():
    return lax.axis_index("x"), lax.axis_index("y"), lax.axis_index("c")


def _index(p):
    return 4 * p[0] + 2 * p[1] + p[2]


def _all_gather(arrs, name):
    n = len(arrs)

    def body(*refs):
        ins, outs = refs[:n], refs[n:2 * n]
        send_sems, recv_sems, local_sems = refs[2 * n:]
        x, y, c = _place()
        me, sibling = (x, y, c), (x, y, 1 - c)
        chips = [(1 - x, y), (x, 1 - y), (1 - x, 1 - y)]

        def copy(a, k, block, to, own=False):
            dst = outs[a].at[_index(block)]
            return pltpu.make_async_remote_copy(
                src_ref=ins[a] if own else dst, dst_ref=dst,
                send_sem=send_sems.at[k * n + a], recv_sem=recv_sems.at[k * n + a],
                device_id=to, device_id_type=MESH)

        mine = [pltpu.make_async_copy(ins[a], outs[a].at[_index(me)], local_sems.at[a]) for a in range(n)]
        for cp in mine:
            cp.start()
        first = []
        for a in range(n):
            first.append(copy(a, 0, me, sibling, own=True))
            first += [copy(a, 1 + j, me, (*chip, c), own=True) for j, chip in enumerate(chips)]
        for cp in first:
            cp.start()
        passed = []
        for j, chip in enumerate(chips):
            for a in range(n):
                copy(a, 1 + j, (*chip, c), me).wait_recv()
                cp = copy(a, 4 + j, (*chip, c), sibling)
                cp.start()
                passed.append(cp)
        for a in range(n):
            copy(a, 0, sibling, me).wait_recv()
            for j, chip in enumerate(chips):
                copy(a, 4 + j, (*chip, 1 - c), me).wait_recv()
        for cp in first + passed:
            cp.wait_send()
        for cp in mine:
            cp.wait()

    return pl.pallas_call(
        body, name=name, in_specs=[HBM] * n, out_specs=[HBM] * n,
        out_shape=[jax.ShapeDtypeStruct((N_DEV,) + a.shape, a.dtype) for a in arrs],
        scratch_shapes=[pltpu.SemaphoreType.DMA((7 * n,)), pltpu.SemaphoreType.DMA((7 * n,)),
                        pltpu.SemaphoreType.DMA((n,))],
    )(*arrs)


SEM = pl.BlockSpec(memory_space=pltpu.SEMAPHORE)
EFFECT = pltpu.SideEffectType.DATAFLOW_SIDE_EFFECTING


def _peers(x, y, c):
    return [(1 - x if j & 4 else x, 1 - y if j & 2 else y, 1 - c if j & 1 else c) for j in range(1, N_DEV)]


def _exchange_pieces(g_ref, land_ref, width, tail):
    if not tail:
        return [(lambda i: g_ref.at[i], lambda s: land_ref.at[s])]
    return [(lambda i: g_ref.at[i], lambda s: land_ref.at[s, :, pl.ds(0, width)]),
            (lambda i: g_ref.at[i + 1, :, pl.ds(0, LANE)], lambda s: land_ref.at[s, :, pl.ds(width, LANE)])]


def _exchange_start(grad, tail, name):
    width = grad.shape[2]
    n_p = 2 if tail else 1
    land_shape = (N_DEV, grad.shape[1], width + (LANE if tail else 0))

    def body(g_ref, land_ref, send_sems, recv_sems, g_thru, land_thru, token):
        x, y, c = _place()
        me = _index((x, y, c))
        for j, peer in enumerate(_peers(x, y, c)):
            for p, (src, dst) in enumerate(_exchange_pieces(g_ref, land_ref, width, tail)):
                pltpu.make_async_remote_copy(src_ref=src(_index(peer)), dst_ref=dst(me), send_sem=send_sems.at[j * n_p + p],
                                             recv_sem=recv_sems.at[j * n_p + p], device_id=peer,
                                             device_id_type=MESH).start()
        token[...] = jnp.zeros_like(token)

    return pl.pallas_call(
        body, name=name,
        out_shape=(pltpu.SemaphoreType.DMA((7 * n_p,)), pltpu.SemaphoreType.DMA((7 * n_p,)),
                   pltpu.HBM(grad.shape, grad.dtype), pltpu.HBM(land_shape, grad.dtype),
                   jax.ShapeDtypeStruct((8, LANE), F32)),
        in_specs=(HBM, HBM), out_specs=(SEM, SEM, HBM, HBM, pl.BlockSpec(memory_space=pltpu.VMEM)),
        input_output_aliases={0: 2, 1: 3},
        compiler_params=pltpu.CompilerParams(has_side_effects=EFFECT),
    )(pltpu.with_memory_space_constraint(grad, pltpu.HBM),
      pltpu.with_memory_space_constraint(lax.empty(land_shape, grad.dtype), pltpu.HBM))


def _exchange_wait(send_sems, recv_sems, g_thru, land_thru, after, tail, name):
    width = g_thru.shape[2]
    n_p = 2 if tail else 1

    def body(g_ref, land_ref, send_sems, recv_sems, after_ref, g_dead, got_ref):
        x, y, c = _place()
        for j, peer in enumerate(_peers(x, y, c)):
            for p, (src, dst) in enumerate(_exchange_pieces(g_ref, land_ref, width, tail)):
                cp = pltpu.make_async_remote_copy(src_ref=src(_index(peer)), dst_ref=dst(_index(peer)),
                                                  send_sem=send_sems.at[j * n_p + p], recv_sem=recv_sems.at[j * n_p + p],
                                                  device_id=peer, device_id_type=MESH)
                cp.wait_send()
                cp.wait_recv()

    return pl.pallas_call(
        body, name=name,
        out_shape=(pltpu.HBM(g_thru.shape, g_thru.dtype), pltpu.HBM(land_thru.shape, land_thru.dtype)),
        in_specs=(HBM, HBM, SEM, SEM, pl.BlockSpec(memory_space=pl.ANY)), out_specs=(HBM, HBM),
        input_output_aliases={0: 0, 1: 1},
        compiler_params=pltpu.CompilerParams(has_side_effects=EFFECT),
    )(g_thru, land_thru, send_sems, recv_sems, after)


def _chips(x, y):
    return [(1 - x, y), (x, 1 - y), (1 - x, 1 - y)]


def _gather_start(block, after, name):
    land_shape = (N_DEV,) + block.shape

    def body(b_ref, land_ref, after_ref, send_sems, recv_sems, b_thru, land_thru, token):
        x, y, c = _place()
        me = _index((x, y, c))
        for k, to in enumerate([(x, y, 1 - c)] + [(*chip, c) for chip in _chips(x, y)]):
            pltpu.make_async_remote_copy(src_ref=b_ref, dst_ref=land_ref.at[me], send_sem=send_sems.at[k],
                                         recv_sem=recv_sems.at[k], device_id=to, device_id_type=MESH).start()
        token[...] = jnp.zeros_like(token)

    return pl.pallas_call(
        body, name=name,
        out_shape=(pltpu.SemaphoreType.DMA((4,)), pltpu.SemaphoreType.DMA((4,)),
                   pltpu.HBM(block.shape, block.dtype), pltpu.HBM(land_shape, block.dtype),
                   jax.ShapeDtypeStruct((8, LANE), F32)),
        in_specs=(HBM, HBM, pl.BlockSpec(memory_space=pl.ANY)),
        out_specs=(SEM, SEM, HBM, HBM, pl.BlockSpec(memory_space=pltpu.VMEM)),
        input_output_aliases={0: 2, 1: 3},
        compiler_params=pltpu.CompilerParams(has_side_effects=EFFECT),
    )(pltpu.with_memory_space_constraint(block, pltpu.HBM),
      pltpu.with_memory_space_constraint(lax.empty(land_shape, block.dtype), pltpu.HBM), after)


def _gather_forward(send_sems, recv_sems, b_thru, land_thru, after, name):
    def body(b_ref, land_ref, send_sems, recv_sems, after_ref, b_dead, land_out, send2, recv2, token):
        x, y, c = _place()
        sibling = (x, y, 1 - c)
        for k, frm in enumerate([sibling] + [(*chip, c) for chip in _chips(x, y)]):
            cp = pltpu.make_async_remote_copy(src_ref=b_ref, dst_ref=land_ref.at[_index(frm)], send_sem=send_sems.at[k],
                                              recv_sem=recv_sems.at[k], device_id=frm, device_id_type=MESH)
            cp.wait_send()
            cp.wait_recv()
        for j, chip in enumerate(_chips(x, y)):
            slot = land_ref.at[_index((*chip, c))]
            pltpu.make_async_remote_copy(src_ref=slot, dst_ref=slot, send_sem=send2.at[j], recv_sem=recv2.at[j],
                                         device_id=sibling, device_id_type=MESH).start()
        token[...] = jnp.zeros_like(token)

    return pl.pallas_call(
        body, name=name,
        out_shape=(pltpu.HBM(b_thru.shape, b_thru.dtype), pltpu.HBM(land_thru.shape, land_thru.dtype),
                   pltpu.SemaphoreType.DMA((3,)), pltpu.SemaphoreType.DMA((3,)), jax.ShapeDtypeStruct((8, LANE), F32)),
        in_specs=(HBM, HBM, SEM, SEM, pl.BlockSpec(memory_space=pl.ANY)),
        out_specs=(HBM, HBM, SEM, SEM, pl.BlockSpec(memory_space=pltpu.VMEM)),
        input_output_aliases={0: 0, 1: 1},
        compiler_params=pltpu.CompilerParams(has_side_effects=EFFECT),
    )(b_thru, land_thru, send_sems, recv_sems, after)


def _gather_finish(land_thru, send2, recv2, after, name):
    def body(land_ref, send2, recv2, after_ref, land_out):
        x, y, c = _place()
        for j, chip in enumerate(_chips(x, y)):
            cp = pltpu.make_async_remote_copy(src_ref=land_ref.at[_index((*chip, c))],
                                              dst_ref=land_ref.at[_index((*chip, 1 - c))], send_sem=send2.at[j],
                                              recv_sem=recv2.at[j], device_id=(x, y, 1 - c), device_id_type=MESH)
            cp.wait_send()
            cp.wait_recv()

    return pl.pallas_call(
        body, name=name, out_shape=pltpu.HBM(land_thru.shape, land_thru.dtype),
        in_specs=(HBM, SEM, SEM, pl.BlockSpec(memory_space=pl.ANY)), out_specs=HBM,
        input_output_aliases={0: 0},
        compiler_params=pltpu.CompilerParams(has_side_effects=EFFECT),
    )(land_thru, send2, recv2, after)


class _Gathering:
    ORDER = ("w_out", "w_up", "w_down")

    def __init__(self, blocks, me):
        self.blocks, self.me, self.state = blocks, me, {}

    def begin(self, after):
        token = after
        for name in self.ORDER:
            *self.state[name], token = _gather_start(self.blocks[name], token, "gather1_" + name)
        return token

    def forward(self, name, after):
        *self.state[name], token = _gather_forward(*self.state[name], after, "gather2_" + name)
        return token

    def get(self, name, after):
        block, land, send2, recv2 = self.state[name]
        land = _gather_finish(land, send2, recv2, after, "gather3_" + name)
        land = lax.dynamic_update_index_in_dim(land, block[None], self.me, 0)
        return land if name == "w_up" else land.reshape(-1, land.shape[2])


def _own_slab(grad, land, me, tail):
    own = lax.dynamic_index_in_dim(grad, me, 0, keepdims=True)
    if tail:
        nxt = lax.dynamic_index_in_dim(grad, me + 1, 0, keepdims=True)[:, :, :LANE]
        own = jnp.concatenate([own, nxt], axis=2)
    return lax.dynamic_update_index_in_dim(land, own, me, 0)


def _assemble_w_in(g, ninp):
    _, d, pw = g.shape
    per = IN_SLAB // LANE
    last = N_DEV * per

    def body(a_ref, b_ref, o_ref):
        t = pl.program_id(0)
        main = jnp.where(t <= last, a_ref[...], jnp.zeros_like(a_ref))
        carry = jnp.where((t % per == 0) & (t > 0) & (t < last), b_ref[...], jnp.zeros_like(b_ref))
        o_ref[...] = main + carry

    def main_map(t):
        k = jnp.minimum(t // per, N_DEV - 1)
        return k, 0, jnp.minimum(t - per * k, per)

    def carry_map(t):
        return jnp.maximum(jnp.minimum(t // per, N_DEV - 1), 1) - 1, 0, per

    return pl.pallas_call(
        body, name="assemble_w_in", grid=(ninp // LANE,),
        in_specs=[pl.BlockSpec((None, d, LANE), main_map), pl.BlockSpec((None, d, LANE), carry_map)],
        out_specs=pl.BlockSpec((d, LANE), lambda t: (0, t)),
        out_shape=jax.ShapeDtypeStruct((d, ninp), g.dtype),
        compiler_params=_params("parallel"),
    )(g, g)


def _rows(n, want):
    t = min(n, want)
    t -= t % 16
    while n % t:
        t -= 16
    return t


def _adam_math(w, g, m, v):
    m2 = ADAM_B1 * m + (1.0 - ADAM_B1) * g
    v2 = ADAM_B2 * v + (1.0 - ADAM_B2) * (g * g)
    m_hat = m2 / (1.0 - ADAM_B1 ** ADAM_STEP)
    v_hat = v2 / (1.0 - ADAM_B2 ** ADAM_STEP)
    return -ADAM_LR * (m_hat / (jnp.sqrt(v_hat) + ADAM_EPS) + ADAM_WD * w), m2, v2


def _slot_sum(r_ref):
    acc = r_ref[0].astype(F32)
    for i in range(1, N_DEV):
        acc = acc + r_ref[i].astype(F32)
    return acc


def _sum_slots(r, name, tr=128):
    _, rows, cols = r.shape
    tr = _rows(rows, tr)

    def body(r_ref, g_ref):
        g_ref[...] = _slot_sum(r_ref)

    return pl.pallas_call(
        body, name=name, grid=(rows // tr,),
        in_specs=[pl.BlockSpec((N_DEV, tr, cols), lambda i: (0, i, 0))],
        out_specs=pl.BlockSpec((tr, cols), lambda i: (i, 0)),
        out_shape=jax.ShapeDtypeStruct((rows, cols), F32),
        compiler_params=_params("parallel"),
    )(r)


def _adamw(w, g, m, v, name, tr=256):
    rows, cols = w.shape
    tr = _rows(rows, tr)

    def body(w_ref, g_ref, m_ref, v_ref, d_ref, m2_ref, v2_ref):
        d_ref[...], m2_ref[...], v2_ref[...] = _adam_math(w_ref[...], g_ref[...], m_ref[...], v_ref[...])

    blk = pl.BlockSpec((tr, cols), lambda i: (i, 0))
    out = jax.ShapeDtypeStruct((rows, cols), F32)
    return pl.pallas_call(
        body, name=name, grid=(rows // tr,), in_specs=[blk] * 4, out_specs=[blk] * 3, out_shape=[out] * 3,
        compiler_params=_params("parallel"),
    )(w, g, m, v)


def _sum_adamw(r, w, m, v, name, tr=128):
    rows, cols = w.shape
    tr = _rows(rows, tr)

    def body(r_ref, w_ref, m_ref, v_ref, g_ref, d_ref, m2_ref, v2_ref):
        g = _slot_sum(r_ref)
        g_ref[...] = g
        d_ref[...], m2_ref[...], v2_ref[...] = _adam_math(w_ref[...], g, m_ref[...], v_ref[...])

    blk = pl.BlockSpec((tr, cols), lambda i: (i, 0))
    out = jax.ShapeDtypeStruct((rows, cols), F32)
    return pl.pallas_call(
        body, name=name, grid=(rows // tr,),
        in_specs=[pl.BlockSpec((N_DEV, tr, cols), lambda i: (0, i, 0)), blk, blk, blk],
        out_specs=[blk] * 4, out_shape=[out] * 4,
        compiler_params=_params("parallel"),
    )(r, w, m, v)


def _pack(pieces, sizes):
    flat = [jnp.pad(p.reshape(-1).astype(F32), (0, s - p.size)) for p, s in zip(pieces, sizes)]
    total = sum(sizes)
    padded = -(-total // (16 * LANE)) * (16 * LANE)
    return jnp.pad(jnp.concatenate(flat), (0, padded - total)).reshape(-1, LANE)


def _unpack(packed, shapes, sizes):
    flat = packed.reshape(-1)
    out, off = [], 0
    for shp, s in zip(shapes, sizes):
        n = 1
        for k in shp:
            n *= k
        out.append(flat[off:off + n].reshape(shp))
        off += s
    return out


def _lanes(n):
    return -(-n // LANE) * LANE


WEIGHTS = ("w_in", "b_gates", "w_sc_conv", "mh_gain", "w_out", "ln1_g", "ln1_b", "w_up", "w_ffn_conv", "b_ffn_conv",
           "w_down", "ln2_g", "ln2_b")
BIG = ("w_in", "w_out", "w_up", "w_down")
SMALL = tuple(n for n in WEIGHTS if n not in BIG)


def kernel(x, w_in, b_gates, w_sc_conv, mh_gain, w_out, ln1_g, ln1_b, w_up, w_ffn_conv, b_ffn_conv, w_down, ln2_g, ln2_b, loss_target, m_w_in, m_b_gates, m_w_sc_conv, m_mh_gain, m_w_out, m_ln1_g, m_ln1_b, m_w_up, m_w_ffn_conv, m_b_ffn_conv, m_w_down, m_ln2_g, m_ln2_b, v_w_in, v_b_gates, v_w_sc_conv, v_mh_gain, v_w_out, v_ln1_g, v_ln1_b, v_w_up, v_w_ffn_conv, v_b_ffn_conv, v_w_down, v_ln2_g, v_ln2_b):
    w = dict(zip(WEIGHTS, (w_in, b_gates, w_sc_conv, mh_gain, w_out, ln1_g, ln1_b, w_up, w_ffn_conv, b_ffn_conv,
                           w_down, ln2_g, ln2_b)))
    m = dict(zip(WEIGHTS, (m_w_in, m_b_gates, m_w_sc_conv, m_mh_gain, m_w_out, m_ln1_g, m_ln1_b, m_w_up,
                           m_w_ffn_conv, m_b_ffn_conv, m_w_down, m_ln2_g, m_ln2_b)))
    v = dict(zip(WEIGHTS, (v_w_in, v_b_gates, v_w_sc_conv, v_mh_gain, v_w_out, v_ln1_g, v_ln1_b, v_w_up,
                           v_w_ffn_conv, v_b_ffn_conv, v_w_down, v_ln2_g, v_ln2_b)))
    me = _index(_place())
    d = x.shape[2]
    ws_in = w_in.shape[2]
    assert ws_in == IN_SLAB + 1 and N_DEV <= LANE, w_in.shape
    ninp = (N_DEV + 1) * IN_SLAB
    ws_sc, ws_fc = w_sc_conv.shape[2], w_ffn_conv.shape[2]

    w_in_shift = lax.dynamic_update_slice(jnp.zeros((d, IN_SLAB + LANE), BF16), w_in[0].astype(BF16), (0, me))
    taps8 = lambda a: jnp.pad(a[0], ((0, 5), (0, 0)))
    g_in, g_sc, g_fc = _all_gather([w_in_shift, taps8(w_sc_conv), taps8(w_ffn_conv)], "gather_w_in")
    w_in_full = _assemble_w_in(g_in, ninp)
    w_sc_full = g_sc[:, :3].transpose(1, 0, 2).reshape(3, N_DEV * ws_sc)
    w_fc_full = g_fc[:, :3].transpose(1, 0, 2).reshape(3, N_DEV * ws_fc)
    wx = _Gathering({n: w[n][0].astype(BF16) for n in ("w_out", "w_up", "w_down")}, me)

    in_flight = {}

    def send(name, g):
        g = g if g.ndim == 3 else g.reshape(N_DEV, g.shape[0] // N_DEV, g.shape[1])
        *in_flight[name], token = _exchange_start(g, name == "w_in", "send_" + name)
        return token

    loss_t, grad_x, _, _, _, _, small = _local_step(
        x[0], loss_target[0], w_in_full, b_gates, w_sc_full, mh_gain, None, ln1_g, ln1_b, None,
        w_fc_full, b_ffn_conv, None, ln2_g, ln2_b, on_grad=send, wx=wx)

    landed = {}
    for name in ("w_down", "w_up", "w_out", "w_in"):
        sent, got = _exchange_wait(*in_flight[name], grad_x, name == "w_in", "recv_" + name)
        landed[name] = _own_slab(sent, got, me, name == "w_in")
    r_in, r_out, r_up, r_down = landed["w_in"], landed["w_out"], landed["w_up"], landed["w_down"]
    grads, deltas, new_m, new_v = {}, {}, {}, {}
    g_shift = _sum_slots(r_in, "sum_w_in")
    grads["w_in"] = lax.dynamic_slice(g_shift, (0, me), (d, ws_in))
    deltas["w_in"], new_m["w_in"], new_v["w_in"] = _adamw(w_in[0], grads["w_in"], m_w_in[0], v_w_in[0], "adamw_w_in")
    for name, r in (("w_out", r_out), ("w_up", r_up), ("w_down", r_down)):
        grads[name], deltas[name], new_m[name], new_v[name] = _sum_adamw(r, w[name][0], m[name][0], v[name][0],
                                                                         "adamw_" + name)

    names = ("loss",) + SMALL
    parts = dict(small, loss=loss_t[0, :1])
    sizes = [_lanes(parts[n].size) for n in names]
    (g_small,) = _all_gather([_pack([parts[n] for n in names], sizes)], "gather_small")
    summed = _unpack(_sum_slots(g_small, "sum_small", tr=g_small.shape[1]), [parts[n].shape for n in names], sizes)
    full = dict(zip(names, summed))
    full["w_sc_conv"] = lax.dynamic_slice(full["w_sc_conv"], (0, me * ws_sc), (3, ws_sc))
    full["w_ffn_conv"] = lax.dynamic_slice(full["w_ffn_conv"], (0, me * ws_fc), (3, ws_fc))
    for n in SMALL:
        grads[n] = full[n].reshape(w[n].shape)
    sizes = [_lanes(w[n].size) for n in SMALL]
    shapes = [w[n].shape for n in SMALL]
    packed = [_pack([t[n] for n in SMALL], sizes) for t in (w, grads, m, v)]
    for res, t in zip(_adamw(*packed, "adamw_small"), (deltas, new_m, new_v)):
        t.update(zip(SMALL, _unpack(res, shapes, sizes)))

    big = lambda t: {n: (t[n].reshape(w[n].shape) if n in BIG else t[n]) for n in WEIGHTS}
    grads, deltas, new_m, new_v = big(grads), big(deltas), big(new_m), big(new_v)
    return (full["loss"].reshape(()), grad_x[None], *[grads[n] for n in WEIGHTS], *[deltas[n] for n in WEIGHTS],
            *[new_m[n] for n in WEIGHTS], *[new_v[n] for n in WEIGHTS])
```

```python
import functools

import jax
import jax.numpy as jnp
from jax import lax
from jax.experimental import pallas as pl
from jax.experimental.pallas import tpu as pltpu

F32 = jnp.float32
BF16 = jnp.bfloat16
MESH = pl.DeviceIdType.MESH

N_DEV = 8
NH = 4
CHUNK = 64
LN_EPS = 1e-5
HN_EPS = 1e-6
ALPHA = 2.0 ** 0.25
LANE = 128
IN_SLAB = 7 * LANE
VMEM_LIMIT = 56 * 1024 * 1024
ADAM_LR, ADAM_B1, ADAM_B2, ADAM_EPS, ADAM_WD, ADAM_STEP = 0.001, 0.9, 0.999, 1e-08, 0.01, 10

_NN = (((1,), (0,)), ((), ()))
_NT = (((1,), (1,)), ((), ()))
_TN = (((0,), (0,)), ((), ()))


def _dot(a, b, dn=_NN):
    return lax.dot_general(a, b, dn, preferred_element_type=F32)


def _params(*sem):
    return pltpu.CompilerParams(dimension_semantics=sem if sem else None, vmem_limit_bytes=VMEM_LIMIT)


def _iota(shape, axis):
    return lax.broadcasted_iota(jnp.int32, shape, axis)


def _fit(n, want):
    if n <= want:
        return n
    t = want - want % LANE
    while n % t:
        t -= LANE
    return t


def _matmul(a, b, mode, out_dtype, name, tm=1024, tn=512, tk=1024, add=None, add_scale=1.0,
            b_blocked=False, o_width=None, after=None):
    if mode == "tn":
        kd, m = a.shape
    else:
        m, kd = a.shape
    if b_blocked:
        nb, rows, w = b.shape
        n = nb * w if mode == "nn" else rows
        assert (nb * w if mode == "nt" else rows) == kd, (name, b.shape, kd)
    else:
        n = b.shape[0] if mode == "nt" else b.shape[1]
    tm, tn, tk = _fit(m, tm), _fit(n, tn), _fit(kd, tk)
    if b_blocked and mode == "nn":
        tn = _fit(w, tn)
    if b_blocked and mode == "nt":
        tk = _fit(w, tk)
    if o_width is not None:
        tn = _fit(o_width, tn)
    assert m % tm == 0 and n % tn == 0 and kd % tk == 0, (name, m, n, kd, tm, tn, tk)
    nk = kd // tk
    dn = {"nn": _NN, "nt": _NT, "tn": _TN}[mode]
    a_spec = (pl.BlockSpec((tk, tm), lambda i, j, k: (k, i)) if mode == "tn"
              else pl.BlockSpec((tm, tk), lambda i, j, k: (i, k)))
    if b_blocked and mode == "nn":
        per = w // tn
        b_spec = pl.BlockSpec((None, tk, tn), lambda i, j, k: (j // per, k, j % per))
    elif b_blocked:
        per = w // tk
        b_spec = pl.BlockSpec((None, tn, tk), lambda i, j, k: (k // per, j, k % per))
    elif mode == "nt":
        b_spec = pl.BlockSpec((tn, tk), lambda i, j, k: (j, k))
    else:
        b_spec = pl.BlockSpec((tk, tn), lambda i, j, k: (k, j))
    if o_width is None:
        o_spec = pl.BlockSpec((tm, tn), lambda i, j, k: (i, j))
        o_shape = (m, n)
    else:
        oper = o_width // tn
        o_spec = pl.BlockSpec((None, tm, tn), lambda i, j, k: (j // oper, i, j % oper))
        o_shape = (n // o_width, m, o_width)
    has_add = add is not None
    n_in = 2 + has_add + (after is not None)
    in_place = nk > 1 and out_dtype == F32

    def body(*refs):
        a_ref, b_ref = refs[:2]
        add_ref = refs[2] if has_add else None
        o_ref = refs[n_in]

        def finish(r):
            if has_add:
                r = r + add_scale * add_ref[...]
            o_ref[...] = r.astype(out_dtype)

        if nk == 1:
            finish(_dot(a_ref[...], b_ref[...], dn))
        else:
            acc = o_ref if in_place else refs[-1]
            k = pl.program_id(2)

            @pl.when(k == 0)
            def _():
                acc[...] = _dot(a_ref[...], b_ref[...], dn)

            @pl.when(k > 0)
            def _():
                acc[...] += _dot(a_ref[...], b_ref[...], dn)

            if not (in_place and not has_add):
                @pl.when(k == nk - 1)
                def _():
                    finish(acc[...])

    in_specs = [a_spec, b_spec] + ([pl.BlockSpec((tm, tn), lambda i, j, k: (i, j))] if has_add else [])
    args = (a, b) + ((add,) if has_add else ())
    if after is not None:
        in_specs.append(pl.BlockSpec(memory_space=pl.ANY))
        args += (after,)
    return pl.pallas_call(
        body, name=name, grid=(m // tm, n // tn, nk),
        in_specs=in_specs, out_specs=o_spec,
        out_shape=jax.ShapeDtypeStruct(o_shape, out_dtype),
        scratch_shapes=[pltpu.VMEM((tm, tn), F32)] if nk > 1 and not in_place else [],
        compiler_params=_params("parallel", "parallel", "arbitrary"),
    )(*args)


def _shift_down(u, s):
    return jnp.where(_iota(u.shape, 0) >= s, pltpu.roll(u, s, 0), 0.0)


def _shift_up(u, s):
    t = u.shape[0]
    return jnp.where(_iota(u.shape, 0) < t - s, pltpu.roll(u, t - s, 0), 0.0)


SLAB = 8


def _rolled(u):
    return pltpu.roll(u, 2, 0), pltpu.roll(u, 1, 0)


def _conv(u, w, rolled=None):
    u2, u1 = _rolled(u) if rolled is None else rolled
    raw = w[0:1] * u2 + w[1:2] * u1 + w[2:3] * u
    head = u[0:SLAB]
    mended = w[0:1] * _shift_down(head, 2) + w[1:2] * _shift_down(head, 1) + w[2:3] * head
    return jnp.concatenate([mended, raw[SLAB:]], axis=0)


def _conv_t(dy, w):
    t = dy.shape[0]
    raw = w[2:3] * dy + w[1:2] * pltpu.roll(dy, t - 1, 0) + w[0:1] * pltpu.roll(dy, t - 2, 0)
    tail = dy[t - SLAB:]
    mended = w[2:3] * tail + w[1:2] * _shift_up(tail, 1) + w[0:1] * _shift_up(tail, 2)
    return jnp.concatenate([raw[:t - SLAB], mended], axis=0)


def _conv_dw(dy, u, rolled=None):
    t = dy.shape[0]
    u2, u1 = _rolled(u) if rolled is None else rolled
    head, tail = dy[0:SLAB], u[t - SLAB:]
    r = _iota(head.shape, 0)
    wrap2 = jnp.sum(jnp.where(r < 2, head * pltpu.roll(tail, 2, 0), 0.0), axis=0, keepdims=True)
    wrap1 = jnp.sum(jnp.where(r < 1, head * pltpu.roll(tail, 1, 0), 0.0), axis=0, keepdims=True)
    d0 = jnp.sum(dy * u2, axis=0, keepdims=True) - wrap2
    d1 = jnp.sum(dy * u1, axis=0, keepdims=True) - wrap1
    d2 = jnp.sum(dy * u, axis=0, keepdims=True)
    r3 = _iota((3, dy.shape[1]), 0)
    return jnp.where(r3 == 0, d0, jnp.where(r3 == 1, d1, d2))


def _sigmoid(x):
    return 0.5 * jnp.tanh(0.5 * x) + 0.5


def _sconv_fwd(proj, w_sc, t, wc):
    nb = wc // LANE

    def body(cb_ref, cc_ref, ch_ref, w_ref, y_ref):
        u = cc_ref[...] * ch_ref[...]
        y_ref[...] = (cb_ref[...] * _conv(u, w_ref[...])).astype(BF16)

    col = lambda off: pl.BlockSpec((t, LANE), lambda j: (0, j + off))
    return pl.pallas_call(
        body, name="sconv_fwd", grid=(nb,),
        in_specs=[col(0), col(nb), col(2 * nb), pl.BlockSpec((3, LANE), lambda j: (0, j))],
        out_specs=pl.BlockSpec((t, LANE), lambda j: (0, j)),
        out_shape=jax.ShapeDtypeStruct((t, wc), BF16),
        compiler_params=_params("parallel"),
    )(proj, proj, proj, w_sc)


def _sconv_bwd(dy, proj, w_sc, t, wc):
    nb = wc // LANE

    def body(dy_ref, cb_ref, cc_ref, ch_ref, w_ref, dcb_ref, dcc_ref, dch_ref, dw_ref):
        cc, ch, w, d = cc_ref[...], ch_ref[...], w_ref[...], dy_ref[...]
        u = cc * ch
        ru = _rolled(u)
        dcb_ref[...] = (d * _conv(u, w, ru)).astype(BF16)
        dcu = d * cb_ref[...]
        dw_ref[...] = _conv_dw(dcu, u, ru)
        du = _conv_t(dcu, w)
        dcc_ref[...] = (du * ch).astype(BF16)
        dch_ref[...] = (du * cc).astype(BF16)

    col = lambda off: pl.BlockSpec((t, LANE), lambda j: (0, j + off))
    act = jax.ShapeDtypeStruct((t, wc), BF16)
    return pl.pallas_call(
        body, name="sconv_bwd", grid=(nb,),
        in_specs=[col(0), col(0), col(nb), col(2 * nb), pl.BlockSpec((3, LANE), lambda j: (0, j))],
        out_specs=[col(0), col(0), col(0), pl.BlockSpec((3, LANE), lambda j: (0, j))],
        out_shape=[act, act, act, jax.ShapeDtypeStruct((3, wc), F32)],
        compiler_params=_params("parallel"),
    )(dy, proj, proj, proj, w_sc)


def _gates_prep(proj, bias_tile, t, gate_tile):
    def body(g_ref, b_ref, o_ref):
        g = g_ref[...] + b_ref[...]
        lane = _iota(g.shape, 1)
        is_f = (lane >= NH) & (lane < 2 * NH)
        lf = jnp.minimum(g, 0.0) - jnp.log(1.0 + jnp.exp(-jnp.abs(g)))
        c = jnp.where(is_f, lf, 0.0)
        r = _iota(g.shape, 0) % CHUNK
        s = 1
        while s < CHUNK:
            c = c + jnp.where(r >= s, pltpu.roll(c, s, 0), 0.0)
            s *= 2
        o_ref[...] = jnp.where(is_f, c, jnp.where(lane < NH, g, 0.0))

    return pl.pallas_call(
        body, name="gates_prep", grid=(1,),
        in_specs=[pl.BlockSpec((t, LANE), lambda i: (0, gate_tile)), pl.BlockSpec((1, LANE), lambda i: (0, 0))],
        out_specs=pl.BlockSpec((t, LANE), lambda i: (0, 0)),
        out_shape=jax.ShapeDtypeStruct((t, LANE), F32),
        compiler_params=_params("arbitrary"),
    )(proj, bias_tile)


def _gates_bwd(dgate, proj, bias_tile, t, gate_tile):
    def body(dg_ref, g_ref, b_ref, o_ref, s_ref):
        g = g_ref[...] + b_ref[...]
        lane = _iota(g.shape, 1)
        r = _iota(g.shape, 0) % CHUNK
        dsig = 1.0 - _sigmoid(g)
        out = jnp.zeros(g.shape, F32)
        for h in range(NH):
            d = dg_ref[h]
            c = d
            s = 1
            while s < CHUNK:
                c = c + jnp.where(r + s < CHUNK, pltpu.roll(c, t - s, 0), 0.0)
                s *= 2
            di = jnp.broadcast_to(d[:, 0:1], g.shape)
            db = jnp.broadcast_to(c[:, 1:2], g.shape)
            out = out + jnp.where(lane == h, di, 0.0) + jnp.where(lane == NH + h, db * dsig, 0.0)
        o_ref[...] = out.astype(BF16)
        s_ref[...] = jnp.sum(out, axis=0, keepdims=True)

    return pl.pallas_call(
        body, name="gates_bwd", grid=(1,),
        in_specs=[pl.BlockSpec((NH, t, LANE), lambda i: (0, 0, 0)),
                  pl.BlockSpec((t, LANE), lambda i: (0, gate_tile)), pl.BlockSpec((1, LANE), lambda i: (0, 0))],
        out_specs=[pl.BlockSpec((t, LANE), lambda i: (0, 0)), pl.BlockSpec((1, LANE), lambda i: (0, 0))],
        out_shape=[jax.ShapeDtypeStruct((t, LANE), BF16), jax.ShapeDtypeStruct((1, LANE), F32)],
        compiler_params=_params("arbitrary"),
    )(dgate, proj, bias_tile)


def _chunk_gates(gc, gr, h, mprev):
    L = CHUNK
    icol, bcol = gc[:, h:h + 1], gc[:, h + NH:h + NH + 1]
    irow, brow = gr[h:h + 1, :], gr[h + NH:h + NH + 1, :]
    tri = _iota((L, L), 0) >= _iota((L, L), 1)
    log_d = jnp.where(tri, bcol - brow + irow, -jnp.inf)
    inter = bcol + mprev
    mt = jnp.maximum(inter, jnp.max(log_d, axis=1, keepdims=True))
    dw = jnp.exp(log_d - mt)
    iw = jnp.exp(inter - mt)
    g = brow[:, L - 1:L]
    wlog_col = g - bcol + icol
    wlog_row = g - brow + irow
    mnew = jnp.maximum(g + mprev, jnp.max(wlog_row, axis=1, keepdims=True))
    wcol = jnp.exp(wlog_col - mnew)
    decay = jnp.exp(g + mprev - mnew)
    return dw, iw, mt, wcol, decay, mnew


def _mlstm_fwd(proj, gcol, grow, t, wc, dh):
    nc = t // CHUNK
    wm = NH * dh
    assert wc == wm, (wc, wm)
    qoff = 3 * wc // wm
    scale = dh ** -0.5

    def body(q_ref, k_ref, v_ref, gc_ref, gr_ref, h_ref, cs_ref, ns_ref, c_s, n_s, m_s):
        @pl.when(pl.program_id(0) == 0)
        def _():
            c_s[...] = jnp.zeros_like(c_s)
            n_s[...] = jnp.zeros_like(n_s)
            m_s[...] = jnp.zeros_like(m_s)

        gc, gr = gc_ref[...], gr_ref[0]
        for h in range(NH):
            cols = slice(h * dh, (h + 1) * dh)
            mprev = m_s[h, 0:1, 0:1]
            cprev = c_s[h]
            n8 = n_s[h]
            nprev = n8[0:1]
            cs_ref[h] = cprev
            ns_ref[h] = jnp.where(_iota(n8.shape, 0) == 1, mprev, n8)

            dw, iw, mt, wcol, decay, mnew = _chunk_gates(gc, gr, h, mprev)
            qs = q_ref[:, cols] * scale
            k = k_ref[:, cols]
            qs_b, k_b, v_b = qs.astype(BF16), k.astype(BF16), v_ref[:, cols].astype(BF16)
            s = _dot(qs_b, k_b, _NT) * dw
            num = _dot(s.astype(BF16), v_b) + iw * _dot(qs_b, cprev.astype(BF16))
            den = jnp.sum(s, axis=1, keepdims=True) + iw * jnp.sum(qs * nprev, axis=1, keepdims=True)
            h_ref[:, cols] = num / jnp.maximum(jnp.abs(den), jnp.exp(-mt))

            wk = wcol * k
            c_s[h] = decay * cprev + _dot(wk.astype(BF16), v_b, _TN)
            n_s[h] = decay * n8 + jnp.sum(wk, axis=0, keepdims=True)
            m_s[h] = jnp.broadcast_to(mnew, m_s.shape[1:])

    grp = lambda off: pl.BlockSpec((CHUNK, wm), lambda c: (c, qoff + off))
    return pl.pallas_call(
        body, name="mlstm_fwd", grid=(nc,),
        in_specs=[grp(0), grp(1), grp(2),
                  pl.BlockSpec((CHUNK, LANE), lambda c: (c, 0)),
                  pl.BlockSpec((1, 8, CHUNK), lambda c: (c, 0, 0))],
        out_specs=[pl.BlockSpec((CHUNK, wm), lambda c: (c, 0)),
                   pl.BlockSpec((NH, None, dh, dh), lambda c: (0, c, 0, 0)),
                   pl.BlockSpec((NH, None, 8, dh), lambda c: (0, c, 0, 0))],
        out_shape=[jax.ShapeDtypeStruct((t, wm), F32),
                   jax.ShapeDtypeStruct((NH, nc, dh, dh), F32),
                   jax.ShapeDtypeStruct((NH, nc, 8, dh), F32)],
        scratch_shapes=[pltpu.VMEM((NH, dh, dh), F32), pltpu.VMEM((NH, 8, dh), F32), pltpu.VMEM((NH, 8, LANE), F32)],
        compiler_params=_params("arbitrary"),
    )(proj, proj, proj, gcol, grow)


def _mlstm_bwd(proj, gcol, grow, hval, dh_in, cs, ns, t, wc, dh):
    nc = t // CHUNK
    wm = NH * dh
    assert wc == wm, (wc, wm)
    qoff = 3 * wc // wm
    scale = dh ** -0.5
    L = CHUNK

    def body(q_ref, k_ref, v_ref, gc_ref, gr_ref, h_ref, dh_ref, cs_ref, ns_ref,
             dq_ref, dk_ref, dv_ref, dg_ref, dc_s, dn_s):
        @pl.when(pl.program_id(0) == 0)
        def _():
            dc_s[...] = jnp.zeros_like(dc_s)
            dn_s[...] = jnp.zeros_like(dn_s)

        gc, gr = gc_ref[...], gr_ref[0]
        eye = _iota((L, L), 0) == _iota((L, L), 1)
        lane = _iota((L, LANE), 1)
        last = _iota((L, 1), 0) == L - 1
        for h in range(NH):
            cols = slice(h * dh, (h + 1) * dh)
            ns8 = ns_ref[h]
            nprev = ns8[0:1]
            mprev = ns8[1:2, 0:1]
            cprev = cs_ref[h]
            dcn = dc_s[h]
            dn8 = dn_s[h]
            dnn = dn8[0:1]

            dw, iw, mt, wcol, decay, _ = _chunk_gates(gc, gr, h, mprev)
            qs = q_ref[:, cols] * scale
            k = k_ref[:, cols]
            qs_b, k_b, v_b = qs.astype(BF16), k.astype(BF16), v_ref[:, cols].astype(BF16)
            qk = _dot(qs_b, k_b, _NT)
            s = qk * dw
            den = jnp.sum(s, axis=1, keepdims=True) + iw * jnp.sum(qs * nprev, axis=1, keepdims=True)
            emt = jnp.exp(-mt)
            r = 1.0 / jnp.maximum(jnp.abs(den), emt)
            dout = dh_ref[:, cols]
            dnum = dout * r
            dden = (-jnp.sum(dout * h_ref[:, cols], axis=1, keepdims=True) * r
                    * jnp.where(jnp.abs(den) > emt, jnp.sign(den), 0.0))
            dnum_b = dnum.astype(BF16)
            cprev_b = cprev.astype(BF16)
            dcn_b = dcn.astype(BF16)

            gd = (_dot(dnum_b, v_b, _NT) + dden) * dw
            gd_b = gd.astype(BF16)
            dqs_inter = iw * (_dot(dnum_b, cprev_b, _NT) + dden * nprev)
            dqs = _dot(gd_b, k_b) + dqs_inter
            dk_inter = wcol * (_dot(v_b, dcn_b, _NT) + dnn)
            dk = _dot(gd_b, qs_b, _TN) + dk_inter
            wk = wcol * k
            dv = _dot(s.astype(BF16), dnum_b, _TN) + _dot(wk.astype(BF16), dcn_b)

            e = gd * qk
            e_cols = jnp.sum(jnp.where(eye, jnp.sum(e, axis=0, keepdims=True), 0.0), axis=1, keepdims=True)
            k_inter = jnp.sum(k * dk_inter, axis=1, keepdims=True)
            rq = jnp.sum(e, axis=1, keepdims=True) + jnp.sum(qs * dqs_inter, axis=1, keepdims=True)
            rk = e_cols + k_inter
            hsum = jnp.sum(k_inter, axis=0, keepdims=True)
            jdec = decay * (jnp.sum(jnp.sum(dcn * cprev, axis=1, keepdims=True), axis=0, keepdims=True)
                            + jnp.sum(dnn * nprev, axis=1, keepdims=True))
            db = rq - rk + jnp.where(last, hsum + jdec, 0.0)
            dg_ref[h] = jnp.where(lane == 0, rk, jnp.where(lane == 1, db, 0.0))

            dq_ref[:, cols] = (dqs * scale).astype(BF16)
            dk_ref[:, cols] = dk.astype(BF16)
            dv_ref[:, cols] = dv.astype(BF16)

            iq = iw * qs
            dc_s[h] = decay * dcn + _dot(iq.astype(BF16), dnum_b, _TN)
            dn_s[h] = decay * dn8 + jnp.sum(iq * dden, axis=0, keepdims=True)

    rc = lambda c: nc - 1 - c
    grp = lambda off: pl.BlockSpec((L, wm), lambda c: (rc(c), qoff + off))
    hm = pl.BlockSpec((L, wm), lambda c: (rc(c), 0))
    act = jax.ShapeDtypeStruct((t, wm), BF16)
    return pl.pallas_call(
        body, name="mlstm_bwd", grid=(nc,),
        in_specs=[grp(0), grp(1), grp(2),
                  pl.BlockSpec((L, LANE), lambda c: (rc(c), 0)),
                  pl.BlockSpec((1, 8, L), lambda c: (rc(c), 0, 0)),
                  hm, hm,
                  pl.BlockSpec((NH, None, dh, dh), lambda c: (0, rc(c), 0, 0)),
                  pl.BlockSpec((NH, None, 8, dh), lambda c: (0, rc(c), 0, 0))],
        out_specs=[hm, hm, hm, pl.BlockSpec((NH, L, LANE), lambda c: (0, rc(c), 0))],
        out_shape=[act, act, act, jax.ShapeDtypeStruct((NH, t, LANE), F32)],
        scratch_shapes=[pltpu.VMEM((NH, dh, dh), F32), pltpu.VMEM((NH, 8, dh), F32)],
        compiler_params=_params("arbitrary"),
    )(proj, proj, proj, gcol, grow, hval, dh_in, cs, ns)


def _head_norm(hv):
    mu = jnp.mean(hv, axis=1, keepdims=True)
    hc = hv - mu
    rstd = lax.rsqrt(jnp.mean(hc * hc, axis=1, keepdims=True) + HN_EPS)
    return hc * rstd, rstd


def _hnorm_fwd(hval, proj, gain, t, wc, dh, tr=256):
    ooff = 3 * wc // dh + 3 * NH

    def body(h_ref, o_ref, g_ref, y_ref):
        hhat, _ = _head_norm(h_ref[...])
        y_ref[...] = (_sigmoid(o_ref[...]) * hhat * g_ref[...]).astype(BF16)

    return pl.pallas_call(
        body, name="hnorm_fwd", grid=(t // tr, NH),
        in_specs=[pl.BlockSpec((tr, dh), lambda i, h: (i, h)),
                  pl.BlockSpec((tr, dh), lambda i, h: (i, ooff + h)),
                  pl.BlockSpec((1, dh), lambda i, h: (0, h))],
        out_specs=pl.BlockSpec((tr, dh), lambda i, h: (i, h)),
        out_shape=jax.ShapeDtypeStruct((t, NH * dh), BF16),
        compiler_params=_params("parallel", "parallel"),
    )(hval, proj, gain)


def _hnorm_bwd(dy, hval, proj, gain, t, wc, dh, tr=256):
    ooff = 3 * wc // dh + 3 * NH
    yoff = wc // dh

    def body(dy_ref, h_ref, o_ref, g_ref, do_ref, dh_ref, dg_ref):
        i = pl.program_id(1)
        hhat, rstd = _head_norm(h_ref[...])
        gain_v = g_ref[...]
        sig = _sigmoid(o_ref[...])
        d = dy_ref[...]
        do_ref[...] = (d * hhat * gain_v * sig * (1.0 - sig)).astype(BF16)
        dhn = d * sig
        part = jnp.sum(dhn * hhat, axis=0, keepdims=True)

        @pl.when(i == 0)
        def _():
            dg_ref[...] = part

        @pl.when(i > 0)
        def _():
            dg_ref[...] += part

        dhat = dhn * gain_v
        dh_ref[...] = rstd * (dhat - jnp.mean(dhat, axis=1, keepdims=True)
                              - hhat * jnp.mean(dhat * hhat, axis=1, keepdims=True))

    blk = lambda off: pl.BlockSpec((tr, dh), lambda h, i: (i, off + h))
    return pl.pallas_call(
        body, name="hnorm_bwd", grid=(NH, t // tr),
        in_specs=[blk(yoff), blk(0), blk(ooff), pl.BlockSpec((1, dh), lambda h, i: (0, h))],
        out_specs=[blk(0), blk(0), pl.BlockSpec((1, dh), lambda h, i: (0, h))],
        out_shape=[jax.ShapeDtypeStruct((t, NH * dh), BF16), jax.ShapeDtypeStruct((t, NH * dh), F32),
                   jax.ShapeDtypeStruct((1, NH * dh), F32)],
        compiler_params=_params("parallel", "arbitrary"),
    )(dy, hval, proj, gain)


def _ln_stats(z):
    mu = jnp.mean(z, axis=1, keepdims=True)
    zc = z - mu
    rstd = lax.rsqrt(jnp.mean(zc * zc, axis=1, keepdims=True) + LN_EPS)
    return zc * rstd, rstd


def _ln_bwd(dy, xhat, rstd, g):
    dxh = dy * g
    return rstd * (dxh - jnp.mean(dxh, axis=1, keepdims=True) - xhat * jnp.mean(dxh * xhat, axis=1, keepdims=True))


def _accum(ref, i, part):
    @pl.when(i == 0)
    def _():
        ref[...] = part

    @pl.when(i > 0)
    def _():
        ref[...] += part


def _ln1_fwd(x, mix, g, b, tr=256):
    t, d = x.shape

    def body(x_ref, m_ref, g_ref, b_ref, xh_ref, rs_ref, xb_ref):
        xhat, rstd = _ln_stats(ALPHA * x_ref[...] + m_ref[...])
        xh_ref[...] = xhat
        rs_ref[...] = rstd
        xb_ref[...] = (xhat * g_ref[...] + b_ref[...]).astype(BF16)

    row = pl.BlockSpec((tr, d), lambda i: (i, 0))
    vec = pl.BlockSpec((1, d), lambda i: (0, 0))
    return pl.pallas_call(
        body, name="ln1_fwd", grid=(t // tr,),
        in_specs=[row, row, vec, vec],
        out_specs=[row, pl.BlockSpec((tr, 1), lambda i: (i, 0)), row],
        out_shape=[jax.ShapeDtypeStruct((t, d), F32), jax.ShapeDtypeStruct((t, 1), F32),
                   jax.ShapeDtypeStruct((t, d), BF16)],
        compiler_params=_params("parallel"),
    )(x, mix, g, b)


def _ln2_loss(xhat1, g1, b1, ff, target, g2, b2, tr=256):
    t, d = ff.shape

    def body(xh_ref, g1_ref, b1_ref, f_ref, t_ref, g_ref, b_ref, dz_ref, dzb_ref, dg_ref, db_ref, l_ref):
        i = pl.program_id(0)
        x1 = xh_ref[...] * g1_ref[...] + b1_ref[...]
        xhat, rstd = _ln_stats(ALPHA * x1 + f_ref[...])
        gv = g_ref[...]
        e = xhat * gv + b_ref[...] - t_ref[...]
        lsum = jnp.sum(jnp.sum(e * e, axis=1, keepdims=True), axis=0, keepdims=True) * (0.5 / d)
        dy = e * (1.0 / d)
        _accum(dg_ref, i, jnp.sum(dy * xhat, axis=0, keepdims=True))
        _accum(db_ref, i, jnp.sum(dy, axis=0, keepdims=True))
        _accum(l_ref, i, jnp.broadcast_to(lsum, l_ref.shape))
        dz = _ln_bwd(dy, xhat, rstd, gv)
        dz_ref[...] = dz
        dzb_ref[...] = dz.astype(BF16)

    row = pl.BlockSpec((tr, d), lambda i: (i, 0))
    vec = pl.BlockSpec((1, d), lambda i: (0, 0))
    return pl.pallas_call(
        body, name="ln2_loss", grid=(t // tr,),
        in_specs=[row, vec, vec, row, row, vec, vec],
        out_specs=[row, row, vec, vec, pl.BlockSpec((8, LANE), lambda i: (0, 0))],
        out_shape=[jax.ShapeDtypeStruct((t, d), F32), jax.ShapeDtypeStruct((t, d), BF16),
                   jax.ShapeDtypeStruct((1, d), F32), jax.ShapeDtypeStruct((1, d), F32),
                   jax.ShapeDtypeStruct((8, LANE), F32)],
        compiler_params=_params("arbitrary"),
    )(xhat1, g1, b1, ff, target, g2, b2)


def _ln1_bwd(dz2, dffn, xhat1, rstd1, g1, tr=256):
    t, d = dz2.shape

    def body(a_ref, f_ref, xh_ref, rs_ref, g_ref, dz_ref, dzb_ref, dg_ref, db_ref):
        i = pl.program_id(0)
        dy = ALPHA * a_ref[...] + f_ref[...]
        xhat = xh_ref[...]
        _accum(dg_ref, i, jnp.sum(dy * xhat, axis=0, keepdims=True))
        _accum(db_ref, i, jnp.sum(dy, axis=0, keepdims=True))
        dz = _ln_bwd(dy, xhat, rs_ref[...], g_ref[...])
        dz_ref[...] = dz
        dzb_ref[...] = dz.astype(BF16)

    row = pl.BlockSpec((tr, d), lambda i: (i, 0))
    vec = pl.BlockSpec((1, d), lambda i: (0, 0))
    return pl.pallas_call(
        body, name="ln1_bwd", grid=(t // tr,),
        in_specs=[row, row, row, pl.BlockSpec((tr, 1), lambda i: (i, 0)), vec],
        out_specs=[row, row, vec, vec],
        out_shape=[jax.ShapeDtypeStruct((t, d), F32), jax.ShapeDtypeStruct((t, d), BF16),
                   jax.ShapeDtypeStruct((1, d), F32), jax.ShapeDtypeStruct((1, d), F32)],
        compiler_params=_params("arbitrary"),
    )(dz2, dffn, xhat1, rstd1, g1)


def _ffn_act_fwd(hid0, w_fc, b_fc, t, dff):
    nb = dff // LANE

    def body(hv_ref, hg_ref, wv_ref, wg_ref, bv_ref, bg_ref, a_ref):
        val = _conv(hv_ref[...], wv_ref[...]) + bv_ref[...]
        gate = _conv(hg_ref[...], wg_ref[...]) + bg_ref[...]
        a_ref[...] = (gate * _sigmoid(gate) * val).astype(BF16)

    col = lambda off: pl.BlockSpec((t, LANE), lambda j: (0, j + off))
    w3 = lambda off: pl.BlockSpec((3, LANE), lambda j: (0, j + off))
    w1 = lambda off: pl.BlockSpec((1, LANE), lambda j: (0, j + off))
    return pl.pallas_call(
        body, name="ffn_act_fwd", grid=(nb,),
        in_specs=[col(0), col(nb), w3(0), w3(nb), w1(0), w1(nb)],
        out_specs=col(0),
        out_shape=jax.ShapeDtypeStruct((t, dff), BF16),
        compiler_params=_params("parallel"),
    )(hid0, hid0, w_fc, w_fc, b_fc, b_fc)


def _ffn_act_bwd(da, hid0, w_fc, b_fc, t, dff):
    nb = dff // LANE

    def body(da_ref, hv_ref, hg_ref, wv_ref, wg_ref, bv_ref, bg_ref,
             dhv_ref, dhg_ref, dwv_ref, dwg_ref, dbv_ref, dbg_ref):
        hv, hg, wv, wg = hv_ref[...], hg_ref[...], wv_ref[...], wg_ref[...]
        rv, rg = _rolled(hv), _rolled(hg)
        val = _conv(hv, wv, rv) + bv_ref[...]
        gate = _conv(hg, wg, rg) + bg_ref[...]
        sig = _sigmoid(gate)
        d = da_ref[...]
        dsig = d * sig
        dval = dsig * gate
        dgate = dsig * val * (1.0 + gate * (1.0 - sig))
        dhv_ref[...] = _conv_t(dval, wv).astype(BF16)
        dhg_ref[...] = _conv_t(dgate, wg).astype(BF16)
        dwv_ref[...] = _conv_dw(dval, hv, rv)
        dwg_ref[...] = _conv_dw(dgate, hg, rg)
        dbv_ref[...] = jnp.sum(dval, axis=0, keepdims=True)
        dbg_ref[...] = jnp.sum(dgate, axis=0, keepdims=True)

    col = lambda off: pl.BlockSpec((t, LANE), lambda j: (0, j + off))
    w3 = lambda off: pl.BlockSpec((3, LANE), lambda j: (0, j + off))
    w1 = lambda off: pl.BlockSpec((1, LANE), lambda j: (0, j + off))
    act = jax.ShapeDtypeStruct((t, dff), BF16)
    s3 = jax.ShapeDtypeStruct((3, dff), F32)
    s1 = jax.ShapeDtypeStruct((1, dff), F32)
    return pl.pallas_call(
        body, name="ffn_act_bwd", grid=(nb,),
        in_specs=[col(0), col(0), col(nb), w3(0), w3(nb), w1(0), w1(nb)],
        out_specs=[col(0), col(0), w3(0), w3(0), w1(0), w1(0)],
        out_shape=[act, act, s3, s3, s1, s1],
        compiler_params=_params("parallel"),
    )(da, hid0, hid0, w_fc, w_fc, b_fc, b_fc)


class _Ready:
    def __init__(self, **weights):
        self.weights = weights

    def begin(self, after):
        return None

    def forward(self, name, after):
        return None

    def get(self, name, after):
        return self.weights[name]


def _local_step(x, target, w_in, b_gates, w_sc, gain, w_out, ln1_g, ln1_b, w_up, w_fc, b_fc, w_down, ln2_g, ln2_b,
                on_grad=None, wx=None):
    t, d = x.shape
    wc = d // 2
    dh = (d - wc) // NH
    wm = NH * dh
    dff = w_fc.shape[1] // 2
    if wx is None:
        wx = _Ready(w_out=w_out, w_up=w_up, w_down=w_down)
    ninp = w_in.shape[1]
    nin = 3 * wc + 4 * wm
    gate_tile = nin // LANE
    nc = t // CHUNK
    bias_tile = jnp.pad(b_gates, ((0, 0), (0, LANE - 2 * NH)))

    x_b = x.astype(BF16)
    proj = _matmul(x_b, w_in, "nn", F32, "proj", tm=1024, tn=1152, tk=d, after=wx.begin(w_in))
    y_conv = _sconv_fwd(proj, w_sc, t, wc)
    gcol = _gates_prep(proj, bias_tile, t, gate_tile)
    grow = gcol[:, :8].T.reshape(8, nc, CHUNK).transpose(1, 0, 2)
    hval, cs, ns = _mlstm_fwd(proj, gcol, grow, t, wc, dh)
    y_m = _hnorm_fwd(hval, proj, gain, t, wc, dh)
    y = jnp.concatenate([y_conv, y_m], axis=1)
    tok = wx.forward("w_up", wx.forward("w_out", y))
    w_out = wx.get("w_out", tok)
    mix = _matmul(y, w_out, "nn", F32, "out_proj", tm=512, tn=1024, tk=d, after=tok)
    xhat1, rstd1, x1_b = _ln1_fwd(x, mix, ln1_g, ln1_b)
    tok = wx.forward("w_down", x1_b)
    w_up = wx.get("w_up", tok)
    wsl = w_up.shape[2]
    hid0 = _matmul(x1_b, w_up, "nn", F32, "ffn_up", tm=512, tn=wsl, tk=d, b_blocked=True, after=tok)
    act = _ffn_act_fwd(hid0, w_fc, b_fc, t, dff)
    w_down = wx.get("w_down", act)
    ff = _matmul(act, w_down, "nn", F32, "ffn_down", tm=1024, tn=512, tk=dff)
    dz2, dz2_b, d_ln2_g, d_ln2_b, loss = _ln2_loss(xhat1, ln1_g, ln1_b, ff, target, ln2_g, ln2_b)

    emit = on_grad if on_grad is not None else (lambda name, g: None)
    d_w_down = _matmul(act, dz2_b, "tn", BF16, "ffn_down_dw", tm=512, tn=1024, tk=t)
    d_act = _matmul(dz2_b, w_down, "nt", F32, "ffn_down_dx", tm=1024, tn=512, tk=d, after=emit("w_down", d_w_down))
    dhv, dhg, dwv, dwg, dbv, dbg = _ffn_act_bwd(d_act, hid0, w_fc, b_fc, t, dff)
    d_hid0 = jnp.concatenate([dhv, dhg], axis=1)
    d_w_fc = jnp.concatenate([dwv, dwg], axis=1)
    d_b_fc = jnp.concatenate([dbv, dbg], axis=1)
    d_w_up = _matmul(x1_b, d_hid0, "tn", BF16, "ffn_up_dw", tm=512, tn=wsl, tk=t, o_width=wsl)
    d_x1_ffn = _matmul(d_hid0, w_up, "nt", F32, "ffn_up_dx", tm=1024, tn=1024, tk=wsl, b_blocked=True,
                       after=emit("w_up", d_w_up))
    dz1, dz1_b, d_ln1_g, d_ln1_b = _ln1_bwd(dz2, d_x1_ffn, xhat1, rstd1, ln1_g)

    d_w_out = _matmul(y, dz1_b, "tn", BF16, "out_proj_dw", tm=512, tn=1024, tk=t)
    dy = _matmul(dz1_b, w_out, "nt", F32, "out_proj_dx", tm=512, tn=1024, tk=d, after=emit("w_out", d_w_out))
    dcb, dcc, dch, d_w_sc = _sconv_bwd(dy, proj, w_sc, t, wc)
    d_o, d_hval, d_gain = _hnorm_bwd(dy, hval, proj, gain, t, wc, dh)
    dq, dk, dv, dgate = _mlstm_bwd(proj, gcol, grow, hval, d_hval, cs, ns, t, wc, dh)
    dgt, d_b_gates = _gates_bwd(dgate, proj, bias_tile, t, gate_tile)
    pad = jnp.zeros((t, ninp - nin - LANE), BF16)
    d_proj = jnp.concatenate([dcb, dcc, dch, dq, dk, dv, d_o, dgt, pad], axis=1)
    d_w_in = _matmul(x_b, d_proj, "tn", BF16, "proj_dw", tm=512, tn=IN_SLAB, tk=t, o_width=IN_SLAB)
    grad_x = _matmul(d_proj, w_in, "nt", F32, "proj_dx", tm=512, tn=512, tk=ninp, add=dz1, add_scale=ALPHA,
                     after=emit("w_in", d_w_in))

    small = dict(b_gates=d_b_gates[:, :2 * NH], w_sc_conv=d_w_sc, mh_gain=d_gain, ln1_g=d_ln1_g, ln1_b=d_ln1_b,
                 w_ffn_conv=d_w_fc, b_ffn_conv=d_b_fc, ln2_g=d_ln2_g, ln2_b=d_ln2_b)
    return loss, grad_x, d_w_in, d_w_out, d_w_up, d_w_down, small


HBM = pl.BlockSpec(memory_space=pltpu.HBM)


def _place():
    return lax.axis_index("x"), lax.axis_index("y"), lax.axis_index("c")


def _index(p):
    return 4 * p[0] + 2 * p[1] + p[2]


def _all_gather(arrs, name):
    n = len(arrs)

    def body(*refs):
        ins, outs = refs[:n], refs[n:2 * n]
        send_sems, recv_sems, local_sems = refs[2 * n:]
        x, y, c = _place()
        me, sibling = (x, y, c), (x, y, 1 - c)
        chips = [(1 - x, y), (x, 1 - y), (1 - x, 1 - y)]

        def copy(a, k, block, to, own=False):
            dst = outs[a].at[_index(block)]
            return pltpu.make_async_remote_copy(
                src_ref=ins[a] if own else dst, dst_ref=dst,
                send_sem=send_sems.at[k * n + a], recv_sem=recv_sems.at[k * n + a],
                device_id=to, device_id_type=MESH)

        mine = [pltpu.make_async_copy(ins[a], outs[a].at[_index(me)], local_sems.at[a]) for a in range(n)]
        for cp in mine:
            cp.start()
        first = []
        for a in range(n):
            first.append(copy(a, 0, me, sibling, own=True))
            first += [copy(a, 1 + j, me, (*chip, c), own=True) for j, chip in enumerate(chips)]
        for cp in first:
            cp.start()
        passed = []
        for j, chip in enumerate(chips):
            for a in range(n):
                copy(a, 1 + j, (*chip, c), me).wait_recv()
                cp = copy(a, 4 + j, (*chip, c), sibling)
                cp.start()
                passed.append(cp)
        for a in range(n):
            copy(a, 0, sibling, me).wait_recv()
            for j, chip in enumerate(chips):
                copy(a, 4 + j, (*chip, 1 - c), me).wait_recv()
        for cp in first + passed:
            cp.wait_send()
        for cp in mine:
            cp.wait()

    return pl.pallas_call(
        body, name=name, in_specs=[HBM] * n, out_specs=[HBM] * n,
        out_shape=[jax.ShapeDtypeStruct((N_DEV,) + a.shape, a.dtype) for a in arrs],
        scratch_shapes=[pltpu.SemaphoreType.DMA((7 * n,)), pltpu.SemaphoreType.DMA((7 * n,)),
                        pltpu.SemaphoreType.DMA((n,))],
    )(*arrs)


SEM = pl.BlockSpec(memory_space=pltpu.SEMAPHORE)
EFFECT = pltpu.SideEffectType.DATAFLOW_SIDE_EFFECTING


def _peers(x, y, c):
    return [(1 - x if j & 4 else x, 1 - y if j & 2 else y, 1 - c if j & 1 else c) for j in range(1, N_DEV)]


def _exchange_pieces(g_ref, land_ref, width, tail):
    if not tail:
        return [(lambda i: g_ref.at[i], lambda s: land_ref.at[s])]
    return [(lambda i: g_ref.at[i], lambda s: land_ref.at[s, :, pl.ds(0, width)]),
            (lambda i: g_ref.at[i + 1, :, pl.ds(0, LANE)], lambda s: land_ref.at[s, :, pl.ds(width, LANE)])]


def _exchange_start(grad, tail, name):
    width = grad.shape[2]
    n_p = 2 if tail else 1
    land_shape = (N_DEV, grad.shape[1], width + (LANE if tail else 0))

    def body(g_ref, land_ref, send_sems, recv_sems, g_thru, land_thru, token):
        x, y, c = _place()
        me = _index((x, y, c))
        for j, peer in enumerate(_peers(x, y, c)):
            for p, (src, dst) in enumerate(_exchange_pieces(g_ref, land_ref, width, tail)):
                pltpu.make_async_remote_copy(src_ref=src(_index(peer)), dst_ref=dst(me), send_sem=send_sems.at[j * n_p + p],
                                             recv_sem=recv_sems.at[j * n_p + p], device_id=peer,
                                             device_id_type=MESH).start()
        token[...] = jnp.zeros_like(token)

    return pl.pallas_call(
        body, name=name,
        out_shape=(pltpu.SemaphoreType.DMA((7 * n_p,)), pltpu.SemaphoreType.DMA((7 * n_p,)),
                   pltpu.HBM(grad.shape, grad.dtype), pltpu.HBM(land_shape, grad.dtype),
                   jax.ShapeDtypeStruct((8, LANE), F32)),
        in_specs=(HBM, HBM), out_specs=(SEM, SEM, HBM, HBM, pl.BlockSpec(memory_space=pltpu.VMEM)),
        input_output_aliases={0: 2, 1: 3},
        compiler_params=pltpu.CompilerParams(has_side_effects=EFFECT),
    )(pltpu.with_memory_space_constraint(grad, pltpu.HBM),
      pltpu.with_memory_space_constraint(lax.empty(land_shape, grad.dtype), pltpu.HBM))


def _exchange_wait(send_sems, recv_sems, g_thru, land_thru, after, tail, name):
    width = g_thru.shape[2]
    n_p = 2 if tail else 1

    def body(g_ref, land_ref, send_sems, recv_sems, after_ref, g_dead, got_ref):
        x, y, c = _place()
        for j, peer in enumerate(_peers(x, y, c)):
            for p, (src, dst) in enumerate(_exchange_pieces(g_ref, land_ref, width, tail)):
                cp = pltpu.make_async_remote_copy(src_ref=src(_index(peer)), dst_ref=dst(_index(peer)),
                                                  send_sem=send_sems.at[j * n_p + p], recv_sem=recv_sems.at[j * n_p + p],
                                                  device_id=peer, device_id_type=MESH)
                cp.wait_send()
                cp.wait_recv()

    return pl.pallas_call(
        body, name=name,
        out_shape=(pltpu.HBM(g_thru.shape, g_thru.dtype), pltpu.HBM(land_thru.shape, land_thru.dtype)),
        in_specs=(HBM, HBM, SEM, SEM, pl.BlockSpec(memory_space=pl.ANY)), out_specs=(HBM, HBM),
        input_output_aliases={0: 0, 1: 1},
        compiler_params=pltpu.CompilerParams(has_side_effects=EFFECT),
    )(g_thru, land_thru, send_sems, recv_sems, after)


def _chips(x, y):
    return [(1 - x, y), (x, 1 - y), (1 - x, 1 - y)]


def _gather_start(block, after, name):
    land_shape = (N_DEV,) + block.shape

    def body(b_ref, land_ref, after_ref, send_sems, recv_sems, b_thru, land_thru, token):
        x, y, c = _place()
        me = _index((x, y, c))
        for k, to in enumerate([(x, y, 1 - c)] + [(*chip, c) for chip in _chips(x, y)]):
            pltpu.make_async_remote_copy(src_ref=b_ref, dst_ref=land_ref.at[me], send_sem=send_sems.at[k],
                                         recv_sem=recv_sems.at[k], device_id=to, device_id_type=MESH).start()
        token[...] = jnp.zeros_like(token)

    return pl.pallas_call(
        body, name=name,
        out_shape=(pltpu.SemaphoreType.DMA((4,)), pltpu.SemaphoreType.DMA((4,)),
                   pltpu.HBM(block.shape, block.dtype), pltpu.HBM(land_shape, block.dtype),
                   jax.ShapeDtypeStruct((8, LANE), F32)),
        in_specs=(HBM, HBM, pl.BlockSpec(memory_space=pl.ANY)),
        out_specs=(SEM, SEM, HBM, HBM, pl.BlockSpec(memory_space=pltpu.VMEM)),
        input_output_aliases={0: 2, 1: 3},
        compiler_params=pltpu.CompilerParams(has_side_effects=EFFECT),
    )(pltpu.with_memory_space_constraint(block, pltpu.HBM),
      pltpu.with_memory_space_constraint(lax.empty(land_shape, block.dtype), pltpu.HBM), after)


def _gather_forward(send_sems, recv_sems, b_thru, land_thru, after, name):
    def body(b_ref, land_ref, send_sems, recv_sems, after_ref, b_dead, land_out, send2, recv2, token):
        x, y, c = _place()
        sibling = (x, y, 1 - c)
        for k, frm in enumerate([sibling] + [(*chip, c) for chip in _chips(x, y)]):
            cp = pltpu.make_async_remote_copy(src_ref=b_ref, dst_ref=land_ref.at[_index(frm)], send_sem=send_sems.at[k],
                                              recv_sem=recv_sems.at[k], device_id=frm, device_id_type=MESH)
            cp.wait_send()
            cp.wait_recv()
        for j, chip in enumerate(_chips(x, y)):
            slot = land_ref.at[_index((*chip, c))]
            pltpu.make_async_remote_copy(src_ref=slot, dst_ref=slot, send_sem=send2.at[j], recv_sem=recv2.at[j],
                                         device_id=sibling, device_id_type=MESH).start()
        token[...] = jnp.zeros_like(token)

    return pl.pallas_call(
        body, name=name,
        out_shape=(pltpu.HBM(b_thru.shape, b_thru.dtype), pltpu.HBM(land_thru.shape, land_thru.dtype),
                   pltpu.SemaphoreType.DMA((3,)), pltpu.SemaphoreType.DMA((3,)), jax.ShapeDtypeStruct((8, LANE), F32)),
        in_specs=(HBM, HBM, SEM, SEM, pl.BlockSpec(memory_space=pl.ANY)),
        out_specs=(HBM, HBM, SEM, SEM, pl.BlockSpec(memory_space=pltpu.VMEM)),
        input_output_aliases={0: 0, 1: 1},
        compiler_params=pltpu.CompilerParams(has_side_effects=EFFECT),
    )(b_thru, land_thru, send_sems, recv_sems, after)


def _gather_finish(land_thru, send2, recv2, after, name):
    def body(land_ref, send2, recv2, after_ref, land_out):
        x, y, c = _place()
        for j, chip in enumerate(_chips(x, y)):
            cp = pltpu.make_async_remote_copy(src_ref=land_ref.at[_index((*chip, c))],
                                              dst_ref=land_ref.at[_index((*chip, 1 - c))], send_sem=send2.at[j],
                                              recv_sem=recv2.at[j], device_id=(x, y, 1 - c), device_id_type=MESH)
            cp.wait_send()
            cp.wait_recv()

    return pl.pallas_call(
        body, name=name, out_shape=pltpu.HBM(land_thru.shape, land_thru.dtype),
        in_specs=(HBM, SEM, SEM, pl.BlockSpec(memory_space=pl.ANY)), out_specs=HBM,
        input_output_aliases={0: 0},
        compiler_params=pltpu.CompilerParams(has_side_effects=EFFECT),
    )(land_thru, send2, recv2, after)


class _Gathering:
    ORDER = ("w_out", "w_up", "w_down")

    def __init__(self, blocks, me):
        self.blocks, self.me, self.state = blocks, me, {}

    def begin(self, after):
        token = after
        for name in self.ORDER:
            *self.state[name], token = _gather_start(self.blocks[name], token, "gather1_" + name)
        return token

    def forward(self, name, after):
        *self.state[name], token = _gather_forward(*self.state[name], after, "gather2_" + name)
        return token

    def get(self, name, after):
        block, land, send2, recv2 = self.state[name]
        land = _gather_finish(land, send2, recv2, after, "gather3_" + name)
        land = lax.dynamic_update_index_in_dim(land, block[None], self.me, 0)
        return land if name == "w_up" else land.reshape(-1, land.shape[2])


def _own_slab(grad, land, me, tail):
    own = lax.dynamic_index_in_dim(grad, me, 0, keepdims=True)
    if tail:
        nxt = lax.dynamic_index_in_dim(grad, me + 1, 0, keepdims=True)[:, :, :LANE]
        own = jnp.concatenate([own, nxt], axis=2)
    return lax.dynamic_update_index_in_dim(land, own, me, 0)


def _assemble_w_in(g, ninp):
    _, d, pw = g.shape
    per = IN_SLAB // LANE
    assert ninp == (N_DEV + 1) * IN_SLAB and pw == IN_SLAB + LANE
    tr = _rows(d, 512)

    def body(a_ref, b_ref, o_ref):
        s = pl.program_id(0)
        a = a_ref[...]
        o_ref[:, 0:LANE] = (jnp.where(s < N_DEV, a[:, 0:LANE], jnp.zeros_like(b_ref))
                            + jnp.where(s > 0, b_ref[...], jnp.zeros_like(b_ref)))
        o_ref[:, LANE:IN_SLAB] = jnp.where(s < N_DEV, a[:, LANE:IN_SLAB], jnp.zeros_like(a[:, LANE:IN_SLAB]))

    return pl.pallas_call(
        body, name="assemble_w_in", grid=(N_DEV + 1, d // tr),
        in_specs=[pl.BlockSpec((None, tr, pw), lambda s, i: (jnp.minimum(s, N_DEV - 1), i, 0)),
                  pl.BlockSpec((None, tr, LANE), lambda s, i: (jnp.maximum(s, 1) - 1, i, per))],
        out_specs=pl.BlockSpec((tr, IN_SLAB), lambda s, i: (i, s)),
        out_shape=jax.ShapeDtypeStruct((d, ninp), g.dtype),
        compiler_params=_params("parallel", "parallel"),
    )(g, g)


def _rows(n, want):
    t = min(n, want)
    t -= t % 16
    while n % t:
        t -= 16
    return t


def _adam_math(w, g, m, v):
    m2 = ADAM_B1 * m + (1.0 - ADAM_B1) * g
    v2 = ADAM_B2 * v + (1.0 - ADAM_B2) * (g * g)
    m_hat = m2 / (1.0 - ADAM_B1 ** ADAM_STEP)
    v_hat = v2 / (1.0 - ADAM_B2 ** ADAM_STEP)
    return -ADAM_LR * (m_hat / (jnp.sqrt(v_hat) + ADAM_EPS) + ADAM_WD * w), m2, v2


def _slot_sum(r_ref):
    acc = r_ref[0].astype(F32)
    for i in range(1, N_DEV):
        acc = acc + r_ref[i].astype(F32)
    return acc


def _shift_w_in(w_pad):
    d, pw = w_pad.shape
    tr = _rows(d, 512)

    def body(w_ref, o_ref):
        o_ref[...] = pltpu.roll(w_ref[...], _index(_place()), 1).astype(BF16)

    blk = pl.BlockSpec((tr, pw), lambda i: (i, 0))
    return pl.pallas_call(
        body, name="shift_w_in", grid=(d // tr,), in_specs=[blk], out_specs=blk,
        out_shape=jax.ShapeDtypeStruct((d, pw), BF16), compiler_params=_params("parallel"),
    )(w_pad)


def _sum_slots(r, name, tr=128, unshift=False):
    _, rows, cols = r.shape
    tr = _rows(rows, tr)

    def body(r_ref, g_ref):
        g = _slot_sum(r_ref)
        g_ref[...] = pltpu.roll(g, lax.rem(cols - _index(_place()), cols), 1) if unshift else g

    return pl.pallas_call(
        body, name=name, grid=(rows // tr,),
        in_specs=[pl.BlockSpec((N_DEV, tr, cols), lambda i: (0, i, 0))],
        out_specs=pl.BlockSpec((tr, cols), lambda i: (i, 0)),
        out_shape=jax.ShapeDtypeStruct((rows, cols), F32),
        compiler_params=_params("parallel"),
    )(r)


def _adamw(w, g, m, v, name, tr=256):
    rows, cols = w.shape
    tr = _rows(rows, tr)

    def body(w_ref, g_ref, m_ref, v_ref, d_ref, m2_ref, v2_ref):
        d_ref[...], m2_ref[...], v2_ref[...] = _adam_math(w_ref[...], g_ref[...], m_ref[...], v_ref[...])

    blk = pl.BlockSpec((tr, cols), lambda i: (i, 0))
    out = jax.ShapeDtypeStruct((rows, cols), F32)
    return pl.pallas_call(
        body, name=name, grid=(rows // tr,), in_specs=[blk] * 4, out_specs=[blk] * 3, out_shape=[out] * 3,
        compiler_params=_params("parallel"),
    )(w, g, m, v)


def _sum_adamw(r, w, m, v, name, tr=128):
    rows, cols = w.shape
    tr = _rows(rows, tr)

    def body(r_ref, w_ref, m_ref, v_ref, g_ref, d_ref, m2_ref, v2_ref):
        g = _slot_sum(r_ref)
        g_ref[...] = g
        d_ref[...], m2_ref[...], v2_ref[...] = _adam_math(w_ref[...], g, m_ref[...], v_ref[...])

    blk = pl.BlockSpec((tr, cols), lambda i: (i, 0))
    out = jax.ShapeDtypeStruct((rows, cols), F32)
    return pl.pallas_call(
        body, name=name, grid=(rows // tr,),
        in_specs=[pl.BlockSpec((N_DEV, tr, cols), lambda i: (0, i, 0)), blk, blk, blk],
        out_specs=[blk] * 4, out_shape=[out] * 4,
        compiler_params=_params("parallel"),
    )(r, w, m, v)


def _pack(pieces, sizes):
    flat = [jnp.pad(p.reshape(-1).astype(F32), (0, s - p.size)) for p, s in zip(pieces, sizes)]
    total = sum(sizes)
    padded = -(-total // (16 * LANE)) * (16 * LANE)
    return jnp.pad(jnp.concatenate(flat), (0, padded - total)).reshape(-1, LANE)


def _unpack(packed, shapes, sizes):
    flat = packed.reshape(-1)
    out, off = [], 0
    for shp, s in zip(shapes, sizes):
        n = 1
        for k in shp:
            n *= k
        out.append(flat[off:off + n].reshape(shp))
        off += s
    return out


def _lanes(n):
    return -(-n // LANE) * LANE


WEIGHTS = ("w_in", "b_gates", "w_sc_conv", "mh_gain", "w_out", "ln1_g", "ln1_b", "w_up", "w_ffn_conv", "b_ffn_conv",
           "w_down", "ln2_g", "ln2_b")
BIG = ("w_in", "w_out", "w_up", "w_down")
SMALL = tuple(n for n in WEIGHTS if n not in BIG)


def kernel(x, w_in, b_gates, w_sc_conv, mh_gain, w_out, ln1_g, ln1_b, w_up, w_ffn_conv, b_ffn_conv, w_down, ln2_g, ln2_b, loss_target, m_w_in, m_b_gates, m_w_sc_conv, m_mh_gain, m_w_out, m_ln1_g, m_ln1_b, m_w_up, m_w_ffn_conv, m_b_ffn_conv, m_w_down, m_ln2_g, m_ln2_b, v_w_in, v_b_gates, v_w_sc_conv, v_mh_gain, v_w_out, v_ln1_g, v_ln1_b, v_w_up, v_w_ffn_conv, v_b_ffn_conv, v_w_down, v_ln2_g, v_ln2_b):
    w = dict(zip(WEIGHTS, (w_in, b_gates, w_sc_conv, mh_gain, w_out, ln1_g, ln1_b, w_up, w_ffn_conv, b_ffn_conv,
                           w_down, ln2_g, ln2_b)))
    m = dict(zip(WEIGHTS, (m_w_in, m_b_gates, m_w_sc_conv, m_mh_gain, m_w_out, m_ln1_g, m_ln1_b, m_w_up,
                           m_w_ffn_conv, m_b_ffn_conv, m_w_down, m_ln2_g, m_ln2_b)))
    v = dict(zip(WEIGHTS, (v_w_in, v_b_gates, v_w_sc_conv, v_mh_gain, v_w_out, v_ln1_g, v_ln1_b, v_w_up,
                           v_w_ffn_conv, v_b_ffn_conv, v_w_down, v_ln2_g, v_ln2_b)))
    me = _index(_place())
    d = x.shape[2]
    ws_in = w_in.shape[2]
    assert ws_in == IN_SLAB + 1 and N_DEV <= LANE, w_in.shape
    ninp = (N_DEV + 1) * IN_SLAB
    ws_sc, ws_fc = w_sc_conv.shape[2], w_ffn_conv.shape[2]

    w_in_shift = _shift_w_in(jnp.pad(w_in[0], ((0, 0), (0, IN_SLAB + LANE - ws_in))))
    taps8 = lambda a: jnp.pad(a[0], ((0, 5), (0, 0)))
    g_in, g_sc, g_fc = _all_gather([w_in_shift, taps8(w_sc_conv), taps8(w_ffn_conv)], "gather_w_in")
    w_in_full = _assemble_w_in(g_in, ninp)
    w_sc_full = g_sc[:, :3].transpose(1, 0, 2).reshape(3, N_DEV * ws_sc)
    w_fc_full = g_fc[:, :3].transpose(1, 0, 2).reshape(3, N_DEV * ws_fc)
    wx = _Gathering({n: w[n][0].astype(BF16) for n in ("w_out", "w_up", "w_down")}, me)

    in_flight = {}

    def send(name, g):
        g = g if g.ndim == 3 else g.reshape(N_DEV, g.shape[0] // N_DEV, g.shape[1])
        *in_flight[name], token = _exchange_start(g, name == "w_in", "send_" + name)
        return token

    loss_t, grad_x, _, _, _, _, small = _local_step(
        x[0], loss_target[0], w_in_full, b_gates, w_sc_full, mh_gain, None, ln1_g, ln1_b, None,
        w_fc_full, b_ffn_conv, None, ln2_g, ln2_b, on_grad=send, wx=wx)

    landed = {}
    for name in ("w_down", "w_up", "w_out", "w_in"):
        sent, got = _exchange_wait(*in_flight[name], grad_x, name == "w_in", "recv_" + name)
        landed[name] = _own_slab(sent, got, me, name == "w_in")
    r_in, r_out, r_up, r_down = landed["w_in"], landed["w_out"], landed["w_up"], landed["w_down"]
    grads, deltas, new_m, new_v = {}, {}, {}, {}
    grads["w_in"] = _sum_slots(r_in, "sum_w_in", unshift=True)[:, :ws_in]
    deltas["w_in"], new_m["w_in"], new_v["w_in"] = _adamw(w_in[0], grads["w_in"], m_w_in[0], v_w_in[0], "adamw_w_in")
    for name, r in (("w_out", r_out), ("w_up", r_up), ("w_down", r_down)):
        grads[name], deltas[name], new_m[name], new_v[name] = _sum_adamw(r, w[name][0], m[name][0], v[name][0],
                                                                         "adamw_" + name)

    names = ("loss",) + SMALL
    parts = dict(small, loss=loss_t[0, :1])
    sizes = [_lanes(parts[n].size) for n in names]
    (g_small,) = _all_gather([_pack([parts[n] for n in names], sizes)], "gather_small")
    summed = _unpack(_sum_slots(g_small, "sum_small", tr=g_small.shape[1]), [parts[n].shape for n in names], sizes)
    full = dict(zip(names, summed))
    full["w_sc_conv"] = lax.dynamic_slice(full["w_sc_conv"], (0, me * ws_sc), (3, ws_sc))
    full["w_ffn_conv"] = lax.dynamic_slice(full["w_ffn_conv"], (0, me * ws_fc), (3, ws_fc))
    for n in SMALL:
        grads[n] = full[n].reshape(w[n].shape)
    sizes = [_lanes(w[n].size) for n in SMALL]
    shapes = [w[n].shape for n in SMALL]
    packed = [_pack([t[n] for n in SMALL], sizes) for t in (w, grads, m, v)]
    for res, t in zip(_adamw(*packed, "adamw_small"), (deltas, new_m, new_v)):
        t.update(zip(SMALL, _unpack(res, shapes, sizes)))

    big = lambda t: {n: (t[n].reshape(w[n].shape) if n in BIG else t[n]) for n in WEIGHTS}
    grads, deltas, new_m, new_v = big(grads), big(deltas), big(new_m), big(new_v)
    return (full["loss"].reshape(()), grad_x[None], *[grads[n] for n in WEIGHTS], *[deltas[n] for n in WEIGHTS],
            *[new_m[n] for n in WEIGHTS], *[new_v[n] for n in WEIGHTS])
```

```python
import functools

import jax
import jax.numpy as jnp
from jax import lax
from jax.experimental import pallas as pl
from jax.experimental.pallas import tpu as pltpu

F32 = jnp.float32
BF16 = jnp.bfloat16
MESH = pl.DeviceIdType.MESH

N_DEV = 8
NH = 4
CHUNK = 64
LN_EPS = 1e-5
HN_EPS = 1e-6
ALPHA = 2.0 ** 0.25
LANE = 128
IN_SLAB = 7 * LANE
VMEM_LIMIT = 56 * 1024 * 1024
ADAM_LR, ADAM_B1, ADAM_B2, ADAM_EPS, ADAM_WD, ADAM_STEP = 0.001, 0.9, 0.999, 1e-08, 0.01, 10

_NN = (((1,), (0,)), ((), ()))
_NT = (((1,), (1,)), ((), ()))
_TN = (((0,), (0,)), ((), ()))


def _dot(a, b, dn=_NN):
    return lax.dot_general(a, b, dn, preferred_element_type=F32)


def _params(*sem):
    return pltpu.CompilerParams(dimension_semantics=sem if sem else None, vmem_limit_bytes=VMEM_LIMIT)


def _iota(shape, axis):
    return lax.broadcasted_iota(jnp.int32, shape, axis)


def _fit(n, want):
    if n <= want:
        return n
    t = want - want % LANE
    while n % t:
        t -= LANE
    return t


def _matmul(a, b, mode, out_dtype, name, tm=1024, tn=512, tk=1024, add=None, add_scale=1.0,
            b_blocked=False, o_width=None, after=None):
    if mode == "tn":
        kd, m = a.shape
    else:
        m, kd = a.shape
    if b_blocked:
        nb, rows, w = b.shape
        n = nb * w if mode == "nn" else rows
        assert (nb * w if mode == "nt" else rows) == kd, (name, b.shape, kd)
    else:
        n = b.shape[0] if mode == "nt" else b.shape[1]
    tm, tn, tk = _fit(m, tm), _fit(n, tn), _fit(kd, tk)
    if b_blocked and mode == "nn":
        tn = _fit(w, tn)
    if b_blocked and mode == "nt":
        tk = _fit(w, tk)
    if o_width is not None:
        tn = _fit(o_width, tn)
    assert m % tm == 0 and n % tn == 0 and kd % tk == 0, (name, m, n, kd, tm, tn, tk)
    nk = kd // tk
    dn = {"nn": _NN, "nt": _NT, "tn": _TN}[mode]
    a_spec = (pl.BlockSpec((tk, tm), lambda i, j, k: (k, i)) if mode == "tn"
              else pl.BlockSpec((tm, tk), lambda i, j, k: (i, k)))
    if b_blocked and mode == "nn":
        per = w // tn
        b_spec = pl.BlockSpec((None, tk, tn), lambda i, j, k: (j // per, k, j % per))
    elif b_blocked:
        per = w // tk
        b_spec = pl.BlockSpec((None, tn, tk), lambda i, j, k: (k // per, j, k % per))
    elif mode == "nt":
        b_spec = pl.BlockSpec((tn, tk), lambda i, j, k: (j, k))
    else:
        b_spec = pl.BlockSpec((tk, tn), lambda i, j, k: (k, j))
    if o_width is None:
        o_spec = pl.BlockSpec((tm, tn), lambda i, j, k: (i, j))
        o_shape = (m, n)
    else:
        oper = o_width // tn
        o_spec = pl.BlockSpec((None, tm, tn), lambda i, j, k: (j // oper, i, j % oper))
        o_shape = (n // o_width, m, o_width)
    has_add = add is not None
    n_in = 2 + has_add + (after is not None)
    in_place = nk > 1 and out_dtype == F32

    def body(*refs):
        a_ref, b_ref = refs[:2]
        add_ref = refs[2] if has_add else None
        o_ref = refs[n_in]

        def finish(r):
            if has_add:
                r = r + add_scale * add_ref[...]
            o_ref[...] = r.astype(out_dtype)

        if nk == 1:
            finish(_dot(a_ref[...], b_ref[...], dn))
        else:
            acc = o_ref if in_place else refs[-1]
            k = pl.program_id(2)

            @pl.when(k == 0)
            def _():
                acc[...] = _dot(a_ref[...], b_ref[...], dn)

            @pl.when(k > 0)
            def _():
                acc[...] += _dot(a_ref[...], b_ref[...], dn)

            if not (in_place and not has_add):
                @pl.when(k == nk - 1)
                def _():
                    finish(acc[...])

    in_specs = [a_spec, b_spec] + ([pl.BlockSpec((tm, tn), lambda i, j, k: (i, j))] if has_add else [])
    args = (a, b) + ((add,) if has_add else ())
    if after is not None:
        in_specs.append(pl.BlockSpec(memory_space=pl.ANY))
        args += (after,)
    return pl.pallas_call(
        body, name=name, grid=(m // tm, n // tn, nk),
        in_specs=in_specs, out_specs=o_spec,
        out_shape=jax.ShapeDtypeStruct(o_shape, out_dtype),
        scratch_shapes=[pltpu.VMEM((tm, tn), F32)] if nk > 1 and not in_place else [],
        compiler_params=_params("parallel", "parallel", "arbitrary"),
    )(*args)


def _shift_down(u, s):
    return jnp.where(_iota(u.shape, 0) >= s, pltpu.roll(u, s, 0), 0.0)


def _shift_up(u, s):
    t = u.shape[0]
    return jnp.where(_iota(u.shape, 0) < t - s, pltpu.roll(u, t - s, 0), 0.0)


SLAB = 8


def _rolled(u):
    return pltpu.roll(u, 2, 0), pltpu.roll(u, 1, 0)


def _conv(u, w, rolled=None):
    u2, u1 = _rolled(u) if rolled is None else rolled
    raw = w[0:1] * u2 + w[1:2] * u1 + w[2:3] * u
    head = u[0:SLAB]
    mended = w[0:1] * _shift_down(head, 2) + w[1:2] * _shift_down(head, 1) + w[2:3] * head
    return jnp.concatenate([mended, raw[SLAB:]], axis=0)


def _conv_t(dy, w):
    t = dy.shape[0]
    raw = w[2:3] * dy + w[1:2] * pltpu.roll(dy, t - 1, 0) + w[0:1] * pltpu.roll(dy, t - 2, 0)
    tail = dy[t - SLAB:]
    mended = w[2:3] * tail + w[1:2] * _shift_up(tail, 1) + w[0:1] * _shift_up(tail, 2)
    return jnp.concatenate([raw[:t - SLAB], mended], axis=0)


def _conv_dw(dy, u, rolled=None):
    t = dy.shape[0]
    u2, u1 = _rolled(u) if rolled is None else rolled
    head, tail = dy[0:SLAB], u[t - SLAB:]
    r = _iota(head.shape, 0)
    wrap2 = jnp.sum(jnp.where(r < 2, head * pltpu.roll(tail, 2, 0), 0.0), axis=0, keepdims=True)
    wrap1 = jnp.sum(jnp.where(r < 1, head * pltpu.roll(tail, 1, 0), 0.0), axis=0, keepdims=True)
    d0 = jnp.sum(dy * u2, axis=0, keepdims=True) - wrap2
    d1 = jnp.sum(dy * u1, axis=0, keepdims=True) - wrap1
    d2 = jnp.sum(dy * u, axis=0, keepdims=True)
    r3 = _iota((3, dy.shape[1]), 0)
    return jnp.where(r3 == 0, d0, jnp.where(r3 == 1, d1, d2))


def _sigmoid(x):
    return 0.5 * jnp.tanh(0.5 * x) + 0.5


def _sconv_fwd(proj, w_sc, t, wc):
    nb = wc // LANE

    def body(cb_ref, cc_ref, ch_ref, w_ref, y_ref):
        u = cc_ref[...] * ch_ref[...]
        y_ref[...] = (cb_ref[...] * _conv(u, w_ref[...])).astype(BF16)

    col = lambda off: pl.BlockSpec((t, LANE), lambda j: (0, j + off))
    return pl.pallas_call(
        body, name="sconv_fwd", grid=(nb,),
        in_specs=[col(0), col(nb), col(2 * nb), pl.BlockSpec((3, LANE), lambda j: (0, j))],
        out_specs=pl.BlockSpec((t, LANE), lambda j: (0, j)),
        out_shape=jax.ShapeDtypeStruct((t, wc), BF16),
        compiler_params=_params("parallel"),
    )(proj, proj, proj, w_sc)


def _sconv_bwd(dy, proj, w_sc, t, wc):
    nb = wc // LANE

    def body(dy_ref, cb_ref, cc_ref, ch_ref, w_ref, dcb_ref, dcc_ref, dch_ref, dw_ref):
        cc, ch, w, d = cc_ref[...], ch_ref[...], w_ref[...], dy_ref[...]
        u = cc * ch
        ru = _rolled(u)
        dcb_ref[...] = (d * _conv(u, w, ru)).astype(BF16)
        dcu = d * cb_ref[...]
        dw_ref[...] = _conv_dw(dcu, u, ru)
        du = _conv_t(dcu, w)
        dcc_ref[...] = (du * ch).astype(BF16)
        dch_ref[...] = (du * cc).astype(BF16)

    col = lambda off: pl.BlockSpec((t, LANE), lambda j: (0, j + off))
    act = jax.ShapeDtypeStruct((t, wc), BF16)
    return pl.pallas_call(
        body, name="sconv_bwd", grid=(nb,),
        in_specs=[col(0), col(0), col(nb), col(2 * nb), pl.BlockSpec((3, LANE), lambda j: (0, j))],
        out_specs=[col(0), col(0), col(0), pl.BlockSpec((3, LANE), lambda j: (0, j))],
        out_shape=[act, act, act, jax.ShapeDtypeStruct((3, wc), F32)],
        compiler_params=_params("parallel"),
    )(dy, proj, proj, proj, w_sc)


def _gates_prep(proj, bias_tile, t, gate_tile):
    def body(g_ref, b_ref, o_ref):
        g = g_ref[...] + b_ref[...]
        lane = _iota(g.shape, 1)
        is_f = (lane >= NH) & (lane < 2 * NH)
        lf = jnp.minimum(g, 0.0) - jnp.log(1.0 + jnp.exp(-jnp.abs(g)))
        c = jnp.where(is_f, lf, 0.0)
        r = _iota(g.shape, 0) % CHUNK
        s = 1
        while s < CHUNK:
            c = c + jnp.where(r >= s, pltpu.roll(c, s, 0), 0.0)
            s *= 2
        o_ref[...] = jnp.where(is_f, c, jnp.where(lane < NH, g, 0.0))

    return pl.pallas_call(
        body, name="gates_prep", grid=(1,),
        in_specs=[pl.BlockSpec((t, LANE), lambda i: (0, gate_tile)), pl.BlockSpec((1, LANE), lambda i: (0, 0))],
        out_specs=pl.BlockSpec((t, LANE), lambda i: (0, 0)),
        out_shape=jax.ShapeDtypeStruct((t, LANE), F32),
        compiler_params=_params("arbitrary"),
    )(proj, bias_tile)


def _gates_bwd(dgate, proj, bias_tile, t, gate_tile):
    def body(dg_ref, g_ref, b_ref, o_ref, s_ref):
        g = g_ref[...] + b_ref[...]
        lane = _iota(g.shape, 1)
        r = _iota(g.shape, 0) % CHUNK
        dsig = 1.0 - _sigmoid(g)
        out = jnp.zeros(g.shape, F32)
        for h in range(NH):
            d = dg_ref[h]
            c = d
            s = 1
            while s < CHUNK:
                c = c + jnp.where(r + s < CHUNK, pltpu.roll(c, t - s, 0), 0.0)
                s *= 2
            di = jnp.broadcast_to(d[:, 0:1], g.shape)
            db = jnp.broadcast_to(c[:, 1:2], g.shape)
            out = out + jnp.where(lane == h, di, 0.0) + jnp.where(lane == NH + h, db * dsig, 0.0)
        o_ref[...] = out.astype(BF16)
        s_ref[...] = jnp.sum(out, axis=0, keepdims=True)

    return pl.pallas_call(
        body, name="gates_bwd", grid=(1,),
        in_specs=[pl.BlockSpec((NH, t, LANE), lambda i: (0, 0, 0)),
                  pl.BlockSpec((t, LANE), lambda i: (0, gate_tile)), pl.BlockSpec((1, LANE), lambda i: (0, 0))],
        out_specs=[pl.BlockSpec((t, LANE), lambda i: (0, 0)), pl.BlockSpec((1, LANE), lambda i: (0, 0))],
        out_shape=[jax.ShapeDtypeStruct((t, LANE), BF16), jax.ShapeDtypeStruct((1, LANE), F32)],
        compiler_params=_params("arbitrary"),
    )(dgate, proj, bias_tile)


def _chunk_gates(gc, gr, h, mprev):
    L = CHUNK
    icol, bcol = gc[:, h:h + 1], gc[:, h + NH:h + NH + 1]
    irow, brow = gr[h:h + 1, :], gr[h + NH:h + NH + 1, :]
    tri = _iota((L, L), 0) >= _iota((L, L), 1)
    log_d = jnp.where(tri, bcol - brow + irow, -jnp.inf)
    inter = bcol + mprev
    mt = jnp.maximum(inter, jnp.max(log_d, axis=1, keepdims=True))
    dw = jnp.exp(log_d - mt)
    iw = jnp.exp(inter - mt)
    g = brow[:, L - 1:L]
    wlog_col = g - bcol + icol
    wlog_row = g - brow + irow
    mnew = jnp.maximum(g + mprev, jnp.max(wlog_row, axis=1, keepdims=True))
    wcol = jnp.exp(wlog_col - mnew)
    decay = jnp.exp(g + mprev - mnew)
    return dw, iw, mt, wcol, decay, mnew


def _mlstm_fwd(proj, gcol, grow, t, wc, dh):
    nc = t // CHUNK
    wm = NH * dh
    assert wc == wm, (wc, wm)
    qoff = 3 * wc // wm
    scale = dh ** -0.5

    def body(q_ref, k_ref, v_ref, gc_ref, gr_ref, h_ref, cs_ref, ns_ref, c_s, n_s, m_s):
        @pl.when(pl.program_id(0) == 0)
        def _():
            c_s[...] = jnp.zeros_like(c_s)
            n_s[...] = jnp.zeros_like(n_s)
            m_s[...] = jnp.zeros_like(m_s)

        gc, gr = gc_ref[...], gr_ref[0]
        for h in range(NH):
            cols = slice(h * dh, (h + 1) * dh)
            mprev = m_s[h, 0:1, 0:1]
            cprev = c_s[h]
            n8 = n_s[h]
            nprev = n8[0:1]
            cs_ref[h] = cprev
            ns_ref[h] = jnp.where(_iota(n8.shape, 0) == 1, mprev, n8)

            dw, iw, mt, wcol, decay, mnew = _chunk_gates(gc, gr, h, mprev)
            qs = q_ref[:, cols] * scale
            k = k_ref[:, cols]
            qs_b, k_b, v_b = qs.astype(BF16), k.astype(BF16), v_ref[:, cols].astype(BF16)
            s = _dot(qs_b, k_b, _NT) * dw
            num = _dot(s.astype(BF16), v_b) + iw * _dot(qs_b, cprev.astype(BF16))
            den = jnp.sum(s, axis=1, keepdims=True) + iw * jnp.sum(qs * nprev, axis=1, keepdims=True)
            h_ref[:, cols] = num / jnp.maximum(jnp.abs(den), jnp.exp(-mt))

            wk = wcol * k
            c_s[h] = decay * cprev + _dot(wk.astype(BF16), v_b, _TN)
            n_s[h] = decay * n8 + jnp.sum(wk, axis=0, keepdims=True)
            m_s[h] = jnp.broadcast_to(mnew, m_s.shape[1:])

    grp = lambda off: pl.BlockSpec((CHUNK, wm), lambda c: (c, qoff + off))
    return pl.pallas_call(
        body, name="mlstm_fwd", grid=(nc,),
        in_specs=[grp(0), grp(1), grp(2),
                  pl.BlockSpec((CHUNK, LANE), lambda c: (c, 0)),
                  pl.BlockSpec((1, 8, CHUNK), lambda c: (c, 0, 0))],
        out_specs=[pl.BlockSpec((CHUNK, wm), lambda c: (c, 0)),
                   pl.BlockSpec((NH, None, dh, dh), lambda c: (0, c, 0, 0)),
                   pl.BlockSpec((NH, None, 8, dh), lambda c: (0, c, 0, 0))],
        out_shape=[jax.ShapeDtypeStruct((t, wm), F32),
                   jax.ShapeDtypeStruct((NH, nc, dh, dh), F32),
                   jax.ShapeDtypeStruct((NH, nc, 8, dh), F32)],
        scratch_shapes=[pltpu.VMEM((NH, dh, dh), F32), pltpu.VMEM((NH, 8, dh), F32), pltpu.VMEM((NH, 8, LANE), F32)],
        compiler_params=_params("arbitrary"),
    )(proj, proj, proj, gcol, grow)


def _mlstm_bwd(proj, gcol, grow, hval, dh_in, cs, ns, t, wc, dh):
    nc = t // CHUNK
    wm = NH * dh
    assert wc == wm, (wc, wm)
    qoff = 3 * wc // wm
    scale = dh ** -0.5
    L = CHUNK

    def body(q_ref, k_ref, v_ref, gc_ref, gr_ref, h_ref, dh_ref, cs_ref, ns_ref,
             dq_ref, dk_ref, dv_ref, dg_ref, dc_s, dn_s):
        @pl.when(pl.program_id(0) == 0)
        def _():
            dc_s[...] = jnp.zeros_like(dc_s)
            dn_s[...] = jnp.zeros_like(dn_s)

        gc, gr = gc_ref[...], gr_ref[0]
        eye = _iota((L, L), 0) == _iota((L, L), 1)
        lane = _iota((L, LANE), 1)
        last = _iota((L, 1), 0) == L - 1
        for h in range(NH):
            cols = slice(h * dh, (h + 1) * dh)
            ns8 = ns_ref[h]
            nprev = ns8[0:1]
            mprev = ns8[1:2, 0:1]
            cprev = cs_ref[h]
            dcn = dc_s[h]
            dn8 = dn_s[h]
            dnn = dn8[0:1]

            dw, iw, mt, wcol, decay, _ = _chunk_gates(gc, gr, h, mprev)
            qs = q_ref[:, cols] * scale
            k = k_ref[:, cols]
            qs_b, k_b, v_b = qs.astype(BF16), k.astype(BF16), v_ref[:, cols].astype(BF16)
            qk = _dot(qs_b, k_b, _NT)
            s = qk * dw
            den = jnp.sum(s, axis=1, keepdims=True) + iw * jnp.sum(qs * nprev, axis=1, keepdims=True)
            emt = jnp.exp(-mt)
            r = 1.0 / jnp.maximum(jnp.abs(den), emt)
            dout = dh_ref[:, cols]
            dnum = dout * r
            dden = (-jnp.sum(dout * h_ref[:, cols], axis=1, keepdims=True) * r
                    * jnp.where(jnp.abs(den) > emt, jnp.sign(den), 0.0))
            dnum_b = dnum.astype(BF16)
            cprev_b = cprev.astype(BF16)
            dcn_b = dcn.astype(BF16)

            gd = (_dot(dnum_b, v_b, _NT) + dden) * dw
            gd_b = gd.astype(BF16)
            dqs_inter = iw * (_dot(dnum_b, cprev_b, _NT) + dden * nprev)
            dqs = _dot(gd_b, k_b) + dqs_inter
            dk_inter = wcol * (_dot(v_b, dcn_b, _NT) + dnn)
            dk = _dot(gd_b, qs_b, _TN) + dk_inter
            wk = wcol * k
            dv = _dot(s.astype(BF16), dnum_b, _TN) + _dot(wk.astype(BF16), dcn_b)

            e = gd * qk
            e_cols = jnp.sum(jnp.where(eye, jnp.sum(e, axis=0, keepdims=True), 0.0), axis=1, keepdims=True)
            k_inter = jnp.sum(k * dk_inter, axis=1, keepdims=True)
            rq = jnp.sum(e, axis=1, keepdims=True) + jnp.sum(qs * dqs_inter, axis=1, keepdims=True)
            rk = e_cols + k_inter
            hsum = jnp.sum(k_inter, axis=0, keepdims=True)
            jdec = decay * (jnp.sum(jnp.sum(dcn * cprev, axis=1, keepdims=True), axis=0, keepdims=True)
                            + jnp.sum(dnn * nprev, axis=1, keepdims=True))
            db = rq - rk + jnp.where(last, hsum + jdec, 0.0)
            dg_ref[h] = jnp.where(lane == 0, rk, jnp.where(lane == 1, db, 0.0))

            dq_ref[:, cols] = (dqs * scale).astype(BF16)
            dk_ref[:, cols] = dk.astype(BF16)
            dv_ref[:, cols] = dv.astype(BF16)

            iq = iw * qs
            dc_s[h] = decay * dcn + _dot(iq.astype(BF16), dnum_b, _TN)
            dn_s[h] = decay * dn8 + jnp.sum(iq * dden, axis=0, keepdims=True)

    rc = lambda c: nc - 1 - c
    grp = lambda off: pl.BlockSpec((L, wm), lambda c: (rc(c), qoff + off))
    hm = pl.BlockSpec((L, wm), lambda c: (rc(c), 0))
    act = jax.ShapeDtypeStruct((t, wm), BF16)
    return pl.pallas_call(
        body, name="mlstm_bwd", grid=(nc,),
        in_specs=[grp(0), grp(1), grp(2),
                  pl.BlockSpec((L, LANE), lambda c: (rc(c), 0)),
                  pl.BlockSpec((1, 8, L), lambda c: (rc(c), 0, 0)),
                  hm, hm,
                  pl.BlockSpec((NH, None, dh, dh), lambda c: (0, rc(c), 0, 0)),
                  pl.BlockSpec((NH, None, 8, dh), lambda c: (0, rc(c), 0, 0))],
        out_specs=[hm, hm, hm, pl.BlockSpec((NH, L, LANE), lambda c: (0, rc(c), 0))],
        out_shape=[act, act, act, jax.ShapeDtypeStruct((NH, t, LANE), F32)],
        scratch_shapes=[pltpu.VMEM((NH, dh, dh), F32), pltpu.VMEM((NH, 8, dh), F32)],
        compiler_params=_params("arbitrary"),
    )(proj, proj, proj, gcol, grow, hval, dh_in, cs, ns)


def _head_norm(hv):
    mu = jnp.mean(hv, axis=1, keepdims=True)
    hc = hv - mu
    rstd = lax.rsqrt(jnp.mean(hc * hc, axis=1, keepdims=True) + HN_EPS)
    return hc * rstd, rstd


def _hnorm_fwd(hval, proj, gain, t, wc, dh, tr=256):
    ooff = 3 * wc // dh + 3 * NH

    def body(h_ref, o_ref, g_ref, y_ref):
        hhat, _ = _head_norm(h_ref[...])
        y_ref[...] = (_sigmoid(o_ref[...]) * hhat * g_ref[...]).astype(BF16)

    return pl.pallas_call(
        body, name="hnorm_fwd", grid=(t // tr, NH),
        in_specs=[pl.BlockSpec((tr, dh), lambda i, h: (i, h)),
                  pl.BlockSpec((tr, dh), lambda i, h: (i, ooff + h)),
                  pl.BlockSpec((1, dh), lambda i, h: (0, h))],
        out_specs=pl.BlockSpec((tr, dh), lambda i, h: (i, h)),
        out_shape=jax.ShapeDtypeStruct((t, NH * dh), BF16),
        compiler_params=_params("parallel", "parallel"),
    )(hval, proj, gain)


def _hnorm_bwd(dy, hval, proj, gain, t, wc, dh, tr=256):
    ooff = 3 * wc // dh + 3 * NH
    yoff = wc // dh

    def body(dy_ref, h_ref, o_ref, g_ref, do_ref, dh_ref, dg_ref):
        i = pl.program_id(1)
        hhat, rstd = _head_norm(h_ref[...])
        gain_v = g_ref[...]
        sig = _sigmoid(o_ref[...])
        d = dy_ref[...]
        do_ref[...] = (d * hhat * gain_v * sig * (1.0 - sig)).astype(BF16)
        dhn = d * sig
        part = jnp.sum(dhn * hhat, axis=0, keepdims=True)

        @pl.when(i == 0)
        def _():
            dg_ref[...] = part

        @pl.when(i > 0)
        def _():
            dg_ref[...] += part

        dhat = dhn * gain_v
        dh_ref[...] = rstd * (dhat - jnp.mean(dhat, axis=1, keepdims=True)
                              - hhat * jnp.mean(dhat * hhat, axis=1, keepdims=True))

    blk = lambda off: pl.BlockSpec((tr, dh), lambda h, i: (i, off + h))
    return pl.pallas_call(
        body, name="hnorm_bwd", grid=(NH, t // tr),
        in_specs=[blk(yoff), blk(0), blk(ooff), pl.BlockSpec((1, dh), lambda h, i: (0, h))],
        out_specs=[blk(0), blk(0), pl.BlockSpec((1, dh), lambda h, i: (0, h))],
        out_shape=[jax.ShapeDtypeStruct((t, NH * dh), BF16), jax.ShapeDtypeStruct((t, NH * dh), F32),
                   jax.ShapeDtypeStruct((1, NH * dh), F32)],
        compiler_params=_params("parallel", "arbitrary"),
    )(dy, hval, proj, gain)


def _ln_stats(z):
    mu = jnp.mean(z, axis=1, keepdims=True)
    zc = z - mu
    rstd = lax.rsqrt(jnp.mean(zc * zc, axis=1, keepdims=True) + LN_EPS)
    return zc * rstd, rstd


def _ln_bwd(dy, xhat, rstd, g):
    dxh = dy * g
    return rstd * (dxh - jnp.mean(dxh, axis=1, keepdims=True) - xhat * jnp.mean(dxh * xhat, axis=1, keepdims=True))


def _accum(ref, i, part):
    @pl.when(i == 0)
    def _():
        ref[...] = part

    @pl.when(i > 0)
    def _():
        ref[...] += part


def _ln1_fwd(x, mix, g, b, tr=256):
    t, d = x.shape

    def body(x_ref, m_ref, g_ref, b_ref, xh_ref, rs_ref, xb_ref):
        xhat, rstd = _ln_stats(ALPHA * x_ref[...] + m_ref[...])
        xh_ref[...] = xhat
        rs_ref[...] = rstd
        xb_ref[...] = (xhat * g_ref[...] + b_ref[...]).astype(BF16)

    row = pl.BlockSpec((tr, d), lambda i: (i, 0))
    vec = pl.BlockSpec((1, d), lambda i: (0, 0))
    return pl.pallas_call(
        body, name="ln1_fwd", grid=(t // tr,),
        in_specs=[row, row, vec, vec],
        out_specs=[row, pl.BlockSpec((tr, 1), lambda i: (i, 0)), row],
        out_shape=[jax.ShapeDtypeStruct((t, d), F32), jax.ShapeDtypeStruct((t, 1), F32),
                   jax.ShapeDtypeStruct((t, d), BF16)],
        compiler_params=_params("parallel"),
    )(x, mix, g, b)


def _ln2_loss(xhat1, g1, b1, ff, target, g2, b2, tr=256):
    t, d = ff.shape

    def body(xh_ref, g1_ref, b1_ref, f_ref, t_ref, g_ref, b_ref, dz_ref, dzb_ref, dg_ref, db_ref, l_ref):
        i = pl.program_id(0)
        x1 = xh_ref[...] * g1_ref[...] + b1_ref[...]
        xhat, rstd = _ln_stats(ALPHA * x1 + f_ref[...])
        gv = g_ref[...]
        e = xhat * gv + b_ref[...] - t_ref[...]
        lsum = jnp.sum(jnp.sum(e * e, axis=1, keepdims=True), axis=0, keepdims=True) * (0.5 / d)
        dy = e * (1.0 / d)
        _accum(dg_ref, i, jnp.sum(dy * xhat, axis=0, keepdims=True))
        _accum(db_ref, i, jnp.sum(dy, axis=0, keepdims=True))
        _accum(l_ref, i, jnp.broadcast_to(lsum, l_ref.shape))
        dz = _ln_bwd(dy, xhat, rstd, gv)
        dz_ref[...] = dz
        dzb_ref[...] = dz.astype(BF16)

    row = pl.BlockSpec((tr, d), lambda i: (i, 0))
    vec = pl.BlockSpec((1, d), lambda i: (0, 0))
    return pl.pallas_call(
        body, name="ln2_loss", grid=(t // tr,),
        in_specs=[row, vec, vec, row, row, vec, vec],
        out_specs=[row, row, vec, vec, pl.BlockSpec((8, LANE), lambda i: (0, 0))],
        out_shape=[jax.ShapeDtypeStruct((t, d), F32), jax.ShapeDtypeStruct((t, d), BF16),
                   jax.ShapeDtypeStruct((1, d), F32), jax.ShapeDtypeStruct((1, d), F32),
                   jax.ShapeDtypeStruct((8, LANE), F32)],
        compiler_params=_params("arbitrary"),
    )(xhat1, g1, b1, ff, target, g2, b2)


def _ln1_bwd(dz2, dffn, xhat1, rstd1, g1, tr=256):
    t, d = dz2.shape

    def body(a_ref, f_ref, xh_ref, rs_ref, g_ref, dz_ref, dzb_ref, dg_ref, db_ref):
        i = pl.program_id(0)
        dy = ALPHA * a_ref[...] + f_ref[...]
        xhat = xh_ref[...]
        _accum(dg_ref, i, jnp.sum(dy * xhat, axis=0, keepdims=True))
        _accum(db_ref, i, jnp.sum(dy, axis=0, keepdims=True))
        dz = _ln_bwd(dy, xhat, rs_ref[...], g_ref[...])
        dz_ref[...] = dz
        dzb_ref[...] = dz.astype(BF16)

    row = pl.BlockSpec((tr, d), lambda i: (i, 0))
    vec = pl.BlockSpec((1, d), lambda i: (0, 0))
    return pl.pallas_call(
        body, name="ln1_bwd", grid=(t // tr,),
        in_specs=[row, row, row, pl.BlockSpec((tr, 1), lambda i: (i, 0)), vec],
        out_specs=[row, row, vec, vec],
        out_shape=[jax.ShapeDtypeStruct((t, d), F32), jax.ShapeDtypeStruct((t, d), BF16),
                   jax.ShapeDtypeStruct((1, d), F32), jax.ShapeDtypeStruct((1, d), F32)],
        compiler_params=_params("arbitrary"),
    )(dz2, dffn, xhat1, rstd1, g1)


def _ffn_act_fwd(hid0, w_fc, b_fc, t, dff):
    nb = dff // LANE

    def body(hv_ref, hg_ref, wv_ref, wg_ref, bv_ref, bg_ref, a_ref):
        val = _conv(hv_ref[...], wv_ref[...]) + bv_ref[...]
        gate = _conv(hg_ref[...], wg_ref[...]) + bg_ref[...]
        a_ref[...] = (gate * _sigmoid(gate) * val).astype(BF16)

    col = lambda off: pl.BlockSpec((t, LANE), lambda j: (0, j + off))
    w3 = lambda off: pl.BlockSpec((3, LANE), lambda j: (0, j + off))
    w1 = lambda off: pl.BlockSpec((1, LANE), lambda j: (0, j + off))
    return pl.pallas_call(
        body, name="ffn_act_fwd", grid=(nb,),
        in_specs=[col(0), col(nb), w3(0), w3(nb), w1(0), w1(nb)],
        out_specs=col(0),
        out_shape=jax.ShapeDtypeStruct((t, dff), BF16),
        compiler_params=_params("parallel"),
    )(hid0, hid0, w_fc, w_fc, b_fc, b_fc)


def _ffn_act_bwd(da, hid0, w_fc, b_fc, t, dff):
    nb = dff // LANE

    def body(da_ref, hv_ref, hg_ref, wv_ref, wg_ref, bv_ref, bg_ref,
             dhv_ref, dhg_ref, dwv_ref, dwg_ref, dbv_ref, dbg_ref):
        hv, hg, wv, wg = hv_ref[...], hg_ref[...], wv_ref[...], wg_ref[...]
        rv, rg = _rolled(hv), _rolled(hg)
        val = _conv(hv, wv, rv) + bv_ref[...]
        gate = _conv(hg, wg, rg) + bg_ref[...]
        sig = _sigmoid(gate)
        d = da_ref[...]
        dsig = d * sig
        dval = dsig * gate
        dgate = dsig * val * (1.0 + gate * (1.0 - sig))
        dhv_ref[...] = _conv_t(dval, wv).astype(BF16)
        dhg_ref[...] = _conv_t(dgate, wg).astype(BF16)
        dwv_ref[...] = _conv_dw(dval, hv, rv)
        dwg_ref[...] = _conv_dw(dgate, hg, rg)
        dbv_ref[...] = jnp.sum(dval, axis=0, keepdims=True)
        dbg_ref[...] = jnp.sum(dgate, axis=0, keepdims=True)

    col = lambda off: pl.BlockSpec((t, LANE), lambda j: (0, j + off))
    w3 = lambda off: pl.BlockSpec((3, LANE), lambda j: (0, j + off))
    w1 = lambda off: pl.BlockSpec((1, LANE), lambda j: (0, j + off))
    act = jax.ShapeDtypeStruct((t, dff), BF16)
    s3 = jax.ShapeDtypeStruct((3, dff), F32)
    s1 = jax.ShapeDtypeStruct((1, dff), F32)
    return pl.pallas_call(
        body, name="ffn_act_bwd", grid=(nb,),
        in_specs=[col(0), col(0), col(nb), w3(0), w3(nb), w1(0), w1(nb)],
        out_specs=[col(0), col(0), w3(0), w3(0), w1(0), w1(0)],
        out_shape=[act, act, s3, s3, s1, s1],
        compiler_params=_params("parallel"),
    )(da, hid0, hid0, w_fc, w_fc, b_fc, b_fc)


class _Ready:
    def __init__(self, **weights):
        self.weights = weights

    def begin(self, after):
        return None

    def forward(self, name, after):
        return None

    def get(self, name, after):
        return self.weights[name]


class _Kept:
    def __init__(self):
        self.grads = {}

    def start(self, name, grad):
        self.grads[name] = grad
        return None

    def relay(self, name, after):
        return None


def _behind(a, token):
    return a if token is None else a + token[0:1, 0:1].reshape((1,) * a.ndim)


def _local_step(x, target, w_in, b_gates, w_sc, gain, w_out, ln1_g, ln1_b, w_up, w_fc, b_fc, w_down, ln2_g, ln2_b,
                gx=None, wx=None):
    t, d = x.shape
    wc = d // 2
    dh = (d - wc) // NH
    wm = NH * dh
    dff = w_fc.shape[1] // 2
    if wx is None:
        wx = _Ready(w_out=w_out, w_up=w_up, w_down=w_down)
    ninp = w_in.shape[1]
    nin = 3 * wc + 4 * wm
    gate_tile = nin // LANE
    nc = t // CHUNK
    bias_tile = jnp.pad(b_gates, ((0, 0), (0, LANE - 2 * NH)))

    x_b = x.astype(BF16)
    proj = _matmul(x_b, w_in, "nn", F32, "proj", tm=1024, tn=1152, tk=d, after=wx.begin(w_in))
    y_conv = _sconv_fwd(proj, w_sc, t, wc)
    gcol = _gates_prep(proj, bias_tile, t, gate_tile)
    grow = gcol[:, :8].T.reshape(8, nc, CHUNK).transpose(1, 0, 2)
    hval, cs, ns = _mlstm_fwd(proj, gcol, grow, t, wc, dh)
    y_m = _hnorm_fwd(hval, proj, gain, t, wc, dh)
    y = jnp.concatenate([y_conv, y_m], axis=1)
    tok = wx.forward("w_up", wx.forward("w_out", y))
    w_out = wx.get("w_out", tok)
    mix = _matmul(y, w_out, "nn", F32, "out_proj", tm=512, tn=1024, tk=d, after=tok)
    xhat1, rstd1, x1_b = _ln1_fwd(x, mix, ln1_g, ln1_b)
    tok = wx.forward("w_down", x1_b)
    w_up = wx.get("w_up", tok)
    wsl = w_up.shape[2]
    hid0 = _matmul(x1_b, w_up, "nn", F32, "ffn_up", tm=512, tn=wsl, tk=d, b_blocked=True, after=tok)
    act = _ffn_act_fwd(hid0, w_fc, b_fc, t, dff)
    w_down = wx.get("w_down", act)
    ff = _matmul(act, w_down, "nn", F32, "ffn_down", tm=1024, tn=512, tk=dff)
    dz2, dz2_b, d_ln2_g, d_ln2_b, loss = _ln2_loss(xhat1, ln1_g, ln1_b, ff, target, ln2_g, ln2_b)

    if gx is None:
        gx = _Kept()
    d_w_down = _matmul(act, dz2_b, "tn", BF16, "ffn_down_dw", tm=512, tn=1024, tk=t)
    d_act = _matmul(dz2_b, w_down, "nt", F32, "ffn_down_dx", tm=1024, tn=512, tk=d, after=gx.start("w_down", d_w_down))
    dhv, dhg, dwv, dwg, dbv, dbg = _ffn_act_bwd(d_act, hid0, w_fc, _behind(b_fc, gx.relay("w_down", d_act)), t, dff)
    d_hid0 = jnp.concatenate([dhv, dhg], axis=1)
    d_w_fc = jnp.concatenate([dwv, dwg], axis=1)
    d_b_fc = jnp.concatenate([dbv, dbg], axis=1)
    d_w_up = _matmul(x1_b, d_hid0, "tn", BF16, "ffn_up_dw", tm=512, tn=wsl, tk=t, o_width=wsl)
    d_x1_ffn = _matmul(d_hid0, w_up, "nt", F32, "ffn_up_dx", tm=1024, tn=1024, tk=wsl, b_blocked=True,
                       after=gx.start("w_up", d_w_up))
    dz1, dz1_b, d_ln1_g, d_ln1_b = _ln1_bwd(dz2, d_x1_ffn, xhat1, rstd1, _behind(ln1_g, gx.relay("w_up", d_x1_ffn)))

    d_w_out = _matmul(y, dz1_b, "tn", BF16, "out_proj_dw", tm=512, tn=1024, tk=t)
    dy = _matmul(dz1_b, w_out, "nt", F32, "out_proj_dx", tm=512, tn=1024, tk=d, after=gx.start("w_out", d_w_out))
    dcb, dcc, dch, d_w_sc = _sconv_bwd(dy, proj, _behind(w_sc, gx.relay("w_out", dy)), t, wc)
    d_o, d_hval, d_gain = _hnorm_bwd(dy, hval, proj, gain, t, wc, dh)
    dq, dk, dv, dgate = _mlstm_bwd(proj, gcol, grow, hval, d_hval, cs, ns, t, wc, dh)
    dgt, d_b_gates = _gates_bwd(dgate, proj, bias_tile, t, gate_tile)
    pad = jnp.zeros((t, ninp - nin - LANE), BF16)
    d_proj = jnp.concatenate([dcb, dcc, dch, dq, dk, dv, d_o, dgt, pad], axis=1)
    d_w_in = _matmul(x_b, d_proj, "tn", BF16, "proj_dw", tm=512, tn=IN_SLAB, tk=t, o_width=IN_SLAB)
    grad_x = _matmul(d_proj, w_in, "nt", F32, "proj_dx", tm=512, tn=512, tk=ninp, add=dz1, add_scale=ALPHA,
                     after=gx.start("w_in", d_w_in))
    gx.relay("w_in", grad_x)

    small = dict(b_gates=d_b_gates[:, :2 * NH], w_sc_conv=d_w_sc, mh_gain=d_gain, ln1_g=d_ln1_g, ln1_b=d_ln1_b,
                 w_ffn_conv=d_w_fc, b_ffn_conv=d_b_fc, ln2_g=d_ln2_g, ln2_b=d_ln2_b)
    return loss, grad_x, small, gx


HBM = pl.BlockSpec(memory_space=pltpu.HBM)


def _place():
    return lax.axis_index("x"), lax.axis_index("y"), lax.axis_index("c")


def _index(p):
    return 4 * p[0] + 2 * p[1] + p[2]


def _all_gather(arrs, name):
    n = len(arrs)

    def body(*refs):
        ins, outs = refs[:n], refs[n:2 * n]
        send_sems, recv_sems, local_sems = refs[2 * n:]
        x, y, c = _place()
        me, sibling = (x, y, c), (x, y, 1 - c)
        chips = [(1 - x, y), (x, 1 - y), (1 - x, 1 - y)]

        def copy(a, k, block, to, own=False):
            dst = outs[a].at[_index(block)]
            return pltpu.make_async_remote_copy(
                src_ref=ins[a] if own else dst, dst_ref=dst,
                send_sem=send_sems.at[k * n + a], recv_sem=recv_sems.at[k * n + a],
                device_id=to, device_id_type=MESH)

        mine = [pltpu.make_async_copy(ins[a], outs[a].at[_index(me)], local_sems.at[a]) for a in range(n)]
        for cp in mine:
            cp.start()
        first = []
        for a in range(n):
            first.append(copy(a, 0, me, sibling, own=True))
            first += [copy(a, 1 + j, me, (*chip, c), own=True) for j, chip in enumerate(chips)]
        for cp in first:
            cp.start()
        passed = []
        for j, chip in enumerate(chips):
            for a in range(n):
                copy(a, 1 + j, (*chip, c), me).wait_recv()
                cp = copy(a, 4 + j, (*chip, c), sibling)
                cp.start()
                passed.append(cp)
        for a in range(n):
            copy(a, 0, sibling, me).wait_recv()
            for j, chip in enumerate(chips):
                copy(a, 4 + j, (*chip, 1 - c), me).wait_recv()
        for cp in first + passed:
            cp.wait_send()
        for cp in mine:
            cp.wait()

    return pl.pallas_call(
        body, name=name, in_specs=[HBM] * n, out_specs=[HBM] * n,
        out_shape=[jax.ShapeDtypeStruct((N_DEV,) + a.shape, a.dtype) for a in arrs],
        scratch_shapes=[pltpu.SemaphoreType.DMA((7 * n,)), pltpu.SemaphoreType.DMA((7 * n,)),
                        pltpu.SemaphoreType.DMA((n,))],
    )(*arrs)


SEM = pl.BlockSpec(memory_space=pltpu.SEMAPHORE)
EFFECT = pltpu.SideEffectType.DATAFLOW_SIDE_EFFECTING


def _chips(x, y):
    return [(1 - x, y), (x, 1 - y), (1 - x, 1 - y)]


N_CHIP = N_DEV // 2


def _pair_route(x, y, c):
    return [((x, y, 1 - c), 2 * q + (1 - c), q, q) for q in range(N_CHIP)]


def _chip_route(x, y, c):
    mine = 2 * x + y
    return [((*chip, c), 2 * chip[0] + chip[1], mine, 2 * chip[0] + chip[1]) for chip in _chips(x, y)]


def _exchange_pieces(g_ref, land_ref, width, tail):
    if not tail:
        return [(lambda i: g_ref.at[i], lambda s: land_ref.at[s])]
    return [(lambda i: g_ref.at[i], lambda s: land_ref.at[s, :, pl.ds(0, width)]),
            (lambda i: g_ref.at[i + 1, :, pl.ds(0, LANE)], lambda s: land_ref.at[s, :, pl.ds(width, LANE)])]


def _exchange_start(grad, route, tail, name):
    width = grad.shape[2]
    n_p = 2 if tail else 1
    n_c = len(route(0, 0, 0))
    land_shape = (N_CHIP, grad.shape[1], width + (LANE if tail else 0))

    def body(g_ref, land_ref, send_sems, recv_sems, g_thru, land_thru, token):
        for j, (peer, slab, slot, _) in enumerate(route(*_place())):
            for p, (src, dst) in enumerate(_exchange_pieces(g_ref, land_ref, width, tail)):
                pltpu.make_async_remote_copy(src_ref=src(slab), dst_ref=dst(slot), send_sem=send_sems.at[j * n_p + p],
                                             recv_sem=recv_sems.at[j * n_p + p], device_id=peer,
                                             device_id_type=MESH).start()
        token[...] = jnp.zeros_like(token)

    return pl.pallas_call(
        body, name=name,
        out_shape=(pltpu.SemaphoreType.DMA((n_c * n_p,)), pltpu.SemaphoreType.DMA((n_c * n_p,)),
                   pltpu.HBM(grad.shape, grad.dtype), pltpu.HBM(land_shape, grad.dtype),
                   jax.ShapeDtypeStruct((8, LANE), F32)),
        in_specs=(HBM, HBM), out_specs=(SEM, SEM, HBM, HBM, pl.BlockSpec(memory_space=pltpu.VMEM)),
        input_output_aliases={0: 2, 1: 3},
        compiler_params=pltpu.CompilerParams(has_side_effects=EFFECT),
    )(pltpu.with_memory_space_constraint(grad, pltpu.HBM),
      pltpu.with_memory_space_constraint(lax.empty(land_shape, grad.dtype), pltpu.HBM))


def _exchange_wait(send_sems, recv_sems, g_thru, land_thru, after, route, tail, name):
    width = g_thru.shape[2]
    n_p = 2 if tail else 1

    def body(g_ref, land_ref, send_sems, recv_sems, after_ref, g_dead, got_ref):
        for j, (peer, slab, _, slot) in enumerate(route(*_place())):
            for p, (src, dst) in enumerate(_exchange_pieces(g_ref, land_ref, width, tail)):
                cp = pltpu.make_async_remote_copy(src_ref=src(slab), dst_ref=dst(slot),
                                                  send_sem=send_sems.at[j * n_p + p], recv_sem=recv_sems.at[j * n_p + p],
                                                  device_id=peer, device_id_type=MESH)
                cp.wait_send()
                cp.wait_recv()

    return pl.pallas_call(
        body, name=name,
        out_shape=(pltpu.HBM(g_thru.shape, g_thru.dtype), pltpu.HBM(land_thru.shape, land_thru.dtype)),
        in_specs=(HBM, HBM, SEM, SEM, pl.BlockSpec(memory_space=pl.ANY)), out_specs=(HBM, HBM),
        input_output_aliases={0: 0, 1: 1},
        compiler_params=pltpu.CompilerParams(has_side_effects=EFFECT),
    )(g_thru, land_thru, send_sems, recv_sems, after)


def _pair_add(grad, pair, core, tail, name):
    width = grad.shape[2]
    rows = grad.shape[1]
    tr = _rows(rows, 256)
    total = pair.shape[2]

    def body(core_ref, *refs):
        if tail:
            g_ref, t_ref, p_ref, o_ref = refs
            o_ref[:, 0:width] = (g_ref[...].astype(F32) + p_ref[:, 0:width].astype(F32)).astype(BF16)
            o_ref[:, width:total] = (t_ref[...].astype(F32) + p_ref[:, width:total].astype(F32)).astype(BF16)
        else:
            g_ref, p_ref, o_ref = refs
            o_ref[...] = (g_ref[...].astype(F32) + p_ref[...].astype(F32)).astype(BF16)

    slab = pl.BlockSpec((None, tr, total), lambda q, i, core_ref: (q, i, 0))
    in_specs = [pl.BlockSpec((None, tr, width), lambda q, i, core_ref: (2 * q + core_ref[0], i, 0))]
    if tail:
        in_specs.append(pl.BlockSpec((None, tr, LANE), lambda q, i, core_ref: (2 * q + core_ref[0] + 1, i, 0)))
    return pl.pallas_call(
        body, name=name,
        grid_spec=pltpu.PrefetchScalarGridSpec(num_scalar_prefetch=1, grid=(N_CHIP, rows // tr),
                                               in_specs=in_specs + [slab], out_specs=slab),
        out_shape=jax.ShapeDtypeStruct(pair.shape, BF16),
        compiler_params=_params("parallel", "parallel"),
    )(core, *([grad, grad] if tail else [grad]), pair)


def _gather_start(block, after, name):
    land_shape = (N_DEV,) + block.shape

    def body(b_ref, land_ref, after_ref, send_sems, recv_sems, b_thru, land_thru, token):
        x, y, c = _place()
        me = _index((x, y, c))
        for k, to in enumerate([(x, y, 1 - c)] + [(*chip, c) for chip in _chips(x, y)]):
            pltpu.make_async_remote_copy(src_ref=b_ref, dst_ref=land_ref.at[me], send_sem=send_sems.at[k],
                                         recv_sem=recv_sems.at[k], device_id=to, device_id_type=MESH).start()
        token[...] = jnp.zeros_like(token)

    return pl.pallas_call(
        body, name=name,
        out_shape=(pltpu.SemaphoreType.DMA((4,)), pltpu.SemaphoreType.DMA((4,)),
                   pltpu.HBM(block.shape, block.dtype), pltpu.HBM(land_shape, block.dtype),
                   jax.ShapeDtypeStruct((8, LANE), F32)),
        in_specs=(HBM, HBM, pl.BlockSpec(memory_space=pl.ANY)),
        out_specs=(SEM, SEM, HBM, HBM, pl.BlockSpec(memory_space=pltpu.VMEM)),
        input_output_aliases={0: 2, 1: 3},
        compiler_params=pltpu.CompilerParams(has_side_effects=EFFECT),
    )(pltpu.with_memory_space_constraint(block, pltpu.HBM),
      pltpu.with_memory_space_constraint(lax.empty(land_shape, block.dtype), pltpu.HBM), after)


def _gather_forward(send_sems, recv_sems, b_thru, land_thru, after, name):
    def body(b_ref, land_ref, send_sems, recv_sems, after_ref, b_dead, land_out, send2, recv2, token):
        x, y, c = _place()
        sibling = (x, y, 1 - c)
        for k, frm in enumerate([sibling] + [(*chip, c) for chip in _chips(x, y)]):
            cp = pltpu.make_async_remote_copy(src_ref=b_ref, dst_ref=land_ref.at[_index(frm)], send_sem=send_sems.at[k],
                                              recv_sem=recv_sems.at[k], device_id=frm, device_id_type=MESH)
            cp.wait_send()
            cp.wait_recv()
        for j, chip in enumerate(_chips(x, y)):
            slot = land_ref.at[_index((*chip, c))]
            pltpu.make_async_remote_copy(src_ref=slot, dst_ref=slot, send_sem=send2.at[j], recv_sem=recv2.at[j],
                                         device_id=sibling, device_id_type=MESH).start()
        token[...] = jnp.zeros_like(token)

    return pl.pallas_call(
        body, name=name,
        out_shape=(pltpu.HBM(b_thru.shape, b_thru.dtype), pltpu.HBM(land_thru.shape, land_thru.dtype),
                   pltpu.SemaphoreType.DMA((3,)), pltpu.SemaphoreType.DMA((3,)), jax.ShapeDtypeStruct((8, LANE), F32)),
        in_specs=(HBM, HBM, SEM, SEM, pl.BlockSpec(memory_space=pl.ANY)),
        out_specs=(HBM, HBM, SEM, SEM, pl.BlockSpec(memory_space=pltpu.VMEM)),
        input_output_aliases={0: 0, 1: 1},
        compiler_params=pltpu.CompilerParams(has_side_effects=EFFECT),
    )(b_thru, land_thru, send_sems, recv_sems, after)


def _gather_finish(land_thru, send2, recv2, after, name):
    def body(land_ref, send2, recv2, after_ref, land_out):
        x, y, c = _place()
        for j, chip in enumerate(_chips(x, y)):
            cp = pltpu.make_async_remote_copy(src_ref=land_ref.at[_index((*chip, c))],
                                              dst_ref=land_ref.at[_index((*chip, 1 - c))], send_sem=send2.at[j],
                                              recv_sem=recv2.at[j], device_id=(x, y, 1 - c), device_id_type=MESH)
            cp.wait_send()
            cp.wait_recv()

    return pl.pallas_call(
        body, name=name, out_shape=pltpu.HBM(land_thru.shape, land_thru.dtype),
        in_specs=(HBM, SEM, SEM, pl.BlockSpec(memory_space=pl.ANY)), out_specs=HBM,
        input_output_aliases={0: 0},
        compiler_params=pltpu.CompilerParams(has_side_effects=EFFECT),
    )(land_thru, send2, recv2, after)


class _Gathering:
    ORDER = ("w_out", "w_up", "w_down")

    def __init__(self, blocks, me):
        self.blocks, self.me, self.state = blocks, me, {}

    def begin(self, after):
        token = after
        for name in self.ORDER:
            *self.state[name], token = _gather_start(self.blocks[name], token, "gather1_" + name)
        return token

    def forward(self, name, after):
        *self.state[name], token = _gather_forward(*self.state[name], after, "gather2_" + name)
        return token

    def get(self, name, after):
        block, land, send2, recv2 = self.state[name]
        land = _gather_finish(land, send2, recv2, after, "gather3_" + name)
        land = lax.dynamic_update_index_in_dim(land, block[None], self.me, 0)
        return land if name == "w_up" else land.reshape(-1, land.shape[2])


class _Reducing:
    def __init__(self, core, chip):
        self.core, self.chip, self.state = core, chip, {}

    def start(self, name, grad):
        g = grad if grad.ndim == 3 else grad.reshape(N_DEV, grad.shape[0] // N_DEV, grad.shape[1])
        *self.state[name], token = _exchange_start(g, _pair_route, name == "w_in", "pair_send_" + name)
        return token

    def relay(self, name, after):
        tail = name == "w_in"
        grad, pair = _exchange_wait(*self.state[name], after, _pair_route, tail, "pair_recv_" + name)
        total = _pair_add(grad, pair, self.core, tail, "pair_add_" + name)
        *self.state[name], token = _exchange_start(total, _chip_route, False, "chip_send_" + name)
        return token

    def finish(self, name, after):
        total, land = _exchange_wait(*self.state[name], after, _chip_route, False, "chip_recv_" + name)
        own = lax.dynamic_index_in_dim(total, self.chip, 0, keepdims=True)
        return lax.dynamic_update_index_in_dim(land, own, self.chip, 0)


def _assemble_w_in(g, ninp):
    _, d, pw = g.shape
    per = IN_SLAB // LANE
    assert ninp == (N_DEV + 1) * IN_SLAB and pw == IN_SLAB + LANE
    tr = _rows(d, 512)

    def body(a_ref, b_ref, o_ref):
        s = pl.program_id(0)
        a = a_ref[...]
        o_ref[:, 0:LANE] = (jnp.where(s < N_DEV, a[:, 0:LANE], jnp.zeros_like(b_ref))
                            + jnp.where(s > 0, b_ref[...], jnp.zeros_like(b_ref)))
        o_ref[:, LANE:IN_SLAB] = jnp.where(s < N_DEV, a[:, LANE:IN_SLAB], jnp.zeros_like(a[:, LANE:IN_SLAB]))

    return pl.pallas_call(
        body, name="assemble_w_in", grid=(N_DEV + 1, d // tr),
        in_specs=[pl.BlockSpec((None, tr, pw), lambda s, i: (jnp.minimum(s, N_DEV - 1), i, 0)),
                  pl.BlockSpec((None, tr, LANE), lambda s, i: (jnp.maximum(s, 1) - 1, i, per))],
        out_specs=pl.BlockSpec((tr, IN_SLAB), lambda s, i: (i, s)),
        out_shape=jax.ShapeDtypeStruct((d, ninp), g.dtype),
        compiler_params=_params("parallel", "parallel"),
    )(g, g)


def _rows(n, want):
    t = min(n, want)
    t -= t % 16
    while n % t:
        t -= 16
    return t


def _adam_math(w, g, m, v):
    m2 = ADAM_B1 * m + (1.0 - ADAM_B1) * g
    v2 = ADAM_B2 * v + (1.0 - ADAM_B2) * (g * g)
    m_hat = m2 / (1.0 - ADAM_B1 ** ADAM_STEP)
    v_hat = v2 / (1.0 - ADAM_B2 ** ADAM_STEP)
    return -ADAM_LR * (m_hat / (jnp.sqrt(v_hat) + ADAM_EPS) + ADAM_WD * w), m2, v2


def _slot_sum(r_ref):
    acc = r_ref[0].astype(F32)
    for i in range(1, r_ref.shape[0]):
        acc = acc + r_ref[i].astype(F32)
    return acc


def _shift_w_in(w_pad):
    d, pw = w_pad.shape
    tr = _rows(d, 512)

    def body(w_ref, o_ref):
        o_ref[...] = pltpu.roll(w_ref[...], _index(_place()), 1).astype(BF16)

    blk = pl.BlockSpec((tr, pw), lambda i: (i, 0))
    return pl.pallas_call(
        body, name="shift_w_in", grid=(d // tr,), in_specs=[blk], out_specs=blk,
        out_shape=jax.ShapeDtypeStruct((d, pw), BF16), compiler_params=_params("parallel"),
    )(w_pad)


def _sum_slots(r, name, tr=128, unshift=False):
    _, rows, cols = r.shape
    tr = _rows(rows, tr)

    def body(r_ref, g_ref):
        g = _slot_sum(r_ref)
        g_ref[...] = pltpu.roll(g, lax.rem(cols - _index(_place()), cols), 1) if unshift else g

    return pl.pallas_call(
        body, name=name, grid=(rows // tr,),
        in_specs=[pl.BlockSpec((r.shape[0], tr, cols), lambda i: (0, i, 0))],
        out_specs=pl.BlockSpec((tr, cols), lambda i: (i, 0)),
        out_shape=jax.ShapeDtypeStruct((rows, cols), F32),
        compiler_params=_params("parallel"),
    )(r)


def _adamw(w, g, m, v, name, tr=256):
    rows, cols = w.shape
    tr = _rows(rows, tr)

    def body(w_ref, g_ref, m_ref, v_ref, d_ref, m2_ref, v2_ref):
        d_ref[...], m2_ref[...], v2_ref[...] = _adam_math(w_ref[...], g_ref[...], m_ref[...], v_ref[...])

    blk = pl.BlockSpec((tr, cols), lambda i: (i, 0))
    out = jax.ShapeDtypeStruct((rows, cols), F32)
    return pl.pallas_call(
        body, name=name, grid=(rows // tr,), in_specs=[blk] * 4, out_specs=[blk] * 3, out_shape=[out] * 3,
        compiler_params=_params("parallel"),
    )(w, g, m, v)


def _sum_adamw(r, w, m, v, name, tr=128):
    rows, cols = w.shape
    tr = _rows(rows, tr)

    def body(r_ref, w_ref, m_ref, v_ref, g_ref, d_ref, m2_ref, v2_ref):
        g = _slot_sum(r_ref)
        g_ref[...] = g
        d_ref[...], m2_ref[...], v2_ref[...] = _adam_math(w_ref[...], g, m_ref[...], v_ref[...])

    blk = pl.BlockSpec((tr, cols), lambda i: (i, 0))
    out = jax.ShapeDtypeStruct((rows, cols), F32)
    return pl.pallas_call(
        body, name=name, grid=(rows // tr,),
        in_specs=[pl.BlockSpec((r.shape[0], tr, cols), lambda i: (0, i, 0)), blk, blk, blk],
        out_specs=[blk] * 4, out_shape=[out] * 4,
        compiler_params=_params("parallel"),
    )(r, w, m, v)


def _pack(pieces, sizes):
    flat = [jnp.pad(p.reshape(-1).astype(F32), (0, s - p.size)) for p, s in zip(pieces, sizes)]
    total = sum(sizes)
    padded = -(-total // (16 * LANE)) * (16 * LANE)
    return jnp.pad(jnp.concatenate(flat), (0, padded - total)).reshape(-1, LANE)


def _unpack(packed, shapes, sizes):
    flat = packed.reshape(-1)
    out, off = [], 0
    for shp, s in zip(shapes, sizes):
        n = 1
        for k in shp:
            n *= k
        out.append(flat[off:off + n].reshape(shp))
        off += s
    return out


def _lanes(n):
    return -(-n // LANE) * LANE


WEIGHTS = ("w_in", "b_gates", "w_sc_conv", "mh_gain", "w_out", "ln1_g", "ln1_b", "w_up", "w_ffn_conv", "b_ffn_conv",
           "w_down", "ln2_g", "ln2_b")
BIG = ("w_in", "w_out", "w_up", "w_down")
SMALL = tuple(n for n in WEIGHTS if n not in BIG)


def kernel(x, w_in, b_gates, w_sc_conv, mh_gain, w_out, ln1_g, ln1_b, w_up, w_ffn_conv, b_ffn_conv, w_down, ln2_g, ln2_b, loss_target, m_w_in, m_b_gates, m_w_sc_conv, m_mh_gain, m_w_out, m_ln1_g, m_ln1_b, m_w_up, m_w_ffn_conv, m_b_ffn_conv, m_w_down, m_ln2_g, m_ln2_b, v_w_in, v_b_gates, v_w_sc_conv, v_mh_gain, v_w_out, v_ln1_g, v_ln1_b, v_w_up, v_w_ffn_conv, v_b_ffn_conv, v_w_down, v_ln2_g, v_ln2_b):
    w = dict(zip(WEIGHTS, (w_in, b_gates, w_sc_conv, mh_gain, w_out, ln1_g, ln1_b, w_up, w_ffn_conv, b_ffn_conv,
                           w_down, ln2_g, ln2_b)))
    m = dict(zip(WEIGHTS, (m_w_in, m_b_gates, m_w_sc_conv, m_mh_gain, m_w_out, m_ln1_g, m_ln1_b, m_w_up,
                           m_w_ffn_conv, m_b_ffn_conv, m_w_down, m_ln2_g, m_ln2_b)))
    v = dict(zip(WEIGHTS, (v_w_in, v_b_gates, v_w_sc_conv, v_mh_gain, v_w_out, v_ln1_g, v_ln1_b, v_w_up,
                           v_w_ffn_conv, v_b_ffn_conv, v_w_down, v_ln2_g, v_ln2_b)))
    me = _index(_place())
    d = x.shape[2]
    ws_in = w_in.shape[2]
    assert ws_in == IN_SLAB + 1 and N_DEV <= LANE, w_in.shape
    ninp = (N_DEV + 1) * IN_SLAB
    ws_sc, ws_fc = w_sc_conv.shape[2], w_ffn_conv.shape[2]

    w_in_shift = _shift_w_in(jnp.pad(w_in[0], ((0, 0), (0, IN_SLAB + LANE - ws_in))))
    taps8 = lambda a: jnp.pad(a[0], ((0, 5), (0, 0)))
    g_in, g_sc, g_fc = _all_gather([w_in_shift, taps8(w_sc_conv), taps8(w_ffn_conv)], "gather_w_in")
    w_in_full = _assemble_w_in(g_in, ninp)
    w_sc_full = g_sc[:, :3].transpose(1, 0, 2).reshape(3, N_DEV * ws_sc)
    w_fc_full = g_fc[:, :3].transpose(1, 0, 2).reshape(3, N_DEV * ws_fc)
    wx = _Gathering({n: w[n][0].astype(BF16) for n in ("w_out", "w_up", "w_down")}, me)

    xi, yi, ci = _place()
    gx = _Reducing(jnp.reshape(ci, (1,)).astype(jnp.int32), 2 * xi + yi)
    loss_t, grad_x, small, _ = _local_step(
        x[0], loss_target[0], w_in_full, b_gates, w_sc_full, mh_gain, None, ln1_g, ln1_b, None,
        w_fc_full, b_ffn_conv, None, ln2_g, ln2_b, gx=gx, wx=wx)

    landed = {name: gx.finish(name, grad_x) for name in ("w_down", "w_up", "w_out", "w_in")}
    r_in, r_out, r_up, r_down = landed["w_in"], landed["w_out"], landed["w_up"], landed["w_down"]
    grads, deltas, new_m, new_v = {}, {}, {}, {}
    grads["w_in"] = _sum_slots(r_in, "sum_w_in", unshift=True)[:, :ws_in]
    deltas["w_in"], new_m["w_in"], new_v["w_in"] = _adamw(w_in[0], grads["w_in"], m_w_in[0], v_w_in[0], "adamw_w_in")
    for name, r in (("w_out", r_out), ("w_up", r_up), ("w_down", r_down)):
        grads[name], deltas[name], new_m[name], new_v[name] = _sum_adamw(r, w[name][0], m[name][0], v[name][0],
                                                                         "adamw_" + name)

    names = ("loss",) + SMALL
    parts = dict(small, loss=loss_t[0, :1])
    sizes = [_lanes(parts[n].size) for n in names]
    (g_small,) = _all_gather([_pack([parts[n] for n in names], sizes)], "gather_small")
    summed = _unpack(_sum_slots(g_small, "sum_small", tr=g_small.shape[1]), [parts[n].shape for n in names], sizes)
    full = dict(zip(names, summed))
    full["w_sc_conv"] = lax.dynamic_slice(full["w_sc_conv"], (0, me * ws_sc), (3, ws_sc))
    full["w_ffn_conv"] = lax.dynamic_slice(full["w_ffn_conv"], (0, me * ws_fc), (3, ws_fc))
    for n in SMALL:
        grads[n] = full[n].reshape(w[n].shape)
    sizes = [_lanes(w[n].size) for n in SMALL]
    shapes = [w[n].shape for n in SMALL]
    packed = [_pack([t[n] for n in SMALL], sizes) for t in (w, grads, m, v)]
    for res, t in zip(_adamw(*packed, "adamw_small"), (deltas, new_m, new_v)):
        t.update(zip(SMALL, _unpack(res, shapes, sizes)))

    big = lambda t: {n: (t[n].reshape(w[n].shape) if n in BIG else t[n]) for n in WEIGHTS}
    grads, deltas, new_m, new_v = big(grads), big(deltas), big(new_m), big(new_v)
    return (full["loss"].reshape(()), grad_x[None], *[grads[n] for n in WEIGHTS], *[deltas[n] for n in WEIGHTS],
            *[new_m[n] for n in WEIGHTS], *[new_v[n] for n in WEIGHTS])
```

```python
import functools

import jax
import jax.numpy as jnp
from jax import lax
from jax.experimental import pallas as pl
from jax.experimental.pallas import tpu as pltpu

F32 = jnp.float32
BF16 = jnp.bfloat16
MESH = pl.DeviceIdType.MESH

N_DEV = 8
NH = 4
CHUNK = 64
LN_EPS = 1e-5
HN_EPS = 1e-6
ALPHA = 2.0 ** 0.25
LANE = 128
IN_SLAB = 7 * LANE
VMEM_LIMIT = 56 * 1024 * 1024
ADAM_LR, ADAM_B1, ADAM_B2, ADAM_EPS, ADAM_WD, ADAM_STEP = 0.001, 0.9, 0.999, 1e-08, 0.01, 10

_NN = (((1,), (0,)), ((), ()))
_NT = (((1,), (1,)), ((), ()))
_TN = (((0,), (0,)), ((), ()))


def _dot(a, b, dn=_NN):
    return lax.dot_general(a, b, dn, preferred_element_type=F32)


def _params(*sem):
    return pltpu.CompilerParams(dimension_semantics=sem if sem else None, vmem_limit_bytes=VMEM_LIMIT)


def _iota(shape, axis):
    return lax.broadcasted_iota(jnp.int32, shape, axis)


def _fit(n, want):
    if n <= want:
        return n
    t = want - want % LANE
    while n % t:
        t -= LANE
    return t


def _matmul(a, b, mode, out_dtype, name, tm=1024, tn=512, tk=1024, add=None, add_scale=1.0,
            b_blocked=False, o_width=None, after=None):
    if mode == "tn":
        kd, m = a.shape
    else:
        m, kd = a.shape
    if b_blocked:
        nb, rows, w = b.shape
        n = nb * w if mode == "nn" else rows
        assert (nb * w if mode == "nt" else rows) == kd, (name, b.shape, kd)
    else:
        n = b.shape[0] if mode == "nt" else b.shape[1]
    tm, tn, tk = _fit(m, tm), _fit(n, tn), _fit(kd, tk)
    if b_blocked and mode == "nn":
        tn = _fit(w, tn)
    if b_blocked and mode == "nt":
        tk = _fit(w, tk)
    if o_width is not None:
        tn = _fit(o_width, tn)
    assert m % tm == 0 and n % tn == 0 and kd % tk == 0, (name, m, n, kd, tm, tn, tk)
    nk = kd // tk
    dn = {"nn": _NN, "nt": _NT, "tn": _TN}[mode]
    a_spec = (pl.BlockSpec((tk, tm), lambda i, j, k: (k, i)) if mode == "tn"
              else pl.BlockSpec((tm, tk), lambda i, j, k: (i, k)))
    if b_blocked and mode == "nn":
        per = w // tn
        b_spec = pl.BlockSpec((None, tk, tn), lambda i, j, k: (j // per, k, j % per))
    elif b_blocked:
        per = w // tk
        b_spec = pl.BlockSpec((None, tn, tk), lambda i, j, k: (k // per, j, k % per))
    elif mode == "nt":
        b_spec = pl.BlockSpec((tn, tk), lambda i, j, k: (j, k))
    else:
        b_spec = pl.BlockSpec((tk, tn), lambda i, j, k: (k, j))
    if o_width is None:
        o_spec = pl.BlockSpec((tm, tn), lambda i, j, k: (i, j))
        o_shape = (m, n)
    else:
        oper = o_width // tn
        o_spec = pl.BlockSpec((None, tm, tn), lambda i, j, k: (j // oper, i, j % oper))
        o_shape = (n // o_width, m, o_width)
    has_add = add is not None
    n_in = 2 + has_add + (after is not None)
    in_place = nk > 1 and out_dtype == F32

    def body(*refs):
        a_ref, b_ref = refs[:2]
        add_ref = refs[2] if has_add else None
        o_ref = refs[n_in]

        def finish(r):
            if has_add:
                r = r + add_scale * add_ref[...]
            o_ref[...] = r.astype(out_dtype)

        if nk == 1:
            finish(_dot(a_ref[...], b_ref[...], dn))
        else:
            acc = o_ref if in_place else refs[-1]
            k = pl.program_id(2)

            @pl.when(k == 0)
            def _():
                acc[...] = _dot(a_ref[...], b_ref[...], dn)

            @pl.when(k > 0)
            def _():
                acc[...] += _dot(a_ref[...], b_ref[...], dn)

            if not (in_place and not has_add):
                @pl.when(k == nk - 1)
                def _():
                    finish(acc[...])

    in_specs = [a_spec, b_spec] + ([pl.BlockSpec((tm, tn), lambda i, j, k: (i, j))] if has_add else [])
    args = (a, b) + ((add,) if has_add else ())
    if after is not None:
        in_specs.append(pl.BlockSpec(memory_space=pl.ANY))
        args += (after,)
    return pl.pallas_call(
        body, name=name, grid=(m // tm, n // tn, nk),
        in_specs=in_specs, out_specs=o_spec,
        out_shape=jax.ShapeDtypeStruct(o_shape, out_dtype),
        scratch_shapes=[pltpu.VMEM((tm, tn), F32)] if nk > 1 and not in_place else [],
        compiler_params=_params("parallel", "parallel", "arbitrary"),
    )(*args)


def _shift_down(u, s):
    return jnp.where(_iota(u.shape, 0) >= s, pltpu.roll(u, s, 0), 0.0)


def _shift_up(u, s):
    t = u.shape[0]
    return jnp.where(_iota(u.shape, 0) < t - s, pltpu.roll(u, t - s, 0), 0.0)


SLAB = 8


def _rolled(u):
    return pltpu.roll(u, 2, 0), pltpu.roll(u, 1, 0)


def _conv(u, w, rolled=None):
    u2, u1 = _rolled(u) if rolled is None else rolled
    raw = w[0:1] * u2 + w[1:2] * u1 + w[2:3] * u
    head = u[0:SLAB]
    mended = w[0:1] * _shift_down(head, 2) + w[1:2] * _shift_down(head, 1) + w[2:3] * head
    return jnp.concatenate([mended, raw[SLAB:]], axis=0)


def _conv_t(dy, w):
    t = dy.shape[0]
    raw = w[2:3] * dy + w[1:2] * pltpu.roll(dy, t - 1, 0) + w[0:1] * pltpu.roll(dy, t - 2, 0)
    tail = dy[t - SLAB:]
    mended = w[2:3] * tail + w[1:2] * _shift_up(tail, 1) + w[0:1] * _shift_up(tail, 2)
    return jnp.concatenate([raw[:t - SLAB], mended], axis=0)


def _conv_dw(dy, u, rolled=None):
    t = dy.shape[0]
    u2, u1 = _rolled(u) if rolled is None else rolled
    head, tail = dy[0:SLAB], u[t - SLAB:]
    r = _iota(head.shape, 0)
    wrap2 = jnp.sum(jnp.where(r < 2, head * pltpu.roll(tail, 2, 0), 0.0), axis=0, keepdims=True)
    wrap1 = jnp.sum(jnp.where(r < 1, head * pltpu.roll(tail, 1, 0), 0.0), axis=0, keepdims=True)
    d0 = jnp.sum(dy * u2, axis=0, keepdims=True) - wrap2
    d1 = jnp.sum(dy * u1, axis=0, keepdims=True) - wrap1
    d2 = jnp.sum(dy * u, axis=0, keepdims=True)
    r3 = _iota((3, dy.shape[1]), 0)
    return jnp.where(r3 == 0, d0, jnp.where(r3 == 1, d1, d2))


def _sigmoid(x):
    return 0.5 * jnp.tanh(0.5 * x) + 0.5


def _sconv_fwd(proj, w_sc, t, wc):
    nb = wc // LANE

    def body(cb_ref, cc_ref, ch_ref, w_ref, y_ref):
        u = cc_ref[...] * ch_ref[...]
        y_ref[...] = (cb_ref[...] * _conv(u, w_ref[...])).astype(BF16)

    col = lambda off: pl.BlockSpec((t, LANE), lambda j: (0, j + off))
    return pl.pallas_call(
        body, name="sconv_fwd", grid=(nb,),
        in_specs=[col(0), col(nb), col(2 * nb), pl.BlockSpec((3, LANE), lambda j: (0, j))],
        out_specs=pl.BlockSpec((t, LANE), lambda j: (0, j)),
        out_shape=jax.ShapeDtypeStruct((t, wc), BF16),
        compiler_params=_params("parallel"),
    )(proj, proj, proj, w_sc)


def _sconv_bwd(dy, proj, w_sc, t, wc):
    nb = wc // LANE

    def body(dy_ref, cb_ref, cc_ref, ch_ref, w_ref, dcb_ref, dcc_ref, dch_ref, dw_ref):
        cc, ch, w, d = cc_ref[...], ch_ref[...], w_ref[...], dy_ref[...]
        u = cc * ch
        ru = _rolled(u)
        dcb_ref[...] = (d * _conv(u, w, ru)).astype(BF16)
        dcu = d * cb_ref[...]
        dw_ref[...] = _conv_dw(dcu, u, ru)
        du = _conv_t(dcu, w)
        dcc_ref[...] = (du * ch).astype(BF16)
        dch_ref[...] = (du * cc).astype(BF16)

    col = lambda off: pl.BlockSpec((t, LANE), lambda j: (0, j + off))
    act = jax.ShapeDtypeStruct((t, wc), BF16)
    return pl.pallas_call(
        body, name="sconv_bwd", grid=(nb,),
        in_specs=[col(0), col(0), col(nb), col(2 * nb), pl.BlockSpec((3, LANE), lambda j: (0, j))],
        out_specs=[col(0), col(0), col(0), pl.BlockSpec((3, LANE), lambda j: (0, j))],
        out_shape=[act, act, act, jax.ShapeDtypeStruct((3, wc), F32)],
        compiler_params=_params("parallel"),
    )(dy, proj, proj, proj, w_sc)


def _gates_prep(proj, bias_tile, t, gate_tile):
    def body(g_ref, b_ref, o_ref):
        g = g_ref[...] + b_ref[...]
        lane = _iota(g.shape, 1)
        is_f = (lane >= NH) & (lane < 2 * NH)
        lf = jnp.minimum(g, 0.0) - jnp.log(1.0 + jnp.exp(-jnp.abs(g)))
        c = jnp.where(is_f, lf, 0.0)
        r = _iota(g.shape, 0) % CHUNK
        s = 1
        while s < CHUNK:
            c = c + jnp.where(r >= s, pltpu.roll(c, s, 0), 0.0)
            s *= 2
        o_ref[...] = jnp.where(is_f, c, jnp.where(lane < NH, g, 0.0))

    return pl.pallas_call(
        body, name="gates_prep", grid=(1,),
        in_specs=[pl.BlockSpec((t, LANE), lambda i: (0, gate_tile)), pl.BlockSpec((1, LANE), lambda i: (0, 0))],
        out_specs=pl.BlockSpec((t, LANE), lambda i: (0, 0)),
        out_shape=jax.ShapeDtypeStruct((t, LANE), F32),
        compiler_params=_params("arbitrary"),
    )(proj, bias_tile)


def _gates_bwd(dgate, proj, bias_tile, t, gate_tile):
    def body(dg_ref, g_ref, b_ref, o_ref, s_ref):
        g = g_ref[...] + b_ref[...]
        lane = _iota(g.shape, 1)
        r = _iota(g.shape, 0) % CHUNK
        dsig = 1.0 - _sigmoid(g)
        out = jnp.zeros(g.shape, F32)
        for h in range(NH):
            d = dg_ref[h]
            c = d
            s = 1
            while s < CHUNK:
                c = c + jnp.where(r + s < CHUNK, pltpu.roll(c, t - s, 0), 0.0)
                s *= 2
            di = jnp.broadcast_to(d[:, 0:1], g.shape)
            db = jnp.broadcast_to(c[:, 1:2], g.shape)
            out = out + jnp.where(lane == h, di, 0.0) + jnp.where(lane == NH + h, db * dsig, 0.0)
        o_ref[...] = out.astype(BF16)
        s_ref[...] = jnp.sum(out, axis=0, keepdims=True)

    return pl.pallas_call(
        body, name="gates_bwd", grid=(1,),
        in_specs=[pl.BlockSpec((NH, t, LANE), lambda i: (0, 0, 0)),
                  pl.BlockSpec((t, LANE), lambda i: (0, gate_tile)), pl.BlockSpec((1, LANE), lambda i: (0, 0))],
        out_specs=[pl.BlockSpec((t, LANE), lambda i: (0, 0)), pl.BlockSpec((1, LANE), lambda i: (0, 0))],
        out_shape=[jax.ShapeDtypeStruct((t, LANE), BF16), jax.ShapeDtypeStruct((1, LANE), F32)],
        compiler_params=_params("arbitrary"),
    )(dgate, proj, bias_tile)


def _chunk_gates(gc, gr, h, mprev):
    L = CHUNK
    icol, bcol = gc[:, h:h + 1], gc[:, h + NH:h + NH + 1]
    irow, brow = gr[h:h + 1, :], gr[h + NH:h + NH + 1, :]
    tri = _iota((L, L), 0) >= _iota((L, L), 1)
    log_d = jnp.where(tri, bcol - brow + irow, -jnp.inf)
    inter = bcol + mprev
    mt = jnp.maximum(inter, jnp.max(log_d, axis=1, keepdims=True))
    dw = jnp.exp(log_d - mt)
    iw = jnp.exp(inter - mt)
    g = brow[:, L - 1:L]
    wlog_col = g - bcol + icol
    wlog_row = g - brow + irow
    mnew = jnp.maximum(g + mprev, jnp.max(wlog_row, axis=1, keepdims=True))
    wcol = jnp.exp(wlog_col - mnew)
    decay = jnp.exp(g + mprev - mnew)
    return dw, iw, mt, wcol, decay, mnew


def _mlstm_fwd(proj, gcol, grow, t, wc, dh):
    nc = t // CHUNK
    wm = NH * dh
    assert wc == wm, (wc, wm)
    qoff = 3 * wc // wm
    scale = dh ** -0.5

    def body(q_ref, k_ref, v_ref, gc_ref, gr_ref, h_ref, cs_ref, ns_ref, c_s, n_s, m_s):
        @pl.when(pl.program_id(0) == 0)
        def _():
            c_s[...] = jnp.zeros_like(c_s)
            n_s[...] = jnp.zeros_like(n_s)
            m_s[...] = jnp.zeros_like(m_s)

        gc, gr = gc_ref[...], gr_ref[0]
        for h in range(NH):
            cols = slice(h * dh, (h + 1) * dh)
            mprev = m_s[h, 0:1, 0:1]
            cprev = c_s[h]
            n8 = n_s[h]
            nprev = n8[0:1]
            cs_ref[h] = cprev
            ns_ref[h] = jnp.where(_iota(n8.shape, 0) == 1, mprev, n8)

            dw, iw, mt, wcol, decay, mnew = _chunk_gates(gc, gr, h, mprev)
            qs = q_ref[:, cols] * scale
            k = k_ref[:, cols]
            qs_b, k_b, v_b = qs.astype(BF16), k.astype(BF16), v_ref[:, cols].astype(BF16)
            s = _dot(qs_b, k_b, _NT) * dw
            num = _dot(s.astype(BF16), v_b) + iw * _dot(qs_b, cprev.astype(BF16))
            den = jnp.sum(s, axis=1, keepdims=True) + iw * jnp.sum(qs * nprev, axis=1, keepdims=True)
            h_ref[:, cols] = num / jnp.maximum(jnp.abs(den), jnp.exp(-mt))

            wk = wcol * k
            c_s[h] = decay * cprev + _dot(wk.astype(BF16), v_b, _TN)
            n_s[h] = decay * n8 + jnp.sum(wk, axis=0, keepdims=True)
            m_s[h] = jnp.broadcast_to(mnew, m_s.shape[1:])

    grp = lambda off: pl.BlockSpec((CHUNK, wm), lambda c: (c, qoff + off))
    return pl.pallas_call(
        body, name="mlstm_fwd", grid=(nc,),
        in_specs=[grp(0), grp(1), grp(2),
                  pl.BlockSpec((CHUNK, LANE), lambda c: (c, 0)),
                  pl.BlockSpec((1, 8, CHUNK), lambda c: (c, 0, 0))],
        out_specs=[pl.BlockSpec((CHUNK, wm), lambda c: (c, 0)),
                   pl.BlockSpec((NH, None, dh, dh), lambda c: (0, c, 0, 0)),
                   pl.BlockSpec((NH, None, 8, dh), lambda c: (0, c, 0, 0))],
        out_shape=[jax.ShapeDtypeStruct((t, wm), F32),
                   jax.ShapeDtypeStruct((NH, nc, dh, dh), F32),
                   jax.ShapeDtypeStruct((NH, nc, 8, dh), F32)],
        scratch_shapes=[pltpu.VMEM((NH, dh, dh), F32), pltpu.VMEM((NH, 8, dh), F32), pltpu.VMEM((NH, 8, LANE), F32)],
        compiler_params=_params("arbitrary"),
    )(proj, proj, proj, gcol, grow)


def _mlstm_bwd(proj, gcol, grow, hval, dh_in, cs, ns, t, wc, dh):
    nc = t // CHUNK
    wm = NH * dh
    assert wc == wm, (wc, wm)
    qoff = 3 * wc // wm
    scale = dh ** -0.5
    L = CHUNK

    def body(q_ref, k_ref, v_ref, gc_ref, gr_ref, h_ref, dh_ref, cs_ref, ns_ref,
             dq_ref, dk_ref, dv_ref, dg_ref, dc_s, dn_s):
        @pl.when(pl.program_id(0) == 0)
        def _():
            dc_s[...] = jnp.zeros_like(dc_s)
            dn_s[...] = jnp.zeros_like(dn_s)

        gc, gr = gc_ref[...], gr_ref[0]
        eye = _iota((L, L), 0) == _iota((L, L), 1)
        lane = _iota((L, LANE), 1)
        last = _iota((L, 1), 0) == L - 1
        for h in range(NH):
            cols = slice(h * dh, (h + 1) * dh)
            ns8 = ns_ref[h]
            nprev = ns8[0:1]
            mprev = ns8[1:2, 0:1]
            cprev = cs_ref[h]
            dcn = dc_s[h]
            dn8 = dn_s[h]
            dnn = dn8[0:1]

            dw, iw, mt, wcol, decay, _ = _chunk_gates(gc, gr, h, mprev)
            qs = q_ref[:, cols] * scale
            k = k_ref[:, cols]
            qs_b, k_b, v_b = qs.astype(BF16), k.astype(BF16), v_ref[:, cols].astype(BF16)
            qk = _dot(qs_b, k_b, _NT)
            s = qk * dw
            den = jnp.sum(s, axis=1, keepdims=True) + iw * jnp.sum(qs * nprev, axis=1, keepdims=True)
            emt = jnp.exp(-mt)
            r = 1.0 / jnp.maximum(jnp.abs(den), emt)
            dout = dh_ref[:, cols]
            dnum = dout * r
            dden = (-jnp.sum(dout * h_ref[:, cols], axis=1, keepdims=True) * r
                    * jnp.where(jnp.abs(den) > emt, jnp.sign(den), 0.0))
            dnum_b = dnum.astype(BF16)
            cprev_b = cprev.astype(BF16)
            dcn_b = dcn.astype(BF16)

            gd = (_dot(dnum_b, v_b, _NT) + dden) * dw
            gd_b = gd.astype(BF16)
            dqs_inter = iw * (_dot(dnum_b, cprev_b, _NT) + dden * nprev)
            dqs = _dot(gd_b, k_b) + dqs_inter
            dk_inter = wcol * (_dot(v_b, dcn_b, _NT) + dnn)
            dk = _dot(gd_b, qs_b, _TN) + dk_inter
            wk = wcol * k
            dv = _dot(s.astype(BF16), dnum_b, _TN) + _dot(wk.astype(BF16), dcn_b)

            e = gd * qk
            e_cols = jnp.sum(jnp.where(eye, jnp.sum(e, axis=0, keepdims=True), 0.0), axis=1, keepdims=True)
            k_inter = jnp.sum(k * dk_inter, axis=1, keepdims=True)
            rq = jnp.sum(e, axis=1, keepdims=True) + jnp.sum(qs * dqs_inter, axis=1, keepdims=True)
            rk = e_cols + k_inter
            hsum = jnp.sum(k_inter, axis=0, keepdims=True)
            jdec = decay * (jnp.sum(jnp.sum(dcn * cprev, axis=1, keepdims=True), axis=0, keepdims=True)
                            + jnp.sum(dnn * nprev, axis=1, keepdims=True))
            db = rq - rk + jnp.where(last, hsum + jdec, 0.0)
            dg_ref[h] = jnp.where(lane == 0, rk, jnp.where(lane == 1, db, 0.0))

            dq_ref[:, cols] = (dqs * scale).astype(BF16)
            dk_ref[:, cols] = dk.astype(BF16)
            dv_ref[:, cols] = dv.astype(BF16)

            iq = iw * qs
            dc_s[h] = decay * dcn + _dot(iq.astype(BF16), dnum_b, _TN)
            dn_s[h] = decay * dn8 + jnp.sum(iq * dden, axis=0, keepdims=True)

    rc = lambda c: nc - 1 - c
    grp = lambda off: pl.BlockSpec((L, wm), lambda c: (rc(c), qoff + off))
    hm = pl.BlockSpec((L, wm), lambda c: (rc(c), 0))
    act = jax.ShapeDtypeStruct((t, wm), BF16)
    return pl.pallas_call(
        body, name="mlstm_bwd", grid=(nc,),
        in_specs=[grp(0), grp(1), grp(2),
                  pl.BlockSpec((L, LANE), lambda c: (rc(c), 0)),
                  pl.BlockSpec((1, 8, L), lambda c: (rc(c), 0, 0)),
                  hm, hm,
                  pl.BlockSpec((NH, None, dh, dh), lambda c: (0, rc(c), 0, 0)),
                  pl.BlockSpec((NH, None, 8, dh), lambda c: (0, rc(c), 0, 0))],
        out_specs=[hm, hm, hm, pl.BlockSpec((NH, L, LANE), lambda c: (0, rc(c), 0))],
        out_shape=[act, act, act, jax.ShapeDtypeStruct((NH, t, LANE), F32)],
        scratch_shapes=[pltpu.VMEM((NH, dh, dh), F32), pltpu.VMEM((NH, 8, dh), F32)],
        compiler_params=_params("arbitrary"),
    )(proj, proj, proj, gcol, grow, hval, dh_in, cs, ns)


def _head_norm(hv):
    mu = jnp.mean(hv, axis=1, keepdims=True)
    hc = hv - mu
    rstd = lax.rsqrt(jnp.mean(hc * hc, axis=1, keepdims=True) + HN_EPS)
    return hc * rstd, rstd


def _hnorm_fwd(hval, proj, gain, t, wc, dh, tr=256):
    ooff = 3 * wc // dh + 3 * NH

    def body(h_ref, o_ref, g_ref, y_ref):
        hhat, _ = _head_norm(h_ref[...])
        y_ref[...] = (_sigmoid(o_ref[...]) * hhat * g_ref[...]).astype(BF16)

    return pl.pallas_call(
        body, name="hnorm_fwd", grid=(t // tr, NH),
        in_specs=[pl.BlockSpec((tr, dh), lambda i, h: (i, h)),
                  pl.BlockSpec((tr, dh), lambda i, h: (i, ooff + h)),
                  pl.BlockSpec((1, dh), lambda i, h: (0, h))],
        out_specs=pl.BlockSpec((tr, dh), lambda i, h: (i, h)),
        out_shape=jax.ShapeDtypeStruct((t, NH * dh), BF16),
        compiler_params=_params("parallel", "parallel"),
    )(hval, proj, gain)


def _hnorm_bwd(dy, hval, proj, gain, t, wc, dh, tr=256):
    ooff = 3 * wc // dh + 3 * NH
    yoff = wc // dh

    def body(dy_ref, h_ref, o_ref, g_ref, do_ref, dh_ref, dg_ref):
        i = pl.program_id(1)
        hhat, rstd = _head_norm(h_ref[...])
        gain_v = g_ref[...]
        sig = _sigmoid(o_ref[...])
        d = dy_ref[...]
        do_ref[...] = (d * hhat * gain_v * sig * (1.0 - sig)).astype(BF16)
        dhn = d * sig
        part = jnp.sum(dhn * hhat, axis=0, keepdims=True)

        @pl.when(i == 0)
        def _():
            dg_ref[...] = part

        @pl.when(i > 0)
        def _():
            dg_ref[...] += part

        dhat = dhn * gain_v
        dh_ref[...] = rstd * (dhat - jnp.mean(dhat, axis=1, keepdims=True)
                              - hhat * jnp.mean(dhat * hhat, axis=1, keepdims=True))

    blk = lambda off: pl.BlockSpec((tr, dh), lambda h, i: (i, off + h))
    return pl.pallas_call(
        body, name="hnorm_bwd", grid=(NH, t // tr),
        in_specs=[blk(yoff), blk(0), blk(ooff), pl.BlockSpec((1, dh), lambda h, i: (0, h))],
        out_specs=[blk(0), blk(0), pl.BlockSpec((1, dh), lambda h, i: (0, h))],
        out_shape=[jax.ShapeDtypeStruct((t, NH * dh), BF16), jax.ShapeDtypeStruct((t, NH * dh), F32),
                   jax.ShapeDtypeStruct((1, NH * dh), F32)],
        compiler_params=_params("parallel", "arbitrary"),
    )(dy, hval, proj, gain)


def _ln_stats(z):
    mu = jnp.mean(z, axis=1, keepdims=True)
    zc = z - mu
    rstd = lax.rsqrt(jnp.mean(zc * zc, axis=1, keepdims=True) + LN_EPS)
    return zc * rstd, rstd


def _ln_bwd(dy, xhat, rstd, g):
    dxh = dy * g
    return rstd * (dxh - jnp.mean(dxh, axis=1, keepdims=True) - xhat * jnp.mean(dxh * xhat, axis=1, keepdims=True))


def _accum(ref, i, part):
    @pl.when(i == 0)
    def _():
        ref[...] = part

    @pl.when(i > 0)
    def _():
        ref[...] += part


def _ln1_fwd(x, mix, g, b, tr=256):
    t, d = x.shape

    def body(x_ref, m_ref, g_ref, b_ref, xh_ref, rs_ref, xb_ref):
        xhat, rstd = _ln_stats(ALPHA * x_ref[...] + m_ref[...])
        xh_ref[...] = xhat
        rs_ref[...] = rstd
        xb_ref[...] = (xhat * g_ref[...] + b_ref[...]).astype(BF16)

    row = pl.BlockSpec((tr, d), lambda i: (i, 0))
    vec = pl.BlockSpec((1, d), lambda i: (0, 0))
    return pl.pallas_call(
        body, name="ln1_fwd", grid=(t // tr,),
        in_specs=[row, row, vec, vec],
        out_specs=[row, pl.BlockSpec((tr, 1), lambda i: (i, 0)), row],
        out_shape=[jax.ShapeDtypeStruct((t, d), F32), jax.ShapeDtypeStruct((t, 1), F32),
                   jax.ShapeDtypeStruct((t, d), BF16)],
        compiler_params=_params("parallel"),
    )(x, mix, g, b)


def _ln2_loss(xhat1, g1, b1, ff, target, g2, b2, tr=256):
    t, d = ff.shape

    def body(xh_ref, g1_ref, b1_ref, f_ref, t_ref, g_ref, b_ref, dz_ref, dzb_ref, dg_ref, db_ref, l_ref):
        i = pl.program_id(0)
        x1 = xh_ref[...] * g1_ref[...] + b1_ref[...]
        xhat, rstd = _ln_stats(ALPHA * x1 + f_ref[...])
        gv = g_ref[...]
        e = xhat * gv + b_ref[...] - t_ref[...]
        lsum = jnp.sum(jnp.sum(e * e, axis=1, keepdims=True), axis=0, keepdims=True) * (0.5 / d)
        dy = e * (1.0 / d)
        _accum(dg_ref, i, jnp.sum(dy * xhat, axis=0, keepdims=True))
        _accum(db_ref, i, jnp.sum(dy, axis=0, keepdims=True))
        _accum(l_ref, i, jnp.broadcast_to(lsum, l_ref.shape))
        dz = _ln_bwd(dy, xhat, rstd, gv)
        dz_ref[...] = dz
        dzb_ref[...] = dz.astype(BF16)

    row = pl.BlockSpec((tr, d), lambda i: (i, 0))
    vec = pl.BlockSpec((1, d), lambda i: (0, 0))
    return pl.pallas_call(
        body, name="ln2_loss", grid=(t // tr,),
        in_specs=[row, vec, vec, row, row, vec, vec],
        out_specs=[row, row, vec, vec, pl.BlockSpec((8, LANE), lambda i: (0, 0))],
        out_shape=[jax.ShapeDtypeStruct((t, d), F32), jax.ShapeDtypeStruct((t, d), BF16),
                   jax.ShapeDtypeStruct((1, d), F32), jax.ShapeDtypeStruct((1, d), F32),
                   jax.ShapeDtypeStruct((8, LANE), F32)],
        compiler_params=_params("arbitrary"),
    )(xhat1, g1, b1, ff, target, g2, b2)


def _ln1_bwd(dz2, dffn, xhat1, rstd1, g1, tr=256):
    t, d = dz2.shape

    def body(a_ref, f_ref, xh_ref, rs_ref, g_ref, dz_ref, dzb_ref, dg_ref, db_ref):
        i = pl.program_id(0)
        dy = ALPHA * a_ref[...] + f_ref[...]
        xhat = xh_ref[...]
        _accum(dg_ref, i, jnp.sum(dy * xhat, axis=0, keepdims=True))
        _accum(db_ref, i, jnp.sum(dy, axis=0, keepdims=True))
        dz = _ln_bwd(dy, xhat, rs_ref[...], g_ref[...])
        dz_ref[...] = dz
        dzb_ref[...] = dz.astype(BF16)

    row = pl.BlockSpec((tr, d), lambda i: (i, 0))
    vec = pl.BlockSpec((1, d), lambda i: (0, 0))
    return pl.pallas_call(
        body, name="ln1_bwd", grid=(t // tr,),
        in_specs=[row, row, row, pl.BlockSpec((tr, 1), lambda i: (i, 0)), vec],
        out_specs=[row, row, vec, vec],
        out_shape=[jax.ShapeDtypeStruct((t, d), F32), jax.ShapeDtypeStruct((t, d), BF16),
                   jax.ShapeDtypeStruct((1, d), F32), jax.ShapeDtypeStruct((1, d), F32)],
        compiler_params=_params("arbitrary"),
    )(dz2, dffn, xhat1, rstd1, g1)


def _ffn_act_fwd(hid0, w_fc, b_fc, t, dff):
    nb = dff // LANE

    def body(hv_ref, hg_ref, wv_ref, wg_ref, bv_ref, bg_ref, a_ref):
        val = _conv(hv_ref[...], wv_ref[...]) + bv_ref[...]
        gate = _conv(hg_ref[...], wg_ref[...]) + bg_ref[...]
        a_ref[...] = (gate * _sigmoid(gate) * val).astype(BF16)

    col = lambda off: pl.BlockSpec((t, LANE), lambda j: (0, j + off))
    w3 = lambda off: pl.BlockSpec((3, LANE), lambda j: (0, j + off))
    w1 = lambda off: pl.BlockSpec((1, LANE), lambda j: (0, j + off))
    return pl.pallas_call(
        body, name="ffn_act_fwd", grid=(nb,),
        in_specs=[col(0), col(nb), w3(0), w3(nb), w1(0), w1(nb)],
        out_specs=col(0),
        out_shape=jax.ShapeDtypeStruct((t, dff), BF16),
        compiler_params=_params("parallel"),
    )(hid0, hid0, w_fc, w_fc, b_fc, b_fc)


def _ffn_act_bwd(da, hid0, w_fc, b_fc, t, dff):
    nb = dff // LANE

    def body(da_ref, hv_ref, hg_ref, wv_ref, wg_ref, bv_ref, bg_ref,
             dhv_ref, dhg_ref, dwv_ref, dwg_ref, dbv_ref, dbg_ref):
        hv, hg, wv, wg = hv_ref[...], hg_ref[...], wv_ref[...], wg_ref[...]
        rv, rg = _rolled(hv), _rolled(hg)
        val = _conv(hv, wv, rv) + bv_ref[...]
        gate = _conv(hg, wg, rg) + bg_ref[...]
        sig = _sigmoid(gate)
        d = da_ref[...]
        dsig = d * sig
        dval = dsig * gate
        dgate = dsig * val * (1.0 + gate * (1.0 - sig))
        dhv_ref[...] = _conv_t(dval, wv).astype(BF16)
        dhg_ref[...] = _conv_t(dgate, wg).astype(BF16)
        dwv_ref[...] = _conv_dw(dval, hv, rv)
        dwg_ref[...] = _conv_dw(dgate, hg, rg)
        dbv_ref[...] = jnp.sum(dval, axis=0, keepdims=True)
        dbg_ref[...] = jnp.sum(dgate, axis=0, keepdims=True)

    col = lambda off: pl.BlockSpec((t, LANE), lambda j: (0, j + off))
    w3 = lambda off: pl.BlockSpec((3, LANE), lambda j: (0, j + off))
    w1 = lambda off: pl.BlockSpec((1, LANE), lambda j: (0, j + off))
    act = jax.ShapeDtypeStruct((t, dff), BF16)
    s3 = jax.ShapeDtypeStruct((3, dff), F32)
    s1 = jax.ShapeDtypeStruct((1, dff), F32)
    return pl.pallas_call(
        body, name="ffn_act_bwd", grid=(nb,),
        in_specs=[col(0), col(0), col(nb), w3(0), w3(nb), w1(0), w1(nb)],
        out_specs=[col(0), col(0), w3(0), w3(0), w1(0), w1(0)],
        out_shape=[act, act, s3, s3, s1, s1],
        compiler_params=_params("parallel"),
    )(da, hid0, hid0, w_fc, w_fc, b_fc, b_fc)


class _Ready:
    def __init__(self, **weights):
        self.weights = weights

    def begin(self, after):
        return None

    def forward(self, name, after):
        return None

    def get(self, name, after):
        return self.weights[name]


class _Kept:
    def __init__(self):
        self.grads = {}

    def start(self, name, grad):
        self.grads[name] = grad
        return None

    def relay(self, name, after):
        return None


def _behind(a, token):
    return a if token is None else a + token[0:1, 0:1].reshape((1,) * a.ndim)


def _local_step(x, target, w_in, b_gates, w_sc, gain, w_out, ln1_g, ln1_b, w_up, w_fc, b_fc, w_down, ln2_g, ln2_b,
                gx=None, wx=None):
    t, d = x.shape
    wc = d // 2
    dh = (d - wc) // NH
    wm = NH * dh
    dff = w_fc.shape[1] // 2
    if wx is None:
        wx = _Ready(w_out=w_out, w_up=w_up, w_down=w_down)
    ninp = w_in.shape[1]
    nin = 3 * wc + 4 * wm
    gate_tile = nin // LANE
    nc = t // CHUNK
    bias_tile = jnp.pad(b_gates, ((0, 0), (0, LANE - 2 * NH)))

    x_b = x.astype(BF16)
    proj = _matmul(x_b, w_in, "nn", F32, "proj", tm=1024, tn=1152, tk=d, after=wx.begin(w_in))
    y_conv = _sconv_fwd(proj, w_sc, t, wc)
    gcol = _gates_prep(proj, bias_tile, t, gate_tile)
    grow = gcol[:, :8].T.reshape(8, nc, CHUNK).transpose(1, 0, 2)
    hval, cs, ns = _mlstm_fwd(proj, gcol, grow, t, wc, dh)
    y_m = _hnorm_fwd(hval, proj, gain, t, wc, dh)
    y = jnp.concatenate([y_conv, y_m], axis=1)
    tok = wx.forward("w_up", wx.forward("w_out", y))
    w_out = wx.get("w_out", tok)
    mix = _matmul(y, w_out, "nn", F32, "out_proj", tm=512, tn=1024, tk=d, after=tok)
    xhat1, rstd1, x1_b = _ln1_fwd(x, mix, ln1_g, ln1_b)
    tok = wx.forward("w_down", x1_b)
    w_up = wx.get("w_up", tok)
    wsl = w_up.shape[2]
    hid0 = _matmul(x1_b, w_up, "nn", F32, "ffn_up", tm=512, tn=wsl, tk=d, b_blocked=True, after=tok)
    act = _ffn_act_fwd(hid0, w_fc, b_fc, t, dff)
    w_down = wx.get("w_down", act)
    ff = _matmul(act, w_down, "nn", F32, "ffn_down", tm=1024, tn=512, tk=dff)
    dz2, dz2_b, d_ln2_g, d_ln2_b, loss = _ln2_loss(xhat1, ln1_g, ln1_b, ff, target, ln2_g, ln2_b)

    if gx is None:
        gx = _Kept()
    d_w_down = _matmul(act, dz2_b, "tn", BF16, "ffn_down_dw", tm=512, tn=1024, tk=t)
    d_act = _matmul(dz2_b, w_down, "nt", F32, "ffn_down_dx", tm=1024, tn=512, tk=d, after=gx.start("w_down", d_w_down))
    dhv, dhg, dwv, dwg, dbv, dbg = _ffn_act_bwd(d_act, hid0, w_fc, _behind(b_fc, gx.relay("w_down", d_act)), t, dff)
    d_hid0 = jnp.concatenate([dhv, dhg], axis=1)
    d_w_fc = jnp.concatenate([dwv, dwg], axis=1)
    d_b_fc = jnp.concatenate([dbv, dbg], axis=1)
    d_w_up = _matmul(x1_b, d_hid0, "tn", BF16, "ffn_up_dw", tm=512, tn=wsl, tk=t, o_width=wsl)
    d_x1_ffn = _matmul(d_hid0, w_up, "nt", F32, "ffn_up_dx", tm=1024, tn=1024, tk=wsl, b_blocked=True,
                       after=gx.start("w_up", d_w_up))
    dz1, dz1_b, d_ln1_g, d_ln1_b = _ln1_bwd(dz2, d_x1_ffn, xhat1, rstd1, _behind(ln1_g, gx.relay("w_up", d_x1_ffn)))

    d_w_out = _matmul(y, dz1_b, "tn", BF16, "out_proj_dw", tm=512, tn=1024, tk=t)
    dy = _matmul(dz1_b, w_out, "nt", F32, "out_proj_dx", tm=512, tn=1024, tk=d, after=gx.start("w_out", d_w_out))
    dcb, dcc, dch, d_w_sc = _sconv_bwd(dy, proj, _behind(w_sc, gx.relay("w_out", dy)), t, wc)
    d_o, d_hval, d_gain = _hnorm_bwd(dy, hval, proj, gain, t, wc, dh)
    dq, dk, dv, dgate = _mlstm_bwd(proj, gcol, grow, hval, d_hval, cs, ns, t, wc, dh)
    dgt, d_b_gates = _gates_bwd(dgate, proj, bias_tile, t, gate_tile)
    pad = jnp.zeros((t, ninp - nin - LANE), BF16)
    d_proj = jnp.concatenate([dcb, dcc, dch, dq, dk, dv, d_o, dgt, pad], axis=1)
    d_w_in = _matmul(x_b, d_proj, "tn", BF16, "proj_dw", tm=512, tn=IN_SLAB, tk=t, o_width=IN_SLAB)
    grad_x = _matmul(d_proj, w_in, "nt", F32, "proj_dx", tm=512, tn=512, tk=ninp, add=dz1, add_scale=ALPHA,
                     after=gx.start("w_in", d_w_in))
    gx.relay("w_in", grad_x)

    small = dict(b_gates=d_b_gates[:, :2 * NH], w_sc_conv=d_w_sc, mh_gain=d_gain, ln1_g=d_ln1_g, ln1_b=d_ln1_b,
                 w_ffn_conv=d_w_fc, b_ffn_conv=d_b_fc, ln2_g=d_ln2_g, ln2_b=d_ln2_b)
    return loss, grad_x, small, gx


HBM = pl.BlockSpec(memory_space=pltpu.HBM)


def _place():
    return lax.axis_index("x"), lax.axis_index("y"), lax.axis_index("c")


def _index(p):
    return 4 * p[0] + 2 * p[1] + p[2]


def _all_gather(arrs, name):
    n = len(arrs)

    def body(*refs):
        ins, outs = refs[:n], refs[n:2 * n]
        send_sems, recv_sems, local_sems = refs[2 * n:]
        x, y, c = _place()
        me, sibling = (x, y, c), (x, y, 1 - c)
        chips = [(1 - x, y), (x, 1 - y), (1 - x, 1 - y)]

        def copy(a, k, block, to, own=False):
            dst = outs[a].at[_index(block)]
            return pltpu.make_async_remote_copy(
                src_ref=ins[a] if own else dst, dst_ref=dst,
                send_sem=send_sems.at[k * n + a], recv_sem=recv_sems.at[k * n + a],
                device_id=to, device_id_type=MESH)

        mine = [pltpu.make_async_copy(ins[a], outs[a].at[_index(me)], local_sems.at[a]) for a in range(n)]
        for cp in mine:
            cp.start()
        first = []
        for a in range(n):
            first.append(copy(a, 0, me, sibling, own=True))
            first += [copy(a, 1 + j, me, (*chip, c), own=True) for j, chip in enumerate(chips)]
        for cp in first:
            cp.start()
        passed = []
        for j, chip in enumerate(chips):
            for a in range(n):
                copy(a, 1 + j, (*chip, c), me).wait_recv()
                cp = copy(a, 4 + j, (*chip, c), sibling)
                cp.start()
                passed.append(cp)
        for a in range(n):
            copy(a, 0, sibling, me).wait_recv()
            for j, chip in enumerate(chips):
                copy(a, 4 + j, (*chip, 1 - c), me).wait_recv()
        for cp in first + passed:
            cp.wait_send()
        for cp in mine:
            cp.wait()

    return pl.pallas_call(
        body, name=name, in_specs=[HBM] * n, out_specs=[HBM] * n,
        out_shape=[jax.ShapeDtypeStruct((N_DEV,) + a.shape, a.dtype) for a in arrs],
        scratch_shapes=[pltpu.SemaphoreType.DMA((7 * n,)), pltpu.SemaphoreType.DMA((7 * n,)),
                        pltpu.SemaphoreType.DMA((n,))],
    )(*arrs)


SEM = pl.BlockSpec(memory_space=pltpu.SEMAPHORE)
EFFECT = pltpu.SideEffectType.DATAFLOW_SIDE_EFFECTING


def _chips(x, y):
    return [(1 - x, y), (x, 1 - y), (1 - x, 1 - y)]


N_CHIP = N_DEV // 2


def _pair_route(x, y, c):
    return [((x, y, 1 - c), 2 * q + (1 - c), q, q) for q in range(N_CHIP)]


def _chip_route(x, y, c):
    mine = 2 * x + y
    return [((*chip, c), 2 * chip[0] + chip[1], mine, 2 * chip[0] + chip[1]) for chip in _chips(x, y)]


def _exchange_pieces(g_ref, land_ref, width, tail):
    if not tail:
        return [(lambda i: g_ref.at[i], lambda s: land_ref.at[s])]
    return [(lambda i: g_ref.at[i], lambda s: land_ref.at[s, :, pl.ds(0, width)]),
            (lambda i: g_ref.at[i + 1, :, pl.ds(0, LANE)], lambda s: land_ref.at[s, :, pl.ds(width, LANE)])]


def _exchange_start(grad, route, tail, name):
    width = grad.shape[2]
    n_p = 2 if tail else 1
    n_c = len(route(0, 0, 0))
    land_shape = (N_CHIP, grad.shape[1], width + (LANE if tail else 0))

    def body(g_ref, land_ref, send_sems, recv_sems, g_thru, land_thru, token):
        for j, (peer, slab, slot, _) in enumerate(route(*_place())):
            for p, (src, dst) in enumerate(_exchange_pieces(g_ref, land_ref, width, tail)):
                pltpu.make_async_remote_copy(src_ref=src(slab), dst_ref=dst(slot), send_sem=send_sems.at[j * n_p + p],
                                             recv_sem=recv_sems.at[j * n_p + p], device_id=peer,
                                             device_id_type=MESH).start()
        token[...] = jnp.zeros_like(token)

    return pl.pallas_call(
        body, name=name,
        out_shape=(pltpu.SemaphoreType.DMA((n_c * n_p,)), pltpu.SemaphoreType.DMA((n_c * n_p,)),
                   pltpu.HBM(grad.shape, grad.dtype), pltpu.HBM(land_shape, grad.dtype),
                   jax.ShapeDtypeStruct((8, LANE), F32)),
        in_specs=(HBM, HBM), out_specs=(SEM, SEM, HBM, HBM, pl.BlockSpec(memory_space=pltpu.VMEM)),
        input_output_aliases={0: 2, 1: 3},
        compiler_params=pltpu.CompilerParams(has_side_effects=EFFECT),
    )(pltpu.with_memory_space_constraint(grad, pltpu.HBM),
      pltpu.with_memory_space_constraint(lax.empty(land_shape, grad.dtype), pltpu.HBM))


def _exchange_wait(send_sems, recv_sems, g_thru, land_thru, after, route, tail, name):
    width = g_thru.shape[2]
    n_p = 2 if tail else 1

    def body(g_ref, land_ref, send_sems, recv_sems, after_ref, g_dead, got_ref):
        for j, (peer, slab, _, slot) in enumerate(route(*_place())):
            for p, (src, dst) in enumerate(_exchange_pieces(g_ref, land_ref, width, tail)):
                cp = pltpu.make_async_remote_copy(src_ref=src(slab), dst_ref=dst(slot),
                                                  send_sem=send_sems.at[j * n_p + p], recv_sem=recv_sems.at[j * n_p + p],
                                                  device_id=peer, device_id_type=MESH)
                cp.wait_send()
                cp.wait_recv()

    return pl.pallas_call(
        body, name=name,
        out_shape=(pltpu.HBM(g_thru.shape, g_thru.dtype), pltpu.HBM(land_thru.shape, land_thru.dtype)),
        in_specs=(HBM, HBM, SEM, SEM, pl.BlockSpec(memory_space=pl.ANY)), out_specs=(HBM, HBM),
        input_output_aliases={0: 0, 1: 1},
        compiler_params=pltpu.CompilerParams(has_side_effects=EFFECT),
    )(g_thru, land_thru, send_sems, recv_sems, after)


def _pair_add(grad, pair, core, tail, name):
    width = grad.shape[2]
    rows = grad.shape[1]
    tr = _rows(rows, 256)
    total = pair.shape[2]

    def body(core_ref, *refs):
        if tail:
            g_ref, t_ref, p_ref, o_ref = refs
            o_ref[:, 0:width] = (g_ref[...].astype(F32) + p_ref[:, 0:width].astype(F32)).astype(BF16)
            o_ref[:, width:total] = (t_ref[...].astype(F32) + p_ref[:, width:total].astype(F32)).astype(BF16)
        else:
            g_ref, p_ref, o_ref = refs
            o_ref[...] = (g_ref[...].astype(F32) + p_ref[...].astype(F32)).astype(BF16)

    slab = pl.BlockSpec((None, tr, total), lambda q, i, core_ref: (q, i, 0))
    in_specs = [pl.BlockSpec((None, tr, width), lambda q, i, core_ref: (2 * q + core_ref[0], i, 0))]
    if tail:
        in_specs.append(pl.BlockSpec((None, tr, LANE), lambda q, i, core_ref: (2 * q + core_ref[0] + 1, i, 0)))
    return pl.pallas_call(
        body, name=name,
        grid_spec=pltpu.PrefetchScalarGridSpec(num_scalar_prefetch=1, grid=(N_CHIP, rows // tr),
                                               in_specs=in_specs + [slab], out_specs=slab),
        out_shape=jax.ShapeDtypeStruct(pair.shape, BF16),
        compiler_params=_params("parallel", "parallel"),
    )(core, *([grad, grad] if tail else [grad]), pair)


def _gather_start(block, after, name):
    land_shape = (N_DEV,) + block.shape

    def body(b_ref, land_ref, after_ref, send_sems, recv_sems, b_thru, land_thru, token):
        x, y, c = _place()
        me = _index((x, y, c))
        for k, to in enumerate([(x, y, 1 - c)] + [(*chip, c) for chip in _chips(x, y)]):
            pltpu.make_async_remote_copy(src_ref=b_ref, dst_ref=land_ref.at[me], send_sem=send_sems.at[k],
                                         recv_sem=recv_sems.at[k], device_id=to, device_id_type=MESH).start()
        token[...] = jnp.zeros_like(token)

    return pl.pallas_call(
        body, name=name,
        out_shape=(pltpu.SemaphoreType.DMA((4,)), pltpu.SemaphoreType.DMA((4,)),
                   pltpu.HBM(block.shape, block.dtype), pltpu.HBM(land_shape, block.dtype),
                   jax.ShapeDtypeStruct((8, LANE), F32)),
        in_specs=(HBM, HBM, pl.BlockSpec(memory_space=pl.ANY)),
        out_specs=(SEM, SEM, HBM, HBM, pl.BlockSpec(memory_space=pltpu.VMEM)),
        input_output_aliases={0: 2, 1: 3},
        compiler_params=pltpu.CompilerParams(has_side_effects=EFFECT),
    )(pltpu.with_memory_space_constraint(block, pltpu.HBM),
      pltpu.with_memory_space_constraint(lax.empty(land_shape, block.dtype), pltpu.HBM), after)


def _gather_forward(send_sems, recv_sems, b_thru, land_thru, after, name):
    def body(b_ref, land_ref, send_sems, recv_sems, after_ref, b_dead, land_out, send2, recv2, token):
        x, y, c = _place()
        sibling = (x, y, 1 - c)
        for k, frm in enumerate([sibling] + [(*chip, c) for chip in _chips(x, y)]):
            cp = pltpu.make_async_remote_copy(src_ref=b_ref, dst_ref=land_ref.at[_index(frm)], send_sem=send_sems.at[k],
                                              recv_sem=recv_sems.at[k], device_id=frm, device_id_type=MESH)
            cp.wait_send()
            cp.wait_recv()
        for j, chip in enumerate(_chips(x, y)):
            slot = land_ref.at[_index((*chip, c))]
            pltpu.make_async_remote_copy(src_ref=slot, dst_ref=slot, send_sem=send2.at[j], recv_sem=recv2.at[j],
                                         device_id=sibling, device_id_type=MESH).start()
        token[...] = jnp.zeros_like(token)

    return pl.pallas_call(
        body, name=name,
        out_shape=(pltpu.HBM(b_thru.shape, b_thru.dtype), pltpu.HBM(land_thru.shape, land_thru.dtype),
                   pltpu.SemaphoreType.DMA((3,)), pltpu.SemaphoreType.DMA((3,)), jax.ShapeDtypeStruct((8, LANE), F32)),
        in_specs=(HBM, HBM, SEM, SEM, pl.BlockSpec(memory_space=pl.ANY)),
        out_specs=(HBM, HBM, SEM, SEM, pl.BlockSpec(memory_space=pltpu.VMEM)),
        input_output_aliases={0: 0, 1: 1},
        compiler_params=pltpu.CompilerParams(has_side_effects=EFFECT),
    )(b_thru, land_thru, send_sems, recv_sems, after)


def _gather_finish(land_thru, send2, recv2, after, name):
    def body(land_ref, send2, recv2, after_ref, land_out):
        x, y, c = _place()
        for j, chip in enumerate(_chips(x, y)):
            cp = pltpu.make_async_remote_copy(src_ref=land_ref.at[_index((*chip, c))],
                                              dst_ref=land_ref.at[_index((*chip, 1 - c))], send_sem=send2.at[j],
                                              recv_sem=recv2.at[j], device_id=(x, y, 1 - c), device_id_type=MESH)
            cp.wait_send()
            cp.wait_recv()

    return pl.pallas_call(
        body, name=name, out_shape=pltpu.HBM(land_thru.shape, land_thru.dtype),
        in_specs=(HBM, SEM, SEM, pl.BlockSpec(memory_space=pl.ANY)), out_specs=HBM,
        input_output_aliases={0: 0},
        compiler_params=pltpu.CompilerParams(has_side_effects=EFFECT),
    )(land_thru, send2, recv2, after)


class _Gathering:
    def __init__(self, blocks, me, after):
        self.blocks, self.me, self.state, self.token = blocks, me, {}, after
        for name, block in blocks.items():
            *self.state[name], self.token = _gather_start(block, self.token, "gather1_" + name)

    def begin(self, after):
        return self.token

    def forward(self, name, after):
        *self.state[name], token = _gather_forward(*self.state[name], after, "gather2_" + name)
        return token

    def get(self, name, after):
        block, land, send2, recv2 = self.state[name]
        land = _gather_finish(land, send2, recv2, after, "gather3_" + name)
        land = lax.dynamic_update_index_in_dim(land, block[None], self.me, 0)
        return land if name not in ("w_out", "w_down") else land.reshape(-1, land.shape[2])


class _Reducing:
    def __init__(self, core, chip):
        self.core, self.chip, self.state = core, chip, {}

    def start(self, name, grad):
        g = grad if grad.ndim == 3 else grad.reshape(N_DEV, grad.shape[0] // N_DEV, grad.shape[1])
        *self.state[name], token = _exchange_start(g, _pair_route, name == "w_in", "pair_send_" + name)
        return token

    def relay(self, name, after):
        tail = name == "w_in"
        grad, pair = _exchange_wait(*self.state[name], after, _pair_route, tail, "pair_recv_" + name)
        total = _pair_add(grad, pair, self.core, tail, "pair_add_" + name)
        *self.state[name], token = _exchange_start(total, _chip_route, False, "chip_send_" + name)
        return token

    def finish(self, name, after):
        total, land = _exchange_wait(*self.state[name], after, _chip_route, False, "chip_recv_" + name)
        own = lax.dynamic_index_in_dim(total, self.chip, 0, keepdims=True)
        return lax.dynamic_update_index_in_dim(land, own, self.chip, 0)


def _assemble_w_in(g, ninp):
    _, d, pw = g.shape
    per = IN_SLAB // LANE
    assert ninp == (N_DEV + 1) * IN_SLAB and pw == IN_SLAB + LANE
    tr = _rows(d, 512)

    def body(a_ref, b_ref, o_ref):
        s = pl.program_id(0)
        a = a_ref[...]
        o_ref[:, 0:LANE] = (jnp.where(s < N_DEV, a[:, 0:LANE], jnp.zeros_like(b_ref))
                            + jnp.where(s > 0, b_ref[...], jnp.zeros_like(b_ref)))
        o_ref[:, LANE:IN_SLAB] = jnp.where(s < N_DEV, a[:, LANE:IN_SLAB], jnp.zeros_like(a[:, LANE:IN_SLAB]))

    return pl.pallas_call(
        body, name="assemble_w_in", grid=(N_DEV + 1, d // tr),
        in_specs=[pl.BlockSpec((None, tr, pw), lambda s, i: (jnp.minimum(s, N_DEV - 1), i, 0)),
                  pl.BlockSpec((None, tr, LANE), lambda s, i: (jnp.maximum(s, 1) - 1, i, per))],
        out_specs=pl.BlockSpec((tr, IN_SLAB), lambda s, i: (i, s)),
        out_shape=jax.ShapeDtypeStruct((d, ninp), g.dtype),
        compiler_params=_params("parallel", "parallel"),
    )(g, g)


def _rows(n, want):
    t = min(n, want)
    t -= t % 16
    while n % t:
        t -= 16
    return t


def _adam_math(w, g, m, v):
    m2 = ADAM_B1 * m + (1.0 - ADAM_B1) * g
    v2 = ADAM_B2 * v + (1.0 - ADAM_B2) * (g * g)
    m_hat = m2 / (1.0 - ADAM_B1 ** ADAM_STEP)
    v_hat = v2 / (1.0 - ADAM_B2 ** ADAM_STEP)
    return -ADAM_LR * (m_hat / (jnp.sqrt(v_hat) + ADAM_EPS) + ADAM_WD * w), m2, v2


def _slot_sum(r_ref):
    acc = r_ref[0].astype(F32)
    for i in range(1, r_ref.shape[0]):
        acc = acc + r_ref[i].astype(F32)
    return acc


def _shift_w_in(w_pad):
    d, pw = w_pad.shape
    tr = _rows(d, 512)

    def body(w_ref, o_ref):
        o_ref[...] = pltpu.roll(w_ref[...], _index(_place()), 1).astype(BF16)

    blk = pl.BlockSpec((tr, pw), lambda i: (i, 0))
    return pl.pallas_call(
        body, name="shift_w_in", grid=(d // tr,), in_specs=[blk], out_specs=blk,
        out_shape=jax.ShapeDtypeStruct((d, pw), BF16), compiler_params=_params("parallel"),
    )(w_pad)


def _sum_slots(r, name, tr=128, unshift=False):
    _, rows, cols = r.shape
    tr = _rows(rows, tr)

    def body(r_ref, g_ref):
        g = _slot_sum(r_ref)
        g_ref[...] = pltpu.roll(g, lax.rem(cols - _index(_place()), cols), 1) if unshift else g

    return pl.pallas_call(
        body, name=name, grid=(rows // tr,),
        in_specs=[pl.BlockSpec((r.shape[0], tr, cols), lambda i: (0, i, 0))],
        out_specs=pl.BlockSpec((tr, cols), lambda i: (i, 0)),
        out_shape=jax.ShapeDtypeStruct((rows, cols), F32),
        compiler_params=_params("parallel"),
    )(r)


def _adamw(w, g, m, v, name, tr=256):
    rows, cols = w.shape
    tr = _rows(rows, tr)

    def body(w_ref, g_ref, m_ref, v_ref, d_ref, m2_ref, v2_ref):
        d_ref[...], m2_ref[...], v2_ref[...] = _adam_math(w_ref[...], g_ref[...], m_ref[...], v_ref[...])

    blk = pl.BlockSpec((tr, cols), lambda i: (i, 0))
    out = jax.ShapeDtypeStruct((rows, cols), F32)
    return pl.pallas_call(
        body, name=name, grid=(rows // tr,), in_specs=[blk] * 4, out_specs=[blk] * 3, out_shape=[out] * 3,
        compiler_params=_params("parallel"),
    )(w, g, m, v)


def _sum_adamw(r, w, m, v, name, tr=128):
    rows, cols = w.shape
    tr = _rows(rows, tr)

    def body(r_ref, w_ref, m_ref, v_ref, g_ref, d_ref, m2_ref, v2_ref):
        g = _slot_sum(r_ref)
        g_ref[...] = g
        d_ref[...], m2_ref[...], v2_ref[...] = _adam_math(w_ref[...], g, m_ref[...], v_ref[...])

    blk = pl.BlockSpec((tr, cols), lambda i: (i, 0))
    out = jax.ShapeDtypeStruct((rows, cols), F32)
    return pl.pallas_call(
        body, name=name, grid=(rows // tr,),
        in_specs=[pl.BlockSpec((r.shape[0], tr, cols), lambda i: (0, i, 0)), blk, blk, blk],
        out_specs=[blk] * 4, out_shape=[out] * 4,
        compiler_params=_params("parallel"),
    )(r, w, m, v)


def _pack(pieces, sizes):
    flat = [jnp.pad(p.reshape(-1).astype(F32), (0, s - p.size)) for p, s in zip(pieces, sizes)]
    total = sum(sizes)
    padded = -(-total // (16 * LANE)) * (16 * LANE)
    return jnp.pad(jnp.concatenate(flat), (0, padded - total)).reshape(-1, LANE)


def _unpack(packed, shapes, sizes):
    flat = packed.reshape(-1)
    out, off = [], 0
    for shp, s in zip(shapes, sizes):
        n = 1
        for k in shp:
            n *= k
        out.append(flat[off:off + n].reshape(shp))
        off += s
    return out


def _lanes(n):
    return -(-n // LANE) * LANE


WEIGHTS = ("w_in", "b_gates", "w_sc_conv", "mh_gain", "w_out", "ln1_g", "ln1_b", "w_up", "w_ffn_conv", "b_ffn_conv",
           "w_down", "ln2_g", "ln2_b")
BIG = ("w_in", "w_out", "w_up", "w_down")
SMALL = tuple(n for n in WEIGHTS if n not in BIG)


def kernel(x, w_in, b_gates, w_sc_conv, mh_gain, w_out, ln1_g, ln1_b, w_up, w_ffn_conv, b_ffn_conv, w_down, ln2_g, ln2_b, loss_target, m_w_in, m_b_gates, m_w_sc_conv, m_mh_gain, m_w_out, m_ln1_g, m_ln1_b, m_w_up, m_w_ffn_conv, m_b_ffn_conv, m_w_down, m_ln2_g, m_ln2_b, v_w_in, v_b_gates, v_w_sc_conv, v_mh_gain, v_w_out, v_ln1_g, v_ln1_b, v_w_up, v_w_ffn_conv, v_b_ffn_conv, v_w_down, v_ln2_g, v_ln2_b):
    w = dict(zip(WEIGHTS, (w_in, b_gates, w_sc_conv, mh_gain, w_out, ln1_g, ln1_b, w_up, w_ffn_conv, b_ffn_conv,
                           w_down, ln2_g, ln2_b)))
    m = dict(zip(WEIGHTS, (m_w_in, m_b_gates, m_w_sc_conv, m_mh_gain, m_w_out, m_ln1_g, m_ln1_b, m_w_up,
                           m_w_ffn_conv, m_b_ffn_conv, m_w_down, m_ln2_g, m_ln2_b)))
    v = dict(zip(WEIGHTS, (v_w_in, v_b_gates, v_w_sc_conv, v_mh_gain, v_w_out, v_ln1_g, v_ln1_b, v_w_up,
                           v_w_ffn_conv, v_b_ffn_conv, v_w_down, v_ln2_g, v_ln2_b)))
    me = _index(_place())
    d = x.shape[2]
    ws_in = w_in.shape[2]
    assert ws_in == IN_SLAB + 1 and N_DEV <= LANE, w_in.shape
    ninp = (N_DEV + 1) * IN_SLAB
    ws_sc, ws_fc = w_sc_conv.shape[2], w_ffn_conv.shape[2]

    w_in_shift = _shift_w_in(jnp.pad(w_in[0], ((0, 0), (0, IN_SLAB + LANE - ws_in))))
    taps8 = lambda a: jnp.pad(a[0], ((0, 5), (0, 0)))
    blocks = dict(w_in=w_in_shift, w_sc=taps8(w_sc_conv), w_fc=taps8(w_ffn_conv),
                  **{n: w[n][0].astype(BF16) for n in ("w_out", "w_up", "w_down")})
    wx = _Gathering(blocks, me, w_in_shift)
    token = wx.begin(None)
    for n in ("w_in", "w_sc", "w_fc"):
        token = wx.forward(n, token)
    g_in, g_sc, g_fc = (wx.get(n, token) for n in ("w_in", "w_sc", "w_fc"))
    w_in_full = _assemble_w_in(g_in, ninp)
    w_sc_full = g_sc[:, :3].transpose(1, 0, 2).reshape(3, N_DEV * ws_sc)
    w_fc_full = g_fc[:, :3].transpose(1, 0, 2).reshape(3, N_DEV * ws_fc)

    xi, yi, ci = _place()
    gx = _Reducing(jnp.reshape(ci, (1,)).astype(jnp.int32), 2 * xi + yi)
    loss_t, grad_x, small, _ = _local_step(
        x[0], loss_target[0], w_in_full, b_gates, w_sc_full, mh_gain, None, ln1_g, ln1_b, None,
        w_fc_full, b_ffn_conv, None, ln2_g, ln2_b, gx=gx, wx=wx)

    grads, deltas, new_m, new_v = {}, {}, {}, {}
    for name in ("w_down", "w_up", "w_out"):
        grads[name], deltas[name], new_m[name], new_v[name] = _sum_adamw(
            gx.finish(name, grad_x), w[name][0], m[name][0], v[name][0], "adamw_" + name)

    names = ("loss",) + SMALL
    parts = dict(small, loss=loss_t[0, :1])
    sizes = [_lanes(parts[n].size) for n in names]
    (g_small,) = _all_gather([_pack([parts[n] for n in names], sizes)], "gather_small")
    summed = _unpack(_sum_slots(g_small, "sum_small", tr=g_small.shape[1]), [parts[n].shape for n in names], sizes)
    full = dict(zip(names, summed))
    full["w_sc_conv"] = lax.dynamic_slice(full["w_sc_conv"], (0, me * ws_sc), (3, ws_sc))
    full["w_ffn_conv"] = lax.dynamic_slice(full["w_ffn_conv"], (0, me * ws_fc), (3, ws_fc))
    for n in SMALL:
        grads[n] = full[n].reshape(w[n].shape)
    sizes = [_lanes(w[n].size) for n in SMALL]
    shapes = [w[n].shape for n in SMALL]
    packed = [_pack([t[n] for n in SMALL], sizes) for t in (w, grads, m, v)]
    small_out = _adamw(*packed, "adamw_small")
    for res, t in zip(small_out, (deltas, new_m, new_v)):
        t.update(zip(SMALL, _unpack(res, shapes, sizes)))

    done = sum(t[0:1, 0:1] for t in (deltas["w_down"], deltas["w_up"], deltas["w_out"], small_out[0]))
    grads["w_in"] = _sum_slots(gx.finish("w_in", done), "sum_w_in", unshift=True)[:, :ws_in]
    deltas["w_in"], new_m["w_in"], new_v["w_in"] = _adamw(w_in[0], grads["w_in"], m_w_in[0], v_w_in[0], "adamw_w_in")

    big = lambda t: {n: (t[n].reshape(w[n].shape) if n in BIG else t[n]) for n in WEIGHTS}
    grads, deltas, new_m, new_v = big(grads), big(deltas), big(new_m), big(new_v)
    return (full["loss"].reshape(()), grad_x[None], *[grads[n] for n in WEIGHTS], *[deltas[n] for n in WEIGHTS],
            *[new_m[n] for n in WEIGHTS], *[new_v[n] for n in WEIGHTS])
```

```python
import functools

import jax
import jax.numpy as jnp
from jax import lax
from jax.experimental import pallas as pl
from jax.experimental.pallas import tpu as pltpu

F32 = jnp.float32
BF16 = jnp.bfloat16
MESH = pl.DeviceIdType.MESH

N_DEV = 8
NH = 4
CHUNK = 64
LN_EPS = 1e-5
HN_EPS = 1e-6
ALPHA = 2.0 ** 0.25
LANE = 128
IN_SLAB = 7 * LANE
VMEM_LIMIT = 56 * 1024 * 1024
ADAM_LR, ADAM_B1, ADAM_B2, ADAM_EPS, ADAM_WD, ADAM_STEP = 0.001, 0.9, 0.999, 1e-08, 0.01, 10

_NN = (((1,), (0,)), ((), ()))
_NT = (((1,), (1,)), ((), ()))
_TN = (((0,), (0,)), ((), ()))


def _dot(a, b, dn=_NN):
    return lax.dot_general(a, b, dn, preferred_element_type=F32)


def _params(*sem):
    return pltpu.CompilerParams(dimension_semantics=sem if sem else None, vmem_limit_bytes=VMEM_LIMIT)


def _iota(shape, axis):
    return lax.broadcasted_iota(jnp.int32, shape, axis)


def _fit(n, want):
    if n <= want:
        return n
    t = want - want % LANE
    while n % t:
        t -= LANE
    return t


def _matmul(a, b, mode, out_dtype, name, tm=1024, tn=512, tk=1024, add=None, add_scale=1.0,
            a_blocked=False, b_blocked=False, o_width=None, after=None):
    if a_blocked:
        na, a_rows, wa = a.shape
        kd, m = (a_rows, na * wa) if mode == "tn" else (na * wa, a_rows)
    elif mode == "tn":
        kd, m = a.shape
    else:
        m, kd = a.shape
    if b_blocked:
        nb, rows, w = b.shape
        n = rows if mode == "nt" else nb * w
        assert (nb * w if mode == "nt" else rows) == kd, (name, b.shape, kd)
    else:
        n = b.shape[0] if mode == "nt" else b.shape[1]
    tm, tn, tk = _fit(m, tm), _fit(n, tn), _fit(kd, tk)
    if a_blocked and mode == "tn":
        tm = _fit(wa, tm)
    if a_blocked and mode != "tn":
        tk = _fit(wa, tk)
    if b_blocked and mode != "nt":
        tn = _fit(w, tn)
    if b_blocked and mode == "nt":
        tk = _fit(w, tk)
    if o_width is not None:
        tn = _fit(o_width, tn)
    assert m % tm == 0 and n % tn == 0 and kd % tk == 0, (name, m, n, kd, tm, tn, tk)
    assert not (a_blocked and mode != "tn" and wa % tk) and not (b_blocked and mode == "nt" and w % tk), (name, tk)
    nk = kd // tk
    dn = {"nn": _NN, "nt": _NT, "tn": _TN}[mode]
    if a_blocked and mode == "tn":
        a_per = wa // tm
        a_spec = pl.BlockSpec((None, tk, tm), lambda i, j, k: (i // a_per, k, i % a_per))
    elif a_blocked:
        a_per = wa // tk
        a_spec = pl.BlockSpec((None, tm, tk), lambda i, j, k: (k // a_per, i, k % a_per))
    elif mode == "tn":
        a_spec = pl.BlockSpec((tk, tm), lambda i, j, k: (k, i))
    else:
        a_spec = pl.BlockSpec((tm, tk), lambda i, j, k: (i, k))
    if b_blocked and mode != "nt":
        per = w // tn
        b_spec = pl.BlockSpec((None, tk, tn), lambda i, j, k: (j // per, k, j % per))
    elif b_blocked:
        per = w // tk
        b_spec = pl.BlockSpec((None, tn, tk), lambda i, j, k: (k // per, j, k % per))
    elif mode == "nt":
        b_spec = pl.BlockSpec((tn, tk), lambda i, j, k: (j, k))
    else:
        b_spec = pl.BlockSpec((tk, tn), lambda i, j, k: (k, j))
    if o_width is None:
        o_spec = pl.BlockSpec((tm, tn), lambda i, j, k: (i, j))
        o_shape = (m, n)
    else:
        oper = o_width // tn
        o_spec = pl.BlockSpec((None, tm, tn), lambda i, j, k: (j // oper, i, j % oper))
        o_shape = (n // o_width, m, o_width)
    has_add = add is not None
    n_in = 2 + has_add + (after is not None)
    in_place = nk > 1 and out_dtype == F32

    def body(*refs):
        a_ref, b_ref = refs[:2]
        add_ref = refs[2] if has_add else None
        o_ref = refs[n_in]

        def finish(r):
            if has_add:
                r = r + add_scale * add_ref[...]
            o_ref[...] = r.astype(out_dtype)

        if nk == 1:
            finish(_dot(a_ref[...], b_ref[...], dn))
        else:
            acc = o_ref if in_place else refs[-1]
            k = pl.program_id(2)

            @pl.when(k == 0)
            def _():
                acc[...] = _dot(a_ref[...], b_ref[...], dn)

            @pl.when(k > 0)
            def _():
                acc[...] += _dot(a_ref[...], b_ref[...], dn)

            if not (in_place and not has_add):
                @pl.when(k == nk - 1)
                def _():
                    finish(acc[...])

    in_specs = [a_spec, b_spec] + ([pl.BlockSpec((tm, tn), lambda i, j, k: (i, j))] if has_add else [])
    args = (a, b) + ((add,) if has_add else ())
    if after is not None:
        in_specs.append(pl.BlockSpec(memory_space=pl.ANY))
        args += (after,)
    return pl.pallas_call(
        body, name=name, grid=(m // tm, n // tn, nk),
        in_specs=in_specs, out_specs=o_spec,
        out_shape=jax.ShapeDtypeStruct(o_shape, out_dtype),
        scratch_shapes=[pltpu.VMEM((tm, tn), F32)] if nk > 1 and not in_place else [],
        compiler_params=_params("parallel", "parallel", "arbitrary"),
    )(*args)


def _shift_down(u, s):
    return jnp.where(_iota(u.shape, 0) >= s, pltpu.roll(u, s, 0), 0.0)


def _shift_up(u, s):
    t = u.shape[0]
    return jnp.where(_iota(u.shape, 0) < t - s, pltpu.roll(u, t - s, 0), 0.0)


SLAB = 8


def _rolled(u):
    return pltpu.roll(u, 2, 0), pltpu.roll(u, 1, 0)


def _conv(u, w, rolled=None):
    u2, u1 = _rolled(u) if rolled is None else rolled
    raw = w[0:1] * u2 + w[1:2] * u1 + w[2:3] * u
    head = u[0:SLAB]
    mended = w[0:1] * _shift_down(head, 2) + w[1:2] * _shift_down(head, 1) + w[2:3] * head
    return jnp.concatenate([mended, raw[SLAB:]], axis=0)


def _conv_t(dy, w):
    t = dy.shape[0]
    raw = w[2:3] * dy + w[1:2] * pltpu.roll(dy, t - 1, 0) + w[0:1] * pltpu.roll(dy, t - 2, 0)
    tail = dy[t - SLAB:]
    mended = w[2:3] * tail + w[1:2] * _shift_up(tail, 1) + w[0:1] * _shift_up(tail, 2)
    return jnp.concatenate([raw[:t - SLAB], mended], axis=0)


def _conv_dw(dy, u, rolled=None):
    t = dy.shape[0]
    u2, u1 = _rolled(u) if rolled is None else rolled
    head, tail = dy[0:SLAB], u[t - SLAB:]
    r = _iota(head.shape, 0)
    wrap2 = jnp.sum(jnp.where(r < 2, head * pltpu.roll(tail, 2, 0), 0.0), axis=0, keepdims=True)
    wrap1 = jnp.sum(jnp.where(r < 1, head * pltpu.roll(tail, 1, 0), 0.0), axis=0, keepdims=True)
    d0 = jnp.sum(dy * u2, axis=0, keepdims=True) - wrap2
    d1 = jnp.sum(dy * u1, axis=0, keepdims=True) - wrap1
    d2 = jnp.sum(dy * u, axis=0, keepdims=True)
    r3 = _iota((3, dy.shape[1]), 0)
    return jnp.where(r3 == 0, d0, jnp.where(r3 == 1, d1, d2))


def _sigmoid(x):
    return 0.5 * jnp.tanh(0.5 * x) + 0.5


def _sconv_fwd(proj, w_sc, t, wc):
    nb = wc // LANE

    def body(cb_ref, cc_ref, ch_ref, w_ref, y_ref):
        u = cc_ref[...] * ch_ref[...]
        y_ref[...] = (cb_ref[...] * _conv(u, w_ref[...])).astype(BF16)

    col = lambda off: pl.BlockSpec((t, LANE), lambda j: (0, j + off))
    return pl.pallas_call(
        body, name="sconv_fwd", grid=(nb,),
        in_specs=[col(0), col(nb), col(2 * nb), pl.BlockSpec((3, LANE), lambda j: (0, j))],
        out_specs=pl.BlockSpec((None, t, LANE), lambda j: (0, 0, j)),
        out_shape=jax.ShapeDtypeStruct((2, t, wc), BF16),
        compiler_params=_params("parallel"),
    )(proj, proj, proj, w_sc)


def _sconv_bwd(dy, proj, w_sc, t, wc):
    nb = wc // LANE

    def body(dy_ref, cb_ref, cc_ref, ch_ref, w_ref, dcb_ref, dcc_ref, dch_ref, dw_ref):
        cc, ch, w, d = cc_ref[...], ch_ref[...], w_ref[...], dy_ref[...]
        u = cc * ch
        ru = _rolled(u)
        dcb_ref[...] = (d * _conv(u, w, ru)).astype(BF16)
        dcu = d * cb_ref[...]
        dw_ref[...] = _conv_dw(dcu, u, ru)
        du = _conv_t(dcu, w)
        dcc_ref[...] = (du * ch).astype(BF16)
        dch_ref[...] = (du * cc).astype(BF16)

    col = lambda off: pl.BlockSpec((t, LANE), lambda j: (0, j + off))
    act = jax.ShapeDtypeStruct((t, wc), BF16)
    return pl.pallas_call(
        body, name="sconv_bwd", grid=(nb,),
        in_specs=[col(0), col(0), col(nb), col(2 * nb), pl.BlockSpec((3, LANE), lambda j: (0, j))],
        out_specs=[col(0), col(0), col(0), pl.BlockSpec((3, LANE), lambda j: (0, j))],
        out_shape=[act, act, act, jax.ShapeDtypeStruct((3, wc), F32)],
        compiler_params=_params("parallel"),
    )(dy, proj, proj, proj, w_sc)


def _gates_prep(proj, bias_tile, t, gate_tile):
    def body(g_ref, b_ref, o_ref):
        g = g_ref[...] + b_ref[...]
        lane = _iota(g.shape, 1)
        is_f = (lane >= NH) & (lane < 2 * NH)
        lf = jnp.minimum(g, 0.0) - jnp.log(1.0 + jnp.exp(-jnp.abs(g)))
        c = jnp.where(is_f, lf, 0.0)
        r = _iota(g.shape, 0) % CHUNK
        s = 1
        while s < CHUNK:
            c = c + jnp.where(r >= s, pltpu.roll(c, s, 0), 0.0)
            s *= 2
        o_ref[...] = jnp.where(is_f, c, jnp.where(lane < NH, g, 0.0))

    return pl.pallas_call(
        body, name="gates_prep", grid=(1,),
        in_specs=[pl.BlockSpec((t, LANE), lambda i: (0, gate_tile)), pl.BlockSpec((1, LANE), lambda i: (0, 0))],
        out_specs=pl.BlockSpec((t, LANE), lambda i: (0, 0)),
        out_shape=jax.ShapeDtypeStruct((t, LANE), F32),
        compiler_params=_params("arbitrary"),
    )(proj, bias_tile)


def _gates_bwd(dgate, proj, bias_tile, t, gate_tile):
    def body(dg_ref, g_ref, b_ref, o_ref, s_ref):
        g = g_ref[...] + b_ref[...]
        lane = _iota(g.shape, 1)
        r = _iota(g.shape, 0) % CHUNK
        dsig = 1.0 - _sigmoid(g)
        out = jnp.zeros(g.shape, F32)
        for h in range(NH):
            d = dg_ref[h]
            c = d
            s = 1
            while s < CHUNK:
                c = c + jnp.where(r + s < CHUNK, pltpu.roll(c, t - s, 0), 0.0)
                s *= 2
            di = jnp.broadcast_to(d[:, 0:1], g.shape)
            db = jnp.broadcast_to(c[:, 1:2], g.shape)
            out = out + jnp.where(lane == h, di, 0.0) + jnp.where(lane == NH + h, db * dsig, 0.0)
        o_ref[...] = out.astype(BF16)
        s_ref[...] = jnp.sum(out, axis=0, keepdims=True)

    return pl.pallas_call(
        body, name="gates_bwd", grid=(1,),
        in_specs=[pl.BlockSpec((NH, t, LANE), lambda i: (0, 0, 0)),
                  pl.BlockSpec((t, LANE), lambda i: (0, gate_tile)), pl.BlockSpec((1, LANE), lambda i: (0, 0))],
        out_specs=[pl.BlockSpec((t, LANE), lambda i: (0, 0)), pl.BlockSpec((1, LANE), lambda i: (0, 0))],
        out_shape=[jax.ShapeDtypeStruct((t, LANE), BF16), jax.ShapeDtypeStruct((1, LANE), F32)],
        compiler_params=_params("arbitrary"),
    )(dgate, proj, bias_tile)


def _chunk_gates(gc, gr, h, mprev):
    L = CHUNK
    icol, bcol = gc[:, h:h + 1], gc[:, h + NH:h + NH + 1]
    irow, brow = gr[h:h + 1, :], gr[h + NH:h + NH + 1, :]
    tri = _iota((L, L), 0) >= _iota((L, L), 1)
    log_d = jnp.where(tri, bcol - brow + irow, -jnp.inf)
    inter = bcol + mprev
    mt = jnp.maximum(inter, jnp.max(log_d, axis=1, keepdims=True))
    dw = jnp.exp(log_d - mt)
    iw = jnp.exp(inter - mt)
    g = brow[:, L - 1:L]
    wlog_col = g - bcol + icol
    wlog_row = g - brow + irow
    mnew = jnp.maximum(g + mprev, jnp.max(wlog_row, axis=1, keepdims=True))
    wcol = jnp.exp(wlog_col - mnew)
    decay = jnp.exp(g + mprev - mnew)
    return dw, iw, mt, wcol, decay, mnew


def _mlstm_fwd(proj, gcol, grow, t, wc, dh):
    nc = t // CHUNK
    wm = NH * dh
    assert wc == wm, (wc, wm)
    qoff = 3 * wc // wm
    scale = dh ** -0.5

    def body(q_ref, k_ref, v_ref, gc_ref, gr_ref, h_ref, cs_ref, ns_ref, c_s, n_s, m_s):
        @pl.when(pl.program_id(0) == 0)
        def _():
            c_s[...] = jnp.zeros_like(c_s)
            n_s[...] = jnp.zeros_like(n_s)
            m_s[...] = jnp.zeros_like(m_s)

        gc, gr = gc_ref[...], gr_ref[0]
        for h in range(NH):
            cols = slice(h * dh, (h + 1) * dh)
            mprev = m_s[h, 0:1, 0:1]
            cprev = c_s[h]
            n8 = n_s[h]
            nprev = n8[0:1]
            cs_ref[h] = cprev
            ns_ref[h] = jnp.where(_iota(n8.shape, 0) == 1, mprev, n8)

            dw, iw, mt, wcol, decay, mnew = _chunk_gates(gc, gr, h, mprev)
            qs = q_ref[:, cols] * scale
            k = k_ref[:, cols]
            qs_b, k_b, v_b = qs.astype(BF16), k.astype(BF16), v_ref[:, cols].astype(BF16)
            s = _dot(qs_b, k_b, _NT) * dw
            num = _dot(s.astype(BF16), v_b) + iw * _dot(qs_b, cprev.astype(BF16))
            den = jnp.sum(s, axis=1, keepdims=True) + iw * jnp.sum(qs * nprev, axis=1, keepdims=True)
            h_ref[:, cols] = num / jnp.maximum(jnp.abs(den), jnp.exp(-mt))

            wk = wcol * k
            c_s[h] = decay * cprev + _dot(wk.astype(BF16), v_b, _TN)
            n_s[h] = decay * n8 + jnp.sum(wk, axis=0, keepdims=True)
            m_s[h] = jnp.broadcast_to(mnew, m_s.shape[1:])

    grp = lambda off: pl.BlockSpec((CHUNK, wm), lambda c: (c, qoff + off))
    return pl.pallas_call(
        body, name="mlstm_fwd", grid=(nc,),
        in_specs=[grp(0), grp(1), grp(2),
                  pl.BlockSpec((CHUNK, LANE), lambda c: (c, 0)),
                  pl.BlockSpec((1, 8, CHUNK), lambda c: (c, 0, 0))],
        out_specs=[pl.BlockSpec((CHUNK, wm), lambda c: (c, 0)),
                   pl.BlockSpec((NH, None, dh, dh), lambda c: (0, c, 0, 0)),
                   pl.BlockSpec((NH, None, 8, dh), lambda c: (0, c, 0, 0))],
        out_shape=[jax.ShapeDtypeStruct((t, wm), F32),
                   jax.ShapeDtypeStruct((NH, nc, dh, dh), F32),
                   jax.ShapeDtypeStruct((NH, nc, 8, dh), F32)],
        scratch_shapes=[pltpu.VMEM((NH, dh, dh), F32), pltpu.VMEM((NH, 8, dh), F32), pltpu.VMEM((NH, 8, LANE), F32)],
        compiler_params=_params("arbitrary"),
    )(proj, proj, proj, gcol, grow)


def _mlstm_bwd(proj, gcol, grow, hval, dh_in, cs, ns, t, wc, dh):
    nc = t // CHUNK
    wm = NH * dh
    assert wc == wm, (wc, wm)
    qoff = 3 * wc // wm
    scale = dh ** -0.5
    L = CHUNK

    def body(q_ref, k_ref, v_ref, gc_ref, gr_ref, h_ref, dh_ref, cs_ref, ns_ref,
             dq_ref, dk_ref, dv_ref, dg_ref, dc_s, dn_s):
        @pl.when(pl.program_id(0) == 0)
        def _():
            dc_s[...] = jnp.zeros_like(dc_s)
            dn_s[...] = jnp.zeros_like(dn_s)

        gc, gr = gc_ref[...], gr_ref[0]
        eye = _iota((L, L), 0) == _iota((L, L), 1)
        lane = _iota((L, LANE), 1)
        last = _iota((L, 1), 0) == L - 1
        for h in range(NH):
            cols = slice(h * dh, (h + 1) * dh)
            ns8 = ns_ref[h]
            nprev = ns8[0:1]
            mprev = ns8[1:2, 0:1]
            cprev = cs_ref[h]
            dcn = dc_s[h]
            dn8 = dn_s[h]
            dnn = dn8[0:1]

            dw, iw, mt, wcol, decay, _ = _chunk_gates(gc, gr, h, mprev)
            qs = q_ref[:, cols] * scale
            k = k_ref[:, cols]
            qs_b, k_b, v_b = qs.astype(BF16), k.astype(BF16), v_ref[:, cols].astype(BF16)
            qk = _dot(qs_b, k_b, _NT)
            s = qk * dw
            den = jnp.sum(s, axis=1, keepdims=True) + iw * jnp.sum(qs * nprev, axis=1, keepdims=True)
            emt = jnp.exp(-mt)
            r = 1.0 / jnp.maximum(jnp.abs(den), emt)
            dout = dh_ref[:, cols]
            dnum = dout * r
            dden = (-jnp.sum(dout * h_ref[:, cols], axis=1, keepdims=True) * r
                    * jnp.where(jnp.abs(den) > emt, jnp.sign(den), 0.0))
            dnum_b = dnum.astype(BF16)
            cprev_b = cprev.astype(BF16)
            dcn_b = dcn.astype(BF16)

            gd = (_dot(dnum_b, v_b, _NT) + dden) * dw
            gd_b = gd.astype(BF16)
            dqs_inter = iw * (_dot(dnum_b, cprev_b, _NT) + dden * nprev)
            dqs = _dot(gd_b, k_b) + dqs_inter
            dk_inter = wcol * (_dot(v_b, dcn_b, _NT) + dnn)
            dk = _dot(gd_b, qs_b, _TN) + dk_inter
            wk = wcol * k
            dv = _dot(s.astype(BF16), dnum_b, _TN) + _dot(wk.astype(BF16), dcn_b)

            e = gd * qk
            e_cols = jnp.sum(jnp.where(eye, jnp.sum(e, axis=0, keepdims=True), 0.0), axis=1, keepdims=True)
            k_inter = jnp.sum(k * dk_inter, axis=1, keepdims=True)
            rq = jnp.sum(e, axis=1, keepdims=True) + jnp.sum(qs * dqs_inter, axis=1, keepdims=True)
            rk = e_cols + k_inter
            hsum = jnp.sum(k_inter, axis=0, keepdims=True)
            jdec = decay * (jnp.sum(jnp.sum(dcn * cprev, axis=1, keepdims=True), axis=0, keepdims=True)
                            + jnp.sum(dnn * nprev, axis=1, keepdims=True))
            db = rq - rk + jnp.where(last, hsum + jdec, 0.0)
            dg_ref[h] = jnp.where(lane == 0, rk, jnp.where(lane == 1, db, 0.0))

            dq_ref[:, cols] = (dqs * scale).astype(BF16)
            dk_ref[:, cols] = dk.astype(BF16)
            dv_ref[:, cols] = dv.astype(BF16)

            iq = iw * qs
            dc_s[h] = decay * dcn + _dot(iq.astype(BF16), dnum_b, _TN)
            dn_s[h] = decay * dn8 + jnp.sum(iq * dden, axis=0, keepdims=True)

    rc = lambda c: nc - 1 - c
    grp = lambda off: pl.BlockSpec((L, wm), lambda c: (rc(c), qoff + off))
    hm = pl.BlockSpec((L, wm), lambda c: (rc(c), 0))
    act = jax.ShapeDtypeStruct((t, wm), BF16)
    return pl.pallas_call(
        body, name="mlstm_bwd", grid=(nc,),
        in_specs=[grp(0), grp(1), grp(2),
                  pl.BlockSpec((L, LANE), lambda c: (rc(c), 0)),
                  pl.BlockSpec((1, 8, L), lambda c: (rc(c), 0, 0)),
                  hm, hm,
                  pl.BlockSpec((NH, None, dh, dh), lambda c: (0, rc(c), 0, 0)),
                  pl.BlockSpec((NH, None, 8, dh), lambda c: (0, rc(c), 0, 0))],
        out_specs=[hm, hm, hm, pl.BlockSpec((NH, L, LANE), lambda c: (0, rc(c), 0))],
        out_shape=[act, act, act, jax.ShapeDtypeStruct((NH, t, LANE), F32)],
        scratch_shapes=[pltpu.VMEM((NH, dh, dh), F32), pltpu.VMEM((NH, 8, dh), F32)],
        compiler_params=_params("arbitrary"),
    )(proj, proj, proj, gcol, grow, hval, dh_in, cs, ns)


def _head_norm(hv):
    mu = jnp.mean(hv, axis=1, keepdims=True)
    hc = hv - mu
    rstd = lax.rsqrt(jnp.mean(hc * hc, axis=1, keepdims=True) + HN_EPS)
    return hc * rstd, rstd


def _hnorm_fwd(hval, proj, gain, y, t, wc, dh, tr=256):
    ooff = 3 * wc // dh + 3 * NH

    def body(h_ref, o_ref, g_ref, y_in, y_ref):
        hhat, _ = _head_norm(h_ref[...])
        y_ref[...] = (_sigmoid(o_ref[...]) * hhat * g_ref[...]).astype(BF16)

    return pl.pallas_call(
        body, name="hnorm_fwd", grid=(t // tr, NH),
        in_specs=[pl.BlockSpec((tr, dh), lambda i, h: (i, h)),
                  pl.BlockSpec((tr, dh), lambda i, h: (i, ooff + h)),
                  pl.BlockSpec((1, dh), lambda i, h: (0, h)),
                  pl.BlockSpec(memory_space=pl.ANY)],
        out_specs=pl.BlockSpec((None, tr, dh), lambda i, h: (1, i, h)),
        out_shape=jax.ShapeDtypeStruct(y.shape, BF16),
        input_output_aliases={3: 0},
        compiler_params=_params("parallel", "parallel"),
    )(hval, proj, gain, y)


def _hnorm_bwd(dy, hval, proj, gain, t, wc, dh, tr=256):
    ooff = 3 * wc // dh + 3 * NH
    yoff = wc // dh

    def body(dy_ref, h_ref, o_ref, g_ref, do_ref, dh_ref, dg_ref):
        i = pl.program_id(1)
        hhat, rstd = _head_norm(h_ref[...])
        gain_v = g_ref[...]
        sig = _sigmoid(o_ref[...])
        d = dy_ref[...]
        do_ref[...] = (d * hhat * gain_v * sig * (1.0 - sig)).astype(BF16)
        dhn = d * sig
        part = jnp.sum(dhn * hhat, axis=0, keepdims=True)

        @pl.when(i == 0)
        def _():
            dg_ref[...] = part

        @pl.when(i > 0)
        def _():
            dg_ref[...] += part

        dhat = dhn * gain_v
        dh_ref[...] = rstd * (dhat - jnp.mean(dhat, axis=1, keepdims=True)
                              - hhat * jnp.mean(dhat * hhat, axis=1, keepdims=True))

    blk = lambda off: pl.BlockSpec((tr, dh), lambda h, i: (i, off + h))
    return pl.pallas_call(
        body, name="hnorm_bwd", grid=(NH, t // tr),
        in_specs=[blk(yoff), blk(0), blk(ooff), pl.BlockSpec((1, dh), lambda h, i: (0, h))],
        out_specs=[blk(0), blk(0), pl.BlockSpec((1, dh), lambda h, i: (0, h))],
        out_shape=[jax.ShapeDtypeStruct((t, NH * dh), BF16), jax.ShapeDtypeStruct((t, NH * dh), F32),
                   jax.ShapeDtypeStruct((1, NH * dh), F32)],
        compiler_params=_params("parallel", "arbitrary"),
    )(dy, hval, proj, gain)


def _ln_stats(z):
    mu = jnp.mean(z, axis=1, keepdims=True)
    zc = z - mu
    rstd = lax.rsqrt(jnp.mean(zc * zc, axis=1, keepdims=True) + LN_EPS)
    return zc * rstd, rstd


def _ln_bwd(dy, xhat, rstd, g):
    dxh = dy * g
    return rstd * (dxh - jnp.mean(dxh, axis=1, keepdims=True) - xhat * jnp.mean(dxh * xhat, axis=1, keepdims=True))


def _accum(ref, i, part):
    @pl.when(i == 0)
    def _():
        ref[...] = part

    @pl.when(i > 0)
    def _():
        ref[...] += part


def _ln1_fwd(x, mix, g, b, tr=256):
    t, d = x.shape

    def body(x_ref, m_ref, g_ref, b_ref, xh_ref, rs_ref, xb_ref):
        xhat, rstd = _ln_stats(ALPHA * x_ref[...] + m_ref[...])
        xh_ref[...] = xhat
        rs_ref[...] = rstd
        xb_ref[...] = (xhat * g_ref[...] + b_ref[...]).astype(BF16)

    row = pl.BlockSpec((tr, d), lambda i: (i, 0))
    vec = pl.BlockSpec((1, d), lambda i: (0, 0))
    return pl.pallas_call(
        body, name="ln1_fwd", grid=(t // tr,),
        in_specs=[row, row, vec, vec],
        out_specs=[row, pl.BlockSpec((tr, 1), lambda i: (i, 0)), row],
        out_shape=[jax.ShapeDtypeStruct((t, d), F32), jax.ShapeDtypeStruct((t, 1), F32),
                   jax.ShapeDtypeStruct((t, d), BF16)],
        compiler_params=_params("parallel"),
    )(x, mix, g, b)


def _ln2_loss(xhat1, g1, b1, ff, target, g2, b2, tr=256):
    t, d = ff.shape

    def body(xh_ref, g1_ref, b1_ref, f_ref, t_ref, g_ref, b_ref, dz_ref, dzb_ref, dg_ref, db_ref, l_ref):
        i = pl.program_id(0)
        x1 = xh_ref[...] * g1_ref[...] + b1_ref[...]
        xhat, rstd = _ln_stats(ALPHA * x1 + f_ref[...])
        gv = g_ref[...]
        e = xhat * gv + b_ref[...] - t_ref[...]
        lsum = jnp.sum(jnp.sum(e * e, axis=1, keepdims=True), axis=0, keepdims=True) * (0.5 / d)
        dy = e * (1.0 / d)
        _accum(dg_ref, i, jnp.sum(dy * xhat, axis=0, keepdims=True))
        _accum(db_ref, i, jnp.sum(dy, axis=0, keepdims=True))
        _accum(l_ref, i, jnp.broadcast_to(lsum, l_ref.shape))
        dz = _ln_bwd(dy, xhat, rstd, gv)
        dz_ref[...] = dz
        dzb_ref[...] = dz.astype(BF16)

    row = pl.BlockSpec((tr, d), lambda i: (i, 0))
    vec = pl.BlockSpec((1, d), lambda i: (0, 0))
    return pl.pallas_call(
        body, name="ln2_loss", grid=(t // tr,),
        in_specs=[row, vec, vec, row, row, vec, vec],
        out_specs=[row, row, vec, vec, pl.BlockSpec((8, LANE), lambda i: (0, 0))],
        out_shape=[jax.ShapeDtypeStruct((t, d), F32), jax.ShapeDtypeStruct((t, d), BF16),
                   jax.ShapeDtypeStruct((1, d), F32), jax.ShapeDtypeStruct((1, d), F32),
                   jax.ShapeDtypeStruct((8, LANE), F32)],
        compiler_params=_params("arbitrary"),
    )(xhat1, g1, b1, ff, target, g2, b2)


def _ln1_bwd(dz2, dffn, xhat1, rstd1, g1, tr=256):
    t, d = dz2.shape

    def body(a_ref, f_ref, xh_ref, rs_ref, g_ref, dz_ref, dzb_ref, dg_ref, db_ref):
        i = pl.program_id(0)
        dy = ALPHA * a_ref[...] + f_ref[...]
        xhat = xh_ref[...]
        _accum(dg_ref, i, jnp.sum(dy * xhat, axis=0, keepdims=True))
        _accum(db_ref, i, jnp.sum(dy, axis=0, keepdims=True))
        dz = _ln_bwd(dy, xhat, rs_ref[...], g_ref[...])
        dz_ref[...] = dz
        dzb_ref[...] = dz.astype(BF16)

    row = pl.BlockSpec((tr, d), lambda i: (i, 0))
    vec = pl.BlockSpec((1, d), lambda i: (0, 0))
    return pl.pallas_call(
        body, name="ln1_bwd", grid=(t // tr,),
        in_specs=[row, row, row, pl.BlockSpec((tr, 1), lambda i: (i, 0)), vec],
        out_specs=[row, row, vec, vec],
        out_shape=[jax.ShapeDtypeStruct((t, d), F32), jax.ShapeDtypeStruct((t, d), BF16),
                   jax.ShapeDtypeStruct((1, d), F32), jax.ShapeDtypeStruct((1, d), F32)],
        compiler_params=_params("arbitrary"),
    )(dz2, dffn, xhat1, rstd1, g1)


def _ffn_act_fwd(hid0, w_fc, b_fc, t, dff):
    nb = dff // LANE

    def body(hv_ref, hg_ref, wv_ref, wg_ref, bv_ref, bg_ref, a_ref):
        val = _conv(hv_ref[...], wv_ref[...]) + bv_ref[...]
        gate = _conv(hg_ref[...], wg_ref[...]) + bg_ref[...]
        a_ref[...] = (gate * _sigmoid(gate) * val).astype(BF16)

    col = lambda off: pl.BlockSpec((t, LANE), lambda j: (0, j + off))
    w3 = lambda off: pl.BlockSpec((3, LANE), lambda j: (0, j + off))
    w1 = lambda off: pl.BlockSpec((1, LANE), lambda j: (0, j + off))
    return pl.pallas_call(
        body, name="ffn_act_fwd", grid=(nb,),
        in_specs=[col(0), col(nb), w3(0), w3(nb), w1(0), w1(nb)],
        out_specs=col(0),
        out_shape=jax.ShapeDtypeStruct((t, dff), BF16),
        compiler_params=_params("parallel"),
    )(hid0, hid0, w_fc, w_fc, b_fc, b_fc)


def _ffn_act_bwd(da, hid0, w_fc, b_fc, t, dff):
    nb = dff // LANE

    def body(da_ref, hv_ref, hg_ref, wv_ref, wg_ref, bv_ref, bg_ref,
             dh_ref, dwv_ref, dwg_ref, dbv_ref, dbg_ref):
        hv, hg, wv, wg = hv_ref[...], hg_ref[...], wv_ref[...], wg_ref[...]
        rv, rg = _rolled(hv), _rolled(hg)
        val = _conv(hv, wv, rv) + bv_ref[...]
        gate = _conv(hg, wg, rg) + bg_ref[...]
        sig = _sigmoid(gate)
        d = da_ref[...]
        dsig = d * sig
        dval = dsig * gate
        dgate = dsig * val * (1.0 + gate * (1.0 - sig))
        dh_ref[0] = _conv_t(dval, wv).astype(BF16)
        dh_ref[1] = _conv_t(dgate, wg).astype(BF16)
        dwv_ref[...] = _conv_dw(dval, hv, rv)
        dwg_ref[...] = _conv_dw(dgate, hg, rg)
        dbv_ref[...] = jnp.sum(dval, axis=0, keepdims=True)
        dbg_ref[...] = jnp.sum(dgate, axis=0, keepdims=True)

    col = lambda off: pl.BlockSpec((t, LANE), lambda j: (0, j + off))
    w3 = lambda off: pl.BlockSpec((3, LANE), lambda j: (0, j + off))
    w1 = lambda off: pl.BlockSpec((1, LANE), lambda j: (0, j + off))
    s3 = jax.ShapeDtypeStruct((3, dff), F32)
    s1 = jax.ShapeDtypeStruct((1, dff), F32)
    return pl.pallas_call(
        body, name="ffn_act_bwd", grid=(nb,),
        in_specs=[col(0), col(0), col(nb), w3(0), w3(nb), w1(0), w1(nb)],
        out_specs=[pl.BlockSpec((2, t, LANE), lambda j: (0, 0, j)), w3(0), w3(0), w1(0), w1(0)],
        out_shape=[jax.ShapeDtypeStruct((2, t, dff), BF16), s3, s3, s1, s1],
        compiler_params=_params("parallel"),
    )(da, hid0, hid0, w_fc, w_fc, b_fc, b_fc)


class _Ready:
    def __init__(self, **weights):
        self.weights = weights

    def begin(self, after):
        return None

    def forward(self, name, after):
        return None

    def get(self, name, after):
        return self.weights[name]


class _Kept:
    def __init__(self):
        self.grads = {}

    def start(self, name, grad):
        self.grads[name] = grad
        return None

    def relay(self, name, after):
        return None


def _behind(a, token):
    return a if token is None else a + token[0:1, 0:1].reshape((1,) * a.ndim)


def _local_step(x, target, w_in, b_gates, w_sc, gain, w_out, ln1_g, ln1_b, w_up, w_fc, b_fc, w_down, ln2_g, ln2_b,
                gx=None, wx=None):
    t, d = x.shape
    wc = d // 2
    dh = (d - wc) // NH
    wm = NH * dh
    dff = w_fc.shape[1] // 2
    if wx is None:
        wx = _Ready(w_out=w_out, w_up=w_up, w_down=w_down)
    ninp = w_in.shape[1]
    nin = 3 * wc + 4 * wm
    gate_tile = nin // LANE
    nc = t // CHUNK
    bias_tile = jnp.pad(b_gates, ((0, 0), (0, LANE - 2 * NH)))

    x_b = x.astype(BF16)
    proj = _matmul(x_b, w_in, "nn", F32, "proj", tm=1024, tn=1152, tk=d, after=wx.begin(w_in))
    y = _sconv_fwd(proj, w_sc, t, wc)
    gcol = _gates_prep(proj, bias_tile, t, gate_tile)
    grow = gcol[:, :8].T.reshape(8, nc, CHUNK).transpose(1, 0, 2)
    hval, cs, ns = _mlstm_fwd(proj, gcol, grow, t, wc, dh)
    y = _hnorm_fwd(hval, proj, gain, y, t, wc, dh)
    tok = wx.forward("w_up", wx.forward("w_out", y))
    w_out = wx.get("w_out", tok)
    mix = _matmul(y, w_out, "nn", F32, "out_proj", tm=512, tn=1024, tk=wc, a_blocked=True, after=tok)
    xhat1, rstd1, x1_b = _ln1_fwd(x, mix, ln1_g, ln1_b)
    tok = wx.forward("w_down", x1_b)
    w_up = wx.get("w_up", tok)
    wsl = w_up.shape[2]
    hid0 = _matmul(x1_b, w_up, "nn", F32, "ffn_up", tm=512, tn=wsl, tk=d, b_blocked=True, after=tok)
    act = _ffn_act_fwd(hid0, w_fc, b_fc, t, dff)
    w_down = wx.get("w_down", act)
    ff = _matmul(act, w_down, "nn", F32, "ffn_down", tm=1024, tn=512, tk=dff)
    dz2, dz2_b, d_ln2_g, d_ln2_b, loss = _ln2_loss(xhat1, ln1_g, ln1_b, ff, target, ln2_g, ln2_b)

    if gx is None:
        gx = _Kept()
    d_w_down = _matmul(act, dz2_b, "tn", BF16, "ffn_down_dw", tm=512, tn=1024, tk=t)
    d_act = _matmul(dz2_b, w_down, "nt", F32, "ffn_down_dx", tm=1024, tn=512, tk=d, after=gx.start("w_down", d_w_down))
    d_hid0, dwv, dwg, dbv, dbg = _ffn_act_bwd(d_act, hid0, w_fc, _behind(b_fc, gx.relay("w_down", d_act)), t, dff)
    d_w_fc = jnp.concatenate([dwv, dwg], axis=1)
    d_b_fc = jnp.concatenate([dbv, dbg], axis=1)
    d_w_up = _matmul(x1_b, d_hid0, "tn", BF16, "ffn_up_dw", tm=512, tn=wsl, tk=t, b_blocked=True, o_width=wsl)
    d_x1_ffn = _matmul(d_hid0, w_up, "nt", F32, "ffn_up_dx", tm=1024, tn=1024, tk=wsl, a_blocked=True, b_blocked=True,
                       after=gx.start("w_up", d_w_up))
    dz1, dz1_b, d_ln1_g, d_ln1_b = _ln1_bwd(dz2, d_x1_ffn, xhat1, rstd1, _behind(ln1_g, gx.relay("w_up", d_x1_ffn)))

    d_w_out = _matmul(y, dz1_b, "tn", BF16, "out_proj_dw", tm=512, tn=1024, tk=t, a_blocked=True)
    dy = _matmul(dz1_b, w_out, "nt", F32, "out_proj_dx", tm=512, tn=1024, tk=d, after=gx.start("w_out", d_w_out))
    dcb, dcc, dch, d_w_sc = _sconv_bwd(dy, proj, _behind(w_sc, gx.relay("w_out", dy)), t, wc)
    d_o, d_hval, d_gain = _hnorm_bwd(dy, hval, proj, gain, t, wc, dh)
    dq, dk, dv, dgate = _mlstm_bwd(proj, gcol, grow, hval, d_hval, cs, ns, t, wc, dh)
    dgt, d_b_gates = _gates_bwd(dgate, proj, bias_tile, t, gate_tile)
    pad = jnp.zeros((t, ninp - nin - LANE), BF16)
    d_proj = jnp.concatenate([dcb, dcc, dch, dq, dk, dv, d_o, dgt, pad], axis=1)
    d_w_in = _matmul(x_b, d_proj, "tn", BF16, "proj_dw", tm=512, tn=IN_SLAB, tk=t, o_width=IN_SLAB)
    grad_x = _matmul(d_proj, w_in, "nt", F32, "proj_dx", tm=512, tn=512, tk=ninp, add=dz1, add_scale=ALPHA,
                     after=gx.start("w_in", d_w_in))
    gx.relay("w_in", grad_x)

    small = dict(b_gates=d_b_gates[:, :2 * NH], w_sc_conv=d_w_sc, mh_gain=d_gain, ln1_g=d_ln1_g, ln1_b=d_ln1_b,
                 w_ffn_conv=d_w_fc, b_ffn_conv=d_b_fc, ln2_g=d_ln2_g, ln2_b=d_ln2_b)
    return loss, grad_x, small, gx


HBM = pl.BlockSpec(memory_space=pltpu.HBM)


def _place():
    return lax.axis_index("x"), lax.axis_index("y"), lax.axis_index("c")


def _index(p):
    return 4 * p[0] + 2 * p[1] + p[2]


def _all_gather(arrs, name):
    n = len(arrs)

    def body(*refs):
        ins, outs = refs[:n], refs[n:2 * n]
        send_sems, recv_sems, local_sems = refs[2 * n:]
        x, y, c = _place()
        me, sibling = (x, y, c), (x, y, 1 - c)
        chips = [(1 - x, y), (x, 1 - y), (1 - x, 1 - y)]

        def copy(a, k, block, to, own=False):
            dst = outs[a].at[_index(block)]
            return pltpu.make_async_remote_copy(
                src_ref=ins[a] if own else dst, dst_ref=dst,
                send_sem=send_sems.at[k * n + a], recv_sem=recv_sems.at[k * n + a],
                device_id=to, device_id_type=MESH)

        mine = [pltpu.make_async_copy(ins[a], outs[a].at[_index(me)], local_sems.at[a]) for a in range(n)]
        for cp in mine:
            cp.start()
        first = []
        for a in range(n):
            first.append(copy(a, 0, me, sibling, own=True))
            first += [copy(a, 1 + j, me, (*chip, c), own=True) for j, chip in enumerate(chips)]
        for cp in first:
            cp.start()
        passed = []
        for j, chip in enumerate(chips):
            for a in range(n):
                copy(a, 1 + j, (*chip, c), me).wait_recv()
                cp = copy(a, 4 + j, (*chip, c), sibling)
                cp.start()
                passed.append(cp)
        for a in range(n):
            copy(a, 0, sibling, me).wait_recv()
            for j, chip in enumerate(chips):
                copy(a, 4 + j, (*chip, 1 - c), me).wait_recv()
        for cp in first + passed:
            cp.wait_send()
        for cp in mine:
            cp.wait()

    return pl.pallas_call(
        body, name=name, in_specs=[HBM] * n, out_specs=[HBM] * n,
        out_shape=[jax.ShapeDtypeStruct((N_DEV,) + a.shape, a.dtype) for a in arrs],
        scratch_shapes=[pltpu.SemaphoreType.DMA((7 * n,)), pltpu.SemaphoreType.DMA((7 * n,)),
                        pltpu.SemaphoreType.DMA((n,))],
    )(*arrs)


SEM = pl.BlockSpec(memory_space=pltpu.SEMAPHORE)
EFFECT = pltpu.SideEffectType.DATAFLOW_SIDE_EFFECTING


def _chips(x, y):
    return [(1 - x, y), (x, 1 - y), (1 - x, 1 - y)]


N_CHIP = N_DEV // 2


def _pair_route(x, y, c):
    return [((x, y, 1 - c), 2 * q + (1 - c), q, q) for q in range(N_CHIP)]


def _chip_route(x, y, c):
    mine = 2 * x + y
    return [((*chip, c), 2 * chip[0] + chip[1], mine, 2 * chip[0] + chip[1]) for chip in _chips(x, y)]


def _exchange_pieces(g_ref, land_ref, width, tail):
    if not tail:
        return [(lambda i: g_ref.at[i], lambda s: land_ref.at[s])]
    return [(lambda i: g_ref.at[i], lambda s: land_ref.at[s, :, pl.ds(0, width)]),
            (lambda i: g_ref.at[i + 1, :, pl.ds(0, LANE)], lambda s: land_ref.at[s, :, pl.ds(width, LANE)])]


def _exchange_start(grad, route, tail, name):
    width = grad.shape[2]
    n_p = 2 if tail else 1
    n_c = len(route(0, 0, 0))
    land_shape = (N_CHIP, grad.shape[1], width + (LANE if tail else 0))

    def body(g_ref, land_ref, send_sems, recv_sems, g_thru, land_thru, token):
        for j, (peer, slab, slot, _) in enumerate(route(*_place())):
            for p, (src, dst) in enumerate(_exchange_pieces(g_ref, land_ref, width, tail)):
                pltpu.make_async_remote_copy(src_ref=src(slab), dst_ref=dst(slot), send_sem=send_sems.at[j * n_p + p],
                                             recv_sem=recv_sems.at[j * n_p + p], device_id=peer,
                                             device_id_type=MESH).start()
        token[...] = jnp.zeros_like(token)

    return pl.pallas_call(
        body, name=name,
        out_shape=(pltpu.SemaphoreType.DMA((n_c * n_p,)), pltpu.SemaphoreType.DMA((n_c * n_p,)),
                   pltpu.HBM(grad.shape, grad.dtype), pltpu.HBM(land_shape, grad.dtype),
                   jax.ShapeDtypeStruct((8, LANE), F32)),
        in_specs=(HBM, HBM), out_specs=(SEM, SEM, HBM, HBM, pl.BlockSpec(memory_space=pltpu.VMEM)),
        input_output_aliases={0: 2, 1: 3},
        compiler_params=pltpu.CompilerParams(has_side_effects=EFFECT),
    )(pltpu.with_memory_space_constraint(grad, pltpu.HBM),
      pltpu.with_memory_space_constraint(lax.empty(land_shape, grad.dtype), pltpu.HBM))


def _exchange_wait(send_sems, recv_sems, g_thru, land_thru, after, route, tail, name):
    width = g_thru.shape[2]
    n_p = 2 if tail else 1

    def body(g_ref, land_ref, send_sems, recv_sems, after_ref, g_dead, got_ref):
        for j, (peer, slab, _, slot) in enumerate(route(*_place())):
            for p, (src, dst) in enumerate(_exchange_pieces(g_ref, land_ref, width, tail)):
                cp = pltpu.make_async_remote_copy(src_ref=src(slab), dst_ref=dst(slot),
                                                  send_sem=send_sems.at[j * n_p + p], recv_sem=recv_sems.at[j * n_p + p],
                                                  device_id=peer, device_id_type=MESH)
                cp.wait_send()
                cp.wait_recv()

    return pl.pallas_call(
        body, name=name,
        out_shape=(pltpu.HBM(g_thru.shape, g_thru.dtype), pltpu.HBM(land_thru.shape, land_thru.dtype)),
        in_specs=(HBM, HBM, SEM, SEM, pl.BlockSpec(memory_space=pl.ANY)), out_specs=(HBM, HBM),
        input_output_aliases={0: 0, 1: 1},
        compiler_params=pltpu.CompilerParams(has_side_effects=EFFECT),
    )(g_thru, land_thru, send_sems, recv_sems, after)


def _pair_add(grad, pair, core, tail, name):
    width = grad.shape[2]
    rows = grad.shape[1]
    tr = _rows(rows, 256)
    total = pair.shape[2]

    def body(core_ref, *refs):
        if tail:
            g_ref, t_ref, p_ref, o_ref = refs
            o_ref[:, 0:width] = (g_ref[...].astype(F32) + p_ref[:, 0:width].astype(F32)).astype(BF16)
            o_ref[:, width:total] = (t_ref[...].astype(F32) + p_ref[:, width:total].astype(F32)).astype(BF16)
        else:
            g_ref, p_ref, o_ref = refs
            o_ref[...] = (g_ref[...].astype(F32) + p_ref[...].astype(F32)).astype(BF16)

    slab = pl.BlockSpec((None, tr, total), lambda q, i, core_ref: (q, i, 0))
    in_specs = [pl.BlockSpec((None, tr, width), lambda q, i, core_ref: (2 * q + core_ref[0], i, 0))]
    if tail:
        in_specs.append(pl.BlockSpec((None, tr, LANE), lambda q, i, core_ref: (2 * q + core_ref[0] + 1, i, 0)))
    return pl.pallas_call(
        body, name=name,
        grid_spec=pltpu.PrefetchScalarGridSpec(num_scalar_prefetch=1, grid=(N_CHIP, rows // tr),
                                               in_specs=in_specs + [slab], out_specs=slab),
        out_shape=jax.ShapeDtypeStruct(pair.shape, BF16),
        compiler_params=_params("parallel", "parallel"),
    )(core, *([grad, grad] if tail else [grad]), pair)


def _gather_start(blocks, after, name):
    n = len(blocks)
    lands = [(N_DEV,) + b.shape for b in blocks]

    def body(*refs):
        b_refs, land_refs = refs[:n], refs[n:2 * n]
        send_sems, recv_sems = refs[2 * n + 1:3 * n + 1], refs[3 * n + 1:4 * n + 1]
        token = refs[-1]
        x, y, c = _place()
        me = _index((x, y, c))
        for a in range(n):
            for k, to in enumerate([(x, y, 1 - c)] + [(*chip, c) for chip in _chips(x, y)]):
                pltpu.make_async_remote_copy(src_ref=b_refs[a], dst_ref=land_refs[a].at[me], send_sem=send_sems[a].at[k],
                                             recv_sem=recv_sems[a].at[k], device_id=to, device_id_type=MESH).start()
        token[...] = jnp.zeros_like(token)

    sems = [pltpu.SemaphoreType.DMA((4,))] * n
    out = pl.pallas_call(
        body, name=name,
        out_shape=(*sems, *sems, *[pltpu.HBM(b.shape, b.dtype) for b in blocks],
                   *[pltpu.HBM(s, b.dtype) for s, b in zip(lands, blocks)], jax.ShapeDtypeStruct((8, LANE), F32)),
        in_specs=(*[HBM] * (2 * n), pl.BlockSpec(memory_space=pl.ANY)),
        out_specs=(*[SEM] * (2 * n), *[HBM] * (2 * n), pl.BlockSpec(memory_space=pltpu.VMEM)),
        input_output_aliases={i: 2 * n + i for i in range(2 * n)},
        compiler_params=pltpu.CompilerParams(has_side_effects=EFFECT),
    )(*[pltpu.with_memory_space_constraint(b, pltpu.HBM) for b in blocks],
      *[pltpu.with_memory_space_constraint(lax.empty(s, b.dtype), pltpu.HBM) for s, b in zip(lands, blocks)], after)
    return [(out[a], out[n + a], out[2 * n + a], out[3 * n + a]) for a in range(n)], out[-1]


def _gather_forward(send_sems, recv_sems, b_thru, land_thru, after, name):
    def body(b_ref, land_ref, send_sems, recv_sems, after_ref, b_dead, land_out, send2, recv2, token):
        x, y, c = _place()
        sibling = (x, y, 1 - c)
        for k, frm in enumerate([sibling] + [(*chip, c) for chip in _chips(x, y)]):
            cp = pltpu.make_async_remote_copy(src_ref=b_ref, dst_ref=land_ref.at[_index(frm)], send_sem=send_sems.at[k],
                                              recv_sem=recv_sems.at[k], device_id=frm, device_id_type=MESH)
            cp.wait_send()
            cp.wait_recv()
        for j, chip in enumerate(_chips(x, y)):
            slot = land_ref.at[_index((*chip, c))]
            pltpu.make_async_remote_copy(src_ref=slot, dst_ref=slot, send_sem=send2.at[j], recv_sem=recv2.at[j],
                                         device_id=sibling, device_id_type=MESH).start()
        token[...] = jnp.zeros_like(token)

    return pl.pallas_call(
        body, name=name,
        out_shape=(pltpu.HBM(b_thru.shape, b_thru.dtype), pltpu.HBM(land_thru.shape, land_thru.dtype),
                   pltpu.SemaphoreType.DMA((3,)), pltpu.SemaphoreType.DMA((3,)), jax.ShapeDtypeStruct((8, LANE), F32)),
        in_specs=(HBM, HBM, SEM, SEM, pl.BlockSpec(memory_space=pl.ANY)),
        out_specs=(HBM, HBM, SEM, SEM, pl.BlockSpec(memory_space=pltpu.VMEM)),
        input_output_aliases={0: 0, 1: 1},
        compiler_params=pltpu.CompilerParams(has_side_effects=EFFECT),
    )(b_thru, land_thru, send_sems, recv_sems, after)


def _gather_finish(land_thru, send2, recv2, after, name):
    def body(land_ref, send2, recv2, after_ref, land_out):
        x, y, c = _place()
        for j, chip in enumerate(_chips(x, y)):
            cp = pltpu.make_async_remote_copy(src_ref=land_ref.at[_index((*chip, c))],
                                              dst_ref=land_ref.at[_index((*chip, 1 - c))], send_sem=send2.at[j],
                                              recv_sem=recv2.at[j], device_id=(x, y, 1 - c), device_id_type=MESH)
            cp.wait_send()
            cp.wait_recv()

    return pl.pallas_call(
        body, name=name, out_shape=pltpu.HBM(land_thru.shape, land_thru.dtype),
        in_specs=(HBM, SEM, SEM, pl.BlockSpec(memory_space=pl.ANY)), out_specs=HBM,
        input_output_aliases={0: 0},
        compiler_params=pltpu.CompilerParams(has_side_effects=EFFECT),
    )(land_thru, send2, recv2, after)


class _Gathering:
    def __init__(self, blocks, me, after):
        started, self.token = _gather_start(list(blocks.values()), after, "gather1")
        self.me, self.state = me, dict(zip(blocks, started))

    def begin(self, after):
        return self.token

    def forward(self, name, after):
        *self.state[name], token = _gather_forward(*self.state[name], after, "gather2_" + name)
        return token

    def get(self, name, after):
        block, land, send2, recv2 = self.state[name]
        land = _gather_finish(land, send2, recv2, after, "gather3_" + name)
        land = lax.dynamic_update_index_in_dim(land, block[None], self.me, 0)
        return land if name not in ("w_out", "w_down") else land.reshape(-1, land.shape[2])


class _Reducing:
    def __init__(self, core, chip):
        self.core, self.chip, self.state, self.token = core, chip, {}, None

    def start(self, name, grad):
        g = grad if grad.ndim == 3 else grad.reshape(N_DEV, grad.shape[0] // N_DEV, grad.shape[1])
        *self.state[name], token = _exchange_start(g, _pair_route, name == "w_in", "pair_send_" + name)
        return token

    def relay(self, name, after):
        tail = name == "w_in"
        grad, pair = _exchange_wait(*self.state[name], after, _pair_route, tail, "pair_recv_" + name)
        total = _pair_add(grad, pair, self.core, tail, "pair_add_" + name)
        *self.state[name], self.token = _exchange_start(total, _chip_route, False, "chip_send_" + name)
        return self.token

    def finish(self, name, after):
        total, land = _exchange_wait(*self.state[name], after, _chip_route, False, "chip_recv_" + name)
        own = lax.dynamic_index_in_dim(total, self.chip, 0, keepdims=True)
        return lax.dynamic_update_index_in_dim(land, own, self.chip, 0)


def _assemble_w_in(g, ninp):
    _, d, pw = g.shape
    per = IN_SLAB // LANE
    assert ninp == (N_DEV + 1) * IN_SLAB and pw == IN_SLAB + LANE
    tr = _rows(d, 512)

    def body(a_ref, b_ref, o_ref):
        s = pl.program_id(0)
        a = a_ref[...]
        o_ref[:, 0:LANE] = (jnp.where(s < N_DEV, a[:, 0:LANE], jnp.zeros_like(b_ref))
                            + jnp.where(s > 0, b_ref[...], jnp.zeros_like(b_ref)))
        o_ref[:, LANE:IN_SLAB] = jnp.where(s < N_DEV, a[:, LANE:IN_SLAB], jnp.zeros_like(a[:, LANE:IN_SLAB]))

    return pl.pallas_call(
        body, name="assemble_w_in", grid=(N_DEV + 1, d // tr),
        in_specs=[pl.BlockSpec((None, tr, pw), lambda s, i: (jnp.minimum(s, N_DEV - 1), i, 0)),
                  pl.BlockSpec((None, tr, LANE), lambda s, i: (jnp.maximum(s, 1) - 1, i, per))],
        out_specs=pl.BlockSpec((tr, IN_SLAB), lambda s, i: (i, s)),
        out_shape=jax.ShapeDtypeStruct((d, ninp), g.dtype),
        compiler_params=_params("parallel", "parallel"),
    )(g, g)


def _rows(n, want):
    t = min(n, want)
    t -= t % 16
    while n % t:
        t -= 16
    return t


def _adam_math(w, g, m, v):
    m2 = ADAM_B1 * m + (1.0 - ADAM_B1) * g
    v2 = ADAM_B2 * v + (1.0 - ADAM_B2) * (g * g)
    m_hat = m2 / (1.0 - ADAM_B1 ** ADAM_STEP)
    v_hat = v2 / (1.0 - ADAM_B2 ** ADAM_STEP)
    return -ADAM_LR * (m_hat / (jnp.sqrt(v_hat) + ADAM_EPS) + ADAM_WD * w), m2, v2


def _slot_sum(r_ref):
    acc = r_ref[0].astype(F32)
    for i in range(1, r_ref.shape[0]):
        acc = acc + r_ref[i].astype(F32)
    return acc


def _shift_w_in(w, pw):
    _, d, ws = w.shape
    tr = _rows(d, 512)

    def body(w_ref, o_ref, wide):
        wide[...] = jnp.zeros_like(wide)
        wide[:, 0:ws] = w_ref[...]
        o_ref[...] = pltpu.roll(wide[...], _index(_place()), 1).astype(BF16)

    return pl.pallas_call(
        body, name="shift_w_in", grid=(d // tr,),
        in_specs=[pl.BlockSpec((None, tr, ws), lambda i: (0, i, 0))],
        out_specs=pl.BlockSpec((tr, pw), lambda i: (i, 0)),
        out_shape=jax.ShapeDtypeStruct((d, pw), BF16),
        scratch_shapes=[pltpu.VMEM((tr, pw), F32)], compiler_params=_params("parallel"),
    )(w)


def _sum_adamw_shifted(r, w, m, v, name, tr=128):
    _, d, pw = r.shape
    ws = w.shape[2]
    tr = _rows(d, tr)

    def body(r_ref, w_ref, m_ref, v_ref, g_ref, d_ref, m2_ref, v2_ref, wide):
        wide[...] = pltpu.roll(_slot_sum(r_ref), lax.rem(pw - _index(_place()), pw), 1)
        g = wide[:, 0:ws]
        g_ref[...] = g
        d_ref[...], m2_ref[...], v2_ref[...] = _adam_math(w_ref[...], g, m_ref[...], v_ref[...])

    blk = pl.BlockSpec((None, tr, ws), lambda i: (0, i, 0))
    out = jax.ShapeDtypeStruct(w.shape, F32)
    return pl.pallas_call(
        body, name=name, grid=(d // tr,),
        in_specs=[pl.BlockSpec((r.shape[0], tr, pw), lambda i: (0, i, 0)), blk, blk, blk],
        out_specs=[blk] * 4, out_shape=[out] * 4,
        scratch_shapes=[pltpu.VMEM((tr, pw), F32)], compiler_params=_params("parallel"),
    )(r, w, m, v)


def _sum_slots(r, name, tr=128):
    _, rows, cols = r.shape
    tr = _rows(rows, tr)

    def body(r_ref, g_ref):
        g_ref[...] = _slot_sum(r_ref)

    return pl.pallas_call(
        body, name=name, grid=(rows // tr,),
        in_specs=[pl.BlockSpec((r.shape[0], tr, cols), lambda i: (0, i, 0))],
        out_specs=pl.BlockSpec((tr, cols), lambda i: (i, 0)),
        out_shape=jax.ShapeDtypeStruct((rows, cols), F32),
        compiler_params=_params("parallel"),
    )(r)


def _adamw(w, g, m, v, name, tr=256):
    rows, cols = w.shape
    tr = _rows(rows, tr)

    def body(w_ref, g_ref, m_ref, v_ref, d_ref, m2_ref, v2_ref):
        d_ref[...], m2_ref[...], v2_ref[...] = _adam_math(w_ref[...], g_ref[...], m_ref[...], v_ref[...])

    blk = pl.BlockSpec((tr, cols), lambda i: (i, 0))
    out = jax.ShapeDtypeStruct((rows, cols), F32)
    return pl.pallas_call(
        body, name=name, grid=(rows // tr,), in_specs=[blk] * 4, out_specs=[blk] * 3, out_shape=[out] * 3,
        compiler_params=_params("parallel"),
    )(w, g, m, v)


def _sum_adamw(r, w, m, v, name, tr=128):
    rows, cols = w.shape
    tr = _rows(rows, tr)

    def body(r_ref, w_ref, m_ref, v_ref, g_ref, d_ref, m2_ref, v2_ref):
        g = _slot_sum(r_ref)
        g_ref[...] = g
        d_ref[...], m2_ref[...], v2_ref[...] = _adam_math(w_ref[...], g, m_ref[...], v_ref[...])

    blk = pl.BlockSpec((tr, cols), lambda i: (i, 0))
    out = jax.ShapeDtypeStruct((rows, cols), F32)
    return pl.pallas_call(
        body, name=name, grid=(rows // tr,),
        in_specs=[pl.BlockSpec((r.shape[0], tr, cols), lambda i: (0, i, 0)), blk, blk, blk],
        out_specs=[blk] * 4, out_shape=[out] * 4,
        compiler_params=_params("parallel"),
    )(r, w, m, v)


def _pack(pieces, sizes):
    flat = [jnp.pad(p.reshape(-1).astype(F32), (0, s - p.size)) for p, s in zip(pieces, sizes)]
    total = sum(sizes)
    padded = -(-total // (16 * LANE)) * (16 * LANE)
    return jnp.pad(jnp.concatenate(flat), (0, padded - total)).reshape(-1, LANE)


def _unpack(packed, shapes, sizes):
    flat = packed.reshape(-1)
    out, off = [], 0
    for shp, s in zip(shapes, sizes):
        n = 1
        for k in shp:
            n *= k
        out.append(flat[off:off + n].reshape(shp))
        off += s
    return out


def _lanes(n):
    return -(-n // LANE) * LANE


WEIGHTS = ("w_in", "b_gates", "w_sc_conv", "mh_gain", "w_out", "ln1_g", "ln1_b", "w_up", "w_ffn_conv", "b_ffn_conv",
           "w_down", "ln2_g", "ln2_b")
BIG = ("w_in", "w_out", "w_up", "w_down")
SMALL = tuple(n for n in WEIGHTS if n not in BIG)


def kernel(x, w_in, b_gates, w_sc_conv, mh_gain, w_out, ln1_g, ln1_b, w_up, w_ffn_conv, b_ffn_conv, w_down, ln2_g, ln2_b, loss_target, m_w_in, m_b_gates, m_w_sc_conv, m_mh_gain, m_w_out, m_ln1_g, m_ln1_b, m_w_up, m_w_ffn_conv, m_b_ffn_conv, m_w_down, m_ln2_g, m_ln2_b, v_w_in, v_b_gates, v_w_sc_conv, v_mh_gain, v_w_out, v_ln1_g, v_ln1_b, v_w_up, v_w_ffn_conv, v_b_ffn_conv, v_w_down, v_ln2_g, v_ln2_b):
    w = dict(zip(WEIGHTS, (w_in, b_gates, w_sc_conv, mh_gain, w_out, ln1_g, ln1_b, w_up, w_ffn_conv, b_ffn_conv,
                           w_down, ln2_g, ln2_b)))
    m = dict(zip(WEIGHTS, (m_w_in, m_b_gates, m_w_sc_conv, m_mh_gain, m_w_out, m_ln1_g, m_ln1_b, m_w_up,
                           m_w_ffn_conv, m_b_ffn_conv, m_w_down, m_ln2_g, m_ln2_b)))
    v = dict(zip(WEIGHTS, (v_w_in, v_b_gates, v_w_sc_conv, v_mh_gain, v_w_out, v_ln1_g, v_ln1_b, v_w_up,
                           v_w_ffn_conv, v_b_ffn_conv, v_w_down, v_ln2_g, v_ln2_b)))
    me = _index(_place())
    d = x.shape[2]
    ws_in = w_in.shape[2]
    assert ws_in == IN_SLAB + 1 and N_DEV <= LANE, w_in.shape
    ninp = (N_DEV + 1) * IN_SLAB
    ws_sc, ws_fc = w_sc_conv.shape[2], w_ffn_conv.shape[2]

    w_in_shift = _shift_w_in(w_in, IN_SLAB + LANE)
    taps8 = lambda a: jnp.pad(a[0], ((0, 5), (0, 0)))
    blocks = dict(w_in=w_in_shift, w_sc=taps8(w_sc_conv), w_fc=taps8(w_ffn_conv),
                  **{n: w[n][0].astype(BF16) for n in ("w_out", "w_up", "w_down")})
    wx = _Gathering(blocks, me, w_in_shift)
    token = wx.begin(None)
    for n in ("w_in", "w_sc", "w_fc"):
        token = wx.forward(n, token)
    g_in, g_sc, g_fc = (wx.get(n, token) for n in ("w_in", "w_sc", "w_fc"))
    w_in_full = _assemble_w_in(g_in, ninp)
    w_sc_full = g_sc[:, :3].transpose(1, 0, 2).reshape(3, N_DEV * ws_sc)
    w_fc_full = g_fc[:, :3].transpose(1, 0, 2).reshape(3, N_DEV * ws_fc)

    xi, yi, ci = _place()
    gx = _Reducing(jnp.reshape(ci, (1,)).astype(jnp.int32), 2 * xi + yi)
    loss_t, grad_x, small, _ = _local_step(
        x[0], loss_target[0], w_in_full, b_gates, w_sc_full, mh_gain, None, ln1_g, ln1_b, None,
        w_fc_full, b_ffn_conv, None, ln2_g, ln2_b, gx=gx, wx=wx)

    grads, deltas, new_m, new_v = {}, {}, {}, {}
    for name in ("w_down", "w_up", "w_out"):
        grads[name], deltas[name], new_m[name], new_v[name] = _sum_adamw(
            gx.finish(name, gx.token), w[name][0], m[name][0], v[name][0], "adamw_" + name)

    names = ("loss",) + SMALL
    parts = dict(small, loss=loss_t[0, :1])
    sizes = [_lanes(parts[n].size) for n in names]
    (g_small,) = _all_gather([_pack([parts[n] for n in names], sizes)], "gather_small")
    summed = _unpack(_sum_slots(g_small, "sum_small", tr=g_small.shape[1]), [parts[n].shape for n in names], sizes)
    full = dict(zip(names, summed))
    full["w_sc_conv"] = lax.dynamic_slice(full["w_sc_conv"], (0, me * ws_sc), (3, ws_sc))
    full["w_ffn_conv"] = lax.dynamic_slice(full["w_ffn_conv"], (0, me * ws_fc), (3, ws_fc))
    for n in SMALL:
        grads[n] = full[n].reshape(w[n].shape)
    sizes = [_lanes(w[n].size) for n in SMALL]
    shapes = [w[n].shape for n in SMALL]
    packed = [_pack([t[n] for n in SMALL], sizes) for t in (w, grads, m, v)]
    small_out = _adamw(*packed, "adamw_small")
    for res, t in zip(small_out, (deltas, new_m, new_v)):
        t.update(zip(SMALL, _unpack(res, shapes, sizes)))

    done = sum(t[0:1, 0:1] for t in (deltas["w_down"], deltas["w_up"], deltas["w_out"], small_out[0]))
    grads["w_in"], deltas["w_in"], new_m["w_in"], new_v["w_in"] = _sum_adamw_shifted(
        gx.finish("w_in", done), w_in, m_w_in, v_w_in, "adamw_w_in")

    big = lambda t: {n: (t[n].reshape(w[n].shape) if n in BIG else t[n]) for n in WEIGHTS}
    grads, deltas, new_m, new_v = big(grads), big(deltas), big(new_m), big(new_v)
    return (full["loss"].reshape(()), grad_x[None], *[grads[n] for n in WEIGHTS], *[deltas[n] for n in WEIGHTS],
            *[new_m[n] for n in WEIGHTS], *[new_v[n] for n in WEIGHTS])
```

```python
import functools

import jax
import jax.numpy as jnp
from jax import lax
from jax.experimental import pallas as pl
from jax.experimental.pallas import tpu as pltpu

F32 = jnp.float32
BF16 = jnp.bfloat16
MESH = pl.DeviceIdType.MESH

N_DEV = 8
NH = 4
CHUNK = 64
LN_EPS = 1e-5
HN_EPS = 1e-6
ALPHA = 2.0 ** 0.25
LANE = 128
IN_SLAB = 7 * LANE
VMEM_LIMIT = 56 * 1024 * 1024
ADAM_LR, ADAM_B1, ADAM_B2, ADAM_EPS, ADAM_WD, ADAM_STEP = 0.001, 0.9, 0.999, 1e-08, 0.01, 10

_NN = (((1,), (0,)), ((), ()))
_NT = (((1,), (1,)), ((), ()))
_TN = (((0,), (0,)), ((), ()))


def _dot(a, b, dn=_NN):
    return lax.dot_general(a, b, dn, preferred_element_type=F32)


def _params(*sem):
    return pltpu.CompilerParams(dimension_semantics=sem if sem else None, vmem_limit_bytes=VMEM_LIMIT)


def _iota(shape, axis):
    return lax.broadcasted_iota(jnp.int32, shape, axis)


def _fit(n, want):
    if n <= want:
        return n
    t = want - want % LANE
    while n % t:
        t -= LANE
    return t


def _matmul(a, b, mode, out_dtype, name, tm=1024, tn=512, tk=1024, add=None, add_scale=1.0,
            a_blocked=False, b_blocked=False, o_width=None, after=None):
    if a_blocked:
        na, a_rows, wa = a.shape
        kd, m = (a_rows, na * wa) if mode == "tn" else (na * wa, a_rows)
    elif mode == "tn":
        kd, m = a.shape
    else:
        m, kd = a.shape
    if b_blocked:
        nb, rows, w = b.shape
        n = rows if mode == "nt" else nb * w
        assert (nb * w if mode == "nt" else rows) == kd, (name, b.shape, kd)
    else:
        n = b.shape[0] if mode == "nt" else b.shape[1]
    tm, tn, tk = _fit(m, tm), _fit(n, tn), _fit(kd, tk)
    if a_blocked and mode == "tn":
        tm = _fit(wa, tm)
    if a_blocked and mode != "tn":
        tk = _fit(wa, tk)
    if b_blocked and mode != "nt":
        tn = _fit(w, tn)
    if b_blocked and mode == "nt":
        tk = _fit(w, tk)
    if o_width is not None:
        tn = _fit(o_width, tn)
    assert m % tm == 0 and n % tn == 0 and kd % tk == 0, (name, m, n, kd, tm, tn, tk)
    assert not (a_blocked and mode != "tn" and wa % tk) and not (b_blocked and mode == "nt" and w % tk), (name, tk)
    nk = kd // tk
    dn = {"nn": _NN, "nt": _NT, "tn": _TN}[mode]
    if a_blocked and mode == "tn":
        a_per = wa // tm
        a_spec = pl.BlockSpec((None, tk, tm), lambda i, j, k: (i // a_per, k, i % a_per))
    elif a_blocked:
        a_per = wa // tk
        a_spec = pl.BlockSpec((None, tm, tk), lambda i, j, k: (k // a_per, i, k % a_per))
    elif mode == "tn":
        a_spec = pl.BlockSpec((tk, tm), lambda i, j, k: (k, i))
    else:
        a_spec = pl.BlockSpec((tm, tk), lambda i, j, k: (i, k))
    if b_blocked and mode != "nt":
        per = w // tn
        b_spec = pl.BlockSpec((None, tk, tn), lambda i, j, k: (j // per, k, j % per))
    elif b_blocked:
        per = w // tk
        b_spec = pl.BlockSpec((None, tn, tk), lambda i, j, k: (k // per, j, k % per))
    elif mode == "nt":
        b_spec = pl.BlockSpec((tn, tk), lambda i, j, k: (j, k))
    else:
        b_spec = pl.BlockSpec((tk, tn), lambda i, j, k: (k, j))
    if o_width is None:
        o_spec = pl.BlockSpec((tm, tn), lambda i, j, k: (i, j))
        o_shape = (m, n)
    else:
        oper = o_width // tn
        o_spec = pl.BlockSpec((None, tm, tn), lambda i, j, k: (j // oper, i, j % oper))
        o_shape = (n // o_width, m, o_width)
    has_add = add is not None
    n_in = 2 + has_add + (after is not None)
    in_place = nk > 1 and out_dtype == F32

    def body(*refs):
        a_ref, b_ref = refs[:2]
        add_ref = refs[2] if has_add else None
        o_ref = refs[n_in]

        def finish(r):
            if has_add:
                r = r + add_scale * add_ref[...]
            o_ref[...] = r.astype(out_dtype)

        if nk == 1:
            finish(_dot(a_ref[...], b_ref[...], dn))
        else:
            acc = o_ref if in_place else refs[-1]
            k = pl.program_id(2)

            @pl.when(k == 0)
            def _():
                acc[...] = _dot(a_ref[...], b_ref[...], dn)

            @pl.when(k > 0)
            def _():
                acc[...] += _dot(a_ref[...], b_ref[...], dn)

            if not (in_place and not has_add):
                @pl.when(k == nk - 1)
                def _():
                    finish(acc[...])

    in_specs = [a_spec, b_spec] + ([pl.BlockSpec((tm, tn), lambda i, j, k: (i, j))] if has_add else [])
    args = (a, b) + ((add,) if has_add else ())
    if after is not None:
        in_specs.append(pl.BlockSpec(memory_space=pl.ANY))
        args += (after,)
    return pl.pallas_call(
        body, name=name, grid=(m // tm, n // tn, nk),
        in_specs=in_specs, out_specs=o_spec,
        out_shape=jax.ShapeDtypeStruct(o_shape, out_dtype),
        scratch_shapes=[pltpu.VMEM((tm, tn), F32)] if nk > 1 and not in_place else [],
        compiler_params=_params("parallel", "parallel", "arbitrary"),
    )(*args)


def _shift_down(u, s):
    return jnp.where(_iota(u.shape, 0) >= s, pltpu.roll(u, s, 0), 0.0)


def _shift_up(u, s):
    t = u.shape[0]
    return jnp.where(_iota(u.shape, 0) < t - s, pltpu.roll(u, t - s, 0), 0.0)


SLAB = 8


def _rolled(u):
    return pltpu.roll(u, 2, 0), pltpu.roll(u, 1, 0)


def _conv(u, w, rolled=None):
    u2, u1 = _rolled(u) if rolled is None else rolled
    raw = w[0:1] * u2 + w[1:2] * u1 + w[2:3] * u
    head = u[0:SLAB]
    mended = w[0:1] * _shift_down(head, 2) + w[1:2] * _shift_down(head, 1) + w[2:3] * head
    return jnp.concatenate([mended, raw[SLAB:]], axis=0)


def _conv_t(dy, w):
    t = dy.shape[0]
    raw = w[2:3] * dy + w[1:2] * pltpu.roll(dy, t - 1, 0) + w[0:1] * pltpu.roll(dy, t - 2, 0)
    tail = dy[t - SLAB:]
    mended = w[2:3] * tail + w[1:2] * _shift_up(tail, 1) + w[0:1] * _shift_up(tail, 2)
    return jnp.concatenate([raw[:t - SLAB], mended], axis=0)


def _conv_dw(dy, u, rolled=None):
    t = dy.shape[0]
    u2, u1 = _rolled(u) if rolled is None else rolled
    head, tail = dy[0:SLAB], u[t - SLAB:]
    r = _iota(head.shape, 0)
    wrap2 = jnp.sum(jnp.where(r < 2, head * pltpu.roll(tail, 2, 0), 0.0), axis=0, keepdims=True)
    wrap1 = jnp.sum(jnp.where(r < 1, head * pltpu.roll(tail, 1, 0), 0.0), axis=0, keepdims=True)
    d0 = jnp.sum(dy * u2, axis=0, keepdims=True) - wrap2
    d1 = jnp.sum(dy * u1, axis=0, keepdims=True) - wrap1
    d2 = jnp.sum(dy * u, axis=0, keepdims=True)
    r3 = _iota((3, dy.shape[1]), 0)
    return jnp.where(r3 == 0, d0, jnp.where(r3 == 1, d1, d2))


def _sigmoid(x):
    return 0.5 * jnp.tanh(0.5 * x) + 0.5


def _sconv_fwd(proj, w_sc, t, wc):
    nb = wc // LANE

    def body(cb_ref, cc_ref, ch_ref, w_ref, y_ref):
        u = cc_ref[...] * ch_ref[...]
        y_ref[...] = (cb_ref[...] * _conv(u, w_ref[...])).astype(BF16)

    col = lambda off: pl.BlockSpec((t, LANE), lambda j: (0, j + off))
    return pl.pallas_call(
        body, name="sconv_fwd", grid=(nb,),
        in_specs=[col(0), col(nb), col(2 * nb), pl.BlockSpec((3, LANE), lambda j: (0, j))],
        out_specs=pl.BlockSpec((None, t, LANE), lambda j: (0, 0, j)),
        out_shape=jax.ShapeDtypeStruct((2, t, wc), BF16),
        compiler_params=_params("parallel"),
    )(proj, proj, proj, w_sc)


def _sconv_bwd(dy, proj, w_sc, t, wc):
    nb = wc // LANE

    def body(dy_ref, cb_ref, cc_ref, ch_ref, w_ref, dcb_ref, dcc_ref, dch_ref, dw_ref):
        cc, ch, w, d = cc_ref[...], ch_ref[...], w_ref[...], dy_ref[...]
        u = cc * ch
        ru = _rolled(u)
        dcb_ref[...] = (d * _conv(u, w, ru)).astype(BF16)
        dcu = d * cb_ref[...]
        dw_ref[...] = _conv_dw(dcu, u, ru)
        du = _conv_t(dcu, w)
        dcc_ref[...] = (du * ch).astype(BF16)
        dch_ref[...] = (du * cc).astype(BF16)

    col = lambda off: pl.BlockSpec((t, LANE), lambda j: (0, j + off))
    act = jax.ShapeDtypeStruct((t, wc), BF16)
    return pl.pallas_call(
        body, name="sconv_bwd", grid=(nb,),
        in_specs=[col(0), col(0), col(nb), col(2 * nb), pl.BlockSpec((3, LANE), lambda j: (0, j))],
        out_specs=[col(0), col(0), col(0), pl.BlockSpec((3, LANE), lambda j: (0, j))],
        out_shape=[act, act, act, jax.ShapeDtypeStruct((3, wc), F32)],
        compiler_params=_params("parallel"),
    )(dy, proj, proj, proj, w_sc)


def _gates_prep(proj, bias_tile, t, gate_tile):
    def body(g_ref, b_ref, o_ref):
        g = g_ref[...] + b_ref[...]
        lane = _iota(g.shape, 1)
        is_f = (lane >= NH) & (lane < 2 * NH)
        lf = jnp.minimum(g, 0.0) - jnp.log(1.0 + jnp.exp(-jnp.abs(g)))
        c = jnp.where(is_f, lf, 0.0)
        r = _iota(g.shape, 0) % CHUNK
        s = 1
        while s < CHUNK:
            c = c + jnp.where(r >= s, pltpu.roll(c, s, 0), 0.0)
            s *= 2
        o_ref[...] = jnp.where(is_f, c, jnp.where(lane < NH, g, 0.0))

    return pl.pallas_call(
        body, name="gates_prep", grid=(1,),
        in_specs=[pl.BlockSpec((t, LANE), lambda i: (0, gate_tile)), pl.BlockSpec((1, LANE), lambda i: (0, 0))],
        out_specs=pl.BlockSpec((t, LANE), lambda i: (0, 0)),
        out_shape=jax.ShapeDtypeStruct((t, LANE), F32),
        compiler_params=_params("arbitrary"),
    )(proj, bias_tile)


def _gates_bwd(dgate, proj, bias_tile, t, gate_tile):
    def body(dg_ref, g_ref, b_ref, o_ref, s_ref):
        g = g_ref[...] + b_ref[...]
        lane = _iota(g.shape, 1)
        r = _iota(g.shape, 0) % CHUNK
        dsig = 1.0 - _sigmoid(g)
        out = jnp.zeros(g.shape, F32)
        for h in range(NH):
            d = dg_ref[h]
            c = d
            s = 1
            while s < CHUNK:
                c = c + jnp.where(r + s < CHUNK, pltpu.roll(c, t - s, 0), 0.0)
                s *= 2
            di = jnp.broadcast_to(d[:, 0:1], g.shape)
            db = jnp.broadcast_to(c[:, 1:2], g.shape)
            out = out + jnp.where(lane == h, di, 0.0) + jnp.where(lane == NH + h, db * dsig, 0.0)
        o_ref[...] = out.astype(BF16)
        s_ref[...] = jnp.sum(out, axis=0, keepdims=True)

    return pl.pallas_call(
        body, name="gates_bwd", grid=(1,),
        in_specs=[pl.BlockSpec((NH, t, LANE), lambda i: (0, 0, 0)),
                  pl.BlockSpec((t, LANE), lambda i: (0, gate_tile)), pl.BlockSpec((1, LANE), lambda i: (0, 0))],
        out_specs=[pl.BlockSpec((t, LANE), lambda i: (0, 0)), pl.BlockSpec((1, LANE), lambda i: (0, 0))],
        out_shape=[jax.ShapeDtypeStruct((t, LANE), BF16), jax.ShapeDtypeStruct((1, LANE), F32)],
        compiler_params=_params("arbitrary"),
    )(dgate, proj, bias_tile)


def _chunk_gates(gc, gr, h, mprev):
    L = CHUNK
    icol, bcol = gc[:, h:h + 1], gc[:, h + NH:h + NH + 1]
    irow, brow = gr[h:h + 1, :], gr[h + NH:h + NH + 1, :]
    tri = _iota((L, L), 0) >= _iota((L, L), 1)
    log_d = jnp.where(tri, bcol - brow + irow, -jnp.inf)
    inter = bcol + mprev
    mt = jnp.maximum(inter, jnp.max(log_d, axis=1, keepdims=True))
    dw = jnp.exp(log_d - mt)
    iw = jnp.exp(inter - mt)
    g = brow[:, L - 1:L]
    wlog_col = g - bcol + icol
    wlog_row = g - brow + irow
    mnew = jnp.maximum(g + mprev, jnp.max(wlog_row, axis=1, keepdims=True))
    wcol = jnp.exp(wlog_col - mnew)
    decay = jnp.exp(g + mprev - mnew)
    return dw, iw, mt, wcol, decay, mnew


def _mlstm_fwd(proj, gcol, grow, t, wc, dh):
    nc = t // CHUNK
    wm = NH * dh
    assert wc == wm, (wc, wm)
    qoff = 3 * wc // wm
    scale = dh ** -0.5

    def body(q_ref, k_ref, v_ref, gc_ref, gr_ref, h_ref, cs_ref, ns_ref, c_s, n_s, m_s):
        @pl.when(pl.program_id(0) == 0)
        def _():
            c_s[...] = jnp.zeros_like(c_s)
            n_s[...] = jnp.zeros_like(n_s)
            m_s[...] = jnp.zeros_like(m_s)

        gc, gr = gc_ref[...], gr_ref[0]
        for h in range(NH):
            cols = slice(h * dh, (h + 1) * dh)
            mprev = m_s[h, 0:1, 0:1]
            cprev = c_s[h]
            n8 = n_s[h]
            nprev = n8[0:1]
            cs_ref[h] = cprev
            ns_ref[h] = jnp.where(_iota(n8.shape, 0) == 1, mprev, n8)

            dw, iw, mt, wcol, decay, mnew = _chunk_gates(gc, gr, h, mprev)
            qs = q_ref[:, cols] * scale
            k = k_ref[:, cols]
            qs_b, k_b, v_b = qs.astype(BF16), k.astype(BF16), v_ref[:, cols].astype(BF16)
            s = _dot(qs_b, k_b, _NT) * dw
            num = _dot(s.astype(BF16), v_b) + iw * _dot(qs_b, cprev.astype(BF16))
            den = jnp.sum(s, axis=1, keepdims=True) + iw * jnp.sum(qs * nprev, axis=1, keepdims=True)
            h_ref[:, cols] = num / jnp.maximum(jnp.abs(den), jnp.exp(-mt))

            wk = wcol * k
            c_s[h] = decay * cprev + _dot(wk.astype(BF16), v_b, _TN)
            n_s[h] = decay * n8 + jnp.sum(wk, axis=0, keepdims=True)
            m_s[h] = jnp.broadcast_to(mnew, m_s.shape[1:])

    grp = lambda off: pl.BlockSpec((CHUNK, wm), lambda c: (c, qoff + off))
    return pl.pallas_call(
        body, name="mlstm_fwd", grid=(nc,),
        in_specs=[grp(0), grp(1), grp(2),
                  pl.BlockSpec((CHUNK, LANE), lambda c: (c, 0)),
                  pl.BlockSpec((1, 8, CHUNK), lambda c: (c, 0, 0))],
        out_specs=[pl.BlockSpec((CHUNK, wm), lambda c: (c, 0)),
                   pl.BlockSpec((NH, None, dh, dh), lambda c: (0, c, 0, 0)),
                   pl.BlockSpec((NH, None, 8, dh), lambda c: (0, c, 0, 0))],
        out_shape=[jax.ShapeDtypeStruct((t, wm), F32),
                   jax.ShapeDtypeStruct((NH, nc, dh, dh), F32),
                   jax.ShapeDtypeStruct((NH, nc, 8, dh), F32)],
        scratch_shapes=[pltpu.VMEM((NH, dh, dh), F32), pltpu.VMEM((NH, 8, dh), F32), pltpu.VMEM((NH, 8, LANE), F32)],
        compiler_params=_params("arbitrary"),
    )(proj, proj, proj, gcol, grow)


def _mlstm_bwd(proj, gcol, grow, hval, dh_in, cs, ns, t, wc, dh):
    nc = t // CHUNK
    wm = NH * dh
    assert wc == wm, (wc, wm)
    qoff = 3 * wc // wm
    scale = dh ** -0.5
    L = CHUNK

    def body(q_ref, k_ref, v_ref, gc_ref, gr_ref, h_ref, dh_ref, cs_ref, ns_ref,
             dq_ref, dk_ref, dv_ref, dg_ref, dc_s, dn_s):
        @pl.when(pl.program_id(0) == 0)
        def _():
            dc_s[...] = jnp.zeros_like(dc_s)
            dn_s[...] = jnp.zeros_like(dn_s)

        gc, gr = gc_ref[...], gr_ref[0]
        eye = _iota((L, L), 0) == _iota((L, L), 1)
        lane = _iota((L, LANE), 1)
        last = _iota((L, 1), 0) == L - 1
        for h in range(NH):
            cols = slice(h * dh, (h + 1) * dh)
            ns8 = ns_ref[h]
            nprev = ns8[0:1]
            mprev = ns8[1:2, 0:1]
            cprev = cs_ref[h]
            dcn = dc_s[h]
            dn8 = dn_s[h]
            dnn = dn8[0:1]

            dw, iw, mt, wcol, decay, _ = _chunk_gates(gc, gr, h, mprev)
            qs = q_ref[:, cols] * scale
            k = k_ref[:, cols]
            qs_b, k_b, v_b = qs.astype(BF16), k.astype(BF16), v_ref[:, cols].astype(BF16)
            qk = _dot(qs_b, k_b, _NT)
            s = qk * dw
            den = jnp.sum(s, axis=1, keepdims=True) + iw * jnp.sum(qs * nprev, axis=1, keepdims=True)
            emt = jnp.exp(-mt)
            r = 1.0 / jnp.maximum(jnp.abs(den), emt)
            dout = dh_ref[:, cols]
            dnum = dout * r
            dden = (-jnp.sum(dout * h_ref[:, cols], axis=1, keepdims=True) * r
                    * jnp.where(jnp.abs(den) > emt, jnp.sign(den), 0.0))
            dnum_b = dnum.astype(BF16)
            cprev_b = cprev.astype(BF16)
            dcn_b = dcn.astype(BF16)

            gd = (_dot(dnum_b, v_b, _NT) + dden) * dw
            gd_b = gd.astype(BF16)
            dqs_inter = iw * (_dot(dnum_b, cprev_b, _NT) + dden * nprev)
            dqs = _dot(gd_b, k_b) + dqs_inter
            dk_inter = wcol * (_dot(v_b, dcn_b, _NT) + dnn)
            dk = _dot(gd_b, qs_b, _TN) + dk_inter
            wk = wcol * k
            dv = _dot(s.astype(BF16), dnum_b, _TN) + _dot(wk.astype(BF16), dcn_b)

            e = gd * qk
            e_cols = jnp.sum(jnp.where(eye, jnp.sum(e, axis=0, keepdims=True), 0.0), axis=1, keepdims=True)
            k_inter = jnp.sum(k * dk_inter, axis=1, keepdims=True)
            rq = jnp.sum(e, axis=1, keepdims=True) + jnp.sum(qs * dqs_inter, axis=1, keepdims=True)
            rk = e_cols + k_inter
            hsum = jnp.sum(k_inter, axis=0, keepdims=True)
            jdec = decay * (jnp.sum(jnp.sum(dcn * cprev, axis=1, keepdims=True), axis=0, keepdims=True)
                            + jnp.sum(dnn * nprev, axis=1, keepdims=True))
            db = rq - rk + jnp.where(last, hsum + jdec, 0.0)
            dg_ref[h] = jnp.where(lane == 0, rk, jnp.where(lane == 1, db, 0.0))

            dq_ref[:, cols] = (dqs * scale).astype(BF16)
            dk_ref[:, cols] = dk.astype(BF16)
            dv_ref[:, cols] = dv.astype(BF16)

            iq = iw * qs
            dc_s[h] = decay * dcn + _dot(iq.astype(BF16), dnum_b, _TN)
            dn_s[h] = decay * dn8 + jnp.sum(iq * dden, axis=0, keepdims=True)

    rc = lambda c: nc - 1 - c
    grp = lambda off: pl.BlockSpec((L, wm), lambda c: (rc(c), qoff + off))
    hm = pl.BlockSpec((L, wm), lambda c: (rc(c), 0))
    act = jax.ShapeDtypeStruct((t, wm), BF16)
    return pl.pallas_call(
        body, name="mlstm_bwd", grid=(nc,),
        in_specs=[grp(0), grp(1), grp(2),
                  pl.BlockSpec((L, LANE), lambda c: (rc(c), 0)),
                  pl.BlockSpec((1, 8, L), lambda c: (rc(c), 0, 0)),
                  hm, hm,
                  pl.BlockSpec((NH, None, dh, dh), lambda c: (0, rc(c), 0, 0)),
                  pl.BlockSpec((NH, None, 8, dh), lambda c: (0, rc(c), 0, 0))],
        out_specs=[hm, hm, hm, pl.BlockSpec((NH, L, LANE), lambda c: (0, rc(c), 0))],
        out_shape=[act, act, act, jax.ShapeDtypeStruct((NH, t, LANE), F32)],
        scratch_shapes=[pltpu.VMEM((NH, dh, dh), F32), pltpu.VMEM((NH, 8, dh), F32)],
        compiler_params=_params("arbitrary"),
    )(proj, proj, proj, gcol, grow, hval, dh_in, cs, ns)


def _head_norm(hv):
    mu = jnp.mean(hv, axis=1, keepdims=True)
    hc = hv - mu
    rstd = lax.rsqrt(jnp.mean(hc * hc, axis=1, keepdims=True) + HN_EPS)
    return hc * rstd, rstd


def _hnorm_fwd(hval, proj, gain, y, t, wc, dh, tr=256):
    ooff = 3 * wc // dh + 3 * NH

    def body(h_ref, o_ref, g_ref, y_in, y_ref):
        hhat, _ = _head_norm(h_ref[...])
        y_ref[...] = (_sigmoid(o_ref[...]) * hhat * g_ref[...]).astype(BF16)

    return pl.pallas_call(
        body, name="hnorm_fwd", grid=(t // tr, NH),
        in_specs=[pl.BlockSpec((tr, dh), lambda i, h: (i, h)),
                  pl.BlockSpec((tr, dh), lambda i, h: (i, ooff + h)),
                  pl.BlockSpec((1, dh), lambda i, h: (0, h)),
                  pl.BlockSpec(memory_space=pl.ANY)],
        out_specs=pl.BlockSpec((None, tr, dh), lambda i, h: (1, i, h)),
        out_shape=jax.ShapeDtypeStruct(y.shape, BF16),
        input_output_aliases={3: 0},
        compiler_params=_params("parallel", "parallel"),
    )(hval, proj, gain, y)


def _hnorm_bwd(dy, hval, proj, gain, t, wc, dh, tr=256):
    ooff = 3 * wc // dh + 3 * NH
    yoff = wc // dh

    def body(dy_ref, h_ref, o_ref, g_ref, do_ref, dh_ref, dg_ref):
        i = pl.program_id(1)
        hhat, rstd = _head_norm(h_ref[...])
        gain_v = g_ref[...]
        sig = _sigmoid(o_ref[...])
        d = dy_ref[...]
        do_ref[...] = (d * hhat * gain_v * sig * (1.0 - sig)).astype(BF16)
        dhn = d * sig
        part = jnp.sum(dhn * hhat, axis=0, keepdims=True)

        @pl.when(i == 0)
        def _():
            dg_ref[...] = part

        @pl.when(i > 0)
        def _():
            dg_ref[...] += part

        dhat = dhn * gain_v
        dh_ref[...] = rstd * (dhat - jnp.mean(dhat, axis=1, keepdims=True)
                              - hhat * jnp.mean(dhat * hhat, axis=1, keepdims=True))

    blk = lambda off: pl.BlockSpec((tr, dh), lambda h, i: (i, off + h))
    return pl.pallas_call(
        body, name="hnorm_bwd", grid=(NH, t // tr),
        in_specs=[blk(yoff), blk(0), blk(ooff), pl.BlockSpec((1, dh), lambda h, i: (0, h))],
        out_specs=[blk(0), blk(0), pl.BlockSpec((1, dh), lambda h, i: (0, h))],
        out_shape=[jax.ShapeDtypeStruct((t, NH * dh), BF16), jax.ShapeDtypeStruct((t, NH * dh), F32),
                   jax.ShapeDtypeStruct((1, NH * dh), F32)],
        compiler_params=_params("parallel", "arbitrary"),
    )(dy, hval, proj, gain)


def _ln_stats(z):
    mu = jnp.mean(z, axis=1, keepdims=True)
    zc = z - mu
    rstd = lax.rsqrt(jnp.mean(zc * zc, axis=1, keepdims=True) + LN_EPS)
    return zc * rstd, rstd


def _ln_bwd(dy, xhat, rstd, g):
    dxh = dy * g
    return rstd * (dxh - jnp.mean(dxh, axis=1, keepdims=True) - xhat * jnp.mean(dxh * xhat, axis=1, keepdims=True))


def _accum(ref, i, part):
    @pl.when(i == 0)
    def _():
        ref[...] = part

    @pl.when(i > 0)
    def _():
        ref[...] += part


def _ln1_fwd(x, mix, g, b, tr=256):
    t, d = x.shape

    def body(x_ref, m_ref, g_ref, b_ref, xh_ref, rs_ref, xb_ref):
        xhat, rstd = _ln_stats(ALPHA * x_ref[...] + m_ref[...])
        xh_ref[...] = xhat
        rs_ref[...] = rstd
        xb_ref[...] = (xhat * g_ref[...] + b_ref[...]).astype(BF16)

    row = pl.BlockSpec((tr, d), lambda i: (i, 0))
    vec = pl.BlockSpec((1, d), lambda i: (0, 0))
    return pl.pallas_call(
        body, name="ln1_fwd", grid=(t // tr,),
        in_specs=[row, row, vec, vec],
        out_specs=[row, pl.BlockSpec((tr, 1), lambda i: (i, 0)), row],
        out_shape=[jax.ShapeDtypeStruct((t, d), F32), jax.ShapeDtypeStruct((t, 1), F32),
                   jax.ShapeDtypeStruct((t, d), BF16)],
        compiler_params=_params("parallel"),
    )(x, mix, g, b)


def _ln2_loss(xhat1, g1, b1, ff, target, g2, b2, tr=256):
    t, d = ff.shape

    def body(xh_ref, g1_ref, b1_ref, f_ref, t_ref, g_ref, b_ref, dz_ref, dzb_ref, dg_ref, db_ref, l_ref):
        i = pl.program_id(0)
        x1 = xh_ref[...] * g1_ref[...] + b1_ref[...]
        xhat, rstd = _ln_stats(ALPHA * x1 + f_ref[...])
        gv = g_ref[...]
        e = xhat * gv + b_ref[...] - t_ref[...]
        lsum = jnp.sum(jnp.sum(e * e, axis=1, keepdims=True), axis=0, keepdims=True) * (0.5 / d)
        dy = e * (1.0 / d)
        _accum(dg_ref, i, jnp.sum(dy * xhat, axis=0, keepdims=True))
        _accum(db_ref, i, jnp.sum(dy, axis=0, keepdims=True))
        _accum(l_ref, i, jnp.broadcast_to(lsum, l_ref.shape))
        dz = _ln_bwd(dy, xhat, rstd, gv)
        dz_ref[...] = dz
        dzb_ref[...] = dz.astype(BF16)

    row = pl.BlockSpec((tr, d), lambda i: (i, 0))
    vec = pl.BlockSpec((1, d), lambda i: (0, 0))
    return pl.pallas_call(
        body, name="ln2_loss", grid=(t // tr,),
        in_specs=[row, vec, vec, row, row, vec, vec],
        out_specs=[row, row, vec, vec, pl.BlockSpec((8, LANE), lambda i: (0, 0))],
        out_shape=[jax.ShapeDtypeStruct((t, d), F32), jax.ShapeDtypeStruct((t, d), BF16),
                   jax.ShapeDtypeStruct((1, d), F32), jax.ShapeDtypeStruct((1, d), F32),
                   jax.ShapeDtypeStruct((8, LANE), F32)],
        compiler_params=_params("arbitrary"),
    )(xhat1, g1, b1, ff, target, g2, b2)


def _ln1_bwd(dz2, dffn, xhat1, rstd1, g1, tr=256):
    t, d = dz2.shape

    def body(a_ref, f_ref, xh_ref, rs_ref, g_ref, dz_ref, dzb_ref, dg_ref, db_ref):
        i = pl.program_id(0)
        dy = ALPHA * a_ref[...] + f_ref[...]
        xhat = xh_ref[...]
        _accum(dg_ref, i, jnp.sum(dy * xhat, axis=0, keepdims=True))
        _accum(db_ref, i, jnp.sum(dy, axis=0, keepdims=True))
        dz = _ln_bwd(dy, xhat, rs_ref[...], g_ref[...])
        dz_ref[...] = dz
        dzb_ref[...] = dz.astype(BF16)

    row = pl.BlockSpec((tr, d), lambda i: (i, 0))
    vec = pl.BlockSpec((1, d), lambda i: (0, 0))
    return pl.pallas_call(
        body, name="ln1_bwd", grid=(t // tr,),
        in_specs=[row, row, row, pl.BlockSpec((tr, 1), lambda i: (i, 0)), vec],
        out_specs=[row, row, vec, vec],
        out_shape=[jax.ShapeDtypeStruct((t, d), F32), jax.ShapeDtypeStruct((t, d), BF16),
                   jax.ShapeDtypeStruct((1, d), F32), jax.ShapeDtypeStruct((1, d), F32)],
        compiler_params=_params("arbitrary"),
    )(dz2, dffn, xhat1, rstd1, g1)


def _ffn_act_fwd(hid0, w_fc, b_fc, t, dff):
    nb = dff // LANE

    def body(hv_ref, hg_ref, wv_ref, wg_ref, bv_ref, bg_ref, a_ref):
        val = _conv(hv_ref[...], wv_ref[...]) + bv_ref[...]
        gate = _conv(hg_ref[...], wg_ref[...]) + bg_ref[...]
        a_ref[...] = (gate * _sigmoid(gate) * val).astype(BF16)

    col = lambda off: pl.BlockSpec((t, LANE), lambda j: (0, j + off))
    w3 = lambda off: pl.BlockSpec((3, LANE), lambda j: (0, j + off))
    w1 = lambda off: pl.BlockSpec((1, LANE), lambda j: (0, j + off))
    return pl.pallas_call(
        body, name="ffn_act_fwd", grid=(nb,),
        in_specs=[col(0), col(nb), w3(0), w3(nb), w1(0), w1(nb)],
        out_specs=col(0),
        out_shape=jax.ShapeDtypeStruct((t, dff), BF16),
        compiler_params=_params("parallel"),
    )(hid0, hid0, w_fc, w_fc, b_fc, b_fc)


def _ffn_act_bwd(da, hid0, w_fc, b_fc, t, dff):
    nb = dff // LANE

    def body(da_ref, hv_ref, hg_ref, wv_ref, wg_ref, bv_ref, bg_ref,
             dh_ref, dwv_ref, dwg_ref, dbv_ref, dbg_ref):
        hv, hg, wv, wg = hv_ref[...], hg_ref[...], wv_ref[...], wg_ref[...]
        rv, rg = _rolled(hv), _rolled(hg)
        val = _conv(hv, wv, rv) + bv_ref[...]
        gate = _conv(hg, wg, rg) + bg_ref[...]
        sig = _sigmoid(gate)
        d = da_ref[...]
        dsig = d * sig
        dval = dsig * gate
        dgate = dsig * val * (1.0 + gate * (1.0 - sig))
        dh_ref[0] = _conv_t(dval, wv).astype(BF16)
        dh_ref[1] = _conv_t(dgate, wg).astype(BF16)
        dwv_ref[...] = _conv_dw(dval, hv, rv)
        dwg_ref[...] = _conv_dw(dgate, hg, rg)
        dbv_ref[...] = jnp.sum(dval, axis=0, keepdims=True)
        dbg_ref[...] = jnp.sum(dgate, axis=0, keepdims=True)

    col = lambda off: pl.BlockSpec((t, LANE), lambda j: (0, j + off))
    w3 = lambda off: pl.BlockSpec((3, LANE), lambda j: (0, j + off))
    w1 = lambda off: pl.BlockSpec((1, LANE), lambda j: (0, j + off))
    s3 = jax.ShapeDtypeStruct((3, dff), F32)
    s1 = jax.ShapeDtypeStruct((1, dff), F32)
    return pl.pallas_call(
        body, name="ffn_act_bwd", grid=(nb,),
        in_specs=[col(0), col(0), col(nb), w3(0), w3(nb), w1(0), w1(nb)],
        out_specs=[pl.BlockSpec((2, t, LANE), lambda j: (0, 0, j)), w3(0), w3(0), w1(0), w1(0)],
        out_shape=[jax.ShapeDtypeStruct((2, t, dff), BF16), s3, s3, s1, s1],
        compiler_params=_params("parallel"),
    )(da, hid0, hid0, w_fc, w_fc, b_fc, b_fc)


class _Ready:
    def __init__(self, **weights):
        self.weights = weights

    def begin(self, after):
        return None

    def forward(self, name, after):
        return None

    def get(self, name, after):
        return self.weights[name]


class _Kept:
    def __init__(self):
        self.grads = {}

    def start(self, name, grad):
        self.grads[name] = grad
        return None

    def relay(self, name, after):
        return None


def _behind(a, token):
    return a if token is None else a + token[0:1, 0:1].reshape((1,) * a.ndim)


def _local_step(x, target, w_in, b_gates, w_sc, gain, w_out, ln1_g, ln1_b, w_up, w_fc, b_fc, w_down, ln2_g, ln2_b,
                gx=None, wx=None):
    t, d = x.shape
    wc = d // 2
    dh = (d - wc) // NH
    wm = NH * dh
    dff = w_fc.shape[1] // 2
    if wx is None:
        wx = _Ready(w_out=w_out, w_up=w_up, w_down=w_down)
    ninp = w_in.shape[0]
    nin = 3 * wc + 4 * wm
    gate_tile = nin // LANE
    nc = t // CHUNK
    bias_tile = jnp.pad(b_gates, ((0, 0), (0, LANE - 2 * NH)))

    x_b = x.astype(BF16)
    proj = _matmul(x_b, w_in, "nt", F32, "proj", tm=1024, tn=1152, tk=d, after=wx.begin(w_in))
    y = _sconv_fwd(proj, w_sc, t, wc)
    gcol = _gates_prep(proj, bias_tile, t, gate_tile)
    grow = gcol[:, :8].T.reshape(8, nc, CHUNK).transpose(1, 0, 2)
    hval, cs, ns = _mlstm_fwd(proj, gcol, grow, t, wc, dh)
    y = _hnorm_fwd(hval, proj, gain, y, t, wc, dh)
    tok = wx.forward("w_up", wx.forward("w_out", y))
    w_out = wx.get("w_out", tok)
    mix = _matmul(y, w_out, "nn", F32, "out_proj", tm=512, tn=1024, tk=wc, a_blocked=True, after=tok)
    xhat1, rstd1, x1_b = _ln1_fwd(x, mix, ln1_g, ln1_b)
    tok = wx.forward("w_down", x1_b)
    w_up = wx.get("w_up", tok)
    wsl = w_up.shape[2]
    hid0 = _matmul(x1_b, w_up, "nn", F32, "ffn_up", tm=512, tn=wsl, tk=d, b_blocked=True, after=tok)
    act = _ffn_act_fwd(hid0, w_fc, b_fc, t, dff)
    w_down = wx.get("w_down", act)
    ff = _matmul(act, w_down, "nn", F32, "ffn_down", tm=1024, tn=512, tk=dff)
    dz2, dz2_b, d_ln2_g, d_ln2_b, loss = _ln2_loss(xhat1, ln1_g, ln1_b, ff, target, ln2_g, ln2_b)

    if gx is None:
        gx = _Kept()
    d_w_down = _matmul(act, dz2_b, "tn", BF16, "ffn_down_dw", tm=512, tn=1024, tk=t)
    d_act = _matmul(dz2_b, w_down, "nt", F32, "ffn_down_dx", tm=1024, tn=512, tk=d, after=gx.start("w_down", d_w_down))
    d_hid0, dwv, dwg, dbv, dbg = _ffn_act_bwd(d_act, hid0, w_fc, _behind(b_fc, gx.relay("w_down", d_act)), t, dff)
    d_w_fc = jnp.concatenate([dwv, dwg], axis=1)
    d_b_fc = jnp.concatenate([dbv, dbg], axis=1)
    d_w_up = _matmul(x1_b, d_hid0, "tn", BF16, "ffn_up_dw", tm=512, tn=wsl, tk=t, b_blocked=True, o_width=wsl)
    d_x1_ffn = _matmul(d_hid0, w_up, "nt", F32, "ffn_up_dx", tm=1024, tn=1024, tk=wsl, a_blocked=True, b_blocked=True,
                       after=gx.start("w_up", d_w_up))
    dz1, dz1_b, d_ln1_g, d_ln1_b = _ln1_bwd(dz2, d_x1_ffn, xhat1, rstd1, _behind(ln1_g, gx.relay("w_up", d_x1_ffn)))

    d_w_out = _matmul(y, dz1_b, "tn", BF16, "out_proj_dw", tm=512, tn=1024, tk=t, a_blocked=True)
    dy = _matmul(dz1_b, w_out, "nt", F32, "out_proj_dx", tm=512, tn=1024, tk=d, after=gx.start("w_out", d_w_out))
    dcb, dcc, dch, d_w_sc = _sconv_bwd(dy, proj, _behind(w_sc, gx.relay("w_out", dy)), t, wc)
    d_o, d_hval, d_gain = _hnorm_bwd(dy, hval, proj, gain, t, wc, dh)
    dq, dk, dv, dgate = _mlstm_bwd(proj, gcol, grow, hval, d_hval, cs, ns, t, wc, dh)
    dgt, d_b_gates = _gates_bwd(dgate, proj, bias_tile, t, gate_tile)
    pad = jnp.zeros((t, ninp - nin - LANE), BF16)
    d_proj = jnp.concatenate([dcb, dcc, dch, dq, dk, dv, d_o, dgt, pad], axis=1)
    d_w_in = _matmul(d_proj, x_b, "tn", BF16, "proj_dw", tm=IN_SLAB, tn=1024, tk=t)
    grad_x = _matmul(d_proj, w_in, "nn", F32, "proj_dx", tm=512, tn=512, tk=ninp, add=dz1, add_scale=ALPHA,
                     after=gx.start("w_in", d_w_in.reshape(ninp // IN_SLAB, IN_SLAB, d)))
    gx.relay("w_in", grad_x)

    small = dict(b_gates=d_b_gates[:, :2 * NH], w_sc_conv=d_w_sc, mh_gain=d_gain, ln1_g=d_ln1_g, ln1_b=d_ln1_b,
                 w_ffn_conv=d_w_fc, b_ffn_conv=d_b_fc, ln2_g=d_ln2_g, ln2_b=d_ln2_b)
    return loss, grad_x, small, gx


HBM = pl.BlockSpec(memory_space=pltpu.HBM)


def _place():
    return lax.axis_index("x"), lax.axis_index("y"), lax.axis_index("c")


def _index(p):
    return 4 * p[0] + 2 * p[1] + p[2]


def _all_gather(arrs, name):
    n = len(arrs)

    def body(*refs):
        ins, outs = refs[:n], refs[n:2 * n]
        send_sems, recv_sems, local_sems = refs[2 * n:]
        x, y, c = _place()
        me, sibling = (x, y, c), (x, y, 1 - c)
        chips = [(1 - x, y), (x, 1 - y), (1 - x, 1 - y)]

        def copy(a, k, block, to, own=False):
            dst = outs[a].at[_index(block)]
            return pltpu.make_async_remote_copy(
                src_ref=ins[a] if own else dst, dst_ref=dst,
                send_sem=send_sems.at[k * n + a], recv_sem=recv_sems.at[k * n + a],
                device_id=to, device_id_type=MESH)

        mine = [pltpu.make_async_copy(ins[a], outs[a].at[_index(me)], local_sems.at[a]) for a in range(n)]
        for cp in mine:
            cp.start()
        first = []
        for a in range(n):
            first.append(copy(a, 0, me, sibling, own=True))
            first += [copy(a, 1 + j, me, (*chip, c), own=True) for j, chip in enumerate(chips)]
        for cp in first:
            cp.start()
        passed = []
        for j, chip in enumerate(chips):
            for a in range(n):
                copy(a, 1 + j, (*chip, c), me).wait_recv()
                cp = copy(a, 4 + j, (*chip, c), sibling)
                cp.start()
                passed.append(cp)
        for a in range(n):
            copy(a, 0, sibling, me).wait_recv()
            for j, chip in enumerate(chips):
                copy(a, 4 + j, (*chip, 1 - c), me).wait_recv()
        for cp in first + passed:
            cp.wait_send()
        for cp in mine:
            cp.wait()

    return pl.pallas_call(
        body, name=name, in_specs=[HBM] * n, out_specs=[HBM] * n,
        out_shape=[jax.ShapeDtypeStruct((N_DEV,) + a.shape, a.dtype) for a in arrs],
        scratch_shapes=[pltpu.SemaphoreType.DMA((7 * n,)), pltpu.SemaphoreType.DMA((7 * n,)),
                        pltpu.SemaphoreType.DMA((n,))],
    )(*arrs)


SEM = pl.BlockSpec(memory_space=pltpu.SEMAPHORE)
EFFECT = pltpu.SideEffectType.DATAFLOW_SIDE_EFFECTING


def _chips(x, y):
    return [(1 - x, y), (x, 1 - y), (1 - x, 1 - y)]


N_CHIP = N_DEV // 2


def _pair_route(x, y, c):
    return [((x, y, 1 - c), 2 * q + (1 - c), q, q) for q in range(N_CHIP)]


def _chip_route(x, y, c):
    mine = 2 * x + y
    return [((*chip, c), 2 * chip[0] + chip[1], mine, 2 * chip[0] + chip[1]) for chip in _chips(x, y)]


def _exchange_pieces(g_ref, land_ref, width, tail):
    if not tail:
        return [(lambda i: g_ref.at[i], lambda s: land_ref.at[s])]
    return [(lambda i: g_ref.at[i], lambda s: land_ref.at[s, pl.ds(0, width), :]),
            (lambda i: g_ref.at[i + 1, pl.ds(0, LANE), :], lambda s: land_ref.at[s, pl.ds(width, LANE), :])]


def _exchange_start(grad, route, tail, name):
    width = grad.shape[1]
    n_p = 2 if tail else 1
    n_c = len(route(0, 0, 0))
    land_shape = (N_CHIP, width + (LANE if tail else 0), grad.shape[2])

    def body(g_ref, land_ref, send_sems, recv_sems, g_thru, land_thru, token):
        for j, (peer, slab, slot, _) in enumerate(route(*_place())):
            for p, (src, dst) in enumerate(_exchange_pieces(g_ref, land_ref, width, tail)):
                pltpu.make_async_remote_copy(src_ref=src(slab), dst_ref=dst(slot), send_sem=send_sems.at[j * n_p + p],
                                             recv_sem=recv_sems.at[j * n_p + p], device_id=peer,
                                             device_id_type=MESH).start()
        token[...] = jnp.zeros_like(token)

    return pl.pallas_call(
        body, name=name,
        out_shape=(pltpu.SemaphoreType.DMA((n_c * n_p,)), pltpu.SemaphoreType.DMA((n_c * n_p,)),
                   pltpu.HBM(grad.shape, grad.dtype), pltpu.HBM(land_shape, grad.dtype),
                   jax.ShapeDtypeStruct((8, LANE), F32)),
        in_specs=(HBM, HBM), out_specs=(SEM, SEM, HBM, HBM, pl.BlockSpec(memory_space=pltpu.VMEM)),
        input_output_aliases={0: 2, 1: 3},
        compiler_params=pltpu.CompilerParams(has_side_effects=EFFECT),
    )(pltpu.with_memory_space_constraint(grad, pltpu.HBM),
      pltpu.with_memory_space_constraint(lax.empty(land_shape, grad.dtype), pltpu.HBM))


def _exchange_wait(send_sems, recv_sems, g_thru, land_thru, after, route, tail, name):
    width = g_thru.shape[1]
    n_p = 2 if tail else 1

    def body(g_ref, land_ref, send_sems, recv_sems, after_ref, g_dead, got_ref):
        for j, (peer, slab, _, slot) in enumerate(route(*_place())):
            for p, (src, dst) in enumerate(_exchange_pieces(g_ref, land_ref, width, tail)):
                cp = pltpu.make_async_remote_copy(src_ref=src(slab), dst_ref=dst(slot),
                                                  send_sem=send_sems.at[j * n_p + p], recv_sem=recv_sems.at[j * n_p + p],
                                                  device_id=peer, device_id_type=MESH)
                cp.wait_send()
                cp.wait_recv()

    return pl.pallas_call(
        body, name=name,
        out_shape=(pltpu.HBM(g_thru.shape, g_thru.dtype), pltpu.HBM(land_thru.shape, land_thru.dtype)),
        in_specs=(HBM, HBM, SEM, SEM, pl.BlockSpec(memory_space=pl.ANY)), out_specs=(HBM, HBM),
        input_output_aliases={0: 0, 1: 1},
        compiler_params=pltpu.CompilerParams(has_side_effects=EFFECT),
    )(g_thru, land_thru, send_sems, recv_sems, after)


def _pair_add(grad, pair, core, tail, name):
    rows, cols = grad.shape[1], grad.shape[2]
    total = pair.shape[1]

    def body(core_ref, *refs):
        if tail:
            g_ref, t_ref, p_ref, o_ref = refs
            o_ref[0:rows, :] = (g_ref[...].astype(F32) + p_ref[0:rows, :].astype(F32)).astype(BF16)
            o_ref[rows:total, :] = (t_ref[...].astype(F32) + p_ref[rows:total, :].astype(F32)).astype(BF16)
        else:
            g_ref, p_ref, o_ref = refs
            o_ref[...] = (g_ref[...].astype(F32) + p_ref[...].astype(F32)).astype(BF16)

    if tail:
        tc = _fit(cols, 512)
        grid = (N_CHIP, cols // tc)
        slab = pl.BlockSpec((None, total, tc), lambda q, i, core_ref: (q, 0, i))
        in_specs = [pl.BlockSpec((None, rows, tc), lambda q, i, core_ref: (2 * q + core_ref[0], 0, i)),
                    pl.BlockSpec((None, LANE, tc), lambda q, i, core_ref: (2 * q + core_ref[0] + 1, 0, i))]
    else:
        tr = _rows(rows, 256)
        grid = (N_CHIP, rows // tr)
        slab = pl.BlockSpec((None, tr, cols), lambda q, i, core_ref: (q, i, 0))
        in_specs = [pl.BlockSpec((None, tr, cols), lambda q, i, core_ref: (2 * q + core_ref[0], i, 0))]
    return pl.pallas_call(
        body, name=name,
        grid_spec=pltpu.PrefetchScalarGridSpec(num_scalar_prefetch=1, grid=grid,
                                               in_specs=in_specs + [slab], out_specs=slab),
        out_shape=jax.ShapeDtypeStruct(pair.shape, BF16),
        compiler_params=_params("parallel", "parallel"),
    )(core, *([grad, grad] if tail else [grad]), pair)


def _gather_start(blocks, after, name):
    n = len(blocks)
    lands = [(N_DEV,) + b.shape for b in blocks]

    def body(*refs):
        b_refs, land_refs = refs[:n], refs[n:2 * n]
        send_sems, recv_sems = refs[2 * n + 1:3 * n + 1], refs[3 * n + 1:4 * n + 1]
        token = refs[-1]
        x, y, c = _place()
        me = _index((x, y, c))
        for a in range(n):
            for k, to in enumerate([(x, y, 1 - c)] + [(*chip, c) for chip in _chips(x, y)]):
                pltpu.make_async_remote_copy(src_ref=b_refs[a], dst_ref=land_refs[a].at[me], send_sem=send_sems[a].at[k],
                                             recv_sem=recv_sems[a].at[k], device_id=to, device_id_type=MESH).start()
        token[...] = jnp.zeros_like(token)

    sems = [pltpu.SemaphoreType.DMA((4,))] * n
    out = pl.pallas_call(
        body, name=name,
        out_shape=(*sems, *sems, *[pltpu.HBM(b.shape, b.dtype) for b in blocks],
                   *[pltpu.HBM(s, b.dtype) for s, b in zip(lands, blocks)], jax.ShapeDtypeStruct((8, LANE), F32)),
        in_specs=(*[HBM] * (2 * n), pl.BlockSpec(memory_space=pl.ANY)),
        out_specs=(*[SEM] * (2 * n), *[HBM] * (2 * n), pl.BlockSpec(memory_space=pltpu.VMEM)),
        input_output_aliases={i: 2 * n + i for i in range(2 * n)},
        compiler_params=pltpu.CompilerParams(has_side_effects=EFFECT),
    )(*[pltpu.with_memory_space_constraint(b, pltpu.HBM) for b in blocks],
      *[pltpu.with_memory_space_constraint(lax.empty(s, b.dtype), pltpu.HBM) for s, b in zip(lands, blocks)], after)
    return [(out[a], out[n + a], out[2 * n + a], out[3 * n + a]) for a in range(n)], out[-1]


def _gather_forward(send_sems, recv_sems, b_thru, land_thru, after, name):
    def body(b_ref, land_ref, send_sems, recv_sems, after_ref, b_dead, land_out, send2, recv2, token):
        x, y, c = _place()
        sibling = (x, y, 1 - c)
        for k, frm in enumerate([sibling] + [(*chip, c) for chip in _chips(x, y)]):
            cp = pltpu.make_async_remote_copy(src_ref=b_ref, dst_ref=land_ref.at[_index(frm)], send_sem=send_sems.at[k],
                                              recv_sem=recv_sems.at[k], device_id=frm, device_id_type=MESH)
            cp.wait_send()
            cp.wait_recv()
        for j, chip in enumerate(_chips(x, y)):
            slot = land_ref.at[_index((*chip, c))]
            pltpu.make_async_remote_copy(src_ref=slot, dst_ref=slot, send_sem=send2.at[j], recv_sem=recv2.at[j],
                                         device_id=sibling, device_id_type=MESH).start()
        token[...] = jnp.zeros_like(token)

    return pl.pallas_call(
        body, name=name,
        out_shape=(pltpu.HBM(b_thru.shape, b_thru.dtype), pltpu.HBM(land_thru.shape, land_thru.dtype),
                   pltpu.SemaphoreType.DMA((3,)), pltpu.SemaphoreType.DMA((3,)), jax.ShapeDtypeStruct((8, LANE), F32)),
        in_specs=(HBM, HBM, SEM, SEM, pl.BlockSpec(memory_space=pl.ANY)),
        out_specs=(HBM, HBM, SEM, SEM, pl.BlockSpec(memory_space=pltpu.VMEM)),
        input_output_aliases={0: 0, 1: 1},
        compiler_params=pltpu.CompilerParams(has_side_effects=EFFECT),
    )(b_thru, land_thru, send_sems, recv_sems, after)


def _gather_finish(land_thru, send2, recv2, after, name):
    def body(land_ref, send2, recv2, after_ref, land_out):
        x, y, c = _place()
        for j, chip in enumerate(_chips(x, y)):
            cp = pltpu.make_async_remote_copy(src_ref=land_ref.at[_index((*chip, c))],
                                              dst_ref=land_ref.at[_index((*chip, 1 - c))], send_sem=send2.at[j],
                                              recv_sem=recv2.at[j], device_id=(x, y, 1 - c), device_id_type=MESH)
            cp.wait_send()
            cp.wait_recv()

    return pl.pallas_call(
        body, name=name, out_shape=pltpu.HBM(land_thru.shape, land_thru.dtype),
        in_specs=(HBM, SEM, SEM, pl.BlockSpec(memory_space=pl.ANY)), out_specs=HBM,
        input_output_aliases={0: 0},
        compiler_params=pltpu.CompilerParams(has_side_effects=EFFECT),
    )(land_thru, send2, recv2, after)


class _Gathering:
    def __init__(self, blocks, me, after):
        started, self.token = _gather_start(list(blocks.values()), after, "gather1")
        self.me, self.state = me, dict(zip(blocks, started))

    def begin(self, after):
        return self.token

    def forward(self, name, after):
        *self.state[name], token = _gather_forward(*self.state[name], after, "gather2_" + name)
        return token

    def get(self, name, after):
        block, land, send2, recv2 = self.state[name]
        land = _gather_finish(land, send2, recv2, after, "gather3_" + name)
        land = lax.dynamic_update_index_in_dim(land, block[None], self.me, 0)
        return land if name not in ("w_out", "w_down") else land.reshape(-1, land.shape[2])


class _Reducing:
    def __init__(self, core, chip):
        self.core, self.chip, self.state, self.token = core, chip, {}, None

    def start(self, name, grad):
        g = grad if grad.ndim == 3 else grad.reshape(N_DEV, grad.shape[0] // N_DEV, grad.shape[1])
        *self.state[name], token = _exchange_start(g, _pair_route, name == "w_in", "pair_send_" + name)
        return token

    def relay(self, name, after):
        tail = name == "w_in"
        grad, pair = _exchange_wait(*self.state[name], after, _pair_route, tail, "pair_recv_" + name)
        total = _pair_add(grad, pair, self.core, tail, "pair_add_" + name)
        *self.state[name], self.token = _exchange_start(total, _chip_route, False, "chip_send_" + name)
        return self.token

    def finish(self, name, after):
        total, land = _exchange_wait(*self.state[name], after, _chip_route, False, "chip_recv_" + name)
        own = lax.dynamic_index_in_dim(total, self.chip, 0, keepdims=True)
        return lax.dynamic_update_index_in_dim(land, own, self.chip, 0)


def _assemble_w_in(g, ninp):
    _, ph, d = g.shape
    per = IN_SLAB // LANE
    assert ninp == (N_DEV + 1) * IN_SLAB and ph == IN_SLAB + LANE
    tc = _fit(d, 512)

    def body(a_ref, b_ref, o_ref):
        s = pl.program_id(0)
        head = a_ref[0:LANE, :]
        rest = a_ref[LANE:IN_SLAB, :]
        o_ref[0:LANE, :] = (jnp.where(s < N_DEV, head, jnp.zeros_like(head))
                            + jnp.where(s > 0, b_ref[...], jnp.zeros_like(head)))
        o_ref[LANE:IN_SLAB, :] = jnp.where(s < N_DEV, rest, jnp.zeros_like(rest))

    return pl.pallas_call(
        body, name="assemble_w_in", grid=(N_DEV + 1, d // tc),
        in_specs=[pl.BlockSpec((None, ph, tc), lambda s, j: (jnp.minimum(s, N_DEV - 1), 0, j)),
                  pl.BlockSpec((None, LANE, tc), lambda s, j: (jnp.maximum(s, 1) - 1, per, j))],
        out_specs=pl.BlockSpec((IN_SLAB, tc), lambda s, j: (s, j)),
        out_shape=jax.ShapeDtypeStruct((ninp, d), g.dtype),
        compiler_params=_params("parallel", "parallel"),
    )(g, g)


def _rows(n, want):
    t = min(n, want)
    t -= t % 16
    while n % t:
        t -= 16
    return t


def _adam_math(w, g, m, v):
    m2 = ADAM_B1 * m + (1.0 - ADAM_B1) * g
    v2 = ADAM_B2 * v + (1.0 - ADAM_B2) * (g * g)
    m_hat = m2 / (1.0 - ADAM_B1 ** ADAM_STEP)
    v_hat = v2 / (1.0 - ADAM_B2 ** ADAM_STEP)
    return -ADAM_LR * (m_hat / (jnp.sqrt(v_hat) + ADAM_EPS) + ADAM_WD * w), m2, v2


def _slot_sum(r_ref):
    acc = r_ref[0].astype(F32)
    for i in range(1, r_ref.shape[0]):
        acc = acc + r_ref[i].astype(F32)
    return acc


def _shift_w_in(w, ph):
    ws, d = w.shape
    tc = _fit(d, 256)

    def body(w_ref, o_ref, tall):
        tall[...] = jnp.zeros_like(tall)
        tall[0:ws, :] = w_ref[...]
        o_ref[...] = pltpu.roll(tall[...], _index(_place()), 0).astype(BF16)

    return pl.pallas_call(
        body, name="shift_w_in", grid=(d // tc,),
        in_specs=[pl.BlockSpec((ws, tc), lambda j: (0, j))],
        out_specs=pl.BlockSpec((ph, tc), lambda j: (0, j)),
        out_shape=jax.ShapeDtypeStruct((ph, d), BF16),
        scratch_shapes=[pltpu.VMEM((ph, tc), F32)], compiler_params=_params("parallel"),
    )(w)


def _sum_adamw_shifted(r, w, m, v, name):
    _, ph, d = r.shape
    ws = w.shape[0]
    tc = _fit(d, 256)

    def body(r_ref, w_ref, m_ref, v_ref, g_ref, d_ref, m2_ref, v2_ref, tall):
        tall[...] = pltpu.roll(_slot_sum(r_ref), lax.rem(ph - _index(_place()), ph), 0)
        g = tall[0:ws, :]
        g_ref[...] = g
        d_ref[...], m2_ref[...], v2_ref[...] = _adam_math(w_ref[...], g, m_ref[...], v_ref[...])

    blk = pl.BlockSpec((ws, tc), lambda j: (0, j))
    out = jax.ShapeDtypeStruct(w.shape, F32)
    return pl.pallas_call(
        body, name=name, grid=(d // tc,),
        in_specs=[pl.BlockSpec((r.shape[0], ph, tc), lambda j: (0, 0, j)), blk, blk, blk],
        out_specs=[blk] * 4, out_shape=[out] * 4,
        scratch_shapes=[pltpu.VMEM((ph, tc), F32)], compiler_params=_params("parallel"),
    )(r, w, m, v)


def _sum_slots(r, name, tr=128):
    _, rows, cols = r.shape
    tr = _rows(rows, tr)

    def body(r_ref, g_ref):
        g_ref[...] = _slot_sum(r_ref)

    return pl.pallas_call(
        body, name=name, grid=(rows // tr,),
        in_specs=[pl.BlockSpec((r.shape[0], tr, cols), lambda i: (0, i, 0))],
        out_specs=pl.BlockSpec((tr, cols), lambda i: (i, 0)),
        out_shape=jax.ShapeDtypeStruct((rows, cols), F32),
        compiler_params=_params("parallel"),
    )(r)


def _adamw(w, g, m, v, name, tr=256):
    rows, cols = w.shape
    tr = _rows(rows, tr)

    def body(w_ref, g_ref, m_ref, v_ref, d_ref, m2_ref, v2_ref):
        d_ref[...], m2_ref[...], v2_ref[...] = _adam_math(w_ref[...], g_ref[...], m_ref[...], v_ref[...])

    blk = pl.BlockSpec((tr, cols), lambda i: (i, 0))
    out = jax.ShapeDtypeStruct((rows, cols), F32)
    return pl.pallas_call(
        body, name=name, grid=(rows // tr,), in_specs=[blk] * 4, out_specs=[blk] * 3, out_shape=[out] * 3,
        compiler_params=_params("parallel"),
    )(w, g, m, v)


def _sum_adamw(r, w, m, v, name, tr=128):
    rows, cols = w.shape
    tr = _rows(rows, tr)

    def body(r_ref, w_ref, m_ref, v_ref, g_ref, d_ref, m2_ref, v2_ref):
        g = _slot_sum(r_ref)
        g_ref[...] = g
        d_ref[...], m2_ref[...], v2_ref[...] = _adam_math(w_ref[...], g, m_ref[...], v_ref[...])

    blk = pl.BlockSpec((tr, cols), lambda i: (i, 0))
    out = jax.ShapeDtypeStruct((rows, cols), F32)
    return pl.pallas_call(
        body, name=name, grid=(rows // tr,),
        in_specs=[pl.BlockSpec((r.shape[0], tr, cols), lambda i: (0, i, 0)), blk, blk, blk],
        out_specs=[blk] * 4, out_shape=[out] * 4,
        compiler_params=_params("parallel"),
    )(r, w, m, v)


def _pack(pieces, sizes):
    flat = [jnp.pad(p.reshape(-1).astype(F32), (0, s - p.size)) for p, s in zip(pieces, sizes)]
    total = sum(sizes)
    padded = -(-total // (16 * LANE)) * (16 * LANE)
    return jnp.pad(jnp.concatenate(flat), (0, padded - total)).reshape(-1, LANE)


def _unpack(packed, shapes, sizes):
    flat = packed.reshape(-1)
    out, off = [], 0
    for shp, s in zip(shapes, sizes):
        n = 1
        for k in shp:
            n *= k
        out.append(flat[off:off + n].reshape(shp))
        off += s
    return out


def _lanes(n):
    return -(-n // LANE) * LANE


WEIGHTS = ("w_in", "b_gates", "w_sc_conv", "mh_gain", "w_out", "ln1_g", "ln1_b", "w_up", "w_ffn_conv", "b_ffn_conv",
           "w_down", "ln2_g", "ln2_b")
BIG = ("w_in", "w_out", "w_up", "w_down")
SMALL = tuple(n for n in WEIGHTS if n not in BIG)


def kernel(x, w_in, b_gates, w_sc_conv, mh_gain, w_out, ln1_g, ln1_b, w_up, w_ffn_conv, b_ffn_conv, w_down, ln2_g, ln2_b, loss_target, m_w_in, m_b_gates, m_w_sc_conv, m_mh_gain, m_w_out, m_ln1_g, m_ln1_b, m_w_up, m_w_ffn_conv, m_b_ffn_conv, m_w_down, m_ln2_g, m_ln2_b, v_w_in, v_b_gates, v_w_sc_conv, v_mh_gain, v_w_out, v_ln1_g, v_ln1_b, v_w_up, v_w_ffn_conv, v_b_ffn_conv, v_w_down, v_ln2_g, v_ln2_b):
    w = dict(zip(WEIGHTS, (w_in, b_gates, w_sc_conv, mh_gain, w_out, ln1_g, ln1_b, w_up, w_ffn_conv, b_ffn_conv,
                           w_down, ln2_g, ln2_b)))
    m = dict(zip(WEIGHTS, (m_w_in, m_b_gates, m_w_sc_conv, m_mh_gain, m_w_out, m_ln1_g, m_ln1_b, m_w_up,
                           m_w_ffn_conv, m_b_ffn_conv, m_w_down, m_ln2_g, m_ln2_b)))
    v = dict(zip(WEIGHTS, (v_w_in, v_b_gates, v_w_sc_conv, v_mh_gain, v_w_out, v_ln1_g, v_ln1_b, v_w_up,
                           v_w_ffn_conv, v_b_ffn_conv, v_w_down, v_ln2_g, v_ln2_b)))
    me = _index(_place())
    d = x.shape[2]
    ws_in = w_in.shape[2]
    assert ws_in == IN_SLAB + 1 and N_DEV <= LANE, w_in.shape
    ninp = (N_DEV + 1) * IN_SLAB
    ws_sc, ws_fc = w_sc_conv.shape[2], w_ffn_conv.shape[2]
    w_in_t, m_in_t, v_in_t = (jnp.transpose(a[0]) for a in (w_in, m_w_in, v_w_in))

    w_in_shift = _shift_w_in(w_in_t, IN_SLAB + LANE)
    taps8 = lambda a: jnp.pad(a[0], ((0, 5), (0, 0)))
    blocks = dict(w_in=w_in_shift, w_sc=taps8(w_sc_conv), w_fc=taps8(w_ffn_conv),
                  **{n: w[n][0].astype(BF16) for n in ("w_out", "w_up", "w_down")})
    wx = _Gathering(blocks, me, w_in_shift)
    token = wx.begin(None)
    for n in ("w_in", "w_sc", "w_fc"):
        token = wx.forward(n, token)
    g_in, g_sc, g_fc = (wx.get(n, token) for n in ("w_in", "w_sc", "w_fc"))
    w_in_full = _assemble_w_in(g_in, ninp)
    w_sc_full = g_sc[:, :3].transpose(1, 0, 2).reshape(3, N_DEV * ws_sc)
    w_fc_full = g_fc[:, :3].transpose(1, 0, 2).reshape(3, N_DEV * ws_fc)

    xi, yi, ci = _place()
    gx = _Reducing(jnp.reshape(ci, (1,)).astype(jnp.int32), 2 * xi + yi)
    loss_t, grad_x, small, _ = _local_step(
        x[0], loss_target[0], w_in_full, b_gates, w_sc_full, mh_gain, None, ln1_g, ln1_b, None,
        w_fc_full, b_ffn_conv, None, ln2_g, ln2_b, gx=gx, wx=wx)

    grads, deltas, new_m, new_v = {}, {}, {}, {}
    for name in ("w_down", "w_up", "w_out"):
        grads[name], deltas[name], new_m[name], new_v[name] = _sum_adamw(
            gx.finish(name, gx.token), w[name][0], m[name][0], v[name][0], "adamw_" + name)

    names = ("loss",) + SMALL
    parts = dict(small, loss=loss_t[0, :1])
    sizes = [_lanes(parts[n].size) for n in names]
    (g_small,) = _all_gather([_pack([parts[n] for n in names], sizes)], "gather_small")
    summed = _unpack(_sum_slots(g_small, "sum_small", tr=g_small.shape[1]), [parts[n].shape for n in names], sizes)
    full = dict(zip(names, summed))
    full["w_sc_conv"] = lax.dynamic_slice(full["w_sc_conv"], (0, me * ws_sc), (3, ws_sc))
    full["w_ffn_conv"] = lax.dynamic_slice(full["w_ffn_conv"], (0, me * ws_fc), (3, ws_fc))
    for n in SMALL:
        grads[n] = full[n].reshape(w[n].shape)
    sizes = [_lanes(w[n].size) for n in SMALL]
    shapes = [w[n].shape for n in SMALL]
    packed = [_pack([t[n] for n in SMALL], sizes) for t in (w, grads, m, v)]
    small_out = _adamw(*packed, "adamw_small")
    for res, t in zip(small_out, (deltas, new_m, new_v)):
        t.update(zip(SMALL, _unpack(res, shapes, sizes)))

    done = sum(t[0:1, 0:1] for t in (deltas["w_down"], deltas["w_up"], deltas["w_out"], small_out[0]))
    grads["w_in"], deltas["w_in"], new_m["w_in"], new_v["w_in"] = (
        jnp.transpose(a)[None] for a in _sum_adamw_shifted(gx.finish("w_in", done), w_in_t, m_in_t, v_in_t, "adamw_w_in"))

    big = lambda t: {n: (t[n].reshape(w[n].shape) if n in BIG else t[n]) for n in WEIGHTS}
    grads, deltas, new_m, new_v = big(grads), big(deltas), big(new_m), big(new_v)
    return (full["loss"].reshape(()), grad_x[None], *[grads[n] for n in WEIGHTS], *[deltas[n] for n in WEIGHTS],
            *[new_m[n] for n in WEIGHTS], *[new_v[n] for n in WEIGHTS])
```

```python
import functools

import jax
import jax.numpy as jnp
from jax import lax
from jax.experimental import pallas as pl
from jax.experimental.pallas import tpu as pltpu

F32 = jnp.float32
BF16 = jnp.bfloat16
MESH = pl.DeviceIdType.MESH

N_DEV = 8
NH = 4
CHUNK = 64
LN_EPS = 1e-5
HN_EPS = 1e-6
ALPHA = 2.0 ** 0.25
LANE = 128
IN_SLAB = 7 * LANE
VMEM_LIMIT = 56 * 1024 * 1024
ADAM_LR, ADAM_B1, ADAM_B2, ADAM_EPS, ADAM_WD, ADAM_STEP = 0.001, 0.9, 0.999, 1e-08, 0.01, 10

_NN = (((1,), (0,)), ((), ()))
_NT = (((1,), (1,)), ((), ()))
_TN = (((0,), (0,)), ((), ()))


def _dot(a, b, dn=_NN):
    return lax.dot_general(a, b, dn, preferred_element_type=F32)


def _params(*sem):
    return pltpu.CompilerParams(dimension_semantics=sem if sem else None, vmem_limit_bytes=VMEM_LIMIT)


def _iota(shape, axis):
    return lax.broadcasted_iota(jnp.int32, shape, axis)


def _fit(n, want):
    if n <= want:
        return n
    t = want - want % LANE
    while n % t:
        t -= LANE
    return t


def _matmul(a, b, mode, out_dtype, name, tm=1024, tn=512, tk=1024, add=None, add_scale=1.0,
            a_blocked=False, b_blocked=False, o_width=None, after=None):
    a_parts = a if isinstance(a, tuple) else None
    b_parts = b if isinstance(b, tuple) else None
    if a_parts:
        a_blocked, (a_rows, wa), na = True, a[0].shape, len(a)
        kd, m = (a_rows, na * wa) if mode == "tn" else (na * wa, a_rows)
    elif a_blocked:
        na, a_rows, wa = a.shape
        kd, m = (a_rows, na * wa) if mode == "tn" else (na * wa, a_rows)
    elif mode == "tn":
        kd, m = a.shape
    else:
        m, kd = a.shape
    if b_parts:
        b_blocked, (rows, w), nb = True, b[0].shape, len(b)
    elif b_blocked:
        nb, rows, w = b.shape
    if b_blocked:
        n = rows if mode == "nt" else nb * w
        assert (nb * w if mode == "nt" else rows) == kd, (name, kd)
    else:
        n = b.shape[0] if mode == "nt" else b.shape[1]
    tm, tn, tk = _fit(m, tm), _fit(n, tn), _fit(kd, tk)
    if a_blocked and mode == "tn":
        tm = _fit(wa, tm)
    if a_blocked and mode != "tn":
        tk = _fit(wa, tk)
    if b_blocked and mode != "nt":
        tn = _fit(w, tn)
    if b_blocked and mode == "nt":
        tk = _fit(w, tk)
    if o_width is not None:
        tn = _fit(o_width, tn)
    assert m % tm == 0 and n % tn == 0 and kd % tk == 0, (name, m, n, kd, tm, tn, tk)
    assert not (a_blocked and mode != "tn" and wa % tk) and not (b_blocked and mode == "nt" and w % tk), (name, tk)
    nk = kd // tk
    dn = {"nn": _NN, "nt": _NT, "tn": _TN}[mode]
    if a_blocked and mode == "tn":
        a_per = wa // tm
        a_spec = pl.BlockSpec((None, tk, tm), lambda i, j, k: (i // a_per, k, i % a_per))
    elif a_blocked:
        a_per = wa // tk
        a_spec = pl.BlockSpec((None, tm, tk), lambda i, j, k: (k // a_per, i, k % a_per))
    elif mode == "tn":
        a_spec = pl.BlockSpec((tk, tm), lambda i, j, k: (k, i))
    else:
        a_spec = pl.BlockSpec((tm, tk), lambda i, j, k: (i, k))
    if b_blocked and mode != "nt":
        per = w // tn
        b_spec = pl.BlockSpec((None, tk, tn), lambda i, j, k: (j // per, k, j % per))
    elif b_blocked:
        per = w // tk
        b_spec = pl.BlockSpec((None, tn, tk), lambda i, j, k: (k // per, j, k % per))
    elif mode == "nt":
        b_spec = pl.BlockSpec((tn, tk), lambda i, j, k: (j, k))
    else:
        b_spec = pl.BlockSpec((tk, tn), lambda i, j, k: (k, j))
    if o_width is None:
        o_spec = pl.BlockSpec((tm, tn), lambda i, j, k: (i, j))
        o_shape = (m, n)
    else:
        oper = o_width // tn
        o_spec = pl.BlockSpec((None, tm, tn), lambda i, j, k: (j // oper, i, j % oper))
        o_shape = (n // o_width, m, o_width)
    a_list, a_specs = [a], [a_spec]
    if a_parts:
        hold = lambda x, s: jnp.clip(x - s * a_per, 0, a_per - 1)
        a_list = list(a_parts)
        a_specs = [(pl.BlockSpec((tk, tm), lambda i, j, k, s=s: (k, hold(i, s))) if mode == "tn"
                    else pl.BlockSpec((tm, tk), lambda i, j, k, s=s: (i, hold(k, s)))) for s in range(na)]
    b_list, b_specs = [b], [b_spec]
    if b_parts:
        hold_b = lambda x, s: jnp.clip(x - s * per, 0, per - 1)
        b_list = list(b_parts)
        b_specs = [(pl.BlockSpec((tn, tk), lambda i, j, k, s=s: (j, hold_b(k, s))) if mode == "nt"
                    else pl.BlockSpec((tk, tn), lambda i, j, k, s=s: (k, hold_b(j, s)))) for s in range(nb)]
    n_a, n_b = len(a_list), len(b_list)
    has_add = add is not None
    n_in = n_a + n_b + has_add + (after is not None)
    in_place = nk > 1 and out_dtype == F32

    def body(*refs):
        add_ref = refs[n_a + n_b] if has_add else None
        o_ref = refs[n_in]
        i, j, k = pl.program_id(0), pl.program_id(1), pl.program_id(2)

        def finish(r):
            if has_add:
                r = r + add_scale * add_ref[...]
            o_ref[...] = r.astype(out_dtype)

        def step(a_ref, b_ref):
            if nk == 1:
                finish(_dot(a_ref[...], b_ref[...], dn))
                return
            acc = o_ref if in_place else refs[-1]

            @pl.when(k == 0)
            def _():
                acc[...] = _dot(a_ref[...], b_ref[...], dn)

            @pl.when(k > 0)
            def _():
                acc[...] += _dot(a_ref[...], b_ref[...], dn)

        if n_a == 1 and n_b == 1:
            step(refs[0], refs[1])
        else:
            slab_a = ((i if mode == "tn" else k) // a_per) if n_a > 1 else 0
            slab_b = ((k if mode == "nt" else j) // per) if n_b > 1 else 0
            for sa in range(n_a):
                for sb in range(n_b):
                    pl.when((slab_a == sa) & (slab_b == sb))(functools.partial(step, refs[sa], refs[n_a + sb]))
        if nk > 1 and not (in_place and not has_add):
            @pl.when(k == nk - 1)
            def _():
                finish((o_ref if in_place else refs[-1])[...])

    in_specs = a_specs + b_specs + ([pl.BlockSpec((tm, tn), lambda i, j, k: (i, j))] if has_add else [])
    args = (*a_list, *b_list) + ((add,) if has_add else ())
    if after is not None:
        in_specs.append(pl.BlockSpec(memory_space=pl.ANY))
        args += (after,)
    return pl.pallas_call(
        body, name=name, grid=(m // tm, n // tn, nk),
        in_specs=in_specs, out_specs=o_spec,
        out_shape=jax.ShapeDtypeStruct(o_shape, out_dtype),
        scratch_shapes=[pltpu.VMEM((tm, tn), F32)] if nk > 1 and not in_place else [],
        compiler_params=_params("parallel", "parallel", "arbitrary"),
    )(*args)


def _shift_down(u, s):
    return jnp.where(_iota(u.shape, 0) >= s, pltpu.roll(u, s, 0), 0.0)


def _shift_up(u, s):
    t = u.shape[0]
    return jnp.where(_iota(u.shape, 0) < t - s, pltpu.roll(u, t - s, 0), 0.0)


SLAB = 8


def _rolled(u):
    return pltpu.roll(u, 2, 0), pltpu.roll(u, 1, 0)


def _conv(u, w, rolled=None):
    u2, u1 = _rolled(u) if rolled is None else rolled
    raw = w[0:1] * u2 + w[1:2] * u1 + w[2:3] * u
    head = u[0:SLAB]
    mended = w[0:1] * _shift_down(head, 2) + w[1:2] * _shift_down(head, 1) + w[2:3] * head
    return jnp.concatenate([mended, raw[SLAB:]], axis=0)


def _conv_t(dy, w):
    t = dy.shape[0]
    raw = w[2:3] * dy + w[1:2] * pltpu.roll(dy, t - 1, 0) + w[0:1] * pltpu.roll(dy, t - 2, 0)
    tail = dy[t - SLAB:]
    mended = w[2:3] * tail + w[1:2] * _shift_up(tail, 1) + w[0:1] * _shift_up(tail, 2)
    return jnp.concatenate([raw[:t - SLAB], mended], axis=0)


def _conv_dw(dy, u, rolled=None):
    t = dy.shape[0]
    u2, u1 = _rolled(u) if rolled is None else rolled
    head, tail = dy[0:SLAB], u[t - SLAB:]
    r = _iota(head.shape, 0)
    wrap2 = jnp.sum(jnp.where(r < 2, head * pltpu.roll(tail, 2, 0), 0.0), axis=0, keepdims=True)
    wrap1 = jnp.sum(jnp.where(r < 1, head * pltpu.roll(tail, 1, 0), 0.0), axis=0, keepdims=True)
    d0 = jnp.sum(dy * u2, axis=0, keepdims=True) - wrap2
    d1 = jnp.sum(dy * u1, axis=0, keepdims=True) - wrap1
    d2 = jnp.sum(dy * u, axis=0, keepdims=True)
    r3 = _iota((3, dy.shape[1]), 0)
    return jnp.where(r3 == 0, d0, jnp.where(r3 == 1, d1, d2))


def _sigmoid(x):
    return 0.5 * jnp.tanh(0.5 * x) + 0.5


def _sconv_fwd(proj, w_sc, t, wc):
    nb = wc // LANE

    def body(cb_ref, cc_ref, ch_ref, w_ref, y_ref):
        u = cc_ref[...] * ch_ref[...]
        y_ref[...] = (cb_ref[...] * _conv(u, w_ref[...])).astype(BF16)

    col = lambda off: pl.BlockSpec((t, LANE), lambda j: (0, j + off))
    return pl.pallas_call(
        body, name="sconv_fwd", grid=(nb,),
        in_specs=[col(0), col(nb), col(2 * nb), pl.BlockSpec((3, LANE), lambda j: (0, j))],
        out_specs=pl.BlockSpec((None, t, LANE), lambda j: (0, 0, j)),
        out_shape=jax.ShapeDtypeStruct((2, t, wc), BF16),
        compiler_params=_params("parallel"),
    )(proj, proj, proj, w_sc)


def _sconv_bwd(dy, proj, w_sc, t, wc):
    nb = wc // LANE

    def body(dy_ref, cb_ref, cc_ref, ch_ref, w_ref, dcb_ref, dcc_ref, dch_ref, dw_ref):
        cc, ch, w, d = cc_ref[...], ch_ref[...], w_ref[...], dy_ref[...]
        u = cc * ch
        ru = _rolled(u)
        dcb_ref[...] = (d * _conv(u, w, ru)).astype(BF16)
        dcu = d * cb_ref[...]
        dw_ref[...] = _conv_dw(dcu, u, ru)
        du = _conv_t(dcu, w)
        dcc_ref[...] = (du * ch).astype(BF16)
        dch_ref[...] = (du * cc).astype(BF16)

    col = lambda off: pl.BlockSpec((t, LANE), lambda j: (0, j + off))
    act = jax.ShapeDtypeStruct((t, wc), BF16)
    return pl.pallas_call(
        body, name="sconv_bwd", grid=(nb,),
        in_specs=[col(0), col(0), col(nb), col(2 * nb), pl.BlockSpec((3, LANE), lambda j: (0, j))],
        out_specs=[col(0), col(0), col(0), pl.BlockSpec((3, LANE), lambda j: (0, j))],
        out_shape=[act, act, act, jax.ShapeDtypeStruct((3, wc), F32)],
        compiler_params=_params("parallel"),
    )(dy, proj, proj, proj, w_sc)


def _gates_prep(proj, bias_tile, t, gate_tile):
    def body(g_ref, b_ref, o_ref):
        g = g_ref[...] + b_ref[...]
        lane = _iota(g.shape, 1)
        is_f = (lane >= NH) & (lane < 2 * NH)
        lf = jnp.minimum(g, 0.0) - jnp.log(1.0 + jnp.exp(-jnp.abs(g)))
        c = jnp.where(is_f, lf, 0.0)
        r = _iota(g.shape, 0) % CHUNK
        s = 1
        while s < CHUNK:
            c = c + jnp.where(r >= s, pltpu.roll(c, s, 0), 0.0)
            s *= 2
        o_ref[...] = jnp.where(is_f, c, jnp.where(lane < NH, g, 0.0))

    return pl.pallas_call(
        body, name="gates_prep", grid=(1,),
        in_specs=[pl.BlockSpec((t, LANE), lambda i: (0, gate_tile)), pl.BlockSpec((1, LANE), lambda i: (0, 0))],
        out_specs=pl.BlockSpec((t, LANE), lambda i: (0, 0)),
        out_shape=jax.ShapeDtypeStruct((t, LANE), F32),
        compiler_params=_params("arbitrary"),
    )(proj, bias_tile)


def _gates_bwd(dgate, proj, bias_tile, t, gate_tile):
    def body(dg_ref, g_ref, b_ref, o_ref, s_ref):
        g = g_ref[...] + b_ref[...]
        lane = _iota(g.shape, 1)
        r = _iota(g.shape, 0) % CHUNK
        dsig = 1.0 - _sigmoid(g)
        out = jnp.zeros(g.shape, F32)
        for h in range(NH):
            d = dg_ref[h]
            c = d
            s = 1
            while s < CHUNK:
                c = c + jnp.where(r + s < CHUNK, pltpu.roll(c, t - s, 0), 0.0)
                s *= 2
            di = jnp.broadcast_to(d[:, 0:1], g.shape)
            db = jnp.broadcast_to(c[:, 1:2], g.shape)
            out = out + jnp.where(lane == h, di, 0.0) + jnp.where(lane == NH + h, db * dsig, 0.0)
        o_ref[...] = out.astype(BF16)
        s_ref[...] = jnp.sum(out, axis=0, keepdims=True)

    return pl.pallas_call(
        body, name="gates_bwd", grid=(1,),
        in_specs=[pl.BlockSpec((NH, t, LANE), lambda i: (0, 0, 0)),
                  pl.BlockSpec((t, LANE), lambda i: (0, gate_tile)), pl.BlockSpec((1, LANE), lambda i: (0, 0))],
        out_specs=[pl.BlockSpec((t, LANE), lambda i: (0, 0)), pl.BlockSpec((1, LANE), lambda i: (0, 0))],
        out_shape=[jax.ShapeDtypeStruct((t, LANE), BF16), jax.ShapeDtypeStruct((1, LANE), F32)],
        compiler_params=_params("arbitrary"),
    )(dgate, proj, bias_tile)


def _chunk_gates(gc, gr, h, mprev):
    L = CHUNK
    icol, bcol = gc[:, h:h + 1], gc[:, h + NH:h + NH + 1]
    irow, brow = gr[h:h + 1, :], gr[h + NH:h + NH + 1, :]
    tri = _iota((L, L), 0) >= _iota((L, L), 1)
    log_d = jnp.where(tri, bcol - brow + irow, -jnp.inf)
    inter = bcol + mprev
    mt = jnp.maximum(inter, jnp.max(log_d, axis=1, keepdims=True))
    dw = jnp.exp(log_d - mt)
    iw = jnp.exp(inter - mt)
    g = brow[:, L - 1:L]
    wlog_col = g - bcol + icol
    wlog_row = g - brow + irow
    mnew = jnp.maximum(g + mprev, jnp.max(wlog_row, axis=1, keepdims=True))
    wcol = jnp.exp(wlog_col - mnew)
    decay = jnp.exp(g + mprev - mnew)
    return dw, iw, mt, wcol, decay, mnew


def _mlstm_fwd(proj, gcol, grow, t, wc, dh):
    nc = t // CHUNK
    wm = NH * dh
    assert wc == wm, (wc, wm)
    qoff = 3 * wc // wm
    scale = dh ** -0.5

    def body(q_ref, k_ref, v_ref, gc_ref, gr_ref, h_ref, cs_ref, ns_ref, c_s, n_s, m_s):
        @pl.when(pl.program_id(0) == 0)
        def _():
            c_s[...] = jnp.zeros_like(c_s)
            n_s[...] = jnp.zeros_like(n_s)
            m_s[...] = jnp.zeros_like(m_s)

        gc, gr = gc_ref[...], gr_ref[0]
        for h in range(NH):
            cols = slice(h * dh, (h + 1) * dh)
            mprev = m_s[h, 0:1, 0:1]
            cprev = c_s[h]
            n8 = n_s[h]
            nprev = n8[0:1]
            cs_ref[h] = cprev
            ns_ref[h] = jnp.where(_iota(n8.shape, 0) == 1, mprev, n8)

            dw, iw, mt, wcol, decay, mnew = _chunk_gates(gc, gr, h, mprev)
            qs = q_ref[:, cols] * scale
            k = k_ref[:, cols]
            qs_b, k_b, v_b = qs.astype(BF16), k.astype(BF16), v_ref[:, cols].astype(BF16)
            s = _dot(qs_b, k_b, _NT) * dw
            num = _dot(s.astype(BF16), v_b) + iw * _dot(qs_b, cprev.astype(BF16))
            den = jnp.sum(s, axis=1, keepdims=True) + iw * jnp.sum(qs * nprev, axis=1, keepdims=True)
            h_ref[:, cols] = num / jnp.maximum(jnp.abs(den), jnp.exp(-mt))

            wk = wcol * k
            c_s[h] = decay * cprev + _dot(wk.astype(BF16), v_b, _TN)
            n_s[h] = decay * n8 + jnp.sum(wk, axis=0, keepdims=True)
            m_s[h] = jnp.broadcast_to(mnew, m_s.shape[1:])

    grp = lambda off: pl.BlockSpec((CHUNK, wm), lambda c: (c, qoff + off))
    return pl.pallas_call(
        body, name="mlstm_fwd", grid=(nc,),
        in_specs=[grp(0), grp(1), grp(2),
                  pl.BlockSpec((CHUNK, LANE), lambda c: (c, 0)),
                  pl.BlockSpec((1, 8, CHUNK), lambda c: (c, 0, 0))],
        out_specs=[pl.BlockSpec((CHUNK, wm), lambda c: (c, 0)),
                   pl.BlockSpec((NH, None, dh, dh), lambda c: (0, c, 0, 0)),
                   pl.BlockSpec((NH, None, 8, dh), lambda c: (0, c, 0, 0))],
        out_shape=[jax.ShapeDtypeStruct((t, wm), F32),
                   jax.ShapeDtypeStruct((NH, nc, dh, dh), F32),
                   jax.ShapeDtypeStruct((NH, nc, 8, dh), F32)],
        scratch_shapes=[pltpu.VMEM((NH, dh, dh), F32), pltpu.VMEM((NH, 8, dh), F32), pltpu.VMEM((NH, 8, LANE), F32)],
        compiler_params=_params("arbitrary"),
    )(proj, proj, proj, gcol, grow)


def _mlstm_bwd(proj, gcol, grow, hval, dh_in, cs, ns, t, wc, dh):
    nc = t // CHUNK
    wm = NH * dh
    assert wc == wm, (wc, wm)
    qoff = 3 * wc // wm
    scale = dh ** -0.5
    L = CHUNK

    def body(q_ref, k_ref, v_ref, gc_ref, gr_ref, h_ref, dh_ref, cs_ref, ns_ref,
             dq_ref, dk_ref, dv_ref, dg_ref, dc_s, dn_s):
        @pl.when(pl.program_id(0) == 0)
        def _():
            dc_s[...] = jnp.zeros_like(dc_s)
            dn_s[...] = jnp.zeros_like(dn_s)

        gc, gr = gc_ref[...], gr_ref[0]
        eye = _iota((L, L), 0) == _iota((L, L), 1)
        lane = _iota((L, LANE), 1)
        last = _iota((L, 1), 0) == L - 1
        for h in range(NH):
            cols = slice(h * dh, (h + 1) * dh)
            ns8 = ns_ref[h]
            nprev = ns8[0:1]
            mprev = ns8[1:2, 0:1]
            cprev = cs_ref[h]
            dcn = dc_s[h]
            dn8 = dn_s[h]
            dnn = dn8[0:1]

            dw, iw, mt, wcol, decay, _ = _chunk_gates(gc, gr, h, mprev)
            qs = q_ref[:, cols] * scale
            k = k_ref[:, cols]
            qs_b, k_b, v_b = qs.astype(BF16), k.astype(BF16), v_ref[:, cols].astype(BF16)
            qk = _dot(qs_b, k_b, _NT)
            s = qk * dw
            den = jnp.sum(s, axis=1, keepdims=True) + iw * jnp.sum(qs * nprev, axis=1, keepdims=True)
            emt = jnp.exp(-mt)
            r = 1.0 / jnp.maximum(jnp.abs(den), emt)
            dout = dh_ref[:, cols]
            dnum = dout * r
            dden = (-jnp.sum(dout * h_ref[:, cols], axis=1, keepdims=True) * r
                    * jnp.where(jnp.abs(den) > emt, jnp.sign(den), 0.0))
            dnum_b = dnum.astype(BF16)
            cprev_b = cprev.astype(BF16)
            dcn_b = dcn.astype(BF16)

            gd = (_dot(dnum_b, v_b, _NT) + dden) * dw
            gd_b = gd.astype(BF16)
            dqs_inter = iw * (_dot(dnum_b, cprev_b, _NT) + dden * nprev)
            dqs = _dot(gd_b, k_b) + dqs_inter
            dk_inter = wcol * (_dot(v_b, dcn_b, _NT) + dnn)
            dk = _dot(gd_b, qs_b, _TN) + dk_inter
            wk = wcol * k
            dv = _dot(s.astype(BF16), dnum_b, _TN) + _dot(wk.astype(BF16), dcn_b)

            e = gd * qk
            e_cols = jnp.sum(jnp.where(eye, jnp.sum(e, axis=0, keepdims=True), 0.0), axis=1, keepdims=True)
            k_inter = jnp.sum(k * dk_inter, axis=1, keepdims=True)
            rq = jnp.sum(e, axis=1, keepdims=True) + jnp.sum(qs * dqs_inter, axis=1, keepdims=True)
            rk = e_cols + k_inter
            hsum = jnp.sum(k_inter, axis=0, keepdims=True)
            jdec = decay * (jnp.sum(jnp.sum(dcn * cprev, axis=1, keepdims=True), axis=0, keepdims=True)
                            + jnp.sum(dnn * nprev, axis=1, keepdims=True))
            db = rq - rk + jnp.where(last, hsum + jdec, 0.0)
            dg_ref[h] = jnp.where(lane == 0, rk, jnp.where(lane == 1, db, 0.0))

            dq_ref[:, cols] = (dqs * scale).astype(BF16)
            dk_ref[:, cols] = dk.astype(BF16)
            dv_ref[:, cols] = dv.astype(BF16)

            iq = iw * qs
            dc_s[h] = decay * dcn + _dot(iq.astype(BF16), dnum_b, _TN)
            dn_s[h] = decay * dn8 + jnp.sum(iq * dden, axis=0, keepdims=True)

    rc = lambda c: nc - 1 - c
    grp = lambda off: pl.BlockSpec((L, wm), lambda c: (rc(c), qoff + off))
    hm = pl.BlockSpec((L, wm), lambda c: (rc(c), 0))
    act = jax.ShapeDtypeStruct((t, wm), BF16)
    return pl.pallas_call(
        body, name="mlstm_bwd", grid=(nc,),
        in_specs=[grp(0), grp(1), grp(2),
                  pl.BlockSpec((L, LANE), lambda c: (rc(c), 0)),
                  pl.BlockSpec((1, 8, L), lambda c: (rc(c), 0, 0)),
                  hm, hm,
                  pl.BlockSpec((NH, None, dh, dh), lambda c: (0, rc(c), 0, 0)),
                  pl.BlockSpec((NH, None, 8, dh), lambda c: (0, rc(c), 0, 0))],
        out_specs=[hm, hm, hm, pl.BlockSpec((NH, L, LANE), lambda c: (0, rc(c), 0))],
        out_shape=[act, act, act, jax.ShapeDtypeStruct((NH, t, LANE), F32)],
        scratch_shapes=[pltpu.VMEM((NH, dh, dh), F32), pltpu.VMEM((NH, 8, dh), F32)],
        compiler_params=_params("arbitrary"),
    )(proj, proj, proj, gcol, grow, hval, dh_in, cs, ns)


def _head_norm(hv):
    mu = jnp.mean(hv, axis=1, keepdims=True)
    hc = hv - mu
    rstd = lax.rsqrt(jnp.mean(hc * hc, axis=1, keepdims=True) + HN_EPS)
    return hc * rstd, rstd


def _hnorm_fwd(hval, proj, gain, y, t, wc, dh, tr=256):
    ooff = 3 * wc // dh + 3 * NH

    def body(h_ref, o_ref, g_ref, y_in, y_ref):
        hhat, _ = _head_norm(h_ref[...])
        y_ref[...] = (_sigmoid(o_ref[...]) * hhat * g_ref[...]).astype(BF16)

    return pl.pallas_call(
        body, name="hnorm_fwd", grid=(t // tr, NH),
        in_specs=[pl.BlockSpec((tr, dh), lambda i, h: (i, h)),
                  pl.BlockSpec((tr, dh), lambda i, h: (i, ooff + h)),
                  pl.BlockSpec((1, dh), lambda i, h: (0, h)),
                  pl.BlockSpec(memory_space=pl.ANY)],
        out_specs=pl.BlockSpec((None, tr, dh), lambda i, h: (1, i, h)),
        out_shape=jax.ShapeDtypeStruct(y.shape, BF16),
        input_output_aliases={3: 0},
        compiler_params=_params("parallel", "parallel"),
    )(hval, proj, gain, y)


def _hnorm_bwd(dy, hval, proj, gain, t, wc, dh, tr=256):
    ooff = 3 * wc // dh + 3 * NH
    yoff = wc // dh

    def body(dy_ref, h_ref, o_ref, g_ref, do_ref, dh_ref, dg_ref):
        i = pl.program_id(1)
        hhat, rstd = _head_norm(h_ref[...])
        gain_v = g_ref[...]
        sig = _sigmoid(o_ref[...])
        d = dy_ref[...]
        do_ref[...] = (d * hhat * gain_v * sig * (1.0 - sig)).astype(BF16)
        dhn = d * sig
        part = jnp.sum(dhn * hhat, axis=0, keepdims=True)

        @pl.when(i == 0)
        def _():
            dg_ref[...] = part

        @pl.when(i > 0)
        def _():
            dg_ref[...] += part

        dhat = dhn * gain_v
        dh_ref[...] = rstd * (dhat - jnp.mean(dhat, axis=1, keepdims=True)
                              - hhat * jnp.mean(dhat * hhat, axis=1, keepdims=True))

    blk = lambda off: pl.BlockSpec((tr, dh), lambda h, i: (i, off + h))
    return pl.pallas_call(
        body, name="hnorm_bwd", grid=(NH, t // tr),
        in_specs=[blk(yoff), blk(0), blk(ooff), pl.BlockSpec((1, dh), lambda h, i: (0, h))],
        out_specs=[blk(0), blk(0), pl.BlockSpec((1, dh), lambda h, i: (0, h))],
        out_shape=[jax.ShapeDtypeStruct((t, NH * dh), BF16), jax.ShapeDtypeStruct((t, NH * dh), F32),
                   jax.ShapeDtypeStruct((1, NH * dh), F32)],
        compiler_params=_params("parallel", "arbitrary"),
    )(dy, hval, proj, gain)


def _ln_stats(z):
    mu = jnp.mean(z, axis=1, keepdims=True)
    zc = z - mu
    rstd = lax.rsqrt(jnp.mean(zc * zc, axis=1, keepdims=True) + LN_EPS)
    return zc * rstd, rstd


def _ln_bwd(dy, xhat, rstd, g):
    dxh = dy * g
    return rstd * (dxh - jnp.mean(dxh, axis=1, keepdims=True) - xhat * jnp.mean(dxh * xhat, axis=1, keepdims=True))


def _accum(ref, i, part):
    @pl.when(i == 0)
    def _():
        ref[...] = part

    @pl.when(i > 0)
    def _():
        ref[...] += part


def _ln1_fwd(x, mix, g, b, tr=256):
    t, d = x.shape

    def body(x_ref, m_ref, g_ref, b_ref, xh_ref, rs_ref, xb_ref):
        xhat, rstd = _ln_stats(ALPHA * x_ref[...] + m_ref[...])
        xh_ref[...] = xhat
        rs_ref[...] = rstd
        xb_ref[...] = (xhat * g_ref[...] + b_ref[...]).astype(BF16)

    row = pl.BlockSpec((tr, d), lambda i: (i, 0))
    vec = pl.BlockSpec((1, d), lambda i: (0, 0))
    return pl.pallas_call(
        body, name="ln1_fwd", grid=(t // tr,),
        in_specs=[row, row, vec, vec],
        out_specs=[row, pl.BlockSpec((tr, 1), lambda i: (i, 0)), row],
        out_shape=[jax.ShapeDtypeStruct((t, d), F32), jax.ShapeDtypeStruct((t, 1), F32),
                   jax.ShapeDtypeStruct((t, d), BF16)],
        compiler_params=_params("parallel"),
    )(x, mix, g, b)


def _ln2_loss(xhat1, g1, b1, ff, target, g2, b2, tr=256):
    t, d = ff.shape

    def body(xh_ref, g1_ref, b1_ref, f_ref, t_ref, g_ref, b_ref, dz_ref, dzb_ref, dg_ref, db_ref, l_ref):
        i = pl.program_id(0)
        x1 = xh_ref[...] * g1_ref[...] + b1_ref[...]
        xhat, rstd = _ln_stats(ALPHA * x1 + f_ref[...])
        gv = g_ref[...]
        e = xhat * gv + b_ref[...] - t_ref[...]
        lsum = jnp.sum(jnp.sum(e * e, axis=1, keepdims=True), axis=0, keepdims=True) * (0.5 / d)
        dy = e * (1.0 / d)
        _accum(dg_ref, i, jnp.sum(dy * xhat, axis=0, keepdims=True))
        _accum(db_ref, i, jnp.sum(dy, axis=0, keepdims=True))
        _accum(l_ref, i, jnp.broadcast_to(lsum, l_ref.shape))
        dz = _ln_bwd(dy, xhat, rstd, gv)
        dz_ref[...] = dz
        dzb_ref[...] = dz.astype(BF16)

    row = pl.BlockSpec((tr, d), lambda i: (i, 0))
    vec = pl.BlockSpec((1, d), lambda i: (0, 0))
    return pl.pallas_call(
        body, name="ln2_loss", grid=(t // tr,),
        in_specs=[row, vec, vec, row, row, vec, vec],
        out_specs=[row, row, vec, vec, pl.BlockSpec((8, LANE), lambda i: (0, 0))],
        out_shape=[jax.ShapeDtypeStruct((t, d), F32), jax.ShapeDtypeStruct((t, d), BF16),
                   jax.ShapeDtypeStruct((1, d), F32), jax.ShapeDtypeStruct((1, d), F32),
                   jax.ShapeDtypeStruct((8, LANE), F32)],
        compiler_params=_params("arbitrary"),
    )(xhat1, g1, b1, ff, target, g2, b2)


def _ln1_bwd(dz2, dffn, xhat1, rstd1, g1, tr=256):
    t, d = dz2.shape

    def body(a_ref, f_ref, xh_ref, rs_ref, g_ref, dz_ref, dzb_ref, dg_ref, db_ref):
        i = pl.program_id(0)
        dy = ALPHA * a_ref[...] + f_ref[...]
        xhat = xh_ref[...]
        _accum(dg_ref, i, jnp.sum(dy * xhat, axis=0, keepdims=True))
        _accum(db_ref, i, jnp.sum(dy, axis=0, keepdims=True))
        dz = _ln_bwd(dy, xhat, rs_ref[...], g_ref[...])
        dz_ref[...] = dz
        dzb_ref[...] = dz.astype(BF16)

    row = pl.BlockSpec((tr, d), lambda i: (i, 0))
    vec = pl.BlockSpec((1, d), lambda i: (0, 0))
    return pl.pallas_call(
        body, name="ln1_bwd", grid=(t // tr,),
        in_specs=[row, row, row, pl.BlockSpec((tr, 1), lambda i: (i, 0)), vec],
        out_specs=[row, row, vec, vec],
        out_shape=[jax.ShapeDtypeStruct((t, d), F32), jax.ShapeDtypeStruct((t, d), BF16),
                   jax.ShapeDtypeStruct((1, d), F32), jax.ShapeDtypeStruct((1, d), F32)],
        compiler_params=_params("arbitrary"),
    )(dz2, dffn, xhat1, rstd1, g1)


def _ffn_act_fwd(hid0, w_fc, b_fc, t, dff):
    nb = dff // LANE

    def body(hv_ref, hg_ref, wv_ref, wg_ref, bv_ref, bg_ref, a_ref):
        val = _conv(hv_ref[...], wv_ref[...]) + bv_ref[...]
        gate = _conv(hg_ref[...], wg_ref[...]) + bg_ref[...]
        a_ref[...] = (gate * _sigmoid(gate) * val).astype(BF16)

    col = lambda off: pl.BlockSpec((t, LANE), lambda j: (0, j + off))
    w3 = lambda off: pl.BlockSpec((3, LANE), lambda j: (0, j + off))
    w1 = lambda off: pl.BlockSpec((1, LANE), lambda j: (0, j + off))
    return pl.pallas_call(
        body, name="ffn_act_fwd", grid=(nb,),
        in_specs=[col(0), col(nb), w3(0), w3(nb), w1(0), w1(nb)],
        out_specs=col(0),
        out_shape=jax.ShapeDtypeStruct((t, dff), BF16),
        compiler_params=_params("parallel"),
    )(hid0, hid0, w_fc, w_fc, b_fc, b_fc)


def _ffn_act_bwd(da, hid0, w_fc, b_fc, t, dff):
    nb = dff // LANE

    def body(da_ref, hv_ref, hg_ref, wv_ref, wg_ref, bv_ref, bg_ref,
             dhv_ref, dhg_ref, dwv_ref, dwg_ref, dbv_ref, dbg_ref):
        hv, hg, wv, wg = hv_ref[...], hg_ref[...], wv_ref[...], wg_ref[...]
        rv, rg = _rolled(hv), _rolled(hg)
        val = _conv(hv, wv, rv) + bv_ref[...]
        gate = _conv(hg, wg, rg) + bg_ref[...]
        sig = _sigmoid(gate)
        d = da_ref[...]
        dsig = d * sig
        dval = dsig * gate
        dgate = dsig * val * (1.0 + gate * (1.0 - sig))
        dhv_ref[...] = _conv_t(dval, wv).astype(BF16)
        dhg_ref[...] = _conv_t(dgate, wg).astype(BF16)
        dwv_ref[...] = _conv_dw(dval, hv, rv)
        dwg_ref[...] = _conv_dw(dgate, hg, rg)
        dbv_ref[...] = jnp.sum(dval, axis=0, keepdims=True)
        dbg_ref[...] = jnp.sum(dgate, axis=0, keepdims=True)

    col = lambda off: pl.BlockSpec((t, LANE), lambda j: (0, j + off))
    w3 = lambda off: pl.BlockSpec((3, LANE), lambda j: (0, j + off))
    w1 = lambda off: pl.BlockSpec((1, LANE), lambda j: (0, j + off))
    s3 = jax.ShapeDtypeStruct((3, dff), F32)
    s1 = jax.ShapeDtypeStruct((1, dff), F32)
    return pl.pallas_call(
        body, name="ffn_act_bwd", grid=(nb,),
        in_specs=[col(0), col(0), col(nb), w3(0), w3(nb), w1(0), w1(nb)],
        out_specs=[col(0), col(0), w3(0), w3(0), w1(0), w1(0)],
        out_shape=[jax.ShapeDtypeStruct((t, dff), BF16)] * 2 + [s3, s3, s1, s1],
        compiler_params=_params("parallel"),
    )(da, hid0, hid0, w_fc, w_fc, b_fc, b_fc)


class _Ready:
    def __init__(self, **weights):
        self.weights = weights

    def begin(self, after):
        return None

    def forward(self, name, after):
        return None

    def get(self, name, after):
        return self.weights[name]


class _Kept:
    def __init__(self):
        self.grads = {}

    def start(self, name, grad):
        self.grads[name] = grad
        return None

    def relay(self, name, after):
        return None

    def meanwhile(self, small, loss, after):
        return None


def _behind(a, token):
    return a if token is None else a + token[0:1, 0:1].reshape((1,) * a.ndim)


def _local_step(x, target, w_in, b_gates, w_sc, gain, w_out, ln1_g, ln1_b, w_up, w_fc, b_fc, w_down, ln2_g, ln2_b,
                gx=None, wx=None):
    t, d = x.shape
    wc = d // 2
    dh = (d - wc) // NH
    wm = NH * dh
    dff = w_fc.shape[1] // 2
    if wx is None:
        wx = _Ready(w_out=w_out, w_up=w_up, w_down=w_down)
    ninp = w_in.shape[0]
    nin = 3 * wc + 4 * wm
    gate_tile = nin // LANE
    nc = t // CHUNK
    bias_tile = jnp.pad(b_gates, ((0, 0), (0, LANE - 2 * NH)))

    x_b = x.astype(BF16)
    proj = _matmul(x_b, w_in, "nt", F32, "proj", tm=1024, tn=1152, tk=d, after=wx.begin(w_in))
    y = _sconv_fwd(proj, w_sc, t, wc)
    gcol = _gates_prep(proj, bias_tile, t, gate_tile)
    grow = gcol[:, :8].T.reshape(8, nc, CHUNK).transpose(1, 0, 2)
    hval, cs, ns = _mlstm_fwd(proj, gcol, grow, t, wc, dh)
    y = _hnorm_fwd(hval, proj, gain, y, t, wc, dh)
    tok = wx.forward("w_up", wx.forward("w_out", y))
    w_out = wx.get("w_out", tok)
    mix = _matmul(y, w_out, "nn", F32, "out_proj", tm=512, tn=1024, tk=wc, a_blocked=True, after=tok)
    xhat1, rstd1, x1_b = _ln1_fwd(x, mix, ln1_g, ln1_b)
    tok = wx.forward("w_down", x1_b)
    w_up = wx.get("w_up", tok)
    wsl = w_up.shape[2]
    hid0 = _matmul(x1_b, w_up, "nn", F32, "ffn_up", tm=512, tn=wsl, tk=d, b_blocked=True, after=tok)
    act = _ffn_act_fwd(hid0, w_fc, b_fc, t, dff)
    w_down = wx.get("w_down", act)
    ff = _matmul(act, w_down, "nn", F32, "ffn_down", tm=1024, tn=512, tk=dff)
    dz2, dz2_b, d_ln2_g, d_ln2_b, loss = _ln2_loss(xhat1, ln1_g, ln1_b, ff, target, ln2_g, ln2_b)

    if gx is None:
        gx = _Kept()
    d_w_down = _matmul(act, dz2_b, "tn", BF16, "ffn_down_dw", tm=512, tn=1024, tk=t)
    d_act = _matmul(dz2_b, w_down, "nt", F32, "ffn_down_dx", tm=1024, tn=512, tk=d, after=gx.start("w_down", d_w_down))
    *d_hid0, dwv, dwg, dbv, dbg = _ffn_act_bwd(d_act, hid0, w_fc, _behind(b_fc, gx.relay("w_down", d_act)), t, dff)
    d_w_fc = jnp.concatenate([dwv, dwg], axis=1)
    d_b_fc = jnp.concatenate([dbv, dbg], axis=1)
    d_hid0 = tuple(d_hid0[:2])
    d_w_up = _matmul(x1_b, d_hid0, "tn", BF16, "ffn_up_dw", tm=512, tn=wsl, tk=t, o_width=wsl)
    d_x1_ffn = _matmul(d_hid0, w_up, "nt", F32, "ffn_up_dx", tm=1024, tn=1024, tk=wsl, b_blocked=True,
                       after=gx.start("w_up", d_w_up))
    dz1, dz1_b, d_ln1_g, d_ln1_b = _ln1_bwd(dz2, d_x1_ffn, xhat1, rstd1, _behind(ln1_g, gx.relay("w_up", d_x1_ffn)))

    d_w_out = _matmul(y, dz1_b, "tn", BF16, "out_proj_dw", tm=512, tn=1024, tk=t, a_blocked=True)
    dy = _matmul(dz1_b, w_out, "nt", F32, "out_proj_dx", tm=512, tn=1024, tk=d, after=gx.start("w_out", d_w_out))
    dcb, dcc, dch, d_w_sc = _sconv_bwd(dy, proj, _behind(w_sc, gx.relay("w_out", dy)), t, wc)
    d_o, d_hval, d_gain = _hnorm_bwd(dy, hval, proj, gain, t, wc, dh)
    dq, dk, dv, dgate = _mlstm_bwd(proj, gcol, grow, hval, d_hval, cs, ns, t, wc, dh)
    dgt, d_b_gates = _gates_bwd(dgate, proj, bias_tile, t, gate_tile)
    pad = jnp.zeros((t, ninp - nin - LANE), BF16)
    d_proj = jnp.concatenate([dcb, dcc, dch, dq, dk, dv, d_o, dgt, pad], axis=1)
    d_w_in = _matmul(d_proj, x_b, "tn", BF16, "proj_dw", tm=IN_SLAB, tn=1024, tk=t)
    small = dict(b_gates=d_b_gates[:, :2 * NH], w_sc_conv=d_w_sc, mh_gain=d_gain, ln1_g=d_ln1_g, ln1_b=d_ln1_b,
                 w_ffn_conv=d_w_fc, b_ffn_conv=d_b_fc, ln2_g=d_ln2_g, ln2_b=d_ln2_b)
    token = gx.start("w_in", d_w_in.reshape(ninp // IN_SLAB, IN_SLAB, d))
    token = gx.relay("w_in", gx.meanwhile(small, loss, token))
    grad_x = _matmul(d_proj, w_in, "nn", F32, "proj_dx", tm=512, tn=512, tk=ninp, add=dz1, add_scale=ALPHA, after=token)
    return loss, grad_x, small, gx


HBM = pl.BlockSpec(memory_space=pltpu.HBM)


def _place():
    return lax.axis_index("x"), lax.axis_index("y"), lax.axis_index("c")


def _index(p):
    return 4 * p[0] + 2 * p[1] + p[2]


def _all_gather(arrs, name):
    n = len(arrs)

    def body(*refs):
        ins, outs = refs[:n], refs[n:2 * n]
        send_sems, recv_sems, local_sems = refs[2 * n:]
        x, y, c = _place()
        me, sibling = (x, y, c), (x, y, 1 - c)
        chips = [(1 - x, y), (x, 1 - y), (1 - x, 1 - y)]

        def copy(a, k, block, to, own=False):
            dst = outs[a].at[_index(block)]
            return pltpu.make_async_remote_copy(
                src_ref=ins[a] if own else dst, dst_ref=dst,
                send_sem=send_sems.at[k * n + a], recv_sem=recv_sems.at[k * n + a],
                device_id=to, device_id_type=MESH)

        mine = [pltpu.make_async_copy(ins[a], outs[a].at[_index(me)], local_sems.at[a]) for a in range(n)]
        for cp in mine:
            cp.start()
        first = []
        for a in range(n):
            first.append(copy(a, 0, me, sibling, own=True))
            first += [copy(a, 1 + j, me, (*chip, c), own=True) for j, chip in enumerate(chips)]
        for cp in first:
            cp.start()
        passed = []
        for j, chip in enumerate(chips):
            for a in range(n):
                copy(a, 1 + j, (*chip, c), me).wait_recv()
                cp = copy(a, 4 + j, (*chip, c), sibling)
                cp.start()
                passed.append(cp)
        for a in range(n):
            copy(a, 0, sibling, me).wait_recv()
            for j, chip in enumerate(chips):
                copy(a, 4 + j, (*chip, 1 - c), me).wait_recv()
        for cp in first + passed:
            cp.wait_send()
        for cp in mine:
            cp.wait()

    return pl.pallas_call(
        body, name=name, in_specs=[HBM] * n, out_specs=[HBM] * n,
        out_shape=[jax.ShapeDtypeStruct((N_DEV,) + a.shape, a.dtype) for a in arrs],
        scratch_shapes=[pltpu.SemaphoreType.DMA((7 * n,)), pltpu.SemaphoreType.DMA((7 * n,)),
                        pltpu.SemaphoreType.DMA((n,))],
    )(*arrs)


SEM = pl.BlockSpec(memory_space=pltpu.SEMAPHORE)
EFFECT = pltpu.SideEffectType.DATAFLOW_SIDE_EFFECTING


def _chips(x, y):
    return [(1 - x, y), (x, 1 - y), (1 - x, 1 - y)]


N_CHIP = N_DEV // 2


def _pair_route(x, y, c):
    return [((x, y, 1 - c), 2 * q + (1 - c), q, q) for q in range(N_CHIP)]


def _chip_route(x, y, c):
    mine = 2 * x + y
    return [((*chip, c), 2 * chip[0] + chip[1], mine, 2 * chip[0] + chip[1]) for chip in _chips(x, y)]


def _exchange_pieces(g_ref, land_ref, width, tail):
    if not tail:
        return [(lambda i: g_ref.at[i], lambda s: land_ref.at[s])]
    return [(lambda i: g_ref.at[i], lambda s: land_ref.at[s, pl.ds(0, width), :]),
            (lambda i: g_ref.at[i + 1, pl.ds(0, LANE), :], lambda s: land_ref.at[s, pl.ds(width, LANE), :])]


def _exchange_start(grad, route, tail, name):
    width = grad.shape[1]
    n_p = 2 if tail else 1
    n_c = len(route(0, 0, 0))
    land_shape = (N_CHIP, width + (LANE if tail else 0), grad.shape[2])

    def body(g_ref, land_ref, send_sems, recv_sems, g_thru, land_thru, token):
        for j, (peer, slab, slot, _) in enumerate(route(*_place())):
            for p, (src, dst) in enumerate(_exchange_pieces(g_ref, land_ref, width, tail)):
                pltpu.make_async_remote_copy(src_ref=src(slab), dst_ref=dst(slot), send_sem=send_sems.at[j * n_p + p],
                                             recv_sem=recv_sems.at[j * n_p + p], device_id=peer,
                                             device_id_type=MESH).start()
        token[...] = jnp.zeros_like(token)

    return pl.pallas_call(
        body, name=name,
        out_shape=(pltpu.SemaphoreType.DMA((n_c * n_p,)), pltpu.SemaphoreType.DMA((n_c * n_p,)),
                   pltpu.HBM(grad.shape, grad.dtype), pltpu.HBM(land_shape, grad.dtype),
                   jax.ShapeDtypeStruct((8, LANE), F32)),
        in_specs=(HBM, HBM), out_specs=(SEM, SEM, HBM, HBM, pl.BlockSpec(memory_space=pltpu.VMEM)),
        input_output_aliases={0: 2, 1: 3},
        compiler_params=pltpu.CompilerParams(has_side_effects=EFFECT),
    )(pltpu.with_memory_space_constraint(grad, pltpu.HBM),
      pltpu.with_memory_space_constraint(lax.empty(land_shape, grad.dtype), pltpu.HBM))


def _exchange_wait(send_sems, recv_sems, g_thru, land_thru, after, route, tail, name):
    width = g_thru.shape[1]
    n_p = 2 if tail else 1

    def body(g_ref, land_ref, send_sems, recv_sems, after_ref, g_dead, got_ref):
        for j, (peer, slab, _, slot) in enumerate(route(*_place())):
            for p, (src, dst) in enumerate(_exchange_pieces(g_ref, land_ref, width, tail)):
                cp = pltpu.make_async_remote_copy(src_ref=src(slab), dst_ref=dst(slot),
                                                  send_sem=send_sems.at[j * n_p + p], recv_sem=recv_sems.at[j * n_p + p],
                                                  device_id=peer, device_id_type=MESH)
                cp.wait_send()
                cp.wait_recv()

    return pl.pallas_call(
        body, name=name,
        out_shape=(pltpu.HBM(g_thru.shape, g_thru.dtype), pltpu.HBM(land_thru.shape, land_thru.dtype)),
        in_specs=(HBM, HBM, SEM, SEM, pl.BlockSpec(memory_space=pl.ANY)), out_specs=(HBM, HBM),
        input_output_aliases={0: 0, 1: 1},
        compiler_params=pltpu.CompilerParams(has_side_effects=EFFECT),
    )(g_thru, land_thru, send_sems, recv_sems, after)


def _pair_add(grad, pair, core, tail, name):
    rows, cols = grad.shape[1], grad.shape[2]
    total = pair.shape[1]

    def body(core_ref, *refs):
        if tail:
            g_ref, t_ref, p_ref, o_ref = refs
            o_ref[0:rows, :] = (g_ref[...].astype(F32) + p_ref[0:rows, :].astype(F32)).astype(BF16)
            o_ref[rows:total, :] = (t_ref[...].astype(F32) + p_ref[rows:total, :].astype(F32)).astype(BF16)
        else:
            g_ref, p_ref, o_ref = refs
            o_ref[...] = (g_ref[...].astype(F32) + p_ref[...].astype(F32)).astype(BF16)

    if tail:
        tc = _fit(cols, 512)
        grid = (N_CHIP, cols // tc)
        slab = pl.BlockSpec((None, total, tc), lambda q, i, core_ref: (q, 0, i))
        in_specs = [pl.BlockSpec((None, rows, tc), lambda q, i, core_ref: (2 * q + core_ref[0], 0, i)),
                    pl.BlockSpec((None, LANE, tc), lambda q, i, core_ref: (2 * q + core_ref[0] + 1, 0, i))]
    else:
        tr = _rows(rows, 256)
        grid = (N_CHIP, rows // tr)
        slab = pl.BlockSpec((None, tr, cols), lambda q, i, core_ref: (q, i, 0))
        in_specs = [pl.BlockSpec((None, tr, cols), lambda q, i, core_ref: (2 * q + core_ref[0], i, 0))]
    return pl.pallas_call(
        body, name=name,
        grid_spec=pltpu.PrefetchScalarGridSpec(num_scalar_prefetch=1, grid=grid,
                                               in_specs=in_specs + [slab], out_specs=slab),
        out_shape=jax.ShapeDtypeStruct(pair.shape, BF16),
        compiler_params=_params("parallel", "parallel"),
    )(core, *([grad, grad] if tail else [grad]), pair)


def _gather_start(blocks, after, name):
    n = len(blocks)
    lands = [(N_DEV,) + b.shape for b in blocks]

    def body(*refs):
        b_refs, land_refs = refs[:n], refs[n:2 * n]
        send_sems, recv_sems = refs[2 * n + 1:3 * n + 1], refs[3 * n + 1:4 * n + 1]
        token = refs[-1]
        x, y, c = _place()
        me = _index((x, y, c))
        for a in range(n):
            for k, to in enumerate([(x, y, 1 - c)] + [(*chip, c) for chip in _chips(x, y)]):
                pltpu.make_async_remote_copy(src_ref=b_refs[a], dst_ref=land_refs[a].at[me], send_sem=send_sems[a].at[k],
                                             recv_sem=recv_sems[a].at[k], device_id=to, device_id_type=MESH).start()
        token[...] = jnp.zeros_like(token)

    sems = [pltpu.SemaphoreType.DMA((4,))] * n
    out = pl.pallas_call(
        body, name=name,
        out_shape=(*sems, *sems, *[pltpu.HBM(b.shape, b.dtype) for b in blocks],
                   *[pltpu.HBM(s, b.dtype) for s, b in zip(lands, blocks)], jax.ShapeDtypeStruct((8, LANE), F32)),
        in_specs=(*[HBM] * (2 * n), pl.BlockSpec(memory_space=pl.ANY)),
        out_specs=(*[SEM] * (2 * n), *[HBM] * (2 * n), pl.BlockSpec(memory_space=pltpu.VMEM)),
        input_output_aliases={i: 2 * n + i for i in range(2 * n)},
        compiler_params=pltpu.CompilerParams(has_side_effects=EFFECT),
    )(*[pltpu.with_memory_space_constraint(b, pltpu.HBM) for b in blocks],
      *[pltpu.with_memory_space_constraint(lax.empty(s, b.dtype), pltpu.HBM) for s, b in zip(lands, blocks)], after)
    return [(out[a], out[n + a], out[2 * n + a], out[3 * n + a]) for a in range(n)], out[-1]


def _gather_forward(send_sems, recv_sems, b_thru, land_thru, after, name):
    def body(b_ref, land_ref, send_sems, recv_sems, after_ref, b_dead, land_out, send2, recv2, token):
        x, y, c = _place()
        sibling = (x, y, 1 - c)
        for k, frm in enumerate([sibling] + [(*chip, c) for chip in _chips(x, y)]):
            cp = pltpu.make_async_remote_copy(src_ref=b_ref, dst_ref=land_ref.at[_index(frm)], send_sem=send_sems.at[k],
                                              recv_sem=recv_sems.at[k], device_id=frm, device_id_type=MESH)
            cp.wait_send()
            cp.wait_recv()
        for j, chip in enumerate(_chips(x, y)):
            slot = land_ref.at[_index((*chip, c))]
            pltpu.make_async_remote_copy(src_ref=slot, dst_ref=slot, send_sem=send2.at[j], recv_sem=recv2.at[j],
                                         device_id=sibling, device_id_type=MESH).start()
        token[...] = jnp.zeros_like(token)

    return pl.pallas_call(
        body, name=name,
        out_shape=(pltpu.HBM(b_thru.shape, b_thru.dtype), pltpu.HBM(land_thru.shape, land_thru.dtype),
                   pltpu.SemaphoreType.DMA((3,)), pltpu.SemaphoreType.DMA((3,)), jax.ShapeDtypeStruct((8, LANE), F32)),
        in_specs=(HBM, HBM, SEM, SEM, pl.BlockSpec(memory_space=pl.ANY)),
        out_specs=(HBM, HBM, SEM, SEM, pl.BlockSpec(memory_space=pltpu.VMEM)),
        input_output_aliases={0: 0, 1: 1},
        compiler_params=pltpu.CompilerParams(has_side_effects=EFFECT),
    )(b_thru, land_thru, send_sems, recv_sems, after)


def _gather_finish(land_thru, send2, recv2, after, name):
    def body(land_ref, send2, recv2, after_ref, land_out):
        x, y, c = _place()
        for j, chip in enumerate(_chips(x, y)):
            cp = pltpu.make_async_remote_copy(src_ref=land_ref.at[_index((*chip, c))],
                                              dst_ref=land_ref.at[_index((*chip, 1 - c))], send_sem=send2.at[j],
                                              recv_sem=recv2.at[j], device_id=(x, y, 1 - c), device_id_type=MESH)
            cp.wait_send()
            cp.wait_recv()

    return pl.pallas_call(
        body, name=name, out_shape=pltpu.HBM(land_thru.shape, land_thru.dtype),
        in_specs=(HBM, SEM, SEM, pl.BlockSpec(memory_space=pl.ANY)), out_specs=HBM,
        input_output_aliases={0: 0},
        compiler_params=pltpu.CompilerParams(has_side_effects=EFFECT),
    )(land_thru, send2, recv2, after)


class _Gathering:
    def __init__(self, first, later, me):
        started, token = _gather_start(list(first.values()), next(iter(first.values())), "gather1_first")
        cast = [_behind(a, token).astype(BF16) for a in later.values()]
        started_later, self.token = _gather_start(cast, token, "gather1_later")
        self.me, self.state = me, dict(zip([*first, *later], started + started_later))

    def begin(self, after):
        return self.token

    def forward(self, name, after):
        *self.state[name], token = _gather_forward(*self.state[name], after, "gather2_" + name)
        return token

    def get(self, name, after):
        block, land, send2, recv2 = self.state[name]
        land = _gather_finish(land, send2, recv2, after, "gather3_" + name)
        land = lax.dynamic_update_index_in_dim(land, block[None], self.me, 0)
        return land if name not in ("w_out", "w_down") else land.reshape(-1, land.shape[2])


class _Reducing:
    def __init__(self, core, chip, gather_small):
        self.core, self.chip, self.state, self.token, self.gather_small = core, chip, {}, None, gather_small

    def meanwhile(self, small, loss, after):
        self.small_sum = self.gather_small(small, loss, after)
        return self.small_sum

    def start(self, name, grad):
        g = grad if grad.ndim == 3 else grad.reshape(N_DEV, grad.shape[0] // N_DEV, grad.shape[1])
        *self.state[name], token = _exchange_start(g, _pair_route, name == "w_in", "pair_send_" + name)
        return token

    def relay(self, name, after):
        tail = name == "w_in"
        grad, pair = _exchange_wait(*self.state[name], after, _pair_route, tail, "pair_recv_" + name)
        total = _pair_add(grad, pair, self.core, tail, "pair_add_" + name)
        *self.state[name], self.token = _exchange_start(total, _chip_route, False, "chip_send_" + name)
        return self.token

    def finish(self, name, after):
        total, land = _exchange_wait(*self.state[name], after, _chip_route, False, "chip_recv_" + name)
        own = lax.dynamic_index_in_dim(total, self.chip, 0, keepdims=True)
        return lax.dynamic_update_index_in_dim(land, own, self.chip, 0)


def _assemble_w_in(g, ninp):
    _, ph, d = g.shape
    per = IN_SLAB // LANE
    assert ninp == (N_DEV + 1) * IN_SLAB and ph == IN_SLAB + LANE
    tc = _fit(d, 512)

    def body(a_ref, b_ref, o_ref):
        s = pl.program_id(0)
        head = a_ref[0:LANE, :]
        rest = a_ref[LANE:IN_SLAB, :]
        o_ref[0:LANE, :] = (jnp.where(s < N_DEV, head, jnp.zeros_like(head))
                            + jnp.where(s > 0, b_ref[...], jnp.zeros_like(head)))
        o_ref[LANE:IN_SLAB, :] = jnp.where(s < N_DEV, rest, jnp.zeros_like(rest))

    return pl.pallas_call(
        body, name="assemble_w_in", grid=(N_DEV + 1, d // tc),
        in_specs=[pl.BlockSpec((None, ph, tc), lambda s, j: (jnp.minimum(s, N_DEV - 1), 0, j)),
                  pl.BlockSpec((None, LANE, tc), lambda s, j: (jnp.maximum(s, 1) - 1, per, j))],
        out_specs=pl.BlockSpec((IN_SLAB, tc), lambda s, j: (s, j)),
        out_shape=jax.ShapeDtypeStruct((ninp, d), g.dtype),
        compiler_params=_params("parallel", "parallel"),
    )(g, g)


def _rows(n, want):
    t = min(n, want)
    t -= t % 16
    while n % t:
        t -= 16
    return t


def _adam_math(w, g, m, v):
    m2 = ADAM_B1 * m + (1.0 - ADAM_B1) * g
    v2 = ADAM_B2 * v + (1.0 - ADAM_B2) * (g * g)
    m_hat = m2 / (1.0 - ADAM_B1 ** ADAM_STEP)
    v_hat = v2 / (1.0 - ADAM_B2 ** ADAM_STEP)
    return -ADAM_LR * (m_hat / (jnp.sqrt(v_hat) + ADAM_EPS) + ADAM_WD * w), m2, v2


def _slot_sum(r_ref):
    acc = r_ref[0].astype(F32)
    for i in range(1, r_ref.shape[0]):
        acc = acc + r_ref[i].astype(F32)
    return acc


def _shift_w_in(w, ph):
    ws, d = w.shape
    tc = _fit(d, 256)

    def body(w_ref, o_ref, tall):
        tall[...] = jnp.zeros_like(tall)
        tall[0:ws, :] = w_ref[...]
        o_ref[...] = pltpu.roll(tall[...], _index(_place()), 0).astype(BF16)

    return pl.pallas_call(
        body, name="shift_w_in", grid=(d // tc,),
        in_specs=[pl.BlockSpec((ws, tc), lambda j: (0, j))],
        out_specs=pl.BlockSpec((ph, tc), lambda j: (0, j)),
        out_shape=jax.ShapeDtypeStruct((ph, d), BF16),
        scratch_shapes=[pltpu.VMEM((ph, tc), F32)], compiler_params=_params("parallel"),
    )(w)


def _sum_adamw_shifted(r, w, m, v, name):
    _, ph, d = r.shape
    ws = w.shape[0]
    tc = _fit(d, 256)

    def body(r_ref, w_ref, m_ref, v_ref, g_ref, d_ref, m2_ref, v2_ref, tall):
        tall[...] = pltpu.roll(_slot_sum(r_ref), lax.rem(ph - _index(_place()), ph), 0)
        g = tall[0:ws, :]
        g_ref[...] = g
        d_ref[...], m2_ref[...], v2_ref[...] = _adam_math(w_ref[...], g, m_ref[...], v_ref[...])

    blk = pl.BlockSpec((ws, tc), lambda j: (0, j))
    out = jax.ShapeDtypeStruct(w.shape, F32)
    return pl.pallas_call(
        body, name=name, grid=(d // tc,),
        in_specs=[pl.BlockSpec((r.shape[0], ph, tc), lambda j: (0, 0, j)), blk, blk, blk],
        out_specs=[blk] * 4, out_shape=[out] * 4,
        scratch_shapes=[pltpu.VMEM((ph, tc), F32)], compiler_params=_params("parallel"),
    )(r, w, m, v)


def _sum_slots(r, name, tr=128):
    _, rows, cols = r.shape
    tr = _rows(rows, tr)

    def body(r_ref, g_ref):
        g_ref[...] = _slot_sum(r_ref)

    return pl.pallas_call(
        body, name=name, grid=(rows // tr,),
        in_specs=[pl.BlockSpec((r.shape[0], tr, cols), lambda i: (0, i, 0))],
        out_specs=pl.BlockSpec((tr, cols), lambda i: (i, 0)),
        out_shape=jax.ShapeDtypeStruct((rows, cols), F32),
        compiler_params=_params("parallel"),
    )(r)


def _adamw(w, g, m, v, name, tr=256):
    rows, cols = w.shape
    tr = _rows(rows, tr)

    def body(w_ref, g_ref, m_ref, v_ref, d_ref, m2_ref, v2_ref):
        d_ref[...], m2_ref[...], v2_ref[...] = _adam_math(w_ref[...], g_ref[...], m_ref[...], v_ref[...])

    blk = pl.BlockSpec((tr, cols), lambda i: (i, 0))
    out = jax.ShapeDtypeStruct((rows, cols), F32)
    return pl.pallas_call(
        body, name=name, grid=(rows // tr,), in_specs=[blk] * 4, out_specs=[blk] * 3, out_shape=[out] * 3,
        compiler_params=_params("parallel"),
    )(w, g, m, v)


def _sum_adamw(r, w, m, v, name, tr=128):
    rows, cols = w.shape
    tr = _rows(rows, tr)

    def body(r_ref, w_ref, m_ref, v_ref, g_ref, d_ref, m2_ref, v2_ref):
        g = _slot_sum(r_ref)
        g_ref[...] = g
        d_ref[...], m2_ref[...], v2_ref[...] = _adam_math(w_ref[...], g, m_ref[...], v_ref[...])

    blk = pl.BlockSpec((tr, cols), lambda i: (i, 0))
    out = jax.ShapeDtypeStruct((rows, cols), F32)
    return pl.pallas_call(
        body, name=name, grid=(rows // tr,),
        in_specs=[pl.BlockSpec((r.shape[0], tr, cols), lambda i: (0, i, 0)), blk, blk, blk],
        out_specs=[blk] * 4, out_shape=[out] * 4,
        compiler_params=_params("parallel"),
    )(r, w, m, v)


def _pack(pieces, sizes):
    flat = [jnp.pad(p.reshape(-1).astype(F32), (0, s - p.size)) for p, s in zip(pieces, sizes)]
    total = sum(sizes)
    padded = -(-total // (16 * LANE)) * (16 * LANE)
    return jnp.pad(jnp.concatenate(flat), (0, padded - total)).reshape(-1, LANE)


def _unpack(packed, shapes, sizes):
    flat = packed.reshape(-1)
    out, off = [], 0
    for shp, s in zip(shapes, sizes):
        n = 1
        for k in shp:
            n *= k
        out.append(flat[off:off + n].reshape(shp))
        off += s
    return out


def _lanes(n):
    return -(-n // LANE) * LANE


WEIGHTS = ("w_in", "b_gates", "w_sc_conv", "mh_gain", "w_out", "ln1_g", "ln1_b", "w_up", "w_ffn_conv", "b_ffn_conv",
           "w_down", "ln2_g", "ln2_b")
BIG = ("w_in", "w_out", "w_up", "w_down")
SMALL = tuple(n for n in WEIGHTS if n not in BIG)


def kernel(x, w_in, b_gates, w_sc_conv, mh_gain, w_out, ln1_g, ln1_b, w_up, w_ffn_conv, b_ffn_conv, w_down, ln2_g, ln2_b, loss_target, m_w_in, m_b_gates, m_w_sc_conv, m_mh_gain, m_w_out, m_ln1_g, m_ln1_b, m_w_up, m_w_ffn_conv, m_b_ffn_conv, m_w_down, m_ln2_g, m_ln2_b, v_w_in, v_b_gates, v_w_sc_conv, v_mh_gain, v_w_out, v_ln1_g, v_ln1_b, v_w_up, v_w_ffn_conv, v_b_ffn_conv, v_w_down, v_ln2_g, v_ln2_b):
    w = dict(zip(WEIGHTS, (w_in, b_gates, w_sc_conv, mh_gain, w_out, ln1_g, ln1_b, w_up, w_ffn_conv, b_ffn_conv,
                           w_down, ln2_g, ln2_b)))
    m = dict(zip(WEIGHTS, (m_w_in, m_b_gates, m_w_sc_conv, m_mh_gain, m_w_out, m_ln1_g, m_ln1_b, m_w_up,
                           m_w_ffn_conv, m_b_ffn_conv, m_w_down, m_ln2_g, m_ln2_b)))
    v = dict(zip(WEIGHTS, (v_w_in, v_b_gates, v_w_sc_conv, v_mh_gain, v_w_out, v_ln1_g, v_ln1_b, v_w_up,
                           v_w_ffn_conv, v_b_ffn_conv, v_w_down, v_ln2_g, v_ln2_b)))
    me = _index(_place())
    d = x.shape[2]
    ws_in = w_in.shape[2]
    assert ws_in == IN_SLAB + 1 and N_DEV <= LANE, w_in.shape
    ninp = (N_DEV + 1) * IN_SLAB
    ws_sc, ws_fc = w_sc_conv.shape[2], w_ffn_conv.shape[2]
    w_in_t, m_in_t, v_in_t = (jnp.transpose(a[0]) for a in (w_in, m_w_in, v_w_in))

    w_in_shift = _shift_w_in(w_in_t, IN_SLAB + LANE)
    taps8 = lambda a: jnp.pad(a[0], ((0, 5), (0, 0)))
    wx = _Gathering(dict(w_in=w_in_shift, w_sc=taps8(w_sc_conv), w_fc=taps8(w_ffn_conv)),
                    {n: w[n][0] for n in ("w_out", "w_up", "w_down")}, me)
    token = wx.begin(None)
    for n in ("w_in", "w_sc", "w_fc"):
        token = wx.forward(n, token)
    g_in, g_sc, g_fc = (wx.get(n, token) for n in ("w_in", "w_sc", "w_fc"))
    w_in_full = _assemble_w_in(g_in, ninp)
    w_sc_full = g_sc[:, :3].transpose(1, 0, 2).reshape(3, N_DEV * ws_sc)
    w_fc_full = g_fc[:, :3].transpose(1, 0, 2).reshape(3, N_DEV * ws_fc)

    xi, yi, ci = _place()
    names = ("loss",) + SMALL
    pieces = {}

    def gather_small(small, loss_t, after):
        pieces.update(small, loss=loss_t[0, :1])
        sizes = [_lanes(pieces[n].size) for n in names]
        (g_small,) = _all_gather([_behind(_pack([pieces[n] for n in names], sizes), after)], "gather_small")
        return _sum_slots(g_small, "sum_small", tr=g_small.shape[1])

    gx = _Reducing(jnp.reshape(ci, (1,)).astype(jnp.int32), 2 * xi + yi, gather_small)
    loss_t, grad_x, small, _ = _local_step(
        x[0], loss_target[0], w_in_full, b_gates, w_sc_full, mh_gain, None, ln1_g, ln1_b, None,
        w_fc_full, b_ffn_conv, None, ln2_g, ln2_b, gx=gx, wx=wx)

    grads, deltas, new_m, new_v = {}, {}, {}, {}
    for name in ("w_down", "w_up", "w_out"):
        grads[name], deltas[name], new_m[name], new_v[name] = _sum_adamw(
            gx.finish(name, gx.token), w[name][0], m[name][0], v[name][0], "adamw_" + name)

    summed = _unpack(gx.small_sum, [pieces[n].shape for n in names], [_lanes(pieces[n].size) for n in names])
    full = dict(zip(names, summed))
    full["w_sc_conv"] = lax.dynamic_slice(full["w_sc_conv"], (0, me * ws_sc), (3, ws_sc))
    full["w_ffn_conv"] = lax.dynamic_slice(full["w_ffn_conv"], (0, me * ws_fc), (3, ws_fc))
    for n in SMALL:
        grads[n] = full[n].reshape(w[n].shape)
    sizes = [_lanes(w[n].size) for n in SMALL]
    shapes = [w[n].shape for n in SMALL]
    packed = [_pack([t[n] for n in SMALL], sizes) for t in (w, grads, m, v)]
    small_out = _adamw(*packed, "adamw_small")
    for res, t in zip(small_out, (deltas, new_m, new_v)):
        t.update(zip(SMALL, _unpack(res, shapes, sizes)))

    done = sum(t[0:1, 0:1] for t in (grad_x, deltas["w_down"], deltas["w_up"], deltas["w_out"], small_out[0]))
    grads["w_in"], deltas["w_in"], new_m["w_in"], new_v["w_in"] = (
        jnp.transpose(a)[None] for a in _sum_adamw_shifted(gx.finish("w_in", done), w_in_t, m_in_t, v_in_t, "adamw_w_in"))

    big = lambda t: {n: (t[n].reshape(w[n].shape) if n in BIG else t[n]) for n in WEIGHTS}
    grads, deltas, new_m, new_v = big(grads), big(deltas), big(new_m), big(new_v)
    return (full["loss"].reshape(()), grad_x[None], *[grads[n] for n in WEIGHTS], *[deltas[n] for n in WEIGHTS],
            *[new_m[n] for n in WEIGHTS], *[new_v[n] for n in WEIGHTS])
```

```python
import functools

import jax
import jax.numpy as jnp
from jax import lax
from jax.experimental import pallas as pl
from jax.experimental.pallas import tpu as pltpu

F32 = jnp.float32
BF16 = jnp.bfloat16
MESH = pl.DeviceIdType.MESH

N_DEV = 8
NH = 4
CHUNK = 64
LN_EPS = 1e-5
HN_EPS = 1e-6
ALPHA = 2.0 ** 0.25
LANE = 128
IN_SLAB = 7 * LANE
VMEM_LIMIT = 56 * 1024 * 1024
ADAM_LR, ADAM_B1, ADAM_B2, ADAM_EPS, ADAM_WD, ADAM_STEP = 0.001, 0.9, 0.999, 1e-08, 0.01, 10

_NN = (((1,), (0,)), ((), ()))
_NT = (((1,), (1,)), ((), ()))
_TN = (((0,), (0,)), ((), ()))


def _dot(a, b, dn=_NN):
    return lax.dot_general(a, b, dn, preferred_element_type=F32)


def _params(*sem):
    return pltpu.CompilerParams(dimension_semantics=sem if sem else None, vmem_limit_bytes=VMEM_LIMIT)


def _iota(shape, axis):
    return lax.broadcasted_iota(jnp.int32, shape, axis)


def _fit(n, want):
    if n <= want:
        return n
    t = want - want % LANE
    while n % t:
        t -= LANE
    return t


def _matmul(a, b, mode, out_dtype, name, tm=1024, tn=512, tk=1024, add=None, add_scale=1.0,
            a_blocked=False, b_blocked=False, o_width=None, after=None):
    a_parts = a if isinstance(a, tuple) else None
    b_parts = b if isinstance(b, tuple) else None
    if a_parts:
        a_blocked, (a_rows, wa), na = True, a[0].shape, len(a)
        kd, m = (a_rows, na * wa) if mode == "tn" else (na * wa, a_rows)
    elif a_blocked:
        na, a_rows, wa = a.shape
        kd, m = (a_rows, na * wa) if mode == "tn" else (na * wa, a_rows)
    elif mode == "tn":
        kd, m = a.shape
    else:
        m, kd = a.shape
    if b_parts:
        b_blocked, (rows, w), nb = True, b[0].shape, len(b)
    elif b_blocked:
        nb, rows, w = b.shape
    if b_blocked:
        n = rows if mode == "nt" else nb * w
        assert (nb * w if mode == "nt" else rows) == kd, (name, kd)
    else:
        n = b.shape[0] if mode == "nt" else b.shape[1]
    tm, tn, tk = _fit(m, tm), _fit(n, tn), _fit(kd, tk)
    if a_blocked and mode == "tn":
        tm = _fit(wa, tm)
    if a_blocked and mode != "tn":
        tk = _fit(wa, tk)
    if b_blocked and mode != "nt":
        tn = _fit(w, tn)
    if b_blocked and mode == "nt":
        tk = _fit(w, tk)
    if o_width is not None:
        tn = _fit(o_width, tn)
    assert m % tm == 0 and n % tn == 0 and kd % tk == 0, (name, m, n, kd, tm, tn, tk)
    assert not (a_blocked and mode != "tn" and wa % tk) and not (b_blocked and mode == "nt" and w % tk), (name, tk)
    nk = kd // tk
    dn = {"nn": _NN, "nt": _NT, "tn": _TN}[mode]
    if a_blocked and mode == "tn":
        a_per = wa // tm
        a_spec = pl.BlockSpec((None, tk, tm), lambda i, j, k: (i // a_per, k, i % a_per))
    elif a_blocked:
        a_per = wa // tk
        a_spec = pl.BlockSpec((None, tm, tk), lambda i, j, k: (k // a_per, i, k % a_per))
    elif mode == "tn":
        a_spec = pl.BlockSpec((tk, tm), lambda i, j, k: (k, i))
    else:
        a_spec = pl.BlockSpec((tm, tk), lambda i, j, k: (i, k))
    if b_blocked and mode != "nt":
        per = w // tn
        b_spec = pl.BlockSpec((None, tk, tn), lambda i, j, k: (j // per, k, j % per))
    elif b_blocked:
        per = w // tk
        b_spec = pl.BlockSpec((None, tn, tk), lambda i, j, k: (k // per, j, k % per))
    elif mode == "nt":
        b_spec = pl.BlockSpec((tn, tk), lambda i, j, k: (j, k))
    else:
        b_spec = pl.BlockSpec((tk, tn), lambda i, j, k: (k, j))
    if o_width is None:
        o_spec = pl.BlockSpec((tm, tn), lambda i, j, k: (i, j))
        o_shape = (m, n)
    else:
        oper = o_width // tn
        o_spec = pl.BlockSpec((None, tm, tn), lambda i, j, k: (j // oper, i, j % oper))
        o_shape = (n // o_width, m, o_width)
    a_list, a_specs = [a], [a_spec]
    if a_parts:
        hold = lambda x, s: jnp.clip(x - s * a_per, 0, a_per - 1)
        a_list = list(a_parts)
        a_specs = [(pl.BlockSpec((tk, tm), lambda i, j, k, s=s: (k, hold(i, s))) if mode == "tn"
                    else pl.BlockSpec((tm, tk), lambda i, j, k, s=s: (i, hold(k, s)))) for s in range(na)]
    b_list, b_specs = [b], [b_spec]
    if b_parts:
        hold_b = lambda x, s: jnp.clip(x - s * per, 0, per - 1)
        b_list = list(b_parts)
        b_specs = [(pl.BlockSpec((tn, tk), lambda i, j, k, s=s: (j, hold_b(k, s))) if mode == "nt"
                    else pl.BlockSpec((tk, tn), lambda i, j, k, s=s: (k, hold_b(j, s)))) for s in range(nb)]
    n_a, n_b = len(a_list), len(b_list)
    has_add = add is not None
    n_in = n_a + n_b + has_add + (after is not None)
    in_place = nk > 1 and out_dtype == F32

    def body(*refs):
        add_ref = refs[n_a + n_b] if has_add else None
        o_ref = refs[n_in]
        i, j, k = pl.program_id(0), pl.program_id(1), pl.program_id(2)

        def finish(r):
            if has_add:
                r = r + add_scale * add_ref[...]
            o_ref[...] = r.astype(out_dtype)

        def step(a_ref, b_ref):
            if nk == 1:
                finish(_dot(a_ref[...], b_ref[...], dn))
                return
            acc = o_ref if in_place else refs[-1]

            @pl.when(k == 0)
            def _():
                acc[...] = _dot(a_ref[...], b_ref[...], dn)

            @pl.when(k > 0)
            def _():
                acc[...] += _dot(a_ref[...], b_ref[...], dn)

        if n_a == 1 and n_b == 1:
            step(refs[0], refs[1])
        else:
            slab_a = ((i if mode == "tn" else k) // a_per) if n_a > 1 else 0
            slab_b = ((k if mode == "nt" else j) // per) if n_b > 1 else 0
            for sa in range(n_a):
                for sb in range(n_b):
                    pl.when((slab_a == sa) & (slab_b == sb))(functools.partial(step, refs[sa], refs[n_a + sb]))
        if nk > 1 and not (in_place and not has_add):
            @pl.when(k == nk - 1)
            def _():
                finish((o_ref if in_place else refs[-1])[...])

    in_specs = a_specs + b_specs + ([pl.BlockSpec((tm, tn), lambda i, j, k: (i, j))] if has_add else [])
    args = (*a_list, *b_list) + ((add,) if has_add else ())
    if after is not None:
        in_specs.append(pl.BlockSpec(memory_space=pl.ANY))
        args += (after,)
    return pl.pallas_call(
        body, name=name, grid=(m // tm, n // tn, nk),
        in_specs=in_specs, out_specs=o_spec,
        out_shape=jax.ShapeDtypeStruct(o_shape, out_dtype),
        scratch_shapes=[pltpu.VMEM((tm, tn), F32)] if nk > 1 and not in_place else [],
        compiler_params=_params("parallel", "parallel", "arbitrary"),
    )(*args)


def _shift_down(u, s):
    return jnp.where(_iota(u.shape, 0) >= s, pltpu.roll(u, s, 0), 0.0)


def _shift_up(u, s):
    t = u.shape[0]
    return jnp.where(_iota(u.shape, 0) < t - s, pltpu.roll(u, t - s, 0), 0.0)


SLAB = 8


def _rolled(u):
    return pltpu.roll(u, 2, 0), pltpu.roll(u, 1, 0)


def _conv(u, w, rolled=None):
    u2, u1 = _rolled(u) if rolled is None else rolled
    raw = w[0:1] * u2 + w[1:2] * u1 + w[2:3] * u
    head = u[0:SLAB]
    mended = w[0:1] * _shift_down(head, 2) + w[1:2] * _shift_down(head, 1) + w[2:3] * head
    return jnp.concatenate([mended, raw[SLAB:]], axis=0)


def _conv_t(dy, w):
    t = dy.shape[0]
    raw = w[2:3] * dy + w[1:2] * pltpu.roll(dy, t - 1, 0) + w[0:1] * pltpu.roll(dy, t - 2, 0)
    tail = dy[t - SLAB:]
    mended = w[2:3] * tail + w[1:2] * _shift_up(tail, 1) + w[0:1] * _shift_up(tail, 2)
    return jnp.concatenate([raw[:t - SLAB], mended], axis=0)


def _conv_dw(dy, u, rolled=None):
    t = dy.shape[0]
    u2, u1 = _rolled(u) if rolled is None else rolled
    head, tail = dy[0:SLAB], u[t - SLAB:]
    r = _iota(head.shape, 0)
    wrap2 = jnp.sum(jnp.where(r < 2, head * pltpu.roll(tail, 2, 0), 0.0), axis=0, keepdims=True)
    wrap1 = jnp.sum(jnp.where(r < 1, head * pltpu.roll(tail, 1, 0), 0.0), axis=0, keepdims=True)
    d0 = jnp.sum(dy * u2, axis=0, keepdims=True) - wrap2
    d1 = jnp.sum(dy * u1, axis=0, keepdims=True) - wrap1
    d2 = jnp.sum(dy * u, axis=0, keepdims=True)
    r3 = _iota((3, dy.shape[1]), 0)
    return jnp.where(r3 == 0, d0, jnp.where(r3 == 1, d1, d2))


def _sigmoid(x):
    return 0.5 * jnp.tanh(0.5 * x) + 0.5


def _sconv_fwd(proj, w_sc, t, wc):
    nb = wc // LANE

    def body(cb_ref, cc_ref, ch_ref, w_ref, y_ref):
        u = cc_ref[...] * ch_ref[...]
        y_ref[...] = (cb_ref[...] * _conv(u, w_ref[...])).astype(BF16)

    col = lambda off: pl.BlockSpec((t, LANE), lambda j: (0, j + off))
    return pl.pallas_call(
        body, name="sconv_fwd", grid=(nb,),
        in_specs=[col(0), col(nb), col(2 * nb), pl.BlockSpec((3, LANE), lambda j: (0, j))],
        out_specs=pl.BlockSpec((None, t, LANE), lambda j: (0, 0, j)),
        out_shape=jax.ShapeDtypeStruct((2, t, wc), BF16),
        compiler_params=_params("parallel"),
    )(proj, proj, proj, w_sc)


def _sconv_bwd(dy, proj, w_sc, t, wc):
    nb = wc // LANE

    def body(dy_ref, cb_ref, cc_ref, ch_ref, w_ref, dcb_ref, dcc_ref, dch_ref, dw_ref):
        cc, ch, w, d = cc_ref[...], ch_ref[...], w_ref[...], dy_ref[...]
        u = cc * ch
        ru = _rolled(u)
        dcb_ref[...] = (d * _conv(u, w, ru)).astype(BF16)
        dcu = d * cb_ref[...]
        dw_ref[...] = _conv_dw(dcu, u, ru)
        du = _conv_t(dcu, w)
        dcc_ref[...] = (du * ch).astype(BF16)
        dch_ref[...] = (du * cc).astype(BF16)

    col = lambda off: pl.BlockSpec((t, LANE), lambda j: (0, j + off))
    act = jax.ShapeDtypeStruct((t, wc), BF16)
    return pl.pallas_call(
        body, name="sconv_bwd", grid=(nb,),
        in_specs=[col(0), col(0), col(nb), col(2 * nb), pl.BlockSpec((3, LANE), lambda j: (0, j))],
        out_specs=[col(0), col(0), col(0), pl.BlockSpec((3, LANE), lambda j: (0, j))],
        out_shape=[act, act, act, jax.ShapeDtypeStruct((3, wc), F32)],
        compiler_params=_params("parallel"),
    )(dy, proj, proj, proj, w_sc)


def _gates_prep(proj, bias_tile, t, gate_tile):
    def body(g_ref, b_ref, o_ref):
        g = g_ref[...] + b_ref[...]
        lane = _iota(g.shape, 1)
        is_f = (lane >= NH) & (lane < 2 * NH)
        lf = jnp.minimum(g, 0.0) - jnp.log(1.0 + jnp.exp(-jnp.abs(g)))
        c = jnp.where(is_f, lf, 0.0)
        r = _iota(g.shape, 0) % CHUNK
        s = 1
        while s < CHUNK:
            c = c + jnp.where(r >= s, pltpu.roll(c, s, 0), 0.0)
            s *= 2
        o_ref[...] = jnp.where(is_f, c, jnp.where(lane < NH, g, 0.0))

    return pl.pallas_call(
        body, name="gates_prep", grid=(1,),
        in_specs=[pl.BlockSpec((t, LANE), lambda i: (0, gate_tile)), pl.BlockSpec((1, LANE), lambda i: (0, 0))],
        out_specs=pl.BlockSpec((t, LANE), lambda i: (0, 0)),
        out_shape=jax.ShapeDtypeStruct((t, LANE), F32),
        compiler_params=_params("arbitrary"),
    )(proj, bias_tile)


def _gates_bwd(dgate, proj, bias_tile, t, gate_tile):
    def body(dg_ref, g_ref, b_ref, o_ref, s_ref):
        g = g_ref[...] + b_ref[...]
        lane = _iota(g.shape, 1)
        r = _iota(g.shape, 0) % CHUNK
        dsig = 1.0 - _sigmoid(g)
        out = jnp.zeros(g.shape, F32)
        for h in range(NH):
            d = dg_ref[h]
            c = d
            s = 1
            while s < CHUNK:
                c = c + jnp.where(r + s < CHUNK, pltpu.roll(c, t - s, 0), 0.0)
                s *= 2
            di = jnp.broadcast_to(d[:, 0:1], g.shape)
            db = jnp.broadcast_to(c[:, 1:2], g.shape)
            out = out + jnp.where(lane == h, di, 0.0) + jnp.where(lane == NH + h, db * dsig, 0.0)
        o_ref[...] = out.astype(BF16)
        s_ref[...] = jnp.sum(out, axis=0, keepdims=True)

    return pl.pallas_call(
        body, name="gates_bwd", grid=(1,),
        in_specs=[pl.BlockSpec((NH, t, LANE), lambda i: (0, 0, 0)),
                  pl.BlockSpec((t, LANE), lambda i: (0, gate_tile)), pl.BlockSpec((1, LANE), lambda i: (0, 0))],
        out_specs=[pl.BlockSpec((t, LANE), lambda i: (0, 0)), pl.BlockSpec((1, LANE), lambda i: (0, 0))],
        out_shape=[jax.ShapeDtypeStruct((t, LANE), BF16), jax.ShapeDtypeStruct((1, LANE), F32)],
        compiler_params=_params("arbitrary"),
    )(dgate, proj, bias_tile)


def _chunk_gates(gc, gr, h, mprev):
    L = CHUNK
    icol, bcol = gc[:, h:h + 1], gc[:, h + NH:h + NH + 1]
    irow, brow = gr[h:h + 1, :], gr[h + NH:h + NH + 1, :]
    tri = _iota((L, L), 0) >= _iota((L, L), 1)
    log_d = jnp.where(tri, bcol - brow + irow, -jnp.inf)
    inter = bcol + mprev
    mt = jnp.maximum(inter, jnp.max(log_d, axis=1, keepdims=True))
    dw = jnp.exp(log_d - mt)
    iw = jnp.exp(inter - mt)
    g = brow[:, L - 1:L]
    wlog_col = g - bcol + icol
    wlog_row = g - brow + irow
    mnew = jnp.maximum(g + mprev, jnp.max(wlog_row, axis=1, keepdims=True))
    wcol = jnp.exp(wlog_col - mnew)
    decay = jnp.exp(g + mprev - mnew)
    return dw, iw, mt, wcol, decay, mnew


def _mlstm_fwd(proj, gcol, grow, t, wc, dh):
    nc = t // CHUNK
    wm = NH * dh
    assert wc == wm, (wc, wm)
    qoff = 3 * wc // wm
    scale = dh ** -0.5

    def body(q_ref, k_ref, v_ref, gc_ref, gr_ref, h_ref, cs_ref, ns_ref, c_s, n_s, m_s):
        @pl.when(pl.program_id(0) == 0)
        def _():
            c_s[...] = jnp.zeros_like(c_s)
            n_s[...] = jnp.zeros_like(n_s)
            m_s[...] = jnp.zeros_like(m_s)

        gc, gr = gc_ref[...], gr_ref[0]
        for h in range(NH):
            cols = slice(h * dh, (h + 1) * dh)
            mprev = m_s[h, 0:1, 0:1]
            cprev = c_s[h]
            n8 = n_s[h]
            nprev = n8[0:1]
            cs_ref[h] = cprev
            ns_ref[h] = jnp.where(_iota(n8.shape, 0) == 1, mprev, n8)

            dw, iw, mt, wcol, decay, mnew = _chunk_gates(gc, gr, h, mprev)
            qs = q_ref[:, cols] * scale
            k = k_ref[:, cols]
            qs_b, k_b, v_b = qs.astype(BF16), k.astype(BF16), v_ref[:, cols].astype(BF16)
            s = _dot(qs_b, k_b, _NT) * dw
            num = _dot(s.astype(BF16), v_b) + iw * _dot(qs_b, cprev.astype(BF16))
            den = jnp.sum(s, axis=1, keepdims=True) + iw * jnp.sum(qs * nprev, axis=1, keepdims=True)
            h_ref[:, cols] = num / jnp.maximum(jnp.abs(den), jnp.exp(-mt))

            wk = wcol * k
            c_s[h] = decay * cprev + _dot(wk.astype(BF16), v_b, _TN)
            n_s[h] = decay * n8 + jnp.sum(wk, axis=0, keepdims=True)
            m_s[h] = jnp.broadcast_to(mnew, m_s.shape[1:])

    grp = lambda off: pl.BlockSpec((CHUNK, wm), lambda c: (c, qoff + off))
    return pl.pallas_call(
        body, name="mlstm_fwd", grid=(nc,),
        in_specs=[grp(0), grp(1), grp(2),
                  pl.BlockSpec((CHUNK, LANE), lambda c: (c, 0)),
                  pl.BlockSpec((1, 8, CHUNK), lambda c: (c, 0, 0))],
        out_specs=[pl.BlockSpec((CHUNK, wm), lambda c: (c, 0)),
                   pl.BlockSpec((NH, None, dh, dh), lambda c: (0, c, 0, 0)),
                   pl.BlockSpec((NH, None, 8, dh), lambda c: (0, c, 0, 0))],
        out_shape=[jax.ShapeDtypeStruct((t, wm), F32),
                   jax.ShapeDtypeStruct((NH, nc, dh, dh), F32),
                   jax.ShapeDtypeStruct((NH, nc, 8, dh), F32)],
        scratch_shapes=[pltpu.VMEM((NH, dh, dh), F32), pltpu.VMEM((NH, 8, dh), F32), pltpu.VMEM((NH, 8, LANE), F32)],
        compiler_params=_params("arbitrary"),
    )(proj, proj, proj, gcol, grow)


def _mlstm_bwd(proj, gcol, grow, hval, dh_in, cs, ns, t, wc, dh):
    nc = t // CHUNK
    wm = NH * dh
    assert wc == wm, (wc, wm)
    qoff = 3 * wc // wm
    scale = dh ** -0.5
    L = CHUNK

    def body(q_ref, k_ref, v_ref, gc_ref, gr_ref, h_ref, dh_ref, cs_ref, ns_ref,
             dq_ref, dk_ref, dv_ref, dg_ref, dc_s, dn_s):
        @pl.when(pl.program_id(0) == 0)
        def _():
            dc_s[...] = jnp.zeros_like(dc_s)
            dn_s[...] = jnp.zeros_like(dn_s)

        gc, gr = gc_ref[...], gr_ref[0]
        eye = _iota((L, L), 0) == _iota((L, L), 1)
        lane = _iota((L, LANE), 1)
        last = _iota((L, 1), 0) == L - 1
        for h in range(NH):
            cols = slice(h * dh, (h + 1) * dh)
            ns8 = ns_ref[h]
            nprev = ns8[0:1]
            mprev = ns8[1:2, 0:1]
            cprev = cs_ref[h]
            dcn = dc_s[h]
            dn8 = dn_s[h]
            dnn = dn8[0:1]

            dw, iw, mt, wcol, decay, _ = _chunk_gates(gc, gr, h, mprev)
            qs = q_ref[:, cols] * scale
            k = k_ref[:, cols]
            qs_b, k_b, v_b = qs.astype(BF16), k.astype(BF16), v_ref[:, cols].astype(BF16)
            qk = _dot(qs_b, k_b, _NT)
            s = qk * dw
            den = jnp.sum(s, axis=1, keepdims=True) + iw * jnp.sum(qs * nprev, axis=1, keepdims=True)
            emt = jnp.exp(-mt)
            r = 1.0 / jnp.maximum(jnp.abs(den), emt)
            dout = dh_ref[:, cols]
            dnum = dout * r
            dden = (-jnp.sum(dout * h_ref[:, cols], axis=1, keepdims=True) * r
                    * jnp.where(jnp.abs(den) > emt, jnp.sign(den), 0.0))
            dnum_b = dnum.astype(BF16)
            cprev_b = cprev.astype(BF16)
            dcn_b = dcn.astype(BF16)

            gd = (_dot(dnum_b, v_b, _NT) + dden) * dw
            gd_b = gd.astype(BF16)
            dqs_inter = iw * (_dot(dnum_b, cprev_b, _NT) + dden * nprev)
            dqs = _dot(gd_b, k_b) + dqs_inter
            dk_inter = wcol * (_dot(v_b, dcn_b, _NT) + dnn)
            dk = _dot(gd_b, qs_b, _TN) + dk_inter
            wk = wcol * k
            dv = _dot(s.astype(BF16), dnum_b, _TN) + _dot(wk.astype(BF16), dcn_b)

            e = gd * qk
            e_cols = jnp.sum(jnp.where(eye, jnp.sum(e, axis=0, keepdims=True), 0.0), axis=1, keepdims=True)
            k_inter = jnp.sum(k * dk_inter, axis=1, keepdims=True)
            rq = jnp.sum(e, axis=1, keepdims=True) + jnp.sum(qs * dqs_inter, axis=1, keepdims=True)
            rk = e_cols + k_inter
            hsum = jnp.sum(k_inter, axis=0, keepdims=True)
            jdec = decay * (jnp.sum(jnp.sum(dcn * cprev, axis=1, keepdims=True), axis=0, keepdims=True)
                            + jnp.sum(dnn * nprev, axis=1, keepdims=True))
            db = rq - rk + jnp.where(last, hsum + jdec, 0.0)
            dg_ref[h] = jnp.where(lane == 0, rk, jnp.where(lane == 1, db, 0.0))

            dq_ref[:, cols] = (dqs * scale).astype(BF16)
            dk_ref[:, cols] = dk.astype(BF16)
            dv_ref[:, cols] = dv.astype(BF16)

            iq = iw * qs
            dc_s[h] = decay * dcn + _dot(iq.astype(BF16), dnum_b, _TN)
            dn_s[h] = decay * dn8 + jnp.sum(iq * dden, axis=0, keepdims=True)

    rc = lambda c: nc - 1 - c
    grp = lambda off: pl.BlockSpec((L, wm), lambda c: (rc(c), qoff + off))
    hm = pl.BlockSpec((L, wm), lambda c: (rc(c), 0))
    act = jax.ShapeDtypeStruct((t, wm), BF16)
    return pl.pallas_call(
        body, name="mlstm_bwd", grid=(nc,),
        in_specs=[grp(0), grp(1), grp(2),
                  pl.BlockSpec((L, LANE), lambda c: (rc(c), 0)),
                  pl.BlockSpec((1, 8, L), lambda c: (rc(c), 0, 0)),
                  hm, hm,
                  pl.BlockSpec((NH, None, dh, dh), lambda c: (0, rc(c), 0, 0)),
                  pl.BlockSpec((NH, None, 8, dh), lambda c: (0, rc(c), 0, 0))],
        out_specs=[hm, hm, hm, pl.BlockSpec((NH, L, LANE), lambda c: (0, rc(c), 0))],
        out_shape=[act, act, act, jax.ShapeDtypeStruct((NH, t, LANE), F32)],
        scratch_shapes=[pltpu.VMEM((NH, dh, dh), F32), pltpu.VMEM((NH, 8, dh), F32)],
        compiler_params=_params("arbitrary"),
    )(proj, proj, proj, gcol, grow, hval, dh_in, cs, ns)


def _head_norm(hv):
    mu = jnp.mean(hv, axis=1, keepdims=True)
    hc = hv - mu
    rstd = lax.rsqrt(jnp.mean(hc * hc, axis=1, keepdims=True) + HN_EPS)
    return hc * rstd, rstd


def _hnorm_fwd(hval, proj, gain, y, t, wc, dh, tr=256):
    ooff = 3 * wc // dh + 3 * NH

    def body(h_ref, o_ref, g_ref, y_in, y_ref):
        hhat, _ = _head_norm(h_ref[...])
        y_ref[...] = (_sigmoid(o_ref[...]) * hhat * g_ref[...]).astype(BF16)

    return pl.pallas_call(
        body, name="hnorm_fwd", grid=(t // tr, NH),
        in_specs=[pl.BlockSpec((tr, dh), lambda i, h: (i, h)),
                  pl.BlockSpec((tr, dh), lambda i, h: (i, ooff + h)),
                  pl.BlockSpec((1, dh), lambda i, h: (0, h)),
                  pl.BlockSpec(memory_space=pl.ANY)],
        out_specs=pl.BlockSpec((None, tr, dh), lambda i, h: (1, i, h)),
        out_shape=jax.ShapeDtypeStruct(y.shape, BF16),
        input_output_aliases={3: 0},
        compiler_params=_params("parallel", "parallel"),
    )(hval, proj, gain, y)


def _hnorm_bwd(dy, hval, proj, gain, t, wc, dh, tr=256):
    ooff = 3 * wc // dh + 3 * NH
    yoff = wc // dh

    def body(dy_ref, h_ref, o_ref, g_ref, do_ref, dh_ref, dg_ref):
        i = pl.program_id(1)
        hhat, rstd = _head_norm(h_ref[...])
        gain_v = g_ref[...]
        sig = _sigmoid(o_ref[...])
        d = dy_ref[...]
        do_ref[...] = (d * hhat * gain_v * sig * (1.0 - sig)).astype(BF16)
        dhn = d * sig
        part = jnp.sum(dhn * hhat, axis=0, keepdims=True)

        @pl.when(i == 0)
        def _():
            dg_ref[...] = part

        @pl.when(i > 0)
        def _():
            dg_ref[...] += part

        dhat = dhn * gain_v
        dh_ref[...] = rstd * (dhat - jnp.mean(dhat, axis=1, keepdims=True)
                              - hhat * jnp.mean(dhat * hhat, axis=1, keepdims=True))

    blk = lambda off: pl.BlockSpec((tr, dh), lambda h, i: (i, off + h))
    return pl.pallas_call(
        body, name="hnorm_bwd", grid=(NH, t // tr),
        in_specs=[blk(yoff), blk(0), blk(ooff), pl.BlockSpec((1, dh), lambda h, i: (0, h))],
        out_specs=[blk(0), blk(0), pl.BlockSpec((1, dh), lambda h, i: (0, h))],
        out_shape=[jax.ShapeDtypeStruct((t, NH * dh), BF16), jax.ShapeDtypeStruct((t, NH * dh), F32),
                   jax.ShapeDtypeStruct((1, NH * dh), F32)],
        compiler_params=_params("parallel", "arbitrary"),
    )(dy, hval, proj, gain)


def _ln_stats(z):
    mu = jnp.mean(z, axis=1, keepdims=True)
    zc = z - mu
    rstd = lax.rsqrt(jnp.mean(zc * zc, axis=1, keepdims=True) + LN_EPS)
    return zc * rstd, rstd


def _ln_bwd(dy, xhat, rstd, g):
    dxh = dy * g
    return rstd * (dxh - jnp.mean(dxh, axis=1, keepdims=True) - xhat * jnp.mean(dxh * xhat, axis=1, keepdims=True))


def _accum(ref, i, part):
    @pl.when(i == 0)
    def _():
        ref[...] = part

    @pl.when(i > 0)
    def _():
        ref[...] += part


def _ln1_fwd(x, mix, g, b, tr=256):
    t, d = x.shape

    def body(x_ref, m_ref, g_ref, b_ref, xh_ref, rs_ref, xb_ref):
        xhat, rstd = _ln_stats(ALPHA * x_ref[...] + m_ref[...])
        xh_ref[...] = xhat
        rs_ref[...] = rstd
        xb_ref[...] = (xhat * g_ref[...] + b_ref[...]).astype(BF16)

    row = pl.BlockSpec((tr, d), lambda i: (i, 0))
    vec = pl.BlockSpec((1, d), lambda i: (0, 0))
    return pl.pallas_call(
        body, name="ln1_fwd", grid=(t // tr,),
        in_specs=[row, row, vec, vec],
        out_specs=[row, pl.BlockSpec((tr, 1), lambda i: (i, 0)), row],
        out_shape=[jax.ShapeDtypeStruct((t, d), F32), jax.ShapeDtypeStruct((t, 1), F32),
                   jax.ShapeDtypeStruct((t, d), BF16)],
        compiler_params=_params("parallel"),
    )(x, mix, g, b)


def _ln2_loss(xhat1, g1, b1, ff, target, g2, b2, tr=256):
    t, d = ff.shape

    def body(xh_ref, g1_ref, b1_ref, f_ref, t_ref, g_ref, b_ref, dz_ref, dzb_ref, dg_ref, db_ref, l_ref):
        i = pl.program_id(0)
        x1 = xh_ref[...] * g1_ref[...] + b1_ref[...]
        xhat, rstd = _ln_stats(ALPHA * x1 + f_ref[...])
        gv = g_ref[...]
        e = xhat * gv + b_ref[...] - t_ref[...]
        lsum = jnp.sum(jnp.sum(e * e, axis=1, keepdims=True), axis=0, keepdims=True) * (0.5 / d)
        dy = e * (1.0 / d)
        _accum(dg_ref, i, jnp.sum(dy * xhat, axis=0, keepdims=True))
        _accum(db_ref, i, jnp.sum(dy, axis=0, keepdims=True))
        _accum(l_ref, i, jnp.broadcast_to(lsum, l_ref.shape))
        dz = _ln_bwd(dy, xhat, rstd, gv)
        dz_ref[...] = dz
        dzb_ref[...] = dz.astype(BF16)

    row = pl.BlockSpec((tr, d), lambda i: (i, 0))
    vec = pl.BlockSpec((1, d), lambda i: (0, 0))
    return pl.pallas_call(
        body, name="ln2_loss", grid=(t // tr,),
        in_specs=[row, vec, vec, row, row, vec, vec],
        out_specs=[row, row, vec, vec, pl.BlockSpec((8, LANE), lambda i: (0, 0))],
        out_shape=[jax.ShapeDtypeStruct((t, d), F32), jax.ShapeDtypeStruct((t, d), BF16),
                   jax.ShapeDtypeStruct((1, d), F32), jax.ShapeDtypeStruct((1, d), F32),
                   jax.ShapeDtypeStruct((8, LANE), F32)],
        compiler_params=_params("arbitrary"),
    )(xhat1, g1, b1, ff, target, g2, b2)


def _ln1_bwd(dz2, dffn, xhat1, rstd1, g1, tr=256):
    t, d = dz2.shape

    def body(a_ref, f_ref, xh_ref, rs_ref, g_ref, dz_ref, dzb_ref, dg_ref, db_ref):
        i = pl.program_id(0)
        dy = ALPHA * a_ref[...] + f_ref[...]
        xhat = xh_ref[...]
        _accum(dg_ref, i, jnp.sum(dy * xhat, axis=0, keepdims=True))
        _accum(db_ref, i, jnp.sum(dy, axis=0, keepdims=True))
        dz = _ln_bwd(dy, xhat, rs_ref[...], g_ref[...])
        dz_ref[...] = dz
        dzb_ref[...] = dz.astype(BF16)

    row = pl.BlockSpec((tr, d), lambda i: (i, 0))
    vec = pl.BlockSpec((1, d), lambda i: (0, 0))
    return pl.pallas_call(
        body, name="ln1_bwd", grid=(t // tr,),
        in_specs=[row, row, row, pl.BlockSpec((tr, 1), lambda i: (i, 0)), vec],
        out_specs=[row, row, vec, vec],
        out_shape=[jax.ShapeDtypeStruct((t, d), F32), jax.ShapeDtypeStruct((t, d), BF16),
                   jax.ShapeDtypeStruct((1, d), F32), jax.ShapeDtypeStruct((1, d), F32)],
        compiler_params=_params("arbitrary"),
    )(dz2, dffn, xhat1, rstd1, g1)


def _ffn_act_fwd(hid0, w_fc, b_fc, t, dff):
    nb = dff // LANE

    def body(hv_ref, hg_ref, wv_ref, wg_ref, bv_ref, bg_ref, a_ref):
        val = _conv(hv_ref[...], wv_ref[...]) + bv_ref[...]
        gate = _conv(hg_ref[...], wg_ref[...]) + bg_ref[...]
        a_ref[...] = (gate * _sigmoid(gate) * val).astype(BF16)

    col = lambda off: pl.BlockSpec((t, LANE), lambda j: (0, j + off))
    w3 = lambda off: pl.BlockSpec((3, LANE), lambda j: (0, j + off))
    w1 = lambda off: pl.BlockSpec((1, LANE), lambda j: (0, j + off))
    return pl.pallas_call(
        body, name="ffn_act_fwd", grid=(nb,),
        in_specs=[col(0), col(nb), w3(0), w3(nb), w1(0), w1(nb)],
        out_specs=col(0),
        out_shape=jax.ShapeDtypeStruct((t, dff), BF16),
        compiler_params=_params("parallel"),
    )(hid0, hid0, w_fc, w_fc, b_fc, b_fc)


def _ffn_act_bwd(da, hid0, w_fc, b_fc, t, dff):
    nb = dff // LANE

    def body(da_ref, hv_ref, hg_ref, wv_ref, wg_ref, bv_ref, bg_ref,
             dhv_ref, dhg_ref, dwv_ref, dwg_ref, dbv_ref, dbg_ref):
        hv, hg, wv, wg = hv_ref[...], hg_ref[...], wv_ref[...], wg_ref[...]
        rv, rg = _rolled(hv), _rolled(hg)
        val = _conv(hv, wv, rv) + bv_ref[...]
        gate = _conv(hg, wg, rg) + bg_ref[...]
        sig = _sigmoid(gate)
        d = da_ref[...]
        dsig = d * sig
        dval = dsig * gate
        dgate = dsig * val * (1.0 + gate * (1.0 - sig))
        dhv_ref[...] = _conv_t(dval, wv).astype(BF16)
        dhg_ref[...] = _conv_t(dgate, wg).astype(BF16)
        dwv_ref[...] = _conv_dw(dval, hv, rv)
        dwg_ref[...] = _conv_dw(dgate, hg, rg)
        dbv_ref[...] = jnp.sum(dval, axis=0, keepdims=True)
        dbg_ref[...] = jnp.sum(dgate, axis=0, keepdims=True)

    col = lambda off: pl.BlockSpec((t, LANE), lambda j: (0, j + off))
    w3 = lambda off: pl.BlockSpec((3, LANE), lambda j: (0, j + off))
    w1 = lambda off: pl.BlockSpec((1, LANE), lambda j: (0, j + off))
    s3 = jax.ShapeDtypeStruct((3, dff), F32)
    s1 = jax.ShapeDtypeStruct((1, dff), F32)
    return pl.pallas_call(
        body, name="ffn_act_bwd", grid=(nb,),
        in_specs=[col(0), col(0), col(nb), w3(0), w3(nb), w1(0), w1(nb)],
        out_specs=[col(0), col(0), w3(0), w3(0), w1(0), w1(0)],
        out_shape=[jax.ShapeDtypeStruct((t, dff), BF16)] * 2 + [s3, s3, s1, s1],
        compiler_params=_params("parallel"),
    )(da, hid0, hid0, w_fc, w_fc, b_fc, b_fc)


class _Ready:
    def __init__(self, **weights):
        self.weights = weights

    def begin(self, after):
        return None

    def forward(self, name, after):
        return None

    def get(self, name, after):
        return self.weights[name]


class _Kept:
    def __init__(self):
        self.grads = {}

    def start(self, name, grad):
        self.grads[name] = grad
        return None

    def relay(self, name, after):
        return None

    def meanwhile(self, small, loss, after):
        return None


def _behind(a, token):
    return a if token is None else a + token[0:1, 0:1].reshape((1,) * a.ndim)


def _local_step(x, target, w_in, b_gates, w_sc, gain, w_out, ln1_g, ln1_b, w_up, w_fc, b_fc, w_down, ln2_g, ln2_b,
                gx=None, wx=None):
    t, d = x.shape
    wc = d // 2
    dh = (d - wc) // NH
    wm = NH * dh
    dff = w_fc.shape[1] // 2
    if wx is None:
        wx = _Ready(w_out=w_out, w_up=w_up, w_down=w_down)
    ninp = w_in.shape[0]
    nin = 3 * wc + 4 * wm
    gate_tile = nin // LANE
    nc = t // CHUNK
    bias_tile = jnp.pad(b_gates, ((0, 0), (0, LANE - 2 * NH)))

    x_b = _behind(x, wx.begin(w_in)).astype(BF16)
    proj = _matmul(x_b, w_in, "nt", F32, "proj", tm=512, tn=2688, tk=d, after=wx.begin(w_in))
    y = _sconv_fwd(proj, w_sc, t, wc)
    gcol = _gates_prep(proj, bias_tile, t, gate_tile)
    grow = gcol[:, :8].T.reshape(8, nc, CHUNK).transpose(1, 0, 2)
    hval, cs, ns = _mlstm_fwd(proj, gcol, grow, t, wc, dh)
    y = _hnorm_fwd(hval, proj, gain, y, t, wc, dh)
    tok = wx.forward("w_out", y)
    w_out = wx.get("w_out", tok)
    mix = _matmul(y, w_out, "nn", F32, "out_proj", tm=512, tn=1024, tk=wc, a_blocked=True, after=tok)
    xhat1, rstd1, x1_b = _ln1_fwd(x, mix, _behind(ln1_g, wx.forward("w_up", mix)), ln1_b)
    w_up = wx.get("w_up", x1_b)
    wsl = w_up.shape[2]
    hid0 = _matmul(x1_b, w_up, "nn", F32, "ffn_up", tm=512, tn=wsl, tk=d, b_blocked=True)
    act = _ffn_act_fwd(hid0, w_fc, _behind(b_fc, wx.forward("w_down", hid0)), t, dff)
    w_down = wx.get("w_down", act)
    ff = _matmul(act, w_down, "nn", F32, "ffn_down", tm=1024, tn=512, tk=dff)
    dz2, dz2_b, d_ln2_g, d_ln2_b, loss = _ln2_loss(xhat1, ln1_g, ln1_b, ff, target, ln2_g, ln2_b)

    if gx is None:
        gx = _Kept()
    d_w_down = _matmul(act, dz2_b, "tn", BF16, "ffn_down_dw", tm=512, tn=1024, tk=t)
    d_act = _matmul(dz2_b, w_down, "nt", F32, "ffn_down_dx", tm=1024, tn=512, tk=d, after=gx.start("w_down", d_w_down))
    *d_hid0, dwv, dwg, dbv, dbg = _ffn_act_bwd(d_act, hid0, w_fc, _behind(b_fc, gx.relay("w_down", d_act)), t, dff)
    d_w_fc = jnp.concatenate([dwv, dwg], axis=1)
    d_b_fc = jnp.concatenate([dbv, dbg], axis=1)
    d_hid0 = tuple(d_hid0[:2])
    d_w_up = _matmul(x1_b, d_hid0, "tn", BF16, "ffn_up_dw", tm=512, tn=wsl, tk=t, o_width=wsl)
    d_x1_ffn = _matmul(d_hid0, w_up, "nt", F32, "ffn_up_dx", tm=1024, tn=1024, tk=wsl, b_blocked=True,
                       after=gx.start("w_up", d_w_up))
    dz1, dz1_b, d_ln1_g, d_ln1_b = _ln1_bwd(dz2, d_x1_ffn, xhat1, rstd1, _behind(ln1_g, gx.relay("w_up", d_x1_ffn)))

    d_w_out = _matmul(y, dz1_b, "tn", BF16, "out_proj_dw", tm=512, tn=1024, tk=t, a_blocked=True)
    dy = _matmul(dz1_b, w_out, "nt", F32, "out_proj_dx", tm=512, tn=1024, tk=d, after=gx.start("w_out", d_w_out))
    dcb, dcc, dch, d_w_sc = _sconv_bwd(dy, proj, _behind(w_sc, gx.relay("w_out", dy)), t, wc)
    d_o, d_hval, d_gain = _hnorm_bwd(dy, hval, proj, gain, t, wc, dh)
    dq, dk, dv, dgate = _mlstm_bwd(proj, gcol, grow, hval, d_hval, cs, ns, t, wc, dh)
    dgt, d_b_gates = _gates_bwd(dgate, proj, bias_tile, t, gate_tile)
    pad = jnp.zeros((t, ninp - nin - LANE), BF16)
    d_proj = jnp.concatenate([dcb, dcc, dch, dq, dk, dv, d_o, dgt, pad], axis=1)
    d_w_in = _matmul(d_proj, x_b, "tn", BF16, "proj_dw", tm=IN_SLAB, tn=1024, tk=t)
    small = dict(b_gates=d_b_gates[:, :2 * NH], w_sc_conv=d_w_sc, mh_gain=d_gain, ln1_g=d_ln1_g, ln1_b=d_ln1_b,
                 w_ffn_conv=d_w_fc, b_ffn_conv=d_b_fc, ln2_g=d_ln2_g, ln2_b=d_ln2_b)
    token = gx.start("w_in", d_w_in.reshape(ninp // IN_SLAB, IN_SLAB, d))
    token = gx.relay("w_in", gx.meanwhile(small, loss, token))
    grad_x = _matmul(d_proj, w_in, "nn", F32, "proj_dx", tm=512, tn=512, tk=ninp, add=dz1, add_scale=ALPHA, after=token)
    return loss, grad_x, small, gx


HBM = pl.BlockSpec(memory_space=pltpu.HBM)


def _place():
    return lax.axis_index("x"), lax.axis_index("y"), lax.axis_index("c")


def _index(p):
    return 4 * p[0] + 2 * p[1] + p[2]


def _all_gather(arrs, name):
    n = len(arrs)

    def body(*refs):
        ins, outs = refs[:n], refs[n:2 * n]
        send_sems, recv_sems, local_sems = refs[2 * n:]
        x, y, c = _place()
        me, sibling = (x, y, c), (x, y, 1 - c)
        chips = [(1 - x, y), (x, 1 - y), (1 - x, 1 - y)]

        def copy(a, k, block, to, own=False):
            dst = outs[a].at[_index(block)]
            return pltpu.make_async_remote_copy(
                src_ref=ins[a] if own else dst, dst_ref=dst,
                send_sem=send_sems.at[k * n + a], recv_sem=recv_sems.at[k * n + a],
                device_id=to, device_id_type=MESH)

        mine = [pltpu.make_async_copy(ins[a], outs[a].at[_index(me)], local_sems.at[a]) for a in range(n)]
        for cp in mine:
            cp.start()
        first = []
        for a in range(n):
            first.append(copy(a, 0, me, sibling, own=True))
            first += [copy(a, 1 + j, me, (*chip, c), own=True) for j, chip in enumerate(chips)]
        for cp in first:
            cp.start()
        passed = []
        for j, chip in enumerate(chips):
            for a in range(n):
                copy(a, 1 + j, (*chip, c), me).wait_recv()
                cp = copy(a, 4 + j, (*chip, c), sibling)
                cp.start()
                passed.append(cp)
        for a in range(n):
            copy(a, 0, sibling, me).wait_recv()
            for j, chip in enumerate(chips):
                copy(a, 4 + j, (*chip, 1 - c), me).wait_recv()
        for cp in first + passed:
            cp.wait_send()
        for cp in mine:
            cp.wait()

    return pl.pallas_call(
        body, name=name, in_specs=[HBM] * n, out_specs=[HBM] * n,
        out_shape=[jax.ShapeDtypeStruct((N_DEV,) + a.shape, a.dtype) for a in arrs],
        scratch_shapes=[pltpu.SemaphoreType.DMA((7 * n,)), pltpu.SemaphoreType.DMA((7 * n,)),
                        pltpu.SemaphoreType.DMA((n,))],
    )(*arrs)


SEM = pl.BlockSpec(memory_space=pltpu.SEMAPHORE)
EFFECT = pltpu.SideEffectType.DATAFLOW_SIDE_EFFECTING


def _chips(x, y):
    return [(1 - x, y), (x, 1 - y), (1 - x, 1 - y)]


N_CHIP = N_DEV // 2


def _pair_route(x, y, c):
    return [((x, y, 1 - c), 2 * q + (1 - c), q, q) for q in range(N_CHIP)]


def _chip_route(x, y, c):
    mine = 2 * x + y
    return [((*chip, c), 2 * chip[0] + chip[1], mine, 2 * chip[0] + chip[1]) for chip in _chips(x, y)]


def _exchange_pieces(g_ref, land_ref, width, tail):
    if not tail:
        return [(lambda i: g_ref.at[i], lambda s: land_ref.at[s])]
    return [(lambda i: g_ref.at[i], lambda s: land_ref.at[s, pl.ds(0, width), :]),
            (lambda i: g_ref.at[i + 1, pl.ds(0, LANE), :], lambda s: land_ref.at[s, pl.ds(width, LANE), :])]


def _exchange_start(grad, route, tail, name):
    width = grad.shape[1]
    n_p = 2 if tail else 1
    n_c = len(route(0, 0, 0))
    land_shape = (N_CHIP, width + (LANE if tail else 0), grad.shape[2])

    def body(g_ref, land_ref, send_sems, recv_sems, g_thru, land_thru, token):
        for j, (peer, slab, slot, _) in enumerate(route(*_place())):
            for p, (src, dst) in enumerate(_exchange_pieces(g_ref, land_ref, width, tail)):
                pltpu.make_async_remote_copy(src_ref=src(slab), dst_ref=dst(slot), send_sem=send_sems.at[j * n_p + p],
                                             recv_sem=recv_sems.at[j * n_p + p], device_id=peer,
                                             device_id_type=MESH).start()
        token[...] = jnp.zeros_like(token)

    return pl.pallas_call(
        body, name=name,
        out_shape=(pltpu.SemaphoreType.DMA((n_c * n_p,)), pltpu.SemaphoreType.DMA((n_c * n_p,)),
                   pltpu.HBM(grad.shape, grad.dtype), pltpu.HBM(land_shape, grad.dtype),
                   jax.ShapeDtypeStruct((8, LANE), F32)),
        in_specs=(HBM, HBM), out_specs=(SEM, SEM, HBM, HBM, pl.BlockSpec(memory_space=pltpu.VMEM)),
        input_output_aliases={0: 2, 1: 3},
        compiler_params=pltpu.CompilerParams(has_side_effects=EFFECT),
    )(pltpu.with_memory_space_constraint(grad, pltpu.HBM),
      pltpu.with_memory_space_constraint(lax.empty(land_shape, grad.dtype), pltpu.HBM))


def _exchange_wait(send_sems, recv_sems, g_thru, land_thru, after, route, tail, name):
    width = g_thru.shape[1]
    n_p = 2 if tail else 1

    def body(g_ref, land_ref, send_sems, recv_sems, after_ref, g_dead, got_ref):
        for j, (peer, slab, _, slot) in enumerate(route(*_place())):
            for p, (src, dst) in enumerate(_exchange_pieces(g_ref, land_ref, width, tail)):
                cp = pltpu.make_async_remote_copy(src_ref=src(slab), dst_ref=dst(slot),
                                                  send_sem=send_sems.at[j * n_p + p], recv_sem=recv_sems.at[j * n_p + p],
                                                  device_id=peer, device_id_type=MESH)
                cp.wait_send()
                cp.wait_recv()

    return pl.pallas_call(
        body, name=name,
        out_shape=(pltpu.HBM(g_thru.shape, g_thru.dtype), pltpu.HBM(land_thru.shape, land_thru.dtype)),
        in_specs=(HBM, HBM, SEM, SEM, pl.BlockSpec(memory_space=pl.ANY)), out_specs=(HBM, HBM),
        input_output_aliases={0: 0, 1: 1},
        compiler_params=pltpu.CompilerParams(has_side_effects=EFFECT),
    )(g_thru, land_thru, send_sems, recv_sems, after)


def _pair_add(grad, pair, core, tail, name):
    rows, cols = grad.shape[1], grad.shape[2]
    total = pair.shape[1]

    def body(core_ref, *refs):
        if tail:
            g_ref, t_ref, p_ref, o_ref = refs
            o_ref[0:rows, :] = (g_ref[...].astype(F32) + p_ref[0:rows, :].astype(F32)).astype(BF16)
            o_ref[rows:total, :] = (t_ref[...].astype(F32) + p_ref[rows:total, :].astype(F32)).astype(BF16)
        else:
            g_ref, p_ref, o_ref = refs
            o_ref[...] = (g_ref[...].astype(F32) + p_ref[...].astype(F32)).astype(BF16)

    if tail:
        tc = _fit(cols, 512)
        grid = (N_CHIP, cols // tc)
        slab = pl.BlockSpec((None, total, tc), lambda q, i, core_ref: (q, 0, i))
        in_specs = [pl.BlockSpec((None, rows, tc), lambda q, i, core_ref: (2 * q + core_ref[0], 0, i)),
                    pl.BlockSpec((None, LANE, tc), lambda q, i, core_ref: (2 * q + core_ref[0] + 1, 0, i))]
    else:
        tr = _rows(rows, 1024)
        grid = (N_CHIP, rows // tr)
        slab = pl.BlockSpec((None, tr, cols), lambda q, i, core_ref: (q, i, 0))
        in_specs = [pl.BlockSpec((None, tr, cols), lambda q, i, core_ref: (2 * q + core_ref[0], i, 0))]
    return pl.pallas_call(
        body, name=name,
        grid_spec=pltpu.PrefetchScalarGridSpec(num_scalar_prefetch=1, grid=grid,
                                               in_specs=in_specs + [slab], out_specs=slab),
        out_shape=jax.ShapeDtypeStruct(pair.shape, BF16),
        compiler_params=_params("parallel", "parallel"),
    )(core, *([grad, grad] if tail else [grad]), pair)


def _gather_start(blocks, after, name):
    n = len(blocks)
    lands = [(N_DEV,) + b.shape for b in blocks]

    def body(*refs):
        b_refs, land_refs = refs[:n], refs[n:2 * n]
        send_sems, recv_sems = refs[2 * n + 1:3 * n + 1], refs[3 * n + 1:4 * n + 1]
        token = refs[-1]
        x, y, c = _place()
        me = _index((x, y, c))
        for a in range(n):
            for k, to in enumerate([(x, y, 1 - c)] + [(*chip, c) for chip in _chips(x, y)]):
                pltpu.make_async_remote_copy(src_ref=b_refs[a], dst_ref=land_refs[a].at[me], send_sem=send_sems[a].at[k],
                                             recv_sem=recv_sems[a].at[k], device_id=to, device_id_type=MESH).start()
        token[...] = jnp.zeros_like(token)

    sems = [pltpu.SemaphoreType.DMA((4,))] * n
    out = pl.pallas_call(
        body, name=name,
        out_shape=(*sems, *sems, *[pltpu.HBM(b.shape, b.dtype) for b in blocks],
                   *[pltpu.HBM(s, b.dtype) for s, b in zip(lands, blocks)], jax.ShapeDtypeStruct((8, LANE), F32)),
        in_specs=(*[HBM] * (2 * n), pl.BlockSpec(memory_space=pl.ANY)),
        out_specs=(*[SEM] * (2 * n), *[HBM] * (2 * n), pl.BlockSpec(memory_space=pltpu.VMEM)),
        input_output_aliases={i: 2 * n + i for i in range(2 * n)},
        compiler_params=pltpu.CompilerParams(has_side_effects=EFFECT),
    )(*[pltpu.with_memory_space_constraint(b, pltpu.HBM) for b in blocks],
      *[pltpu.with_memory_space_constraint(lax.empty(s, b.dtype), pltpu.HBM) for s, b in zip(lands, blocks)], after)
    return [(out[a], out[n + a], out[2 * n + a], out[3 * n + a]) for a in range(n)], out[-1]


def _gather_forward(send_sems, recv_sems, b_thru, land_thru, after, name):
    def body(b_ref, land_ref, send_sems, recv_sems, after_ref, b_dead, land_out, send2, recv2, token):
        x, y, c = _place()
        sibling = (x, y, 1 - c)
        for k, frm in enumerate([sibling] + [(*chip, c) for chip in _chips(x, y)]):
            cp = pltpu.make_async_remote_copy(src_ref=b_ref, dst_ref=land_ref.at[_index(frm)], send_sem=send_sems.at[k],
                                              recv_sem=recv_sems.at[k], device_id=frm, device_id_type=MESH)
            cp.wait_send()
            cp.wait_recv()
        for j, chip in enumerate(_chips(x, y)):
            slot = land_ref.at[_index((*chip, c))]
            pltpu.make_async_remote_copy(src_ref=slot, dst_ref=slot, send_sem=send2.at[j], recv_sem=recv2.at[j],
                                         device_id=sibling, device_id_type=MESH).start()
        token[...] = jnp.zeros_like(token)

    return pl.pallas_call(
        body, name=name,
        out_shape=(pltpu.HBM(b_thru.shape, b_thru.dtype), pltpu.HBM(land_thru.shape, land_thru.dtype),
                   pltpu.SemaphoreType.DMA((3,)), pltpu.SemaphoreType.DMA((3,)), jax.ShapeDtypeStruct((8, LANE), F32)),
        in_specs=(HBM, HBM, SEM, SEM, pl.BlockSpec(memory_space=pl.ANY)),
        out_specs=(HBM, HBM, SEM, SEM, pl.BlockSpec(memory_space=pltpu.VMEM)),
        input_output_aliases={0: 0, 1: 1},
        compiler_params=pltpu.CompilerParams(has_side_effects=EFFECT),
    )(b_thru, land_thru, send_sems, recv_sems, after)


def _gather_finish(land_thru, send2, recv2, after, name):
    def body(land_ref, send2, recv2, after_ref, land_out):
        x, y, c = _place()
        for j, chip in enumerate(_chips(x, y)):
            cp = pltpu.make_async_remote_copy(src_ref=land_ref.at[_index((*chip, c))],
                                              dst_ref=land_ref.at[_index((*chip, 1 - c))], send_sem=send2.at[j],
                                              recv_sem=recv2.at[j], device_id=(x, y, 1 - c), device_id_type=MESH)
            cp.wait_send()
            cp.wait_recv()

    return pl.pallas_call(
        body, name=name, out_shape=pltpu.HBM(land_thru.shape, land_thru.dtype),
        in_specs=(HBM, SEM, SEM, pl.BlockSpec(memory_space=pl.ANY)), out_specs=HBM,
        input_output_aliases={0: 0},
        compiler_params=pltpu.CompilerParams(has_side_effects=EFFECT),
    )(land_thru, send2, recv2, after)


class _Gathering:
    def __init__(self, first, later, me):
        started, token = _gather_start(list(first.values()), next(iter(first.values())), "gather1_first")
        cast = [_behind(a, token).astype(BF16) for a in later.values()]
        started_later, self.token = _gather_start(cast, token, "gather1_later")
        self.me, self.state = me, dict(zip([*first, *later], started + started_later))

    def begin(self, after):
        return self.token

    def forward(self, name, after):
        *self.state[name], token = _gather_forward(*self.state[name], after, "gather2_" + name)
        return token

    def get(self, name, after):
        block, land, send2, recv2 = self.state[name]
        land = _gather_finish(land, send2, recv2, after, "gather3_" + name)
        land = lax.dynamic_update_index_in_dim(land, block[None], self.me, 0)
        return land if name not in ("w_out", "w_down") else land.reshape(-1, land.shape[2])


class _Reducing:
    def __init__(self, core, chip, gather_small):
        self.core, self.chip, self.state, self.token, self.gather_small = core, chip, {}, None, gather_small

    def meanwhile(self, small, loss, after):
        self.small_sum = self.gather_small(small, loss, after)
        return self.small_sum

    def start(self, name, grad):
        g = grad if grad.ndim == 3 else grad.reshape(N_DEV, grad.shape[0] // N_DEV, grad.shape[1])
        *self.state[name], token = _exchange_start(g, _pair_route, name == "w_in", "pair_send_" + name)
        return token

    def relay(self, name, after):
        tail = name == "w_in"
        grad, pair = _exchange_wait(*self.state[name], after, _pair_route, tail, "pair_recv_" + name)
        total = _pair_add(grad, pair, self.core, tail, "pair_add_" + name)
        *self.state[name], self.token = _exchange_start(total, _chip_route, False, "chip_send_" + name)
        return self.token

    def finish(self, name, after):
        total, land = _exchange_wait(*self.state[name], after, _chip_route, False, "chip_recv_" + name)
        own = lax.dynamic_index_in_dim(total, self.chip, 0, keepdims=True)
        return lax.dynamic_update_index_in_dim(land, own, self.chip, 0)


def _assemble_w_in(g, ninp):
    _, ph, d = g.shape
    per = IN_SLAB // LANE
    assert ninp == (N_DEV + 1) * IN_SLAB and ph == IN_SLAB + LANE
    tc = _fit(d, 512)

    def body(a_ref, b_ref, o_ref):
        s = pl.program_id(0)
        head = a_ref[0:LANE, :]
        rest = a_ref[LANE:IN_SLAB, :]
        o_ref[0:LANE, :] = (jnp.where(s < N_DEV, head, jnp.zeros_like(head))
                            + jnp.where(s > 0, b_ref[...], jnp.zeros_like(head)))
        o_ref[LANE:IN_SLAB, :] = jnp.where(s < N_DEV, rest, jnp.zeros_like(rest))

    return pl.pallas_call(
        body, name="assemble_w_in", grid=(N_DEV + 1, d // tc),
        in_specs=[pl.BlockSpec((None, ph, tc), lambda s, j: (jnp.minimum(s, N_DEV - 1), 0, j)),
                  pl.BlockSpec((None, LANE, tc), lambda s, j: (jnp.maximum(s, 1) - 1, per, j))],
        out_specs=pl.BlockSpec((IN_SLAB, tc), lambda s, j: (s, j)),
        out_shape=jax.ShapeDtypeStruct((ninp, d), g.dtype),
        compiler_params=_params("parallel", "parallel"),
    )(g, g)


def _rows(n, want):
    t = min(n, want)
    t -= t % 16
    while n % t:
        t -= 16
    return t


def _adam_math(w, g, m, v):
    m2 = ADAM_B1 * m + (1.0 - ADAM_B1) * g
    v2 = ADAM_B2 * v + (1.0 - ADAM_B2) * (g * g)
    m_hat = m2 / (1.0 - ADAM_B1 ** ADAM_STEP)
    v_hat = v2 / (1.0 - ADAM_B2 ** ADAM_STEP)
    return -ADAM_LR * (m_hat / (jnp.sqrt(v_hat) + ADAM_EPS) + ADAM_WD * w), m2, v2


def _slot_sum(r_ref):
    acc = r_ref[0].astype(F32)
    for i in range(1, r_ref.shape[0]):
        acc = acc + r_ref[i].astype(F32)
    return acc


def _shift_w_in(w, ph):
    ws, d = w.shape
    tc = _fit(d, 256)

    def body(w_ref, o_ref, tall):
        tall[...] = jnp.zeros_like(tall)
        tall[0:ws, :] = w_ref[...]
        o_ref[...] = pltpu.roll(tall[...], _index(_place()), 0).astype(BF16)

    return pl.pallas_call(
        body, name="shift_w_in", grid=(d // tc,),
        in_specs=[pl.BlockSpec((ws, tc), lambda j: (0, j))],
        out_specs=pl.BlockSpec((ph, tc), lambda j: (0, j)),
        out_shape=jax.ShapeDtypeStruct((ph, d), BF16),
        scratch_shapes=[pltpu.VMEM((ph, tc), F32)], compiler_params=_params("parallel"),
    )(w)


def _sum_adamw_shifted(r, w, m, v, name):
    _, ph, d = r.shape
    ws = w.shape[0]
    tc = _fit(d, 256)

    def body(r_ref, w_ref, m_ref, v_ref, g_ref, d_ref, m2_ref, v2_ref, tall):
        tall[...] = pltpu.roll(_slot_sum(r_ref), lax.rem(ph - _index(_place()), ph), 0)
        g = tall[0:ws, :]
        g_ref[...] = g
        d_ref[...], m2_ref[...], v2_ref[...] = _adam_math(w_ref[...], g, m_ref[...], v_ref[...])

    blk = pl.BlockSpec((ws, tc), lambda j: (0, j))
    out = jax.ShapeDtypeStruct(w.shape, F32)
    return pl.pallas_call(
        body, name=name, grid=(d // tc,),
        in_specs=[pl.BlockSpec((r.shape[0], ph, tc), lambda j: (0, 0, j)), blk, blk, blk],
        out_specs=[blk] * 4, out_shape=[out] * 4,
        scratch_shapes=[pltpu.VMEM((ph, tc), F32)], compiler_params=_params("parallel"),
    )(r, w, m, v)


def _sum_slots(r, name, tr=128):
    _, rows, cols = r.shape
    tr = _rows(rows, tr)

    def body(r_ref, g_ref):
        g_ref[...] = _slot_sum(r_ref)

    return pl.pallas_call(
        body, name=name, grid=(rows // tr,),
        in_specs=[pl.BlockSpec((r.shape[0], tr, cols), lambda i: (0, i, 0))],
        out_specs=pl.BlockSpec((tr, cols), lambda i: (i, 0)),
        out_shape=jax.ShapeDtypeStruct((rows, cols), F32),
        compiler_params=_params("parallel"),
    )(r)


def _adamw(w, g, m, v, name, tr=256):
    rows, cols = w.shape
    tr = _rows(rows, tr)

    def body(w_ref, g_ref, m_ref, v_ref, d_ref, m2_ref, v2_ref):
        d_ref[...], m2_ref[...], v2_ref[...] = _adam_math(w_ref[...], g_ref[...], m_ref[...], v_ref[...])

    blk = pl.BlockSpec((tr, cols), lambda i: (i, 0))
    out = jax.ShapeDtypeStruct((rows, cols), F32)
    return pl.pallas_call(
        body, name=name, grid=(rows // tr,), in_specs=[blk] * 4, out_specs=[blk] * 3, out_shape=[out] * 3,
        compiler_params=_params("parallel"),
    )(w, g, m, v)


def _sum_adamw(r, w, m, v, name, tr=128):
    rows, cols = w.shape
    tr = _rows(rows, tr)

    def body(r_ref, w_ref, m_ref, v_ref, g_ref, d_ref, m2_ref, v2_ref):
        g = _slot_sum(r_ref)
        g_ref[...] = g
        d_ref[...], m2_ref[...], v2_ref[...] = _adam_math(w_ref[...], g, m_ref[...], v_ref[...])

    blk = pl.BlockSpec((tr, cols), lambda i: (i, 0))
    out = jax.ShapeDtypeStruct((rows, cols), F32)
    return pl.pallas_call(
        body, name=name, grid=(rows // tr,),
        in_specs=[pl.BlockSpec((r.shape[0], tr, cols), lambda i: (0, i, 0)), blk, blk, blk],
        out_specs=[blk] * 4, out_shape=[out] * 4,
        compiler_params=_params("parallel"),
    )(r, w, m, v)


def _pack(pieces, sizes):
    flat = [jnp.pad(p.reshape(-1).astype(F32), (0, s - p.size)) for p, s in zip(pieces, sizes)]
    total = sum(sizes)
    padded = -(-total // (16 * LANE)) * (16 * LANE)
    return jnp.pad(jnp.concatenate(flat), (0, padded - total)).reshape(-1, LANE)


def _unpack(packed, shapes, sizes):
    flat = packed.reshape(-1)
    out, off = [], 0
    for shp, s in zip(shapes, sizes):
        n = 1
        for k in shp:
            n *= k
        out.append(flat[off:off + n].reshape(shp))
        off += s
    return out


def _lanes(n):
    return -(-n // LANE) * LANE


WEIGHTS = ("w_in", "b_gates", "w_sc_conv", "mh_gain", "w_out", "ln1_g", "ln1_b", "w_up", "w_ffn_conv", "b_ffn_conv",
           "w_down", "ln2_g", "ln2_b")
BIG = ("w_in", "w_out", "w_up", "w_down")
SMALL = tuple(n for n in WEIGHTS if n not in BIG)


def kernel(x, w_in, b_gates, w_sc_conv, mh_gain, w_out, ln1_g, ln1_b, w_up, w_ffn_conv, b_ffn_conv, w_down, ln2_g, ln2_b, loss_target, m_w_in, m_b_gates, m_w_sc_conv, m_mh_gain, m_w_out, m_ln1_g, m_ln1_b, m_w_up, m_w_ffn_conv, m_b_ffn_conv, m_w_down, m_ln2_g, m_ln2_b, v_w_in, v_b_gates, v_w_sc_conv, v_mh_gain, v_w_out, v_ln1_g, v_ln1_b, v_w_up, v_w_ffn_conv, v_b_ffn_conv, v_w_down, v_ln2_g, v_ln2_b):
    w = dict(zip(WEIGHTS, (w_in, b_gates, w_sc_conv, mh_gain, w_out, ln1_g, ln1_b, w_up, w_ffn_conv, b_ffn_conv,
                           w_down, ln2_g, ln2_b)))
    m = dict(zip(WEIGHTS, (m_w_in, m_b_gates, m_w_sc_conv, m_mh_gain, m_w_out, m_ln1_g, m_ln1_b, m_w_up,
                           m_w_ffn_conv, m_b_ffn_conv, m_w_down, m_ln2_g, m_ln2_b)))
    v = dict(zip(WEIGHTS, (v_w_in, v_b_gates, v_w_sc_conv, v_mh_gain, v_w_out, v_ln1_g, v_ln1_b, v_w_up,
                           v_w_ffn_conv, v_b_ffn_conv, v_w_down, v_ln2_g, v_ln2_b)))
    me = _index(_place())
    d = x.shape[2]
    ws_in = w_in.shape[2]
    assert ws_in == IN_SLAB + 1 and N_DEV <= LANE, w_in.shape
    ninp = (N_DEV + 1) * IN_SLAB
    ws_sc, ws_fc = w_sc_conv.shape[2], w_ffn_conv.shape[2]
    w_in_t, m_in_t, v_in_t = (jnp.transpose(a[0]) for a in (w_in, m_w_in, v_w_in))

    w_in_shift = _shift_w_in(w_in_t, IN_SLAB + LANE)
    taps8 = lambda a: jnp.pad(a[0], ((0, 5), (0, 0)))
    wx = _Gathering(dict(w_in=w_in_shift, w_sc=taps8(w_sc_conv), w_fc=taps8(w_ffn_conv)),
                    {n: w[n][0] for n in ("w_out", "w_up", "w_down")}, me)
    token = wx.begin(None)
    for n in ("w_in", "w_sc", "w_fc"):
        token = wx.forward(n, token)
    g_in, g_sc, g_fc = (wx.get(n, token) for n in ("w_in", "w_sc", "w_fc"))
    w_in_full = _assemble_w_in(g_in, ninp)
    w_sc_full = g_sc[:, :3].transpose(1, 0, 2).reshape(3, N_DEV * ws_sc)
    w_fc_full = g_fc[:, :3].transpose(1, 0, 2).reshape(3, N_DEV * ws_fc)

    xi, yi, ci = _place()
    names = ("loss",) + SMALL
    pieces = {}

    def gather_small(small, loss_t, after):
        pieces.update(small, loss=loss_t[0, :1])
        sizes = [_lanes(pieces[n].size) for n in names]
        (g_small,) = _all_gather([_behind(_pack([pieces[n] for n in names], sizes), after)], "gather_small")
        return _sum_slots(g_small, "sum_small", tr=g_small.shape[1])

    gx = _Reducing(jnp.reshape(ci, (1,)).astype(jnp.int32), 2 * xi + yi, gather_small)
    loss_t, grad_x, small, _ = _local_step(
        x[0], loss_target[0], w_in_full, b_gates, w_sc_full, mh_gain, None, ln1_g, ln1_b, None,
        w_fc_full, b_ffn_conv, None, ln2_g, ln2_b, gx=gx, wx=wx)

    grads, deltas, new_m, new_v = {}, {}, {}, {}
    for name in ("w_down", "w_up", "w_out"):
        grads[name], deltas[name], new_m[name], new_v[name] = _sum_adamw(
            gx.finish(name, gx.token), w[name][0], m[name][0], v[name][0], "adamw_" + name)

    summed = _unpack(gx.small_sum, [pieces[n].shape for n in names], [_lanes(pieces[n].size) for n in names])
    full = dict(zip(names, summed))
    full["w_sc_conv"] = lax.dynamic_slice(full["w_sc_conv"], (0, me * ws_sc), (3, ws_sc))
    full["w_ffn_conv"] = lax.dynamic_slice(full["w_ffn_conv"], (0, me * ws_fc), (3, ws_fc))
    for n in SMALL:
        grads[n] = full[n].reshape(w[n].shape)
    sizes = [_lanes(w[n].size) for n in SMALL]
    shapes = [w[n].shape for n in SMALL]
    packed = [_pack([t[n] for n in SMALL], sizes) for t in (w, grads, m, v)]
    small_out = _adamw(*packed, "adamw_small")
    for res, t in zip(small_out, (deltas, new_m, new_v)):
        t.update(zip(SMALL, _unpack(res, shapes, sizes)))

    done = sum(t[0:1, 0:1] for t in (grad_x, deltas["w_down"], deltas["w_up"], deltas["w_out"], small_out[0]))
    grads["w_in"], deltas["w_in"], new_m["w_in"], new_v["w_in"] = (
        jnp.transpose(a)[None] for a in _sum_adamw_shifted(gx.finish("w_in", done), w_in_t, m_in_t, v_in_t, "adamw_w_in"))

    big = lambda t: {n: (t[n].reshape(w[n].shape) if n in BIG else t[n]) for n in WEIGHTS}
    grads, deltas, new_m, new_v = big(grads), big(deltas), big(new_m), big(new_v)
    return (full["loss"].reshape(()), grad_x[None], *[grads[n] for n in WEIGHTS], *[deltas[n] for n in WEIGHTS],
            *[new_m[n] for n in WEIGHTS], *[new_v[n] for n in WEIGHTS])
```

```python
import functools

import jax
import jax.numpy as jnp
from jax import lax
from jax.experimental import pallas as pl
from jax.experimental.pallas import tpu as pltpu

F32 = jnp.float32
BF16 = jnp.bfloat16
MESH = pl.DeviceIdType.MESH

N_DEV = 8
NH = 4
CHUNK = 64
LN_EPS = 1e-5
HN_EPS = 1e-6
ALPHA = 2.0 ** 0.25
LANE = 128
IN_SLAB = 7 * LANE
IN_TAIL = 16
VMEM_LIMIT = 56 * 1024 * 1024
ADAM_LR, ADAM_B1, ADAM_B2, ADAM_EPS, ADAM_WD, ADAM_STEP = 0.001, 0.9, 0.999, 1e-08, 0.01, 10

_NN = (((1,), (0,)), ((), ()))
_NT = (((1,), (1,)), ((), ()))
_TN = (((0,), (0,)), ((), ()))


def _dot(a, b, dn=_NN):
    return lax.dot_general(a, b, dn, preferred_element_type=F32)


def _params(*sem):
    return pltpu.CompilerParams(dimension_semantics=sem if sem else None, vmem_limit_bytes=VMEM_LIMIT)


def _iota(shape, axis):
    return lax.broadcasted_iota(jnp.int32, shape, axis)


def _fit(n, want):
    if n <= want:
        return n
    t = want - want % LANE
    while n % t:
        t -= LANE
    return t


def _matmul(a, b, mode, out_dtype, name, tm=1024, tn=512, tk=1024, add=None, add_scale=1.0,
            a_blocked=False, b_blocked=False, o_width=None, after=None):
    a_parts = a if isinstance(a, tuple) else None
    b_parts = b if isinstance(b, tuple) else None
    if a_parts:
        a_blocked, (a_rows, wa), na = True, a[0].shape, len(a)
        kd, m = (a_rows, na * wa) if mode == "tn" else (na * wa, a_rows)
    elif a_blocked:
        na, a_rows, wa = a.shape
        kd, m = (a_rows, na * wa) if mode == "tn" else (na * wa, a_rows)
    elif mode == "tn":
        kd, m = a.shape
    else:
        m, kd = a.shape
    if b_parts:
        b_blocked, (rows, w), nb = True, b[0].shape, len(b)
    elif b_blocked:
        nb, rows, w = b.shape
    if b_blocked:
        n = rows if mode == "nt" else nb * w
        assert (nb * w if mode == "nt" else rows) == kd, (name, kd)
    else:
        n = b.shape[0] if mode == "nt" else b.shape[1]
    tm, tn, tk = _fit(m, tm), _fit(n, tn), _fit(kd, tk)
    if a_blocked and mode == "tn":
        tm = _fit(wa, tm)
    if a_blocked and mode != "tn":
        tk = _fit(wa, tk)
    if b_blocked and mode != "nt":
        tn = _fit(w, tn)
    if b_blocked and mode == "nt":
        tk = _fit(w, tk)
    if o_width is not None:
        tn = _fit(o_width, tn)
    assert m % tm == 0 and n % tn == 0 and kd % tk == 0, (name, m, n, kd, tm, tn, tk)
    assert not (a_blocked and mode != "tn" and wa % tk) and not (b_blocked and mode == "nt" and w % tk), (name, tk)
    nk = kd // tk
    dn = {"nn": _NN, "nt": _NT, "tn": _TN}[mode]
    if a_blocked and mode == "tn":
        a_per = wa // tm
        a_spec = pl.BlockSpec((None, tk, tm), lambda i, j, k: (i // a_per, k, i % a_per))
    elif a_blocked:
        a_per = wa // tk
        a_spec = pl.BlockSpec((None, tm, tk), lambda i, j, k: (k // a_per, i, k % a_per))
    elif mode == "tn":
        a_spec = pl.BlockSpec((tk, tm), lambda i, j, k: (k, i))
    else:
        a_spec = pl.BlockSpec((tm, tk), lambda i, j, k: (i, k))
    if b_blocked and mode != "nt":
        per = w // tn
        b_spec = pl.BlockSpec((None, tk, tn), lambda i, j, k: (j // per, k, j % per))
    elif b_blocked:
        per = w // tk
        b_spec = pl.BlockSpec((None, tn, tk), lambda i, j, k: (k // per, j, k % per))
    elif mode == "nt":
        b_spec = pl.BlockSpec((tn, tk), lambda i, j, k: (j, k))
    else:
        b_spec = pl.BlockSpec((tk, tn), lambda i, j, k: (k, j))
    if o_width is None:
        o_spec = pl.BlockSpec((tm, tn), lambda i, j, k: (i, j))
        o_shape = (m, n)
    else:
        oper = o_width // tn
        o_spec = pl.BlockSpec((None, tm, tn), lambda i, j, k: (j // oper, i, j % oper))
        o_shape = (n // o_width, m, o_width)
    a_list, a_specs = [a], [a_spec]
    if a_parts:
        hold = lambda x, s: jnp.clip(x - s * a_per, 0, a_per - 1)
        a_list = list(a_parts)
        a_specs = [(pl.BlockSpec((tk, tm), lambda i, j, k, s=s: (k, hold(i, s))) if mode == "tn"
                    else pl.BlockSpec((tm, tk), lambda i, j, k, s=s: (i, hold(k, s)))) for s in range(na)]
    b_list, b_specs = [b], [b_spec]
    if b_parts:
        hold_b = lambda x, s: jnp.clip(x - s * per, 0, per - 1)
        b_list = list(b_parts)
        b_specs = [(pl.BlockSpec((tn, tk), lambda i, j, k, s=s: (j, hold_b(k, s))) if mode == "nt"
                    else pl.BlockSpec((tk, tn), lambda i, j, k, s=s: (k, hold_b(j, s)))) for s in range(nb)]
    n_a, n_b = len(a_list), len(b_list)
    has_add = add is not None
    n_in = n_a + n_b + has_add + (after is not None)
    in_place = nk > 1 and out_dtype == F32

    def body(*refs):
        add_ref = refs[n_a + n_b] if has_add else None
        o_ref = refs[n_in]
        i, j, k = pl.program_id(0), pl.program_id(1), pl.program_id(2)

        def finish(r):
            if has_add:
                r = r + add_scale * add_ref[...]
            o_ref[...] = r.astype(out_dtype)

        def step(a_ref, b_ref):
            if nk == 1:
                finish(_dot(a_ref[...], b_ref[...], dn))
                return
            acc = o_ref if in_place else refs[-1]

            @pl.when(k == 0)
            def _():
                acc[...] = _dot(a_ref[...], b_ref[...], dn)

            @pl.when(k > 0)
            def _():
                acc[...] += _dot(a_ref[...], b_ref[...], dn)

        if n_a == 1 and n_b == 1:
            step(refs[0], refs[1])
        else:
            slab_a = ((i if mode == "tn" else k) // a_per) if n_a > 1 else 0
            slab_b = ((k if mode == "nt" else j) // per) if n_b > 1 else 0
            for sa in range(n_a):
                for sb in range(n_b):
                    pl.when((slab_a == sa) & (slab_b == sb))(functools.partial(step, refs[sa], refs[n_a + sb]))
        if nk > 1 and not (in_place and not has_add):
            @pl.when(k == nk - 1)
            def _():
                finish((o_ref if in_place else refs[-1])[...])

    in_specs = a_specs + b_specs + ([pl.BlockSpec((tm, tn), lambda i, j, k: (i, j))] if has_add else [])
    args = (*a_list, *b_list) + ((add,) if has_add else ())
    if after is not None:
        in_specs.append(pl.BlockSpec(memory_space=pl.ANY))
        args += (after,)
    return pl.pallas_call(
        body, name=name, grid=(m // tm, n // tn, nk),
        in_specs=in_specs, out_specs=o_spec,
        out_shape=jax.ShapeDtypeStruct(o_shape, out_dtype),
        scratch_shapes=[pltpu.VMEM((tm, tn), F32)] if nk > 1 and not in_place else [],
        compiler_params=_params("parallel", "parallel", "arbitrary"),
    )(*args)


def _shift_down(u, s):
    return jnp.where(_iota(u.shape, 0) >= s, pltpu.roll(u, s, 0), 0.0)


def _shift_up(u, s):
    t = u.shape[0]
    return jnp.where(_iota(u.shape, 0) < t - s, pltpu.roll(u, t - s, 0), 0.0)


SLAB = 8


def _rolled(u):
    return pltpu.roll(u, 2, 0), pltpu.roll(u, 1, 0)


def _conv(u, w, rolled=None):
    u2, u1 = _rolled(u) if rolled is None else rolled
    raw = w[0:1] * u2 + w[1:2] * u1 + w[2:3] * u
    head = u[0:SLAB]
    mended = w[0:1] * _shift_down(head, 2) + w[1:2] * _shift_down(head, 1) + w[2:3] * head
    return jnp.concatenate([mended, raw[SLAB:]], axis=0)


def _conv_t(dy, w):
    t = dy.shape[0]
    raw = w[2:3] * dy + w[1:2] * pltpu.roll(dy, t - 1, 0) + w[0:1] * pltpu.roll(dy, t - 2, 0)
    tail = dy[t - SLAB:]
    mended = w[2:3] * tail + w[1:2] * _shift_up(tail, 1) + w[0:1] * _shift_up(tail, 2)
    return jnp.concatenate([raw[:t - SLAB], mended], axis=0)


def _conv_dw(dy, u, rolled=None):
    t = dy.shape[0]
    u2, u1 = _rolled(u) if rolled is None else rolled
    head, tail = dy[0:SLAB], u[t - SLAB:]
    r = _iota(head.shape, 0)
    wrap2 = jnp.sum(jnp.where(r < 2, head * pltpu.roll(tail, 2, 0), 0.0), axis=0, keepdims=True)
    wrap1 = jnp.sum(jnp.where(r < 1, head * pltpu.roll(tail, 1, 0), 0.0), axis=0, keepdims=True)
    d0 = jnp.sum(dy * u2, axis=0, keepdims=True) - wrap2
    d1 = jnp.sum(dy * u1, axis=0, keepdims=True) - wrap1
    d2 = jnp.sum(dy * u, axis=0, keepdims=True)
    r3 = _iota((3, dy.shape[1]), 0)
    return jnp.where(r3 == 0, d0, jnp.where(r3 == 1, d1, d2))


def _sigmoid(x):
    return 0.5 * jnp.tanh(0.5 * x) + 0.5


def _sconv_fwd(proj, w_sc, t, wc):
    nb = wc // LANE

    def body(cb_ref, cc_ref, ch_ref, w_ref, y_ref):
        u = cc_ref[...] * ch_ref[...]
        y_ref[...] = (cb_ref[...] * _conv(u, w_ref[...])).astype(BF16)

    col = lambda off: pl.BlockSpec((t, LANE), lambda j: (0, j + off))
    return pl.pallas_call(
        body, name="sconv_fwd", grid=(nb,),
        in_specs=[col(0), col(nb), col(2 * nb), pl.BlockSpec((3, LANE), lambda j: (0, j))],
        out_specs=pl.BlockSpec((None, t, LANE), lambda j: (0, 0, j)),
        out_shape=jax.ShapeDtypeStruct((2, t, wc), BF16),
        compiler_params=_params("parallel"),
    )(proj, proj, proj, w_sc)


def _sconv_bwd(dy, proj, w_sc, t, wc):
    nb = wc // LANE

    def body(dy_ref, cb_ref, cc_ref, ch_ref, w_ref, dcb_ref, dcc_ref, dch_ref, dw_ref):
        cc, ch, w, d = cc_ref[...], ch_ref[...], w_ref[...], dy_ref[...]
        u = cc * ch
        ru = _rolled(u)
        dcb_ref[...] = (d * _conv(u, w, ru)).astype(BF16)
        dcu = d * cb_ref[...]
        dw_ref[...] = _conv_dw(dcu, u, ru)
        du = _conv_t(dcu, w)
        dcc_ref[...] = (du * ch).astype(BF16)
        dch_ref[...] = (du * cc).astype(BF16)

    col = lambda off: pl.BlockSpec((t, LANE), lambda j: (0, j + off))
    act = jax.ShapeDtypeStruct((t, wc), BF16)
    return pl.pallas_call(
        body, name="sconv_bwd", grid=(nb,),
        in_specs=[col(0), col(0), col(nb), col(2 * nb), pl.BlockSpec((3, LANE), lambda j: (0, j))],
        out_specs=[col(0), col(0), col(0), pl.BlockSpec((3, LANE), lambda j: (0, j))],
        out_shape=[act, act, act, jax.ShapeDtypeStruct((3, wc), F32)],
        compiler_params=_params("parallel"),
    )(dy, proj, proj, proj, w_sc)


def _gates_prep(proj, bias_tile, t, gate_tile):
    def body(g_ref, b_ref, o_ref):
        g = g_ref[...] + b_ref[...]
        lane = _iota(g.shape, 1)
        is_f = (lane >= NH) & (lane < 2 * NH)
        lf = jnp.minimum(g, 0.0) - jnp.log(1.0 + jnp.exp(-jnp.abs(g)))
        c = jnp.where(is_f, lf, 0.0)
        r = _iota(g.shape, 0) % CHUNK
        s = 1
        while s < CHUNK:
            c = c + jnp.where(r >= s, pltpu.roll(c, s, 0), 0.0)
            s *= 2
        o_ref[...] = jnp.where(is_f, c, jnp.where(lane < NH, g, 0.0))

    return pl.pallas_call(
        body, name="gates_prep", grid=(1,),
        in_specs=[pl.BlockSpec((t, LANE), lambda i: (0, gate_tile)), pl.BlockSpec((1, LANE), lambda i: (0, 0))],
        out_specs=pl.BlockSpec((t, LANE), lambda i: (0, 0)),
        out_shape=jax.ShapeDtypeStruct((t, LANE), F32),
        compiler_params=_params("arbitrary"),
    )(proj, bias_tile)


def _gates_bwd(dgate, proj, bias_tile, t, gate_tile):
    def body(dg_ref, g_ref, b_ref, o_ref, s_ref):
        g = g_ref[...] + b_ref[...]
        lane = _iota(g.shape, 1)
        r = _iota(g.shape, 0) % CHUNK
        dsig = 1.0 - _sigmoid(g)
        out = jnp.zeros(g.shape, F32)
        for h in range(NH):
            d = dg_ref[h]
            c = d
            s = 1
            while s < CHUNK:
                c = c + jnp.where(r + s < CHUNK, pltpu.roll(c, t - s, 0), 0.0)
                s *= 2
            di = jnp.broadcast_to(d[:, 0:1], g.shape)
            db = jnp.broadcast_to(c[:, 1:2], g.shape)
            out = out + jnp.where(lane == h, di, 0.0) + jnp.where(lane == NH + h, db * dsig, 0.0)
        o_ref[...] = out.astype(BF16)
        s_ref[...] = jnp.sum(out, axis=0, keepdims=True)

    return pl.pallas_call(
        body, name="gates_bwd", grid=(1,),
        in_specs=[pl.BlockSpec((NH, t, LANE), lambda i: (0, 0, 0)),
                  pl.BlockSpec((t, LANE), lambda i: (0, gate_tile)), pl.BlockSpec((1, LANE), lambda i: (0, 0))],
        out_specs=[pl.BlockSpec((t, LANE), lambda i: (0, 0)), pl.BlockSpec((1, LANE), lambda i: (0, 0))],
        out_shape=[jax.ShapeDtypeStruct((t, LANE), BF16), jax.ShapeDtypeStruct((1, LANE), F32)],
        compiler_params=_params("arbitrary"),
    )(dgate, proj, bias_tile)


def _chunk_gates(gc, gr, h, mprev):
    L = CHUNK
    icol, bcol = gc[:, h:h + 1], gc[:, h + NH:h + NH + 1]
    irow, brow = gr[h:h + 1, :], gr[h + NH:h + NH + 1, :]
    tri = _iota((L, L), 0) >= _iota((L, L), 1)
    log_d = jnp.where(tri, bcol - brow + irow, -jnp.inf)
    inter = bcol + mprev
    mt = jnp.maximum(inter, jnp.max(log_d, axis=1, keepdims=True))
    dw = jnp.exp(log_d - mt)
    iw = jnp.exp(inter - mt)
    g = brow[:, L - 1:L]
    wlog_col = g - bcol + icol
    wlog_row = g - brow + irow
    mnew = jnp.maximum(g + mprev, jnp.max(wlog_row, axis=1, keepdims=True))
    wcol = jnp.exp(wlog_col - mnew)
    decay = jnp.exp(g + mprev - mnew)
    return dw, iw, mt, wcol, decay, mnew


def _mlstm_fwd(proj, gcol, grow, t, wc, dh):
    nc = t // CHUNK
    wm = NH * dh
    assert wc == wm, (wc, wm)
    qoff = 3 * wc // wm
    scale = dh ** -0.5

    def body(q_ref, k_ref, v_ref, gc_ref, gr_ref, h_ref, cs_ref, ns_ref, c_s, n_s, m_s):
        @pl.when(pl.program_id(0) == 0)
        def _():
            c_s[...] = jnp.zeros_like(c_s)
            n_s[...] = jnp.zeros_like(n_s)
            m_s[...] = jnp.zeros_like(m_s)

        gc, gr = gc_ref[...], gr_ref[0]
        for h in range(NH):
            cols = slice(h * dh, (h + 1) * dh)
            mprev = m_s[h, 0:1, 0:1]
            cprev = c_s[h]
            n8 = n_s[h]
            nprev = n8[0:1]
            cs_ref[h] = cprev
            ns_ref[h] = jnp.where(_iota(n8.shape, 0) == 1, mprev, n8)

            dw, iw, mt, wcol, decay, mnew = _chunk_gates(gc, gr, h, mprev)
            qs = q_ref[:, cols] * scale
            k = k_ref[:, cols]
            qs_b, k_b, v_b = qs.astype(BF16), k.astype(BF16), v_ref[:, cols].astype(BF16)
            s = _dot(qs_b, k_b, _NT) * dw
            num = _dot(s.astype(BF16), v_b) + iw * _dot(qs_b, cprev.astype(BF16))
            den = jnp.sum(s, axis=1, keepdims=True) + iw * jnp.sum(qs * nprev, axis=1, keepdims=True)
            h_ref[:, cols] = num / jnp.maximum(jnp.abs(den), jnp.exp(-mt))

            wk = wcol * k
            c_s[h] = decay * cprev + _dot(wk.astype(BF16), v_b, _TN)
            n_s[h] = decay * n8 + jnp.sum(wk, axis=0, keepdims=True)
            m_s[h] = jnp.broadcast_to(mnew, m_s.shape[1:])

    grp = lambda off: pl.BlockSpec((CHUNK, wm), lambda c: (c, qoff + off))
    return pl.pallas_call(
        body, name="mlstm_fwd", grid=(nc,),
        in_specs=[grp(0), grp(1), grp(2),
                  pl.BlockSpec((CHUNK, LANE), lambda c: (c, 0)),
                  pl.BlockSpec((1, 8, CHUNK), lambda c: (c, 0, 0))],
        out_specs=[pl.BlockSpec((CHUNK, wm), lambda c: (c, 0)),
                   pl.BlockSpec((NH, None, dh, dh), lambda c: (0, c, 0, 0)),
                   pl.BlockSpec((NH, None, 8, dh), lambda c: (0, c, 0, 0))],
        out_shape=[jax.ShapeDtypeStruct((t, wm), F32),
                   jax.ShapeDtypeStruct((NH, nc, dh, dh), F32),
                   jax.ShapeDtypeStruct((NH, nc, 8, dh), F32)],
        scratch_shapes=[pltpu.VMEM((NH, dh, dh), F32), pltpu.VMEM((NH, 8, dh), F32), pltpu.VMEM((NH, 8, LANE), F32)],
        compiler_params=_params("arbitrary"),
    )(proj, proj, proj, gcol, grow)


def _mlstm_bwd(proj, gcol, grow, hval, dh_in, cs, ns, t, wc, dh):
    nc = t // CHUNK
    wm = NH * dh
    assert wc == wm, (wc, wm)
    qoff = 3 * wc // wm
    scale = dh ** -0.5
    L = CHUNK

    def body(q_ref, k_ref, v_ref, gc_ref, gr_ref, h_ref, dh_ref, cs_ref, ns_ref,
             dq_ref, dk_ref, dv_ref, dg_ref, dc_s, dn_s):
        @pl.when(pl.program_id(0) == 0)
        def _():
            dc_s[...] = jnp.zeros_like(dc_s)
            dn_s[...] = jnp.zeros_like(dn_s)

        gc, gr = gc_ref[...], gr_ref[0]
        eye = _iota((L, L), 0) == _iota((L, L), 1)
        lane = _iota((L, LANE), 1)
        last = _iota((L, 1), 0) == L - 1
        for h in range(NH):
            cols = slice(h * dh, (h + 1) * dh)
            ns8 = ns_ref[h]
            nprev = ns8[0:1]
            mprev = ns8[1:2, 0:1]
            cprev = cs_ref[h]
            dcn = dc_s[h]
            dn8 = dn_s[h]
            dnn = dn8[0:1]

            dw, iw, mt, wcol, decay, _ = _chunk_gates(gc, gr, h, mprev)
            qs = q_ref[:, cols] * scale
            k = k_ref[:, cols]
            qs_b, k_b, v_b = qs.astype(BF16), k.astype(BF16), v_ref[:, cols].astype(BF16)
            qk = _dot(qs_b, k_b, _NT)
            s = qk * dw
            den = jnp.sum(s, axis=1, keepdims=True) + iw * jnp.sum(qs * nprev, axis=1, keepdims=True)
            emt = jnp.exp(-mt)
            r = 1.0 / jnp.maximum(jnp.abs(den), emt)
            dout = dh_ref[:, cols]
            dnum = dout * r
            dden = (-jnp.sum(dout * h_ref[:, cols], axis=1, keepdims=True) * r
                    * jnp.where(jnp.abs(den) > emt, jnp.sign(den), 0.0))
            dnum_b = dnum.astype(BF16)
            cprev_b = cprev.astype(BF16)
            dcn_b = dcn.astype(BF16)

            gd = (_dot(dnum_b, v_b, _NT) + dden) * dw
            gd_b = gd.astype(BF16)
            dqs_inter = iw * (_dot(dnum_b, cprev_b, _NT) + dden * nprev)
            dqs = _dot(gd_b, k_b) + dqs_inter
            dk_inter = wcol * (_dot(v_b, dcn_b, _NT) + dnn)
            dk = _dot(gd_b, qs_b, _TN) + dk_inter
            wk = wcol * k
            dv = _dot(s.astype(BF16), dnum_b, _TN) + _dot(wk.astype(BF16), dcn_b)

            e = gd * qk
            e_cols = jnp.sum(jnp.where(eye, jnp.sum(e, axis=0, keepdims=True), 0.0), axis=1, keepdims=True)
            k_inter = jnp.sum(k * dk_inter, axis=1, keepdims=True)
            rq = jnp.sum(e, axis=1, keepdims=True) + jnp.sum(qs * dqs_inter, axis=1, keepdims=True)
            rk = e_cols + k_inter
            hsum = jnp.sum(k_inter, axis=0, keepdims=True)
            jdec = decay * (jnp.sum(jnp.sum(dcn * cprev, axis=1, keepdims=True), axis=0, keepdims=True)
                            + jnp.sum(dnn * nprev, axis=1, keepdims=True))
            db = rq - rk + jnp.where(last, hsum + jdec, 0.0)
            dg_ref[h] = jnp.where(lane == 0, rk, jnp.where(lane == 1, db, 0.0))

            dq_ref[:, cols] = (dqs * scale).astype(BF16)
            dk_ref[:, cols] = dk.astype(BF16)
            dv_ref[:, cols] = dv.astype(BF16)

            iq = iw * qs
            dc_s[h] = decay * dcn + _dot(iq.astype(BF16), dnum_b, _TN)
            dn_s[h] = decay * dn8 + jnp.sum(iq * dden, axis=0, keepdims=True)

    rc = lambda c: nc - 1 - c
    grp = lambda off: pl.BlockSpec((L, wm), lambda c: (rc(c), qoff + off))
    hm = pl.BlockSpec((L, wm), lambda c: (rc(c), 0))
    act = jax.ShapeDtypeStruct((t, wm), BF16)
    return pl.pallas_call(
        body, name="mlstm_bwd", grid=(nc,),
        in_specs=[grp(0), grp(1), grp(2),
                  pl.BlockSpec((L, LANE), lambda c: (rc(c), 0)),
                  pl.BlockSpec((1, 8, L), lambda c: (rc(c), 0, 0)),
                  hm, hm,
                  pl.BlockSpec((NH, None, dh, dh), lambda c: (0, rc(c), 0, 0)),
                  pl.BlockSpec((NH, None, 8, dh), lambda c: (0, rc(c), 0, 0))],
        out_specs=[hm, hm, hm, pl.BlockSpec((NH, L, LANE), lambda c: (0, rc(c), 0))],
        out_shape=[act, act, act, jax.ShapeDtypeStruct((NH, t, LANE), F32)],
        scratch_shapes=[pltpu.VMEM((NH, dh, dh), F32), pltpu.VMEM((NH, 8, dh), F32)],
        compiler_params=_params("arbitrary"),
    )(proj, proj, proj, gcol, grow, hval, dh_in, cs, ns)


def _head_norm(hv):
    mu = jnp.mean(hv, axis=1, keepdims=True)
    hc = hv - mu
    rstd = lax.rsqrt(jnp.mean(hc * hc, axis=1, keepdims=True) + HN_EPS)
    return hc * rstd, rstd


def _hnorm_fwd(hval, proj, gain, y, t, wc, dh, tr=256):
    ooff = 3 * wc // dh + 3 * NH

    def body(h_ref, o_ref, g_ref, y_in, y_ref):
        hhat, _ = _head_norm(h_ref[...])
        y_ref[...] = (_sigmoid(o_ref[...]) * hhat * g_ref[...]).astype(BF16)

    return pl.pallas_call(
        body, name="hnorm_fwd", grid=(t // tr, NH),
        in_specs=[pl.BlockSpec((tr, dh), lambda i, h: (i, h)),
                  pl.BlockSpec((tr, dh), lambda i, h: (i, ooff + h)),
                  pl.BlockSpec((1, dh), lambda i, h: (0, h)),
                  pl.BlockSpec(memory_space=pl.ANY)],
        out_specs=pl.BlockSpec((None, tr, dh), lambda i, h: (1, i, h)),
        out_shape=jax.ShapeDtypeStruct(y.shape, BF16),
        input_output_aliases={3: 0},
        compiler_params=_params("parallel", "parallel"),
    )(hval, proj, gain, y)


def _hnorm_bwd(dy, hval, proj, gain, t, wc, dh, tr=256):
    ooff = 3 * wc // dh + 3 * NH
    yoff = wc // dh

    def body(dy_ref, h_ref, o_ref, g_ref, do_ref, dh_ref, dg_ref):
        i = pl.program_id(1)
        hhat, rstd = _head_norm(h_ref[...])
        gain_v = g_ref[...]
        sig = _sigmoid(o_ref[...])
        d = dy_ref[...]
        do_ref[...] = (d * hhat * gain_v * sig * (1.0 - sig)).astype(BF16)
        dhn = d * sig
        part = jnp.sum(dhn * hhat, axis=0, keepdims=True)

        @pl.when(i == 0)
        def _():
            dg_ref[...] = part

        @pl.when(i > 0)
        def _():
            dg_ref[...] += part

        dhat = dhn * gain_v
        dh_ref[...] = rstd * (dhat - jnp.mean(dhat, axis=1, keepdims=True)
                              - hhat * jnp.mean(dhat * hhat, axis=1, keepdims=True))

    blk = lambda off: pl.BlockSpec((tr, dh), lambda h, i: (i, off + h))
    return pl.pallas_call(
        body, name="hnorm_bwd", grid=(NH, t // tr),
        in_specs=[blk(yoff), blk(0), blk(ooff), pl.BlockSpec((1, dh), lambda h, i: (0, h))],
        out_specs=[blk(0), blk(0), pl.BlockSpec((1, dh), lambda h, i: (0, h))],
        out_shape=[jax.ShapeDtypeStruct((t, NH * dh), BF16), jax.ShapeDtypeStruct((t, NH * dh), F32),
                   jax.ShapeDtypeStruct((1, NH * dh), F32)],
        compiler_params=_params("parallel", "arbitrary"),
    )(dy, hval, proj, gain)


def _ln_stats(z):
    mu = jnp.mean(z, axis=1, keepdims=True)
    zc = z - mu
    rstd = lax.rsqrt(jnp.mean(zc * zc, axis=1, keepdims=True) + LN_EPS)
    return zc * rstd, rstd


def _ln_bwd(dy, xhat, rstd, g):
    dxh = dy * g
    return rstd * (dxh - jnp.mean(dxh, axis=1, keepdims=True) - xhat * jnp.mean(dxh * xhat, axis=1, keepdims=True))


def _accum(ref, i, part):
    @pl.when(i == 0)
    def _():
        ref[...] = part

    @pl.when(i > 0)
    def _():
        ref[...] += part


def _ln1_fwd(x, mix, g, b, tr=256):
    t, d = x.shape

    def body(x_ref, m_ref, g_ref, b_ref, xh_ref, rs_ref, xb_ref):
        xhat, rstd = _ln_stats(ALPHA * x_ref[...] + m_ref[...])
        xh_ref[...] = xhat
        rs_ref[...] = rstd
        xb_ref[...] = (xhat * g_ref[...] + b_ref[...]).astype(BF16)

    row = pl.BlockSpec((tr, d), lambda i: (i, 0))
    vec = pl.BlockSpec((1, d), lambda i: (0, 0))
    return pl.pallas_call(
        body, name="ln1_fwd", grid=(t // tr,),
        in_specs=[row, row, vec, vec],
        out_specs=[row, pl.BlockSpec((tr, 1), lambda i: (i, 0)), row],
        out_shape=[jax.ShapeDtypeStruct((t, d), F32), jax.ShapeDtypeStruct((t, 1), F32),
                   jax.ShapeDtypeStruct((t, d), BF16)],
        compiler_params=_params("parallel"),
    )(x, mix, g, b)


def _ln2_loss(xhat1, g1, b1, ff, target, g2, b2, tr=256):
    t, d = ff.shape

    def body(xh_ref, g1_ref, b1_ref, f_ref, t_ref, g_ref, b_ref, dz_ref, dzb_ref, dg_ref, db_ref, l_ref):
        i = pl.program_id(0)
        x1 = xh_ref[...] * g1_ref[...] + b1_ref[...]
        xhat, rstd = _ln_stats(ALPHA * x1 + f_ref[...])
        gv = g_ref[...]
        e = xhat * gv + b_ref[...] - t_ref[...]
        lsum = jnp.sum(jnp.sum(e * e, axis=1, keepdims=True), axis=0, keepdims=True) * (0.5 / d)
        dy = e * (1.0 / d)
        _accum(dg_ref, i, jnp.sum(dy * xhat, axis=0, keepdims=True))
        _accum(db_ref, i, jnp.sum(dy, axis=0, keepdims=True))
        _accum(l_ref, i, jnp.broadcast_to(lsum, l_ref.shape))
        dz = _ln_bwd(dy, xhat, rstd, gv)
        dz_ref[...] = dz
        dzb_ref[...] = dz.astype(BF16)

    row = pl.BlockSpec((tr, d), lambda i: (i, 0))
    vec = pl.BlockSpec((1, d), lambda i: (0, 0))
    return pl.pallas_call(
        body, name="ln2_loss", grid=(t // tr,),
        in_specs=[row, vec, vec, row, row, vec, vec],
        out_specs=[row, row, vec, vec, pl.BlockSpec((8, LANE), lambda i: (0, 0))],
        out_shape=[jax.ShapeDtypeStruct((t, d), F32), jax.ShapeDtypeStruct((t, d), BF16),
                   jax.ShapeDtypeStruct((1, d), F32), jax.ShapeDtypeStruct((1, d), F32),
                   jax.ShapeDtypeStruct((8, LANE), F32)],
        compiler_params=_params("arbitrary"),
    )(xhat1, g1, b1, ff, target, g2, b2)


def _ln1_bwd(dz2, dffn, xhat1, rstd1, g1, tr=256):
    t, d = dz2.shape

    def body(a_ref, f_ref, xh_ref, rs_ref, g_ref, dz_ref, dzb_ref, dg_ref, db_ref):
        i = pl.program_id(0)
        dy = ALPHA * a_ref[...] + f_ref[...]
        xhat = xh_ref[...]
        _accum(dg_ref, i, jnp.sum(dy * xhat, axis=0, keepdims=True))
        _accum(db_ref, i, jnp.sum(dy, axis=0, keepdims=True))
        dz = _ln_bwd(dy, xhat, rs_ref[...], g_ref[...])
        dz_ref[...] = dz
        dzb_ref[...] = dz.astype(BF16)

    row = pl.BlockSpec((tr, d), lambda i: (i, 0))
    vec = pl.BlockSpec((1, d), lambda i: (0, 0))
    return pl.pallas_call(
        body, name="ln1_bwd", grid=(t // tr,),
        in_specs=[row, row, row, pl.BlockSpec((tr, 1), lambda i: (i, 0)), vec],
        out_specs=[row, row, vec, vec],
        out_shape=[jax.ShapeDtypeStruct((t, d), F32), jax.ShapeDtypeStruct((t, d), BF16),
                   jax.ShapeDtypeStruct((1, d), F32), jax.ShapeDtypeStruct((1, d), F32)],
        compiler_params=_params("arbitrary"),
    )(dz2, dffn, xhat1, rstd1, g1)


def _ffn_act_fwd(hid0, w_fc, b_fc, t, dff):
    nb = dff // LANE

    def body(hv_ref, hg_ref, wv_ref, wg_ref, bv_ref, bg_ref, a_ref):
        val = _conv(hv_ref[...], wv_ref[...]) + bv_ref[...]
        gate = _conv(hg_ref[...], wg_ref[...]) + bg_ref[...]
        a_ref[...] = (gate * _sigmoid(gate) * val).astype(BF16)

    col = lambda off: pl.BlockSpec((t, LANE), lambda j: (0, j + off))
    w3 = lambda off: pl.BlockSpec((3, LANE), lambda j: (0, j + off))
    w1 = lambda off: pl.BlockSpec((1, LANE), lambda j: (0, j + off))
    return pl.pallas_call(
        body, name="ffn_act_fwd", grid=(nb,),
        in_specs=[col(0), col(nb), w3(0), w3(nb), w1(0), w1(nb)],
        out_specs=col(0),
        out_shape=jax.ShapeDtypeStruct((t, dff), BF16),
        compiler_params=_params("parallel"),
    )(hid0, hid0, w_fc, w_fc, b_fc, b_fc)


def _ffn_act_bwd(da, hid0, w_fc, b_fc, t, dff):
    nb = dff // LANE

    def body(da_ref, hv_ref, hg_ref, wv_ref, wg_ref, bv_ref, bg_ref,
             dhv_ref, dhg_ref, dwv_ref, dwg_ref, dbv_ref, dbg_ref):
        hv, hg, wv, wg = hv_ref[...], hg_ref[...], wv_ref[...], wg_ref[...]
        rv, rg = _rolled(hv), _rolled(hg)
        val = _conv(hv, wv, rv) + bv_ref[...]
        gate = _conv(hg, wg, rg) + bg_ref[...]
        sig = _sigmoid(gate)
        d = da_ref[...]
        dsig = d * sig
        dval = dsig * gate
        dgate = dsig * val * (1.0 + gate * (1.0 - sig))
        dhv_ref[...] = _conv_t(dval, wv).astype(BF16)
        dhg_ref[...] = _conv_t(dgate, wg).astype(BF16)
        dwv_ref[...] = _conv_dw(dval, hv, rv)
        dwg_ref[...] = _conv_dw(dgate, hg, rg)
        dbv_ref[...] = jnp.sum(dval, axis=0, keepdims=True)
        dbg_ref[...] = jnp.sum(dgate, axis=0, keepdims=True)

    col = lambda off: pl.BlockSpec((t, LANE), lambda j: (0, j + off))
    w3 = lambda off: pl.BlockSpec((3, LANE), lambda j: (0, j + off))
    w1 = lambda off: pl.BlockSpec((1, LANE), lambda j: (0, j + off))
    s3 = jax.ShapeDtypeStruct((3, dff), F32)
    s1 = jax.ShapeDtypeStruct((1, dff), F32)
    return pl.pallas_call(
        body, name="ffn_act_bwd", grid=(nb,),
        in_specs=[col(0), col(0), col(nb), w3(0), w3(nb), w1(0), w1(nb)],
        out_specs=[col(0), col(0), w3(0), w3(0), w1(0), w1(0)],
        out_shape=[jax.ShapeDtypeStruct((t, dff), BF16)] * 2 + [s3, s3, s1, s1],
        compiler_params=_params("parallel"),
    )(da, hid0, hid0, w_fc, w_fc, b_fc, b_fc)


class _Ready:
    def __init__(self, **weights):
        self.weights = weights

    def begin(self, after):
        return None

    def forward(self, name, after):
        return None

    def get(self, name, after):
        return self.weights[name]


class _Kept:
    def __init__(self):
        self.grads = {}

    def start(self, name, grad):
        self.grads[name] = grad
        return None

    def relay(self, name, after):
        return None

    def meanwhile(self, small, loss, after):
        return None


def _behind(a, token):
    return a if token is None else a + token[0:1, 0:1].reshape((1,) * a.ndim)


def _local_step(x, target, w_in, b_gates, w_sc, gain, w_out, ln1_g, ln1_b, w_up, w_fc, b_fc, w_down, ln2_g, ln2_b,
                gx=None, wx=None):
    t, d = x.shape
    wc = d // 2
    dh = (d - wc) // NH
    wm = NH * dh
    dff = w_fc.shape[1] // 2
    if wx is None:
        wx = _Ready(w_out=w_out, w_up=w_up, w_down=w_down)
    ninp = w_in.shape[0]
    nin = 3 * wc + 4 * wm
    gate_tile = nin // LANE
    nc = t // CHUNK
    bias_tile = jnp.pad(b_gates, ((0, 0), (0, LANE - 2 * NH)))

    x_b = _behind(x, wx.begin(w_in)).astype(BF16)
    proj = _matmul(x_b, w_in, "nt", F32, "proj", tm=512, tn=2688, tk=d, after=wx.begin(w_in))
    y = _sconv_fwd(proj, w_sc, t, wc)
    gcol = _gates_prep(proj, bias_tile, t, gate_tile)
    grow = gcol[:, :8].T.reshape(8, nc, CHUNK).transpose(1, 0, 2)
    hval, cs, ns = _mlstm_fwd(proj, gcol, grow, t, wc, dh)
    y = _hnorm_fwd(hval, proj, gain, y, t, wc, dh)
    tok = wx.forward("w_out", y)
    w_out = wx.get("w_out", tok)
    mix = _matmul(y, w_out, "nn", F32, "out_proj", tm=512, tn=1024, tk=wc, a_blocked=True, after=tok)
    xhat1, rstd1, x1_b = _ln1_fwd(x, mix, _behind(ln1_g, wx.forward("w_up", mix)), ln1_b)
    w_up = wx.get("w_up", x1_b)
    wsl = w_up.shape[2]
    hid0 = _matmul(x1_b, w_up, "nn", F32, "ffn_up", tm=512, tn=wsl, tk=d, b_blocked=True)
    act = _ffn_act_fwd(hid0, w_fc, _behind(b_fc, wx.forward("w_down", hid0)), t, dff)
    w_down = wx.get("w_down", act)
    ff = _matmul(act, w_down, "nn", F32, "ffn_down", tm=1024, tn=512, tk=dff)
    dz2, dz2_b, d_ln2_g, d_ln2_b, loss = _ln2_loss(xhat1, ln1_g, ln1_b, ff, target, ln2_g, ln2_b)

    if gx is None:
        gx = _Kept()
    d_w_down = _matmul(act, dz2_b, "tn", BF16, "ffn_down_dw", tm=512, tn=1024, tk=t)
    d_act = _matmul(dz2_b, w_down, "nt", F32, "ffn_down_dx", tm=1024, tn=512, tk=d, after=gx.start("w_down", d_w_down))
    *d_hid0, dwv, dwg, dbv, dbg = _ffn_act_bwd(d_act, hid0, w_fc, _behind(b_fc, gx.relay("w_down", d_act)), t, dff)
    d_w_fc = jnp.concatenate([dwv, dwg], axis=1)
    d_b_fc = jnp.concatenate([dbv, dbg], axis=1)
    d_hid0 = tuple(d_hid0[:2])
    d_w_up = _matmul(x1_b, d_hid0, "tn", BF16, "ffn_up_dw", tm=512, tn=wsl, tk=t, o_width=wsl)
    d_x1_ffn = _matmul(d_hid0, w_up, "nt", F32, "ffn_up_dx", tm=1024, tn=1024, tk=wsl, b_blocked=True,
                       after=gx.start("w_up", d_w_up))
    dz1, dz1_b, d_ln1_g, d_ln1_b = _ln1_bwd(dz2, d_x1_ffn, xhat1, rstd1, _behind(ln1_g, gx.relay("w_up", d_x1_ffn)))

    d_w_out = _matmul(y, dz1_b, "tn", BF16, "out_proj_dw", tm=512, tn=1024, tk=t, a_blocked=True)
    dy = _matmul(dz1_b, w_out, "nt", F32, "out_proj_dx", tm=512, tn=1024, tk=d, after=gx.start("w_out", d_w_out))
    dcb, dcc, dch, d_w_sc = _sconv_bwd(dy, proj, _behind(w_sc, gx.relay("w_out", dy)), t, wc)
    d_o, d_hval, d_gain = _hnorm_bwd(dy, hval, proj, gain, t, wc, dh)
    dq, dk, dv, dgate = _mlstm_bwd(proj, gcol, grow, hval, d_hval, cs, ns, t, wc, dh)
    dgt, d_b_gates = _gates_bwd(dgate, proj, bias_tile, t, gate_tile)
    pad = jnp.zeros((t, ninp - nin - LANE), BF16)
    d_proj = jnp.concatenate([dcb, dcc, dch, dq, dk, dv, d_o, dgt, pad], axis=1)
    d_w_in = _matmul(d_proj, x_b, "tn", BF16, "proj_dw", tm=IN_SLAB, tn=1024, tk=t)
    small = dict(b_gates=d_b_gates[:, :2 * NH], w_sc_conv=d_w_sc, mh_gain=d_gain, ln1_g=d_ln1_g, ln1_b=d_ln1_b,
                 w_ffn_conv=d_w_fc, b_ffn_conv=d_b_fc, ln2_g=d_ln2_g, ln2_b=d_ln2_b)
    token = gx.start("w_in", d_w_in.reshape(ninp // IN_SLAB, IN_SLAB, d))
    token = gx.relay("w_in", gx.meanwhile(small, loss, token))
    grad_x = _matmul(d_proj, w_in, "nn", F32, "proj_dx", tm=512, tn=512, tk=ninp, add=dz1, add_scale=ALPHA, after=token)
    return loss, grad_x, small, gx


HBM = pl.BlockSpec(memory_space=pltpu.HBM)


def _place():
    return lax.axis_index("x"), lax.axis_index("y"), lax.axis_index("c")


def _index(p):
    return 4 * p[0] + 2 * p[1] + p[2]


def _all_gather(arrs, name):
    n = len(arrs)

    def body(*refs):
        ins, outs = refs[:n], refs[n:2 * n]
        send_sems, recv_sems, local_sems = refs[2 * n:]
        x, y, c = _place()
        me, sibling = (x, y, c), (x, y, 1 - c)
        chips = [(1 - x, y), (x, 1 - y), (1 - x, 1 - y)]

        def copy(a, k, block, to, own=False):
            dst = outs[a].at[_index(block)]
            return pltpu.make_async_remote_copy(
                src_ref=ins[a] if own else dst, dst_ref=dst,
                send_sem=send_sems.at[k * n + a], recv_sem=recv_sems.at[k * n + a],
                device_id=to, device_id_type=MESH)

        mine = [pltpu.make_async_copy(ins[a], outs[a].at[_index(me)], local_sems.at[a]) for a in range(n)]
        for cp in mine:
            cp.start()
        first = []
        for a in range(n):
            first.append(copy(a, 0, me, sibling, own=True))
            first += [copy(a, 1 + j, me, (*chip, c), own=True) for j, chip in enumerate(chips)]
        for cp in first:
            cp.start()
        passed = []
        for j, chip in enumerate(chips):
            for a in range(n):
                copy(a, 1 + j, (*chip, c), me).wait_recv()
                cp = copy(a, 4 + j, (*chip, c), sibling)
                cp.start()
                passed.append(cp)
        for a in range(n):
            copy(a, 0, sibling, me).wait_recv()
            for j, chip in enumerate(chips):
                copy(a, 4 + j, (*chip, 1 - c), me).wait_recv()
        for cp in first + passed:
            cp.wait_send()
        for cp in mine:
            cp.wait()

    return pl.pallas_call(
        body, name=name, in_specs=[HBM] * n, out_specs=[HBM] * n,
        out_shape=[jax.ShapeDtypeStruct((N_DEV,) + a.shape, a.dtype) for a in arrs],
        scratch_shapes=[pltpu.SemaphoreType.DMA((7 * n,)), pltpu.SemaphoreType.DMA((7 * n,)),
                        pltpu.SemaphoreType.DMA((n,))],
    )(*arrs)


SEM = pl.BlockSpec(memory_space=pltpu.SEMAPHORE)
EFFECT = pltpu.SideEffectType.DATAFLOW_SIDE_EFFECTING


def _chips(x, y):
    return [(1 - x, y), (x, 1 - y), (1 - x, 1 - y)]


N_CHIP = N_DEV // 2


def _pair_route(x, y, c):
    return [((x, y, 1 - c), 2 * q + (1 - c), q, q) for q in range(N_CHIP)]


def _chip_route(x, y, c):
    mine = 2 * x + y
    return [((*chip, c), 2 * chip[0] + chip[1], mine, 2 * chip[0] + chip[1]) for chip in _chips(x, y)]


def _exchange_pieces(g_ref, land_ref, width, tail):
    if not tail:
        return [(lambda i: g_ref.at[i], lambda s: land_ref.at[s])]
    return [(lambda i: g_ref.at[i], lambda s: land_ref.at[s, pl.ds(0, width), :]),
            (lambda i: g_ref.at[i + 1, pl.ds(0, IN_TAIL), :], lambda s: land_ref.at[s, pl.ds(width, IN_TAIL), :])]


def _exchange_start(grad, route, tail, name):
    width = grad.shape[1]
    n_p = 2 if tail else 1
    n_c = len(route(0, 0, 0))
    land_shape = (N_CHIP, width + (IN_TAIL if tail else 0), grad.shape[2])

    def body(g_ref, land_ref, send_sems, recv_sems, g_thru, land_thru, token):
        for j, (peer, slab, slot, _) in enumerate(route(*_place())):
            for p, (src, dst) in enumerate(_exchange_pieces(g_ref, land_ref, width, tail)):
                pltpu.make_async_remote_copy(src_ref=src(slab), dst_ref=dst(slot), send_sem=send_sems.at[j * n_p + p],
                                             recv_sem=recv_sems.at[j * n_p + p], device_id=peer,
                                             device_id_type=MESH).start()
        token[...] = jnp.zeros_like(token)

    return pl.pallas_call(
        body, name=name,
        out_shape=(pltpu.SemaphoreType.DMA((n_c * n_p,)), pltpu.SemaphoreType.DMA((n_c * n_p,)),
                   pltpu.HBM(grad.shape, grad.dtype), pltpu.HBM(land_shape, grad.dtype),
                   jax.ShapeDtypeStruct((8, LANE), F32)),
        in_specs=(HBM, HBM), out_specs=(SEM, SEM, HBM, HBM, pl.BlockSpec(memory_space=pltpu.VMEM)),
        input_output_aliases={0: 2, 1: 3},
        compiler_params=pltpu.CompilerParams(has_side_effects=EFFECT),
    )(pltpu.with_memory_space_constraint(grad, pltpu.HBM),
      pltpu.with_memory_space_constraint(lax.empty(land_shape, grad.dtype), pltpu.HBM))


def _exchange_wait(send_sems, recv_sems, g_thru, land_thru, after, route, tail, name):
    width = g_thru.shape[1]
    n_p = 2 if tail else 1

    def body(g_ref, land_ref, send_sems, recv_sems, after_ref, g_dead, got_ref):
        for j, (peer, slab, _, slot) in enumerate(route(*_place())):
            for p, (src, dst) in enumerate(_exchange_pieces(g_ref, land_ref, width, tail)):
                cp = pltpu.make_async_remote_copy(src_ref=src(slab), dst_ref=dst(slot),
                                                  send_sem=send_sems.at[j * n_p + p], recv_sem=recv_sems.at[j * n_p + p],
                                                  device_id=peer, device_id_type=MESH)
                cp.wait_send()
                cp.wait_recv()

    return pl.pallas_call(
        body, name=name,
        out_shape=(pltpu.HBM(g_thru.shape, g_thru.dtype), pltpu.HBM(land_thru.shape, land_thru.dtype)),
        in_specs=(HBM, HBM, SEM, SEM, pl.BlockSpec(memory_space=pl.ANY)), out_specs=(HBM, HBM),
        input_output_aliases={0: 0, 1: 1},
        compiler_params=pltpu.CompilerParams(has_side_effects=EFFECT),
    )(g_thru, land_thru, send_sems, recv_sems, after)


def _pair_add(grad, pair, core, tail, name):
    rows, cols = grad.shape[1], grad.shape[2]
    total = pair.shape[1]

    def body(core_ref, *refs):
        if tail:
            g_ref, t_ref, p_ref, o_ref = refs
            o_ref[0:rows, :] = (g_ref[...].astype(F32) + p_ref[0:rows, :].astype(F32)).astype(BF16)
            o_ref[rows:total, :] = (t_ref[...].astype(F32) + p_ref[rows:total, :].astype(F32)).astype(BF16)
        else:
            g_ref, p_ref, o_ref = refs
            o_ref[...] = (g_ref[...].astype(F32) + p_ref[...].astype(F32)).astype(BF16)

    if tail:
        tc = _fit(cols, 512)
        grid = (N_CHIP, cols // tc)
        slab = pl.BlockSpec((None, total, tc), lambda q, i, core_ref: (q, 0, i))
        in_specs = [pl.BlockSpec((None, rows, tc), lambda q, i, core_ref: (2 * q + core_ref[0], 0, i)),
                    pl.BlockSpec((None, IN_TAIL, tc), lambda q, i, core_ref: (2 * q + core_ref[0] + 1, 0, i))]
    else:
        tr = _rows(rows, 1024)
        grid = (N_CHIP, rows // tr)
        slab = pl.BlockSpec((None, tr, cols), lambda q, i, core_ref: (q, i, 0))
        in_specs = [pl.BlockSpec((None, tr, cols), lambda q, i, core_ref: (2 * q + core_ref[0], i, 0))]
    return pl.pallas_call(
        body, name=name,
        grid_spec=pltpu.PrefetchScalarGridSpec(num_scalar_prefetch=1, grid=grid,
                                               in_specs=in_specs + [slab], out_specs=slab),
        out_shape=jax.ShapeDtypeStruct(pair.shape, BF16),
        compiler_params=_params("parallel", "parallel"),
    )(core, *([grad, grad] if tail else [grad]), pair)


def _gather_start(blocks, after, name, spare=()):
    n = len(blocks)
    lands = [(N_DEV + (a in spare),) + b.shape for a, b in enumerate(blocks)]

    def body(*refs):
        b_refs, land_refs = refs[:n], refs[n:2 * n]
        send_sems, recv_sems = refs[2 * n + 1:3 * n + 1], refs[3 * n + 1:4 * n + 1]
        token = refs[-1]
        x, y, c = _place()
        me = _index((x, y, c))
        for a in range(n):
            for k, to in enumerate([(x, y, 1 - c)] + [(*chip, c) for chip in _chips(x, y)]):
                pltpu.make_async_remote_copy(src_ref=b_refs[a], dst_ref=land_refs[a].at[me], send_sem=send_sems[a].at[k],
                                             recv_sem=recv_sems[a].at[k], device_id=to, device_id_type=MESH).start()
        token[...] = jnp.zeros_like(token)

    sems = [pltpu.SemaphoreType.DMA((4,))] * n
    out = pl.pallas_call(
        body, name=name,
        out_shape=(*sems, *sems, *[pltpu.HBM(b.shape, b.dtype) for b in blocks],
                   *[pltpu.HBM(s, b.dtype) for s, b in zip(lands, blocks)], jax.ShapeDtypeStruct((8, LANE), F32)),
        in_specs=(*[HBM] * (2 * n), pl.BlockSpec(memory_space=pl.ANY)),
        out_specs=(*[SEM] * (2 * n), *[HBM] * (2 * n), pl.BlockSpec(memory_space=pltpu.VMEM)),
        input_output_aliases={i: 2 * n + i for i in range(2 * n)},
        compiler_params=pltpu.CompilerParams(has_side_effects=EFFECT),
    )(*[pltpu.with_memory_space_constraint(b, pltpu.HBM) for b in blocks],
      *[pltpu.with_memory_space_constraint(lax.empty(s, b.dtype), pltpu.HBM) for s, b in zip(lands, blocks)], after)
    return [(out[a], out[n + a], out[2 * n + a], out[3 * n + a]) for a in range(n)], out[-1]


def _gather_forward(send_sems, recv_sems, b_thru, land_thru, after, name):
    def body(b_ref, land_ref, send_sems, recv_sems, after_ref, b_dead, land_out, send2, recv2, token):
        x, y, c = _place()
        sibling = (x, y, 1 - c)
        for k, frm in enumerate([sibling] + [(*chip, c) for chip in _chips(x, y)]):
            cp = pltpu.make_async_remote_copy(src_ref=b_ref, dst_ref=land_ref.at[_index(frm)], send_sem=send_sems.at[k],
                                              recv_sem=recv_sems.at[k], device_id=frm, device_id_type=MESH)
            cp.wait_send()
            cp.wait_recv()
        for j, chip in enumerate(_chips(x, y)):
            slot = land_ref.at[_index((*chip, c))]
            pltpu.make_async_remote_copy(src_ref=slot, dst_ref=slot, send_sem=send2.at[j], recv_sem=recv2.at[j],
                                         device_id=sibling, device_id_type=MESH).start()
        token[...] = jnp.zeros_like(token)

    return pl.pallas_call(
        body, name=name,
        out_shape=(pltpu.HBM(b_thru.shape, b_thru.dtype), pltpu.HBM(land_thru.shape, land_thru.dtype),
                   pltpu.SemaphoreType.DMA((3,)), pltpu.SemaphoreType.DMA((3,)), jax.ShapeDtypeStruct((8, LANE), F32)),
        in_specs=(HBM, HBM, SEM, SEM, pl.BlockSpec(memory_space=pl.ANY)),
        out_specs=(HBM, HBM, SEM, SEM, pl.BlockSpec(memory_space=pltpu.VMEM)),
        input_output_aliases={0: 0, 1: 1},
        compiler_params=pltpu.CompilerParams(has_side_effects=EFFECT),
    )(b_thru, land_thru, send_sems, recv_sems, after)


def _gather_finish(land_thru, send2, recv2, after, name):
    def body(land_ref, send2, recv2, after_ref, land_out):
        x, y, c = _place()
        for j, chip in enumerate(_chips(x, y)):
            cp = pltpu.make_async_remote_copy(src_ref=land_ref.at[_index((*chip, c))],
                                              dst_ref=land_ref.at[_index((*chip, 1 - c))], send_sem=send2.at[j],
                                              recv_sem=recv2.at[j], device_id=(x, y, 1 - c), device_id_type=MESH)
            cp.wait_send()
            cp.wait_recv()

    return pl.pallas_call(
        body, name=name, out_shape=pltpu.HBM(land_thru.shape, land_thru.dtype),
        in_specs=(HBM, SEM, SEM, pl.BlockSpec(memory_space=pl.ANY)), out_specs=HBM,
        input_output_aliases={0: 0},
        compiler_params=pltpu.CompilerParams(has_side_effects=EFFECT),
    )(land_thru, send2, recv2, after)


class _Gathering:
    def __init__(self, first, later, me):
        started, token = _gather_start(list(first.values()), next(iter(first.values())), "gather1_first", spare=(0,))
        cast = [_behind(a, token).astype(BF16) for a in later.values()]
        started_later, self.token = _gather_start(cast, token, "gather1_later")
        self.me, self.state = me, dict(zip([*first, *later], started + started_later))

    def begin(self, after):
        return self.token

    def forward(self, name, after):
        *self.state[name], token = _gather_forward(*self.state[name], after, "gather2_" + name)
        return token

    def get(self, name, after):
        block, land, send2, recv2 = self.state[name]
        land = _gather_finish(land, send2, recv2, after, "gather3_" + name)
        land = lax.dynamic_update_index_in_dim(land, block[None], self.me, 0)
        return land if name not in ("w_out", "w_down") else land.reshape(-1, land.shape[2])


class _Reducing:
    def __init__(self, core, chip, gather_small):
        self.core, self.chip, self.state, self.token, self.gather_small = core, chip, {}, None, gather_small

    def meanwhile(self, small, loss, after):
        self.small_sum = self.gather_small(small, loss, after)
        return self.small_sum

    def start(self, name, grad):
        g = grad if grad.ndim == 3 else grad.reshape(N_DEV, grad.shape[0] // N_DEV, grad.shape[1])
        *self.state[name], token = _exchange_start(g, _pair_route, name == "w_in", "pair_send_" + name)
        return token

    def relay(self, name, after):
        tail = name == "w_in"
        grad, pair = _exchange_wait(*self.state[name], after, _pair_route, tail, "pair_recv_" + name)
        total = _pair_add(grad, pair, self.core, tail, "pair_add_" + name)
        *self.state[name], self.token = _exchange_start(total, _chip_route, False, "chip_send_" + name)
        return self.token

    def finish(self, name, after):
        total, land = _exchange_wait(*self.state[name], after, _chip_route, False, "chip_recv_" + name)
        own = lax.dynamic_index_in_dim(total, self.chip, 0, keepdims=True)
        return lax.dynamic_update_index_in_dim(land, own, self.chip, 0)


def _carry_w_in(main, tail):
    slabs, _, d = main.shape
    tc = _fit(d, 512)
    assert slabs == N_DEV + 1 and tail.shape[:2] == (N_DEV, IN_TAIL), (main.shape, tail.shape)
    top = lambda off: pl.BlockSpec((None, IN_TAIL, tc), lambda s, j: (s + off, 0, j))

    def carry(m_ref, t_ref, o_ref):
        o_ref[...] = m_ref[...] + t_ref[...]

    main = pl.pallas_call(
        carry, name="carry_w_in", grid=(N_DEV - 1, d // tc), in_specs=[top(1), top(0)], out_specs=top(1),
        out_shape=jax.ShapeDtypeStruct(main.shape, main.dtype), input_output_aliases={0: 0},
        compiler_params=_params("parallel", "parallel"),
    )(main, tail)

    def last(m_ref, t_ref, o_ref):
        o_ref[...] = jnp.zeros_like(o_ref)
        o_ref[0:IN_TAIL, :] = t_ref[...]

    return pl.pallas_call(
        last, name="last_slab_w_in", grid=(d // tc,),
        in_specs=[pl.BlockSpec(memory_space=pl.ANY), pl.BlockSpec((None, IN_TAIL, tc), lambda j: (N_DEV - 1, 0, j))],
        out_specs=pl.BlockSpec((None, IN_SLAB, tc), lambda j: (N_DEV, 0, j)),
        out_shape=jax.ShapeDtypeStruct(main.shape, main.dtype), input_output_aliases={0: 0},
        compiler_params=_params("parallel"),
    )(main, tail)


def _rows(n, want):
    t = min(n, want)
    t -= t % 16
    while n % t:
        t -= 16
    return t


def _adam_math(w, g, m, v):
    m2 = ADAM_B1 * m + (1.0 - ADAM_B1) * g
    v2 = ADAM_B2 * v + (1.0 - ADAM_B2) * (g * g)
    m_hat = m2 / (1.0 - ADAM_B1 ** ADAM_STEP)
    v_hat = v2 / (1.0 - ADAM_B2 ** ADAM_STEP)
    return -ADAM_LR * (m_hat / (jnp.sqrt(v_hat) + ADAM_EPS) + ADAM_WD * w), m2, v2


def _slot_sum(r_ref):
    acc = r_ref[0].astype(F32)
    for i in range(1, r_ref.shape[0]):
        acc = acc + r_ref[i].astype(F32)
    return acc


def _shift_w_in(w):
    ws, d = w.shape
    tc = _fit(d, 256)

    def body(w_ref, main_ref, tail_ref, tall):
        tall[...] = jnp.zeros_like(tall)
        tall[0:ws, :] = w_ref[...]
        moved = pltpu.roll(tall[...], _index(_place()), 0).astype(BF16)
        main_ref[...] = moved[0:IN_SLAB]
        tail_ref[...] = moved[IN_SLAB:]

    return pl.pallas_call(
        body, name="shift_w_in", grid=(d // tc,),
        in_specs=[pl.BlockSpec((ws, tc), lambda j: (0, j))],
        out_specs=[pl.BlockSpec((IN_SLAB, tc), lambda j: (0, j)), pl.BlockSpec((IN_TAIL, tc), lambda j: (0, j))],
        out_shape=[jax.ShapeDtypeStruct((IN_SLAB, d), BF16), jax.ShapeDtypeStruct((IN_TAIL, d), BF16)],
        scratch_shapes=[pltpu.VMEM((IN_SLAB + IN_TAIL, tc), F32)], compiler_params=_params("parallel"),
    )(w)


def _sum_adamw_shifted(r, w, m, v, name):
    _, ph, d = r.shape
    ws = w.shape[0]
    tc = _fit(d, 256)

    def body(r_ref, w_ref, m_ref, v_ref, g_ref, d_ref, m2_ref, v2_ref, tall):
        tall[...] = pltpu.roll(_slot_sum(r_ref), lax.rem(ph - _index(_place()), ph), 0)
        g = tall[0:ws, :]
        g_ref[...] = g
        d_ref[...], m2_ref[...], v2_ref[...] = _adam_math(w_ref[...], g, m_ref[...], v_ref[...])

    blk = pl.BlockSpec((ws, tc), lambda j: (0, j))
    out = jax.ShapeDtypeStruct(w.shape, F32)
    return pl.pallas_call(
        body, name=name, grid=(d // tc,),
        in_specs=[pl.BlockSpec((r.shape[0], ph, tc), lambda j: (0, 0, j)), blk, blk, blk],
        out_specs=[blk] * 4, out_shape=[out] * 4,
        scratch_shapes=[pltpu.VMEM((ph, tc), F32)], compiler_params=_params("parallel"),
    )(r, w, m, v)


def _sum_slots(r, name, tr=128):
    _, rows, cols = r.shape
    tr = _rows(rows, tr)

    def body(r_ref, g_ref):
        g_ref[...] = _slot_sum(r_ref)

    return pl.pallas_call(
        body, name=name, grid=(rows // tr,),
        in_specs=[pl.BlockSpec((r.shape[0], tr, cols), lambda i: (0, i, 0))],
        out_specs=pl.BlockSpec((tr, cols), lambda i: (i, 0)),
        out_shape=jax.ShapeDtypeStruct((rows, cols), F32),
        compiler_params=_params("parallel"),
    )(r)


def _adamw(w, g, m, v, name, tr=256):
    rows, cols = w.shape
    tr = _rows(rows, tr)

    def body(w_ref, g_ref, m_ref, v_ref, d_ref, m2_ref, v2_ref):
        d_ref[...], m2_ref[...], v2_ref[...] = _adam_math(w_ref[...], g_ref[...], m_ref[...], v_ref[...])

    blk = pl.BlockSpec((tr, cols), lambda i: (i, 0))
    out = jax.ShapeDtypeStruct((rows, cols), F32)
    return pl.pallas_call(
        body, name=name, grid=(rows // tr,), in_specs=[blk] * 4, out_specs=[blk] * 3, out_shape=[out] * 3,
        compiler_params=_params("parallel"),
    )(w, g, m, v)


def _sum_adamw(r, w, m, v, name, tr=128):
    rows, cols = w.shape
    tr = _rows(rows, tr)

    def body(r_ref, w_ref, m_ref, v_ref, g_ref, d_ref, m2_ref, v2_ref):
        g = _slot_sum(r_ref)
        g_ref[...] = g
        d_ref[...], m2_ref[...], v2_ref[...] = _adam_math(w_ref[...], g, m_ref[...], v_ref[...])

    blk = pl.BlockSpec((tr, cols), lambda i: (i, 0))
    out = jax.ShapeDtypeStruct((rows, cols), F32)
    return pl.pallas_call(
        body, name=name, grid=(rows // tr,),
        in_specs=[pl.BlockSpec((r.shape[0], tr, cols), lambda i: (0, i, 0)), blk, blk, blk],
        out_specs=[blk] * 4, out_shape=[out] * 4,
        compiler_params=_params("parallel"),
    )(r, w, m, v)


def _pack(pieces, sizes):
    flat = [jnp.pad(p.reshape(-1).astype(F32), (0, s - p.size)) for p, s in zip(pieces, sizes)]
    total = sum(sizes)
    padded = -(-total // (16 * LANE)) * (16 * LANE)
    return jnp.pad(jnp.concatenate(flat), (0, padded - total)).reshape(-1, LANE)


def _unpack(packed, shapes, sizes):
    flat = packed.reshape(-1)
    out, off = [], 0
    for shp, s in zip(shapes, sizes):
        n = 1
        for k in shp:
            n *= k
        out.append(flat[off:off + n].reshape(shp))
        off += s
    return out


def _lanes(n):
    return -(-n // LANE) * LANE


WEIGHTS = ("w_in", "b_gates", "w_sc_conv", "mh_gain", "w_out", "ln1_g", "ln1_b", "w_up", "w_ffn_conv", "b_ffn_conv",
           "w_down", "ln2_g", "ln2_b")
BIG = ("w_in", "w_out", "w_up", "w_down")
SMALL = tuple(n for n in WEIGHTS if n not in BIG)


def kernel(x, w_in, b_gates, w_sc_conv, mh_gain, w_out, ln1_g, ln1_b, w_up, w_ffn_conv, b_ffn_conv, w_down, ln2_g, ln2_b, loss_target, m_w_in, m_b_gates, m_w_sc_conv, m_mh_gain, m_w_out, m_ln1_g, m_ln1_b, m_w_up, m_w_ffn_conv, m_b_ffn_conv, m_w_down, m_ln2_g, m_ln2_b, v_w_in, v_b_gates, v_w_sc_conv, v_mh_gain, v_w_out, v_ln1_g, v_ln1_b, v_w_up, v_w_ffn_conv, v_b_ffn_conv, v_w_down, v_ln2_g, v_ln2_b):
    w = dict(zip(WEIGHTS, (w_in, b_gates, w_sc_conv, mh_gain, w_out, ln1_g, ln1_b, w_up, w_ffn_conv, b_ffn_conv,
                           w_down, ln2_g, ln2_b)))
    m = dict(zip(WEIGHTS, (m_w_in, m_b_gates, m_w_sc_conv, m_mh_gain, m_w_out, m_ln1_g, m_ln1_b, m_w_up,
                           m_w_ffn_conv, m_b_ffn_conv, m_w_down, m_ln2_g, m_ln2_b)))
    v = dict(zip(WEIGHTS, (v_w_in, v_b_gates, v_w_sc_conv, v_mh_gain, v_w_out, v_ln1_g, v_ln1_b, v_w_up,
                           v_w_ffn_conv, v_b_ffn_conv, v_w_down, v_ln2_g, v_ln2_b)))
    me = _index(_place())
    d = x.shape[2]
    ws_in = w_in.shape[2]
    assert ws_in == IN_SLAB + 1 and N_DEV <= LANE, w_in.shape
    ninp = (N_DEV + 1) * IN_SLAB
    ws_sc, ws_fc = w_sc_conv.shape[2], w_ffn_conv.shape[2]
    w_in_t, m_in_t, v_in_t = (jnp.transpose(a[0]) for a in (w_in, m_w_in, v_w_in))

    w_in_main, w_in_tail = _shift_w_in(w_in_t)
    taps8 = lambda a: jnp.pad(a[0], ((0, 5), (0, 0)))
    at_once = ("w_in", "w_tail", "w_sc", "w_fc")
    wx = _Gathering(dict(zip(at_once, (w_in_main, w_in_tail, taps8(w_sc_conv), taps8(w_ffn_conv)))),
                    {n: w[n][0] for n in ("w_out", "w_up", "w_down")}, me)
    token = wx.begin(None)
    for n in at_once:
        token = wx.forward(n, token)
    g_in, g_tail, g_sc, g_fc = (wx.get(n, token) for n in at_once)
    w_in_full = _carry_w_in(g_in, g_tail).reshape(ninp, d)
    w_sc_full = g_sc[:, :3].transpose(1, 0, 2).reshape(3, N_DEV * ws_sc)
    w_fc_full = g_fc[:, :3].transpose(1, 0, 2).reshape(3, N_DEV * ws_fc)

    xi, yi, ci = _place()
    names = ("loss",) + SMALL
    pieces = {}

    def gather_small(small, loss_t, after):
        pieces.update(small, loss=loss_t[0, :1])
        sizes = [_lanes(pieces[n].size) for n in names]
        (g_small,) = _all_gather([_behind(_pack([pieces[n] for n in names], sizes), after)], "gather_small")
        return _sum_slots(g_small, "sum_small", tr=g_small.shape[1])

    gx = _Reducing(jnp.reshape(ci, (1,)).astype(jnp.int32), 2 * xi + yi, gather_small)
    loss_t, grad_x, small, _ = _local_step(
        x[0], loss_target[0], w_in_full, b_gates, w_sc_full, mh_gain, None, ln1_g, ln1_b, None,
        w_fc_full, b_ffn_conv, None, ln2_g, ln2_b, gx=gx, wx=wx)

    grads, deltas, new_m, new_v = {}, {}, {}, {}
    for name in ("w_down", "w_up", "w_out"):
        grads[name], deltas[name], new_m[name], new_v[name] = _sum_adamw(
            gx.finish(name, gx.token), w[name][0], m[name][0], v[name][0], "adamw_" + name)

    summed = _unpack(gx.small_sum, [pieces[n].shape for n in names], [_lanes(pieces[n].size) for n in names])
    full = dict(zip(names, summed))
    full["w_sc_conv"] = lax.dynamic_slice(full["w_sc_conv"], (0, me * ws_sc), (3, ws_sc))
    full["w_ffn_conv"] = lax.dynamic_slice(full["w_ffn_conv"], (0, me * ws_fc), (3, ws_fc))
    for n in SMALL:
        grads[n] = full[n].reshape(w[n].shape)
    sizes = [_lanes(w[n].size) for n in SMALL]
    shapes = [w[n].shape for n in SMALL]
    packed = [_pack([t[n] for n in SMALL], sizes) for t in (w, grads, m, v)]
    small_out = _adamw(*packed, "adamw_small")
    for res, t in zip(small_out, (deltas, new_m, new_v)):
        t.update(zip(SMALL, _unpack(res, shapes, sizes)))

    done = sum(t[0:1, 0:1] for t in (grad_x, deltas["w_down"], deltas["w_up"], deltas["w_out"], small_out[0]))
    grads["w_in"], deltas["w_in"], new_m["w_in"], new_v["w_in"] = (
        jnp.transpose(a)[None] for a in _sum_adamw_shifted(gx.finish("w_in", done), w_in_t, m_in_t, v_in_t, "adamw_w_in"))

    big = lambda t: {n: (t[n].reshape(w[n].shape) if n in BIG else t[n]) for n in WEIGHTS}
    grads, deltas, new_m, new_v = big(grads), big(deltas), big(new_m), big(new_v)
    return (full["loss"].reshape(()), grad_x[None], *[grads[n] for n in WEIGHTS], *[deltas[n] for n in WEIGHTS],
            *[new_m[n] for n in WEIGHTS], *[new_v[n] for n in WEIGHTS])
```

```python
import functools

import jax
import jax.numpy as jnp
from jax import lax
from jax.experimental import pallas as pl
from jax.experimental.pallas import tpu as pltpu

F32 = jnp.float32
BF16 = jnp.bfloat16
MESH = pl.DeviceIdType.MESH

N_DEV = 8
NH = 4
CHUNK = 64
LN_EPS = 1e-5
HN_EPS = 1e-6
ALPHA = 2.0 ** 0.25
LANE = 128
IN_SLAB = 7 * LANE
IN_TAIL = 16
VMEM_LIMIT = 56 * 1024 * 1024
ADAM_LR, ADAM_B1, ADAM_B2, ADAM_EPS, ADAM_WD, ADAM_STEP = 0.001, 0.9, 0.999, 1e-08, 0.01, 10

_NN = (((1,), (0,)), ((), ()))
_NT = (((1,), (1,)), ((), ()))
_TN = (((0,), (0,)), ((), ()))


def _dot(a, b, dn=_NN):
    return lax.dot_general(a, b, dn, preferred_element_type=F32)


def _params(*sem):
    return pltpu.CompilerParams(dimension_semantics=sem if sem else None, vmem_limit_bytes=VMEM_LIMIT)


def _iota(shape, axis):
    return lax.broadcasted_iota(jnp.int32, shape, axis)


def _fit(n, want):
    if n <= want:
        return n
    t = want - want % LANE
    while n % t:
        t -= LANE
    return t


def _matmul(a, b, mode, out_dtype, name, tm=1024, tn=512, tk=1024, add=None, add_scale=1.0,
            a_blocked=False, b_blocked=False, o_width=None, after=None):
    a_parts = a if isinstance(a, tuple) else None
    b_parts = b if isinstance(b, tuple) else None
    if a_parts:
        a_blocked, (a_rows, wa), na = True, a[0].shape, len(a)
        kd, m = (a_rows, na * wa) if mode == "tn" else (na * wa, a_rows)
    elif a_blocked:
        na, a_rows, wa = a.shape
        kd, m = (a_rows, na * wa) if mode == "tn" else (na * wa, a_rows)
    elif mode == "tn":
        kd, m = a.shape
    else:
        m, kd = a.shape
    if b_parts:
        b_blocked, (rows, w), nb = True, b[0].shape, len(b)
    elif b_blocked:
        nb, rows, w = b.shape
    if b_blocked:
        n = rows if mode == "nt" else nb * w
        assert (nb * w if mode == "nt" else rows) == kd, (name, kd)
    else:
        n = b.shape[0] if mode == "nt" else b.shape[1]
    tm, tn, tk = _fit(m, tm), _fit(n, tn), _fit(kd, tk)
    if a_blocked and mode == "tn":
        tm = _fit(wa, tm)
    if a_blocked and mode != "tn":
        tk = _fit(wa, tk)
    if b_blocked and mode != "nt":
        tn = _fit(w, tn)
    if b_blocked and mode == "nt":
        tk = _fit(w, tk)
    if o_width is not None:
        tn = _fit(o_width, tn)
    assert m % tm == 0 and n % tn == 0 and kd % tk == 0, (name, m, n, kd, tm, tn, tk)
    assert not (a_blocked and mode != "tn" and wa % tk) and not (b_blocked and mode == "nt" and w % tk), (name, tk)
    nk = kd // tk
    dn = {"nn": _NN, "nt": _NT, "tn": _TN}[mode]
    if a_blocked and mode == "tn":
        a_per = wa // tm
        a_spec = pl.BlockSpec((None, tk, tm), lambda i, j, k: (i // a_per, k, i % a_per))
    elif a_blocked:
        a_per = wa // tk
        a_spec = pl.BlockSpec((None, tm, tk), lambda i, j, k: (k // a_per, i, k % a_per))
    elif mode == "tn":
        a_spec = pl.BlockSpec((tk, tm), lambda i, j, k: (k, i))
    else:
        a_spec = pl.BlockSpec((tm, tk), lambda i, j, k: (i, k))
    if b_blocked and mode != "nt":
        per = w // tn
        b_spec = pl.BlockSpec((None, tk, tn), lambda i, j, k: (j // per, k, j % per))
    elif b_blocked:
        per = w // tk
        b_spec = pl.BlockSpec((None, tn, tk), lambda i, j, k: (k // per, j, k % per))
    elif mode == "nt":
        b_spec = pl.BlockSpec((tn, tk), lambda i, j, k: (j, k))
    else:
        b_spec = pl.BlockSpec((tk, tn), lambda i, j, k: (k, j))
    if o_width is None:
        o_spec = pl.BlockSpec((tm, tn), lambda i, j, k: (i, j))
        o_shape = (m, n)
    else:
        oper = o_width // tn
        o_spec = pl.BlockSpec((None, tm, tn), lambda i, j, k: (j // oper, i, j % oper))
        o_shape = (n // o_width, m, o_width)
    a_list, a_specs = [a], [a_spec]
    if a_parts:
        hold = lambda x, s: jnp.clip(x - s * a_per, 0, a_per - 1)
        a_list = list(a_parts)
        a_specs = [(pl.BlockSpec((tk, tm), lambda i, j, k, s=s: (k, hold(i, s))) if mode == "tn"
                    else pl.BlockSpec((tm, tk), lambda i, j, k, s=s: (i, hold(k, s)))) for s in range(na)]
    b_list, b_specs = [b], [b_spec]
    if b_parts:
        hold_b = lambda x, s: jnp.clip(x - s * per, 0, per - 1)
        b_list = list(b_parts)
        b_specs = [(pl.BlockSpec((tn, tk), lambda i, j, k, s=s: (j, hold_b(k, s))) if mode == "nt"
                    else pl.BlockSpec((tk, tn), lambda i, j, k, s=s: (k, hold_b(j, s)))) for s in range(nb)]
    n_a, n_b = len(a_list), len(b_list)
    has_add = add is not None
    n_in = n_a + n_b + has_add + (after is not None)
    in_place = nk > 1 and out_dtype == F32

    def body(*refs):
        add_ref = refs[n_a + n_b] if has_add else None
        o_ref = refs[n_in]
        i, j, k = pl.program_id(0), pl.program_id(1), pl.program_id(2)

        def finish(r):
            if has_add:
                r = r + add_scale * add_ref[...]
            o_ref[...] = r.astype(out_dtype)

        def step(a_ref, b_ref):
            if nk == 1:
                finish(_dot(a_ref[...], b_ref[...], dn))
                return
            acc = o_ref if in_place else refs[-1]

            @pl.when(k == 0)
            def _():
                acc[...] = _dot(a_ref[...], b_ref[...], dn)

            @pl.when(k > 0)
            def _():
                acc[...] += _dot(a_ref[...], b_ref[...], dn)

        if n_a == 1 and n_b == 1:
            step(refs[0], refs[1])
        else:
            slab_a = ((i if mode == "tn" else k) // a_per) if n_a > 1 else 0
            slab_b = ((k if mode == "nt" else j) // per) if n_b > 1 else 0
            for sa in range(n_a):
                for sb in range(n_b):
                    pl.when((slab_a == sa) & (slab_b == sb))(functools.partial(step, refs[sa], refs[n_a + sb]))
        if nk > 1 and not (in_place and not has_add):
            @pl.when(k == nk - 1)
            def _():
                finish((o_ref if in_place else refs[-1])[...])

    in_specs = a_specs + b_specs + ([pl.BlockSpec((tm, tn), lambda i, j, k: (i, j))] if has_add else [])
    args = (*a_list, *b_list) + ((add,) if has_add else ())
    if after is not None:
        in_specs.append(pl.BlockSpec(memory_space=pl.ANY))
        args += (after,)
    return pl.pallas_call(
        body, name=name, grid=(m // tm, n // tn, nk),
        in_specs=in_specs, out_specs=o_spec,
        out_shape=jax.ShapeDtypeStruct(o_shape, out_dtype),
        scratch_shapes=[pltpu.VMEM((tm, tn), F32)] if nk > 1 and not in_place else [],
        compiler_params=_params("parallel", "parallel", "arbitrary"),
    )(*args)


def _shift_down(u, s):
    return jnp.where(_iota(u.shape, 0) >= s, pltpu.roll(u, s, 0), 0.0)


def _shift_up(u, s):
    t = u.shape[0]
    return jnp.where(_iota(u.shape, 0) < t - s, pltpu.roll(u, t - s, 0), 0.0)


SLAB = 8


def _rolled(u):
    return pltpu.roll(u, 2, 0), pltpu.roll(u, 1, 0)


def _conv(u, w, rolled=None):
    u2, u1 = _rolled(u) if rolled is None else rolled
    raw = w[0:1] * u2 + w[1:2] * u1 + w[2:3] * u
    head = u[0:SLAB]
    mended = w[0:1] * _shift_down(head, 2) + w[1:2] * _shift_down(head, 1) + w[2:3] * head
    return jnp.concatenate([mended, raw[SLAB:]], axis=0)


def _conv_t(dy, w):
    t = dy.shape[0]
    raw = w[2:3] * dy + w[1:2] * pltpu.roll(dy, t - 1, 0) + w[0:1] * pltpu.roll(dy, t - 2, 0)
    tail = dy[t - SLAB:]
    mended = w[2:3] * tail + w[1:2] * _shift_up(tail, 1) + w[0:1] * _shift_up(tail, 2)
    return jnp.concatenate([raw[:t - SLAB], mended], axis=0)


def _conv_dw(dy, u, rolled=None):
    t = dy.shape[0]
    u2, u1 = _rolled(u) if rolled is None else rolled
    head, tail = dy[0:SLAB], u[t - SLAB:]
    r = _iota(head.shape, 0)
    wrap2 = jnp.sum(jnp.where(r < 2, head * pltpu.roll(tail, 2, 0), 0.0), axis=0, keepdims=True)
    wrap1 = jnp.sum(jnp.where(r < 1, head * pltpu.roll(tail, 1, 0), 0.0), axis=0, keepdims=True)
    d0 = jnp.sum(dy * u2, axis=0, keepdims=True) - wrap2
    d1 = jnp.sum(dy * u1, axis=0, keepdims=True) - wrap1
    d2 = jnp.sum(dy * u, axis=0, keepdims=True)
    r3 = _iota((3, dy.shape[1]), 0)
    return jnp.where(r3 == 0, d0, jnp.where(r3 == 1, d1, d2))


def _sigmoid(x):
    return 0.5 * jnp.tanh(0.5 * x) + 0.5


def _sconv_fwd(proj, w_sc, t, wc):
    nb = wc // LANE

    def body(cb_ref, cc_ref, ch_ref, w_ref, y_ref):
        u = cc_ref[...] * ch_ref[...]
        y_ref[...] = (cb_ref[...] * _conv(u, w_ref[...])).astype(BF16)

    col = lambda off: pl.BlockSpec((t, LANE), lambda j: (0, j + off))
    return pl.pallas_call(
        body, name="sconv_fwd", grid=(nb,),
        in_specs=[col(0), col(nb), col(2 * nb), pl.BlockSpec((3, LANE), lambda j: (0, j))],
        out_specs=pl.BlockSpec((None, t, LANE), lambda j: (0, 0, j)),
        out_shape=jax.ShapeDtypeStruct((2, t, wc), BF16),
        compiler_params=_params("parallel"),
    )(proj, proj, proj, w_sc)


def _sconv_bwd(dy, proj, w_sc, t, wc):
    nb = wc // LANE

    def body(dy_ref, cb_ref, cc_ref, ch_ref, w_ref, dcb_ref, dcc_ref, dch_ref, dw_ref):
        cc, ch, w, d = cc_ref[...], ch_ref[...], w_ref[...], dy_ref[...]
        u = cc * ch
        ru = _rolled(u)
        dcb_ref[...] = (d * _conv(u, w, ru)).astype(BF16)
        dcu = d * cb_ref[...]
        dw_ref[...] = _conv_dw(dcu, u, ru)
        du = _conv_t(dcu, w)
        dcc_ref[...] = (du * ch).astype(BF16)
        dch_ref[...] = (du * cc).astype(BF16)

    col = lambda off: pl.BlockSpec((t, LANE), lambda j: (0, j + off))
    act = jax.ShapeDtypeStruct((t, wc), BF16)
    return pl.pallas_call(
        body, name="sconv_bwd", grid=(nb,),
        in_specs=[col(0), col(0), col(nb), col(2 * nb), pl.BlockSpec((3, LANE), lambda j: (0, j))],
        out_specs=[col(0), col(0), col(0), pl.BlockSpec((3, LANE), lambda j: (0, j))],
        out_shape=[act, act, act, jax.ShapeDtypeStruct((3, wc), F32)],
        compiler_params=_params("parallel"),
    )(dy, proj, proj, proj, w_sc)


def _gates_prep(proj, bias_tile, t, gate_tile):
    def body(g_ref, b_ref, o_ref):
        g = g_ref[...] + b_ref[...]
        lane = _iota(g.shape, 1)
        is_f = (lane >= NH) & (lane < 2 * NH)
        lf = jnp.minimum(g, 0.0) - jnp.log(1.0 + jnp.exp(-jnp.abs(g)))
        c = jnp.where(is_f, lf, 0.0)
        r = _iota(g.shape, 0) % CHUNK
        s = 1
        while s < CHUNK:
            c = c + jnp.where(r >= s, pltpu.roll(c, s, 0), 0.0)
            s *= 2
        o_ref[...] = jnp.where(is_f, c, jnp.where(lane < NH, g, 0.0))

    return pl.pallas_call(
        body, name="gates_prep", grid=(1,),
        in_specs=[pl.BlockSpec((t, LANE), lambda i: (0, gate_tile)), pl.BlockSpec((1, LANE), lambda i: (0, 0))],
        out_specs=pl.BlockSpec((t, LANE), lambda i: (0, 0)),
        out_shape=jax.ShapeDtypeStruct((t, LANE), F32),
        compiler_params=_params("arbitrary"),
    )(proj, bias_tile)


def _gates_bwd(dgate, proj, bias_tile, t, gate_tile):
    def body(dg_ref, g_ref, b_ref, o_ref, s_ref):
        g = g_ref[...] + b_ref[...]
        lane = _iota(g.shape, 1)
        r = _iota(g.shape, 0) % CHUNK
        dsig = 1.0 - _sigmoid(g)
        out = jnp.zeros(g.shape, F32)
        for h in range(NH):
            d = dg_ref[h]
            c = d
            s = 1
            while s < CHUNK:
                c = c + jnp.where(r + s < CHUNK, pltpu.roll(c, t - s, 0), 0.0)
                s *= 2
            di = jnp.broadcast_to(d[:, 0:1], g.shape)
            db = jnp.broadcast_to(c[:, 1:2], g.shape)
            out = out + jnp.where(lane == h, di, 0.0) + jnp.where(lane == NH + h, db * dsig, 0.0)
        o_ref[...] = out.astype(BF16)
        s_ref[...] = jnp.sum(out, axis=0, keepdims=True)

    return pl.pallas_call(
        body, name="gates_bwd", grid=(1,),
        in_specs=[pl.BlockSpec((NH, t, LANE), lambda i: (0, 0, 0)),
                  pl.BlockSpec((t, LANE), lambda i: (0, gate_tile)), pl.BlockSpec((1, LANE), lambda i: (0, 0))],
        out_specs=[pl.BlockSpec((t, LANE), lambda i: (0, 0)), pl.BlockSpec((1, LANE), lambda i: (0, 0))],
        out_shape=[jax.ShapeDtypeStruct((t, LANE), BF16), jax.ShapeDtypeStruct((1, LANE), F32)],
        compiler_params=_params("arbitrary"),
    )(dgate, proj, bias_tile)


def _in_turn(heads):
    while heads:
        heads = [g for g in heads if next(g, heads) is not heads]


def _chunk_gates(gc, gr, h, mprev):
    L = CHUNK
    icol, bcol = gc[:, h:h + 1], gc[:, h + NH:h + NH + 1]
    irow, brow = gr[h:h + 1, :], gr[h + NH:h + NH + 1, :]
    tri = _iota((L, L), 0) >= _iota((L, L), 1)
    log_d = jnp.where(tri, bcol - brow + irow, -jnp.inf)
    inter = bcol + mprev
    mt = jnp.maximum(inter, jnp.max(log_d, axis=1, keepdims=True))
    dw = jnp.exp(log_d - mt)
    iw = jnp.exp(inter - mt)
    g = brow[:, L - 1:L]
    wlog_col = g - bcol + icol
    wlog_row = g - brow + irow
    mnew = jnp.maximum(g + mprev, jnp.max(wlog_row, axis=1, keepdims=True))
    wcol = jnp.exp(wlog_col - mnew)
    decay = jnp.exp(g + mprev - mnew)
    return dw, iw, mt, wcol, decay, mnew


def _mlstm_fwd(proj, gcol, grow, t, wc, dh):
    nc = t // CHUNK
    wm = NH * dh
    assert wc == wm, (wc, wm)
    qoff = 3 * wc // wm
    scale = dh ** -0.5

    def body(q_ref, k_ref, v_ref, gc_ref, gr_ref, h_ref, cs_ref, ns_ref, c_s, n_s, m_s):
        @pl.when(pl.program_id(0) == 0)
        def _():
            c_s[...] = jnp.zeros_like(c_s)
            n_s[...] = jnp.zeros_like(n_s)
            m_s[...] = jnp.zeros_like(m_s)

        gc, gr = gc_ref[...], gr_ref[0]
        done = [None] * NH

        def head(h):
            cols = slice(h * dh, (h + 1) * dh)
            mprev = m_s[h, 0:1, 0:1]
            cprev = c_s[h]
            n8 = n_s[h]
            nprev = n8[0:1]
            qs = q_ref[:, cols] * scale
            k = k_ref[:, cols]
            qs_b, k_b, v_b = qs.astype(BF16), k.astype(BF16), v_ref[:, cols].astype(BF16)
            qk = _dot(qs_b, k_b, _NT)
            yield
            q_c = _dot(qs_b, cprev.astype(BF16))
            yield
            dw, iw, mt, wcol, decay, mnew = _chunk_gates(gc, gr, h, mprev)
            yield
            s = qk * dw
            wk = wcol * k
            num = _dot(s.astype(BF16), v_b) + iw * q_c
            yield
            c_new = decay * cprev + _dot(wk.astype(BF16), v_b, _TN)
            yield
            den = jnp.sum(s, axis=1, keepdims=True) + iw * jnp.sum(qs * nprev, axis=1, keepdims=True)
            done[h] = (cprev, jnp.where(_iota(n8.shape, 0) == 1, mprev, n8),
                       num / jnp.maximum(jnp.abs(den), jnp.exp(-mt)), c_new,
                       decay * n8 + jnp.sum(wk, axis=0, keepdims=True), mnew)

        _in_turn([head(h) for h in range(NH)])
        for h, (c_old, n_old, h_out, c_new, n_new, m_new) in enumerate(done):
            cs_ref[h] = c_old
            ns_ref[h] = n_old
            h_ref[:, h * dh:(h + 1) * dh] = h_out
            c_s[h] = c_new
            n_s[h] = n_new
            m_s[h] = jnp.broadcast_to(m_new, m_s.shape[1:])

    grp = lambda off: pl.BlockSpec((CHUNK, wm), lambda c: (c, qoff + off))
    return pl.pallas_call(
        body, name="mlstm_fwd", grid=(nc,),
        in_specs=[grp(0), grp(1), grp(2),
                  pl.BlockSpec((CHUNK, LANE), lambda c: (c, 0)),
                  pl.BlockSpec((1, 8, CHUNK), lambda c: (c, 0, 0))],
        out_specs=[pl.BlockSpec((CHUNK, wm), lambda c: (c, 0)),
                   pl.BlockSpec((NH, None, dh, dh), lambda c: (0, c, 0, 0)),
                   pl.BlockSpec((NH, None, 8, dh), lambda c: (0, c, 0, 0))],
        out_shape=[jax.ShapeDtypeStruct((t, wm), F32),
                   jax.ShapeDtypeStruct((NH, nc, dh, dh), F32),
                   jax.ShapeDtypeStruct((NH, nc, 8, dh), F32)],
        scratch_shapes=[pltpu.VMEM((NH, dh, dh), F32), pltpu.VMEM((NH, 8, dh), F32), pltpu.VMEM((NH, 8, LANE), F32)],
        compiler_params=_params("arbitrary"),
    )(proj, proj, proj, gcol, grow)


def _mlstm_bwd(proj, gcol, grow, hval, dh_in, cs, ns, t, wc, dh):
    nc = t // CHUNK
    wm = NH * dh
    assert wc == wm, (wc, wm)
    qoff = 3 * wc // wm
    scale = dh ** -0.5
    L = CHUNK

    def body(q_ref, k_ref, v_ref, gc_ref, gr_ref, h_ref, dh_ref, cs_ref, ns_ref,
             dq_ref, dk_ref, dv_ref, dg_ref, dc_s, dn_s):
        @pl.when(pl.program_id(0) == 0)
        def _():
            dc_s[...] = jnp.zeros_like(dc_s)
            dn_s[...] = jnp.zeros_like(dn_s)

        gc, gr = gc_ref[...], gr_ref[0]
        eye = _iota((L, L), 0) == _iota((L, L), 1)
        lane = _iota((L, LANE), 1)
        last = _iota((L, 1), 0) == L - 1
        done = [None] * NH

        def head(h):
            cols = slice(h * dh, (h + 1) * dh)
            ns8 = ns_ref[h]
            nprev = ns8[0:1]
            mprev = ns8[1:2, 0:1]
            cprev = cs_ref[h]
            dcn = dc_s[h]
            dn8 = dn_s[h]
            dnn = dn8[0:1]

            qs = q_ref[:, cols] * scale
            k = k_ref[:, cols]
            qs_b, k_b, v_b = qs.astype(BF16), k.astype(BF16), v_ref[:, cols].astype(BF16)
            qk = _dot(qs_b, k_b, _NT)
            yield
            dw, iw, mt, wcol, decay, _ = _chunk_gates(gc, gr, h, mprev)
            yield
            s = qk * dw
            den = jnp.sum(s, axis=1, keepdims=True) + iw * jnp.sum(qs * nprev, axis=1, keepdims=True)
            emt = jnp.exp(-mt)
            r = 1.0 / jnp.maximum(jnp.abs(den), emt)
            dout = dh_ref[:, cols]
            dnum = dout * r
            dden = (-jnp.sum(dout * h_ref[:, cols], axis=1, keepdims=True) * r
                    * jnp.where(jnp.abs(den) > emt, jnp.sign(den), 0.0))
            dnum_b = dnum.astype(BF16)
            cprev_b = cprev.astype(BF16)
            dcn_b = dcn.astype(BF16)
            yield

            g_raw = _dot(dnum_b, v_b, _NT)
            yield
            q_inter = _dot(dnum_b, cprev_b, _NT)
            yield
            k_raw = _dot(v_b, dcn_b, _NT)
            yield
            gd = (g_raw + dden) * dw
            gd_b = gd.astype(BF16)
            dqs_inter = iw * (q_inter + dden * nprev)
            dk_inter = wcol * (k_raw + dnn)
            wk = wcol * k
            iq = iw * qs
            dqs = _dot(gd_b, k_b) + dqs_inter
            yield
            dk = _dot(gd_b, qs_b, _TN) + dk_inter
            yield
            dv = _dot(s.astype(BF16), dnum_b, _TN) + _dot(wk.astype(BF16), dcn_b)
            yield
            dc_new = decay * dcn + _dot(iq.astype(BF16), dnum_b, _TN)
            yield

            e = gd * qk
            e_cols = jnp.sum(jnp.where(eye, jnp.sum(e, axis=0, keepdims=True), 0.0), axis=1, keepdims=True)
            yield
            k_inter = jnp.sum(k * dk_inter, axis=1, keepdims=True)
            rq = jnp.sum(e, axis=1, keepdims=True) + jnp.sum(qs * dqs_inter, axis=1, keepdims=True)
            rk = e_cols + k_inter
            hsum = jnp.sum(k_inter, axis=0, keepdims=True)
            jdec = decay * (jnp.sum(jnp.sum(dcn * cprev, axis=1, keepdims=True), axis=0, keepdims=True)
                            + jnp.sum(dnn * nprev, axis=1, keepdims=True))
            db = rq - rk + jnp.where(last, hsum + jdec, 0.0)
            done[h] = (jnp.where(lane == 0, rk, jnp.where(lane == 1, db, 0.0)),
                       (dqs * scale).astype(BF16), dk.astype(BF16), dv.astype(BF16), dc_new,
                       decay * dn8 + jnp.sum(iq * dden, axis=0, keepdims=True))

        _in_turn([head(h) for h in range(NH)])
        for h, (dgate, dq, dk, dv, dc_new, dn_new) in enumerate(done):
            cols = slice(h * dh, (h + 1) * dh)
            dg_ref[h] = dgate
            dq_ref[:, cols] = dq
            dk_ref[:, cols] = dk
            dv_ref[:, cols] = dv
            dc_s[h] = dc_new
            dn_s[h] = dn_new

    rc = lambda c: nc - 1 - c
    grp = lambda off: pl.BlockSpec((L, wm), lambda c: (rc(c), qoff + off))
    hm = pl.BlockSpec((L, wm), lambda c: (rc(c), 0))
    act = jax.ShapeDtypeStruct((t, wm), BF16)
    return pl.pallas_call(
        body, name="mlstm_bwd", grid=(nc,),
        in_specs=[grp(0), grp(1), grp(2),
                  pl.BlockSpec((L, LANE), lambda c: (rc(c), 0)),
                  pl.BlockSpec((1, 8, L), lambda c: (rc(c), 0, 0)),
                  hm, hm,
                  pl.BlockSpec((NH, None, dh, dh), lambda c: (0, rc(c), 0, 0)),
                  pl.BlockSpec((NH, None, 8, dh), lambda c: (0, rc(c), 0, 0))],
        out_specs=[hm, hm, hm, pl.BlockSpec((NH, L, LANE), lambda c: (0, rc(c), 0))],
        out_shape=[act, act, act, jax.ShapeDtypeStruct((NH, t, LANE), F32)],
        scratch_shapes=[pltpu.VMEM((NH, dh, dh), F32), pltpu.VMEM((NH, 8, dh), F32)],
        compiler_params=_params("arbitrary"),
    )(proj, proj, proj, gcol, grow, hval, dh_in, cs, ns)


def _head_norm(hv):
    mu = jnp.mean(hv, axis=1, keepdims=True)
    hc = hv - mu
    rstd = lax.rsqrt(jnp.mean(hc * hc, axis=1, keepdims=True) + HN_EPS)
    return hc * rstd, rstd


def _hnorm_fwd(hval, proj, gain, y, t, wc, dh, tr=256):
    ooff = 3 * wc // dh + 3 * NH

    def body(h_ref, o_ref, g_ref, y_in, y_ref):
        hhat, _ = _head_norm(h_ref[...])
        y_ref[...] = (_sigmoid(o_ref[...]) * hhat * g_ref[...]).astype(BF16)

    return pl.pallas_call(
        body, name="hnorm_fwd", grid=(t // tr, NH),
        in_specs=[pl.BlockSpec((tr, dh), lambda i, h: (i, h)),
                  pl.BlockSpec((tr, dh), lambda i, h: (i, ooff + h)),
                  pl.BlockSpec((1, dh), lambda i, h: (0, h)),
                  pl.BlockSpec(memory_space=pl.ANY)],
        out_specs=pl.BlockSpec((None, tr, dh), lambda i, h: (1, i, h)),
        out_shape=jax.ShapeDtypeStruct(y.shape, BF16),
        input_output_aliases={3: 0},
        compiler_params=_params("parallel", "parallel"),
    )(hval, proj, gain, y)


def _hnorm_bwd(dy, hval, proj, gain, t, wc, dh, tr=256):
    ooff = 3 * wc // dh + 3 * NH
    yoff = wc // dh

    def body(dy_ref, h_ref, o_ref, g_ref, do_ref, dh_ref, dg_ref):
        i = pl.program_id(1)
        hhat, rstd = _head_norm(h_ref[...])
        gain_v = g_ref[...]
        sig = _sigmoid(o_ref[...])
        d = dy_ref[...]
        do_ref[...] = (d * hhat * gain_v * sig * (1.0 - sig)).astype(BF16)
        dhn = d * sig
        part = jnp.sum(dhn * hhat, axis=0, keepdims=True)

        @pl.when(i == 0)
        def _():
            dg_ref[...] = part

        @pl.when(i > 0)
        def _():
            dg_ref[...] += part

        dhat = dhn * gain_v
        dh_ref[...] = rstd * (dhat - jnp.mean(dhat, axis=1, keepdims=True)
                              - hhat * jnp.mean(dhat * hhat, axis=1, keepdims=True))

    blk = lambda off: pl.BlockSpec((tr, dh), lambda h, i: (i, off + h))
    return pl.pallas_call(
        body, name="hnorm_bwd", grid=(NH, t // tr),
        in_specs=[blk(yoff), blk(0), blk(ooff), pl.BlockSpec((1, dh), lambda h, i: (0, h))],
        out_specs=[blk(0), blk(0), pl.BlockSpec((1, dh), lambda h, i: (0, h))],
        out_shape=[jax.ShapeDtypeStruct((t, NH * dh), BF16), jax.ShapeDtypeStruct((t, NH * dh), F32),
                   jax.ShapeDtypeStruct((1, NH * dh), F32)],
        compiler_params=_params("parallel", "arbitrary"),
    )(dy, hval, proj, gain)


def _ln_stats(z):
    mu = jnp.mean(z, axis=1, keepdims=True)
    zc = z - mu
    rstd = lax.rsqrt(jnp.mean(zc * zc, axis=1, keepdims=True) + LN_EPS)
    return zc * rstd, rstd


def _ln_bwd(dy, xhat, rstd, g):
    dxh = dy * g
    return rstd * (dxh - jnp.mean(dxh, axis=1, keepdims=True) - xhat * jnp.mean(dxh * xhat, axis=1, keepdims=True))


def _accum(ref, i, part):
    @pl.when(i == 0)
    def _():
        ref[...] = part

    @pl.when(i > 0)
    def _():
        ref[...] += part


def _ln1_fwd(x, mix, g, b, tr=256):
    t, d = x.shape

    def body(x_ref, m_ref, g_ref, b_ref, xh_ref, rs_ref, xb_ref):
        xhat, rstd = _ln_stats(ALPHA * x_ref[...] + m_ref[...])
        xh_ref[...] = xhat
        rs_ref[...] = rstd
        xb_ref[...] = (xhat * g_ref[...] + b_ref[...]).astype(BF16)

    row = pl.BlockSpec((tr, d), lambda i: (i, 0))
    vec = pl.BlockSpec((1, d), lambda i: (0, 0))
    return pl.pallas_call(
        body, name="ln1_fwd", grid=(t // tr,),
        in_specs=[row, row, vec, vec],
        out_specs=[row, pl.BlockSpec((tr, 1), lambda i: (i, 0)), row],
        out_shape=[jax.ShapeDtypeStruct((t, d), F32), jax.ShapeDtypeStruct((t, 1), F32),
                   jax.ShapeDtypeStruct((t, d), BF16)],
        compiler_params=_params("parallel"),
    )(x, mix, g, b)


def _ln2_loss(xhat1, g1, b1, ff, target, g2, b2, tr=256):
    t, d = ff.shape

    def body(xh_ref, g1_ref, b1_ref, f_ref, t_ref, g_ref, b_ref, dz_ref, dzb_ref, dg_ref, db_ref, l_ref):
        i = pl.program_id(0)
        x1 = xh_ref[...] * g1_ref[...] + b1_ref[...]
        xhat, rstd = _ln_stats(ALPHA * x1 + f_ref[...])
        gv = g_ref[...]
        e = xhat * gv + b_ref[...] - t_ref[...]
        lsum = jnp.sum(jnp.sum(e * e, axis=1, keepdims=True), axis=0, keepdims=True) * (0.5 / d)
        dy = e * (1.0 / d)
        _accum(dg_ref, i, jnp.sum(dy * xhat, axis=0, keepdims=True))
        _accum(db_ref, i, jnp.sum(dy, axis=0, keepdims=True))
        _accum(l_ref, i, jnp.broadcast_to(lsum, l_ref.shape))
        dz = _ln_bwd(dy, xhat, rstd, gv)
        dz_ref[...] = dz
        dzb_ref[...] = dz.astype(BF16)

    row = pl.BlockSpec((tr, d), lambda i: (i, 0))
    vec = pl.BlockSpec((1, d), lambda i: (0, 0))
    return pl.pallas_call(
        body, name="ln2_loss", grid=(t // tr,),
        in_specs=[row, vec, vec, row, row, vec, vec],
        out_specs=[row, row, vec, vec, pl.BlockSpec((8, LANE), lambda i: (0, 0))],
        out_shape=[jax.ShapeDtypeStruct((t, d), F32), jax.ShapeDtypeStruct((t, d), BF16),
                   jax.ShapeDtypeStruct((1, d), F32), jax.ShapeDtypeStruct((1, d), F32),
                   jax.ShapeDtypeStruct((8, LANE), F32)],
        compiler_params=_params("arbitrary"),
    )(xhat1, g1, b1, ff, target, g2, b2)


def _ln1_bwd(dz2, dffn, xhat1, rstd1, g1, tr=256):
    t, d = dz2.shape

    def body(a_ref, f_ref, xh_ref, rs_ref, g_ref, dz_ref, dzb_ref, dg_ref, db_ref):
        i = pl.program_id(0)
        dy = ALPHA * a_ref[...] + f_ref[...]
        xhat = xh_ref[...]
        _accum(dg_ref, i, jnp.sum(dy * xhat, axis=0, keepdims=True))
        _accum(db_ref, i, jnp.sum(dy, axis=0, keepdims=True))
        dz = _ln_bwd(dy, xhat, rs_ref[...], g_ref[...])
        dz_ref[...] = dz
        dzb_ref[...] = dz.astype(BF16)

    row = pl.BlockSpec((tr, d), lambda i: (i, 0))
    vec = pl.BlockSpec((1, d), lambda i: (0, 0))
    return pl.pallas_call(
        body, name="ln1_bwd", grid=(t // tr,),
        in_specs=[row, row, row, pl.BlockSpec((tr, 1), lambda i: (i, 0)), vec],
        out_specs=[row, row, vec, vec],
        out_shape=[jax.ShapeDtypeStruct((t, d), F32), jax.ShapeDtypeStruct((t, d), BF16),
                   jax.ShapeDtypeStruct((1, d), F32), jax.ShapeDtypeStruct((1, d), F32)],
        compiler_params=_params("arbitrary"),
    )(dz2, dffn, xhat1, rstd1, g1)


def _ffn_act_fwd(hid0, w_fc, b_fc, t, dff):
    nb = dff // LANE

    def body(hv_ref, hg_ref, wv_ref, wg_ref, bv_ref, bg_ref, a_ref):
        val = _conv(hv_ref[...], wv_ref[...]) + bv_ref[...]
        gate = _conv(hg_ref[...], wg_ref[...]) + bg_ref[...]
        a_ref[...] = (gate * _sigmoid(gate) * val).astype(BF16)

    col = lambda off: pl.BlockSpec((t, LANE), lambda j: (0, j + off))
    w3 = lambda off: pl.BlockSpec((3, LANE), lambda j: (0, j + off))
    w1 = lambda off: pl.BlockSpec((1, LANE), lambda j: (0, j + off))
    return pl.pallas_call(
        body, name="ffn_act_fwd", grid=(nb,),
        in_specs=[col(0), col(nb), w3(0), w3(nb), w1(0), w1(nb)],
        out_specs=col(0),
        out_shape=jax.ShapeDtypeStruct((t, dff), BF16),
        compiler_params=_params("parallel"),
    )(hid0, hid0, w_fc, w_fc, b_fc, b_fc)


def _ffn_act_bwd(da, hid0, w_fc, b_fc, t, dff):
    nb = dff // LANE

    def body(da_ref, hv_ref, hg_ref, wv_ref, wg_ref, bv_ref, bg_ref,
             dhv_ref, dhg_ref, dwv_ref, dwg_ref, dbv_ref, dbg_ref):
        hv, hg, wv, wg = hv_ref[...], hg_ref[...], wv_ref[...], wg_ref[...]
        rv, rg = _rolled(hv), _rolled(hg)
        val = _conv(hv, wv, rv) + bv_ref[...]
        gate = _conv(hg, wg, rg) + bg_ref[...]
        sig = _sigmoid(gate)
        d = da_ref[...]
        dsig = d * sig
        dval = dsig * gate
        dgate = dsig * val * (1.0 + gate * (1.0 - sig))
        dhv_ref[...] = _conv_t(dval, wv).astype(BF16)
        dhg_ref[...] = _conv_t(dgate, wg).astype(BF16)
        dwv_ref[...] = _conv_dw(dval, hv, rv)
        dwg_ref[...] = _conv_dw(dgate, hg, rg)
        dbv_ref[...] = jnp.sum(dval, axis=0, keepdims=True)
        dbg_ref[...] = jnp.sum(dgate, axis=0, keepdims=True)

    col = lambda off: pl.BlockSpec((t, LANE), lambda j: (0, j + off))
    w3 = lambda off: pl.BlockSpec((3, LANE), lambda j: (0, j + off))
    w1 = lambda off: pl.BlockSpec((1, LANE), lambda j: (0, j + off))
    s3 = jax.ShapeDtypeStruct((3, dff), F32)
    s1 = jax.ShapeDtypeStruct((1, dff), F32)
    return pl.pallas_call(
        body, name="ffn_act_bwd", grid=(nb,),
        in_specs=[col(0), col(0), col(nb), w3(0), w3(nb), w1(0), w1(nb)],
        out_specs=[col(0), col(0), w3(0), w3(0), w1(0), w1(0)],
        out_shape=[jax.ShapeDtypeStruct((t, dff), BF16)] * 2 + [s3, s3, s1, s1],
        compiler_params=_params("parallel"),
    )(da, hid0, hid0, w_fc, w_fc, b_fc, b_fc)


class _Ready:
    def __init__(self, **weights):
        self.weights = weights

    def begin(self, after):
        return None

    def forward(self, name, after):
        return None

    def get(self, name, after):
        return self.weights[name]


class _Kept:
    def __init__(self):
        self.grads = {}

    def start(self, name, grad):
        self.grads[name] = grad
        return None

    def relay(self, name, after):
        return None

    def meanwhile(self, small, loss, after):
        return None


def _behind(a, token):
    return a if token is None else a + token[0:1, 0:1].reshape((1,) * a.ndim)


def _local_step(x, target, w_in, b_gates, w_sc, gain, w_out, ln1_g, ln1_b, w_up, w_fc, b_fc, w_down, ln2_g, ln2_b,
                gx=None, wx=None):
    t, d = x.shape
    wc = d // 2
    dh = (d - wc) // NH
    wm = NH * dh
    dff = w_fc.shape[1] // 2
    if wx is None:
        wx = _Ready(w_out=w_out, w_up=w_up, w_down=w_down)
    ninp = w_in.shape[0]
    nin = 3 * wc + 4 * wm
    gate_tile = nin // LANE
    nc = t // CHUNK
    bias_tile = jnp.pad(b_gates, ((0, 0), (0, LANE - 2 * NH)))

    x_b = _behind(x, wx.begin(w_in)).astype(BF16)
    proj = _matmul(x_b, w_in, "nt", F32, "proj", tm=512, tn=2688, tk=d, after=wx.begin(w_in))
    y = _sconv_fwd(proj, w_sc, t, wc)
    gcol = _gates_prep(proj, bias_tile, t, gate_tile)
    grow = gcol[:, :8].T.reshape(8, nc, CHUNK).transpose(1, 0, 2)
    hval, cs, ns = _mlstm_fwd(proj, gcol, grow, t, wc, dh)
    y = _hnorm_fwd(hval, proj, gain, y, t, wc, dh)
    tok = wx.forward("w_out", y)
    w_out = wx.get("w_out", tok)
    mix = _matmul(y, w_out, "nn", F32, "out_proj", tm=512, tn=1024, tk=wc, a_blocked=True, after=tok)
    xhat1, rstd1, x1_b = _ln1_fwd(x, mix, _behind(ln1_g, wx.forward("w_up", mix)), ln1_b)
    w_up = wx.get("w_up", x1_b)
    wsl = w_up.shape[2]
    hid0 = _matmul(x1_b, w_up, "nn", F32, "ffn_up", tm=512, tn=wsl, tk=d, b_blocked=True)
    act = _ffn_act_fwd(hid0, w_fc, _behind(b_fc, wx.forward("w_down", hid0)), t, dff)
    w_down = wx.get("w_down", act)
    ff = _matmul(act, w_down, "nn", F32, "ffn_down", tm=1024, tn=512, tk=dff)
    dz2, dz2_b, d_ln2_g, d_ln2_b, loss = _ln2_loss(xhat1, ln1_g, ln1_b, ff, target, ln2_g, ln2_b)

    if gx is None:
        gx = _Kept()
    d_w_down = _matmul(act, dz2_b, "tn", BF16, "ffn_down_dw", tm=512, tn=1024, tk=t)
    d_act = _matmul(dz2_b, w_down, "nt", F32, "ffn_down_dx", tm=1024, tn=512, tk=d, after=gx.start("w_down", d_w_down))
    *d_hid0, dwv, dwg, dbv, dbg = _ffn_act_bwd(d_act, hid0, w_fc, _behind(b_fc, gx.relay("w_down", d_act)), t, dff)
    d_w_fc = jnp.concatenate([dwv, dwg], axis=1)
    d_b_fc = jnp.concatenate([dbv, dbg], axis=1)
    d_hid0 = tuple(d_hid0[:2])
    d_w_up = _matmul(x1_b, d_hid0, "tn", BF16, "ffn_up_dw", tm=512, tn=wsl, tk=t, o_width=wsl)
    d_x1_ffn = _matmul(d_hid0, w_up, "nt", F32, "ffn_up_dx", tm=1024, tn=1024, tk=wsl, b_blocked=True,
                       after=gx.start("w_up", d_w_up))
    dz1, dz1_b, d_ln1_g, d_ln1_b = _ln1_bwd(dz2, d_x1_ffn, xhat1, rstd1, _behind(ln1_g, gx.relay("w_up", d_x1_ffn)))

    d_w_out = _matmul(y, dz1_b, "tn", BF16, "out_proj_dw", tm=512, tn=1024, tk=t, a_blocked=True)
    dy = _matmul(dz1_b, w_out, "nt", F32, "out_proj_dx", tm=512, tn=1024, tk=d, after=gx.start("w_out", d_w_out))
    dcb, dcc, dch, d_w_sc = _sconv_bwd(dy, proj, _behind(w_sc, gx.relay("w_out", dy)), t, wc)
    d_o, d_hval, d_gain = _hnorm_bwd(dy, hval, proj, gain, t, wc, dh)
    dq, dk, dv, dgate = _mlstm_bwd(proj, gcol, grow, hval, d_hval, cs, ns, t, wc, dh)
    dgt, d_b_gates = _gates_bwd(dgate, proj, bias_tile, t, gate_tile)
    pad = jnp.zeros((t, ninp - nin - LANE), BF16)
    d_proj = jnp.concatenate([dcb, dcc, dch, dq, dk, dv, d_o, dgt, pad], axis=1)
    d_w_in = _matmul(d_proj, x_b, "tn", BF16, "proj_dw", tm=IN_SLAB, tn=1024, tk=t)
    small = dict(b_gates=d_b_gates[:, :2 * NH], w_sc_conv=d_w_sc, mh_gain=d_gain, ln1_g=d_ln1_g, ln1_b=d_ln1_b,
                 w_ffn_conv=d_w_fc, b_ffn_conv=d_b_fc, ln2_g=d_ln2_g, ln2_b=d_ln2_b)
    token = gx.start("w_in", d_w_in.reshape(ninp // IN_SLAB, IN_SLAB, d))
    token = gx.relay("w_in", gx.meanwhile(small, loss, token))
    grad_x = _matmul(d_proj, w_in, "nn", F32, "proj_dx", tm=512, tn=512, tk=ninp, add=dz1, add_scale=ALPHA, after=token)
    return loss, grad_x, small, gx


HBM = pl.BlockSpec(memory_space=pltpu.HBM)


def _place():
    return lax.axis_index("x"), lax.axis_index("y"), lax.axis_index("c")


def _index(p):
    return 4 * p[0] + 2 * p[1] + p[2]


def _all_gather(arrs, name):
    n = len(arrs)

    def body(*refs):
        ins, outs = refs[:n], refs[n:2 * n]
        send_sems, recv_sems, local_sems = refs[2 * n:]
        x, y, c = _place()
        me, sibling = (x, y, c), (x, y, 1 - c)
        chips = [(1 - x, y), (x, 1 - y), (1 - x, 1 - y)]

        def copy(a, k, block, to, own=False):
            dst = outs[a].at[_index(block)]
            return pltpu.make_async_remote_copy(
                src_ref=ins[a] if own else dst, dst_ref=dst,
                send_sem=send_sems.at[k * n + a], recv_sem=recv_sems.at[k * n + a],
                device_id=to, device_id_type=MESH)

        mine = [pltpu.make_async_copy(ins[a], outs[a].at[_index(me)], local_sems.at[a]) for a in range(n)]
        for cp in mine:
            cp.start()
        first = []
        for a in range(n):
            first.append(copy(a, 0, me, sibling, own=True))
            first += [copy(a, 1 + j, me, (*chip, c), own=True) for j, chip in enumerate(chips)]
        for cp in first:
            cp.start()
        passed = []
        for j, chip in enumerate(chips):
            for a in range(n):
                copy(a, 1 + j, (*chip, c), me).wait_recv()
                cp = copy(a, 4 + j, (*chip, c), sibling)
                cp.start()
                passed.append(cp)
        for a in range(n):
            copy(a, 0, sibling, me).wait_recv()
            for j, chip in enumerate(chips):
                copy(a, 4 + j, (*chip, 1 - c), me).wait_recv()
        for cp in first + passed:
            cp.wait_send()
        for cp in mine:
            cp.wait()

    return pl.pallas_call(
        body, name=name, in_specs=[HBM] * n, out_specs=[HBM] * n,
        out_shape=[jax.ShapeDtypeStruct((N_DEV,) + a.shape, a.dtype) for a in arrs],
        scratch_shapes=[pltpu.SemaphoreType.DMA((7 * n,)), pltpu.SemaphoreType.DMA((7 * n,)),
                        pltpu.SemaphoreType.DMA((n,))],
    )(*arrs)


SEM = pl.BlockSpec(memory_space=pltpu.SEMAPHORE)
EFFECT = pltpu.SideEffectType.DATAFLOW_SIDE_EFFECTING


def _chips(x, y):
    return [(1 - x, y), (x, 1 - y), (1 - x, 1 - y)]


N_CHIP = N_DEV // 2


def _pair_route(x, y, c):
    return [((x, y, 1 - c), 2 * q + (1 - c), q, q) for q in range(N_CHIP)]


def _chip_route(x, y, c):
    mine = 2 * x + y
    return [((*chip, c), 2 * chip[0] + chip[1], mine, 2 * chip[0] + chip[1]) for chip in _chips(x, y)]


def _exchange_pieces(g_ref, land_ref, width, tail):
    if not tail:
        return [(lambda i: g_ref.at[i], lambda s: land_ref.at[s])]
    return [(lambda i: g_ref.at[i], lambda s: land_ref.at[s, pl.ds(0, width), :]),
            (lambda i: g_ref.at[i + 1, pl.ds(0, IN_TAIL), :], lambda s: land_ref.at[s, pl.ds(width, IN_TAIL), :])]


def _exchange_start(grad, route, tail, name):
    width = grad.shape[1]
    n_p = 2 if tail else 1
    n_c = len(route(0, 0, 0))
    land_shape = (N_CHIP, width + (IN_TAIL if tail else 0), grad.shape[2])

    def body(g_ref, land_ref, send_sems, recv_sems, g_thru, land_thru, token):
        for j, (peer, slab, slot, _) in enumerate(route(*_place())):
            for p, (src, dst) in enumerate(_exchange_pieces(g_ref, land_ref, width, tail)):
                pltpu.make_async_remote_copy(src_ref=src(slab), dst_ref=dst(slot), send_sem=send_sems.at[j * n_p + p],
                                             recv_sem=recv_sems.at[j * n_p + p], device_id=peer,
                                             device_id_type=MESH).start()
        token[...] = jnp.zeros_like(token)

    return pl.pallas_call(
        body, name=name,
        out_shape=(pltpu.SemaphoreType.DMA((n_c * n_p,)), pltpu.SemaphoreType.DMA((n_c * n_p,)),
                   pltpu.HBM(grad.shape, grad.dtype), pltpu.HBM(land_shape, grad.dtype),
                   jax.ShapeDtypeStruct((8, LANE), F32)),
        in_specs=(HBM, HBM), out_specs=(SEM, SEM, HBM, HBM, pl.BlockSpec(memory_space=pltpu.VMEM)),
        input_output_aliases={0: 2, 1: 3},
        compiler_params=pltpu.CompilerParams(has_side_effects=EFFECT),
    )(pltpu.with_memory_space_constraint(grad, pltpu.HBM),
      pltpu.with_memory_space_constraint(lax.empty(land_shape, grad.dtype), pltpu.HBM))


def _exchange_wait(send_sems, recv_sems, g_thru, land_thru, after, route, tail, name):
    width = g_thru.shape[1]
    n_p = 2 if tail else 1

    def body(g_ref, land_ref, send_sems, recv_sems, after_ref, g_dead, got_ref):
        for j, (peer, slab, _, slot) in enumerate(route(*_place())):
            for p, (src, dst) in enumerate(_exchange_pieces(g_ref, land_ref, width, tail)):
                cp = pltpu.make_async_remote_copy(src_ref=src(slab), dst_ref=dst(slot),
                                                  send_sem=send_sems.at[j * n_p + p], recv_sem=recv_sems.at[j * n_p + p],
                                                  device_id=peer, device_id_type=MESH)
                cp.wait_send()
                cp.wait_recv()

    return pl.pallas_call(
        body, name=name,
        out_shape=(pltpu.HBM(g_thru.shape, g_thru.dtype), pltpu.HBM(land_thru.shape, land_thru.dtype)),
        in_specs=(HBM, HBM, SEM, SEM, pl.BlockSpec(memory_space=pl.ANY)), out_specs=(HBM, HBM),
        input_output_aliases={0: 0, 1: 1},
        compiler_params=pltpu.CompilerParams(has_side_effects=EFFECT),
    )(g_thru, land_thru, send_sems, recv_sems, after)


def _pair_add(grad, pair, core, tail, name):
    rows, cols = grad.shape[1], grad.shape[2]
    total = pair.shape[1]

    def body(core_ref, *refs):
        if tail:
            g_ref, t_ref, p_ref, o_ref = refs
            o_ref[0:rows, :] = (g_ref[...].astype(F32) + p_ref[0:rows, :].astype(F32)).astype(BF16)
            o_ref[rows:total, :] = (t_ref[...].astype(F32) + p_ref[rows:total, :].astype(F32)).astype(BF16)
        else:
            g_ref, p_ref, o_ref = refs
            o_ref[...] = (g_ref[...].astype(F32) + p_ref[...].astype(F32)).astype(BF16)

    if tail:
        tc = _fit(cols, 512)
        grid = (N_CHIP, cols // tc)
        slab = pl.BlockSpec((None, total, tc), lambda q, i, core_ref: (q, 0, i))
        in_specs = [pl.BlockSpec((None, rows, tc), lambda q, i, core_ref: (2 * q + core_ref[0], 0, i)),
                    pl.BlockSpec((None, IN_TAIL, tc), lambda q, i, core_ref: (2 * q + core_ref[0] + 1, 0, i))]
    else:
        tr = _rows(rows, 1024)
        grid = (N_CHIP, rows // tr)
        slab = pl.BlockSpec((None, tr, cols), lambda q, i, core_ref: (q, i, 0))
        in_specs = [pl.BlockSpec((None, tr, cols), lambda q, i, core_ref: (2 * q + core_ref[0], i, 0))]
    return pl.pallas_call(
        body, name=name,
        grid_spec=pltpu.PrefetchScalarGridSpec(num_scalar_prefetch=1, grid=grid,
                                               in_specs=in_specs + [slab], out_specs=slab),
        out_shape=jax.ShapeDtypeStruct(pair.shape, BF16),
        compiler_params=_params("parallel", "parallel"),
    )(core, *([grad, grad] if tail else [grad]), pair)


def _gather_start(blocks, after, name, spare=()):
    n = len(blocks)
    lands = [(N_DEV + (a in spare),) + b.shape for a, b in enumerate(blocks)]

    def body(*refs):
        b_refs, land_refs = refs[:n], refs[n:2 * n]
        send_sems, recv_sems = refs[2 * n + 1:3 * n + 1], refs[3 * n + 1:4 * n + 1]
        token = refs[-1]
        x, y, c = _place()
        me = _index((x, y, c))
        for a in range(n):
            for k, to in enumerate([(x, y, 1 - c)] + [(*chip, c) for chip in _chips(x, y)]):
                pltpu.make_async_remote_copy(src_ref=b_refs[a], dst_ref=land_refs[a].at[me], send_sem=send_sems[a].at[k],
                                             recv_sem=recv_sems[a].at[k], device_id=to, device_id_type=MESH).start()
        token[...] = jnp.zeros_like(token)

    sems = [pltpu.SemaphoreType.DMA((4,))] * n
    out = pl.pallas_call(
        body, name=name,
        out_shape=(*sems, *sems, *[pltpu.HBM(b.shape, b.dtype) for b in blocks],
                   *[pltpu.HBM(s, b.dtype) for s, b in zip(lands, blocks)], jax.ShapeDtypeStruct((8, LANE), F32)),
        in_specs=(*[HBM] * (2 * n), pl.BlockSpec(memory_space=pl.ANY)),
        out_specs=(*[SEM] * (2 * n), *[HBM] * (2 * n), pl.BlockSpec(memory_space=pltpu.VMEM)),
        input_output_aliases={i: 2 * n + i for i in range(2 * n)},
        compiler_params=pltpu.CompilerParams(has_side_effects=EFFECT),
    )(*[pltpu.with_memory_space_constraint(b, pltpu.HBM) for b in blocks],
      *[pltpu.with_memory_space_constraint(lax.empty(s, b.dtype), pltpu.HBM) for s, b in zip(lands, blocks)], after)
    return [(out[a], out[n + a], out[2 * n + a], out[3 * n + a]) for a in range(n)], out[-1]


def _gather_forward(send_sems, recv_sems, b_thru, land_thru, after, name):
    def body(b_ref, land_ref, send_sems, recv_sems, after_ref, b_dead, land_out, send2, recv2, token):
        x, y, c = _place()
        sibling = (x, y, 1 - c)
        for k, frm in enumerate([sibling] + [(*chip, c) for chip in _chips(x, y)]):
            cp = pltpu.make_async_remote_copy(src_ref=b_ref, dst_ref=land_ref.at[_index(frm)], send_sem=send_sems.at[k],
                                              recv_sem=recv_sems.at[k], device_id=frm, device_id_type=MESH)
            cp.wait_send()
            cp.wait_recv()
        for j, chip in enumerate(_chips(x, y)):
            slot = land_ref.at[_index((*chip, c))]
            pltpu.make_async_remote_copy(src_ref=slot, dst_ref=slot, send_sem=send2.at[j], recv_sem=recv2.at[j],
                                         device_id=sibling, device_id_type=MESH).start()
        token[...] = jnp.zeros_like(token)

    return pl.pallas_call(
        body, name=name,
        out_shape=(pltpu.HBM(b_thru.shape, b_thru.dtype), pltpu.HBM(land_thru.shape, land_thru.dtype),
                   pltpu.SemaphoreType.DMA((3,)), pltpu.SemaphoreType.DMA((3,)), jax.ShapeDtypeStruct((8, LANE), F32)),
        in_specs=(HBM, HBM, SEM, SEM, pl.BlockSpec(memory_space=pl.ANY)),
        out_specs=(HBM, HBM, SEM, SEM, pl.BlockSpec(memory_space=pltpu.VMEM)),
        input_output_aliases={0: 0, 1: 1},
        compiler_params=pltpu.CompilerParams(has_side_effects=EFFECT),
    )(b_thru, land_thru, send_sems, recv_sems, after)


def _gather_finish(land_thru, send2, recv2, after, name):
    def body(land_ref, send2, recv2, after_ref, land_out):
        x, y, c = _place()
        for j, chip in enumerate(_chips(x, y)):
            cp = pltpu.make_async_remote_copy(src_ref=land_ref.at[_index((*chip, c))],
                                              dst_ref=land_ref.at[_index((*chip, 1 - c))], send_sem=send2.at[j],
                                              recv_sem=recv2.at[j], device_id=(x, y, 1 - c), device_id_type=MESH)
            cp.wait_send()
            cp.wait_recv()

    return pl.pallas_call(
        body, name=name, out_shape=pltpu.HBM(land_thru.shape, land_thru.dtype),
        in_specs=(HBM, SEM, SEM, pl.BlockSpec(memory_space=pl.ANY)), out_specs=HBM,
        input_output_aliases={0: 0},
        compiler_params=pltpu.CompilerParams(has_side_effects=EFFECT),
    )(land_thru, send2, recv2, after)


class _Gathering:
    def __init__(self, first, later, me):
        started, token = _gather_start(list(first.values()), next(iter(first.values())), "gather1_first", spare=(0,))
        cast = [_behind(a, token).astype(BF16) for a in later.values()]
        started_later, self.token = _gather_start(cast, token, "gather1_later")
        self.me, self.state = me, dict(zip([*first, *later], started + started_later))

    def begin(self, after):
        return self.token

    def forward(self, name, after):
        *self.state[name], token = _gather_forward(*self.state[name], after, "gather2_" + name)
        return token

    def get(self, name, after):
        block, land, send2, recv2 = self.state[name]
        land = _gather_finish(land, send2, recv2, after, "gather3_" + name)
        land = lax.dynamic_update_index_in_dim(land, block[None], self.me, 0)
        return land if name not in ("w_out", "w_down") else land.reshape(-1, land.shape[2])


class _Reducing:
    def __init__(self, core, chip, gather_small):
        self.core, self.chip, self.state, self.token, self.gather_small = core, chip, {}, None, gather_small

    def meanwhile(self, small, loss, after):
        self.small_sum = self.gather_small(small, loss, after)
        return self.small_sum

    def start(self, name, grad):
        g = grad if grad.ndim == 3 else grad.reshape(N_DEV, grad.shape[0] // N_DEV, grad.shape[1])
        *self.state[name], token = _exchange_start(g, _pair_route, name == "w_in", "pair_send_" + name)
        return token

    def relay(self, name, after):
        tail = name == "w_in"
        grad, pair = _exchange_wait(*self.state[name], after, _pair_route, tail, "pair_recv_" + name)
        total = _pair_add(grad, pair, self.core, tail, "pair_add_" + name)
        *self.state[name], self.token = _exchange_start(total, _chip_route, False, "chip_send_" + name)
        return self.token

    def finish(self, name, after):
        total, land = _exchange_wait(*self.state[name], after, _chip_route, False, "chip_recv_" + name)
        own = lax.dynamic_index_in_dim(total, self.chip, 0, keepdims=True)
        return lax.dynamic_update_index_in_dim(land, own, self.chip, 0)


def _carry_w_in(main, tail):
    slabs, _, d = main.shape
    tc = _fit(d, 2048)
    assert slabs == N_DEV + 1 and tail.shape[:2] == (N_DEV, IN_TAIL), (main.shape, tail.shape)
    top = lambda off: pl.BlockSpec((None, IN_TAIL, tc), lambda s, j: (s + off, 0, j))

    def carry(m_ref, t_ref, o_ref):
        o_ref[...] = m_ref[...] + t_ref[...]

    main = pl.pallas_call(
        carry, name="carry_w_in", grid=(N_DEV - 1, d // tc), in_specs=[top(1), top(0)], out_specs=top(1),
        out_shape=jax.ShapeDtypeStruct(main.shape, main.dtype), input_output_aliases={0: 0},
        compiler_params=_params("parallel", "parallel"),
    )(main, tail)

    def last(m_ref, t_ref, o_ref):
        o_ref[...] = jnp.zeros_like(o_ref)
        o_ref[0:IN_TAIL, :] = t_ref[...]

    return pl.pallas_call(
        last, name="last_slab_w_in", grid=(d // tc,),
        in_specs=[pl.BlockSpec(memory_space=pl.ANY), pl.BlockSpec((None, IN_TAIL, tc), lambda j: (N_DEV - 1, 0, j))],
        out_specs=pl.BlockSpec((None, IN_SLAB, tc), lambda j: (N_DEV, 0, j)),
        out_shape=jax.ShapeDtypeStruct(main.shape, main.dtype), input_output_aliases={0: 0},
        compiler_params=_params("parallel"),
    )(main, tail)


def _rows(n, want):
    t = min(n, want)
    t -= t % 16
    while n % t:
        t -= 16
    return t


def _adam_math(w, g, m, v):
    m2 = ADAM_B1 * m + (1.0 - ADAM_B1) * g
    v2 = ADAM_B2 * v + (1.0 - ADAM_B2) * (g * g)
    m_hat = m2 / (1.0 - ADAM_B1 ** ADAM_STEP)
    v_hat = v2 / (1.0 - ADAM_B2 ** ADAM_STEP)
    return -ADAM_LR * (m_hat / (jnp.sqrt(v_hat) + ADAM_EPS) + ADAM_WD * w), m2, v2


def _slot_sum(r_ref):
    acc = r_ref[0].astype(F32)
    for i in range(1, r_ref.shape[0]):
        acc = acc + r_ref[i].astype(F32)
    return acc


def _shift_w_in(w):
    ws, d = w.shape
    tc = _fit(d, 256)

    def body(w_ref, main_ref, tail_ref, tall):
        tall[...] = jnp.zeros_like(tall)
        tall[0:ws, :] = w_ref[...]
        moved = pltpu.roll(tall[...], _index(_place()), 0).astype(BF16)
        main_ref[...] = moved[0:IN_SLAB]
        tail_ref[...] = moved[IN_SLAB:]

    return pl.pallas_call(
        body, name="shift_w_in", grid=(d // tc,),
        in_specs=[pl.BlockSpec((ws, tc), lambda j: (0, j))],
        out_specs=[pl.BlockSpec((IN_SLAB, tc), lambda j: (0, j)), pl.BlockSpec((IN_TAIL, tc), lambda j: (0, j))],
        out_shape=[jax.ShapeDtypeStruct((IN_SLAB, d), BF16), jax.ShapeDtypeStruct((IN_TAIL, d), BF16)],
        scratch_shapes=[pltpu.VMEM((IN_SLAB + IN_TAIL, tc), F32)], compiler_params=_params("parallel"),
    )(w)


def _sum_adamw_shifted(r, w, m, v, name):
    _, ph, d = r.shape
    ws = w.shape[0]
    tc = _fit(d, 256)

    def body(r_ref, w_ref, m_ref, v_ref, g_ref, d_ref, m2_ref, v2_ref, tall):
        tall[...] = pltpu.roll(_slot_sum(r_ref), lax.rem(ph - _index(_place()), ph), 0)
        g = tall[0:ws, :]
        g_ref[...] = g
        d_ref[...], m2_ref[...], v2_ref[...] = _adam_math(w_ref[...], g, m_ref[...], v_ref[...])

    blk = pl.BlockSpec((ws, tc), lambda j: (0, j))
    out = jax.ShapeDtypeStruct(w.shape, F32)
    return pl.pallas_call(
        body, name=name, grid=(d // tc,),
        in_specs=[pl.BlockSpec((r.shape[0], ph, tc), lambda j: (0, 0, j)), blk, blk, blk],
        out_specs=[blk] * 4, out_shape=[out] * 4,
        scratch_shapes=[pltpu.VMEM((ph, tc), F32)], compiler_params=_params("parallel"),
    )(r, w, m, v)


def _sum_slots(r, name, tr=128):
    _, rows, cols = r.shape
    tr = _rows(rows, tr)

    def body(r_ref, g_ref):
        g_ref[...] = _slot_sum(r_ref)

    return pl.pallas_call(
        body, name=name, grid=(rows // tr,),
        in_specs=[pl.BlockSpec((r.shape[0], tr, cols), lambda i: (0, i, 0))],
        out_specs=pl.BlockSpec((tr, cols), lambda i: (i, 0)),
        out_shape=jax.ShapeDtypeStruct((rows, cols), F32),
        compiler_params=_params("parallel"),
    )(r)


def _adamw(w, g, m, v, name, tr=256):
    rows, cols = w.shape
    tr = _rows(rows, tr)

    def body(w_ref, g_ref, m_ref, v_ref, d_ref, m2_ref, v2_ref):
        d_ref[...], m2_ref[...], v2_ref[...] = _adam_math(w_ref[...], g_ref[...], m_ref[...], v_ref[...])

    blk = pl.BlockSpec((tr, cols), lambda i: (i, 0))
    out = jax.ShapeDtypeStruct((rows, cols), F32)
    return pl.pallas_call(
        body, name=name, grid=(rows // tr,), in_specs=[blk] * 4, out_specs=[blk] * 3, out_shape=[out] * 3,
        compiler_params=_params("parallel"),
    )(w, g, m, v)


def _sum_adamw(r, w, m, v, name, tr=256):
    rows, cols = w.shape
    tr = _rows(rows, tr)

    def body(r_ref, w_ref, m_ref, v_ref, g_ref, d_ref, m2_ref, v2_ref):
        g = _slot_sum(r_ref)
        g_ref[...] = g
        d_ref[...], m2_ref[...], v2_ref[...] = _adam_math(w_ref[...], g, m_ref[...], v_ref[...])

    blk = pl.BlockSpec((tr, cols), lambda i: (i, 0))
    out = jax.ShapeDtypeStruct((rows, cols), F32)
    return pl.pallas_call(
        body, name=name, grid=(rows // tr,),
        in_specs=[pl.BlockSpec((r.shape[0], tr, cols), lambda i: (0, i, 0)), blk, blk, blk],
        out_specs=[blk] * 4, out_shape=[out] * 4,
        compiler_params=_params("parallel"),
    )(r, w, m, v)


def _pack(pieces, sizes):
    flat = [jnp.pad(p.reshape(-1).astype(F32), (0, s - p.size)) for p, s in zip(pieces, sizes)]
    total = sum(sizes)
    padded = -(-total // (16 * LANE)) * (16 * LANE)
    return jnp.pad(jnp.concatenate(flat), (0, padded - total)).reshape(-1, LANE)


def _unpack(packed, shapes, sizes):
    flat = packed.reshape(-1)
    out, off = [], 0
    for shp, s in zip(shapes, sizes):
        n = 1
        for k in shp:
            n *= k
        out.append(flat[off:off + n].reshape(shp))
        off += s
    return out


def _lanes(n):
    return -(-n // LANE) * LANE


WEIGHTS = ("w_in", "b_gates", "w_sc_conv", "mh_gain", "w_out", "ln1_g", "ln1_b", "w_up", "w_ffn_conv", "b_ffn_conv",
           "w_down", "ln2_g", "ln2_b")
BIG = ("w_in", "w_out", "w_up", "w_down")
SMALL = tuple(n for n in WEIGHTS if n not in BIG)


def kernel(x, w_in, b_gates, w_sc_conv, mh_gain, w_out, ln1_g, ln1_b, w_up, w_ffn_conv, b_ffn_conv, w_down, ln2_g, ln2_b, loss_target, m_w_in, m_b_gates, m_w_sc_conv, m_mh_gain, m_w_out, m_ln1_g, m_ln1_b, m_w_up, m_w_ffn_conv, m_b_ffn_conv, m_w_down, m_ln2_g, m_ln2_b, v_w_in, v_b_gates, v_w_sc_conv, v_mh_gain, v_w_out, v_ln1_g, v_ln1_b, v_w_up, v_w_ffn_conv, v_b_ffn_conv, v_w_down, v_ln2_g, v_ln2_b):
    w = dict(zip(WEIGHTS, (w_in, b_gates, w_sc_conv, mh_gain, w_out, ln1_g, ln1_b, w_up, w_ffn_conv, b_ffn_conv,
                           w_down, ln2_g, ln2_b)))
    m = dict(zip(WEIGHTS, (m_w_in, m_b_gates, m_w_sc_conv, m_mh_gain, m_w_out, m_ln1_g, m_ln1_b, m_w_up,
                           m_w_ffn_conv, m_b_ffn_conv, m_w_down, m_ln2_g, m_ln2_b)))
    v = dict(zip(WEIGHTS, (v_w_in, v_b_gates, v_w_sc_conv, v_mh_gain, v_w_out, v_ln1_g, v_ln1_b, v_w_up,
                           v_w_ffn_conv, v_b_ffn_conv, v_w_down, v_ln2_g, v_ln2_b)))
    me = _index(_place())
    d = x.shape[2]
    ws_in = w_in.shape[2]
    assert ws_in == IN_SLAB + 1 and N_DEV <= LANE, w_in.shape
    ninp = (N_DEV + 1) * IN_SLAB
    ws_sc, ws_fc = w_sc_conv.shape[2], w_ffn_conv.shape[2]
    w_in_t, m_in_t, v_in_t = (jnp.transpose(a[0]) for a in (w_in, m_w_in, v_w_in))

    w_in_main, w_in_tail = _shift_w_in(w_in_t)
    taps8 = lambda a: jnp.pad(a[0], ((0, 5), (0, 0)))
    at_once = ("w_in", "w_tail", "w_sc", "w_fc")
    wx = _Gathering(dict(zip(at_once, (w_in_main, w_in_tail, taps8(w_sc_conv), taps8(w_ffn_conv)))),
                    {n: w[n][0] for n in ("w_out", "w_up", "w_down")}, me)
    token = wx.begin(None)
    for n in at_once:
        token = wx.forward(n, token)
    g_in, g_tail, g_sc, g_fc = (wx.get(n, token) for n in at_once)
    w_in_full = _carry_w_in(g_in, g_tail).reshape(ninp, d)
    w_sc_full = g_sc[:, :3].transpose(1, 0, 2).reshape(3, N_DEV * ws_sc)
    w_fc_full = g_fc[:, :3].transpose(1, 0, 2).reshape(3, N_DEV * ws_fc)

    xi, yi, ci = _place()
    names = ("loss",) + SMALL
    pieces = {}

    def gather_small(small, loss_t, after):
        pieces.update(small, loss=loss_t[0, :1])
        sizes = [_lanes(pieces[n].size) for n in names]
        (g_small,) = _all_gather([_behind(_pack([pieces[n] for n in names], sizes), after)], "gather_small")
        return _sum_slots(g_small, "sum_small", tr=g_small.shape[1])

    gx = _Reducing(jnp.reshape(ci, (1,)).astype(jnp.int32), 2 * xi + yi, gather_small)
    loss_t, grad_x, small, _ = _local_step(
        x[0], loss_target[0], w_in_full, b_gates, w_sc_full, mh_gain, None, ln1_g, ln1_b, None,
        w_fc_full, b_ffn_conv, None, ln2_g, ln2_b, gx=gx, wx=wx)

    grads, deltas, new_m, new_v = {}, {}, {}, {}
    for name in ("w_down", "w_up", "w_out"):
        grads[name], deltas[name], new_m[name], new_v[name] = _sum_adamw(
            gx.finish(name, gx.token), w[name][0], m[name][0], v[name][0], "adamw_" + name)

    summed = _unpack(gx.small_sum, [pieces[n].shape for n in names], [_lanes(pieces[n].size) for n in names])
    full = dict(zip(names, summed))
    full["w_sc_conv"] = lax.dynamic_slice(full["w_sc_conv"], (0, me * ws_sc), (3, ws_sc))
    full["w_ffn_conv"] = lax.dynamic_slice(full["w_ffn_conv"], (0, me * ws_fc), (3, ws_fc))
    for n in SMALL:
        grads[n] = full[n].reshape(w[n].shape)
    sizes = [_lanes(w[n].size) for n in SMALL]
    shapes = [w[n].shape for n in SMALL]
    packed = [_pack([t[n] for n in SMALL], sizes) for t in (w, grads, m, v)]
    small_out = _adamw(*packed, "adamw_small")
    for res, t in zip(small_out, (deltas, new_m, new_v)):
        t.update(zip(SMALL, _unpack(res, shapes, sizes)))

    done = sum(t[0:1, 0:1] for t in (grad_x, deltas["w_down"], deltas["w_up"], deltas["w_out"], small_out[0]))
    grads["w_in"], deltas["w_in"], new_m["w_in"], new_v["w_in"] = (
        jnp.transpose(a)[None] for a in _sum_adamw_shifted(gx.finish("w_in", done), w_in_t, m_in_t, v_in_t, "adamw_w_in"))

    big = lambda t: {n: (t[n].reshape(w[n].shape) if n in BIG else t[n]) for n in WEIGHTS}
    grads, deltas, new_m, new_v = big(grads), big(deltas), big(new_m), big(new_v)
    return (full["loss"].reshape(()), grad_x[None], *[grads[n] for n in WEIGHTS], *[deltas[n] for n in WEIGHTS],
            *[new_m[n] for n in WEIGHTS], *[new_v[n] for n in WEIGHTS])
```

```python
import functools

import jax
import jax.numpy as jnp
from jax import lax
from jax.experimental import pallas as pl
from jax.experimental.pallas import tpu as pltpu

F32 = jnp.float32
BF16 = jnp.bfloat16
MESH = pl.DeviceIdType.MESH

N_DEV = 8
NH = 4
CHUNK = 64
LN_EPS = 1e-5
HN_EPS = 1e-6
ALPHA = 2.0 ** 0.25
LANE = 128
IN_SLAB = 7 * LANE
IN_TAIL = 16
VMEM_LIMIT = 56 * 1024 * 1024
ADAM_LR, ADAM_B1, ADAM_B2, ADAM_EPS, ADAM_WD, ADAM_STEP = 0.001, 0.9, 0.999, 1e-08, 0.01, 10

_NN = (((1,), (0,)), ((), ()))
_NT = (((1,), (1,)), ((), ()))
_TN = (((0,), (0,)), ((), ()))


def _dot(a, b, dn=_NN):
    return lax.dot_general(a, b, dn, preferred_element_type=F32)


def _params(*sem):
    return pltpu.CompilerParams(dimension_semantics=sem if sem else None, vmem_limit_bytes=VMEM_LIMIT)


def _iota(shape, axis):
    return lax.broadcasted_iota(jnp.int32, shape, axis)


def _fit(n, want):
    if n <= want:
        return n
    t = want - want % LANE
    while n % t:
        t -= LANE
    return t


def _matmul(a, b, mode, out_dtype, name, tm=1024, tn=512, tk=1024, add=None, add_scale=1.0,
            a_blocked=False, b_blocked=False, o_width=None, after=None):
    a_parts = a if isinstance(a, tuple) else None
    b_parts = b if isinstance(b, tuple) else None
    if a_parts:
        a_blocked, (a_rows, wa), na = True, a[0].shape, len(a)
        kd, m = (a_rows, na * wa) if mode == "tn" else (na * wa, a_rows)
    elif a_blocked:
        na, a_rows, wa = a.shape
        kd, m = (a_rows, na * wa) if mode == "tn" else (na * wa, a_rows)
    elif mode == "tn":
        kd, m = a.shape
    else:
        m, kd = a.shape
    if b_parts:
        b_blocked, (rows, w), nb = True, b[0].shape, len(b)
    elif b_blocked:
        nb, rows, w = b.shape
    if b_blocked:
        n = rows if mode == "nt" else nb * w
        assert (nb * w if mode == "nt" else rows) == kd, (name, kd)
    else:
        n = b.shape[0] if mode == "nt" else b.shape[1]
    tm, tn, tk = _fit(m, tm), _fit(n, tn), _fit(kd, tk)
    if a_blocked and mode == "tn":
        tm = _fit(wa, tm)
    if a_blocked and mode != "tn":
        tk = _fit(wa, tk)
    if b_blocked and mode != "nt":
        tn = _fit(w, tn)
    if b_blocked and mode == "nt":
        tk = _fit(w, tk)
    if o_width is not None:
        tn = _fit(o_width, tn)
    assert m % tm == 0 and n % tn == 0 and kd % tk == 0, (name, m, n, kd, tm, tn, tk)
    assert not (a_blocked and mode != "tn" and wa % tk) and not (b_blocked and mode == "nt" and w % tk), (name, tk)
    nk = kd // tk
    dn = {"nn": _NN, "nt": _NT, "tn": _TN}[mode]
    if a_blocked and mode == "tn":
        a_per = wa // tm
        a_spec = pl.BlockSpec((None, tk, tm), lambda i, j, k: (i // a_per, k, i % a_per))
    elif a_blocked:
        a_per = wa // tk
        a_spec = pl.BlockSpec((None, tm, tk), lambda i, j, k: (k // a_per, i, k % a_per))
    elif mode == "tn":
        a_spec = pl.BlockSpec((tk, tm), lambda i, j, k: (k, i))
    else:
        a_spec = pl.BlockSpec((tm, tk), lambda i, j, k: (i, k))
    if b_blocked and mode != "nt":
        per = w // tn
        b_spec = pl.BlockSpec((None, tk, tn), lambda i, j, k: (j // per, k, j % per))
    elif b_blocked:
        per = w // tk
        b_spec = pl.BlockSpec((None, tn, tk), lambda i, j, k: (k // per, j, k % per))
    elif mode == "nt":
        b_spec = pl.BlockSpec((tn, tk), lambda i, j, k: (j, k))
    else:
        b_spec = pl.BlockSpec((tk, tn), lambda i, j, k: (k, j))
    if o_width is None:
        o_spec = pl.BlockSpec((tm, tn), lambda i, j, k: (i, j))
        o_shape = (m, n)
    else:
        oper = o_width // tn
        o_spec = pl.BlockSpec((None, tm, tn), lambda i, j, k: (j // oper, i, j % oper))
        o_shape = (n // o_width, m, o_width)
    a_list, a_specs = [a], [a_spec]
    if a_parts:
        hold = lambda x, s: jnp.clip(x - s * a_per, 0, a_per - 1)
        a_list = list(a_parts)
        a_specs = [(pl.BlockSpec((tk, tm), lambda i, j, k, s=s: (k, hold(i, s))) if mode == "tn"
                    else pl.BlockSpec((tm, tk), lambda i, j, k, s=s: (i, hold(k, s)))) for s in range(na)]
    b_list, b_specs = [b], [b_spec]
    if b_parts:
        hold_b = lambda x, s: jnp.clip(x - s * per, 0, per - 1)
        b_list = list(b_parts)
        b_specs = [(pl.BlockSpec((tn, tk), lambda i, j, k, s=s: (j, hold_b(k, s))) if mode == "nt"
                    else pl.BlockSpec((tk, tn), lambda i, j, k, s=s: (k, hold_b(j, s)))) for s in range(nb)]
    n_a, n_b = len(a_list), len(b_list)
    has_add = add is not None
    n_in = n_a + n_b + has_add + (after is not None)
    in_place = nk > 1 and out_dtype == F32

    def body(*refs):
        add_ref = refs[n_a + n_b] if has_add else None
        o_ref = refs[n_in]
        i, j, k = pl.program_id(0), pl.program_id(1), pl.program_id(2)

        def finish(r):
            if has_add:
                r = r + add_scale * add_ref[...]
            o_ref[...] = r.astype(out_dtype)

        def step(a_ref, b_ref):
            if nk == 1:
                finish(_dot(a_ref[...], b_ref[...], dn))
                return
            acc = o_ref if in_place else refs[-1]

            @pl.when(k == 0)
            def _():
                acc[...] = _dot(a_ref[...], b_ref[...], dn)

            @pl.when(k > 0)
            def _():
                acc[...] += _dot(a_ref[...], b_ref[...], dn)

        if n_a == 1 and n_b == 1:
            step(refs[0], refs[1])
        else:
            slab_a = ((i if mode == "tn" else k) // a_per) if n_a > 1 else 0
            slab_b = ((k if mode == "nt" else j) // per) if n_b > 1 else 0
            for sa in range(n_a):
                for sb in range(n_b):
                    pl.when((slab_a == sa) & (slab_b == sb))(functools.partial(step, refs[sa], refs[n_a + sb]))
        if nk > 1 and not (in_place and not has_add):
            @pl.when(k == nk - 1)
            def _():
                finish((o_ref if in_place else refs[-1])[...])

    in_specs = a_specs + b_specs + ([pl.BlockSpec((tm, tn), lambda i, j, k: (i, j))] if has_add else [])
    args = (*a_list, *b_list) + ((add,) if has_add else ())
    if after is not None:
        in_specs.append(pl.BlockSpec(memory_space=pl.ANY))
        args += (after,)
    return pl.pallas_call(
        body, name=name, grid=(m // tm, n // tn, nk),
        in_specs=in_specs, out_specs=o_spec,
        out_shape=jax.ShapeDtypeStruct(o_shape, out_dtype),
        scratch_shapes=[pltpu.VMEM((tm, tn), F32)] if nk > 1 and not in_place else [],
        compiler_params=_params("parallel", "parallel", "arbitrary"),
    )(*args)


def _shift_down(u, s):
    return jnp.where(_iota(u.shape, 0) >= s, pltpu.roll(u, s, 0), 0.0)


def _shift_up(u, s):
    t = u.shape[0]
    return jnp.where(_iota(u.shape, 0) < t - s, pltpu.roll(u, t - s, 0), 0.0)


SLAB = 8


def _rolled(u):
    return pltpu.roll(u, 2, 0), pltpu.roll(u, 1, 0)


def _conv(u, w, rolled=None):
    u2, u1 = _rolled(u) if rolled is None else rolled
    raw = w[0:1] * u2 + w[1:2] * u1 + w[2:3] * u
    head = u[0:SLAB]
    mended = w[0:1] * _shift_down(head, 2) + w[1:2] * _shift_down(head, 1) + w[2:3] * head
    return jnp.concatenate([mended, raw[SLAB:]], axis=0)


def _conv_t(dy, w):
    t = dy.shape[0]
    raw = w[2:3] * dy + w[1:2] * pltpu.roll(dy, t - 1, 0) + w[0:1] * pltpu.roll(dy, t - 2, 0)
    tail = dy[t - SLAB:]
    mended = w[2:3] * tail + w[1:2] * _shift_up(tail, 1) + w[0:1] * _shift_up(tail, 2)
    return jnp.concatenate([raw[:t - SLAB], mended], axis=0)


def _conv_dw(dy, u, rolled=None):
    t = dy.shape[0]
    u2, u1 = _rolled(u) if rolled is None else rolled
    head, tail = dy[0:SLAB], u[t - SLAB:]
    r = _iota(head.shape, 0)
    wrap2 = jnp.sum(jnp.where(r < 2, head * pltpu.roll(tail, 2, 0), 0.0), axis=0, keepdims=True)
    wrap1 = jnp.sum(jnp.where(r < 1, head * pltpu.roll(tail, 1, 0), 0.0), axis=0, keepdims=True)
    d0 = jnp.sum(dy * u2, axis=0, keepdims=True) - wrap2
    d1 = jnp.sum(dy * u1, axis=0, keepdims=True) - wrap1
    d2 = jnp.sum(dy * u, axis=0, keepdims=True)
    r3 = _iota((3, dy.shape[1]), 0)
    return jnp.where(r3 == 0, d0, jnp.where(r3 == 1, d1, d2))


def _sigmoid(x):
    return 0.5 * jnp.tanh(0.5 * x) + 0.5


def _sconv_fwd(proj, w_sc, t, wc):
    nb = wc // LANE

    def body(cb_ref, cc_ref, ch_ref, w_ref, y_ref):
        u = cc_ref[...] * ch_ref[...]
        y_ref[...] = (cb_ref[...] * _conv(u, w_ref[...])).astype(BF16)

    col = lambda off: pl.BlockSpec((t, LANE), lambda j: (0, j + off))
    return pl.pallas_call(
        body, name="sconv_fwd", grid=(nb,),
        in_specs=[col(0), col(nb), col(2 * nb), pl.BlockSpec((3, LANE), lambda j: (0, j))],
        out_specs=pl.BlockSpec((None, t, LANE), lambda j: (0, 0, j)),
        out_shape=jax.ShapeDtypeStruct((2, t, wc), BF16),
        compiler_params=_params("parallel"),
    )(proj, proj, proj, w_sc)


def _sconv_bwd(dy, proj, w_sc, t, wc):
    nb = wc // LANE

    def body(dy_ref, cb_ref, cc_ref, ch_ref, w_ref, dcb_ref, dcc_ref, dch_ref, dw_ref):
        cc, ch, w, d = cc_ref[...], ch_ref[...], w_ref[...], dy_ref[...]
        u = cc * ch
        ru = _rolled(u)
        dcb_ref[...] = (d * _conv(u, w, ru)).astype(BF16)
        dcu = d * cb_ref[...]
        dw_ref[...] = _conv_dw(dcu, u, ru)
        du = _conv_t(dcu, w)
        dcc_ref[...] = (du * ch).astype(BF16)
        dch_ref[...] = (du * cc).astype(BF16)

    col = lambda off: pl.BlockSpec((t, LANE), lambda j: (0, j + off))
    act = jax.ShapeDtypeStruct((t, wc), BF16)
    return pl.pallas_call(
        body, name="sconv_bwd", grid=(nb,),
        in_specs=[col(0), col(0), col(nb), col(2 * nb), pl.BlockSpec((3, LANE), lambda j: (0, j))],
        out_specs=[col(0), col(0), col(0), pl.BlockSpec((3, LANE), lambda j: (0, j))],
        out_shape=[act, act, act, jax.ShapeDtypeStruct((3, wc), F32)],
        compiler_params=_params("parallel"),
    )(dy, proj, proj, proj, w_sc)


def _gates_prep(proj, bias_tile, t, gate_tile):
    def body(g_ref, b_ref, o_ref):
        g = g_ref[...] + b_ref[...]
        lane = _iota(g.shape, 1)
        is_f = (lane >= NH) & (lane < 2 * NH)
        lf = jnp.minimum(g, 0.0) - jnp.log(1.0 + jnp.exp(-jnp.abs(g)))
        c = jnp.where(is_f, lf, 0.0)
        r = _iota(g.shape, 0) % CHUNK
        s = 1
        while s < CHUNK:
            c = c + jnp.where(r >= s, pltpu.roll(c, s, 0), 0.0)
            s *= 2
        o_ref[...] = jnp.where(is_f, c, jnp.where(lane < NH, g, 0.0))

    return pl.pallas_call(
        body, name="gates_prep", grid=(1,),
        in_specs=[pl.BlockSpec((t, LANE), lambda i: (0, gate_tile)), pl.BlockSpec((1, LANE), lambda i: (0, 0))],
        out_specs=pl.BlockSpec((t, LANE), lambda i: (0, 0)),
        out_shape=jax.ShapeDtypeStruct((t, LANE), F32),
        compiler_params=_params("arbitrary"),
    )(proj, bias_tile)


def _gates_bwd(dgate, proj, bias_tile, t, gate_tile):
    def body(dg_ref, g_ref, b_ref, o_ref, s_ref):
        g = g_ref[...] + b_ref[...]
        lane = _iota(g.shape, 1)
        r = _iota(g.shape, 0) % CHUNK
        dsig = 1.0 - _sigmoid(g)
        out = jnp.zeros(g.shape, F32)
        for h in range(NH):
            d = dg_ref[h]
            c = d
            s = 1
            while s < CHUNK:
                c = c + jnp.where(r + s < CHUNK, pltpu.roll(c, t - s, 0), 0.0)
                s *= 2
            di = jnp.broadcast_to(d[:, 0:1], g.shape)
            db = jnp.broadcast_to(c[:, 1:2], g.shape)
            out = out + jnp.where(lane == h, di, 0.0) + jnp.where(lane == NH + h, db * dsig, 0.0)
        o_ref[...] = out.astype(BF16)
        s_ref[...] = jnp.sum(out, axis=0, keepdims=True)

    return pl.pallas_call(
        body, name="gates_bwd", grid=(1,),
        in_specs=[pl.BlockSpec((NH, t, LANE), lambda i: (0, 0, 0)),
                  pl.BlockSpec((t, LANE), lambda i: (0, gate_tile)), pl.BlockSpec((1, LANE), lambda i: (0, 0))],
        out_specs=[pl.BlockSpec((t, LANE), lambda i: (0, 0)), pl.BlockSpec((1, LANE), lambda i: (0, 0))],
        out_shape=[jax.ShapeDtypeStruct((t, LANE), BF16), jax.ShapeDtypeStruct((1, LANE), F32)],
        compiler_params=_params("arbitrary"),
    )(dgate, proj, bias_tile)


def _in_turn(heads):
    while heads:
        heads = [g for g in heads if next(g, heads) is not heads]


def _chunk_gates(gc, gr, h, mprev):
    L = CHUNK
    icol, bcol = gc[:, h:h + 1], gc[:, h + NH:h + NH + 1]
    irow, brow = gr[h:h + 1, :], gr[h + NH:h + NH + 1, :]
    tri = _iota((L, L), 0) >= _iota((L, L), 1)
    log_d = jnp.where(tri, bcol - brow + irow, -jnp.inf)
    inter = bcol + mprev
    mt = jnp.maximum(inter, jnp.max(log_d, axis=1, keepdims=True))
    dw = jnp.exp(log_d - mt)
    iw = jnp.exp(inter - mt)
    g = brow[:, L - 1:L]
    wlog_col = g - bcol + icol
    wlog_row = g - brow + irow
    mnew = jnp.maximum(g + mprev, jnp.max(wlog_row, axis=1, keepdims=True))
    wcol = jnp.exp(wlog_col - mnew)
    decay = jnp.exp(g + mprev - mnew)
    return dw, iw, mt, wcol, decay, mnew


def _mlstm_fwd(proj, gcol, grow, t, wc, dh):
    nc = t // CHUNK
    wm = NH * dh
    assert wc == wm, (wc, wm)
    qoff = 3 * wc // wm
    scale = dh ** -0.5

    def body(q_ref, k_ref, v_ref, gc_ref, gr_ref, h_ref, cs_ref, ns_ref, c_s, n_s, m_s):
        @pl.when(pl.program_id(0) == 0)
        def _():
            c_s[...] = jnp.zeros_like(c_s)
            n_s[...] = jnp.zeros_like(n_s)
            m_s[...] = jnp.zeros_like(m_s)

        gc, gr = gc_ref[...], gr_ref[0]
        done = [None] * NH

        def head(h):
            cols = slice(h * dh, (h + 1) * dh)
            mprev = m_s[h, 0:1, 0:1]
            cprev = c_s[h]
            n8 = n_s[h]
            nprev = n8[0:1]
            qs = q_ref[:, cols] * scale
            k = k_ref[:, cols]
            qs_b, k_b, v_b = qs.astype(BF16), k.astype(BF16), v_ref[:, cols].astype(BF16)
            qk = _dot(qs_b, k_b, _NT)
            yield
            q_c = _dot(qs_b, cprev.astype(BF16))
            yield
            dw, iw, mt, wcol, decay, mnew = _chunk_gates(gc, gr, h, mprev)
            yield
            s = qk * dw
            wk = wcol * k
            num = _dot(s.astype(BF16), v_b) + iw * q_c
            yield
            c_new = decay * cprev + _dot(wk.astype(BF16), v_b, _TN)
            yield
            den = jnp.sum(s, axis=1, keepdims=True) + iw * jnp.sum(qs * nprev, axis=1, keepdims=True)
            done[h] = (cprev, jnp.where(_iota(n8.shape, 0) == 1, mprev, n8),
                       num / jnp.maximum(jnp.abs(den), jnp.exp(-mt)), c_new,
                       decay * n8 + jnp.sum(wk, axis=0, keepdims=True), mnew)

        _in_turn([head(h) for h in range(NH)])
        for h, (c_old, n_old, h_out, c_new, n_new, m_new) in enumerate(done):
            cs_ref[h] = c_old
            ns_ref[h] = n_old
            h_ref[:, h * dh:(h + 1) * dh] = h_out
            c_s[h] = c_new
            n_s[h] = n_new
            m_s[h] = jnp.broadcast_to(m_new, m_s.shape[1:])

    grp = lambda off: pl.BlockSpec((CHUNK, wm), lambda c: (c, qoff + off))
    return pl.pallas_call(
        body, name="mlstm_fwd", grid=(nc,),
        in_specs=[grp(0), grp(1), grp(2),
                  pl.BlockSpec((CHUNK, LANE), lambda c: (c, 0)),
                  pl.BlockSpec((1, 8, CHUNK), lambda c: (c, 0, 0))],
        out_specs=[pl.BlockSpec((CHUNK, wm), lambda c: (c, 0)),
                   pl.BlockSpec((NH, None, dh, dh), lambda c: (0, c, 0, 0)),
                   pl.BlockSpec((NH, None, 8, dh), lambda c: (0, c, 0, 0))],
        out_shape=[jax.ShapeDtypeStruct((t, wm), F32),
                   jax.ShapeDtypeStruct((NH, nc, dh, dh), F32),
                   jax.ShapeDtypeStruct((NH, nc, 8, dh), F32)],
        scratch_shapes=[pltpu.VMEM((NH, dh, dh), F32), pltpu.VMEM((NH, 8, dh), F32), pltpu.VMEM((NH, 8, LANE), F32)],
        compiler_params=_params("arbitrary"),
    )(proj, proj, proj, gcol, grow)


def _mlstm_bwd(proj, gcol, grow, hval, dh_in, cs, ns, t, wc, dh):
    nc = t // CHUNK
    wm = NH * dh
    assert wc == wm, (wc, wm)
    qoff = 3 * wc // wm
    scale = dh ** -0.5
    L = CHUNK

    def body(q_ref, k_ref, v_ref, gc_ref, gr_ref, h_ref, dh_ref, cs_ref, ns_ref,
             dq_ref, dk_ref, dv_ref, dg_ref, dc_s, dn_s):
        @pl.when(pl.program_id(0) == 0)
        def _():
            dc_s[...] = jnp.zeros_like(dc_s)
            dn_s[...] = jnp.zeros_like(dn_s)

        gc, gr = gc_ref[...], gr_ref[0]
        eye = _iota((L, L), 0) == _iota((L, L), 1)
        lane = _iota((L, LANE), 1)
        last = _iota((L, 1), 0) == L - 1
        done = [None] * NH

        def head(h):
            cols = slice(h * dh, (h + 1) * dh)
            ns8 = ns_ref[h]
            nprev = ns8[0:1]
            mprev = ns8[1:2, 0:1]
            cprev = cs_ref[h]
            dcn = dc_s[h]
            dn8 = dn_s[h]
            dnn = dn8[0:1]

            qs = q_ref[:, cols] * scale
            k = k_ref[:, cols]
            qs_b, k_b, v_b = qs.astype(BF16), k.astype(BF16), v_ref[:, cols].astype(BF16)
            qk = _dot(qs_b, k_b, _NT)
            yield
            dw, iw, mt, wcol, decay, _ = _chunk_gates(gc, gr, h, mprev)
            yield
            s = qk * dw
            den = jnp.sum(s, axis=1, keepdims=True) + iw * jnp.sum(qs * nprev, axis=1, keepdims=True)
            emt = jnp.exp(-mt)
            r = 1.0 / jnp.maximum(jnp.abs(den), emt)
            dout = dh_ref[:, cols]
            dnum = dout * r
            dden = (-jnp.sum(dout * h_ref[:, cols], axis=1, keepdims=True) * r
                    * jnp.where(jnp.abs(den) > emt, jnp.sign(den), 0.0))
            dnum_b = dnum.astype(BF16)
            cprev_b = cprev.astype(BF16)
            dcn_b = dcn.astype(BF16)
            yield

            g_raw = _dot(dnum_b, v_b, _NT)
            yield
            q_inter = _dot(dnum_b, cprev_b, _NT)
            yield
            k_raw = _dot(v_b, dcn_b, _NT)
            yield
            gd = (g_raw + dden) * dw
            gd_b = gd.astype(BF16)
            dqs_inter = iw * (q_inter + dden * nprev)
            dk_inter = wcol * (k_raw + dnn)
            wk = wcol * k
            iq = iw * qs
            dqs = _dot(gd_b, k_b) + dqs_inter
            yield
            dk = _dot(gd_b, qs_b, _TN) + dk_inter
            yield
            dv = _dot(s.astype(BF16), dnum_b, _TN) + _dot(wk.astype(BF16), dcn_b)
            yield
            dc_new = decay * dcn + _dot(iq.astype(BF16), dnum_b, _TN)
            yield

            e = gd * qk
            e_cols = jnp.sum(jnp.where(eye, jnp.sum(e, axis=0, keepdims=True), 0.0), axis=1, keepdims=True)
            yield
            k_inter = jnp.sum(k * dk_inter, axis=1, keepdims=True)
            rq = jnp.sum(e, axis=1, keepdims=True) + jnp.sum(qs * dqs_inter, axis=1, keepdims=True)
            rk = e_cols + k_inter
            hsum = jnp.sum(k_inter, axis=0, keepdims=True)
            jdec = decay * (jnp.sum(jnp.sum(dcn * cprev, axis=1, keepdims=True), axis=0, keepdims=True)
                            + jnp.sum(dnn * nprev, axis=1, keepdims=True))
            db = rq - rk + jnp.where(last, hsum + jdec, 0.0)
            done[h] = (jnp.where(lane == 0, rk, jnp.where(lane == 1, db, 0.0)),
                       (dqs * scale).astype(BF16), dk.astype(BF16), dv.astype(BF16), dc_new,
                       decay * dn8 + jnp.sum(iq * dden, axis=0, keepdims=True))

        _in_turn([head(h) for h in range(NH)])
        for h, (dgate, dq, dk, dv, dc_new, dn_new) in enumerate(done):
            cols = slice(h * dh, (h + 1) * dh)
            dg_ref[h] = dgate
            dq_ref[:, cols] = dq
            dk_ref[:, cols] = dk
            dv_ref[:, cols] = dv
            dc_s[h] = dc_new
            dn_s[h] = dn_new

    rc = lambda c: nc - 1 - c
    grp = lambda off: pl.BlockSpec((L, wm), lambda c: (rc(c), qoff + off))
    hm = pl.BlockSpec((L, wm), lambda c: (rc(c), 0))
    act = jax.ShapeDtypeStruct((t, wm), BF16)
    return pl.pallas_call(
        body, name="mlstm_bwd", grid=(nc,),
        in_specs=[grp(0), grp(1), grp(2),
                  pl.BlockSpec((L, LANE), lambda c: (rc(c), 0)),
                  pl.BlockSpec((1, 8, L), lambda c: (rc(c), 0, 0)),
                  hm, hm,
                  pl.BlockSpec((NH, None, dh, dh), lambda c: (0, rc(c), 0, 0)),
                  pl.BlockSpec((NH, None, 8, dh), lambda c: (0, rc(c), 0, 0))],
        out_specs=[hm, hm, hm, pl.BlockSpec((NH, L, LANE), lambda c: (0, rc(c), 0))],
        out_shape=[act, act, act, jax.ShapeDtypeStruct((NH, t, LANE), F32)],
        scratch_shapes=[pltpu.VMEM((NH, dh, dh), F32), pltpu.VMEM((NH, 8, dh), F32)],
        compiler_params=_params("arbitrary"),
    )(proj, proj, proj, gcol, grow, hval, dh_in, cs, ns)


def _head_norm(hv):
    mu = jnp.mean(hv, axis=1, keepdims=True)
    hc = hv - mu
    rstd = lax.rsqrt(jnp.mean(hc * hc, axis=1, keepdims=True) + HN_EPS)
    return hc * rstd, rstd


def _hnorm_fwd(hval, proj, gain, y, t, wc, dh, tr=512):
    ooff = 3 * wc // dh + 3 * NH
    tr = min(tr, t)

    def body(h_ref, o_ref, g_ref, y_in, y_ref):
        hhat, _ = _head_norm(h_ref[...])
        y_ref[...] = (_sigmoid(o_ref[...]) * hhat * g_ref[...]).astype(BF16)

    return pl.pallas_call(
        body, name="hnorm_fwd", grid=(t // tr, NH),
        in_specs=[pl.BlockSpec((tr, dh), lambda i, h: (i, h)),
                  pl.BlockSpec((tr, dh), lambda i, h: (i, ooff + h)),
                  pl.BlockSpec((1, dh), lambda i, h: (0, h)),
                  pl.BlockSpec(memory_space=pl.ANY)],
        out_specs=pl.BlockSpec((None, tr, dh), lambda i, h: (1, i, h)),
        out_shape=jax.ShapeDtypeStruct(y.shape, BF16),
        input_output_aliases={3: 0},
        compiler_params=_params("parallel", "parallel"),
    )(hval, proj, gain, y)


def _hnorm_bwd(dy, hval, proj, gain, t, wc, dh, tr=512):
    ooff = 3 * wc // dh + 3 * NH
    tr = min(tr, t)
    yoff = wc // dh

    def body(dy_ref, h_ref, o_ref, g_ref, do_ref, dh_ref, dg_ref):
        i = pl.program_id(1)
        hhat, rstd = _head_norm(h_ref[...])
        gain_v = g_ref[...]
        sig = _sigmoid(o_ref[...])
        d = dy_ref[...]
        do_ref[...] = (d * hhat * gain_v * sig * (1.0 - sig)).astype(BF16)
        dhn = d * sig
        part = jnp.sum(dhn * hhat, axis=0, keepdims=True)

        @pl.when(i == 0)
        def _():
            dg_ref[...] = part

        @pl.when(i > 0)
        def _():
            dg_ref[...] += part

        dhat = dhn * gain_v
        dh_ref[...] = rstd * (dhat - jnp.mean(dhat, axis=1, keepdims=True)
                              - hhat * jnp.mean(dhat * hhat, axis=1, keepdims=True))

    blk = lambda off: pl.BlockSpec((tr, dh), lambda h, i: (i, off + h))
    return pl.pallas_call(
        body, name="hnorm_bwd", grid=(NH, t // tr),
        in_specs=[blk(yoff), blk(0), blk(ooff), pl.BlockSpec((1, dh), lambda h, i: (0, h))],
        out_specs=[blk(0), blk(0), pl.BlockSpec((1, dh), lambda h, i: (0, h))],
        out_shape=[jax.ShapeDtypeStruct((t, NH * dh), BF16), jax.ShapeDtypeStruct((t, NH * dh), F32),
                   jax.ShapeDtypeStruct((1, NH * dh), F32)],
        compiler_params=_params("parallel", "arbitrary"),
    )(dy, hval, proj, gain)


def _ln_stats(z):
    mu = jnp.mean(z, axis=1, keepdims=True)
    zc = z - mu
    rstd = lax.rsqrt(jnp.mean(zc * zc, axis=1, keepdims=True) + LN_EPS)
    return zc * rstd, rstd


def _ln_bwd(dy, xhat, rstd, g):
    dxh = dy * g
    return rstd * (dxh - jnp.mean(dxh, axis=1, keepdims=True) - xhat * jnp.mean(dxh * xhat, axis=1, keepdims=True))


def _accum(ref, i, part):
    @pl.when(i == 0)
    def _():
        ref[...] = part

    @pl.when(i > 0)
    def _():
        ref[...] += part


def _ln1_fwd(x, mix, g, b, tr=256):
    t, d = x.shape

    def body(x_ref, m_ref, g_ref, b_ref, xh_ref, rs_ref, xb_ref):
        xhat, rstd = _ln_stats(ALPHA * x_ref[...] + m_ref[...])
        xh_ref[...] = xhat
        rs_ref[...] = rstd
        xb_ref[...] = (xhat * g_ref[...] + b_ref[...]).astype(BF16)

    row = pl.BlockSpec((tr, d), lambda i: (i, 0))
    vec = pl.BlockSpec((1, d), lambda i: (0, 0))
    return pl.pallas_call(
        body, name="ln1_fwd", grid=(t // tr,),
        in_specs=[row, row, vec, vec],
        out_specs=[row, pl.BlockSpec((tr, 1), lambda i: (i, 0)), row],
        out_shape=[jax.ShapeDtypeStruct((t, d), F32), jax.ShapeDtypeStruct((t, 1), F32),
                   jax.ShapeDtypeStruct((t, d), BF16)],
        compiler_params=_params("parallel"),
    )(x, mix, g, b)


def _ln2_loss(xhat1, g1, b1, ff, target, g2, b2, tr=256):
    t, d = ff.shape

    def body(xh_ref, g1_ref, b1_ref, f_ref, t_ref, g_ref, b_ref, dz_ref, dzb_ref, dg_ref, db_ref, l_ref):
        i = pl.program_id(0)
        x1 = xh_ref[...] * g1_ref[...] + b1_ref[...]
        xhat, rstd = _ln_stats(ALPHA * x1 + f_ref[...])
        gv = g_ref[...]
        e = xhat * gv + b_ref[...] - t_ref[...]
        lsum = jnp.sum(jnp.sum(e * e, axis=1, keepdims=True), axis=0, keepdims=True) * (0.5 / d)
        dy = e * (1.0 / d)
        _accum(dg_ref, i, jnp.sum(dy * xhat, axis=0, keepdims=True))
        _accum(db_ref, i, jnp.sum(dy, axis=0, keepdims=True))
        _accum(l_ref, i, jnp.broadcast_to(lsum, l_ref.shape))
        dz = _ln_bwd(dy, xhat, rstd, gv)
        dz_ref[...] = dz
        dzb_ref[...] = dz.astype(BF16)

    row = pl.BlockSpec((tr, d), lambda i: (i, 0))
    vec = pl.BlockSpec((1, d), lambda i: (0, 0))
    return pl.pallas_call(
        body, name="ln2_loss", grid=(t // tr,),
        in_specs=[row, vec, vec, row, row, vec, vec],
        out_specs=[row, row, vec, vec, pl.BlockSpec((8, LANE), lambda i: (0, 0))],
        out_shape=[jax.ShapeDtypeStruct((t, d), F32), jax.ShapeDtypeStruct((t, d), BF16),
                   jax.ShapeDtypeStruct((1, d), F32), jax.ShapeDtypeStruct((1, d), F32),
                   jax.ShapeDtypeStruct((8, LANE), F32)],
        compiler_params=_params("arbitrary"),
    )(xhat1, g1, b1, ff, target, g2, b2)


def _ln1_bwd(dz2, dffn, xhat1, rstd1, g1, tr=256):
    t, d = dz2.shape

    def body(a_ref, f_ref, xh_ref, rs_ref, g_ref, dz_ref, dzb_ref, dg_ref, db_ref):
        i = pl.program_id(0)
        dy = ALPHA * a_ref[...] + f_ref[...]
        xhat = xh_ref[...]
        _accum(dg_ref, i, jnp.sum(dy * xhat, axis=0, keepdims=True))
        _accum(db_ref, i, jnp.sum(dy, axis=0, keepdims=True))
        dz = _ln_bwd(dy, xhat, rs_ref[...], g_ref[...])
        dz_ref[...] = dz
        dzb_ref[...] = dz.astype(BF16)

    row = pl.BlockSpec((tr, d), lambda i: (i, 0))
    vec = pl.BlockSpec((1, d), lambda i: (0, 0))
    return pl.pallas_call(
        body, name="ln1_bwd", grid=(t // tr,),
        in_specs=[row, row, row, pl.BlockSpec((tr, 1), lambda i: (i, 0)), vec],
        out_specs=[row, row, vec, vec],
        out_shape=[jax.ShapeDtypeStruct((t, d), F32), jax.ShapeDtypeStruct((t, d), BF16),
                   jax.ShapeDtypeStruct((1, d), F32), jax.ShapeDtypeStruct((1, d), F32)],
        compiler_params=_params("arbitrary"),
    )(dz2, dffn, xhat1, rstd1, g1)


def _ffn_act_fwd(hid0, w_fc, b_fc, t, dff):
    nb = dff // LANE

    def body(hv_ref, hg_ref, wv_ref, wg_ref, bv_ref, bg_ref, a_ref):
        val = _conv(hv_ref[...], wv_ref[...]) + bv_ref[...]
        gate = _conv(hg_ref[...], wg_ref[...]) + bg_ref[...]
        a_ref[...] = (gate * _sigmoid(gate) * val).astype(BF16)

    col = lambda off: pl.BlockSpec((t, LANE), lambda j: (0, j + off))
    w3 = lambda off: pl.BlockSpec((3, LANE), lambda j: (0, j + off))
    w1 = lambda off: pl.BlockSpec((1, LANE), lambda j: (0, j + off))
    return pl.pallas_call(
        body, name="ffn_act_fwd", grid=(nb,),
        in_specs=[col(0), col(nb), w3(0), w3(nb), w1(0), w1(nb)],
        out_specs=col(0),
        out_shape=jax.ShapeDtypeStruct((t, dff), BF16),
        compiler_params=_params("parallel"),
    )(hid0, hid0, w_fc, w_fc, b_fc, b_fc)


def _ffn_act_bwd(da, hid0, w_fc, b_fc, t, dff):
    nb = dff // LANE

    def body(da_ref, hv_ref, hg_ref, wv_ref, wg_ref, bv_ref, bg_ref,
             dhv_ref, dhg_ref, dwv_ref, dwg_ref, dbv_ref, dbg_ref):
        hv, hg, wv, wg = hv_ref[...], hg_ref[...], wv_ref[...], wg_ref[...]
        rv, rg = _rolled(hv), _rolled(hg)
        val = _conv(hv, wv, rv) + bv_ref[...]
        gate = _conv(hg, wg, rg) + bg_ref[...]
        sig = _sigmoid(gate)
        d = da_ref[...]
        dsig = d * sig
        dval = dsig * gate
        dgate = dsig * val * (1.0 + gate * (1.0 - sig))
        dhv_ref[...] = _conv_t(dval, wv).astype(BF16)
        dhg_ref[...] = _conv_t(dgate, wg).astype(BF16)
        dwv_ref[...] = _conv_dw(dval, hv, rv)
        dwg_ref[...] = _conv_dw(dgate, hg, rg)
        dbv_ref[...] = jnp.sum(dval, axis=0, keepdims=True)
        dbg_ref[...] = jnp.sum(dgate, axis=0, keepdims=True)

    col = lambda off: pl.BlockSpec((t, LANE), lambda j: (0, j + off))
    w3 = lambda off: pl.BlockSpec((3, LANE), lambda j: (0, j + off))
    w1 = lambda off: pl.BlockSpec((1, LANE), lambda j: (0, j + off))
    s3 = jax.ShapeDtypeStruct((3, dff), F32)
    s1 = jax.ShapeDtypeStruct((1, dff), F32)
    return pl.pallas_call(
        body, name="ffn_act_bwd", grid=(nb,),
        in_specs=[col(0), col(0), col(nb), w3(0), w3(nb), w1(0), w1(nb)],
        out_specs=[col(0), col(0), w3(0), w3(0), w1(0), w1(0)],
        out_shape=[jax.ShapeDtypeStruct((t, dff), BF16)] * 2 + [s3, s3, s1, s1],
        compiler_params=_params("parallel"),
    )(da, hid0, hid0, w_fc, w_fc, b_fc, b_fc)


class _Ready:
    def __init__(self, **weights):
        self.weights = weights

    def begin(self, after):
        return None

    def forward(self, name, after):
        return None

    def get(self, name, after):
        return self.weights[name]


class _Kept:
    def __init__(self):
        self.grads = {}

    def start(self, name, grad):
        self.grads[name] = grad
        return None

    def relay(self, name, after):
        return None

    def meanwhile(self, small, loss, after):
        return None


def _behind(a, token):
    return a if token is None else a + token[0:1, 0:1].reshape((1,) * a.ndim)


def _local_step(x, target, w_in, b_gates, w_sc, gain, w_out, ln1_g, ln1_b, w_up, w_fc, b_fc, w_down, ln2_g, ln2_b,
                gx=None, wx=None):
    t, d = x.shape
    wc = d // 2
    dh = (d - wc) // NH
    wm = NH * dh
    dff = w_fc.shape[1] // 2
    if wx is None:
        wx = _Ready(w_out=w_out, w_up=w_up, w_down=w_down)
    ninp = w_in.shape[0]
    nin = 3 * wc + 4 * wm
    gate_tile = nin // LANE
    nc = t // CHUNK
    bias_tile = jnp.pad(b_gates, ((0, 0), (0, LANE - 2 * NH)))

    x_b = _behind(x, wx.begin(w_in)).astype(BF16)
    proj = _matmul(x_b, w_in, "nt", F32, "proj", tm=512, tn=2688, tk=d, after=wx.begin(w_in))
    y = _sconv_fwd(proj, w_sc, t, wc)
    gcol = _gates_prep(proj, bias_tile, t, gate_tile)
    grow = gcol[:, :8].T.reshape(8, nc, CHUNK).transpose(1, 0, 2)
    hval, cs, ns = _mlstm_fwd(proj, gcol, grow, t, wc, dh)
    y = _hnorm_fwd(hval, proj, gain, y, t, wc, dh)
    tok = wx.forward("w_out", y)
    w_out = wx.get("w_out", tok)
    mix = _matmul(y, w_out, "nn", F32, "out_proj", tm=512, tn=1024, tk=wc, a_blocked=True, after=tok)
    xhat1, rstd1, x1_b = _ln1_fwd(x, mix, _behind(ln1_g, wx.forward("w_up", mix)), ln1_b)
    w_up = wx.get("w_up", x1_b)
    wsl = w_up.shape[2]
    hid0 = _matmul(x1_b, w_up, "nn", F32, "ffn_up", tm=512, tn=wsl, tk=d, b_blocked=True)
    act = _ffn_act_fwd(hid0, w_fc, _behind(b_fc, wx.forward("w_down", hid0)), t, dff)
    w_down = wx.get("w_down", act)
    ff = _matmul(act, w_down, "nn", F32, "ffn_down", tm=1024, tn=512, tk=dff)
    dz2, dz2_b, d_ln2_g, d_ln2_b, loss = _ln2_loss(xhat1, ln1_g, ln1_b, ff, target, ln2_g, ln2_b)

    if gx is None:
        gx = _Kept()
    d_w_down = _matmul(act, dz2_b, "tn", BF16, "ffn_down_dw", tm=512, tn=1024, tk=t)
    d_act = _matmul(dz2_b, w_down, "nt", F32, "ffn_down_dx", tm=1024, tn=512, tk=d, after=gx.start("w_down", d_w_down))
    *d_hid0, dwv, dwg, dbv, dbg = _ffn_act_bwd(d_act, hid0, w_fc, _behind(b_fc, gx.relay("w_down", d_act)), t, dff)
    d_w_fc = jnp.concatenate([dwv, dwg], axis=1)
    d_b_fc = jnp.concatenate([dbv, dbg], axis=1)
    d_hid0 = tuple(d_hid0[:2])
    d_w_up = _matmul(x1_b, d_hid0, "tn", BF16, "ffn_up_dw", tm=512, tn=wsl, tk=t, o_width=wsl)
    d_x1_ffn = _matmul(d_hid0, w_up, "nt", F32, "ffn_up_dx", tm=1024, tn=1024, tk=wsl, b_blocked=True,
                       after=gx.start("w_up", d_w_up))
    dz1, dz1_b, d_ln1_g, d_ln1_b = _ln1_bwd(dz2, d_x1_ffn, xhat1, rstd1, _behind(ln1_g, gx.relay("w_up", d_x1_ffn)))

    d_w_out = _matmul(y, dz1_b, "tn", BF16, "out_proj_dw", tm=512, tn=1024, tk=t, a_blocked=True)
    dy = _matmul(dz1_b, w_out, "nt", F32, "out_proj_dx", tm=512, tn=1024, tk=d, after=gx.start("w_out", d_w_out))
    dcb, dcc, dch, d_w_sc = _sconv_bwd(dy, proj, _behind(w_sc, gx.relay("w_out", dy)), t, wc)
    d_o, d_hval, d_gain = _hnorm_bwd(dy, hval, proj, gain, t, wc, dh)
    dq, dk, dv, dgate = _mlstm_bwd(proj, gcol, grow, hval, d_hval, cs, ns, t, wc, dh)
    dgt, d_b_gates = _gates_bwd(dgate, proj, bias_tile, t, gate_tile)
    pad = jnp.zeros((t, ninp - nin - LANE), BF16)
    d_proj = jnp.concatenate([dcb, dcc, dch, dq, dk, dv, d_o, dgt, pad], axis=1)
    d_w_in = _matmul(d_proj, x_b, "tn", BF16, "proj_dw", tm=IN_SLAB, tn=1024, tk=t)
    small = dict(b_gates=d_b_gates[:, :2 * NH], w_sc_conv=d_w_sc, mh_gain=d_gain, ln1_g=d_ln1_g, ln1_b=d_ln1_b,
                 w_ffn_conv=d_w_fc, b_ffn_conv=d_b_fc, ln2_g=d_ln2_g, ln2_b=d_ln2_b)
    token = gx.start("w_in", d_w_in.reshape(ninp // IN_SLAB, IN_SLAB, d))
    token = gx.relay("w_in", gx.meanwhile(small, loss, token))
    grad_x = _matmul(d_proj, w_in, "nn", F32, "proj_dx", tm=512, tn=512, tk=ninp, add=dz1, add_scale=ALPHA, after=token)
    return loss, grad_x, small, gx


HBM = pl.BlockSpec(memory_space=pltpu.HBM)


def _place():
    return lax.axis_index("x"), lax.axis_index("y"), lax.axis_index("c")


def _index(p):
    return 4 * p[0] + 2 * p[1] + p[2]


def _all_gather(arrs, name):
    n = len(arrs)

    def body(*refs):
        ins, outs = refs[:n], refs[n:2 * n]
        send_sems, recv_sems, local_sems = refs[2 * n:]
        x, y, c = _place()
        me, sibling = (x, y, c), (x, y, 1 - c)
        chips = [(1 - x, y), (x, 1 - y), (1 - x, 1 - y)]

        def copy(a, k, block, to, own=False):
            dst = outs[a].at[_index(block)]
            return pltpu.make_async_remote_copy(
                src_ref=ins[a] if own else dst, dst_ref=dst,
                send_sem=send_sems.at[k * n + a], recv_sem=recv_sems.at[k * n + a],
                device_id=to, device_id_type=MESH)

        mine = [pltpu.make_async_copy(ins[a], outs[a].at[_index(me)], local_sems.at[a]) for a in range(n)]
        for cp in mine:
            cp.start()
        first = []
        for a in range(n):
            first.append(copy(a, 0, me, sibling, own=True))
            first += [copy(a, 1 + j, me, (*chip, c), own=True) for j, chip in enumerate(chips)]
        for cp in first:
            cp.start()
        passed = []
        for j, chip in enumerate(chips):
            for a in range(n):
                copy(a, 1 + j, (*chip, c), me).wait_recv()
                cp = copy(a, 4 + j, (*chip, c), sibling)
                cp.start()
                passed.append(cp)
        for a in range(n):
            copy(a, 0, sibling, me).wait_recv()
            for j, chip in enumerate(chips):
                copy(a, 4 + j, (*chip, 1 - c), me).wait_recv()
        for cp in first + passed:
            cp.wait_send()
        for cp in mine:
            cp.wait()

    return pl.pallas_call(
        body, name=name, in_specs=[HBM] * n, out_specs=[HBM] * n,
        out_shape=[jax.ShapeDtypeStruct((N_DEV,) + a.shape, a.dtype) for a in arrs],
        scratch_shapes=[pltpu.SemaphoreType.DMA((7 * n,)), pltpu.SemaphoreType.DMA((7 * n,)),
                        pltpu.SemaphoreType.DMA((n,))],
    )(*arrs)


SEM = pl.BlockSpec(memory_space=pltpu.SEMAPHORE)
EFFECT = pltpu.SideEffectType.DATAFLOW_SIDE_EFFECTING


def _chips(x, y):
    return [(1 - x, y), (x, 1 - y), (1 - x, 1 - y)]


N_CHIP = N_DEV // 2


def _pair_route(x, y, c):
    return [((x, y, 1 - c), 2 * q + (1 - c), q, q) for q in range(N_CHIP)]


def _chip_route(x, y, c):
    mine = 2 * x + y
    return [((*chip, c), 2 * chip[0] + chip[1], mine, 2 * chip[0] + chip[1]) for chip in _chips(x, y)]


def _exchange_pieces(g_ref, land_ref, width, tail):
    if not tail:
        return [(lambda i: g_ref.at[i], lambda s: land_ref.at[s])]
    return [(lambda i: g_ref.at[i], lambda s: land_ref.at[s, pl.ds(0, width), :]),
            (lambda i: g_ref.at[i + 1, pl.ds(0, IN_TAIL), :], lambda s: land_ref.at[s, pl.ds(width, IN_TAIL), :])]


def _exchange_start(grad, route, tail, name):
    width = grad.shape[1]
    n_p = 2 if tail else 1
    n_c = len(route(0, 0, 0))
    land_shape = (N_CHIP, width + (IN_TAIL if tail else 0), grad.shape[2])

    def body(g_ref, land_ref, send_sems, recv_sems, g_thru, land_thru, token):
        for j, (peer, slab, slot, _) in enumerate(route(*_place())):
            for p, (src, dst) in enumerate(_exchange_pieces(g_ref, land_ref, width, tail)):
                pltpu.make_async_remote_copy(src_ref=src(slab), dst_ref=dst(slot), send_sem=send_sems.at[j * n_p + p],
                                             recv_sem=recv_sems.at[j * n_p + p], device_id=peer,
                                             device_id_type=MESH).start()
        token[...] = jnp.zeros_like(token)

    return pl.pallas_call(
        body, name=name,
        out_shape=(pltpu.SemaphoreType.DMA((n_c * n_p,)), pltpu.SemaphoreType.DMA((n_c * n_p,)),
                   pltpu.HBM(grad.shape, grad.dtype), pltpu.HBM(land_shape, grad.dtype),
                   jax.ShapeDtypeStruct((8, LANE), F32)),
        in_specs=(HBM, HBM), out_specs=(SEM, SEM, HBM, HBM, pl.BlockSpec(memory_space=pltpu.VMEM)),
        input_output_aliases={0: 2, 1: 3},
        compiler_params=pltpu.CompilerParams(has_side_effects=EFFECT),
    )(pltpu.with_memory_space_constraint(grad, pltpu.HBM),
      pltpu.with_memory_space_constraint(lax.empty(land_shape, grad.dtype), pltpu.HBM))


def _exchange_wait(send_sems, recv_sems, g_thru, land_thru, after, route, tail, name):
    width = g_thru.shape[1]
    n_p = 2 if tail else 1

    def body(g_ref, land_ref, send_sems, recv_sems, after_ref, g_dead, got_ref):
        for j, (peer, slab, _, slot) in enumerate(route(*_place())):
            for p, (src, dst) in enumerate(_exchange_pieces(g_ref, land_ref, width, tail)):
                cp = pltpu.make_async_remote_copy(src_ref=src(slab), dst_ref=dst(slot),
                                                  send_sem=send_sems.at[j * n_p + p], recv_sem=recv_sems.at[j * n_p + p],
                                                  device_id=peer, device_id_type=MESH)
                cp.wait_send()
                cp.wait_recv()

    return pl.pallas_call(
        body, name=name,
        out_shape=(pltpu.HBM(g_thru.shape, g_thru.dtype), pltpu.HBM(land_thru.shape, land_thru.dtype)),
        in_specs=(HBM, HBM, SEM, SEM, pl.BlockSpec(memory_space=pl.ANY)), out_specs=(HBM, HBM),
        input_output_aliases={0: 0, 1: 1},
        compiler_params=pltpu.CompilerParams(has_side_effects=EFFECT),
    )(g_thru, land_thru, send_sems, recv_sems, after)


def _pair_add(grad, pair, core, tail, name):
    rows, cols = grad.shape[1], grad.shape[2]
    total = pair.shape[1]

    def body(core_ref, *refs):
        if tail:
            g_ref, t_ref, p_ref, o_ref = refs
            o_ref[0:rows, :] = (g_ref[...].astype(F32) + p_ref[0:rows, :].astype(F32)).astype(BF16)
            o_ref[rows:total, :] = (t_ref[...].astype(F32) + p_ref[rows:total, :].astype(F32)).astype(BF16)
        else:
            g_ref, p_ref, o_ref = refs
            o_ref[...] = (g_ref[...].astype(F32) + p_ref[...].astype(F32)).astype(BF16)

    if tail:
        tc = _fit(cols, 512)
        grid = (N_CHIP, cols // tc)
        slab = pl.BlockSpec((None, total, tc), lambda q, i, core_ref: (q, 0, i))
        in_specs = [pl.BlockSpec((None, rows, tc), lambda q, i, core_ref: (2 * q + core_ref[0], 0, i)),
                    pl.BlockSpec((None, IN_TAIL, tc), lambda q, i, core_ref: (2 * q + core_ref[0] + 1, 0, i))]
    else:
        tr = _rows(rows, 1024)
        grid = (N_CHIP, rows // tr)
        slab = pl.BlockSpec((None, tr, cols), lambda q, i, core_ref: (q, i, 0))
        in_specs = [pl.BlockSpec((None, tr, cols), lambda q, i, core_ref: (2 * q + core_ref[0], i, 0))]
    return pl.pallas_call(
        body, name=name,
        grid_spec=pltpu.PrefetchScalarGridSpec(num_scalar_prefetch=1, grid=grid,
                                               in_specs=in_specs + [slab], out_specs=slab),
        out_shape=jax.ShapeDtypeStruct(pair.shape, BF16),
        compiler_params=_params("parallel", "parallel"),
    )(core, *([grad, grad] if tail else [grad]), pair)


def _gather_start(blocks, after, name, spare=()):
    n = len(blocks)
    lands = [(N_DEV + (a in spare),) + b.shape for a, b in enumerate(blocks)]

    def body(*refs):
        b_refs, land_refs = refs[:n], refs[n:2 * n]
        send_sems, recv_sems = refs[2 * n + 1:3 * n + 1], refs[3 * n + 1:4 * n + 1]
        token = refs[-1]
        x, y, c = _place()
        me = _index((x, y, c))
        for a in range(n):
            for k, to in enumerate([(x, y, 1 - c)] + [(*chip, c) for chip in _chips(x, y)]):
                pltpu.make_async_remote_copy(src_ref=b_refs[a], dst_ref=land_refs[a].at[me], send_sem=send_sems[a].at[k],
                                             recv_sem=recv_sems[a].at[k], device_id=to, device_id_type=MESH).start()
        token[...] = jnp.zeros_like(token)

    sems = [pltpu.SemaphoreType.DMA((4,))] * n
    out = pl.pallas_call(
        body, name=name,
        out_shape=(*sems, *sems, *[pltpu.HBM(b.shape, b.dtype) for b in blocks],
                   *[pltpu.HBM(s, b.dtype) for s, b in zip(lands, blocks)], jax.ShapeDtypeStruct((8, LANE), F32)),
        in_specs=(*[HBM] * (2 * n), pl.BlockSpec(memory_space=pl.ANY)),
        out_specs=(*[SEM] * (2 * n), *[HBM] * (2 * n), pl.BlockSpec(memory_space=pltpu.VMEM)),
        input_output_aliases={i: 2 * n + i for i in range(2 * n)},
        compiler_params=pltpu.CompilerParams(has_side_effects=EFFECT),
    )(*[pltpu.with_memory_space_constraint(b, pltpu.HBM) for b in blocks],
      *[pltpu.with_memory_space_constraint(lax.empty(s, b.dtype), pltpu.HBM) for s, b in zip(lands, blocks)], after)
    return [(out[a], out[n + a], out[2 * n + a], out[3 * n + a]) for a in range(n)], out[-1]


def _gather_forward(send_sems, recv_sems, b_thru, land_thru, after, name):
    def body(b_ref, land_ref, send_sems, recv_sems, after_ref, b_dead, land_out, send2, recv2, token):
        x, y, c = _place()
        sibling = (x, y, 1 - c)
        for k, frm in enumerate([sibling] + [(*chip, c) for chip in _chips(x, y)]):
            cp = pltpu.make_async_remote_copy(src_ref=b_ref, dst_ref=land_ref.at[_index(frm)], send_sem=send_sems.at[k],
                                              recv_sem=recv_sems.at[k], device_id=frm, device_id_type=MESH)
            cp.wait_send()
            cp.wait_recv()
        for j, chip in enumerate(_chips(x, y)):
            slot = land_ref.at[_index((*chip, c))]
            pltpu.make_async_remote_copy(src_ref=slot, dst_ref=slot, send_sem=send2.at[j], recv_sem=recv2.at[j],
                                         device_id=sibling, device_id_type=MESH).start()
        token[...] = jnp.zeros_like(token)

    return pl.pallas_call(
        body, name=name,
        out_shape=(pltpu.HBM(b_thru.shape, b_thru.dtype), pltpu.HBM(land_thru.shape, land_thru.dtype),
                   pltpu.SemaphoreType.DMA((3,)), pltpu.SemaphoreType.DMA((3,)), jax.ShapeDtypeStruct((8, LANE), F32)),
        in_specs=(HBM, HBM, SEM, SEM, pl.BlockSpec(memory_space=pl.ANY)),
        out_specs=(HBM, HBM, SEM, SEM, pl.BlockSpec(memory_space=pltpu.VMEM)),
        input_output_aliases={0: 0, 1: 1},
        compiler_params=pltpu.CompilerParams(has_side_effects=EFFECT),
    )(b_thru, land_thru, send_sems, recv_sems, after)


def _gather_finish(land_thru, send2, recv2, after, name):
    def body(land_ref, send2, recv2, after_ref, land_out):
        x, y, c = _place()
        for j, chip in enumerate(_chips(x, y)):
            cp = pltpu.make_async_remote_copy(src_ref=land_ref.at[_index((*chip, c))],
                                              dst_ref=land_ref.at[_index((*chip, 1 - c))], send_sem=send2.at[j],
                                              recv_sem=recv2.at[j], device_id=(x, y, 1 - c), device_id_type=MESH)
            cp.wait_send()
            cp.wait_recv()

    return pl.pallas_call(
        body, name=name, out_shape=pltpu.HBM(land_thru.shape, land_thru.dtype),
        in_specs=(HBM, SEM, SEM, pl.BlockSpec(memory_space=pl.ANY)), out_specs=HBM,
        input_output_aliases={0: 0},
        compiler_params=pltpu.CompilerParams(has_side_effects=EFFECT),
    )(land_thru, send2, recv2, after)


class _Gathering:
    def __init__(self, first, later, me):
        started, token = _gather_start(list(first.values()), next(iter(first.values())), "gather1_first", spare=(0,))
        cast = [_behind(a, token).astype(BF16) for a in later.values()]
        started_later, self.token = _gather_start(cast, token, "gather1_later")
        self.me, self.state = me, dict(zip([*first, *later], started + started_later))

    def begin(self, after):
        return self.token

    def forward(self, name, after):
        *self.state[name], token = _gather_forward(*self.state[name], after, "gather2_" + name)
        return token

    def get(self, name, after):
        block, land, send2, recv2 = self.state[name]
        land = _gather_finish(land, send2, recv2, after, "gather3_" + name)
        land = lax.dynamic_update_index_in_dim(land, block[None], self.me, 0)
        return land if name not in ("w_out", "w_down") else land.reshape(-1, land.shape[2])


class _Reducing:
    def __init__(self, core, chip, gather_small):
        self.core, self.chip, self.state, self.token, self.gather_small = core, chip, {}, None, gather_small

    def meanwhile(self, small, loss, after):
        self.small_sum = self.gather_small(small, loss, after)
        return self.small_sum

    def start(self, name, grad):
        g = grad if grad.ndim == 3 else grad.reshape(N_DEV, grad.shape[0] // N_DEV, grad.shape[1])
        *self.state[name], token = _exchange_start(g, _pair_route, name == "w_in", "pair_send_" + name)
        return token

    def relay(self, name, after):
        tail = name == "w_in"
        grad, pair = _exchange_wait(*self.state[name], after, _pair_route, tail, "pair_recv_" + name)
        total = _pair_add(grad, pair, self.core, tail, "pair_add_" + name)
        *self.state[name], self.token = _exchange_start(total, _chip_route, False, "chip_send_" + name)
        return self.token

    def finish(self, name, after):
        total, land = _exchange_wait(*self.state[name], after, _chip_route, False, "chip_recv_" + name)
        own = lax.dynamic_index_in_dim(total, self.chip, 0, keepdims=True)
        return lax.dynamic_update_index_in_dim(land, own, self.chip, 0)


def _carry_w_in(main, tail):
    slabs, _, d = main.shape
    tc = _fit(d, 2048)
    assert slabs == N_DEV + 1 and tail.shape[:2] == (N_DEV, IN_TAIL), (main.shape, tail.shape)
    top = lambda off: pl.BlockSpec((None, IN_TAIL, tc), lambda s, j: (s + off, 0, j))

    def carry(m_ref, t_ref, o_ref):
        o_ref[...] = m_ref[...] + t_ref[...]

    main = pl.pallas_call(
        carry, name="carry_w_in", grid=(N_DEV - 1, d // tc), in_specs=[top(1), top(0)], out_specs=top(1),
        out_shape=jax.ShapeDtypeStruct(main.shape, main.dtype), input_output_aliases={0: 0},
        compiler_params=_params("parallel", "parallel"),
    )(main, tail)

    def last(m_ref, t_ref, o_ref):
        o_ref[...] = jnp.zeros_like(o_ref)
        o_ref[0:IN_TAIL, :] = t_ref[...]

    return pl.pallas_call(
        last, name="last_slab_w_in", grid=(d // tc,),
        in_specs=[pl.BlockSpec(memory_space=pl.ANY), pl.BlockSpec((None, IN_TAIL, tc), lambda j: (N_DEV - 1, 0, j))],
        out_specs=pl.BlockSpec((None, IN_SLAB, tc), lambda j: (N_DEV, 0, j)),
        out_shape=jax.ShapeDtypeStruct(main.shape, main.dtype), input_output_aliases={0: 0},
        compiler_params=_params("parallel"),
    )(main, tail)


def _rows(n, want):
    t = min(n, want)
    t -= t % 16
    while n % t:
        t -= 16
    return t


def _adam_math(w, g, m, v):
    m2 = ADAM_B1 * m + (1.0 - ADAM_B1) * g
    v2 = ADAM_B2 * v + (1.0 - ADAM_B2) * (g * g)
    m_hat = m2 * (1.0 / (1.0 - ADAM_B1 ** ADAM_STEP))
    v_hat = v2 * (1.0 / (1.0 - ADAM_B2 ** ADAM_STEP))
    return -ADAM_LR * (m_hat / (jnp.sqrt(v_hat) + ADAM_EPS) + ADAM_WD * w), m2, v2


def _slot_sum(r_ref):
    acc = r_ref[0].astype(F32)
    for i in range(1, r_ref.shape[0]):
        acc = acc + r_ref[i].astype(F32)
    return acc


def _shift_w_in(w):
    ws, d = w.shape
    tc = _fit(d, 256)

    def body(w_ref, main_ref, tail_ref, tall):
        tall[...] = jnp.zeros_like(tall)
        tall[0:ws, :] = w_ref[...]
        moved = pltpu.roll(tall[...], _index(_place()), 0).astype(BF16)
        main_ref[...] = moved[0:IN_SLAB]
        tail_ref[...] = moved[IN_SLAB:]

    return pl.pallas_call(
        body, name="shift_w_in", grid=(d // tc,),
        in_specs=[pl.BlockSpec((ws, tc), lambda j: (0, j))],
        out_specs=[pl.BlockSpec((IN_SLAB, tc), lambda j: (0, j)), pl.BlockSpec((IN_TAIL, tc), lambda j: (0, j))],
        out_shape=[jax.ShapeDtypeStruct((IN_SLAB, d), BF16), jax.ShapeDtypeStruct((IN_TAIL, d), BF16)],
        scratch_shapes=[pltpu.VMEM((IN_SLAB + IN_TAIL, tc), F32)], compiler_params=_params("parallel"),
    )(w)


def _sum_adamw_shifted(r, w, m, v, name):
    _, ph, d = r.shape
    ws = w.shape[0]
    tc = _fit(d, 256)

    def body(r_ref, w_ref, m_ref, v_ref, g_ref, d_ref, m2_ref, v2_ref, tall):
        tall[...] = pltpu.roll(_slot_sum(r_ref), lax.rem(ph - _index(_place()), ph), 0)
        g = tall[0:ws, :]
        g_ref[...] = g
        d_ref[...], m2_ref[...], v2_ref[...] = _adam_math(w_ref[...], g, m_ref[...], v_ref[...])

    blk = pl.BlockSpec((ws, tc), lambda j: (0, j))
    out = jax.ShapeDtypeStruct(w.shape, F32)
    return pl.pallas_call(
        body, name=name, grid=(d // tc,),
        in_specs=[pl.BlockSpec((r.shape[0], ph, tc), lambda j: (0, 0, j)), blk, blk, blk],
        out_specs=[blk] * 4, out_shape=[out] * 4,
        scratch_shapes=[pltpu.VMEM((ph, tc), F32)], compiler_params=_params("parallel"),
    )(r, w, m, v)


def _sum_slots(r, name, tr=128):
    _, rows, cols = r.shape
    tr = _rows(rows, tr)

    def body(r_ref, g_ref):
        g_ref[...] = _slot_sum(r_ref)

    return pl.pallas_call(
        body, name=name, grid=(rows // tr,),
        in_specs=[pl.BlockSpec((r.shape[0], tr, cols), lambda i: (0, i, 0))],
        out_specs=pl.BlockSpec((tr, cols), lambda i: (i, 0)),
        out_shape=jax.ShapeDtypeStruct((rows, cols), F32),
        compiler_params=_params("parallel"),
    )(r)


def _adamw(w, g, m, v, name, tr=256):
    rows, cols = w.shape
    tr = _rows(rows, tr)

    def body(w_ref, g_ref, m_ref, v_ref, d_ref, m2_ref, v2_ref):
        d_ref[...], m2_ref[...], v2_ref[...] = _adam_math(w_ref[...], g_ref[...], m_ref[...], v_ref[...])

    blk = pl.BlockSpec((tr, cols), lambda i: (i, 0))
    out = jax.ShapeDtypeStruct((rows, cols), F32)
    return pl.pallas_call(
        body, name=name, grid=(rows // tr,), in_specs=[blk] * 4, out_specs=[blk] * 3, out_shape=[out] * 3,
        compiler_params=_params("parallel"),
    )(w, g, m, v)


def _sum_adamw(r, w, m, v, name, tr=256):
    rows, cols = w.shape
    tr = _rows(rows, tr)

    def body(r_ref, w_ref, m_ref, v_ref, g_ref, d_ref, m2_ref, v2_ref):
        g = _slot_sum(r_ref)
        g_ref[...] = g
        d_ref[...], m2_ref[...], v2_ref[...] = _adam_math(w_ref[...], g, m_ref[...], v_ref[...])

    blk = pl.BlockSpec((tr, cols), lambda i: (i, 0))
    out = jax.ShapeDtypeStruct((rows, cols), F32)
    return pl.pallas_call(
        body, name=name, grid=(rows // tr,),
        in_specs=[pl.BlockSpec((r.shape[0], tr, cols), lambda i: (0, i, 0)), blk, blk, blk],
        out_specs=[blk] * 4, out_shape=[out] * 4,
        compiler_params=_params("parallel"),
    )(r, w, m, v)


def _pack(pieces, sizes):
    flat = [jnp.pad(p.reshape(-1).astype(F32), (0, s - p.size)) for p, s in zip(pieces, sizes)]
    total = sum(sizes)
    padded = -(-total // (16 * LANE)) * (16 * LANE)
    return jnp.pad(jnp.concatenate(flat), (0, padded - total)).reshape(-1, LANE)


def _unpack(packed, shapes, sizes):
    flat = packed.reshape(-1)
    out, off = [], 0
    for shp, s in zip(shapes, sizes):
        n = 1
        for k in shp:
            n *= k
        out.append(flat[off:off + n].reshape(shp))
        off += s
    return out


def _lanes(n):
    return -(-n // LANE) * LANE


WEIGHTS = ("w_in", "b_gates", "w_sc_conv", "mh_gain", "w_out", "ln1_g", "ln1_b", "w_up", "w_ffn_conv", "b_ffn_conv",
           "w_down", "ln2_g", "ln2_b")
BIG = ("w_in", "w_out", "w_up", "w_down")
SMALL = tuple(n for n in WEIGHTS if n not in BIG)


def kernel(x, w_in, b_gates, w_sc_conv, mh_gain, w_out, ln1_g, ln1_b, w_up, w_ffn_conv, b_ffn_conv, w_down, ln2_g, ln2_b, loss_target, m_w_in, m_b_gates, m_w_sc_conv, m_mh_gain, m_w_out, m_ln1_g, m_ln1_b, m_w_up, m_w_ffn_conv, m_b_ffn_conv, m_w_down, m_ln2_g, m_ln2_b, v_w_in, v_b_gates, v_w_sc_conv, v_mh_gain, v_w_out, v_ln1_g, v_ln1_b, v_w_up, v_w_ffn_conv, v_b_ffn_conv, v_w_down, v_ln2_g, v_ln2_b):
    w = dict(zip(WEIGHTS, (w_in, b_gates, w_sc_conv, mh_gain, w_out, ln1_g, ln1_b, w_up, w_ffn_conv, b_ffn_conv,
                           w_down, ln2_g, ln2_b)))
    m = dict(zip(WEIGHTS, (m_w_in, m_b_gates, m_w_sc_conv, m_mh_gain, m_w_out, m_ln1_g, m_ln1_b, m_w_up,
                           m_w_ffn_conv, m_b_ffn_conv, m_w_down, m_ln2_g, m_ln2_b)))
    v = dict(zip(WEIGHTS, (v_w_in, v_b_gates, v_w_sc_conv, v_mh_gain, v_w_out, v_ln1_g, v_ln1_b, v_w_up,
                           v_w_ffn_conv, v_b_ffn_conv, v_w_down, v_ln2_g, v_ln2_b)))
    me = _index(_place())
    d = x.shape[2]
    ws_in = w_in.shape[2]
    assert ws_in == IN_SLAB + 1 and N_DEV <= LANE, w_in.shape
    ninp = (N_DEV + 1) * IN_SLAB
    ws_sc, ws_fc = w_sc_conv.shape[2], w_ffn_conv.shape[2]
    w_in_t, m_in_t, v_in_t = (jnp.transpose(a[0]) for a in (w_in, m_w_in, v_w_in))

    w_in_main, w_in_tail = _shift_w_in(w_in_t)
    taps8 = lambda a: jnp.pad(a[0], ((0, 5), (0, 0)))
    at_once = ("w_in", "w_tail", "w_sc", "w_fc")
    wx = _Gathering(dict(zip(at_once, (w_in_main, w_in_tail, taps8(w_sc_conv), taps8(w_ffn_conv)))),
                    {n: w[n][0] for n in ("w_out", "w_up", "w_down")}, me)
    token = wx.begin(None)
    for n in at_once:
        token = wx.forward(n, token)
    g_in, g_tail, g_sc, g_fc = (wx.get(n, token) for n in at_once)
    w_in_full = _carry_w_in(g_in, g_tail).reshape(ninp, d)
    w_sc_full = g_sc[:, :3].transpose(1, 0, 2).reshape(3, N_DEV * ws_sc)
    w_fc_full = g_fc[:, :3].transpose(1, 0, 2).reshape(3, N_DEV * ws_fc)

    xi, yi, ci = _place()
    names = ("loss",) + SMALL
    pieces = {}

    def gather_small(small, loss_t, after):
        pieces.update(small, loss=loss_t[0, :1])
        sizes = [_lanes(pieces[n].size) for n in names]
        (g_small,) = _all_gather([_behind(_pack([pieces[n] for n in names], sizes), after)], "gather_small")
        return _sum_slots(g_small, "sum_small", tr=g_small.shape[1])

    gx = _Reducing(jnp.reshape(ci, (1,)).astype(jnp.int32), 2 * xi + yi, gather_small)
    loss_t, grad_x, small, _ = _local_step(
        x[0], loss_target[0], w_in_full, b_gates, w_sc_full, mh_gain, None, ln1_g, ln1_b, None,
        w_fc_full, b_ffn_conv, None, ln2_g, ln2_b, gx=gx, wx=wx)

    grads, deltas, new_m, new_v = {}, {}, {}, {}
    for name in ("w_down", "w_up", "w_out"):
        grads[name], deltas[name], new_m[name], new_v[name] = _sum_adamw(
            gx.finish(name, gx.token), w[name][0], m[name][0], v[name][0], "adamw_" + name)

    summed = _unpack(gx.small_sum, [pieces[n].shape for n in names], [_lanes(pieces[n].size) for n in names])
    full = dict(zip(names, summed))
    full["w_sc_conv"] = lax.dynamic_slice(full["w_sc_conv"], (0, me * ws_sc), (3, ws_sc))
    full["w_ffn_conv"] = lax.dynamic_slice(full["w_ffn_conv"], (0, me * ws_fc), (3, ws_fc))
    for n in SMALL:
        grads[n] = full[n].reshape(w[n].shape)
    sizes = [_lanes(w[n].size) for n in SMALL]
    shapes = [w[n].shape for n in SMALL]
    packed = [_pack([t[n] for n in SMALL], sizes) for t in (w, grads, m, v)]
    small_out = _adamw(*packed, "adamw_small")
    for res, t in zip(small_out, (deltas, new_m, new_v)):
        t.update(zip(SMALL, _unpack(res, shapes, sizes)))

    done = sum(t[0:1, 0:1] for t in (grad_x, deltas["w_down"], deltas["w_up"], deltas["w_out"], small_out[0]))
    grads["w_in"], deltas["w_in"], new_m["w_in"], new_v["w_in"] = (
        jnp.transpose(a)[None] for a in _sum_adamw_shifted(gx.finish("w_in", done), w_in_t, m_in_t, v_in_t, "adamw_w_in"))

    big = lambda t: {n: (t[n].reshape(w[n].shape) if n in BIG else t[n]) for n in WEIGHTS}
    grads, deltas, new_m, new_v = big(grads), big(deltas), big(new_m), big(new_v)
    return (full["loss"].reshape(()), grad_x[None], *[grads[n] for n in WEIGHTS], *[deltas[n] for n in WEIGHTS],
            *[new_m[n] for n in WEIGHTS], *[new_v[n] for n in WEIGHTS])
```

```python
import functools

import jax
import jax.numpy as jnp
from jax import lax
from jax.experimental import pallas as pl
from jax.experimental.pallas import tpu as pltpu

F32 = jnp.float32
BF16 = jnp.bfloat16
MESH = pl.DeviceIdType.MESH

N_DEV = 8
NH = 4
CHUNK = 64
LN_EPS = 1e-5
HN_EPS = 1e-6
ALPHA = 2.0 ** 0.25
LANE = 128
IN_SLAB = 7 * LANE
IN_TAIL = 16
VMEM_LIMIT = 56 * 1024 * 1024
ADAM_LR, ADAM_B1, ADAM_B2, ADAM_EPS, ADAM_WD, ADAM_STEP = 0.001, 0.9, 0.999, 1e-08, 0.01, 10

_NN = (((1,), (0,)), ((), ()))
_NT = (((1,), (1,)), ((), ()))
_TN = (((0,), (0,)), ((), ()))


def _dot(a, b, dn=_NN):
    return lax.dot_general(a, b, dn, preferred_element_type=F32)


def _params(*sem):
    return pltpu.CompilerParams(dimension_semantics=sem if sem else None, vmem_limit_bytes=VMEM_LIMIT)


def _iota(shape, axis):
    return lax.broadcasted_iota(jnp.int32, shape, axis)


def _fit(n, want):
    if n <= want:
        return n
    t = want - want % LANE
    while n % t:
        t -= LANE
    return t


def _matmul(a, b, mode, out_dtype, name, tm=1024, tn=512, tk=1024, add=None, add_scale=1.0,
            a_blocked=False, b_blocked=False, o_width=None, after=None):
    a_parts = a if isinstance(a, tuple) else None
    b_parts = b if isinstance(b, tuple) else None
    if a_parts:
        a_blocked, (a_rows, wa), na = True, a[0].shape, len(a)
        kd, m = (a_rows, na * wa) if mode == "tn" else (na * wa, a_rows)
    elif a_blocked:
        na, a_rows, wa = a.shape
        kd, m = (a_rows, na * wa) if mode == "tn" else (na * wa, a_rows)
    elif mode == "tn":
        kd, m = a.shape
    else:
        m, kd = a.shape
    if b_parts:
        b_blocked, (rows, w), nb = True, b[0].shape, len(b)
    elif b_blocked:
        nb, rows, w = b.shape
    if b_blocked:
        n = rows if mode == "nt" else nb * w
        assert (nb * w if mode == "nt" else rows) == kd, (name, kd)
    else:
        n = b.shape[0] if mode == "nt" else b.shape[1]
    tm, tn, tk = _fit(m, tm), _fit(n, tn), _fit(kd, tk)
    if a_blocked and mode == "tn":
        tm = _fit(wa, tm)
    if a_blocked and mode != "tn":
        tk = _fit(wa, tk)
    if b_blocked and mode != "nt":
        tn = _fit(w, tn)
    if b_blocked and mode == "nt":
        tk = _fit(w, tk)
    if o_width is not None:
        tn = _fit(o_width, tn)
    assert m % tm == 0 and n % tn == 0 and kd % tk == 0, (name, m, n, kd, tm, tn, tk)
    assert not (a_blocked and mode != "tn" and wa % tk) and not (b_blocked and mode == "nt" and w % tk), (name, tk)
    nk = kd // tk
    dn = {"nn": _NN, "nt": _NT, "tn": _TN}[mode]
    if a_blocked and mode == "tn":
        a_per = wa // tm
        a_spec = pl.BlockSpec((None, tk, tm), lambda i, j, k: (i // a_per, k, i % a_per))
    elif a_blocked:
        a_per = wa // tk
        a_spec = pl.BlockSpec((None, tm, tk), lambda i, j, k: (k // a_per, i, k % a_per))
    elif mode == "tn":
        a_spec = pl.BlockSpec((tk, tm), lambda i, j, k: (k, i))
    else:
        a_spec = pl.BlockSpec((tm, tk), lambda i, j, k: (i, k))
    if b_blocked and mode != "nt":
        per = w // tn
        b_spec = pl.BlockSpec((None, tk, tn), lambda i, j, k: (j // per, k, j % per))
    elif b_blocked:
        per = w // tk
        b_spec = pl.BlockSpec((None, tn, tk), lambda i, j, k: (k // per, j, k % per))
    elif mode == "nt":
        b_spec = pl.BlockSpec((tn, tk), lambda i, j, k: (j, k))
    else:
        b_spec = pl.BlockSpec((tk, tn), lambda i, j, k: (k, j))
    if o_width is None:
        o_spec = pl.BlockSpec((tm, tn), lambda i, j, k: (i, j))
        o_shape = (m, n)
    else:
        oper = o_width // tn
        o_spec = pl.BlockSpec((None, tm, tn), lambda i, j, k: (j // oper, i, j % oper))
        o_shape = (n // o_width, m, o_width)
    a_list, a_specs = [a], [a_spec]
    if a_parts:
        hold = lambda x, s: jnp.clip(x - s * a_per, 0, a_per - 1)
        a_list = list(a_parts)
        a_specs = [(pl.BlockSpec((tk, tm), lambda i, j, k, s=s: (k, hold(i, s))) if mode == "tn"
                    else pl.BlockSpec((tm, tk), lambda i, j, k, s=s: (i, hold(k, s)))) for s in range(na)]
    b_list, b_specs = [b], [b_spec]
    if b_parts:
        hold_b = lambda x, s: jnp.clip(x - s * per, 0, per - 1)
        b_list = list(b_parts)
        b_specs = [(pl.BlockSpec((tn, tk), lambda i, j, k, s=s: (j, hold_b(k, s))) if mode == "nt"
                    else pl.BlockSpec((tk, tn), lambda i, j, k, s=s: (k, hold_b(j, s)))) for s in range(nb)]
    n_a, n_b = len(a_list), len(b_list)
    has_add = add is not None
    n_in = n_a + n_b + has_add + (after is not None)
    in_place = nk > 1 and out_dtype == F32

    def body(*refs):
        add_ref = refs[n_a + n_b] if has_add else None
        o_ref = refs[n_in]
        i, j, k = pl.program_id(0), pl.program_id(1), pl.program_id(2)

        def finish(r):
            if has_add:
                r = r + add_scale * add_ref[...]
            o_ref[...] = r.astype(out_dtype)

        def step(a_ref, b_ref):
            if nk == 1:
                finish(_dot(a_ref[...], b_ref[...], dn))
                return
            acc = o_ref if in_place else refs[-1]

            @pl.when(k == 0)
            def _():
                acc[...] = _dot(a_ref[...], b_ref[...], dn)

            @pl.when(k > 0)
            def _():
                acc[...] += _dot(a_ref[...], b_ref[...], dn)

        if n_a == 1 and n_b == 1:
            step(refs[0], refs[1])
        else:
            slab_a = ((i if mode == "tn" else k) // a_per) if n_a > 1 else 0
            slab_b = ((k if mode == "nt" else j) // per) if n_b > 1 else 0
            for sa in range(n_a):
                for sb in range(n_b):
                    pl.when((slab_a == sa) & (slab_b == sb))(functools.partial(step, refs[sa], refs[n_a + sb]))
        if nk > 1 and not (in_place and not has_add):
            @pl.when(k == nk - 1)
            def _():
                finish((o_ref if in_place else refs[-1])[...])

    in_specs = a_specs + b_specs + ([pl.BlockSpec((tm, tn), lambda i, j, k: (i, j))] if has_add else [])
    args = (*a_list, *b_list) + ((add,) if has_add else ())
    if after is not None:
        in_specs.append(pl.BlockSpec(memory_space=pl.ANY))
        args += (after,)
    return pl.pallas_call(
        body, name=name, grid=(m // tm, n // tn, nk),
        in_specs=in_specs, out_specs=o_spec,
        out_shape=jax.ShapeDtypeStruct(o_shape, out_dtype),
        scratch_shapes=[pltpu.VMEM((tm, tn), F32)] if nk > 1 and not in_place else [],
        compiler_params=_params("parallel", "parallel", "arbitrary"),
    )(*args)


def _shift_down(u, s):
    return jnp.where(_iota(u.shape, 0) >= s, pltpu.roll(u, s, 0), 0.0)


def _shift_up(u, s):
    t = u.shape[0]
    return jnp.where(_iota(u.shape, 0) < t - s, pltpu.roll(u, t - s, 0), 0.0)


SLAB = 8


def _rolled(u):
    return pltpu.roll(u, 2, 0), pltpu.roll(u, 1, 0)


def _conv(u, w, rolled=None):
    u2, u1 = _rolled(u) if rolled is None else rolled
    raw = w[0:1] * u2 + w[1:2] * u1 + w[2:3] * u
    head = u[0:SLAB]
    mended = w[0:1] * _shift_down(head, 2) + w[1:2] * _shift_down(head, 1) + w[2:3] * head
    return jnp.concatenate([mended, raw[SLAB:]], axis=0)


def _conv_t(dy, w):
    t = dy.shape[0]
    raw = w[2:3] * dy + w[1:2] * pltpu.roll(dy, t - 1, 0) + w[0:1] * pltpu.roll(dy, t - 2, 0)
    tail = dy[t - SLAB:]
    mended = w[2:3] * tail + w[1:2] * _shift_up(tail, 1) + w[0:1] * _shift_up(tail, 2)
    return jnp.concatenate([raw[:t - SLAB], mended], axis=0)


def _conv_dw(dy, u, rolled=None):
    t = dy.shape[0]
    u2, u1 = _rolled(u) if rolled is None else rolled
    head, tail = dy[0:SLAB], u[t - SLAB:]
    r = _iota(head.shape, 0)
    wrap2 = jnp.sum(jnp.where(r < 2, head * pltpu.roll(tail, 2, 0), 0.0), axis=0, keepdims=True)
    wrap1 = jnp.sum(jnp.where(r < 1, head * pltpu.roll(tail, 1, 0), 0.0), axis=0, keepdims=True)
    d0 = jnp.sum(dy * u2, axis=0, keepdims=True) - wrap2
    d1 = jnp.sum(dy * u1, axis=0, keepdims=True) - wrap1
    d2 = jnp.sum(dy * u, axis=0, keepdims=True)
    r3 = _iota((3, dy.shape[1]), 0)
    return jnp.where(r3 == 0, d0, jnp.where(r3 == 1, d1, d2))


def _sigmoid(x):
    return 0.5 * jnp.tanh(0.5 * x) + 0.5


def _sconv_fwd(proj, w_sc, t, wc):
    nb = wc // LANE

    def body(cb_ref, cc_ref, ch_ref, w_ref, y_ref):
        u = cc_ref[...] * ch_ref[...]
        y_ref[...] = (cb_ref[...] * _conv(u, w_ref[...])).astype(BF16)

    col = lambda off: pl.BlockSpec((t, LANE), lambda j: (0, j + off))
    return pl.pallas_call(
        body, name="sconv_fwd", grid=(nb,),
        in_specs=[col(0), col(nb), col(2 * nb), pl.BlockSpec((3, LANE), lambda j: (0, j))],
        out_specs=pl.BlockSpec((None, t, LANE), lambda j: (0, 0, j)),
        out_shape=jax.ShapeDtypeStruct((2, t, wc), BF16),
        compiler_params=_params("parallel"),
    )(proj, proj, proj, w_sc)


def _sconv_bwd(dy, proj, w_sc, t, wc):
    nb = wc // LANE

    def body(dy_ref, cb_ref, cc_ref, ch_ref, w_ref, dcb_ref, dcc_ref, dch_ref, dw_ref):
        cc, ch, w, d = cc_ref[...], ch_ref[...], w_ref[...], dy_ref[...]
        u = cc * ch
        ru = _rolled(u)
        dcb_ref[...] = (d * _conv(u, w, ru)).astype(BF16)
        dcu = d * cb_ref[...]
        dw_ref[...] = _conv_dw(dcu, u, ru)
        du = _conv_t(dcu, w)
        dcc_ref[...] = (du * ch).astype(BF16)
        dch_ref[...] = (du * cc).astype(BF16)

    col = lambda off: pl.BlockSpec((t, LANE), lambda j: (0, j + off))
    act = jax.ShapeDtypeStruct((t, wc), BF16)
    return pl.pallas_call(
        body, name="sconv_bwd", grid=(nb,),
        in_specs=[col(0), col(0), col(nb), col(2 * nb), pl.BlockSpec((3, LANE), lambda j: (0, j))],
        out_specs=[col(0), col(0), col(0), pl.BlockSpec((3, LANE), lambda j: (0, j))],
        out_shape=[act, act, act, jax.ShapeDtypeStruct((3, wc), F32)],
        compiler_params=_params("parallel"),
    )(dy, proj, proj, proj, w_sc)


def _gates_prep(proj, bias_tile, t, gate_tile):
    def body(g_ref, b_ref, o_ref):
        g = g_ref[...] + b_ref[...]
        lane = _iota(g.shape, 1)
        is_f = (lane >= NH) & (lane < 2 * NH)
        lf = jnp.minimum(g, 0.0) - jnp.log(1.0 + jnp.exp(-jnp.abs(g)))
        c = jnp.where(is_f, lf, 0.0)
        r = _iota(g.shape, 0) % CHUNK
        s = 1
        while s < CHUNK:
            c = c + jnp.where(r >= s, pltpu.roll(c, s, 0), 0.0)
            s *= 2
        o_ref[...] = jnp.where(is_f, c, jnp.where(lane < NH, g, 0.0))

    return pl.pallas_call(
        body, name="gates_prep", grid=(1,),
        in_specs=[pl.BlockSpec((t, LANE), lambda i: (0, gate_tile)), pl.BlockSpec((1, LANE), lambda i: (0, 0))],
        out_specs=pl.BlockSpec((t, LANE), lambda i: (0, 0)),
        out_shape=jax.ShapeDtypeStruct((t, LANE), F32),
        compiler_params=_params("arbitrary"),
    )(proj, bias_tile)


def _gates_bwd(dgate, proj, bias_tile, t, gate_tile):
    def body(dg_ref, g_ref, b_ref, o_ref, s_ref):
        g = g_ref[...] + b_ref[...]
        lane = _iota(g.shape, 1)
        r = _iota(g.shape, 0) % CHUNK
        dsig = 1.0 - _sigmoid(g)
        out = jnp.zeros(g.shape, F32)
        for h in range(NH):
            d = dg_ref[h]
            c = d
            s = 1
            while s < CHUNK:
                c = c + jnp.where(r + s < CHUNK, pltpu.roll(c, t - s, 0), 0.0)
                s *= 2
            di = jnp.broadcast_to(d[:, 0:1], g.shape)
            db = jnp.broadcast_to(c[:, 1:2], g.shape)
            out = out + jnp.where(lane == h, di, 0.0) + jnp.where(lane == NH + h, db * dsig, 0.0)
        o_ref[...] = out.astype(BF16)
        s_ref[...] = jnp.sum(out, axis=0, keepdims=True)

    return pl.pallas_call(
        body, name="gates_bwd", grid=(1,),
        in_specs=[pl.BlockSpec((NH, t, LANE), lambda i: (0, 0, 0)),
                  pl.BlockSpec((t, LANE), lambda i: (0, gate_tile)), pl.BlockSpec((1, LANE), lambda i: (0, 0))],
        out_specs=[pl.BlockSpec((t, LANE), lambda i: (0, 0)), pl.BlockSpec((1, LANE), lambda i: (0, 0))],
        out_shape=[jax.ShapeDtypeStruct((t, LANE), BF16), jax.ShapeDtypeStruct((1, LANE), F32)],
        compiler_params=_params("arbitrary"),
    )(dgate, proj, bias_tile)


def _in_turn(heads):
    while heads:
        heads = [g for g in heads if next(g, heads) is not heads]


def _chunk_gates(gc, gr, h, mprev):
    L = CHUNK
    icol, bcol = gc[:, h:h + 1], gc[:, h + NH:h + NH + 1]
    irow, brow = gr[h:h + 1, :], gr[h + NH:h + NH + 1, :]
    tri = _iota((L, L), 0) >= _iota((L, L), 1)
    log_d = jnp.where(tri, bcol - brow + irow, -jnp.inf)
    inter = bcol + mprev
    mt = jnp.maximum(inter, jnp.max(log_d, axis=1, keepdims=True))
    dw = jnp.exp(log_d - mt)
    iw = jnp.exp(inter - mt)
    g = brow[:, L - 1:L]
    wlog_col = g - bcol + icol
    wlog_row = g - brow + irow
    mnew = jnp.maximum(g + mprev, jnp.max(wlog_row, axis=1, keepdims=True))
    wcol = jnp.exp(wlog_col - mnew)
    decay = jnp.exp(g + mprev - mnew)
    return dw, iw, mt, wcol, decay, mnew


def _mlstm_fwd(proj, gcol, grow, t, wc, dh):
    nc = t // CHUNK
    wm = NH * dh
    assert wc == wm, (wc, wm)
    qoff = 3 * wc // wm
    scale = dh ** -0.5

    def body(q_ref, k_ref, v_ref, gc_ref, gr_ref, h_ref, cs_ref, ns_ref, c_s, n_s, m_s):
        @pl.when(pl.program_id(0) == 0)
        def _():
            c_s[...] = jnp.zeros_like(c_s)
            n_s[...] = jnp.zeros_like(n_s)
            m_s[...] = jnp.zeros_like(m_s)

        gc, gr = gc_ref[...], gr_ref[0]
        done = [None] * NH

        def head(h):
            cols = slice(h * dh, (h + 1) * dh)
            mprev = m_s[h, 0:1, 0:1]
            cprev = c_s[h]
            n8 = n_s[h]
            nprev = n8[0:1]
            qs = q_ref[:, cols] * scale
            k = k_ref[:, cols]
            qs_b, k_b, v_b = qs.astype(BF16), k.astype(BF16), v_ref[:, cols].astype(BF16)
            qk = _dot(qs_b, k_b, _NT)
            yield
            q_c = _dot(qs_b, cprev.astype(BF16))
            yield
            dw, iw, mt, wcol, decay, mnew = _chunk_gates(gc, gr, h, mprev)
            yield
            s = qk * dw
            wk = wcol * k
            num = _dot(s.astype(BF16), v_b) + iw * q_c
            yield
            c_new = decay * cprev + _dot(wk.astype(BF16), v_b, _TN)
            yield
            den = jnp.sum(s, axis=1, keepdims=True) + iw * jnp.sum(qs * nprev, axis=1, keepdims=True)
            done[h] = (cprev, jnp.where(_iota(n8.shape, 0) == 1, mprev, n8),
                       num / jnp.maximum(jnp.abs(den), jnp.exp(-mt)), c_new,
                       decay * n8 + jnp.sum(wk, axis=0, keepdims=True), mnew)

        _in_turn([head(h) for h in range(NH)])
        for h, (c_old, n_old, h_out, c_new, n_new, m_new) in enumerate(done):
            cs_ref[h] = c_old
            ns_ref[h] = n_old
            h_ref[:, h * dh:(h + 1) * dh] = h_out
            c_s[h] = c_new
            n_s[h] = n_new
            m_s[h] = jnp.broadcast_to(m_new, m_s.shape[1:])

    grp = lambda off: pl.BlockSpec((CHUNK, wm), lambda c: (c, qoff + off))
    return pl.pallas_call(
        body, name="mlstm_fwd", grid=(nc,),
        in_specs=[grp(0), grp(1), grp(2),
                  pl.BlockSpec((CHUNK, LANE), lambda c: (c, 0)),
                  pl.BlockSpec((1, 8, CHUNK), lambda c: (c, 0, 0))],
        out_specs=[pl.BlockSpec((CHUNK, wm), lambda c: (c, 0)),
                   pl.BlockSpec((NH, None, dh, dh), lambda c: (0, c, 0, 0)),
                   pl.BlockSpec((NH, None, 8, dh), lambda c: (0, c, 0, 0))],
        out_shape=[jax.ShapeDtypeStruct((t, wm), F32),
                   jax.ShapeDtypeStruct((NH, nc, dh, dh), F32),
                   jax.ShapeDtypeStruct((NH, nc, 8, dh), F32)],
        scratch_shapes=[pltpu.VMEM((NH, dh, dh), F32), pltpu.VMEM((NH, 8, dh), F32), pltpu.VMEM((NH, 8, LANE), F32)],
        compiler_params=_params("arbitrary"),
    )(proj, proj, proj, gcol, grow)


def _mlstm_bwd(proj, gcol, grow, hval, dh_in, cs, ns, t, wc, dh):
    nc = t // CHUNK
    wm = NH * dh
    assert wc == wm, (wc, wm)
    qoff = 3 * wc // wm
    scale = dh ** -0.5
    L = CHUNK

    def body(q_ref, k_ref, v_ref, gc_ref, gr_ref, h_ref, dh_ref, cs_ref, ns_ref,
             dq_ref, dk_ref, dv_ref, dg_ref, dc_s, dn_s):
        @pl.when(pl.program_id(0) == 0)
        def _():
            dc_s[...] = jnp.zeros_like(dc_s)
            dn_s[...] = jnp.zeros_like(dn_s)

        gc, gr = gc_ref[...], gr_ref[0]
        eye = _iota((L, L), 0) == _iota((L, L), 1)
        lane = _iota((L, LANE), 1)
        last = _iota((L, 1), 0) == L - 1
        done = [None] * NH

        def head(h):
            cols = slice(h * dh, (h + 1) * dh)
            ns8 = ns_ref[h]
            nprev = ns8[0:1]
            mprev = ns8[1:2, 0:1]
            cprev = cs_ref[h]
            dcn = dc_s[h]
            dn8 = dn_s[h]
            dnn = dn8[0:1]

            qs = q_ref[:, cols] * scale
            k = k_ref[:, cols]
            qs_b, k_b, v_b = qs.astype(BF16), k.astype(BF16), v_ref[:, cols].astype(BF16)
            qk = _dot(qs_b, k_b, _NT)
            yield
            dw, iw, mt, wcol, decay, _ = _chunk_gates(gc, gr, h, mprev)
            yield
            s = qk * dw
            den = jnp.sum(s, axis=1, keepdims=True) + iw * jnp.sum(qs * nprev, axis=1, keepdims=True)
            emt = jnp.exp(-mt)
            r = 1.0 / jnp.maximum(jnp.abs(den), emt)
            dout = dh_ref[:, cols]
            dnum = dout * r
            dden = (-jnp.sum(dout * h_ref[:, cols], axis=1, keepdims=True) * r
                    * jnp.where(jnp.abs(den) > emt, jnp.sign(den), 0.0))
            dnum_b = dnum.astype(BF16)
            cprev_b = cprev.astype(BF16)
            dcn_b = dcn.astype(BF16)
            yield

            g_raw = _dot(dnum_b, v_b, _NT)
            yield
            q_inter = _dot(dnum_b, cprev_b, _NT)
            yield
            k_raw = _dot(v_b, dcn_b, _NT)
            yield
            gd = (g_raw + dden) * dw
            gd_b = gd.astype(BF16)
            dqs_inter = iw * (q_inter + dden * nprev)
            dk_inter = wcol * (k_raw + dnn)
            wk = wcol * k
            iq = iw * qs
            dqs = _dot(gd_b, k_b) + dqs_inter
            yield
            dk = _dot(gd_b, qs_b, _TN) + dk_inter
            yield
            dv = _dot(s.astype(BF16), dnum_b, _TN) + _dot(wk.astype(BF16), dcn_b)
            yield
            dc_new = decay * dcn + _dot(iq.astype(BF16), dnum_b, _TN)
            yield

            e = gd * qk
            e_cols = jnp.sum(jnp.where(eye, jnp.sum(e, axis=0, keepdims=True), 0.0), axis=1, keepdims=True)
            yield
            k_inter = jnp.sum(k * dk_inter, axis=1, keepdims=True)
            rq = jnp.sum(e, axis=1, keepdims=True) + jnp.sum(qs * dqs_inter, axis=1, keepdims=True)
            rk = e_cols + k_inter
            hsum = jnp.sum(k_inter, axis=0, keepdims=True)
            jdec = decay * (jnp.sum(jnp.sum(dcn * cprev, axis=1, keepdims=True), axis=0, keepdims=True)
                            + jnp.sum(dnn * nprev, axis=1, keepdims=True))
            db = rq - rk + jnp.where(last, hsum + jdec, 0.0)
            done[h] = (jnp.where(lane == 0, rk, jnp.where(lane == 1, db, 0.0)),
                       (dqs * scale).astype(BF16), dk.astype(BF16), dv.astype(BF16), dc_new,
                       decay * dn8 + jnp.sum(iq * dden, axis=0, keepdims=True))

        _in_turn([head(h) for h in range(NH)])
        for h, (dgate, dq, dk, dv, dc_new, dn_new) in enumerate(done):
            cols = slice(h * dh, (h + 1) * dh)
            dg_ref[h] = dgate
            dq_ref[:, cols] = dq
            dk_ref[:, cols] = dk
            dv_ref[:, cols] = dv
            dc_s[h] = dc_new
            dn_s[h] = dn_new

    rc = lambda c: nc - 1 - c
    grp = lambda off: pl.BlockSpec((L, wm), lambda c: (rc(c), qoff + off))
    hm = pl.BlockSpec((L, wm), lambda c: (rc(c), 0))
    act = jax.ShapeDtypeStruct((t, wm), BF16)
    return pl.pallas_call(
        body, name="mlstm_bwd", grid=(nc,),
        in_specs=[grp(0), grp(1), grp(2),
                  pl.BlockSpec((L, LANE), lambda c: (rc(c), 0)),
                  pl.BlockSpec((1, 8, L), lambda c: (rc(c), 0, 0)),
                  hm, hm,
                  pl.BlockSpec((NH, None, dh, dh), lambda c: (0, rc(c), 0, 0)),
                  pl.BlockSpec((NH, None, 8, dh), lambda c: (0, rc(c), 0, 0))],
        out_specs=[hm, hm, hm, pl.BlockSpec((NH, L, LANE), lambda c: (0, rc(c), 0))],
        out_shape=[act, act, act, jax.ShapeDtypeStruct((NH, t, LANE), F32)],
        scratch_shapes=[pltpu.VMEM((NH, dh, dh), F32), pltpu.VMEM((NH, 8, dh), F32)],
        compiler_params=_params("arbitrary"),
    )(proj, proj, proj, gcol, grow, hval, dh_in, cs, ns)


def _head_norm(hv):
    mu = jnp.mean(hv, axis=1, keepdims=True)
    hc = hv - mu
    rstd = lax.rsqrt(jnp.mean(hc * hc, axis=1, keepdims=True) + HN_EPS)
    return hc * rstd, rstd


def _hnorm_fwd(hval, proj, gain, y, t, wc, dh, tr=512):
    ooff = 3 * wc // dh + 3 * NH
    tr = min(tr, t)

    def body(h_ref, o_ref, g_ref, y_in, y_ref):
        hhat, _ = _head_norm(h_ref[...])
        y_ref[...] = (_sigmoid(o_ref[...]) * hhat * g_ref[...]).astype(BF16)

    return pl.pallas_call(
        body, name="hnorm_fwd", grid=(t // tr, NH),
        in_specs=[pl.BlockSpec((tr, dh), lambda i, h: (i, h)),
                  pl.BlockSpec((tr, dh), lambda i, h: (i, ooff + h)),
                  pl.BlockSpec((1, dh), lambda i, h: (0, h)),
                  pl.BlockSpec(memory_space=pl.ANY)],
        out_specs=pl.BlockSpec((None, tr, dh), lambda i, h: (1, i, h)),
        out_shape=jax.ShapeDtypeStruct(y.shape, BF16),
        input_output_aliases={3: 0},
        compiler_params=_params("parallel", "parallel"),
    )(hval, proj, gain, y)


def _hnorm_bwd(dy, hval, proj, gain, t, wc, dh, tr=512):
    ooff = 3 * wc // dh + 3 * NH
    tr = min(tr, t)
    yoff = wc // dh

    def body(dy_ref, h_ref, o_ref, g_ref, do_ref, dh_ref, dg_ref):
        i = pl.program_id(1)
        hhat, rstd = _head_norm(h_ref[...])
        gain_v = g_ref[...]
        sig = _sigmoid(o_ref[...])
        d = dy_ref[...]
        do_ref[...] = (d * hhat * gain_v * sig * (1.0 - sig)).astype(BF16)
        dhn = d * sig
        part = jnp.sum(dhn * hhat, axis=0, keepdims=True)

        @pl.when(i == 0)
        def _():
            dg_ref[...] = part

        @pl.when(i > 0)
        def _():
            dg_ref[...] += part

        dhat = dhn * gain_v
        dh_ref[...] = rstd * (dhat - jnp.mean(dhat, axis=1, keepdims=True)
                              - hhat * jnp.mean(dhat * hhat, axis=1, keepdims=True))

    blk = lambda off: pl.BlockSpec((tr, dh), lambda h, i: (i, off + h))
    return pl.pallas_call(
        body, name="hnorm_bwd", grid=(NH, t // tr),
        in_specs=[blk(yoff), blk(0), blk(ooff), pl.BlockSpec((1, dh), lambda h, i: (0, h))],
        out_specs=[blk(0), blk(0), pl.BlockSpec((1, dh), lambda h, i: (0, h))],
        out_shape=[jax.ShapeDtypeStruct((t, NH * dh), BF16), jax.ShapeDtypeStruct((t, NH * dh), F32),
                   jax.ShapeDtypeStruct((1, NH * dh), F32)],
        compiler_params=_params("parallel", "arbitrary"),
    )(dy, hval, proj, gain)


def _ln_stats(z):
    mu = jnp.mean(z, axis=1, keepdims=True)
    zc = z - mu
    rstd = lax.rsqrt(jnp.mean(zc * zc, axis=1, keepdims=True) + LN_EPS)
    return zc * rstd, rstd


def _ln_bwd(dy, xhat, rstd, g):
    dxh = dy * g
    return rstd * (dxh - jnp.mean(dxh, axis=1, keepdims=True) - xhat * jnp.mean(dxh * xhat, axis=1, keepdims=True))


def _accum(ref, i, part):
    @pl.when(i == 0)
    def _():
        ref[...] = part

    @pl.when(i > 0)
    def _():
        ref[...] += part


def _ln1_fwd(x, mix, g, b, tr=256):
    t, d = x.shape

    def body(x_ref, m_ref, g_ref, b_ref, xh_ref, rs_ref, xb_ref):
        xhat, rstd = _ln_stats(ALPHA * x_ref[...] + m_ref[...])
        xh_ref[...] = xhat
        rs_ref[...] = rstd
        xb_ref[...] = (xhat * g_ref[...] + b_ref[...]).astype(BF16)

    row = pl.BlockSpec((tr, d), lambda i: (i, 0))
    vec = pl.BlockSpec((1, d), lambda i: (0, 0))
    return pl.pallas_call(
        body, name="ln1_fwd", grid=(t // tr,),
        in_specs=[row, row, vec, vec],
        out_specs=[row, pl.BlockSpec((tr, 1), lambda i: (i, 0)), row],
        out_shape=[jax.ShapeDtypeStruct((t, d), F32), jax.ShapeDtypeStruct((t, 1), F32),
                   jax.ShapeDtypeStruct((t, d), BF16)],
        compiler_params=_params("parallel"),
    )(x, mix, g, b)


def _ln2_loss(xhat1, g1, b1, ff, target, g2, b2, tr=256):
    t, d = ff.shape

    def body(xh_ref, g1_ref, b1_ref, f_ref, t_ref, g_ref, b_ref, dz_ref, dzb_ref, dg_ref, db_ref, l_ref):
        i = pl.program_id(0)
        x1 = xh_ref[...] * g1_ref[...] + b1_ref[...]
        xhat, rstd = _ln_stats(ALPHA * x1 + f_ref[...])
        gv = g_ref[...]
        e = xhat * gv + b_ref[...] - t_ref[...]
        lsum = jnp.sum(jnp.sum(e * e, axis=1, keepdims=True), axis=0, keepdims=True) * (0.5 / d)
        dy = e * (1.0 / d)
        _accum(dg_ref, i, jnp.sum(dy * xhat, axis=0, keepdims=True))
        _accum(db_ref, i, jnp.sum(dy, axis=0, keepdims=True))
        _accum(l_ref, i, jnp.broadcast_to(lsum, l_ref.shape))
        dz = _ln_bwd(dy, xhat, rstd, gv)
        dz_ref[...] = dz
        dzb_ref[...] = dz.astype(BF16)

    row = pl.BlockSpec((tr, d), lambda i: (i, 0))
    vec = pl.BlockSpec((1, d), lambda i: (0, 0))
    return pl.pallas_call(
        body, name="ln2_loss", grid=(t // tr,),
        in_specs=[row, vec, vec, row, row, vec, vec],
        out_specs=[row, row, vec, vec, pl.BlockSpec((8, LANE), lambda i: (0, 0))],
        out_shape=[jax.ShapeDtypeStruct((t, d), F32), jax.ShapeDtypeStruct((t, d), BF16),
                   jax.ShapeDtypeStruct((1, d), F32), jax.ShapeDtypeStruct((1, d), F32),
                   jax.ShapeDtypeStruct((8, LANE), F32)],
        compiler_params=_params("arbitrary"),
    )(xhat1, g1, b1, ff, target, g2, b2)


def _ln1_bwd(dz2, dffn, xhat1, rstd1, g1, tr=256):
    t, d = dz2.shape

    def body(a_ref, f_ref, xh_ref, rs_ref, g_ref, dz_ref, dzb_ref, dg_ref, db_ref):
        i = pl.program_id(0)
        dy = ALPHA * a_ref[...] + f_ref[...]
        xhat = xh_ref[...]
        _accum(dg_ref, i, jnp.sum(dy * xhat, axis=0, keepdims=True))
        _accum(db_ref, i, jnp.sum(dy, axis=0, keepdims=True))
        dz = _ln_bwd(dy, xhat, rs_ref[...], g_ref[...])
        dz_ref[...] = dz
        dzb_ref[...] = dz.astype(BF16)

    row = pl.BlockSpec((tr, d), lambda i: (i, 0))
    vec = pl.BlockSpec((1, d), lambda i: (0, 0))
    return pl.pallas_call(
        body, name="ln1_bwd", grid=(t // tr,),
        in_specs=[row, row, row, pl.BlockSpec((tr, 1), lambda i: (i, 0)), vec],
        out_specs=[row, row, vec, vec],
        out_shape=[jax.ShapeDtypeStruct((t, d), F32), jax.ShapeDtypeStruct((t, d), BF16),
                   jax.ShapeDtypeStruct((1, d), F32), jax.ShapeDtypeStruct((1, d), F32)],
        compiler_params=_params("arbitrary"),
    )(dz2, dffn, xhat1, rstd1, g1)


def _ffn_act_fwd(hid0, w_fc, b_fc, t, dff):
    nb = dff // LANE

    def body(hv_ref, hg_ref, wv_ref, wg_ref, bv_ref, bg_ref, a_ref):
        val = _conv(hv_ref[...], wv_ref[...]) + bv_ref[...]
        gate = _conv(hg_ref[...], wg_ref[...]) + bg_ref[...]
        a_ref[...] = (gate * _sigmoid(gate) * val).astype(BF16)

    col = lambda off: pl.BlockSpec((t, LANE), lambda j: (0, j + off))
    w3 = lambda off: pl.BlockSpec((3, LANE), lambda j: (0, j + off))
    w1 = lambda off: pl.BlockSpec((1, LANE), lambda j: (0, j + off))
    return pl.pallas_call(
        body, name="ffn_act_fwd", grid=(nb,),
        in_specs=[col(0), col(nb), w3(0), w3(nb), w1(0), w1(nb)],
        out_specs=col(0),
        out_shape=jax.ShapeDtypeStruct((t, dff), BF16),
        compiler_params=_params("parallel"),
    )(hid0, hid0, w_fc, w_fc, b_fc, b_fc)


def _ffn_act_bwd(da, hid0, w_fc, b_fc, t, dff):
    nb = dff // LANE

    def body(da_ref, hv_ref, hg_ref, wv_ref, wg_ref, bv_ref, bg_ref,
             dhv_ref, dhg_ref, dwv_ref, dwg_ref, dbv_ref, dbg_ref):
        hv, hg, wv, wg = hv_ref[...], hg_ref[...], wv_ref[...], wg_ref[...]
        rv, rg = _rolled(hv), _rolled(hg)
        val = _conv(hv, wv, rv) + bv_ref[...]
        gate = _conv(hg, wg, rg) + bg_ref[...]
        sig = _sigmoid(gate)
        d = da_ref[...]
        dsig = d * sig
        dval = dsig * gate
        dgate = dsig * val * (1.0 + gate * (1.0 - sig))
        dhv_ref[...] = _conv_t(dval, wv).astype(BF16)
        dhg_ref[...] = _conv_t(dgate, wg).astype(BF16)
        dwv_ref[...] = _conv_dw(dval, hv, rv)
        dwg_ref[...] = _conv_dw(dgate, hg, rg)
        dbv_ref[...] = jnp.sum(dval, axis=0, keepdims=True)
        dbg_ref[...] = jnp.sum(dgate, axis=0, keepdims=True)

    col = lambda off: pl.BlockSpec((t, LANE), lambda j: (0, j + off))
    w3 = lambda off: pl.BlockSpec((3, LANE), lambda j: (0, j + off))
    w1 = lambda off: pl.BlockSpec((1, LANE), lambda j: (0, j + off))
    s3 = jax.ShapeDtypeStruct((3, dff), F32)
    s1 = jax.ShapeDtypeStruct((1, dff), F32)
    return pl.pallas_call(
        body, name="ffn_act_bwd", grid=(nb,),
        in_specs=[col(0), col(0), col(nb), w3(0), w3(nb), w1(0), w1(nb)],
        out_specs=[col(0), col(0), w3(0), w3(0), w1(0), w1(0)],
        out_shape=[jax.ShapeDtypeStruct((t, dff), BF16)] * 2 + [s3, s3, s1, s1],
        compiler_params=_params("parallel"),
    )(da, hid0, hid0, w_fc, w_fc, b_fc, b_fc)


class _Ready:
    def __init__(self, **weights):
        self.weights = weights

    def begin(self, after):
        return None

    def forward(self, name, after):
        return None

    def get(self, name, after):
        return self.weights[name]


class _Kept:
    def __init__(self):
        self.grads = {}

    def start(self, name, grad):
        self.grads[name] = grad
        return None

    def relay(self, name, after):
        return None

    def meanwhile(self, small, loss, after):
        return None


def _behind(a, token):
    return a if token is None else a + token[0:1, 0:1].reshape((1,) * a.ndim)


def _local_step(x, target, w_in, b_gates, w_sc, gain, w_out, ln1_g, ln1_b, w_up, w_fc, b_fc, w_down, ln2_g, ln2_b,
                gx=None, wx=None, x_b=None):
    t, d = x.shape
    wc = d // 2
    dh = (d - wc) // NH
    wm = NH * dh
    dff = w_fc.shape[1] // 2
    if wx is None:
        wx = _Ready(w_out=w_out, w_up=w_up, w_down=w_down)
    ninp = w_in.shape[0]
    nin = 3 * wc + 4 * wm
    gate_tile = nin // LANE
    nc = t // CHUNK
    bias_tile = jnp.pad(b_gates, ((0, 0), (0, LANE - 2 * NH)))

    if x_b is None:
        x_b = x.astype(BF16)
    proj = _matmul(x_b, w_in, "nt", F32, "proj", tm=512, tn=2688, tk=d, after=wx.begin(w_in))
    y = _sconv_fwd(proj, w_sc, t, wc)
    gcol = _gates_prep(proj, bias_tile, t, gate_tile)
    grow = gcol[:, :8].T.reshape(8, nc, CHUNK).transpose(1, 0, 2)
    hval, cs, ns = _mlstm_fwd(proj, gcol, grow, t, wc, dh)
    y = _hnorm_fwd(hval, proj, gain, y, t, wc, dh)
    tok = wx.forward("w_out", y)
    w_out = wx.get("w_out", tok)
    mix = _matmul(y, w_out, "nn", F32, "out_proj", tm=512, tn=1024, tk=wc, a_blocked=True, after=tok)
    xhat1, rstd1, x1_b = _ln1_fwd(x, mix, _behind(ln1_g, wx.forward("w_up", mix)), ln1_b)
    w_up = wx.get("w_up", x1_b)
    wsl = w_up.shape[2]
    hid0 = _matmul(x1_b, w_up, "nn", F32, "ffn_up", tm=512, tn=wsl, tk=d, b_blocked=True)
    act = _ffn_act_fwd(hid0, w_fc, _behind(b_fc, wx.forward("w_down", hid0)), t, dff)
    w_down = wx.get("w_down", act)
    ff = _matmul(act, w_down, "nn", F32, "ffn_down", tm=1024, tn=512, tk=dff)
    dz2, dz2_b, d_ln2_g, d_ln2_b, loss = _ln2_loss(xhat1, ln1_g, ln1_b, ff, target, ln2_g, ln2_b)

    if gx is None:
        gx = _Kept()
    d_w_down = _matmul(act, dz2_b, "tn", BF16, "ffn_down_dw", tm=512, tn=1024, tk=t)
    d_act = _matmul(dz2_b, w_down, "nt", F32, "ffn_down_dx", tm=1024, tn=512, tk=d, after=gx.start("w_down", d_w_down))
    *d_hid0, dwv, dwg, dbv, dbg = _ffn_act_bwd(d_act, hid0, w_fc, _behind(b_fc, gx.relay("w_down", d_act)), t, dff)
    d_w_fc = jnp.concatenate([dwv, dwg], axis=1)
    d_b_fc = jnp.concatenate([dbv, dbg], axis=1)
    d_hid0 = tuple(d_hid0[:2])
    d_w_up = _matmul(x1_b, d_hid0, "tn", BF16, "ffn_up_dw", tm=512, tn=wsl, tk=t, o_width=wsl)
    d_x1_ffn = _matmul(d_hid0, w_up, "nt", F32, "ffn_up_dx", tm=1024, tn=1024, tk=wsl, b_blocked=True,
                       after=gx.start("w_up", d_w_up))
    dz1, dz1_b, d_ln1_g, d_ln1_b = _ln1_bwd(dz2, d_x1_ffn, xhat1, rstd1, _behind(ln1_g, gx.relay("w_up", d_x1_ffn)))

    d_w_out = _matmul(y, dz1_b, "tn", BF16, "out_proj_dw", tm=512, tn=1024, tk=t, a_blocked=True)
    dy = _matmul(dz1_b, w_out, "nt", F32, "out_proj_dx", tm=512, tn=1024, tk=d, after=gx.start("w_out", d_w_out))
    dcb, dcc, dch, d_w_sc = _sconv_bwd(dy, proj, _behind(w_sc, gx.relay("w_out", dy)), t, wc)
    d_o, d_hval, d_gain = _hnorm_bwd(dy, hval, proj, gain, t, wc, dh)
    dq, dk, dv, dgate = _mlstm_bwd(proj, gcol, grow, hval, d_hval, cs, ns, t, wc, dh)
    dgt, d_b_gates = _gates_bwd(dgate, proj, bias_tile, t, gate_tile)
    pad = jnp.zeros((t, ninp - nin - LANE), BF16)
    d_proj = jnp.concatenate([dcb, dcc, dch, dq, dk, dv, d_o, dgt, pad], axis=1)
    d_w_in = _matmul(d_proj, x_b, "tn", BF16, "proj_dw", tm=IN_SLAB, tn=1024, tk=t)
    small = dict(b_gates=d_b_gates[:, :2 * NH], w_sc_conv=d_w_sc, mh_gain=d_gain, ln1_g=d_ln1_g, ln1_b=d_ln1_b,
                 w_ffn_conv=d_w_fc, b_ffn_conv=d_b_fc, ln2_g=d_ln2_g, ln2_b=d_ln2_b)
    token = gx.start("w_in", d_w_in.reshape(ninp // IN_SLAB, IN_SLAB, d))
    token = gx.relay("w_in", gx.meanwhile(small, loss, token))
    grad_x = _matmul(d_proj, w_in, "nn", F32, "proj_dx", tm=512, tn=512, tk=ninp, add=dz1, add_scale=ALPHA, after=token)
    return loss, grad_x, small, gx


HBM = pl.BlockSpec(memory_space=pltpu.HBM)


def _place():
    return lax.axis_index("x"), lax.axis_index("y"), lax.axis_index("c")


def _index(p):
    return 4 * p[0] + 2 * p[1] + p[2]


def _all_gather(arrs, name):
    n = len(arrs)

    def body(*refs):
        ins, outs = refs[:n], refs[n:2 * n]
        send_sems, recv_sems, local_sems = refs[2 * n:]
        x, y, c = _place()
        me, sibling = (x, y, c), (x, y, 1 - c)
        chips = [(1 - x, y), (x, 1 - y), (1 - x, 1 - y)]

        def copy(a, k, block, to, own=False):
            dst = outs[a].at[_index(block)]
            return pltpu.make_async_remote_copy(
                src_ref=ins[a] if own else dst, dst_ref=dst,
                send_sem=send_sems.at[k * n + a], recv_sem=recv_sems.at[k * n + a],
                device_id=to, device_id_type=MESH)

        mine = [pltpu.make_async_copy(ins[a], outs[a].at[_index(me)], local_sems.at[a]) for a in range(n)]
        for cp in mine:
            cp.start()
        first = []
        for a in range(n):
            first.append(copy(a, 0, me, sibling, own=True))
            first += [copy(a, 1 + j, me, (*chip, c), own=True) for j, chip in enumerate(chips)]
        for cp in first:
            cp.start()
        passed = []
        for j, chip in enumerate(chips):
            for a in range(n):
                copy(a, 1 + j, (*chip, c), me).wait_recv()
                cp = copy(a, 4 + j, (*chip, c), sibling)
                cp.start()
                passed.append(cp)
        for a in range(n):
            copy(a, 0, sibling, me).wait_recv()
            for j, chip in enumerate(chips):
                copy(a, 4 + j, (*chip, 1 - c), me).wait_recv()
        for cp in first + passed:
            cp.wait_send()
        for cp in mine:
            cp.wait()

    return pl.pallas_call(
        body, name=name, in_specs=[HBM] * n, out_specs=[HBM] * n,
        out_shape=[jax.ShapeDtypeStruct((N_DEV,) + a.shape, a.dtype) for a in arrs],
        scratch_shapes=[pltpu.SemaphoreType.DMA((7 * n,)), pltpu.SemaphoreType.DMA((7 * n,)),
                        pltpu.SemaphoreType.DMA((n,))],
    )(*arrs)


SEM = pl.BlockSpec(memory_space=pltpu.SEMAPHORE)
EFFECT = pltpu.SideEffectType.DATAFLOW_SIDE_EFFECTING


def _chips(x, y):
    return [(1 - x, y), (x, 1 - y), (1 - x, 1 - y)]


N_CHIP = N_DEV // 2


def _pair_route(x, y, c):
    return [((x, y, 1 - c), 2 * q + (1 - c), q, q) for q in range(N_CHIP)]


def _chip_route(x, y, c):
    mine = 2 * x + y
    return [((*chip, c), 2 * chip[0] + chip[1], mine, 2 * chip[0] + chip[1]) for chip in _chips(x, y)]


def _exchange_pieces(g_ref, land_ref, width, tail):
    if not tail:
        return [(lambda i: g_ref.at[i], lambda s: land_ref.at[s])]
    return [(lambda i: g_ref.at[i], lambda s: land_ref.at[s, pl.ds(0, width), :]),
            (lambda i: g_ref.at[i + 1, pl.ds(0, IN_TAIL), :], lambda s: land_ref.at[s, pl.ds(width, IN_TAIL), :])]


def _exchange_start(grad, route, tail, name):
    width = grad.shape[1]
    n_p = 2 if tail else 1
    n_c = len(route(0, 0, 0))
    land_shape = (N_CHIP, width + (IN_TAIL if tail else 0), grad.shape[2])

    def body(g_ref, land_ref, send_sems, recv_sems, g_thru, land_thru, token):
        for j, (peer, slab, slot, _) in enumerate(route(*_place())):
            for p, (src, dst) in enumerate(_exchange_pieces(g_ref, land_ref, width, tail)):
                pltpu.make_async_remote_copy(src_ref=src(slab), dst_ref=dst(slot), send_sem=send_sems.at[j * n_p + p],
                                             recv_sem=recv_sems.at[j * n_p + p], device_id=peer,
                                             device_id_type=MESH).start()
        token[...] = jnp.zeros_like(token)

    return pl.pallas_call(
        body, name=name,
        out_shape=(pltpu.SemaphoreType.DMA((n_c * n_p,)), pltpu.SemaphoreType.DMA((n_c * n_p,)),
                   pltpu.HBM(grad.shape, grad.dtype), pltpu.HBM(land_shape, grad.dtype),
                   jax.ShapeDtypeStruct((8, LANE), F32)),
        in_specs=(HBM, HBM), out_specs=(SEM, SEM, HBM, HBM, pl.BlockSpec(memory_space=pltpu.VMEM)),
        input_output_aliases={0: 2, 1: 3},
        compiler_params=pltpu.CompilerParams(has_side_effects=EFFECT),
    )(pltpu.with_memory_space_constraint(grad, pltpu.HBM),
      pltpu.with_memory_space_constraint(lax.empty(land_shape, grad.dtype), pltpu.HBM))


def _exchange_wait(send_sems, recv_sems, g_thru, land_thru, after, route, tail, name):
    width = g_thru.shape[1]
    n_p = 2 if tail else 1

    def body(g_ref, land_ref, send_sems, recv_sems, after_ref, g_dead, got_ref):
        for j, (peer, slab, _, slot) in enumerate(route(*_place())):
            for p, (src, dst) in enumerate(_exchange_pieces(g_ref, land_ref, width, tail)):
                cp = pltpu.make_async_remote_copy(src_ref=src(slab), dst_ref=dst(slot),
                                                  send_sem=send_sems.at[j * n_p + p], recv_sem=recv_sems.at[j * n_p + p],
                                                  device_id=peer, device_id_type=MESH)
                cp.wait_send()
                cp.wait_recv()

    return pl.pallas_call(
        body, name=name,
        out_shape=(pltpu.HBM(g_thru.shape, g_thru.dtype), pltpu.HBM(land_thru.shape, land_thru.dtype)),
        in_specs=(HBM, HBM, SEM, SEM, pl.BlockSpec(memory_space=pl.ANY)), out_specs=(HBM, HBM),
        input_output_aliases={0: 0, 1: 1},
        compiler_params=pltpu.CompilerParams(has_side_effects=EFFECT),
    )(g_thru, land_thru, send_sems, recv_sems, after)


def _pair_add(grad, pair, core, tail, name):
    rows, cols = grad.shape[1], grad.shape[2]
    total = pair.shape[1]

    def body(core_ref, *refs):
        if tail:
            g_ref, t_ref, p_ref, o_ref = refs
            o_ref[0:rows, :] = (g_ref[...].astype(F32) + p_ref[0:rows, :].astype(F32)).astype(BF16)
            o_ref[rows:total, :] = (t_ref[...].astype(F32) + p_ref[rows:total, :].astype(F32)).astype(BF16)
        else:
            g_ref, p_ref, o_ref = refs
            o_ref[...] = (g_ref[...].astype(F32) + p_ref[...].astype(F32)).astype(BF16)

    if tail:
        tc = _fit(cols, 512)
        grid = (N_CHIP, cols // tc)
        slab = pl.BlockSpec((None, total, tc), lambda q, i, core_ref: (q, 0, i))
        in_specs = [pl.BlockSpec((None, rows, tc), lambda q, i, core_ref: (2 * q + core_ref[0], 0, i)),
                    pl.BlockSpec((None, IN_TAIL, tc), lambda q, i, core_ref: (2 * q + core_ref[0] + 1, 0, i))]
    else:
        tr = _rows(rows, 1024)
        grid = (N_CHIP, rows // tr)
        slab = pl.BlockSpec((None, tr, cols), lambda q, i, core_ref: (q, i, 0))
        in_specs = [pl.BlockSpec((None, tr, cols), lambda q, i, core_ref: (2 * q + core_ref[0], i, 0))]
    return pl.pallas_call(
        body, name=name,
        grid_spec=pltpu.PrefetchScalarGridSpec(num_scalar_prefetch=1, grid=grid,
                                               in_specs=in_specs + [slab], out_specs=slab),
        out_shape=jax.ShapeDtypeStruct(pair.shape, BF16),
        compiler_params=_params("parallel", "parallel"),
    )(core, *([grad, grad] if tail else [grad]), pair)


def _gather_start(blocks, after, name, spare=()):
    n = len(blocks)
    lands = [(N_DEV + (a in spare),) + b.shape for a, b in enumerate(blocks)]

    def body(*refs):
        b_refs, land_refs = refs[:n], refs[n:2 * n]
        send_sems, recv_sems = refs[2 * n + 1:3 * n + 1], refs[3 * n + 1:4 * n + 1]
        token = refs[-1]
        x, y, c = _place()
        me = _index((x, y, c))
        for a in range(n):
            for k, to in enumerate([(x, y, 1 - c)] + [(*chip, c) for chip in _chips(x, y)]):
                pltpu.make_async_remote_copy(src_ref=b_refs[a], dst_ref=land_refs[a].at[me], send_sem=send_sems[a].at[k],
                                             recv_sem=recv_sems[a].at[k], device_id=to, device_id_type=MESH).start()
        token[...] = jnp.zeros_like(token)

    sems = [pltpu.SemaphoreType.DMA((4,))] * n
    out = pl.pallas_call(
        body, name=name,
        out_shape=(*sems, *sems, *[pltpu.HBM(b.shape, b.dtype) for b in blocks],
                   *[pltpu.HBM(s, b.dtype) for s, b in zip(lands, blocks)], jax.ShapeDtypeStruct((8, LANE), F32)),
        in_specs=(*[HBM] * (2 * n), pl.BlockSpec(memory_space=pl.ANY)),
        out_specs=(*[SEM] * (2 * n), *[HBM] * (2 * n), pl.BlockSpec(memory_space=pltpu.VMEM)),
        input_output_aliases={i: 2 * n + i for i in range(2 * n)},
        compiler_params=pltpu.CompilerParams(has_side_effects=EFFECT),
    )(*[pltpu.with_memory_space_constraint(b, pltpu.HBM) for b in blocks],
      *[pltpu.with_memory_space_constraint(lax.empty(s, b.dtype), pltpu.HBM) for s, b in zip(lands, blocks)], after)
    return [(out[a], out[n + a], out[2 * n + a], out[3 * n + a]) for a in range(n)], out[-1]


def _gather_forward(send_sems, recv_sems, b_thru, land_thru, after, name):
    def body(b_ref, land_ref, send_sems, recv_sems, after_ref, b_dead, land_out, send2, recv2, token):
        x, y, c = _place()
        sibling = (x, y, 1 - c)
        for k, frm in enumerate([sibling] + [(*chip, c) for chip in _chips(x, y)]):
            cp = pltpu.make_async_remote_copy(src_ref=b_ref, dst_ref=land_ref.at[_index(frm)], send_sem=send_sems.at[k],
                                              recv_sem=recv_sems.at[k], device_id=frm, device_id_type=MESH)
            cp.wait_send()
            cp.wait_recv()
        for j, chip in enumerate(_chips(x, y)):
            slot = land_ref.at[_index((*chip, c))]
            pltpu.make_async_remote_copy(src_ref=slot, dst_ref=slot, send_sem=send2.at[j], recv_sem=recv2.at[j],
                                         device_id=sibling, device_id_type=MESH).start()
        token[...] = jnp.zeros_like(token)

    return pl.pallas_call(
        body, name=name,
        out_shape=(pltpu.HBM(b_thru.shape, b_thru.dtype), pltpu.HBM(land_thru.shape, land_thru.dtype),
                   pltpu.SemaphoreType.DMA((3,)), pltpu.SemaphoreType.DMA((3,)), jax.ShapeDtypeStruct((8, LANE), F32)),
        in_specs=(HBM, HBM, SEM, SEM, pl.BlockSpec(memory_space=pl.ANY)),
        out_specs=(HBM, HBM, SEM, SEM, pl.BlockSpec(memory_space=pltpu.VMEM)),
        input_output_aliases={0: 0, 1: 1},
        compiler_params=pltpu.CompilerParams(has_side_effects=EFFECT),
    )(b_thru, land_thru, send_sems, recv_sems, after)


def _gather_finish(land_thru, send2, recv2, after, name):
    def body(land_ref, send2, recv2, after_ref, land_out):
        x, y, c = _place()
        for j, chip in enumerate(_chips(x, y)):
            cp = pltpu.make_async_remote_copy(src_ref=land_ref.at[_index((*chip, c))],
                                              dst_ref=land_ref.at[_index((*chip, 1 - c))], send_sem=send2.at[j],
                                              recv_sem=recv2.at[j], device_id=(x, y, 1 - c), device_id_type=MESH)
            cp.wait_send()
            cp.wait_recv()

    return pl.pallas_call(
        body, name=name, out_shape=pltpu.HBM(land_thru.shape, land_thru.dtype),
        in_specs=(HBM, SEM, SEM, pl.BlockSpec(memory_space=pl.ANY)), out_specs=HBM,
        input_output_aliases={0: 0},
        compiler_params=pltpu.CompilerParams(has_side_effects=EFFECT),
    )(land_thru, send2, recv2, after)


class _Gathering:
    def __init__(self, first, later, me):
        started, token = _gather_start(list(first.values()), next(iter(first.values())), "gather1_first", spare=(0,))
        cast = [_behind(a, token).astype(BF16) for a in later.values()]
        started_later, self.token = _gather_start(cast, token, "gather1_later")
        self.me, self.state = me, dict(zip([*first, *later], started + started_later))

    def begin(self, after):
        return self.token

    def forward(self, name, after):
        *self.state[name], token = _gather_forward(*self.state[name], after, "gather2_" + name)
        return token

    def get(self, name, after):
        block, land, send2, recv2 = self.state[name]
        land = _gather_finish(land, send2, recv2, after, "gather3_" + name)
        land = lax.dynamic_update_index_in_dim(land, block[None], self.me, 0)
        return land if name not in ("w_out", "w_down") else land.reshape(-1, land.shape[2])


class _Reducing:
    def __init__(self, core, chip, gather_small):
        self.core, self.chip, self.state, self.token, self.gather_small = core, chip, {}, None, gather_small

    def meanwhile(self, small, loss, after):
        self.small_sum = self.gather_small(small, loss, after)
        return self.small_sum

    def start(self, name, grad):
        g = grad if grad.ndim == 3 else grad.reshape(N_DEV, grad.shape[0] // N_DEV, grad.shape[1])
        *self.state[name], token = _exchange_start(g, _pair_route, name == "w_in", "pair_send_" + name)
        return token

    def relay(self, name, after):
        tail = name == "w_in"
        grad, pair = _exchange_wait(*self.state[name], after, _pair_route, tail, "pair_recv_" + name)
        total = _pair_add(grad, pair, self.core, tail, "pair_add_" + name)
        *self.state[name], self.token = _exchange_start(total, _chip_route, False, "chip_send_" + name)
        return self.token

    def finish(self, name, after):
        total, land = _exchange_wait(*self.state[name], after, _chip_route, False, "chip_recv_" + name)
        own = lax.dynamic_index_in_dim(total, self.chip, 0, keepdims=True)
        return lax.dynamic_update_index_in_dim(land, own, self.chip, 0)


def _carry_w_in(main, tail):
    slabs, _, d = main.shape
    tc = _fit(d, 2048)
    assert slabs == N_DEV + 1 and tail.shape[:2] == (N_DEV, IN_TAIL), (main.shape, tail.shape)
    top = lambda off: pl.BlockSpec((None, IN_TAIL, tc), lambda s, j: (s + off, 0, j))

    def carry(m_ref, t_ref, o_ref):
        o_ref[...] = m_ref[...] + t_ref[...]

    main = pl.pallas_call(
        carry, name="carry_w_in", grid=(N_DEV - 1, d // tc), in_specs=[top(1), top(0)], out_specs=top(1),
        out_shape=jax.ShapeDtypeStruct(main.shape, main.dtype), input_output_aliases={0: 0},
        compiler_params=_params("parallel", "parallel"),
    )(main, tail)

    def last(m_ref, t_ref, o_ref):
        o_ref[...] = jnp.zeros_like(o_ref)
        o_ref[0:IN_TAIL, :] = t_ref[...]

    return pl.pallas_call(
        last, name="last_slab_w_in", grid=(d // tc,),
        in_specs=[pl.BlockSpec(memory_space=pl.ANY), pl.BlockSpec((None, IN_TAIL, tc), lambda j: (N_DEV - 1, 0, j))],
        out_specs=pl.BlockSpec((None, IN_SLAB, tc), lambda j: (N_DEV, 0, j)),
        out_shape=jax.ShapeDtypeStruct(main.shape, main.dtype), input_output_aliases={0: 0},
        compiler_params=_params("parallel"),
    )(main, tail)


def _rows(n, want):
    t = min(n, want)
    t -= t % 16
    while n % t:
        t -= 16
    return t


def _adam_math(w, g, m, v):
    m2 = ADAM_B1 * m + (1.0 - ADAM_B1) * g
    v2 = ADAM_B2 * v + (1.0 - ADAM_B2) * (g * g)
    m_hat = m2 * (1.0 / (1.0 - ADAM_B1 ** ADAM_STEP))
    v_hat = v2 * (1.0 / (1.0 - ADAM_B2 ** ADAM_STEP))
    return -ADAM_LR * (m_hat / (jnp.sqrt(v_hat) + ADAM_EPS) + ADAM_WD * w), m2, v2


def _slot_sum(r_ref):
    acc = r_ref[0].astype(F32)
    for i in range(1, r_ref.shape[0]):
        acc = acc + r_ref[i].astype(F32)
    return acc


def _shift_w_in(w):
    ws, d = w.shape
    tc = _fit(d, 256)

    def body(w_ref, main_ref, tail_ref, tall):
        tall[...] = jnp.zeros_like(tall)
        tall[0:ws, :] = w_ref[...]
        moved = pltpu.roll(tall[...], _index(_place()), 0).astype(BF16)
        main_ref[...] = moved[0:IN_SLAB]
        tail_ref[...] = moved[IN_SLAB:]

    return pl.pallas_call(
        body, name="shift_w_in", grid=(d // tc,),
        in_specs=[pl.BlockSpec((ws, tc), lambda j: (0, j))],
        out_specs=[pl.BlockSpec((IN_SLAB, tc), lambda j: (0, j)), pl.BlockSpec((IN_TAIL, tc), lambda j: (0, j))],
        out_shape=[jax.ShapeDtypeStruct((IN_SLAB, d), BF16), jax.ShapeDtypeStruct((IN_TAIL, d), BF16)],
        scratch_shapes=[pltpu.VMEM((IN_SLAB + IN_TAIL, tc), F32)], compiler_params=_params("parallel"),
    )(w)


def _sum_adamw_shifted(r, w, m, v, name):
    _, ph, d = r.shape
    ws = w.shape[0]
    tc = _fit(d, 256)

    def body(r_ref, w_ref, m_ref, v_ref, g_ref, d_ref, m2_ref, v2_ref, tall):
        tall[...] = pltpu.roll(_slot_sum(r_ref), lax.rem(ph - _index(_place()), ph), 0)
        g = tall[0:ws, :]
        g_ref[...] = g
        d_ref[...], m2_ref[...], v2_ref[...] = _adam_math(w_ref[...], g, m_ref[...], v_ref[...])

    blk = pl.BlockSpec((ws, tc), lambda j: (0, j))
    out = jax.ShapeDtypeStruct(w.shape, F32)
    return pl.pallas_call(
        body, name=name, grid=(d // tc,),
        in_specs=[pl.BlockSpec((r.shape[0], ph, tc), lambda j: (0, 0, j)), blk, blk, blk],
        out_specs=[blk] * 4, out_shape=[out] * 4,
        scratch_shapes=[pltpu.VMEM((ph, tc), F32)], compiler_params=_params("parallel"),
    )(r, w, m, v)


def _sum_slots(r, name, tr=128):
    _, rows, cols = r.shape
    tr = _rows(rows, tr)

    def body(r_ref, g_ref):
        g_ref[...] = _slot_sum(r_ref)

    return pl.pallas_call(
        body, name=name, grid=(rows // tr,),
        in_specs=[pl.BlockSpec((r.shape[0], tr, cols), lambda i: (0, i, 0))],
        out_specs=pl.BlockSpec((tr, cols), lambda i: (i, 0)),
        out_shape=jax.ShapeDtypeStruct((rows, cols), F32),
        compiler_params=_params("parallel"),
    )(r)


def _adamw(w, g, m, v, name, after, tr=256):
    rows, cols = w.shape
    tr = _rows(rows, tr)

    def body(w_ref, g_ref, m_ref, v_ref, after_ref, d_ref, m2_ref, v2_ref):
        d_ref[...], m2_ref[...], v2_ref[...] = _adam_math(w_ref[...], g_ref[...], m_ref[...], v_ref[...])

    blk = pl.BlockSpec((tr, cols), lambda i: (i, 0))
    out = jax.ShapeDtypeStruct((rows, cols), F32)
    return pl.pallas_call(
        body, name=name, grid=(rows // tr,), in_specs=[blk] * 4 + [pl.BlockSpec(memory_space=pl.ANY)],
        out_specs=[blk] * 3, out_shape=[out] * 3, compiler_params=_params("parallel"),
    )(w, g, m, v, after)


def _sum_adamw(r, w, m, v, name, tr=256):
    rows, cols = w.shape
    tr = _rows(rows, tr)

    def body(r_ref, w_ref, m_ref, v_ref, g_ref, d_ref, m2_ref, v2_ref):
        g = _slot_sum(r_ref)
        g_ref[...] = g
        d_ref[...], m2_ref[...], v2_ref[...] = _adam_math(w_ref[...], g, m_ref[...], v_ref[...])

    blk = pl.BlockSpec((tr, cols), lambda i: (i, 0))
    out = jax.ShapeDtypeStruct((rows, cols), F32)
    return pl.pallas_call(
        body, name=name, grid=(rows // tr,),
        in_specs=[pl.BlockSpec((r.shape[0], tr, cols), lambda i: (0, i, 0)), blk, blk, blk],
        out_specs=[blk] * 4, out_shape=[out] * 4,
        compiler_params=_params("parallel"),
    )(r, w, m, v)


def _pack(pieces, sizes):
    flat = [jnp.pad(p.reshape(-1).astype(F32), (0, s - p.size)) for p, s in zip(pieces, sizes)]
    total = sum(sizes)
    padded = -(-total // (16 * LANE)) * (16 * LANE)
    return jnp.pad(jnp.concatenate(flat), (0, padded - total)).reshape(-1, LANE)


def _unpack(packed, shapes, sizes):
    flat = packed.reshape(-1)
    out, off = [], 0
    for shp, s in zip(shapes, sizes):
        n = 1
        for k in shp:
            n *= k
        out.append(flat[off:off + n].reshape(shp))
        off += s
    return out


def _lanes(n):
    return -(-n // LANE) * LANE


WEIGHTS = ("w_in", "b_gates", "w_sc_conv", "mh_gain", "w_out", "ln1_g", "ln1_b", "w_up", "w_ffn_conv", "b_ffn_conv",
           "w_down", "ln2_g", "ln2_b")
BIG = ("w_in", "w_out", "w_up", "w_down")
SMALL = tuple(n for n in WEIGHTS if n not in BIG)


def kernel(x, w_in, b_gates, w_sc_conv, mh_gain, w_out, ln1_g, ln1_b, w_up, w_ffn_conv, b_ffn_conv, w_down, ln2_g, ln2_b, loss_target, m_w_in, m_b_gates, m_w_sc_conv, m_mh_gain, m_w_out, m_ln1_g, m_ln1_b, m_w_up, m_w_ffn_conv, m_b_ffn_conv, m_w_down, m_ln2_g, m_ln2_b, v_w_in, v_b_gates, v_w_sc_conv, v_mh_gain, v_w_out, v_ln1_g, v_ln1_b, v_w_up, v_w_ffn_conv, v_b_ffn_conv, v_w_down, v_ln2_g, v_ln2_b):
    w = dict(zip(WEIGHTS, (w_in, b_gates, w_sc_conv, mh_gain, w_out, ln1_g, ln1_b, w_up, w_ffn_conv, b_ffn_conv,
                           w_down, ln2_g, ln2_b)))
    m = dict(zip(WEIGHTS, (m_w_in, m_b_gates, m_w_sc_conv, m_mh_gain, m_w_out, m_ln1_g, m_ln1_b, m_w_up,
                           m_w_ffn_conv, m_b_ffn_conv, m_w_down, m_ln2_g, m_ln2_b)))
    v = dict(zip(WEIGHTS, (v_w_in, v_b_gates, v_w_sc_conv, v_mh_gain, v_w_out, v_ln1_g, v_ln1_b, v_w_up,
                           v_w_ffn_conv, v_b_ffn_conv, v_w_down, v_ln2_g, v_ln2_b)))
    me = _index(_place())
    d = x.shape[2]
    ws_in = w_in.shape[2]
    assert ws_in == IN_SLAB + 1 and N_DEV <= LANE, w_in.shape
    ninp = (N_DEV + 1) * IN_SLAB
    ws_sc, ws_fc = w_sc_conv.shape[2], w_ffn_conv.shape[2]
    w_in_t, m_in_t, v_in_t = (jnp.transpose(a[0]) for a in (w_in, m_w_in, v_w_in))

    w_in_main, w_in_tail = _shift_w_in(w_in_t)
    taps8 = lambda a: jnp.pad(a[0], ((0, 5), (0, 0)))
    at_once = ("w_in", "w_tail", "w_sc", "w_fc")
    wx = _Gathering(dict(zip(at_once, (w_in_main, w_in_tail, taps8(w_sc_conv), taps8(w_ffn_conv)))),
                    {n: w[n][0] for n in ("w_out", "w_up", "w_down")}, me)
    token = x_b = _behind(x[0], wx.begin(None)).astype(BF16)
    for n in at_once:
        token = wx.forward(n, token)
    g_in, g_tail, g_sc, g_fc = (wx.get(n, token) for n in at_once)
    w_in_full = _carry_w_in(g_in, g_tail).reshape(ninp, d)
    w_sc_full = g_sc[:, :3].transpose(1, 0, 2).reshape(3, N_DEV * ws_sc)
    w_fc_full = g_fc[:, :3].transpose(1, 0, 2).reshape(3, N_DEV * ws_fc)

    xi, yi, ci = _place()
    names = ("loss",) + SMALL
    pieces = {}

    def gather_small(small, loss_t, after):
        pieces.update(small, loss=loss_t[0, :1])
        sizes = [_lanes(pieces[n].size) for n in names]
        (g_small,) = _all_gather([_behind(_pack([pieces[n] for n in names], sizes), after)], "gather_small")
        return _sum_slots(g_small, "sum_small", tr=g_small.shape[1])

    gx = _Reducing(jnp.reshape(ci, (1,)).astype(jnp.int32), 2 * xi + yi, gather_small)
    loss_t, grad_x, small, _ = _local_step(
        x[0], loss_target[0], w_in_full, b_gates, w_sc_full, mh_gain, None, ln1_g, ln1_b, None,
        w_fc_full, b_ffn_conv, None, ln2_g, ln2_b, gx=gx, wx=wx, x_b=x_b)

    grads, deltas, new_m, new_v = {}, {}, {}, {}
    for name in ("w_down", "w_up", "w_out"):
        grads[name], deltas[name], new_m[name], new_v[name] = _sum_adamw(
            gx.finish(name, gx.token), w[name][0], m[name][0], v[name][0], "adamw_" + name)

    done = sum(t[0:1, 0:1] for t in (grad_x, deltas["w_down"], deltas["w_up"], deltas["w_out"]))
    w_in_out = _sum_adamw_shifted(gx.finish("w_in", done), w_in_t, m_in_t, v_in_t, "adamw_w_in")
    grads["w_in"], deltas["w_in"], new_m["w_in"], new_v["w_in"] = (jnp.transpose(a)[None] for a in w_in_out)

    summed = _unpack(gx.small_sum, [pieces[n].shape for n in names], [_lanes(pieces[n].size) for n in names])
    full = dict(zip(names, summed))
    full["w_sc_conv"] = lax.dynamic_slice(full["w_sc_conv"], (0, me * ws_sc), (3, ws_sc))
    full["w_ffn_conv"] = lax.dynamic_slice(full["w_ffn_conv"], (0, me * ws_fc), (3, ws_fc))
    for n in SMALL:
        grads[n] = full[n].reshape(w[n].shape)
    sizes = [_lanes(w[n].size) for n in SMALL]
    shapes = [w[n].shape for n in SMALL]
    packed = [_pack([t[n] for n in SMALL], sizes) for t in (w, grads, m, v)]
    for res, t in zip(_adamw(*packed, "adamw_small", after=w_in_out[1]), (deltas, new_m, new_v)):
        t.update(zip(SMALL, _unpack(res, shapes, sizes)))

    big = lambda t: {n: (t[n].reshape(w[n].shape) if n in BIG else t[n]) for n in WEIGHTS}
    grads, deltas, new_m, new_v = big(grads), big(deltas), big(new_m), big(new_v)
    return (full["loss"].reshape(()), grad_x[None], *[grads[n] for n in WEIGHTS], *[deltas[n] for n in WEIGHTS],
            *[new_m[n] for n in WEIGHTS], *[new_v[n] for n in WEIGHTS])
```

```python
import functools

import jax
import jax.numpy as jnp
from jax import lax
from jax.experimental import pallas as pl
from jax.experimental.pallas import tpu as pltpu

F32 = jnp.float32
BF16 = jnp.bfloat16
MESH = pl.DeviceIdType.MESH

N_DEV = 8
NH = 4
CHUNK = 64
LN_EPS = 1e-5
HN_EPS = 1e-6
ALPHA = 2.0 ** 0.25
LANE = 128
IN_SLAB = 7 * LANE
IN_TAIL = 16
VMEM_LIMIT = 56 * 1024 * 1024
ADAM_LR, ADAM_B1, ADAM_B2, ADAM_EPS, ADAM_WD, ADAM_STEP = 0.001, 0.9, 0.999, 1e-08, 0.01, 10

_NN = (((1,), (0,)), ((), ()))
_NT = (((1,), (1,)), ((), ()))
_TN = (((0,), (0,)), ((), ()))


def _dot(a, b, dn=_NN):
    return lax.dot_general(a, b, dn, preferred_element_type=F32)


def _params(*sem):
    return pltpu.CompilerParams(dimension_semantics=sem if sem else None, vmem_limit_bytes=VMEM_LIMIT)


def _iota(shape, axis):
    return lax.broadcasted_iota(jnp.int32, shape, axis)


def _fit(n, want):
    if n <= want:
        return n
    t = want - want % LANE
    while n % t:
        t -= LANE
    return t


def _matmul(a, b, mode, out_dtype, name, tm=1024, tn=512, tk=1024, add=None, add_scale=1.0,
            a_blocked=False, b_blocked=False, o_width=None, after=None):
    a_parts = a if isinstance(a, tuple) else None
    b_parts = b if isinstance(b, tuple) else None
    if a_parts:
        a_blocked, (a_rows, wa), na = True, a[0].shape, len(a)
        kd, m = (a_rows, na * wa) if mode == "tn" else (na * wa, a_rows)
    elif a_blocked:
        na, a_rows, wa = a.shape
        kd, m = (a_rows, na * wa) if mode == "tn" else (na * wa, a_rows)
    elif mode == "tn":
        kd, m = a.shape
    else:
        m, kd = a.shape
    if b_parts:
        b_blocked, (rows, w), nb = True, b[0].shape, len(b)
    elif b_blocked:
        nb, rows, w = b.shape
    if b_blocked:
        n = rows if mode == "nt" else nb * w
        assert (nb * w if mode == "nt" else rows) == kd, (name, kd)
    else:
        n = b.shape[0] if mode == "nt" else b.shape[1]
    tm, tn, tk = _fit(m, tm), _fit(n, tn), _fit(kd, tk)
    if a_blocked and mode == "tn":
        tm = _fit(wa, tm)
    if a_blocked and mode != "tn":
        tk = _fit(wa, tk)
    if b_blocked and mode != "nt":
        tn = _fit(w, tn)
    if b_blocked and mode == "nt":
        tk = _fit(w, tk)
    if o_width is not None:
        tn = _fit(o_width, tn)
    assert m % tm == 0 and n % tn == 0 and kd % tk == 0, (name, m, n, kd, tm, tn, tk)
    assert not (a_blocked and mode != "tn" and wa % tk) and not (b_blocked and mode == "nt" and w % tk), (name, tk)
    nk = kd // tk
    dn = {"nn": _NN, "nt": _NT, "tn": _TN}[mode]
    if a_blocked and mode == "tn":
        a_per = wa // tm
        a_spec = pl.BlockSpec((None, tk, tm), lambda i, j, k: (i // a_per, k, i % a_per))
    elif a_blocked:
        a_per = wa // tk
        a_spec = pl.BlockSpec((None, tm, tk), lambda i, j, k: (k // a_per, i, k % a_per))
    elif mode == "tn":
        a_spec = pl.BlockSpec((tk, tm), lambda i, j, k: (k, i))
    else:
        a_spec = pl.BlockSpec((tm, tk), lambda i, j, k: (i, k))
    if b_blocked and mode != "nt":
        per = w // tn
        b_spec = pl.BlockSpec((None, tk, tn), lambda i, j, k: (j // per, k, j % per))
    elif b_blocked:
        per = w // tk
        b_spec = pl.BlockSpec((None, tn, tk), lambda i, j, k: (k // per, j, k % per))
    elif mode == "nt":
        b_spec = pl.BlockSpec((tn, tk), lambda i, j, k: (j, k))
    else:
        b_spec = pl.BlockSpec((tk, tn), lambda i, j, k: (k, j))
    if o_width is None:
        o_spec = pl.BlockSpec((tm, tn), lambda i, j, k: (i, j))
        o_shape = (m, n)
    else:
        oper = o_width // tn
        o_spec = pl.BlockSpec((None, tm, tn), lambda i, j, k: (j // oper, i, j % oper))
        o_shape = (n // o_width, m, o_width)
    a_list, a_specs = [a], [a_spec]
    if a_parts:
        hold = lambda x, s: jnp.clip(x - s * a_per, 0, a_per - 1)
        a_list = list(a_parts)
        a_specs = [(pl.BlockSpec((tk, tm), lambda i, j, k, s=s: (k, hold(i, s))) if mode == "tn"
                    else pl.BlockSpec((tm, tk), lambda i, j, k, s=s: (i, hold(k, s)))) for s in range(na)]
    b_list, b_specs = [b], [b_spec]
    if b_parts:
        hold_b = lambda x, s: jnp.clip(x - s * per, 0, per - 1)
        b_list = list(b_parts)
        b_specs = [(pl.BlockSpec((tn, tk), lambda i, j, k, s=s: (j, hold_b(k, s))) if mode == "nt"
                    else pl.BlockSpec((tk, tn), lambda i, j, k, s=s: (k, hold_b(j, s)))) for s in range(nb)]
    n_a, n_b = len(a_list), len(b_list)
    has_add = add is not None
    n_in = n_a + n_b + has_add + (after is not None)
    in_place = nk > 1 and out_dtype == F32

    def body(*refs):
        add_ref = refs[n_a + n_b] if has_add else None
        o_ref = refs[n_in]
        i, j, k = pl.program_id(0), pl.program_id(1), pl.program_id(2)

        def finish(r):
            if has_add:
                r = r + add_scale * add_ref[...]
            o_ref[...] = r.astype(out_dtype)

        def step(a_ref, b_ref):
            if nk == 1:
                finish(_dot(a_ref[...], b_ref[...], dn))
                return
            acc = o_ref if in_place else refs[-1]

            @pl.when(k == 0)
            def _():
                acc[...] = _dot(a_ref[...], b_ref[...], dn)

            @pl.when(k > 0)
            def _():
                acc[...] += _dot(a_ref[...], b_ref[...], dn)

        if n_a == 1 and n_b == 1:
            step(refs[0], refs[1])
        else:
            slab_a = ((i if mode == "tn" else k) // a_per) if n_a > 1 else 0
            slab_b = ((k if mode == "nt" else j) // per) if n_b > 1 else 0
            for sa in range(n_a):
                for sb in range(n_b):
                    pl.when((slab_a == sa) & (slab_b == sb))(functools.partial(step, refs[sa], refs[n_a + sb]))
        if nk > 1 and not (in_place and not has_add):
            @pl.when(k == nk - 1)
            def _():
                finish((o_ref if in_place else refs[-1])[...])

    in_specs = a_specs + b_specs + ([pl.BlockSpec((tm, tn), lambda i, j, k: (i, j))] if has_add else [])
    args = (*a_list, *b_list) + ((add,) if has_add else ())
    if after is not None:
        in_specs.append(pl.BlockSpec(memory_space=pl.ANY))
        args += (after,)
    return pl.pallas_call(
        body, name=name, grid=(m // tm, n // tn, nk),
        in_specs=in_specs, out_specs=o_spec,
        out_shape=jax.ShapeDtypeStruct(o_shape, out_dtype),
        scratch_shapes=[pltpu.VMEM((tm, tn), F32)] if nk > 1 and not in_place else [],
        compiler_params=_params("parallel", "parallel", "arbitrary"),
    )(*args)


def _shift_down(u, s):
    return jnp.where(_iota(u.shape, 0) >= s, pltpu.roll(u, s, 0), 0.0)


def _shift_up(u, s):
    t = u.shape[0]
    return jnp.where(_iota(u.shape, 0) < t - s, pltpu.roll(u, t - s, 0), 0.0)


SLAB = 8


def _rolled(u):
    return pltpu.roll(u, 2, 0), pltpu.roll(u, 1, 0)


def _conv(u, w, rolled=None):
    u2, u1 = _rolled(u) if rolled is None else rolled
    raw = w[0:1] * u2 + w[1:2] * u1 + w[2:3] * u
    head = u[0:SLAB]
    mended = w[0:1] * _shift_down(head, 2) + w[1:2] * _shift_down(head, 1) + w[2:3] * head
    return jnp.concatenate([mended, raw[SLAB:]], axis=0)


def _conv_t(dy, w):
    t = dy.shape[0]
    raw = w[2:3] * dy + w[1:2] * pltpu.roll(dy, t - 1, 0) + w[0:1] * pltpu.roll(dy, t - 2, 0)
    tail = dy[t - SLAB:]
    mended = w[2:3] * tail + w[1:2] * _shift_up(tail, 1) + w[0:1] * _shift_up(tail, 2)
    return jnp.concatenate([raw[:t - SLAB], mended], axis=0)


def _conv_dw(dy, u, rolled=None):
    t = dy.shape[0]
    u2, u1 = _rolled(u) if rolled is None else rolled
    head, tail = dy[0:SLAB], u[t - SLAB:]
    r = _iota(head.shape, 0)
    wrap2 = jnp.sum(jnp.where(r < 2, head * pltpu.roll(tail, 2, 0), 0.0), axis=0, keepdims=True)
    wrap1 = jnp.sum(jnp.where(r < 1, head * pltpu.roll(tail, 1, 0), 0.0), axis=0, keepdims=True)
    d0 = jnp.sum(dy * u2, axis=0, keepdims=True) - wrap2
    d1 = jnp.sum(dy * u1, axis=0, keepdims=True) - wrap1
    d2 = jnp.sum(dy * u, axis=0, keepdims=True)
    r3 = _iota((3, dy.shape[1]), 0)
    return jnp.where(r3 == 0, d0, jnp.where(r3 == 1, d1, d2))


def _sigmoid(x):
    return 0.5 * jnp.tanh(0.5 * x) + 0.5


def _sconv_fwd(proj, w_sc, t, wc):
    nb = wc // LANE

    def body(cb_ref, cc_ref, ch_ref, w_ref, y_ref):
        u = cc_ref[...] * ch_ref[...]
        y_ref[...] = (cb_ref[...] * _conv(u, w_ref[...])).astype(BF16)

    col = lambda off: pl.BlockSpec((t, LANE), lambda j: (0, j + off))
    return pl.pallas_call(
        body, name="sconv_fwd", grid=(nb,),
        in_specs=[col(0), col(nb), col(2 * nb), pl.BlockSpec((3, LANE), lambda j: (0, j))],
        out_specs=pl.BlockSpec((None, t, LANE), lambda j: (0, 0, j)),
        out_shape=jax.ShapeDtypeStruct((2, t, wc), BF16),
        compiler_params=_params("parallel"),
    )(proj, proj, proj, w_sc)


def _sconv_bwd(dy, proj, w_sc, t, wc):
    nb = wc // LANE

    def body(dy_ref, cb_ref, cc_ref, ch_ref, w_ref, dcb_ref, dcc_ref, dch_ref, dw_ref):
        cc, ch, w, d = cc_ref[...], ch_ref[...], w_ref[...], dy_ref[...]
        u = cc * ch
        ru = _rolled(u)
        dcb_ref[...] = (d * _conv(u, w, ru)).astype(BF16)
        dcu = d * cb_ref[...]
        dw_ref[...] = _conv_dw(dcu, u, ru)
        du = _conv_t(dcu, w)
        dcc_ref[...] = (du * ch).astype(BF16)
        dch_ref[...] = (du * cc).astype(BF16)

    col = lambda off: pl.BlockSpec((t, LANE), lambda j: (0, j + off))
    act = jax.ShapeDtypeStruct((t, wc), BF16)
    return pl.pallas_call(
        body, name="sconv_bwd", grid=(nb,),
        in_specs=[col(0), col(0), col(nb), col(2 * nb), pl.BlockSpec((3, LANE), lambda j: (0, j))],
        out_specs=[col(0), col(0), col(0), pl.BlockSpec((3, LANE), lambda j: (0, j))],
        out_shape=[act, act, act, jax.ShapeDtypeStruct((3, wc), F32)],
        compiler_params=_params("parallel"),
    )(dy, proj, proj, proj, w_sc)


def _gates_prep(proj, bias_tile, t, gate_tile):
    def body(g_ref, b_ref, o_ref):
        g = g_ref[...] + b_ref[...]
        lane = _iota(g.shape, 1)
        is_f = (lane >= NH) & (lane < 2 * NH)
        lf = jnp.minimum(g, 0.0) - jnp.log(1.0 + jnp.exp(-jnp.abs(g)))
        c = jnp.where(is_f, lf, 0.0)
        r = _iota(g.shape, 0) % CHUNK
        s = 1
        while s < CHUNK:
            c = c + jnp.where(r >= s, pltpu.roll(c, s, 0), 0.0)
            s *= 2
        o_ref[...] = jnp.where(is_f, c, jnp.where(lane < NH, g, 0.0))

    return pl.pallas_call(
        body, name="gates_prep", grid=(1,),
        in_specs=[pl.BlockSpec((t, LANE), lambda i: (0, gate_tile)), pl.BlockSpec((1, LANE), lambda i: (0, 0))],
        out_specs=pl.BlockSpec((t, LANE), lambda i: (0, 0)),
        out_shape=jax.ShapeDtypeStruct((t, LANE), F32),
        compiler_params=_params("arbitrary"),
    )(proj, bias_tile)


def _gates_bwd(dgate, proj, bias_tile, t, gate_tile):
    def body(dg_ref, g_ref, b_ref, o_ref, s_ref):
        g = g_ref[...] + b_ref[...]
        lane = _iota(g.shape, 1)
        r = _iota(g.shape, 0) % CHUNK
        dsig = 1.0 - _sigmoid(g)
        out = jnp.zeros(g.shape, F32)
        for h in range(NH):
            d = dg_ref[h]
            c = d
            s = 1
            while s < CHUNK:
                c = c + jnp.where(r + s < CHUNK, pltpu.roll(c, t - s, 0), 0.0)
                s *= 2
            di = jnp.broadcast_to(d[:, 0:1], g.shape)
            db = jnp.broadcast_to(c[:, 1:2], g.shape)
            out = out + jnp.where(lane == h, di, 0.0) + jnp.where(lane == NH + h, db * dsig, 0.0)
        o_ref[...] = out.astype(BF16)
        s_ref[...] = jnp.sum(out, axis=0, keepdims=True)

    return pl.pallas_call(
        body, name="gates_bwd", grid=(1,),
        in_specs=[pl.BlockSpec((NH, t, LANE), lambda i: (0, 0, 0)),
                  pl.BlockSpec((t, LANE), lambda i: (0, gate_tile)), pl.BlockSpec((1, LANE), lambda i: (0, 0))],
        out_specs=[pl.BlockSpec((t, LANE), lambda i: (0, 0)), pl.BlockSpec((1, LANE), lambda i: (0, 0))],
        out_shape=[jax.ShapeDtypeStruct((t, LANE), BF16), jax.ShapeDtypeStruct((1, LANE), F32)],
        compiler_params=_params("arbitrary"),
    )(dgate, proj, bias_tile)


def _in_turn(heads):
    while heads:
        heads = [g for g in heads if next(g, heads) is not heads]


def _chunk_gates(gc, gr, h, mprev):
    L = CHUNK
    icol, bcol = gc[:, h:h + 1], gc[:, h + NH:h + NH + 1]
    irow, brow = gr[h:h + 1, :], gr[h + NH:h + NH + 1, :]
    tri = _iota((L, L), 0) >= _iota((L, L), 1)
    log_d = jnp.where(tri, bcol - brow + irow, -jnp.inf)
    inter = bcol + mprev
    mt = jnp.maximum(inter, jnp.max(log_d, axis=1, keepdims=True))
    dw = jnp.exp(log_d - mt)
    iw = jnp.exp(inter - mt)
    g = brow[:, L - 1:L]
    wlog_col = g - bcol + icol
    wlog_row = g - brow + irow
    mnew = jnp.maximum(g + mprev, jnp.max(wlog_row, axis=1, keepdims=True))
    wcol = jnp.exp(wlog_col - mnew)
    decay = jnp.exp(g + mprev - mnew)
    return dw, iw, mt, wcol, decay, mnew


def _mlstm_fwd(proj, gcol, grow, t, wc, dh):
    nc = t // CHUNK
    wm = NH * dh
    assert wc == wm, (wc, wm)
    qoff = 3 * wc // wm
    scale = dh ** -0.5

    def body(q_ref, k_ref, v_ref, gc_ref, gr_ref, h_ref, cs_ref, ns_ref, c_s, n_s, m_s):
        @pl.when(pl.program_id(0) == 0)
        def _():
            c_s[...] = jnp.zeros_like(c_s)
            n_s[...] = jnp.zeros_like(n_s)
            m_s[...] = jnp.zeros_like(m_s)

        gc, gr = gc_ref[...], gr_ref[0]
        done = [None] * NH

        def head(h):
            cols = slice(h * dh, (h + 1) * dh)
            mprev = m_s[h, 0:1, 0:1]
            cprev = c_s[h]
            n8 = n_s[h]
            nprev = n8[0:1]
            qs = q_ref[:, cols] * scale
            k = k_ref[:, cols]
            qs_b, k_b, v_b = qs.astype(BF16), k.astype(BF16), v_ref[:, cols].astype(BF16)
            qk = _dot(qs_b, k_b, _NT)
            yield
            q_c = _dot(qs_b, cprev.astype(BF16))
            yield
            dw, iw, mt, wcol, decay, mnew = _chunk_gates(gc, gr, h, mprev)
            yield
            s = qk * dw
            wk = wcol * k
            num = _dot(s.astype(BF16), v_b) + iw * q_c
            yield
            c_new = decay * cprev + _dot(wk.astype(BF16), v_b, _TN)
            yield
            den = jnp.sum(s, axis=1, keepdims=True) + iw * jnp.sum(qs * nprev, axis=1, keepdims=True)
            done[h] = (cprev, jnp.where(_iota(n8.shape, 0) == 1, mprev, n8),
                       num / jnp.maximum(jnp.abs(den), jnp.exp(-mt)), c_new,
                       decay * n8 + jnp.sum(wk, axis=0, keepdims=True), mnew)

        _in_turn([head(h) for h in range(NH)])
        for h, (c_old, n_old, h_out, c_new, n_new, m_new) in enumerate(done):
            cs_ref[h] = c_old
            ns_ref[h] = n_old
            h_ref[:, h * dh:(h + 1) * dh] = h_out
            c_s[h] = c_new
            n_s[h] = n_new
            m_s[h] = jnp.broadcast_to(m_new, m_s.shape[1:])

    grp = lambda off: pl.BlockSpec((CHUNK, wm), lambda c: (c, qoff + off))
    return pl.pallas_call(
        body, name="mlstm_fwd", grid=(nc,),
        in_specs=[grp(0), grp(1), grp(2),
                  pl.BlockSpec((CHUNK, LANE), lambda c: (c, 0)),
                  pl.BlockSpec((1, 8, CHUNK), lambda c: (c, 0, 0))],
        out_specs=[pl.BlockSpec((CHUNK, wm), lambda c: (c, 0)),
                   pl.BlockSpec((NH, None, dh, dh), lambda c: (0, c, 0, 0)),
                   pl.BlockSpec((NH, None, 8, dh), lambda c: (0, c, 0, 0))],
        out_shape=[jax.ShapeDtypeStruct((t, wm), F32),
                   jax.ShapeDtypeStruct((NH, nc, dh, dh), F32),
                   jax.ShapeDtypeStruct((NH, nc, 8, dh), F32)],
        scratch_shapes=[pltpu.VMEM((NH, dh, dh), F32), pltpu.VMEM((NH, 8, dh), F32), pltpu.VMEM((NH, 8, LANE), F32)],
        compiler_params=_params("arbitrary"),
    )(proj, proj, proj, gcol, grow)


def _mlstm_bwd(proj, gcol, grow, hval, dh_in, cs, ns, t, wc, dh):
    nc = t // CHUNK
    wm = NH * dh
    assert wc == wm, (wc, wm)
    qoff = 3 * wc // wm
    scale = dh ** -0.5
    L = CHUNK

    def body(q_ref, k_ref, v_ref, gc_ref, gr_ref, h_ref, dh_ref, cs_ref, ns_ref,
             dq_ref, dk_ref, dv_ref, dg_ref, dc_s, dn_s):
        @pl.when(pl.program_id(0) == 0)
        def _():
            dc_s[...] = jnp.zeros_like(dc_s)
            dn_s[...] = jnp.zeros_like(dn_s)

        gc, gr = gc_ref[...], gr_ref[0]
        eye = _iota((L, L), 0) == _iota((L, L), 1)
        lane = _iota((L, LANE), 1)
        last = _iota((L, 1), 0) == L - 1
        done = [None] * NH

        def head(h):
            cols = slice(h * dh, (h + 1) * dh)
            ns8 = ns_ref[h]
            nprev = ns8[0:1]
            mprev = ns8[1:2, 0:1]
            cprev = cs_ref[h]
            dcn = dc_s[h]
            dn8 = dn_s[h]
            dnn = dn8[0:1]

            qs = q_ref[:, cols] * scale
            k = k_ref[:, cols]
            qs_b, k_b, v_b = qs.astype(BF16), k.astype(BF16), v_ref[:, cols].astype(BF16)
            qk = _dot(qs_b, k_b, _NT)
            yield
            dw, iw, mt, wcol, decay, _ = _chunk_gates(gc, gr, h, mprev)
            yield
            s = qk * dw
            den = jnp.sum(s, axis=1, keepdims=True) + iw * jnp.sum(qs * nprev, axis=1, keepdims=True)
            emt = jnp.exp(-mt)
            r = 1.0 / jnp.maximum(jnp.abs(den), emt)
            dout = dh_ref[:, cols]
            dnum = dout * r
            dden = (-jnp.sum(dout * h_ref[:, cols], axis=1, keepdims=True) * r
                    * jnp.where(jnp.abs(den) > emt, jnp.sign(den), 0.0))
            dnum_b = dnum.astype(BF16)
            cprev_b = cprev.astype(BF16)
            dcn_b = dcn.astype(BF16)
            yield

            g_raw = _dot(dnum_b, v_b, _NT)
            yield
            q_inter = _dot(dnum_b, cprev_b, _NT)
            yield
            k_raw = _dot(v_b, dcn_b, _NT)
            yield
            gd = (g_raw + dden) * dw
            gd_b = gd.astype(BF16)
            dqs_inter = iw * (q_inter + dden * nprev)
            dk_inter = wcol * (k_raw + dnn)
            wk = wcol * k
            iq = iw * qs
            dqs = _dot(gd_b, k_b) + dqs_inter
            yield
            dk = _dot(gd_b, qs_b, _TN) + dk_inter
            yield
            dv = _dot(s.astype(BF16), dnum_b, _TN) + _dot(wk.astype(BF16), dcn_b)
            yield
            dc_new = decay * dcn + _dot(iq.astype(BF16), dnum_b, _TN)
            yield

            e = gd * qk
            e_cols = jnp.sum(jnp.where(eye, jnp.sum(e, axis=0, keepdims=True), 0.0), axis=1, keepdims=True)
            yield
            k_inter = jnp.sum(k * dk_inter, axis=1, keepdims=True)
            rq = jnp.sum(e, axis=1, keepdims=True) + jnp.sum(qs * dqs_inter, axis=1, keepdims=True)
            rk = e_cols + k_inter
            hsum = jnp.sum(k_inter, axis=0, keepdims=True)
            jdec = decay * (jnp.sum(jnp.sum(dcn * cprev, axis=1, keepdims=True), axis=0, keepdims=True)
                            + jnp.sum(dnn * nprev, axis=1, keepdims=True))
            db = rq - rk + jnp.where(last, hsum + jdec, 0.0)
            done[h] = (jnp.where(lane == 0, rk, jnp.where(lane == 1, db, 0.0)),
                       (dqs * scale).astype(BF16), dk.astype(BF16), dv.astype(BF16), dc_new,
                       decay * dn8 + jnp.sum(iq * dden, axis=0, keepdims=True))

        _in_turn([head(h) for h in range(NH)])
        for h, (dgate, dq, dk, dv, dc_new, dn_new) in enumerate(done):
            cols = slice(h * dh, (h + 1) * dh)
            dg_ref[h] = dgate
            dq_ref[:, cols] = dq
            dk_ref[:, cols] = dk
            dv_ref[:, cols] = dv
            dc_s[h] = dc_new
            dn_s[h] = dn_new

    rc = lambda c: nc - 1 - c
    grp = lambda off: pl.BlockSpec((L, wm), lambda c: (rc(c), qoff + off))
    hm = pl.BlockSpec((L, wm), lambda c: (rc(c), 0))
    act = jax.ShapeDtypeStruct((t, wm), BF16)
    return pl.pallas_call(
        body, name="mlstm_bwd", grid=(nc,),
        in_specs=[grp(0), grp(1), grp(2),
                  pl.BlockSpec((L, LANE), lambda c: (rc(c), 0)),
                  pl.BlockSpec((1, 8, L), lambda c: (rc(c), 0, 0)),
                  hm, hm,
                  pl.BlockSpec((NH, None, dh, dh), lambda c: (0, rc(c), 0, 0)),
                  pl.BlockSpec((NH, None, 8, dh), lambda c: (0, rc(c), 0, 0))],
        out_specs=[hm, hm, hm, pl.BlockSpec((NH, L, LANE), lambda c: (0, rc(c), 0))],
        out_shape=[act, act, act, jax.ShapeDtypeStruct((NH, t, LANE), F32)],
        scratch_shapes=[pltpu.VMEM((NH, dh, dh), F32), pltpu.VMEM((NH, 8, dh), F32)],
        compiler_params=_params("arbitrary"),
    )(proj, proj, proj, gcol, grow, hval, dh_in, cs, ns)


def _head_norm(hv):
    mu = jnp.mean(hv, axis=1, keepdims=True)
    hc = hv - mu
    rstd = lax.rsqrt(jnp.mean(hc * hc, axis=1, keepdims=True) + HN_EPS)
    return hc * rstd, rstd


def _hnorm_fwd(hval, proj, gain, y, t, wc, dh, tr=512):
    ooff = 3 * wc // dh + 3 * NH
    tr = min(tr, t)

    def body(h_ref, o_ref, g_ref, y_in, y_ref):
        hhat, _ = _head_norm(h_ref[...])
        y_ref[...] = (_sigmoid(o_ref[...]) * hhat * g_ref[...]).astype(BF16)

    return pl.pallas_call(
        body, name="hnorm_fwd", grid=(t // tr, NH),
        in_specs=[pl.BlockSpec((tr, dh), lambda i, h: (i, h)),
                  pl.BlockSpec((tr, dh), lambda i, h: (i, ooff + h)),
                  pl.BlockSpec((1, dh), lambda i, h: (0, h)),
                  pl.BlockSpec(memory_space=pl.ANY)],
        out_specs=pl.BlockSpec((None, tr, dh), lambda i, h: (1, i, h)),
        out_shape=jax.ShapeDtypeStruct(y.shape, BF16),
        input_output_aliases={3: 0},
        compiler_params=_params("parallel", "parallel"),
    )(hval, proj, gain, y)


def _hnorm_bwd(dy, hval, proj, gain, t, wc, dh, tr=512):
    ooff = 3 * wc // dh + 3 * NH
    tr = min(tr, t)
    yoff = wc // dh

    def body(dy_ref, h_ref, o_ref, g_ref, do_ref, dh_ref, dg_ref):
        i = pl.program_id(1)
        hhat, rstd = _head_norm(h_ref[...])
        gain_v = g_ref[...]
        sig = _sigmoid(o_ref[...])
        d = dy_ref[...]
        do_ref[...] = (d * hhat * gain_v * sig * (1.0 - sig)).astype(BF16)
        dhn = d * sig
        part = jnp.sum(dhn * hhat, axis=0, keepdims=True)

        @pl.when(i == 0)
        def _():
            dg_ref[...] = part

        @pl.when(i > 0)
        def _():
            dg_ref[...] += part

        dhat = dhn * gain_v
        dh_ref[...] = rstd * (dhat - jnp.mean(dhat, axis=1, keepdims=True)
                              - hhat * jnp.mean(dhat * hhat, axis=1, keepdims=True))

    blk = lambda off: pl.BlockSpec((tr, dh), lambda h, i: (i, off + h))
    return pl.pallas_call(
        body, name="hnorm_bwd", grid=(NH, t // tr),
        in_specs=[blk(yoff), blk(0), blk(ooff), pl.BlockSpec((1, dh), lambda h, i: (0, h))],
        out_specs=[blk(0), blk(0), pl.BlockSpec((1, dh), lambda h, i: (0, h))],
        out_shape=[jax.ShapeDtypeStruct((t, NH * dh), BF16), jax.ShapeDtypeStruct((t, NH * dh), F32),
                   jax.ShapeDtypeStruct((1, NH * dh), F32)],
        compiler_params=_params("parallel", "arbitrary"),
    )(dy, hval, proj, gain)


def _ln_stats(z):
    mu = jnp.mean(z, axis=1, keepdims=True)
    zc = z - mu
    rstd = lax.rsqrt(jnp.mean(zc * zc, axis=1, keepdims=True) + LN_EPS)
    return zc * rstd, rstd


def _ln_bwd(dy, xhat, rstd, g):
    dxh = dy * g
    return rstd * (dxh - jnp.mean(dxh, axis=1, keepdims=True) - xhat * jnp.mean(dxh * xhat, axis=1, keepdims=True))


def _accum(ref, i, part):
    @pl.when(i == 0)
    def _():
        ref[...] = part

    @pl.when(i > 0)
    def _():
        ref[...] += part


def _ln1_fwd(x, mix, g, b, tr=256):
    t, d = x.shape

    def body(x_ref, m_ref, g_ref, b_ref, xh_ref, rs_ref, xb_ref):
        xhat, rstd = _ln_stats(ALPHA * x_ref[...] + m_ref[...])
        xh_ref[...] = xhat
        rs_ref[...] = rstd
        xb_ref[...] = (xhat * g_ref[...] + b_ref[...]).astype(BF16)

    row = pl.BlockSpec((tr, d), lambda i: (i, 0))
    vec = pl.BlockSpec((1, d), lambda i: (0, 0))
    return pl.pallas_call(
        body, name="ln1_fwd", grid=(t // tr,),
        in_specs=[row, row, vec, vec],
        out_specs=[row, pl.BlockSpec((tr, 1), lambda i: (i, 0)), row],
        out_shape=[jax.ShapeDtypeStruct((t, d), F32), jax.ShapeDtypeStruct((t, 1), F32),
                   jax.ShapeDtypeStruct((t, d), BF16)],
        compiler_params=_params("parallel"),
    )(x, mix, g, b)


def _ln2_loss(xhat1, g1, b1, ff, target, g2, b2, tr=256):
    t, d = ff.shape

    def body(xh_ref, g1_ref, b1_ref, f_ref, t_ref, g_ref, b_ref, dz_ref, dzb_ref, dg_ref, db_ref, l_ref):
        i = pl.program_id(0)
        x1 = xh_ref[...] * g1_ref[...] + b1_ref[...]
        xhat, rstd = _ln_stats(ALPHA * x1 + f_ref[...])
        gv = g_ref[...]
        e = xhat * gv + b_ref[...] - t_ref[...]
        lsum = jnp.sum(jnp.sum(e * e, axis=1, keepdims=True), axis=0, keepdims=True) * (0.5 / d)
        dy = e * (1.0 / d)
        _accum(dg_ref, i, jnp.sum(dy * xhat, axis=0, keepdims=True))
        _accum(db_ref, i, jnp.sum(dy, axis=0, keepdims=True))
        _accum(l_ref, i, jnp.broadcast_to(lsum, l_ref.shape))
        dz = _ln_bwd(dy, xhat, rstd, gv)
        dz_ref[...] = dz
        dzb_ref[...] = dz.astype(BF16)

    row = pl.BlockSpec((tr, d), lambda i: (i, 0))
    vec = pl.BlockSpec((1, d), lambda i: (0, 0))
    return pl.pallas_call(
        body, name="ln2_loss", grid=(t // tr,),
        in_specs=[row, vec, vec, row, row, vec, vec],
        out_specs=[row, row, vec, vec, pl.BlockSpec((8, LANE), lambda i: (0, 0))],
        out_shape=[jax.ShapeDtypeStruct((t, d), F32), jax.ShapeDtypeStruct((t, d), BF16),
                   jax.ShapeDtypeStruct((1, d), F32), jax.ShapeDtypeStruct((1, d), F32),
                   jax.ShapeDtypeStruct((8, LANE), F32)],
        compiler_params=_params("arbitrary"),
    )(xhat1, g1, b1, ff, target, g2, b2)


def _ln1_bwd(dz2, dffn, xhat1, rstd1, g1, tr=256):
    t, d = dz2.shape

    def body(a_ref, f_ref, xh_ref, rs_ref, g_ref, dz_ref, dzb_ref, dg_ref, db_ref):
        i = pl.program_id(0)
        dy = ALPHA * a_ref[...] + f_ref[...]
        xhat = xh_ref[...]
        _accum(dg_ref, i, jnp.sum(dy * xhat, axis=0, keepdims=True))
        _accum(db_ref, i, jnp.sum(dy, axis=0, keepdims=True))
        dz = _ln_bwd(dy, xhat, rs_ref[...], g_ref[...])
        dz_ref[...] = dz
        dzb_ref[...] = dz.astype(BF16)

    row = pl.BlockSpec((tr, d), lambda i: (i, 0))
    vec = pl.BlockSpec((1, d), lambda i: (0, 0))
    return pl.pallas_call(
        body, name="ln1_bwd", grid=(t // tr,),
        in_specs=[row, row, row, pl.BlockSpec((tr, 1), lambda i: (i, 0)), vec],
        out_specs=[row, row, vec, vec],
        out_shape=[jax.ShapeDtypeStruct((t, d), F32), jax.ShapeDtypeStruct((t, d), BF16),
                   jax.ShapeDtypeStruct((1, d), F32), jax.ShapeDtypeStruct((1, d), F32)],
        compiler_params=_params("arbitrary"),
    )(dz2, dffn, xhat1, rstd1, g1)


def _ffn_act_fwd(hid0, w_fc, b_fc, t, dff):
    nb = dff // LANE

    def body(hv_ref, hg_ref, wv_ref, wg_ref, bv_ref, bg_ref, a_ref):
        val = _conv(hv_ref[...], wv_ref[...]) + bv_ref[...]
        gate = _conv(hg_ref[...], wg_ref[...]) + bg_ref[...]
        a_ref[...] = (gate * _sigmoid(gate) * val).astype(BF16)

    col = lambda off: pl.BlockSpec((t, LANE), lambda j: (0, j + off))
    w3 = lambda off: pl.BlockSpec((3, LANE), lambda j: (0, j + off))
    w1 = lambda off: pl.BlockSpec((1, LANE), lambda j: (0, j + off))
    return pl.pallas_call(
        body, name="ffn_act_fwd", grid=(nb,),
        in_specs=[col(0), col(nb), w3(0), w3(nb), w1(0), w1(nb)],
        out_specs=col(0),
        out_shape=jax.ShapeDtypeStruct((t, dff), BF16),
        compiler_params=_params("parallel"),
    )(hid0, hid0, w_fc, w_fc, b_fc, b_fc)


def _ffn_act_bwd(da, hid0, w_fc, b_fc, t, dff):
    nb = dff // LANE

    def body(da_ref, hv_ref, hg_ref, wv_ref, wg_ref, bv_ref, bg_ref,
             dhv_ref, dhg_ref, dwv_ref, dwg_ref, dbv_ref, dbg_ref):
        hv, hg, wv, wg = hv_ref[...], hg_ref[...], wv_ref[...], wg_ref[...]
        rv, rg = _rolled(hv), _rolled(hg)
        val = _conv(hv, wv, rv) + bv_ref[...]
        gate = _conv(hg, wg, rg) + bg_ref[...]
        sig = _sigmoid(gate)
        d = da_ref[...]
        dsig = d * sig
        dval = dsig * gate
        dgate = dsig * val * (1.0 + gate * (1.0 - sig))
        dhv_ref[...] = _conv_t(dval, wv).astype(BF16)
        dhg_ref[...] = _conv_t(dgate, wg).astype(BF16)
        dwv_ref[...] = _conv_dw(dval, hv, rv)
        dwg_ref[...] = _conv_dw(dgate, hg, rg)
        dbv_ref[...] = jnp.sum(dval, axis=0, keepdims=True)
        dbg_ref[...] = jnp.sum(dgate, axis=0, keepdims=True)

    col = lambda off: pl.BlockSpec((t, LANE), lambda j: (0, j + off))
    w3 = lambda off: pl.BlockSpec((3, LANE), lambda j: (0, j + off))
    w1 = lambda off: pl.BlockSpec((1, LANE), lambda j: (0, j + off))
    s3 = jax.ShapeDtypeStruct((3, dff), F32)
    s1 = jax.ShapeDtypeStruct((1, dff), F32)
    return pl.pallas_call(
        body, name="ffn_act_bwd", grid=(nb,),
        in_specs=[col(0), col(0), col(nb), w3(0), w3(nb), w1(0), w1(nb)],
        out_specs=[col(0), col(0), w3(0), w3(0), w1(0), w1(0)],
        out_shape=[jax.ShapeDtypeStruct((t, dff), BF16)] * 2 + [s3, s3, s1, s1],
        compiler_params=_params("parallel"),
    )(da, hid0, hid0, w_fc, w_fc, b_fc, b_fc)


class _Ready:
    def __init__(self, **weights):
        self.weights = weights

    def begin(self, after):
        return None

    def forward(self, name, after):
        return None

    def get(self, name, after):
        return self.weights[name]


class _Kept:
    def __init__(self):
        self.grads = {}

    def start(self, name, grad):
        self.grads[name] = grad
        return None

    def relay(self, name, after):
        return None

    def meanwhile(self, small, loss, after):
        return None


def _behind(a, token):
    return a if token is None else a + token[0:1, 0:1].reshape((1,) * a.ndim)


def _local_step(x, target, w_in, b_gates, w_sc, gain, w_out, ln1_g, ln1_b, w_up, w_fc, b_fc, w_down, ln2_g, ln2_b,
                gx=None, wx=None, x_b=None):
    t, d = x.shape
    wc = d // 2
    dh = (d - wc) // NH
    wm = NH * dh
    dff = w_fc.shape[1] // 2
    if wx is None:
        wx = _Ready(w_out=w_out, w_up=w_up, w_down=w_down)
    ninp = w_in.shape[0]
    nin = 3 * wc + 4 * wm
    gate_tile = nin // LANE
    nc = t // CHUNK
    bias_tile = jnp.pad(b_gates, ((0, 0), (0, LANE - 2 * NH)))

    if x_b is None:
        x_b = x.astype(BF16)
    proj = _matmul(x_b, w_in, "nt", F32, "proj", tm=512, tn=2688, tk=d, after=wx.begin(w_in))
    y = _sconv_fwd(proj, w_sc, t, wc)
    gcol = _gates_prep(proj, bias_tile, t, gate_tile)
    grow = gcol[:, :8].T.reshape(8, nc, CHUNK).transpose(1, 0, 2)
    hval, cs, ns = _mlstm_fwd(proj, gcol, grow, t, wc, dh)
    y = _hnorm_fwd(hval, proj, gain, y, t, wc, dh)
    tok = wx.forward("w_out", y)
    w_out = wx.get("w_out", tok)
    mix = _matmul(y, w_out, "nn", F32, "out_proj", tm=512, tn=1024, tk=wc, a_blocked=True, after=tok)
    xhat1, rstd1, x1_b = _ln1_fwd(x, mix, _behind(ln1_g, wx.forward("w_up", mix)), ln1_b)
    w_up = wx.get("w_up", x1_b)
    wsl = w_up.shape[2]
    hid0 = _matmul(x1_b, w_up, "nn", F32, "ffn_up", tm=512, tn=wsl, tk=d, b_blocked=True)
    act = _ffn_act_fwd(hid0, w_fc, _behind(b_fc, wx.forward("w_down", hid0)), t, dff)
    w_down = wx.get("w_down", act)
    ff = _matmul(act, w_down, "nn", F32, "ffn_down", tm=1024, tn=512, tk=dff)
    dz2, dz2_b, d_ln2_g, d_ln2_b, loss = _ln2_loss(xhat1, ln1_g, ln1_b, ff, target, ln2_g, ln2_b)

    if gx is None:
        gx = _Kept()
    d_w_down = _matmul(act, dz2_b, "tn", BF16, "ffn_down_dw", tm=512, tn=1024, tk=t)
    d_act = _matmul(dz2_b, w_down, "nt", F32, "ffn_down_dx", tm=1024, tn=512, tk=d, after=gx.start("w_down", d_w_down))
    *d_hid0, dwv, dwg, dbv, dbg = _ffn_act_bwd(d_act, hid0, w_fc, _behind(b_fc, gx.relay("w_down", d_act)), t, dff)
    d_w_fc = jnp.concatenate([dwv, dwg], axis=1)
    d_b_fc = jnp.concatenate([dbv, dbg], axis=1)
    d_hid0 = tuple(d_hid0[:2])
    d_w_up = _matmul(x1_b, d_hid0, "tn", BF16, "ffn_up_dw", tm=512, tn=wsl, tk=t, o_width=wsl)
    d_x1_ffn = _matmul(d_hid0, w_up, "nt", F32, "ffn_up_dx", tm=1024, tn=1024, tk=wsl, b_blocked=True,
                       after=gx.start("w_up", d_w_up))
    dz1, dz1_b, d_ln1_g, d_ln1_b = _ln1_bwd(dz2, d_x1_ffn, xhat1, rstd1, _behind(ln1_g, gx.relay("w_up", d_x1_ffn)))

    d_w_out = _matmul(y, dz1_b, "tn", BF16, "out_proj_dw", tm=512, tn=1024, tk=t, a_blocked=True)
    dy = _matmul(dz1_b, w_out, "nt", F32, "out_proj_dx", tm=512, tn=1024, tk=d, after=gx.start("w_out", d_w_out))
    dcb, dcc, dch, d_w_sc = _sconv_bwd(dy, proj, _behind(w_sc, gx.relay("w_out", dy)), t, wc)
    d_o, d_hval, d_gain = _hnorm_bwd(dy, hval, proj, gain, t, wc, dh)
    dq, dk, dv, dgate = _mlstm_bwd(proj, gcol, grow, hval, d_hval, cs, ns, t, wc, dh)
    dgt, d_b_gates = _gates_bwd(dgate, proj, bias_tile, t, gate_tile)
    pad = jnp.zeros((t, ninp - nin - LANE), BF16)
    d_proj = jnp.concatenate([dcb, dcc, dch, dq, dk, dv, d_o, dgt, pad], axis=1)
    d_w_in = _matmul(d_proj, x_b, "tn", BF16, "proj_dw", tm=IN_SLAB, tn=1024, tk=t)
    small = dict(b_gates=d_b_gates[:, :2 * NH], w_sc_conv=d_w_sc, mh_gain=d_gain, ln1_g=d_ln1_g, ln1_b=d_ln1_b,
                 w_ffn_conv=d_w_fc, b_ffn_conv=d_b_fc, ln2_g=d_ln2_g, ln2_b=d_ln2_b)
    token = gx.start("w_in", d_w_in.reshape(ninp // IN_SLAB, IN_SLAB, d))
    token = gx.relay("w_in", gx.meanwhile(small, loss, token))
    grad_x = _matmul(d_proj, w_in, "nn", F32, "proj_dx", tm=512, tn=512, tk=ninp, add=dz1, add_scale=ALPHA, after=token)
    return loss, grad_x, small, gx


HBM = pl.BlockSpec(memory_space=pltpu.HBM)


def _place():
    return lax.axis_index("x"), lax.axis_index("y"), lax.axis_index("c")


def _index(p):
    return 4 * p[0] + 2 * p[1] + p[2]


def _all_gather(arrs, name):
    n = len(arrs)

    def body(*refs):
        ins, outs = refs[:n], refs[n:2 * n]
        send_sems, recv_sems, local_sems = refs[2 * n:]
        x, y, c = _place()
        me, sibling = (x, y, c), (x, y, 1 - c)
        chips = [(1 - x, y), (x, 1 - y), (1 - x, 1 - y)]

        def copy(a, k, block, to, own=False):
            dst = outs[a].at[_index(block)]
            return pltpu.make_async_remote_copy(
                src_ref=ins[a] if own else dst, dst_ref=dst,
                send_sem=send_sems.at[k * n + a], recv_sem=recv_sems.at[k * n + a],
                device_id=to, device_id_type=MESH)

        mine = [pltpu.make_async_copy(ins[a], outs[a].at[_index(me)], local_sems.at[a]) for a in range(n)]
        for cp in mine:
            cp.start()
        first = []
        for a in range(n):
            first.append(copy(a, 0, me, sibling, own=True))
            first += [copy(a, 1 + j, me, (*chip, c), own=True) for j, chip in enumerate(chips)]
        for cp in first:
            cp.start()
        passed = []
        for j, chip in enumerate(chips):
            for a in range(n):
                copy(a, 1 + j, (*chip, c), me).wait_recv()
                cp = copy(a, 4 + j, (*chip, c), sibling)
                cp.start()
                passed.append(cp)
        for a in range(n):
            copy(a, 0, sibling, me).wait_recv()
            for j, chip in enumerate(chips):
                copy(a, 4 + j, (*chip, 1 - c), me).wait_recv()
        for cp in first + passed:
            cp.wait_send()
        for cp in mine:
            cp.wait()

    return pl.pallas_call(
        body, name=name, in_specs=[HBM] * n, out_specs=[HBM] * n,
        out_shape=[jax.ShapeDtypeStruct((N_DEV,) + a.shape, a.dtype) for a in arrs],
        scratch_shapes=[pltpu.SemaphoreType.DMA((7 * n,)), pltpu.SemaphoreType.DMA((7 * n,)),
                        pltpu.SemaphoreType.DMA((n,))],
    )(*arrs)


SEM = pl.BlockSpec(memory_space=pltpu.SEMAPHORE)
EFFECT = pltpu.SideEffectType.DATAFLOW_SIDE_EFFECTING


def _chips(x, y):
    return [(1 - x, y), (x, 1 - y), (1 - x, 1 - y)]


N_CHIP = N_DEV // 2


def _pair_route(x, y, c):
    return [((x, y, 1 - c), 2 * q + (1 - c), q, q) for q in range(N_CHIP)]


def _chip_route(x, y, c):
    mine = 2 * x + y
    return [((*chip, c), 2 * chip[0] + chip[1], mine, 2 * chip[0] + chip[1]) for chip in _chips(x, y)]


def _exchange_pieces(g_ref, land_ref, width, tail):
    if not tail:
        return [(lambda i: g_ref.at[i], lambda s: land_ref.at[s])]
    return [(lambda i: g_ref.at[i], lambda s: land_ref.at[s, pl.ds(0, width), :]),
            (lambda i: g_ref.at[i + 1, pl.ds(0, IN_TAIL), :], lambda s: land_ref.at[s, pl.ds(width, IN_TAIL), :])]


def _exchange_start(grad, route, tail, name):
    width = grad.shape[1]
    n_p = 2 if tail else 1
    n_c = len(route(0, 0, 0))
    land_shape = (N_CHIP, width + (IN_TAIL if tail else 0), grad.shape[2])

    def body(g_ref, land_ref, send_sems, recv_sems, g_thru, land_thru, token):
        for j, (peer, slab, slot, _) in enumerate(route(*_place())):
            for p, (src, dst) in enumerate(_exchange_pieces(g_ref, land_ref, width, tail)):
                pltpu.make_async_remote_copy(src_ref=src(slab), dst_ref=dst(slot), send_sem=send_sems.at[j * n_p + p],
                                             recv_sem=recv_sems.at[j * n_p + p], device_id=peer,
                                             device_id_type=MESH).start()
        token[...] = jnp.zeros_like(token)

    return pl.pallas_call(
        body, name=name,
        out_shape=(pltpu.SemaphoreType.DMA((n_c * n_p,)), pltpu.SemaphoreType.DMA((n_c * n_p,)),
                   pltpu.HBM(grad.shape, grad.dtype), pltpu.HBM(land_shape, grad.dtype),
                   jax.ShapeDtypeStruct((8, LANE), F32)),
        in_specs=(HBM, HBM), out_specs=(SEM, SEM, HBM, HBM, pl.BlockSpec(memory_space=pltpu.VMEM)),
        input_output_aliases={0: 2, 1: 3},
        compiler_params=pltpu.CompilerParams(has_side_effects=EFFECT),
    )(pltpu.with_memory_space_constraint(grad, pltpu.HBM),
      pltpu.with_memory_space_constraint(lax.empty(land_shape, grad.dtype), pltpu.HBM))


def _exchange_wait(send_sems, recv_sems, g_thru, land_thru, after, route, tail, name):
    width = g_thru.shape[1]
    n_p = 2 if tail else 1

    def body(g_ref, land_ref, send_sems, recv_sems, after_ref, g_dead, got_ref):
        for j, (peer, slab, _, slot) in enumerate(route(*_place())):
            for p, (src, dst) in enumerate(_exchange_pieces(g_ref, land_ref, width, tail)):
                cp = pltpu.make_async_remote_copy(src_ref=src(slab), dst_ref=dst(slot),
                                                  send_sem=send_sems.at[j * n_p + p], recv_sem=recv_sems.at[j * n_p + p],
                                                  device_id=peer, device_id_type=MESH)
                cp.wait_send()
                cp.wait_recv()

    return pl.pallas_call(
        body, name=name,
        out_shape=(pltpu.HBM(g_thru.shape, g_thru.dtype), pltpu.HBM(land_thru.shape, land_thru.dtype)),
        in_specs=(HBM, HBM, SEM, SEM, pl.BlockSpec(memory_space=pl.ANY)), out_specs=(HBM, HBM),
        input_output_aliases={0: 0, 1: 1},
        compiler_params=pltpu.CompilerParams(has_side_effects=EFFECT),
    )(g_thru, land_thru, send_sems, recv_sems, after)


def _pair_add(grad, pair, core, tail, name):
    rows, cols = grad.shape[1], grad.shape[2]
    total = pair.shape[1]

    def body(core_ref, *refs):
        if tail:
            g_ref, t_ref, p_ref, o_ref = refs
            o_ref[0:rows, :] = (g_ref[...].astype(F32) + p_ref[0:rows, :].astype(F32)).astype(BF16)
            o_ref[rows:total, :] = (t_ref[...].astype(F32) + p_ref[rows:total, :].astype(F32)).astype(BF16)
        else:
            g_ref, p_ref, o_ref = refs
            o_ref[...] = (g_ref[...].astype(F32) + p_ref[...].astype(F32)).astype(BF16)

    if tail:
        tc = _fit(cols, 512)
        grid = (N_CHIP, cols // tc)
        slab = pl.BlockSpec((None, total, tc), lambda q, i, core_ref: (q, 0, i))
        in_specs = [pl.BlockSpec((None, rows, tc), lambda q, i, core_ref: (2 * q + core_ref[0], 0, i)),
                    pl.BlockSpec((None, IN_TAIL, tc), lambda q, i, core_ref: (2 * q + core_ref[0] + 1, 0, i))]
    else:
        tr = _rows(rows, 1024)
        grid = (N_CHIP, rows // tr)
        slab = pl.BlockSpec((None, tr, cols), lambda q, i, core_ref: (q, i, 0))
        in_specs = [pl.BlockSpec((None, tr, cols), lambda q, i, core_ref: (2 * q + core_ref[0], i, 0))]
    return pl.pallas_call(
        body, name=name,
        grid_spec=pltpu.PrefetchScalarGridSpec(num_scalar_prefetch=1, grid=grid,
                                               in_specs=in_specs + [slab], out_specs=slab),
        out_shape=jax.ShapeDtypeStruct(pair.shape, BF16),
        compiler_params=_params("parallel", "parallel"),
    )(core, *([grad, grad] if tail else [grad]), pair)


def _gather_start(blocks, after, name, spare=()):
    n = len(blocks)
    lands = [(N_DEV + (a in spare),) + b.shape for a, b in enumerate(blocks)]

    def body(*refs):
        b_refs, land_refs = refs[:n], refs[n:2 * n]
        send_sems, recv_sems = refs[2 * n + 1:3 * n + 1], refs[3 * n + 1:4 * n + 1]
        token = refs[-1]
        x, y, c = _place()
        me = _index((x, y, c))
        for a in range(n):
            for k, to in enumerate([(x, y, 1 - c)] + [(*chip, c) for chip in _chips(x, y)]):
                pltpu.make_async_remote_copy(src_ref=b_refs[a], dst_ref=land_refs[a].at[me], send_sem=send_sems[a].at[k],
                                             recv_sem=recv_sems[a].at[k], device_id=to, device_id_type=MESH).start()
        token[...] = jnp.zeros_like(token)

    sems = [pltpu.SemaphoreType.DMA((4,))] * n
    out = pl.pallas_call(
        body, name=name,
        out_shape=(*sems, *sems, *[pltpu.HBM(b.shape, b.dtype) for b in blocks],
                   *[pltpu.HBM(s, b.dtype) for s, b in zip(lands, blocks)], jax.ShapeDtypeStruct((8, LANE), F32)),
        in_specs=(*[HBM] * (2 * n), pl.BlockSpec(memory_space=pl.ANY)),
        out_specs=(*[SEM] * (2 * n), *[HBM] * (2 * n), pl.BlockSpec(memory_space=pltpu.VMEM)),
        input_output_aliases={i: 2 * n + i for i in range(2 * n)},
        compiler_params=pltpu.CompilerParams(has_side_effects=EFFECT),
    )(*[pltpu.with_memory_space_constraint(b, pltpu.HBM) for b in blocks],
      *[pltpu.with_memory_space_constraint(lax.empty(s, b.dtype), pltpu.HBM) for s, b in zip(lands, blocks)], after)
    return [(out[a], out[n + a], out[2 * n + a], out[3 * n + a]) for a in range(n)], out[-1]


def _gather_forward(send_sems, recv_sems, b_thru, land_thru, after, name):
    def body(b_ref, land_ref, send_sems, recv_sems, after_ref, b_dead, land_out, send2, recv2, token):
        x, y, c = _place()
        sibling = (x, y, 1 - c)
        for k, frm in enumerate([sibling] + [(*chip, c) for chip in _chips(x, y)]):
            cp = pltpu.make_async_remote_copy(src_ref=b_ref, dst_ref=land_ref.at[_index(frm)], send_sem=send_sems.at[k],
                                              recv_sem=recv_sems.at[k], device_id=frm, device_id_type=MESH)
            cp.wait_send()
            cp.wait_recv()
        for j, chip in enumerate(_chips(x, y)):
            slot = land_ref.at[_index((*chip, c))]
            pltpu.make_async_remote_copy(src_ref=slot, dst_ref=slot, send_sem=send2.at[j], recv_sem=recv2.at[j],
                                         device_id=sibling, device_id_type=MESH).start()
        token[...] = jnp.zeros_like(token)

    return pl.pallas_call(
        body, name=name,
        out_shape=(pltpu.HBM(b_thru.shape, b_thru.dtype), pltpu.HBM(land_thru.shape, land_thru.dtype),
                   pltpu.SemaphoreType.DMA((3,)), pltpu.SemaphoreType.DMA((3,)), jax.ShapeDtypeStruct((8, LANE), F32)),
        in_specs=(HBM, HBM, SEM, SEM, pl.BlockSpec(memory_space=pl.ANY)),
        out_specs=(HBM, HBM, SEM, SEM, pl.BlockSpec(memory_space=pltpu.VMEM)),
        input_output_aliases={0: 0, 1: 1},
        compiler_params=pltpu.CompilerParams(has_side_effects=EFFECT),
    )(b_thru, land_thru, send_sems, recv_sems, after)


def _gather_finish(land_thru, send2, recv2, after, name):
    def body(land_ref, send2, recv2, after_ref, land_out):
        x, y, c = _place()
        for j, chip in enumerate(_chips(x, y)):
            cp = pltpu.make_async_remote_copy(src_ref=land_ref.at[_index((*chip, c))],
                                              dst_ref=land_ref.at[_index((*chip, 1 - c))], send_sem=send2.at[j],
                                              recv_sem=recv2.at[j], device_id=(x, y, 1 - c), device_id_type=MESH)
            cp.wait_send()
            cp.wait_recv()

    return pl.pallas_call(
        body, name=name, out_shape=pltpu.HBM(land_thru.shape, land_thru.dtype),
        in_specs=(HBM, SEM, SEM, pl.BlockSpec(memory_space=pl.ANY)), out_specs=HBM,
        input_output_aliases={0: 0},
        compiler_params=pltpu.CompilerParams(has_side_effects=EFFECT),
    )(land_thru, send2, recv2, after)


class _Gathering:
    def __init__(self, first, later, me):
        started, token = _gather_start(list(first.values()), next(iter(first.values())), "gather1_first", spare=(0,))
        cast = [_behind(a, token).astype(BF16) for a in later.values()]
        started_later, self.token = _gather_start(cast, token, "gather1_later")
        self.me, self.state = me, dict(zip([*first, *later], started + started_later))

    def begin(self, after):
        return self.token

    def forward(self, name, after):
        *self.state[name], token = _gather_forward(*self.state[name], after, "gather2_" + name)
        return token

    def get(self, name, after):
        block, land, send2, recv2 = self.state[name]
        land = _gather_finish(land, send2, recv2, after, "gather3_" + name)
        land = lax.dynamic_update_index_in_dim(land, block[None], self.me, 0)
        return land if name not in ("w_out", "w_down") else land.reshape(-1, land.shape[2])


class _Reducing:
    def __init__(self, core, chip, gather_small):
        self.core, self.chip, self.state, self.token, self.gather_small = core, chip, {}, None, gather_small

    def meanwhile(self, small, loss, after):
        self.small_sum = self.gather_small(small, loss, after)
        return self.small_sum

    def start(self, name, grad):
        g = grad if grad.ndim == 3 else grad.reshape(N_DEV, grad.shape[0] // N_DEV, grad.shape[1])
        *self.state[name], token = _exchange_start(g, _pair_route, name == "w_in", "pair_send_" + name)
        return token

    def relay(self, name, after):
        tail = name == "w_in"
        grad, pair = _exchange_wait(*self.state[name], after, _pair_route, tail, "pair_recv_" + name)
        total = _pair_add(grad, pair, self.core, tail, "pair_add_" + name)
        *self.state[name], self.token = _exchange_start(total, _chip_route, False, "chip_send_" + name)
        return self.token

    def finish(self, name, after):
        total, land = _exchange_wait(*self.state[name], after, _chip_route, False, "chip_recv_" + name)
        own = lax.dynamic_index_in_dim(total, self.chip, 0, keepdims=True)
        return lax.dynamic_update_index_in_dim(land, own, self.chip, 0)


def _carry_w_in(main, tail):
    slabs, _, d = main.shape
    tc = _fit(d, 2048)
    assert slabs == N_DEV + 1 and tail.shape[:2] == (N_DEV, IN_TAIL), (main.shape, tail.shape)
    top = lambda off: pl.BlockSpec((None, IN_TAIL, tc), lambda s, j: (s + off, 0, j))

    def carry(m_ref, t_ref, o_ref):
        o_ref[...] = m_ref[...] + t_ref[...]

    main = pl.pallas_call(
        carry, name="carry_w_in", grid=(N_DEV - 1, d // tc), in_specs=[top(1), top(0)], out_specs=top(1),
        out_shape=jax.ShapeDtypeStruct(main.shape, main.dtype), input_output_aliases={0: 0},
        compiler_params=_params("parallel", "parallel"),
    )(main, tail)

    def last(m_ref, t_ref, o_ref):
        o_ref[...] = jnp.zeros_like(o_ref)
        o_ref[0:IN_TAIL, :] = t_ref[...]

    return pl.pallas_call(
        last, name="last_slab_w_in", grid=(d // tc,),
        in_specs=[pl.BlockSpec(memory_space=pl.ANY), pl.BlockSpec((None, IN_TAIL, tc), lambda j: (N_DEV - 1, 0, j))],
        out_specs=pl.BlockSpec((None, IN_SLAB, tc), lambda j: (N_DEV, 0, j)),
        out_shape=jax.ShapeDtypeStruct(main.shape, main.dtype), input_output_aliases={0: 0},
        compiler_params=_params("parallel"),
    )(main, tail)


def _rows(n, want):
    t = min(n, want)
    t -= t % 16
    while n % t:
        t -= 16
    return t


def _adam_math(w, g, m, v):
    m2 = ADAM_B1 * m + (1.0 - ADAM_B1) * g
    v2 = ADAM_B2 * v + (1.0 - ADAM_B2) * (g * g)
    m_hat = m2 * (1.0 / (1.0 - ADAM_B1 ** ADAM_STEP))
    v_hat = v2 * (1.0 / (1.0 - ADAM_B2 ** ADAM_STEP))
    return -ADAM_LR * (m_hat / (jnp.sqrt(v_hat) + ADAM_EPS) + ADAM_WD * w), m2, v2


def _slot_sum(r_ref):
    acc = r_ref[0].astype(F32)
    for i in range(1, r_ref.shape[0]):
        acc = acc + r_ref[i].astype(F32)
    return acc


def _shift_w_in(w):
    ws, d = w.shape
    tc = _fit(d, 256)

    def body(w_ref, main_ref, tail_ref, tall):
        tall[...] = jnp.zeros_like(tall)
        tall[0:ws, :] = w_ref[...]
        moved = pltpu.roll(tall[...], _index(_place()), 0).astype(BF16)
        main_ref[...] = moved[0:IN_SLAB]
        tail_ref[...] = moved[IN_SLAB:]

    return pl.pallas_call(
        body, name="shift_w_in", grid=(d // tc,),
        in_specs=[pl.BlockSpec((ws, tc), lambda j: (0, j))],
        out_specs=[pl.BlockSpec((IN_SLAB, tc), lambda j: (0, j)), pl.BlockSpec((IN_TAIL, tc), lambda j: (0, j))],
        out_shape=[jax.ShapeDtypeStruct((IN_SLAB, d), BF16), jax.ShapeDtypeStruct((IN_TAIL, d), BF16)],
        scratch_shapes=[pltpu.VMEM((IN_SLAB + IN_TAIL, tc), F32)], compiler_params=_params("parallel"),
    )(w)


def _sum_adamw_shifted(r, w, m, v, name):
    _, ph, d = r.shape
    ws = w.shape[0]
    tc = _fit(d, 256)

    def body(r_ref, w_ref, m_ref, v_ref, g_ref, d_ref, m2_ref, v2_ref, tall):
        tall[...] = pltpu.roll(_slot_sum(r_ref), lax.rem(ph - _index(_place()), ph), 0)
        g = tall[0:ws, :]
        g_ref[...] = g
        d_ref[...], m2_ref[...], v2_ref[...] = _adam_math(w_ref[...], g, m_ref[...], v_ref[...])

    blk = pl.BlockSpec((ws, tc), lambda j: (0, j))
    out = jax.ShapeDtypeStruct(w.shape, F32)
    return pl.pallas_call(
        body, name=name, grid=(d // tc,),
        in_specs=[pl.BlockSpec((r.shape[0], ph, tc), lambda j: (0, 0, j)), blk, blk, blk],
        out_specs=[blk] * 4, out_shape=[out] * 4,
        scratch_shapes=[pltpu.VMEM((ph, tc), F32)], compiler_params=_params("parallel"),
    )(r, w, m, v)


def _sum_slots(r, name, tr=128):
    _, rows, cols = r.shape
    tr = _rows(rows, tr)

    def body(r_ref, g_ref):
        g_ref[...] = _slot_sum(r_ref)

    return pl.pallas_call(
        body, name=name, grid=(rows // tr,),
        in_specs=[pl.BlockSpec((r.shape[0], tr, cols), lambda i: (0, i, 0))],
        out_specs=pl.BlockSpec((tr, cols), lambda i: (i, 0)),
        out_shape=jax.ShapeDtypeStruct((rows, cols), F32),
        compiler_params=_params("parallel"),
    )(r)


def _adamw(w, g, m, v, name, tr=256):
    rows, cols = w.shape
    tr = _rows(rows, tr)

    def body(w_ref, g_ref, m_ref, v_ref, d_ref, m2_ref, v2_ref):
        d_ref[...], m2_ref[...], v2_ref[...] = _adam_math(w_ref[...], g_ref[...], m_ref[...], v_ref[...])

    blk = pl.BlockSpec((tr, cols), lambda i: (i, 0))
    out = jax.ShapeDtypeStruct((rows, cols), F32)
    return pl.pallas_call(
        body, name=name, grid=(rows // tr,), in_specs=[blk] * 4, out_specs=[blk] * 3, out_shape=[out] * 3,
        compiler_params=_params("parallel"),
    )(w, g, m, v)


def _sum_adamw(r, w, m, v, name, tr=256):
    rows, cols = w.shape
    tr = _rows(rows, tr)

    def body(r_ref, w_ref, m_ref, v_ref, g_ref, d_ref, m2_ref, v2_ref):
        g = _slot_sum(r_ref)
        g_ref[...] = g
        d_ref[...], m2_ref[...], v2_ref[...] = _adam_math(w_ref[...], g, m_ref[...], v_ref[...])

    blk = pl.BlockSpec((tr, cols), lambda i: (i, 0))
    out = jax.ShapeDtypeStruct((rows, cols), F32)
    return pl.pallas_call(
        body, name=name, grid=(rows // tr,),
        in_specs=[pl.BlockSpec((r.shape[0], tr, cols), lambda i: (0, i, 0)), blk, blk, blk],
        out_specs=[blk] * 4, out_shape=[out] * 4,
        compiler_params=_params("parallel"),
    )(r, w, m, v)


def _pack(pieces, sizes):
    flat = [jnp.pad(p.reshape(-1).astype(F32), (0, s - p.size)) for p, s in zip(pieces, sizes)]
    total = sum(sizes)
    padded = -(-total // (16 * LANE)) * (16 * LANE)
    return jnp.pad(jnp.concatenate(flat), (0, padded - total)).reshape(-1, LANE)


def _unpack(packed, shapes, sizes):
    flat = packed.reshape(-1)
    out, off = [], 0
    for shp, s in zip(shapes, sizes):
        n = 1
        for k in shp:
            n *= k
        out.append(flat[off:off + n].reshape(shp))
        off += s
    return out


def _lanes(n):
    return -(-n // LANE) * LANE


WEIGHTS = ("w_in", "b_gates", "w_sc_conv", "mh_gain", "w_out", "ln1_g", "ln1_b", "w_up", "w_ffn_conv", "b_ffn_conv",
           "w_down", "ln2_g", "ln2_b")
BIG = ("w_in", "w_out", "w_up", "w_down")
SMALL = tuple(n for n in WEIGHTS if n not in BIG)


def kernel(x, w_in, b_gates, w_sc_conv, mh_gain, w_out, ln1_g, ln1_b, w_up, w_ffn_conv, b_ffn_conv, w_down, ln2_g, ln2_b, loss_target, m_w_in, m_b_gates, m_w_sc_conv, m_mh_gain, m_w_out, m_ln1_g, m_ln1_b, m_w_up, m_w_ffn_conv, m_b_ffn_conv, m_w_down, m_ln2_g, m_ln2_b, v_w_in, v_b_gates, v_w_sc_conv, v_mh_gain, v_w_out, v_ln1_g, v_ln1_b, v_w_up, v_w_ffn_conv, v_b_ffn_conv, v_w_down, v_ln2_g, v_ln2_b):
    w = dict(zip(WEIGHTS, (w_in, b_gates, w_sc_conv, mh_gain, w_out, ln1_g, ln1_b, w_up, w_ffn_conv, b_ffn_conv,
                           w_down, ln2_g, ln2_b)))
    m = dict(zip(WEIGHTS, (m_w_in, m_b_gates, m_w_sc_conv, m_mh_gain, m_w_out, m_ln1_g, m_ln1_b, m_w_up,
                           m_w_ffn_conv, m_b_ffn_conv, m_w_down, m_ln2_g, m_ln2_b)))
    v = dict(zip(WEIGHTS, (v_w_in, v_b_gates, v_w_sc_conv, v_mh_gain, v_w_out, v_ln1_g, v_ln1_b, v_w_up,
                           v_w_ffn_conv, v_b_ffn_conv, v_w_down, v_ln2_g, v_ln2_b)))
    me = _index(_place())
    d = x.shape[2]
    ws_in = w_in.shape[2]
    assert ws_in == IN_SLAB + 1 and N_DEV <= LANE, w_in.shape
    ninp = (N_DEV + 1) * IN_SLAB
    ws_sc, ws_fc = w_sc_conv.shape[2], w_ffn_conv.shape[2]
    w_in_t, m_in_t, v_in_t = (jnp.transpose(a[0]) for a in (w_in, m_w_in, v_w_in))

    w_in_main, w_in_tail = _shift_w_in(w_in_t)
    taps8 = lambda a: jnp.pad(a[0], ((0, 5), (0, 0)))
    at_once = ("w_in", "w_tail", "w_sc", "w_fc")
    wx = _Gathering(dict(zip(at_once, (w_in_main, w_in_tail, taps8(w_sc_conv), taps8(w_ffn_conv)))),
                    {n: w[n][0] for n in ("w_out", "w_up", "w_down")}, me)
    token = x_b = _behind(x[0], wx.begin(None)).astype(BF16)
    for n in at_once:
        token = wx.forward(n, token)
    g_in, g_tail, g_sc, g_fc = (wx.get(n, token) for n in at_once)
    w_in_full = _carry_w_in(g_in, g_tail).reshape(ninp, d)
    w_sc_full = g_sc[:, :3].transpose(1, 0, 2).reshape(3, N_DEV * ws_sc)
    w_fc_full = g_fc[:, :3].transpose(1, 0, 2).reshape(3, N_DEV * ws_fc)

    xi, yi, ci = _place()
    names = ("loss",) + SMALL
    pieces = {}

    def gather_small(small, loss_t, after):
        pieces.update(small, loss=loss_t[0, :1])
        sizes = [_lanes(pieces[n].size) for n in names]
        (g_small,) = _all_gather([_behind(_pack([pieces[n] for n in names], sizes), after)], "gather_small")
        return _sum_slots(g_small, "sum_small", tr=g_small.shape[1])

    gx = _Reducing(jnp.reshape(ci, (1,)).astype(jnp.int32), 2 * xi + yi, gather_small)
    loss_t, grad_x, small, _ = _local_step(
        x[0], loss_target[0], w_in_full, b_gates, w_sc_full, mh_gain, None, ln1_g, ln1_b, None,
        w_fc_full, b_ffn_conv, None, ln2_g, ln2_b, gx=gx, wx=wx, x_b=x_b)

    grads, deltas, new_m, new_v = {}, {}, {}, {}
    for name in ("w_down", "w_up", "w_out"):
        grads[name], deltas[name], new_m[name], new_v[name] = _sum_adamw(
            gx.finish(name, gx.token), w[name][0], m[name][0], v[name][0], "adamw_" + name)

    summed = _unpack(gx.small_sum, [pieces[n].shape for n in names], [_lanes(pieces[n].size) for n in names])
    full = dict(zip(names, summed))
    full["w_sc_conv"] = lax.dynamic_slice(full["w_sc_conv"], (0, me * ws_sc), (3, ws_sc))
    full["w_ffn_conv"] = lax.dynamic_slice(full["w_ffn_conv"], (0, me * ws_fc), (3, ws_fc))
    for n in SMALL:
        grads[n] = full[n].reshape(w[n].shape)
    sizes = [_lanes(w[n].size) for n in SMALL]
    shapes = [w[n].shape for n in SMALL]
    packed = [_pack([t[n] for n in SMALL], sizes) for t in (w, grads, m, v)]
    small_out = _adamw(*packed, "adamw_small")
    for res, t in zip(small_out, (deltas, new_m, new_v)):
        t.update(zip(SMALL, _unpack(res, shapes, sizes)))

    done = sum(t[0:1, 0:1] for t in (grad_x, deltas["w_down"], deltas["w_up"], deltas["w_out"], small_out[0]))
    grads["w_in"], deltas["w_in"], new_m["w_in"], new_v["w_in"] = (
        jnp.transpose(a)[None] for a in _sum_adamw_shifted(gx.finish("w_in", done), w_in_t, m_in_t, v_in_t, "adamw_w_in"))

    big = lambda t: {n: (t[n].reshape(w[n].shape) if n in BIG else t[n]) for n in WEIGHTS}
    grads, deltas, new_m, new_v = big(grads), big(deltas), big(new_m), big(new_v)
    return (full["loss"].reshape(()), grad_x[None], *[grads[n] for n in WEIGHTS], *[deltas[n] for n in WEIGHTS],
            *[new_m[n] for n in WEIGHTS], *[new_v[n] for n in WEIGHTS])
```

```python
import functools

import jax
import jax.numpy as jnp
from jax import lax
from jax.experimental import pallas as pl
from jax.experimental.pallas import tpu as pltpu

F32 = jnp.float32
BF16 = jnp.bfloat16
MESH = pl.DeviceIdType.MESH

N_DEV = 8
NH = 4
CHUNK = 64
LN_EPS = 1e-5
HN_EPS = 1e-6
ALPHA = 2.0 ** 0.25
LANE = 128
IN_SLAB = 7 * LANE
IN_TAIL = 16
VMEM_LIMIT = 56 * 1024 * 1024
ADAM_LR, ADAM_B1, ADAM_B2, ADAM_EPS, ADAM_WD, ADAM_STEP = 0.001, 0.9, 0.999, 1e-08, 0.01, 10

_NN = (((1,), (0,)), ((), ()))
_NT = (((1,), (1,)), ((), ()))
_TN = (((0,), (0,)), ((), ()))


def _dot(a, b, dn=_NN):
    return lax.dot_general(a, b, dn, preferred_element_type=F32)


def _params(*sem):
    return pltpu.CompilerParams(dimension_semantics=sem if sem else None, vmem_limit_bytes=VMEM_LIMIT)


def _iota(shape, axis):
    return lax.broadcasted_iota(jnp.int32, shape, axis)


def _fit(n, want):
    if n <= want:
        return n
    t = want - want % LANE
    while n % t:
        t -= LANE
    return t


def _matmul(a, b, mode, out_dtype, name, tm=1024, tn=512, tk=1024, add=None, add_scale=1.0,
            a_blocked=False, b_blocked=False, o_width=None, after=None, n=None):
    a_parts = a if isinstance(a, tuple) else None
    b_parts = b if isinstance(b, tuple) else None
    if a_parts:
        a_blocked, (a_rows, wa), na = True, a[0].shape, len(a)
        kd, m = (a_rows, na * wa) if mode == "tn" else (na * wa, a_rows)
    elif a_blocked:
        na, a_rows, wa = a.shape
        kd, m = (a_rows, na * wa) if mode == "tn" else (na * wa, a_rows)
    elif mode == "tn":
        kd, m = a.shape
    else:
        m, kd = a.shape
    if b_parts:
        b_blocked, (rows, w), nb = True, b[0].shape, len(b)
    elif b_blocked:
        nb, rows, w = b.shape
    if b_blocked:
        n = rows if mode == "nt" else nb * w
        assert (nb * w if mode == "nt" else rows) == kd, (name, kd)
    else:
        n = n or (b.shape[0] if mode == "nt" else b.shape[1])
    tm, tn, tk = _fit(m, tm), _fit(n, tn), _fit(kd, tk)
    if a_blocked and mode == "tn":
        tm = _fit(wa, tm)
    if a_blocked and mode != "tn":
        tk = _fit(wa, tk)
    if b_blocked and mode != "nt":
        tn = _fit(w, tn)
    if b_blocked and mode == "nt":
        tk = _fit(w, tk)
    if o_width is not None:
        tn = _fit(o_width, tn)
    assert m % tm == 0 and n % tn == 0 and kd % tk == 0, (name, m, n, kd, tm, tn, tk)
    assert not (a_blocked and mode != "tn" and wa % tk) and not (b_blocked and mode == "nt" and w % tk), (name, tk)
    nk = kd // tk
    dn = {"nn": _NN, "nt": _NT, "tn": _TN}[mode]
    if a_blocked and mode == "tn":
        a_per = wa // tm
        a_spec = pl.BlockSpec((None, tk, tm), lambda i, j, k: (i // a_per, k, i % a_per))
    elif a_blocked:
        a_per = wa // tk
        a_spec = pl.BlockSpec((None, tm, tk), lambda i, j, k: (k // a_per, i, k % a_per))
    elif mode == "tn":
        a_spec = pl.BlockSpec((tk, tm), lambda i, j, k: (k, i))
    else:
        a_spec = pl.BlockSpec((tm, tk), lambda i, j, k: (i, k))
    if b_blocked and mode != "nt":
        per = w // tn
        b_spec = pl.BlockSpec((None, tk, tn), lambda i, j, k: (j // per, k, j % per))
    elif b_blocked:
        per = w // tk
        b_spec = pl.BlockSpec((None, tn, tk), lambda i, j, k: (k // per, j, k % per))
    elif mode == "nt":
        b_spec = pl.BlockSpec((tn, tk), lambda i, j, k: (j, k))
    else:
        b_spec = pl.BlockSpec((tk, tn), lambda i, j, k: (k, j))
    if o_width is None:
        o_spec = pl.BlockSpec((tm, tn), lambda i, j, k: (i, j))
        o_shape = (m, n)
    else:
        oper = o_width // tn
        o_spec = pl.BlockSpec((None, tm, tn), lambda i, j, k: (j // oper, i, j % oper))
        o_shape = (n // o_width, m, o_width)
    a_list, a_specs = [a], [a_spec]
    if a_parts:
        hold = lambda x, s: jnp.clip(x - s * a_per, 0, a_per - 1)
        a_list = list(a_parts)
        a_specs = [(pl.BlockSpec((tk, tm), lambda i, j, k, s=s: (k, hold(i, s))) if mode == "tn"
                    else pl.BlockSpec((tm, tk), lambda i, j, k, s=s: (i, hold(k, s)))) for s in range(na)]
    b_list, b_specs = [b], [b_spec]
    if b_parts:
        hold_b = lambda x, s: jnp.clip(x - s * per, 0, per - 1)
        b_list = list(b_parts)
        b_specs = [(pl.BlockSpec((tn, tk), lambda i, j, k, s=s: (j, hold_b(k, s))) if mode == "nt"
                    else pl.BlockSpec((tk, tn), lambda i, j, k, s=s: (k, hold_b(j, s)))) for s in range(nb)]
    n_a, n_b = len(a_list), len(b_list)
    has_add = add is not None
    n_in = n_a + n_b + has_add + (after is not None)
    in_place = nk > 1 and out_dtype == F32

    def body(*refs):
        add_ref = refs[n_a + n_b] if has_add else None
        o_ref = refs[n_in]
        i, j, k = pl.program_id(0), pl.program_id(1), pl.program_id(2)

        def finish(r):
            if has_add:
                r = r + add_scale * add_ref[...]
            o_ref[...] = r.astype(out_dtype)

        def step(a_ref, b_ref):
            if nk == 1:
                finish(_dot(a_ref[...], b_ref[...], dn))
                return
            acc = o_ref if in_place else refs[-1]

            @pl.when(k == 0)
            def _():
                acc[...] = _dot(a_ref[...], b_ref[...], dn)

            @pl.when(k > 0)
            def _():
                acc[...] += _dot(a_ref[...], b_ref[...], dn)

        if n_a == 1 and n_b == 1:
            step(refs[0], refs[1])
        else:
            slab_a = ((i if mode == "tn" else k) // a_per) if n_a > 1 else 0
            slab_b = ((k if mode == "nt" else j) // per) if n_b > 1 else 0
            for sa in range(n_a):
                for sb in range(n_b):
                    pl.when((slab_a == sa) & (slab_b == sb))(functools.partial(step, refs[sa], refs[n_a + sb]))
        if nk > 1 and not (in_place and not has_add):
            @pl.when(k == nk - 1)
            def _():
                finish((o_ref if in_place else refs[-1])[...])

    in_specs = a_specs + b_specs + ([pl.BlockSpec((tm, tn), lambda i, j, k: (i, j))] if has_add else [])
    args = (*a_list, *b_list) + ((add,) if has_add else ())
    if after is not None:
        in_specs.append(pl.BlockSpec(memory_space=pl.ANY))
        args += (after,)
    return pl.pallas_call(
        body, name=name, grid=(m // tm, n // tn, nk),
        in_specs=in_specs, out_specs=o_spec,
        out_shape=jax.ShapeDtypeStruct(o_shape, out_dtype),
        scratch_shapes=[pltpu.VMEM((tm, tn), F32)] if nk > 1 and not in_place else [],
        compiler_params=_params("parallel", "parallel", "arbitrary"),
    )(*args)


def _shift_down(u, s):
    return jnp.where(_iota(u.shape, 0) >= s, pltpu.roll(u, s, 0), 0.0)


def _shift_up(u, s):
    t = u.shape[0]
    return jnp.where(_iota(u.shape, 0) < t - s, pltpu.roll(u, t - s, 0), 0.0)


SLAB = 8


def _rolled(u):
    return pltpu.roll(u, 2, 0), pltpu.roll(u, 1, 0)


def _conv(u, w, rolled=None):
    u2, u1 = _rolled(u) if rolled is None else rolled
    raw = w[0:1] * u2 + w[1:2] * u1 + w[2:3] * u
    head = u[0:SLAB]
    mended = w[0:1] * _shift_down(head, 2) + w[1:2] * _shift_down(head, 1) + w[2:3] * head
    return jnp.concatenate([mended, raw[SLAB:]], axis=0)


def _conv_t(dy, w):
    t = dy.shape[0]
    raw = w[2:3] * dy + w[1:2] * pltpu.roll(dy, t - 1, 0) + w[0:1] * pltpu.roll(dy, t - 2, 0)
    tail = dy[t - SLAB:]
    mended = w[2:3] * tail + w[1:2] * _shift_up(tail, 1) + w[0:1] * _shift_up(tail, 2)
    return jnp.concatenate([raw[:t - SLAB], mended], axis=0)


def _conv_dw(dy, u, rolled=None):
    t = dy.shape[0]
    u2, u1 = _rolled(u) if rolled is None else rolled
    head, tail = dy[0:SLAB], u[t - SLAB:]
    r = _iota(head.shape, 0)
    wrap2 = jnp.sum(jnp.where(r < 2, head * pltpu.roll(tail, 2, 0), 0.0), axis=0, keepdims=True)
    wrap1 = jnp.sum(jnp.where(r < 1, head * pltpu.roll(tail, 1, 0), 0.0), axis=0, keepdims=True)
    d0 = jnp.sum(dy * u2, axis=0, keepdims=True) - wrap2
    d1 = jnp.sum(dy * u1, axis=0, keepdims=True) - wrap1
    d2 = jnp.sum(dy * u, axis=0, keepdims=True)
    r3 = _iota((3, dy.shape[1]), 0)
    return jnp.where(r3 == 0, d0, jnp.where(r3 == 1, d1, d2))


def _sigmoid(x):
    return 0.5 * jnp.tanh(0.5 * x) + 0.5


def _sconv_fwd(proj, w_sc, t, wc):
    nb = wc // LANE

    def body(cb_ref, cc_ref, ch_ref, w_ref, y_ref):
        u = cc_ref[...] * ch_ref[...]
        y_ref[...] = (cb_ref[...] * _conv(u, w_ref[...])).astype(BF16)

    col = lambda off: pl.BlockSpec((t, LANE), lambda j: (0, j + off))
    return pl.pallas_call(
        body, name="sconv_fwd", grid=(nb,),
        in_specs=[col(0), col(nb), col(2 * nb), pl.BlockSpec((3, LANE), lambda j: (0, j))],
        out_specs=pl.BlockSpec((None, t, LANE), lambda j: (0, 0, j)),
        out_shape=jax.ShapeDtypeStruct((2, t, wc), BF16),
        compiler_params=_params("parallel"),
    )(proj, proj, proj, w_sc)


def _sconv_bwd(dy, proj, w_sc, t, wc):
    nb = wc // LANE

    def body(dy_ref, cb_ref, cc_ref, ch_ref, w_ref, dcb_ref, dcc_ref, dch_ref, dw_ref):
        cc, ch, w, d = cc_ref[...], ch_ref[...], w_ref[...], dy_ref[...]
        u = cc * ch
        ru = _rolled(u)
        dcb_ref[...] = (d * _conv(u, w, ru)).astype(BF16)
        dcu = d * cb_ref[...]
        dw_ref[...] = _conv_dw(dcu, u, ru)
        du = _conv_t(dcu, w)
        dcc_ref[...] = (du * ch).astype(BF16)
        dch_ref[...] = (du * cc).astype(BF16)

    col = lambda off: pl.BlockSpec((t, LANE), lambda j: (0, j + off))
    act = jax.ShapeDtypeStruct((t, wc), BF16)
    return pl.pallas_call(
        body, name="sconv_bwd", grid=(nb,),
        in_specs=[col(0), col(0), col(nb), col(2 * nb), pl.BlockSpec((3, LANE), lambda j: (0, j))],
        out_specs=[col(0), col(0), col(0), pl.BlockSpec((3, LANE), lambda j: (0, j))],
        out_shape=[act, act, act, jax.ShapeDtypeStruct((3, wc), F32)],
        compiler_params=_params("parallel"),
    )(dy, proj, proj, proj, w_sc)


def _gates_prep(proj, bias_tile, t, gate_tile):
    def body(g_ref, b_ref, o_ref):
        g = g_ref[...] + b_ref[...]
        lane = _iota(g.shape, 1)
        is_f = (lane >= NH) & (lane < 2 * NH)
        lf = jnp.minimum(g, 0.0) - jnp.log(1.0 + jnp.exp(-jnp.abs(g)))
        c = jnp.where(is_f, lf, 0.0)
        r = _iota(g.shape, 0) % CHUNK
        s = 1
        while s < CHUNK:
            c = c + jnp.where(r >= s, pltpu.roll(c, s, 0), 0.0)
            s *= 2
        o_ref[...] = jnp.where(is_f, c, jnp.where(lane < NH, g, 0.0))

    return pl.pallas_call(
        body, name="gates_prep", grid=(1,),
        in_specs=[pl.BlockSpec((t, LANE), lambda i: (0, gate_tile)), pl.BlockSpec((1, LANE), lambda i: (0, 0))],
        out_specs=pl.BlockSpec((t, LANE), lambda i: (0, 0)),
        out_shape=jax.ShapeDtypeStruct((t, LANE), F32),
        compiler_params=_params("arbitrary"),
    )(proj, bias_tile)


def _gates_bwd(dgate, proj, bias_tile, t, gate_tile):
    def body(dg_ref, g_ref, b_ref, o_ref, s_ref):
        g = g_ref[...] + b_ref[...]
        lane = _iota(g.shape, 1)
        r = _iota(g.shape, 0) % CHUNK
        dsig = 1.0 - _sigmoid(g)
        out = jnp.zeros(g.shape, F32)
        for h in range(NH):
            d = dg_ref[h]
            c = d
            s = 1
            while s < CHUNK:
                c = c + jnp.where(r + s < CHUNK, pltpu.roll(c, t - s, 0), 0.0)
                s *= 2
            di = jnp.broadcast_to(d[:, 0:1], g.shape)
            db = jnp.broadcast_to(c[:, 1:2], g.shape)
            out = out + jnp.where(lane == h, di, 0.0) + jnp.where(lane == NH + h, db * dsig, 0.0)
        o_ref[...] = out.astype(BF16)
        s_ref[...] = jnp.sum(out, axis=0, keepdims=True)

    return pl.pallas_call(
        body, name="gates_bwd", grid=(1,),
        in_specs=[pl.BlockSpec((NH, t, LANE), lambda i: (0, 0, 0)),
                  pl.BlockSpec((t, LANE), lambda i: (0, gate_tile)), pl.BlockSpec((1, LANE), lambda i: (0, 0))],
        out_specs=[pl.BlockSpec((t, LANE), lambda i: (0, 0)), pl.BlockSpec((1, LANE), lambda i: (0, 0))],
        out_shape=[jax.ShapeDtypeStruct((t, LANE), BF16), jax.ShapeDtypeStruct((1, LANE), F32)],
        compiler_params=_params("arbitrary"),
    )(dgate, proj, bias_tile)


def _in_turn(heads):
    while heads:
        heads = [g for g in heads if next(g, heads) is not heads]


def _chunk_gates(gc, gr, h, mprev):
    L = CHUNK
    icol, bcol = gc[:, h:h + 1], gc[:, h + NH:h + NH + 1]
    irow, brow = gr[h:h + 1, :], gr[h + NH:h + NH + 1, :]
    tri = _iota((L, L), 0) >= _iota((L, L), 1)
    log_d = jnp.where(tri, bcol - brow + irow, -jnp.inf)
    inter = bcol + mprev
    mt = jnp.maximum(inter, jnp.max(log_d, axis=1, keepdims=True))
    dw = jnp.exp(log_d - mt)
    iw = jnp.exp(inter - mt)
    g = brow[:, L - 1:L]
    wlog_col = g - bcol + icol
    wlog_row = g - brow + irow
    mnew = jnp.maximum(g + mprev, jnp.max(wlog_row, axis=1, keepdims=True))
    wcol = jnp.exp(wlog_col - mnew)
    decay = jnp.exp(g + mprev - mnew)
    return dw, iw, mt, wcol, decay, mnew


def _mlstm_fwd(proj, gcol, grow, t, wc, dh):
    nc = t // CHUNK
    wm = NH * dh
    assert wc == wm, (wc, wm)
    qoff = 3 * wc // wm
    scale = dh ** -0.5

    def body(q_ref, k_ref, v_ref, gc_ref, gr_ref, h_ref, cs_ref, ns_ref, c_s, n_s, m_s):
        @pl.when(pl.program_id(0) == 0)
        def _():
            c_s[...] = jnp.zeros_like(c_s)
            n_s[...] = jnp.zeros_like(n_s)
            m_s[...] = jnp.zeros_like(m_s)

        gc, gr = gc_ref[...], gr_ref[0]
        done = [None] * NH

        def head(h):
            cols = slice(h * dh, (h + 1) * dh)
            mprev = m_s[h, 0:1, 0:1]
            cprev = c_s[h]
            n8 = n_s[h]
            nprev = n8[0:1]
            qs = q_ref[:, cols] * scale
            k = k_ref[:, cols]
            qs_b, k_b, v_b = qs.astype(BF16), k.astype(BF16), v_ref[:, cols].astype(BF16)
            qk = _dot(qs_b, k_b, _NT)
            yield
            q_c = _dot(qs_b, cprev.astype(BF16))
            yield
            dw, iw, mt, wcol, decay, mnew = _chunk_gates(gc, gr, h, mprev)
            yield
            s = qk * dw
            wk = wcol * k
            num = _dot(s.astype(BF16), v_b) + iw * q_c
            yield
            c_new = decay * cprev + _dot(wk.astype(BF16), v_b, _TN)
            yield
            den = jnp.sum(s, axis=1, keepdims=True) + iw * jnp.sum(qs * nprev, axis=1, keepdims=True)
            done[h] = (cprev, jnp.where(_iota(n8.shape, 0) == 1, mprev, n8),
                       num / jnp.maximum(jnp.abs(den), jnp.exp(-mt)), c_new,
                       decay * n8 + jnp.sum(wk, axis=0, keepdims=True), mnew)

        _in_turn([head(h) for h in range(NH)])
        for h, (c_old, n_old, h_out, c_new, n_new, m_new) in enumerate(done):
            cs_ref[h] = c_old
            ns_ref[h] = n_old
            h_ref[:, h * dh:(h + 1) * dh] = h_out
            c_s[h] = c_new
            n_s[h] = n_new
            m_s[h] = jnp.broadcast_to(m_new, m_s.shape[1:])

    grp = lambda off: pl.BlockSpec((CHUNK, wm), lambda c: (c, qoff + off))
    return pl.pallas_call(
        body, name="mlstm_fwd", grid=(nc,),
        in_specs=[grp(0), grp(1), grp(2),
                  pl.BlockSpec((CHUNK, LANE), lambda c: (c, 0)),
                  pl.BlockSpec((1, 8, CHUNK), lambda c: (c, 0, 0))],
        out_specs=[pl.BlockSpec((CHUNK, wm), lambda c: (c, 0)),
                   pl.BlockSpec((NH, None, dh, dh), lambda c: (0, c, 0, 0)),
                   pl.BlockSpec((NH, None, 8, dh), lambda c: (0, c, 0, 0))],
        out_shape=[jax.ShapeDtypeStruct((t, wm), F32),
                   jax.ShapeDtypeStruct((NH, nc, dh, dh), F32),
                   jax.ShapeDtypeStruct((NH, nc, 8, dh), F32)],
        scratch_shapes=[pltpu.VMEM((NH, dh, dh), F32), pltpu.VMEM((NH, 8, dh), F32), pltpu.VMEM((NH, 8, LANE), F32)],
        compiler_params=_params("arbitrary"),
    )(proj, proj, proj, gcol, grow)


def _mlstm_bwd(proj, gcol, grow, hval, dh_in, cs, ns, t, wc, dh):
    nc = t // CHUNK
    wm = NH * dh
    assert wc == wm, (wc, wm)
    qoff = 3 * wc // wm
    scale = dh ** -0.5
    L = CHUNK

    def body(q_ref, k_ref, v_ref, gc_ref, gr_ref, h_ref, dh_ref, cs_ref, ns_ref,
             dq_ref, dk_ref, dv_ref, dg_ref, dc_s, dn_s):
        @pl.when(pl.program_id(0) == 0)
        def _():
            dc_s[...] = jnp.zeros_like(dc_s)
            dn_s[...] = jnp.zeros_like(dn_s)

        gc, gr = gc_ref[...], gr_ref[0]
        eye = _iota((L, L), 0) == _iota((L, L), 1)
        lane = _iota((L, LANE), 1)
        last = _iota((L, 1), 0) == L - 1
        done = [None] * NH

        def head(h):
            cols = slice(h * dh, (h + 1) * dh)
            ns8 = ns_ref[h]
            nprev = ns8[0:1]
            mprev = ns8[1:2, 0:1]
            cprev = cs_ref[h]
            dcn = dc_s[h]
            dn8 = dn_s[h]
            dnn = dn8[0:1]

            qs = q_ref[:, cols] * scale
            k = k_ref[:, cols]
            qs_b, k_b, v_b = qs.astype(BF16), k.astype(BF16), v_ref[:, cols].astype(BF16)
            qk = _dot(qs_b, k_b, _NT)
            yield
            dw, iw, mt, wcol, decay, _ = _chunk_gates(gc, gr, h, mprev)
            yield
            s = qk * dw
            den = jnp.sum(s, axis=1, keepdims=True) + iw * jnp.sum(qs * nprev, axis=1, keepdims=True)
            emt = jnp.exp(-mt)
            r = 1.0 / jnp.maximum(jnp.abs(den), emt)
            dout = dh_ref[:, cols]
            dnum = dout * r
            dden = (-jnp.sum(dout * h_ref[:, cols], axis=1, keepdims=True) * r
                    * jnp.where(jnp.abs(den) > emt, jnp.sign(den), 0.0))
            dnum_b = dnum.astype(BF16)
            cprev_b = cprev.astype(BF16)
            dcn_b = dcn.astype(BF16)
            yield

            g_raw = _dot(dnum_b, v_b, _NT)
            yield
            q_inter = _dot(dnum_b, cprev_b, _NT)
            yield
            k_raw = _dot(v_b, dcn_b, _NT)
            yield
            gd = (g_raw + dden) * dw
            gd_b = gd.astype(BF16)
            dqs_inter = iw * (q_inter + dden * nprev)
            dk_inter = wcol * (k_raw + dnn)
            wk = wcol * k
            iq = iw * qs
            dqs = _dot(gd_b, k_b) + dqs_inter
            yield
            dk = _dot(gd_b, qs_b, _TN) + dk_inter
            yield
            dv = _dot(s.astype(BF16), dnum_b, _TN) + _dot(wk.astype(BF16), dcn_b)
            yield
            dc_new = decay * dcn + _dot(iq.astype(BF16), dnum_b, _TN)
            yield

            e = gd * qk
            e_cols = jnp.sum(jnp.where(eye, jnp.sum(e, axis=0, keepdims=True), 0.0), axis=1, keepdims=True)
            yield
            k_inter = jnp.sum(k * dk_inter, axis=1, keepdims=True)
            rq = jnp.sum(e, axis=1, keepdims=True) + jnp.sum(qs * dqs_inter, axis=1, keepdims=True)
            rk = e_cols + k_inter
            hsum = jnp.sum(k_inter, axis=0, keepdims=True)
            jdec = decay * (jnp.sum(jnp.sum(dcn * cprev, axis=1, keepdims=True), axis=0, keepdims=True)
                            + jnp.sum(dnn * nprev, axis=1, keepdims=True))
            db = rq - rk + jnp.where(last, hsum + jdec, 0.0)
            done[h] = (jnp.where(lane == 0, rk, jnp.where(lane == 1, db, 0.0)),
                       (dqs * scale).astype(BF16), dk.astype(BF16), dv.astype(BF16), dc_new,
                       decay * dn8 + jnp.sum(iq * dden, axis=0, keepdims=True))

        _in_turn([head(h) for h in range(NH)])
        for h, (dgate, dq, dk, dv, dc_new, dn_new) in enumerate(done):
            cols = slice(h * dh, (h + 1) * dh)
            dg_ref[h] = dgate
            dq_ref[:, cols] = dq
            dk_ref[:, cols] = dk
            dv_ref[:, cols] = dv
            dc_s[h] = dc_new
            dn_s[h] = dn_new

    rc = lambda c: nc - 1 - c
    grp = lambda off: pl.BlockSpec((L, wm), lambda c: (rc(c), qoff + off))
    hm = pl.BlockSpec((L, wm), lambda c: (rc(c), 0))
    act = jax.ShapeDtypeStruct((t, wm), BF16)
    return pl.pallas_call(
        body, name="mlstm_bwd", grid=(nc,),
        in_specs=[grp(0), grp(1), grp(2),
                  pl.BlockSpec((L, LANE), lambda c: (rc(c), 0)),
                  pl.BlockSpec((1, 8, L), lambda c: (rc(c), 0, 0)),
                  hm, hm,
                  pl.BlockSpec((NH, None, dh, dh), lambda c: (0, rc(c), 0, 0)),
                  pl.BlockSpec((NH, None, 8, dh), lambda c: (0, rc(c), 0, 0))],
        out_specs=[hm, hm, hm, pl.BlockSpec((NH, L, LANE), lambda c: (0, rc(c), 0))],
        out_shape=[act, act, act, jax.ShapeDtypeStruct((NH, t, LANE), F32)],
        scratch_shapes=[pltpu.VMEM((NH, dh, dh), F32), pltpu.VMEM((NH, 8, dh), F32)],
        compiler_params=_params("arbitrary"),
    )(proj, proj, proj, gcol, grow, hval, dh_in, cs, ns)


def _head_norm(hv):
    mu = jnp.mean(hv, axis=1, keepdims=True)
    hc = hv - mu
    rstd = lax.rsqrt(jnp.mean(hc * hc, axis=1, keepdims=True) + HN_EPS)
    return hc * rstd, rstd


def _hnorm_fwd(hval, proj, gain, y, t, wc, dh, tr=512):
    ooff = 3 * wc // dh + 3 * NH
    tr = min(tr, t)

    def body(h_ref, o_ref, g_ref, y_in, y_ref):
        hhat, _ = _head_norm(h_ref[...])
        y_ref[...] = (_sigmoid(o_ref[...]) * hhat * g_ref[...]).astype(BF16)

    return pl.pallas_call(
        body, name="hnorm_fwd", grid=(t // tr, NH),
        in_specs=[pl.BlockSpec((tr, dh), lambda i, h: (i, h)),
                  pl.BlockSpec((tr, dh), lambda i, h: (i, ooff + h)),
                  pl.BlockSpec((1, dh), lambda i, h: (0, h)),
                  pl.BlockSpec(memory_space=pl.ANY)],
        out_specs=pl.BlockSpec((None, tr, dh), lambda i, h: (1, i, h)),
        out_shape=jax.ShapeDtypeStruct(y.shape, BF16),
        input_output_aliases={3: 0},
        compiler_params=_params("parallel", "parallel"),
    )(hval, proj, gain, y)


def _hnorm_bwd(dy, hval, proj, gain, t, wc, dh, tr=512):
    ooff = 3 * wc // dh + 3 * NH
    tr = min(tr, t)
    yoff = wc // dh

    def body(dy_ref, h_ref, o_ref, g_ref, do_ref, dh_ref, dg_ref):
        i = pl.program_id(1)
        hhat, rstd = _head_norm(h_ref[...])
        gain_v = g_ref[...]
        sig = _sigmoid(o_ref[...])
        d = dy_ref[...]
        do_ref[...] = (d * hhat * gain_v * sig * (1.0 - sig)).astype(BF16)
        dhn = d * sig
        part = jnp.sum(dhn * hhat, axis=0, keepdims=True)

        @pl.when(i == 0)
        def _():
            dg_ref[...] = part

        @pl.when(i > 0)
        def _():
            dg_ref[...] += part

        dhat = dhn * gain_v
        dh_ref[...] = rstd * (dhat - jnp.mean(dhat, axis=1, keepdims=True)
                              - hhat * jnp.mean(dhat * hhat, axis=1, keepdims=True))

    blk = lambda off: pl.BlockSpec((tr, dh), lambda h, i: (i, off + h))
    return pl.pallas_call(
        body, name="hnorm_bwd", grid=(NH, t // tr),
        in_specs=[blk(yoff), blk(0), blk(ooff), pl.BlockSpec((1, dh), lambda h, i: (0, h))],
        out_specs=[blk(0), blk(0), pl.BlockSpec((1, dh), lambda h, i: (0, h))],
        out_shape=[jax.ShapeDtypeStruct((t, NH * dh), BF16), jax.ShapeDtypeStruct((t, NH * dh), F32),
                   jax.ShapeDtypeStruct((1, NH * dh), F32)],
        compiler_params=_params("parallel", "arbitrary"),
    )(dy, hval, proj, gain)


def _ln_stats(z):
    mu = jnp.mean(z, axis=1, keepdims=True)
    zc = z - mu
    rstd = lax.rsqrt(jnp.mean(zc * zc, axis=1, keepdims=True) + LN_EPS)
    return zc * rstd, rstd


def _ln_bwd(dy, xhat, rstd, g):
    dxh = dy * g
    return rstd * (dxh - jnp.mean(dxh, axis=1, keepdims=True) - xhat * jnp.mean(dxh * xhat, axis=1, keepdims=True))


def _accum(ref, i, part):
    @pl.when(i == 0)
    def _():
        ref[...] = part

    @pl.when(i > 0)
    def _():
        ref[...] += part


def _ln1_fwd(x, mix, g, b, tr=256):
    t, d = x.shape

    def body(x_ref, m_ref, g_ref, b_ref, xh_ref, rs_ref, xb_ref):
        xhat, rstd = _ln_stats(ALPHA * x_ref[...] + m_ref[...])
        xh_ref[...] = xhat
        rs_ref[...] = rstd
        xb_ref[...] = (xhat * g_ref[...] + b_ref[...]).astype(BF16)

    row = pl.BlockSpec((tr, d), lambda i: (i, 0))
    vec = pl.BlockSpec((1, d), lambda i: (0, 0))
    return pl.pallas_call(
        body, name="ln1_fwd", grid=(t // tr,),
        in_specs=[row, row, vec, vec],
        out_specs=[row, pl.BlockSpec((tr, 1), lambda i: (i, 0)), row],
        out_shape=[jax.ShapeDtypeStruct((t, d), F32), jax.ShapeDtypeStruct((t, 1), F32),
                   jax.ShapeDtypeStruct((t, d), BF16)],
        compiler_params=_params("parallel"),
    )(x, mix, g, b)


def _ln2_loss(xhat1, g1, b1, ff, target, g2, b2, tr=256):
    t, d = ff.shape

    def body(xh_ref, g1_ref, b1_ref, f_ref, t_ref, g_ref, b_ref, dz_ref, dzb_ref, dg_ref, db_ref, l_ref):
        i = pl.program_id(0)
        x1 = xh_ref[...] * g1_ref[...] + b1_ref[...]
        xhat, rstd = _ln_stats(ALPHA * x1 + f_ref[...])
        gv = g_ref[...]
        e = xhat * gv + b_ref[...] - t_ref[...]
        lsum = jnp.sum(jnp.sum(e * e, axis=1, keepdims=True), axis=0, keepdims=True) * (0.5 / d)
        dy = e * (1.0 / d)
        _accum(dg_ref, i, jnp.sum(dy * xhat, axis=0, keepdims=True))
        _accum(db_ref, i, jnp.sum(dy, axis=0, keepdims=True))
        _accum(l_ref, i, jnp.broadcast_to(lsum, l_ref.shape))
        dz = _ln_bwd(dy, xhat, rstd, gv)
        dz_ref[...] = dz
        dzb_ref[...] = dz.astype(BF16)

    row = pl.BlockSpec((tr, d), lambda i: (i, 0))
    vec = pl.BlockSpec((1, d), lambda i: (0, 0))
    return pl.pallas_call(
        body, name="ln2_loss", grid=(t // tr,),
        in_specs=[row, vec, vec, row, row, vec, vec],
        out_specs=[row, row, vec, vec, pl.BlockSpec((8, LANE), lambda i: (0, 0))],
        out_shape=[jax.ShapeDtypeStruct((t, d), F32), jax.ShapeDtypeStruct((t, d), BF16),
                   jax.ShapeDtypeStruct((1, d), F32), jax.ShapeDtypeStruct((1, d), F32),
                   jax.ShapeDtypeStruct((8, LANE), F32)],
        compiler_params=_params("arbitrary"),
    )(xhat1, g1, b1, ff, target, g2, b2)


def _ln1_bwd(dz2, dffn, xhat1, rstd1, g1, tr=256):
    t, d = dz2.shape

    def body(a_ref, f_ref, xh_ref, rs_ref, g_ref, dz_ref, dzb_ref, dg_ref, db_ref):
        i = pl.program_id(0)
        dy = ALPHA * a_ref[...] + f_ref[...]
        xhat = xh_ref[...]
        _accum(dg_ref, i, jnp.sum(dy * xhat, axis=0, keepdims=True))
        _accum(db_ref, i, jnp.sum(dy, axis=0, keepdims=True))
        dz = _ln_bwd(dy, xhat, rs_ref[...], g_ref[...])
        dz_ref[...] = dz
        dzb_ref[...] = dz.astype(BF16)

    row = pl.BlockSpec((tr, d), lambda i: (i, 0))
    vec = pl.BlockSpec((1, d), lambda i: (0, 0))
    return pl.pallas_call(
        body, name="ln1_bwd", grid=(t // tr,),
        in_specs=[row, row, row, pl.BlockSpec((tr, 1), lambda i: (i, 0)), vec],
        out_specs=[row, row, vec, vec],
        out_shape=[jax.ShapeDtypeStruct((t, d), F32), jax.ShapeDtypeStruct((t, d), BF16),
                   jax.ShapeDtypeStruct((1, d), F32), jax.ShapeDtypeStruct((1, d), F32)],
        compiler_params=_params("arbitrary"),
    )(dz2, dffn, xhat1, rstd1, g1)


def _ffn_act_fwd(hid0, w_fc, b_fc, t, dff):
    nb = dff // LANE

    def body(hv_ref, hg_ref, wv_ref, wg_ref, bv_ref, bg_ref, a_ref):
        val = _conv(hv_ref[...], wv_ref[...]) + bv_ref[...]
        gate = _conv(hg_ref[...], wg_ref[...]) + bg_ref[...]
        a_ref[...] = (gate * _sigmoid(gate) * val).astype(BF16)

    col = lambda off: pl.BlockSpec((t, LANE), lambda j: (0, j + off))
    w3 = lambda off: pl.BlockSpec((3, LANE), lambda j: (0, j + off))
    w1 = lambda off: pl.BlockSpec((1, LANE), lambda j: (0, j + off))
    return pl.pallas_call(
        body, name="ffn_act_fwd", grid=(nb,),
        in_specs=[col(0), col(nb), w3(0), w3(nb), w1(0), w1(nb)],
        out_specs=col(0),
        out_shape=jax.ShapeDtypeStruct((t, dff), BF16),
        compiler_params=_params("parallel"),
    )(hid0, hid0, w_fc, w_fc, b_fc, b_fc)


def _ffn_act_bwd(da, hid0, w_fc, b_fc, t, dff):
    nb = dff // LANE

    def body(da_ref, hv_ref, hg_ref, wv_ref, wg_ref, bv_ref, bg_ref,
             dhv_ref, dhg_ref, dwv_ref, dwg_ref, dbv_ref, dbg_ref):
        hv, hg, wv, wg = hv_ref[...], hg_ref[...], wv_ref[...], wg_ref[...]
        rv, rg = _rolled(hv), _rolled(hg)
        val = _conv(hv, wv, rv) + bv_ref[...]
        gate = _conv(hg, wg, rg) + bg_ref[...]
        sig = _sigmoid(gate)
        d = da_ref[...]
        dsig = d * sig
        dval = dsig * gate
        dgate = dsig * val * (1.0 + gate * (1.0 - sig))
        dhv_ref[...] = _conv_t(dval, wv).astype(BF16)
        dhg_ref[...] = _conv_t(dgate, wg).astype(BF16)
        dwv_ref[...] = _conv_dw(dval, hv, rv)
        dwg_ref[...] = _conv_dw(dgate, hg, rg)
        dbv_ref[...] = jnp.sum(dval, axis=0, keepdims=True)
        dbg_ref[...] = jnp.sum(dgate, axis=0, keepdims=True)

    col = lambda off: pl.BlockSpec((t, LANE), lambda j: (0, j + off))
    w3 = lambda off: pl.BlockSpec((3, LANE), lambda j: (0, j + off))
    w1 = lambda off: pl.BlockSpec((1, LANE), lambda j: (0, j + off))
    s3 = jax.ShapeDtypeStruct((3, dff), F32)
    s1 = jax.ShapeDtypeStruct((1, dff), F32)
    return pl.pallas_call(
        body, name="ffn_act_bwd", grid=(nb,),
        in_specs=[col(0), col(0), col(nb), w3(0), w3(nb), w1(0), w1(nb)],
        out_specs=[col(0), col(0), w3(0), w3(0), w1(0), w1(0)],
        out_shape=[jax.ShapeDtypeStruct((t, dff), BF16)] * 2 + [s3, s3, s1, s1],
        compiler_params=_params("parallel"),
    )(da, hid0, hid0, w_fc, w_fc, b_fc, b_fc)


class _Ready:
    def __init__(self, **weights):
        self.weights = weights

    def begin(self, after):
        return None

    def forward(self, name, after):
        return None

    def get(self, name, after):
        return self.weights[name]


class _Kept:
    def __init__(self):
        self.grads = {}

    def start(self, name, grad):
        self.grads[name] = grad
        return None

    def relay(self, name, after):
        return None

    def meanwhile(self, small, loss, after):
        return None


def _behind(a, token):
    return a if token is None else a + token[0:1, 0:1].reshape((1,) * a.ndim)


def _local_step(x, target, w_in, b_gates, w_sc, gain, w_out, ln1_g, ln1_b, w_up, w_fc, b_fc, w_down, ln2_g, ln2_b,
                gx=None, wx=None, x_b=None):
    t, d = x.shape
    wc = d // 2
    dh = (d - wc) // NH
    wm = NH * dh
    dff = w_fc.shape[1] // 2
    if wx is None:
        wx = _Ready(w_out=w_out, w_up=w_up, w_down=w_down)
    ninp = 3 * wc + 4 * wm + LANE
    nin = 3 * wc + 4 * wm
    gate_tile = nin // LANE
    nc = t // CHUNK
    bias_tile = jnp.pad(b_gates, ((0, 0), (0, LANE - 2 * NH)))

    if x_b is None:
        x_b = x.astype(BF16)
    proj = _matmul(x_b, w_in, "nt", F32, "proj", tm=512, tn=2432, tk=d, n=ninp, after=wx.begin(w_in))
    y = _sconv_fwd(proj, w_sc, t, wc)
    gcol = _gates_prep(proj, bias_tile, t, gate_tile)
    grow = gcol[:, :8].T.reshape(8, nc, CHUNK).transpose(1, 0, 2)
    hval, cs, ns = _mlstm_fwd(proj, gcol, grow, t, wc, dh)
    y = _hnorm_fwd(hval, proj, gain, y, t, wc, dh)
    tok = wx.forward("w_out", y)
    w_out = wx.get("w_out", tok)
    mix = _matmul(y, w_out, "nn", F32, "out_proj", tm=512, tn=1024, tk=wc, a_blocked=True, after=tok)
    xhat1, rstd1, x1_b = _ln1_fwd(x, mix, _behind(ln1_g, wx.forward("w_up", mix)), ln1_b)
    w_up = wx.get("w_up", x1_b)
    wsl = w_up.shape[2]
    hid0 = _matmul(x1_b, w_up, "nn", F32, "ffn_up", tm=512, tn=wsl, tk=d, b_blocked=True)
    act = _ffn_act_fwd(hid0, w_fc, _behind(b_fc, wx.forward("w_down", hid0)), t, dff)
    w_down = wx.get("w_down", act)
    ff = _matmul(act, w_down, "nn", F32, "ffn_down", tm=1024, tn=512, tk=dff)
    dz2, dz2_b, d_ln2_g, d_ln2_b, loss = _ln2_loss(xhat1, ln1_g, ln1_b, ff, target, ln2_g, ln2_b)

    if gx is None:
        gx = _Kept()
    d_w_down = _matmul(act, dz2_b, "tn", BF16, "ffn_down_dw", tm=512, tn=1024, tk=t)
    d_act = _matmul(dz2_b, w_down, "nt", F32, "ffn_down_dx", tm=1024, tn=512, tk=d, after=gx.start("w_down", d_w_down))
    *d_hid0, dwv, dwg, dbv, dbg = _ffn_act_bwd(d_act, hid0, w_fc, _behind(b_fc, gx.relay("w_down", d_act)), t, dff)
    d_w_fc = jnp.concatenate([dwv, dwg], axis=1)
    d_b_fc = jnp.concatenate([dbv, dbg], axis=1)
    d_hid0 = tuple(d_hid0[:2])
    d_w_up = _matmul(x1_b, d_hid0, "tn", BF16, "ffn_up_dw", tm=512, tn=wsl, tk=t, o_width=wsl)
    d_x1_ffn = _matmul(d_hid0, w_up, "nt", F32, "ffn_up_dx", tm=1024, tn=1024, tk=wsl, b_blocked=True,
                       after=gx.start("w_up", d_w_up))
    dz1, dz1_b, d_ln1_g, d_ln1_b = _ln1_bwd(dz2, d_x1_ffn, xhat1, rstd1, _behind(ln1_g, gx.relay("w_up", d_x1_ffn)))

    d_w_out = _matmul(y, dz1_b, "tn", BF16, "out_proj_dw", tm=512, tn=1024, tk=t, a_blocked=True)
    dy = _matmul(dz1_b, w_out, "nt", F32, "out_proj_dx", tm=512, tn=1024, tk=d, after=gx.start("w_out", d_w_out))
    dcb, dcc, dch, d_w_sc = _sconv_bwd(dy, proj, _behind(w_sc, gx.relay("w_out", dy)), t, wc)
    d_o, d_hval, d_gain = _hnorm_bwd(dy, hval, proj, gain, t, wc, dh)
    dq, dk, dv, dgate = _mlstm_bwd(proj, gcol, grow, hval, d_hval, cs, ns, t, wc, dh)
    dgt, d_b_gates = _gates_bwd(dgate, proj, bias_tile, t, gate_tile)
    d_proj = jnp.concatenate([dcb, dcc, dch, dq, dk, dv, d_o, dgt], axis=1)
    d_w_in = _matmul(d_proj, x_b, "tn", BF16, "proj_dw", tm=2432, tn=1024, tk=t)
    small = dict(b_gates=d_b_gates[:, :2 * NH], w_sc_conv=d_w_sc, mh_gain=d_gain, ln1_g=d_ln1_g, ln1_b=d_ln1_b,
                 w_ffn_conv=d_w_fc, b_ffn_conv=d_b_fc, ln2_g=d_ln2_g, ln2_b=d_ln2_b)
    token = gx.start("w_in", d_w_in)
    token = gx.relay("w_in", gx.meanwhile(small, loss, token))
    grad_x = _matmul(d_proj, w_in, "nn", F32, "proj_dx", tm=512, tn=512, tk=ninp, add=dz1, add_scale=ALPHA, after=token)
    return loss, grad_x, small, gx


HBM = pl.BlockSpec(memory_space=pltpu.HBM)


def _place():
    return lax.axis_index("x"), lax.axis_index("y"), lax.axis_index("c")


def _index(p):
    return 4 * p[0] + 2 * p[1] + p[2]


def _all_gather(arrs, name):
    n = len(arrs)

    def body(*refs):
        ins, outs = refs[:n], refs[n:2 * n]
        send_sems, recv_sems, local_sems = refs[2 * n:]
        x, y, c = _place()
        me, sibling = (x, y, c), (x, y, 1 - c)
        chips = [(1 - x, y), (x, 1 - y), (1 - x, 1 - y)]

        def copy(a, k, block, to, own=False):
            dst = outs[a].at[_index(block)]
            return pltpu.make_async_remote_copy(
                src_ref=ins[a] if own else dst, dst_ref=dst,
                send_sem=send_sems.at[k * n + a], recv_sem=recv_sems.at[k * n + a],
                device_id=to, device_id_type=MESH)

        mine = [pltpu.make_async_copy(ins[a], outs[a].at[_index(me)], local_sems.at[a]) for a in range(n)]
        for cp in mine:
            cp.start()
        first = []
        for a in range(n):
            first.append(copy(a, 0, me, sibling, own=True))
            first += [copy(a, 1 + j, me, (*chip, c), own=True) for j, chip in enumerate(chips)]
        for cp in first:
            cp.start()
        passed = []
        for j, chip in enumerate(chips):
            for a in range(n):
                copy(a, 1 + j, (*chip, c), me).wait_recv()
                cp = copy(a, 4 + j, (*chip, c), sibling)
                cp.start()
                passed.append(cp)
        for a in range(n):
            copy(a, 0, sibling, me).wait_recv()
            for j, chip in enumerate(chips):
                copy(a, 4 + j, (*chip, 1 - c), me).wait_recv()
        for cp in first + passed:
            cp.wait_send()
        for cp in mine:
            cp.wait()

    return pl.pallas_call(
        body, name=name, in_specs=[HBM] * n, out_specs=[HBM] * n,
        out_shape=[jax.ShapeDtypeStruct((N_DEV,) + a.shape, a.dtype) for a in arrs],
        scratch_shapes=[pltpu.SemaphoreType.DMA((7 * n,)), pltpu.SemaphoreType.DMA((7 * n,)),
                        pltpu.SemaphoreType.DMA((n,))],
    )(*arrs)


SEM = pl.BlockSpec(memory_space=pltpu.SEMAPHORE)
EFFECT = pltpu.SideEffectType.DATAFLOW_SIDE_EFFECTING


def _chips(x, y):
    return [(1 - x, y), (x, 1 - y), (1 - x, 1 - y)]


N_CHIP = N_DEV // 2


def _pair_route(x, y, c):
    return [((x, y, 1 - c), 2 * q + (1 - c), q, q) for q in range(N_CHIP)]


def _chip_route(x, y, c):
    mine = 2 * x + y
    return [((*chip, c), 2 * chip[0] + chip[1], mine, 2 * chip[0] + chip[1]) for chip in _chips(x, y)]


def _exchange_pieces(g_ref, land_ref, width, tail):
    if not tail:
        return [(lambda i: g_ref.at[i], lambda s: land_ref.at[s])]
    rows = lambda i, n: pl.ds(pl.multiple_of(i * width, IN_TAIL), n)
    return [(lambda i: g_ref.at[rows(i, width), :], lambda s: land_ref.at[s, pl.ds(0, width), :]),
            (lambda i: g_ref.at[rows(i + 1, IN_TAIL), :], lambda s: land_ref.at[s, pl.ds(width, IN_TAIL), :])]


def _exchange_start(grad, route, tail, name):
    width = IN_SLAB if tail else grad.shape[1]
    n_p = 2 if tail else 1
    n_c = len(route(0, 0, 0))
    land_shape = (N_CHIP, width + (IN_TAIL if tail else 0), grad.shape[-1])

    def body(g_ref, land_ref, send_sems, recv_sems, g_thru, land_thru, token):
        for j, (peer, slab, slot, _) in enumerate(route(*_place())):
            for p, (src, dst) in enumerate(_exchange_pieces(g_ref, land_ref, width, tail)):
                pltpu.make_async_remote_copy(src_ref=src(slab), dst_ref=dst(slot), send_sem=send_sems.at[j * n_p + p],
                                             recv_sem=recv_sems.at[j * n_p + p], device_id=peer,
                                             device_id_type=MESH).start()
        token[...] = jnp.zeros_like(token)

    return pl.pallas_call(
        body, name=name,
        out_shape=(pltpu.SemaphoreType.DMA((n_c * n_p,)), pltpu.SemaphoreType.DMA((n_c * n_p,)),
                   pltpu.HBM(grad.shape, grad.dtype), pltpu.HBM(land_shape, grad.dtype),
                   jax.ShapeDtypeStruct((8, LANE), F32)),
        in_specs=(HBM, HBM), out_specs=(SEM, SEM, HBM, HBM, pl.BlockSpec(memory_space=pltpu.VMEM)),
        input_output_aliases={0: 2, 1: 3},
        compiler_params=pltpu.CompilerParams(has_side_effects=EFFECT),
    )(pltpu.with_memory_space_constraint(grad, pltpu.HBM),
      pltpu.with_memory_space_constraint(lax.empty(land_shape, grad.dtype), pltpu.HBM))


def _exchange_wait(send_sems, recv_sems, g_thru, land_thru, after, route, tail, name):
    width = IN_SLAB if tail else g_thru.shape[1]
    n_p = 2 if tail else 1

    def body(g_ref, land_ref, send_sems, recv_sems, after_ref, g_dead, got_ref):
        for j, (peer, slab, _, slot) in enumerate(route(*_place())):
            for p, (src, dst) in enumerate(_exchange_pieces(g_ref, land_ref, width, tail)):
                cp = pltpu.make_async_remote_copy(src_ref=src(slab), dst_ref=dst(slot),
                                                  send_sem=send_sems.at[j * n_p + p], recv_sem=recv_sems.at[j * n_p + p],
                                                  device_id=peer, device_id_type=MESH)
                cp.wait_send()
                cp.wait_recv()

    return pl.pallas_call(
        body, name=name,
        out_shape=(pltpu.HBM(g_thru.shape, g_thru.dtype), pltpu.HBM(land_thru.shape, land_thru.dtype)),
        in_specs=(HBM, HBM, SEM, SEM, pl.BlockSpec(memory_space=pl.ANY)), out_specs=(HBM, HBM),
        input_output_aliases={0: 0, 1: 1},
        compiler_params=pltpu.CompilerParams(has_side_effects=EFFECT),
    )(g_thru, land_thru, send_sems, recv_sems, after)


def _pair_add(grad, pair, core, tail, name):
    rows, cols = (IN_SLAB if tail else grad.shape[1]), grad.shape[-1]
    total = pair.shape[1]

    def body(core_ref, *refs):
        if tail:
            g_ref, t_ref, p_ref, o_ref = refs
            o_ref[0:rows, :] = (g_ref[...].astype(F32) + p_ref[0:rows, :].astype(F32)).astype(BF16)
            o_ref[rows:total, :] = (t_ref[...].astype(F32) + p_ref[rows:total, :].astype(F32)).astype(BF16)
        else:
            g_ref, p_ref, o_ref = refs
            o_ref[...] = (g_ref[...].astype(F32) + p_ref[...].astype(F32)).astype(BF16)

    if tail:
        tc = _fit(cols, 512)
        grid = (N_CHIP, cols // tc)
        slab = pl.BlockSpec((None, total, tc), lambda q, i, core_ref: (q, 0, i))
        per = IN_SLAB // IN_TAIL
        in_specs = [pl.BlockSpec((rows, tc), lambda q, i, core_ref: (2 * q + core_ref[0], i)),
                    pl.BlockSpec((IN_TAIL, tc), lambda q, i, core_ref: ((2 * q + core_ref[0] + 1) * per, i))]
    else:
        tr = _rows(rows, 1024)
        grid = (N_CHIP, rows // tr)
        slab = pl.BlockSpec((None, tr, cols), lambda q, i, core_ref: (q, i, 0))
        in_specs = [pl.BlockSpec((None, tr, cols), lambda q, i, core_ref: (2 * q + core_ref[0], i, 0))]
    return pl.pallas_call(
        body, name=name,
        grid_spec=pltpu.PrefetchScalarGridSpec(num_scalar_prefetch=1, grid=grid,
                                               in_specs=in_specs + [slab], out_specs=slab),
        out_shape=jax.ShapeDtypeStruct(pair.shape, BF16),
        compiler_params=_params("parallel", "parallel"),
    )(core, *([grad, grad] if tail else [grad]), pair)


def _gather_start(blocks, after, name, spare=()):
    n = len(blocks)
    lands = [(N_DEV + (a in spare),) + b.shape for a, b in enumerate(blocks)]

    def body(*refs):
        b_refs, land_refs = refs[:n], refs[n:2 * n]
        send_sems, recv_sems = refs[2 * n + 1:3 * n + 1], refs[3 * n + 1:4 * n + 1]
        token = refs[-1]
        x, y, c = _place()
        me = _index((x, y, c))
        for a in range(n):
            for k, to in enumerate([(x, y, 1 - c)] + [(*chip, c) for chip in _chips(x, y)]):
                pltpu.make_async_remote_copy(src_ref=b_refs[a], dst_ref=land_refs[a].at[me], send_sem=send_sems[a].at[k],
                                             recv_sem=recv_sems[a].at[k], device_id=to, device_id_type=MESH).start()
        token[...] = jnp.zeros_like(token)

    sems = [pltpu.SemaphoreType.DMA((4,))] * n
    out = pl.pallas_call(
        body, name=name,
        out_shape=(*sems, *sems, *[pltpu.HBM(b.shape, b.dtype) for b in blocks],
                   *[pltpu.HBM(s, b.dtype) for s, b in zip(lands, blocks)], jax.ShapeDtypeStruct((8, LANE), F32)),
        in_specs=(*[HBM] * (2 * n), pl.BlockSpec(memory_space=pl.ANY)),
        out_specs=(*[SEM] * (2 * n), *[HBM] * (2 * n), pl.BlockSpec(memory_space=pltpu.VMEM)),
        input_output_aliases={i: 2 * n + i for i in range(2 * n)},
        compiler_params=pltpu.CompilerParams(has_side_effects=EFFECT),
    )(*[pltpu.with_memory_space_constraint(b, pltpu.HBM) for b in blocks],
      *[pltpu.with_memory_space_constraint(lax.empty(s, b.dtype), pltpu.HBM) for s, b in zip(lands, blocks)], after)
    return [(out[a], out[n + a], out[2 * n + a], out[3 * n + a]) for a in range(n)], out[-1]


def _gather_forward(send_sems, recv_sems, b_thru, land_thru, after, name):
    def body(b_ref, land_ref, send_sems, recv_sems, after_ref, b_dead, land_out, send2, recv2, token):
        x, y, c = _place()
        sibling = (x, y, 1 - c)
        for k, frm in enumerate([sibling] + [(*chip, c) for chip in _chips(x, y)]):
            cp = pltpu.make_async_remote_copy(src_ref=b_ref, dst_ref=land_ref.at[_index(frm)], send_sem=send_sems.at[k],
                                              recv_sem=recv_sems.at[k], device_id=frm, device_id_type=MESH)
            cp.wait_send()
            cp.wait_recv()
        for j, chip in enumerate(_chips(x, y)):
            slot = land_ref.at[_index((*chip, c))]
            pltpu.make_async_remote_copy(src_ref=slot, dst_ref=slot, send_sem=send2.at[j], recv_sem=recv2.at[j],
                                         device_id=sibling, device_id_type=MESH).start()
        token[...] = jnp.zeros_like(token)

    return pl.pallas_call(
        body, name=name,
        out_shape=(pltpu.HBM(b_thru.shape, b_thru.dtype), pltpu.HBM(land_thru.shape, land_thru.dtype),
                   pltpu.SemaphoreType.DMA((3,)), pltpu.SemaphoreType.DMA((3,)), jax.ShapeDtypeStruct((8, LANE), F32)),
        in_specs=(HBM, HBM, SEM, SEM, pl.BlockSpec(memory_space=pl.ANY)),
        out_specs=(HBM, HBM, SEM, SEM, pl.BlockSpec(memory_space=pltpu.VMEM)),
        input_output_aliases={0: 0, 1: 1},
        compiler_params=pltpu.CompilerParams(has_side_effects=EFFECT),
    )(b_thru, land_thru, send_sems, recv_sems, after)


def _gather_finish(land_thru, send2, recv2, after, name):
    def body(land_ref, send2, recv2, after_ref, land_out):
        x, y, c = _place()
        for j, chip in enumerate(_chips(x, y)):
            cp = pltpu.make_async_remote_copy(src_ref=land_ref.at[_index((*chip, c))],
                                              dst_ref=land_ref.at[_index((*chip, 1 - c))], send_sem=send2.at[j],
                                              recv_sem=recv2.at[j], device_id=(x, y, 1 - c), device_id_type=MESH)
            cp.wait_send()
            cp.wait_recv()

    return pl.pallas_call(
        body, name=name, out_shape=pltpu.HBM(land_thru.shape, land_thru.dtype),
        in_specs=(HBM, SEM, SEM, pl.BlockSpec(memory_space=pl.ANY)), out_specs=HBM,
        input_output_aliases={0: 0},
        compiler_params=pltpu.CompilerParams(has_side_effects=EFFECT),
    )(land_thru, send2, recv2, after)


class _Gathering:
    def __init__(self, first, later, me):
        started, token = _gather_start(list(first.values()), next(iter(first.values())), "gather1_first", spare=(0,))
        cast = [_behind(a, token).astype(BF16) for a in later.values()]
        started_later, self.token = _gather_start(cast, token, "gather1_later")
        self.me, self.state = me, dict(zip([*first, *later], started + started_later))

    def begin(self, after):
        return self.token

    def forward(self, name, after):
        *self.state[name], token = _gather_forward(*self.state[name], after, "gather2_" + name)
        return token

    def get(self, name, after):
        block, land, send2, recv2 = self.state[name]
        land = _gather_finish(land, send2, recv2, after, "gather3_" + name)
        land = lax.dynamic_update_index_in_dim(land, block[None], self.me, 0)
        return land if name not in ("w_out", "w_down") else land.reshape(-1, land.shape[2])


class _Reducing:
    def __init__(self, core, chip, gather_small):
        self.core, self.chip, self.state, self.token, self.gather_small = core, chip, {}, None, gather_small

    def meanwhile(self, small, loss, after):
        self.small_sum = self.gather_small(small, loss, after)
        return self.small_sum

    def start(self, name, grad):
        tail = name == "w_in"
        g = grad if tail or grad.ndim == 3 else grad.reshape(N_DEV, grad.shape[0] // N_DEV, grad.shape[1])
        *self.state[name], token = _exchange_start(g, _pair_route, tail, "pair_send_" + name)
        return token

    def relay(self, name, after):
        tail = name == "w_in"
        grad, pair = _exchange_wait(*self.state[name], after, _pair_route, tail, "pair_recv_" + name)
        total = _pair_add(grad, pair, self.core, tail, "pair_add_" + name)
        *self.state[name], self.token = _exchange_start(total, _chip_route, False, "chip_send_" + name)
        return self.token

    def finish(self, name, after):
        total, land = _exchange_wait(*self.state[name], after, _chip_route, False, "chip_recv_" + name)
        own = lax.dynamic_index_in_dim(total, self.chip, 0, keepdims=True)
        return lax.dynamic_update_index_in_dim(land, own, self.chip, 0)


def _carry_w_in(main, tail):
    slabs, _, d = main.shape
    tc = _fit(d, 2048)
    assert slabs == N_DEV + 1 and tail.shape[:2] == (N_DEV, IN_TAIL), (main.shape, tail.shape)
    top = lambda off: pl.BlockSpec((None, IN_TAIL, tc), lambda s, j: (s + off, 0, j))

    def carry(m_ref, t_ref, o_ref):
        o_ref[...] = m_ref[...] + t_ref[...]

    main = pl.pallas_call(
        carry, name="carry_w_in", grid=(N_DEV - 1, d // tc), in_specs=[top(1), top(0)], out_specs=top(1),
        out_shape=jax.ShapeDtypeStruct(main.shape, main.dtype), input_output_aliases={0: 0},
        compiler_params=_params("parallel", "parallel"),
    )(main, tail)

    def last(m_ref, t_ref, o_ref):
        o_ref[...] = jnp.zeros_like(o_ref)
        o_ref[0:IN_TAIL, :] = t_ref[...]

    return pl.pallas_call(
        last, name="last_slab_w_in", grid=(d // tc,),
        in_specs=[pl.BlockSpec(memory_space=pl.ANY), pl.BlockSpec((None, IN_TAIL, tc), lambda j: (N_DEV - 1, 0, j))],
        out_specs=pl.BlockSpec((None, LANE, tc), lambda j: (N_DEV, 0, j)),
        out_shape=jax.ShapeDtypeStruct(main.shape, main.dtype), input_output_aliases={0: 0},
        compiler_params=_params("parallel"),
    )(main, tail)


def _rows(n, want):
    t = min(n, want)
    t -= t % 16
    while n % t:
        t -= 16
    return t


def _adam_math(w, g, m, v):
    m2 = ADAM_B1 * m + (1.0 - ADAM_B1) * g
    v2 = ADAM_B2 * v + (1.0 - ADAM_B2) * (g * g)
    m_hat = m2 * (1.0 / (1.0 - ADAM_B1 ** ADAM_STEP))
    v_hat = v2 * (1.0 / (1.0 - ADAM_B2 ** ADAM_STEP))
    return -ADAM_LR * (m_hat / (jnp.sqrt(v_hat) + ADAM_EPS) + ADAM_WD * w), m2, v2


def _slot_sum(r_ref):
    acc = r_ref[0].astype(F32)
    for i in range(1, r_ref.shape[0]):
        acc = acc + r_ref[i].astype(F32)
    return acc


def _shift_w_in(w):
    ws, d = w.shape
    tc = _fit(d, 256)

    def body(w_ref, main_ref, tail_ref, tall):
        tall[...] = jnp.zeros_like(tall)
        tall[0:ws, :] = w_ref[...]
        moved = pltpu.roll(tall[...], _index(_place()), 0).astype(BF16)
        main_ref[...] = moved[0:IN_SLAB]
        tail_ref[...] = moved[IN_SLAB:]

    return pl.pallas_call(
        body, name="shift_w_in", grid=(d // tc,),
        in_specs=[pl.BlockSpec((ws, tc), lambda j: (0, j))],
        out_specs=[pl.BlockSpec((IN_SLAB, tc), lambda j: (0, j)), pl.BlockSpec((IN_TAIL, tc), lambda j: (0, j))],
        out_shape=[jax.ShapeDtypeStruct((IN_SLAB, d), BF16), jax.ShapeDtypeStruct((IN_TAIL, d), BF16)],
        scratch_shapes=[pltpu.VMEM((IN_SLAB + IN_TAIL, tc), F32)], compiler_params=_params("parallel"),
    )(w)


def _sum_adamw_shifted(r, w, m, v, name):
    _, ph, d = r.shape
    ws = w.shape[0]
    tc = _fit(d, 256)

    def body(r_ref, w_ref, m_ref, v_ref, g_ref, d_ref, m2_ref, v2_ref, tall):
        tall[...] = pltpu.roll(_slot_sum(r_ref), lax.rem(ph - _index(_place()), ph), 0)
        g = tall[0:ws, :]
        g_ref[...] = g
        d_ref[...], m2_ref[...], v2_ref[...] = _adam_math(w_ref[...], g, m_ref[...], v_ref[...])

    blk = pl.BlockSpec((ws, tc), lambda j: (0, j))
    out = jax.ShapeDtypeStruct(w.shape, F32)
    return pl.pallas_call(
        body, name=name, grid=(d // tc,),
        in_specs=[pl.BlockSpec((r.shape[0], ph, tc), lambda j: (0, 0, j)), blk, blk, blk],
        out_specs=[blk] * 4, out_shape=[out] * 4,
        scratch_shapes=[pltpu.VMEM((ph, tc), F32)], compiler_params=_params("parallel"),
    )(r, w, m, v)


def _sum_slots(r, name, tr=128):
    _, rows, cols = r.shape
    tr = _rows(rows, tr)

    def body(r_ref, g_ref):
        g_ref[...] = _slot_sum(r_ref)

    return pl.pallas_call(
        body, name=name, grid=(rows // tr,),
        in_specs=[pl.BlockSpec((r.shape[0], tr, cols), lambda i: (0, i, 0))],
        out_specs=pl.BlockSpec((tr, cols), lambda i: (i, 0)),
        out_shape=jax.ShapeDtypeStruct((rows, cols), F32),
        compiler_params=_params("parallel"),
    )(r)


def _adamw(w, g, m, v, name, tr=256):
    rows, cols = w.shape
    tr = _rows(rows, tr)

    def body(w_ref, g_ref, m_ref, v_ref, d_ref, m2_ref, v2_ref):
        d_ref[...], m2_ref[...], v2_ref[...] = _adam_math(w_ref[...], g_ref[...], m_ref[...], v_ref[...])

    blk = pl.BlockSpec((tr, cols), lambda i: (i, 0))
    out = jax.ShapeDtypeStruct((rows, cols), F32)
    return pl.pallas_call(
        body, name=name, grid=(rows // tr,), in_specs=[blk] * 4, out_specs=[blk] * 3, out_shape=[out] * 3,
        compiler_params=_params("parallel"),
    )(w, g, m, v)


def _sum_adamw(r, w, m, v, name, tr=256):
    rows, cols = w.shape
    tr = _rows(rows, tr)

    def body(r_ref, w_ref, m_ref, v_ref, g_ref, d_ref, m2_ref, v2_ref):
        g = _slot_sum(r_ref)
        g_ref[...] = g
        d_ref[...], m2_ref[...], v2_ref[...] = _adam_math(w_ref[...], g, m_ref[...], v_ref[...])

    blk = pl.BlockSpec((tr, cols), lambda i: (i, 0))
    out = jax.ShapeDtypeStruct((rows, cols), F32)
    return pl.pallas_call(
        body, name=name, grid=(rows // tr,),
        in_specs=[pl.BlockSpec((r.shape[0], tr, cols), lambda i: (0, i, 0)), blk, blk, blk],
        out_specs=[blk] * 4, out_shape=[out] * 4,
        compiler_params=_params("parallel"),
    )(r, w, m, v)


def _pack(pieces, sizes):
    flat = [jnp.pad(p.reshape(-1).astype(F32), (0, s - p.size)) for p, s in zip(pieces, sizes)]
    total = sum(sizes)
    padded = -(-total // (16 * LANE)) * (16 * LANE)
    return jnp.pad(jnp.concatenate(flat), (0, padded - total)).reshape(-1, LANE)


def _unpack(packed, shapes, sizes):
    flat = packed.reshape(-1)
    out, off = [], 0
    for shp, s in zip(shapes, sizes):
        n = 1
        for k in shp:
            n *= k
        out.append(flat[off:off + n].reshape(shp))
        off += s
    return out


def _lanes(n):
    return -(-n // LANE) * LANE


WEIGHTS = ("w_in", "b_gates", "w_sc_conv", "mh_gain", "w_out", "ln1_g", "ln1_b", "w_up", "w_ffn_conv", "b_ffn_conv",
           "w_down", "ln2_g", "ln2_b")
BIG = ("w_in", "w_out", "w_up", "w_down")
SMALL = tuple(n for n in WEIGHTS if n not in BIG)


def kernel(x, w_in, b_gates, w_sc_conv, mh_gain, w_out, ln1_g, ln1_b, w_up, w_ffn_conv, b_ffn_conv, w_down, ln2_g, ln2_b, loss_target, m_w_in, m_b_gates, m_w_sc_conv, m_mh_gain, m_w_out, m_ln1_g, m_ln1_b, m_w_up, m_w_ffn_conv, m_b_ffn_conv, m_w_down, m_ln2_g, m_ln2_b, v_w_in, v_b_gates, v_w_sc_conv, v_mh_gain, v_w_out, v_ln1_g, v_ln1_b, v_w_up, v_w_ffn_conv, v_b_ffn_conv, v_w_down, v_ln2_g, v_ln2_b):
    w = dict(zip(WEIGHTS, (w_in, b_gates, w_sc_conv, mh_gain, w_out, ln1_g, ln1_b, w_up, w_ffn_conv, b_ffn_conv,
                           w_down, ln2_g, ln2_b)))
    m = dict(zip(WEIGHTS, (m_w_in, m_b_gates, m_w_sc_conv, m_mh_gain, m_w_out, m_ln1_g, m_ln1_b, m_w_up,
                           m_w_ffn_conv, m_b_ffn_conv, m_w_down, m_ln2_g, m_ln2_b)))
    v = dict(zip(WEIGHTS, (v_w_in, v_b_gates, v_w_sc_conv, v_mh_gain, v_w_out, v_ln1_g, v_ln1_b, v_w_up,
                           v_w_ffn_conv, v_b_ffn_conv, v_w_down, v_ln2_g, v_ln2_b)))
    me = _index(_place())
    d = x.shape[2]
    ws_in = w_in.shape[2]
    assert ws_in == IN_SLAB + 1 and N_DEV <= LANE, w_in.shape
    ninp = (N_DEV + 1) * IN_SLAB
    ws_sc, ws_fc = w_sc_conv.shape[2], w_ffn_conv.shape[2]
    w_in_t, m_in_t, v_in_t = (jnp.transpose(a[0]) for a in (w_in, m_w_in, v_w_in))

    w_in_main, w_in_tail = _shift_w_in(w_in_t)
    taps8 = lambda a: jnp.pad(a[0], ((0, 5), (0, 0)))
    at_once = ("w_in", "w_tail", "w_sc", "w_fc")
    wx = _Gathering(dict(zip(at_once, (w_in_main, w_in_tail, taps8(w_sc_conv), taps8(w_ffn_conv)))),
                    {n: w[n][0] for n in ("w_out", "w_up", "w_down")}, me)
    token = x_b = _behind(x[0], wx.begin(None)).astype(BF16)
    for n in at_once:
        token = wx.forward(n, token)
    g_in, g_tail, g_sc, g_fc = (wx.get(n, token) for n in at_once)
    w_in_full = _carry_w_in(g_in, g_tail).reshape(ninp, d)
    w_sc_full = g_sc[:, :3].transpose(1, 0, 2).reshape(3, N_DEV * ws_sc)
    w_fc_full = g_fc[:, :3].transpose(1, 0, 2).reshape(3, N_DEV * ws_fc)

    xi, yi, ci = _place()
    names = ("loss",) + SMALL
    pieces = {}

    def gather_small(small, loss_t, after):
        pieces.update(small, loss=loss_t[0, :1])
        sizes = [_lanes(pieces[n].size) for n in names]
        (g_small,) = _all_gather([_behind(_pack([pieces[n] for n in names], sizes), after)], "gather_small")
        return _sum_slots(g_small, "sum_small", tr=g_small.shape[1])

    gx = _Reducing(jnp.reshape(ci, (1,)).astype(jnp.int32), 2 * xi + yi, gather_small)
    loss_t, grad_x, small, _ = _local_step(
        x[0], loss_target[0], w_in_full, b_gates, w_sc_full, mh_gain, None, ln1_g, ln1_b, None,
        w_fc_full, b_ffn_conv, None, ln2_g, ln2_b, gx=gx, wx=wx, x_b=x_b)

    grads, deltas, new_m, new_v = {}, {}, {}, {}
    for name in ("w_down", "w_up", "w_out"):
        grads[name], deltas[name], new_m[name], new_v[name] = _sum_adamw(
            gx.finish(name, gx.token), w[name][0], m[name][0], v[name][0], "adamw_" + name)

    summed = _unpack(gx.small_sum, [pieces[n].shape for n in names], [_lanes(pieces[n].size) for n in names])
    full = dict(zip(names, summed))
    full["w_sc_conv"] = lax.dynamic_slice(full["w_sc_conv"], (0, me * ws_sc), (3, ws_sc))
    full["w_ffn_conv"] = lax.dynamic_slice(full["w_ffn_conv"], (0, me * ws_fc), (3, ws_fc))
    for n in SMALL:
        grads[n] = full[n].reshape(w[n].shape)
    sizes = [_lanes(w[n].size) for n in SMALL]
    shapes = [w[n].shape for n in SMALL]
    packed = [_pack([t[n] for n in SMALL], sizes) for t in (w, grads, m, v)]
    small_out = _adamw(*packed, "adamw_small")
    for res, t in zip(small_out, (deltas, new_m, new_v)):
        t.update(zip(SMALL, _unpack(res, shapes, sizes)))

    done = sum(t[0:1, 0:1] for t in (grad_x, deltas["w_down"], deltas["w_up"], deltas["w_out"], small_out[0]))
    grads["w_in"], deltas["w_in"], new_m["w_in"], new_v["w_in"] = (
        jnp.transpose(a)[None] for a in _sum_adamw_shifted(gx.finish("w_in", done), w_in_t, m_in_t, v_in_t, "adamw_w_in"))

    big = lambda t: {n: (t[n].reshape(w[n].shape) if n in BIG else t[n]) for n in WEIGHTS}
    grads, deltas, new_m, new_v = big(grads), big(deltas), big(new_m), big(new_v)
    return (full["loss"].reshape(()), grad_x[None], *[grads[n] for n in WEIGHTS], *[deltas[n] for n in WEIGHTS],
            *[new_m[n] for n in WEIGHTS], *[new_v[n] for n in WEIGHTS])
```

```python
import functools

import jax
import jax.numpy as jnp
from jax import lax
from jax.experimental import pallas as pl
from jax.experimental.pallas import tpu as pltpu

F32 = jnp.float32
BF16 = jnp.bfloat16
MESH = pl.DeviceIdType.MESH

N_DEV = 8
NH = 4
CHUNK = 64
LN_EPS = 1e-5
HN_EPS = 1e-6
ALPHA = 2.0 ** 0.25
LANE = 128
IN_SLAB = 7 * LANE
IN_TAIL = 16
VMEM_LIMIT = 56 * 1024 * 1024
ADAM_LR, ADAM_B1, ADAM_B2, ADAM_EPS, ADAM_WD, ADAM_STEP = 0.001, 0.9, 0.999, 1e-08, 0.01, 10

_NN = (((1,), (0,)), ((), ()))
_NT = (((1,), (1,)), ((), ()))
_TN = (((0,), (0,)), ((), ()))


def _dot(a, b, dn=_NN):
    return lax.dot_general(a, b, dn, preferred_element_type=F32)


def _params(*sem):
    return pltpu.CompilerParams(dimension_semantics=sem if sem else None, vmem_limit_bytes=VMEM_LIMIT)


def _iota(shape, axis):
    return lax.broadcasted_iota(jnp.int32, shape, axis)


def _fit(n, want):
    if n <= want:
        return n
    t = want - want % LANE
    while n % t:
        t -= LANE
    return t


def _matmul(a, b, mode, out_dtype, name, tm=1024, tn=512, tk=1024, add=None, add_scale=1.0,
            a_blocked=False, b_blocked=False, o_width=None, after=None, n=None):
    a_parts = a if isinstance(a, tuple) else None
    b_parts = b if isinstance(b, tuple) else None
    if a_parts:
        a_blocked, (a_rows, wa), na = True, a[0].shape, len(a)
        kd, m = (a_rows, na * wa) if mode == "tn" else (na * wa, a_rows)
    elif a_blocked:
        na, a_rows, wa = a.shape
        kd, m = (a_rows, na * wa) if mode == "tn" else (na * wa, a_rows)
    elif mode == "tn":
        kd, m = a.shape
    else:
        m, kd = a.shape
    if b_parts:
        b_blocked, (rows, w), nb = True, b[0].shape, len(b)
    elif b_blocked:
        nb, rows, w = b.shape
    if b_blocked:
        n = rows if mode == "nt" else nb * w
        assert (nb * w if mode == "nt" else rows) == kd, (name, kd)
    else:
        n = n or (b.shape[0] if mode == "nt" else b.shape[1])
    tm, tn, tk = _fit(m, tm), _fit(n, tn), _fit(kd, tk)
    if a_blocked and mode == "tn":
        tm = _fit(wa, tm)
    if a_blocked and mode != "tn":
        tk = _fit(wa, tk)
    if b_blocked and mode != "nt":
        tn = _fit(w, tn)
    if b_blocked and mode == "nt":
        tk = _fit(w, tk)
    if o_width is not None:
        tn = _fit(o_width, tn)
    assert m % tm == 0 and n % tn == 0 and kd % tk == 0, (name, m, n, kd, tm, tn, tk)
    assert not (a_blocked and mode != "tn" and wa % tk) and not (b_blocked and mode == "nt" and w % tk), (name, tk)
    nk = kd // tk
    dn = {"nn": _NN, "nt": _NT, "tn": _TN}[mode]
    if a_blocked and mode == "tn":
        a_per = wa // tm
        a_spec = pl.BlockSpec((None, tk, tm), lambda i, j, k: (i // a_per, k, i % a_per))
    elif a_blocked:
        a_per = wa // tk
        a_spec = pl.BlockSpec((None, tm, tk), lambda i, j, k: (k // a_per, i, k % a_per))
    elif mode == "tn":
        a_spec = pl.BlockSpec((tk, tm), lambda i, j, k: (k, i))
    else:
        a_spec = pl.BlockSpec((tm, tk), lambda i, j, k: (i, k))
    if b_blocked and mode != "nt":
        per = w // tn
        b_spec = pl.BlockSpec((None, tk, tn), lambda i, j, k: (j // per, k, j % per))
    elif b_blocked:
        per = w // tk
        b_spec = pl.BlockSpec((None, tn, tk), lambda i, j, k: (k // per, j, k % per))
    elif mode == "nt":
        b_spec = pl.BlockSpec((tn, tk), lambda i, j, k: (j, k))
    else:
        b_spec = pl.BlockSpec((tk, tn), lambda i, j, k: (k, j))
    if o_width is None:
        o_spec = pl.BlockSpec((tm, tn), lambda i, j, k: (i, j))
        o_shape = (m, n)
    else:
        oper = o_width // tn
        o_spec = pl.BlockSpec((None, tm, tn), lambda i, j, k: (j // oper, i, j % oper))
        o_shape = (n // o_width, m, o_width)
    a_list, a_specs = [a], [a_spec]
    if a_parts:
        hold = lambda x, s: jnp.clip(x - s * a_per, 0, a_per - 1)
        a_list = list(a_parts)
        a_specs = [(pl.BlockSpec((tk, tm), lambda i, j, k, s=s: (k, hold(i, s))) if mode == "tn"
                    else pl.BlockSpec((tm, tk), lambda i, j, k, s=s: (i, hold(k, s)))) for s in range(na)]
    b_list, b_specs = [b], [b_spec]
    if b_parts:
        hold_b = lambda x, s: jnp.clip(x - s * per, 0, per - 1)
        b_list = list(b_parts)
        b_specs = [(pl.BlockSpec((tn, tk), lambda i, j, k, s=s: (j, hold_b(k, s))) if mode == "nt"
                    else pl.BlockSpec((tk, tn), lambda i, j, k, s=s: (k, hold_b(j, s)))) for s in range(nb)]
    n_a, n_b = len(a_list), len(b_list)
    has_add = add is not None
    n_in = n_a + n_b + has_add + (after is not None)
    in_place = nk > 1 and out_dtype == F32

    def body(*refs):
        add_ref = refs[n_a + n_b] if has_add else None
        o_ref = refs[n_in]
        i, j, k = pl.program_id(0), pl.program_id(1), pl.program_id(2)

        def finish(r):
            if has_add:
                r = r + add_scale * add_ref[...]
            o_ref[...] = r.astype(out_dtype)

        def step(a_ref, b_ref):
            if nk == 1:
                finish(_dot(a_ref[...], b_ref[...], dn))
                return
            acc = o_ref if in_place else refs[-1]

            @pl.when(k == 0)
            def _():
                acc[...] = _dot(a_ref[...], b_ref[...], dn)

            @pl.when(k > 0)
            def _():
                acc[...] += _dot(a_ref[...], b_ref[...], dn)

        if n_a == 1 and n_b == 1:
            step(refs[0], refs[1])
        else:
            slab_a = ((i if mode == "tn" else k) // a_per) if n_a > 1 else 0
            slab_b = ((k if mode == "nt" else j) // per) if n_b > 1 else 0
            for sa in range(n_a):
                for sb in range(n_b):
                    pl.when((slab_a == sa) & (slab_b == sb))(functools.partial(step, refs[sa], refs[n_a + sb]))
        if nk > 1 and not (in_place and not has_add):
            @pl.when(k == nk - 1)
            def _():
                finish((o_ref if in_place else refs[-1])[...])

    in_specs = a_specs + b_specs + ([pl.BlockSpec((tm, tn), lambda i, j, k: (i, j))] if has_add else [])
    args = (*a_list, *b_list) + ((add,) if has_add else ())
    if after is not None:
        in_specs.append(pl.BlockSpec(memory_space=pl.ANY))
        args += (after,)
    return pl.pallas_call(
        body, name=name, grid=(m // tm, n // tn, nk),
        in_specs=in_specs, out_specs=o_spec,
        out_shape=jax.ShapeDtypeStruct(o_shape, out_dtype),
        scratch_shapes=[pltpu.VMEM((tm, tn), F32)] if nk > 1 and not in_place else [],
        compiler_params=_params("parallel", "parallel", "arbitrary"),
    )(*args)


def _shift_down(u, s):
    return jnp.where(_iota(u.shape, 0) >= s, pltpu.roll(u, s, 0), 0.0)


def _shift_up(u, s):
    t = u.shape[0]
    return jnp.where(_iota(u.shape, 0) < t - s, pltpu.roll(u, t - s, 0), 0.0)


SLAB = 8


def _rolled(u):
    return pltpu.roll(u, 2, 0), pltpu.roll(u, 1, 0)


def _conv(u, w, rolled=None):
    u2, u1 = _rolled(u) if rolled is None else rolled
    raw = w[0:1] * u2 + w[1:2] * u1 + w[2:3] * u
    head = u[0:SLAB]
    mended = w[0:1] * _shift_down(head, 2) + w[1:2] * _shift_down(head, 1) + w[2:3] * head
    return jnp.concatenate([mended, raw[SLAB:]], axis=0)


def _conv_t(dy, w):
    t = dy.shape[0]
    raw = w[2:3] * dy + w[1:2] * pltpu.roll(dy, t - 1, 0) + w[0:1] * pltpu.roll(dy, t - 2, 0)
    tail = dy[t - SLAB:]
    mended = w[2:3] * tail + w[1:2] * _shift_up(tail, 1) + w[0:1] * _shift_up(tail, 2)
    return jnp.concatenate([raw[:t - SLAB], mended], axis=0)


def _conv_dw(dy, u, rolled=None):
    t = dy.shape[0]
    u2, u1 = _rolled(u) if rolled is None else rolled
    head, tail = dy[0:SLAB], u[t - SLAB:]
    r = _iota(head.shape, 0)
    wrap2 = jnp.sum(jnp.where(r < 2, head * pltpu.roll(tail, 2, 0), 0.0), axis=0, keepdims=True)
    wrap1 = jnp.sum(jnp.where(r < 1, head * pltpu.roll(tail, 1, 0), 0.0), axis=0, keepdims=True)
    d0 = jnp.sum(dy * u2, axis=0, keepdims=True) - wrap2
    d1 = jnp.sum(dy * u1, axis=0, keepdims=True) - wrap1
    d2 = jnp.sum(dy * u, axis=0, keepdims=True)
    r3 = _iota((3, dy.shape[1]), 0)
    return jnp.where(r3 == 0, d0, jnp.where(r3 == 1, d1, d2))


def _sigmoid(x):
    return 0.5 * jnp.tanh(0.5 * x) + 0.5


def _sconv_fwd(proj, w_sc, t, wc):
    nb = wc // LANE

    def body(cb_ref, cc_ref, ch_ref, w_ref, y_ref):
        u = cc_ref[...] * ch_ref[...]
        y_ref[...] = (cb_ref[...] * _conv(u, w_ref[...])).astype(BF16)

    col = lambda off: pl.BlockSpec((t, LANE), lambda j: (0, j + off))
    return pl.pallas_call(
        body, name="sconv_fwd", grid=(nb,),
        in_specs=[col(0), col(nb), col(2 * nb), pl.BlockSpec((3, LANE), lambda j: (0, j))],
        out_specs=pl.BlockSpec((None, t, LANE), lambda j: (0, 0, j)),
        out_shape=jax.ShapeDtypeStruct((2, t, wc), BF16),
        compiler_params=_params("parallel"),
    )(proj, proj, proj, w_sc)


def _sconv_bwd(dy, proj, w_sc, t, wc):
    nb = wc // LANE

    def body(dy_ref, cb_ref, cc_ref, ch_ref, w_ref, dcb_ref, dcc_ref, dch_ref, dw_ref):
        cc, ch, w, d = cc_ref[...], ch_ref[...], w_ref[...], dy_ref[...]
        u = cc * ch
        ru = _rolled(u)
        dcb_ref[...] = (d * _conv(u, w, ru)).astype(BF16)
        dcu = d * cb_ref[...]
        dw_ref[...] = _conv_dw(dcu, u, ru)
        du = _conv_t(dcu, w)
        dcc_ref[...] = (du * ch).astype(BF16)
        dch_ref[...] = (du * cc).astype(BF16)

    col = lambda off: pl.BlockSpec((t, LANE), lambda j: (0, j + off))
    act = jax.ShapeDtypeStruct((t, wc), BF16)
    return pl.pallas_call(
        body, name="sconv_bwd", grid=(nb,),
        in_specs=[col(0), col(0), col(nb), col(2 * nb), pl.BlockSpec((3, LANE), lambda j: (0, j))],
        out_specs=[col(0), col(0), col(0), pl.BlockSpec((3, LANE), lambda j: (0, j))],
        out_shape=[act, act, act, jax.ShapeDtypeStruct((3, wc), F32)],
        compiler_params=_params("parallel"),
    )(dy, proj, proj, proj, w_sc)


def _gates_prep(proj, bias_tile, t, gate_tile):
    def body(g_ref, b_ref, o_ref):
        g = g_ref[...] + b_ref[...]
        lane = _iota(g.shape, 1)
        is_f = (lane >= NH) & (lane < 2 * NH)
        lf = jnp.minimum(g, 0.0) - jnp.log(1.0 + jnp.exp(-jnp.abs(g)))
        c = jnp.where(is_f, lf, 0.0)
        r = _iota(g.shape, 0) % CHUNK
        s = 1
        while s < CHUNK:
            c = c + jnp.where(r >= s, pltpu.roll(c, s, 0), 0.0)
            s *= 2
        o_ref[...] = jnp.where(is_f, c, jnp.where(lane < NH, g, 0.0))

    return pl.pallas_call(
        body, name="gates_prep", grid=(1,),
        in_specs=[pl.BlockSpec((t, LANE), lambda i: (0, gate_tile)), pl.BlockSpec((1, LANE), lambda i: (0, 0))],
        out_specs=pl.BlockSpec((t, LANE), lambda i: (0, 0)),
        out_shape=jax.ShapeDtypeStruct((t, LANE), F32),
        compiler_params=_params("arbitrary"),
    )(proj, bias_tile)


def _gates_bwd(dgate, proj, bias_tile, t, gate_tile):
    def body(dg_ref, g_ref, b_ref, o_ref, s_ref):
        g = g_ref[...] + b_ref[...]
        lane = _iota(g.shape, 1)
        r = _iota(g.shape, 0) % CHUNK
        dsig = 1.0 - _sigmoid(g)
        out = jnp.zeros(g.shape, F32)
        for h in range(NH):
            d = dg_ref[h]
            c = d
            s = 1
            while s < CHUNK:
                c = c + jnp.where(r + s < CHUNK, pltpu.roll(c, t - s, 0), 0.0)
                s *= 2
            di = jnp.broadcast_to(d[:, 0:1], g.shape)
            db = jnp.broadcast_to(c[:, 1:2], g.shape)
            out = out + jnp.where(lane == h, di, 0.0) + jnp.where(lane == NH + h, db * dsig, 0.0)
        o_ref[...] = out.astype(BF16)
        s_ref[...] = jnp.sum(out, axis=0, keepdims=True)

    return pl.pallas_call(
        body, name="gates_bwd", grid=(1,),
        in_specs=[pl.BlockSpec((NH, t, LANE), lambda i: (0, 0, 0)),
                  pl.BlockSpec((t, LANE), lambda i: (0, gate_tile)), pl.BlockSpec((1, LANE), lambda i: (0, 0))],
        out_specs=[pl.BlockSpec((t, LANE), lambda i: (0, 0)), pl.BlockSpec((1, LANE), lambda i: (0, 0))],
        out_shape=[jax.ShapeDtypeStruct((t, LANE), BF16), jax.ShapeDtypeStruct((1, LANE), F32)],
        compiler_params=_params("arbitrary"),
    )(dgate, proj, bias_tile)


def _in_turn(heads):
    while heads:
        heads = [g for g in heads if next(g, heads) is not heads]


def _chunk_gates(gc, gr, h, mprev):
    L = CHUNK
    icol, bcol = gc[:, h:h + 1], gc[:, h + NH:h + NH + 1]
    irow, brow = gr[h:h + 1, :], gr[h + NH:h + NH + 1, :]
    tri = _iota((L, L), 0) >= _iota((L, L), 1)
    log_d = jnp.where(tri, bcol - brow + irow, -jnp.inf)
    inter = bcol + mprev
    mt = jnp.maximum(inter, jnp.max(log_d, axis=1, keepdims=True))
    dw = jnp.exp(log_d - mt)
    iw = jnp.exp(inter - mt)
    g = brow[:, L - 1:L]
    wlog_col = g - bcol + icol
    wlog_row = g - brow + irow
    mnew = jnp.maximum(g + mprev, jnp.max(wlog_row, axis=1, keepdims=True))
    wcol = jnp.exp(wlog_col - mnew)
    decay = jnp.exp(g + mprev - mnew)
    return dw, iw, mt, wcol, decay, mnew


def _mlstm_fwd(proj, gcol, grow, t, wc, dh):
    nc = t // CHUNK
    wm = NH * dh
    assert wc == wm, (wc, wm)
    qoff = 3 * wc // wm
    scale = dh ** -0.5

    def body(q_ref, k_ref, v_ref, gc_ref, gr_ref, h_ref, cs_ref, ns_ref, c_s, n_s, m_s):
        @pl.when(pl.program_id(0) == 0)
        def _():
            c_s[...] = jnp.zeros_like(c_s)
            n_s[...] = jnp.zeros_like(n_s)
            m_s[...] = jnp.zeros_like(m_s)

        gc, gr = gc_ref[...], gr_ref[0]
        done = [None] * NH

        def head(h):
            cols = slice(h * dh, (h + 1) * dh)
            mprev = m_s[h, 0:1, 0:1]
            cprev = c_s[h]
            n8 = n_s[h]
            nprev = n8[0:1]
            qs = q_ref[:, cols] * scale
            k = k_ref[:, cols]
            qs_b, k_b, v_b = qs.astype(BF16), k.astype(BF16), v_ref[:, cols].astype(BF16)
            qk = _dot(qs_b, k_b, _NT)
            yield
            q_c = _dot(qs_b, cprev.astype(BF16))
            yield
            dw, iw, mt, wcol, decay, mnew = _chunk_gates(gc, gr, h, mprev)
            yield
            s = qk * dw
            wk = wcol * k
            num = _dot(s.astype(BF16), v_b) + iw * q_c
            yield
            c_new = decay * cprev + _dot(wk.astype(BF16), v_b, _TN)
            yield
            den = jnp.sum(s, axis=1, keepdims=True) + iw * jnp.sum(qs * nprev, axis=1, keepdims=True)
            done[h] = (cprev, jnp.where(_iota(n8.shape, 0) == 1, mprev, n8),
                       num / jnp.maximum(jnp.abs(den), jnp.exp(-mt)), c_new,
                       decay * n8 + jnp.sum(wk, axis=0, keepdims=True), mnew)

        _in_turn([head(h) for h in range(NH)])
        for h, (c_old, n_old, h_out, c_new, n_new, m_new) in enumerate(done):
            cs_ref[h] = c_old
            ns_ref[h] = n_old
            h_ref[:, h * dh:(h + 1) * dh] = h_out
            c_s[h] = c_new
            n_s[h] = n_new
            m_s[h] = jnp.broadcast_to(m_new, m_s.shape[1:])

    grp = lambda off: pl.BlockSpec((CHUNK, wm), lambda c: (c, qoff + off))
    return pl.pallas_call(
        body, name="mlstm_fwd", grid=(nc,),
        in_specs=[grp(0), grp(1), grp(2),
                  pl.BlockSpec((CHUNK, LANE), lambda c: (c, 0)),
                  pl.BlockSpec((1, 8, CHUNK), lambda c: (c, 0, 0))],
        out_specs=[pl.BlockSpec((CHUNK, wm), lambda c: (c, 0)),
                   pl.BlockSpec((NH, None, dh, dh), lambda c: (0, c, 0, 0)),
                   pl.BlockSpec((NH, None, 8, dh), lambda c: (0, c, 0, 0))],
        out_shape=[jax.ShapeDtypeStruct((t, wm), F32),
                   jax.ShapeDtypeStruct((NH, nc, dh, dh), F32),
                   jax.ShapeDtypeStruct((NH, nc, 8, dh), F32)],
        scratch_shapes=[pltpu.VMEM((NH, dh, dh), F32), pltpu.VMEM((NH, 8, dh), F32), pltpu.VMEM((NH, 8, LANE), F32)],
        compiler_params=_params("arbitrary"),
    )(proj, proj, proj, gcol, grow)


def _mlstm_bwd(proj, gcol, grow, hval, dh_in, cs, ns, t, wc, dh):
    nc = t // CHUNK
    wm = NH * dh
    assert wc == wm, (wc, wm)
    qoff = 3 * wc // wm
    scale = dh ** -0.5
    L = CHUNK

    def body(q_ref, k_ref, v_ref, gc_ref, gr_ref, h_ref, dh_ref, cs_ref, ns_ref,
             dq_ref, dk_ref, dv_ref, dg_ref, dc_s, dn_s):
        @pl.when(pl.program_id(0) == 0)
        def _():
            dc_s[...] = jnp.zeros_like(dc_s)
            dn_s[...] = jnp.zeros_like(dn_s)

        gc, gr = gc_ref[...], gr_ref[0]
        eye = _iota((L, L), 0) == _iota((L, L), 1)
        lane = _iota((L, LANE), 1)
        last = _iota((L, 1), 0) == L - 1
        done = [None] * NH

        def head(h):
            cols = slice(h * dh, (h + 1) * dh)
            ns8 = ns_ref[h]
            nprev = ns8[0:1]
            mprev = ns8[1:2, 0:1]
            cprev = cs_ref[h]
            dcn = dc_s[h]
            dn8 = dn_s[h]
            dnn = dn8[0:1]

            qs = q_ref[:, cols] * scale
            k = k_ref[:, cols]
            qs_b, k_b, v_b = qs.astype(BF16), k.astype(BF16), v_ref[:, cols].astype(BF16)
            qk = _dot(qs_b, k_b, _NT)
            yield
            dw, iw, mt, wcol, decay, _ = _chunk_gates(gc, gr, h, mprev)
            yield
            s = qk * dw
            den = jnp.sum(s, axis=1, keepdims=True) + iw * jnp.sum(qs * nprev, axis=1, keepdims=True)
            emt = jnp.exp(-mt)
            r = 1.0 / jnp.maximum(jnp.abs(den), emt)
            dout = dh_ref[:, cols]
            dnum = dout * r
            dden = (-jnp.sum(dout * h_ref[:, cols], axis=1, keepdims=True) * r
                    * jnp.where(jnp.abs(den) > emt, jnp.sign(den), 0.0))
            dnum_b = dnum.astype(BF16)
            cprev_b = cprev.astype(BF16)
            dcn_b = dcn.astype(BF16)
            yield

            g_raw = _dot(dnum_b, v_b, _NT)
            yield
            q_inter = _dot(dnum_b, cprev_b, _NT)
            yield
            k_raw = _dot(v_b, dcn_b, _NT)
            yield
            gd = (g_raw + dden) * dw
            gd_b = gd.astype(BF16)
            dqs_inter = iw * (q_inter + dden * nprev)
            dk_inter = wcol * (k_raw + dnn)
            wk = wcol * k
            iq = iw * qs
            dqs = _dot(gd_b, k_b) + dqs_inter
            yield
            dk = _dot(gd_b, qs_b, _TN) + dk_inter
            yield
            dv = _dot(s.astype(BF16), dnum_b, _TN) + _dot(wk.astype(BF16), dcn_b)
            yield
            dc_new = decay * dcn + _dot(iq.astype(BF16), dnum_b, _TN)
            yield

            e = gd * qk
            e_cols = jnp.sum(jnp.where(eye, jnp.sum(e, axis=0, keepdims=True), 0.0), axis=1, keepdims=True)
            yield
            k_inter = jnp.sum(k * dk_inter, axis=1, keepdims=True)
            rq = jnp.sum(e, axis=1, keepdims=True) + jnp.sum(qs * dqs_inter, axis=1, keepdims=True)
            rk = e_cols + k_inter
            hsum = jnp.sum(k_inter, axis=0, keepdims=True)
            jdec = decay * (jnp.sum(jnp.sum(dcn * cprev, axis=1, keepdims=True), axis=0, keepdims=True)
                            + jnp.sum(dnn * nprev, axis=1, keepdims=True))
            db = rq - rk + jnp.where(last, hsum + jdec, 0.0)
            done[h] = (jnp.where(lane == 0, rk, jnp.where(lane == 1, db, 0.0)),
                       (dqs * scale).astype(BF16), dk.astype(BF16), dv.astype(BF16), dc_new,
                       decay * dn8 + jnp.sum(iq * dden, axis=0, keepdims=True))

        _in_turn([head(h) for h in range(NH)])
        for h, (dgate, dq, dk, dv, dc_new, dn_new) in enumerate(done):
            cols = slice(h * dh, (h + 1) * dh)
            dg_ref[h] = dgate
            dq_ref[:, cols] = dq
            dk_ref[:, cols] = dk
            dv_ref[:, cols] = dv
            dc_s[h] = dc_new
            dn_s[h] = dn_new

    rc = lambda c: nc - 1 - c
    grp = lambda off: pl.BlockSpec((L, wm), lambda c: (rc(c), qoff + off))
    hm = pl.BlockSpec((L, wm), lambda c: (rc(c), 0))
    act = jax.ShapeDtypeStruct((t, wm), BF16)
    return pl.pallas_call(
        body, name="mlstm_bwd", grid=(nc,),
        in_specs=[grp(0), grp(1), grp(2),
                  pl.BlockSpec((L, LANE), lambda c: (rc(c), 0)),
                  pl.BlockSpec((1, 8, L), lambda c: (rc(c), 0, 0)),
                  hm, hm,
                  pl.BlockSpec((NH, None, dh, dh), lambda c: (0, rc(c), 0, 0)),
                  pl.BlockSpec((NH, None, 8, dh), lambda c: (0, rc(c), 0, 0))],
        out_specs=[hm, hm, hm, pl.BlockSpec((NH, L, LANE), lambda c: (0, rc(c), 0))],
        out_shape=[act, act, act, jax.ShapeDtypeStruct((NH, t, LANE), F32)],
        scratch_shapes=[pltpu.VMEM((NH, dh, dh), F32), pltpu.VMEM((NH, 8, dh), F32)],
        compiler_params=_params("arbitrary"),
    )(proj, proj, proj, gcol, grow, hval, dh_in, cs, ns)


def _head_norm(hv):
    mu = jnp.mean(hv, axis=1, keepdims=True)
    hc = hv - mu
    rstd = lax.rsqrt(jnp.mean(hc * hc, axis=1, keepdims=True) + HN_EPS)
    return hc * rstd, rstd


def _hnorm_fwd(hval, proj, gain, y, t, wc, dh, tr=512):
    ooff = 3 * wc // dh + 3 * NH
    tr = min(tr, t)

    def body(h_ref, o_ref, g_ref, y_in, y_ref):
        hhat, _ = _head_norm(h_ref[...])
        y_ref[...] = (_sigmoid(o_ref[...]) * hhat * g_ref[...]).astype(BF16)

    return pl.pallas_call(
        body, name="hnorm_fwd", grid=(t // tr, NH),
        in_specs=[pl.BlockSpec((tr, dh), lambda i, h: (i, h)),
                  pl.BlockSpec((tr, dh), lambda i, h: (i, ooff + h)),
                  pl.BlockSpec((1, dh), lambda i, h: (0, h)),
                  pl.BlockSpec(memory_space=pl.ANY)],
        out_specs=pl.BlockSpec((None, tr, dh), lambda i, h: (1, i, h)),
        out_shape=jax.ShapeDtypeStruct(y.shape, BF16),
        input_output_aliases={3: 0},
        compiler_params=_params("parallel", "parallel"),
    )(hval, proj, gain, y)


def _hnorm_bwd(dy, hval, proj, gain, t, wc, dh, tr=512):
    ooff = 3 * wc // dh + 3 * NH
    tr = min(tr, t)
    yoff = wc // dh

    def body(dy_ref, h_ref, o_ref, g_ref, do_ref, dh_ref, dg_ref):
        i = pl.program_id(1)
        hhat, rstd = _head_norm(h_ref[...])
        gain_v = g_ref[...]
        sig = _sigmoid(o_ref[...])
        d = dy_ref[...]
        do_ref[...] = (d * hhat * gain_v * sig * (1.0 - sig)).astype(BF16)
        dhn = d * sig
        part = jnp.sum(dhn * hhat, axis=0, keepdims=True)

        @pl.when(i == 0)
        def _():
            dg_ref[...] = part

        @pl.when(i > 0)
        def _():
            dg_ref[...] += part

        dhat = dhn * gain_v
        dh_ref[...] = rstd * (dhat - jnp.mean(dhat, axis=1, keepdims=True)
                              - hhat * jnp.mean(dhat * hhat, axis=1, keepdims=True))

    blk = lambda off: pl.BlockSpec((tr, dh), lambda h, i: (i, off + h))
    return pl.pallas_call(
        body, name="hnorm_bwd", grid=(NH, t // tr),
        in_specs=[blk(yoff), blk(0), blk(ooff), pl.BlockSpec((1, dh), lambda h, i: (0, h))],
        out_specs=[blk(0), blk(0), pl.BlockSpec((1, dh), lambda h, i: (0, h))],
        out_shape=[jax.ShapeDtypeStruct((t, NH * dh), BF16), jax.ShapeDtypeStruct((t, NH * dh), F32),
                   jax.ShapeDtypeStruct((1, NH * dh), F32)],
        compiler_params=_params("parallel", "arbitrary"),
    )(dy, hval, proj, gain)


def _ln_stats(z):
    mu = jnp.mean(z, axis=1, keepdims=True)
    zc = z - mu
    rstd = lax.rsqrt(jnp.mean(zc * zc, axis=1, keepdims=True) + LN_EPS)
    return zc * rstd, rstd


def _ln_bwd(dy, xhat, rstd, g):
    dxh = dy * g
    return rstd * (dxh - jnp.mean(dxh, axis=1, keepdims=True) - xhat * jnp.mean(dxh * xhat, axis=1, keepdims=True))


def _accum(ref, i, part):
    @pl.when(i == 0)
    def _():
        ref[...] = part

    @pl.when(i > 0)
    def _():
        ref[...] += part


def _ln1_fwd(x, mix, g, b, tr=256):
    t, d = x.shape

    def body(x_ref, m_ref, g_ref, b_ref, xh_ref, rs_ref, xb_ref):
        xhat, rstd = _ln_stats(ALPHA * x_ref[...] + m_ref[...])
        xh_ref[...] = xhat
        rs_ref[...] = rstd
        xb_ref[...] = (xhat * g_ref[...] + b_ref[...]).astype(BF16)

    row = pl.BlockSpec((tr, d), lambda i: (i, 0))
    vec = pl.BlockSpec((1, d), lambda i: (0, 0))
    return pl.pallas_call(
        body, name="ln1_fwd", grid=(t // tr,),
        in_specs=[row, row, vec, vec],
        out_specs=[row, pl.BlockSpec((tr, 1), lambda i: (i, 0)), row],
        out_shape=[jax.ShapeDtypeStruct((t, d), F32), jax.ShapeDtypeStruct((t, 1), F32),
                   jax.ShapeDtypeStruct((t, d), BF16)],
        compiler_params=_params("parallel"),
    )(x, mix, g, b)


def _ln2_loss(xhat1, g1, b1, ff, target, g2, b2, tr=256):
    t, d = ff.shape

    def body(xh_ref, g1_ref, b1_ref, f_ref, t_ref, g_ref, b_ref, dz_ref, dzb_ref, dg_ref, db_ref, l_ref):
        i = pl.program_id(0)
        x1 = xh_ref[...] * g1_ref[...] + b1_ref[...]
        xhat, rstd = _ln_stats(ALPHA * x1 + f_ref[...])
        gv = g_ref[...]
        e = xhat * gv + b_ref[...] - t_ref[...]
        lsum = jnp.sum(jnp.sum(e * e, axis=1, keepdims=True), axis=0, keepdims=True) * (0.5 / d)
        dy = e * (1.0 / d)
        _accum(dg_ref, i, jnp.sum(dy * xhat, axis=0, keepdims=True))
        _accum(db_ref, i, jnp.sum(dy, axis=0, keepdims=True))
        _accum(l_ref, i, jnp.broadcast_to(lsum, l_ref.shape))
        dz = _ln_bwd(dy, xhat, rstd, gv)
        dz_ref[...] = dz
        dzb_ref[...] = dz.astype(BF16)

    row = pl.BlockSpec((tr, d), lambda i: (i, 0))
    vec = pl.BlockSpec((1, d), lambda i: (0, 0))
    return pl.pallas_call(
        body, name="ln2_loss", grid=(t // tr,),
        in_specs=[row, vec, vec, row, row, vec, vec],
        out_specs=[row, row, vec, vec, pl.BlockSpec((8, LANE), lambda i: (0, 0))],
        out_shape=[jax.ShapeDtypeStruct((t, d), F32), jax.ShapeDtypeStruct((t, d), BF16),
                   jax.ShapeDtypeStruct((1, d), F32), jax.ShapeDtypeStruct((1, d), F32),
                   jax.ShapeDtypeStruct((8, LANE), F32)],
        compiler_params=_params("arbitrary"),
    )(xhat1, g1, b1, ff, target, g2, b2)


def _ln1_bwd(dz2, dffn, xhat1, rstd1, g1, tr=256):
    t, d = dz2.shape

    def body(a_ref, f_ref, xh_ref, rs_ref, g_ref, dz_ref, dzb_ref, dg_ref, db_ref):
        i = pl.program_id(0)
        dy = ALPHA * a_ref[...] + f_ref[...]
        xhat = xh_ref[...]
        _accum(dg_ref, i, jnp.sum(dy * xhat, axis=0, keepdims=True))
        _accum(db_ref, i, jnp.sum(dy, axis=0, keepdims=True))
        dz = _ln_bwd(dy, xhat, rs_ref[...], g_ref[...])
        dz_ref[...] = dz
        dzb_ref[...] = dz.astype(BF16)

    row = pl.BlockSpec((tr, d), lambda i: (i, 0))
    vec = pl.BlockSpec((1, d), lambda i: (0, 0))
    return pl.pallas_call(
        body, name="ln1_bwd", grid=(t // tr,),
        in_specs=[row, row, row, pl.BlockSpec((tr, 1), lambda i: (i, 0)), vec],
        out_specs=[row, row, vec, vec],
        out_shape=[jax.ShapeDtypeStruct((t, d), F32), jax.ShapeDtypeStruct((t, d), BF16),
                   jax.ShapeDtypeStruct((1, d), F32), jax.ShapeDtypeStruct((1, d), F32)],
        compiler_params=_params("arbitrary"),
    )(dz2, dffn, xhat1, rstd1, g1)


def _ffn_act_fwd(hid0, w_fc, b_fc, t, dff):
    nb = dff // LANE

    def body(hv_ref, hg_ref, wv_ref, wg_ref, bv_ref, bg_ref, a_ref):
        val = _conv(hv_ref[...], wv_ref[...]) + bv_ref[...]
        gate = _conv(hg_ref[...], wg_ref[...]) + bg_ref[...]
        a_ref[...] = (gate * _sigmoid(gate) * val).astype(BF16)

    col = lambda off: pl.BlockSpec((t, LANE), lambda j: (0, j + off))
    w3 = lambda off: pl.BlockSpec((3, LANE), lambda j: (0, j + off))
    w1 = lambda off: pl.BlockSpec((1, LANE), lambda j: (0, j + off))
    return pl.pallas_call(
        body, name="ffn_act_fwd", grid=(nb,),
        in_specs=[col(0), col(nb), w3(0), w3(nb), w1(0), w1(nb)],
        out_specs=col(0),
        out_shape=jax.ShapeDtypeStruct((t, dff), BF16),
        compiler_params=_params("parallel"),
    )(hid0, hid0, w_fc, w_fc, b_fc, b_fc)


def _ffn_act_bwd(da, hid0, w_fc, b_fc, t, dff):
    nb = dff // LANE

    def body(da_ref, hv_ref, hg_ref, wv_ref, wg_ref, bv_ref, bg_ref,
             dhv_ref, dhg_ref, dwv_ref, dwg_ref, dbv_ref, dbg_ref):
        hv, hg, wv, wg = hv_ref[...], hg_ref[...], wv_ref[...], wg_ref[...]
        rv, rg = _rolled(hv), _rolled(hg)
        val = _conv(hv, wv, rv) + bv_ref[...]
        gate = _conv(hg, wg, rg) + bg_ref[...]
        sig = _sigmoid(gate)
        d = da_ref[...]
        dsig = d * sig
        dval = dsig * gate
        dgate = dsig * val * (1.0 + gate * (1.0 - sig))
        dhv_ref[...] = _conv_t(dval, wv).astype(BF16)
        dhg_ref[...] = _conv_t(dgate, wg).astype(BF16)
        dwv_ref[...] = _conv_dw(dval, hv, rv)
        dwg_ref[...] = _conv_dw(dgate, hg, rg)
        dbv_ref[...] = jnp.sum(dval, axis=0, keepdims=True)
        dbg_ref[...] = jnp.sum(dgate, axis=0, keepdims=True)

    col = lambda off: pl.BlockSpec((t, LANE), lambda j: (0, j + off))
    w3 = lambda off: pl.BlockSpec((3, LANE), lambda j: (0, j + off))
    w1 = lambda off: pl.BlockSpec((1, LANE), lambda j: (0, j + off))
    s3 = jax.ShapeDtypeStruct((3, dff), F32)
    s1 = jax.ShapeDtypeStruct((1, dff), F32)
    return pl.pallas_call(
        body, name="ffn_act_bwd", grid=(nb,),
        in_specs=[col(0), col(0), col(nb), w3(0), w3(nb), w1(0), w1(nb)],
        out_specs=[col(0), col(0), w3(0), w3(0), w1(0), w1(0)],
        out_shape=[jax.ShapeDtypeStruct((t, dff), BF16)] * 2 + [s3, s3, s1, s1],
        compiler_params=_params("parallel"),
    )(da, hid0, hid0, w_fc, w_fc, b_fc, b_fc)


class _Ready:
    def __init__(self, **weights):
        self.weights = weights

    def begin(self, after):
        return None

    def forward(self, name, after):
        return None

    def get(self, name, after):
        return self.weights[name]


class _Kept:
    def __init__(self):
        self.grads = {}

    def start(self, name, grad):
        self.grads[name] = grad
        return None

    def relay(self, name, after):
        return None

    def meanwhile(self, small, loss, after):
        return None


def _behind(a, token):
    return a if token is None else a + token[0:1, 0:1].reshape((1,) * a.ndim)


def _local_step(x, target, w_in, b_gates, w_sc, gain, w_out, ln1_g, ln1_b, w_up, w_fc, b_fc, w_down, ln2_g, ln2_b,
                gx=None, wx=None, x_b=None):
    t, d = x.shape
    wc = d // 2
    dh = (d - wc) // NH
    wm = NH * dh
    dff = w_fc.shape[1] // 2
    if wx is None:
        wx = _Ready(w_out=w_out, w_up=w_up, w_down=w_down)
    ninp = 3 * wc + 4 * wm + LANE
    nin = 3 * wc + 4 * wm
    gate_tile = nin // LANE
    nc = t // CHUNK
    bias_tile = jnp.pad(b_gates, ((0, 0), (0, LANE - 2 * NH)))

    if x_b is None:
        x_b = x.astype(BF16)
    proj = _matmul(x_b, w_in, "nt", F32, "proj", tm=512, tn=2432, tk=d, n=ninp, after=wx.begin(w_in))
    y = _sconv_fwd(proj, w_sc, t, wc)
    gcol = _gates_prep(proj, bias_tile, t, gate_tile)
    grow = gcol[:, :8].T.reshape(8, nc, CHUNK).transpose(1, 0, 2)
    hval, cs, ns = _mlstm_fwd(proj, gcol, grow, t, wc, dh)
    y = _hnorm_fwd(hval, proj, gain, y, t, wc, dh)
    tok = wx.forward("w_out", y)
    w_out = wx.get("w_out", tok)
    mix = _matmul(y, w_out, "nn", F32, "out_proj", tm=512, tn=1024, tk=wc, a_blocked=True, after=tok)
    xhat1, rstd1, x1_b = _ln1_fwd(x, mix, _behind(ln1_g, wx.forward("w_up", mix)), ln1_b)
    w_up = wx.get("w_up", x1_b)
    wsl = 2 * dff // N_DEV
    hid0 = _matmul(x1_b, w_up, "nn", F32, "ffn_up", tm=1024, tn=1024, tk=d)
    act = _ffn_act_fwd(hid0, w_fc, _behind(b_fc, wx.forward("w_down", hid0)), t, dff)
    w_down = wx.get("w_down", act)
    ff = _matmul(act, w_down, "nn", F32, "ffn_down", tm=1024, tn=512, tk=dff)
    dz2, dz2_b, d_ln2_g, d_ln2_b, loss = _ln2_loss(xhat1, ln1_g, ln1_b, ff, target, ln2_g, ln2_b)

    if gx is None:
        gx = _Kept()
    d_w_down = _matmul(act, dz2_b, "tn", BF16, "ffn_down_dw", tm=512, tn=1024, tk=t)
    d_act = _matmul(dz2_b, w_down, "nt", F32, "ffn_down_dx", tm=1024, tn=512, tk=d, after=gx.start("w_down", d_w_down))
    *d_hid0, dwv, dwg, dbv, dbg = _ffn_act_bwd(d_act, hid0, w_fc, _behind(b_fc, gx.relay("w_down", d_act)), t, dff)
    d_w_fc = jnp.concatenate([dwv, dwg], axis=1)
    d_b_fc = jnp.concatenate([dbv, dbg], axis=1)
    d_hid0 = tuple(d_hid0[:2])
    d_w_up = _matmul(x1_b, d_hid0, "tn", BF16, "ffn_up_dw", tm=512, tn=wsl, tk=t, o_width=wsl)
    d_x1_ffn = _matmul(d_hid0, w_up, "nt", F32, "ffn_up_dx", tm=512, tn=512, tk=dff,
                       after=gx.start("w_up", d_w_up))
    dz1, dz1_b, d_ln1_g, d_ln1_b = _ln1_bwd(dz2, d_x1_ffn, xhat1, rstd1, _behind(ln1_g, gx.relay("w_up", d_x1_ffn)))

    d_w_out = _matmul(y, dz1_b, "tn", BF16, "out_proj_dw", tm=512, tn=1024, tk=t, a_blocked=True)
    dy = _matmul(dz1_b, w_out, "nt", F32, "out_proj_dx", tm=512, tn=1024, tk=d, after=gx.start("w_out", d_w_out))
    dcb, dcc, dch, d_w_sc = _sconv_bwd(dy, proj, _behind(w_sc, gx.relay("w_out", dy)), t, wc)
    d_o, d_hval, d_gain = _hnorm_bwd(dy, hval, proj, gain, t, wc, dh)
    dq, dk, dv, dgate = _mlstm_bwd(proj, gcol, grow, hval, d_hval, cs, ns, t, wc, dh)
    dgt, d_b_gates = _gates_bwd(dgate, proj, bias_tile, t, gate_tile)
    d_proj = jnp.concatenate([dcb, dcc, dch, dq, dk, dv, d_o, dgt], axis=1)
    d_w_in = _matmul(d_proj, x_b, "tn", BF16, "proj_dw", tm=2432, tn=1024, tk=t)
    small = dict(b_gates=d_b_gates[:, :2 * NH], w_sc_conv=d_w_sc, mh_gain=d_gain, ln1_g=d_ln1_g, ln1_b=d_ln1_b,
                 w_ffn_conv=d_w_fc, b_ffn_conv=d_b_fc, ln2_g=d_ln2_g, ln2_b=d_ln2_b)
    token = gx.start("w_in", d_w_in)
    token = gx.relay("w_in", gx.meanwhile(small, loss, token))
    grad_x = _matmul(d_proj, w_in, "nn", F32, "proj_dx", tm=512, tn=512, tk=ninp, add=dz1, add_scale=ALPHA, after=token)
    return loss, grad_x, small, gx


HBM = pl.BlockSpec(memory_space=pltpu.HBM)


def _place():
    return lax.axis_index("x"), lax.axis_index("y"), lax.axis_index("c")


def _index(p):
    return 4 * p[0] + 2 * p[1] + p[2]


def _all_gather(arrs, name):
    n = len(arrs)

    def body(*refs):
        ins, outs = refs[:n], refs[n:2 * n]
        send_sems, recv_sems, local_sems = refs[2 * n:]
        x, y, c = _place()
        me, sibling = (x, y, c), (x, y, 1 - c)
        chips = [(1 - x, y), (x, 1 - y), (1 - x, 1 - y)]

        def copy(a, k, block, to, own=False):
            dst = outs[a].at[_index(block)]
            return pltpu.make_async_remote_copy(
                src_ref=ins[a] if own else dst, dst_ref=dst,
                send_sem=send_sems.at[k * n + a], recv_sem=recv_sems.at[k * n + a],
                device_id=to, device_id_type=MESH)

        mine = [pltpu.make_async_copy(ins[a], outs[a].at[_index(me)], local_sems.at[a]) for a in range(n)]
        for cp in mine:
            cp.start()
        first = []
        for a in range(n):
            first.append(copy(a, 0, me, sibling, own=True))
            first += [copy(a, 1 + j, me, (*chip, c), own=True) for j, chip in enumerate(chips)]
        for cp in first:
            cp.start()
        passed = []
        for j, chip in enumerate(chips):
            for a in range(n):
                copy(a, 1 + j, (*chip, c), me).wait_recv()
                cp = copy(a, 4 + j, (*chip, c), sibling)
                cp.start()
                passed.append(cp)
        for a in range(n):
            copy(a, 0, sibling, me).wait_recv()
            for j, chip in enumerate(chips):
                copy(a, 4 + j, (*chip, 1 - c), me).wait_recv()
        for cp in first + passed:
            cp.wait_send()
        for cp in mine:
            cp.wait()

    return pl.pallas_call(
        body, name=name, in_specs=[HBM] * n, out_specs=[HBM] * n,
        out_shape=[jax.ShapeDtypeStruct((N_DEV,) + a.shape, a.dtype) for a in arrs],
        scratch_shapes=[pltpu.SemaphoreType.DMA((7 * n,)), pltpu.SemaphoreType.DMA((7 * n,)),
                        pltpu.SemaphoreType.DMA((n,))],
    )(*arrs)


SEM = pl.BlockSpec(memory_space=pltpu.SEMAPHORE)
EFFECT = pltpu.SideEffectType.DATAFLOW_SIDE_EFFECTING


def _chips(x, y):
    return [(1 - x, y), (x, 1 - y), (1 - x, 1 - y)]


N_CHIP = N_DEV // 2


def _pair_route(x, y, c):
    return [((x, y, 1 - c), 2 * q + (1 - c), q, q) for q in range(N_CHIP)]


def _chip_route(x, y, c):
    mine = 2 * x + y
    return [((*chip, c), 2 * chip[0] + chip[1], mine, 2 * chip[0] + chip[1]) for chip in _chips(x, y)]


def _exchange_pieces(g_ref, land_ref, width, tail):
    if not tail:
        return [(lambda i: g_ref.at[i], lambda s: land_ref.at[s])]
    rows = lambda i, n: pl.ds(pl.multiple_of(i * width, IN_TAIL), n)
    return [(lambda i: g_ref.at[rows(i, width), :], lambda s: land_ref.at[s, pl.ds(0, width), :]),
            (lambda i: g_ref.at[rows(i + 1, IN_TAIL), :], lambda s: land_ref.at[s, pl.ds(width, IN_TAIL), :])]


def _exchange_start(grad, route, tail, name):
    width = IN_SLAB if tail else grad.shape[1]
    n_p = 2 if tail else 1
    n_c = len(route(0, 0, 0))
    land_shape = (N_CHIP, width + (IN_TAIL if tail else 0), grad.shape[-1])

    def body(g_ref, land_ref, send_sems, recv_sems, g_thru, land_thru, token):
        for j, (peer, slab, slot, _) in enumerate(route(*_place())):
            for p, (src, dst) in enumerate(_exchange_pieces(g_ref, land_ref, width, tail)):
                pltpu.make_async_remote_copy(src_ref=src(slab), dst_ref=dst(slot), send_sem=send_sems.at[j * n_p + p],
                                             recv_sem=recv_sems.at[j * n_p + p], device_id=peer,
                                             device_id_type=MESH).start()
        token[...] = jnp.zeros_like(token)

    return pl.pallas_call(
        body, name=name,
        out_shape=(pltpu.SemaphoreType.DMA((n_c * n_p,)), pltpu.SemaphoreType.DMA((n_c * n_p,)),
                   pltpu.HBM(grad.shape, grad.dtype), pltpu.HBM(land_shape, grad.dtype),
                   jax.ShapeDtypeStruct((8, LANE), F32)),
        in_specs=(HBM, HBM), out_specs=(SEM, SEM, HBM, HBM, pl.BlockSpec(memory_space=pltpu.VMEM)),
        input_output_aliases={0: 2, 1: 3},
        compiler_params=pltpu.CompilerParams(has_side_effects=EFFECT),
    )(pltpu.with_memory_space_constraint(grad, pltpu.HBM),
      pltpu.with_memory_space_constraint(lax.empty(land_shape, grad.dtype), pltpu.HBM))


def _exchange_wait(send_sems, recv_sems, g_thru, land_thru, after, route, tail, name):
    width = IN_SLAB if tail else g_thru.shape[1]
    n_p = 2 if tail else 1

    def body(g_ref, land_ref, send_sems, recv_sems, after_ref, g_dead, got_ref):
        for j, (peer, slab, _, slot) in enumerate(route(*_place())):
            for p, (src, dst) in enumerate(_exchange_pieces(g_ref, land_ref, width, tail)):
                cp = pltpu.make_async_remote_copy(src_ref=src(slab), dst_ref=dst(slot),
                                                  send_sem=send_sems.at[j * n_p + p], recv_sem=recv_sems.at[j * n_p + p],
                                                  device_id=peer, device_id_type=MESH)
                cp.wait_send()
                cp.wait_recv()

    return pl.pallas_call(
        body, name=name,
        out_shape=(pltpu.HBM(g_thru.shape, g_thru.dtype), pltpu.HBM(land_thru.shape, land_thru.dtype)),
        in_specs=(HBM, HBM, SEM, SEM, pl.BlockSpec(memory_space=pl.ANY)), out_specs=(HBM, HBM),
        input_output_aliases={0: 0, 1: 1},
        compiler_params=pltpu.CompilerParams(has_side_effects=EFFECT),
    )(g_thru, land_thru, send_sems, recv_sems, after)


def _pair_add(grad, pair, core, tail, name):
    rows, cols = (IN_SLAB if tail else grad.shape[1]), grad.shape[-1]
    total = pair.shape[1]

    def body(core_ref, *refs):
        if tail:
            g_ref, t_ref, p_ref, o_ref = refs
            o_ref[0:rows, :] = (g_ref[...].astype(F32) + p_ref[0:rows, :].astype(F32)).astype(BF16)
            o_ref[rows:total, :] = (t_ref[...].astype(F32) + p_ref[rows:total, :].astype(F32)).astype(BF16)
        else:
            g_ref, p_ref, o_ref = refs
            o_ref[...] = (g_ref[...].astype(F32) + p_ref[...].astype(F32)).astype(BF16)

    if tail:
        tc = _fit(cols, 512)
        grid = (N_CHIP, cols // tc)
        slab = pl.BlockSpec((None, total, tc), lambda q, i, core_ref: (q, 0, i))
        per = IN_SLAB // IN_TAIL
        in_specs = [pl.BlockSpec((rows, tc), lambda q, i, core_ref: (2 * q + core_ref[0], i)),
                    pl.BlockSpec((IN_TAIL, tc), lambda q, i, core_ref: ((2 * q + core_ref[0] + 1) * per, i))]
    else:
        tr = _rows(rows, 1024)
        grid = (N_CHIP, rows // tr)
        slab = pl.BlockSpec((None, tr, cols), lambda q, i, core_ref: (q, i, 0))
        in_specs = [pl.BlockSpec((None, tr, cols), lambda q, i, core_ref: (2 * q + core_ref[0], i, 0))]
    return pl.pallas_call(
        body, name=name,
        grid_spec=pltpu.PrefetchScalarGridSpec(num_scalar_prefetch=1, grid=grid,
                                               in_specs=in_specs + [slab], out_specs=slab),
        out_shape=jax.ShapeDtypeStruct(pair.shape, BF16),
        compiler_params=_params("parallel", "parallel"),
    )(core, *([grad, grad] if tail else [grad]), pair)


def _slot(land_ref, block_shape, i):
    if land_ref.ndim > len(block_shape):
        return land_ref.at[i]
    cols = block_shape[1]
    return land_ref.at[:, pl.ds(pl.multiple_of(i * cols, LANE), cols)]


def _gather_start(blocks, after, name, spare=(), wide=()):
    n = len(blocks)
    lands = [(b.shape[0], N_DEV * b.shape[1]) if a in wide else (N_DEV + (a in spare),) + b.shape
             for a, b in enumerate(blocks)]

    def body(*refs):
        b_refs, land_refs = refs[:n], refs[n:2 * n]
        send_sems, recv_sems = refs[2 * n + 1:3 * n + 1], refs[3 * n + 1:4 * n + 1]
        token = refs[-1]
        x, y, c = _place()
        me = _index((x, y, c))
        for a in range(n):
            for k, to in enumerate([(x, y, 1 - c)] + [(*chip, c) for chip in _chips(x, y)]):
                pltpu.make_async_remote_copy(src_ref=b_refs[a], dst_ref=_slot(land_refs[a], blocks[a].shape, me),
                                             send_sem=send_sems[a].at[k],
                                             recv_sem=recv_sems[a].at[k], device_id=to, device_id_type=MESH).start()
        token[...] = jnp.zeros_like(token)

    sems = [pltpu.SemaphoreType.DMA((4,))] * n
    out = pl.pallas_call(
        body, name=name,
        out_shape=(*sems, *sems, *[pltpu.HBM(b.shape, b.dtype) for b in blocks],
                   *[pltpu.HBM(s, b.dtype) for s, b in zip(lands, blocks)], jax.ShapeDtypeStruct((8, LANE), F32)),
        in_specs=(*[HBM] * (2 * n), pl.BlockSpec(memory_space=pl.ANY)),
        out_specs=(*[SEM] * (2 * n), *[HBM] * (2 * n), pl.BlockSpec(memory_space=pltpu.VMEM)),
        input_output_aliases={i: 2 * n + i for i in range(2 * n)},
        compiler_params=pltpu.CompilerParams(has_side_effects=EFFECT),
    )(*[pltpu.with_memory_space_constraint(b, pltpu.HBM) for b in blocks],
      *[pltpu.with_memory_space_constraint(lax.empty(s, b.dtype), pltpu.HBM) for s, b in zip(lands, blocks)], after)
    return [(out[a], out[n + a], out[2 * n + a], out[3 * n + a]) for a in range(n)], out[-1]


def _gather_forward(send_sems, recv_sems, b_thru, land_thru, after, name):
    def body(b_ref, land_ref, send_sems, recv_sems, after_ref, b_dead, land_out, send2, recv2, token):
        x, y, c = _place()
        sibling = (x, y, 1 - c)
        for k, frm in enumerate([sibling] + [(*chip, c) for chip in _chips(x, y)]):
            cp = pltpu.make_async_remote_copy(src_ref=b_ref, dst_ref=_slot(land_ref, b_thru.shape, _index(frm)),
                                              send_sem=send_sems.at[k],
                                              recv_sem=recv_sems.at[k], device_id=frm, device_id_type=MESH)
            cp.wait_send()
            cp.wait_recv()
        for j, chip in enumerate(_chips(x, y)):
            slot = _slot(land_ref, b_thru.shape, _index((*chip, c)))
            pltpu.make_async_remote_copy(src_ref=slot, dst_ref=slot, send_sem=send2.at[j], recv_sem=recv2.at[j],
                                         device_id=sibling, device_id_type=MESH).start()
        token[...] = jnp.zeros_like(token)

    return pl.pallas_call(
        body, name=name,
        out_shape=(pltpu.HBM(b_thru.shape, b_thru.dtype), pltpu.HBM(land_thru.shape, land_thru.dtype),
                   pltpu.SemaphoreType.DMA((3,)), pltpu.SemaphoreType.DMA((3,)), jax.ShapeDtypeStruct((8, LANE), F32)),
        in_specs=(HBM, HBM, SEM, SEM, pl.BlockSpec(memory_space=pl.ANY)),
        out_specs=(HBM, HBM, SEM, SEM, pl.BlockSpec(memory_space=pltpu.VMEM)),
        input_output_aliases={0: 0, 1: 1},
        compiler_params=pltpu.CompilerParams(has_side_effects=EFFECT),
    )(b_thru, land_thru, send_sems, recv_sems, after)


def _gather_finish(land_thru, send2, recv2, after, name, block_shape):
    def body(land_ref, send2, recv2, after_ref, land_out):
        x, y, c = _place()
        for j, chip in enumerate(_chips(x, y)):
            cp = pltpu.make_async_remote_copy(src_ref=_slot(land_ref, block_shape, _index((*chip, c))),
                                              dst_ref=_slot(land_ref, block_shape, _index((*chip, 1 - c))),
                                              send_sem=send2.at[j],
                                              recv_sem=recv2.at[j], device_id=(x, y, 1 - c), device_id_type=MESH)
            cp.wait_send()
            cp.wait_recv()

    return pl.pallas_call(
        body, name=name, out_shape=pltpu.HBM(land_thru.shape, land_thru.dtype),
        in_specs=(HBM, SEM, SEM, pl.BlockSpec(memory_space=pl.ANY)), out_specs=HBM,
        input_output_aliases={0: 0},
        compiler_params=pltpu.CompilerParams(has_side_effects=EFFECT),
    )(land_thru, send2, recv2, after)


class _Gathering:
    WIDE = ("w_up",)

    def __init__(self, first, later, me):
        started, token = _gather_start(list(first.values()), next(iter(first.values())), "gather1_first", spare=(0,))
        cast = [_behind(a, token).astype(BF16) for a in later.values()]
        wide = tuple(a for a, n in enumerate(later) if n in self.WIDE)
        started_later, self.token = _gather_start(cast, token, "gather1_later", wide=wide)
        self.me, self.state = me, dict(zip([*first, *later], started + started_later))

    def begin(self, after):
        return self.token

    def forward(self, name, after):
        *self.state[name], token = _gather_forward(*self.state[name], after, "gather2_" + name)
        return token

    def get(self, name, after):
        block, land, send2, recv2 = self.state[name]
        land = _gather_finish(land, send2, recv2, after, "gather3_" + name, block.shape)
        if name in self.WIDE:
            return lax.dynamic_update_slice(land, block, (0, self.me * block.shape[1]))
        land = lax.dynamic_update_index_in_dim(land, block[None], self.me, 0)
        return land if name not in ("w_out", "w_down") else land.reshape(-1, land.shape[2])


class _Reducing:
    def __init__(self, core, chip, gather_small):
        self.core, self.chip, self.state, self.token, self.gather_small = core, chip, {}, None, gather_small

    def meanwhile(self, small, loss, after):
        self.small_sum = self.gather_small(small, loss, after)
        return self.small_sum

    def start(self, name, grad):
        tail = name == "w_in"
        g = grad if tail or grad.ndim == 3 else grad.reshape(N_DEV, grad.shape[0] // N_DEV, grad.shape[1])
        *self.state[name], token = _exchange_start(g, _pair_route, tail, "pair_send_" + name)
        return token

    def relay(self, name, after):
        tail = name == "w_in"
        grad, pair = _exchange_wait(*self.state[name], after, _pair_route, tail, "pair_recv_" + name)
        total = _pair_add(grad, pair, self.core, tail, "pair_add_" + name)
        *self.state[name], self.token = _exchange_start(total, _chip_route, False, "chip_send_" + name)
        return self.token

    def finish(self, name, after):
        total, land = _exchange_wait(*self.state[name], after, _chip_route, False, "chip_recv_" + name)
        own = lax.dynamic_index_in_dim(total, self.chip, 0, keepdims=True)
        return lax.dynamic_update_index_in_dim(land, own, self.chip, 0)


def _carry_w_in(main, tail):
    slabs, _, d = main.shape
    tc = _fit(d, 2048)
    assert slabs == N_DEV + 1 and tail.shape[:2] == (N_DEV, IN_TAIL), (main.shape, tail.shape)
    top = lambda off: pl.BlockSpec((None, IN_TAIL, tc), lambda s, j: (s + off, 0, j))

    def carry(m_ref, t_ref, o_ref):
        o_ref[...] = m_ref[...] + t_ref[...]

    main = pl.pallas_call(
        carry, name="carry_w_in", grid=(N_DEV - 1, d // tc), in_specs=[top(1), top(0)], out_specs=top(1),
        out_shape=jax.ShapeDtypeStruct(main.shape, main.dtype), input_output_aliases={0: 0},
        compiler_params=_params("parallel", "parallel"),
    )(main, tail)

    def last(m_ref, t_ref, o_ref):
        o_ref[...] = jnp.zeros_like(o_ref)
        o_ref[0:IN_TAIL, :] = t_ref[...]

    return pl.pallas_call(
        last, name="last_slab_w_in", grid=(d // tc,),
        in_specs=[pl.BlockSpec(memory_space=pl.ANY), pl.BlockSpec((None, IN_TAIL, tc), lambda j: (N_DEV - 1, 0, j))],
        out_specs=pl.BlockSpec((None, LANE, tc), lambda j: (N_DEV, 0, j)),
        out_shape=jax.ShapeDtypeStruct(main.shape, main.dtype), input_output_aliases={0: 0},
        compiler_params=_params("parallel"),
    )(main, tail)


def _rows(n, want):
    t = min(n, want)
    t -= t % 16
    while n % t:
        t -= 16
    return t


def _adam_math(w, g, m, v):
    m2 = ADAM_B1 * m + (1.0 - ADAM_B1) * g
    v2 = ADAM_B2 * v + (1.0 - ADAM_B2) * (g * g)
    m_hat = m2 * (1.0 / (1.0 - ADAM_B1 ** ADAM_STEP))
    v_hat = v2 * (1.0 / (1.0 - ADAM_B2 ** ADAM_STEP))
    return -ADAM_LR * (m_hat / (jnp.sqrt(v_hat) + ADAM_EPS) + ADAM_WD * w), m2, v2


def _slot_sum(r_ref):
    acc = r_ref[0].astype(F32)
    for i in range(1, r_ref.shape[0]):
        acc = acc + r_ref[i].astype(F32)
    return acc


def _shift_w_in(w):
    ws, d = w.shape
    tc = _fit(d, 256)

    def body(w_ref, main_ref, tail_ref, tall):
        tall[...] = jnp.zeros_like(tall)
        tall[0:ws, :] = w_ref[...]
        moved = pltpu.roll(tall[...], _index(_place()), 0).astype(BF16)
        main_ref[...] = moved[0:IN_SLAB]
        tail_ref[...] = moved[IN_SLAB:]

    return pl.pallas_call(
        body, name="shift_w_in", grid=(d // tc,),
        in_specs=[pl.BlockSpec((ws, tc), lambda j: (0, j))],
        out_specs=[pl.BlockSpec((IN_SLAB, tc), lambda j: (0, j)), pl.BlockSpec((IN_TAIL, tc), lambda j: (0, j))],
        out_shape=[jax.ShapeDtypeStruct((IN_SLAB, d), BF16), jax.ShapeDtypeStruct((IN_TAIL, d), BF16)],
        scratch_shapes=[pltpu.VMEM((IN_SLAB + IN_TAIL, tc), F32)], compiler_params=_params("parallel"),
    )(w)


def _sum_adamw_shifted(r, w, m, v, name):
    _, ph, d = r.shape
    ws = w.shape[0]
    tc = _fit(d, 256)

    def body(r_ref, w_ref, m_ref, v_ref, g_ref, d_ref, m2_ref, v2_ref, tall):
        tall[...] = pltpu.roll(_slot_sum(r_ref), lax.rem(ph - _index(_place()), ph), 0)
        g = tall[0:ws, :]
        g_ref[...] = g
        d_ref[...], m2_ref[...], v2_ref[...] = _adam_math(w_ref[...], g, m_ref[...], v_ref[...])

    blk = pl.BlockSpec((ws, tc), lambda j: (0, j))
    out = jax.ShapeDtypeStruct(w.shape, F32)
    return pl.pallas_call(
        body, name=name, grid=(d // tc,),
        in_specs=[pl.BlockSpec((r.shape[0], ph, tc), lambda j: (0, 0, j)), blk, blk, blk],
        out_specs=[blk] * 4, out_shape=[out] * 4,
        scratch_shapes=[pltpu.VMEM((ph, tc), F32)], compiler_params=_params("parallel"),
    )(r, w, m, v)


def _sum_slots(r, name, tr=128):
    _, rows, cols = r.shape
    tr = _rows(rows, tr)

    def body(r_ref, g_ref):
        g_ref[...] = _slot_sum(r_ref)

    return pl.pallas_call(
        body, name=name, grid=(rows // tr,),
        in_specs=[pl.BlockSpec((r.shape[0], tr, cols), lambda i: (0, i, 0))],
        out_specs=pl.BlockSpec((tr, cols), lambda i: (i, 0)),
        out_shape=jax.ShapeDtypeStruct((rows, cols), F32),
        compiler_params=_params("parallel"),
    )(r)


def _adamw(w, g, m, v, name, tr=256):
    rows, cols = w.shape
    tr = _rows(rows, tr)

    def body(w_ref, g_ref, m_ref, v_ref, d_ref, m2_ref, v2_ref):
        d_ref[...], m2_ref[...], v2_ref[...] = _adam_math(w_ref[...], g_ref[...], m_ref[...], v_ref[...])

    blk = pl.BlockSpec((tr, cols), lambda i: (i, 0))
    out = jax.ShapeDtypeStruct((rows, cols), F32)
    return pl.pallas_call(
        body, name=name, grid=(rows // tr,), in_specs=[blk] * 4, out_specs=[blk] * 3, out_shape=[out] * 3,
        compiler_params=_params("parallel"),
    )(w, g, m, v)


def _sum_adamw(r, w, m, v, name, tr=256):
    rows, cols = w.shape
    tr = _rows(rows, tr)

    def body(r_ref, w_ref, m_ref, v_ref, g_ref, d_ref, m2_ref, v2_ref):
        g = _slot_sum(r_ref)
        g_ref[...] = g
        d_ref[...], m2_ref[...], v2_ref[...] = _adam_math(w_ref[...], g, m_ref[...], v_ref[...])

    blk = pl.BlockSpec((tr, cols), lambda i: (i, 0))
    out = jax.ShapeDtypeStruct((rows, cols), F32)
    return pl.pallas_call(
        body, name=name, grid=(rows // tr,),
        in_specs=[pl.BlockSpec((r.shape[0], tr, cols), lambda i: (0, i, 0)), blk, blk, blk],
        out_specs=[blk] * 4, out_shape=[out] * 4,
        compiler_params=_params("parallel"),
    )(r, w, m, v)


def _pack(pieces, sizes):
    flat = [jnp.pad(p.reshape(-1).astype(F32), (0, s - p.size)) for p, s in zip(pieces, sizes)]
    total = sum(sizes)
    padded = -(-total // (16 * LANE)) * (16 * LANE)
    return jnp.pad(jnp.concatenate(flat), (0, padded - total)).reshape(-1, LANE)


def _unpack(packed, shapes, sizes):
    flat = packed.reshape(-1)
    out, off = [], 0
    for shp, s in zip(shapes, sizes):
        n = 1
        for k in shp:
            n *= k
        out.append(flat[off:off + n].reshape(shp))
        off += s
    return out


def _lanes(n):
    return -(-n // LANE) * LANE


WEIGHTS = ("w_in", "b_gates", "w_sc_conv", "mh_gain", "w_out", "ln1_g", "ln1_b", "w_up", "w_ffn_conv", "b_ffn_conv",
           "w_down", "ln2_g", "ln2_b")
BIG = ("w_in", "w_out", "w_up", "w_down")
SMALL = tuple(n for n in WEIGHTS if n not in BIG)


def kernel(x, w_in, b_gates, w_sc_conv, mh_gain, w_out, ln1_g, ln1_b, w_up, w_ffn_conv, b_ffn_conv, w_down, ln2_g, ln2_b, loss_target, m_w_in, m_b_gates, m_w_sc_conv, m_mh_gain, m_w_out, m_ln1_g, m_ln1_b, m_w_up, m_w_ffn_conv, m_b_ffn_conv, m_w_down, m_ln2_g, m_ln2_b, v_w_in, v_b_gates, v_w_sc_conv, v_mh_gain, v_w_out, v_ln1_g, v_ln1_b, v_w_up, v_w_ffn_conv, v_b_ffn_conv, v_w_down, v_ln2_g, v_ln2_b):
    w = dict(zip(WEIGHTS, (w_in, b_gates, w_sc_conv, mh_gain, w_out, ln1_g, ln1_b, w_up, w_ffn_conv, b_ffn_conv,
                           w_down, ln2_g, ln2_b)))
    m = dict(zip(WEIGHTS, (m_w_in, m_b_gates, m_w_sc_conv, m_mh_gain, m_w_out, m_ln1_g, m_ln1_b, m_w_up,
                           m_w_ffn_conv, m_b_ffn_conv, m_w_down, m_ln2_g, m_ln2_b)))
    v = dict(zip(WEIGHTS, (v_w_in, v_b_gates, v_w_sc_conv, v_mh_gain, v_w_out, v_ln1_g, v_ln1_b, v_w_up,
                           v_w_ffn_conv, v_b_ffn_conv, v_w_down, v_ln2_g, v_ln2_b)))
    me = _index(_place())
    d = x.shape[2]
    ws_in = w_in.shape[2]
    assert ws_in == IN_SLAB + 1 and N_DEV <= LANE, w_in.shape
    ninp = (N_DEV + 1) * IN_SLAB
    ws_sc, ws_fc = w_sc_conv.shape[2], w_ffn_conv.shape[2]
    w_in_t, m_in_t, v_in_t = (jnp.transpose(a[0]) for a in (w_in, m_w_in, v_w_in))

    w_in_main, w_in_tail = _shift_w_in(w_in_t)
    taps8 = lambda a: jnp.pad(a[0], ((0, 5), (0, 0)))
    at_once = ("w_in", "w_tail", "w_sc", "w_fc")
    wx = _Gathering(dict(zip(at_once, (w_in_main, w_in_tail, taps8(w_sc_conv), taps8(w_ffn_conv)))),
                    {n: w[n][0] for n in ("w_out", "w_up", "w_down")}, me)
    token = x_b = _behind(x[0], wx.begin(None)).astype(BF16)
    for n in at_once:
        token = wx.forward(n, token)
    g_in, g_tail, g_sc, g_fc = (wx.get(n, token) for n in at_once)
    w_in_full = _carry_w_in(g_in, g_tail).reshape(ninp, d)
    w_sc_full = g_sc[:, :3].transpose(1, 0, 2).reshape(3, N_DEV * ws_sc)
    w_fc_full = g_fc[:, :3].transpose(1, 0, 2).reshape(3, N_DEV * ws_fc)

    xi, yi, ci = _place()
    names = ("loss",) + SMALL
    pieces = {}

    def gather_small(small, loss_t, after):
        pieces.update(small, loss=loss_t[0, :1])
        sizes = [_lanes(pieces[n].size) for n in names]
        (g_small,) = _all_gather([_behind(_pack([pieces[n] for n in names], sizes), after)], "gather_small")
        return _sum_slots(g_small, "sum_small", tr=g_small.shape[1])

    gx = _Reducing(jnp.reshape(ci, (1,)).astype(jnp.int32), 2 * xi + yi, gather_small)
    loss_t, grad_x, small, _ = _local_step(
        x[0], loss_target[0], w_in_full, b_gates, w_sc_full, mh_gain, None, ln1_g, ln1_b, None,
        w_fc_full, b_ffn_conv, None, ln2_g, ln2_b, gx=gx, wx=wx, x_b=x_b)

    grads, deltas, new_m, new_v = {}, {}, {}, {}
    for name in ("w_down", "w_up", "w_out"):
        grads[name], deltas[name], new_m[name], new_v[name] = _sum_adamw(
            gx.finish(name, gx.token), w[name][0], m[name][0], v[name][0], "adamw_" + name)

    summed = _unpack(gx.small_sum, [pieces[n].shape for n in names], [_lanes(pieces[n].size) for n in names])
    full = dict(zip(names, summed))
    full["w_sc_conv"] = lax.dynamic_slice(full["w_sc_conv"], (0, me * ws_sc), (3, ws_sc))
    full["w_ffn_conv"] = lax.dynamic_slice(full["w_ffn_conv"], (0, me * ws_fc), (3, ws_fc))
    for n in SMALL:
        grads[n] = full[n].reshape(w[n].shape)
    sizes = [_lanes(w[n].size) for n in SMALL]
    shapes = [w[n].shape for n in SMALL]
    packed = [_pack([t[n] for n in SMALL], sizes) for t in (w, grads, m, v)]
    small_out = _adamw(*packed, "adamw_small")
    for res, t in zip(small_out, (deltas, new_m, new_v)):
        t.update(zip(SMALL, _unpack(res, shapes, sizes)))

    done = sum(t[0:1, 0:1] for t in (grad_x, deltas["w_down"], deltas["w_up"], deltas["w_out"], small_out[0]))
    grads["w_in"], deltas["w_in"], new_m["w_in"], new_v["w_in"] = (
        jnp.transpose(a)[None] for a in _sum_adamw_shifted(gx.finish("w_in", done), w_in_t, m_in_t, v_in_t, "adamw_w_in"))

    big = lambda t: {n: (t[n].reshape(w[n].shape) if n in BIG else t[n]) for n in WEIGHTS}
    grads, deltas, new_m, new_v = big(grads), big(deltas), big(new_m), big(new_v)
    return (full["loss"].reshape(()), grad_x[None], *[grads[n] for n in WEIGHTS], *[deltas[n] for n in WEIGHTS],
            *[new_m[n] for n in WEIGHTS], *[new_v[n] for n in WEIGHTS])
```

```python
import functools

import jax
import jax.numpy as jnp
from jax import lax
from jax.experimental import pallas as pl
from jax.experimental.pallas import tpu as pltpu

F32 = jnp.float32
BF16 = jnp.bfloat16
MESH = pl.DeviceIdType.MESH

N_DEV = 8
NH = 4
CHUNK = 64
LN_EPS = 1e-5
HN_EPS = 1e-6
ALPHA = 2.0 ** 0.25
LANE = 128
IN_SLAB = 7 * LANE
IN_TAIL = 16
VMEM_LIMIT = 56 * 1024 * 1024
ADAM_LR, ADAM_B1, ADAM_B2, ADAM_EPS, ADAM_WD, ADAM_STEP = 0.001, 0.9, 0.999, 1e-08, 0.01, 10

_NN = (((1,), (0,)), ((), ()))
_NT = (((1,), (1,)), ((), ()))
_TN = (((0,), (0,)), ((), ()))


def _dot(a, b, dn=_NN):
    return lax.dot_general(a, b, dn, preferred_element_type=F32)


def _params(*sem):
    return pltpu.CompilerParams(dimension_semantics=sem if sem else None, vmem_limit_bytes=VMEM_LIMIT)


def _iota(shape, axis):
    return lax.broadcasted_iota(jnp.int32, shape, axis)


def _fit(n, want):
    if n <= want:
        return n
    t = want - want % LANE
    while n % t:
        t -= LANE
    return t


def _matmul(a, b, mode, out_dtype, name, tm=1024, tn=512, tk=1024, add=None, add_scale=1.0,
            a_blocked=False, b_blocked=False, o_width=None, after=None, n=None):
    a_parts = a if isinstance(a, tuple) else None
    b_parts = b if isinstance(b, tuple) else None
    if a_parts:
        a_blocked, (a_rows, wa), na = True, a[0].shape, len(a)
        kd, m = (a_rows, na * wa) if mode == "tn" else (na * wa, a_rows)
    elif a_blocked:
        na, a_rows, wa = a.shape
        kd, m = (a_rows, na * wa) if mode == "tn" else (na * wa, a_rows)
    elif mode == "tn":
        kd, m = a.shape
    else:
        m, kd = a.shape
    if b_parts:
        b_blocked, (rows, w), nb = True, b[0].shape, len(b)
    elif b_blocked:
        nb, rows, w = b.shape
    if b_blocked:
        n = rows if mode == "nt" else nb * w
        assert (nb * w if mode == "nt" else rows) == kd, (name, kd)
    else:
        n = n or (b.shape[0] if mode == "nt" else b.shape[1])
    tm, tn, tk = _fit(m, tm), _fit(n, tn), _fit(kd, tk)
    if a_blocked and mode == "tn":
        tm = _fit(wa, tm)
    if a_blocked and mode != "tn":
        tk = _fit(wa, tk)
    if b_blocked and mode != "nt":
        tn = _fit(w, tn)
    if b_blocked and mode == "nt":
        tk = _fit(w, tk)
    if o_width is not None:
        tn = _fit(o_width, tn)
    assert m % tm == 0 and n % tn == 0 and kd % tk == 0, (name, m, n, kd, tm, tn, tk)
    assert not (a_blocked and mode != "tn" and wa % tk) and not (b_blocked and mode == "nt" and w % tk), (name, tk)
    nk = kd // tk
    dn = {"nn": _NN, "nt": _NT, "tn": _TN}[mode]
    if a_blocked and mode == "tn":
        a_per = wa // tm
        a_spec = pl.BlockSpec((None, tk, tm), lambda i, j, k: (i // a_per, k, i % a_per))
    elif a_blocked:
        a_per = wa // tk
        a_spec = pl.BlockSpec((None, tm, tk), lambda i, j, k: (k // a_per, i, k % a_per))
    elif mode == "tn":
        a_spec = pl.BlockSpec((tk, tm), lambda i, j, k: (k, i))
    else:
        a_spec = pl.BlockSpec((tm, tk), lambda i, j, k: (i, k))
    if b_blocked and mode != "nt":
        per = w // tn
        b_spec = pl.BlockSpec((None, tk, tn), lambda i, j, k: (j // per, k, j % per))
    elif b_blocked:
        per = w // tk
        b_spec = pl.BlockSpec((None, tn, tk), lambda i, j, k: (k // per, j, k % per))
    elif mode == "nt":
        b_spec = pl.BlockSpec((tn, tk), lambda i, j, k: (j, k))
    else:
        b_spec = pl.BlockSpec((tk, tn), lambda i, j, k: (k, j))
    if o_width is None:
        o_spec = pl.BlockSpec((tm, tn), lambda i, j, k: (i, j))
        o_shape = (m, n)
    else:
        oper = o_width // tn
        o_spec = pl.BlockSpec((None, tm, tn), lambda i, j, k: (j // oper, i, j % oper))
        o_shape = (n // o_width, m, o_width)
    a_list, a_specs = [a], [a_spec]
    if a_parts:
        hold = lambda x, s: jnp.clip(x - s * a_per, 0, a_per - 1)
        a_list = list(a_parts)
        a_specs = [(pl.BlockSpec((tk, tm), lambda i, j, k, s=s: (k, hold(i, s))) if mode == "tn"
                    else pl.BlockSpec((tm, tk), lambda i, j, k, s=s: (i, hold(k, s)))) for s in range(na)]
    b_list, b_specs = [b], [b_spec]
    if b_parts:
        hold_b = lambda x, s: jnp.clip(x - s * per, 0, per - 1)
        b_list = list(b_parts)
        b_specs = [(pl.BlockSpec((tn, tk), lambda i, j, k, s=s: (j, hold_b(k, s))) if mode == "nt"
                    else pl.BlockSpec((tk, tn), lambda i, j, k, s=s: (k, hold_b(j, s)))) for s in range(nb)]
    n_a, n_b = len(a_list), len(b_list)
    has_add = add is not None
    n_in = n_a + n_b + has_add + (after is not None)
    in_place = nk > 1 and out_dtype == F32

    def body(*refs):
        add_ref = refs[n_a + n_b] if has_add else None
        o_ref = refs[n_in]
        i, j, k = pl.program_id(0), pl.program_id(1), pl.program_id(2)

        def finish(r):
            if has_add:
                r = r + add_scale * add_ref[...]
            o_ref[...] = r.astype(out_dtype)

        def step(a_ref, b_ref):
            if nk == 1:
                finish(_dot(a_ref[...], b_ref[...], dn))
                return
            acc = o_ref if in_place else refs[-1]

            @pl.when(k == 0)
            def _():
                acc[...] = _dot(a_ref[...], b_ref[...], dn)

            @pl.when(k > 0)
            def _():
                acc[...] += _dot(a_ref[...], b_ref[...], dn)

        if n_a == 1 and n_b == 1:
            step(refs[0], refs[1])
        else:
            slab_a = ((i if mode == "tn" else k) // a_per) if n_a > 1 else 0
            slab_b = ((k if mode == "nt" else j) // per) if n_b > 1 else 0
            for sa in range(n_a):
                for sb in range(n_b):
                    pl.when((slab_a == sa) & (slab_b == sb))(functools.partial(step, refs[sa], refs[n_a + sb]))
        if nk > 1 and not (in_place and not has_add):
            @pl.when(k == nk - 1)
            def _():
                finish((o_ref if in_place else refs[-1])[...])

    in_specs = a_specs + b_specs + ([pl.BlockSpec((tm, tn), lambda i, j, k: (i, j))] if has_add else [])
    args = (*a_list, *b_list) + ((add,) if has_add else ())
    if after is not None:
        in_specs.append(pl.BlockSpec(memory_space=pl.ANY))
        args += (after,)
    return pl.pallas_call(
        body, name=name, grid=(m // tm, n // tn, nk),
        in_specs=in_specs, out_specs=o_spec,
        out_shape=jax.ShapeDtypeStruct(o_shape, out_dtype),
        scratch_shapes=[pltpu.VMEM((tm, tn), F32)] if nk > 1 and not in_place else [],
        compiler_params=_params("parallel", "parallel", "arbitrary"),
    )(*args)


def _shift_down(u, s):
    return jnp.where(_iota(u.shape, 0) >= s, pltpu.roll(u, s, 0), 0.0)


def _shift_up(u, s):
    t = u.shape[0]
    return jnp.where(_iota(u.shape, 0) < t - s, pltpu.roll(u, t - s, 0), 0.0)


SLAB = 8


def _rolled(u):
    return pltpu.roll(u, 2, 0), pltpu.roll(u, 1, 0)


def _conv(u, w, rolled=None):
    u2, u1 = _rolled(u) if rolled is None else rolled
    raw = w[0:1] * u2 + w[1:2] * u1 + w[2:3] * u
    head = u[0:SLAB]
    mended = w[0:1] * _shift_down(head, 2) + w[1:2] * _shift_down(head, 1) + w[2:3] * head
    return jnp.concatenate([mended, raw[SLAB:]], axis=0)


def _conv_t(dy, w):
    t = dy.shape[0]
    raw = w[2:3] * dy + w[1:2] * pltpu.roll(dy, t - 1, 0) + w[0:1] * pltpu.roll(dy, t - 2, 0)
    tail = dy[t - SLAB:]
    mended = w[2:3] * tail + w[1:2] * _shift_up(tail, 1) + w[0:1] * _shift_up(tail, 2)
    return jnp.concatenate([raw[:t - SLAB], mended], axis=0)


def _conv_dw(dy, u, rolled=None):
    t = dy.shape[0]
    u2, u1 = _rolled(u) if rolled is None else rolled
    head, tail = dy[0:SLAB], u[t - SLAB:]
    r = _iota(head.shape, 0)
    wrap2 = jnp.sum(jnp.where(r < 2, head * pltpu.roll(tail, 2, 0), 0.0), axis=0, keepdims=True)
    wrap1 = jnp.sum(jnp.where(r < 1, head * pltpu.roll(tail, 1, 0), 0.0), axis=0, keepdims=True)
    d0 = jnp.sum(dy * u2, axis=0, keepdims=True) - wrap2
    d1 = jnp.sum(dy * u1, axis=0, keepdims=True) - wrap1
    d2 = jnp.sum(dy * u, axis=0, keepdims=True)
    r3 = _iota((3, dy.shape[1]), 0)
    return jnp.where(r3 == 0, d0, jnp.where(r3 == 1, d1, d2))


def _sigmoid(x):
    return 0.5 * jnp.tanh(0.5 * x) + 0.5


def _sconv_fwd(proj, w_sc, t, wc):
    nb = wc // LANE

    def body(cb_ref, cc_ref, ch_ref, w_ref, y_ref):
        u = cc_ref[...] * ch_ref[...]
        y_ref[...] = (cb_ref[...] * _conv(u, w_ref[...])).astype(BF16)

    col = lambda off: pl.BlockSpec((t, LANE), lambda j: (0, j + off))
    return pl.pallas_call(
        body, name="sconv_fwd", grid=(nb,),
        in_specs=[col(0), col(nb), col(2 * nb), pl.BlockSpec((3, LANE), lambda j: (0, j))],
        out_specs=pl.BlockSpec((None, t, LANE), lambda j: (0, 0, j)),
        out_shape=jax.ShapeDtypeStruct((2, t, wc), BF16),
        compiler_params=_params("parallel"),
    )(proj, proj, proj, w_sc)


def _sconv_bwd(dy, proj, w_sc, t, wc):
    nb = wc // LANE

    def body(dy_ref, cb_ref, cc_ref, ch_ref, w_ref, dcb_ref, dcc_ref, dch_ref, dw_ref):
        cc, ch, w, d = cc_ref[...], ch_ref[...], w_ref[...], dy_ref[...]
        u = cc * ch
        ru = _rolled(u)
        dcb_ref[...] = (d * _conv(u, w, ru)).astype(BF16)
        dcu = d * cb_ref[...]
        dw_ref[...] = _conv_dw(dcu, u, ru)
        du = _conv_t(dcu, w)
        dcc_ref[...] = (du * ch).astype(BF16)
        dch_ref[...] = (du * cc).astype(BF16)

    col = lambda off: pl.BlockSpec((t, LANE), lambda j: (0, j + off))
    act = jax.ShapeDtypeStruct((t, wc), BF16)
    return pl.pallas_call(
        body, name="sconv_bwd", grid=(nb,),
        in_specs=[col(0), col(0), col(nb), col(2 * nb), pl.BlockSpec((3, LANE), lambda j: (0, j))],
        out_specs=[col(0), col(0), col(0), pl.BlockSpec((3, LANE), lambda j: (0, j))],
        out_shape=[act, act, act, jax.ShapeDtypeStruct((3, wc), F32)],
        compiler_params=_params("parallel"),
    )(dy, proj, proj, proj, w_sc)


def _gates_prep(proj, bias_tile, t, gate_tile):
    def body(g_ref, b_ref, o_ref):
        g = g_ref[...] + b_ref[...]
        lane = _iota(g.shape, 1)
        is_f = (lane >= NH) & (lane < 2 * NH)
        lf = jnp.minimum(g, 0.0) - jnp.log(1.0 + jnp.exp(-jnp.abs(g)))
        c = jnp.where(is_f, lf, 0.0)
        r = _iota(g.shape, 0) % CHUNK
        s = 1
        while s < CHUNK:
            c = c + jnp.where(r >= s, pltpu.roll(c, s, 0), 0.0)
            s *= 2
        o_ref[...] = jnp.where(is_f, c, jnp.where(lane < NH, g, 0.0))

    return pl.pallas_call(
        body, name="gates_prep", grid=(1,),
        in_specs=[pl.BlockSpec((t, LANE), lambda i: (0, gate_tile)), pl.BlockSpec((1, LANE), lambda i: (0, 0))],
        out_specs=pl.BlockSpec((t, LANE), lambda i: (0, 0)),
        out_shape=jax.ShapeDtypeStruct((t, LANE), F32),
        compiler_params=_params("arbitrary"),
    )(proj, bias_tile)


def _gates_bwd(dgate, proj, bias_tile, t, gate_tile):
    def body(dg_ref, g_ref, b_ref, o_ref, s_ref):
        g = g_ref[...] + b_ref[...]
        lane = _iota(g.shape, 1)
        r = _iota(g.shape, 0) % CHUNK
        dsig = 1.0 - _sigmoid(g)
        out = jnp.zeros(g.shape, F32)
        for h in range(NH):
            d = dg_ref[h]
            c = d
            s = 1
            while s < CHUNK:
                c = c + jnp.where(r + s < CHUNK, pltpu.roll(c, t - s, 0), 0.0)
                s *= 2
            di = jnp.broadcast_to(d[:, 0:1], g.shape)
            db = jnp.broadcast_to(c[:, 1:2], g.shape)
            out = out + jnp.where(lane == h, di, 0.0) + jnp.where(lane == NH + h, db * dsig, 0.0)
        o_ref[...] = out.astype(BF16)
        s_ref[...] = jnp.sum(out, axis=0, keepdims=True)

    return pl.pallas_call(
        body, name="gates_bwd", grid=(1,),
        in_specs=[pl.BlockSpec((NH, t, LANE), lambda i: (0, 0, 0)),
                  pl.BlockSpec((t, LANE), lambda i: (0, gate_tile)), pl.BlockSpec((1, LANE), lambda i: (0, 0))],
        out_specs=[pl.BlockSpec((t, LANE), lambda i: (0, 0)), pl.BlockSpec((1, LANE), lambda i: (0, 0))],
        out_shape=[jax.ShapeDtypeStruct((t, LANE), BF16), jax.ShapeDtypeStruct((1, LANE), F32)],
        compiler_params=_params("arbitrary"),
    )(dgate, proj, bias_tile)


def _in_turn(heads):
    while heads:
        heads = [g for g in heads if next(g, heads) is not heads]


def _chunk_gates(gc, gr, h, mprev):
    L = CHUNK
    icol, bcol = gc[:, h:h + 1], gc[:, h + NH:h + NH + 1]
    irow, brow = gr[h:h + 1, :], gr[h + NH:h + NH + 1, :]
    tri = _iota((L, L), 0) >= _iota((L, L), 1)
    log_d = jnp.where(tri, bcol - brow + irow, -jnp.inf)
    inter = bcol + mprev
    mt = jnp.maximum(inter, jnp.max(log_d, axis=1, keepdims=True))
    dw = jnp.exp(log_d - mt)
    iw = jnp.exp(inter - mt)
    g = brow[:, L - 1:L]
    wlog_col = g - bcol + icol
    wlog_row = g - brow + irow
    mnew = jnp.maximum(g + mprev, jnp.max(wlog_row, axis=1, keepdims=True))
    wcol = jnp.exp(wlog_col - mnew)
    decay = jnp.exp(g + mprev - mnew)
    return dw, iw, mt, wcol, decay, mnew


def _mlstm_fwd(proj, gcol, grow, t, wc, dh):
    nc = t // CHUNK
    wm = NH * dh
    assert wc == wm, (wc, wm)
    qoff = 3 * wc // wm
    scale = dh ** -0.5

    def body(q_ref, k_ref, v_ref, gc_ref, gr_ref, h_ref, cs_ref, ns_ref, c_s, n_s, m_s):
        @pl.when(pl.program_id(0) == 0)
        def _():
            c_s[...] = jnp.zeros_like(c_s)
            n_s[...] = jnp.zeros_like(n_s)
            m_s[...] = jnp.zeros_like(m_s)

        gc, gr = gc_ref[...], gr_ref[0]
        done = [None] * NH

        def head(h):
            cols = slice(h * dh, (h + 1) * dh)
            mprev = m_s[h, 0:1, 0:1]
            cprev = c_s[h]
            n8 = n_s[h]
            nprev = n8[0:1]
            qs = q_ref[:, cols] * scale
            k = k_ref[:, cols]
            qs_b, k_b, v_b = qs.astype(BF16), k.astype(BF16), v_ref[:, cols].astype(BF16)
            qk = _dot(qs_b, k_b, _NT)
            yield
            q_c = _dot(qs_b, cprev.astype(BF16))
            yield
            dw, iw, mt, wcol, decay, mnew = _chunk_gates(gc, gr, h, mprev)
            yield
            s = qk * dw
            wk = wcol * k
            num = _dot(s.astype(BF16), v_b) + iw * q_c
            yield
            c_new = decay * cprev + _dot(wk.astype(BF16), v_b, _TN)
            yield
            den = jnp.sum(s, axis=1, keepdims=True) + iw * jnp.sum(qs * nprev, axis=1, keepdims=True)
            done[h] = (cprev, jnp.where(_iota(n8.shape, 0) == 1, mprev, n8),
                       num / jnp.maximum(jnp.abs(den), jnp.exp(-mt)), c_new,
                       decay * n8 + jnp.sum(wk, axis=0, keepdims=True), mnew)

        _in_turn([head(h) for h in range(NH)])
        for h, (c_old, n_old, h_out, c_new, n_new, m_new) in enumerate(done):
            cs_ref[h] = c_old
            ns_ref[h] = n_old
            h_ref[:, h * dh:(h + 1) * dh] = h_out
            c_s[h] = c_new
            n_s[h] = n_new
            m_s[h] = jnp.broadcast_to(m_new, m_s.shape[1:])

    grp = lambda off: pl.BlockSpec((CHUNK, wm), lambda c: (c, qoff + off))
    return pl.pallas_call(
        body, name="mlstm_fwd", grid=(nc,),
        in_specs=[grp(0), grp(1), grp(2),
                  pl.BlockSpec((CHUNK, LANE), lambda c: (c, 0)),
                  pl.BlockSpec((1, 8, CHUNK), lambda c: (c, 0, 0))],
        out_specs=[pl.BlockSpec((CHUNK, wm), lambda c: (c, 0)),
                   pl.BlockSpec((NH, None, dh, dh), lambda c: (0, c, 0, 0)),
                   pl.BlockSpec((NH, None, 8, dh), lambda c: (0, c, 0, 0))],
        out_shape=[jax.ShapeDtypeStruct((t, wm), F32),
                   jax.ShapeDtypeStruct((NH, nc, dh, dh), F32),
                   jax.ShapeDtypeStruct((NH, nc, 8, dh), F32)],
        scratch_shapes=[pltpu.VMEM((NH, dh, dh), F32), pltpu.VMEM((NH, 8, dh), F32), pltpu.VMEM((NH, 8, LANE), F32)],
        compiler_params=_params("arbitrary"),
    )(proj, proj, proj, gcol, grow)


def _mlstm_bwd(proj, gcol, grow, hval, dh_in, cs, ns, t, wc, dh):
    nc = t // CHUNK
    wm = NH * dh
    assert wc == wm, (wc, wm)
    qoff = 3 * wc // wm
    scale = dh ** -0.5
    L = CHUNK

    def body(q_ref, k_ref, v_ref, gc_ref, gr_ref, h_ref, dh_ref, cs_ref, ns_ref,
             dq_ref, dk_ref, dv_ref, dg_ref, dc_s, dn_s):
        @pl.when(pl.program_id(0) == 0)
        def _():
            dc_s[...] = jnp.zeros_like(dc_s)
            dn_s[...] = jnp.zeros_like(dn_s)

        gc, gr = gc_ref[...], gr_ref[0]
        eye = _iota((L, L), 0) == _iota((L, L), 1)
        lane = _iota((L, LANE), 1)
        last = _iota((L, 1), 0) == L - 1
        done = [None] * NH

        def head(h):
            cols = slice(h * dh, (h + 1) * dh)
            ns8 = ns_ref[h]
            nprev = ns8[0:1]
            mprev = ns8[1:2, 0:1]
            cprev = cs_ref[h]
            dcn = dc_s[h]
            dn8 = dn_s[h]
            dnn = dn8[0:1]

            qs = q_ref[:, cols] * scale
            k = k_ref[:, cols]
            qs_b, k_b, v_b = qs.astype(BF16), k.astype(BF16), v_ref[:, cols].astype(BF16)
            qk = _dot(qs_b, k_b, _NT)
            yield
            dw, iw, mt, wcol, decay, _ = _chunk_gates(gc, gr, h, mprev)
            yield
            s = qk * dw
            den = jnp.sum(s, axis=1, keepdims=True) + iw * jnp.sum(qs * nprev, axis=1, keepdims=True)
            emt = jnp.exp(-mt)
            r = 1.0 / jnp.maximum(jnp.abs(den), emt)
            dout = dh_ref[:, cols]
            dnum = dout * r
            dden = (-jnp.sum(dout * h_ref[:, cols], axis=1, keepdims=True) * r
                    * jnp.where(jnp.abs(den) > emt, jnp.sign(den), 0.0))
            dnum_b = dnum.astype(BF16)
            cprev_b = cprev.astype(BF16)
            dcn_b = dcn.astype(BF16)
            yield

            g_raw = _dot(dnum_b, v_b, _NT)
            yield
            q_inter = _dot(dnum_b, cprev_b, _NT)
            yield
            k_raw = _dot(v_b, dcn_b, _NT)
            yield
            gd = (g_raw + dden) * dw
            gd_b = gd.astype(BF16)
            dqs_inter = iw * (q_inter + dden * nprev)
            dk_inter = wcol * (k_raw + dnn)
            wk = wcol * k
            iq = iw * qs
            dqs = _dot(gd_b, k_b) + dqs_inter
            yield
            dk = _dot(gd_b, qs_b, _TN) + dk_inter
            yield
            dv = _dot(s.astype(BF16), dnum_b, _TN) + _dot(wk.astype(BF16), dcn_b)
            yield
            dc_new = decay * dcn + _dot(iq.astype(BF16), dnum_b, _TN)
            yield

            e = gd * qk
            e_cols = jnp.sum(jnp.where(eye, jnp.sum(e, axis=0, keepdims=True), 0.0), axis=1, keepdims=True)
            yield
            k_inter = jnp.sum(k * dk_inter, axis=1, keepdims=True)
            rq = jnp.sum(e, axis=1, keepdims=True) + jnp.sum(qs * dqs_inter, axis=1, keepdims=True)
            rk = e_cols + k_inter
            hsum = jnp.sum(k_inter, axis=0, keepdims=True)
            jdec = decay *(jnp.sum(jnp.sum(dcn * cprev, axis=1, keepdims=True), axis=0, keepdims=True)
                            + jnp.sum(dnn * nprev, axis=1, keepdims=True))
            db = rq - rk + jnp.where(last, hsum + jdec, 0.0)
            done[h] = (jnp.where(lane == 0, rk, jnp.where(lane == 1, db, 0.0)),
                       (dqs * scale).astype(BF16), dk.astype(BF16), dv.astype(BF16), dc_new,
                       decay * dn8 + jnp.sum(iq * dden, axis=0, keepdims=True))

        _in_turn([head(h) for h in range(NH)])
        for h, (dgate, dq, dk, dv, dc_new, dn_new) in enumerate(done):
            cols = slice(h * dh, (h + 1) * dh)
            dg_ref[h] = dgate
            dq_ref[:, cols] = dq
            dk_ref[:, cols] = dk
            dv_ref[:, cols] = dv
            dc_s[h] = dc_new
            dn_s[h] = dn_new

    rc = lambda c: nc - 1 - c
    grp = lambda off: pl.BlockSpec((L, wm), lambda c: (rc(c), qoff + off))
    hm = pl.BlockSpec((L, wm), lambda c: (rc(c), 0))
    act = jax.ShapeDtypeStruct((t, wm), BF16)
    return pl.pallas_call(
        body, name="mlstm_bwd", grid=(nc,),
        in_specs=[grp(0), grp(1), grp(2),
                  pl.BlockSpec((L, LANE), lambda c: (rc(c), 0)),
                  pl.BlockSpec((1, 8, L), lambda c: (rc(c), 0, 0)),
                  hm, hm,
                  pl.BlockSpec((NH, None, dh, dh), lambda c: (0, rc(c), 0, 0)),
                  pl.BlockSpec((NH, None, 8, dh), lambda c: (0, rc(c), 0, 0))],
        out_specs=[hm, hm, hm, pl.BlockSpec((NH, L, LANE), lambda c: (0, rc(c), 0))],
        out_shape=[act, act, act, jax.ShapeDtypeStruct((NH, t, LANE), F32)],
        scratch_shapes=[pltpu.VMEM((NH, dh, dh), F32), pltpu.VMEM((NH, 8, dh), F32)],
        compiler_params=_params("arbitrary"),
    )(proj, proj, proj, gcol, grow, hval, dh_in, cs, ns)


def _head_norm(hv):
    mu = jnp.mean(hv, axis=1, keepdims=True)
    hc = hv - mu
    rstd = lax.rsqrt(jnp.mean(hc * hc, axis=1, keepdims=True) + HN_EPS)
    return hc * rstd, rstd


def _hnorm_fwd(hval, proj, gain, y, t, wc, dh, tr=512):
    ooff = 3 * wc // dh + 3 * NH
    tr = min(tr, t)

    def body(h_ref, o_ref, g_ref, y_in, y_ref):
        hhat, _ = _head_norm(h_ref[...])
        y_ref[...] = (_sigmoid(o_ref[...]) * hhat * g_ref[...]).astype(BF16)

    return pl.pallas_call(
        body, name="hnorm_fwd", grid=(t // tr, NH),
        in_specs=[pl.BlockSpec((tr, dh), lambda i, h: (i, h)),
                  pl.BlockSpec((tr, dh), lambda i, h: (i, ooff + h)),
                  pl.BlockSpec((1, dh), lambda i, h: (0, h)),
                  pl.BlockSpec(memory_space=pl.ANY)],
        out_specs=pl.BlockSpec((None, tr, dh), lambda i, h: (1, i, h)),
        out_shape=jax.ShapeDtypeStruct(y.shape, BF16),
        input_output_aliases={3: 0},
        compiler_params=_params("parallel", "parallel"),
    )(hval, proj, gain, y)


def _hnorm_bwd(dy, hval, proj, gain, t, wc, dh, tr=512):
    ooff = 3 * wc // dh + 3 * NH
    tr = min(tr, t)
    yoff = wc // dh

    def body(dy_ref, h_ref, o_ref, g_ref, do_ref, dh_ref, dg_ref):
        i = pl.program_id(1)
        hhat, rstd = _head_norm(h_ref[...])
        gain_v = g_ref[...]
        sig = _sigmoid(o_ref[...])
        d = dy_ref[...]
        do_ref[...] = (d * hhat * gain_v * sig * (1.0 - sig)).astype(BF16)
        dhn = d * sig
        part = jnp.sum(dhn * hhat, axis=0, keepdims=True)

        @pl.when(i == 0)
        def _():
            dg_ref[...] = part

        @pl.when(i > 0)
        def _():
            dg_ref[...] += part

        dhat = dhn * gain_v
        dh_ref[...] = rstd * (dhat - jnp.mean(dhat, axis=1, keepdims=True)
                              - hhat * jnp.mean(dhat * hhat, axis=1, keepdims=True))

    blk = lambda off: pl.BlockSpec((tr, dh), lambda h, i: (i, off + h))
    return pl.pallas_call(
        body, name="hnorm_bwd", grid=(NH, t // tr),
        in_specs=[blk(yoff), blk(0), blk(ooff), pl.BlockSpec((1, dh), lambda h, i: (0, h))],
        out_specs=[blk(0), blk(0), pl.BlockSpec((1, dh), lambda h, i: (0, h))],
        out_shape=[jax.ShapeDtypeStruct((t, NH * dh), BF16), jax.ShapeDtypeStruct((t, NH * dh), F32),
                   jax.ShapeDtypeStruct((1, NH * dh), F32)],
        compiler_params=_params("parallel", "arbitrary"),
    )(dy, hval, proj, gain)


def _ln_stats(z):
    mu = jnp.mean(z, axis=1, keepdims=True)
    zc = z - mu
    rstd = lax.rsqrt(jnp.mean(zc * zc, axis=1, keepdims=True) + LN_EPS)
    return zc * rstd, rstd


def _ln_bwd(dy, xhat, rstd, g):
    dxh = dy * g
    return rstd * (dxh - jnp.mean(dxh, axis=1, keepdims=True) - xhat * jnp.mean(dxh * xhat, axis=1, keepdims=True))


def _accum(ref, i, part):
    @pl.when(i == 0)
    def _():
        ref[...] = part

    @pl.when(i > 0)
    def _():
        ref[...] += part


def _ln1_fwd(x, mix, g, b, tr=256):
    t, d = x.shape

    def body(x_ref, m_ref, g_ref, b_ref, xh_ref, rs_ref, xb_ref):
        xhat, rstd = _ln_stats(ALPHA * x_ref[...] + m_ref[...])
        xh_ref[...] = xhat
        rs_ref[...] = rstd
        xb_ref[...] = (xhat * g_ref[...] + b_ref[...]).astype(BF16)

    row = pl.BlockSpec((tr, d), lambda i: (i, 0))
    vec = pl.BlockSpec((1, d), lambda i: (0, 0))
    return pl.pallas_call(
        body, name="ln1_fwd", grid=(t // tr,),
        in_specs=[row, row, vec, vec],
        out_specs=[row, pl.BlockSpec((tr, 1), lambda i: (i, 0)), row],
        out_shape=[jax.ShapeDtypeStruct((t, d), F32), jax.ShapeDtypeStruct((t, 1), F32),
                   jax.ShapeDtypeStruct((t, d), BF16)],
        compiler_params=_params("parallel"),
    )(x, mix, g, b)


def _ln2_loss(xhat1, g1, b1, ff, target, g2, b2, tr=256):
    t, d = ff.shape

    def body(xh_ref, g1_ref, b1_ref, f_ref, t_ref, g_ref, b_ref, dz_ref, dzb_ref, dg_ref, db_ref, l_ref):
        i = pl.program_id(0)
        x1 = xh_ref[...] * g1_ref[...] + b1_ref[...]
        xhat, rstd = _ln_stats(ALPHA * x1 + f_ref[...])
        gv = g_ref[...]
        e = xhat * gv + b_ref[...] - t_ref[...]
        lsum = jnp.sum(jnp.sum(e * e, axis=1, keepdims=True), axis=0, keepdims=True) * (0.5 / d)
        dy = e * (1.0 / d)
        _accum(dg_ref, i, jnp.sum(dy * xhat, axis=0, keepdims=True))
        _accum(db_ref, i, jnp.sum(dy, axis=0, keepdims=True))
        _accum(l_ref, i, jnp.broadcast_to(lsum, l_ref.shape))
        dz = _ln_bwd(dy, xhat, rstd, gv)
        dz_ref[...] = dz
        dzb_ref[...] = dz.astype(BF16)

    row = pl.BlockSpec((tr, d), lambda i: (i, 0))
    vec = pl.BlockSpec((1, d), lambda i: (0, 0))
    return pl.pallas_call(
        body, name="ln2_loss", grid=(t // tr,),
        in_specs=[row, vec, vec, row, row, vec, vec],
        out_specs=[row, row, vec, vec, pl.BlockSpec((8, LANE), lambda i: (0, 0))],
        out_shape=[jax.ShapeDtypeStruct((t, d), F32), jax.ShapeDtypeStruct((t, d), BF16),
                   jax.ShapeDtypeStruct((1, d), F32), jax.ShapeDtypeStruct((1, d), F32),
                   jax.ShapeDtypeStruct((8, LANE), F32)],
        compiler_params=_params("arbitrary"),
    )(xhat1, g1, b1, ff, target, g2, b2)


def _ln1_bwd(dz2, dffn, xhat1, rstd1, g1, tr=256):
    t, d = dz2.shape

    def body(a_ref, f_ref, xh_ref, rs_ref, g_ref, dz_ref, dzb_ref, dg_ref, db_ref):
        i = pl.program_id(0)
        dy = ALPHA * a_ref[...] + f_ref[...]
        xhat = xh_ref[...]
        _accum(dg_ref, i, jnp.sum(dy * xhat, axis=0, keepdims=True))
        _accum(db_ref, i, jnp.sum(dy, axis=0, keepdims=True))
        dz = _ln_bwd(dy, xhat, rs_ref[...], g_ref[...])
        dz_ref[...] = dz
        dzb_ref[...] = dz.astype(BF16)

    row = pl.BlockSpec((tr, d), lambda i: (i, 0))
    vec = pl.BlockSpec((1, d), lambda i: (0, 0))
    return pl.pallas_call(
        body, name="ln1_bwd", grid=(t // tr,),
        in_specs=[row, row, row, pl.BlockSpec((tr, 1), lambda i: (i, 0)), vec],
        out_specs=[row, row, vec, vec],
        out_shape=[jax.ShapeDtypeStruct((t, d), F32), jax.ShapeDtypeStruct((t, d), BF16),
                   jax.ShapeDtypeStruct((1, d), F32), jax.ShapeDtypeStruct((1, d), F32)],
        compiler_params=_params("arbitrary"),
    )(dz2, dffn, xhat1, rstd1, g1)


def _ffn_act_fwd(hid0, w_fc, b_fc, t, dff):
    nb = dff // LANE

    def body(hv_ref, hg_ref, wv_ref, wg_ref, bv_ref, bg_ref, a_ref):
        val = _conv(hv_ref[...], wv_ref[...]) + bv_ref[...]
        gate = _conv(hg_ref[...], wg_ref[...]) + bg_ref[...]
        a_ref[...] = (gate * _sigmoid(gate) * val).astype(BF16)

    col = lambda off: pl.BlockSpec((t, LANE), lambda j: (0, j + off))
    w3 = lambda off: pl.BlockSpec((3, LANE), lambda j: (0, j + off))
    w1 = lambda off: pl.BlockSpec((1, LANE), lambda j: (0, j + off))
    return pl.pallas_call(
        body, name="ffn_act_fwd", grid=(nb,),
        in_specs=[col(0), col(nb), w3(0), w3(nb), w1(0), w1(nb)],
        out_specs=col(0),
        out_shape=jax.ShapeDtypeStruct((t, dff), BF16),
        compiler_params=_params("parallel"),
    )(hid0, hid0, w_fc, w_fc, b_fc, b_fc)


def _ffn_act_bwd(da, hid0, w_fc, b_fc, t, dff):
    nb = dff // LANE

    def body(da_ref, hv_ref, hg_ref, wv_ref, wg_ref, bv_ref, bg_ref,
             dhv_ref, dhg_ref, dwv_ref, dwg_ref, dbv_ref, dbg_ref):
        hv, hg, wv, wg = hv_ref[...], hg_ref[...], wv_ref[...], wg_ref[...]
        rv, rg = _rolled(hv), _rolled(hg)
        val = _conv(hv, wv, rv) + bv_ref[...]
        gate = _conv(hg, wg, rg) + bg_ref[...]
        sig = _sigmoid(gate)
        d = da_ref[...]
        dsig = d * sig
        dval = dsig * gate
        dgate = dsig * val * (1.0 + gate * (1.0 - sig))
        dhv_ref[...] = _conv_t(dval, wv).astype(BF16)
        dhg_ref[...] = _conv_t(dgate, wg).astype(BF16)
        dwv_ref[...] = _conv_dw(dval, hv, rv)
        dwg_ref[...] = _conv_dw(dgate, hg, rg)
        dbv_ref[...] = jnp.sum(dval, axis=0, keepdims=True)
        dbg_ref[...] = jnp.sum(dgate, axis=0, keepdims=True)

    col = lambda off: pl.BlockSpec((t, LANE), lambda j: (0, j + off))
    w3 = lambda off: pl.BlockSpec((3, LANE), lambda j: (0, j + off))
    w1 = lambda off: pl.BlockSpec((1, LANE), lambda j: (0, j + off))
    s3 = jax.ShapeDtypeStruct((3, dff), F32)
    s1 = jax.ShapeDtypeStruct((1, dff), F32)
    return pl.pallas_call(
        body, name="ffn_act_bwd", grid=(nb,),
        in_specs=[col(0), col(0), col(nb), w3(0), w3(nb), w1(0), w1(nb)],
        out_specs=[col(0), col(0), w3(0), w3(0), w1(0), w1(0)],
        out_shape=[jax.ShapeDtypeStruct((t, dff), BF16)] * 2 + [s3, s3, s1, s1],
        compiler_params=_params("parallel"),
    )(da, hid0, hid0, w_fc, w_fc, b_fc, b_fc)


class _Ready:
    def __init__(self, **weights):
        self.weights = weights

    def begin(self, after):
        return None

    def forward(self, name, after):
        return None

    def get(self, name, after):
        return self.weights[name]


class _Kept:
    def __init__(self):
        self.grads = {}

    def start(self, name, grad):
        self.grads[name] = grad
        return None

    def relay(self, name, after):
        return None

    def meanwhile(self, small, loss, after):
        return None


def _behind(a, token):
    return a if token is None else a + token[0:1, 0:1].reshape((1,) * a.ndim)


def _local_step(x, target, w_in, b_gates, w_sc, gain, w_out, ln1_g, ln1_b, w_up, w_fc, b_fc, w_down, ln2_g, ln2_b,
                gx=None, wx=None, x_b=None):
    t, d = x.shape
    wc = d // 2
    dh = (d - wc) // NH
    wm = NH * dh
    dff = w_fc.shape[1] // 2
    if wx is None:
        wx = _Ready(w_out=w_out, w_up=w_up, w_down=w_down)
    ninp = 3 * wc + 4 * wm + LANE
    nin = 3 * wc + 4 * wm
    gate_tile = nin // LANE
    nc = t // CHUNK
    bias_tile = jnp.pad(b_gates, ((0, 0), (0, LANE - 2 * NH)))

    if x_b is None:
        x_b = x.astype(BF16)
    proj = _matmul(x_b, w_in, "nt", F32, "proj", tm=512, tn=2432, tk=d, n=ninp, after=wx.begin(w_in))
    y = _sconv_fwd(proj, w_sc, t, wc)
    gcol = _gates_prep(proj, bias_tile, t, gate_tile)
    grow = gcol[:, :8].T.reshape(8, nc, CHUNK).transpose(1, 0, 2)
    hval, cs, ns = _mlstm_fwd(proj, gcol, grow, t, wc, dh)
    y = _hnorm_fwd(hval, proj, gain, y, t, wc, dh)
    tok = wx.forward("w_out", y)
    w_out = wx.get("w_out", tok)
    mix = _matmul(y, w_out, "nn", F32, "out_proj", tm=512, tn=1024, tk=wc, a_blocked=True, after=tok)
    xhat1, rstd1, x1_b = _ln1_fwd(x, mix, _behind(ln1_g, wx.forward("w_up", mix)), ln1_b)
    w_up = wx.get("w_up", x1_b)
    wsl = w_up.shape[2]
    hid0 = _matmul(x1_b, w_up, "nn", F32, "ffn_up", tm=1024, tn=wsl, tk=d, b_blocked=True)
    act = _ffn_act_fwd(hid0, w_fc, _behind(b_fc, wx.forward("w_down", hid0)), t, dff)
    w_down = wx.get("w_down", act)
    ff = _matmul(act, w_down, "nn", F32, "ffn_down", tm=1024, tn=512, tk=dff)
    dz2, dz2_b, d_ln2_g, d_ln2_b, loss = _ln2_loss(xhat1, ln1_g, ln1_b, ff, target, ln2_g, ln2_b)

    if gx is None:
        gx = _Kept()
    d_w_down = _matmul(act, dz2_b, "tn", BF16, "ffn_down_dw", tm=1408, tn=1024, tk=t)
    d_act = _matmul(dz2_b, w_down, "nt", F32, "ffn_down_dx", tm=1024, tn=512, tk=d, after=gx.start("w_down", d_w_down))
    *d_hid0, dwv, dwg, dbv, dbg = _ffn_act_bwd(d_act, hid0, w_fc, _behind(b_fc, gx.relay("w_down", d_act)), t, dff)
    d_w_fc = jnp.concatenate([dwv, dwg], axis=1)
    d_b_fc = jnp.concatenate([dbv, dbg], axis=1)
    d_hid0 = tuple(d_hid0[:2])
    d_w_up = _matmul(x1_b, d_hid0, "tn", BF16, "ffn_up_dw", tm=512, tn=wsl, tk=t, o_width=wsl)
    d_x1_ffn = _matmul(d_hid0, w_up, "nt", F32, "ffn_up_dx", tm=1024, tn=1024, tk=wsl, b_blocked=True,
                       after=gx.start("w_up", d_w_up))
    dz1, dz1_b, d_ln1_g, d_ln1_b = _ln1_bwd(dz2, d_x1_ffn, xhat1, rstd1, _behind(ln1_g, gx.relay("w_up", d_x1_ffn)))

    d_w_out = _matmul(y, dz1_b, "tn", BF16, "out_proj_dw", tm=512, tn=1024, tk=t, a_blocked=True)
    dy = _matmul(dz1_b, w_out, "nt", F32, "out_proj_dx", tm=512, tn=1024, tk=d, after=gx.start("w_out", d_w_out))
    dcb, dcc, dch, d_w_sc = _sconv_bwd(dy, proj, _behind(w_sc, gx.relay("w_out", dy)), t, wc)
    d_o, d_hval, d_gain = _hnorm_bwd(dy, hval, proj, gain, t, wc, dh)
    dq, dk, dv, dgate = _mlstm_bwd(proj, gcol, grow, hval, d_hval, cs, ns, t, wc, dh)
    dgt, d_b_gates = _gates_bwd(dgate, proj, bias_tile, t, gate_tile)
    d_proj = jnp.concatenate([dcb, dcc, dch, dq, dk, dv, d_o, dgt], axis=1)
    d_w_in = _matmul(d_proj, x_b, "tn", BF16, "proj_dw", tm=2432, tn=1024, tk=t)
    small = dict(b_gates=d_b_gates[:, :2 * NH], w_sc_conv=d_w_sc, mh_gain=d_gain, ln1_g=d_ln1_g, ln1_b=d_ln1_b,
                 w_ffn_conv=d_w_fc, b_ffn_conv=d_b_fc, ln2_g=d_ln2_g, ln2_b=d_ln2_b)
    token = gx.start("w_in", d_w_in)
    token = gx.relay("w_in", gx.meanwhile(small, loss, token))
    grad_x = _matmul(d_proj, w_in, "nn", F32, "proj_dx", tm=512, tn=512, tk=ninp, add=dz1, add_scale=ALPHA, after=token)
    return loss, grad_x, small, gx


HBM = pl.BlockSpec(memory_space=pltpu.HBM)


def _place():
    return lax.axis_index("x"), lax.axis_index("y"), lax.axis_index("c")


def _index(p):
    return 4 * p[0] + 2 * p[1] + p[2]


def _all_gather(arrs, name):
    n = len(arrs)

    def body(*refs):
        ins, outs = refs[:n], refs[n:2 * n]
        send_sems, recv_sems, local_sems = refs[2 * n:]
        x, y, c = _place()
        me, sibling = (x, y, c), (x, y, 1 - c)
        chips = [(1 - x, y), (x, 1 - y), (1 - x, 1 - y)]

        def copy(a, k, block, to, own=False):
            dst = outs[a].at[_index(block)]
            return pltpu.make_async_remote_copy(
                src_ref=ins[a] if own else dst, dst_ref=dst,
                send_sem=send_sems.at[k * n + a], recv_sem=recv_sems.at[k * n + a],
                device_id=to, device_id_type=MESH)

        mine = [pltpu.make_async_copy(ins[a], outs[a].at[_index(me)], local_sems.at[a]) for a in range(n)]
        for cp in mine:
            cp.start()
        first = []
        for a in range(n):
            first.append(copy(a, 0, me, sibling, own=True))
            first += [copy(a, 1 + j, me, (*chip, c), own=True) for j, chip in enumerate(chips)]
        for cp in first:
            cp.start()
        passed = []
        for j, chip in enumerate(chips):
            for a in range(n):
                copy(a, 1 + j, (*chip, c), me).wait_recv()
                cp = copy(a, 4 + j, (*chip, c), sibling)
                cp.start()
                passed.append(cp)
        for a in range(n):
            copy(a, 0, sibling, me).wait_recv()
            for j, chip in enumerate(chips):
                copy(a, 4 + j, (*chip, 1 - c), me).wait_recv()
        for cp in first + passed:
            cp.wait_send()
        for cp in mine:
            cp.wait()

    return pl.pallas_call(
        body, name=name, in_specs=[HBM] * n, out_specs=[HBM] * n,
        out_shape=[jax.ShapeDtypeStruct((N_DEV,) + a.shape, a.dtype) for a in arrs],
        scratch_shapes=[pltpu.SemaphoreType.DMA((7 * n,)), pltpu.SemaphoreType.DMA((7 * n,)),
                        pltpu.SemaphoreType.DMA((n,))],
    )(*arrs)


SEM = pl.BlockSpec(memory_space=pltpu.SEMAPHORE)
EFFECT = pltpu.SideEffectType.DATAFLOW_SIDE_EFFECTING


def _chips(x, y):
    return [(1 - x, y), (x, 1 - y), (1 - x, 1 - y)]


N_CHIP = N_DEV // 2


def _pair_route(x, y, c):
    return [((x, y, 1 - c), 2 * q + (1 - c), q, q) for q in range(N_CHIP)]


def _chip_route(x, y, c):
    mine = 2 * x + y
    return [((*chip, c), 2 * chip[0] + chip[1], mine, 2 * chip[0] + chip[1]) for chip in _chips(x, y)]


def _exchange_pieces(g_ref, land_ref, width, tail):
    if not tail:
        return [(lambda i: g_ref.at[i], lambda s: land_ref.at[s])]
    rows = lambda i, n: pl.ds(pl.multiple_of(i * width, IN_TAIL), n)
    return [(lambda i: g_ref.at[rows(i, width), :], lambda s: land_ref.at[s, pl.ds(0, width), :]),
            (lambda i: g_ref.at[rows(i + 1, IN_TAIL), :], lambda s: land_ref.at[s, pl.ds(width, IN_TAIL), :])]


def _exchange_start(grad, route, tail, name):
    width = IN_SLAB if tail else grad.shape[1]
    n_p = 2 if tail else 1
    n_c = len(route(0, 0, 0))
    land_shape = (N_CHIP, width + (IN_TAIL if tail else 0), grad.shape[-1])

    def body(g_ref, land_ref, send_sems, recv_sems, g_thru, land_thru, token):
        for j, (peer, slab, slot, _) in enumerate(route(*_place())):
            for p, (src, dst) in enumerate(_exchange_pieces(g_ref, land_ref, width, tail)):
                pltpu.make_async_remote_copy(src_ref=src(slab), dst_ref=dst(slot), send_sem=send_sems.at[j * n_p + p],
                                             recv_sem=recv_sems.at[j * n_p + p], device_id=peer,
                                             device_id_type=MESH).start()
        token[...] = jnp.zeros_like(token)

    return pl.pallas_call(
        body, name=name,
        out_shape=(pltpu.SemaphoreType.DMA((n_c * n_p,)), pltpu.SemaphoreType.DMA((n_c * n_p,)),
                   pltpu.HBM(grad.shape, grad.dtype), pltpu.HBM(land_shape, grad.dtype),
                   jax.ShapeDtypeStruct((8, LANE), F32)),
        in_specs=(HBM, HBM), out_specs=(SEM, SEM, HBM, HBM, pl.BlockSpec(memory_space=pltpu.VMEM)),
        input_output_aliases={0: 2, 1: 3},
        compiler_params=pltpu.CompilerParams(has_side_effects=EFFECT),
    )(pltpu.with_memory_space_constraint(grad, pltpu.HBM),
      pltpu.with_memory_space_constraint(lax.empty(land_shape, grad.dtype), pltpu.HBM))


def _exchange_wait(send_sems, recv_sems, g_thru, land_thru, after, route, tail, name):
    width = IN_SLAB if tail else g_thru.shape[1]
    n_p = 2 if tail else 1

    def body(g_ref, land_ref, send_sems, recv_sems, after_ref, g_dead, got_ref):
        for j, (peer, slab, _, slot) in enumerate(route(*_place())):
            for p, (src, dst) in enumerate(_exchange_pieces(g_ref, land_ref, width, tail)):
                cp = pltpu.make_async_remote_copy(src_ref=src(slab), dst_ref=dst(slot),
                                                  send_sem=send_sems.at[j * n_p + p], recv_sem=recv_sems.at[j * n_p + p],
                                                  device_id=peer, device_id_type=MESH)
                cp.wait_send()
                cp.wait_recv()

    return pl.pallas_call(
        body, name=name,
        out_shape=(pltpu.HBM(g_thru.shape, g_thru.dtype), pltpu.HBM(land_thru.shape, land_thru.dtype)),
        in_specs=(HBM, HBM, SEM, SEM, pl.BlockSpec(memory_space=pl.ANY)), out_specs=(HBM, HBM),
        input_output_aliases={0: 0, 1: 1},
        compiler_params=pltpu.CompilerParams(has_side_effects=EFFECT),
    )(g_thru, land_thru, send_sems, recv_sems, after)


def _pair_add(grad, pair, core, tail, name):
    rows, cols = (IN_SLAB if tail else grad.shape[1]), grad.shape[-1]
    total = pair.shape[1]

    def body(core_ref, *refs):
        if tail:
            g_ref, t_ref, p_ref, o_ref = refs
            o_ref[0:rows, :] = (g_ref[...].astype(F32) + p_ref[0:rows, :].astype(F32)).astype(BF16)
            o_ref[rows:total, :] = (t_ref[...].astype(F32) + p_ref[rows:total, :].astype(F32)).astype(BF16)
        else:
            g_ref, p_ref, o_ref = refs
            o_ref[...] = (g_ref[...].astype(F32) + p_ref[...].astype(F32)).astype(BF16)

    if tail:
        tc = _fit(cols, 512)
        grid = (N_CHIP, cols // tc)
        slab = pl.BlockSpec((None, total, tc), lambda q, i, core_ref: (q, 0, i))
        per = IN_SLAB // IN_TAIL
        in_specs = [pl.BlockSpec((rows, tc), lambda q, i, core_ref: (2 * q + core_ref[0], i)),
                    pl.BlockSpec((IN_TAIL, tc), lambda q, i, core_ref: ((2 * q + core_ref[0] + 1) * per, i))]
    else:
        tr = _rows(rows, 1024)
        grid = (N_CHIP, rows // tr)
        slab = pl.BlockSpec((None, tr, cols), lambda q, i, core_ref: (q, i, 0))
        in_specs = [pl.BlockSpec((None, tr, cols), lambda q, i, core_ref: (2 * q + core_ref[0], i, 0))]
    return pl.pallas_call(
        body, name=name,
        grid_spec=pltpu.PrefetchScalarGridSpec(num_scalar_prefetch=1, grid=grid,
                                               in_specs=in_specs + [slab], out_specs=slab),
        out_shape=jax.ShapeDtypeStruct(pair.shape, BF16),
        compiler_params=_params("parallel", "parallel"),
    )(core, *([grad, grad] if tail else [grad]), pair)


def _gather_start(blocks, after, name, spare=()):
    n = len(blocks)
    lands = [(N_DEV + (a in spare),) + b.shape for a, b in enumerate(blocks)]

    def body(*refs):
        b_refs, land_refs = refs[:n], refs[n:2 * n]
        send_sems, recv_sems = refs[2 * n + 1:3 * n + 1], refs[3 * n + 1:4 * n + 1]
        token = refs[-1]
        x, y, c = _place()
        me = _index((x, y, c))
        for a in range(n):
            for k, to in enumerate([(x, y, 1 - c)] + [(*chip, c) for chip in _chips(x, y)]):
                pltpu.make_async_remote_copy(src_ref=b_refs[a], dst_ref=land_refs[a].at[me], send_sem=send_sems[a].at[k],
                                             recv_sem=recv_sems[a].at[k], device_id=to, device_id_type=MESH).start()
        token[...] = jnp.zeros_like(token)

    sems = [pltpu.SemaphoreType.DMA((4,))] * n
    out = pl.pallas_call(
        body, name=name,
        out_shape=(*sems, *sems, *[pltpu.HBM(b.shape, b.dtype) for b in blocks],
                   *[pltpu.HBM(s, b.dtype) for s, b in zip(lands, blocks)], jax.ShapeDtypeStruct((8, LANE), F32)),
        in_specs=(*[HBM] * (2 * n), pl.BlockSpec(memory_space=pl.ANY)),
        out_specs=(*[SEM] * (2 * n), *[HBM] * (2 * n), pl.BlockSpec(memory_space=pltpu.VMEM)),
        input_output_aliases={i: 2 * n + i for i in range(2 * n)},
        compiler_params=pltpu.CompilerParams(has_side_effects=EFFECT),
    )(*[pltpu.with_memory_space_constraint(b, pltpu.HBM) for b in blocks],
      *[pltpu.with_memory_space_constraint(lax.empty(s, b.dtype), pltpu.HBM) for s, b in zip(lands, blocks)], after)
    return [(out[a], out[n + a], out[2 * n + a], out[3 * n + a]) for a in range(n)], out[-1]


def _gather_forward(send_sems, recv_sems, b_thru, land_thru, after, name):
    def body(b_ref, land_ref, send_sems, recv_sems, after_ref, b_dead, land_out, send2, recv2, token):
        x, y, c = _place()
        sibling = (x, y, 1 - c)
        for k, frm in enumerate([sibling] + [(*chip, c) for chip in _chips(x, y)]):
            cp = pltpu.make_async_remote_copy(src_ref=b_ref, dst_ref=land_ref.at[_index(frm)], send_sem=send_sems.at[k],
                                              recv_sem=recv_sems.at[k], device_id=frm, device_id_type=MESH)
            cp.wait_send()
            cp.wait_recv()
        for j, chip in enumerate(_chips(x, y)):
            slot = land_ref.at[_index((*chip, c))]
            pltpu.make_async_remote_copy(src_ref=slot, dst_ref=slot, send_sem=send2.at[j], recv_sem=recv2.at[j],
                                         device_id=sibling, device_id_type=MESH).start()
        token[...] = jnp.zeros_like(token)

    return pl.pallas_call(
        body, name=name,
        out_shape=(pltpu.HBM(b_thru.shape, b_thru.dtype), pltpu.HBM(land_thru.shape, land_thru.dtype),
                   pltpu.SemaphoreType.DMA((3,)), pltpu.SemaphoreType.DMA((3,)), jax.ShapeDtypeStruct((8, LANE), F32)),
        in_specs=(HBM, HBM, SEM, SEM, pl.BlockSpec(memory_space=pl.ANY)),
        out_specs=(HBM, HBM, SEM, SEM, pl.BlockSpec(memory_space=pltpu.VMEM)),
        input_output_aliases={0: 0, 1: 1},
        compiler_params=pltpu.CompilerParams(has_side_effects=EFFECT),
    )(b_thru, land_thru, send_sems, recv_sems, after)


def _gather_finish(land_thru, send2, recv2, after, name):
    def body(land_ref, send2, recv2, after_ref, land_out):
        x, y, c = _place()
        for j, chip in enumerate(_chips(x, y)):
            cp = pltpu.make_async_remote_copy(src_ref=land_ref.at[_index((*chip, c))],
                                              dst_ref=land_ref.at[_index((*chip, 1 - c))], send_sem=send2.at[j],
                                              recv_sem=recv2.at[j], device_id=(x, y, 1 - c), device_id_type=MESH)
            cp.wait_send()
            cp.wait_recv()

    return pl.pallas_call(
        body, name=name, out_shape=pltpu.HBM(land_thru.shape, land_thru.dtype),
        in_specs=(HBM, SEM, SEM, pl.BlockSpec(memory_space=pl.ANY)), out_specs=HBM,
        input_output_aliases={0: 0},
        compiler_params=pltpu.CompilerParams(has_side_effects=EFFECT),
    )(land_thru, send2, recv2, after)


class _Gathering:
    def __init__(self, first, later, me):
        started, token = _gather_start(list(first.values()), next(iter(first.values())), "gather1_first", spare=(0,))
        cast = [_behind(a, token).astype(BF16) for a in later.values()]
        started_later, self.token = _gather_start(cast, token, "gather1_later")
        self.me, self.state = me, dict(zip([*first, *later], started + started_later))

    def begin(self, after):
        return self.token

    def forward(self, name, after):
        *self.state[name], token = _gather_forward(*self.state[name], after, "gather2_" + name)
        return token

    def get(self, name, after):
        block, land, send2, recv2 = self.state[name]
        land = _gather_finish(land, send2, recv2, after, "gather3_" + name)
        land = lax.dynamic_update_index_in_dim(land, block[None], self.me, 0)
        return land if name not in ("w_out", "w_down") else land.reshape(-1, land.shape[2])


class _Reducing:
    def __init__(self, core, chip, gather_small):
        self.core, self.chip, self.state, self.token, self.gather_small = core, chip, {}, None, gather_small

    def meanwhile(self, small, loss, after):
        self.small_sum = self.gather_small(small, loss, after)
        return self.small_sum

    def start(self, name, grad):
        tail = name == "w_in"
        g = grad if tail or grad.ndim == 3 else grad.reshape(N_DEV, grad.shape[0] // N_DEV, grad.shape[1])
        *self.state[name], token = _exchange_start(g, _pair_route, tail, "pair_send_" + name)
        return token

    def relay(self, name, after):
        tail = name == "w_in"
        grad, pair = _exchange_wait(*self.state[name], after, _pair_route, tail, "pair_recv_" + name)
        total = _pair_add(grad, pair, self.core, tail, "pair_add_" + name)
        *self.state[name], self.token = _exchange_start(total, _chip_route, False, "chip_send_" + name)
        return self.token

    def finish(self, name, after):
        total, land = _exchange_wait(*self.state[name], after, _chip_route, False, "chip_recv_" + name)
        own = lax.dynamic_index_in_dim(total, self.chip, 0, keepdims=True)
        return lax.dynamic_update_index_in_dim(land, own, self.chip, 0)


def _carry_w_in(main, tail):
    slabs, _, d = main.shape
    tc = _fit(d, 2048)
    assert slabs == N_DEV + 1 and tail.shape[:2] == (N_DEV, IN_TAIL), (main.shape, tail.shape)
    top = lambda off: pl.BlockSpec((None, IN_TAIL, tc), lambda s, j: (s + off, 0, j))

    def carry(m_ref, t_ref, o_ref):
        o_ref[...] = m_ref[...] + t_ref[...]

    main = pl.pallas_call(
        carry, name="carry_w_in", grid=(N_DEV - 1, d // tc), in_specs=[top(1), top(0)], out_specs=top(1),
        out_shape=jax.ShapeDtypeStruct(main.shape, main.dtype), input_output_aliases={0: 0},
        compiler_params=_params("parallel", "parallel"),
    )(main, tail)

    def last(m_ref, t_ref, o_ref):
        o_ref[...] = jnp.zeros_like(o_ref)
        o_ref[0:IN_TAIL, :] = t_ref[...]

    return pl.pallas_call(
        last, name="last_slab_w_in", grid=(d // tc,),
        in_specs=[pl.BlockSpec(memory_space=pl.ANY), pl.BlockSpec((None, IN_TAIL, tc), lambda j: (N_DEV - 1, 0, j))],
        out_specs=pl.BlockSpec((None, LANE, tc), lambda j: (N_DEV, 0, j)),
        out_shape=jax.ShapeDtypeStruct(main.shape, main.dtype), input_output_aliases={0: 0},
        compiler_params=_params("parallel"),
    )(main, tail)


def _rows(n, want):
    t = min(n, want)
    t -= t % 16
    while n % t:
        t -= 16
    return t


def _adam_math(w, g, m, v):
    m2 = ADAM_B1 * m + (1.0 - ADAM_B1) * g
    v2 = ADAM_B2 * v + (1.0 - ADAM_B2) * (g * g)
    m_hat = m2 * (1.0 / (1.0 - ADAM_B1 ** ADAM_STEP))
    v_hat = v2 * (1.0 / (1.0 - ADAM_B2 ** ADAM_STEP))
    return -ADAM_LR * (m_hat / (jnp.sqrt(v_hat) + ADAM_EPS) + ADAM_WD * w), m2, v2


def _slot_sum(r_ref):
    acc = r_ref[0].astype(F32)
    for i in range(1, r_ref.shape[0]):
        acc = acc + r_ref[i].astype(F32)
    return acc


def _shift_w_in(w):
    ws, d = w.shape
    tc = _fit(d, 256)

    def body(w_ref, main_ref, tail_ref, tall):
        tall[...] = jnp.zeros_like(tall)
        tall[0:ws, :] = w_ref[...]
        moved = pltpu.roll(tall[...], _index(_place()), 0).astype(BF16)
        main_ref[...] = moved[0:IN_SLAB]
        tail_ref[...] = moved[IN_SLAB:]

    return pl.pallas_call(
        body, name="shift_w_in", grid=(d // tc,),
        in_specs=[pl.BlockSpec((ws, tc), lambda j: (0, j))],
        out_specs=[pl.BlockSpec((IN_SLAB, tc), lambda j: (0, j)), pl.BlockSpec((IN_TAIL, tc), lambda j: (0, j))],
        out_shape=[jax.ShapeDtypeStruct((IN_SLAB, d), BF16), jax.ShapeDtypeStruct((IN_TAIL, d), BF16)],
        scratch_shapes=[pltpu.VMEM((IN_SLAB + IN_TAIL, tc), F32)], compiler_params=_params("parallel"),
    )(w)


def _sum_adamw_shifted(r, w, m, v, name):
    _, ph, d = r.shape
    ws = w.shape[0]
    tc = _fit(d, 256)

    def body(r_ref, w_ref, m_ref, v_ref, g_ref, d_ref, m2_ref, v2_ref, tall):
        tall[...] = pltpu.roll(_slot_sum(r_ref), lax.rem(ph - _index(_place()), ph), 0)
        g = tall[0:ws, :]
        g_ref[...] = g
        d_ref[...], m2_ref[...], v2_ref[...] = _adam_math(w_ref[...], g, m_ref[...], v_ref[...])

    blk = pl.BlockSpec((ws, tc), lambda j: (0, j))
    out = jax.ShapeDtypeStruct(w.shape, F32)
    return pl.pallas_call(
        body, name=name, grid=(d // tc,),
        in_specs=[pl.BlockSpec((r.shape[0], ph, tc), lambda j: (0, 0, j)), blk, blk, blk],
        out_specs=[blk] * 4, out_shape=[out] * 4,
        scratch_shapes=[pltpu.VMEM((ph, tc), F32)], compiler_params=_params("parallel"),
    )(r, w, m, v)


def _sum_slots(r, name, tr=128):
    _, rows, cols = r.shape
    tr = _rows(rows, tr)

    def body(r_ref, g_ref):
        g_ref[...] = _slot_sum(r_ref)

    return pl.pallas_call(
        body, name=name, grid=(rows // tr,),
        in_specs=[pl.BlockSpec((r.shape[0], tr, cols), lambda i: (0, i, 0))],
        out_specs=pl.BlockSpec((tr, cols), lambda i: (i, 0)),
        out_shape=jax.ShapeDtypeStruct((rows, cols), F32),
        compiler_params=_params("parallel"),
    )(r)


def _adamw(w, g, m, v, name, tr=256):
    rows, cols = w.shape
    tr = _rows(rows, tr)

    def body(w_ref, g_ref, m_ref, v_ref, d_ref, m2_ref, v2_ref):
        d_ref[...], m2_ref[...], v2_ref[...] = _adam_math(w_ref[...], g_ref[...], m_ref[...], v_ref[...])

    blk = pl.BlockSpec((tr, cols), lambda i: (i, 0))
    out = jax.ShapeDtypeStruct((rows, cols), F32)
    return pl.pallas_call(
        body, name=name, grid=(rows // tr,), in_specs=[blk] * 4, out_specs=[blk] * 3, out_shape=[out] * 3,
        compiler_params=_params("parallel"),
    )(w, g, m, v)


def _sum_adamw(r, w, m, v, name, tr=256):
    rows, cols = w.shape
    tr = _rows(rows, tr)

    def body(r_ref, w_ref, m_ref, v_ref, g_ref, d_ref, m2_ref, v2_ref):
        g = _slot_sum(r_ref)
        g_ref[...] = g
        d_ref[...], m2_ref[...], v2_ref[...] = _adam_math(w_ref[...], g, m_ref[...], v_ref[...])

    blk = pl.BlockSpec((tr, cols), lambda i: (i, 0))
    out = jax.ShapeDtypeStruct((rows, cols), F32)
    return pl.pallas_call(
        body, name=name, grid=(rows // tr,),
        in_specs=[pl.BlockSpec((r.shape[0], tr, cols), lambda i: (0, i, 0)), blk, blk, blk],
        out_specs=[blk] * 4, out_shape=[out] * 4,
        compiler_params=_params("parallel"),
    )(r, w, m, v)


def _pack(pieces, sizes):
    flat = [jnp.pad(p.reshape(-1).astype(F32), (0, s - p.size)) for p, s in zip(pieces, sizes)]
    total = sum(sizes)
    padded = -(-total // (16 * LANE)) * (16 * LANE)
    return jnp.pad(jnp.concatenate(flat), (0, padded - total)).reshape(-1, LANE)


def _unpack(packed, shapes, sizes):
    flat = packed.reshape(-1)
    out, off = [], 0
    for shp, s in zip(shapes, sizes):
        n = 1
        for k in shp:
            n *= k
        out.append(flat[off:off + n].reshape(shp))
        off += s
    return out


def _lanes(n):
    return -(-n // LANE) * LANE


WEIGHTS = ("w_in", "b_gates", "w_sc_conv", "mh_gain", "w_out", "ln1_g", "ln1_b", "w_up", "w_ffn_conv", "b_ffn_conv",
           "w_down", "ln2_g", "ln2_b")
BIG = ("w_in", "w_out", "w_up", "w_down")
SMALL = tuple(n for n in WEIGHTS if n not in BIG)


def kernel(x, w_in, b_gates, w_sc_conv, mh_gain, w_out, ln1_g, ln1_b, w_up, w_ffn_conv, b_ffn_conv, w_down, ln2_g, ln2_b, loss_target, m_w_in, m_b_gates, m_w_sc_conv, m_mh_gain, m_w_out, m_ln1_g, m_ln1_b, m_w_up, m_w_ffn_conv, m_b_ffn_conv, m_w_down, m_ln2_g, m_ln2_b, v_w_in, v_b_gates, v_w_sc_conv, v_mh_gain, v_w_out, v_ln1_g, v_ln1_b, v_w_up, v_w_ffn_conv, v_b_ffn_conv, v_w_down, v_ln2_g, v_ln2_b):
    w = dict(zip(WEIGHTS, (w_in, b_gates, w_sc_conv, mh_gain, w_out, ln1_g, ln1_b, w_up, w_ffn_conv, b_ffn_conv,
                           w_down, ln2_g, ln2_b)))
    m = dict(zip(WEIGHTS, (m_w_in, m_b_gates, m_w_sc_conv, m_mh_gain, m_w_out, m_ln1_g, m_ln1_b, m_w_up,
                           m_w_ffn_conv, m_b_ffn_conv, m_w_down, m_ln2_g, m_ln2_b)))
    v = dict(zip(WEIGHTS, (v_w_in, v_b_gates, v_w_sc_conv, v_mh_gain, v_w_out, v_ln1_g, v_ln1_b, v_w_up,
                           v_w_ffn_conv, v_b_ffn_conv, v_w_down, v_ln2_g, v_ln2_b)))
    me = _index(_place())
    d = x.shape[2]
    ws_in = w_in.shape[2]
    assert ws_in == IN_SLAB + 1 and N_DEV <= LANE, w_in.shape
    ninp = (N_DEV + 1) * IN_SLAB
    ws_sc, ws_fc = w_sc_conv.shape[2], w_ffn_conv.shape[2]
    w_in_t, m_in_t, v_in_t = (jnp.transpose(a[0]) for a in (w_in, m_w_in, v_w_in))

    w_in_main, w_in_tail = _shift_w_in(w_in_t)
    taps8 = lambda a: jnp.pad(a[0], ((0, 5), (0, 0)))
    at_once = ("w_in", "w_tail", "w_sc", "w_fc")
    wx = _Gathering(dict(zip(at_once, (w_in_main, w_in_tail, taps8(w_sc_conv), taps8(w_ffn_conv)))),
                    {n: w[n][0] for n in ("w_out", "w_up", "w_down")}, me)
    token = x_b = _behind(x[0], wx.begin(None)).astype(BF16)
    for n in at_once:
        token = wx.forward(n, token)
    g_in, g_tail, g_sc, g_fc = (wx.get(n, token) for n in at_once)
    w_in_full = _carry_w_in(g_in, g_tail).reshape(ninp, d)
    w_sc_full = g_sc[:, :3].transpose(1, 0, 2).reshape(3, N_DEV * ws_sc)
    w_fc_full = g_fc[:, :3].transpose(1, 0, 2).reshape(3, N_DEV * ws_fc)

    xi, yi, ci = _place()
    names = ("loss",) + SMALL
    pieces = {}

    def gather_small(small, loss_t, after):
        pieces.update(small, loss=loss_t[0, :1])
        sizes = [_lanes(pieces[n].size) for n in names]
        (g_small,) = _all_gather([_behind(_pack([pieces[n] for n in names], sizes), after)], "gather_small")
        return _sum_slots(g_small, "sum_small", tr=g_small.shape[1])

    gx = _Reducing(jnp.reshape(ci, (1,)).astype(jnp.int32), 2 * xi + yi, gather_small)
    loss_t, grad_x, small, _ = _local_step(
        x[0], loss_target[0], w_in_full, b_gates, w_sc_full, mh_gain, None, ln1_g, ln1_b, None,
        w_fc_full, b_ffn_conv, None, ln2_g, ln2_b, gx=gx, wx=wx, x_b=x_b)

    grads, deltas, new_m, new_v = {}, {}, {}, {}
    for name in ("w_down", "w_up", "w_out"):
        grads[name], deltas[name], new_m[name], new_v[name] = _sum_adamw(
            gx.finish(name, gx.token), w[name][0], m[name][0], v[name][0], "adamw_" + name)

    summed = _unpack(gx.small_sum, [pieces[n].shape for n in names], [_lanes(pieces[n].size) for n in names])
    full = dict(zip(names, summed))
    full["w_sc_conv"] = lax.dynamic_slice(full["w_sc_conv"], (0, me * ws_sc), (3, ws_sc))
    full["w_ffn_conv"] = lax.dynamic_slice(full["w_ffn_conv"], (0, me * ws_fc), (3, ws_fc))
    for n in SMALL:
        grads[n] = full[n].reshape(w[n].shape)
    sizes = [_lanes(w[n].size) for n in SMALL]
    shapes = [w[n].shape for n in SMALL]
    packed = [_pack([t[n] for n in SMALL], sizes) for t in (w, grads, m, v)]
    small_out = _adamw(*packed, "adamw_small")
    for res, t in zip(small_out, (deltas, new_m, new_v)):
        t.update(zip(SMALL, _unpack(res, shapes, sizes)))

    done = sum(t[0:1, 0:1] for t in (grad_x, deltas["w_down"], deltas["w_up"], deltas["w_out"], small_out[0]))
    grads["w_in"], deltas["w_in"], new_m["w_in"], new_v["w_in"] = (
        jnp.transpose(a)[None] for a in _sum_adamw_shifted(gx.finish("w_in", done), w_in_t, m_in_t, v_in_t, "adamw_w_in"))

    big = lambda t: {n: (t[n].reshape(w[n].shape) if n in BIG else t[n]) for n in WEIGHTS}
    grads, deltas, new_m, new_v = big(grads), big(deltas), big(new_m), big(new_v)
    return (full["loss"].reshape(()), grad_x[None], *[grads[n] for n in WEIGHTS], *[deltas[n] for n in WEIGHTS],
            *[new_m[n] for n in WEIGHTS], *[new_v[n] for n in WEIGHTS])
```

```python
import functools

import jax
import jax.numpy as jnp
from jax import lax
from jax.experimental import pallas as pl
from jax.experimental.pallas import tpu as pltpu

F32 = jnp.float32
BF16 = jnp.bfloat16
MESH = pl.DeviceIdType.MESH

N_DEV = 8
NH = 4
CHUNK = 64
LN_EPS = 1e-5
HN_EPS = 1e-6
ALPHA = 2.0 ** 0.25
LANE = 128
IN_SLAB = 7 * LANE
IN_TAIL = 16
VMEM_LIMIT = 56 * 1024 * 1024
ADAM_LR, ADAM_B1, ADAM_B2, ADAM_EPS, ADAM_WD, ADAM_STEP = 0.001, 0.9, 0.999, 1e-08, 0.01, 10

_NN = (((1,), (0,)), ((), ()))
_NT = (((1,), (1,)), ((), ()))
_TN = (((0,), (0,)), ((), ()))


def _dot(a, b, dn=_NN):
    return lax.dot_general(a, b, dn, preferred_element_type=F32)


def _params(*sem):
    return pltpu.CompilerParams(dimension_semantics=sem if sem else None, vmem_limit_bytes=VMEM_LIMIT)


def _iota(shape, axis):
    return lax.broadcasted_iota(jnp.int32, shape, axis)


def _fit(n, want):
    if n <= want:
        return n
    t = want - want % LANE
    while n % t:
        t -= LANE
    return t


def _matmul(a, b, mode, out_dtype, name, tm=1024, tn=512, tk=1024, add=None, add_scale=1.0,
            a_blocked=False, b_blocked=False, o_width=None, after=None, n=None):
    a_parts = a if isinstance(a, tuple) else None
    b_parts = b if isinstance(b, tuple) else None
    if a_parts:
        a_blocked, (a_rows, wa), na = True, a[0].shape, len(a)
        kd, m = (a_rows, na * wa) if mode == "tn" else (na * wa, a_rows)
    elif a_blocked:
        na, a_rows, wa = a.shape
        kd, m = (a_rows, na * wa) if mode == "tn" else (na * wa, a_rows)
    elif mode == "tn":
        kd, m = a.shape
    else:
        m, kd = a.shape
    if b_parts:
        b_blocked, (rows, w), nb = True, b[0].shape, len(b)
    elif b_blocked:
        nb, rows, w = b.shape
    if b_blocked:
        n = rows if mode == "nt" else nb * w
        assert (nb * w if mode == "nt" else rows) == kd, (name, kd)
    else:
        n = n or (b.shape[0] if mode == "nt" else b.shape[1])
    tm, tn, tk = _fit(m, tm), _fit(n, tn), _fit(kd, tk)
    if a_blocked and mode == "tn":
        tm = _fit(wa, tm)
    if a_blocked and mode != "tn":
        tk = _fit(wa, tk)
    if b_blocked and mode != "nt":
        tn = _fit(w, tn)
    if b_blocked and mode == "nt":
        tk = _fit(w, tk)
    if o_width is not None:
        tn = _fit(o_width, tn)
    assert m % tm == 0 and n % tn == 0 and kd % tk == 0, (name, m, n, kd, tm, tn, tk)
    assert not (a_blocked and mode != "tn" and wa % tk) and not (b_blocked and mode == "nt" and w % tk), (name, tk)
    nk = kd // tk
    dn = {"nn": _NN, "nt": _NT, "tn": _TN}[mode]
    if a_blocked and mode == "tn":
        a_per = wa // tm
        a_spec = pl.BlockSpec((None, tk, tm), lambda i, j, k: (i // a_per, k, i % a_per))
    elif a_blocked:
        a_per = wa // tk
        a_spec = pl.BlockSpec((None, tm, tk), lambda i, j, k: (k // a_per, i, k % a_per))
    elif mode == "tn":
        a_spec = pl.BlockSpec((tk, tm), lambda i, j, k: (k, i))
    else:
        a_spec = pl.BlockSpec((tm, tk), lambda i, j, k: (i, k))
    if b_blocked and mode != "nt":
        per = w // tn
        b_spec = pl.BlockSpec((None, tk, tn), lambda i, j, k: (j // per, k, j % per))
    elif b_blocked:
        per = w // tk
        b_spec = pl.BlockSpec((None, tn, tk), lambda i, j, k: (k // per, j, k % per))
    elif mode == "nt":
        b_spec = pl.BlockSpec((tn, tk), lambda i, j, k: (j, k))
    else:
        b_spec = pl.BlockSpec((tk, tn), lambda i, j, k: (k, j))
    if o_width is None:
        o_spec = pl.BlockSpec((tm, tn), lambda i, j, k: (i, j))
        o_shape = (m, n)
    else:
        oper = o_width // tn
        o_spec = pl.BlockSpec((None, tm, tn), lambda i, j, k: (j // oper, i, j % oper))
        o_shape = (n // o_width, m, o_width)
    a_list, a_specs = [a], [a_spec]
    if a_parts:
        hold = lambda x, s: jnp.clip(x - s * a_per, 0, a_per - 1)
        a_list = list(a_parts)
        a_specs = [(pl.BlockSpec((tk, tm), lambda i, j, k, s=s: (k, hold(i, s))) if mode == "tn"
                    else pl.BlockSpec((tm, tk), lambda i, j, k, s=s: (i, hold(k, s)))) for s in range(na)]
    b_list, b_specs = [b], [b_spec]
    if b_parts:
        hold_b = lambda x, s: jnp.clip(x - s * per, 0, per - 1)
        b_list = list(b_parts)
        b_specs = [(pl.BlockSpec((tn, tk), lambda i, j, k, s=s: (j, hold_b(k, s))) if mode == "nt"
                    else pl.BlockSpec((tk, tn), lambda i, j, k, s=s: (k, hold_b(j, s)))) for s in range(nb)]
    n_a, n_b = len(a_list), len(b_list)
    has_add = add is not None
    n_in = n_a + n_b + has_add + (after is not None)
    in_place = nk > 1 and out_dtype == F32

    def body(*refs):
        add_ref = refs[n_a + n_b] if has_add else None
        o_ref = refs[n_in]
        i, j, k = pl.program_id(0), pl.program_id(1), pl.program_id(2)

        def finish(r):
            if has_add:
                r = r + add_scale * add_ref[...]
            o_ref[...] = r.astype(out_dtype)

        def step(a_ref, b_ref):
            if nk == 1:
                finish(_dot(a_ref[...], b_ref[...], dn))
                return
            acc = o_ref if in_place else refs[-1]

            @pl.when(k == 0)
            def _():
                acc[...] = _dot(a_ref[...], b_ref[...], dn)

            @pl.when(k > 0)
            def _():
                acc[...] += _dot(a_ref[...], b_ref[...], dn)

        if n_a == 1 and n_b == 1:
            step(refs[0], refs[1])
        else:
            slab_a = ((i if mode == "tn" else k) // a_per) if n_a > 1 else 0
            slab_b = ((k if mode == "nt" else j) // per) if n_b > 1 else 0
            for sa in range(n_a):
                for sb in range(n_b):
                    pl.when((slab_a == sa) & (slab_b == sb))(functools.partial(step, refs[sa], refs[n_a + sb]))
        if nk > 1 and not (in_place and not has_add):
            @pl.when(k == nk - 1)
            def _():
                finish((o_ref if in_place else refs[-1])[...])

    in_specs = a_specs + b_specs + ([pl.BlockSpec((tm, tn), lambda i, j, k: (i, j))] if has_add else [])
    args = (*a_list, *b_list) + ((add,) if has_add else ())
    if after is not None:
        in_specs.append(pl.BlockSpec(memory_space=pl.ANY))
        args += (after,)
    return pl.pallas_call(
        body, name=name, grid=(m // tm, n // tn, nk),
        in_specs=in_specs, out_specs=o_spec,
        out_shape=jax.ShapeDtypeStruct(o_shape, out_dtype),
        scratch_shapes=[pltpu.VMEM((tm, tn), F32)] if nk > 1 and not in_place else [],
        compiler_params=_params("parallel", "parallel", "arbitrary"),
    )(*args)


def _shift_down(u, s):
    return jnp.where(_iota(u.shape, 0) >= s, pltpu.roll(u, s, 0), 0.0)


def _shift_up(u, s):
    t = u.shape[0]
    return jnp.where(_iota(u.shape, 0) < t - s, pltpu.roll(u, t - s, 0), 0.0)


SLAB = 8


def _rolled(u):
    return pltpu.roll(u, 2, 0), pltpu.roll(u, 1, 0)


def _conv(u, w, rolled=None):
    u2, u1 = _rolled(u) if rolled is None else rolled
    raw = w[0:1] * u2 + w[1:2] * u1 + w[2:3] * u
    head = u[0:SLAB]
    mended = w[0:1] * _shift_down(head, 2) + w[1:2] * _shift_down(head, 1) + w[2:3] * head
    return jnp.concatenate([mended, raw[SLAB:]], axis=0)


def _conv_t(dy, w):
    t = dy.shape[0]
    raw = w[2:3] * dy + w[1:2] * pltpu.roll(dy, t - 1, 0) + w[0:1] * pltpu.roll(dy, t - 2, 0)
    tail = dy[t - SLAB:]
    mended = w[2:3] * tail + w[1:2] * _shift_up(tail, 1) + w[0:1] * _shift_up(tail, 2)
    return jnp.concatenate([raw[:t - SLAB], mended], axis=0)


def _conv_dw(dy, u, rolled=None):
    t = dy.shape[0]
    u2, u1 = _rolled(u) if rolled is None else rolled
    head, tail = dy[0:SLAB], u[t - SLAB:]
    r = _iota(head.shape, 0)
    wrap2 = jnp.sum(jnp.where(r < 2, head * pltpu.roll(tail, 2, 0), 0.0), axis=0, keepdims=True)
    wrap1 = jnp.sum(jnp.where(r < 1, head * pltpu.roll(tail, 1, 0), 0.0), axis=0, keepdims=True)
    d0 = jnp.sum(dy * u2, axis=0, keepdims=True) - wrap2
    d1 = jnp.sum(dy * u1, axis=0, keepdims=True) - wrap1
    d2 = jnp.sum(dy * u, axis=0, keepdims=True)
    r3 = _iota((3, dy.shape[1]), 0)
    return jnp.where(r3 == 0, d0, jnp.where(r3 == 1, d1, d2))


def _sigmoid(x):
    return 0.5 * jnp.tanh(0.5 * x) + 0.5


def _sconv_fwd(proj, w_sc, t, wc):
    nb = wc // LANE

    def body(cb_ref, cc_ref, ch_ref, w_ref, y_ref):
        u = cc_ref[...] * ch_ref[...]
        y_ref[...] = (cb_ref[...] * _conv(u, w_ref[...])).astype(BF16)

    col = lambda off: pl.BlockSpec((t, LANE), lambda j: (0, j + off))
    return pl.pallas_call(
        body, name="sconv_fwd", grid=(nb,),
        in_specs=[col(0), col(nb), col(2 * nb), pl.BlockSpec((3, LANE), lambda j: (0, j))],
        out_specs=pl.BlockSpec((None, t, LANE), lambda j: (0, 0, j)),
        out_shape=jax.ShapeDtypeStruct((2, t, wc), BF16),
        compiler_params=_params("parallel"),
    )(proj, proj, proj, w_sc)


def _sconv_bwd(dy, proj, w_sc, t, wc):
    nb = wc // LANE

    def body(dy_ref, cb_ref, cc_ref, ch_ref, w_ref, dcb_ref, dcc_ref, dch_ref, dw_ref):
        cc, ch, w, d = cc_ref[...], ch_ref[...], w_ref[...], dy_ref[...]
        u = cc * ch
        ru = _rolled(u)
        dcb_ref[...] = (d * _conv(u, w, ru)).astype(BF16)
        dcu = d * cb_ref[...]
        dw_ref[...] = _conv_dw(dcu, u, ru)
        du = _conv_t(dcu, w)
        dcc_ref[...] = (du * ch).astype(BF16)
        dch_ref[...] = (du * cc).astype(BF16)

    col = lambda off: pl.BlockSpec((t, LANE), lambda j: (0, j + off))
    act = jax.ShapeDtypeStruct((t, wc), BF16)
    return pl.pallas_call(
        body, name="sconv_bwd", grid=(nb,),
        in_specs=[col(0), col(0), col(nb), col(2 * nb), pl.BlockSpec((3, LANE), lambda j: (0, j))],
        out_specs=[col(0), col(0), col(0), pl.BlockSpec((3, LANE), lambda j: (0, j))],
        out_shape=[act, act, act, jax.ShapeDtypeStruct((3, wc), F32)],
        compiler_params=_params("parallel"),
    )(dy, proj, proj, proj, w_sc)


def _gates_prep(proj, bias_tile, t, gate_tile):
    def body(g_ref, b_ref, o_ref):
        g = g_ref[...] + b_ref[...]
        lane = _iota(g.shape, 1)
        is_f = (lane >= NH) & (lane < 2 * NH)
        lf = jnp.minimum(g, 0.0) - jnp.log(1.0 + jnp.exp(-jnp.abs(g)))
        c = jnp.where(is_f, lf, 0.0)
        r = _iota(g.shape, 0) % CHUNK
        s = 1
        while s < CHUNK:
            c = c + jnp.where(r >= s, pltpu.roll(c, s, 0), 0.0)
            s *= 2
        o_ref[...] = jnp.where(is_f, c, jnp.where(lane < NH, g, 0.0))

    return pl.pallas_call(
        body, name="gates_prep", grid=(1,),
        in_specs=[pl.BlockSpec((t, LANE), lambda i: (0, gate_tile)), pl.BlockSpec((1, LANE), lambda i: (0, 0))],
        out_specs=pl.BlockSpec((t, LANE), lambda i: (0, 0)),
        out_shape=jax.ShapeDtypeStruct((t, LANE), F32),
        compiler_params=_params("arbitrary"),
    )(proj, bias_tile)


def _gates_bwd(dgate, proj, bias_tile, t, gate_tile):
    def body(dg_ref, g_ref, b_ref, o_ref, s_ref):
        g = g_ref[...] + b_ref[...]
        lane = _iota(g.shape, 1)
        r = _iota(g.shape, 0) % CHUNK
        dsig = 1.0 - _sigmoid(g)
        out = jnp.zeros(g.shape, F32)
        for h in range(NH):
            d = dg_ref[h]
            c = d
            s = 1
            while s < CHUNK:
                c = c + jnp.where(r + s < CHUNK, pltpu.roll(c, t - s, 0), 0.0)
                s *= 2
            di = jnp.broadcast_to(d[:, 0:1], g.shape)
            db = jnp.broadcast_to(c[:, 1:2], g.shape)
            out = out + jnp.where(lane == h, di, 0.0) + jnp.where(lane == NH + h, db * dsig, 0.0)
        o_ref[...] = out.astype(BF16)
        s_ref[...] = jnp.sum(out, axis=0, keepdims=True)

    return pl.pallas_call(
        body, name="gates_bwd", grid=(1,),
        in_specs=[pl.BlockSpec((NH, t, LANE), lambda i: (0, 0, 0)),
                  pl.BlockSpec((t, LANE), lambda i: (0, gate_tile)), pl.BlockSpec((1, LANE), lambda i: (0, 0))],
        out_specs=[pl.BlockSpec((t, LANE), lambda i: (0, 0)), pl.BlockSpec((1, LANE), lambda i: (0, 0))],
        out_shape=[jax.ShapeDtypeStruct((t, LANE), BF16), jax.ShapeDtypeStruct((1, LANE), F32)],
        compiler_params=_params("arbitrary"),
    )(dgate, proj, bias_tile)


def _in_turn(heads):
    while heads:
        heads = [g for g in heads if next(g, heads) is not heads]


def _chunk_gates(gc, gr, h, mprev):
    L = CHUNK
    icol, bcol = gc[:, h:h + 1], gc[:, h + NH:h + NH + 1]
    irow, brow = gr[h:h + 1, :], gr[h + NH:h + NH + 1, :]
    tri = _iota((L, L), 0) >= _iota((L, L), 1)
    log_d = jnp.where(tri, bcol - brow + irow, -jnp.inf)
    inter = bcol + mprev
    mt = jnp.maximum(inter, jnp.max(log_d, axis=1, keepdims=True))
    dw = jnp.exp(log_d - mt)
    iw = jnp.exp(inter - mt)
    g = brow[:, L - 1:L]
    wlog_col = g - bcol + icol
    wlog_row = g - brow + irow
    mnew = jnp.maximum(g + mprev, jnp.max(wlog_row, axis=1, keepdims=True))
    wcol = jnp.exp(wlog_col - mnew)
    decay = jnp.exp(g + mprev - mnew)
    return dw, iw, mt, wcol, decay, mnew


def _mlstm_fwd(proj, gcol, grow, t, wc, dh):
    nc = t // CHUNK
    wm = NH * dh
    assert wc == wm, (wc, wm)
    qoff = 3 * wc // wm
    scale = dh ** -0.5

    def body(q_ref, k_ref, v_ref, gc_ref, gr_ref, h_ref, cs_ref, ns_ref, c_s, n_s, m_s):
        @pl.when(pl.program_id(0) == 0)
        def _():
            c_s[...] = jnp.zeros_like(c_s)
            n_s[...] = jnp.zeros_like(n_s)
            m_s[...] = jnp.zeros_like(m_s)

        gc, gr = gc_ref[...], gr_ref[0]
        done = [None] * NH

        def head(h):
            cols = slice(h * dh, (h + 1) * dh)
            mprev = m_s[h, 0:1, 0:1]
            cprev = c_s[h]
            n8 = n_s[h]
            nprev = n8[0:1]
            qs = q_ref[:, cols] * scale
            k = k_ref[:, cols]
            qs_b, k_b, v_b = qs.astype(BF16), k.astype(BF16), v_ref[:, cols].astype(BF16)
            qk = _dot(qs_b, k_b, _NT)
            yield
            q_c = _dot(qs_b, cprev.astype(BF16))
            yield
            dw, iw, mt, wcol, decay, mnew = _chunk_gates(gc, gr, h, mprev)
            yield
            s = qk * dw
            wk = wcol * k
            num = _dot(s.astype(BF16), v_b) + iw * q_c
            yield
            c_new = decay * cprev + _dot(wk.astype(BF16), v_b, _TN)
            yield
            den = jnp.sum(s, axis=1, keepdims=True) + iw * jnp.sum(qs * nprev, axis=1, keepdims=True)
            done[h] = (cprev, jnp.where(_iota(n8.shape, 0) == 1, mprev, n8),
                       num / jnp.maximum(jnp.abs(den), jnp.exp(-mt)), c_new,
                       decay * n8 + jnp.sum(wk, axis=0, keepdims=True), mnew)

        _in_turn([head(h) for h in range(NH)])
        for h, (c_old, n_old, h_out, c_new, n_new, m_new) in enumerate(done):
            cs_ref[h] = c_old
            ns_ref[h] = n_old
            h_ref[:, h * dh:(h + 1) * dh] = h_out
            c_s[h] = c_new
            n_s[h] = n_new
            m_s[h] = jnp.broadcast_to(m_new, m_s.shape[1:])

    grp = lambda off: pl.BlockSpec((CHUNK, wm), lambda c: (c, qoff + off))
    return pl.pallas_call(
        body, name="mlstm_fwd", grid=(nc,),
        in_specs=[grp(0), grp(1), grp(2),
                  pl.BlockSpec((CHUNK, LANE), lambda c: (c, 0)),
                  pl.BlockSpec((1, 8, CHUNK), lambda c: (c, 0, 0))],
        out_specs=[pl.BlockSpec((CHUNK, wm), lambda c: (c, 0)),
                   pl.BlockSpec((NH, None, dh, dh), lambda c: (0, c, 0, 0)),
                   pl.BlockSpec((NH, None, 8, dh), lambda c: (0, c, 0, 0))],
        out_shape=[jax.ShapeDtypeStruct((t, wm), F32),
                   jax.ShapeDtypeStruct((NH, nc, dh, dh), F32),
                   jax.ShapeDtypeStruct((NH, nc, 8, dh), F32)],
        scratch_shapes=[pltpu.VMEM((NH, dh, dh), F32), pltpu.VMEM((NH, 8, dh), F32), pltpu.VMEM((NH, 8, LANE), F32)],
        compiler_params=_params("arbitrary"),
    )(proj, proj, proj, gcol, grow)


def _mlstm_bwd(proj, gcol, grow, hval, dh_in, cs, ns, t, wc, dh):
    nc = t // CHUNK
    wm = NH * dh
    assert wc == wm, (wc, wm)
    qoff = 3 * wc // wm
    scale = dh ** -0.5
    L = CHUNK

    def body(q_ref, k_ref, v_ref, gc_ref, gr_ref, h_ref, dh_ref, cs_ref, ns_ref,
             dq_ref, dk_ref, dv_ref, dg_ref, dc_s, dn_s):
        @pl.when(pl.program_id(0) == 0)
        def _():
            dc_s[...] = jnp.zeros_like(dc_s)
            dn_s[...] = jnp.zeros_like(dn_s)

        gc, gr = gc_ref[...], gr_ref[0]
        eye = _iota((L, L), 0) == _iota((L, L), 1)
        lane = _iota((L, LANE), 1)
        last = _iota((L, 1), 0) == L - 1
        done = [None] * NH

        def head(h):
            cols = slice(h * dh, (h + 1) * dh)
            ns8 = ns_ref[h]
            nprev = ns8[0:1]
            mprev = ns8[1:2, 0:1]
            cprev = cs_ref[h]
            dcn = dc_s[h]
            dn8 = dn_s[h]
            dnn = dn8[0:1]

            qs = q_ref[:, cols] * scale
            k = k_ref[:, cols]
            qs_b, k_b, v_b = qs.astype(BF16), k.astype(BF16), v_ref[:, cols].astype(BF16)
            qk = _dot(qs_b, k_b, _NT)
            yield
            dw, iw, mt, wcol, decay, _ = _chunk_gates(gc, gr, h, mprev)
            yield
            s = qk * dw
            den = jnp.sum(s, axis=1, keepdims=True) + iw * jnp.sum(qs * nprev, axis=1, keepdims=True)
            emt = jnp.exp(-mt)
            r = 1.0 / jnp.maximum(jnp.abs(den), emt)
            dout = dh_ref[:, cols]
            dnum = dout * r
            dden = (-jnp.sum(dout * h_ref[:, cols], axis=1, keepdims=True) * r
                    * jnp.where(jnp.abs(den) > emt, jnp.sign(den), 0.0))
            dnum_b = dnum.astype(BF16)
            cprev_b = cprev.astype(BF16)
            dcn_b = dcn.astype(BF16)
            yield

            g_raw = _dot(dnum_b, v_b, _NT)
            yield
            q_inter = _dot(dnum_b, cprev_b, _NT)
            yield
            k_raw = _dot(v_b, dcn_b, _NT)
            yield
            gd = (g_raw + dden) * dw
            gd_b = gd.astype(BF16)
            dqs_inter = iw * (q_inter + dden * nprev)
            dk_inter = wcol * (k_raw + dnn)
            wk = wcol * k
            iq = iw * qs
            dqs = _dot(gd_b, k_b) + dqs_inter
            yield
            dk = _dot(gd_b, qs_b, _TN) + dk_inter
            yield
            dv = _dot(s.astype(BF16), dnum_b, _TN) + _dot(wk.astype(BF16), dcn_b)
            yield
            dc_new = decay * dcn + _dot(iq.astype(BF16), dnum_b, _TN)
            yield

            e = gd * qk
            e_cols = jnp.sum(jnp.where(eye, jnp.sum(e, axis=0, keepdims=True), 0.0), axis=1, keepdims=True)
            yield
            k_inter = jnp.sum(k * dk_inter, axis=1, keepdims=True)
            rq = jnp.sum(e, axis=1, keepdims=True) + jnp.sum(qs * dqs_inter, axis=1, keepdims=True)
            rk = e_cols + k_inter
            hsum = jnp.sum(k_inter, axis=0, keepdims=True)
            jdec = decay * (jnp.sum(jnp.sum(dcn * cprev, axis=1, keepdims=True), axis=0, keepdims=True)
                            + jnp.sum(dnn * nprev, axis=1, keepdims=True))
            db = rq - rk + jnp.where(last, hsum + jdec, 0.0)
            done[h] = (jnp.where(lane == 0, rk, jnp.where(lane == 1, db, 0.0)),
                       (dqs * scale).astype(BF16), dk.astype(BF16), dv.astype(BF16), dc_new,
                       decay * dn8 + jnp.sum(iq * dden, axis=0, keepdims=True))

        _in_turn([head(h) for h in range(NH)])
        for h, (dgate, dq, dk, dv, dc_new, dn_new) in enumerate(done):
            cols = slice(h * dh, (h + 1) * dh)
            dg_ref[h] = dgate
            dq_ref[:, cols] = dq
            dk_ref[:, cols] = dk
            dv_ref[:, cols] = dv
            dc_s[h] = dc_new
            dn_s[h] = dn_new

    rc = lambda c: nc - 1 - c
    grp = lambda off: pl.BlockSpec((L, wm), lambda c: (rc(c), qoff + off))
    hm = pl.BlockSpec((L, wm), lambda c: (rc(c), 0))
    act = jax.ShapeDtypeStruct((t, wm), BF16)
    return pl.pallas_call(
        body, name="mlstm_bwd", grid=(nc,),
        in_specs=[grp(0), grp(1), grp(2),
                  pl.BlockSpec((L, LANE), lambda c: (rc(c), 0)),
                  pl.BlockSpec((1, 8, L), lambda c: (rc(c), 0, 0)),
                  hm, hm,
                  pl.BlockSpec((NH, None, dh, dh), lambda c: (0, rc(c), 0, 0)),
                  pl.BlockSpec((NH, None, 8, dh), lambda c: (0, rc(c), 0, 0))],
        out_specs=[hm, hm, hm, pl.BlockSpec((NH, L, LANE), lambda c: (0, rc(c), 0))],
        out_shape=[act, act, act, jax.ShapeDtypeStruct((NH, t, LANE), F32)],
        scratch_shapes=[pltpu.VMEM((NH, dh, dh), F32), pltpu.VMEM((NH, 8, dh), F32)],
        compiler_params=_params("arbitrary"),
    )(proj, proj, proj, gcol, grow, hval, dh_in, cs, ns)


def _head_norm(hv):
    mu = jnp.mean(hv, axis=1, keepdims=True)
    hc = hv - mu
    rstd = lax.rsqrt(jnp.mean(hc * hc, axis=1, keepdims=True) + HN_EPS)
    return hc * rstd, rstd


def _hnorm_fwd(hval, proj, gain, y, t, wc, dh, tr=512):
    ooff = 3 * wc // dh + 3 * NH
    tr = min(tr, t)

    def body(h_ref, o_ref, g_ref, y_in, y_ref):
        hhat, _ = _head_norm(h_ref[...])
        y_ref[...] = (_sigmoid(o_ref[...]) * hhat * g_ref[...]).astype(BF16)

    return pl.pallas_call(
        body, name="hnorm_fwd", grid=(t // tr, NH),
        in_specs=[pl.BlockSpec((tr, dh), lambda i, h: (i, h)),
                  pl.BlockSpec((tr, dh), lambda i, h: (i, ooff + h)),
                  pl.BlockSpec((1, dh), lambda i, h: (0, h)),
                  pl.BlockSpec(memory_space=pl.ANY)],
        out_specs=pl.BlockSpec((None, tr, dh), lambda i, h: (1, i, h)),
        out_shape=jax.ShapeDtypeStruct(y.shape, BF16),
        input_output_aliases={3: 0},
        compiler_params=_params("parallel", "parallel"),
    )(hval, proj, gain, y)


def _hnorm_bwd(dy, hval, proj, gain, t, wc, dh, tr=512):
    ooff = 3 * wc // dh + 3 * NH
    tr = min(tr, t)
    yoff = wc // dh

    def body(dy_ref, h_ref, o_ref, g_ref, do_ref, dh_ref, dg_ref):
        i = pl.program_id(1)
        hhat, rstd = _head_norm(h_ref[...])
        gain_v = g_ref[...]
        sig = _sigmoid(o_ref[...])
        d = dy_ref[...]
        do_ref[...] = (d * hhat * gain_v * sig * (1.0 - sig)).astype(BF16)
        dhn = d * sig
        part = jnp.sum(dhn * hhat, axis=0, keepdims=True)

        @pl.when(i == 0)
        def _():
            dg_ref[...] = part

        @pl.when(i > 0)
        def _():
            dg_ref[...] += part

        dhat = dhn * gain_v
        dh_ref[...] = rstd * (dhat - jnp.mean(dhat, axis=1, keepdims=True)
                              - hhat * jnp.mean(dhat * hhat, axis=1, keepdims=True))

    blk = lambda off: pl.BlockSpec((tr, dh), lambda h, i: (i, off + h))
    return pl.pallas_call(
        body, name="hnorm_bwd", grid=(NH, t // tr),
        in_specs=[blk(yoff), blk(0), blk(ooff), pl.BlockSpec((1, dh), lambda h, i: (0, h))],
        out_specs=[blk(0), blk(0), pl.BlockSpec((1, dh), lambda h, i: (0, h))],
        out_shape=[jax.ShapeDtypeStruct((t, NH * dh), BF16), jax.ShapeDtypeStruct((t, NH * dh), F32),
                   jax.ShapeDtypeStruct((1, NH * dh), F32)],
        compiler_params=_params("parallel", "arbitrary"),
    )(dy, hval, proj, gain)


def _ln_stats(z):
    mu = jnp.mean(z, axis=1, keepdims=True)
    zc = z - mu
    rstd = lax.rsqrt(jnp.mean(zc * zc, axis=1, keepdims=True) + LN_EPS)
    return zc * rstd, rstd


def _ln_bwd(dy, xhat, rstd, g):
    dxh = dy * g
    return rstd * (dxh - jnp.mean(dxh, axis=1, keepdims=True) - xhat * jnp.mean(dxh * xhat, axis=1, keepdims=True))


def _accum(ref, i, part):
    @pl.when(i == 0)
    def _():
        ref[...] = part

    @pl.when(i > 0)
    def _():
        ref[...] += part


def _ln1_fwd(x, mix, g, b, tr=256):
    t, d = x.shape

    def body(x_ref, m_ref, g_ref, b_ref, xh_ref, rs_ref, xb_ref):
        xhat, rstd = _ln_stats(ALPHA * x_ref[...] + m_ref[...])
        xh_ref[...] = xhat
        rs_ref[...] = rstd
        xb_ref[...] = (xhat * g_ref[...] + b_ref[...]).astype(BF16)

    row = pl.BlockSpec((tr, d), lambda i: (i, 0))
    vec = pl.BlockSpec((1, d), lambda i: (0, 0))
    return pl.pallas_call(
        body, name="ln1_fwd", grid=(t // tr,),
        in_specs=[row, row, vec, vec],
        out_specs=[row, pl.BlockSpec((tr, 1), lambda i: (i, 0)), row],
        out_shape=[jax.ShapeDtypeStruct((t, d), F32), jax.ShapeDtypeStruct((t, 1), F32),
                   jax.ShapeDtypeStruct((t, d), BF16)],
        compiler_params=_params("parallel"),
    )(x, mix, g, b)


def _ln2_loss(xhat1, g1, b1, ff, target, g2, b2, tr=256):
    t, d = ff.shape

    def body(xh_ref, g1_ref, b1_ref, f_ref, t_ref, g_ref, b_ref, dz_ref, dzb_ref, dg_ref, db_ref, l_ref):
        i = pl.program_id(0)
        x1 = xh_ref[...] * g1_ref[...] + b1_ref[...]
        xhat, rstd = _ln_stats(ALPHA * x1 + f_ref[...])
        gv = g_ref[...]
        e = xhat * gv + b_ref[...] - t_ref[...]
        lsum = jnp.sum(jnp.sum(e * e, axis=1, keepdims=True), axis=0, keepdims=True) * (0.5 / d)
        dy = e * (1.0 / d)
        _accum(dg_ref, i, jnp.sum(dy * xhat, axis=0, keepdims=True))
        _accum(db_ref, i, jnp.sum(dy, axis=0, keepdims=True))
        _accum(l_ref, i, jnp.broadcast_to(lsum, l_ref.shape))
        dz = _ln_bwd(dy, xhat, rstd, gv)
        dz_ref[...] = dz
        dzb_ref[...] = dz.astype(BF16)

    row = pl.BlockSpec((tr, d), lambda i: (i, 0))
    vec = pl.BlockSpec((1, d), lambda i: (0, 0))
    return pl.pallas_call(
        body, name="ln2_loss", grid=(t // tr,),
        in_specs=[row, vec, vec, row, row, vec, vec],
        out_specs=[row, row, vec, vec, pl.BlockSpec((8, LANE), lambda i: (0, 0))],
        out_shape=[jax.ShapeDtypeStruct((t, d), F32), jax.ShapeDtypeStruct((t, d), BF16),
                   jax.ShapeDtypeStruct((1, d), F32), jax.ShapeDtypeStruct((1, d), F32),
                   jax.ShapeDtypeStruct((8, LANE), F32)],
        compiler_params=_params("arbitrary"),
    )(xhat1, g1, b1, ff, target, g2, b2)


def _ln1_bwd(dz2, dffn, xhat1, rstd1, g1, tr=256):
    t, d = dz2.shape

    def body(a_ref, f_ref, xh_ref, rs_ref, g_ref, dz_ref, dzb_ref, dg_ref, db_ref):
        i = pl.program_id(0)
        dy = ALPHA * a_ref[...] + f_ref[...]
        xhat = xh_ref[...]
        _accum(dg_ref, i, jnp.sum(dy * xhat, axis=0, keepdims=True))
        _accum(db_ref, i, jnp.sum(dy, axis=0, keepdims=True))
        dz = _ln_bwd(dy, xhat, rs_ref[...], g_ref[...])
        dz_ref[...] = dz
        dzb_ref[...] = dz.astype(BF16)

    row = pl.BlockSpec((tr, d), lambda i: (i, 0))
    vec = pl.BlockSpec((1, d), lambda i: (0, 0))
    return pl.pallas_call(
        body, name="ln1_bwd", grid=(t // tr,),
        in_specs=[row, row, row, pl.BlockSpec((tr, 1), lambda i: (i, 0)), vec],
        out_specs=[row, row, vec, vec],
        out_shape=[jax.ShapeDtypeStruct((t, d), F32), jax.ShapeDtypeStruct((t, d), BF16),
                   jax.ShapeDtypeStruct((1, d), F32), jax.ShapeDtypeStruct((1, d), F32)],
        compiler_params=_params("arbitrary"),
    )(dz2, dffn, xhat1, rstd1, g1)


def _ffn_act_fwd(hid0, w_fc, b_fc, t, dff):
    nb = dff // LANE

    def body(hv_ref, hg_ref, wv_ref, wg_ref, bv_ref, bg_ref, a_ref):
        val = _conv(hv_ref[...], wv_ref[...]) + bv_ref[...]
        gate = _conv(hg_ref[...], wg_ref[...]) + bg_ref[...]
        a_ref[...] = (gate * _sigmoid(gate) * val).astype(BF16)

    col = lambda off: pl.BlockSpec((t, LANE), lambda j: (0, j + off))
    w3 = lambda off: pl.BlockSpec((3, LANE), lambda j: (0, j + off))
    w1 = lambda off: pl.BlockSpec((1, LANE), lambda j: (0, j + off))
    return pl.pallas_call(
        body, name="ffn_act_fwd", grid=(nb,),
        in_specs=[col(0), col(nb), w3(0), w3(nb), w1(0), w1(nb)],
        out_specs=col(0),
        out_shape=jax.ShapeDtypeStruct((t, dff), BF16),
        compiler_params=_params("parallel"),
    )(hid0, hid0, w_fc, w_fc, b_fc, b_fc)


def _ffn_act_bwd(da, hid0, w_fc, b_fc, t, dff):
    nb = dff // LANE

    def body(da_ref, hv_ref, hg_ref, wv_ref, wg_ref, bv_ref, bg_ref,
             dhv_ref, dhg_ref, dwv_ref, dwg_ref, dbv_ref, dbg_ref):
        hv, hg, wv, wg = hv_ref[...], hg_ref[...], wv_ref[...], wg_ref[...]
        rv, rg = _rolled(hv), _rolled(hg)
        val = _conv(hv, wv, rv) + bv_ref[...]
        gate = _conv(hg, wg, rg) + bg_ref[...]
        sig = _sigmoid(gate)
        d = da_ref[...]
        dsig = d * sig
        dval = dsig * gate
        dgate = dsig * val * (1.0 + gate * (1.0 - sig))
        dhv_ref[...] = _conv_t(dval, wv).astype(BF16)
        dhg_ref[...] = _conv_t(dgate, wg).astype(BF16)
        dwv_ref[...] = _conv_dw(dval, hv, rv)
        dwg_ref[...] = _conv_dw(dgate, hg, rg)
        dbv_ref[...] = jnp.sum(dval, axis=0, keepdims=True)
        dbg_ref[...] = jnp.sum(dgate, axis=0, keepdims=True)

    col = lambda off: pl.BlockSpec((t, LANE), lambda j: (0, j + off))
    w3 = lambda off: pl.BlockSpec((3, LANE), lambda j: (0, j + off))
    w1 = lambda off: pl.BlockSpec((1, LANE), lambda j: (0, j + off))
    s3 = jax.ShapeDtypeStruct((3, dff), F32)
    s1 = jax.ShapeDtypeStruct((1, dff), F32)
    return pl.pallas_call(
        body, name="ffn_act_bwd", grid=(nb,),
        in_specs=[col(0), col(0), col(nb), w3(0), w3(nb), w1(0), w1(nb)],
        out_specs=[col(0), col(0), w3(0), w3(0), w1(0), w1(0)],
        out_shape=[jax.ShapeDtypeStruct((t, dff), BF16)] * 2 + [s3, s3, s1, s1],
        compiler_params=_params("parallel"),
    )(da, hid0, hid0, w_fc, w_fc, b_fc, b_fc)


class _Ready:
    def __init__(self, **weights):
        self.weights = weights

    def begin(self, after):
        return None

    def forward(self, name, after):
        return None

    def get(self, name, after):
        return self.weights[name]


class _Kept:
    def __init__(self):
        self.grads = {}

    def start(self, name, grad):
        self.grads[name] = grad
        return None

    def relay(self, name, after):
        return None

    def meanwhile(self, small, loss, after):
        return None


def _behind(a, token):
    return a if token is None else a + token[0:1, 0:1].reshape((1,) * a.ndim)


def _local_step(x, target, w_in, b_gates, w_sc, gain, w_out, ln1_g, ln1_b, w_up, w_fc, b_fc, w_down, ln2_g, ln2_b,
                gx=None, wx=None, x_b=None):
    t, d = x.shape
    wc = d // 2
    dh = (d - wc) // NH
    wm = NH * dh
    dff = w_fc.shape[1] // 2
    if wx is None:
        wx = _Ready(w_out=w_out, w_up=w_up, w_down=w_down)
    ninp = 3 * wc + 4 * wm + LANE
    nin = 3 * wc + 4 * wm
    gate_tile = nin // LANE
    nc = t // CHUNK
    bias_tile = jnp.pad(b_gates, ((0, 0), (0, LANE - 2 * NH)))

    if x_b is None:
        x_b = x.astype(BF16)
    proj = _matmul(x_b, w_in, "nt", F32, "proj", tm=512, tn=2432, tk=d, n=ninp, after=wx.begin(w_in))
    y = _sconv_fwd(proj, w_sc, t, wc)
    gcol = _gates_prep(proj, bias_tile, t, gate_tile)
    grow = gcol[:, :8].T.reshape(8, nc, CHUNK).transpose(1, 0, 2)
    hval, cs, ns = _mlstm_fwd(proj, gcol, grow, t, wc, dh)
    y = _hnorm_fwd(hval, proj, gain, y, t, wc, dh)
    tok = wx.forward("w_out", y)
    w_out = wx.get("w_out", tok)
    mix = _matmul(y, w_out, "nn", F32, "out_proj", tm=512, tn=1024, tk=wc, a_blocked=True, after=tok)
    xhat1, rstd1, x1_b = _ln1_fwd(x, mix, _behind(ln1_g, wx.forward("w_up", mix)), ln1_b)
    w_up = wx.get("w_up", x1_b)
    wsl = w_up.shape[2]
    hid0 = _matmul(x1_b, w_up, "nn", F32, "ffn_up", tm=1024, tn=wsl, tk=d, b_blocked=True)
    act = _ffn_act_fwd(hid0, w_fc, _behind(b_fc, wx.forward("w_down", hid0)), t, dff)
    w_down = wx.get("w_down", act)
    ff = _matmul(act, w_down, "nn", F32, "ffn_down", tm=1024, tn=512, tk=dff)
    dz2, dz2_b, d_ln2_g, d_ln2_b, loss = _ln2_loss(xhat1, ln1_g, ln1_b, ff, target, ln2_g, ln2_b)

    if gx is None:
        gx = _Kept()
    d_w_down = _matmul(act, dz2_b, "tn", BF16, "ffn_down_dw", tm=1408, tn=1024, tk=t)
    d_act = _matmul(dz2_b, w_down, "nt", F32, "ffn_down_dx", tm=2048, tn=512, tk=d, after=gx.start("w_down", d_w_down))
    *d_hid0, dwv, dwg, dbv, dbg = _ffn_act_bwd(d_act, hid0, w_fc, _behind(b_fc, gx.relay("w_down", d_act)), t, dff)
    d_w_fc = jnp.concatenate([dwv, dwg], axis=1)
    d_b_fc = jnp.concatenate([dbv, dbg], axis=1)
    d_hid0 = tuple(d_hid0[:2])
    d_w_up = _matmul(x1_b, d_hid0, "tn", BF16, "ffn_up_dw", tm=1024, tn=wsl, tk=t, o_width=wsl)
    d_x1_ffn = _matmul(d_hid0, w_up, "nt", F32, "ffn_up_dx", tm=1024, tn=1024, tk=wsl, b_blocked=True,
                       after=gx.start("w_up", d_w_up))
    dz1, dz1_b, d_ln1_g, d_ln1_b = _ln1_bwd(dz2, d_x1_ffn, xhat1, rstd1, _behind(ln1_g, gx.relay("w_up", d_x1_ffn)))

    d_w_out = _matmul(y, dz1_b, "tn", BF16, "out_proj_dw", tm=1024, tn=1024, tk=t, a_blocked=True)
    dy = _matmul(dz1_b, w_out, "nt", F32, "out_proj_dx", tm=1024, tn=1024, tk=d, after=gx.start("w_out", d_w_out))
    dcb, dcc, dch, d_w_sc = _sconv_bwd(dy, proj, _behind(w_sc, gx.relay("w_out", dy)), t, wc)
    d_o, d_hval, d_gain = _hnorm_bwd(dy, hval, proj, gain, t, wc, dh)
    dq, dk, dv, dgate = _mlstm_bwd(proj, gcol, grow, hval, d_hval, cs, ns, t, wc, dh)
    dgt, d_b_gates = _gates_bwd(dgate, proj, bias_tile, t, gate_tile)
    d_proj = jnp.concatenate([dcb, dcc, dch, dq, dk, dv, d_o, dgt], axis=1)
    d_w_in = _matmul(d_proj, x_b, "tn", BF16, "proj_dw", tm=2432, tn=1024, tk=t)
    small = dict(b_gates=d_b_gates[:, :2 * NH], w_sc_conv=d_w_sc, mh_gain=d_gain, ln1_g=d_ln1_g, ln1_b=d_ln1_b,
                 w_ffn_conv=d_w_fc, b_ffn_conv=d_b_fc, ln2_g=d_ln2_g, ln2_b=d_ln2_b)
    token = gx.start("w_in", d_w_in)
    token = gx.relay("w_in", gx.meanwhile(small, loss, token))
    grad_x = _matmul(d_proj, w_in, "nn", F32, "proj_dx", tm=512, tn=512, tk=ninp, add=dz1, add_scale=ALPHA, after=token)
    return loss, grad_x, small, gx


HBM = pl.BlockSpec(memory_space=pltpu.HBM)


def _place():
    return lax.axis_index("x"), lax.axis_index("y"), lax.axis_index("c")


def _index(p):
    return 4 * p[0] + 2 * p[1] + p[2]


def _all_gather(arrs, name):
    n = len(arrs)

    def body(*refs):
        ins, outs = refs[:n], refs[n:2 * n]
        send_sems, recv_sems, local_sems = refs[2 * n:]
        x, y, c = _place()
        me, sibling = (x, y, c), (x, y, 1 - c)
        chips = [(1 - x, y), (x, 1 - y), (1 - x, 1 - y)]

        def copy(a, k, block, to, own=False):
            dst = outs[a].at[_index(block)]
            return pltpu.make_async_remote_copy(
                src_ref=ins[a] if own else dst, dst_ref=dst,
                send_sem=send_sems.at[k * n + a], recv_sem=recv_sems.at[k * n + a],
                device_id=to, device_id_type=MESH)

        mine = [pltpu.make_async_copy(ins[a], outs[a].at[_index(me)], local_sems.at[a]) for a in range(n)]
        for cp in mine:
            cp.start()
        first = []
        for a in range(n):
            first.append(copy(a, 0, me, sibling, own=True))
            first += [copy(a, 1 + j, me, (*chip, c), own=True) for j, chip in enumerate(chips)]
        for cp in first:
            cp.start()
        passed = []
        for j, chip in enumerate(chips):
            for a in range(n):
                copy(a, 1 + j, (*chip, c), me).wait_recv()
                cp = copy(a, 4 + j, (*chip, c), sibling)
                cp.start()
                passed.append(cp)
        for a in range(n):
            copy(a, 0, sibling, me).wait_recv()
            for j, chip in enumerate(chips):
                copy(a, 4 + j, (*chip, 1 - c), me).wait_recv()
        for cp in first + passed:
            cp.wait_send()
        for cp in mine:
            cp.wait()

    return pl.pallas_call(
        body, name=name, in_specs=[HBM] * n, out_specs=[HBM] * n,
        out_shape=[jax.ShapeDtypeStruct((N_DEV,) + a.shape, a.dtype) for a in arrs],
        scratch_shapes=[pltpu.SemaphoreType.DMA((7 * n,)), pltpu.SemaphoreType.DMA((7 * n,)),
                        pltpu.SemaphoreType.DMA((n,))],
    )(*arrs)


SEM = pl.BlockSpec(memory_space=pltpu.SEMAPHORE)
EFFECT = pltpu.SideEffectType.DATAFLOW_SIDE_EFFECTING


def _chips(x, y):
    return [(1 - x, y), (x, 1 - y), (1 - x, 1 - y)]


N_CHIP = N_DEV // 2


def _pair_route(x, y, c):
    return [((x, y, 1 - c), 2 * q + (1 - c), q, q) for q in range(N_CHIP)]


def _chip_route(x, y, c):
    mine = 2 * x + y
    return [((*chip, c), 2 * chip[0] + chip[1], mine, 2 * chip[0] + chip[1]) for chip in _chips(x, y)]


def _exchange_pieces(g_ref, land_ref, width, tail):
    if not tail:
        return [(lambda i: g_ref.at[i], lambda s: land_ref.at[s])]
    rows = lambda i, n: pl.ds(pl.multiple_of(i * width, IN_TAIL), n)
    return [(lambda i: g_ref.at[rows(i, width), :], lambda s: land_ref.at[s, pl.ds(0, width), :]),
            (lambda i: g_ref.at[rows(i + 1, IN_TAIL), :], lambda s: land_ref.at[s, pl.ds(width, IN_TAIL), :])]


def _exchange_start(grad, route, tail, name):
    width = IN_SLAB if tail else grad.shape[1]
    n_p = 2 if tail else 1
    n_c = len(route(0, 0, 0))
    land_shape = (N_CHIP, width + (IN_TAIL if tail else 0), grad.shape[-1])

    def body(g_ref, land_ref, send_sems, recv_sems, g_thru, land_thru, token):
        for j, (peer, slab, slot, _) in enumerate(route(*_place())):
            for p, (src, dst) in enumerate(_exchange_pieces(g_ref, land_ref, width, tail)):
                pltpu.make_async_remote_copy(src_ref=src(slab), dst_ref=dst(slot), send_sem=send_sems.at[j * n_p + p],
                                             recv_sem=recv_sems.at[j * n_p + p], device_id=peer,
                                             device_id_type=MESH).start()
        token[...] = jnp.zeros_like(token)

    return pl.pallas_call(
        body, name=name,
        out_shape=(pltpu.SemaphoreType.DMA((n_c * n_p,)), pltpu.SemaphoreType.DMA((n_c * n_p,)),
                   pltpu.HBM(grad.shape, grad.dtype), pltpu.HBM(land_shape, grad.dtype),
                   jax.ShapeDtypeStruct((8, LANE), F32)),
        in_specs=(HBM, HBM), out_specs=(SEM, SEM, HBM, HBM, pl.BlockSpec(memory_space=pltpu.VMEM)),
        input_output_aliases={0: 2, 1: 3},
        compiler_params=pltpu.CompilerParams(has_side_effects=EFFECT),
    )(pltpu.with_memory_space_constraint(grad, pltpu.HBM),
      pltpu.with_memory_space_constraint(lax.empty(land_shape, grad.dtype), pltpu.HBM))


def _exchange_wait(send_sems, recv_sems, g_thru, land_thru, after, route, tail, name):
    width = IN_SLAB if tail else g_thru.shape[1]
    n_p = 2 if tail else 1

    def body(g_ref, land_ref, send_sems, recv_sems, after_ref, g_dead, got_ref):
        for j, (peer, slab, _, slot) in enumerate(route(*_place())):
            for p, (src, dst) in enumerate(_exchange_pieces(g_ref, land_ref, width, tail)):
                cp = pltpu.make_async_remote_copy(src_ref=src(slab), dst_ref=dst(slot),
                                                  send_sem=send_sems.at[j * n_p + p], recv_sem=recv_sems.at[j * n_p + p],
                                                  device_id=peer, device_id_type=MESH)
                cp.wait_send()
                cp.wait_recv()

    return pl.pallas_call(
        body, name=name,
        out_shape=(pltpu.HBM(g_thru.shape, g_thru.dtype), pltpu.HBM(land_thru.shape, land_thru.dtype)),
        in_specs=(HBM, HBM, SEM, SEM, pl.BlockSpec(memory_space=pl.ANY)), out_specs=(HBM, HBM),
        input_output_aliases={0: 0, 1: 1},
        compiler_params=pltpu.CompilerParams(has_side_effects=EFFECT),
    )(g_thru, land_thru, send_sems, recv_sems, after)


def _pair_add(grad, pair, core, tail, name):
    rows, cols = (IN_SLAB if tail else grad.shape[1]), grad.shape[-1]
    total = pair.shape[1]

    def body(core_ref, *refs):
        if tail:
            g_ref, t_ref, p_ref, o_ref = refs
            o_ref[0:rows, :] = (g_ref[...].astype(F32) + p_ref[0:rows, :].astype(F32)).astype(BF16)
            o_ref[rows:total, :] = (t_ref[...].astype(F32) + p_ref[rows:total, :].astype(F32)).astype(BF16)
        else:
            g_ref, p_ref, o_ref = refs
            o_ref[...] = (g_ref[...].astype(F32) + p_ref[...].astype(F32)).astype(BF16)

    if tail:
        tc = _fit(cols, 512)
        grid = (N_CHIP, cols // tc)
        slab = pl.BlockSpec((None, total, tc), lambda q, i, core_ref: (q, 0, i))
        per = IN_SLAB // IN_TAIL
        in_specs = [pl.BlockSpec((rows, tc), lambda q, i, core_ref: (2 * q + core_ref[0], i)),
                    pl.BlockSpec((IN_TAIL, tc), lambda q, i, core_ref: ((2 * q + core_ref[0] + 1) * per, i))]
    else:
        tr = _rows(rows, 1024)
        grid = (N_CHIP, rows // tr)
        slab = pl.BlockSpec((None, tr, cols), lambda q, i, core_ref: (q, i, 0))
        in_specs = [pl.BlockSpec((None, tr, cols), lambda q, i, core_ref: (2 * q + core_ref[0], i, 0))]
    return pl.pallas_call(
        body, name=name,
        grid_spec=pltpu.PrefetchScalarGridSpec(num_scalar_prefetch=1, grid=grid,
                                               in_specs=in_specs + [slab], out_specs=slab),
        out_shape=jax.ShapeDtypeStruct(pair.shape, BF16),
        compiler_params=_params("parallel", "parallel"),
    )(core, *([grad, grad] if tail else [grad]), pair)


def _gather_start(blocks, after, name, spare=()):
    n = len(blocks)
    lands = [(N_DEV + (a in spare),) + b.shape for a, b in enumerate(blocks)]

    def body(*refs):
        b_refs, land_refs = refs[:n], refs[n:2 * n]
        send_sems, recv_sems = refs[2 * n + 1:3 * n + 1], refs[3 * n + 1:4 * n + 1]
        token = refs[-1]
        x, y, c = _place()
        me = _index((x, y, c))
        for a in range(n):
            for k, to in enumerate([(x, y, 1 - c)] + [(*chip, c) for chip in _chips(x, y)]):
                pltpu.make_async_remote_copy(src_ref=b_refs[a], dst_ref=land_refs[a].at[me], send_sem=send_sems[a].at[k],
                                             recv_sem=recv_sems[a].at[k], device_id=to, device_id_type=MESH).start()
        token[...] = jnp.zeros_like(token)

    sems = [pltpu.SemaphoreType.DMA((4,))] * n
    out = pl.pallas_call(
        body, name=name,
        out_shape=(*sems, *sems, *[pltpu.HBM(b.shape, b.dtype) for b in blocks],
                   *[pltpu.HBM(s, b.dtype) for s, b in zip(lands, blocks)], jax.ShapeDtypeStruct((8, LANE), F32)),
        in_specs=(*[HBM] * (2 * n), pl.BlockSpec(memory_space=pl.ANY)),
        out_specs=(*[SEM] * (2 * n), *[HBM] * (2 * n), pl.BlockSpec(memory_space=pltpu.VMEM)),
        input_output_aliases={i: 2 * n + i for i in range(2 * n)},
        compiler_params=pltpu.CompilerParams(has_side_effects=EFFECT),
    )(*[pltpu.with_memory_space_constraint(b, pltpu.HBM) for b in blocks],
      *[pltpu.with_memory_space_constraint(lax.empty(s, b.dtype), pltpu.HBM) for s, b in zip(lands, blocks)], after)
    return [(out[a], out[n + a], out[2 * n + a], out[3 * n + a]) for a in range(n)], out[-1]


def _gather_forward(send_sems, recv_sems, b_thru, land_thru, after, name):
    def body(b_ref, land_ref, send_sems, recv_sems, after_ref, b_dead, land_out, send2, recv2, token):
        x, y, c = _place()
        sibling = (x, y, 1 - c)
        for k, frm in enumerate([sibling] + [(*chip, c) for chip in _chips(x, y)]):
            cp = pltpu.make_async_remote_copy(src_ref=b_ref, dst_ref=land_ref.at[_index(frm)], send_sem=send_sems.at[k],
                                              recv_sem=recv_sems.at[k], device_id=frm, device_id_type=MESH)
            cp.wait_send()
            cp.wait_recv()
        for j, chip in enumerate(_chips(x, y)):
            slot = land_ref.at[_index((*chip, c))]
            pltpu.make_async_remote_copy(src_ref=slot, dst_ref=slot, send_sem=send2.at[j], recv_sem=recv2.at[j],
                                         device_id=sibling, device_id_type=MESH).start()
        token[...] = jnp.zeros_like(token)

    return pl.pallas_call(
        body, name=name,
        out_shape=(pltpu.HBM(b_thru.shape, b_thru.dtype), pltpu.HBM(land_thru.shape, land_thru.dtype),
                   pltpu.SemaphoreType.DMA((3,)), pltpu.SemaphoreType.DMA((3,)), jax.ShapeDtypeStruct((8, LANE), F32)),
        in_specs=(HBM, HBM, SEM, SEM, pl.BlockSpec(memory_space=pl.ANY)),
        out_specs=(HBM, HBM, SEM, SEM, pl.BlockSpec(memory_space=pltpu.VMEM)),
        input_output_aliases={0: 0, 1: 1},
        compiler_params=pltpu.CompilerParams(has_side_effects=EFFECT),
    )(b_thru, land_thru, send_sems, recv_sems, after)


def _gather_finish(land_thru, send2, recv2, after, name):
    def body(land_ref, send2, recv2, after_ref, land_out):
        x, y, c = _place()
        for j, chip in enumerate(_chips(x, y)):
            cp = pltpu.make_async_remote_copy(src_ref=land_ref.at[_index((*chip, c))],
                                              dst_ref=land_ref.at[_index((*chip, 1 - c))], send_sem=send2.at[j],
                                              recv_sem=recv2.at[j], device_id=(x, y, 1 - c), device_id_type=MESH)
            cp.wait_send()
            cp.wait_recv()

    return pl.pallas_call(
        body, name=name, out_shape=pltpu.HBM(land_thru.shape, land_thru.dtype),
        in_specs=(HBM, SEM, SEM, pl.BlockSpec(memory_space=pl.ANY)), out_specs=HBM,
        input_output_aliases={0: 0},
        compiler_params=pltpu.CompilerParams(has_side_effects=EFFECT),
    )(land_thru, send2, recv2, after)


class _Gathering:
    def __init__(self, first, later, me):
        started, token = _gather_start(list(first.values()), next(iter(first.values())), "gather1_first", spare=(0,))
        cast = [_behind(a, token).astype(BF16) for a in later.values()]
        started_later, self.token = _gather_start(cast, token, "gather1_later")
        self.me, self.state = me, dict(zip([*first, *later], started + started_later))

    def begin(self, after):
        return self.token

    def forward(self, name, after):
        *self.state[name], token = _gather_forward(*self.state[name], after, "gather2_" + name)
        return token

    def get(self, name, after):
        block, land, send2, recv2 = self.state[name]
        land = _gather_finish(land, send2, recv2, after, "gather3_" + name)
        land = lax.dynamic_update_index_in_dim(land, block[None], self.me, 0)
        return land if name not in ("w_out", "w_down") else land.reshape(-1, land.shape[2])


class _Reducing:
    def __init__(self, core, chip, gather_small):
        self.core, self.chip, self.state, self.token, self.gather_small = core, chip, {}, None, gather_small

    def meanwhile(self, small, loss, after):
        self.small_sum = self.gather_small(small, loss, after)
        return self.small_sum

    def start(self, name, grad):
        tail = name == "w_in"
        g = grad if tail or grad.ndim == 3 else grad.reshape(N_DEV, grad.shape[0] // N_DEV, grad.shape[1])
        *self.state[name], token = _exchange_start(g, _pair_route, tail, "pair_send_" + name)
        return token

    def relay(self, name, after):
        tail = name == "w_in"
        grad, pair = _exchange_wait(*self.state[name], after, _pair_route, tail, "pair_recv_" + name)
        total = _pair_add(grad, pair, self.core, tail, "pair_add_" + name)
        *self.state[name], self.token = _exchange_start(total, _chip_route, False, "chip_send_" + name)
        return self.token

    def finish(self, name, after):
        total, land = _exchange_wait(*self.state[name], after, _chip_route, False, "chip_recv_" + name)
        own = lax.dynamic_index_in_dim(total, self.chip, 0, keepdims=True)
        return lax.dynamic_update_index_in_dim(land, own, self.chip, 0)


def _carry_w_in(main, tail):
    slabs, _, d = main.shape
    tc = _fit(d, 2048)
    assert slabs == N_DEV + 1 and tail.shape[:2] == (N_DEV, IN_TAIL), (main.shape, tail.shape)
    top = lambda off: pl.BlockSpec((None, IN_TAIL, tc), lambda s, j: (s + off, 0, j))

    def carry(m_ref, t_ref, o_ref):
        o_ref[...] = m_ref[...] + t_ref[...]

    main = pl.pallas_call(
        carry, name="carry_w_in", grid=(N_DEV - 1, d // tc), in_specs=[top(1), top(0)], out_specs=top(1),
        out_shape=jax.ShapeDtypeStruct(main.shape, main.dtype), input_output_aliases={0: 0},
        compiler_params=_params("parallel", "parallel"),
    )(main, tail)

    def last(m_ref, t_ref, o_ref):
        o_ref[...] = jnp.zeros_like(o_ref)
        o_ref[0:IN_TAIL, :] = t_ref[...]

    return pl.pallas_call(
        last, name="last_slab_w_in", grid=(d // tc,),
        in_specs=[pl.BlockSpec(memory_space=pl.ANY), pl.BlockSpec((None, IN_TAIL, tc), lambda j: (N_DEV - 1, 0, j))],
        out_specs=pl.BlockSpec((None, LANE, tc), lambda j: (N_DEV, 0, j)),
        out_shape=jax.ShapeDtypeStruct(main.shape, main.dtype), input_output_aliases={0: 0},
        compiler_params=_params("parallel"),
    )(main, tail)


def _rows(n, want):
    t = min(n, want)
    t -= t % 16
    while n % t:
        t -= 16
    return t


def _adam_math(w, g, m, v):
    m2 = ADAM_B1 * m + (1.0 - ADAM_B1) * g
    v2 = ADAM_B2 * v + (1.0 - ADAM_B2) * (g * g)
    m_hat = m2 * (1.0 / (1.0 - ADAM_B1 ** ADAM_STEP))
    v_hat = v2 * (1.0 / (1.0 - ADAM_B2 ** ADAM_STEP))
    return -ADAM_LR * (m_hat / (jnp.sqrt(v_hat) + ADAM_EPS) + ADAM_WD * w), m2, v2


def _slot_sum(r_ref):
    acc = r_ref[0].astype(F32)
    for i in range(1, r_ref.shape[0]):
        acc = acc + r_ref[i].astype(F32)
    return acc


def _shift_w_in(w):
    ws, d = w.shape
    tc = _fit(d, 256)

    def body(w_ref, main_ref, tail_ref, tall):
        tall[...] = jnp.zeros_like(tall)
        tall[0:ws, :] = w_ref[...]
        moved = pltpu.roll(tall[...], _index(_place()), 0).astype(BF16)
        main_ref[...] = moved[0:IN_SLAB]
        tail_ref[...] = moved[IN_SLAB:]

    return pl.pallas_call(
        body, name="shift_w_in", grid=(d // tc,),
        in_specs=[pl.BlockSpec((ws, tc), lambda j: (0, j))],
        out_specs=[pl.BlockSpec((IN_SLAB, tc), lambda j: (0, j)), pl.BlockSpec((IN_TAIL, tc), lambda j: (0, j))],
        out_shape=[jax.ShapeDtypeStruct((IN_SLAB, d), BF16), jax.ShapeDtypeStruct((IN_TAIL, d), BF16)],
        scratch_shapes=[pltpu.VMEM((IN_SLAB + IN_TAIL, tc), F32)], compiler_params=_params("parallel"),
    )(w)


def _sum_adamw_shifted(r, w, m, v, name):
    _, ph, d = r.shape
    ws = w.shape[0]
    tc = _fit(d, 256)

    def body(r_ref, w_ref, m_ref, v_ref, g_ref, d_ref, m2_ref, v2_ref, tall):
        tall[...] = pltpu.roll(_slot_sum(r_ref), lax.rem(ph - _index(_place()), ph), 0)
        g = tall[0:ws, :]
        g_ref[...] = g
        d_ref[...], m2_ref[...], v2_ref[...] = _adam_math(w_ref[...], g, m_ref[...], v_ref[...])

    blk = pl.BlockSpec((ws, tc), lambda j: (0, j))
    out = jax.ShapeDtypeStruct(w.shape, F32)
    return pl.pallas_call(
        body, name=name, grid=(d // tc,),
        in_specs=[pl.BlockSpec((r.shape[0], ph, tc), lambda j: (0, 0, j)), blk, blk, blk],
        out_specs=[blk] * 4, out_shape=[out] * 4,
        scratch_shapes=[pltpu.VMEM((ph, tc), F32)], compiler_params=_params("parallel"),
    )(r, w, m, v)


def _sum_slots(r, name, tr=128):
    _, rows, cols = r.shape
    tr = _rows(rows, tr)

    def body(r_ref, g_ref):
        g_ref[...] = _slot_sum(r_ref)

    return pl.pallas_call(
        body, name=name, grid=(rows // tr,),
        in_specs=[pl.BlockSpec((r.shape[0], tr, cols), lambda i: (0, i, 0))],
        out_specs=pl.BlockSpec((tr, cols), lambda i: (i, 0)),
        out_shape=jax.ShapeDtypeStruct((rows, cols), F32),
        compiler_params=_params("parallel"),
    )(r)


def _adamw(w, g, m, v, name, tr=256):
    rows, cols = w.shape
    tr = _rows(rows, tr)

    def body(w_ref, g_ref, m_ref, v_ref, d_ref, m2_ref, v2_ref):
        d_ref[...], m2_ref[...], v2_ref[...] = _adam_math(w_ref[...], g_ref[...], m_ref[...], v_ref[...])

    blk = pl.BlockSpec((tr, cols), lambda i: (i, 0))
    out = jax.ShapeDtypeStruct((rows, cols), F32)
    return pl.pallas_call(
        body, name=name, grid=(rows // tr,), in_specs=[blk] * 4, out_specs=[blk] * 3, out_shape=[out] * 3,
        compiler_params=_params("parallel"),
    )(w, g, m, v)


def _sum_adamw(r, w, m, v, name, tr=256):
    rows, cols = w.shape
    tr = _rows(rows, tr)

    def body(r_ref, w_ref, m_ref, v_ref, g_ref, d_ref, m2_ref, v2_ref):
        g = _slot_sum(r_ref)
        g_ref[...] = g
        d_ref[...], m2_ref[...], v2_ref[...] = _adam_math(w_ref[...], g, m_ref[...], v_ref[...])

    blk = pl.BlockSpec((tr, cols), lambda i: (i, 0))
    out = jax.ShapeDtypeStruct((rows, cols), F32)
    return pl.pallas_call(
        body, name=name, grid=(rows // tr,),
        in_specs=[pl.BlockSpec((r.shape[0], tr, cols), lambda i: (0, i, 0)), blk, blk, blk],
        out_specs=[blk] * 4, out_shape=[out] * 4,
        compiler_params=_params("parallel"),
    )(r, w, m, v)


def _pack(pieces, sizes):
    flat = [jnp.pad(p.reshape(-1).astype(F32), (0, s - p.size)) for p, s in zip(pieces, sizes)]
    total = sum(sizes)
    padded = -(-total // (16 * LANE)) * (16 * LANE)
    return jnp.pad(jnp.concatenate(flat), (0, padded - total)).reshape(-1, LANE)


def _unpack(packed, shapes, sizes):
    flat = packed.reshape(-1)
    out, off = [], 0
    for shp, s in zip(shapes, sizes):
        n = 1
        for k in shp:
            n *= k
        out.append(flat[off:off + n].reshape(shp))
        off += s
    return out


def _lanes(n):
    return -(-n // LANE) * LANE


WEIGHTS = ("w_in", "b_gates", "w_sc_conv", "mh_gain", "w_out", "ln1_g", "ln1_b", "w_up", "w_ffn_conv", "b_ffn_conv",
           "w_down", "ln2_g", "ln2_b")
BIG = ("w_in", "w_out", "w_up", "w_down")
SMALL = tuple(n for n in WEIGHTS if n not in BIG)


def kernel(x, w_in, b_gates, w_sc_conv, mh_gain, w_out, ln1_g, ln1_b, w_up, w_ffn_conv, b_ffn_conv, w_down, ln2_g, ln2_b, loss_target, m_w_in, m_b_gates, m_w_sc_conv, m_mh_gain, m_w_out, m_ln1_g, m_ln1_b, m_w_up, m_w_ffn_conv, m_b_ffn_conv, m_w_down, m_ln2_g, m_ln2_b, v_w_in, v_b_gates, v_w_sc_conv, v_mh_gain, v_w_out, v_ln1_g, v_ln1_b, v_w_up, v_w_ffn_conv, v_b_ffn_conv, v_w_down, v_ln2_g, v_ln2_b):
    w = dict(zip(WEIGHTS, (w_in, b_gates, w_sc_conv, mh_gain, w_out, ln1_g, ln1_b, w_up, w_ffn_conv, b_ffn_conv,
                           w_down, ln2_g, ln2_b)))
    m = dict(zip(WEIGHTS, (m_w_in, m_b_gates, m_w_sc_conv, m_mh_gain, m_w_out, m_ln1_g, m_ln1_b, m_w_up,
                           m_w_ffn_conv, m_b_ffn_conv, m_w_down, m_ln2_g, m_ln2_b)))
    v = dict(zip(WEIGHTS, (v_w_in, v_b_gates, v_w_sc_conv, v_mh_gain, v_w_out, v_ln1_g, v_ln1_b, v_w_up,
                           v_w_ffn_conv, v_b_ffn_conv, v_w_down, v_ln2_g, v_ln2_b)))
    me = _index(_place())
    d = x.shape[2]
    ws_in = w_in.shape[2]
    assert ws_in == IN_SLAB + 1 and N_DEV <= LANE, w_in.shape
    ninp = (N_DEV + 1) * IN_SLAB
    ws_sc, ws_fc = w_sc_conv.shape[2], w_ffn_conv.shape[2]
    w_in_t, m_in_t, v_in_t = (jnp.transpose(a[0]) for a in (w_in, m_w_in, v_w_in))

    w_in_main, w_in_tail = _shift_w_in(w_in_t)
    taps8 = lambda a: jnp.pad(a[0], ((0, 5), (0, 0)))
    at_once = ("w_in", "w_tail", "w_sc", "w_fc")
    wx = _Gathering(dict(zip(at_once, (w_in_main, w_in_tail, taps8(w_sc_conv), taps8(w_ffn_conv)))),
                    {n: w[n][0] for n in ("w_out", "w_up", "w_down")}, me)
    token = x_b = _behind(x[0], wx.begin(None)).astype(BF16)
    for n in at_once:
        token = wx.forward(n, token)
    g_in, g_tail, g_sc, g_fc = (wx.get(n, token) for n in at_once)
    w_in_full = _carry_w_in(g_in, g_tail).reshape(ninp, d)
    w_sc_full = g_sc[:, :3].transpose(1, 0, 2).reshape(3, N_DEV * ws_sc)
    w_fc_full = g_fc[:, :3].transpose(1, 0, 2).reshape(3, N_DEV * ws_fc)

    xi, yi, ci = _place()
    names = ("loss",) + SMALL
    pieces = {}

    def gather_small(small, loss_t, after):
        pieces.update(small, loss=loss_t[0, :1])
        sizes = [_lanes(pieces[n].size) for n in names]
        (g_small,) = _all_gather([_behind(_pack([pieces[n] for n in names], sizes), after)], "gather_small")
        return _sum_slots(g_small, "sum_small", tr=g_small.shape[1])

    gx = _Reducing(jnp.reshape(ci, (1,)).astype(jnp.int32), 2 * xi + yi, gather_small)
    loss_t, grad_x, small, _ = _local_step(
        x[0], loss_target[0], w_in_full, b_gates, w_sc_full, mh_gain, None, ln1_g, ln1_b, None,
        w_fc_full, b_ffn_conv, None, ln2_g, ln2_b, gx=gx, wx=wx, x_b=x_b)

    grads, deltas, new_m, new_v = {}, {}, {}, {}
    for name in ("w_down", "w_up", "w_out"):
        grads[name], deltas[name], new_m[name], new_v[name] = _sum_adamw(
            gx.finish(name, gx.token), w[name][0], m[name][0], v[name][0], "adamw_" + name)

    summed = _unpack(gx.small_sum, [pieces[n].shape for n in names], [_lanes(pieces[n].size) for n in names])
    full = dict(zip(names, summed))
    full["w_sc_conv"] = lax.dynamic_slice(full["w_sc_conv"], (0, me * ws_sc), (3, ws_sc))
    full["w_ffn_conv"] = lax.dynamic_slice(full["w_ffn_conv"], (0, me * ws_fc), (3, ws_fc))
    for n in SMALL:
        grads[n] = full[n].reshape(w[n].shape)
    sizes = [_lanes(w[n].size) for n in SMALL]
    shapes = [w[n].shape for n in SMALL]
    packed = [_pack([t[n] for n in SMALL], sizes) for t in (w, grads, m, v)]
    small_out = _adamw(*packed, "adamw_small")
    for res, t in zip(small_out, (deltas, new_m, new_v)):
        t.update(zip(SMALL, _unpack(res, shapes, sizes)))

    done = sum(t[0:1, 0:1] for t in (grad_x, deltas["w_down"], deltas["w_up"], deltas["w_out"], small_out[0]))
    grads["w_in"], deltas["w_in"], new_m["w_in"], new_v["w_in"] = (
        jnp.transpose(a)[None] for a in _sum_adamw_shifted(gx.finish("w_in", done), w_in_t, m_in_t, v_in_t, "adamw_w_in"))

    big = lambda t: {n: (t[n].reshape(w[n].shape) if n in BIG else t[n]) for n in WEIGHTS}
    grads, deltas, new_m, new_v = big(grads), big(deltas), big(new_m), big(new_v)
    return (full["loss"].reshape(()), grad_x[None], *[grads[n] for n in WEIGHTS], *[deltas[n] for n in WEIGHTS],
            *[new_m[n] for n in WEIGHTS], *[new_v[n] for n in WEIGHTS])
```

```python
import functools

import jax
import jax.numpy as jnp
from jax import lax
from jax.experimental import pallas as pl
from jax.experimental.pallas import tpu as pltpu

F32 = jnp.float32
BF16 = jnp.bfloat16
MESH = pl.DeviceIdType.MESH

N_DEV = 8
NH = 4
CHUNK = 64
LN_EPS = 1e-5
HN_EPS = 1e-6
ALPHA = 2.0 ** 0.25
LANE = 128
IN_SLAB = 7 * LANE
IN_TAIL = 16
VMEM_LIMIT = 56 * 1024 * 1024
ADAM_LR, ADAM_B1, ADAM_B2, ADAM_EPS, ADAM_WD, ADAM_STEP = 0.001, 0.9, 0.999, 1e-08, 0.01, 10

_NN = (((1,), (0,)), ((), ()))
_NT = (((1,), (1,)), ((), ()))
_TN = (((0,), (0,)), ((), ()))


def _dot(a, b, dn=_NN):
    return lax.dot_general(a, b, dn, preferred_element_type=F32)


def _params(*sem):
    return pltpu.CompilerParams(dimension_semantics=sem if sem else None, vmem_limit_bytes=VMEM_LIMIT)


def _iota(shape, axis):
    return lax.broadcasted_iota(jnp.int32, shape, axis)


def _fit(n, want):
    if n <= want:
        return n
    t = want - want % LANE
    while n % t:
        t -= LANE
    return t


def _matmul(a, b, mode, out_dtype, name, tm=1024, tn=512, tk=1024, add=None, add_scale=1.0,
            a_blocked=False, b_blocked=False, o_width=None, after=None, n=None):
    a_parts = a if isinstance(a, tuple) else None
    b_parts = b if isinstance(b, tuple) else None
    if a_parts:
        a_blocked, (a_rows, wa), na = True, a[0].shape, len(a)
        kd, m = (a_rows, na * wa) if mode == "tn" else (na * wa, a_rows)
    elif a_blocked:
        na, a_rows, wa = a.shape
        kd, m = (a_rows, na * wa) if mode == "tn" else (na * wa, a_rows)
    elif mode == "tn":
        kd, m = a.shape
    else:
        m, kd = a.shape
    if b_parts:
        b_blocked, (rows, w), nb = True, b[0].shape, len(b)
    elif b_blocked:
        nb, rows, w = b.shape
    if b_blocked:
        n = rows if mode == "nt" else nb * w
        assert (nb * w if mode == "nt" else rows) == kd, (name, kd)
    else:
        n = n or (b.shape[0] if mode == "nt" else b.shape[1])
    tm, tn, tk = _fit(m, tm), _fit(n, tn), _fit(kd, tk)
    if a_blocked and mode == "tn":
        tm = _fit(wa, tm)
    if a_blocked and mode != "tn":
        tk = _fit(wa, tk)
    if b_blocked and mode != "nt":
        tn = _fit(w, tn)
    if b_blocked and mode == "nt":
        tk = _fit(w, tk)
    if o_width is not None:
        tn = _fit(o_width, tn)
    assert m % tm == 0 and n % tn == 0 and kd % tk == 0, (name, m, n, kd, tm, tn, tk)
    assert not (a_blocked and mode != "tn" and wa % tk) and not (b_blocked and mode == "nt" and w % tk), (name, tk)
    nk = kd // tk
    dn = {"nn": _NN, "nt": _NT, "tn": _TN}[mode]
    if a_blocked and mode == "tn":
        a_per = wa // tm
        a_spec = pl.BlockSpec((None, tk, tm), lambda i, j, k: (i // a_per, k, i % a_per))
    elif a_blocked:
        a_per = wa // tk
        a_spec = pl.BlockSpec((None, tm, tk), lambda i, j, k: (k // a_per, i, k % a_per))
    elif mode == "tn":
        a_spec = pl.BlockSpec((tk, tm), lambda i, j, k: (k, i))
    else:
        a_spec = pl.BlockSpec((tm, tk), lambda i, j, k: (i, k))
    if b_blocked and mode != "nt":
        per = w // tn
        b_spec = pl.BlockSpec((None, tk, tn), lambda i, j, k: (j // per, k, j % per))
    elif b_blocked:
        per = w // tk
        b_spec = pl.BlockSpec((None, tn, tk), lambda i, j, k: (k // per, j, k % per))
    elif mode == "nt":
        b_spec = pl.BlockSpec((tn, tk), lambda i, j, k: (j, k))
    else:
        b_spec = pl.BlockSpec((tk, tn), lambda i, j, k: (k, j))
    if o_width is None:
        o_spec = pl.BlockSpec((tm, tn), lambda i, j, k: (i, j))
        o_shape = (m, n)
    else:
        oper = o_width // tn
        o_spec = pl.BlockSpec((None, tm, tn), lambda i, j, k: (j // oper, i, j % oper))
        o_shape = (n // o_width, m, o_width)
    a_list, a_specs = [a], [a_spec]
    if a_parts:
        hold = lambda x, s: jnp.clip(x - s * a_per, 0, a_per - 1)
        a_list = list(a_parts)
        a_specs = [(pl.BlockSpec((tk, tm), lambda i, j, k, s=s: (k, hold(i, s))) if mode == "tn"
                    else pl.BlockSpec((tm, tk), lambda i, j, k, s=s: (i, hold(k, s)))) for s in range(na)]
    b_list, b_specs = [b], [b_spec]
    if b_parts:
        hold_b = lambda x, s: jnp.clip(x - s * per, 0, per - 1)
        b_list = list(b_parts)
        b_specs = [(pl.BlockSpec((tn, tk), lambda i, j, k, s=s: (j, hold_b(k, s))) if mode == "nt"
                    else pl.BlockSpec((tk, tn), lambda i, j, k, s=s: (k, hold_b(j, s)))) for s in range(nb)]
    n_a, n_b = len(a_list), len(b_list)
    has_add = add is not None
    n_in = n_a + n_b + has_add + (after is not None)
    in_place = nk > 1 and out_dtype == F32

    def body(*refs):
        add_ref = refs[n_a + n_b] if has_add else None
        o_ref = refs[n_in]
        i, j, k = pl.program_id(0), pl.program_id(1), pl.program_id(2)

        def finish(r):
            if has_add:
                r = r + add_scale * add_ref[...]
            o_ref[...] = r.astype(out_dtype)

        def step(a_ref, b_ref):
            if nk == 1:
                finish(_dot(a_ref[...], b_ref[...], dn))
                return
            acc = o_ref if in_place else refs[-1]

            @pl.when(k == 0)
            def _():
                acc[...] = _dot(a_ref[...], b_ref[...], dn)

            @pl.when(k > 0)
            def _():
                acc[...] += _dot(a_ref[...], b_ref[...], dn)

        if n_a == 1 and n_b == 1:
            step(refs[0], refs[1])
        else:
            slab_a = ((i if mode == "tn" else k) // a_per) if n_a > 1 else 0
            slab_b = ((k if mode == "nt" else j) // per) if n_b > 1 else 0
            for sa in range(n_a):
                for sb in range(n_b):
                    pl.when((slab_a == sa) & (slab_b == sb))(functools.partial(step, refs[sa], refs[n_a + sb]))
        if nk > 1 and not (in_place and not has_add):
            @pl.when(k == nk - 1)
            def _():
                finish((o_ref if in_place else refs[-1])[...])

    in_specs = a_specs + b_specs + ([pl.BlockSpec((tm, tn), lambda i, j, k: (i, j))] if has_add else [])
    args = (*a_list, *b_list) + ((add,) if has_add else ())
    if after is not None:
        in_specs.append(pl.BlockSpec(memory_space=pl.ANY))
        args += (after,)
    return pl.pallas_call(
        body, name=name, grid=(m // tm, n // tn, nk),
        in_specs=in_specs, out_specs=o_spec,
        out_shape=jax.ShapeDtypeStruct(o_shape, out_dtype),
        scratch_shapes=[pltpu.VMEM((tm, tn), F32)] if nk > 1 and not in_place else [],
        compiler_params=_params("parallel", "parallel", "arbitrary"),
    )(*args)


def _shift_down(u, s):
    return jnp.where(_iota(u.shape, 0) >= s, pltpu.roll(u, s, 0), 0.0)


def _shift_up(u, s):
    t = u.shape[0]
    return jnp.where(_iota(u.shape, 0) < t - s, pltpu.roll(u, t - s, 0), 0.0)


SLAB = 8


def _rolled(u):
    return pltpu.roll(u, 2, 0), pltpu.roll(u, 1, 0)


def _conv(u, w, rolled=None):
    u2, u1 = _rolled(u) if rolled is None else rolled
    raw = w[0:1] * u2 + w[1:2] * u1 + w[2:3] * u
    head = u[0:SLAB]
    mended = w[0:1] * _shift_down(head, 2) + w[1:2] * _shift_down(head, 1) + w[2:3] * head
    return jnp.concatenate([mended, raw[SLAB:]], axis=0)


def _conv_t(dy, w):
    t = dy.shape[0]
    raw = w[2:3] * dy + w[1:2] * pltpu.roll(dy, t - 1, 0) + w[0:1] * pltpu.roll(dy, t - 2, 0)
    tail = dy[t - SLAB:]
    mended = w[2:3] * tail + w[1:2] * _shift_up(tail, 1) + w[0:1] * _shift_up(tail, 2)
    return jnp.concatenate([raw[:t - SLAB], mended], axis=0)


def _conv_dw(dy, u, rolled=None):
    t = dy.shape[0]
    u2, u1 = _rolled(u) if rolled is None else rolled
    head, tail = dy[0:SLAB], u[t - SLAB:]
    r = _iota(head.shape, 0)
    wrap2 = jnp.sum(jnp.where(r < 2, head * pltpu.roll(tail, 2, 0), 0.0), axis=0, keepdims=True)
    wrap1 = jnp.sum(jnp.where(r < 1, head * pltpu.roll(tail, 1, 0), 0.0), axis=0, keepdims=True)
    d0 = jnp.sum(dy * u2, axis=0, keepdims=True) - wrap2
    d1 = jnp.sum(dy * u1, axis=0, keepdims=True) - wrap1
    d2 = jnp.sum(dy * u, axis=0, keepdims=True)
    r3 = _iota((3, dy.shape[1]), 0)
    return jnp.where(r3 == 0, d0, jnp.where(r3 == 1, d1, d2))


def _sigmoid(x):
    return 0.5 * jnp.tanh(0.5 * x) + 0.5


def _sconv_fwd(proj, w_sc, t, wc):
    nb = wc // LANE

    def body(cb_ref, cc_ref, ch_ref, w_ref, y_ref):
        u = cc_ref[...] * ch_ref[...]
        y_ref[...] = (cb_ref[...] * _conv(u, w_ref[...])).astype(BF16)

    col = lambda off: pl.BlockSpec((t, LANE), lambda j: (0, j + off))
    return pl.pallas_call(
        body, name="sconv_fwd", grid=(nb,),
        in_specs=[col(0), col(nb), col(2 * nb), pl.BlockSpec((3, LANE), lambda j: (0, j))],
        out_specs=pl.BlockSpec((None, t, LANE), lambda j: (0, 0, j)),
        out_shape=jax.ShapeDtypeStruct((2, t, wc), BF16),
        compiler_params=_params("parallel"),
    )(proj, proj, proj, w_sc)


def _sconv_bwd(dy, proj, w_sc, t, wc):
    nb = wc // LANE

    def body(dy_ref, cb_ref, cc_ref, ch_ref, w_ref, dcb_ref, dcc_ref, dch_ref, dw_ref):
        cc, ch, w, d = cc_ref[...], ch_ref[...], w_ref[...], dy_ref[...]
        u = cc * ch
        ru = _rolled(u)
        dcb_ref[...] = (d * _conv(u, w, ru)).astype(BF16)
        dcu = d * cb_ref[...]
        dw_ref[...] = _conv_dw(dcu, u, ru)
        du = _conv_t(dcu, w)
        dcc_ref[...] = (du * ch).astype(BF16)
        dch_ref[...] = (du * cc).astype(BF16)

    col = lambda off: pl.BlockSpec((t, LANE), lambda j: (0, j + off))
    act = jax.ShapeDtypeStruct((t, wc), BF16)
    return pl.pallas_call(
        body, name="sconv_bwd", grid=(nb,),
        in_specs=[col(0), col(0), col(nb), col(2 * nb), pl.BlockSpec((3, LANE), lambda j: (0, j))],
        out_specs=[col(0), col(0), col(0), pl.BlockSpec((3, LANE), lambda j: (0, j))],
        out_shape=[act, act, act, jax.ShapeDtypeStruct((3, wc), F32)],
        compiler_params=_params("parallel"),
    )(dy, proj, proj, proj, w_sc)


def _gates_prep(proj, bias_tile, t, gate_tile):
    def body(g_ref, b_ref, o_ref):
        g = g_ref[...] + b_ref[...]
        lane = _iota(g.shape, 1)
        is_f = (lane >= NH) & (lane < 2 * NH)
        lf = jnp.minimum(g, 0.0) - jnp.log(1.0 + jnp.exp(-jnp.abs(g)))
        c = jnp.where(is_f, lf, 0.0)
        r = _iota(g.shape, 0) % CHUNK
        s = 1
        while s < CHUNK:
            c = c + jnp.where(r >= s, pltpu.roll(c, s, 0), 0.0)
            s *= 2
        o_ref[...] = jnp.where(is_f, c, jnp.where(lane < NH, g, 0.0))

    return pl.pallas_call(
        body, name="gates_prep", grid=(1,),
        in_specs=[pl.BlockSpec((t, LANE), lambda i: (0, gate_tile)), pl.BlockSpec((1, LANE), lambda i: (0, 0))],
        out_specs=pl.BlockSpec((t, LANE), lambda i: (0, 0)),
        out_shape=jax.ShapeDtypeStruct((t, LANE), F32),
        compiler_params=_params("arbitrary"),
    )(proj, bias_tile)


def _gates_bwd(dgate, proj, bias_tile, t, gate_tile):
    def body(dg_ref, g_ref, b_ref, o_ref, s_ref):
        g = g_ref[...] + b_ref[...]
        lane = _iota(g.shape, 1)
        r = _iota(g.shape, 0) % CHUNK
        dsig = 1.0 - _sigmoid(g)
        out = jnp.zeros(g.shape, F32)
        for h in range(NH):
            d = dg_ref[h]
            c = d
            s = 1
            while s < CHUNK:
                c = c + jnp.where(r + s < CHUNK, pltpu.roll(c, t - s, 0), 0.0)
                s *= 2
            di = jnp.broadcast_to(d[:, 0:1], g.shape)
            db = jnp.broadcast_to(c[:, 1:2], g.shape)
            out = out + jnp.where(lane == h, di, 0.0) + jnp.where(lane == NH + h, db * dsig, 0.0)
        o_ref[...] = out.astype(BF16)
        s_ref[...] = jnp.sum(out, axis=0, keepdims=True)

    return pl.pallas_call(
        body, name="gates_bwd", grid=(1,),
        in_specs=[pl.BlockSpec((NH, t, LANE), lambda i: (0, 0, 0)),
                  pl.BlockSpec((t, LANE), lambda i: (0, gate_tile)), pl.BlockSpec((1, LANE), lambda i: (0, 0))],
        out_specs=[pl.BlockSpec((t, LANE), lambda i: (0, 0)), pl.BlockSpec((1, LANE), lambda i: (0, 0))],
        out_shape=[jax.ShapeDtypeStruct((t, LANE), BF16), jax.ShapeDtypeStruct((1, LANE), F32)],
        compiler_params=_params("arbitrary"),
    )(dgate, proj, bias_tile)


def _in_turn(heads):
    while heads:
        heads = [g for g in heads if next(g, heads) is not heads]


def _chunk_gates(gc, gr, h, mprev):
    L = CHUNK
    icol, bcol = gc[:, h:h + 1], gc[:, h + NH:h + NH + 1]
    irow, brow = gr[h:h + 1, :], gr[h + NH:h + NH + 1, :]
    tri = _iota((L, L), 0) >= _iota((L, L), 1)
    log_d = jnp.where(tri, bcol - brow + irow, -jnp.inf)
    inter = bcol + mprev
    mt = jnp.maximum(inter, jnp.max(log_d, axis=1, keepdims=True))
    dw = jnp.exp(log_d - mt)
    iw = jnp.exp(inter - mt)
    g = brow[:, L - 1:L]
    wlog_col = g - bcol + icol
    wlog_row = g - brow + irow
    mnew = jnp.maximum(g + mprev, jnp.max(wlog_row, axis=1, keepdims=True))
    wcol = jnp.exp(wlog_col - mnew)
    decay = jnp.exp(g + mprev - mnew)
    return dw, iw, mt, wcol, decay, mnew


def _mlstm_fwd(proj, gcol, grow, t, wc, dh):
    nc = t // CHUNK
    wm = NH * dh
    assert wc == wm, (wc, wm)
    qoff = 3 * wc // wm
    scale = dh ** -0.5

    def body(q_ref, k_ref, v_ref, gc_ref, gr_ref, h_ref, cs_ref, ns_ref, c_s, n_s, m_s):
        @pl.when(pl.program_id(0) == 0)
        def _():
            c_s[...] = jnp.zeros_like(c_s)
            n_s[...] = jnp.zeros_like(n_s)
            m_s[...] = jnp.zeros_like(m_s)

        gc, gr = gc_ref[...], gr_ref[0]
        done = [None] * NH

        def head(h):
            cols = slice(h * dh, (h + 1) * dh)
            mprev = m_s[h, 0:1, 0:1]
            cprev = c_s[h]
            n8 = n_s[h]
            nprev = n8[0:1]
            qs = q_ref[:, cols] * scale
            k = k_ref[:, cols]
            qs_b, k_b, v_b = qs.astype(BF16), k.astype(BF16), v_ref[:, cols].astype(BF16)
            qk = _dot(qs_b, k_b, _NT)
            yield
            q_c = _dot(qs_b, cprev.astype(BF16))
            yield
            dw, iw, mt, wcol, decay, mnew = _chunk_gates(gc, gr, h, mprev)
            yield
            s = qk * dw
            wk = wcol * k
            num = _dot(s.astype(BF16), v_b) + iw * q_c
            yield
            c_new = decay * cprev + _dot(wk.astype(BF16), v_b, _TN)
            yield
            den = jnp.sum(s, axis=1, keepdims=True) + iw * jnp.sum(qs * nprev, axis=1, keepdims=True)
            done[h] = (cprev, jnp.where(_iota(n8.shape, 0) == 1, mprev, n8),
                       num / jnp.maximum(jnp.abs(den), jnp.exp(-mt)), c_new,
                       decay * n8 + jnp.sum(wk, axis=0, keepdims=True), mnew)

        _in_turn([head(h) for h in range(NH)])
        for h, (c_old, n_old, h_out, c_new, n_new, m_new) in enumerate(done):
            cs_ref[h] = c_old
            ns_ref[h] = n_old
            h_ref[:, h * dh:(h + 1) * dh] = h_out
            c_s[h] = c_new
            n_s[h] = n_new
            m_s[h] = jnp.broadcast_to(m_new, m_s.shape[1:])

    grp = lambda off: pl.BlockSpec((CHUNK, wm), lambda c: (c, qoff + off))
    return pl.pallas_call(
        body, name="mlstm_fwd", grid=(nc,),
        in_specs=[grp(0), grp(1), grp(2),
                  pl.BlockSpec((CHUNK, LANE), lambda c: (c, 0)),
                  pl.BlockSpec((1, 8, CHUNK), lambda c: (c, 0, 0))],
        out_specs=[pl.BlockSpec((CHUNK, wm), lambda c: (c, 0)),
                   pl.BlockSpec((NH, None, dh, dh), lambda c: (0, c, 0, 0)),
                   pl.BlockSpec((NH, None, 8, dh), lambda c: (0, c, 0, 0))],
        out_shape=[jax.ShapeDtypeStruct((t, wm), F32),
                   jax.ShapeDtypeStruct((NH, nc, dh, dh), F32),
                   jax.ShapeDtypeStruct((NH, nc, 8, dh), F32)],
        scratch_shapes=[pltpu.VMEM((NH, dh, dh), F32), pltpu.VMEM((NH, 8, dh), F32), pltpu.VMEM((NH, 8, LANE), F32)],
        compiler_params=_params("arbitrary"),
    )(proj, proj, proj, gcol, grow)


def _mlstm_bwd(proj, gcol, grow, hval, dh_in, cs, ns, t, wc, dh):
    nc = t // CHUNK
    wm = NH * dh
    assert wc == wm, (wc, wm)
    qoff = 3 * wc // wm
    scale = dh ** -0.5
    L = CHUNK

    def body(q_ref, k_ref, v_ref, gc_ref, gr_ref, h_ref, dh_ref, cs_ref, ns_ref,
             dq_ref, dk_ref, dv_ref, dg_ref, dc_s, dn_s):
        @pl.when(pl.program_id(0) == 0)
        def _():
            dc_s[...] = jnp.zeros_like(dc_s)
            dn_s[...] = jnp.zeros_like(dn_s)

        gc, gr = gc_ref[...], gr_ref[0]
        eye = _iota((L, L), 0) == _iota((L, L), 1)
        lane = _iota((L, LANE), 1)
        last = _iota((L, 1), 0) == L - 1
        done = [None] * NH

        def head(h):
            cols = slice(h * dh, (h + 1) * dh)
            ns8 = ns_ref[h]
            nprev = ns8[0:1]
            mprev = ns8[1:2, 0:1]
            cprev = cs_ref[h]
            dcn = dc_s[h]
            dn8 = dn_s[h]
            dnn = dn8[0:1]

            qs = q_ref[:, cols] * scale
            k = k_ref[:, cols]
            qs_b, k_b, v_b = qs.astype(BF16), k.astype(BF16), v_ref[:, cols].astype(BF16)
            qk = _dot(qs_b, k_b, _NT)
            yield
            dw, iw, mt, wcol, decay, _ = _chunk_gates(gc, gr, h, mprev)
            yield
            s = qk * dw
            den = jnp.sum(s, axis=1, keepdims=True) + iw * jnp.sum(qs * nprev, axis=1, keepdims=True)
            emt = jnp.exp(-mt)
            r = 1.0 / jnp.maximum(jnp.abs(den), emt)
            dout = dh_ref[:, cols]
            dnum = dout * r
            dden = (-jnp.sum(dout * h_ref[:, cols], axis=1, keepdims=True) * r
                    * jnp.where(jnp.abs(den) > emt, jnp.sign(den), 0.0))
            dnum_b = dnum.astype(BF16)
            cprev_b = cprev.astype(BF16)
            dcn_b = dcn.astype(BF16)
            yield

            g_raw = _dot(dnum_b, v_b, _NT)
            yield
            q_inter = _dot(dnum_b, cprev_b, _NT)
            yield
            k_raw = _dot(v_b, dcn_b, _NT)
            yield
            gd = (g_raw + dden) * dw
            gd_b = gd.astype(BF16)
            dqs_inter = iw * (q_inter + dden * nprev)
            dk_inter = wcol * (k_raw + dnn)
            wk = wcol * k
            iq = iw * qs
            dqs = _dot(gd_b, k_b) + dqs_inter
            yield
            dk = _dot(gd_b, qs_b, _TN) + dk_inter
            yield
            dv = _dot(s.astype(BF16), dnum_b, _TN) + _dot(wk.astype(BF16), dcn_b)
            yield
            dc_new = decay * dcn + _dot(iq.astype(BF16), dnum_b, _TN)
            yield

            e = gd * qk
            e_cols = jnp.sum(jnp.where(eye, jnp.sum(e, axis=0, keepdims=True), 0.0), axis=1, keepdims=True)
            yield
            k_inter = jnp.sum(k * dk_inter, axis=1, keepdims=True)
            rq = jnp.sum(e, axis=1, keepdims=True) + jnp.sum(qs * dqs_inter, axis=1, keepdims=True)
            rk = e_cols + k_inter
            hsum = jnp.sum(k_inter, axis=0, keepdims=True)
            jdec = decay * (jnp.sum(jnp.sum(dcn * cprev, axis=1, keepdims=True), axis=0, keepdims=True)
                            + jnp.sum(dnn * nprev, axis=1, keepdims=True))
            db = rq - rk + jnp.where(last, hsum + jdec, 0.0)
            done[h] = (jnp.where(lane == 0, rk, jnp.where(lane == 1, db, 0.0)),
                       (dqs * scale).astype(BF16), dk.astype(BF16), dv.astype(BF16), dc_new,
                       decay * dn8 + jnp.sum(iq * dden, axis=0, keepdims=True))

        _in_turn([head(h) for h in range(NH)])
        for h, (dgate, dq, dk, dv, dc_new, dn_new) in enumerate(done):
            cols = slice(h * dh, (h + 1) * dh)
            dg_ref[h] = dgate
            dq_ref[:, cols] = dq
            dk_ref[:, cols] = dk
            dv_ref[:, cols] = dv
            dc_s[h] = dc_new
            dn_s[h] = dn_new

    rc = lambda c: nc - 1 - c
    grp = lambda off: pl.BlockSpec((L, wm), lambda c: (rc(c), qoff + off))
    hm = pl.BlockSpec((L, wm), lambda c: (rc(c), 0))
    act = jax.ShapeDtypeStruct((t, wm), BF16)
    return pl.pallas_call(
        body, name="mlstm_bwd", grid=(nc,),
        in_specs=[grp(0), grp(1), grp(2),
                  pl.BlockSpec((L, LANE), lambda c: (rc(c), 0)),
                  pl.BlockSpec((1, 8, L), lambda c: (rc(c), 0, 0)),
                  hm, hm,
                  pl.BlockSpec((NH, None, dh, dh), lambda c: (0, rc(c), 0, 0)),
                  pl.BlockSpec((NH, None, 8, dh), lambda c: (0, rc(c), 0, 0))],
        out_specs=[hm, hm, hm, pl.BlockSpec((NH, L, LANE), lambda c: (0, rc(c), 0))],
        out_shape=[act, act, act, jax.ShapeDtypeStruct((NH, t, LANE), F32)],
        scratch_shapes=[pltpu.VMEM((NH, dh, dh), F32), pltpu.VMEM((NH, 8, dh), F32)],
        compiler_params=_params("arbitrary"),
    )(proj, proj, proj, gcol, grow, hval, dh_in, cs, ns)


def _head_norm(hv):
    mu = jnp.mean(hv, axis=1, keepdims=True)
    hc = hv - mu
    rstd = lax.rsqrt(jnp.mean(hc * hc, axis=1, keepdims=True) + HN_EPS)
    return hc * rstd, rstd


def _hnorm_fwd(hval, proj, gain, y, t, wc, dh, tr=512):
    ooff = 3 * wc // dh + 3 * NH
    tr = min(tr, t)

    def body(h_ref, o_ref, g_ref, y_in, y_ref):
        hhat, _ = _head_norm(h_ref[...])
        y_ref[...] = (_sigmoid(o_ref[...]) * hhat * g_ref[...]).astype(BF16)

    return pl.pallas_call(
        body, name="hnorm_fwd", grid=(t // tr, NH),
        in_specs=[pl.BlockSpec((tr, dh), lambda i, h: (i, h)),
                  pl.BlockSpec((tr, dh), lambda i, h: (i, ooff + h)),
                  pl.BlockSpec((1, dh), lambda i, h: (0, h)),
                  pl.BlockSpec(memory_space=pl.ANY)],
        out_specs=pl.BlockSpec((None, tr, dh), lambda i, h: (1, i, h)),
        out_shape=jax.ShapeDtypeStruct(y.shape, BF16),
        input_output_aliases={3: 0},
        compiler_params=_params("parallel", "parallel"),
    )(hval, proj, gain, y)


def _hnorm_bwd(dy, hval, proj, gain, t, wc, dh, tr=512):
    ooff = 3 * wc // dh + 3 * NH
    tr = min(tr, t)
    yoff = wc // dh

    def body(dy_ref, h_ref, o_ref, g_ref, do_ref, dh_ref, dg_ref):
        i = pl.program_id(1)
        hhat, rstd = _head_norm(h_ref[...])
        gain_v = g_ref[...]
        sig = _sigmoid(o_ref[...])
        d = dy_ref[...]
        do_ref[...] = (d * hhat * gain_v * sig * (1.0 - sig)).astype(BF16)
        dhn = d * sig
        part = jnp.sum(dhn * hhat, axis=0, keepdims=True)

        @pl.when(i == 0)
        def _():
            dg_ref[...] = part

        @pl.when(i > 0)
        def _():
            dg_ref[...] += part

        dhat = dhn * gain_v
        dh_ref[...] = rstd * (dhat - jnp.mean(dhat, axis=1, keepdims=True)
                              - hhat * jnp.mean(dhat * hhat, axis=1, keepdims=True))

    blk = lambda off: pl.BlockSpec((tr, dh), lambda h, i: (i, off + h))
    return pl.pallas_call(
        body, name="hnorm_bwd", grid=(NH, t // tr),
        in_specs=[blk(yoff), blk(0), blk(ooff), pl.BlockSpec((1, dh), lambda h, i: (0, h))],
        out_specs=[blk(0), blk(0), pl.BlockSpec((1, dh), lambda h, i: (0, h))],
        out_shape=[jax.ShapeDtypeStruct((t, NH * dh), BF16), jax.ShapeDtypeStruct((t, NH * dh), F32),
                   jax.ShapeDtypeStruct((1, NH * dh), F32)],
        compiler_params=_params("parallel", "arbitrary"),
    )(dy, hval, proj, gain)


def _ln_stats(z):
    mu = jnp.mean(z, axis=1, keepdims=True)
    zc = z - mu
    rstd = lax.rsqrt(jnp.mean(zc * zc, axis=1, keepdims=True) + LN_EPS)
    return zc * rstd, rstd


def _ln_bwd(dy, xhat, rstd, g):
    dxh = dy * g
    return rstd * (dxh - jnp.mean(dxh, axis=1, keepdims=True) - xhat * jnp.mean(dxh * xhat, axis=1, keepdims=True))


def _accum(ref, i, part):
    @pl.when(i == 0)
    def _():
        ref[...] = part

    @pl.when(i > 0)
    def _():
        ref[...] += part


def _ln1_fwd(x, mix, g, b, tr=256):
    t, d = x.shape

    def body(x_ref, m_ref, g_ref, b_ref, xh_ref, rs_ref, xb_ref):
        xhat, rstd = _ln_stats(ALPHA * x_ref[...] + m_ref[...])
        xh_ref[...] = xhat
        rs_ref[...] = rstd
        xb_ref[...] = (xhat * g_ref[...] + b_ref[...]).astype(BF16)

    row = pl.BlockSpec((tr, d), lambda i: (i, 0))
    vec = pl.BlockSpec((1, d), lambda i: (0, 0))
    return pl.pallas_call(
        body, name="ln1_fwd", grid=(t // tr,),
        in_specs=[row, row, vec, vec],
        out_specs=[row, pl.BlockSpec((tr, 1), lambda i: (i, 0)), row],
        out_shape=[jax.ShapeDtypeStruct((t, d), F32), jax.ShapeDtypeStruct((t, 1), F32),
                   jax.ShapeDtypeStruct((t, d), BF16)],
        compiler_params=_params("parallel"),
    )(x, mix, g, b)


def _ln2_loss(xhat1, g1, b1, ff, target, g2, b2, tr=256):
    t, d = ff.shape

    def body(xh_ref, g1_ref, b1_ref, f_ref, t_ref, g_ref, b_ref, dz_ref, dzb_ref, dg_ref, db_ref, l_ref):
        i = pl.program_id(0)
        x1 = xh_ref[...] * g1_ref[...] + b1_ref[...]
        xhat, rstd = _ln_stats(ALPHA * x1 + f_ref[...])
        gv = g_ref[...]
        e = xhat * gv + b_ref[...] - t_ref[...]
        lsum = jnp.sum(jnp.sum(e * e, axis=1, keepdims=True), axis=0, keepdims=True) * (0.5 / d)
        dy = e * (1.0 / d)
        _accum(dg_ref, i, jnp.sum(dy * xhat, axis=0, keepdims=True))
        _accum(db_ref, i, jnp.sum(dy, axis=0, keepdims=True))
        _accum(l_ref, i, jnp.broadcast_to(lsum, l_ref.shape))
        dz = _ln_bwd(dy, xhat, rstd, gv)
        dz_ref[...] = dz
        dzb_ref[...] = dz.astype(BF16)

    row = pl.BlockSpec((tr, d), lambda i: (i, 0))
    vec = pl.BlockSpec((1, d), lambda i: (0, 0))
    return pl.pallas_call(
        body, name="ln2_loss", grid=(t // tr,),
        in_specs=[row, vec, vec, row, row, vec, vec],
        out_specs=[row, row, vec, vec, pl.BlockSpec((8, LANE), lambda i: (0, 0))],
        out_shape=[jax.ShapeDtypeStruct((t, d), F32), jax.ShapeDtypeStruct((t, d), BF16),
                   jax.ShapeDtypeStruct((1, d), F32), jax.ShapeDtypeStruct((1, d), F32),
                   jax.ShapeDtypeStruct((8, LANE), F32)],
        compiler_params=_params("arbitrary"),
    )(xhat1, g1, b1, ff, target, g2, b2)


def _ln1_bwd(dz2, dffn, xhat1, rstd1, g1, tr=256):
    t, d = dz2.shape

    def body(a_ref, f_ref, xh_ref, rs_ref, g_ref, dz_ref, dzb_ref, dg_ref, db_ref):
        i = pl.program_id(0)
        dy = ALPHA * a_ref[...] + f_ref[...]
        xhat = xh_ref[...]
        _accum(dg_ref, i, jnp.sum(dy * xhat, axis=0, keepdims=True))
        _accum(db_ref, i, jnp.sum(dy, axis=0, keepdims=True))
        dz = _ln_bwd(dy, xhat, rs_ref[...], g_ref[...])
        dz_ref[...] = dz
        dzb_ref[...] = dz.astype(BF16)

    row = pl.BlockSpec((tr, d), lambda i: (i, 0))
    vec = pl.BlockSpec((1, d), lambda i: (0, 0))
    return pl.pallas_call(
        body, name="ln1_bwd", grid=(t // tr,),
        in_specs=[row, row, row, pl.BlockSpec((tr, 1), lambda i: (i, 0)), vec],
        out_specs=[row, row, vec, vec],
        out_shape=[jax.ShapeDtypeStruct((t, d), F32), jax.ShapeDtypeStruct((t, d), BF16),
                   jax.ShapeDtypeStruct((1, d), F32), jax.ShapeDtypeStruct((1, d), F32)],
        compiler_params=_params("arbitrary"),
    )(dz2, dffn, xhat1, rstd1, g1)


def _ffn_act_fwd(hid0, w_fc, b_fc, t, dff):
    nb = dff // LANE

    def body(hv_ref, hg_ref, wv_ref, wg_ref, bv_ref, bg_ref, a_ref):
        val = _conv(hv_ref[...], wv_ref[...]) + bv_ref[...]
        gate = _conv(hg_ref[...], wg_ref[...]) + bg_ref[...]
        a_ref[...] = (gate * _sigmoid(gate) * val).astype(BF16)

    col = lambda off: pl.BlockSpec((t, LANE), lambda j: (0, j + off))
    w3 = lambda off: pl.BlockSpec((3, LANE), lambda j: (0, j + off))
    w1 = lambda off: pl.BlockSpec((1, LANE), lambda j: (0, j + off))
    return pl.pallas_call(
        body, name="ffn_act_fwd", grid=(nb,),
        in_specs=[col(0), col(nb), w3(0), w3(nb), w1(0), w1(nb)],
        out_specs=col(0),
        out_shape=jax.ShapeDtypeStruct((t, dff), BF16),
        compiler_params=_params("parallel"),
    )(hid0, hid0, w_fc, w_fc, b_fc, b_fc)


def _ffn_act_bwd(da, hid0, w_fc, b_fc, t, dff):
    nb = dff // LANE

    def body(da_ref, hv_ref, hg_ref, wv_ref, wg_ref, bv_ref, bg_ref,
             dhv_ref, dhg_ref, dwv_ref, dwg_ref, dbv_ref, dbg_ref):
        hv, hg, wv, wg = hv_ref[...], hg_ref[...], wv_ref[...], wg_ref[...]
        rv, rg = _rolled(hv), _rolled(hg)
        val = _conv(hv, wv, rv) + bv_ref[...]
        gate = _conv(hg, wg, rg) + bg_ref[...]
        sig = _sigmoid(gate)
        d = da_ref[...]
        dsig = d * sig
        dval = dsig * gate
        dgate = dsig * val * (1.0 + gate * (1.0 - sig))
        dhv_ref[...] = _conv_t(dval, wv).astype(BF16)
        dhg_ref[...] = _conv_t(dgate, wg).astype(BF16)
        dwv_ref[...] = _conv_dw(dval, hv, rv)
        dwg_ref[...] = _conv_dw(dgate, hg, rg)
        dbv_ref[...] = jnp.sum(dval, axis=0, keepdims=True)
        dbg_ref[...] = jnp.sum(dgate, axis=0, keepdims=True)

    col = lambda off: pl.BlockSpec((t, LANE), lambda j: (0, j + off))
    w3 = lambda off: pl.BlockSpec((3, LANE), lambda j: (0, j + off))
    w1 = lambda off: pl.BlockSpec((1, LANE), lambda j: (0, j + off))
    s3 = jax.ShapeDtypeStruct((3, dff), F32)
    s1 = jax.ShapeDtypeStruct((1, dff), F32)
    return pl.pallas_call(
        body, name="ffn_act_bwd", grid=(nb,),
        in_specs=[col(0), col(0), col(nb), w3(0), w3(nb), w1(0), w1(nb)],
        out_specs=[col(0), col(0), w3(0), w3(0), w1(0), w1(0)],
        out_shape=[jax.ShapeDtypeStruct((t, dff), BF16)] * 2 + [s3, s3, s1, s1],
        compiler_params=_params("parallel"),
    )(da, hid0, hid0, w_fc, w_fc, b_fc, b_fc)


class _Ready:
    def __init__(self, **weights):
        self.weights = weights

    def begin(self, after):
        return None

    def forward(self, name, after):
        return None

    def get(self, name, after):
        return self.weights[name]


class _Kept:
    def __init__(self):
        self.grads = {}

    def start(self, name, grad):
        self.grads[name] = grad
        return None

    def relay(self, name, after):
        return None

    def meanwhile(self, small, loss, after):
        return None


def _behind(a, token):
    return a if token is None else a + token[0:1, 0:1].reshape((1,) * a.ndim)


def _local_step(x, target, w_in, b_gates, w_sc, gain, w_out, ln1_g, ln1_b, w_up, w_fc, b_fc, w_down, ln2_g, ln2_b,
                gx=None, wx=None, x_b=None):
    t, d = x.shape
    wc = d // 2
    dh = (d - wc) // NH
    wm = NH * dh
    dff = w_fc.shape[1] // 2
    if wx is None:
        wx = _Ready(w_out=w_out, w_up=w_up, w_down=w_down)
    ninp = 3 * wc + 4 * wm + LANE
    nin = 3 * wc + 4 * wm
    gate_tile = nin // LANE
    nc = t // CHUNK
    bias_tile = jnp.pad(b_gates, ((0, 0), (0, LANE - 2 * NH)))

    if x_b is None:
        x_b = x.astype(BF16)
    proj = _matmul(x_b, w_in, "nt", F32, "proj", tm=512, tn=2432, tk=d, n=ninp, after=wx.begin(w_in))
    y = _sconv_fwd(proj, w_sc, t, wc)
    gcol = _gates_prep(proj, bias_tile, t, gate_tile)
    grow = gcol[:, :8].T.reshape(8, nc, CHUNK).transpose(1, 0, 2)
    hval, cs, ns = _mlstm_fwd(proj, gcol, grow, t, wc, dh)
    y = _hnorm_fwd(hval, proj, gain, y, t, wc, dh)
    tok = wx.forward("w_out", y)
    w_out = wx.get("w_out", tok)
    mix = _matmul(y, w_out, "nn", F32, "out_proj", tm=512, tn=1024, tk=wc, a_blocked=True, after=tok)
    xhat1, rstd1, x1_b = _ln1_fwd(x, mix, _behind(ln1_g, wx.forward("w_up", mix)), ln1_b)
    w_up = wx.get("w_up", x1_b)
    wsl = w_up.shape[2]
    hid0 = _matmul(x1_b, w_up, "nn", F32, "ffn_up", tm=1024, tn=wsl, tk=d, b_blocked=True)
    act = _ffn_act_fwd(hid0, w_fc, _behind(b_fc, wx.forward("w_down", hid0)), t, dff)
    w_down = wx.get("w_down", act)
    ff = _matmul(act, w_down, "nn", F32, "ffn_down", tm=1024, tn=512, tk=dff)
    dz2, dz2_b, d_ln2_g, d_ln2_b, loss = _ln2_loss(xhat1, ln1_g, ln1_b, ff, target, ln2_g, ln2_b)

    if gx is None:
        gx = _Kept()
    d_w_down = _matmul(act, dz2_b, "tn", BF16, "ffn_down_dw", tm=1408, tn=1024, tk=t)
    d_act = _matmul(dz2_b, w_down, "nt", F32, "ffn_down_dx", tm=2048, tn=512, tk=d, after=gx.start("w_down", d_w_down))
    *d_hid0, dwv, dwg, dbv, dbg = _ffn_act_bwd(d_act, hid0, w_fc, _behind(b_fc, gx.relay("w_down", d_act)), t, dff)
    d_w_fc = jnp.concatenate([dwv, dwg], axis=1)
    d_b_fc = jnp.concatenate([dbv, dbg], axis=1)
    d_hid0 = tuple(d_hid0[:2])
    d_w_up = _matmul(x1_b, d_hid0, "tn", BF16, "ffn_up_dw", tm=1024, tn=wsl, tk=t, o_width=wsl)
    d_x1_ffn = _matmul(d_hid0, w_up, "nt", F32, "ffn_up_dx", tm=1024, tn=1024, tk=wsl, b_blocked=True,
                       after=gx.start("w_up", d_w_up))
    dz1, dz1_b, d_ln1_g, d_ln1_b = _ln1_bwd(dz2, d_x1_ffn, xhat1, rstd1, _behind(ln1_g, gx.relay("w_up", d_x1_ffn)))

    d_w_out = _matmul(y, dz1_b, "tn", BF16, "out_proj_dw", tm=1024, tn=1024, tk=t, a_blocked=True)
    dy = _matmul(dz1_b, w_out, "nt", F32, "out_proj_dx", tm=1024, tn=1024, tk=d, after=gx.start("w_out", d_w_out))
    dcb, dcc, dch, d_w_sc = _sconv_bwd(dy, proj, _behind(w_sc, gx.relay("w_out", dy)), t, wc)
    d_o, d_hval, d_gain = _hnorm_bwd(dy, hval, proj, gain, t, wc, dh)
    dq, dk, dv, dgate = _mlstm_bwd(proj, gcol, grow, hval, d_hval, cs, ns, t, wc, dh)
    dgt, d_b_gates = _gates_bwd(dgate, proj, bias_tile, t, gate_tile)
    d_proj = jnp.concatenate([dcb, dcc, dch, dq, dk, dv, d_o, dgt], axis=1)
    d_w_in = _matmul(d_proj, x_b, "tn", BF16, "proj_dw", tm=2432, tn=1024, tk=t)
    small = dict(b_gates=d_b_gates[:, :2 * NH], w_sc_conv=d_w_sc, mh_gain=d_gain, ln1_g=d_ln1_g, ln1_b=d_ln1_b,
                 w_ffn_conv=d_w_fc, b_ffn_conv=d_b_fc, ln2_g=d_ln2_g, ln2_b=d_ln2_b)
    token = gx.start("w_in", d_w_in)
    token = gx.relay("w_in", gx.meanwhile(small, loss, token))
    grad_x = _matmul(d_proj, w_in, "nn", F32, "proj_dx", tm=512, tn=512, tk=ninp, add=dz1, add_scale=ALPHA, after=token)
    return loss, grad_x, small, gx


HBM = pl.BlockSpec(memory_space=pltpu.HBM)


def _place():
    return lax.axis_index("x"), lax.axis_index("y"), lax.axis_index("c")


def _index(p):
    return 4 * p[0] + 2 * p[1] + p[2]


def _all_gather(arrs, name):
    n = len(arrs)

    def body(*refs):
        ins, outs = refs[:n], refs[n:2 * n]
        send_sems, recv_sems, local_sems = refs[2 * n:]
        x, y, c = _place()
        me, sibling = (x, y, c), (x, y, 1 - c)
        chips = [(1 - x, y), (x, 1 - y), (1 - x, 1 - y)]

        def copy(a, k, block, to, own=False):
            dst = outs[a].at[_index(block)]
            return pltpu.make_async_remote_copy(
                src_ref=ins[a] if own else dst, dst_ref=dst,
                send_sem=send_sems.at[k * n + a], recv_sem=recv_sems.at[k * n + a],
                device_id=to, device_id_type=MESH)

        mine = [pltpu.make_async_copy(ins[a], outs[a].at[_index(me)], local_sems.at[a]) for a in range(n)]
        for cp in mine:
            cp.start()
        first = []
        for a in range(n):
            first.append(copy(a, 0, me, sibling, own=True))
            first += [copy(a, 1 + j, me, (*chip, c), own=True) for j, chip in enumerate(chips)]
        for cp in first:
            cp.start()
        passed = []
        for j, chip in enumerate(chips):
            for a in range(n):
                copy(a, 1 + j, (*chip, c), me).wait_recv()
                cp = copy(a, 4 + j, (*chip, c), sibling)
                cp.start()
                passed.append(cp)
        for a in range(n):
            copy(a, 0, sibling, me).wait_recv()
            for j, chip in enumerate(chips):
                copy(a, 4 + j, (*chip, 1 - c), me).wait_recv()
        for cp in first + passed:
            cp.wait_send()
        for cp in mine:
            cp.wait()

    return pl.pallas_call(
        body, name=name, in_specs=[HBM] * n, out_specs=[HBM] * n,
        out_shape=[jax.ShapeDtypeStruct((N_DEV,) + a.shape, a.dtype) for a in arrs],
        scratch_shapes=[pltpu.SemaphoreType.DMA((7 * n,)), pltpu.SemaphoreType.DMA((7 * n,)),
                        pltpu.SemaphoreType.DMA((n,))],
    )(*arrs)


SEM = pl.BlockSpec(memory_space=pltpu.SEMAPHORE)
EFFECT = pltpu.SideEffectType.DATAFLOW_SIDE_EFFECTING


def _chips(x, y):
    return [(1 - x, y), (x, 1 - y), (1 - x, 1 - y)]


N_CHIP = N_DEV // 2


def _pair_route(x, y, c):
    return [((x, y, 1 - c), 2 * q + (1 - c), q, q) for q in range(N_CHIP)]


def _chip_route(x, y, c):
    mine = 2 * x + y
    return [((*chip, c), 2 * chip[0] + chip[1], mine, 2 * chip[0] + chip[1]) for chip in _chips(x, y)]


def _exchange_pieces(g_ref, land_ref, width, tail):
    if not tail:
        return [(lambda i: g_ref.at[i], lambda s: land_ref.at[s])]
    rows = lambda i, n: pl.ds(pl.multiple_of(i * width, IN_TAIL), n)
    return [(lambda i: g_ref.at[rows(i, width), :], lambda s: land_ref.at[s, pl.ds(0, width), :]),
            (lambda i: g_ref.at[rows(i + 1, IN_TAIL), :], lambda s: land_ref.at[s, pl.ds(width, IN_TAIL), :])]


def _exchange_start(grad, route, tail, name):
    width = IN_SLAB if tail else grad.shape[1]
    n_p = 2 if tail else 1
    n_c = len(route(0, 0, 0))
    land_shape = (N_CHIP, width + (IN_TAIL if tail else 0), grad.shape[-1])

    def body(g_ref, land_ref, send_sems, recv_sems, g_thru, land_thru, token):
        for j, (peer, slab, slot, _) in enumerate(route(*_place())):
            for p, (src, dst) in enumerate(_exchange_pieces(g_ref, land_ref, width, tail)):
                pltpu.make_async_remote_copy(src_ref=src(slab), dst_ref=dst(slot), send_sem=send_sems.at[j * n_p + p],
                                             recv_sem=recv_sems.at[j * n_p + p], device_id=peer,
                                             device_id_type=MESH).start()
        token[...] = jnp.zeros_like(token)

    return pl.pallas_call(
        body, name=name,
        out_shape=(pltpu.SemaphoreType.DMA((n_c * n_p,)), pltpu.SemaphoreType.DMA((n_c * n_p,)),
                   pltpu.HBM(grad.shape, grad.dtype), pltpu.HBM(land_shape, grad.dtype),
                   jax.ShapeDtypeStruct((8, LANE), F32)),
        in_specs=(HBM, HBM), out_specs=(SEM, SEM, HBM, HBM, pl.BlockSpec(memory_space=pltpu.VMEM)),
        input_output_aliases={0: 2, 1: 3},
        compiler_params=pltpu.CompilerParams(has_side_effects=EFFECT),
    )(pltpu.with_memory_space_constraint(grad, pltpu.HBM),
      pltpu.with_memory_space_constraint(lax.empty(land_shape, grad.dtype), pltpu.HBM))


def _exchange_wait(send_sems, recv_sems, g_thru, land_thru, after, route, tail, name):
    width = IN_SLAB if tail else g_thru.shape[1]
    n_p = 2 if tail else 1

    def body(g_ref, land_ref, send_sems, recv_sems, after_ref, g_dead, got_ref):
        for j, (peer, slab, _, slot) in enumerate(route(*_place())):
            for p, (src, dst) in enumerate(_exchange_pieces(g_ref, land_ref, width, tail)):
                cp = pltpu.make_async_remote_copy(src_ref=src(slab), dst_ref=dst(slot),
                                                  send_sem=send_sems.at[j * n_p + p], recv_sem=recv_sems.at[j * n_p + p],
                                                  device_id=peer, device_id_type=MESH)
                cp.wait_send()
                cp.wait_recv()

    return pl.pallas_call(
        body, name=name,
        out_shape=(pltpu.HBM(g_thru.shape, g_thru.dtype), pltpu.HBM(land_thru.shape, land_thru.dtype)),
        in_specs=(HBM, HBM, SEM, SEM, pl.BlockSpec(memory_space=pl.ANY)), out_specs=(HBM, HBM),
        input_output_aliases={0: 0, 1: 1},
        compiler_params=pltpu.CompilerParams(has_side_effects=EFFECT),
    )(g_thru, land_thru, send_sems, recv_sems, after)


def _pair_add(grad, pair, core, tail, name):
    rows, cols = (IN_SLAB if tail else grad.shape[1]), grad.shape[-1]
    total = pair.shape[1]

    def body(core_ref, *refs):
        if tail:
            g_ref, t_ref, p_ref, o_ref = refs
            o_ref[0:rows, :] = (g_ref[...].astype(F32) + p_ref[0:rows, :].astype(F32)).astype(BF16)
            o_ref[rows:total, :] = (t_ref[...].astype(F32) + p_ref[rows:total, :].astype(F32)).astype(BF16)
        else:
            g_ref, p_ref, o_ref = refs
            o_ref[...] = (g_ref[...].astype(F32) + p_ref[...].astype(F32)).astype(BF16)

    if tail:
        tc = _fit(cols, 512)
        grid = (N_CHIP, cols // tc)
        slab = pl.BlockSpec((None, total, tc), lambda q, i, core_ref: (q, 0, i))
        per = IN_SLAB // IN_TAIL
        in_specs = [pl.BlockSpec((rows, tc), lambda q, i, core_ref: (2 * q + core_ref[0], i)),
                    pl.BlockSpec((IN_TAIL, tc), lambda q, i, core_ref: ((2 * q + core_ref[0] + 1) * per, i))]
    else:
        tr = _rows(rows, 1024)
        grid = (N_CHIP, rows // tr)
        slab = pl.BlockSpec((None, tr, cols), lambda q, i, core_ref: (q, i, 0))
        in_specs = [pl.BlockSpec((None, tr, cols), lambda q, i, core_ref: (2 * q + core_ref[0], i, 0))]
    return pl.pallas_call(
        body, name=name,
        grid_spec=pltpu.PrefetchScalarGridSpec(num_scalar_prefetch=1, grid=grid,
                                               in_specs=in_specs + [slab], out_specs=slab),
        out_shape=jax.ShapeDtypeStruct(pair.shape, BF16),
        compiler_params=_params("parallel", "parallel"),
    )(core, *([grad, grad] if tail else [grad]), pair)


def _relay_places(x, y, c):
    came_from = (c * (1 - x) + (1 - c) * x, c * y + (1 - c) * (1 - y), c)
    pass_to = (c * x + (1 - c) * (1 - x), c * (1 - y) + (1 - c) * y, c)
    return 2 - c, came_from, pass_to, pass_to


def _gather_start(blocks, after, name, spare=(), relayed=False):
    n = len(blocks)
    lands = [(N_DEV + (a in spare),) + b.shape for a, b in enumerate(blocks)]

    def body(*refs):
        b_refs, land_refs = refs[:n], refs[n:2 * n]
        send_sems, recv_sems = refs[2 * n + 1:3 * n + 1], refs[3 * n + 1:4 * n + 1]
        token = refs[-1]
        x, y, c = _place()
        me = _index((x, y, c))
        for a in range(n):
            targets = [(x, y, 1 - c)] + [(*chip, c) for chip in _chips(x, y)]
            for k, to in enumerate(targets[:3] if relayed else targets):
                pltpu.make_async_remote_copy(src_ref=b_refs[a], dst_ref=land_refs[a].at[me], send_sem=send_sems[a].at[k],
                                             recv_sem=recv_sems[a].at[k], device_id=to, device_id_type=MESH).start()
        token[...] = jnp.zeros_like(token)

    sems = [pltpu.SemaphoreType.DMA((4,))] * n
    out = pl.pallas_call(
        body, name=name,
        out_shape=(*sems, *sems, *[pltpu.HBM(b.shape, b.dtype) for b in blocks],
                   *[pltpu.HBM(s, b.dtype) for s, b in zip(lands, blocks)], jax.ShapeDtypeStruct((8, LANE), F32)),
        in_specs=(*[HBM] * (2 * n), pl.BlockSpec(memory_space=pl.ANY)),
        out_specs=(*[SEM] * (2 * n), *[HBM] * (2 * n), pl.BlockSpec(memory_space=pltpu.VMEM)),
        input_output_aliases={i: 2 * n + i for i in range(2 * n)},
        compiler_params=pltpu.CompilerParams(has_side_effects=EFFECT),
    )(*[pltpu.with_memory_space_constraint(b, pltpu.HBM) for b in blocks],
      *[pltpu.with_memory_space_constraint(lax.empty(s, b.dtype), pltpu.HBM) for s, b in zip(lands, blocks)], after)
    return [(out[a], out[n + a], out[2 * n + a], out[3 * n + a]) for a in range(n)], out[-1]


def _gather_relay(states, after, name):
    n = len(states)

    def body(*refs):
        land_refs, send_sems, recv_sems = refs[:n], refs[n:2 * n], refs[2 * n:3 * n]
        pass_send, pass_recv = refs[4 * n + 1:5 * n + 1], refs[5 * n + 1:6 * n + 1]
        k_in, came_from, pass_to, _ = _relay_places(*_place())
        for a in range(n):
            slot = land_refs[a].at[_index(came_from)]
            pltpu.make_async_remote_copy(src_ref=slot, dst_ref=slot, send_sem=send_sems[a].at[k_in],
                                         recv_sem=recv_sems[a].at[k_in], device_id=came_from,
                                         device_id_type=MESH).wait_recv()
        for a in range(n):
            slot = land_refs[a].at[_index(came_from)]
            pltpu.make_async_remote_copy(src_ref=slot, dst_ref=slot, send_sem=pass_send[a].at[0],
                                         recv_sem=pass_recv[a].at[0], device_id=pass_to, device_id_type=MESH).start()
        refs[-1][...] = jnp.zeros_like(refs[-1])

    lands = [st[3] for st in states]
    pair = [pltpu.SemaphoreType.DMA((1,))] * n
    out = pl.pallas_call(
        body, name=name,
        out_shape=(*[pltpu.HBM(l.shape, l.dtype) for l in lands], *pair, *pair, jax.ShapeDtypeStruct((8, LANE), F32)),
        in_specs=(*[HBM] * n, *[SEM] * (2 * n), pl.BlockSpec(memory_space=pl.ANY)),
        out_specs=(*[HBM] * n, *[SEM] * (2 * n), pl.BlockSpec(memory_space=pltpu.VMEM)),
        input_output_aliases={a: a for a in range(n)},
        compiler_params=pltpu.CompilerParams(has_side_effects=EFFECT),
    )(*lands, *[st[0] for st in states], *[st[1] for st in states], after)
    return [(st[0], st[1], st[2], out[a], (out[n + a], out[2 * n + a])) for a, st in enumerate(states)], out[-1]


def _gather_forward(send_sems, recv_sems, b_thru, land_thru, after, name, passed=None):
    relayed = passed is not None

    def body(b_ref, land_ref, send_sems, recv_sems, *rest):
        pass_send, pass_recv = rest[:2] if relayed else (None, None)
        send2, recv2, token = rest[-3:]
        x, y, c = _place()
        sibling = (x, y, 1 - c)
        arrivals = [sibling] + [(*chip, c) for chip in _chips(x, y)]
        waits = [(send_sems.at[k], recv_sems.at[k], frm) for k, frm in enumerate(arrivals)]
        sends = [send_sems.at[k] for k in range(4)]
        if relayed:
            k_in, _, _, other = _relay_places(x, y, c)
            waits = [waits[0], (send_sems.at[3 - k_in], recv_sems.at[3 - k_in], other),
                     (pass_send.at[0], pass_recv.at[0], arrivals[3])]
            sends[3] = pass_send.at[0]
        for sem in sends:
            pltpu.make_async_remote_copy(src_ref=b_ref, dst_ref=land_ref.at[0], send_sem=sem, recv_sem=recv_sems.at[0],
                                         device_id=sibling, device_id_type=MESH).wait_send()
        for send_sem, recv_sem, frm in waits:
            pltpu.make_async_remote_copy(src_ref=b_ref, dst_ref=land_ref.at[_index(frm)], send_sem=send_sem,
                                         recv_sem=recv_sem, device_id=frm, device_id_type=MESH).wait_recv()
        for j, chip in enumerate(_chips(x, y)):
            slot = land_ref.at[_index((*chip, c))]
            pltpu.make_async_remote_copy(src_ref=slot, dst_ref=slot, send_sem=send2.at[j], recv_sem=recv2.at[j],
                                         device_id=sibling, device_id_type=MESH).start()
        token[...] = jnp.zeros_like(token)

    extra = list(passed) if relayed else []
    return pl.pallas_call(
        body, name=name,
        out_shape=(pltpu.HBM(b_thru.shape, b_thru.dtype), pltpu.HBM(land_thru.shape, land_thru.dtype),
                   pltpu.SemaphoreType.DMA((3,)), pltpu.SemaphoreType.DMA((3,)), jax.ShapeDtypeStruct((8, LANE), F32)),
        in_specs=(HBM, HBM, SEM, SEM, *[SEM] * len(extra), pl.BlockSpec(memory_space=pl.ANY)),
        out_specs=(HBM, HBM, SEM, SEM, pl.BlockSpec(memory_space=pltpu.VMEM)),
        input_output_aliases={0: 0, 1: 1},
        compiler_params=pltpu.CompilerParams(has_side_effects=EFFECT),
    )(b_thru, land_thru, send_sems, recv_sems, *extra, after)


def _gather_finish(land_thru, send2, recv2, after, name):
    def body(land_ref, send2, recv2, after_ref, land_out):
        x, y, c = _place()
        for j, chip in enumerate(_chips(x, y)):
            cp = pltpu.make_async_remote_copy(src_ref=land_ref.at[_index((*chip, c))],
                                              dst_ref=land_ref.at[_index((*chip, 1 - c))], send_sem=send2.at[j],
                                              recv_sem=recv2.at[j], device_id=(x, y, 1 - c), device_id_type=MESH)
            cp.wait_send()
            cp.wait_recv()

    return pl.pallas_call(
        body, name=name, out_shape=pltpu.HBM(land_thru.shape, land_thru.dtype),
        in_specs=(HBM, SEM, SEM, pl.BlockSpec(memory_space=pl.ANY)), out_specs=HBM,
        input_output_aliases={0: 0},
        compiler_params=pltpu.CompilerParams(has_side_effects=EFFECT),
    )(land_thru, send2, recv2, after)


class _Gathering:
    def __init__(self, first, later, me):
        started, self.token = _gather_start(list(first.values()), next(iter(first.values())), "gather1_first",
                                            spare=(0,), relayed=True)
        self.me, self.state, self.relayed, self.later = me, dict(zip(first, started)), tuple(first), later

    def begin(self, after):
        return self.token

    def relay(self, after):
        states, token = _gather_relay([self.state[n] for n in self.relayed], after, "gather_relay")
        self.state.update(zip(self.relayed, states))
        cast = [_behind(a, token).astype(BF16) for a in self.later.values()]
        started, self.token = _gather_start(cast, token, "gather1_later")
        self.state.update(zip(self.later, started))
        return self.token

    def forward(self, name, after):
        first_leg, passed = self.state[name][:4], (self.state[name][4:] or (None,))[0]
        *self.state[name], token = _gather_forward(*first_leg, after, "gather2_" + name, passed=passed)
        return token

    def get(self, name, after):
        block, land, send2, recv2 = self.state[name]
        land = _gather_finish(land, send2, recv2, after, "gather3_" + name)
        land = lax.dynamic_update_index_in_dim(land, block[None], self.me, 0)
        return land if name not in ("w_out", "w_down") else land.reshape(-1, land.shape[2])


class _Reducing:
    def __init__(self, core, chip, gather_small):
        self.core, self.chip, self.state, self.token, self.gather_small = core, chip, {}, None, gather_small

    def meanwhile(self, small, loss, after):
        self.small_sum = self.gather_small(small, loss, after)
        return self.small_sum

    def start(self, name, grad):
        tail = name == "w_in"
        g = grad if tail or grad.ndim == 3 else grad.reshape(N_DEV, grad.shape[0] // N_DEV, grad.shape[1])
        *self.state[name], token = _exchange_start(g, _pair_route, tail, "pair_send_" + name)
        return token

    def relay(self, name, after):
        tail = name == "w_in"
        grad, pair = _exchange_wait(*self.state[name], after, _pair_route, tail, "pair_recv_" + name)
        total = _pair_add(grad, pair, self.core, tail, "pair_add_" + name)
        *self.state[name], self.token = _exchange_start(total, _chip_route, False, "chip_send_" + name)
        return self.token

    def finish(self, name, after):
        total, land = _exchange_wait(*self.state[name], after, _chip_route, False, "chip_recv_" + name)
        own = lax.dynamic_index_in_dim(total, self.chip, 0, keepdims=True)
        return lax.dynamic_update_index_in_dim(land, own, self.chip, 0)


def _carry_w_in(main, tail):
    slabs, _, d = main.shape
    tc = _fit(d, 2048)
    assert slabs == N_DEV + 1 and tail.shape[:2] == (N_DEV, IN_TAIL), (main.shape, tail.shape)
    top = lambda off: pl.BlockSpec((None, IN_TAIL, tc), lambda s, j: (s + off, 0, j))

    def carry(m_ref, t_ref, o_ref):
        o_ref[...] = m_ref[...] + t_ref[...]

    main = pl.pallas_call(
        carry, name="carry_w_in", grid=(N_DEV - 1, d // tc), in_specs=[top(1), top(0)], out_specs=top(1),
        out_shape=jax.ShapeDtypeStruct(main.shape, main.dtype), input_output_aliases={0: 0},
        compiler_params=_params("parallel", "parallel"),
    )(main, tail)

    def last(m_ref, t_ref, o_ref):
        o_ref[...] = jnp.zeros_like(o_ref)
        o_ref[0:IN_TAIL, :] = t_ref[...]

    return pl.pallas_call(
        last, name="last_slab_w_in", grid=(d // tc,),
        in_specs=[pl.BlockSpec(memory_space=pl.ANY), pl.BlockSpec((None, IN_TAIL, tc), lambda j: (N_DEV - 1, 0, j))],
        out_specs=pl.BlockSpec((None, LANE, tc), lambda j: (N_DEV, 0, j)),
        out_shape=jax.ShapeDtypeStruct(main.shape, main.dtype), input_output_aliases={0: 0},
        compiler_params=_params("parallel"),
    )(main, tail)


def _rows(n, want):
    t = min(n, want)
    t -= t % 16
    while n % t:
        t -= 16
    return t


def _adam_math(w, g, m, v):
    m2 = ADAM_B1 * m + (1.0 - ADAM_B1) * g
    v2 = ADAM_B2 * v + (1.0 - ADAM_B2) * (g * g)
    m_hat = m2 * (1.0 / (1.0 - ADAM_B1 ** ADAM_STEP))
    v_hat = v2 * (1.0 / (1.0 - ADAM_B2 ** ADAM_STEP))
    return -ADAM_LR * (m_hat / (jnp.sqrt(v_hat) + ADAM_EPS) + ADAM_WD * w), m2, v2


def _slot_sum(r_ref):
    acc = r_ref[0].astype(F32)
    for i in range(1, r_ref.shape[0]):
        acc = acc + r_ref[i].astype(F32)
    return acc


def _shift_w_in(w):
    ws, d = w.shape
    tc = _fit(d, 256)

    def body(w_ref, main_ref, tail_ref, tall):
        tall[...] = jnp.zeros_like(tall)
        tall[0:ws, :] = w_ref[...]
        moved = pltpu.roll(tall[...], _index(_place()), 0).astype(BF16)
        main_ref[...] = moved[0:IN_SLAB]
        tail_ref[...] = moved[IN_SLAB:]

    return pl.pallas_call(
        body, name="shift_w_in", grid=(d // tc,),
        in_specs=[pl.BlockSpec((ws, tc), lambda j: (0, j))],
        out_specs=[pl.BlockSpec((IN_SLAB, tc), lambda j: (0, j)), pl.BlockSpec((IN_TAIL, tc), lambda j: (0, j))],
        out_shape=[jax.ShapeDtypeStruct((IN_SLAB, d), BF16), jax.ShapeDtypeStruct((IN_TAIL, d), BF16)],
        scratch_shapes=[pltpu.VMEM((IN_SLAB + IN_TAIL, tc), F32)], compiler_params=_params("parallel"),
    )(w)


def _sum_adamw_shifted(r, w, m, v, name):
    _, ph, d = r.shape
    ws = w.shape[0]
    tc = _fit(d, 256)

    def body(r_ref, w_ref, m_ref, v_ref, g_ref, d_ref, m2_ref, v2_ref, tall):
        tall[...] = pltpu.roll(_slot_sum(r_ref), lax.rem(ph - _index(_place()), ph), 0)
        g = tall[0:ws, :]
        g_ref[...] = g
        d_ref[...], m2_ref[...], v2_ref[...] = _adam_math(w_ref[...], g, m_ref[...], v_ref[...])

    blk = pl.BlockSpec((ws, tc), lambda j: (0, j))
    out = jax.ShapeDtypeStruct(w.shape, F32)
    return pl.pallas_call(
        body, name=name, grid=(d // tc,),
        in_specs=[pl.BlockSpec((r.shape[0], ph, tc), lambda j: (0, 0, j)), blk, blk, blk],
        out_specs=[blk] * 4, out_shape=[out] * 4,
        scratch_shapes=[pltpu.VMEM((ph, tc), F32)], compiler_params=_params("parallel"),
    )(r, w, m, v)


def _sum_slots(r, name, tr=128):
    _, rows, cols = r.shape
    tr = _rows(rows, tr)

    def body(r_ref, g_ref):
        g_ref[...] = _slot_sum(r_ref)

    return pl.pallas_call(
        body, name=name, grid=(rows // tr,),
        in_specs=[pl.BlockSpec((r.shape[0], tr, cols), lambda i: (0, i, 0))],
        out_specs=pl.BlockSpec((tr, cols), lambda i: (i, 0)),
        out_shape=jax.ShapeDtypeStruct((rows, cols), F32),
        compiler_params=_params("parallel"),
    )(r)


def _adamw(w, g, m, v, name, tr=256):
    rows, cols = w.shape
    tr = _rows(rows, tr)

    def body(w_ref, g_ref, m_ref, v_ref, d_ref, m2_ref, v2_ref):
        d_ref[...], m2_ref[...], v2_ref[...] = _adam_math(w_ref[...], g_ref[...], m_ref[...], v_ref[...])

    blk = pl.BlockSpec((tr, cols), lambda i: (i, 0))
    out = jax.ShapeDtypeStruct((rows, cols), F32)
    return pl.pallas_call(
        body, name=name, grid=(rows // tr,), in_specs=[blk] * 4, out_specs=[blk] * 3, out_shape=[out] * 3,
        compiler_params=_params("parallel"),
    )(w, g, m, v)


def _sum_adamw(r, w, m, v, name, tr=256):
    rows, cols = w.shape
    tr = _rows(rows, tr)

    def body(r_ref, w_ref, m_ref, v_ref, g_ref, d_ref, m2_ref, v2_ref):
        g = _slot_sum(r_ref)
        g_ref[...] = g
        d_ref[...], m2_ref[...], v2_ref[...] = _adam_math(w_ref[...], g, m_ref[...], v_ref[...])

    blk = pl.BlockSpec((tr, cols), lambda i: (i, 0))
    out = jax.ShapeDtypeStruct((rows, cols), F32)
    return pl.pallas_call(
        body, name=name, grid=(rows // tr,),
        in_specs=[pl.BlockSpec((r.shape[0], tr, cols), lambda i: (0, i, 0)), blk, blk, blk],
        out_specs=[blk] * 4, out_shape=[out] * 4,
        compiler_params=_params("parallel"),
    )(r, w, m, v)


def _pack(pieces, sizes):
    flat = [jnp.pad(p.reshape(-1).astype(F32), (0, s - p.size)) for p, s in zip(pieces, sizes)]
    total = sum(sizes)
    padded = -(-total // (16 * LANE)) * (16 * LANE)
    return jnp.pad(jnp.concatenate(flat), (0, padded - total)).reshape(-1, LANE)


def _unpack(packed, shapes, sizes):
    flat = packed.reshape(-1)
    out, off = [], 0
    for shp, s in zip(shapes, sizes):
        n = 1
        for k in shp:
            n *= k
        out.append(flat[off:off + n].reshape(shp))
        off += s
    return out


def _lanes(n):
    return -(-n // LANE) * LANE


WEIGHTS = ("w_in", "b_gates", "w_sc_conv", "mh_gain", "w_out", "ln1_g", "ln1_b", "w_up", "w_ffn_conv", "b_ffn_conv",
           "w_down", "ln2_g", "ln2_b")
BIG = ("w_in", "w_out", "w_up", "w_down")
SMALL = tuple(n for n in WEIGHTS if n not in BIG)


def kernel(x, w_in, b_gates, w_sc_conv, mh_gain, w_out, ln1_g, ln1_b, w_up, w_ffn_conv, b_ffn_conv, w_down, ln2_g, ln2_b, loss_target, m_w_in, m_b_gates, m_w_sc_conv, m_mh_gain, m_w_out, m_ln1_g, m_ln1_b, m_w_up, m_w_ffn_conv, m_b_ffn_conv, m_w_down, m_ln2_g, m_ln2_b, v_w_in, v_b_gates, v_w_sc_conv, v_mh_gain, v_w_out, v_ln1_g, v_ln1_b, v_w_up, v_w_ffn_conv, v_b_ffn_conv, v_w_down, v_ln2_g, v_ln2_b):
    w = dict(zip(WEIGHTS, (w_in, b_gates, w_sc_conv, mh_gain, w_out, ln1_g, ln1_b, w_up, w_ffn_conv, b_ffn_conv,
                           w_down, ln2_g, ln2_b)))
    m = dict(zip(WEIGHTS, (m_w_in, m_b_gates, m_w_sc_conv, m_mh_gain, m_w_out, m_ln1_g, m_ln1_b, m_w_up,
                           m_w_ffn_conv, m_b_ffn_conv, m_w_down, m_ln2_g, m_ln2_b)))
    v = dict(zip(WEIGHTS, (v_w_in, v_b_gates, v_w_sc_conv, v_mh_gain, v_w_out, v_ln1_g, v_ln1_b, v_w_up,
                           v_w_ffn_conv, v_b_ffn_conv, v_w_down, v_ln2_g, v_ln2_b)))
    me = _index(_place())
    d = x.shape[2]
    ws_in = w_in.shape[2]
    assert ws_in == IN_SLAB + 1 and N_DEV <= LANE, w_in.shape
    ninp = (N_DEV + 1) * IN_SLAB
    ws_sc, ws_fc = w_sc_conv.shape[2], w_ffn_conv.shape[2]
    w_in_t, m_in_t, v_in_t = (jnp.transpose(a[0]) for a in (w_in, m_w_in, v_w_in))

    w_in_main, w_in_tail = _shift_w_in(w_in_t)
    taps8 = lambda a: jnp.pad(a[0], ((0, 5), (0, 0)))
    at_once = ("w_in", "w_tail", "w_sc", "w_fc")
    wx = _Gathering(dict(zip(at_once, (w_in_main, w_in_tail, taps8(w_sc_conv), taps8(w_ffn_conv)))),
                    {n: w[n][0] for n in ("w_out", "w_up", "w_down")}, me)
    x_b = _behind(x[0], wx.begin(None)).astype(BF16)
    token = wx.relay(x_b)
    for n in at_once:
        token = wx.forward(n, token)
    g_in, g_tail, g_sc, g_fc = (wx.get(n, token) for n in at_once)
    w_in_full = _carry_w_in(g_in, g_tail).reshape(ninp, d)
    w_sc_full = g_sc[:, :3].transpose(1, 0, 2).reshape(3, N_DEV * ws_sc)
    w_fc_full = g_fc[:, :3].transpose(1, 0, 2).reshape(3, N_DEV * ws_fc)

    xi, yi, ci = _place()
    names = ("loss",) + SMALL
    pieces = {}

    def gather_small(small, loss_t, after):
        pieces.update(small, loss=loss_t[0, :1])
        sizes = [_lanes(pieces[n].size) for n in names]
        (g_small,) = _all_gather([_behind(_pack([pieces[n] for n in names], sizes), after)], "gather_small")
        return _sum_slots(g_small, "sum_small", tr=g_small.shape[1])

    gx = _Reducing(jnp.reshape(ci, (1,)).astype(jnp.int32), 2 * xi + yi, gather_small)
    loss_t, grad_x, small, _ = _local_step(
        x[0], loss_target[0], w_in_full, b_gates, w_sc_full, mh_gain, None, ln1_g, ln1_b, None,
        w_fc_full, b_ffn_conv, None, ln2_g, ln2_b, gx=gx, wx=wx, x_b=x_b)

    grads, deltas, new_m, new_v = {}, {}, {}, {}
    for name in ("w_down", "w_up", "w_out"):
        grads[name], deltas[name], new_m[name], new_v[name] = _sum_adamw(
            gx.finish(name, gx.token), w[name][0], m[name][0], v[name][0], "adamw_" + name)

    summed = _unpack(gx.small_sum, [pieces[n].shape for n in names], [_lanes(pieces[n].size) for n in names])
    full = dict(zip(names, summed))
    full["w_sc_conv"] = lax.dynamic_slice(full["w_sc_conv"], (0, me * ws_sc), (3, ws_sc))
    full["w_ffn_conv"] = lax.dynamic_slice(full["w_ffn_conv"], (0, me * ws_fc), (3, ws_fc))
    for n in SMALL:
        grads[n] = full[n].reshape(w[n].shape)
    sizes = [_lanes(w[n].size) for n in SMALL]
    shapes = [w[n].shape for n in SMALL]
    packed = [_pack([t[n] for n in SMALL], sizes) for t in (w, grads, m, v)]
    small_out = _adamw(*packed, "adamw_small")
    for res, t in zip(small_out, (deltas, new_m, new_v)):
        t.update(zip(SMALL, _unpack(res, shapes, sizes)))

    done = sum(t[0:1, 0:1] for t in (grad_x, deltas["w_down"], deltas["w_up"], deltas["w_out"], small_out[0]))
    grads["w_in"], deltas["w_in"], new_m["w_in"], new_v["w_in"] = (
        jnp.transpose(a)[None] for a in _sum_adamw_shifted(gx.finish("w_in", done), w_in_t, m_in_t, v_in_t, "adamw_w_in"))

    big = lambda t: {n: (t[n].reshape(w[n].shape) if n in BIG else t[n]) for n in WEIGHTS}
    grads, deltas, new_m, new_v = big(grads), big(deltas), big(new_m), big(new_v)
    return (full["loss"].reshape(()), grad_x[None], *[grads[n] for n in WEIGHTS], *[deltas[n] for n in WEIGHTS],
            *[new_m[n] for n in WEIGHTS], *[new_v[n] for n in WEIGHTS])
```

```python
import functools

import jax
import jax.numpy as jnp
from jax import lax
from jax.experimental import pallas as pl
from jax.experimental.pallas import tpu as pltpu

F32 = jnp.float32
BF16 = jnp.bfloat16
MESH = pl.DeviceIdType.MESH

N_DEV = 8
NH = 4
CHUNK = 64
LN_EPS = 1e-5
HN_EPS = 1e-6
ALPHA = 2.0 ** 0.25
LANE = 128
IN_SLAB = 7 * LANE
IN_TAIL = 16
VMEM_LIMIT = 56 * 1024 * 1024
ADAM_LR, ADAM_B1, ADAM_B2, ADAM_EPS, ADAM_WD, ADAM_STEP = 0.001, 0.9, 0.999, 1e-08, 0.01, 10

_NN = (((1,), (0,)), ((), ()))
_NT = (((1,), (1,)), ((), ()))
_TN = (((0,), (0,)), ((), ()))


def _dot(a, b, dn=_NN):
    return lax.dot_general(a, b, dn, preferred_element_type=F32)


def _params(*sem):
    return pltpu.CompilerParams(dimension_semantics=sem if sem else None, vmem_limit_bytes=VMEM_LIMIT)


def _iota(shape, axis):
    return lax.broadcasted_iota(jnp.int32, shape, axis)


def _fit(n, want):
    if n <= want:
        return n
    t = want - want % LANE
    while n % t:
        t -= LANE
    return t


def _matmul(a, b, mode, out_dtype, name, tm=1024, tn=512, tk=1024, add=None, add_scale=1.0,
            a_blocked=False, b_blocked=False, o_width=None, after=None, n=None):
    a_parts = a if isinstance(a, tuple) else None
    b_parts = b if isinstance(b, tuple) else None
    if a_parts:
        a_blocked, (a_rows, wa), na = True, a[0].shape, len(a)
        kd, m = (a_rows, na * wa) if mode == "tn" else (na * wa, a_rows)
    elif a_blocked:
        na, a_rows, wa = a.shape
        kd, m = (a_rows, na * wa) if mode == "tn" else (na * wa, a_rows)
    elif mode == "tn":
        kd, m = a.shape
    else:
        m, kd = a.shape
    if b_parts:
        b_blocked, (rows, w), nb = True, b[0].shape, len(b)
    elif b_blocked:
        nb, rows, w = b.shape
    if b_blocked:
        n = rows if mode == "nt" else nb * w
        assert (nb * w if mode == "nt" else rows) == kd, (name, kd)
    else:
        n = n or (b.shape[0] if mode == "nt" else b.shape[1])
    tm, tn, tk = _fit(m, tm), _fit(n, tn), _fit(kd, tk)
    if a_blocked and mode == "tn":
        tm = _fit(wa, tm)
    if a_blocked and mode != "tn":
        tk = _fit(wa, tk)
    if b_blocked and mode != "nt":
        tn = _fit(w, tn)
    if b_blocked and mode == "nt":
        tk = _fit(w, tk)
    if o_width is not None:
        tn = _fit(o_width, tn)
    assert m % tm == 0 and n % tn == 0 and kd % tk == 0, (name, m, n, kd, tm, tn, tk)
    assert not (a_blocked and mode != "tn" and wa % tk) and not (b_blocked and mode == "nt" and w % tk), (name, tk)
    nk = kd // tk
    dn = {"nn": _NN, "nt": _NT, "tn": _TN}[mode]
    if a_blocked and mode == "tn":
        a_per = wa // tm
        a_spec = pl.BlockSpec((None, tk, tm), lambda i, j, k: (i // a_per, k, i % a_per))
    elif a_blocked:
        a_per = wa // tk
        a_spec = pl.BlockSpec((None, tm, tk), lambda i, j, k: (k // a_per, i, k % a_per))
    elif mode == "tn":
        a_spec = pl.BlockSpec((tk, tm), lambda i, j, k: (k, i))
    else:
        a_spec = pl.BlockSpec((tm, tk), lambda i, j, k: (i, k))
    if b_blocked and mode != "nt":
        per = w // tn
        b_spec = pl.BlockSpec((None, tk, tn), lambda i, j, k: (j // per, k, j % per))
    elif b_blocked:
        per = w // tk
        b_spec = pl.BlockSpec((None, tn, tk), lambda i, j, k: (k // per, j, k % per))
    elif mode == "nt":
        b_spec = pl.BlockSpec((tn, tk), lambda i, j, k: (j, k))
    else:
        b_spec = pl.BlockSpec((tk, tn), lambda i, j, k: (k, j))
    if o_width is None:
        o_spec = pl.BlockSpec((tm, tn), lambda i, j, k: (i, j))
        o_shape = (m, n)
    else:
        oper = o_width // tn
        o_spec = pl.BlockSpec((None, tm, tn), lambda i, j, k: (j // oper, i, j % oper))
        o_shape = (n // o_width, m, o_width)
    a_list, a_specs = [a], [a_spec]
    if a_parts:
        hold = lambda x, s: jnp.clip(x - s * a_per, 0, a_per - 1)
        a_list = list(a_parts)
        a_specs = [(pl.BlockSpec((tk, tm), lambda i, j, k, s=s: (k, hold(i, s))) if mode == "tn"
                    else pl.BlockSpec((tm, tk), lambda i, j, k, s=s: (i, hold(k, s)))) for s in range(na)]
    b_list, b_specs = [b], [b_spec]
    if b_parts:
        hold_b = lambda x, s: jnp.clip(x - s * per, 0, per - 1)
        b_list = list(b_parts)
        b_specs = [(pl.BlockSpec((tn, tk), lambda i, j, k, s=s: (j, hold_b(k, s))) if mode == "nt"
                    else pl.BlockSpec((tk, tn), lambda i, j, k, s=s: (k, hold_b(j, s)))) for s in range(nb)]
    n_a, n_b = len(a_list), len(b_list)
    has_add = add is not None
    n_in = n_a + n_b + has_add + (after is not None)
    in_place = nk > 1 and out_dtype == F32

    def body(*refs):
        add_ref = refs[n_a + n_b] if has_add else None
        o_ref = refs[n_in]
        i, j, k = pl.program_id(0), pl.program_id(1), pl.program_id(2)

        def finish(r):
            if has_add:
                r = r + add_scale * add_ref[...]
            o_ref[...] = r.astype(out_dtype)

        def step(a_ref, b_ref):
            if nk == 1:
                finish(_dot(a_ref[...], b_ref[...], dn))
                return
            acc = o_ref if in_place else refs[-1]

            @pl.when(k == 0)
            def _():
                acc[...] = _dot(a_ref[...], b_ref[...], dn)

            @pl.when(k > 0)
            def _():
                acc[...] += _dot(a_ref[...], b_ref[...], dn)

        if n_a == 1 and n_b == 1:
            step(refs[0], refs[1])
        else:
            slab_a = ((i if mode == "tn" else k) // a_per) if n_a > 1 else 0
            slab_b = ((k if mode == "nt" else j) // per) if n_b > 1 else 0
            for sa in range(n_a):
                for sb in range(n_b):
                    pl.when((slab_a == sa) & (slab_b == sb))(functools.partial(step, refs[sa], refs[n_a + sb]))
        if nk > 1 and not (in_place and not has_add):
            @pl.when(k == nk - 1)
            def _():
                finish((o_ref if in_place else refs[-1])[...])

    in_specs = a_specs + b_specs + ([pl.BlockSpec((tm, tn), lambda i, j, k: (i, j))] if has_add else [])
    args = (*a_list, *b_list) + ((add,) if has_add else ())
    if after is not None:
        in_specs.append(pl.BlockSpec(memory_space=pl.ANY))
        args += (after,)
    return pl.pallas_call(
        body, name=name, grid=(m // tm, n // tn, nk),
        in_specs=in_specs, out_specs=o_spec,
        out_shape=jax.ShapeDtypeStruct(o_shape, out_dtype),
        scratch_shapes=[pltpu.VMEM((tm, tn), F32)] if nk > 1 and not in_place else [],
        compiler_params=_params("parallel", "parallel", "arbitrary"),
    )(*args)


def _shift_down(u, s):
    return jnp.where(_iota(u.shape, 0) >= s, pltpu.roll(u, s, 0), 0.0)


def _shift_up(u, s):
    t = u.shape[0]
    return jnp.where(_iota(u.shape, 0) < t - s, pltpu.roll(u, t - s, 0), 0.0)


SLAB = 8


def _rolled(u):
    return pltpu.roll(u, 2, 0), pltpu.roll(u, 1, 0)


def _conv(u, w, rolled=None):
    u2, u1 = _rolled(u) if rolled is None else rolled
    raw = w[0:1] * u2 + w[1:2] * u1 + w[2:3] * u
    head = u[0:SLAB]
    mended = w[0:1] * _shift_down(head, 2) + w[1:2] * _shift_down(head, 1) + w[2:3] * head
    return jnp.concatenate([mended, raw[SLAB:]], axis=0)


def _conv_t(dy, w):
    t = dy.shape[0]
    raw = w[2:3] * dy + w[1:2] * pltpu.roll(dy, t - 1, 0) + w[0:1] * pltpu.roll(dy, t - 2, 0)
    tail = dy[t - SLAB:]
    mended = w[2:3] * tail + w[1:2] * _shift_up(tail, 1) + w[0:1] * _shift_up(tail, 2)
    return jnp.concatenate([raw[:t - SLAB], mended], axis=0)


def _conv_dw(dy, u, rolled=None):
    t = dy.shape[0]
    u2, u1 = _rolled(u) if rolled is None else rolled
    head, tail = dy[0:SLAB], u[t - SLAB:]
    r = _iota(head.shape, 0)
    wrap2 = jnp.sum(jnp.where(r < 2, head * pltpu.roll(tail, 2, 0), 0.0), axis=0, keepdims=True)
    wrap1 = jnp.sum(jnp.where(r < 1, head * pltpu.roll(tail, 1, 0), 0.0), axis=0, keepdims=True)
    d0 = jnp.sum(dy * u2, axis=0, keepdims=True) - wrap2
    d1 = jnp.sum(dy * u1, axis=0, keepdims=True) - wrap1
    d2 = jnp.sum(dy * u, axis=0, keepdims=True)
    r3 = _iota((3, dy.shape[1]), 0)
    return jnp.where(r3 == 0, d0, jnp.where(r3 == 1, d1, d2))


def _sigmoid(x):
    return 0.5 * jnp.tanh(0.5 * x) + 0.5


def _sconv_fwd(proj, w_sc, t, wc):
    nb = wc // LANE

    def body(cb_ref, cc_ref, ch_ref, w_ref, y_ref):
        u = cc_ref[...] * ch_ref[...]
        y_ref[...] = (cb_ref[...] * _conv(u, w_ref[...])).astype(BF16)

    col = lambda off: pl.BlockSpec((t, LANE), lambda j: (0, j + off))
    return pl.pallas_call(
        body, name="sconv_fwd", grid=(nb,),
        in_specs=[col(0), col(nb), col(2 * nb), pl.BlockSpec((3, LANE), lambda j: (0, j))],
        out_specs=pl.BlockSpec((None, t, LANE), lambda j: (0, 0, j)),
        out_shape=jax.ShapeDtypeStruct((2, t, wc), BF16),
        compiler_params=_params("parallel"),
    )(proj, proj, proj, w_sc)


def _sconv_bwd(dy, proj, w_sc, t, wc):
    nb = wc // LANE

    def body(dy_ref, cb_ref, cc_ref, ch_ref, w_ref, dcb_ref, dcc_ref, dch_ref, dw_ref):
        cc, ch, w, d = cc_ref[...], ch_ref[...], w_ref[...], dy_ref[...]
        u = cc * ch
        ru = _rolled(u)
        dcb_ref[...] = (d * _conv(u, w, ru)).astype(BF16)
        dcu = d * cb_ref[...]
        dw_ref[...] = _conv_dw(dcu, u, ru)
        du = _conv_t(dcu, w)
        dcc_ref[...] = (du * ch).astype(BF16)
        dch_ref[...] = (du * cc).astype(BF16)

    col = lambda off: pl.BlockSpec((t, LANE), lambda j: (0, j + off))
    act = jax.ShapeDtypeStruct((t, wc), BF16)
    return pl.pallas_call(
        body, name="sconv_bwd", grid=(nb,),
        in_specs=[col(0), col(0), col(nb), col(2 * nb), pl.BlockSpec((3, LANE), lambda j: (0, j))],
        out_specs=[col(0), col(0), col(0), pl.BlockSpec((3, LANE), lambda j: (0, j))],
        out_shape=[act, act, act, jax.ShapeDtypeStruct((3, wc), F32)],
        compiler_params=_params("parallel"),
    )(dy, proj, proj, proj, w_sc)


def _gates_prep(proj, bias_tile, t, gate_tile):
    def body(g_ref, b_ref, o_ref):
        g = g_ref[...] + b_ref[...]
        lane = _iota(g.shape, 1)
        is_f = (lane >= NH) & (lane < 2 * NH)
        lf = jnp.minimum(g, 0.0) - jnp.log(1.0 + jnp.exp(-jnp.abs(g)))
        c = jnp.where(is_f, lf, 0.0)
        r = _iota(g.shape, 0) % CHUNK
        s = 1
        while s < CHUNK:
            c = c + jnp.where(r >= s, pltpu.roll(c, s, 0), 0.0)
            s *= 2
        o_ref[...] = jnp.where(is_f, c, jnp.where(lane < NH, g, 0.0))

    return pl.pallas_call(
        body, name="gates_prep", grid=(1,),
        in_specs=[pl.BlockSpec((t, LANE), lambda i: (0, gate_tile)), pl.BlockSpec((1, LANE), lambda i: (0, 0))],
        out_specs=pl.BlockSpec((t, LANE), lambda i: (0, 0)),
        out_shape=jax.ShapeDtypeStruct((t, LANE), F32),
        compiler_params=_params("arbitrary"),
    )(proj, bias_tile)


def _gates_bwd(dgate, proj, bias_tile, t, gate_tile):
    def body(dg_ref, g_ref, b_ref, o_ref, s_ref):
        g = g_ref[...] + b_ref[...]
        lane = _iota(g.shape, 1)
        r = _iota(g.shape, 0) % CHUNK
        dsig = 1.0 - _sigmoid(g)
        out = jnp.zeros(g.shape, F32)
        for h in range(NH):
            d = dg_ref[h]
            c = d
            s = 1
            while s < CHUNK:
                c = c + jnp.where(r + s < CHUNK, pltpu.roll(c, t - s, 0), 0.0)
                s *= 2
            di = jnp.broadcast_to(d[:, 0:1], g.shape)
            db = jnp.broadcast_to(c[:, 1:2], g.shape)
            out = out + jnp.where(lane == h, di, 0.0) + jnp.where(lane == NH + h, db * dsig, 0.0)
        o_ref[...] = out.astype(BF16)
        s_ref[...] = jnp.sum(out, axis=0, keepdims=True)

    return pl.pallas_call(
        body, name="gates_bwd", grid=(1,),
        in_specs=[pl.BlockSpec((NH, t, LANE), lambda i: (0, 0, 0)),
                  pl.BlockSpec((t, LANE), lambda i: (0, gate_tile)), pl.BlockSpec((1, LANE), lambda i: (0, 0))],
        out_specs=[pl.BlockSpec((t, LANE), lambda i: (0, 0)), pl.BlockSpec((1, LANE), lambda i: (0, 0))],
        out_shape=[jax.ShapeDtypeStruct((t, LANE), BF16), jax.ShapeDtypeStruct((1, LANE), F32)],
        compiler_params=_params("arbitrary"),
    )(dgate, proj, bias_tile)


def _in_turn(heads):
    while heads:
        heads = [g for g in heads if next(g, heads) is not heads]


def _chunk_gates(gc, gr, h, mprev):
    L = CHUNK
    icol, bcol = gc[:, h:h + 1], gc[:, h + NH:h + NH + 1]
    irow, brow = gr[h:h + 1, :], gr[h + NH:h + NH + 1, :]
    tri = _iota((L, L), 0) >= _iota((L, L), 1)
    log_d = jnp.where(tri, bcol - brow + irow, -jnp.inf)
    inter = bcol + mprev
    mt = jnp.maximum(inter, jnp.max(log_d, axis=1, keepdims=True))
    dw = jnp.exp(log_d - mt)
    iw = jnp.exp(inter - mt)
    g = brow[:, L - 1:L]
    wlog_col = g - bcol + icol
    wlog_row = g - brow + irow
    mnew = jnp.maximum(g + mprev, jnp.max(wlog_row, axis=1, keepdims=True))
    wcol = jnp.exp(wlog_col - mnew)
    decay = jnp.exp(g + mprev - mnew)
    return dw, iw, mt, wcol, decay, mnew


def _mlstm_fwd(proj, gcol, grow, t, wc, dh):
    nc = t // CHUNK
    wm = NH * dh
    assert wc == wm, (wc, wm)
    qoff = 3 * wc // wm
    scale = dh ** -0.5

    def body(q_ref, k_ref, v_ref, gc_ref, gr_ref, h_ref, cs_ref, ns_ref, c_s, n_s, m_s):
        @pl.when(pl.program_id(0) == 0)
        def _():
            c_s[...] = jnp.zeros_like(c_s)
            n_s[...] = jnp.zeros_like(n_s)
            m_s[...] = jnp.zeros_like(m_s)

        gc, gr = gc_ref[...], gr_ref[0]
        done = [None] * NH

        def head(h):
            cols = slice(h * dh, (h + 1) * dh)
            mprev = m_s[h, 0:1, 0:1]
            cprev = c_s[h]
            n8 = n_s[h]
            nprev = n8[0:1]
            qs = q_ref[:, cols] * scale
            k = k_ref[:, cols]
            qs_b, k_b, v_b = qs.astype(BF16), k.astype(BF16), v_ref[:, cols].astype(BF16)
            qk = _dot(qs_b, k_b, _NT)
            yield
            q_c = _dot(qs_b, cprev.astype(BF16))
            yield
            dw, iw, mt, wcol, decay, mnew = _chunk_gates(gc, gr, h, mprev)
            yield
            s = qk * dw
            wk = wcol * k
            num = _dot(s.astype(BF16), v_b) + iw * q_c
            yield
            c_new = decay * cprev + _dot(wk.astype(BF16), v_b, _TN)
            yield
            den = jnp.sum(s, axis=1, keepdims=True) + iw * jnp.sum(qs * nprev, axis=1, keepdims=True)
            done[h] = (cprev, jnp.where(_iota(n8.shape, 0) == 1, mprev, n8),
                       num / jnp.maximum(jnp.abs(den), jnp.exp(-mt)), c_new,
                       decay * n8 + jnp.sum(wk, axis=0, keepdims=True), mnew)

        _in_turn([head(h) for h in range(NH)])
        for h, (c_old, n_old, h_out, c_new, n_new, m_new) in enumerate(done):
            cs_ref[h] = c_old
            ns_ref[h] = n_old
            h_ref[:, h * dh:(h + 1) * dh] = h_out
            c_s[h] = c_new
            n_s[h] = n_new
            m_s[h] = jnp.broadcast_to(m_new, m_s.shape[1:])

    grp = lambda off: pl.BlockSpec((CHUNK, wm), lambda c: (c, qoff + off))
    return pl.pallas_call(
        body, name="mlstm_fwd", grid=(nc,),
        in_specs=[grp(0), grp(1), grp(2),
                  pl.BlockSpec((CHUNK, LANE), lambda c: (c, 0)),
                  pl.BlockSpec((1, 8, CHUNK), lambda c: (c, 0, 0))],
        out_specs=[pl.BlockSpec((CHUNK, wm), lambda c: (c, 0)),
                   pl.BlockSpec((NH, None, dh, dh), lambda c: (0, c, 0, 0)),
                   pl.BlockSpec((NH, None, 8, dh), lambda c: (0, c, 0, 0))],
        out_shape=[jax.ShapeDtypeStruct((t, wm), F32),
                   jax.ShapeDtypeStruct((NH, nc, dh, dh), F32),
                   jax.ShapeDtypeStruct((NH, nc, 8, dh), F32)],
        scratch_shapes=[pltpu.VMEM((NH, dh, dh), F32), pltpu.VMEM((NH, 8, dh), F32), pltpu.VMEM((NH, 8, LANE), F32)],
        compiler_params=_params("arbitrary"),
    )(proj, proj, proj, gcol, grow)


def _mlstm_bwd(proj, gcol, grow, hval, dh_in, cs, ns, t, wc, dh):
    nc = t // CHUNK
    wm = NH * dh
    assert wc == wm, (wc, wm)
    qoff = 3 * wc // wm
    scale = dh ** -0.5
    L = CHUNK

    def body(q_ref, k_ref, v_ref, gc_ref, gr_ref, h_ref, dh_ref, cs_ref, ns_ref,
             dq_ref, dk_ref, dv_ref, dg_ref, dc_s, dn_s):
        @pl.when(pl.program_id(0) == 0)
        def _():
            dc_s[...] = jnp.zeros_like(dc_s)
            dn_s[...] = jnp.zeros_like(dn_s)

        gc, gr = gc_ref[...], gr_ref[0]
        eye = _iota((L, L), 0) == _iota((L, L), 1)
        lane = _iota((L, LANE), 1)
        last = _iota((L, 1), 0) == L - 1
        done = [None] * NH

        def head(h):
            cols = slice(h * dh, (h + 1) * dh)
            ns8 = ns_ref[h]
            nprev = ns8[0:1]
            mprev = ns8[1:2, 0:1]
            cprev = cs_ref[h]
            dcn = dc_s[h]
            dn8 = dn_s[h]
            dnn = dn8[0:1]

            qs = q_ref[:, cols] * scale
            k = k_ref[:, cols]
            qs_b, k_b, v_b = qs.astype(BF16), k.astype(BF16), v_ref[:, cols].astype(BF16)
            qk = _dot(qs_b, k_b, _NT)
            yield
            dw, iw, mt, wcol, decay, _ = _chunk_gates(gc, gr, h, mprev)
            yield
            s = qk * dw
            den = jnp.sum(s, axis=1, keepdims=True) + iw * jnp.sum(qs * nprev, axis=1, keepdims=True)
            emt = jnp.exp(-mt)
            r = 1.0 / jnp.maximum(jnp.abs(den), emt)
            dout = dh_ref[:, cols]
            dnum = dout * r
            dden = (-jnp.sum(dout * h_ref[:, cols], axis=1, keepdims=True) * r
                    * jnp.where(jnp.abs(den) > emt, jnp.sign(den), 0.0))
            dnum_b = dnum.astype(BF16)
            cprev_b = cprev.astype(BF16)
            dcn_b = dcn.astype(BF16)
            yield

            g_raw = _dot(dnum_b, v_b, _NT)
            yield
            q_inter = _dot(dnum_b, cprev_b, _NT)
            yield
            k_raw = _dot(v_b, dcn_b, _NT)
            yield
            gd = (g_raw + dden) * dw
            gd_b = gd.astype(BF16)
            dqs_inter = iw * (q_inter + dden * nprev)
            dk_inter = wcol * (k_raw + dnn)
            wk = wcol * k
            iq = iw * qs
            dqs = _dot(gd_b, k_b) + dqs_inter
            yield
            dk = _dot(gd_b, qs_b, _TN) + dk_inter
            yield
            dv = _dot(s.astype(BF16), dnum_b, _TN) + _dot(wk.astype(BF16), dcn_b)
            yield
            dc_new = decay * dcn + _dot(iq.astype(BF16), dnum_b, _TN)
            yield

            e = gd * qk
            e_cols = jnp.sum(jnp.where(eye, jnp.sum(e, axis=0, keepdims=True), 0.0), axis=1, keepdims=True)
            yield
            k_inter = jnp.sum(k * dk_inter, axis=1, keepdims=True)
            rq = jnp.sum(e, axis=1, keepdims=True) + jnp.sum(qs * dqs_inter, axis=1, keepdims=True)
            rk = e_cols + k_inter
            hsum = jnp.sum(k_inter, axis=0, keepdims=True)
            jdec = decay * (jnp.sum(jnp.sum(dcn * cprev, axis=1, keepdims=True), axis=0, keepdims=True)
                            + jnp.sum(dnn * nprev, axis=1, keepdims=True))
            db = rq - rk + jnp.where(last, hsum + jdec, 0.0)
            done[h] = (jnp.where(lane == 0, rk, jnp.where(lane == 1, db, 0.0)),
                       (dqs * scale).astype(BF16), dk.astype(BF16), dv.astype(BF16), dc_new,
                       decay * dn8 + jnp.sum(iq * dden, axis=0, keepdims=True))

        _in_turn([head(h) for h in range(NH)])
        for h, (dgate, dq, dk, dv, dc_new, dn_new) in enumerate(done):
            cols = slice(h * dh, (h + 1) * dh)
            dg_ref[h] = dgate
            dq_ref[:, cols] = dq
            dk_ref[:, cols] = dk
            dv_ref[:, cols] = dv
            dc_s[h] = dc_new
            dn_s[h] = dn_new

    rc = lambda c: nc - 1 - c
    grp = lambda off: pl.BlockSpec((L, wm), lambda c: (rc(c), qoff + off))
    hm = pl.BlockSpec((L, wm), lambda c: (rc(c), 0))
    act = jax.ShapeDtypeStruct((t, wm), BF16)
    return pl.pallas_call(
        body, name="mlstm_bwd", grid=(nc,),
        in_specs=[grp(0), grp(1), grp(2),
                  pl.BlockSpec((L, LANE), lambda c: (rc(c), 0)),
                  pl.BlockSpec((1, 8, L), lambda c: (rc(c), 0, 0)),
                  hm, hm,
                  pl.BlockSpec((NH, None, dh, dh), lambda c: (0, rc(c), 0, 0)),
                  pl.BlockSpec((NH, None, 8, dh), lambda c: (0, rc(c), 0, 0))],
        out_specs=[hm, hm, hm, pl.BlockSpec((NH, L, LANE), lambda c: (0, rc(c), 0))],
        out_shape=[act, act, act, jax.ShapeDtypeStruct((NH, t, LANE), F32)],
        scratch_shapes=[pltpu.VMEM((NH, dh, dh), F32), pltpu.VMEM((NH, 8, dh), F32)],
        compiler_params=_params("arbitrary"),
    )(proj, proj, proj, gcol, grow, hval, dh_in, cs, ns)


def _head_norm(hv):
    mu = jnp.mean(hv, axis=1, keepdims=True)
    hc = hv - mu
    rstd = lax.rsqrt(jnp.mean(hc * hc, axis=1, keepdims=True) + HN_EPS)
    return hc * rstd, rstd


def _hnorm_fwd(hval, proj, gain, y, t, wc, dh, tr=512):
    ooff = 3 * wc // dh + 3 * NH
    tr = min(tr, t)

    def body(h_ref, o_ref, g_ref, y_in, y_ref):
        hhat, _ = _head_norm(h_ref[...])
        y_ref[...] = (_sigmoid(o_ref[...]) * hhat * g_ref[...]).astype(BF16)

    return pl.pallas_call(
        body, name="hnorm_fwd", grid=(t // tr, NH),
        in_specs=[pl.BlockSpec((tr, dh), lambda i, h: (i, h)),
                  pl.BlockSpec((tr, dh), lambda i, h: (i, ooff + h)),
                  pl.BlockSpec((1, dh), lambda i, h: (0, h)),
                  pl.BlockSpec(memory_space=pl.ANY)],
        out_specs=pl.BlockSpec((None, tr, dh), lambda i, h: (1, i, h)),
        out_shape=jax.ShapeDtypeStruct(y.shape, BF16),
        input_output_aliases={3: 0},
        compiler_params=_params("parallel", "parallel"),
    )(hval, proj, gain, y)


def _hnorm_bwd(dy, hval, proj, gain, t, wc, dh, tr=512):
    ooff = 3 * wc // dh + 3 * NH
    tr = min(tr, t)
    yoff = wc // dh

    def body(dy_ref, h_ref, o_ref, g_ref, do_ref, dh_ref, dg_ref):
        i = pl.program_id(1)
        hhat, rstd = _head_norm(h_ref[...])
        gain_v = g_ref[...]
        sig = _sigmoid(o_ref[...])
        d = dy_ref[...]
        do_ref[...] = (d * hhat * gain_v * sig * (1.0 - sig)).astype(BF16)
        dhn = d * sig
        part = jnp.sum(dhn * hhat, axis=0, keepdims=True)

        @pl.when(i == 0)
        def _():
            dg_ref[...] = part

        @pl.when(i > 0)
        def _():
            dg_ref[...] += part

        dhat = dhn * gain_v
        dh_ref[...] = rstd * (dhat - jnp.mean(dhat, axis=1, keepdims=True)
                              - hhat * jnp.mean(dhat * hhat, axis=1, keepdims=True))

    blk = lambda off: pl.BlockSpec((tr, dh), lambda h, i: (i, off + h))
    return pl.pallas_call(
        body, name="hnorm_bwd", grid=(NH, t // tr),
        in_specs=[blk(yoff), blk(0), blk(ooff), pl.BlockSpec((1, dh), lambda h, i: (0, h))],
        out_specs=[blk(0), blk(0), pl.BlockSpec((1, dh), lambda h, i: (0, h))],
        out_shape=[jax.ShapeDtypeStruct((t, NH * dh), BF16), jax.ShapeDtypeStruct((t, NH * dh), F32),
                   jax.ShapeDtypeStruct((1, NH * dh), F32)],
        compiler_params=_params("parallel", "arbitrary"),
    )(dy, hval, proj, gain)


def _ln_stats(z):
    mu = jnp.mean(z, axis=1, keepdims=True)
    zc = z - mu
    rstd = lax.rsqrt(jnp.mean(zc * zc, axis=1, keepdims=True) + LN_EPS)
    return zc * rstd, rstd


def _ln_bwd(dy, xhat, rstd, g):
    dxh = dy * g
    return rstd * (dxh - jnp.mean(dxh, axis=1, keepdims=True) - xhat * jnp.mean(dxh * xhat, axis=1, keepdims=True))


def _accum(ref, i, part):
    @pl.when(i == 0)
    def _():
        ref[...] = part

    @pl.when(i > 0)
    def _():
        ref[...] += part


def _ln1_fwd(x, mix, g, b, tr=256):
    t, d = x.shape

    def body(x_ref, m_ref, g_ref, b_ref, xh_ref, rs_ref, xb_ref):
        xhat, rstd = _ln_stats(ALPHA * x_ref[...] + m_ref[...])
        xh_ref[...] = xhat
        rs_ref[...] = rstd
        xb_ref[...] = (xhat * g_ref[...] + b_ref[...]).astype(BF16)

    row = pl.BlockSpec((tr, d), lambda i: (i, 0))
    vec = pl.BlockSpec((1, d), lambda i: (0, 0))
    return pl.pallas_call(
        body, name="ln1_fwd", grid=(t // tr,),
        in_specs=[row, row, vec, vec],
        out_specs=[row, pl.BlockSpec((tr, 1), lambda i: (i, 0)), row],
        out_shape=[jax.ShapeDtypeStruct((t, d), F32), jax.ShapeDtypeStruct((t, 1), F32),
                   jax.ShapeDtypeStruct((t, d), BF16)],
        compiler_params=_params("parallel"),
    )(x, mix, g, b)


def _ln2_loss(xhat1, g1, b1, ff, target, g2, b2, tr=256):
    t, d = ff.shape

    def body(xh_ref, g1_ref, b1_ref, f_ref, t_ref, g_ref, b_ref, dz_ref, dzb_ref, dg_ref, db_ref, l_ref):
        i = pl.program_id(0)
        x1 = xh_ref[...] * g1_ref[...] + b1_ref[...]
        xhat, rstd = _ln_stats(ALPHA * x1 + f_ref[...])
        gv = g_ref[...]
        e = xhat * gv + b_ref[...] - t_ref[...]
        lsum = jnp.sum(jnp.sum(e * e, axis=1, keepdims=True), axis=0, keepdims=True) * (0.5 / d)
        dy = e * (1.0 / d)
        _accum(dg_ref, i, jnp.sum(dy * xhat, axis=0, keepdims=True))
        _accum(db_ref, i, jnp.sum(dy, axis=0, keepdims=True))
        _accum(l_ref, i, jnp.broadcast_to(lsum, l_ref.shape))
        dz = _ln_bwd(dy, xhat, rstd, gv)
        dz_ref[...] = dz
        dzb_ref[...] = dz.astype(BF16)

    row = pl.BlockSpec((tr, d), lambda i: (i, 0))
    vec = pl.BlockSpec((1, d), lambda i: (0, 0))
    return pl.pallas_call(
        body, name="ln2_loss", grid=(t // tr,),
        in_specs=[row, vec, vec, row, row, vec, vec],
        out_specs=[row, row, vec, vec, pl.BlockSpec((8, LANE), lambda i: (0, 0))],
        out_shape=[jax.ShapeDtypeStruct((t, d), F32), jax.ShapeDtypeStruct((t, d), BF16),
                   jax.ShapeDtypeStruct((1, d), F32), jax.ShapeDtypeStruct((1, d), F32),
                   jax.ShapeDtypeStruct((8, LANE), F32)],
        compiler_params=_params("arbitrary"),
    )(xhat1, g1, b1, ff, target, g2, b2)


def _ln1_bwd(dz2, dffn, xhat1, rstd1, g1, tr=256):
    t, d = dz2.shape

    def body(a_ref, f_ref, xh_ref, rs_ref, g_ref, dz_ref, dzb_ref, dg_ref, db_ref):
        i = pl.program_id(0)
        dy = ALPHA * a_ref[...] + f_ref[...]
        xhat = xh_ref[...]
        _accum(dg_ref, i, jnp.sum(dy * xhat, axis=0, keepdims=True))
        _accum(db_ref, i, jnp.sum(dy, axis=0, keepdims=True))
        dz = _ln_bwd(dy, xhat, rs_ref[...], g_ref[...])
        dz_ref[...] = dz
        dzb_ref[...] = dz.astype(BF16)

    row = pl.BlockSpec((tr, d), lambda i: (i, 0))
    vec = pl.BlockSpec((1, d), lambda i: (0, 0))
    return pl.pallas_call(
        body, name="ln1_bwd", grid=(t // tr,),
        in_specs=[row, row, row, pl.BlockSpec((tr, 1), lambda i: (i, 0)), vec],
        out_specs=[row, row, vec, vec],
        out_shape=[jax.ShapeDtypeStruct((t, d), F32), jax.ShapeDtypeStruct((t, d), BF16),
                   jax.ShapeDtypeStruct((1, d), F32), jax.ShapeDtypeStruct((1, d), F32)],
        compiler_params=_params("arbitrary"),
    )(dz2, dffn, xhat1, rstd1, g1)


def _ffn_act_fwd(hid0, w_fc, b_fc, t, dff):
    nb = dff // LANE

    def body(hv_ref, hg_ref, wv_ref, wg_ref, bv_ref, bg_ref, a_ref):
        val = _conv(hv_ref[...], wv_ref[...]) + bv_ref[...]
        gate = _conv(hg_ref[...], wg_ref[...]) + bg_ref[...]
        a_ref[...] = (gate * _sigmoid(gate) * val).astype(BF16)

    col = lambda off: pl.BlockSpec((t, LANE), lambda j: (0, j + off))
    w3 = lambda off: pl.BlockSpec((3, LANE), lambda j: (0, j + off))
    w1 = lambda off: pl.BlockSpec((1, LANE), lambda j: (0, j + off))
    return pl.pallas_call(
        body, name="ffn_act_fwd", grid=(nb,),
        in_specs=[col(0), col(nb), w3(0), w3(nb), w1(0), w1(nb)],
        out_specs=col(0),
        out_shape=jax.ShapeDtypeStruct((t, dff), BF16),
        compiler_params=_params("parallel"),
    )(hid0, hid0, w_fc, w_fc, b_fc, b_fc)


def _ffn_act_bwd(da, hid0, w_fc, b_fc, t, dff):
    nb = dff // LANE

    def body(da_ref, hv_ref, hg_ref, wv_ref, wg_ref, bv_ref, bg_ref,
             dhv_ref, dhg_ref, dwv_ref, dwg_ref, dbv_ref, dbg_ref):
        hv, hg, wv, wg = hv_ref[...], hg_ref[...], wv_ref[...], wg_ref[...]
        rv, rg = _rolled(hv), _rolled(hg)
        val = _conv(hv, wv, rv) + bv_ref[...]
        gate = _conv(hg, wg, rg) + bg_ref[...]
        sig = _sigmoid(gate)
        d = da_ref[...]
        dsig = d * sig
        dval = dsig * gate
        dgate = dsig * val * (1.0 + gate * (1.0 - sig))
        dhv_ref[...] = _conv_t(dval, wv).astype(BF16)
        dhg_ref[...] = _conv_t(dgate, wg).astype(BF16)
        dwv_ref[...] = _conv_dw(dval, hv, rv)
        dwg_ref[...] = _conv_dw(dgate, hg, rg)
        dbv_ref[...] = jnp.sum(dval, axis=0, keepdims=True)
        dbg_ref[...] = jnp.sum(dgate, axis=0, keepdims=True)

    col = lambda off: pl.BlockSpec((t, LANE), lambda j: (0, j + off))
    w3 = lambda off: pl.BlockSpec((3, LANE), lambda j: (0, j + off))
    w1 = lambda off: pl.BlockSpec((1, LANE), lambda j: (0, j + off))
    s3 = jax.ShapeDtypeStruct((3, dff), F32)
    s1 = jax.ShapeDtypeStruct((1, dff), F32)
    return pl.pallas_call(
        body, name="ffn_act_bwd", grid=(nb,),
        in_specs=[col(0), col(0), col(nb), w3(0), w3(nb), w1(0), w1(nb)],
        out_specs=[col(0), col(0), w3(0), w3(0), w1(0), w1(0)],
        out_shape=[jax.ShapeDtypeStruct((t, dff), BF16)] * 2 + [s3, s3, s1, s1],
        compiler_params=_params("parallel"),
    )(da, hid0, hid0, w_fc, w_fc, b_fc, b_fc)


class _Ready:
    def __init__(self, **weights):
        self.weights = weights

    def begin(self, after):
        return None

    def forward(self, name, after):
        return None

    def get(self, name, after):
        return self.weights[name]


class _Kept:
    def __init__(self):
        self.grads = {}

    def start(self, name, grad):
        self.grads[name] = grad
        return None

    def relay(self, name, after):
        return None

    def meanwhile(self, small, loss, after):
        return None


def _behind(a, token):
    return a if token is None else a + token[0:1, 0:1].reshape((1,) * a.ndim)


def _local_step(x, target, w_in, b_gates, w_sc, gain, w_out, ln1_g, ln1_b, w_up, w_fc, b_fc, w_down, ln2_g, ln2_b,
                gx=None, wx=None, x_b=None):
    t, d = x.shape
    wc = d // 2
    dh = (d - wc) // NH
    wm = NH * dh
    dff = w_fc.shape[1] // 2
    if wx is None:
        wx = _Ready(w_out=w_out, w_up=w_up, w_down=w_down)
    ninp = 3 * wc + 4 * wm + LANE
    nin = 3 * wc + 4 * wm
    gate_tile = nin // LANE
    nc = t // CHUNK
    bias_tile = jnp.pad(b_gates, ((0, 0), (0, LANE - 2 * NH)))

    if x_b is None:
        x_b = x.astype(BF16)
    proj = _matmul(x_b, w_in, "nt", F32, "proj", tm=512, tn=2432, tk=d, n=ninp, after=wx.begin(w_in))
    y = _sconv_fwd(proj, w_sc, t, wc)
    gcol = _gates_prep(proj, bias_tile, t, gate_tile)
    grow = gcol[:, :8].T.reshape(8, nc, CHUNK).transpose(1, 0, 2)
    hval, cs, ns = _mlstm_fwd(proj, gcol, grow, t, wc, dh)
    y = _hnorm_fwd(hval, proj, gain, y, t, wc, dh)
    tok = wx.forward("w_out", y)
    w_out = wx.get("w_out", tok)
    mix = _matmul(y, w_out, "nn", F32, "out_proj", tm=512, tn=1024, tk=wc, a_blocked=True, after=tok)
    xhat1, rstd1, x1_b = _ln1_fwd(x, mix, _behind(ln1_g, wx.forward("w_up", mix)), ln1_b)
    w_up = wx.get("w_up", x1_b)
    wsl = w_up.shape[2]
    hid0 = _matmul(x1_b, w_up, "nn", F32, "ffn_up", tm=1024, tn=wsl, tk=d, b_blocked=True)
    act = _ffn_act_fwd(hid0, w_fc, _behind(b_fc, wx.forward("w_down", hid0)), t, dff)
    w_down = wx.get("w_down", act)
    ff = _matmul(act, w_down, "nn", F32, "ffn_down", tm=1024, tn=512, tk=dff)
    dz2, dz2_b, d_ln2_g, d_ln2_b, loss = _ln2_loss(xhat1, ln1_g, ln1_b, ff, target, ln2_g, ln2_b)

    if gx is None:
        gx = _Kept()
    d_w_down = _matmul(act, dz2_b, "tn", BF16, "ffn_down_dw", tm=1408, tn=1024, tk=t)
    d_act = _matmul(dz2_b, w_down, "nt", F32, "ffn_down_dx", tm=2048, tn=512, tk=d, after=gx.start("w_down", d_w_down))
    *d_hid0, dwv, dwg, dbv, dbg = _ffn_act_bwd(d_act, hid0, w_fc, _behind(b_fc, gx.relay("w_down", d_act)), t, dff)
    d_w_fc = jnp.concatenate([dwv, dwg], axis=1)
    d_b_fc = jnp.concatenate([dbv, dbg], axis=1)
    d_hid0 = tuple(d_hid0[:2])
    d_w_up = _matmul(x1_b, d_hid0, "tn", BF16, "ffn_up_dw", tm=1024, tn=wsl, tk=t, o_width=wsl)
    d_x1_ffn = _matmul(d_hid0, w_up, "nt", F32, "ffn_up_dx", tm=1024, tn=1024, tk=wsl, b_blocked=True,
                       after=gx.start("w_up", d_w_up))
    dz1, dz1_b, d_ln1_g, d_ln1_b = _ln1_bwd(dz2, d_x1_ffn, xhat1, rstd1, _behind(ln1_g, gx.relay("w_up", d_x1_ffn)))

    d_w_out = _matmul(y, dz1_b, "tn", BF16, "out_proj_dw", tm=1024, tn=1024, tk=t, a_blocked=True)
    dy = _matmul(dz1_b, w_out, "nt", F32, "out_proj_dx", tm=1024, tn=1024, tk=d, after=gx.start("w_out", d_w_out))
    dcb, dcc, dch, d_w_sc = _sconv_bwd(dy, proj, _behind(w_sc, gx.relay("w_out", dy)), t, wc)
    d_o, d_hval, d_gain = _hnorm_bwd(dy, hval, proj, gain, t, wc, dh)
    dq, dk, dv, dgate = _mlstm_bwd(proj, gcol, grow, hval, d_hval, cs, ns, t, wc, dh)
    dgt, d_b_gates = _gates_bwd(dgate, proj, bias_tile, t, gate_tile)
    d_proj = jnp.concatenate([dcb, dcc, dch, dq, dk, dv, d_o, dgt], axis=1)
    d_w_in = _matmul(d_proj, x_b, "tn", BF16, "proj_dw", tm=2432, tn=1024, tk=t)
    small = dict(b_gates=d_b_gates[:, :2 * NH], w_sc_conv=d_w_sc, mh_gain=d_gain, ln1_g=d_ln1_g, ln1_b=d_ln1_b,
                 w_ffn_conv=d_w_fc, b_ffn_conv=d_b_fc, ln2_g=d_ln2_g, ln2_b=d_ln2_b)
    token = gx.start("w_in", d_w_in)
    token = gx.relay("w_in", gx.meanwhile(small, loss, token))
    grad_x = _matmul(d_proj, w_in, "nn", F32, "proj_dx", tm=512, tn=512, tk=ninp, add=dz1, add_scale=ALPHA, after=token)
    return loss, grad_x, small, gx


HBM = pl.BlockSpec(memory_space=pltpu.HBM)


def _place():
    return lax.axis_index("x"), lax.axis_index("y"), lax.axis_index("c")


def _index(p):
    return 4 * p[0] + 2 * p[1] + p[2]


def _all_gather(arrs, name):
    n = len(arrs)

    def body(*refs):
        ins, outs = refs[:n], refs[n:2 * n]
        send_sems, recv_sems, local_sems = refs[2 * n:]
        x, y, c = _place()
        me, sibling = (x, y, c), (x, y, 1 - c)
        chips = [(1 - x, y), (x, 1 - y), (1 - x, 1 - y)]

        def copy(a, k, block, to, own=False):
            dst = outs[a].at[_index(block)]
            return pltpu.make_async_remote_copy(
                src_ref=ins[a] if own else dst, dst_ref=dst,
                send_sem=send_sems.at[k * n + a], recv_sem=recv_sems.at[k * n + a],
                device_id=to, device_id_type=MESH)

        mine = [pltpu.make_async_copy(ins[a], outs[a].at[_index(me)], local_sems.at[a]) for a in range(n)]
        for cp in mine:
            cp.start()
        first = []
        for a in range(n):
            first.append(copy(a, 0, me, sibling, own=True))
            first += [copy(a, 1 + j, me, (*chip, c), own=True) for j, chip in enumerate(chips)]
        for cp in first:
            cp.start()
        passed = []
        for j, chip in enumerate(chips):
            for a in range(n):
                copy(a, 1 + j, (*chip, c), me).wait_recv()
                cp = copy(a, 4 + j, (*chip, c), sibling)
                cp.start()
                passed.append(cp)
        for a in range(n):
            copy(a, 0, sibling, me).wait_recv()
            for j, chip in enumerate(chips):
                copy(a, 4 + j, (*chip, 1 - c), me).wait_recv()
        for cp in first + passed:
            cp.wait_send()
        for cp in mine:
            cp.wait()

    return pl.pallas_call(
        body, name=name, in_specs=[HBM] * n, out_specs=[HBM] * n,
        out_shape=[jax.ShapeDtypeStruct((N_DEV,) + a.shape, a.dtype) for a in arrs],
        scratch_shapes=[pltpu.SemaphoreType.DMA((7 * n,)), pltpu.SemaphoreType.DMA((7 * n,)),
                        pltpu.SemaphoreType.DMA((n,))],
    )(*arrs)


SEM = pl.BlockSpec(memory_space=pltpu.SEMAPHORE)
EFFECT = pltpu.SideEffectType.DATAFLOW_SIDE_EFFECTING


def _chips(x, y):
    return [(1 - x, y), (x, 1 - y), (1 - x, 1 - y)]


N_CHIP = N_DEV // 2


def _pair_route(x, y, c):
    return [((x, y, 1 - c), 2 * q + (1 - c), q, q) for q in range(N_CHIP)]


def _chip_route(x, y, c):
    mine = 2 * x + y
    return [((*chip, c), 2 * chip[0] + chip[1], mine, 2 * chip[0] + chip[1]) for chip in _chips(x, y)]


def _exchange_pieces(g_ref, land_ref, width, tail):
    if not tail:
        return [(lambda i: g_ref.at[i], lambda s: land_ref.at[s])]
    rows = lambda i, n: pl.ds(pl.multiple_of(i * width, IN_TAIL), n)
    return [(lambda i: g_ref.at[rows(i, width), :], lambda s: land_ref.at[s, pl.ds(0, width), :]),
            (lambda i: g_ref.at[rows(i + 1, IN_TAIL), :], lambda s: land_ref.at[s, pl.ds(width, IN_TAIL), :])]


def _exchange_start(grad, route, tail, name):
    width = IN_SLAB if tail else grad.shape[1]
    n_p = 2 if tail else 1
    n_c = len(route(0, 0, 0))
    land_shape = (N_CHIP, width + (IN_TAIL if tail else 0), grad.shape[-1])

    def body(g_ref, land_ref, send_sems, recv_sems, g_thru, land_thru, token):
        for j, (peer, slab, slot, _) in enumerate(route(*_place())):
            for p, (src, dst) in enumerate(_exchange_pieces(g_ref, land_ref, width, tail)):
                pltpu.make_async_remote_copy(src_ref=src(slab), dst_ref=dst(slot), send_sem=send_sems.at[j * n_p + p],
                                             recv_sem=recv_sems.at[j * n_p + p], device_id=peer,
                                             device_id_type=MESH).start()
        token[...] = jnp.zeros_like(token)

    return pl.pallas_call(
        body, name=name,
        out_shape=(pltpu.SemaphoreType.DMA((n_c * n_p,)), pltpu.SemaphoreType.DMA((n_c * n_p,)),
                   pltpu.HBM(grad.shape, grad.dtype), pltpu.HBM(land_shape, grad.dtype),
                   jax.ShapeDtypeStruct((8, LANE), F32)),
        in_specs=(HBM, HBM), out_specs=(SEM, SEM, HBM, HBM, pl.BlockSpec(memory_space=pltpu.VMEM)),
        input_output_aliases={0: 2, 1: 3},
        compiler_params=pltpu.CompilerParams(has_side_effects=EFFECT),
    )(pltpu.with_memory_space_constraint(grad, pltpu.HBM),
      pltpu.with_memory_space_constraint(lax.empty(land_shape, grad.dtype), pltpu.HBM))


def _exchange_wait(send_sems, recv_sems, g_thru, land_thru, after, route, tail, name):
    width = IN_SLAB if tail else g_thru.shape[1]
    n_p = 2 if tail else 1

    def body(g_ref, land_ref, send_sems, recv_sems, after_ref, g_dead, got_ref):
        for j, (peer, slab, _, slot) in enumerate(route(*_place())):
            for p, (src, dst) in enumerate(_exchange_pieces(g_ref, land_ref, width, tail)):
                cp = pltpu.make_async_remote_copy(src_ref=src(slab), dst_ref=dst(slot),
                                                  send_sem=send_sems.at[j * n_p + p], recv_sem=recv_sems.at[j * n_p + p],
                                                  device_id=peer, device_id_type=MESH)
                cp.wait_send()
                cp.wait_recv()

    return pl.pallas_call(
        body, name=name,
        out_shape=(pltpu.HBM(g_thru.shape, g_thru.dtype), pltpu.HBM(land_thru.shape, land_thru.dtype)),
        in_specs=(HBM, HBM, SEM, SEM, pl.BlockSpec(memory_space=pl.ANY)), out_specs=(HBM, HBM),
        input_output_aliases={0: 0, 1: 1},
        compiler_params=pltpu.CompilerParams(has_side_effects=EFFECT),
    )(g_thru, land_thru, send_sems, recv_sems, after)


def _pair_add(grad, pair, core, tail, name):
    rows, cols = (IN_SLAB if tail else grad.shape[1]), grad.shape[-1]
    total = pair.shape[1]

    def body(core_ref, *refs):
        if tail:
            g_ref, t_ref, p_ref, o_ref = refs
            o_ref[0:rows, :] = (g_ref[...].astype(F32) + p_ref[0:rows, :].astype(F32)).astype(BF16)
            o_ref[rows:total, :] = (t_ref[...].astype(F32) + p_ref[rows:total, :].astype(F32)).astype(BF16)
        else:
            g_ref, p_ref, o_ref = refs
            o_ref[...] = (g_ref[...].astype(F32) + p_ref[...].astype(F32)).astype(BF16)

    if tail:
        tc = _fit(cols, 512)
        grid = (N_CHIP, cols // tc)
        slab = pl.BlockSpec((None, total, tc), lambda q, i, core_ref: (q, 0, i))
        per = IN_SLAB // IN_TAIL
        in_specs = [pl.BlockSpec((rows, tc), lambda q, i, core_ref: (2 * q + core_ref[0], i)),
                    pl.BlockSpec((IN_TAIL, tc), lambda q, i, core_ref: ((2 * q + core_ref[0] + 1) * per, i))]
    else:
        tr = _rows(rows, 1024)
        grid = (N_CHIP, rows // tr)
        slab = pl.BlockSpec((None, tr, cols), lambda q, i, core_ref: (q, i, 0))
        in_specs = [pl.BlockSpec((None, tr, cols), lambda q, i, core_ref: (2 * q + core_ref[0], i, 0))]
    return pl.pallas_call(
        body, name=name,
        grid_spec=pltpu.PrefetchScalarGridSpec(num_scalar_prefetch=1, grid=grid,
                                               in_specs=in_specs + [slab], out_specs=slab),
        out_shape=jax.ShapeDtypeStruct(pair.shape, BF16),
        compiler_params=_params("parallel", "parallel"),
    )(core, *([grad, grad] if tail else [grad]), pair)


def _relay_places(x, y, c):
    came_from = (c * (1 - x) + (1 - c) * x, c * y + (1 - c) * (1 - y), c)
    pass_to = (c * x + (1 - c) * (1 - x), c * (1 - y) + (1 - c) * y, c)
    return 2 - c, came_from, pass_to, pass_to


def _gather_start(blocks, after, name, spare=(), relayed=False):
    n = len(blocks)
    lands = [(N_DEV + (a in spare),) + b.shape for a, b in enumerate(blocks)]

    def body(*refs):
        b_refs, land_refs = refs[:n], refs[n:2 * n]
        send_sems, recv_sems = refs[2 * n + 1:3 * n + 1], refs[3 * n + 1:4 * n + 1]
        token = refs[-1]
        x, y, c = _place()
        me = _index((x, y, c))
        for a in range(n):
            targets = [(x, y, 1 - c)] + [(*chip, c) for chip in _chips(x, y)]
            for k, to in enumerate(targets[:3] if relayed else targets):
                pltpu.make_async_remote_copy(src_ref=b_refs[a], dst_ref=land_refs[a].at[me], send_sem=send_sems[a].at[k],
                                             recv_sem=recv_sems[a].at[k], device_id=to, device_id_type=MESH).start()
        token[...] = jnp.zeros_like(token)

    sems = [pltpu.SemaphoreType.DMA((4,))] * n
    out = pl.pallas_call(
        body, name=name,
        out_shape=(*sems, *sems, *[pltpu.HBM(b.shape, b.dtype) for b in blocks],
                   *[pltpu.HBM(s, b.dtype) for s, b in zip(lands, blocks)], jax.ShapeDtypeStruct((8, LANE), F32)),
        in_specs=(*[HBM] * (2 * n), pl.BlockSpec(memory_space=pl.ANY)),
        out_specs=(*[SEM] * (2 * n), *[HBM] * (2 * n), pl.BlockSpec(memory_space=pltpu.VMEM)),
        input_output_aliases={i: 2 * n + i for i in range(2 * n)},
        compiler_params=pltpu.CompilerParams(has_side_effects=EFFECT),
    )(*[pltpu.with_memory_space_constraint(b, pltpu.HBM) for b in blocks],
      *[pltpu.with_memory_space_constraint(lax.empty(s, b.dtype), pltpu.HBM) for s, b in zip(lands, blocks)], after)
    return [(out[a], out[n + a], out[2 * n + a], out[3 * n + a]) for a in range(n)], out[-1]


def _gather_relay(states, after, name):
    n = len(states)

    def body(*refs):
        land_refs, send_sems, recv_sems = refs[:n], refs[n:2 * n], refs[2 * n:3 * n]
        pass_send, pass_recv = refs[4 * n + 1:5 * n + 1], refs[5 * n + 1:6 * n + 1]
        k_in, came_from, pass_to, _ = _relay_places(*_place())
        for a in range(n):
            slot = land_refs[a].at[_index(came_from)]
            pltpu.make_async_remote_copy(src_ref=slot, dst_ref=slot, send_sem=send_sems[a].at[k_in],
                                         recv_sem=recv_sems[a].at[k_in], device_id=came_from,
                                         device_id_type=MESH).wait_recv()
        for a in range(n):
            slot = land_refs[a].at[_index(came_from)]
            pltpu.make_async_remote_copy(src_ref=slot, dst_ref=slot, send_sem=pass_send[a].at[0],
                                         recv_sem=pass_recv[a].at[0], device_id=pass_to, device_id_type=MESH).start()
        refs[-1][...] = jnp.zeros_like(refs[-1])

    lands = [st[3] for st in states]
    pair = [pltpu.SemaphoreType.DMA((1,))] * n
    out = pl.pallas_call(
        body, name=name,
        out_shape=(*[pltpu.HBM(l.shape, l.dtype) for l in lands], *pair, *pair, jax.ShapeDtypeStruct((8, LANE), F32)),
        in_specs=(*[HBM] * n, *[SEM] * (2 * n), pl.BlockSpec(memory_space=pl.ANY)),
        out_specs=(*[HBM] * n, *[SEM] * (2 * n), pl.BlockSpec(memory_space=pltpu.VMEM)),
        input_output_aliases={a: a for a in range(n)},
        compiler_params=pltpu.CompilerParams(has_side_effects=EFFECT),
    )(*lands, *[st[0] for st in states], *[st[1] for st in states], after)
    return [(st[0], st[1], st[2], out[a], (out[n + a], out[2 * n + a])) for a, st in enumerate(states)], out[-1]


def _gather_forward(send_sems, recv_sems, b_thru, land_thru, after, name, passed=None):
    relayed = passed is not None

    def body(b_ref, land_ref, send_sems, recv_sems, *rest):
        pass_send, pass_recv = rest[:2] if relayed else (None, None)
        send2, recv2, token = rest[-3:]
        x, y, c = _place()
        sibling = (x, y, 1 - c)
        arrivals = [sibling] + [(*chip, c) for chip in _chips(x, y)]
        waits = [(send_sems.at[k], recv_sems.at[k], frm) for k, frm in enumerate(arrivals)]
        sends = [send_sems.at[k] for k in range(4)]
        if relayed:
            k_in, _, _, other = _relay_places(x, y, c)
            waits = [waits[0], (send_sems.at[3 - k_in], recv_sems.at[3 - k_in], other),
                     (pass_send.at[0], pass_recv.at[0], arrivals[3])]
            sends[3] = pass_send.at[0]
        for sem in sends:
            pltpu.make_async_remote_copy(src_ref=b_ref, dst_ref=land_ref.at[0], send_sem=sem, recv_sem=recv_sems.at[0],
                                         device_id=sibling, device_id_type=MESH).wait_send()
        for send_sem, recv_sem, frm in waits:
            pltpu.make_async_remote_copy(src_ref=b_ref, dst_ref=land_ref.at[_index(frm)], send_sem=send_sem,
                                         recv_sem=recv_sem, device_id=frm, device_id_type=MESH).wait_recv()
        for j, chip in enumerate(_chips(x, y)):
            slot = land_ref.at[_index((*chip, c))]
            pltpu.make_async_remote_copy(src_ref=slot, dst_ref=slot, send_sem=send2.at[j], recv_sem=recv2.at[j],
                                         device_id=sibling, device_id_type=MESH).start()
        token[...] = jnp.zeros_like(token)

    extra = list(passed) if relayed else []
    return pl.pallas_call(
        body, name=name,
        out_shape=(pltpu.HBM(b_thru.shape, b_thru.dtype), pltpu.HBM(land_thru.shape, land_thru.dtype),
                   pltpu.SemaphoreType.DMA((3,)), pltpu.SemaphoreType.DMA((3,)), jax.ShapeDtypeStruct((8, LANE), F32)),
        in_specs=(HBM, HBM, SEM, SEM, *[SEM] * len(extra), pl.BlockSpec(memory_space=pl.ANY)),
        out_specs=(HBM, HBM, SEM, SEM, pl.BlockSpec(memory_space=pltpu.VMEM)),
        input_output_aliases={0: 0, 1: 1},
        compiler_params=pltpu.CompilerParams(has_side_effects=EFFECT),
    )(b_thru, land_thru, send_sems, recv_sems, *extra, after)


def _gather_finish(land_thru, send2, recv2, after, name):
    def body(land_ref, send2, recv2, after_ref, land_out):
        x, y, c = _place()
        for j, chip in enumerate(_chips(x, y)):
            cp = pltpu.make_async_remote_copy(src_ref=land_ref.at[_index((*chip, c))],
                                              dst_ref=land_ref.at[_index((*chip, 1 - c))], send_sem=send2.at[j],
                                              recv_sem=recv2.at[j], device_id=(x, y, 1 - c), device_id_type=MESH)
            cp.wait_send()
            cp.wait_recv()

    return pl.pallas_call(
        body, name=name, out_shape=pltpu.HBM(land_thru.shape, land_thru.dtype),
        in_specs=(HBM, SEM, SEM, pl.BlockSpec(memory_space=pl.ANY)), out_specs=HBM,
        input_output_aliases={0: 0},
        compiler_params=pltpu.CompilerParams(has_side_effects=EFFECT),
    )(land_thru, send2, recv2, after)


class _Gathering:
    def __init__(self, ahead, later, me):
        cast = [a.astype(BF16) for a in ahead.values()]
        started, self.token = _gather_start(cast, cast[0], "gather1_ahead", relayed=True)
        self.me, self.state, self.relayed, self.later = me, dict(zip(ahead, started)), tuple(ahead), later

    def start_first(self, first):
        started, self.token = _gather_start(list(first.values()), self.token, "gather1_first", spare=(0,), relayed=True)
        self.state.update(zip(first, started))
        self.relayed += tuple(first)

    def begin(self, after):
        return self.token

    def relay(self, after):
        states, token = _gather_relay([self.state[n] for n in self.relayed], after, "gather_relay")
        self.state.update(zip(self.relayed, states))
        cast = [_behind(a, token).astype(BF16) for a in self.later.values()]
        started, self.token = _gather_start(cast, token, "gather1_later")
        self.state.update(zip(self.later, started))
        return self.token

    def forward(self, name, after):
        first_leg, passed = self.state[name][:4], (self.state[name][4:] or (None,))[0]
        *self.state[name], token = _gather_forward(*first_leg, after, "gather2_" + name, passed=passed)
        return token

    def get(self, name, after):
        block, land, send2, recv2 = self.state[name]
        land = _gather_finish(land, send2, recv2, after, "gather3_" + name)
        land = lax.dynamic_update_index_in_dim(land, block[None], self.me, 0)
        return land if name not in ("w_out", "w_down") else land.reshape(-1, land.shape[2])


class _Reducing:
    def __init__(self, core, chip, gather_small):
        self.core, self.chip, self.state, self.token, self.gather_small = core, chip, {}, None, gather_small

    def meanwhile(self, small, loss, after):
        self.small_sum = self.gather_small(small, loss, after)
        return self.small_sum

    def start(self, name, grad):
        tail = name == "w_in"
        g = grad if tail or grad.ndim == 3 else grad.reshape(N_DEV, grad.shape[0] // N_DEV, grad.shape[1])
        *self.state[name], token = _exchange_start(g, _pair_route, tail, "pair_send_" + name)
        return token

    def relay(self, name, after):
        tail = name == "w_in"
        grad, pair = _exchange_wait(*self.state[name], after, _pair_route, tail, "pair_recv_" + name)
        total = _pair_add(grad, pair, self.core, tail, "pair_add_" + name)
        *self.state[name], self.token = _exchange_start(total, _chip_route, False, "chip_send_" + name)
        return self.token

    def finish(self, name, after):
        total, land = _exchange_wait(*self.state[name], after, _chip_route, False, "chip_recv_" + name)
        own = lax.dynamic_index_in_dim(total, self.chip, 0, keepdims=True)
        return lax.dynamic_update_index_in_dim(land, own, self.chip, 0)


def _carry_w_in(main, tail):
    slabs, _, d = main.shape
    tc = _fit(d, 2048)
    assert slabs == N_DEV + 1 and tail.shape[:2] == (N_DEV, IN_TAIL), (main.shape, tail.shape)
    top = lambda off: pl.BlockSpec((None, IN_TAIL, tc), lambda s, j: (s + off, 0, j))

    def carry(m_ref, t_ref, o_ref):
        o_ref[...] = m_ref[...] + t_ref[...]

    main = pl.pallas_call(
        carry, name="carry_w_in", grid=(N_DEV - 1, d // tc), in_specs=[top(1), top(0)], out_specs=top(1),
        out_shape=jax.ShapeDtypeStruct(main.shape, main.dtype), input_output_aliases={0: 0},
        compiler_params=_params("parallel", "parallel"),
    )(main, tail)

    def last(m_ref, t_ref, o_ref):
        o_ref[...] = jnp.zeros_like(o_ref)
        o_ref[0:IN_TAIL, :] = t_ref[...]

    return pl.pallas_call(
        last, name="last_slab_w_in", grid=(d // tc,),
        in_specs=[pl.BlockSpec(memory_space=pl.ANY), pl.BlockSpec((None, IN_TAIL, tc), lambda j: (N_DEV - 1, 0, j))],
        out_specs=pl.BlockSpec((None, LANE, tc), lambda j: (N_DEV, 0, j)),
        out_shape=jax.ShapeDtypeStruct(main.shape, main.dtype), input_output_aliases={0: 0},
        compiler_params=_params("parallel"),
    )(main, tail)


def _rows(n, want):
    t = min(n, want)
    t -= t % 16
    while n % t:
        t -= 16
    return t


def _adam_math(w, g, m, v):
    m2 = ADAM_B1 * m + (1.0 - ADAM_B1) * g
    v2 = ADAM_B2 * v + (1.0 - ADAM_B2) * (g * g)
    m_hat = m2 * (1.0 / (1.0 - ADAM_B1 ** ADAM_STEP))
    v_hat = v2 * (1.0 / (1.0 - ADAM_B2 ** ADAM_STEP))
    return -ADAM_LR * (m_hat / (jnp.sqrt(v_hat) + ADAM_EPS) + ADAM_WD * w), m2, v2


def _slot_sum(r_ref):
    acc = r_ref[0].astype(F32)
    for i in range(1, r_ref.shape[0]):
        acc = acc + r_ref[i].astype(F32)
    return acc


def _shift_w_in(w):
    ws, d = w.shape
    tc = _fit(d, 256)

    def body(w_ref, main_ref, tail_ref, tall):
        tall[...] = jnp.zeros_like(tall)
        tall[0:ws, :] = w_ref[...]
        moved = pltpu.roll(tall[...], _index(_place()), 0).astype(BF16)
        main_ref[...] = moved[0:IN_SLAB]
        tail_ref[...] = moved[IN_SLAB:]

    return pl.pallas_call(
        body, name="shift_w_in", grid=(d // tc,),
        in_specs=[pl.BlockSpec((ws, tc), lambda j: (0, j))],
        out_specs=[pl.BlockSpec((IN_SLAB, tc), lambda j: (0, j)), pl.BlockSpec((IN_TAIL, tc), lambda j: (0, j))],
        out_shape=[jax.ShapeDtypeStruct((IN_SLAB, d), BF16), jax.ShapeDtypeStruct((IN_TAIL, d), BF16)],
        scratch_shapes=[pltpu.VMEM((IN_SLAB + IN_TAIL, tc), F32)], compiler_params=_params("parallel"),
    )(w)


def _sum_adamw_shifted(r, w, m, v, name):
    _, ph, d = r.shape
    ws = w.shape[0]
    tc = _fit(d, 256)

    def body(r_ref, w_ref, m_ref, v_ref, g_ref, d_ref, m2_ref, v2_ref, tall):
        tall[...] = pltpu.roll(_slot_sum(r_ref), lax.rem(ph - _index(_place()), ph), 0)
        g = tall[0:ws, :]
        g_ref[...] = g
        d_ref[...], m2_ref[...], v2_ref[...] = _adam_math(w_ref[...], g, m_ref[...], v_ref[...])

    blk = pl.BlockSpec((ws, tc), lambda j: (0, j))
    out = jax.ShapeDtypeStruct(w.shape, F32)
    return pl.pallas_call(
        body, name=name, grid=(d // tc,),
        in_specs=[pl.BlockSpec((r.shape[0], ph, tc), lambda j: (0, 0, j)), blk, blk, blk],
        out_specs=[blk] * 4, out_shape=[out] * 4,
        scratch_shapes=[pltpu.VMEM((ph, tc), F32)], compiler_params=_params("parallel"),
    )(r, w, m, v)


def _sum_slots(r, name, tr=128):
    _, rows, cols = r.shape
    tr = _rows(rows, tr)

    def body(r_ref, g_ref):
        g_ref[...] = _slot_sum(r_ref)

    return pl.pallas_call(
        body, name=name, grid=(rows // tr,),
        in_specs=[pl.BlockSpec((r.shape[0], tr, cols), lambda i: (0, i, 0))],
        out_specs=pl.BlockSpec((tr, cols), lambda i: (i, 0)),
        out_shape=jax.ShapeDtypeStruct((rows, cols), F32),
        compiler_params=_params("parallel"),
    )(r)


def _adamw(w, g, m, v, name, tr=256):
    rows, cols = w.shape
    tr = _rows(rows, tr)

    def body(w_ref, g_ref, m_ref, v_ref, d_ref, m2_ref, v2_ref):
        d_ref[...], m2_ref[...], v2_ref[...] = _adam_math(w_ref[...], g_ref[...], m_ref[...], v_ref[...])

    blk = pl.BlockSpec((tr, cols), lambda i: (i, 0))
    out = jax.ShapeDtypeStruct((rows, cols), F32)
    return pl.pallas_call(
        body, name=name, grid=(rows // tr,), in_specs=[blk] * 4, out_specs=[blk] * 3, out_shape=[out] * 3,
        compiler_params=_params("parallel"),
    )(w, g, m, v)


def _sum_adamw(r, w, m, v, name, tr=256):
    rows, cols = w.shape
    tr = _rows(rows, tr)

    def body(r_ref, w_ref, m_ref, v_ref, g_ref, d_ref, m2_ref, v2_ref):
        g = _slot_sum(r_ref)
        g_ref[...] = g
        d_ref[...], m2_ref[...], v2_ref[...] = _adam_math(w_ref[...], g, m_ref[...], v_ref[...])

    blk = pl.BlockSpec((tr, cols), lambda i: (i, 0))
    out = jax.ShapeDtypeStruct((rows, cols), F32)
    return pl.pallas_call(
        body, name=name, grid=(rows // tr,),
        in_specs=[pl.BlockSpec((r.shape[0], tr, cols), lambda i: (0, i, 0)), blk, blk, blk],
        out_specs=[blk] * 4, out_shape=[out] * 4,
        compiler_params=_params("parallel"),
    )(r, w, m, v)


def _pack(pieces, sizes):
    flat = [jnp.pad(p.reshape(-1).astype(F32), (0, s - p.size)) for p, s in zip(pieces, sizes)]
    total = sum(sizes)
    padded = -(-total // (16 * LANE)) * (16 * LANE)
    return jnp.pad(jnp.concatenate(flat), (0, padded - total)).reshape(-1, LANE)


def _unpack(packed, shapes, sizes):
    flat = packed.reshape(-1)
    out, off = [], 0
    for shp, s in zip(shapes, sizes):
        n = 1
        for k in shp:
            n *= k
        out.append(flat[off:off + n].reshape(shp))
        off += s
    return out


def _lanes(n):
    return -(-n // LANE) * LANE


WEIGHTS = ("w_in", "b_gates", "w_sc_conv", "mh_gain", "w_out", "ln1_g", "ln1_b", "w_up", "w_ffn_conv", "b_ffn_conv",
           "w_down", "ln2_g", "ln2_b")
BIG = ("w_in", "w_out", "w_up", "w_down")
SMALL = tuple(n for n in WEIGHTS if n not in BIG)


def kernel(x, w_in, b_gates, w_sc_conv, mh_gain, w_out, ln1_g, ln1_b, w_up, w_ffn_conv, b_ffn_conv, w_down, ln2_g, ln2_b, loss_target, m_w_in, m_b_gates, m_w_sc_conv, m_mh_gain, m_w_out, m_ln1_g, m_ln1_b, m_w_up, m_w_ffn_conv, m_b_ffn_conv, m_w_down, m_ln2_g, m_ln2_b, v_w_in, v_b_gates, v_w_sc_conv, v_mh_gain, v_w_out, v_ln1_g, v_ln1_b, v_w_up, v_w_ffn_conv, v_b_ffn_conv, v_w_down, v_ln2_g, v_ln2_b):
    w = dict(zip(WEIGHTS, (w_in, b_gates, w_sc_conv, mh_gain, w_out, ln1_g, ln1_b, w_up, w_ffn_conv, b_ffn_conv,
                           w_down, ln2_g, ln2_b)))
    m = dict(zip(WEIGHTS, (m_w_in, m_b_gates, m_w_sc_conv, m_mh_gain, m_w_out, m_ln1_g, m_ln1_b, m_w_up,
                           m_w_ffn_conv, m_b_ffn_conv, m_w_down, m_ln2_g, m_ln2_b)))
    v = dict(zip(WEIGHTS, (v_w_in, v_b_gates, v_w_sc_conv, v_mh_gain, v_w_out, v_ln1_g, v_ln1_b, v_w_up,
                           v_w_ffn_conv, v_b_ffn_conv, v_w_down, v_ln2_g, v_ln2_b)))
    me = _index(_place())
    d = x.shape[2]
    ws_in = w_in.shape[2]
    assert ws_in == IN_SLAB + 1 and N_DEV <= LANE, w_in.shape
    ninp = (N_DEV + 1) * IN_SLAB
    ws_sc, ws_fc = w_sc_conv.shape[2], w_ffn_conv.shape[2]

    wx = _Gathering({"w_out": w_out[0]}, {n: w[n][0] for n in ("w_up", "w_down")}, me)
    w_in_t = jnp.transpose(_behind(w_in[0], wx.token))
    m_in_t, v_in_t = (jnp.transpose(a[0]) for a in (m_w_in, v_w_in))
    w_in_main, w_in_tail = _shift_w_in(w_in_t)
    taps8 = lambda a: jnp.pad(a[0], ((0, 5), (0, 0)))
    at_once = ("w_in", "w_tail", "w_sc", "w_fc")
    wx.start_first(dict(zip(at_once, (w_in_main, w_in_tail, taps8(w_sc_conv), taps8(w_ffn_conv)))))
    x_b = _behind(x[0], wx.begin(None)).astype(BF16)
    token = wx.relay(x_b)
    for n in at_once:
        token = wx.forward(n, token)
    g_in, g_tail, g_sc, g_fc = (wx.get(n, token) for n in at_once)
    w_in_full = _carry_w_in(g_in, g_tail).reshape(ninp, d)
    w_sc_full = g_sc[:, :3].transpose(1, 0, 2).reshape(3, N_DEV * ws_sc)
    w_fc_full = g_fc[:, :3].transpose(1, 0, 2).reshape(3, N_DEV * ws_fc)

    xi, yi, ci = _place()
    names = ("loss",) + SMALL
    pieces = {}

    def gather_small(small, loss_t, after):
        pieces.update(small, loss=loss_t[0, :1])
        sizes = [_lanes(pieces[n].size) for n in names]
        (g_small,) = _all_gather([_behind(_pack([pieces[n] for n in names], sizes), after)], "gather_small")
        return _sum_slots(g_small, "sum_small", tr=g_small.shape[1])

    gx = _Reducing(jnp.reshape(ci, (1,)).astype(jnp.int32), 2 * xi + yi, gather_small)
    loss_t, grad_x, small, _ = _local_step(
        x[0], loss_target[0], w_in_full, b_gates, w_sc_full, mh_gain, None, ln1_g, ln1_b, None,
        w_fc_full, b_ffn_conv, None, ln2_g, ln2_b, gx=gx, wx=wx, x_b=x_b)

    grads, deltas, new_m, new_v = {}, {}, {}, {}
    for name in ("w_down", "w_up", "w_out"):
        grads[name], deltas[name], new_m[name], new_v[name] = _sum_adamw(
            gx.finish(name, gx.token), w[name][0], m[name][0], v[name][0], "adamw_" + name)

    summed = _unpack(gx.small_sum, [pieces[n].shape for n in names], [_lanes(pieces[n].size) for n in names])
    full = dict(zip(names, summed))
    full["w_sc_conv"] = lax.dynamic_slice(full["w_sc_conv"], (0, me * ws_sc), (3, ws_sc))
    full["w_ffn_conv"] = lax.dynamic_slice(full["w_ffn_conv"], (0, me * ws_fc), (3, ws_fc))
    for n in SMALL:
        grads[n] = full[n].reshape(w[n].shape)
    sizes = [_lanes(w[n].size) for n in SMALL]
    shapes = [w[n].shape for n in SMALL]
    packed = [_pack([t[n] for n in SMALL], sizes) for t in (w, grads, m, v)]
    small_out = _adamw(*packed, "adamw_small")
    for res, t in zip(small_out, (deltas, new_m, new_v)):
        t.update(zip(SMALL, _unpack(res, shapes, sizes)))

    done = sum(t[0:1, 0:1] for t in (grad_x, deltas["w_down"], deltas["w_up"], deltas["w_out"], small_out[0]))
    grads["w_in"], deltas["w_in"], new_m["w_in"], new_v["w_in"] = (
        jnp.transpose(a)[None] for a in _sum_adamw_shifted(gx.finish("w_in", done), w_in_t, m_in_t, v_in_t, "adamw_w_in"))

    big = lambda t: {n: (t[n].reshape(w[n].shape) if n in BIG else t[n]) for n in WEIGHTS}
    grads, deltas, new_m, new_v = big(grads), big(deltas), big(new_m), big(new_v)
    return (full["loss"].reshape(()), grad_x[None], *[grads[n] for n in WEIGHTS], *[deltas[n] for n in WEIGHTS],
            *[new_m[n] for n in WEIGHTS], *[new_v[n] for n in WEIGHTS])
```

```python
import functools

import jax
import jax.numpy as jnp
from jax import lax
from jax.experimental import pallas as pl
from jax.experimental.pallas import tpu as pltpu

F32 = jnp.float32
BF16 = jnp.bfloat16
MESH = pl.DeviceIdType.MESH

N_DEV = 8
NH = 4
CHUNK = 64
LN_EPS = 1e-5
HN_EPS = 1e-6
ALPHA = 2.0 ** 0.25
LANE = 128
IN_SLAB = 7 * LANE
IN_TAIL = 16
VMEM_LIMIT = 56 * 1024 * 1024
ADAM_LR, ADAM_B1, ADAM_B2, ADAM_EPS, ADAM_WD, ADAM_STEP = 0.001, 0.9, 0.999, 1e-08, 0.01, 10

_NN = (((1,), (0,)), ((), ()))
_NT = (((1,), (1,)), ((), ()))
_TN = (((0,), (0,)), ((), ()))


def _dot(a, b, dn=_NN):
    return lax.dot_general(a, b, dn, preferred_element_type=F32)


def _params(*sem):
    return pltpu.CompilerParams(dimension_semantics=sem if sem else None, vmem_limit_bytes=VMEM_LIMIT)


def _iota(shape, axis):
    return lax.broadcasted_iota(jnp.int32, shape, axis)


def _fit(n, want):
    if n <= want:
        return n
    t = want - want % LANE
    while n % t:
        t -= LANE
    return t


def _matmul(a, b, mode, out_dtype, name, tm=1024, tn=512, tk=1024, add=None, add_scale=1.0,
            a_blocked=False, b_blocked=False, o_width=None, after=None, n=None):
    a_parts = a if isinstance(a, tuple) else None
    b_parts = b if isinstance(b, tuple) else None
    if a_parts:
        a_blocked, (a_rows, wa), na = True, a[0].shape, len(a)
        kd, m = (a_rows, na * wa) if mode == "tn" else (na * wa, a_rows)
    elif a_blocked:
        na, a_rows, wa = a.shape
        kd, m = (a_rows, na * wa) if mode == "tn" else (na * wa, a_rows)
    elif mode == "tn":
        kd, m = a.shape
    else:
        m, kd = a.shape
    if b_parts:
        b_blocked, (rows, w), nb = True, b[0].shape, len(b)
    elif b_blocked:
        nb, rows, w = b.shape
    if b_blocked:
        n = rows if mode == "nt" else nb * w
        assert (nb * w if mode == "nt" else rows) == kd, (name, kd)
    else:
        n = n or (b.shape[0] if mode == "nt" else b.shape[1])
    tm, tn, tk = _fit(m, tm), _fit(n, tn), _fit(kd, tk)
    if a_blocked and mode == "tn":
        tm = _fit(wa, tm)
    if a_blocked and mode != "tn":
        tk = _fit(wa, tk)
    if b_blocked and mode != "nt":
        tn = _fit(w, tn)
    if b_blocked and mode == "nt":
        tk = _fit(w, tk)
    if o_width is not None:
        tn = _fit(o_width, tn)
    assert m % tm == 0 and n % tn == 0 and kd % tk == 0, (name, m, n, kd, tm, tn, tk)
    assert not (a_blocked and mode != "tn" and wa % tk) and not (b_blocked and mode == "nt" and w % tk), (name, tk)
    nk = kd // tk
    dn = {"nn": _NN, "nt": _NT, "tn": _TN}[mode]
    if a_blocked and mode == "tn":
        a_per = wa // tm
        a_spec = pl.BlockSpec((None, tk, tm), lambda i, j, k: (i // a_per, k, i % a_per))
    elif a_blocked:
        a_per = wa // tk
        a_spec = pl.BlockSpec((None, tm, tk), lambda i, j, k: (k // a_per, i, k % a_per))
    elif mode == "tn":
        a_spec = pl.BlockSpec((tk, tm), lambda i, j, k: (k, i))
    else:
        a_spec = pl.BlockSpec((tm, tk), lambda i, j, k: (i, k))
    if b_blocked and mode != "nt":
        per = w // tn
        b_spec = pl.BlockSpec((None, tk, tn), lambda i, j, k: (j // per, k, j % per))
    elif b_blocked:
        per = w // tk
        b_spec = pl.BlockSpec((None, tn, tk), lambda i, j, k: (k // per, j, k % per))
    elif mode == "nt":
        b_spec = pl.BlockSpec((tn, tk), lambda i, j, k: (j, k))
    else:
        b_spec = pl.BlockSpec((tk, tn), lambda i, j, k: (k, j))
    if o_width is None:
        o_spec = pl.BlockSpec((tm, tn), lambda i, j, k: (i, j))
        o_shape = (m, n)
    else:
        oper = o_width // tn
        o_spec = pl.BlockSpec((None, tm, tn), lambda i, j, k: (j // oper, i, j % oper))
        o_shape = (n // o_width, m, o_width)
    a_list, a_specs = [a], [a_spec]
    if a_parts:
        hold = lambda x, s: jnp.clip(x - s * a_per, 0, a_per - 1)
        a_list = list(a_parts)
        a_specs = [(pl.BlockSpec((tk, tm), lambda i, j, k, s=s: (k, hold(i, s))) if mode == "tn"
                    else pl.BlockSpec((tm, tk), lambda i, j, k, s=s: (i, hold(k, s)))) for s in range(na)]
    b_list, b_specs = [b], [b_spec]
    if b_parts:
        hold_b = lambda x, s: jnp.clip(x - s * per, 0, per - 1)
        b_list = list(b_parts)
        b_specs = [(pl.BlockSpec((tn, tk), lambda i, j, k, s=s: (j, hold_b(k, s))) if mode == "nt"
                    else pl.BlockSpec((tk, tn), lambda i, j, k, s=s: (k, hold_b(j, s)))) for s in range(nb)]
    n_a, n_b = len(a_list), len(b_list)
    has_add = add is not None
    n_in = n_a + n_b + has_add + (after is not None)
    in_place = nk > 1 and out_dtype == F32

    def body(*refs):
        add_ref = refs[n_a + n_b] if has_add else None
        o_ref = refs[n_in]
        i, j, k = pl.program_id(0), pl.program_id(1), pl.program_id(2)

        def finish(r):
            if has_add:
                r = r + add_scale * add_ref[...]
            o_ref[...] = r.astype(out_dtype)

        def step(a_ref, b_ref):
            if nk == 1:
                finish(_dot(a_ref[...], b_ref[...], dn))
                return
            acc = o_ref if in_place else refs[-1]

            @pl.when(k == 0)
            def _():
                acc[...] = _dot(a_ref[...], b_ref[...], dn)

            @pl.when(k > 0)
            def _():
                acc[...] += _dot(a_ref[...], b_ref[...], dn)

        if n_a == 1 and n_b == 1:
            step(refs[0], refs[1])
        else:
            slab_a = ((i if mode == "tn" else k) // a_per) if n_a > 1 else 0
            slab_b = ((k if mode == "nt" else j) // per) if n_b > 1 else 0
            for sa in range(n_a):
                for sb in range(n_b):
                    pl.when((slab_a == sa) & (slab_b == sb))(functools.partial(step, refs[sa], refs[n_a + sb]))
        if nk > 1 and not (in_place and not has_add):
            @pl.when(k == nk - 1)
            def _():
                finish((o_ref if in_place else refs[-1])[...])

    in_specs = a_specs + b_specs + ([pl.BlockSpec((tm, tn), lambda i, j, k: (i, j))] if has_add else [])
    args = (*a_list, *b_list) + ((add,) if has_add else ())
    if after is not None:
        in_specs.append(pl.BlockSpec(memory_space=pl.ANY))
        args += (after,)
    return pl.pallas_call(
        body, name=name, grid=(m // tm, n // tn, nk),
        in_specs=in_specs, out_specs=o_spec,
        out_shape=jax.ShapeDtypeStruct(o_shape, out_dtype),
        scratch_shapes=[pltpu.VMEM((tm, tn), F32)] if nk > 1 and not in_place else [],
        compiler_params=_params("parallel", "parallel", "arbitrary"),
    )(*args)


def _shift_down(u, s):
    return jnp.where(_iota(u.shape, 0) >= s, pltpu.roll(u, s, 0), 0.0)


def _shift_up(u, s):
    t = u.shape[0]
    return jnp.where(_iota(u.shape, 0) < t - s, pltpu.roll(u, t - s, 0), 0.0)


SLAB = 8


def _rolled(u):
    return pltpu.roll(u, 2, 0), pltpu.roll(u, 1, 0)


def _conv(u, w, rolled=None):
    u2, u1 = _rolled(u) if rolled is None else rolled
    raw = w[0:1] * u2 + w[1:2] * u1 + w[2:3] * u
    head = u[0:SLAB]
    mended = w[0:1] * _shift_down(head, 2) + w[1:2] * _shift_down(head, 1) + w[2:3] * head
    return jnp.concatenate([mended, raw[SLAB:]], axis=0)


def _conv_t(dy, w):
    t = dy.shape[0]
    raw = w[2:3] * dy + w[1:2] * pltpu.roll(dy, t - 1, 0) + w[0:1] * pltpu.roll(dy, t - 2, 0)
    tail = dy[t - SLAB:]
    mended = w[2:3] * tail + w[1:2] * _shift_up(tail, 1) + w[0:1] * _shift_up(tail, 2)
    return jnp.concatenate([raw[:t - SLAB], mended], axis=0)


def _conv_dw(dy, u, rolled=None):
    t = dy.shape[0]
    u2, u1 = _rolled(u) if rolled is None else rolled
    head, tail = dy[0:SLAB], u[t - SLAB:]
    r = _iota(head.shape, 0)
    wrap2 = jnp.sum(jnp.where(r < 2, head * pltpu.roll(tail, 2, 0), 0.0), axis=0, keepdims=True)
    wrap1 = jnp.sum(jnp.where(r < 1, head * pltpu.roll(tail, 1, 0), 0.0), axis=0, keepdims=True)
    d0 = jnp.sum(dy * u2, axis=0, keepdims=True) - wrap2
    d1 = jnp.sum(dy * u1, axis=0, keepdims=True) - wrap1
    d2 = jnp.sum(dy * u, axis=0, keepdims=True)
    r3 = _iota((3, dy.shape[1]), 0)
    return jnp.where(r3 == 0, d0, jnp.where(r3 == 1, d1, d2))


def _sigmoid(x):
    return 0.5 * jnp.tanh(0.5 * x) + 0.5


def _sconv_fwd(proj, w_sc, t, wc):
    nb = wc // LANE

    def body(cb_ref, cc_ref, ch_ref, w_ref, y_ref):
        u = cc_ref[...] * ch_ref[...]
        y_ref[...] = (cb_ref[...] * _conv(u, w_ref[...])).astype(BF16)

    col = lambda off: pl.BlockSpec((t, LANE), lambda j: (0, j + off))
    return pl.pallas_call(
        body, name="sconv_fwd", grid=(nb,),
        in_specs=[col(0), col(nb), col(2 * nb), pl.BlockSpec((3, LANE), lambda j: (0, j))],
        out_specs=pl.BlockSpec((None, t, LANE), lambda j: (0, 0, j)),
        out_shape=jax.ShapeDtypeStruct((2, t, wc), BF16),
        compiler_params=_params("parallel"),
    )(proj, proj, proj, w_sc)


def _sconv_bwd(dy, proj, w_sc, t, wc):
    nb = wc // LANE

    def body(dy_ref, cb_ref, cc_ref, ch_ref, w_ref, dcb_ref, dcc_ref, dch_ref, dw_ref):
        cc, ch, w, d = cc_ref[...], ch_ref[...], w_ref[...], dy_ref[...]
        u = cc * ch
        ru = _rolled(u)
        dcb_ref[...] = (d * _conv(u, w, ru)).astype(BF16)
        dcu = d * cb_ref[...]
        dw_ref[...] = _conv_dw(dcu, u, ru)
        du = _conv_t(dcu, w)
        dcc_ref[...] = (du * ch).astype(BF16)
        dch_ref[...] = (du * cc).astype(BF16)

    col = lambda off: pl.BlockSpec((t, LANE), lambda j: (0, j + off))
    act = jax.ShapeDtypeStruct((t, wc), BF16)
    return pl.pallas_call(
        body, name="sconv_bwd", grid=(nb,),
        in_specs=[col(0), col(0), col(nb), col(2 * nb), pl.BlockSpec((3, LANE), lambda j: (0, j))],
        out_specs=[col(0), col(0), col(0), pl.BlockSpec((3, LANE), lambda j: (0, j))],
        out_shape=[act, act, act, jax.ShapeDtypeStruct((3, wc), F32)],
        compiler_params=_params("parallel"),
    )(dy, proj, proj, proj, w_sc)


def _gates_prep(proj, bias_tile, t, gate_tile):
    def body(g_ref, b_ref, o_ref):
        g = g_ref[...] + b_ref[...]
        lane = _iota(g.shape, 1)
        is_f = (lane >= NH) & (lane < 2 * NH)
        lf = jnp.minimum(g, 0.0) - jnp.log(1.0 + jnp.exp(-jnp.abs(g)))
        c = jnp.where(is_f, lf, 0.0)
        r = _iota(g.shape, 0) % CHUNK
        s = 1
        while s < CHUNK:
            c = c + jnp.where(r >= s, pltpu.roll(c, s, 0), 0.0)
            s *= 2
        o_ref[...] = jnp.where(is_f, c, jnp.where(lane < NH, g, 0.0))

    return pl.pallas_call(
        body, name="gates_prep", grid=(1,),
        in_specs=[pl.BlockSpec((t, LANE), lambda i: (0, gate_tile)), pl.BlockSpec((1, LANE), lambda i: (0, 0))],
        out_specs=pl.BlockSpec((t, LANE), lambda i: (0, 0)),
        out_shape=jax.ShapeDtypeStruct((t, LANE), F32),
        compiler_params=_params("arbitrary"),
    )(proj, bias_tile)


def _gates_bwd(dgate, proj, bias_tile, t, gate_tile):
    def body(dg_ref, g_ref, b_ref, o_ref, s_ref):
        g = g_ref[...] + b_ref[...]
        lane = _iota(g.shape, 1)
        r = _iota(g.shape, 0) % CHUNK
        dsig = 1.0 - _sigmoid(g)
        out = jnp.zeros(g.shape, F32)
        for h in range(NH):
            d = dg_ref[h]
            c = d
            s = 1
            while s < CHUNK:
                c = c + jnp.where(r + s < CHUNK, pltpu.roll(c, t - s, 0), 0.0)
                s *= 2
            di = jnp.broadcast_to(d[:, 0:1], g.shape)
            db = jnp.broadcast_to(c[:, 1:2], g.shape)
            out = out + jnp.where(lane == h, di, 0.0) + jnp.where(lane == NH + h, db * dsig, 0.0)
        o_ref[...] = out.astype(BF16)
        s_ref[...] = jnp.sum(out, axis=0, keepdims=True)

    return pl.pallas_call(
        body, name="gates_bwd", grid=(1,),
        in_specs=[pl.BlockSpec((NH, t, LANE), lambda i: (0, 0, 0)),
                  pl.BlockSpec((t, LANE), lambda i: (0, gate_tile)), pl.BlockSpec((1, LANE), lambda i: (0, 0))],
        out_specs=[pl.BlockSpec((t, LANE), lambda i: (0, 0)), pl.BlockSpec((1, LANE), lambda i: (0, 0))],
        out_shape=[jax.ShapeDtypeStruct((t, LANE), BF16), jax.ShapeDtypeStruct((1, LANE), F32)],
        compiler_params=_params("arbitrary"),
    )(dgate, proj, bias_tile)


def _in_turn(heads):
    while heads:
        heads = [g for g in heads if next(g, heads) is not heads]


def _chunk_gates(gc, gr, h, mprev):
    L = CHUNK
    icol, bcol = gc[:, h:h + 1], gc[:, h + NH:h + NH + 1]
    irow, brow = gr[h:h + 1, :], gr[h + NH:h + NH + 1, :]
    tri = _iota((L, L), 0) >= _iota((L, L), 1)
    log_d = jnp.where(tri, bcol - brow + irow, -jnp.inf)
    inter = bcol + mprev
    mt = jnp.maximum(inter, jnp.max(log_d, axis=1, keepdims=True))
    dw = jnp.exp(log_d - mt)
    iw = jnp.exp(inter - mt)
    g = brow[:, L - 1:L]
    wlog_col = g - bcol + icol
    wlog_row = g - brow + irow
    mnew = jnp.maximum(g + mprev, jnp.max(wlog_row, axis=1, keepdims=True))
    wcol = jnp.exp(wlog_col - mnew)
    decay = jnp.exp(g + mprev - mnew)
    return dw, iw, mt, wcol, decay, mnew


def _mlstm_fwd(proj, gcol, grow, t, wc, dh):
    nc = t // CHUNK
    wm = NH * dh
    assert wc == wm, (wc, wm)
    qoff = 3 * wc // wm
    scale = dh ** -0.5

    def body(q_ref, k_ref, v_ref, gc_ref, gr_ref, h_ref, cs_ref, ns_ref, c_s, n_s, m_s):
        @pl.when(pl.program_id(0) == 0)
        def _():
            c_s[...] = jnp.zeros_like(c_s)
            n_s[...] = jnp.zeros_like(n_s)
            m_s[...] = jnp.zeros_like(m_s)

        gc, gr = gc_ref[...], gr_ref[0]
        done = [None] * NH

        def head(h):
            cols = slice(h * dh, (h + 1) * dh)
            mprev = m_s[h, 0:1, 0:1]
            cprev = c_s[h]
            n8 = n_s[h]
            nprev = n8[0:1]
            qs = q_ref[:, cols] * scale
            k = k_ref[:, cols]
            qs_b, k_b, v_b = qs.astype(BF16), k.astype(BF16), v_ref[:, cols].astype(BF16)
            qk = _dot(qs_b, k_b, _NT)
            yield
            q_c = _dot(qs_b, cprev.astype(BF16))
            yield
            dw, iw, mt, wcol, decay, mnew = _chunk_gates(gc, gr, h, mprev)
            yield
            s = qk * dw
            wk = wcol * k
            num = _dot(s.astype(BF16), v_b) + iw * q_c
            yield
            c_new = decay * cprev + _dot(wk.astype(BF16), v_b, _TN)
            yield
            den = jnp.sum(s, axis=1, keepdims=True) + iw * jnp.sum(qs * nprev, axis=1, keepdims=True)
            done[h] = (cprev, jnp.where(_iota(n8.shape, 0) == 1, mprev, n8),
                       num / jnp.maximum(jnp.abs(den), jnp.exp(-mt)), c_new,
                       decay * n8 + jnp.sum(wk, axis=0, keepdims=True), mnew)

        _in_turn([head(h) for h in range(NH)])
        for h, (c_old, n_old, h_out, c_new, n_new, m_new) in enumerate(done):
            cs_ref[h] = c_old
            ns_ref[h] = n_old
            h_ref[:, h * dh:(h + 1) * dh] = h_out
            c_s[h] = c_new
            n_s[h] = n_new
            m_s[h] = jnp.broadcast_to(m_new, m_s.shape[1:])

    grp = lambda off: pl.BlockSpec((CHUNK, wm), lambda c: (c, qoff + off))
    return pl.pallas_call(
        body, name="mlstm_fwd", grid=(nc,),
        in_specs=[grp(0), grp(1), grp(2),
                  pl.BlockSpec((CHUNK, LANE), lambda c: (c, 0)),
                  pl.BlockSpec((1, 8, CHUNK), lambda c: (c, 0, 0))],
        out_specs=[pl.BlockSpec((CHUNK, wm), lambda c: (c, 0)),
                   pl.BlockSpec((NH, None, dh, dh), lambda c: (0, c, 0, 0)),
                   pl.BlockSpec((NH, None, 8, dh), lambda c: (0, c, 0, 0))],
        out_shape=[jax.ShapeDtypeStruct((t, wm), F32),
                   jax.ShapeDtypeStruct((NH, nc, dh, dh), F32),
                   jax.ShapeDtypeStruct((NH, nc, 8, dh), F32)],
        scratch_shapes=[pltpu.VMEM((NH, dh, dh), F32), pltpu.VMEM((NH, 8, dh), F32), pltpu.VMEM((NH, 8, LANE), F32)],
        compiler_params=_params("arbitrary"),
    )(proj, proj, proj, gcol, grow)


def _mlstm_bwd(proj, gcol, grow, hval, dh_in, cs, ns, t, wc, dh):
    nc = t // CHUNK
    wm = NH * dh
    assert wc == wm, (wc, wm)
    qoff = 3 * wc // wm
    scale = dh ** -0.5
    L = CHUNK

    def body(q_ref, k_ref, v_ref, gc_ref, gr_ref, h_ref, dh_ref, cs_ref, ns_ref,
             dq_ref, dk_ref, dv_ref, dg_ref, dc_s, dn_s):
        @pl.when(pl.program_id(0) == 0)
        def _():
            dc_s[...] = jnp.zeros_like(dc_s)
            dn_s[...] = jnp.zeros_like(dn_s)

        gc, gr = gc_ref[...], gr_ref[0]
        eye = _iota((L, L), 0) == _iota((L, L), 1)
        lane = _iota((L, LANE), 1)
        last = _iota((L, 1), 0) == L - 1
        done = [None] * NH

        def head(h):
            cols = slice(h * dh, (h + 1) * dh)
            ns8 = ns_ref[h]
            nprev = ns8[0:1]
            mprev = ns8[1:2, 0:1]
            cprev = cs_ref[h]
            dcn = dc_s[h]
            dn8 = dn_s[h]
            dnn = dn8[0:1]

            qs = q_ref[:, cols] * scale
            k = k_ref[:, cols]
            qs_b, k_b, v_b = qs.astype(BF16), k.astype(BF16), v_ref[:, cols].astype(BF16)
            qk = _dot(qs_b, k_b, _NT)
            yield
            dw, iw, mt, wcol, decay, _ = _chunk_gates(gc, gr, h, mprev)
            yield
            s = qk * dw
            den = jnp.sum(s, axis=1, keepdims=True) + iw * jnp.sum(qs * nprev, axis=1, keepdims=True)
            emt = jnp.exp(-mt)
            r = 1.0 / jnp.maximum(jnp.abs(den), emt)
            dout = dh_ref[:, cols]
            dnum = dout * r
            dden = (-jnp.sum(dout * h_ref[:, cols], axis=1, keepdims=True) * r
                    * jnp.where(jnp.abs(den) > emt, jnp.sign(den), 0.0))
            dnum_b = dnum.astype(BF16)
            cprev_b = cprev.astype(BF16)
            dcn_b = dcn.astype(BF16)
            yield

            g_raw = _dot(dnum_b, v_b, _NT)
            yield
            q_inter = _dot(dnum_b, cprev_b, _NT)
            yield
            k_raw = _dot(v_b, dcn_b, _NT)
            yield
            gd = (g_raw + dden) * dw
            gd_b = gd.astype(BF16)
            dqs_inter = iw * (q_inter + dden * nprev)
            dk_inter = wcol * (k_raw + dnn)
            wk = wcol * k
            iq = iw * qs
            dqs = _dot(gd_b, k_b) + dqs_inter
            yield
            dk = _dot(gd_b, qs_b, _TN) + dk_inter
            yield
            dv = _dot(s.astype(BF16), dnum_b, _TN) + _dot(wk.astype(BF16), dcn_b)
            yield
            dc_new = decay * dcn + _dot(iq.astype(BF16), dnum_b, _TN)
            yield

            e = gd * qk
            e_cols = jnp.sum(jnp.where(eye, jnp.sum(e, axis=0, keepdims=True), 0.0), axis=1, keepdims=True)
            yield
            k_inter = jnp.sum(k * dk_inter, axis=1, keepdims=True)
            rq = jnp.sum(e, axis=1, keepdims=True) + jnp.sum(qs * dqs_inter, axis=1, keepdims=True)
            rk = e_cols + k_inter
            hsum = jnp.sum(k_inter, axis=0, keepdims=True)
            jdec = decay * (jnp.sum(jnp.sum(dcn * cprev, axis=1, keepdims=True), axis=0, keepdims=True)
                            + jnp.sum(dnn * nprev, axis=1, keepdims=True))
            db = rq - rk + jnp.where(last, hsum + jdec, 0.0)
            done[h] = (jnp.where(lane == 0, rk, jnp.where(lane == 1, db, 0.0)),
                       (dqs * scale).astype(BF16), dk.astype(BF16), dv.astype(BF16), dc_new,
                       decay * dn8 + jnp.sum(iq * dden, axis=0, keepdims=True))

        _in_turn([head(h) for h in range(NH)])
        for h, (dgate, dq, dk, dv, dc_new, dn_new) in enumerate(done):
            cols = slice(h * dh, (h + 1) * dh)
            dg_ref[h] = dgate
            dq_ref[:, cols] = dq
            dk_ref[:, cols] = dk
            dv_ref[:, cols] = dv
            dc_s[h] = dc_new
            dn_s[h] = dn_new

    rc = lambda c: nc - 1 - c
    grp = lambda off: pl.BlockSpec((L, wm), lambda c: (rc(c), qoff + off))
    hm = pl.BlockSpec((L, wm), lambda c: (rc(c), 0))
    act = jax.ShapeDtypeStruct((t, wm), BF16)
    return pl.pallas_call(
        body, name="mlstm_bwd", grid=(nc,),
        in_specs=[grp(0), grp(1), grp(2),
                  pl.BlockSpec((L, LANE), lambda c: (rc(c), 0)),
                  pl.BlockSpec((1, 8, L), lambda c: (rc(c), 0, 0)),
                  hm, hm,
                  pl.BlockSpec((NH, None, dh, dh), lambda c: (0, rc(c), 0, 0)),
                  pl.BlockSpec((NH, None, 8, dh), lambda c: (0, rc(c), 0, 0))],
        out_specs=[hm, hm, hm, pl.BlockSpec((NH, L, LANE), lambda c: (0, rc(c), 0))],
        out_shape=[act, act, act, jax.ShapeDtypeStruct((NH, t, LANE), F32)],
        scratch_shapes=[pltpu.VMEM((NH, dh, dh), F32), pltpu.VMEM((NH, 8, dh), F32)],
        compiler_params=_params("arbitrary"),
    )(proj, proj, proj, gcol, grow, hval, dh_in, cs, ns)


def _head_norm(hv):
    mu = jnp.mean(hv, axis=1, keepdims=True)
    hc = hv - mu
    rstd = lax.rsqrt(jnp.mean(hc * hc, axis=1, keepdims=True) + HN_EPS)
    return hc * rstd, rstd


def _hnorm_fwd(hval, proj, gain, y, t, wc, dh, tr=512):
    ooff = 3 * wc // dh + 3 * NH
    tr = min(tr, t)

    def body(h_ref, o_ref, g_ref, y_in, y_ref):
        hhat, _ = _head_norm(h_ref[...])
        y_ref[...] = (_sigmoid(o_ref[...]) * hhat * g_ref[...]).astype(BF16)

    return pl.pallas_call(
        body, name="hnorm_fwd", grid=(t // tr, NH),
        in_specs=[pl.BlockSpec((tr, dh), lambda i, h: (i, h)),
                  pl.BlockSpec((tr, dh), lambda i, h: (i, ooff + h)),
                  pl.BlockSpec((1, dh), lambda i, h: (0, h)),
                  pl.BlockSpec(memory_space=pl.ANY)],
        out_specs=pl.BlockSpec((None, tr, dh), lambda i, h: (1, i, h)),
        out_shape=jax.ShapeDtypeStruct(y.shape, BF16),
        input_output_aliases={3: 0},
        compiler_params=_params("parallel", "parallel"),
    )(hval, proj, gain, y)


def _hnorm_bwd(dy, hval, proj, gain, t, wc, dh, tr=512):
    ooff = 3 * wc // dh + 3 * NH
    tr = min(tr, t)
    yoff = wc // dh

    def body(dy_ref, h_ref, o_ref, g_ref, do_ref, dh_ref, dg_ref):
        i = pl.program_id(1)
        hhat, rstd = _head_norm(h_ref[...])
        gain_v = g_ref[...]
        sig = _sigmoid(o_ref[...])
        d = dy_ref[...]
        do_ref[...] = (d * hhat * gain_v * sig * (1.0 - sig)).astype(BF16)
        dhn = d * sig
        part = jnp.sum(dhn * hhat, axis=0, keepdims=True)

        @pl.when(i == 0)
        def _():
            dg_ref[...] = part

        @pl.when(i > 0)
        def _():
            dg_ref[...] += part

        dhat = dhn * gain_v
        dh_ref[...] = rstd * (dhat - jnp.mean(dhat, axis=1, keepdims=True)
                              - hhat * jnp.mean(dhat * hhat, axis=1, keepdims=True))

    blk = lambda off: pl.BlockSpec((tr, dh), lambda h, i: (i, off + h))
    return pl.pallas_call(
        body, name="hnorm_bwd", grid=(NH, t // tr),
        in_specs=[blk(yoff), blk(0), blk(ooff), pl.BlockSpec((1, dh), lambda h, i: (0, h))],
        out_specs=[blk(0), blk(0), pl.BlockSpec((1, dh), lambda h, i: (0, h))],
        out_shape=[jax.ShapeDtypeStruct((t, NH * dh), BF16), jax.ShapeDtypeStruct((t, NH * dh), F32),
                   jax.ShapeDtypeStruct((1, NH * dh), F32)],
        compiler_params=_params("parallel", "arbitrary"),
    )(dy, hval, proj, gain)


def _ln_stats(z):
    mu = jnp.mean(z, axis=1, keepdims=True)
    zc = z - mu
    rstd = lax.rsqrt(jnp.mean(zc * zc, axis=1, keepdims=True) + LN_EPS)
    return zc * rstd, rstd


def _ln_bwd(dy, xhat, rstd, g):
    dxh = dy * g
    return rstd * (dxh - jnp.mean(dxh, axis=1, keepdims=True) - xhat * jnp.mean(dxh * xhat, axis=1, keepdims=True))


def _accum(ref, i, part):
    @pl.when(i == 0)
    def _():
        ref[...] = part

    @pl.when(i > 0)
    def _():
        ref[...] += part


def _ln1_fwd(x, mix, g, b, tr=256):
    t, d = x.shape

    def body(x_ref, m_ref, g_ref, b_ref, xh_ref, rs_ref, xb_ref):
        xhat, rstd = _ln_stats(ALPHA * x_ref[...] + m_ref[...])
        xh_ref[...] = xhat
        rs_ref[...] = rstd
        xb_ref[...] = (xhat * g_ref[...] + b_ref[...]).astype(BF16)

    row = pl.BlockSpec((tr, d), lambda i: (i, 0))
    vec = pl.BlockSpec((1, d), lambda i: (0, 0))
    return pl.pallas_call(
        body, name="ln1_fwd", grid=(t // tr,),
        in_specs=[row, row, vec, vec],
        out_specs=[row, pl.BlockSpec((tr, 1), lambda i: (i, 0)), row],
        out_shape=[jax.ShapeDtypeStruct((t, d), F32), jax.ShapeDtypeStruct((t, 1), F32),
                   jax.ShapeDtypeStruct((t, d), BF16)],
        compiler_params=_params("parallel"),
    )(x, mix, g, b)


def _ln2_loss(xhat1, g1, b1, ff, target, g2, b2, tr=256):
    t, d = ff.shape

    def body(xh_ref, g1_ref, b1_ref, f_ref, t_ref, g_ref, b_ref, dz_ref, dzb_ref, dg_ref, db_ref, l_ref):
        i = pl.program_id(0)
        x1 = xh_ref[...] * g1_ref[...] + b1_ref[...]
        xhat, rstd = _ln_stats(ALPHA * x1 + f_ref[...])
        gv = g_ref[...]
        e = xhat * gv + b_ref[...] - t_ref[...]
        lsum = jnp.sum(jnp.sum(e * e, axis=1, keepdims=True), axis=0, keepdims=True) * (0.5 / d)
        dy = e * (1.0 / d)
        _accum(dg_ref, i, jnp.sum(dy * xhat, axis=0, keepdims=True))
        _accum(db_ref, i, jnp.sum(dy, axis=0, keepdims=True))
        _accum(l_ref, i, jnp.broadcast_to(lsum, l_ref.shape))
        dz = _ln_bwd(dy, xhat, rstd, gv)
        dz_ref[...] = dz
        dzb_ref[...] = dz.astype(BF16)

    row = pl.BlockSpec((tr, d), lambda i: (i, 0))
    vec = pl.BlockSpec((1, d), lambda i: (0, 0))
    return pl.pallas_call(
        body, name="ln2_loss", grid=(t // tr,),
        in_specs=[row, vec, vec, row, row, vec, vec],
        out_specs=[row, row, vec, vec, pl.BlockSpec((8, LANE), lambda i: (0, 0))],
        out_shape=[jax.ShapeDtypeStruct((t, d), F32), jax.ShapeDtypeStruct((t, d), BF16),
                   jax.ShapeDtypeStruct((1, d), F32), jax.ShapeDtypeStruct((1, d), F32),
                   jax.ShapeDtypeStruct((8, LANE), F32)],
        compiler_params=_params("arbitrary"),
    )(xhat1, g1, b1, ff, target, g2, b2)


def _ln1_bwd(dz2, dffn, xhat1, rstd1, g1, tr=256):
    t, d = dz2.shape

    def body(a_ref, f_ref, xh_ref, rs_ref, g_ref, dz_ref, dzb_ref, dg_ref, db_ref):
        i = pl.program_id(0)
        dy = ALPHA * a_ref[...] + f_ref[...]
        xhat = xh_ref[...]
        _accum(dg_ref, i, jnp.sum(dy * xhat, axis=0, keepdims=True))
        _accum(db_ref, i, jnp.sum(dy, axis=0, keepdims=True))
        dz = _ln_bwd(dy, xhat, rs_ref[...], g_ref[...])
        dz_ref[...] = dz
        dzb_ref[...] = dz.astype(BF16)

    row = pl.BlockSpec((tr, d), lambda i: (i, 0))
    vec = pl.BlockSpec((1, d), lambda i: (0, 0))
    return pl.pallas_call(
        body, name="ln1_bwd", grid=(t // tr,),
        in_specs=[row, row, row, pl.BlockSpec((tr, 1), lambda i: (i, 0)), vec],
        out_specs=[row, row, vec, vec],
        out_shape=[jax.ShapeDtypeStruct((t, d), F32), jax.ShapeDtypeStruct((t, d), BF16),
                   jax.ShapeDtypeStruct((1, d), F32), jax.ShapeDtypeStruct((1, d), F32)],
        compiler_params=_params("arbitrary"),
    )(dz2, dffn, xhat1, rstd1, g1)


def _ffn_act_fwd(hid0, w_fc, b_fc, t, dff):
    nb = dff // LANE

    def body(hv_ref, hg_ref, wv_ref, wg_ref, bv_ref, bg_ref, a_ref):
        val = _conv(hv_ref[...], wv_ref[...]) + bv_ref[...]
        gate = _conv(hg_ref[...], wg_ref[...]) + bg_ref[...]
        a_ref[...] = (gate * _sigmoid(gate) * val).astype(BF16)

    col = lambda off: pl.BlockSpec((t, LANE), lambda j: (0, j + off))
    w3 = lambda off: pl.BlockSpec((3, LANE), lambda j: (0, j + off))
    w1 = lambda off: pl.BlockSpec((1, LANE), lambda j: (0, j + off))
    return pl.pallas_call(
        body, name="ffn_act_fwd", grid=(nb,),
        in_specs=[col(0), col(nb), w3(0), w3(nb), w1(0), w1(nb)],
        out_specs=col(0),
        out_shape=jax.ShapeDtypeStruct((t, dff), BF16),
        compiler_params=_params("parallel"),
    )(hid0, hid0, w_fc, w_fc, b_fc, b_fc)


def _ffn_act_bwd(da, hid0, w_fc, b_fc, t, dff):
    nb = dff // LANE

    def body(da_ref, hv_ref, hg_ref, wv_ref, wg_ref, bv_ref, bg_ref,
             dhv_ref, dhg_ref, dwv_ref, dwg_ref, dbv_ref, dbg_ref):
        hv, hg, wv, wg = hv_ref[...], hg_ref[...], wv_ref[...], wg_ref[...]
        rv, rg = _rolled(hv), _rolled(hg)
        val = _conv(hv, wv, rv) + bv_ref[...]
        gate = _conv(hg, wg, rg) + bg_ref[...]
        sig = _sigmoid(gate)
        d = da_ref[...]
        dsig = d * sig
        dval = dsig * gate
        dgate = dsig * val * (1.0 + gate * (1.0 - sig))
        dhv_ref[...] = _conv_t(dval, wv).astype(BF16)
        dhg_ref[...] = _conv_t(dgate, wg).astype(BF16)
        dwv_ref[...] = _conv_dw(dval, hv, rv)
        dwg_ref[...] = _conv_dw(dgate, hg, rg)
        dbv_ref[...] = jnp.sum(dval, axis=0, keepdims=True)
        dbg_ref[...] = jnp.sum(dgate, axis=0, keepdims=True)

    col = lambda off: pl.BlockSpec((t, LANE), lambda j: (0, j + off))
    w3 = lambda off: pl.BlockSpec((3, LANE), lambda j: (0, j + off))
    w1 = lambda off: pl.BlockSpec((1, LANE), lambda j: (0, j + off))
    s3 = jax.ShapeDtypeStruct((3, dff), F32)
    s1 = jax.ShapeDtypeStruct((1, dff), F32)
    return pl.pallas_call(
        body, name="ffn_act_bwd", grid=(nb,),
        in_specs=[col(0), col(0), col(nb), w3(0), w3(nb), w1(0), w1(nb)],
        out_specs=[col(0), col(0), w3(0), w3(0), w1(0), w1(0)],
        out_shape=[jax.ShapeDtypeStruct((t, dff), BF16)] * 2 + [s3, s3, s1, s1],
        compiler_params=_params("parallel"),
    )(da, hid0, hid0, w_fc, w_fc, b_fc, b_fc)


class _Ready:
    def __init__(self, **weights):
        self.weights = weights

    def begin(self, after):
        return None

    def forward(self, name, after):
        return None

    def get(self, name, after):
        return self.weights[name]


class _Kept:
    def __init__(self):
        self.grads = {}

    def start(self, name, grad):
        self.grads[name] = grad
        return None

    def relay(self, name, after):
        return None

    def meanwhile(self, small, loss, after):
        return None


def _behind(a, token):
    return a if token is None else a + token[0:1, 0:1].reshape((1,) * a.ndim)


def _local_step(x, target, w_in, b_gates, w_sc, gain, w_out, ln1_g, ln1_b, w_up, w_fc, b_fc, w_down, ln2_g, ln2_b,
                gx=None, wx=None, x_b=None):
    t, d = x.shape
    wc = d // 2
    dh = (d - wc) // NH
    wm = NH * dh
    dff = w_fc.shape[1] // 2
    if wx is None:
        wx = _Ready(w_out=w_out, w_up=w_up, w_down=w_down)
    ninp = 3 * wc + 4 * wm + LANE
    nin = 3 * wc + 4 * wm
    gate_tile = nin // LANE
    nc = t // CHUNK
    bias_tile = jnp.pad(b_gates, ((0, 0), (0, LANE - 2 * NH)))

    if x_b is None:
        x_b = x.astype(BF16)
    proj = _matmul(x_b, w_in, "nt", F32, "proj", tm=512, tn=2432, tk=d, n=ninp, after=wx.begin(w_in))
    y = _sconv_fwd(proj, w_sc, t, wc)
    gcol = _gates_prep(proj, bias_tile, t, gate_tile)
    grow = gcol[:, :8].T.reshape(8, nc, CHUNK).transpose(1, 0, 2)
    hval, cs, ns = _mlstm_fwd(proj, gcol, grow, t, wc, dh)
    y = _hnorm_fwd(hval, proj, gain, y, t, wc, dh)
    tok = wx.forward("w_out", y)
    w_out = wx.get("w_out", tok)
    mix = _matmul(y, w_out, "nn", F32, "out_proj", tm=512, tn=1024, tk=wc, a_blocked=True, after=tok)
    xhat1, rstd1, x1_b = _ln1_fwd(x, mix, _behind(ln1_g, wx.forward("w_up", mix)), ln1_b)
    w_up = wx.get("w_up", x1_b)
    wsl = w_up.shape[2]
    hid0 = _matmul(x1_b, w_up, "nn", F32, "ffn_up", tm=1024, tn=wsl, tk=d, b_blocked=True)
    act = _ffn_act_fwd(hid0, w_fc, _behind(b_fc, wx.forward("w_down", hid0)), t, dff)
    w_down = wx.get("w_down", act)
    ff = _matmul(act, w_down, "nn", F32, "ffn_down", tm=1024, tn=512, tk=dff)
    dz2, dz2_b, d_ln2_g, d_ln2_b, loss = _ln2_loss(xhat1, ln1_g, ln1_b, ff, target, ln2_g, ln2_b)

    if gx is None:
        gx = _Kept()
    d_w_down = _matmul(act, dz2_b, "tn", BF16, "ffn_down_dw", tm=1408, tn=1024, tk=t)
    d_act = _matmul(dz2_b, w_down, "nt", F32, "ffn_down_dx", tm=2048, tn=512, tk=d, after=gx.start("w_down", d_w_down))
    *d_hid0, dwv, dwg, dbv, dbg = _ffn_act_bwd(d_act, hid0, w_fc, _behind(b_fc, gx.relay("w_down", d_act)), t, dff)
    d_w_fc = jnp.concatenate([dwv, dwg], axis=1)
    d_b_fc = jnp.concatenate([dbv, dbg], axis=1)
    d_hid0 = tuple(d_hid0[:2])
    d_w_up = _matmul(x1_b, d_hid0, "tn", BF16, "ffn_up_dw", tm=1024, tn=wsl, tk=t, o_width=wsl)
    d_x1_ffn = _matmul(d_hid0, w_up, "nt", F32, "ffn_up_dx", tm=1024, tn=1024, tk=wsl, b_blocked=True,
                       after=gx.start("w_up", d_w_up))
    dz1, dz1_b, d_ln1_g, d_ln1_b = _ln1_bwd(dz2, d_x1_ffn, xhat1, rstd1, _behind(ln1_g, gx.relay("w_up", d_x1_ffn)))

    d_w_out = _matmul(y, dz1_b, "tn", BF16, "out_proj_dw", tm=1024, tn=1024, tk=t, a_blocked=True)
    dy = _matmul(dz1_b, w_out, "nt", F32, "out_proj_dx", tm=1024, tn=1024, tk=d, after=gx.start("w_out", d_w_out))
    dcb, dcc, dch, d_w_sc = _sconv_bwd(dy, proj, _behind(w_sc, gx.relay("w_out", dy)), t, wc)
    d_o, d_hval, d_gain = _hnorm_bwd(dy, hval, proj, gain, t, wc, dh)
    dq, dk, dv, dgate = _mlstm_bwd(proj, gcol, grow, hval, d_hval, cs, ns, t, wc, dh)
    dgt, d_b_gates = _gates_bwd(dgate, proj, bias_tile, t, gate_tile)
    d_proj = jnp.concatenate([dcb, dcc, dch, dq, dk, dv, d_o, dgt], axis=1)
    d_w_in = _matmul(d_proj, x_b, "tn", BF16, "proj_dw", tm=2432, tn=1024, tk=t)
    small = dict(b_gates=d_b_gates[:, :2 * NH], w_sc_conv=d_w_sc, mh_gain=d_gain, ln1_g=d_ln1_g, ln1_b=d_ln1_b,
                 w_ffn_conv=d_w_fc, b_ffn_conv=d_b_fc, ln2_g=d_ln2_g, ln2_b=d_ln2_b)
    token = gx.start("w_in", d_w_in)
    token = gx.relay("w_in", gx.meanwhile(small, loss, token))
    grad_x = _matmul(d_proj, w_in, "nn", F32, "proj_dx", tm=512, tn=512, tk=ninp, add=dz1, add_scale=ALPHA, after=token)
    return loss, grad_x, small, gx


HBM = pl.BlockSpec(memory_space=pltpu.HBM)


def _place():
    return lax.axis_index("x"), lax.axis_index("y"), lax.axis_index("c")


def _index(p):
    return 4 * p[0] + 2 * p[1] + p[2]


def _all_gather(arrs, name):
    n = len(arrs)

    def body(*refs):
        ins, outs = refs[:n], refs[n:2 * n]
        send_sems, recv_sems, local_sems = refs[2 * n:]
        x, y, c = _place()
        me, sibling = (x, y, c), (x, y, 1 - c)
        chips = [(1 - x, y), (x, 1 - y), (1 - x, 1 - y)]

        def copy(a, k, block, to, own=False):
            dst = outs[a].at[_index(block)]
            return pltpu.make_async_remote_copy(
                src_ref=ins[a] if own else dst, dst_ref=dst,
                send_sem=send_sems.at[k * n + a], recv_sem=recv_sems.at[k * n + a],
                device_id=to, device_id_type=MESH)

        mine = [pltpu.make_async_copy(ins[a], outs[a].at[_index(me)], local_sems.at[a]) for a in range(n)]
        for cp in mine:
            cp.start()
        first = []
        for a in range(n):
            first.append(copy(a, 0, me, sibling, own=True))
            first += [copy(a, 1 + j, me, (*chip, c), own=True) for j, chip in enumerate(chips)]
        for cp in first:
            cp.start()
        passed = []
        for j, chip in enumerate(chips):
            for a in range(n):
                copy(a, 1 + j, (*chip, c), me).wait_recv()
                cp = copy(a, 4 + j, (*chip, c), sibling)
                cp.start()
                passed.append(cp)
        for a in range(n):
            copy(a, 0, sibling, me).wait_recv()
            for j, chip in enumerate(chips):
                copy(a, 4 + j, (*chip, 1 - c), me).wait_recv()
        for cp in first + passed:
            cp.wait_send()
        for cp in mine:
            cp.wait()

    return pl.pallas_call(
        body, name=name, in_specs=[HBM] * n, out_specs=[HBM] * n,
        out_shape=[jax.ShapeDtypeStruct((N_DEV,) + a.shape, a.dtype) for a in arrs],
        scratch_shapes=[pltpu.SemaphoreType.DMA((7 * n,)), pltpu.SemaphoreType.DMA((7 * n,)),
                        pltpu.SemaphoreType.DMA((n,))],
    )(*arrs)


SEM = pl.BlockSpec(memory_space=pltpu.SEMAPHORE)
EFFECT = pltpu.SideEffectType.DATAFLOW_SIDE_EFFECTING


def _chips(x, y):
    return [(1 - x, y), (x, 1 - y), (1 - x, 1 - y)]


N_CHIP = N_DEV // 2


def _pair_route(x, y, c):
    return [((x, y, 1 - c), 2 * q + (1 - c), q, q) for q in range(N_CHIP)]


def _chip_route(x, y, c):
    mine = 2 * x + y
    return [((*chip, c), 2 * chip[0] + chip[1], mine, 2 * chip[0] + chip[1]) for chip in _chips(x, y)]


def _exchange_pieces(g_ref, land_ref, width, tail):
    if not tail:
        return [(lambda i: g_ref.at[i], lambda s: land_ref.at[s])]
    rows = lambda i, n: pl.ds(pl.multiple_of(i * width, IN_TAIL), n)
    return [(lambda i: g_ref.at[rows(i, width), :], lambda s: land_ref.at[s, pl.ds(0, width), :]),
            (lambda i: g_ref.at[rows(i + 1, IN_TAIL), :], lambda s: land_ref.at[s, pl.ds(width, IN_TAIL), :])]


def _exchange_start(grad, route, tail, name):
    width = IN_SLAB if tail else grad.shape[1]
    n_p = 2 if tail else 1
    n_c = len(route(0, 0, 0))
    land_shape = (N_CHIP, width + (IN_TAIL if tail else 0), grad.shape[-1])

    def body(g_ref, land_ref, send_sems, recv_sems, g_thru, land_thru, token):
        for j, (peer, slab, slot, _) in enumerate(route(*_place())):
            for p, (src, dst) in enumerate(_exchange_pieces(g_ref, land_ref, width, tail)):
                pltpu.make_async_remote_copy(src_ref=src(slab), dst_ref=dst(slot), send_sem=send_sems.at[j * n_p + p],
                                             recv_sem=recv_sems.at[j * n_p + p], device_id=peer,
                                             device_id_type=MESH).start()
        token[...] = jnp.zeros_like(token)

    return pl.pallas_call(
        body, name=name,
        out_shape=(pltpu.SemaphoreType.DMA((n_c * n_p,)), pltpu.SemaphoreType.DMA((n_c * n_p,)),
                   pltpu.HBM(grad.shape, grad.dtype), pltpu.HBM(land_shape, grad.dtype),
                   jax.ShapeDtypeStruct((8, LANE), F32)),
        in_specs=(HBM, HBM), out_specs=(SEM, SEM, HBM, HBM, pl.BlockSpec(memory_space=pltpu.VMEM)),
        input_output_aliases={0: 2, 1: 3},
        compiler_params=pltpu.CompilerParams(has_side_effects=EFFECT),
    )(pltpu.with_memory_space_constraint(grad, pltpu.HBM),
      pltpu.with_memory_space_constraint(lax.empty(land_shape, grad.dtype), pltpu.HBM))


def _exchange_wait(send_sems, recv_sems, g_thru, land_thru, after, route, tail, name):
    width = IN_SLAB if tail else g_thru.shape[1]
    n_p = 2 if tail else 1

    def body(g_ref, land_ref, send_sems, recv_sems, after_ref, g_dead, got_ref):
        for j, (peer, slab, _, slot) in enumerate(route(*_place())):
            for p, (src, dst) in enumerate(_exchange_pieces(g_ref, land_ref, width, tail)):
                cp = pltpu.make_async_remote_copy(src_ref=src(slab), dst_ref=dst(slot),
                                                  send_sem=send_sems.at[j * n_p + p], recv_sem=recv_sems.at[j * n_p + p],
                                                  device_id=peer, device_id_type=MESH)
                cp.wait_send()
                cp.wait_recv()

    return pl.pallas_call(
        body, name=name,
        out_shape=(pltpu.HBM(g_thru.shape, g_thru.dtype), pltpu.HBM(land_thru.shape, land_thru.dtype)),
        in_specs=(HBM, HBM, SEM, SEM, pl.BlockSpec(memory_space=pl.ANY)), out_specs=(HBM, HBM),
        input_output_aliases={0: 0, 1: 1},
        compiler_params=pltpu.CompilerParams(has_side_effects=EFFECT),
    )(g_thru, land_thru, send_sems, recv_sems, after)


def _pair_add(grad, pair, core, tail, name):
    rows, cols = (IN_SLAB if tail else grad.shape[1]), grad.shape[-1]
    total = pair.shape[1]

    def body(core_ref, *refs):
        if tail:
            g_ref, t_ref, p_ref, o_ref = refs
            o_ref[0:rows, :] = (g_ref[...].astype(F32) + p_ref[0:rows, :].astype(F32)).astype(BF16)
            o_ref[rows:total, :] = (t_ref[...].astype(F32) + p_ref[rows:total, :].astype(F32)).astype(BF16)
        else:
            g_ref, p_ref, o_ref = refs
            o_ref[...] = (g_ref[...].astype(F32) + p_ref[...].astype(F32)).astype(BF16)

    if tail:
        tc = _fit(cols, 512)
        grid = (N_CHIP, cols // tc)
        slab = pl.BlockSpec((None, total, tc), lambda q, i, core_ref: (q, 0, i))
        per = IN_SLAB // IN_TAIL
        in_specs = [pl.BlockSpec((rows, tc), lambda q, i, core_ref: (2 * q + core_ref[0], i)),
                    pl.BlockSpec((IN_TAIL, tc), lambda q, i, core_ref: ((2 * q + core_ref[0] + 1) * per, i))]
    else:
        tr = _rows(rows, 1024)
        grid = (N_CHIP, rows // tr)
        slab = pl.BlockSpec((None, tr, cols), lambda q, i, core_ref: (q, i, 0))
        in_specs = [pl.BlockSpec((None, tr, cols), lambda q, i, core_ref: (2 * q + core_ref[0], i, 0))]
    return pl.pallas_call(
        body, name=name,
        grid_spec=pltpu.PrefetchScalarGridSpec(num_scalar_prefetch=1, grid=grid,
                                               in_specs=in_specs + [slab], out_specs=slab),
        out_shape=jax.ShapeDtypeStruct(pair.shape, BF16),
        compiler_params=_params("parallel", "parallel"),
    )(core, *([grad, grad] if tail else [grad]), pair)


def _relay_places(x, y, c):
    came_from = (c * (1 - x) + (1 - c) * x, c * y + (1 - c) * (1 - y), c)
    pass_to = (c * x + (1 - c) * (1 - x), c * (1 - y) + (1 - c) * y, c)
    return 2 - c, came_from, pass_to, pass_to


def _gather_start(blocks, after, name, spare=(), relayed=False):
    n = len(blocks)
    lands = [(N_DEV + (a in spare),) + b.shape for a, b in enumerate(blocks)]

    def body(*refs):
        b_refs, land_refs = refs[:n], refs[n:2 * n]
        send_sems, recv_sems = refs[2 * n + 1:3 * n + 1], refs[3 * n + 1:4 * n + 1]
        token = refs[-1]
        x, y, c = _place()
        me = _index((x, y, c))
        for a in range(n):
            targets = [(x, y, 1 - c)] + [(*chip, c) for chip in _chips(x, y)]
            for k, to in enumerate(targets[:3] if relayed else targets):
                pltpu.make_async_remote_copy(src_ref=b_refs[a], dst_ref=land_refs[a].at[me], send_sem=send_sems[a].at[k],
                                             recv_sem=recv_sems[a].at[k], device_id=to, device_id_type=MESH).start()
        token[...] = jnp.zeros_like(token)

    sems = [pltpu.SemaphoreType.DMA((4,))] * n
    out = pl.pallas_call(
        body, name=name,
        out_shape=(*sems, *sems, *[pltpu.HBM(b.shape, b.dtype) for b in blocks],
                   *[pltpu.HBM(s, b.dtype) for s, b in zip(lands, blocks)], jax.ShapeDtypeStruct((8, LANE), F32)),
        in_specs=(*[HBM] * (2 * n), pl.BlockSpec(memory_space=pl.ANY)),
        out_specs=(*[SEM] * (2 * n), *[HBM] * (2 * n), pl.BlockSpec(memory_space=pltpu.VMEM)),
        input_output_aliases={i: 2 * n + i for i in range(2 * n)},
        compiler_params=pltpu.CompilerParams(has_side_effects=EFFECT),
    )(*[pltpu.with_memory_space_constraint(b, pltpu.HBM) for b in blocks],
      *[pltpu.with_memory_space_constraint(lax.empty(s, b.dtype), pltpu.HBM) for s, b in zip(lands, blocks)], after)
    return [(out[a], out[n + a], out[2 * n + a], out[3 * n + a]) for a in range(n)], out[-1]


def _gather_relay(states, after, name):
    n, first_out = len(states), 3 * len(states) + len(after)

    def body(*refs):
        land_refs, send_sems, recv_sems = refs[:n], refs[n:2 * n], refs[2 * n:3 * n]
        pass_send, pass_recv = refs[first_out + n:first_out + 2 * n], refs[first_out + 2 * n:first_out + 3 * n]
        k_in, came_from, pass_to, _ = _relay_places(*_place())
        for a in range(n):
            slot = land_refs[a].at[_index(came_from)]
            pltpu.make_async_remote_copy(src_ref=slot, dst_ref=slot, send_sem=send_sems[a].at[k_in],
                                         recv_sem=recv_sems[a].at[k_in], device_id=came_from,
                                         device_id_type=MESH).wait_recv()
        for a in range(n):
            slot = land_refs[a].at[_index(came_from)]
            pltpu.make_async_remote_copy(src_ref=slot, dst_ref=slot, send_sem=pass_send[a].at[0],
                                         recv_sem=pass_recv[a].at[0], device_id=pass_to, device_id_type=MESH).start()
        refs[-1][...] = jnp.zeros_like(refs[-1])

    lands = [st[3] for st in states]
    pair = [pltpu.SemaphoreType.DMA((1,))] * n
    out = pl.pallas_call(
        body, name=name,
        out_shape=(*[pltpu.HBM(l.shape, l.dtype) for l in lands], *pair, *pair, jax.ShapeDtypeStruct((8, LANE), F32)),
        in_specs=(*[HBM] * n, *[SEM] * (2 * n), *[pl.BlockSpec(memory_space=pl.ANY)] * len(after)),
        out_specs=(*[HBM] * n, *[SEM] * (2 * n), pl.BlockSpec(memory_space=pltpu.VMEM)),
        input_output_aliases={a: a for a in range(n)},
        compiler_params=pltpu.CompilerParams(has_side_effects=EFFECT),
    )(*lands, *[st[0] for st in states], *[st[1] for st in states], *after)
    return [(st[0], st[1], st[2], out[a], (out[n + a], out[2 * n + a])) for a, st in enumerate(states)], out[-1]


def _gather_forward(send_sems, recv_sems, b_thru, land_thru, after, name, passed=None):
    relayed = passed is not None

    def body(b_ref, land_ref, send_sems, recv_sems, *rest):
        pass_send, pass_recv = rest[:2] if relayed else (None, None)
        send2, recv2, token = rest[-3:]
        x, y, c = _place()
        sibling = (x, y, 1 - c)
        arrivals = [sibling] + [(*chip, c) for chip in _chips(x, y)]
        waits = [(send_sems.at[k], recv_sems.at[k], frm) for k, frm in enumerate(arrivals)]
        sends = [send_sems.at[k] for k in range(4)]
        if relayed:
            k_in, _, _, other = _relay_places(x, y, c)
            waits = [waits[0], (send_sems.at[3 - k_in], recv_sems.at[3 - k_in], other),
                     (pass_send.at[0], pass_recv.at[0], arrivals[3])]
            sends[3] = pass_send.at[0]
        for sem in sends:
            pltpu.make_async_remote_copy(src_ref=b_ref, dst_ref=land_ref.at[0], send_sem=sem, recv_sem=recv_sems.at[0],
                                         device_id=sibling, device_id_type=MESH).wait_send()
        for send_sem, recv_sem, frm in waits:
            pltpu.make_async_remote_copy(src_ref=b_ref, dst_ref=land_ref.at[_index(frm)], send_sem=send_sem,
                                         recv_sem=recv_sem, device_id=frm, device_id_type=MESH).wait_recv()
        for j, chip in enumerate(_chips(x, y)):
            slot = land_ref.at[_index((*chip, c))]
            pltpu.make_async_remote_copy(src_ref=slot, dst_ref=slot, send_sem=send2.at[j], recv_sem=recv2.at[j],
                                         device_id=sibling, device_id_type=MESH).start()
        token[...] = jnp.zeros_like(token)

    extra = list(passed) if relayed else []
    return pl.pallas_call(
        body, name=name,
        out_shape=(pltpu.HBM(b_thru.shape, b_thru.dtype), pltpu.HBM(land_thru.shape, land_thru.dtype),
                   pltpu.SemaphoreType.DMA((3,)), pltpu.SemaphoreType.DMA((3,)), jax.ShapeDtypeStruct((8, LANE), F32)),
        in_specs=(HBM, HBM, SEM, SEM, *[SEM] * len(extra), pl.BlockSpec(memory_space=pl.ANY)),
        out_specs=(HBM, HBM, SEM, SEM, pl.BlockSpec(memory_space=pltpu.VMEM)),
        input_output_aliases={0: 0, 1: 1},
        compiler_params=pltpu.CompilerParams(has_side_effects=EFFECT),
    )(b_thru, land_thru, send_sems, recv_sems, *extra, after)


def _gather_finish(land_thru, send2, recv2, after, name):
    def body(land_ref, send2, recv2, after_ref, land_out):
        x, y, c = _place()
        for j, chip in enumerate(_chips(x, y)):
            cp = pltpu.make_async_remote_copy(src_ref=land_ref.at[_index((*chip, c))],
                                              dst_ref=land_ref.at[_index((*chip, 1 - c))], send_sem=send2.at[j],
                                              recv_sem=recv2.at[j], device_id=(x, y, 1 - c), device_id_type=MESH)
            cp.wait_send()
            cp.wait_recv()

    return pl.pallas_call(
        body, name=name, out_shape=pltpu.HBM(land_thru.shape, land_thru.dtype),
        in_specs=(HBM, SEM, SEM, pl.BlockSpec(memory_space=pl.ANY)), out_specs=HBM,
        input_output_aliases={0: 0},
        compiler_params=pltpu.CompilerParams(has_side_effects=EFFECT),
    )(land_thru, send2, recv2, after)


class _Gathering:
    def __init__(self, ahead, later, me):
        cast = [a.astype(BF16) for a in ahead.values()]
        started, self.token = _gather_start(cast, cast[0], "gather1_ahead", relayed=True)
        self.me, self.state, self.relayed, self.later = me, dict(zip(ahead, started)), tuple(ahead), later

    def start_first(self, first):
        started, self.token = _gather_start(list(first.values()), self.token, "gather1_first", spare=(0,), relayed=True)
        self.state.update(zip(first, started))
        self.relayed += tuple(first)

    def begin(self, after):
        return self.token

    def relay(self, *after):
        states, token = _gather_relay([self.state[n] for n in self.relayed], after, "gather_relay")
        self.state.update(zip(self.relayed, states))
        cast = [_behind(a, token).astype(BF16) for a in self.later.values()]
        started, self.token = _gather_start(cast, token, "gather1_later")
        self.state.update(zip(self.later, started))
        return self.token

    def forward(self, name, after):
        first_leg, passed = self.state[name][:4], (self.state[name][4:] or (None,))[0]
        *self.state[name], token = _gather_forward(*first_leg, after, "gather2_" + name, passed=passed)
        return token

    def get(self, name, after):
        block, land, send2, recv2 = self.state[name]
        land = _gather_finish(land, send2, recv2, after, "gather3_" + name)
        land = lax.dynamic_update_index_in_dim(land, block[None], self.me, 0)
        return land if name not in ("w_out", "w_down") else land.reshape(-1, land.shape[2])


class _Reducing:
    def __init__(self, core, chip, gather_small):
        self.core, self.chip, self.state, self.token, self.gather_small = core, chip, {}, None, gather_small

    def meanwhile(self, small, loss, after):
        self.small_sum = self.gather_small(small, loss, after)
        return self.small_sum

    def start(self, name, grad):
        tail = name == "w_in"
        g = grad if tail or grad.ndim == 3 else grad.reshape(N_DEV, grad.shape[0] // N_DEV, grad.shape[1])
        *self.state[name], token = _exchange_start(g, _pair_route, tail, "pair_send_" + name)
        return token

    def relay(self, name, after):
        tail = name == "w_in"
        grad, pair = _exchange_wait(*self.state[name], after, _pair_route, tail, "pair_recv_" + name)
        total = _pair_add(grad, pair, self.core, tail, "pair_add_" + name)
        *self.state[name], self.token = _exchange_start(total, _chip_route, False, "chip_send_" + name)
        return self.token

    def finish(self, name, after):
        total, land = _exchange_wait(*self.state[name], after, _chip_route, False, "chip_recv_" + name)
        own = lax.dynamic_index_in_dim(total, self.chip, 0, keepdims=True)
        return lax.dynamic_update_index_in_dim(land, own, self.chip, 0)


def _carry_w_in(main, tail):
    slabs, _, d = main.shape
    tc = _fit(d, 2048)
    assert slabs == N_DEV + 1 and tail.shape[:2] == (N_DEV, IN_TAIL), (main.shape, tail.shape)
    top = lambda off: pl.BlockSpec((None, IN_TAIL, tc), lambda s, j: (s + off, 0, j))

    def carry(m_ref, t_ref, o_ref):
        o_ref[...] = m_ref[...] + t_ref[...]

    main = pl.pallas_call(
        carry, name="carry_w_in", grid=(N_DEV - 1, d // tc), in_specs=[top(1), top(0)], out_specs=top(1),
        out_shape=jax.ShapeDtypeStruct(main.shape, main.dtype), input_output_aliases={0: 0},
        compiler_params=_params("parallel", "parallel"),
    )(main, tail)

    def last(m_ref, t_ref, o_ref):
        o_ref[...] = jnp.zeros_like(o_ref)
        o_ref[0:IN_TAIL, :] = t_ref[...]

    return pl.pallas_call(
        last, name="last_slab_w_in", grid=(d // tc,),
        in_specs=[pl.BlockSpec(memory_space=pl.ANY), pl.BlockSpec((None, IN_TAIL, tc), lambda j: (N_DEV - 1, 0, j))],
        out_specs=pl.BlockSpec((None, LANE, tc), lambda j: (N_DEV, 0, j)),
        out_shape=jax.ShapeDtypeStruct(main.shape, main.dtype), input_output_aliases={0: 0},
        compiler_params=_params("parallel"),
    )(main, tail)


def _rows(n, want):
    t = min(n, want)
    t -= t % 16
    while n % t:
        t -= 16
    return t


def _adam_math(w, g, m, v):
    m2 = ADAM_B1 * m + (1.0 - ADAM_B1) * g
    v2 = ADAM_B2 * v + (1.0 - ADAM_B2) * (g * g)
    m_hat = m2 * (1.0 / (1.0 - ADAM_B1 ** ADAM_STEP))
    v_hat = v2 * (1.0 / (1.0 - ADAM_B2 ** ADAM_STEP))
    return -ADAM_LR * (m_hat / (jnp.sqrt(v_hat) + ADAM_EPS) + ADAM_WD * w), m2, v2


def _slot_sum(r_ref):
    acc = r_ref[0].astype(F32)
    for i in range(1, r_ref.shape[0]):
        acc = acc + r_ref[i].astype(F32)
    return acc


def _shift_w_in(w):
    ws, d = w.shape
    tc = _fit(d, 256)

    def body(w_ref, main_ref, tail_ref, tall):
        tall[...] = jnp.zeros_like(tall)
        tall[0:ws, :] = w_ref[...]
        moved = pltpu.roll(tall[...], _index(_place()), 0).astype(BF16)
        main_ref[...] = moved[0:IN_SLAB]
        tail_ref[...] = moved[IN_SLAB:]

    return pl.pallas_call(
        body, name="shift_w_in", grid=(d // tc,),
        in_specs=[pl.BlockSpec((ws, tc), lambda j: (0, j))],
        out_specs=[pl.BlockSpec((IN_SLAB, tc), lambda j: (0, j)), pl.BlockSpec((IN_TAIL, tc), lambda j: (0, j))],
        out_shape=[jax.ShapeDtypeStruct((IN_SLAB, d), BF16), jax.ShapeDtypeStruct((IN_TAIL, d), BF16)],
        scratch_shapes=[pltpu.VMEM((IN_SLAB + IN_TAIL, tc), F32)], compiler_params=_params("parallel"),
    )(w)


def _sum_adamw_shifted(r, w, m, v, name):
    _, ph, d = r.shape
    ws = w.shape[0]
    tc = _fit(d, 256)

    def body(r_ref, w_ref, m_ref, v_ref, g_ref, d_ref, m2_ref, v2_ref, tall):
        tall[...] = pltpu.roll(_slot_sum(r_ref), lax.rem(ph - _index(_place()), ph), 0)
        g = tall[0:ws, :]
        g_ref[...] = g
        d_ref[...], m2_ref[...], v2_ref[...] = _adam_math(w_ref[...], g, m_ref[...], v_ref[...])

    blk = pl.BlockSpec((ws, tc), lambda j: (0, j))
    out = jax.ShapeDtypeStruct(w.shape, F32)
    return pl.pallas_call(
        body, name=name, grid=(d // tc,),
        in_specs=[pl.BlockSpec((r.shape[0], ph, tc), lambda j: (0, 0, j)), blk, blk, blk],
        out_specs=[blk] * 4, out_shape=[out] * 4,
        scratch_shapes=[pltpu.VMEM((ph, tc), F32)], compiler_params=_params("parallel"),
    )(r, w, m, v)


def _sum_slots(r, name, tr=128):
    _, rows, cols = r.shape
    tr = _rows(rows, tr)

    def body(r_ref, g_ref):
        g_ref[...] = _slot_sum(r_ref)

    return pl.pallas_call(
        body, name=name, grid=(rows // tr,),
        in_specs=[pl.BlockSpec((r.shape[0], tr, cols), lambda i: (0, i, 0))],
        out_specs=pl.BlockSpec((tr, cols), lambda i: (i, 0)),
        out_shape=jax.ShapeDtypeStruct((rows, cols), F32),
        compiler_params=_params("parallel"),
    )(r)


def _adamw(w, g, m, v, name, tr=256):
    rows, cols = w.shape
    tr = _rows(rows, tr)

    def body(w_ref, g_ref, m_ref, v_ref, d_ref, m2_ref, v2_ref):
        d_ref[...], m2_ref[...], v2_ref[...] = _adam_math(w_ref[...], g_ref[...], m_ref[...], v_ref[...])

    blk = pl.BlockSpec((tr, cols), lambda i: (i, 0))
    out = jax.ShapeDtypeStruct((rows, cols), F32)
    return pl.pallas_call(
        body, name=name, grid=(rows // tr,), in_specs=[blk] * 4, out_specs=[blk] * 3, out_shape=[out] * 3,
        compiler_params=_params("parallel"),
    )(w, g, m, v)


def _sum_adamw(r, w, m, v, name, tr=256):
    rows, cols = w.shape
    tr = _rows(rows, tr)

    def body(r_ref, w_ref, m_ref, v_ref, g_ref, d_ref, m2_ref, v2_ref):
        g = _slot_sum(r_ref)
        g_ref[...] = g
        d_ref[...], m2_ref[...], v2_ref[...] = _adam_math(w_ref[...], g, m_ref[...], v_ref[...])

    blk = pl.BlockSpec((tr, cols), lambda i: (i, 0))
    out = jax.ShapeDtypeStruct((rows, cols), F32)
    return pl.pallas_call(
        body, name=name, grid=(rows // tr,),
        in_specs=[pl.BlockSpec((r.shape[0], tr, cols), lambda i: (0, i, 0)), blk, blk, blk],
        out_specs=[blk] * 4, out_shape=[out] * 4,
        compiler_params=_params("parallel"),
    )(r, w, m, v)


def _pack(pieces, sizes):
    flat = [jnp.pad(p.reshape(-1).astype(F32), (0, s - p.size)) for p, s in zip(pieces, sizes)]
    total = sum(sizes)
    padded = -(-total // (16 * LANE)) * (16 * LANE)
    return jnp.pad(jnp.concatenate(flat), (0, padded - total)).reshape(-1, LANE)


def _unpack(packed, shapes, sizes):
    flat = packed.reshape(-1)
    out, off = [], 0
    for shp, s in zip(shapes, sizes):
        n = 1
        for k in shp:
            n *= k
        out.append(flat[off:off + n].reshape(shp))
        off += s
    return out


def _lanes(n):
    return -(-n // LANE) * LANE


WEIGHTS = ("w_in", "b_gates", "w_sc_conv", "mh_gain", "w_out", "ln1_g", "ln1_b", "w_up", "w_ffn_conv", "b_ffn_conv",
           "w_down", "ln2_g", "ln2_b")
BIG = ("w_in", "w_out", "w_up", "w_down")
SMALL = tuple(n for n in WEIGHTS if n not in BIG)


def kernel(x, w_in, b_gates, w_sc_conv, mh_gain, w_out, ln1_g, ln1_b, w_up, w_ffn_conv, b_ffn_conv, w_down, ln2_g, ln2_b, loss_target, m_w_in, m_b_gates, m_w_sc_conv, m_mh_gain, m_w_out, m_ln1_g, m_ln1_b, m_w_up, m_w_ffn_conv, m_b_ffn_conv, m_w_down, m_ln2_g, m_ln2_b, v_w_in, v_b_gates, v_w_sc_conv, v_mh_gain, v_w_out, v_ln1_g, v_ln1_b, v_w_up, v_w_ffn_conv, v_b_ffn_conv, v_w_down, v_ln2_g, v_ln2_b):
    w = dict(zip(WEIGHTS, (w_in, b_gates, w_sc_conv, mh_gain, w_out, ln1_g, ln1_b, w_up, w_ffn_conv, b_ffn_conv,
                           w_down, ln2_g, ln2_b)))
    m = dict(zip(WEIGHTS, (m_w_in, m_b_gates, m_w_sc_conv, m_mh_gain, m_w_out, m_ln1_g, m_ln1_b, m_w_up,
                           m_w_ffn_conv, m_b_ffn_conv, m_w_down, m_ln2_g, m_ln2_b)))
    v = dict(zip(WEIGHTS, (v_w_in, v_b_gates, v_w_sc_conv, v_mh_gain, v_w_out, v_ln1_g, v_ln1_b, v_w_up,
                           v_w_ffn_conv, v_b_ffn_conv, v_w_down, v_ln2_g, v_ln2_b)))
    me = _index(_place())
    d = x.shape[2]
    ws_in = w_in.shape[2]
    assert ws_in == IN_SLAB + 1 and N_DEV <= LANE, w_in.shape
    ninp = (N_DEV + 1) * IN_SLAB
    ws_sc, ws_fc = w_sc_conv.shape[2], w_ffn_conv.shape[2]

    wx = _Gathering({"w_out": w_out[0]}, {n: w[n][0] for n in ("w_up", "w_down")}, me)
    w_in_t = jnp.transpose(_behind(w_in[0], wx.token))
    w_in_main, w_in_tail = _shift_w_in(w_in_t)
    taps8 = lambda a: jnp.pad(a[0], ((0, 5), (0, 0)))
    at_once = ("w_in", "w_tail", "w_sc", "w_fc")
    wx.start_first(dict(zip(at_once, (w_in_main, w_in_tail, taps8(w_sc_conv), taps8(w_ffn_conv)))))
    x_b = _behind(x[0], wx.begin(None)).astype(BF16)
    m_in_t, v_in_t = (jnp.transpose(_behind(a[0], wx.begin(None))) for a in (m_w_in, v_w_in))
    token = wx.relay(x_b, m_in_t, v_in_t)
    for n in at_once:
        token = wx.forward(n, token)
    g_in, g_tail, g_sc, g_fc = (wx.get(n, token) for n in at_once)
    w_in_full = _carry_w_in(g_in, g_tail).reshape(ninp, d)
    w_sc_full = g_sc[:, :3].transpose(1, 0, 2).reshape(3, N_DEV * ws_sc)
    w_fc_full = g_fc[:, :3].transpose(1, 0, 2).reshape(3, N_DEV * ws_fc)

    xi, yi, ci = _place()
    names = ("loss",) + SMALL
    pieces = {}

    def gather_small(small, loss_t, after):
        pieces.update(small, loss=loss_t[0, :1])
        sizes = [_lanes(pieces[n].size) for n in names]
        (g_small,) = _all_gather([_behind(_pack([pieces[n] for n in names], sizes), after)], "gather_small")
        return _sum_slots(g_small, "sum_small", tr=g_small.shape[1])

    gx = _Reducing(jnp.reshape(ci, (1,)).astype(jnp.int32), 2 * xi + yi, gather_small)
    loss_t, grad_x, small, _ = _local_step(
        x[0], loss_target[0], w_in_full, b_gates, w_sc_full, mh_gain, None, ln1_g, ln1_b, None,
        w_fc_full, b_ffn_conv, None, ln2_g, ln2_b, gx=gx, wx=wx, x_b=x_b)

    grads, deltas, new_m, new_v = {}, {}, {}, {}
    for name in ("w_down", "w_up", "w_out"):
        grads[name], deltas[name], new_m[name], new_v[name] = _sum_adamw(
            gx.finish(name, gx.token), w[name][0], m[name][0], v[name][0], "adamw_" + name)

    summed = _unpack(gx.small_sum, [pieces[n].shape for n in names], [_lanes(pieces[n].size) for n in names])
    full = dict(zip(names, summed))
    full["w_sc_conv"] = lax.dynamic_slice(full["w_sc_conv"], (0, me * ws_sc), (3, ws_sc))
    full["w_ffn_conv"] = lax.dynamic_slice(full["w_ffn_conv"], (0, me * ws_fc), (3, ws_fc))
    for n in SMALL:
        grads[n] = full[n].reshape(w[n].shape)
    sizes = [_lanes(w[n].size) for n in SMALL]
    shapes = [w[n].shape for n in SMALL]
    packed = [_pack([t[n] for n in SMALL], sizes) for t in (w, grads, m, v)]
    small_out = _adamw(*packed, "adamw_small")
    for res, t in zip(small_out, (deltas, new_m, new_v)):
        t.update(zip(SMALL, _unpack(res, shapes, sizes)))

    done = sum(t[0:1, 0:1] for t in (grad_x, deltas["w_down"], deltas["w_up"], deltas["w_out"], small_out[0]))
    grads["w_in"], deltas["w_in"], new_m["w_in"], new_v["w_in"] = (
        jnp.transpose(a)[None] for a in _sum_adamw_shifted(gx.finish("w_in", done), w_in_t, m_in_t, v_in_t, "adamw_w_in"))

    big = lambda t: {n: (t[n].reshape(w[n].shape) if n in BIG else t[n]) for n in WEIGHTS}
    grads, deltas, new_m, new_v = big(grads), big(deltas), big(new_m), big(new_v)
    return (full["loss"].reshape(()), grad_x[None], *[grads[n] for n in WEIGHTS], *[deltas[n] for n in WEIGHTS],
            *[new_m[n] for n in WEIGHTS], *[new_v[n] for n in WEIGHTS])
```

```python
import functools

import jax
import jax.numpy as jnp
from jax import lax
from jax.experimental import pallas as pl
from jax.experimental.pallas import tpu as pltpu

F32 = jnp.float32
BF16 = jnp.bfloat16
MESH = pl.DeviceIdType.MESH

N_DEV = 8
NH = 4
CHUNK = 64
LN_EPS = 1e-5
HN_EPS = 1e-6
ALPHA = 2.0 ** 0.25
LANE = 128
IN_SLAB = 7 * LANE
IN_TAIL = 16
VMEM_LIMIT = 56 * 1024 * 1024
ADAM_LR, ADAM_B1, ADAM_B2, ADAM_EPS, ADAM_WD, ADAM_STEP = 0.001, 0.9, 0.999, 1e-08, 0.01, 10

_NN = (((1,), (0,)), ((), ()))
_NT = (((1,), (1,)), ((), ()))
_TN = (((0,), (0,)), ((), ()))


def _dot(a, b, dn=_NN):
    return lax.dot_general(a, b, dn, preferred_element_type=F32)


def _params(*sem):
    return pltpu.CompilerParams(dimension_semantics=sem if sem else None, vmem_limit_bytes=VMEM_LIMIT)


def _iota(shape, axis):
    return lax.broadcasted_iota(jnp.int32, shape, axis)


def _fit(n, want):
    if n <= want:
        return n
    t = want - want % LANE
    while n % t:
        t -= LANE
    return t


def _matmul(a, b, mode, out_dtype, name, tm=1024, tn=512, tk=1024, add=None, add_scale=1.0,
            a_blocked=False, b_blocked=False, o_width=None, after=None, n=None):
    a_parts = a if isinstance(a, tuple) else None
    b_parts = b if isinstance(b, tuple) else None
    if a_parts:
        a_blocked, (a_rows, wa), na = True, a[0].shape, len(a)
        kd, m = (a_rows, na * wa) if mode == "tn" else (na * wa, a_rows)
    elif a_blocked:
        na, a_rows, wa = a.shape
        kd, m = (a_rows, na * wa) if mode == "tn" else (na * wa, a_rows)
    elif mode == "tn":
        kd, m = a.shape
    else:
        m, kd = a.shape
    if b_parts:
        b_blocked, (rows, w), nb = True, b[0].shape, len(b)
    elif b_blocked:
        nb, rows, w = b.shape
    if b_blocked:
        n = rows if mode == "nt" else nb * w
        assert (nb * w if mode == "nt" else rows) == kd, (name, kd)
    else:
        n = n or (b.shape[0] if mode == "nt" else b.shape[1])
    tm, tn, tk = _fit(m, tm), _fit(n, tn), _fit(kd, tk)
    if a_blocked and mode == "tn":
        tm = _fit(wa, tm)
    if a_blocked and mode != "tn":
        tk = _fit(wa, tk)
    if b_blocked and mode != "nt":
        tn = _fit(w, tn)
    if b_blocked and mode == "nt":
        tk = _fit(w, tk)
    if o_width is not None:
        tn = _fit(o_width, tn)
    assert m % tm == 0 and n % tn == 0 and kd % tk == 0, (name, m, n, kd, tm, tn, tk)
    assert not (a_blocked and mode != "tn" and wa % tk) and not (b_blocked and mode == "nt" and w % tk), (name, tk)
    nk = kd // tk
    dn = {"nn": _NN, "nt": _NT, "tn": _TN}[mode]
    if a_blocked and mode == "tn":
        a_per = wa // tm
        a_spec = pl.BlockSpec((None, tk, tm), lambda i, j, k: (i // a_per, k, i % a_per))
    elif a_blocked:
        a_per = wa // tk
        a_spec = pl.BlockSpec((None, tm, tk), lambda i, j, k: (k // a_per, i, k % a_per))
    elif mode == "tn":
        a_spec = pl.BlockSpec((tk, tm), lambda i, j, k: (k, i))
    else:
        a_spec = pl.BlockSpec((tm, tk), lambda i, j, k: (i, k))
    if b_blocked and mode != "nt":
        per = w // tn
        b_spec = pl.BlockSpec((None, tk, tn), lambda i, j, k: (j // per, k, j % per))
    elif b_blocked:
        per = w // tk
        b_spec = pl.BlockSpec((None, tn, tk), lambda i, j, k: (k // per, j, k % per))
    elif mode == "nt":
        b_spec = pl.BlockSpec((tn, tk), lambda i, j, k: (j, k))
    else:
        b_spec = pl.BlockSpec((tk, tn), lambda i, j, k: (k, j))
    if o_width is None:
        o_spec = pl.BlockSpec((tm, tn), lambda i, j, k: (i, j))
        o_shape = (m, n)
    else:
        oper = o_width // tn
        o_spec = pl.BlockSpec((None, tm, tn), lambda i, j, k: (j // oper, i, j % oper))
        o_shape = (n // o_width, m, o_width)
    a_list, a_specs = [a], [a_spec]
    if a_parts:
        hold = lambda x, s: jnp.clip(x - s * a_per, 0, a_per - 1)
        a_list = list(a_parts)
        a_specs = [(pl.BlockSpec((tk, tm), lambda i, j, k, s=s: (k, hold(i, s))) if mode == "tn"
                    else pl.BlockSpec((tm, tk), lambda i, j, k, s=s: (i, hold(k, s)))) for s in range(na)]
    b_list, b_specs = [b], [b_spec]
    if b_parts:
        hold_b = lambda x, s: jnp.clip(x - s * per, 0, per - 1)
        b_list = list(b_parts)
        b_specs = [(pl.BlockSpec((tn, tk), lambda i, j, k, s=s: (j, hold_b(k, s))) if mode == "nt"
                    else pl.BlockSpec((tk, tn), lambda i, j, k, s=s: (k, hold_b(j, s)))) for s in range(nb)]
    n_a, n_b = len(a_list), len(b_list)
    has_add = add is not None
    n_in = n_a + n_b + has_add + (after is not None)
    in_place = nk > 1 and out_dtype == F32

    def body(*refs):
        add_ref = refs[n_a + n_b] if has_add else None
        o_ref = refs[n_in]
        i, j, k = pl.program_id(0), pl.program_id(1), pl.program_id(2)

        def finish(r):
            if has_add:
                r = r + add_scale * add_ref[...]
            o_ref[...] = r.astype(out_dtype)

        def step(a_ref, b_ref):
            if nk == 1:
                finish(_dot(a_ref[...], b_ref[...], dn))
                return
            acc = o_ref if in_place else refs[-1]

            @pl.when(k == 0)
            def _():
                acc[...] = _dot(a_ref[...], b_ref[...], dn)

            @pl.when(k > 0)
            def _():
                acc[...] += _dot(a_ref[...], b_ref[...], dn)

        if n_a == 1 and n_b == 1:
            step(refs[0], refs[1])
        else:
            slab_a = ((i if mode == "tn" else k) // a_per) if n_a > 1 else 0
            slab_b = ((k if mode == "nt" else j) // per) if n_b > 1 else 0
            for sa in range(n_a):
                for sb in range(n_b):
                    pl.when((slab_a == sa) & (slab_b == sb))(functools.partial(step, refs[sa], refs[n_a + sb]))
        if nk > 1 and not (in_place and not has_add):
            @pl.when(k == nk - 1)
            def _():
                finish((o_ref if in_place else refs[-1])[...])

    in_specs = a_specs + b_specs + ([pl.BlockSpec((tm, tn), lambda i, j, k: (i, j))] if has_add else [])
    args = (*a_list, *b_list) + ((add,) if has_add else ())
    if after is not None:
        in_specs.append(pl.BlockSpec(memory_space=pl.ANY))
        args += (after,)
    return pl.pallas_call(
        body, name=name, grid=(m // tm, n // tn, nk),
        in_specs=in_specs, out_specs=o_spec,
        out_shape=jax.ShapeDtypeStruct(o_shape, out_dtype),
        scratch_shapes=[pltpu.VMEM((tm, tn), F32)] if nk > 1 and not in_place else [],
        compiler_params=_params("parallel", "parallel", "arbitrary"),
    )(*args)


def _shift_down(u, s):
    return jnp.where(_iota(u.shape, 0) >= s, pltpu.roll(u, s, 0), 0.0)


def _shift_up(u, s):
    t = u.shape[0]
    return jnp.where(_iota(u.shape, 0) < t - s, pltpu.roll(u, t - s, 0), 0.0)


SLAB = 8


def _rolled(u):
    return pltpu.roll(u, 2, 0), pltpu.roll(u, 1, 0)


def _conv(u, w, rolled=None):
    u2, u1 = _rolled(u) if rolled is None else rolled
    raw = w[0:1] * u2 + w[1:2] * u1 + w[2:3] * u
    head = u[0:SLAB]
    mended = w[0:1] * _shift_down(head, 2) + w[1:2] * _shift_down(head, 1) + w[2:3] * head
    return jnp.concatenate([mended, raw[SLAB:]], axis=0)


def _conv_t(dy, w):
    t = dy.shape[0]
    raw = w[2:3] * dy + w[1:2] * pltpu.roll(dy, t - 1, 0) + w[0:1] * pltpu.roll(dy, t - 2, 0)
    tail = dy[t - SLAB:]
    mended = w[2:3] * tail + w[1:2] * _shift_up(tail, 1) + w[0:1] * _shift_up(tail, 2)
    return jnp.concatenate([raw[:t - SLAB], mended], axis=0)


def _conv_dw(dy, u, rolled=None):
    t = dy.shape[0]
    u2, u1 = _rolled(u) if rolled is None else rolled
    head, tail = dy[0:SLAB], u[t - SLAB:]
    r = _iota(head.shape, 0)
    wrap2 = jnp.sum(jnp.where(r < 2, head * pltpu.roll(tail, 2, 0), 0.0), axis=0, keepdims=True)
    wrap1 = jnp.sum(jnp.where(r < 1, head * pltpu.roll(tail, 1, 0), 0.0), axis=0, keepdims=True)
    d0 = jnp.sum(dy * u2, axis=0, keepdims=True) - wrap2
    d1 = jnp.sum(dy * u1, axis=0, keepdims=True) - wrap1
    d2 = jnp.sum(dy * u, axis=0, keepdims=True)
    r3 = _iota((3, dy.shape[1]), 0)
    return jnp.where(r3 == 0, d0, jnp.where(r3 == 1, d1, d2))


def _sigmoid(x):
    return 0.5 * jnp.tanh(0.5 * x) + 0.5


def _sconv_fwd(proj, w_sc, t, wc):
    nb = wc // LANE

    def body(cb_ref, cc_ref, ch_ref, w_ref, y_ref):
        u = cc_ref[...] * ch_ref[...]
        y_ref[...] = (cb_ref[...] * _conv(u, w_ref[...])).astype(BF16)

    col = lambda off: pl.BlockSpec((t, LANE), lambda j: (0, j + off))
    return pl.pallas_call(
        body, name="sconv_fwd", grid=(nb,),
        in_specs=[col(0), col(nb), col(2 * nb), pl.BlockSpec((3, LANE), lambda j: (0, j))],
        out_specs=pl.BlockSpec((None, t, LANE), lambda j: (0, 0, j)),
        out_shape=jax.ShapeDtypeStruct((2, t, wc), BF16),
        compiler_params=_params("parallel"),
    )(proj, proj, proj, w_sc)


def _sconv_bwd(dy, proj, w_sc, t, wc):
    nb = wc // LANE

    def body(dy_ref, cb_ref, cc_ref, ch_ref, w_ref, dcb_ref, dcc_ref, dch_ref, dw_ref):
        cc, ch, w, d = cc_ref[...], ch_ref[...], w_ref[...], dy_ref[...]
        u = cc * ch
        ru = _rolled(u)
        dcb_ref[...] = (d * _conv(u, w, ru)).astype(BF16)
        dcu = d * cb_ref[...]
        dw_ref[...] = _conv_dw(dcu, u, ru)
        du = _conv_t(dcu, w)
        dcc_ref[...] = (du * ch).astype(BF16)
        dch_ref[...] = (du * cc).astype(BF16)

    col = lambda off: pl.BlockSpec((t, LANE), lambda j: (0, j + off))
    act = jax.ShapeDtypeStruct((t, wc), BF16)
    return pl.pallas_call(
        body, name="sconv_bwd", grid=(nb,),
        in_specs=[col(0), col(0), col(nb), col(2 * nb), pl.BlockSpec((3, LANE), lambda j: (0, j))],
        out_specs=[col(0), col(0), col(0), pl.BlockSpec((3, LANE), lambda j: (0, j))],
        out_shape=[act, act, act, jax.ShapeDtypeStruct((3, wc), F32)],
        compiler_params=_params("parallel"),
    )(dy, proj, proj, proj, w_sc)


def _gates_prep(proj, bias_tile, t, gate_tile):
    def body(g_ref, b_ref, o_ref):
        g = g_ref[...] + b_ref[...]
        lane = _iota(g.shape, 1)
        is_f = (lane >= NH) & (lane < 2 * NH)
        lf = jnp.minimum(g, 0.0) - jnp.log(1.0 + jnp.exp(-jnp.abs(g)))
        c = jnp.where(is_f, lf, 0.0)
        r = _iota(g.shape, 0) % CHUNK
        s = 1
        while s < CHUNK:
            c = c + jnp.where(r >= s, pltpu.roll(c, s, 0), 0.0)
            s *= 2
        o_ref[...] = jnp.where(is_f, c, jnp.where(lane < NH, g, 0.0))

    return pl.pallas_call(
        body, name="gates_prep", grid=(1,),
        in_specs=[pl.BlockSpec((t, LANE), lambda i: (0, gate_tile)), pl.BlockSpec((1, LANE), lambda i: (0, 0))],
        out_specs=pl.BlockSpec((t, LANE), lambda i: (0, 0)),
        out_shape=jax.ShapeDtypeStruct((t, LANE), F32),
        compiler_params=_params("arbitrary"),
    )(proj, bias_tile)


def _gates_bwd(dgate, proj, bias_tile, t, gate_tile):
    def body(dg_ref, g_ref, b_ref, o_ref, s_ref):
        g = g_ref[...] + b_ref[...]
        lane = _iota(g.shape, 1)
        r = _iota(g.shape, 0) % CHUNK
        dsig = 1.0 - _sigmoid(g)
        out = jnp.zeros(g.shape, F32)
        for h in range(NH):
            d = dg_ref[h]
            c = d
            s = 1
            while s < CHUNK:
                c = c + jnp.where(r + s < CHUNK, pltpu.roll(c, t - s, 0), 0.0)
                s *= 2
            di = jnp.broadcast_to(d[:, 0:1], g.shape)
            db = jnp.broadcast_to(c[:, 1:2], g.shape)
            out = out + jnp.where(lane == h, di, 0.0) + jnp.where(lane == NH + h, db * dsig, 0.0)
        o_ref[...] = out.astype(BF16)
        s_ref[...] = jnp.sum(out, axis=0, keepdims=True)

    return pl.pallas_call(
        body, name="gates_bwd", grid=(1,),
        in_specs=[pl.BlockSpec((NH, t, LANE), lambda i: (0, 0, 0)),
                  pl.BlockSpec((t, LANE), lambda i: (0, gate_tile)), pl.BlockSpec((1, LANE), lambda i: (0, 0))],
        out_specs=[pl.BlockSpec((t, LANE), lambda i: (0, 0)), pl.BlockSpec((1, LANE), lambda i: (0, 0))],
        out_shape=[jax.ShapeDtypeStruct((t, LANE), BF16), jax.ShapeDtypeStruct((1, LANE), F32)],
        compiler_params=_params("arbitrary"),
    )(dgate, proj, bias_tile)


def _in_turn(heads):
    while heads:
        heads = [g for g in heads if next(g, heads) is not heads]


def _chunk_gates(gc, gr, h, mprev):
    L = CHUNK
    icol, bcol = gc[:, h:h + 1], gc[:, h + NH:h + NH + 1]
    irow, brow = gr[h:h + 1, :], gr[h + NH:h + NH + 1, :]
    tri = _iota((L, L), 0) >= _iota((L, L), 1)
    log_d = jnp.where(tri, bcol - brow + irow, -jnp.inf)
    inter = bcol + mprev
    mt = jnp.maximum(inter, jnp.max(log_d, axis=1, keepdims=True))
    dw = jnp.exp(log_d - mt)
    iw = jnp.exp(inter - mt)
    g = brow[:, L - 1:L]
    wlog_col = g - bcol + icol
    wlog_row = g - brow + irow
    mnew = jnp.maximum(g + mprev, jnp.max(wlog_row, axis=1, keepdims=True))
    wcol = jnp.exp(wlog_col - mnew)
    decay = jnp.exp(g + mprev - mnew)
    return dw, iw, mt, wcol, decay, mnew


def _mlstm_fwd(proj, gcol, grow, t, wc, dh):
    nc = t // CHUNK
    wm = NH * dh
    assert wc == wm, (wc, wm)
    qoff = 3 * wc // wm
    scale = dh ** -0.5

    def body(q_ref, k_ref, v_ref, gc_ref, gr_ref, h_ref, cs_ref, ns_ref, c_s, n_s, m_s):
        @pl.when(pl.program_id(0) == 0)
        def _():
            c_s[...] = jnp.zeros_like(c_s)
            n_s[...] = jnp.zeros_like(n_s)
            m_s[...] = jnp.zeros_like(m_s)

        gc, gr = gc_ref[...], gr_ref[0]
        done = [None] * NH

        def head(h):
            cols = slice(h * dh, (h + 1) * dh)
            mprev = m_s[h, 0:1, 0:1]
            cprev = c_s[h]
            n8 = n_s[h]
            nprev = n8[0:1]
            qs = q_ref[:, cols] * scale
            k = k_ref[:, cols]
            qs_b, k_b, v_b = qs.astype(BF16), k.astype(BF16), v_ref[:, cols].astype(BF16)
            qk = _dot(qs_b, k_b, _NT)
            yield
            q_c = _dot(qs_b, cprev.astype(BF16))
            yield
            dw, iw, mt, wcol, decay, mnew = _chunk_gates(gc, gr, h, mprev)
            yield
            s = qk * dw
            wk = wcol * k
            num = _dot(s.astype(BF16), v_b) + iw * q_c
            yield
            c_new = decay * cprev + _dot(wk.astype(BF16), v_b, _TN)
            yield
            den = jnp.sum(s, axis=1, keepdims=True) + iw * jnp.sum(qs * nprev, axis=1, keepdims=True)
            done[h] = (cprev, jnp.where(_iota(n8.shape, 0) == 1, mprev, n8),
                       num / jnp.maximum(jnp.abs(den), jnp.exp(-mt)), c_new,
                       decay * n8 + jnp.sum(wk, axis=0, keepdims=True), mnew)

        _in_turn([head(h) for h in range(NH)])
        for h, (c_old, n_old, h_out, c_new, n_new, m_new) in enumerate(done):
            cs_ref[h] = c_old
            ns_ref[h] = n_old
            h_ref[:, h * dh:(h + 1) * dh] = h_out
            c_s[h] = c_new
            n_s[h] = n_new
            m_s[h] = jnp.broadcast_to(m_new, m_s.shape[1:])

    grp = lambda off: pl.BlockSpec((CHUNK, wm), lambda c: (c, qoff + off))
    return pl.pallas_call(
        body, name="mlstm_fwd", grid=(nc,),
        in_specs=[grp(0), grp(1), grp(2),
                  pl.BlockSpec((CHUNK, LANE), lambda c: (c, 0)),
                  pl.BlockSpec((1, 8, CHUNK), lambda c: (c, 0, 0))],
        out_specs=[pl.BlockSpec((CHUNK, wm), lambda c: (c, 0)),
                   pl.BlockSpec((NH, None, dh, dh), lambda c: (0, c, 0, 0)),
                   pl.BlockSpec((NH, None, 8, dh), lambda c: (0, c, 0, 0))],
        out_shape=[jax.ShapeDtypeStruct((t, wm), F32),
                   jax.ShapeDtypeStruct((NH, nc, dh, dh), F32),
                   jax.ShapeDtypeStruct((NH, nc, 8, dh), F32)],
        scratch_shapes=[pltpu.VMEM((NH, dh, dh), F32), pltpu.VMEM((NH, 8, dh), F32), pltpu.VMEM((NH, 8, LANE), F32)],
        compiler_params=_params("arbitrary"),
    )(proj, proj, proj, gcol, grow)


def _mlstm_bwd(proj, gcol, grow, hval, dh_in, cs, ns, t, wc, dh):
    nc = t // CHUNK
    wm = NH * dh
    assert wc == wm, (wc, wm)
    qoff = 3 * wc // wm
    scale = dh ** -0.5
    L = CHUNK

    def body(q_ref, k_ref, v_ref, gc_ref, gr_ref, h_ref, dh_ref, cs_ref, ns_ref,
             dq_ref, dk_ref, dv_ref, dg_ref, dc_s, dn_s):
        @pl.when(pl.program_id(0) == 0)
        def _():
            dc_s[...] = jnp.zeros_like(dc_s)
            dn_s[...] = jnp.zeros_like(dn_s)

        gc, gr = gc_ref[...], gr_ref[0]
        eye = _iota((L, L), 0) == _iota((L, L), 1)
        lane = _iota((L, LANE), 1)
        last = _iota((L, 1), 0) == L - 1
        done = [None] * NH

        def head(h):
            cols = slice(h * dh, (h + 1) * dh)
            ns8 = ns_ref[h]
            nprev = ns8[0:1]
            mprev = ns8[1:2, 0:1]
            cprev = cs_ref[h]
            dcn = dc_s[h]
            dn8 = dn_s[h]
            dnn = dn8[0:1]

            qs = q_ref[:, cols] * scale
            k = k_ref[:, cols]
            qs_b, k_b, v_b = qs.astype(BF16), k.astype(BF16), v_ref[:, cols].astype(BF16)
            qk = _dot(qs_b, k_b, _NT)
            yield
            dw, iw, mt, wcol, decay, _ = _chunk_gates(gc, gr, h, mprev)
            yield
            s = qk * dw
            den = jnp.sum(s, axis=1, keepdims=True) + iw * jnp.sum(qs * nprev, axis=1, keepdims=True)
            emt = jnp.exp(-mt)
            r = 1.0 / jnp.maximum(jnp.abs(den), emt)
            dout = dh_ref[:, cols]
            dnum = dout * r
            dden = (-jnp.sum(dout * h_ref[:, cols], axis=1, keepdims=True) * r
                    * jnp.where(jnp.abs(den) > emt, jnp.sign(den), 0.0))
            dnum_b = dnum.astype(BF16)
            cprev_b = cprev.astype(BF16)
            dcn_b = dcn.astype(BF16)
            yield

            g_raw = _dot(dnum_b, v_b, _NT)
            yield
            q_inter = _dot(dnum_b, cprev_b, _NT)
            yield
            k_raw = _dot(v_b, dcn_b, _NT)
            yield
            gd = (g_raw + dden) * dw
            gd_b = gd.astype(BF16)
            dqs_inter = iw * (q_inter + dden * nprev)
            dk_inter = wcol * (k_raw + dnn)
            wk = wcol * k
            iq = iw * qs
            dqs = _dot(gd_b, k_b) + dqs_inter
            yield
            dk = _dot(gd_b, qs_b, _TN) + dk_inter
            yield
            dv = _dot(s.astype(BF16), dnum_b, _TN) + _dot(wk.astype(BF16), dcn_b)
            yield
            dc_new = decay * dcn + _dot(iq.astype(BF16), dnum_b, _TN)
            yield

            e = gd * qk
            e_cols = jnp.sum(jnp.where(eye, jnp.sum(e, axis=0, keepdims=True), 0.0), axis=1, keepdims=True)
            yield
            k_inter = jnp.sum(k * dk_inter, axis=1, keepdims=True)
            rq = jnp.sum(e, axis=1, keepdims=True) + jnp.sum(qs * dqs_inter, axis=1, keepdims=True)
            rk = e_cols + k_inter
            hsum = jnp.sum(k_inter, axis=0, keepdims=True)
            jdec = decay * (jnp.sum(jnp.sum(dcn * cprev, axis=1, keepdims=True), axis=0, keepdims=True)
                            + jnp.sum(dnn * nprev, axis=1, keepdims=True))
            db = rq - rk + jnp.where(last, hsum + jdec, 0.0)
            done[h] = (jnp.where(lane == 0, rk, jnp.where(lane == 1, db, 0.0)),
                       (dqs * scale).astype(BF16), dk.astype(BF16), dv.astype(BF16), dc_new,
                       decay * dn8 + jnp.sum(iq * dden, axis=0, keepdims=True))

        _in_turn([head(h) for h in range(NH)])
        for h, (dgate, dq, dk, dv, dc_new, dn_new) in enumerate(done):
            cols = slice(h * dh, (h + 1) * dh)
            dg_ref[h] = dgate
            dq_ref[:, cols] = dq
            dk_ref[:, cols] = dk
            dv_ref[:, cols] = dv
            dc_s[h] = dc_new
            dn_s[h] = dn_new

    rc = lambda c: nc - 1 - c
    grp = lambda off: pl.BlockSpec((L, wm), lambda c: (rc(c), qoff + off))
    hm = pl.BlockSpec((L, wm), lambda c: (rc(c), 0))
    act = jax.ShapeDtypeStruct((t, wm), BF16)
    return pl.pallas_call(
        body, name="mlstm_bwd", grid=(nc,),
        in_specs=[grp(0), grp(1), grp(2),
                  pl.BlockSpec((L, LANE), lambda c: (rc(c), 0)),
                  pl.BlockSpec((1, 8, L), lambda c: (rc(c), 0, 0)),
                  hm, hm,
                  pl.BlockSpec((NH, None, dh, dh), lambda c: (0, rc(c), 0, 0)),
                  pl.BlockSpec((NH, None, 8, dh), lambda c: (0, rc(c), 0, 0))],
        out_specs=[hm, hm, hm, pl.BlockSpec((NH, L, LANE), lambda c: (0, rc(c), 0))],
        out_shape=[act, act, act, jax.ShapeDtypeStruct((NH, t, LANE), F32)],
        scratch_shapes=[pltpu.VMEM((NH, dh, dh), F32), pltpu.VMEM((NH, 8, dh), F32)],
        compiler_params=_params("arbitrary"),
    )(proj, proj, proj, gcol, grow, hval, dh_in, cs, ns)


def _head_norm(hv):
    mu = jnp.mean(hv, axis=1, keepdims=True)
    hc = hv - mu
    rstd = lax.rsqrt(jnp.mean(hc * hc, axis=1, keepdims=True) + HN_EPS)
    return hc * rstd, rstd


def _hnorm_fwd(hval, proj, gain, y, t, wc, dh, tr=512):
    ooff = 3 * wc // dh + 3 * NH
    tr = min(tr, t)

    def body(h_ref, o_ref, g_ref, y_in, y_ref):
        hhat, _ = _head_norm(h_ref[...])
        y_ref[...] = (_sigmoid(o_ref[...]) * hhat * g_ref[...]).astype(BF16)

    return pl.pallas_call(
        body, name="hnorm_fwd", grid=(t // tr, NH),
        in_specs=[pl.BlockSpec((tr, dh), lambda i, h: (i, h)),
                  pl.BlockSpec((tr, dh), lambda i, h: (i, ooff + h)),
                  pl.BlockSpec((1, dh), lambda i, h: (0, h)),
                  pl.BlockSpec(memory_space=pl.ANY)],
        out_specs=pl.BlockSpec((None, tr, dh), lambda i, h: (1, i, h)),
        out_shape=jax.ShapeDtypeStruct(y.shape, BF16),
        input_output_aliases={3: 0},
        compiler_params=_params("parallel", "parallel"),
    )(hval, proj, gain, y)


def _hnorm_bwd(dy, hval, proj, gain, t, wc, dh, tr=512):
    ooff = 3 * wc // dh + 3 * NH
    tr = min(tr, t)
    yoff = wc // dh

    def body(dy_ref, h_ref, o_ref, g_ref, do_ref, dh_ref, dg_ref):
        i = pl.program_id(1)
        hhat, rstd = _head_norm(h_ref[...])
        gain_v = g_ref[...]
        sig = _sigmoid(o_ref[...])
        d = dy_ref[...]
        do_ref[...] = (d * hhat * gain_v * sig * (1.0 - sig)).astype(BF16)
        dhn = d * sig
        part = jnp.sum(dhn * hhat, axis=0, keepdims=True)

        @pl.when(i == 0)
        def _():
            dg_ref[...] = part

        @pl.when(i > 0)
        def _():
            dg_ref[...] += part

        dhat = dhn * gain_v
        dh_ref[...] = rstd * (dhat - jnp.mean(dhat, axis=1, keepdims=True)
                              - hhat * jnp.mean(dhat * hhat, axis=1, keepdims=True))

    blk = lambda off: pl.BlockSpec((tr, dh), lambda h, i: (i, off + h))
    return pl.pallas_call(
        body, name="hnorm_bwd", grid=(NH, t // tr),
        in_specs=[blk(yoff), blk(0), blk(ooff), pl.BlockSpec((1, dh), lambda h, i: (0, h))],
        out_specs=[blk(0), blk(0), pl.BlockSpec((1, dh), lambda h, i: (0, h))],
        out_shape=[jax.ShapeDtypeStruct((t, NH * dh), BF16), jax.ShapeDtypeStruct((t, NH * dh), F32),
                   jax.ShapeDtypeStruct((1, NH * dh), F32)],
        compiler_params=_params("parallel", "arbitrary"),
    )(dy, hval, proj, gain)


def _ln_stats(z):
    mu = jnp.mean(z, axis=1, keepdims=True)
    zc = z - mu
    rstd = lax.rsqrt(jnp.mean(zc * zc, axis=1, keepdims=True) + LN_EPS)
    return zc * rstd, rstd


def _ln_bwd(dy, xhat, rstd, g):
    dxh = dy * g
    return rstd * (dxh - jnp.mean(dxh, axis=1, keepdims=True) - xhat * jnp.mean(dxh * xhat, axis=1, keepdims=True))


def _accum(ref, i, part):
    @pl.when(i == 0)
    def _():
        ref[...] = part

    @pl.when(i > 0)
    def _():
        ref[...] += part


def _ln1_fwd(x, mix, g, b, tr=256):
    t, d = x.shape

    def body(x_ref, m_ref, g_ref, b_ref, xh_ref, rs_ref, xb_ref):
        xhat, rstd = _ln_stats(ALPHA * x_ref[...] + m_ref[...])
        xh_ref[...] = xhat
        rs_ref[...] = rstd
        xb_ref[...] = (xhat * g_ref[...] + b_ref[...]).astype(BF16)

    row = pl.BlockSpec((tr, d), lambda i: (i, 0))
    vec = pl.BlockSpec((1, d), lambda i: (0, 0))
    return pl.pallas_call(
        body, name="ln1_fwd", grid=(t // tr,),
        in_specs=[row, row, vec, vec],
        out_specs=[row, pl.BlockSpec((tr, 1), lambda i: (i, 0)), row],
        out_shape=[jax.ShapeDtypeStruct((t, d), F32), jax.ShapeDtypeStruct((t, 1), F32),
                   jax.ShapeDtypeStruct((t, d), BF16)],
        compiler_params=_params("parallel"),
    )(x, mix, g, b)


def _ln2_loss(xhat1, g1, b1, ff, target, g2, b2, tr=256):
    t, d = ff.shape

    def body(xh_ref, g1_ref, b1_ref, f_ref, t_ref, g_ref, b_ref, dz_ref, dzb_ref, dg_ref, db_ref, l_ref):
        i = pl.program_id(0)
        x1 = xh_ref[...] * g1_ref[...] + b1_ref[...]
        xhat, rstd = _ln_stats(ALPHA * x1 + f_ref[...])
        gv = g_ref[...]
        e = xhat * gv + b_ref[...] - t_ref[...]
        lsum = jnp.sum(jnp.sum(e * e, axis=1, keepdims=True), axis=0, keepdims=True) * (0.5 / d)
        dy = e * (1.0 / d)
        _accum(dg_ref, i, jnp.sum(dy * xhat, axis=0, keepdims=True))
        _accum(db_ref, i, jnp.sum(dy, axis=0, keepdims=True))
        _accum(l_ref, i, jnp.broadcast_to(lsum, l_ref.shape))
        dz = _ln_bwd(dy, xhat, rstd, gv)
        dz_ref[...] = dz
        dzb_ref[...] = dz.astype(BF16)

    row = pl.BlockSpec((tr, d), lambda i: (i, 0))
    vec = pl.BlockSpec((1, d), lambda i: (0, 0))
    return pl.pallas_call(
        body, name="ln2_loss", grid=(t // tr,),
        in_specs=[row, vec, vec, row, row, vec, vec],
        out_specs=[row, row, vec, vec, pl.BlockSpec((8, LANE), lambda i: (0, 0))],
        out_shape=[jax.ShapeDtypeStruct((t, d), F32), jax.ShapeDtypeStruct((t, d), BF16),
                   jax.ShapeDtypeStruct((1, d), F32), jax.ShapeDtypeStruct((1, d), F32),
                   jax.ShapeDtypeStruct((8, LANE), F32)],
        compiler_params=_params("arbitrary"),
    )(xhat1, g1, b1, ff, target, g2, b2)


def _ln1_bwd(dz2, dffn, xhat1, rstd1, g1, tr=256):
    t, d = dz2.shape

    def body(a_ref, f_ref, xh_ref, rs_ref, g_ref, dz_ref, dzb_ref, dg_ref, db_ref):
        i = pl.program_id(0)
        dy = ALPHA * a_ref[...] + f_ref[...]
        xhat = xh_ref[...]
        _accum(dg_ref, i, jnp.sum(dy * xhat, axis=0, keepdims=True))
        _accum(db_ref, i, jnp.sum(dy, axis=0, keepdims=True))
        dz = _ln_bwd(dy, xhat, rs_ref[...], g_ref[...])
        dz_ref[...] = dz
        dzb_ref[...] = dz.astype(BF16)

    row = pl.BlockSpec((tr, d), lambda i: (i, 0))
    vec = pl.BlockSpec((1, d), lambda i: (0, 0))
    return pl.pallas_call(
        body, name="ln1_bwd", grid=(t // tr,),
        in_specs=[row, row, row, pl.BlockSpec((tr, 1), lambda i: (i, 0)), vec],
        out_specs=[row, row, vec, vec],
        out_shape=[jax.ShapeDtypeStruct((t, d), F32), jax.ShapeDtypeStruct((t, d), BF16),
                   jax.ShapeDtypeStruct((1, d), F32), jax.ShapeDtypeStruct((1, d), F32)],
        compiler_params=_params("arbitrary"),
    )(dz2, dffn, xhat1, rstd1, g1)


def _ffn_act_fwd(hid0, w_fc, b_fc, t, dff):
    nb = dff // LANE

    def body(hv_ref, hg_ref, wv_ref, wg_ref, bv_ref, bg_ref, a_ref):
        val = _conv(hv_ref[...], wv_ref[...]) + bv_ref[...]
        gate = _conv(hg_ref[...], wg_ref[...]) + bg_ref[...]
        a_ref[...] = (gate * _sigmoid(gate) * val).astype(BF16)

    col = lambda off: pl.BlockSpec((t, LANE), lambda j: (0, j + off))
    w3 = lambda off: pl.BlockSpec((3, LANE), lambda j: (0, j + off))
    w1 = lambda off: pl.BlockSpec((1, LANE), lambda j: (0, j + off))
    return pl.pallas_call(
        body, name="ffn_act_fwd", grid=(nb,),
        in_specs=[col(0), col(nb), w3(0), w3(nb), w1(0), w1(nb)],
        out_specs=col(0),
        out_shape=jax.ShapeDtypeStruct((t, dff), BF16),
        compiler_params=_params("parallel"),
    )(hid0, hid0, w_fc, w_fc, b_fc, b_fc)


def _ffn_act_bwd(da, hid0, w_fc, b_fc, t, dff):
    nb = dff // LANE

    def body(da_ref, hv_ref, hg_ref, wv_ref, wg_ref, bv_ref, bg_ref,
             dhv_ref, dhg_ref, dwv_ref, dwg_ref, dbv_ref, dbg_ref):
        hv, hg, wv, wg = hv_ref[...], hg_ref[...], wv_ref[...], wg_ref[...]
        rv, rg = _rolled(hv), _rolled(hg)
        val = _conv(hv, wv, rv) + bv_ref[...]
        gate = _conv(hg, wg, rg) + bg_ref[...]
        sig = _sigmoid(gate)
        d = da_ref[...]
        dsig = d * sig
        dval = dsig * gate
        dgate = dsig * val * (1.0 + gate * (1.0 - sig))
        dhv_ref[...] = _conv_t(dval, wv).astype(BF16)
        dhg_ref[...] = _conv_t(dgate, wg).astype(BF16)
        dwv_ref[...] = _conv_dw(dval, hv, rv)
        dwg_ref[...] = _conv_dw(dgate, hg, rg)
        dbv_ref[...] = jnp.sum(dval, axis=0, keepdims=True)
        dbg_ref[...] = jnp.sum(dgate, axis=0, keepdims=True)

    col = lambda off: pl.BlockSpec((t, LANE), lambda j: (0, j + off))
    w3 = lambda off: pl.BlockSpec((3, LANE), lambda j: (0, j + off))
    w1 = lambda off: pl.BlockSpec((1, LANE), lambda j: (0, j + off))
    s3 = jax.ShapeDtypeStruct((3, dff), F32)
    s1 = jax.ShapeDtypeStruct((1, dff), F32)
    return pl.pallas_call(
        body, name="ffn_act_bwd", grid=(nb,),
        in_specs=[col(0), col(0), col(nb), w3(0), w3(nb), w1(0), w1(nb)],
        out_specs=[col(0), col(0), w3(0), w3(0), w1(0), w1(0)],
        out_shape=[jax.ShapeDtypeStruct((t, dff), BF16)] * 2 + [s3, s3, s1, s1],
        compiler_params=_params("parallel"),
    )(da, hid0, hid0, w_fc, w_fc, b_fc, b_fc)


class _Ready:
    def __init__(self, **weights):
        self.weights = weights

    def begin(self, after):
        return None

    def forward(self, name, after):
        return None

    def get(self, name, after):
        return self.weights[name]


class _Kept:
    def __init__(self):
        self.grads = {}

    def start(self, name, grad):
        self.grads[name] = grad
        return None

    def relay(self, name, after):
        return None

    def meanwhile(self, small, loss, after):
        return None


def _behind(a, token):
    return a if token is None else a + token[0:1, 0:1].reshape((1,) * a.ndim)


def _local_step(x, target, w_in, b_gates, w_sc, gain, w_out, ln1_g, ln1_b, w_up, w_fc, b_fc, w_down, ln2_g, ln2_b,
                gx=None, wx=None, x_b=None):
    t, d = x.shape
    wc = d // 2
    dh = (d - wc) // NH
    wm = NH * dh
    dff = w_fc.shape[1] // 2
    if wx is None:
        wx = _Ready(w_out=w_out, w_up=w_up, w_down=w_down)
    ninp = 3 * wc + 4 * wm + LANE
    nin = 3 * wc + 4 * wm
    gate_tile = nin // LANE
    nc = t // CHUNK
    bias_tile = jnp.pad(b_gates, ((0, 0), (0, LANE - 2 * NH)))

    if x_b is None:
        x_b = x.astype(BF16)
    proj = _matmul(x_b, w_in, "nt", F32, "proj", tm=512, tn=2432, tk=d, n=ninp, after=wx.begin(w_in))
    y = _sconv_fwd(proj, w_sc, t, wc)
    gcol = _gates_prep(proj, bias_tile, t, gate_tile)
    grow = gcol[:, :8].T.reshape(8, nc, CHUNK).transpose(1, 0, 2)
    hval, cs, ns = _mlstm_fwd(proj, gcol, grow, t, wc, dh)
    y = _hnorm_fwd(hval, proj, gain, y, t, wc, dh)
    tok = wx.forward("w_out", y)
    w_out = wx.get("w_out", tok)
    mix = _matmul(y, w_out, "nn", F32, "out_proj", tm=512, tn=1024, tk=wc, a_blocked=True, after=tok)
    xhat1, rstd1, x1_b = _ln1_fwd(x, mix, _behind(ln1_g, wx.forward("w_up", mix)), ln1_b)
    w_up = wx.get("w_up", x1_b)
    wsl = w_up.shape[2]
    hid0 = _matmul(x1_b, w_up, "nn", F32, "ffn_up", tm=1024, tn=wsl, tk=d, b_blocked=True)
    act = _ffn_act_fwd(hid0, w_fc, _behind(b_fc, wx.forward("w_down", hid0)), t, dff)
    w_down = wx.get("w_down", act)
    ff = _matmul(act, w_down, "nn", F32, "ffn_down", tm=1024, tn=512, tk=dff)
    dz2, dz2_b, d_ln2_g, d_ln2_b, loss = _ln2_loss(xhat1, ln1_g, ln1_b, ff, target, ln2_g, ln2_b)

    if gx is None:
        gx = _Kept()
    d_w_down = _matmul(act, dz2_b, "tn", BF16, "ffn_down_dw", tm=1408, tn=1024, tk=t)
    d_act = _matmul(dz2_b, w_down, "nt", F32, "ffn_down_dx", tm=2048, tn=512, tk=d, after=gx.start("w_down", d_w_down))
    *d_hid0, dwv, dwg, dbv, dbg = _ffn_act_bwd(d_act, hid0, w_fc, _behind(b_fc, gx.relay("w_down", d_act)), t, dff)
    d_w_fc = jnp.concatenate([dwv, dwg], axis=1)
    d_b_fc = jnp.concatenate([dbv, dbg], axis=1)
    d_hid0 = tuple(d_hid0[:2])
    d_w_up = _matmul(x1_b, d_hid0, "tn", BF16, "ffn_up_dw", tm=1024, tn=wsl, tk=t, o_width=wsl)
    d_x1_ffn = _matmul(d_hid0, w_up, "nt", F32, "ffn_up_dx", tm=1024, tn=1024, tk=wsl, b_blocked=True,
                       after=gx.start("w_up", d_w_up))
    dz1, dz1_b, d_ln1_g, d_ln1_b = _ln1_bwd(dz2, d_x1_ffn, xhat1, rstd1, _behind(ln1_g, gx.relay("w_up", d_x1_ffn)))

    d_w_out = _matmul(y, dz1_b, "tn", BF16, "out_proj_dw", tm=1024, tn=1024, tk=t, a_blocked=True)
    dy = _matmul(dz1_b, w_out, "nt", F32, "out_proj_dx", tm=1024, tn=1024, tk=d, after=gx.start("w_out", d_w_out))
    dcb, dcc, dch, d_w_sc = _sconv_bwd(dy, proj, _behind(w_sc, gx.relay("w_out", dy)), t, wc)
    d_o, d_hval, d_gain = _hnorm_bwd(dy, hval, proj, gain, t, wc, dh)
    dq, dk, dv, dgate = _mlstm_bwd(proj, gcol, grow, hval, d_hval, cs, ns, t, wc, dh)
    dgt, d_b_gates = _gates_bwd(dgate, proj, bias_tile, t, gate_tile)
    d_proj = jnp.concatenate([dcb, dcc, dch, dq, dk, dv, d_o, dgt], axis=1)
    d_w_in = _matmul(d_proj, x_b, "tn", BF16, "proj_dw", tm=2432, tn=1024, tk=t)
    small = dict(b_gates=d_b_gates[:, :2 * NH], w_sc_conv=d_w_sc, mh_gain=d_gain, ln1_g=d_ln1_g, ln1_b=d_ln1_b,
                 w_ffn_conv=d_w_fc, b_ffn_conv=d_b_fc, ln2_g=d_ln2_g, ln2_b=d_ln2_b)
    token = gx.start("w_in", d_w_in)
    token = gx.relay("w_in", gx.meanwhile(small, loss, token))
    grad_x = _matmul(d_proj, w_in, "nn", F32, "proj_dx", tm=512, tn=512, tk=ninp, add=dz1, add_scale=ALPHA, after=token)
    return loss, grad_x, small, gx


HBM = pl.BlockSpec(memory_space=pltpu.HBM)


def _place():
    return lax.axis_index("x"), lax.axis_index("y"), lax.axis_index("c")


def _index(p):
    return 4 * p[0] + 2 * p[1] + p[2]


def _all_gather(arrs, name):
    n = len(arrs)

    def body(*refs):
        ins, outs = refs[:n], refs[n:2 * n]
        send_sems, recv_sems, local_sems = refs[2 * n:]
        x, y, c = _place()
        me, sibling = (x, y, c), (x, y, 1 - c)
        chips = [(1 - x, y), (x, 1 - y), (1 - x, 1 - y)]

        def copy(a, k, block, to, own=False):
            dst = outs[a].at[_index(block)]
            return pltpu.make_async_remote_copy(
                src_ref=ins[a] if own else dst, dst_ref=dst,
                send_sem=send_sems.at[k * n + a], recv_sem=recv_sems.at[k * n + a],
                device_id=to, device_id_type=MESH)

        mine = [pltpu.make_async_copy(ins[a], outs[a].at[_index(me)], local_sems.at[a]) for a in range(n)]
        for cp in mine:
            cp.start()
        first = []
        for a in range(n):
            first.append(copy(a, 0, me, sibling, own=True))
            first += [copy(a, 1 + j, me, (*chip, c), own=True) for j, chip in enumerate(chips)]
        for cp in first:
            cp.start()
        passed = []
        for j, chip in enumerate(chips):
            for a in range(n):
                copy(a, 1 + j, (*chip, c), me).wait_recv()
                cp = copy(a, 4 + j, (*chip, c), sibling)
                cp.start()
                passed.append(cp)
        for a in range(n):
            copy(a, 0, sibling, me).wait_recv()
            for j, chip in enumerate(chips):
                copy(a, 4 + j, (*chip, 1 - c), me).wait_recv()
        for cp in first + passed:
            cp.wait_send()
        for cp in mine:
            cp.wait()

    return pl.pallas_call(
        body, name=name, in_specs=[HBM] * n, out_specs=[HBM] * n,
        out_shape=[jax.ShapeDtypeStruct((N_DEV,) + a.shape, a.dtype) for a in arrs],
        scratch_shapes=[pltpu.SemaphoreType.DMA((7 * n,)), pltpu.SemaphoreType.DMA((7 * n,)),
                        pltpu.SemaphoreType.DMA((n,))],
    )(*arrs)


SEM = pl.BlockSpec(memory_space=pltpu.SEMAPHORE)
EFFECT = pltpu.SideEffectType.DATAFLOW_SIDE_EFFECTING


def _chips(x, y):
    return [(1 - x, y), (x, 1 - y), (1 - x, 1 - y)]


N_CHIP = N_DEV // 2


def _pair_route(x, y, c):
    return [((x, y, 1 - c), 2 * q + (1 - c), q, q) for q in range(N_CHIP)]


def _chip_route(x, y, c):
    mine = 2 * x + y
    return [((*chip, c), 2 * chip[0] + chip[1], mine, 2 * chip[0] + chip[1]) for chip in _chips(x, y)]


def _exchange_pieces(g_ref, land_ref, width, tail):
    if not tail:
        return [(lambda i: g_ref.at[i], lambda s: land_ref.at[s])]
    rows = lambda i, n: pl.ds(pl.multiple_of(i * width, IN_TAIL), n)
    return [(lambda i: g_ref.at[rows(i, width), :], lambda s: land_ref.at[s, pl.ds(0, width), :]),
            (lambda i: g_ref.at[rows(i + 1, IN_TAIL), :], lambda s: land_ref.at[s, pl.ds(width, IN_TAIL), :])]


def _chip_slot(x, y, c):
    return 2 * x + y


def _exchange_start(grad, route, tail, name, own_slot=None):
    width = IN_SLAB if tail else grad.shape[1]
    n_p = 2 if tail else 1
    n_c = len(route(0, 0, 0))
    land_shape = (N_CHIP, width + (IN_TAIL if tail else 0), grad.shape[-1])
    assert not (tail and own_slot)

    def body(g_ref, land_ref, send_sems, recv_sems, g_thru, land_thru, token):
        for j, (peer, slab, slot, _) in enumerate(route(*_place())):
            for p, (src, dst) in enumerate(_exchange_pieces(g_ref, land_ref, width, tail)):
                pltpu.make_async_remote_copy(src_ref=src(slab), dst_ref=dst(slot), send_sem=send_sems.at[j * n_p + p],
                                             recv_sem=recv_sems.at[j * n_p + p], device_id=peer,
                                             device_id_type=MESH).start()
        if own_slot:
            mine = own_slot(*_place())
            pltpu.make_async_copy(g_ref.at[mine], land_ref.at[mine], send_sems.at[n_c * n_p]).start()
        token[...] = jnp.zeros_like(token)

    return pl.pallas_call(
        body, name=name,
        out_shape=(pltpu.SemaphoreType.DMA((n_c * n_p + bool(own_slot),)), pltpu.SemaphoreType.DMA((n_c * n_p,)),
                   pltpu.HBM(grad.shape, grad.dtype), pltpu.HBM(land_shape, grad.dtype),
                   jax.ShapeDtypeStruct((8, LANE), F32)),
        in_specs=(HBM, HBM), out_specs=(SEM, SEM, HBM, HBM, pl.BlockSpec(memory_space=pltpu.VMEM)),
        input_output_aliases={0: 2, 1: 3},
        compiler_params=pltpu.CompilerParams(has_side_effects=EFFECT),
    )(pltpu.with_memory_space_constraint(grad, pltpu.HBM),
      pltpu.with_memory_space_constraint(lax.empty(land_shape, grad.dtype), pltpu.HBM))


def _exchange_wait(send_sems, recv_sems, g_thru, land_thru, after, route, tail, name, own_slot=None):
    width = IN_SLAB if tail else g_thru.shape[1]
    n_p = 2 if tail else 1

    def body(g_ref, land_ref, send_sems, recv_sems, after_ref, g_dead, got_ref):
        places = route(*_place())
        for j, (peer, slab, _, slot) in enumerate(places):
            for p, (src, dst) in enumerate(_exchange_pieces(g_ref, land_ref, width, tail)):
                cp = pltpu.make_async_remote_copy(src_ref=src(slab), dst_ref=dst(slot),
                                                  send_sem=send_sems.at[j * n_p + p], recv_sem=recv_sems.at[j * n_p + p],
                                                  device_id=peer, device_id_type=MESH)
                cp.wait_send()
                cp.wait_recv()
        if own_slot:
            mine = own_slot(*_place())
            pltpu.make_async_copy(g_ref.at[mine], land_ref.at[mine], send_sems.at[len(places) * n_p]).wait()

    return pl.pallas_call(
        body, name=name,
        out_shape=(pltpu.HBM(g_thru.shape, g_thru.dtype), pltpu.HBM(land_thru.shape, land_thru.dtype)),
        in_specs=(HBM, HBM, SEM, SEM, pl.BlockSpec(memory_space=pl.ANY)), out_specs=(HBM, HBM),
        input_output_aliases={0: 0, 1: 1},
        compiler_params=pltpu.CompilerParams(has_side_effects=EFFECT),
    )(g_thru, land_thru, send_sems, recv_sems, after)


def _pair_add(grad, pair, core, tail, name):
    rows, cols = (IN_SLAB if tail else grad.shape[1]), grad.shape[-1]
    total = pair.shape[1]

    def body(core_ref, *refs):
        if tail:
            g_ref, t_ref, p_ref, o_ref = refs
            o_ref[0:rows, :] = (g_ref[...].astype(F32) + p_ref[0:rows, :].astype(F32)).astype(BF16)
            o_ref[rows:total, :] = (t_ref[...].astype(F32) + p_ref[rows:total, :].astype(F32)).astype(BF16)
        else:
            g_ref, p_ref, o_ref = refs
            o_ref[...] = (g_ref[...].astype(F32) + p_ref[...].astype(F32)).astype(BF16)

    if tail:
        tc = _fit(cols, 512)
        grid = (N_CHIP, cols // tc)
        slab = pl.BlockSpec((None, total, tc), lambda q, i, core_ref: (q, 0, i))
        per = IN_SLAB // IN_TAIL
        in_specs = [pl.BlockSpec((rows, tc), lambda q, i, core_ref: (2 * q + core_ref[0], i)),
                    pl.BlockSpec((IN_TAIL, tc), lambda q, i, core_ref: ((2 * q + core_ref[0] + 1) * per, i))]
    else:
        tr = _rows(rows, 1024)
        grid = (N_CHIP, rows // tr)
        slab = pl.BlockSpec((None, tr, cols), lambda q, i, core_ref: (q, i, 0))
        in_specs = [pl.BlockSpec((None, tr, cols), lambda q, i, core_ref: (2 * q + core_ref[0], i, 0))]
    return pl.pallas_call(
        body, name=name,
        grid_spec=pltpu.PrefetchScalarGridSpec(num_scalar_prefetch=1, grid=grid,
                                               in_specs=in_specs + [slab], out_specs=slab),
        out_shape=jax.ShapeDtypeStruct(pair.shape, BF16),
        compiler_params=_params("parallel", "parallel"),
    )(core, *([grad, grad] if tail else [grad]), pair)


def _relay_places(x, y, c):
    came_from = (c * (1 - x) + (1 - c) * x, c * y + (1 - c) * (1 - y), c)
    pass_to = (c * x + (1 - c) * (1 - x), c * (1 - y) + (1 - c) * y, c)
    return 2 - c, came_from, pass_to, pass_to


OWN = 4


def _gather_start(blocks, after, name, spare=(), relayed=False):
    n = len(blocks)
    lands = [(N_DEV + (a in spare),) + b.shape for a, b in enumerate(blocks)]

    def body(*refs):
        b_refs, land_refs = refs[:n], refs[n:2 * n]
        send_sems, recv_sems = refs[2 * n + 1:3 * n + 1], refs[3 * n + 1:4 * n + 1]
        token = refs[-1]
        x, y, c = _place()
        me = _index((x, y, c))
        for a in range(n):
            targets = [(x, y, 1 - c)] + [(*chip, c) for chip in _chips(x, y)]
            for k, to in enumerate(targets[:3] if relayed else targets):
                pltpu.make_async_remote_copy(src_ref=b_refs[a], dst_ref=land_refs[a].at[me], send_sem=send_sems[a].at[k],
                                             recv_sem=recv_sems[a].at[k], device_id=to, device_id_type=MESH).start()
        for a in range(n):
            pltpu.make_async_copy(b_refs[a], land_refs[a].at[me], send_sems[a].at[OWN]).start()
        token[...] = jnp.zeros_like(token)

    sems = [pltpu.SemaphoreType.DMA((OWN + 1,))] * n
    out = pl.pallas_call(
        body, name=name,
        out_shape=(*sems, *sems, *[pltpu.HBM(b.shape, b.dtype) for b in blocks],
                   *[pltpu.HBM(s, b.dtype) for s, b in zip(lands, blocks)], jax.ShapeDtypeStruct((8, LANE), F32)),
        in_specs=(*[HBM] * (2 * n), pl.BlockSpec(memory_space=pl.ANY)),
        out_specs=(*[SEM] * (2 * n), *[HBM] * (2 * n), pl.BlockSpec(memory_space=pltpu.VMEM)),
        input_output_aliases={i: 2 * n + i for i in range(2 * n)},
        compiler_params=pltpu.CompilerParams(has_side_effects=EFFECT),
    )(*[pltpu.with_memory_space_constraint(b, pltpu.HBM) for b in blocks],
      *[pltpu.with_memory_space_constraint(lax.empty(s, b.dtype), pltpu.HBM) for s, b in zip(lands, blocks)], after)
    return [(out[a], out[n + a], out[2 * n + a], out[3 * n + a]) for a in range(n)], out[-1]


def _gather_relay(states, after, name):
    n, first_out = len(states), 3 * len(states) + len(after)

    def body(*refs):
        land_refs, send_sems, recv_sems = refs[:n], refs[n:2 * n], refs[2 * n:3 * n]
        pass_send, pass_recv = refs[first_out + n:first_out + 2 * n], refs[first_out + 2 * n:first_out + 3 * n]
        k_in, came_from, pass_to, _ = _relay_places(*_place())
        for a in range(n):
            slot = land_refs[a].at[_index(came_from)]
            pltpu.make_async_remote_copy(src_ref=slot, dst_ref=slot, send_sem=send_sems[a].at[k_in],
                                         recv_sem=recv_sems[a].at[k_in], device_id=came_from,
                                         device_id_type=MESH).wait_recv()
        for a in range(n):
            slot = land_refs[a].at[_index(came_from)]
            pltpu.make_async_remote_copy(src_ref=slot, dst_ref=slot, send_sem=pass_send[a].at[0],
                                         recv_sem=pass_recv[a].at[0], device_id=pass_to, device_id_type=MESH).start()
        refs[-1][...] = jnp.zeros_like(refs[-1])

    lands = [st[3] for st in states]
    pair = [pltpu.SemaphoreType.DMA((1,))] * n
    out = pl.pallas_call(
        body, name=name,
        out_shape=(*[pltpu.HBM(l.shape, l.dtype) for l in lands], *pair, *pair, jax.ShapeDtypeStruct((8, LANE), F32)),
        in_specs=(*[HBM] * n, *[SEM] * (2 * n), *[pl.BlockSpec(memory_space=pl.ANY)] * len(after)),
        out_specs=(*[HBM] * n, *[SEM] * (2 * n), pl.BlockSpec(memory_space=pltpu.VMEM)),
        input_output_aliases={a: a for a in range(n)},
        compiler_params=pltpu.CompilerParams(has_side_effects=EFFECT),
    )(*lands, *[st[0] for st in states], *[st[1] for st in states], *after)
    return [(st[0], st[1], st[2], out[a], (out[n + a], out[2 * n + a])) for a, st in enumerate(states)], out[-1]


def _gather_forward(send_sems, recv_sems, b_thru, land_thru, after, name, passed=None):
    relayed = passed is not None

    def body(b_ref, land_ref, send_sems, recv_sems, *rest):
        pass_send, pass_recv = rest[:2] if relayed else (None, None)
        send2, recv2, token = rest[-3:]
        x, y, c = _place()
        sibling = (x, y, 1 - c)
        arrivals = [sibling] + [(*chip, c) for chip in _chips(x, y)]
        waits = [(send_sems.at[k], recv_sems.at[k], frm) for k, frm in enumerate(arrivals)]
        sends = [send_sems.at[k] for k in range(4)]
        if relayed:
            k_in, _, _, other = _relay_places(x, y, c)
            waits = [waits[0], (send_sems.at[3 - k_in], recv_sems.at[3 - k_in], other),
                     (pass_send.at[0], pass_recv.at[0], arrivals[3])]
            sends[3] = pass_send.at[0]
        for sem in sends:
            pltpu.make_async_remote_copy(src_ref=b_ref, dst_ref=land_ref.at[0], send_sem=sem, recv_sem=recv_sems.at[0],
                                         device_id=sibling, device_id_type=MESH).wait_send()
        pltpu.make_async_copy(b_ref, land_ref.at[_index((x, y, c))], send_sems.at[OWN]).wait()
        for send_sem, recv_sem, frm in waits:
            pltpu.make_async_remote_copy(src_ref=b_ref, dst_ref=land_ref.at[_index(frm)], send_sem=send_sem,
                                         recv_sem=recv_sem, device_id=frm, device_id_type=MESH).wait_recv()
        for j, chip in enumerate(_chips(x, y)):
            slot = land_ref.at[_index((*chip, c))]
            pltpu.make_async_remote_copy(src_ref=slot, dst_ref=slot, send_sem=send2.at[j], recv_sem=recv2.at[j],
                                         device_id=sibling, device_id_type=MESH).start()
        token[...] = jnp.zeros_like(token)

    extra = list(passed) if relayed else []
    return pl.pallas_call(
        body, name=name,
        out_shape=(pltpu.HBM(b_thru.shape, b_thru.dtype), pltpu.HBM(land_thru.shape, land_thru.dtype),
                   pltpu.SemaphoreType.DMA((3,)), pltpu.SemaphoreType.DMA((3,)), jax.ShapeDtypeStruct((8, LANE), F32)),
        in_specs=(HBM, HBM, SEM, SEM, *[SEM] * len(extra), pl.BlockSpec(memory_space=pl.ANY)),
        out_specs=(HBM, HBM, SEM, SEM, pl.BlockSpec(memory_space=pltpu.VMEM)),
        input_output_aliases={0: 0, 1: 1},
        compiler_params=pltpu.CompilerParams(has_side_effects=EFFECT),
    )(b_thru, land_thru, send_sems, recv_sems, *extra, after)


def _gather_finish(land_thru, send2, recv2, after, name):
    def body(land_ref, send2, recv2, after_ref, land_out):
        x, y, c = _place()
        for j, chip in enumerate(_chips(x, y)):
            cp = pltpu.make_async_remote_copy(src_ref=land_ref.at[_index((*chip, c))],
                                              dst_ref=land_ref.at[_index((*chip, 1 - c))], send_sem=send2.at[j],
                                              recv_sem=recv2.at[j], device_id=(x, y, 1 - c), device_id_type=MESH)
            cp.wait_send()
            cp.wait_recv()

    return pl.pallas_call(
        body, name=name, out_shape=pltpu.HBM(land_thru.shape, land_thru.dtype),
        in_specs=(HBM, SEM, SEM, pl.BlockSpec(memory_space=pl.ANY)), out_specs=HBM,
        input_output_aliases={0: 0},
        compiler_params=pltpu.CompilerParams(has_side_effects=EFFECT),
    )(land_thru, send2, recv2, after)


class _Gathering:
    def __init__(self, ahead, later, me):
        cast = [a.astype(BF16) for a in ahead.values()]
        started, self.token = _gather_start(cast, cast[0], "gather1_ahead", relayed=True)
        self.me, self.state, self.relayed, self.later = me, dict(zip(ahead, started)), tuple(ahead), later

    def start_first(self, first):
        started, self.token = _gather_start(list(first.values()), self.token, "gather1_first", spare=(0,), relayed=True)
        self.state.update(zip(first, started))
        self.relayed += tuple(first)

    def begin(self, after):
        return self.token

    def relay(self, *after):
        states, token = _gather_relay([self.state[n] for n in self.relayed], after, "gather_relay")
        self.state.update(zip(self.relayed, states))
        cast = [_behind(a, token).astype(BF16) for a in self.later.values()]
        started, self.token = _gather_start(cast, token, "gather1_later")
        self.state.update(zip(self.later, started))
        return self.token

    def forward(self, name, after):
        first_leg, passed = self.state[name][:4], (self.state[name][4:] or (None,))[0]
        *self.state[name], token = _gather_forward(*first_leg, after, "gather2_" + name, passed=passed)
        return token

    def get(self, name, after):
        _, land, send2, recv2 = self.state[name]
        land = _gather_finish(land, send2, recv2, after, "gather3_" + name)
        return land if name not in ("w_out", "w_down") else land.reshape(-1, land.shape[2])


class _Reducing:
    def __init__(self, core, chip, gather_small):
        self.core, self.chip, self.state, self.token, self.gather_small = core, chip, {}, None, gather_small

    def meanwhile(self, small, loss, after):
        self.small_sum = self.gather_small(small, loss, after)
        return self.small_sum

    def start(self, name, grad):
        tail = name == "w_in"
        g = grad if tail or grad.ndim == 3 else grad.reshape(N_DEV, grad.shape[0] // N_DEV, grad.shape[1])
        *self.state[name], token = _exchange_start(g, _pair_route, tail, "pair_send_" + name)
        return token

    def relay(self, name, after):
        tail = name == "w_in"
        grad, pair = _exchange_wait(*self.state[name], after, _pair_route, tail, "pair_recv_" + name)
        total = _pair_add(grad, pair, self.core, tail, "pair_add_" + name)
        *self.state[name], self.token = _exchange_start(total, _chip_route, False, "chip_send_" + name,
                                                        own_slot=_chip_slot)
        return self.token

    def finish(self, name, after):
        _, land = _exchange_wait(*self.state[name], after, _chip_route, False, "chip_recv_" + name, own_slot=_chip_slot)
        return land


def _carry_w_in(main, tail):
    slabs, _, d = main.shape
    tc = _fit(d, 2048)
    assert slabs == N_DEV + 1 and tail.shape[:2] == (N_DEV, IN_TAIL), (main.shape, tail.shape)
    top = lambda off: pl.BlockSpec((None, IN_TAIL, tc), lambda s, j: (s + off, 0, j))

    def carry(m_ref, t_ref, o_ref):
        o_ref[...] = m_ref[...] + t_ref[...]

    main = pl.pallas_call(
        carry, name="carry_w_in", grid=(N_DEV - 1, d // tc), in_specs=[top(1), top(0)], out_specs=top(1),
        out_shape=jax.ShapeDtypeStruct(main.shape, main.dtype), input_output_aliases={0: 0},
        compiler_params=_params("parallel", "parallel"),
    )(main, tail)

    def last(m_ref, t_ref, o_ref):
        o_ref[...] = jnp.zeros_like(o_ref)
        o_ref[0:IN_TAIL, :] = t_ref[...]

    return pl.pallas_call(
        last, name="last_slab_w_in", grid=(d // tc,),
        in_specs=[pl.BlockSpec(memory_space=pl.ANY), pl.BlockSpec((None, IN_TAIL, tc), lambda j: (N_DEV - 1, 0, j))],
        out_specs=pl.BlockSpec((None, LANE, tc), lambda j: (N_DEV, 0, j)),
        out_shape=jax.ShapeDtypeStruct(main.shape, main.dtype), input_output_aliases={0: 0},
        compiler_params=_params("parallel"),
    )(main, tail)


def _rows(n, want):
    t = min(n, want)
    t -= t % 16
    while n % t:
        t -= 16
    return t


def _adam_math(w, g, m, v):
    m2 = ADAM_B1 * m + (1.0 - ADAM_B1) * g
    v2 = ADAM_B2 * v + (1.0 - ADAM_B2) * (g * g)
    m_hat = m2 * (1.0 / (1.0 - ADAM_B1 ** ADAM_STEP))
    v_hat = v2 * (1.0 / (1.0 - ADAM_B2 ** ADAM_STEP))
    return -ADAM_LR * (m_hat / (jnp.sqrt(v_hat) + ADAM_EPS) + ADAM_WD * w), m2, v2


def _slot_sum(r_ref):
    acc = r_ref[0].astype(F32)
    for i in range(1, r_ref.shape[0]):
        acc = acc + r_ref[i].astype(F32)
    return acc


def _shift_w_in(w):
    ws, d = w.shape
    tc = _fit(d, 256)

    def body(w_ref, main_ref, tail_ref, tall):
        tall[...] = jnp.zeros_like(tall)
        tall[0:ws, :] = w_ref[...]
        moved = pltpu.roll(tall[...], _index(_place()), 0).astype(BF16)
        main_ref[...] = moved[0:IN_SLAB]
        tail_ref[...] = moved[IN_SLAB:]

    return pl.pallas_call(
        body, name="shift_w_in", grid=(d // tc,),
        in_specs=[pl.BlockSpec((ws, tc), lambda j: (0, j))],
        out_specs=[pl.BlockSpec((IN_SLAB, tc), lambda j: (0, j)), pl.BlockSpec((IN_TAIL, tc), lambda j: (0, j))],
        out_shape=[jax.ShapeDtypeStruct((IN_SLAB, d), BF16), jax.ShapeDtypeStruct((IN_TAIL, d), BF16)],
        scratch_shapes=[pltpu.VMEM((IN_SLAB + IN_TAIL, tc), F32)], compiler_params=_params("parallel"),
    )(w)


def _sum_adamw_shifted(r, w, m, v, name):
    _, ph, d = r.shape
    ws = w.shape[0]
    tc = _fit(d, 256)

    def body(r_ref, w_ref, m_ref, v_ref, g_ref, d_ref, m2_ref, v2_ref, tall):
        tall[...] = pltpu.roll(_slot_sum(r_ref), lax.rem(ph - _index(_place()), ph), 0)
        g = tall[0:ws, :]
        g_ref[...] = g
        d_ref[...], m2_ref[...], v2_ref[...] = _adam_math(w_ref[...], g, m_ref[...], v_ref[...])

    blk = pl.BlockSpec((ws, tc), lambda j: (0, j))
    out = jax.ShapeDtypeStruct(w.shape, F32)
    return pl.pallas_call(
        body, name=name, grid=(d // tc,),
        in_specs=[pl.BlockSpec((r.shape[0], ph, tc), lambda j: (0, 0, j)), blk, blk, blk],
        out_specs=[blk] * 4, out_shape=[out] * 4,
        scratch_shapes=[pltpu.VMEM((ph, tc), F32)], compiler_params=_params("parallel"),
    )(r, w, m, v)


def _sum_slots(r, name, tr=128):
    _, rows, cols = r.shape
    tr = _rows(rows, tr)

    def body(r_ref, g_ref):
        g_ref[...] = _slot_sum(r_ref)

    return pl.pallas_call(
        body, name=name, grid=(rows // tr,),
        in_specs=[pl.BlockSpec((r.shape[0], tr, cols), lambda i: (0, i, 0))],
        out_specs=pl.BlockSpec((tr, cols), lambda i: (i, 0)),
        out_shape=jax.ShapeDtypeStruct((rows, cols), F32),
        compiler_params=_params("parallel"),
    )(r)


def _adamw(w, g, m, v, name, tr=256):
    rows, cols = w.shape
    tr = _rows(rows, tr)

    def body(w_ref, g_ref, m_ref, v_ref, d_ref, m2_ref, v2_ref):
        d_ref[...], m2_ref[...], v2_ref[...] = _adam_math(w_ref[...], g_ref[...], m_ref[...], v_ref[...])

    blk = pl.BlockSpec((tr, cols), lambda i: (i, 0))
    out = jax.ShapeDtypeStruct((rows, cols), F32)
    return pl.pallas_call(
        body, name=name, grid=(rows // tr,), in_specs=[blk] * 4, out_specs=[blk] * 3, out_shape=[out] * 3,
        compiler_params=_params("parallel"),
    )(w, g, m, v)


def _sum_adamw(r, w, m, v, name, tr=256):
    rows, cols = w.shape
    tr = _rows(rows, tr)

    def body(r_ref, w_ref, m_ref, v_ref, g_ref, d_ref, m2_ref, v2_ref):
        g = _slot_sum(r_ref)
        g_ref[...] = g
        d_ref[...], m2_ref[...], v2_ref[...] = _adam_math(w_ref[...], g, m_ref[...], v_ref[...])

    blk = pl.BlockSpec((tr, cols), lambda i: (i, 0))
    out = jax.ShapeDtypeStruct((rows, cols), F32)
    return pl.pallas_call(
        body, name=name, grid=(rows // tr,),
        in_specs=[pl.BlockSpec((r.shape[0], tr, cols), lambda i: (0, i, 0)), blk, blk, blk],
        out_specs=[blk] * 4, out_shape=[out] * 4,
        compiler_params=_params("parallel"),
    )(r, w, m, v)


def _pack(pieces, sizes):
    flat = [jnp.pad(p.reshape(-1).astype(F32), (0, s - p.size)) for p, s in zip(pieces, sizes)]
    total = sum(sizes)
    padded = -(-total // (16 * LANE)) * (16 * LANE)
    return jnp.pad(jnp.concatenate(flat), (0, padded - total)).reshape(-1, LANE)


def _unpack(packed, shapes, sizes):
    flat = packed.reshape(-1)
    out, off = [], 0
    for shp, s in zip(shapes, sizes):
        n = 1
        for k in shp:
            n *= k
        out.append(flat[off:off + n].reshape(shp))
        off += s
    return out


def _lanes(n):
    return -(-n // LANE) * LANE


WEIGHTS = ("w_in", "b_gates", "w_sc_conv", "mh_gain", "w_out", "ln1_g", "ln1_b", "w_up", "w_ffn_conv", "b_ffn_conv",
           "w_down", "ln2_g", "ln2_b")
BIG = ("w_in", "w_out", "w_up", "w_down")
SMALL = tuple(n for n in WEIGHTS if n not in BIG)


def kernel(x, w_in, b_gates, w_sc_conv, mh_gain, w_out, ln1_g, ln1_b, w_up, w_ffn_conv, b_ffn_conv, w_down, ln2_g, ln2_b, loss_target, m_w_in, m_b_gates, m_w_sc_conv, m_mh_gain, m_w_out, m_ln1_g, m_ln1_b, m_w_up, m_w_ffn_conv, m_b_ffn_conv, m_w_down, m_ln2_g, m_ln2_b, v_w_in, v_b_gates, v_w_sc_conv, v_mh_gain, v_w_out, v_ln1_g, v_ln1_b, v_w_up, v_w_ffn_conv, v_b_ffn_conv, v_w_down, v_ln2_g, v_ln2_b):
    w = dict(zip(WEIGHTS, (w_in, b_gates, w_sc_conv, mh_gain, w_out, ln1_g, ln1_b, w_up, w_ffn_conv, b_ffn_conv,
                           w_down, ln2_g, ln2_b)))
    m = dict(zip(WEIGHTS, (m_w_in, m_b_gates, m_w_sc_conv, m_mh_gain, m_w_out, m_ln1_g, m_ln1_b, m_w_up,
                           m_w_ffn_conv, m_b_ffn_conv, m_w_down, m_ln2_g, m_ln2_b)))
    v = dict(zip(WEIGHTS, (v_w_in, v_b_gates, v_w_sc_conv, v_mh_gain, v_w_out, v_ln1_g, v_ln1_b, v_w_up,
                           v_w_ffn_conv, v_b_ffn_conv, v_w_down, v_ln2_g, v_ln2_b)))
    me = _index(_place())
    d = x.shape[2]
    ws_in = w_in.shape[2]
    assert ws_in == IN_SLAB + 1 and N_DEV <= LANE, w_in.shape
    ninp = (N_DEV + 1) * IN_SLAB
    ws_sc, ws_fc = w_sc_conv.shape[2], w_ffn_conv.shape[2]

    wx = _Gathering({"w_out": w_out[0]}, {n: w[n][0] for n in ("w_up", "w_down")}, me)
    w_in_t = jnp.transpose(_behind(w_in[0], wx.token))
    w_in_main, w_in_tail = _shift_w_in(w_in_t)
    taps8 = lambda a: jnp.pad(a[0], ((0, 5), (0, 0)))
    at_once = ("w_in", "w_tail", "w_sc", "w_fc")
    wx.start_first(dict(zip(at_once, (w_in_main, w_in_tail, taps8(w_sc_conv), taps8(w_ffn_conv)))))
    x_b = _behind(x[0], wx.begin(None)).astype(BF16)
    m_in_t, v_in_t = (jnp.transpose(_behind(a[0], wx.begin(None))) for a in (m_w_in, v_w_in))
    token = wx.relay(x_b, m_in_t, v_in_t)
    for n in at_once:
        token = wx.forward(n, token)
    g_in, g_tail, g_sc, g_fc = (wx.get(n, token) for n in at_once)
    w_in_full = _carry_w_in(g_in, g_tail).reshape(ninp, d)
    w_sc_full = g_sc[:, :3].transpose(1, 0, 2).reshape(3, N_DEV * ws_sc)
    w_fc_full = g_fc[:, :3].transpose(1, 0, 2).reshape(3, N_DEV * ws_fc)

    xi, yi, ci = _place()
    names = ("loss",) + SMALL
    pieces = {}

    def gather_small(small, loss_t, after):
        pieces.update(small, loss=loss_t[0, :1])
        sizes = [_lanes(pieces[n].size) for n in names]
        (g_small,) = _all_gather([_behind(_pack([pieces[n] for n in names], sizes), after)], "gather_small")
        return _sum_slots(g_small, "sum_small", tr=g_small.shape[1])

    gx = _Reducing(jnp.reshape(ci, (1,)).astype(jnp.int32), 2 * xi + yi, gather_small)
    loss_t, grad_x, small, _ = _local_step(
        x[0], loss_target[0], w_in_full, b_gates, w_sc_full, mh_gain, None, ln1_g, ln1_b, None,
        w_fc_full, b_ffn_conv, None, ln2_g, ln2_b, gx=gx, wx=wx, x_b=x_b)

    grads, deltas, new_m, new_v = {}, {}, {}, {}
    for name in ("w_down", "w_up", "w_out"):
        grads[name], deltas[name], new_m[name], new_v[name] = _sum_adamw(
            gx.finish(name, gx.token), w[name][0], m[name][0], v[name][0], "adamw_" + name)

    summed = _unpack(gx.small_sum, [pieces[n].shape for n in names], [_lanes(pieces[n].size) for n in names])
    full = dict(zip(names, summed))
    full["w_sc_conv"] = lax.dynamic_slice(full["w_sc_conv"], (0, me * ws_sc), (3, ws_sc))
    full["w_ffn_conv"] = lax.dynamic_slice(full["w_ffn_conv"], (0, me * ws_fc), (3, ws_fc))
    for n in SMALL:
        grads[n] = full[n].reshape(w[n].shape)
    sizes = [_lanes(w[n].size) for n in SMALL]
    shapes = [w[n].shape for n in SMALL]
    packed = [_pack([t[n] for n in SMALL], sizes) for t in (w, grads, m, v)]
    small_out = _adamw(*packed, "adamw_small")
    for res, t in zip(small_out, (deltas, new_m, new_v)):
        t.update(zip(SMALL, _unpack(res, shapes, sizes)))

    done = sum(t[0:1, 0:1] for t in (grad_x, deltas["w_down"], deltas["w_up"], deltas["w_out"], small_out[0]))
    grads["w_in"], deltas["w_in"], new_m["w_in"], new_v["w_in"] = (
        jnp.transpose(a)[None] for a in _sum_adamw_shifted(gx.finish("w_in", done), w_in_t, m_in_t, v_in_t, "adamw_w_in"))

    big = lambda t: {n: (t[n].reshape(w[n].shape) if n in BIG else t[n]) for n in WEIGHTS}
    grads, deltas, new_m, new_v = big(grads), big(deltas), big(new_m), big(new_v)
    return (full["loss"].reshape(()), grad_x[None], *[grads[n] for n in WEIGHTS], *[deltas[n] for n in WEIGHTS],
            *[new_m[n] for n in WEIGHTS], *[new_v[n] for n in WEIGHTS])
```

```python
import functools

import jax
import jax.numpy as jnp
from jax import lax
from jax.experimental import pallas as pl
from jax.experimental.pallas import tpu as pltpu

F32 = jnp.float32
BF16 = jnp.bfloat16
MESH = pl.DeviceIdType.MESH

N_DEV = 8
NH = 4
CHUNK = 64
LN_EPS = 1e-5
HN_EPS = 1e-6
ALPHA = 2.0 ** 0.25
LANE = 128
IN_SLAB = 7 * LANE
IN_TAIL = 16
VMEM_LIMIT = 56 * 1024 * 1024
ADAM_LR, ADAM_B1, ADAM_B2, ADAM_EPS, ADAM_WD, ADAM_STEP = 0.001, 0.9, 0.999, 1e-08, 0.01, 10

_NN = (((1,), (0,)), ((), ()))
_NT = (((1,), (1,)), ((), ()))
_TN = (((0,), (0,)), ((), ()))


def _dot(a, b, dn=_NN):
    return lax.dot_general(a, b, dn, preferred_element_type=F32)


def _params(*sem):
    return pltpu.CompilerParams(dimension_semantics=sem if sem else None, vmem_limit_bytes=VMEM_LIMIT)


def _iota(shape, axis):
    return lax.broadcasted_iota(jnp.int32, shape, axis)


def _fit(n, want):
    if n <= want:
        return n
    t = want - want % LANE
    while n % t:
        t -= LANE
    return t


def _matmul(a, b, mode, out_dtype, name, tm=1024, tn=512, tk=1024, add=None, add_scale=1.0,
            a_blocked=False, b_blocked=False, o_width=None, after=None, n=None):
    a_parts = a if isinstance(a, tuple) else None
    b_parts = b if isinstance(b, tuple) else None
    if a_parts:
        a_blocked, (a_rows, wa), na = True, a[0].shape, len(a)
        kd, m = (a_rows, na * wa) if mode == "tn" else (na * wa, a_rows)
    elif a_blocked:
        na, a_rows, wa = a.shape
        kd, m = (a_rows, na * wa) if mode == "tn" else (na * wa, a_rows)
    elif mode == "tn":
        kd, m = a.shape
    else:
        m, kd = a.shape
    if b_parts:
        b_blocked, (rows, w), nb = True, b[0].shape, len(b)
    elif b_blocked:
        nb, rows, w = b.shape
    if b_blocked:
        n = rows if mode == "nt" else nb * w
        assert (nb * w if mode == "nt" else rows) == kd, (name, kd)
    else:
        n = n or (b.shape[0] if mode == "nt" else b.shape[1])
    tm, tn, tk = _fit(m, tm), _fit(n, tn), _fit(kd, tk)
    if a_blocked and mode == "tn":
        tm = _fit(wa, tm)
    if a_blocked and mode != "tn":
        tk = _fit(wa, tk)
    if b_blocked and mode != "nt":
        tn = _fit(w, tn)
    if b_blocked and mode == "nt":
        tk = _fit(w, tk)
    if o_width is not None:
        tn = _fit(o_width, tn)
    assert m % tm == 0 and n % tn == 0 and kd % tk == 0, (name, m, n, kd, tm, tn, tk)
    assert not (a_blocked and mode != "tn" and wa % tk) and not (b_blocked and mode == "nt" and w % tk), (name, tk)
    nk = kd // tk
    dn = {"nn": _NN, "nt": _NT, "tn": _TN}[mode]
    if a_blocked and mode == "tn":
        a_per = wa // tm
        a_spec = pl.BlockSpec((None, tk, tm), lambda i, j, k: (i // a_per, k, i % a_per))
    elif a_blocked:
        a_per = wa // tk
        a_spec = pl.BlockSpec((None, tm, tk), lambda i, j, k: (k // a_per, i, k % a_per))
    elif mode == "tn":
        a_spec = pl.BlockSpec((tk, tm), lambda i, j, k: (k, i))
    else:
        a_spec = pl.BlockSpec((tm, tk), lambda i, j, k: (i, k))
    if b_blocked and mode != "nt":
        per = w // tn
        b_spec = pl.BlockSpec((None, tk, tn), lambda i, j, k: (j // per, k, j % per))
    elif b_blocked:
        per = w // tk
        b_spec = pl.BlockSpec((None, tn, tk), lambda i, j, k: (k // per, j, k % per))
    elif mode == "nt":
        b_spec = pl.BlockSpec((tn, tk), lambda i, j, k: (j, k))
    else:
        b_spec = pl.BlockSpec((tk, tn), lambda i, j, k: (k, j))
    if o_width is None:
        o_spec = pl.BlockSpec((tm, tn), lambda i, j, k: (i, j))
        o_shape = (m, n)
    else:
        oper = o_width // tn
        o_spec = pl.BlockSpec((None, tm, tn), lambda i, j, k: (j // oper, i, j % oper))
        o_shape = (n // o_width, m, o_width)
    a_list, a_specs = [a], [a_spec]
    if a_parts:
        hold = lambda x, s: jnp.clip(x - s * a_per, 0, a_per - 1)
        a_list = list(a_parts)
        a_specs = [(pl.BlockSpec((tk, tm), lambda i, j, k, s=s: (k, hold(i, s))) if mode == "tn"
                    else pl.BlockSpec((tm, tk), lambda i, j, k, s=s: (i, hold(k, s)))) for s in range(na)]
    b_list, b_specs = [b], [b_spec]
    if b_parts:
        hold_b = lambda x, s: jnp.clip(x - s * per, 0, per - 1)
        b_list = list(b_parts)
        b_specs = [(pl.BlockSpec((tn, tk), lambda i, j, k, s=s: (j, hold_b(k, s))) if mode == "nt"
                    else pl.BlockSpec((tk, tn), lambda i, j, k, s=s: (k, hold_b(j, s)))) for s in range(nb)]
    n_a, n_b = len(a_list), len(b_list)
    has_add = add is not None
    n_in = n_a + n_b + has_add + (after is not None)
    in_place = nk > 1 and out_dtype == F32

    def body(*refs):
        add_ref = refs[n_a + n_b] if has_add else None
        o_ref = refs[n_in]
        i, j, k = pl.program_id(0), pl.program_id(1), pl.program_id(2)

        def finish(r):
            if has_add:
                r = r + add_scale * add_ref[...]
            o_ref[...] = r.astype(out_dtype)

        def step(a_ref, b_ref):
            if nk == 1:
                finish(_dot(a_ref[...], b_ref[...], dn))
                return
            acc = o_ref if in_place else refs[-1]

            @pl.when(k == 0)
            def _():
                acc[...] = _dot(a_ref[...], b_ref[...], dn)

            @pl.when(k > 0)
            def _():
                acc[...] += _dot(a_ref[...], b_ref[...], dn)

        if n_a == 1 and n_b == 1:
            step(refs[0], refs[1])
        else:
            slab_a = ((i if mode == "tn" else k) // a_per) if n_a > 1 else 0
            slab_b = ((k if mode == "nt" else j) // per) if n_b > 1 else 0
            for sa in range(n_a):
                for sb in range(n_b):
                    pl.when((slab_a == sa) & (slab_b == sb))(functools.partial(step, refs[sa], refs[n_a + sb]))
        if nk > 1 and not (in_place and not has_add):
            @pl.when(k == nk - 1)
            def _():
                finish((o_ref if in_place else refs[-1])[...])

    in_specs = a_specs + b_specs + ([pl.BlockSpec((tm, tn), lambda i, j, k: (i, j))] if has_add else [])
    args = (*a_list, *b_list) + ((add,) if has_add else ())
    if after is not None:
        in_specs.append(pl.BlockSpec(memory_space=pl.ANY))
        args += (after,)
    return pl.pallas_call(
        body, name=name, grid=(m // tm, n // tn, nk),
        in_specs=in_specs, out_specs=o_spec,
        out_shape=jax.ShapeDtypeStruct(o_shape, out_dtype),
        scratch_shapes=[pltpu.VMEM((tm, tn), F32)] if nk > 1 and not in_place else [],
        compiler_params=_params("parallel", "parallel", "arbitrary"),
    )(*args)


def _shift_down(u, s):
    return jnp.where(_iota(u.shape, 0) >= s, pltpu.roll(u, s, 0), 0.0)


def _shift_up(u, s):
    t = u.shape[0]
    return jnp.where(_iota(u.shape, 0) < t - s, pltpu.roll(u, t - s, 0), 0.0)


SLAB = 8


def _rolled(u):
    return pltpu.roll(u, 2, 0), pltpu.roll(u, 1, 0)


def _conv(u, w, rolled=None):
    u2, u1 = _rolled(u) if rolled is None else rolled
    raw = w[0:1] * u2 + w[1:2] * u1 + w[2:3] * u
    head = u[0:SLAB]
    mended = w[0:1] * _shift_down(head, 2) + w[1:2] * _shift_down(head, 1) + w[2:3] * head
    return jnp.concatenate([mended, raw[SLAB:]], axis=0)


def _conv_t(dy, w):
    t = dy.shape[0]
    raw = w[2:3] * dy + w[1:2] * pltpu.roll(dy, t - 1, 0) + w[0:1] * pltpu.roll(dy, t - 2, 0)
    tail = dy[t - SLAB:]
    mended = w[2:3] * tail + w[1:2] * _shift_up(tail, 1) + w[0:1] * _shift_up(tail, 2)
    return jnp.concatenate([raw[:t - SLAB], mended], axis=0)


def _conv_dw(dy, u, rolled=None):
    t = dy.shape[0]
    u2, u1 = _rolled(u) if rolled is None else rolled
    head, tail = dy[0:SLAB], u[t - SLAB:]
    r = _iota(head.shape, 0)
    wrap2 = jnp.sum(jnp.where(r < 2, head * pltpu.roll(tail, 2, 0), 0.0), axis=0, keepdims=True)
    wrap1 = jnp.sum(jnp.where(r < 1, head * pltpu.roll(tail, 1, 0), 0.0), axis=0, keepdims=True)
    d0 = jnp.sum(dy * u2, axis=0, keepdims=True) - wrap2
    d1 = jnp.sum(dy * u1, axis=0, keepdims=True) - wrap1
    d2 = jnp.sum(dy * u, axis=0, keepdims=True)
    r3 = _iota((3, dy.shape[1]), 0)
    return jnp.where(r3 == 0, d0, jnp.where(r3 == 1, d1, d2))


def _sigmoid(x):
    return 0.5 * jnp.tanh(0.5 * x) + 0.5


def _sconv_fwd(proj, w_sc, t, wc):
    nb = wc // LANE

    def body(cb_ref, cc_ref, ch_ref, w_ref, y_ref):
        u = cc_ref[...] * ch_ref[...]
        y_ref[...] = (cb_ref[...] * _conv(u, w_ref[...])).astype(BF16)

    col = lambda off: pl.BlockSpec((t, LANE), lambda j: (0, j + off))
    return pl.pallas_call(
        body, name="sconv_fwd", grid=(nb,),
        in_specs=[col(0), col(nb), col(2 * nb), pl.BlockSpec((3, LANE), lambda j: (0, j))],
        out_specs=pl.BlockSpec((None, t, LANE), lambda j: (0, 0, j)),
        out_shape=jax.ShapeDtypeStruct((2, t, wc), BF16),
        compiler_params=_params("parallel"),
    )(proj, proj, proj, w_sc)


def _sconv_bwd(dy, proj, w_sc, t, wc):
    nb = wc // LANE

    def body(dy_ref, cb_ref, cc_ref, ch_ref, w_ref, dcb_ref, dcc_ref, dch_ref, dw_ref):
        cc, ch, w, d = cc_ref[...], ch_ref[...], w_ref[...], dy_ref[...]
        u = cc * ch
        ru = _rolled(u)
        dcb_ref[...] = (d * _conv(u, w, ru)).astype(BF16)
        dcu = d * cb_ref[...]
        dw_ref[...] = _conv_dw(dcu, u, ru)
        du = _conv_t(dcu, w)
        dcc_ref[...] = (du * ch).astype(BF16)
        dch_ref[...] = (du * cc).astype(BF16)

    col = lambda off: pl.BlockSpec((t, LANE), lambda j: (0, j + off))
    act = jax.ShapeDtypeStruct((t, wc), BF16)
    return pl.pallas_call(
        body, name="sconv_bwd", grid=(nb,),
        in_specs=[col(0), col(0), col(nb), col(2 * nb), pl.BlockSpec((3, LANE), lambda j: (0, j))],
        out_specs=[col(0), col(0), col(0), pl.BlockSpec((3, LANE), lambda j: (0, j))],
        out_shape=[act, act, act, jax.ShapeDtypeStruct((3, wc), F32)],
        compiler_params=_params("parallel"),
    )(dy, proj, proj, proj, w_sc)


def _gates_prep(proj, bias_tile, t, gate_tile):
    def body(g_ref, b_ref, o_ref):
        g = g_ref[...] + b_ref[...]
        lane = _iota(g.shape, 1)
        is_f = (lane >= NH) & (lane < 2 * NH)
        lf = jnp.minimum(g, 0.0) - jnp.log(1.0 + jnp.exp(-jnp.abs(g)))
        c = jnp.where(is_f, lf, 0.0)
        r = _iota(g.shape, 0) % CHUNK
        s = 1
        while s < CHUNK:
            c = c + jnp.where(r >= s, pltpu.roll(c, s, 0), 0.0)
            s *= 2
        o_ref[...] = jnp.where(is_f, c, jnp.where(lane < NH, g, 0.0))

    return pl.pallas_call(
        body, name="gates_prep", grid=(1,),
        in_specs=[pl.BlockSpec((t, LANE), lambda i: (0, gate_tile)), pl.BlockSpec((1, LANE), lambda i: (0, 0))],
        out_specs=pl.BlockSpec((t, LANE), lambda i: (0, 0)),
        out_shape=jax.ShapeDtypeStruct((t, LANE), F32),
        compiler_params=_params("arbitrary"),
    )(proj, bias_tile)


def _gates_bwd(dgate, proj, bias_tile, d_proj, t, gate_tile):
    def body(dg_ref, g_ref, b_ref, d_proj_in, o_ref, s_ref):
        g = g_ref[...] + b_ref[...]
        lane = _iota(g.shape, 1)
        r = _iota(g.shape, 0) % CHUNK
        dsig = 1.0 - _sigmoid(g)
        out = jnp.zeros(g.shape, F32)
        for h in range(NH):
            d = dg_ref[h]
            c = d
            s = 1
            while s < CHUNK:
                c = c + jnp.where(r + s < CHUNK, pltpu.roll(c, t - s, 0), 0.0)
                s *= 2
            di = jnp.broadcast_to(d[:, 0:1], g.shape)
            db = jnp.broadcast_to(c[:, 1:2], g.shape)
            out = out + jnp.where(lane == h, di, 0.0) + jnp.where(lane == NH + h, db * dsig, 0.0)
        o_ref[...] = out.astype(BF16)
        s_ref[...] = jnp.sum(out, axis=0, keepdims=True)

    return pl.pallas_call(
        body, name="gates_bwd", grid=(1,),
        in_specs=[pl.BlockSpec((NH, t, LANE), lambda i: (0, 0, 0)),
                  pl.BlockSpec((t, LANE), lambda i: (0, gate_tile)), pl.BlockSpec((1, LANE), lambda i: (0, 0)),
                  pl.BlockSpec(memory_space=pl.ANY)],
        out_specs=[pl.BlockSpec((t, LANE), lambda i: (0, gate_tile)), pl.BlockSpec((1, LANE), lambda i: (0, 0))],
        out_shape=[jax.ShapeDtypeStruct(d_proj.shape, d_proj.dtype), jax.ShapeDtypeStruct((1, LANE), F32)],
        input_output_aliases={3: 0},
        compiler_params=_params("arbitrary"),
    )(dgate, proj, bias_tile, d_proj)


def _in_turn(heads):
    while heads:
        heads = [g for g in heads if next(g, heads) is not heads]


def _chunk_gates(gc, gr, h, mprev):
    L = CHUNK
    icol, bcol = gc[:, h:h + 1], gc[:, h + NH:h + NH + 1]
    irow, brow = gr[h:h + 1, :], gr[h + NH:h + NH + 1, :]
    tri = _iota((L, L), 0) >= _iota((L, L), 1)
    log_d = jnp.where(tri, bcol - brow + irow, -jnp.inf)
    inter = bcol + mprev
    mt = jnp.maximum(inter, jnp.max(log_d, axis=1, keepdims=True))
    dw = jnp.exp(log_d - mt)
    iw = jnp.exp(inter - mt)
    g = brow[:, L - 1:L]
    wlog_col = g - bcol + icol
    wlog_row = g - brow + irow
    mnew = jnp.maximum(g + mprev, jnp.max(wlog_row, axis=1, keepdims=True))
    wcol = jnp.exp(wlog_col - mnew)
    decay = jnp.exp(g + mprev - mnew)
    return dw, iw, mt, wcol, decay, mnew


def _mlstm_fwd(proj, gcol, grow, t, wc, dh):
    nc = t // CHUNK
    wm = NH * dh
    assert wc == wm, (wc, wm)
    qoff = 3 * wc // wm
    scale = dh ** -0.5

    def body(q_ref, k_ref, v_ref, gc_ref, gr_ref, h_ref, cs_ref, ns_ref, c_s, n_s, m_s):
        @pl.when(pl.program_id(0) == 0)
        def _():
            c_s[...] = jnp.zeros_like(c_s)
            n_s[...] = jnp.zeros_like(n_s)
            m_s[...] = jnp.zeros_like(m_s)

        gc, gr = gc_ref[...], gr_ref[0]
        done = [None] * NH

        def head(h):
            cols = slice(h * dh, (h + 1) * dh)
            mprev = m_s[h, 0:1, 0:1]
            cprev = c_s[h]
            n8 = n_s[h]
            nprev = n8[0:1]
            qs = q_ref[:, cols] * scale
            k = k_ref[:, cols]
            qs_b, k_b, v_b = qs.astype(BF16), k.astype(BF16), v_ref[:, cols].astype(BF16)
            qk = _dot(qs_b, k_b, _NT)
            yield
            q_c = _dot(qs_b, cprev.astype(BF16))
            yield
            dw, iw, mt, wcol, decay, mnew = _chunk_gates(gc, gr, h, mprev)
            yield
            s = qk * dw
            wk = wcol * k
            num = _dot(s.astype(BF16), v_b) + iw * q_c
            yield
            c_new = decay * cprev + _dot(wk.astype(BF16), v_b, _TN)
            yield
            den = jnp.sum(s, axis=1, keepdims=True) + iw * jnp.sum(qs * nprev, axis=1, keepdims=True)
            done[h] = (cprev, jnp.where(_iota(n8.shape, 0) == 1, mprev, n8),
                       num / jnp.maximum(jnp.abs(den), jnp.exp(-mt)), c_new,
                       decay * n8 + jnp.sum(wk, axis=0, keepdims=True), mnew)

        _in_turn([head(h) for h in range(NH)])
        for h, (c_old, n_old, h_out, c_new, n_new, m_new) in enumerate(done):
            cs_ref[h] = c_old
            ns_ref[h] = n_old
            h_ref[:, h * dh:(h + 1) * dh] = h_out
            c_s[h] = c_new
            n_s[h] = n_new
            m_s[h] = jnp.broadcast_to(m_new, m_s.shape[1:])

    grp = lambda off: pl.BlockSpec((CHUNK, wm), lambda c: (c, qoff + off))
    return pl.pallas_call(
        body, name="mlstm_fwd", grid=(nc,),
        in_specs=[grp(0), grp(1), grp(2),
                  pl.BlockSpec((CHUNK, LANE), lambda c: (c, 0)),
                  pl.BlockSpec((1, 8, CHUNK), lambda c: (c, 0, 0))],
        out_specs=[pl.BlockSpec((CHUNK, wm), lambda c: (c, 0)),
                   pl.BlockSpec((NH, None, dh, dh), lambda c: (0, c, 0, 0)),
                   pl.BlockSpec((NH, None, 8, dh), lambda c: (0, c, 0, 0))],
        out_shape=[jax.ShapeDtypeStruct((t, wm), F32),
                   jax.ShapeDtypeStruct((NH, nc, dh, dh), F32),
                   jax.ShapeDtypeStruct((NH, nc, 8, dh), F32)],
        scratch_shapes=[pltpu.VMEM((NH, dh, dh), F32), pltpu.VMEM((NH, 8, dh), F32), pltpu.VMEM((NH, 8, LANE), F32)],
        compiler_params=_params("arbitrary"),
    )(proj, proj, proj, gcol, grow)


def _mlstm_bwd(proj, gcol, grow, hval, dh_in, cs, ns, d_proj, t, wc, dh):
    nc = t // CHUNK
    wm = NH * dh
    assert wc == wm, (wc, wm)
    qoff = 3 * wc // wm
    scale = dh ** -0.5
    L = CHUNK

    def body(q_ref, k_ref, v_ref, gc_ref, gr_ref, h_ref, dh_ref, cs_ref, ns_ref, d_proj_in,
             dqkv_ref, dg_ref, dc_s, dn_s):
        @pl.when(pl.program_id(0) == 0)
        def _():
            dc_s[...] = jnp.zeros_like(dc_s)
            dn_s[...] = jnp.zeros_like(dn_s)

        gc, gr = gc_ref[...], gr_ref[0]
        eye = _iota((L, L), 0) == _iota((L, L), 1)
        lane = _iota((L, LANE), 1)
        last = _iota((L, 1), 0) == L - 1
        done = [None] * NH

        def head(h):
            cols = slice(h * dh, (h + 1) * dh)
            ns8 = ns_ref[h]
            nprev = ns8[0:1]
            mprev = ns8[1:2, 0:1]
            cprev = cs_ref[h]
            dcn = dc_s[h]
            dn8 = dn_s[h]
            dnn = dn8[0:1]

            qs = q_ref[:, cols] * scale
            k = k_ref[:, cols]
            qs_b, k_b, v_b = qs.astype(BF16), k.astype(BF16), v_ref[:, cols].astype(BF16)
            qk = _dot(qs_b, k_b, _NT)
            yield
            dw, iw, mt, wcol, decay, _ = _chunk_gates(gc, gr, h, mprev)
            yield
            s = qk * dw
            den = jnp.sum(s, axis=1, keepdims=True) + iw * jnp.sum(qs * nprev, axis=1, keepdims=True)
            emt = jnp.exp(-mt)
            r = 1.0 / jnp.maximum(jnp.abs(den), emt)
            dout = dh_ref[:, cols]
            dnum = dout * r
            dden = (-jnp.sum(dout * h_ref[:, cols], axis=1, keepdims=True) * r
                    * jnp.where(jnp.abs(den) > emt, jnp.sign(den), 0.0))
            dnum_b = dnum.astype(BF16)
            cprev_b = cprev.astype(BF16)
            dcn_b = dcn.astype(BF16)
            yield

            g_raw = _dot(dnum_b, v_b, _NT)
            yield
            q_inter = _dot(dnum_b, cprev_b, _NT)
            yield
            k_raw = _dot(v_b, dcn_b, _NT)
            yield
            gd = (g_raw + dden) * dw
            gd_b = gd.astype(BF16)
            dqs_inter = iw * (q_inter + dden * nprev)
            dk_inter = wcol * (k_raw + dnn)
            wk = wcol * k
            iq = iw * qs
            dqs = _dot(gd_b, k_b) + dqs_inter
            yield
            dk = _dot(gd_b, qs_b, _TN) + dk_inter
            yield
            dv = _dot(s.astype(BF16), dnum_b, _TN) + _dot(wk.astype(BF16), dcn_b)
            yield
            dc_new = decay * dcn + _dot(iq.astype(BF16), dnum_b, _TN)
            yield

            e = gd * qk
            e_cols = jnp.sum(jnp.where(eye, jnp.sum(e, axis=0, keepdims=True), 0.0), axis=1, keepdims=True)
            yield
            k_inter = jnp.sum(k * dk_inter, axis=1, keepdims=True)
            rq = jnp.sum(e, axis=1, keepdims=True) + jnp.sum(qs * dqs_inter, axis=1, keepdims=True)
            rk = e_cols + k_inter
            hsum = jnp.sum(k_inter, axis=0, keepdims=True)
            jdec = decay * (jnp.sum(jnp.sum(dcn * cprev, axis=1, keepdims=True), axis=0, keepdims=True)
                            + jnp.sum(dnn * nprev, axis=1, keepdims=True))
            db = rq - rk + jnp.where(last, hsum + jdec, 0.0)
            done[h] = (jnp.where(lane == 0, rk, jnp.where(lane == 1, db, 0.0)),
                       (dqs * scale).astype(BF16), dk.astype(BF16), dv.astype(BF16), dc_new,
                       decay * dn8 + jnp.sum(iq * dden, axis=0, keepdims=True))

        _in_turn([head(h) for h in range(NH)])
        for h, (dgate, dq, dk, dv, dc_new, dn_new) in enumerate(done):
            dg_ref[h] = dgate
            for part, grad in enumerate((dq, dk, dv)):
                dqkv_ref[:, part * wm + h * dh:part * wm + (h + 1) * dh] = grad
            dc_s[h] = dc_new
            dn_s[h] = dn_new

    rc = lambda c: nc - 1 - c
    grp = lambda off: pl.BlockSpec((L, wm), lambda c: (rc(c), qoff + off))
    hm = pl.BlockSpec((L, wm), lambda c: (rc(c), 0))
    assert qoff % 3 == 0, qoff
    return pl.pallas_call(
        body, name="mlstm_bwd", grid=(nc,),
        in_specs=[grp(0), grp(1), grp(2),
                  pl.BlockSpec((L, LANE), lambda c: (rc(c), 0)),
                  pl.BlockSpec((1, 8, L), lambda c: (rc(c), 0, 0)),
                  hm, hm,
                  pl.BlockSpec((NH, None, dh, dh), lambda c: (0, rc(c), 0, 0)),
                  pl.BlockSpec((NH, None, 8, dh), lambda c: (0, rc(c), 0, 0)),
                  pl.BlockSpec(memory_space=pl.ANY)],
        out_specs=[pl.BlockSpec((L, 3 * wm), lambda c: (rc(c), qoff // 3)),
                   pl.BlockSpec((NH, L, LANE), lambda c: (0, rc(c), 0))],
        out_shape=[jax.ShapeDtypeStruct(d_proj.shape, d_proj.dtype), jax.ShapeDtypeStruct((NH, t, LANE), F32)],
        input_output_aliases={9: 0},
        scratch_shapes=[pltpu.VMEM((NH, dh, dh), F32), pltpu.VMEM((NH, 8, dh), F32)],
        compiler_params=_params("arbitrary"),
    )(proj, proj, proj, gcol, grow, hval, dh_in, cs, ns, d_proj)


def _head_norm(hv):
    mu = jnp.mean(hv, axis=1, keepdims=True)
    hc = hv - mu
    rstd = lax.rsqrt(jnp.mean(hc * hc, axis=1, keepdims=True) + HN_EPS)
    return hc * rstd, rstd


def _hnorm_fwd(hval, proj, gain, y, t, wc, dh, tr=512):
    ooff = 3 * wc // dh + 3 * NH
    tr = min(tr, t)

    def body(h_ref, o_ref, g_ref, y_in, y_ref):
        hhat, _ = _head_norm(h_ref[...])
        y_ref[...] = (_sigmoid(o_ref[...]) * hhat * g_ref[...]).astype(BF16)

    return pl.pallas_call(
        body, name="hnorm_fwd", grid=(t // tr, NH),
        in_specs=[pl.BlockSpec((tr, dh), lambda i, h: (i, h)),
                  pl.BlockSpec((tr, dh), lambda i, h: (i, ooff + h)),
                  pl.BlockSpec((1, dh), lambda i, h: (0, h)),
                  pl.BlockSpec(memory_space=pl.ANY)],
        out_specs=pl.BlockSpec((None, tr, dh), lambda i, h: (1, i, h)),
        out_shape=jax.ShapeDtypeStruct(y.shape, BF16),
        input_output_aliases={3: 0},
        compiler_params=_params("parallel", "parallel"),
    )(hval, proj, gain, y)


def _hnorm_bwd(dy, hval, proj, gain, t, wc, dh, tr=512):
    ooff = 3 * wc // dh + 3 * NH
    tr = min(tr, t)
    yoff = wc // dh

    def body(dy_ref, h_ref, o_ref, g_ref, do_ref, dh_ref, dg_ref):
        i = pl.program_id(1)
        hhat, rstd = _head_norm(h_ref[...])
        gain_v = g_ref[...]
        sig = _sigmoid(o_ref[...])
        d = dy_ref[...]
        do_ref[...] = (d * hhat * gain_v * sig * (1.0 - sig)).astype(BF16)
        dhn = d * sig
        part = jnp.sum(dhn * hhat, axis=0, keepdims=True)

        @pl.when(i == 0)
        def _():
            dg_ref[...] = part

        @pl.when(i > 0)
        def _():
            dg_ref[...] += part

        dhat = dhn * gain_v
        dh_ref[...] = rstd * (dhat - jnp.mean(dhat, axis=1, keepdims=True)
                              - hhat * jnp.mean(dhat * hhat, axis=1, keepdims=True))

    blk = lambda off: pl.BlockSpec((tr, dh), lambda h, i: (i, off + h))
    return pl.pallas_call(
        body, name="hnorm_bwd", grid=(NH, t // tr),
        in_specs=[blk(yoff), blk(0), blk(ooff), pl.BlockSpec((1, dh), lambda h, i: (0, h))],
        out_specs=[blk(ooff), blk(0), pl.BlockSpec((1, dh), lambda h, i: (0, h))],
        out_shape=[jax.ShapeDtypeStruct(proj.shape, BF16), jax.ShapeDtypeStruct((t, NH * dh), F32),
                   jax.ShapeDtypeStruct((1, NH * dh), F32)],
        compiler_params=_params("parallel", "arbitrary"),
    )(dy, hval, proj, gain)


def _ln_stats(z):
    mu = jnp.mean(z, axis=1, keepdims=True)
    zc = z - mu
    rstd = lax.rsqrt(jnp.mean(zc * zc, axis=1, keepdims=True) + LN_EPS)
    return zc * rstd, rstd


def _ln_bwd(dy, xhat, rstd, g):
    dxh = dy * g
    return rstd * (dxh - jnp.mean(dxh, axis=1, keepdims=True) - xhat * jnp.mean(dxh * xhat, axis=1, keepdims=True))


def _accum(ref, i, part):
    @pl.when(i == 0)
    def _():
        ref[...] = part

    @pl.when(i > 0)
    def _():
        ref[...] += part


def _ln1_fwd(x, mix, g, b, tr=256):
    t, d = x.shape

    def body(x_ref, m_ref, g_ref, b_ref, xh_ref, rs_ref, xb_ref):
        xhat, rstd = _ln_stats(ALPHA * x_ref[...] + m_ref[...])
        xh_ref[...] = xhat
        rs_ref[...] = rstd
        xb_ref[...] = (xhat * g_ref[...] + b_ref[...]).astype(BF16)

    row = pl.BlockSpec((tr, d), lambda i: (i, 0))
    vec = pl.BlockSpec((1, d), lambda i: (0, 0))
    return pl.pallas_call(
        body, name="ln1_fwd", grid=(t // tr,),
        in_specs=[row, row, vec, vec],
        out_specs=[row, pl.BlockSpec((tr, 1), lambda i: (i, 0)), row],
        out_shape=[jax.ShapeDtypeStruct((t, d), F32), jax.ShapeDtypeStruct((t, 1), F32),
                   jax.ShapeDtypeStruct((t, d), BF16)],
        compiler_params=_params("parallel"),
    )(x, mix, g, b)


def _ln2_loss(xhat1, g1, b1, ff, target, g2, b2, tr=256):
    t, d = ff.shape

    def body(xh_ref, g1_ref, b1_ref, f_ref, t_ref, g_ref, b_ref, dz_ref, dzb_ref, dg_ref, db_ref, l_ref):
        i = pl.program_id(0)
        x1 = xh_ref[...] * g1_ref[...] + b1_ref[...]
        xhat, rstd = _ln_stats(ALPHA * x1 + f_ref[...])
        gv = g_ref[...]
        e = xhat * gv + b_ref[...] - t_ref[...]
        lsum = jnp.sum(jnp.sum(e * e, axis=1, keepdims=True), axis=0, keepdims=True) * (0.5 / d)
        dy = e * (1.0 / d)
        _accum(dg_ref, i, jnp.sum(dy * xhat, axis=0, keepdims=True))
        _accum(db_ref, i, jnp.sum(dy, axis=0, keepdims=True))
        _accum(l_ref, i, jnp.broadcast_to(lsum, l_ref.shape))
        dz = _ln_bwd(dy, xhat, rstd, gv)
        dz_ref[...] = dz
        dzb_ref[...] = dz.astype(BF16)

    row = pl.BlockSpec((tr, d), lambda i: (i, 0))
    vec = pl.BlockSpec((1, d), lambda i: (0, 0))
    return pl.pallas_call(
        body, name="ln2_loss", grid=(t // tr,),
        in_specs=[row, vec, vec, row, row, vec, vec],
        out_specs=[row, row, vec, vec, pl.BlockSpec((8, LANE), lambda i: (0, 0))],
        out_shape=[jax.ShapeDtypeStruct((t, d), F32), jax.ShapeDtypeStruct((t, d), BF16),
                   jax.ShapeDtypeStruct((1, d), F32), jax.ShapeDtypeStruct((1, d), F32),
                   jax.ShapeDtypeStruct((8, LANE), F32)],
        compiler_params=_params("arbitrary"),
    )(xhat1, g1, b1, ff, target, g2, b2)


def _ln1_bwd(dz2, dffn, xhat1, rstd1, g1, tr=256):
    t, d = dz2.shape

    def body(a_ref, f_ref, xh_ref, rs_ref, g_ref, dz_ref, dzb_ref, dg_ref, db_ref):
        i = pl.program_id(0)
        dy = ALPHA * a_ref[...] + f_ref[...]
        xhat = xh_ref[...]
        _accum(dg_ref, i, jnp.sum(dy * xhat, axis=0, keepdims=True))
        _accum(db_ref, i, jnp.sum(dy, axis=0, keepdims=True))
        dz = _ln_bwd(dy, xhat, rs_ref[...], g_ref[...])
        dz_ref[...] = dz
        dzb_ref[...] = dz.astype(BF16)

    row = pl.BlockSpec((tr, d), lambda i: (i, 0))
    vec = pl.BlockSpec((1, d), lambda i: (0, 0))
    return pl.pallas_call(
        body, name="ln1_bwd", grid=(t // tr,),
        in_specs=[row, row, row, pl.BlockSpec((tr, 1), lambda i: (i, 0)), vec],
        out_specs=[row, row, vec, vec],
        out_shape=[jax.ShapeDtypeStruct((t, d), F32), jax.ShapeDtypeStruct((t, d), BF16),
                   jax.ShapeDtypeStruct((1, d), F32), jax.ShapeDtypeStruct((1, d), F32)],
        compiler_params=_params("arbitrary"),
    )(dz2, dffn, xhat1, rstd1, g1)


def _ffn_act_fwd(hid0, w_fc, b_fc, t, dff):
    nb = dff // LANE

    def body(hv_ref, hg_ref, wv_ref, wg_ref, bv_ref, bg_ref, a_ref):
        val = _conv(hv_ref[...], wv_ref[...]) + bv_ref[...]
        gate = _conv(hg_ref[...], wg_ref[...]) + bg_ref[...]
        a_ref[...] = (gate * _sigmoid(gate) * val).astype(BF16)

    col = lambda off: pl.BlockSpec((t, LANE), lambda j: (0, j + off))
    w3 = lambda off: pl.BlockSpec((3, LANE), lambda j: (0, j + off))
    w1 = lambda off: pl.BlockSpec((1, LANE), lambda j: (0, j + off))
    return pl.pallas_call(
        body, name="ffn_act_fwd", grid=(nb,),
        in_specs=[col(0), col(nb), w3(0), w3(nb), w1(0), w1(nb)],
        out_specs=col(0),
        out_shape=jax.ShapeDtypeStruct((t, dff), BF16),
        compiler_params=_params("parallel"),
    )(hid0, hid0, w_fc, w_fc, b_fc, b_fc)


def _ffn_act_bwd(da, hid0, w_fc, b_fc, t, dff):
    nb = dff // LANE

    def body(da_ref, hv_ref, hg_ref, wv_ref, wg_ref, bv_ref, bg_ref,
             dhv_ref, dhg_ref, dwv_ref, dwg_ref, dbv_ref, dbg_ref):
        hv, hg, wv, wg = hv_ref[...], hg_ref[...], wv_ref[...], wg_ref[...]
        rv, rg = _rolled(hv), _rolled(hg)
        val = _conv(hv, wv, rv) + bv_ref[...]
        gate = _conv(hg, wg, rg) + bg_ref[...]
        sig = _sigmoid(gate)
        d = da_ref[...]
        dsig = d * sig
        dval = dsig * gate
        dgate = dsig * val * (1.0 + gate * (1.0 - sig))
        dhv_ref[...] = _conv_t(dval, wv).astype(BF16)
        dhg_ref[...] = _conv_t(dgate, wg).astype(BF16)
        dwv_ref[...] = _conv_dw(dval, hv, rv)
        dwg_ref[...] = _conv_dw(dgate, hg, rg)
        dbv_ref[...] = jnp.sum(dval, axis=0, keepdims=True)
        dbg_ref[...] = jnp.sum(dgate, axis=0, keepdims=True)

    col = lambda off: pl.BlockSpec((t, LANE), lambda j: (0, j + off))
    w3 = lambda off: pl.BlockSpec((3, LANE), lambda j: (0, j + off))
    w1 = lambda off: pl.BlockSpec((1, LANE), lambda j: (0, j + off))
    s3 = jax.ShapeDtypeStruct((3, dff), F32)
    s1 = jax.ShapeDtypeStruct((1, dff), F32)
    return pl.pallas_call(
        body, name="ffn_act_bwd", grid=(nb,),
        in_specs=[col(0), col(0), col(nb), w3(0), w3(nb), w1(0), w1(nb)],
        out_specs=[col(0), col(0), w3(0), w3(0), w1(0), w1(0)],
        out_shape=[jax.ShapeDtypeStruct((t, dff), BF16)] * 2 + [s3, s3, s1, s1],
        compiler_params=_params("parallel"),
    )(da, hid0, hid0, w_fc, w_fc, b_fc, b_fc)


class _Ready:
    def __init__(self, **weights):
        self.weights = weights

    def begin(self, after):
        return None

    def forward(self, name, after):
        return None

    def get(self, name, after):
        return self.weights[name]


class _Kept:
    def __init__(self):
        self.grads = {}

    def start(self, name, grad):
        self.grads[name] = grad
        return None

    def relay(self, name, after):
        return None

    def meanwhile(self, small, loss, after):
        return None


def _behind(a, token):
    return a if token is None else a + token[0:1, 0:1].reshape((1,) * a.ndim)


def _local_step(x, target, w_in, b_gates, w_sc, gain, w_out, ln1_g, ln1_b, w_up, w_fc, b_fc, w_down, ln2_g, ln2_b,
                gx=None, wx=None, x_b=None):
    t, d = x.shape
    wc = d // 2
    dh = (d - wc) // NH
    wm = NH * dh
    dff = w_fc.shape[1] // 2
    if wx is None:
        wx = _Ready(w_out=w_out, w_up=w_up, w_down=w_down)
    ninp = 3 * wc + 4 * wm + LANE
    nin = 3 * wc + 4 * wm
    gate_tile = nin // LANE
    nc = t // CHUNK
    bias_tile = jnp.pad(b_gates, ((0, 0), (0, LANE - 2 * NH)))

    if x_b is None:
        x_b = x.astype(BF16)
    proj = _matmul(x_b, w_in, "nt", F32, "proj", tm=512, tn=2432, tk=d, n=ninp, after=wx.begin(w_in))
    y = _sconv_fwd(proj, w_sc, t, wc)
    gcol = _gates_prep(proj, bias_tile, t, gate_tile)
    grow = gcol[:, :8].T.reshape(8, nc, CHUNK).transpose(1, 0, 2)
    hval, cs, ns = _mlstm_fwd(proj, gcol, grow, t, wc, dh)
    y = _hnorm_fwd(hval, proj, gain, y, t, wc, dh)
    tok = wx.forward("w_out", y)
    w_out = wx.get("w_out", tok)
    mix = _matmul(y, w_out, "nn", F32, "out_proj", tm=512, tn=1024, tk=wc, a_blocked=True, after=tok)
    xhat1, rstd1, x1_b = _ln1_fwd(x, mix, _behind(ln1_g, wx.forward("w_up", mix)), ln1_b)
    w_up = wx.get("w_up", x1_b)
    wsl = w_up.shape[2]
    hid0 = _matmul(x1_b, w_up, "nn", F32, "ffn_up", tm=1024, tn=wsl, tk=d, b_blocked=True)
    act = _ffn_act_fwd(hid0, w_fc, _behind(b_fc, wx.forward("w_down", hid0)), t, dff)
    w_down = wx.get("w_down", act)
    ff = _matmul(act, w_down, "nn", F32, "ffn_down", tm=1024, tn=512, tk=dff)
    dz2, dz2_b, d_ln2_g, d_ln2_b, loss = _ln2_loss(xhat1, ln1_g, ln1_b, ff, target, ln2_g, ln2_b)

    if gx is None:
        gx = _Kept()
    d_w_down = _matmul(act, dz2_b, "tn", BF16, "ffn_down_dw", tm=1408, tn=1024, tk=t)
    d_act = _matmul(dz2_b, w_down, "nt", F32, "ffn_down_dx", tm=2048, tn=512, tk=d, after=gx.start("w_down", d_w_down))
    *d_hid0, dwv, dwg, dbv, dbg = _ffn_act_bwd(d_act, hid0, w_fc, _behind(b_fc, gx.relay("w_down", d_act)), t, dff)
    d_w_fc = jnp.concatenate([dwv, dwg], axis=1)
    d_b_fc = jnp.concatenate([dbv, dbg], axis=1)
    d_hid0 = tuple(d_hid0[:2])
    d_w_up = _matmul(x1_b, d_hid0, "tn", BF16, "ffn_up_dw", tm=1024, tn=wsl, tk=t, o_width=wsl)
    d_x1_ffn = _matmul(d_hid0, w_up, "nt", F32, "ffn_up_dx", tm=1024, tn=1024, tk=wsl, b_blocked=True,
                       after=gx.start("w_up", d_w_up))
    dz1, dz1_b, d_ln1_g, d_ln1_b = _ln1_bwd(dz2, d_x1_ffn, xhat1, rstd1, _behind(ln1_g, gx.relay("w_up", d_x1_ffn)))

    d_w_out = _matmul(y, dz1_b, "tn", BF16, "out_proj_dw", tm=1024, tn=1024, tk=t, a_blocked=True)
    dy = _matmul(dz1_b, w_out, "nt", F32, "out_proj_dx", tm=1024, tn=1024, tk=d, after=gx.start("w_out", d_w_out))
    dcb, dcc, dch, d_w_sc = _sconv_bwd(dy, proj, _behind(w_sc, gx.relay("w_out", dy)), t, wc)
    d_proj, d_hval, d_gain = _hnorm_bwd(dy, hval, proj, gain, t, wc, dh)
    d_proj, dgate = _mlstm_bwd(proj, gcol, grow, hval, d_hval, cs, ns, d_proj, t, wc, dh)
    d_proj, d_b_gates = _gates_bwd(dgate, proj, bias_tile, d_proj, t, gate_tile)
    for part, grad in enumerate((dcb, dcc, dch)):
        d_proj = lax.dynamic_update_slice(d_proj, grad, (0, part * wc))
    d_w_in = _matmul(d_proj, x_b, "tn", BF16, "proj_dw", tm=2432, tn=1024, tk=t)
    small = dict(b_gates=d_b_gates[:, :2 * NH], w_sc_conv=d_w_sc, mh_gain=d_gain, ln1_g=d_ln1_g, ln1_b=d_ln1_b,
                 w_ffn_conv=d_w_fc, b_ffn_conv=d_b_fc, ln2_g=d_ln2_g, ln2_b=d_ln2_b)
    token = gx.start("w_in", d_w_in)
    token = gx.relay("w_in", gx.meanwhile(small, loss, token))
    grad_x = _matmul(d_proj, w_in, "nn", F32, "proj_dx", tm=512, tn=512, tk=ninp, add=dz1, add_scale=ALPHA, after=token)
    return loss, grad_x, small, gx


HBM = pl.BlockSpec(memory_space=pltpu.HBM)


def _place():
    return lax.axis_index("x"), lax.axis_index("y"), lax.axis_index("c")


def _index(p):
    return 4 * p[0] + 2 * p[1] + p[2]


def _all_gather(arrs, name):
    n = len(arrs)

    def body(*refs):
        ins, outs = refs[:n], refs[n:2 * n]
        send_sems, recv_sems, local_sems = refs[2 * n:]
        x, y, c = _place()
        me, sibling = (x, y, c), (x, y, 1 - c)
        chips = [(1 - x, y), (x, 1 - y), (1 - x, 1 - y)]

        def copy(a, k, block, to, own=False):
            dst = outs[a].at[_index(block)]
            return pltpu.make_async_remote_copy(
                src_ref=ins[a] if own else dst, dst_ref=dst,
                send_sem=send_sems.at[k * n + a], recv_sem=recv_sems.at[k * n + a],
                device_id=to, device_id_type=MESH)

        mine = [pltpu.make_async_copy(ins[a], outs[a].at[_index(me)], local_sems.at[a]) for a in range(n)]
        for cp in mine:
            cp.start()
        first = []
        for a in range(n):
            first.append(copy(a, 0, me, sibling, own=True))
            first += [copy(a, 1 + j, me, (*chip, c), own=True) for j, chip in enumerate(chips)]
        for cp in first:
            cp.start()
        passed = []
        for j, chip in enumerate(chips):
            for a in range(n):
                copy(a, 1 + j, (*chip, c), me).wait_recv()
                cp = copy(a, 4 + j, (*chip, c), sibling)
                cp.start()
                passed.append(cp)
        for a in range(n):
            copy(a, 0, sibling, me).wait_recv()
            for j, chip in enumerate(chips):
                copy(a, 4 + j, (*chip, 1 - c), me).wait_recv()
        for cp in first + passed:
            cp.wait_send()
        for cp in mine:
            cp.wait()

    return pl.pallas_call(
        body, name=name, in_specs=[HBM] * n, out_specs=[HBM] * n,
        out_shape=[jax.ShapeDtypeStruct((N_DEV,) + a.shape, a.dtype) for a in arrs],
        scratch_shapes=[pltpu.SemaphoreType.DMA((7 * n,)), pltpu.SemaphoreType.DMA((7 * n,)),
                        pltpu.SemaphoreType.DMA((n,))],
    )(*arrs)


SEM = pl.BlockSpec(memory_space=pltpu.SEMAPHORE)
EFFECT = pltpu.SideEffectType.DATAFLOW_SIDE_EFFECTING


def _chips(x, y):
    return [(1 - x, y), (x, 1 - y), (1 - x, 1 - y)]


N_CHIP = N_DEV // 2


def _pair_route(x, y, c):
    return [((x, y, 1 - c), 2 * q + (1 - c), q, q) for q in range(N_CHIP)]


def _chip_route(x, y, c):
    mine = 2 * x + y
    return [((*chip, c), 2 * chip[0] + chip[1], mine, 2 * chip[0] + chip[1]) for chip in _chips(x, y)]


def _exchange_pieces(g_ref, land_ref, width, tail):
    if not tail:
        return [(lambda i: g_ref.at[i], lambda s: land_ref.at[s])]
    rows = lambda i, n: pl.ds(pl.multiple_of(i * width, IN_TAIL), n)
    return [(lambda i: g_ref.at[rows(i, width), :], lambda s: land_ref.at[s, pl.ds(0, width), :]),
            (lambda i: g_ref.at[rows(i + 1, IN_TAIL), :], lambda s: land_ref.at[s, pl.ds(width, IN_TAIL), :])]


def _chip_slot(x, y, c):
    return 2 * x + y


def _exchange_start(grad, route, tail, name, own_slot=None):
    width = IN_SLAB if tail else grad.shape[1]
    n_p = 2 if tail else 1
    n_c = len(route(0, 0, 0))
    land_shape = (N_CHIP, width + (IN_TAIL if tail else 0), grad.shape[-1])
    assert not (tail and own_slot)

    def body(g_ref, land_ref, send_sems, recv_sems, g_thru, land_thru, token):
        for j, (peer, slab, slot, _) in enumerate(route(*_place())):
            for p, (src, dst) in enumerate(_exchange_pieces(g_ref, land_ref, width, tail)):
                pltpu.make_async_remote_copy(src_ref=src(slab), dst_ref=dst(slot), send_sem=send_sems.at[j * n_p + p],
                                             recv_sem=recv_sems.at[j * n_p + p], device_id=peer,
                                             device_id_type=MESH).start()
        if own_slot:
            mine = own_slot(*_place())
            pltpu.make_async_copy(g_ref.at[mine], land_ref.at[mine], send_sems.at[n_c * n_p]).start()
        token[...] = jnp.zeros_like(token)

    return pl.pallas_call(
        body, name=name,
        out_shape=(pltpu.SemaphoreType.DMA((n_c * n_p + bool(own_slot),)), pltpu.SemaphoreType.DMA((n_c * n_p,)),
                   pltpu.HBM(grad.shape, grad.dtype), pltpu.HBM(land_shape, grad.dtype),
                   jax.ShapeDtypeStruct((8, LANE), F32)),
        in_specs=(HBM, HBM), out_specs=(SEM, SEM, HBM, HBM, pl.BlockSpec(memory_space=pltpu.VMEM)),
        input_output_aliases={0: 2, 1: 3},
        compiler_params=pltpu.CompilerParams(has_side_effects=EFFECT),
    )(pltpu.with_memory_space_constraint(grad, pltpu.HBM),
      pltpu.with_memory_space_constraint(lax.empty(land_shape, grad.dtype), pltpu.HBM))


def _exchange_wait(send_sems, recv_sems, g_thru, land_thru, after, route, tail, name, own_slot=None):
    width = IN_SLAB if tail else g_thru.shape[1]
    n_p = 2 if tail else 1

    def body(g_ref, land_ref, send_sems, recv_sems, after_ref, g_dead, got_ref):
        places = route(*_place())
        for j, (peer, slab, _, slot) in enumerate(places):
            for p, (src, dst) in enumerate(_exchange_pieces(g_ref, land_ref, width, tail)):
                cp = pltpu.make_async_remote_copy(src_ref=src(slab), dst_ref=dst(slot),
                                                  send_sem=send_sems.at[j * n_p + p], recv_sem=recv_sems.at[j * n_p + p],
                                                  device_id=peer, device_id_type=MESH)
                cp.wait_send()
                cp.wait_recv()
        if own_slot:
            mine = own_slot(*_place())
            pltpu.make_async_copy(g_ref.at[mine], land_ref.at[mine], send_sems.at[len(places) * n_p]).wait()

    return pl.pallas_call(
        body, name=name,
        out_shape=(pltpu.HBM(g_thru.shape, g_thru.dtype), pltpu.HBM(land_thru.shape, land_thru.dtype)),
        in_specs=(HBM, HBM, SEM, SEM, pl.BlockSpec(memory_space=pl.ANY)), out_specs=(HBM, HBM),
        input_output_aliases={0: 0, 1: 1},
        compiler_params=pltpu.CompilerParams(has_side_effects=EFFECT),
    )(g_thru, land_thru, send_sems, recv_sems, after)


def _pair_add(grad, pair, core, tail, name):
    rows, cols = (IN_SLAB if tail else grad.shape[1]), grad.shape[-1]
    total = pair.shape[1]

    def body(core_ref, *refs):
        if tail:
            g_ref, t_ref, p_ref, o_ref = refs
            o_ref[0:rows, :] = (g_ref[...].astype(F32) + p_ref[0:rows, :].astype(F32)).astype(BF16)
            o_ref[rows:total, :] = (t_ref[...].astype(F32) + p_ref[rows:total, :].astype(F32)).astype(BF16)
        else:
            g_ref, p_ref, o_ref = refs
            o_ref[...] = (g_ref[...].astype(F32) + p_ref[...].astype(F32)).astype(BF16)

    if tail:
        tc = _fit(cols, 512)
        grid = (N_CHIP, cols // tc)
        slab = pl.BlockSpec((None, total, tc), lambda q, i, core_ref: (q, 0, i))
        per = IN_SLAB // IN_TAIL
        in_specs = [pl.BlockSpec((rows, tc), lambda q, i, core_ref: (2 * q + core_ref[0], i)),
                    pl.BlockSpec((IN_TAIL, tc), lambda q, i, core_ref: ((2 * q + core_ref[0] + 1) * per, i))]
    else:
        tr = _rows(rows, 1024)
        grid = (N_CHIP, rows // tr)
        slab = pl.BlockSpec((None, tr, cols), lambda q, i, core_ref: (q, i, 0))
        in_specs = [pl.BlockSpec((None, tr, cols), lambda q, i, core_ref: (2 * q + core_ref[0], i, 0))]
    return pl.pallas_call(
        body, name=name,
        grid_spec=pltpu.PrefetchScalarGridSpec(num_scalar_prefetch=1, grid=grid,
                                               in_specs=in_specs + [slab], out_specs=slab),
        out_shape=jax.ShapeDtypeStruct(pair.shape, BF16),
        compiler_params=_params("parallel", "parallel"),
    )(core, *([grad, grad] if tail else [grad]), pair)


def _relay_places(x, y, c):
    came_from = (c * (1 - x) + (1 - c) * x, c * y + (1 - c) * (1 - y), c)
    pass_to = (c * x + (1 - c) * (1 - x), c * (1 - y) + (1 - c) * y, c)
    return 2 - c, came_from, pass_to, pass_to


OWN = 4


def _gather_start(blocks, after, name, spare=(), relayed=False):
    n = len(blocks)
    lands = [(N_DEV + (a in spare),) + b.shape for a, b in enumerate(blocks)]

    def body(*refs):
        b_refs, land_refs = refs[:n], refs[n:2 * n]
        send_sems, recv_sems = refs[2 * n + 1:3 * n + 1], refs[3 * n + 1:4 * n + 1]
        token = refs[-1]
        x, y, c = _place()
        me = _index((x, y, c))
        for a in range(n):
            targets = [(x, y, 1 - c)] + [(*chip, c) for chip in _chips(x, y)]
            for k, to in enumerate(targets[:3] if relayed else targets):
                pltpu.make_async_remote_copy(src_ref=b_refs[a], dst_ref=land_refs[a].at[me], send_sem=send_sems[a].at[k],
                                             recv_sem=recv_sems[a].at[k], device_id=to, device_id_type=MESH).start()
        for a in range(n):
            pltpu.make_async_copy(b_refs[a], land_refs[a].at[me], send_sems[a].at[OWN]).start()
        token[...] = jnp.zeros_like(token)

    sems = [pltpu.SemaphoreType.DMA((OWN + 1,))] * n
    out = pl.pallas_call(
        body, name=name,
        out_shape=(*sems, *sems, *[pltpu.HBM(b.shape, b.dtype) for b in blocks],
                   *[pltpu.HBM(s, b.dtype) for s, b in zip(lands, blocks)], jax.ShapeDtypeStruct((8, LANE), F32)),
        in_specs=(*[HBM] * (2 * n), pl.BlockSpec(memory_space=pl.ANY)),
        out_specs=(*[SEM] * (2 * n), *[HBM] * (2 * n), pl.BlockSpec(memory_space=pltpu.VMEM)),
        input_output_aliases={i: 2 * n + i for i in range(2 * n)},
        compiler_params=pltpu.CompilerParams(has_side_effects=EFFECT),
    )(*[pltpu.with_memory_space_constraint(b, pltpu.HBM) for b in blocks],
      *[pltpu.with_memory_space_constraint(lax.empty(s, b.dtype), pltpu.HBM) for s, b in zip(lands, blocks)], after)
    return [(out[a], out[n + a], out[2 * n + a], out[3 * n + a]) for a in range(n)], out[-1]


def _gather_relay(states, after, name):
    n, first_out = len(states), 3 * len(states) + len(after)

    def body(*refs):
        land_refs, send_sems, recv_sems = refs[:n], refs[n:2 * n], refs[2 * n:3 * n]
        pass_send, pass_recv = refs[first_out + n:first_out + 2 * n], refs[first_out + 2 * n:first_out + 3 * n]
        k_in, came_from, pass_to, _ = _relay_places(*_place())
        for a in range(n):
            slot = land_refs[a].at[_index(came_from)]
            pltpu.make_async_remote_copy(src_ref=slot, dst_ref=slot, send_sem=send_sems[a].at[k_in],
                                         recv_sem=recv_sems[a].at[k_in], device_id=came_from,
                                         device_id_type=MESH).wait_recv()
        for a in range(n):
            slot = land_refs[a].at[_index(came_from)]
            pltpu.make_async_remote_copy(src_ref=slot, dst_ref=slot, send_sem=pass_send[a].at[0],
                                         recv_sem=pass_recv[a].at[0], device_id=pass_to, device_id_type=MESH).start()
        refs[-1][...] = jnp.zeros_like(refs[-1])

    lands = [st[3] for st in states]
    pair = [pltpu.SemaphoreType.DMA((1,))] * n
    out = pl.pallas_call(
        body, name=name,
        out_shape=(*[pltpu.HBM(l.shape, l.dtype) for l in lands], *pair, *pair, jax.ShapeDtypeStruct((8, LANE), F32)),
        in_specs=(*[HBM] * n, *[SEM] * (2 * n), *[pl.BlockSpec(memory_space=pl.ANY)] * len(after)),
        out_specs=(*[HBM] * n, *[SEM] * (2 * n), pl.BlockSpec(memory_space=pltpu.VMEM)),
        input_output_aliases={a: a for a in range(n)},
        compiler_params=pltpu.CompilerParams(has_side_effects=EFFECT),
    )(*lands, *[st[0] for st in states], *[st[1] for st in states], *after)
    return [(st[0], st[1], st[2], out[a], (out[n + a], out[2 * n + a])) for a, st in enumerate(states)], out[-1]


def _gather_forward(send_sems, recv_sems, b_thru, land_thru, after, name, passed=None):
    relayed = passed is not None

    def body(b_ref, land_ref, send_sems, recv_sems, *rest):
        pass_send, pass_recv = rest[:2] if relayed else (None, None)
        send2, recv2, token = rest[-3:]
        x, y, c = _place()
        sibling = (x, y, 1 - c)
        arrivals = [sibling] + [(*chip, c) for chip in _chips(x, y)]
        waits = [(send_sems.at[k], recv_sems.at[k], frm) for k, frm in enumerate(arrivals)]
        sends = [send_sems.at[k] for k in range(4)]
        if relayed:
            k_in, _, _, other = _relay_places(x, y, c)
            waits = [waits[0], (send_sems.at[3 - k_in], recv_sems.at[3 - k_in], other),
                     (pass_send.at[0], pass_recv.at[0], arrivals[3])]
            sends[3] = pass_send.at[0]
        for sem in sends:
            pltpu.make_async_remote_copy(src_ref=b_ref, dst_ref=land_ref.at[0], send_sem=sem, recv_sem=recv_sems.at[0],
                                         device_id=sibling, device_id_type=MESH).wait_send()
        pltpu.make_async_copy(b_ref, land_ref.at[_index((x, y, c))], send_sems.at[OWN]).wait()
        for send_sem, recv_sem, frm in waits:
            pltpu.make_async_remote_copy(src_ref=b_ref, dst_ref=land_ref.at[_index(frm)], send_sem=send_sem,
                                         recv_sem=recv_sem, device_id=frm, device_id_type=MESH).wait_recv()
        for j, chip in enumerate(_chips(x, y)):
            slot = land_ref.at[_index((*chip, c))]
            pltpu.make_async_remote_copy(src_ref=slot, dst_ref=slot, send_sem=send2.at[j], recv_sem=recv2.at[j],
                                         device_id=sibling, device_id_type=MESH).start()
        token[...] = jnp.zeros_like(token)

    extra = list(passed) if relayed else []
    return pl.pallas_call(
        body, name=name,
        out_shape=(pltpu.HBM(b_thru.shape, b_thru.dtype), pltpu.HBM(land_thru.shape, land_thru.dtype),
                   pltpu.SemaphoreType.DMA((3,)), pltpu.SemaphoreType.DMA((3,)), jax.ShapeDtypeStruct((8, LANE), F32)),
        in_specs=(HBM, HBM, SEM, SEM, *[SEM] * len(extra), pl.BlockSpec(memory_space=pl.ANY)),
        out_specs=(HBM, HBM, SEM, SEM, pl.BlockSpec(memory_space=pltpu.VMEM)),
        input_output_aliases={0: 0, 1: 1},
        compiler_params=pltpu.CompilerParams(has_side_effects=EFFECT),
    )(b_thru, land_thru, send_sems, recv_sems, *extra, after)


def _gather_finish(land_thru, send2, recv2, after, name):
    def body(land_ref, send2, recv2, after_ref, land_out):
        x, y, c = _place()
        for j, chip in enumerate(_chips(x, y)):
            cp = pltpu.make_async_remote_copy(src_ref=land_ref.at[_index((*chip, c))],
                                              dst_ref=land_ref.at[_index((*chip, 1 - c))], send_sem=send2.at[j],
                                              recv_sem=recv2.at[j], device_id=(x, y, 1 - c), device_id_type=MESH)
            cp.wait_send()
            cp.wait_recv()

    return pl.pallas_call(
        body, name=name, out_shape=pltpu.HBM(land_thru.shape, land_thru.dtype),
        in_specs=(HBM, SEM, SEM, pl.BlockSpec(memory_space=pl.ANY)), out_specs=HBM,
        input_output_aliases={0: 0},
        compiler_params=pltpu.CompilerParams(has_side_effects=EFFECT),
    )(land_thru, send2, recv2, after)


class _Gathering:
    def __init__(self, ahead, later, me):
        cast = [a.astype(BF16) for a in ahead.values()]
        started, self.token = _gather_start(cast, cast[0], "gather1_ahead", relayed=True)
        self.me, self.state, self.relayed, self.later = me, dict(zip(ahead, started)), tuple(ahead), later

    def start_first(self, first):
        started, self.token = _gather_start(list(first.values()), self.token, "gather1_first", spare=(0,), relayed=True)
        self.state.update(zip(first, started))
        self.relayed += tuple(first)

    def begin(self, after):
        return self.token

    def relay(self, *after):
        states, token = _gather_relay([self.state[n] for n in self.relayed], after, "gather_relay")
        self.state.update(zip(self.relayed, states))
        cast = [_behind(a, token).astype(BF16) for a in self.later.values()]
        started, self.token = _gather_start(cast, token, "gather1_later")
        self.state.update(zip(self.later, started))
        return self.token

    def forward(self, name, after):
        first_leg, passed = self.state[name][:4], (self.state[name][4:] or (None,))[0]
        *self.state[name], token = _gather_forward(*first_leg, after, "gather2_" + name, passed=passed)
        return token

    def get(self, name, after):
        _, land, send2, recv2 = self.state[name]
        land = _gather_finish(land, send2, recv2, after, "gather3_" + name)
        return land if name not in ("w_out", "w_down") else land.reshape(-1, land.shape[2])


class _Reducing:
    def __init__(self, core, chip, gather_small):
        self.core, self.chip, self.state, self.token, self.gather_small = core, chip, {}, None, gather_small

    def meanwhile(self, small, loss, after):
        self.small_sum = self.gather_small(small, loss, after)
        return self.small_sum

    def start(self, name, grad):
        tail = name == "w_in"
        g = grad if tail or grad.ndim == 3 else grad.reshape(N_DEV, grad.shape[0] // N_DEV, grad.shape[1])
        *self.state[name], token = _exchange_start(g, _pair_route, tail, "pair_send_" + name)
        return token

    def relay(self, name, after):
        tail = name == "w_in"
        grad, pair = _exchange_wait(*self.state[name], after, _pair_route, tail, "pair_recv_" + name)
        total = _pair_add(grad, pair, self.core, tail, "pair_add_" + name)
        *self.state[name], self.token = _exchange_start(total, _chip_route, False, "chip_send_" + name,
                                                        own_slot=_chip_slot)
        return self.token

    def finish(self, name, after):
        _, land = _exchange_wait(*self.state[name], after, _chip_route, False, "chip_recv_" + name, own_slot=_chip_slot)
        return land


def _carry_w_in(main, tail):
    slabs, _, d = main.shape
    tc = _fit(d, 2048)
    assert slabs == N_DEV + 1 and tail.shape[:2] == (N_DEV, IN_TAIL), (main.shape, tail.shape)
    top = lambda off: pl.BlockSpec((None, IN_TAIL, tc), lambda s, j: (s + off, 0, j))

    def carry(m_ref, t_ref, o_ref):
        o_ref[...] = m_ref[...] + t_ref[...]

    main = pl.pallas_call(
        carry, name="carry_w_in", grid=(N_DEV - 1, d // tc), in_specs=[top(1), top(0)], out_specs=top(1),
        out_shape=jax.ShapeDtypeStruct(main.shape, main.dtype), input_output_aliases={0: 0},
        compiler_params=_params("parallel", "parallel"),
    )(main, tail)

    def last(m_ref, t_ref, o_ref):
        o_ref[...] = jnp.zeros_like(o_ref)
        o_ref[0:IN_TAIL, :] = t_ref[...]

    return pl.pallas_call(
        last, name="last_slab_w_in", grid=(d // tc,),
        in_specs=[pl.BlockSpec(memory_space=pl.ANY), pl.BlockSpec((None, IN_TAIL, tc), lambda j: (N_DEV - 1, 0, j))],
        out_specs=pl.BlockSpec((None, LANE, tc), lambda j: (N_DEV, 0, j)),
        out_shape=jax.ShapeDtypeStruct(main.shape, main.dtype), input_output_aliases={0: 0},
        compiler_params=_params("parallel"),
    )(main, tail)


def _rows(n, want):
    t = min(n, want)
    t -= t % 16
    while n % t:
        t -= 16
    return t


def _adam_math(w, g, m, v):
    m2 = ADAM_B1 * m + (1.0 - ADAM_B1) * g
    v2 = ADAM_B2 * v + (1.0 - ADAM_B2) * (g * g)
    m_hat = m2 * (1.0 / (1.0 - ADAM_B1 ** ADAM_STEP))
    v_hat = v2 * (1.0 / (1.0 - ADAM_B2 ** ADAM_STEP))
    return -ADAM_LR * (m_hat / (jnp.sqrt(v_hat) + ADAM_EPS) + ADAM_WD * w), m2, v2


def _slot_sum(r_ref):
    acc = r_ref[0].astype(F32)
    for i in range(1, r_ref.shape[0]):
        acc = acc + r_ref[i].astype(F32)
    return acc


def _shift_w_in(w):
    ws, d = w.shape
    tc = _fit(d, 256)

    def body(w_ref, main_ref, tail_ref, tall):
        tall[...] = jnp.zeros_like(tall)
        tall[0:ws, :] = w_ref[...]
        moved = pltpu.roll(tall[...], _index(_place()), 0).astype(BF16)
        main_ref[...] = moved[0:IN_SLAB]
        tail_ref[...] = moved[IN_SLAB:]

    return pl.pallas_call(
        body, name="shift_w_in", grid=(d // tc,),
        in_specs=[pl.BlockSpec((ws, tc), lambda j: (0, j))],
        out_specs=[pl.BlockSpec((IN_SLAB, tc), lambda j: (0, j)), pl.BlockSpec((IN_TAIL, tc), lambda j: (0, j))],
        out_shape=[jax.ShapeDtypeStruct((IN_SLAB, d), BF16), jax.ShapeDtypeStruct((IN_TAIL, d), BF16)],
        scratch_shapes=[pltpu.VMEM((IN_SLAB + IN_TAIL, tc), F32)], compiler_params=_params("parallel"),
    )(w)


def _sum_adamw_shifted(r, w, m, v, name):
    _, ph, d = r.shape
    ws = w.shape[0]
    tc = _fit(d, 256)

    def body(r_ref, w_ref, m_ref, v_ref, g_ref, d_ref, m2_ref, v2_ref, tall):
        tall[...] = pltpu.roll(_slot_sum(r_ref), lax.rem(ph - _index(_place()), ph), 0)
        g = tall[0:ws, :]
        g_ref[...] = g
        d_ref[...], m2_ref[...], v2_ref[...] = _adam_math(w_ref[...], g, m_ref[...], v_ref[...])

    blk = pl.BlockSpec((ws, tc), lambda j: (0, j))
    out = jax.ShapeDtypeStruct(w.shape, F32)
    return pl.pallas_call(
        body, name=name, grid=(d // tc,),
        in_specs=[pl.BlockSpec((r.shape[0], ph, tc), lambda j: (0, 0, j)), blk, blk, blk],
        out_specs=[blk] * 4, out_shape=[out] * 4,
        scratch_shapes=[pltpu.VMEM((ph, tc), F32)], compiler_params=_params("parallel"),
    )(r, w, m, v)


def _sum_slots(r, name, tr=128):
    _, rows, cols = r.shape
    tr = _rows(rows, tr)

    def body(r_ref, g_ref):
        g_ref[...] = _slot_sum(r_ref)

    return pl.pallas_call(
        body, name=name, grid=(rows // tr,),
        in_specs=[pl.BlockSpec((r.shape[0], tr, cols), lambda i: (0, i, 0))],
        out_specs=pl.BlockSpec((tr, cols), lambda i: (i, 0)),
        out_shape=jax.ShapeDtypeStruct((rows, cols), F32),
        compiler_params=_params("parallel"),
    )(r)


def _adamw(w, g, m, v, name, tr=256):
    rows, cols = w.shape
    tr = _rows(rows, tr)

    def body(w_ref, g_ref, m_ref, v_ref, d_ref, m2_ref, v2_ref):
        d_ref[...], m2_ref[...], v2_ref[...] = _adam_math(w_ref[...], g_ref[...], m_ref[...], v_ref[...])

    blk = pl.BlockSpec((tr, cols), lambda i: (i, 0))
    out = jax.ShapeDtypeStruct((rows, cols), F32)
    return pl.pallas_call(
        body, name=name, grid=(rows // tr,), in_specs=[blk] * 4, out_specs=[blk] * 3, out_shape=[out] * 3,
        compiler_params=_params("parallel"),
    )(w, g, m, v)


def _sum_adamw(r, w, m, v, name, tr=256):
    rows, cols = w.shape
    tr = _rows(rows, tr)

    def body(r_ref, w_ref, m_ref, v_ref, g_ref, d_ref, m2_ref, v2_ref):
        g = _slot_sum(r_ref)
        g_ref[...] = g
        d_ref[...], m2_ref[...], v2_ref[...] = _adam_math(w_ref[...], g, m_ref[...], v_ref[...])

    blk = pl.BlockSpec((tr, cols), lambda i: (i, 0))
    out = jax.ShapeDtypeStruct((rows, cols), F32)
    return pl.pallas_call(
        body, name=name, grid=(rows // tr,),
        in_specs=[pl.BlockSpec((r.shape[0], tr, cols), lambda i: (0, i, 0)), blk, blk, blk],
        out_specs=[blk] * 4, out_shape=[out] * 4,
        compiler_params=_params("parallel"),
    )(r, w, m, v)


def _pack(pieces, sizes):
    flat = [jnp.pad(p.reshape(-1).astype(F32), (0, s - p.size)) for p, s in zip(pieces, sizes)]
    total = sum(sizes)
    padded = -(-total // (16 * LANE)) * (16 * LANE)
    return jnp.pad(jnp.concatenate(flat), (0, padded - total)).reshape(-1, LANE)


def _unpack(packed, shapes, sizes):
    flat = packed.reshape(-1)
    out, off = [], 0
    for shp, s in zip(shapes, sizes):
        n = 1
        for k in shp:
            n *= k
        out.append(flat[off:off + n].reshape(shp))
        off += s
    return out


def _lanes(n):
    return -(-n // LANE) * LANE


WEIGHTS = ("w_in", "b_gates", "w_sc_conv", "mh_gain", "w_out", "ln1_g", "ln1_b", "w_up", "w_ffn_conv", "b_ffn_conv",
           "w_down", "ln2_g", "ln2_b")
BIG = ("w_in", "w_out", "w_up", "w_down")
SMALL = tuple(n for n in WEIGHTS if n not in BIG)


def kernel(x, w_in, b_gates, w_sc_conv, mh_gain, w_out, ln1_g, ln1_b, w_up, w_ffn_conv, b_ffn_conv, w_down, ln2_g, ln2_b, loss_target, m_w_in, m_b_gates, m_w_sc_conv, m_mh_gain, m_w_out, m_ln1_g, m_ln1_b, m_w_up, m_w_ffn_conv, m_b_ffn_conv, m_w_down, m_ln2_g, m_ln2_b, v_w_in, v_b_gates, v_w_sc_conv, v_mh_gain, v_w_out, v_ln1_g, v_ln1_b, v_w_up, v_w_ffn_conv, v_b_ffn_conv, v_w_down, v_ln2_g, v_ln2_b):
    w = dict(zip(WEIGHTS, (w_in, b_gates, w_sc_conv, mh_gain, w_out, ln1_g, ln1_b, w_up, w_ffn_conv, b_ffn_conv,
                           w_down, ln2_g, ln2_b)))
    m = dict(zip(WEIGHTS, (m_w_in, m_b_gates, m_w_sc_conv, m_mh_gain, m_w_out, m_ln1_g, m_ln1_b, m_w_up,
                           m_w_ffn_conv, m_b_ffn_conv, m_w_down, m_ln2_g, m_ln2_b)))
    v = dict(zip(WEIGHTS, (v_w_in, v_b_gates, v_w_sc_conv, v_mh_gain, v_w_out, v_ln1_g, v_ln1_b, v_w_up,
                           v_w_ffn_conv, v_b_ffn_conv, v_w_down, v_ln2_g, v_ln2_b)))
    me = _index(_place())
    d = x.shape[2]
    ws_in = w_in.shape[2]
    assert ws_in == IN_SLAB + 1 and N_DEV <= LANE, w_in.shape
    ninp = (N_DEV + 1) * IN_SLAB
    ws_sc, ws_fc = w_sc_conv.shape[2], w_ffn_conv.shape[2]

    wx = _Gathering({"w_out": w_out[0]}, {n: w[n][0] for n in ("w_up", "w_down")}, me)
    w_in_t = jnp.transpose(_behind(w_in[0], wx.token))
    w_in_main, w_in_tail = _shift_w_in(w_in_t)
    taps8 = lambda a: jnp.pad(a[0], ((0, 5), (0, 0)))
    at_once = ("w_in", "w_tail", "w_sc", "w_fc")
    wx.start_first(dict(zip(at_once, (w_in_main, w_in_tail, taps8(w_sc_conv), taps8(w_ffn_conv)))))
    x_b = _behind(x[0], wx.begin(None)).astype(BF16)
    m_in_t, v_in_t = (jnp.transpose(_behind(a[0], wx.begin(None))) for a in (m_w_in, v_w_in))
    token = wx.relay(x_b, m_in_t, v_in_t)
    for n in at_once:
        token = wx.forward(n, token)
    g_in, g_tail, g_sc, g_fc = (wx.get(n, token) for n in at_once)
    w_in_full = _carry_w_in(g_in, g_tail).reshape(ninp, d)
    w_sc_full = g_sc[:, :3].transpose(1, 0, 2).reshape(3, N_DEV * ws_sc)
    w_fc_full = g_fc[:, :3].transpose(1, 0, 2).reshape(3, N_DEV * ws_fc)

    xi, yi, ci = _place()
    names = ("loss",) + SMALL
    pieces = {}

    def gather_small(small, loss_t, after):
        pieces.update(small, loss=loss_t[0, :1])
        sizes = [_lanes(pieces[n].size) for n in names]
        (g_small,) = _all_gather([_behind(_pack([pieces[n] for n in names], sizes), after)], "gather_small")
        return _sum_slots(g_small, "sum_small", tr=g_small.shape[1])

    gx = _Reducing(jnp.reshape(ci, (1,)).astype(jnp.int32), 2 * xi + yi, gather_small)
    loss_t, grad_x, small, _ = _local_step(
        x[0], loss_target[0], w_in_full, b_gates, w_sc_full, mh_gain, None, ln1_g, ln1_b, None,
        w_fc_full, b_ffn_conv, None, ln2_g, ln2_b, gx=gx, wx=wx, x_b=x_b)

    grads, deltas, new_m, new_v = {}, {}, {}, {}
    for name in ("w_down", "w_up", "w_out"):
        grads[name], deltas[name], new_m[name], new_v[name] = _sum_adamw(
            gx.finish(name, gx.token), w[name][0], m[name][0], v[name][0], "adamw_" + name)

    summed = _unpack(gx.small_sum, [pieces[n].shape for n in names], [_lanes(pieces[n].size) for n in names])
    full = dict(zip(names, summed))
    full["w_sc_conv"] = lax.dynamic_slice(full["w_sc_conv"], (0, me * ws_sc), (3, ws_sc))
    full["w_ffn_conv"] = lax.dynamic_slice(full["w_ffn_conv"], (0, me * ws_fc), (3, ws_fc))
    for n in SMALL:
        grads[n] = full[n].reshape(w[n].shape)
    sizes = [_lanes(w[n].size) for n in SMALL]
    shapes = [w[n].shape for n in SMALL]
    packed = [_pack([t[n] for n in SMALL], sizes) for t in (w, grads, m, v)]
    small_out = _adamw(*packed, "adamw_small")
    for res, t in zip(small_out, (deltas, new_m, new_v)):
        t.update(zip(SMALL, _unpack(res, shapes, sizes)))

    done = sum(t[0:1, 0:1] for t in (grad_x, deltas["w_down"], deltas["w_up"], deltas["w_out"], small_out[0]))
    grads["w_in"], deltas["w_in"], new_m["w_in"], new_v["w_in"] = (
        jnp.transpose(a)[None] for a in _sum_adamw_shifted(gx.finish("w_in", done), w_in_t, m_in_t, v_in_t, "adamw_w_in"))

    big = lambda t: {n: (t[n].reshape(w[n].shape) if n in BIG else t[n]) for n in WEIGHTS}
    grads, deltas, new_m, new_v = big(grads), big(deltas), big(new_m), big(new_v)
    return (full["loss"].reshape(()), grad_x[None], *[grads[n] for n in WEIGHTS], *[deltas[n] for n in WEIGHTS],
            *[new_m[n] for n in WEIGHTS], *[new_v[n] for n in WEIGHTS])
```

```python
import functools

import jax
import jax.numpy as jnp
from jax import lax
from jax.experimental import pallas as pl
from jax.experimental.pallas import tpu as pltpu

F32 = jnp.float32
BF16 = jnp.bfloat16
MESH = pl.DeviceIdType.MESH

N_DEV = 8
NH = 4
CHUNK = 64
LN_EPS = 1e-5
HN_EPS = 1e-6
ALPHA = 2.0 ** 0.25
LANE = 128
IN_SLAB = 7 * LANE
IN_TAIL = 16
VMEM_LIMIT = 56 * 1024 * 1024
ADAM_LR, ADAM_B1, ADAM_B2, ADAM_EPS, ADAM_WD, ADAM_STEP = 0.001, 0.9, 0.999, 1e-08, 0.01, 10

_NN = (((1,), (0,)), ((), ()))
_NT = (((1,), (1,)), ((), ()))
_TN = (((0,), (0,)), ((), ()))


def _dot(a, b, dn=_NN):
    return lax.dot_general(a, b, dn, preferred_element_type=F32)


def _params(*sem):
    return pltpu.CompilerParams(dimension_semantics=sem if sem else None, vmem_limit_bytes=VMEM_LIMIT)


def _iota(shape, axis):
    return lax.broadcasted_iota(jnp.int32, shape, axis)


def _fit(n, want):
    if n <= want:
        return n
    t = want - want % LANE
    while n % t:
        t -= LANE
    return t


def _placed(after, body, in_specs, args):
    if after is None:
        return body, in_specs, args
    return (lambda after_ref, *refs: body(*refs)), [pl.BlockSpec(memory_space=pl.ANY)] + in_specs, (after,) + args


def _matmul(a, b, mode, out_dtype, name, tm=1024, tn=512, tk=1024, add=None, add_scale=1.0,
            a_blocked=False, b_blocked=False, o_width=None, after=None, n=None):
    a_parts = a if isinstance(a, tuple) else None
    b_parts = b if isinstance(b, tuple) else None
    if a_parts:
        a_blocked, (a_rows, wa), na = True, a[0].shape, len(a)
        kd, m = (a_rows, na * wa) if mode == "tn" else (na * wa, a_rows)
    elif a_blocked:
        na, a_rows, wa = a.shape
        kd, m = (a_rows, na * wa) if mode == "tn" else (na * wa, a_rows)
    elif mode == "tn":
        kd, m = a.shape
    else:
        m, kd = a.shape
    if b_parts:
        b_blocked, (rows, w), nb = True, b[0].shape, len(b)
    elif b_blocked:
        nb, rows, w = b.shape
    if b_blocked:
        n = rows if mode == "nt" else nb * w
        assert (nb * w if mode == "nt" else rows) == kd, (name, kd)
    else:
        n = n or (b.shape[0] if mode == "nt" else b.shape[1])
    tm, tn, tk = _fit(m, tm), _fit(n, tn), _fit(kd, tk)
    if a_blocked and mode == "tn":
        tm = _fit(wa, tm)
    if a_blocked and mode != "tn":
        tk = _fit(wa, tk)
    if b_blocked and mode != "nt":
        tn = _fit(w, tn)
    if b_blocked and mode == "nt":
        tk = _fit(w, tk)
    if o_width is not None:
        tn = _fit(o_width, tn)
    assert m % tm == 0 and n % tn == 0 and kd % tk == 0, (name, m, n, kd, tm, tn, tk)
    assert not (a_blocked and mode != "tn" and wa % tk) and not (b_blocked and mode == "nt" and w % tk), (name, tk)
    nk = kd // tk
    dn = {"nn": _NN, "nt": _NT, "tn": _TN}[mode]
    if a_blocked and mode == "tn":
        a_per = wa // tm
        a_spec = pl.BlockSpec((None, tk, tm), lambda i, j, k: (i // a_per, k, i % a_per))
    elif a_blocked:
        a_per = wa // tk
        a_spec = pl.BlockSpec((None, tm, tk), lambda i, j, k: (k // a_per, i, k % a_per))
    elif mode == "tn":
        a_spec = pl.BlockSpec((tk, tm), lambda i, j, k: (k, i))
    else:
        a_spec = pl.BlockSpec((tm, tk), lambda i, j, k: (i, k))
    if b_blocked and mode != "nt":
        per = w // tn
        b_spec = pl.BlockSpec((None, tk, tn), lambda i, j, k: (j // per, k, j % per))
    elif b_blocked:
        per = w // tk
        b_spec = pl.BlockSpec((None, tn, tk), lambda i, j, k: (k // per, j, k % per))
    elif mode == "nt":
        b_spec = pl.BlockSpec((tn, tk), lambda i, j, k: (j, k))
    else:
        b_spec = pl.BlockSpec((tk, tn), lambda i, j, k: (k, j))
    if o_width is None:
        o_spec = pl.BlockSpec((tm, tn), lambda i, j, k: (i, j))
        o_shape = (m, n)
    else:
        oper = o_width // tn
        o_spec = pl.BlockSpec((None, tm, tn), lambda i, j, k: (j // oper, i, j % oper))
        o_shape = (n // o_width, m, o_width)
    a_list, a_specs = [a], [a_spec]
    if a_parts:
        hold = lambda x, s: jnp.clip(x - s * a_per, 0, a_per - 1)
        a_list = list(a_parts)
        a_specs = [(pl.BlockSpec((tk, tm), lambda i, j, k, s=s: (k, hold(i, s))) if mode == "tn"
                    else pl.BlockSpec((tm, tk), lambda i, j, k, s=s: (i, hold(k, s)))) for s in range(na)]
    b_list, b_specs = [b], [b_spec]
    if b_parts:
        hold_b = lambda x, s: jnp.clip(x - s * per, 0, per - 1)
        b_list = list(b_parts)
        b_specs = [(pl.BlockSpec((tn, tk), lambda i, j, k, s=s: (j, hold_b(k, s))) if mode == "nt"
                    else pl.BlockSpec((tk, tn), lambda i, j, k, s=s: (k, hold_b(j, s)))) for s in range(nb)]
    n_a, n_b = len(a_list), len(b_list)
    has_add = add is not None
    n_in = n_a + n_b + has_add + (after is not None)
    in_place = nk > 1 and out_dtype == F32

    def body(*refs):
        add_ref = refs[n_a + n_b] if has_add else None
        o_ref = refs[n_in]
        i, j, k = pl.program_id(0), pl.program_id(1), pl.program_id(2)

        def finish(r):
            if has_add:
                r = r + add_scale * add_ref[...]
            o_ref[...] = r.astype(out_dtype)

        def step(a_ref, b_ref):
            if nk == 1:
                finish(_dot(a_ref[...], b_ref[...], dn))
                return
            acc = o_ref if in_place else refs[-1]

            @pl.when(k == 0)
            def _():
                acc[...] = _dot(a_ref[...], b_ref[...], dn)

            @pl.when(k > 0)
            def _():
                acc[...] += _dot(a_ref[...], b_ref[...], dn)

        if n_a == 1 and n_b == 1:
            step(refs[0], refs[1])
        else:
            slab_a = ((i if mode == "tn" else k) // a_per) if n_a > 1 else 0
            slab_b = ((k if mode == "nt" else j) // per) if n_b > 1 else 0
            for sa in range(n_a):
                for sb in range(n_b):
                    pl.when((slab_a == sa) & (slab_b == sb))(functools.partial(step, refs[sa], refs[n_a + sb]))
        if nk > 1 and not (in_place and not has_add):
            @pl.when(k == nk - 1)
            def _():
                finish((o_ref if in_place else refs[-1])[...])

    in_specs = a_specs + b_specs + ([pl.BlockSpec((tm, tn), lambda i, j, k: (i, j))] if has_add else [])
    args = (*a_list, *b_list) + ((add,) if has_add else ())
    if after is not None:
        in_specs.append(pl.BlockSpec(memory_space=pl.ANY))
        args += (after,)
    return pl.pallas_call(
        body, name=name, grid=(m // tm, n // tn, nk),
        in_specs=in_specs, out_specs=o_spec,
        out_shape=jax.ShapeDtypeStruct(o_shape, out_dtype),
        scratch_shapes=[pltpu.VMEM((tm, tn), F32)] if nk > 1 and not in_place else [],
        compiler_params=_params("parallel", "parallel", "arbitrary"),
    )(*args)


def _shift_down(u, s):
    return jnp.where(_iota(u.shape, 0) >= s, pltpu.roll(u, s, 0), 0.0)


def _shift_up(u, s):
    t = u.shape[0]
    return jnp.where(_iota(u.shape, 0) < t - s, pltpu.roll(u, t - s, 0), 0.0)


SLAB = 8


def _rolled(u):
    return pltpu.roll(u, 2, 0), pltpu.roll(u, 1, 0)


def _conv(u, w, rolled=None):
    u2, u1 = _rolled(u) if rolled is None else rolled
    raw = w[0:1] * u2 + w[1:2] * u1 + w[2:3] * u
    head = u[0:SLAB]
    mended = w[0:1] * _shift_down(head, 2) + w[1:2] * _shift_down(head, 1) + w[2:3] * head
    return jnp.concatenate([mended, raw[SLAB:]], axis=0)


def _conv_t(dy, w):
    t = dy.shape[0]
    raw = w[2:3] * dy + w[1:2] * pltpu.roll(dy, t - 1, 0) + w[0:1] * pltpu.roll(dy, t - 2, 0)
    tail = dy[t - SLAB:]
    mended = w[2:3] * tail + w[1:2] * _shift_up(tail, 1) + w[0:1] * _shift_up(tail, 2)
    return jnp.concatenate([raw[:t - SLAB], mended], axis=0)


def _conv_dw(dy, u, rolled=None):
    t = dy.shape[0]
    u2, u1 = _rolled(u) if rolled is None else rolled
    head, tail = dy[0:SLAB], u[t - SLAB:]
    r = _iota(head.shape, 0)
    wrap2 = jnp.sum(jnp.where(r < 2, head * pltpu.roll(tail, 2, 0), 0.0), axis=0, keepdims=True)
    wrap1 = jnp.sum(jnp.where(r < 1, head * pltpu.roll(tail, 1, 0), 0.0), axis=0, keepdims=True)
    d0 = jnp.sum(dy * u2, axis=0, keepdims=True) - wrap2
    d1 = jnp.sum(dy * u1, axis=0, keepdims=True) - wrap1
    d2 = jnp.sum(dy * u, axis=0, keepdims=True)
    r3 = _iota((3, dy.shape[1]), 0)
    return jnp.where(r3 == 0, d0, jnp.where(r3 == 1, d1, d2))


def _sigmoid(x):
    return 0.5 * jnp.tanh(0.5 * x) + 0.5


def _sconv_fwd(proj, w_sc, t, wc):
    nb = wc // LANE

    def body(cb_ref, cc_ref, ch_ref, w_ref, y_ref):
        u = cc_ref[...] * ch_ref[...]
        y_ref[...] = (cb_ref[...] * _conv(u, w_ref[...])).astype(BF16)

    col = lambda off: pl.BlockSpec((t, LANE), lambda j: (0, j + off))
    return pl.pallas_call(
        body, name="sconv_fwd", grid=(nb,),
        in_specs=[col(0), col(nb), col(2 * nb), pl.BlockSpec((3, LANE), lambda j: (0, j))],
        out_specs=pl.BlockSpec((None, t, LANE), lambda j: (0, 0, j)),
        out_shape=jax.ShapeDtypeStruct((2, t, wc), BF16),
        compiler_params=_params("parallel"),
    )(proj, proj, proj, w_sc)


def _sconv_bwd(dy, proj, w_sc, t, wc, after=None):
    nb = wc // LANE

    def body(dy_ref, cb_ref, cc_ref, ch_ref, w_ref, dcb_ref, dcc_ref, dch_ref, dw_ref):
        cc, ch, w, d = cc_ref[...], ch_ref[...], w_ref[...], dy_ref[...]
        u = cc * ch
        ru = _rolled(u)
        dcb_ref[...] = (d * _conv(u, w, ru)).astype(BF16)
        dcu = d * cb_ref[...]
        dw_ref[...] = _conv_dw(dcu, u, ru)
        du = _conv_t(dcu, w)
        dcc_ref[...] = (du * ch).astype(BF16)
        dch_ref[...] = (du * cc).astype(BF16)

    col = lambda off: pl.BlockSpec((t, LANE), lambda j: (0, j + off))
    act = jax.ShapeDtypeStruct((t, wc), BF16)
    body, in_specs, args = _placed(
        after, body, [col(0), col(0), col(nb), col(2 * nb), pl.BlockSpec((3, LANE), lambda j: (0, j))],
        (dy, proj, proj, proj, w_sc))
    return pl.pallas_call(
        body, name="sconv_bwd", grid=(nb,),
        in_specs=in_specs,
        out_specs=[col(0), col(0), col(0), pl.BlockSpec((3, LANE), lambda j: (0, j))],
        out_shape=[act, act, act, jax.ShapeDtypeStruct((3, wc), F32)],
        compiler_params=_params("parallel"),
    )(*args)


def _gates_prep(proj, bias_tile, t, gate_tile):
    def body(g_ref, b_ref, o_ref):
        g = g_ref[...] + b_ref[...]
        lane = _iota(g.shape, 1)
        is_f = (lane >= NH) & (lane < 2 * NH)
        lf = jnp.minimum(g, 0.0) - jnp.log(1.0 + jnp.exp(-jnp.abs(g)))
        c = jnp.where(is_f, lf, 0.0)
        r = _iota(g.shape, 0) % CHUNK
        s = 1
        while s < CHUNK:
            c = c + jnp.where(r >= s, pltpu.roll(c, s, 0), 0.0)
            s *= 2
        o_ref[...] = jnp.where(is_f, c, jnp.where(lane < NH, g, 0.0))

    return pl.pallas_call(
        body, name="gates_prep", grid=(1,),
        in_specs=[pl.BlockSpec((t, LANE), lambda i: (0, gate_tile)), pl.BlockSpec((1, LANE), lambda i: (0, 0))],
        out_specs=pl.BlockSpec((t, LANE), lambda i: (0, 0)),
        out_shape=jax.ShapeDtypeStruct((t, LANE), F32),
        compiler_params=_params("arbitrary"),
    )(proj, bias_tile)


def _gates_bwd(dgate, proj, bias_tile, d_proj, t, gate_tile):
    def body(dg_ref, g_ref, b_ref, d_proj_in, o_ref, s_ref):
        g = g_ref[...] + b_ref[...]
        lane = _iota(g.shape, 1)
        r = _iota(g.shape, 0) % CHUNK
        dsig = 1.0 - _sigmoid(g)
        out = jnp.zeros(g.shape, F32)
        for h in range(NH):
            d = dg_ref[h]
            c = d
            s = 1
            while s < CHUNK:
                c = c + jnp.where(r + s < CHUNK, pltpu.roll(c, t - s, 0), 0.0)
                s *= 2
            di = jnp.broadcast_to(d[:, 0:1], g.shape)
            db = jnp.broadcast_to(c[:, 1:2], g.shape)
            out = out + jnp.where(lane == h, di, 0.0) + jnp.where(lane == NH + h, db * dsig, 0.0)
        o_ref[...] = out.astype(BF16)
        s_ref[...] = jnp.sum(out, axis=0, keepdims=True)

    return pl.pallas_call(
        body, name="gates_bwd", grid=(1,),
        in_specs=[pl.BlockSpec((NH, t, LANE), lambda i: (0, 0, 0)),
                  pl.BlockSpec((t, LANE), lambda i: (0, gate_tile)), pl.BlockSpec((1, LANE), lambda i: (0, 0)),
                  pl.BlockSpec(memory_space=pl.ANY)],
        out_specs=[pl.BlockSpec((t, LANE), lambda i: (0, gate_tile)), pl.BlockSpec((1, LANE), lambda i: (0, 0))],
        out_shape=[jax.ShapeDtypeStruct(d_proj.shape, d_proj.dtype), jax.ShapeDtypeStruct((1, LANE), F32)],
        input_output_aliases={3: 0},
        compiler_params=_params("arbitrary"),
    )(dgate, proj, bias_tile, d_proj)


def _in_turn(heads):
    while heads:
        heads = [g for g in heads if next(g, heads) is not heads]


def _chunk_gates(gc, gr, h, mprev):
    L = CHUNK
    icol, bcol = gc[:, h:h + 1], gc[:, h + NH:h + NH + 1]
    irow, brow = gr[h:h + 1, :], gr[h + NH:h + NH + 1, :]
    tri = _iota((L, L), 0) >= _iota((L, L), 1)
    log_d = jnp.where(tri, bcol - brow + irow, -jnp.inf)
    inter = bcol + mprev
    mt = jnp.maximum(inter, jnp.max(log_d, axis=1, keepdims=True))
    dw = jnp.exp(log_d - mt)
    iw = jnp.exp(inter - mt)
    g = brow[:, L - 1:L]
    wlog_col = g - bcol + icol
    wlog_row = g - brow + irow
    mnew = jnp.maximum(g + mprev, jnp.max(wlog_row, axis=1, keepdims=True))
    wcol = jnp.exp(wlog_col - mnew)
    decay = jnp.exp(g + mprev - mnew)
    return dw, iw, mt, wcol, decay, mnew


def _mlstm_fwd(proj, gcol, grow, t, wc, dh):
    nc = t // CHUNK
    wm = NH * dh
    assert wc == wm, (wc, wm)
    qoff = 3 * wc // wm
    scale = dh ** -0.5

    def body(q_ref, k_ref, v_ref, gc_ref, gr_ref, h_ref, cs_ref, ns_ref, c_s, n_s, m_s):
        @pl.when(pl.program_id(0) == 0)
        def _():
            c_s[...] = jnp.zeros_like(c_s)
            n_s[...] = jnp.zeros_like(n_s)
            m_s[...] = jnp.zeros_like(m_s)

        gc, gr = gc_ref[...], gr_ref[0]
        done = [None] * NH

        def head(h):
            cols = slice(h * dh, (h + 1) * dh)
            mprev = m_s[h, 0:1, 0:1]
            cprev = c_s[h]
            n8 = n_s[h]
            nprev = n8[0:1]
            qs = q_ref[:, cols] * scale
            k = k_ref[:, cols]
            qs_b, k_b, v_b = qs.astype(BF16), k.astype(BF16), v_ref[:, cols].astype(BF16)
            qk = _dot(qs_b, k_b, _NT)
            yield
            q_c = _dot(qs_b, cprev.astype(BF16))
            yield
            dw, iw, mt, wcol, decay, mnew = _chunk_gates(gc, gr, h, mprev)
            yield
            s = qk * dw
            wk = wcol * k
            num = _dot(s.astype(BF16), v_b) + iw * q_c
            yield
            c_new = decay * cprev + _dot(wk.astype(BF16), v_b, _TN)
            yield
            den = jnp.sum(s, axis=1, keepdims=True) + iw * jnp.sum(qs * nprev, axis=1, keepdims=True)
            done[h] = (cprev, jnp.where(_iota(n8.shape, 0) == 1, mprev, n8),
                       num / jnp.maximum(jnp.abs(den), jnp.exp(-mt)), c_new,
                       decay * n8 + jnp.sum(wk, axis=0, keepdims=True), mnew)

        _in_turn([head(h) for h in range(NH)])
        for h, (c_old, n_old, h_out, c_new, n_new, m_new) in enumerate(done):
            cs_ref[h] = c_old
            ns_ref[h] = n_old
            h_ref[:, h * dh:(h + 1) * dh] = h_out
            c_s[h] = c_new
            n_s[h] = n_new
            m_s[h] = jnp.broadcast_to(m_new, m_s.shape[1:])

    grp = lambda off: pl.BlockSpec((CHUNK, wm), lambda c: (c, qoff + off))
    return pl.pallas_call(
        body, name="mlstm_fwd", grid=(nc,),
        in_specs=[grp(0), grp(1), grp(2),
                  pl.BlockSpec((CHUNK, LANE), lambda c: (c, 0)),
                  pl.BlockSpec((1, 8, CHUNK), lambda c: (c, 0, 0))],
        out_specs=[pl.BlockSpec((CHUNK, wm), lambda c: (c, 0)),
                   pl.BlockSpec((NH, None, dh, dh), lambda c: (0, c, 0, 0)),
                   pl.BlockSpec((NH, None, 8, dh), lambda c: (0, c, 0, 0))],
        out_shape=[jax.ShapeDtypeStruct((t, wm), F32),
                   jax.ShapeDtypeStruct((NH, nc, dh, dh), F32),
                   jax.ShapeDtypeStruct((NH, nc, 8, dh), F32)],
        scratch_shapes=[pltpu.VMEM((NH, dh, dh), F32), pltpu.VMEM((NH, 8, dh), F32), pltpu.VMEM((NH, 8, LANE), F32)],
        compiler_params=_params("arbitrary"),
    )(proj, proj, proj, gcol, grow)


def _mlstm_bwd(proj, gcol, grow, hval, dh_in, cs, ns, d_proj, t, wc, dh):
    nc = t // CHUNK
    wm = NH * dh
    assert wc == wm, (wc, wm)
    qoff = 3 * wc // wm
    scale = dh ** -0.5
    L = CHUNK

    def body(q_ref, k_ref, v_ref, gc_ref, gr_ref, h_ref, dh_ref, cs_ref, ns_ref, d_proj_in,
             dqkv_ref, dg_ref, dc_s, dn_s):
        @pl.when(pl.program_id(0) == 0)
        def _():
            dc_s[...] = jnp.zeros_like(dc_s)
            dn_s[...] = jnp.zeros_like(dn_s)

        gc, gr = gc_ref[...], gr_ref[0]
        eye = _iota((L, L), 0) == _iota((L, L), 1)
        lane = _iota((L, LANE), 1)
        last = _iota((L, 1), 0) == L - 1
        done = [None] * NH

        def head(h):
            cols = slice(h * dh, (h + 1) * dh)
            ns8 = ns_ref[h]
            nprev = ns8[0:1]
            mprev = ns8[1:2, 0:1]
            cprev = cs_ref[h]
            dcn = dc_s[h]
            dn8 = dn_s[h]
            dnn = dn8[0:1]

            qs = q_ref[:, cols] * scale
            k = k_ref[:, cols]
            qs_b, k_b, v_b = qs.astype(BF16), k.astype(BF16), v_ref[:, cols].astype(BF16)
            qk = _dot(qs_b, k_b, _NT)
            yield
            dw, iw, mt, wcol, decay, _ = _chunk_gates(gc, gr, h, mprev)
            yield
            s = qk * dw
            den = jnp.sum(s, axis=1, keepdims=True) + iw * jnp.sum(qs * nprev, axis=1, keepdims=True)
            emt = jnp.exp(-mt)
            r = 1.0 / jnp.maximum(jnp.abs(den), emt)
            dout = dh_ref[:, cols]
            dnum = dout * r
            dden = (-jnp.sum(dout * h_ref[:, cols], axis=1, keepdims=True) * r
                    * jnp.where(jnp.abs(den) > emt, jnp.sign(den), 0.0))
            dnum_b = dnum.astype(BF16)
            cprev_b = cprev.astype(BF16)
            dcn_b = dcn.astype(BF16)
            yield

            g_raw = _dot(dnum_b, v_b, _NT)
            yield
            q_inter = _dot(dnum_b, cprev_b, _NT)
            yield
            k_raw = _dot(v_b, dcn_b, _NT)
            yield
            gd = (g_raw + dden) * dw
            gd_b = gd.astype(BF16)
            dqs_inter = iw * (q_inter + dden * nprev)
            dk_inter = wcol * (k_raw + dnn)
            wk = wcol * k
            iq = iw * qs
            dqs = _dot(gd_b, k_b) + dqs_inter
            yield
            dk = _dot(gd_b, qs_b, _TN) + dk_inter
            yield
            dv = _dot(s.astype(BF16), dnum_b, _TN) + _dot(wk.astype(BF16), dcn_b)
            yield
            dc_new = decay * dcn + _dot(iq.astype(BF16), dnum_b, _TN)
            yield

            e = gd * qk
            e_cols = jnp.sum(jnp.where(eye, jnp.sum(e, axis=0, keepdims=True), 0.0), axis=1, keepdims=True)
            yield
            k_inter = jnp.sum(k * dk_inter, axis=1, keepdims=True)
            rq = jnp.sum(e, axis=1, keepdims=True) + jnp.sum(qs * dqs_inter, axis=1, keepdims=True)
            rk = e_cols + k_inter
            hsum = jnp.sum(k_inter, axis=0, keepdims=True)
            jdec = decay * (jnp.sum(jnp.sum(dcn * cprev, axis=1, keepdims=True), axis=0, keepdims=True)
                            + jnp.sum(dnn * nprev, axis=1, keepdims=True))
            db = rq - rk + jnp.where(last, hsum + jdec, 0.0)
            done[h] = (jnp.where(lane == 0, rk, jnp.where(lane == 1, db, 0.0)),
                       (dqs * scale).astype(BF16), dk.astype(BF16), dv.astype(BF16), dc_new,
                       decay * dn8 + jnp.sum(iq * dden, axis=0, keepdims=True))

        _in_turn([head(h) for h in range(NH)])
        for h, (dgate, dq, dk, dv, dc_new, dn_new) in enumerate(done):
            dg_ref[h] = dgate
            for part, grad in enumerate((dq, dk, dv)):
                dqkv_ref[:, part * wm + h * dh:part * wm + (h + 1) * dh] = grad
            dc_s[h] = dc_new
            dn_s[h] = dn_new

    rc = lambda c: nc - 1 - c
    grp = lambda off: pl.BlockSpec((L, wm), lambda c: (rc(c), qoff + off))
    hm = pl.BlockSpec((L, wm), lambda c: (rc(c), 0))
    assert qoff % 3 == 0, qoff
    return pl.pallas_call(
        body, name="mlstm_bwd", grid=(nc,),
        in_specs=[grp(0), grp(1), grp(2),
                  pl.BlockSpec((L, LANE), lambda c: (rc(c), 0)),
                  pl.BlockSpec((1, 8, L), lambda c: (rc(c), 0, 0)),
                  hm, hm,
                  pl.BlockSpec((NH, None, dh, dh), lambda c: (0, rc(c), 0, 0)),
                  pl.BlockSpec((NH, None, 8, dh), lambda c: (0, rc(c), 0, 0)),
                  pl.BlockSpec(memory_space=pl.ANY)],
        out_specs=[pl.BlockSpec((L, 3 * wm), lambda c: (rc(c), qoff // 3)),
                   pl.BlockSpec((NH, L, LANE), lambda c: (0, rc(c), 0))],
        out_shape=[jax.ShapeDtypeStruct(d_proj.shape, d_proj.dtype), jax.ShapeDtypeStruct((NH, t, LANE), F32)],
        input_output_aliases={9: 0},
        scratch_shapes=[pltpu.VMEM((NH, dh, dh), F32), pltpu.VMEM((NH, 8, dh), F32)],
        compiler_params=_params("arbitrary"),
    )(proj, proj, proj, gcol, grow, hval, dh_in, cs, ns, d_proj)


def _head_norm(hv):
    mu = jnp.mean(hv, axis=1, keepdims=True)
    hc = hv - mu
    rstd = lax.rsqrt(jnp.mean(hc * hc, axis=1, keepdims=True) + HN_EPS)
    return hc * rstd, rstd


def _hnorm_fwd(hval, proj, gain, y, t, wc, dh, tr=512):
    ooff = 3 * wc // dh + 3 * NH
    tr = min(tr, t)

    def body(h_ref, o_ref, g_ref, y_in, y_ref):
        hhat, _ = _head_norm(h_ref[...])
        y_ref[...] = (_sigmoid(o_ref[...]) * hhat * g_ref[...]).astype(BF16)

    return pl.pallas_call(
        body, name="hnorm_fwd", grid=(t // tr, NH),
        in_specs=[pl.BlockSpec((tr, dh), lambda i, h: (i, h)),
                  pl.BlockSpec((tr, dh), lambda i, h: (i, ooff + h)),
                  pl.BlockSpec((1, dh), lambda i, h: (0, h)),
                  pl.BlockSpec(memory_space=pl.ANY)],
        out_specs=pl.BlockSpec((None, tr, dh), lambda i, h: (1, i, h)),
        out_shape=jax.ShapeDtypeStruct(y.shape, BF16),
        input_output_aliases={3: 0},
        compiler_params=_params("parallel", "parallel"),
    )(hval, proj, gain, y)


def _hnorm_bwd(dy, hval, proj, gain, t, wc, dh, tr=512):
    ooff = 3 * wc // dh + 3 * NH
    tr = min(tr, t)
    yoff = wc // dh

    def body(dy_ref, h_ref, o_ref, g_ref, do_ref, dh_ref, dg_ref):
        i = pl.program_id(1)
        hhat, rstd = _head_norm(h_ref[...])
        gain_v = g_ref[...]
        sig = _sigmoid(o_ref[...])
        d = dy_ref[...]
        do_ref[...] = (d * hhat * gain_v * sig * (1.0 - sig)).astype(BF16)
        dhn = d * sig
        part = jnp.sum(dhn * hhat, axis=0, keepdims=True)

        @pl.when(i == 0)
        def _():
            dg_ref[...] = part

        @pl.when(i > 0)
        def _():
            dg_ref[...] += part

        dhat = dhn * gain_v
        dh_ref[...] = rstd * (dhat - jnp.mean(dhat, axis=1, keepdims=True)
                              - hhat * jnp.mean(dhat * hhat, axis=1, keepdims=True))

    blk = lambda off: pl.BlockSpec((tr, dh), lambda h, i: (i, off + h))
    return pl.pallas_call(
        body, name="hnorm_bwd", grid=(NH, t // tr),
        in_specs=[blk(yoff), blk(0), blk(ooff), pl.BlockSpec((1, dh), lambda h, i: (0, h))],
        out_specs=[blk(ooff), blk(0), pl.BlockSpec((1, dh), lambda h, i: (0, h))],
        out_shape=[jax.ShapeDtypeStruct(proj.shape, BF16), jax.ShapeDtypeStruct((t, NH * dh), F32),
                   jax.ShapeDtypeStruct((1, NH * dh), F32)],
        compiler_params=_params("parallel", "arbitrary"),
    )(dy, hval, proj, gain)


def _ln_stats(z):
    mu = jnp.mean(z, axis=1, keepdims=True)
    zc = z - mu
    rstd = lax.rsqrt(jnp.mean(zc * zc, axis=1, keepdims=True) + LN_EPS)
    return zc * rstd, rstd


def _ln_bwd(dy, xhat, rstd, g):
    dxh = dy * g
    return rstd * (dxh - jnp.mean(dxh, axis=1, keepdims=True) - xhat * jnp.mean(dxh * xhat, axis=1, keepdims=True))


def _accum(ref, i, part):
    @pl.when(i == 0)
    def _():
        ref[...] = part

    @pl.when(i > 0)
    def _():
        ref[...] += part


def _ln1_fwd(x, mix, g, b, tr=256, after=None):
    t, d = x.shape

    def body(x_ref, m_ref, g_ref, b_ref, xh_ref, rs_ref, xb_ref):
        xhat, rstd = _ln_stats(ALPHA * x_ref[...] + m_ref[...])
        xh_ref[...] = xhat
        rs_ref[...] = rstd
        xb_ref[...] = (xhat * g_ref[...] + b_ref[...]).astype(BF16)

    row = pl.BlockSpec((tr, d), lambda i: (i, 0))
    vec = pl.BlockSpec((1, d), lambda i: (0, 0))
    body, in_specs, args = _placed(after, body, [row, row, vec, vec], (x, mix, g, b))
    return pl.pallas_call(
        body, name="ln1_fwd", grid=(t // tr,),
        in_specs=in_specs,
        out_specs=[row, pl.BlockSpec((tr, 1), lambda i: (i, 0)), row],
        out_shape=[jax.ShapeDtypeStruct((t, d), F32), jax.ShapeDtypeStruct((t, 1), F32),
                   jax.ShapeDtypeStruct((t, d), BF16)],
        compiler_params=_params("parallel"),
    )(*args)


def _ln2_loss(xhat1, g1, b1, ff, target, g2, b2, tr=256):
    t, d = ff.shape

    def body(xh_ref, g1_ref, b1_ref, f_ref, t_ref, g_ref, b_ref, dz_ref, dzb_ref, dg_ref, db_ref, l_ref):
        i = pl.program_id(0)
        x1 = xh_ref[...] * g1_ref[...] + b1_ref[...]
        xhat, rstd = _ln_stats(ALPHA * x1 + f_ref[...])
        gv = g_ref[...]
        e = xhat * gv + b_ref[...] - t_ref[...]
        lsum = jnp.sum(jnp.sum(e * e, axis=1, keepdims=True), axis=0, keepdims=True) * (0.5 / d)
        dy = e * (1.0 / d)
        _accum(dg_ref, i, jnp.sum(dy * xhat, axis=0, keepdims=True))
        _accum(db_ref, i, jnp.sum(dy, axis=0, keepdims=True))
        _accum(l_ref, i, jnp.broadcast_to(lsum, l_ref.shape))
        dz = _ln_bwd(dy, xhat, rstd, gv)
        dz_ref[...] = dz
        dzb_ref[...] = dz.astype(BF16)

    row = pl.BlockSpec((tr, d), lambda i: (i, 0))
    vec = pl.BlockSpec((1, d), lambda i: (0, 0))
    return pl.pallas_call(
        body, name="ln2_loss", grid=(t // tr,),
        in_specs=[row, vec, vec, row, row, vec, vec],
        out_specs=[row, row, vec, vec, pl.BlockSpec((8, LANE), lambda i: (0, 0))],
        out_shape=[jax.ShapeDtypeStruct((t, d), F32), jax.ShapeDtypeStruct((t, d), BF16),
                   jax.ShapeDtypeStruct((1, d), F32), jax.ShapeDtypeStruct((1, d), F32),
                   jax.ShapeDtypeStruct((8, LANE), F32)],
        compiler_params=_params("arbitrary"),
    )(xhat1, g1, b1, ff, target, g2, b2)


def _ln1_bwd(dz2, dffn, xhat1, rstd1, g1, tr=256, after=None):
    t, d = dz2.shape

    def body(a_ref, f_ref, xh_ref, rs_ref, g_ref, dz_ref, dzb_ref, dg_ref, db_ref):
        i = pl.program_id(0)
        dy = ALPHA * a_ref[...] + f_ref[...]
        xhat = xh_ref[...]
        _accum(dg_ref, i, jnp.sum(dy * xhat, axis=0, keepdims=True))
        _accum(db_ref, i, jnp.sum(dy, axis=0, keepdims=True))
        dz = _ln_bwd(dy, xhat, rs_ref[...], g_ref[...])
        dz_ref[...] = dz
        dzb_ref[...] = dz.astype(BF16)

    row = pl.BlockSpec((tr, d), lambda i: (i, 0))
    vec = pl.BlockSpec((1, d), lambda i: (0, 0))
    body, in_specs, args = _placed(after, body, [row, row, row, pl.BlockSpec((tr, 1), lambda i: (i, 0)), vec],
                                   (dz2, dffn, xhat1, rstd1, g1))
    return pl.pallas_call(
        body, name="ln1_bwd", grid=(t // tr,),
        in_specs=in_specs,
        out_specs=[row, row, vec, vec],
        out_shape=[jax.ShapeDtypeStruct((t, d), F32), jax.ShapeDtypeStruct((t, d), BF16),
                   jax.ShapeDtypeStruct((1, d), F32), jax.ShapeDtypeStruct((1, d), F32)],
        compiler_params=_params("arbitrary"),
    )(*args)


def _ffn_act_fwd(hid0, w_fc, b_fc, t, dff, after=None):
    nb = dff // LANE

    def body(hv_ref, hg_ref, wv_ref, wg_ref, bv_ref, bg_ref, a_ref):
        val = _conv(hv_ref[...], wv_ref[...]) + bv_ref[...]
        gate = _conv(hg_ref[...], wg_ref[...]) + bg_ref[...]
        a_ref[...] = (gate * _sigmoid(gate) * val).astype(BF16)

    col = lambda off: pl.BlockSpec((t, LANE), lambda j: (0, j + off))
    w3 = lambda off: pl.BlockSpec((3, LANE), lambda j: (0, j + off))
    w1 = lambda off: pl.BlockSpec((1, LANE), lambda j: (0, j + off))
    body, in_specs, args = _placed(after, body, [col(0), col(nb), w3(0), w3(nb), w1(0), w1(nb)],
                                   (hid0, hid0, w_fc, w_fc, b_fc, b_fc))
    return pl.pallas_call(
        body, name="ffn_act_fwd", grid=(nb,),
        in_specs=in_specs,
        out_specs=col(0),
        out_shape=jax.ShapeDtypeStruct((t, dff), BF16),
        compiler_params=_params("parallel"),
    )(*args)


def _ffn_act_bwd(da, hid0, w_fc, b_fc, t, dff, after=None):
    nb = dff // LANE

    def body(da_ref, hv_ref, hg_ref, wv_ref, wg_ref, bv_ref, bg_ref,
             dhv_ref, dhg_ref, dwv_ref, dwg_ref, dbv_ref, dbg_ref):
        hv, hg, wv, wg = hv_ref[...], hg_ref[...], wv_ref[...], wg_ref[...]
        rv, rg = _rolled(hv), _rolled(hg)
        val = _conv(hv, wv, rv) + bv_ref[...]
        gate = _conv(hg, wg, rg) + bg_ref[...]
        sig = _sigmoid(gate)
        d = da_ref[...]
        dsig = d * sig
        dval = dsig * gate
        dgate = dsig * val * (1.0 + gate * (1.0 - sig))
        dhv_ref[...] = _conv_t(dval, wv).astype(BF16)
        dhg_ref[...] = _conv_t(dgate, wg).astype(BF16)
        dwv_ref[...] = _conv_dw(dval, hv, rv)
        dwg_ref[...] = _conv_dw(dgate, hg, rg)
        dbv_ref[...] = jnp.sum(dval, axis=0, keepdims=True)
        dbg_ref[...] = jnp.sum(dgate, axis=0, keepdims=True)

    col = lambda off: pl.BlockSpec((t, LANE), lambda j: (0, j + off))
    w3 = lambda off: pl.BlockSpec((3, LANE), lambda j: (0, j + off))
    w1 = lambda off: pl.BlockSpec((1, LANE), lambda j: (0, j + off))
    s3 = jax.ShapeDtypeStruct((3, dff), F32)
    s1 = jax.ShapeDtypeStruct((1, dff), F32)
    body, in_specs, args = _placed(after, body, [col(0), col(0), col(nb), w3(0), w3(nb), w1(0), w1(nb)],
                                   (da, hid0, hid0, w_fc, w_fc, b_fc, b_fc))
    return pl.pallas_call(
        body, name="ffn_act_bwd", grid=(nb,),
        in_specs=in_specs,
        out_specs=[col(0), col(0), w3(0), w3(0), w1(0), w1(0)],
        out_shape=[jax.ShapeDtypeStruct((t, dff), BF16)] * 2 + [s3, s3, s1, s1],
        compiler_params=_params("parallel"),
    )(*args)


class _Ready:
    def __init__(self, **weights):
        self.weights = weights

    def begin(self, after):
        return None

    def forward(self, name, after):
        return None

    def get(self, name, after):
        return self.weights[name]


class _Kept:
    def __init__(self):
        self.grads = {}

    def start(self, name, grad):
        self.grads[name] = grad
        return None

    def relay(self, name, after):
        return None

    def meanwhile(self, small, loss, after):
        return None


def _behind(a, token):
    return a if token is None else a + token[0:1, 0:1].reshape((1,) * a.ndim)


def _local_step(x, target, w_in, b_gates, w_sc, gain, w_out, ln1_g, ln1_b, w_up, w_fc, b_fc, w_down, ln2_g, ln2_b,
                gx=None, wx=None, x_b=None):
    t, d = x.shape
    wc = d // 2
    dh = (d - wc) // NH
    wm = NH * dh
    dff = w_fc.shape[1] // 2
    if wx is None:
        wx = _Ready(w_out=w_out, w_up=w_up, w_down=w_down)
    ninp = 3 * wc + 4 * wm + LANE
    nin = 3 * wc + 4 * wm
    gate_tile = nin // LANE
    nc = t // CHUNK
    bias_tile = jnp.pad(b_gates, ((0, 0), (0, LANE - 2 * NH)))

    if x_b is None:
        x_b = x.astype(BF16)
    proj = _matmul(x_b, w_in, "nt", F32, "proj", tm=512, tn=2432, tk=d, n=ninp, after=wx.begin(w_in))
    y = _sconv_fwd(proj, w_sc, t, wc)
    gcol = _gates_prep(proj, bias_tile, t, gate_tile)
    grow = gcol[:, :8].T.reshape(8, nc, CHUNK).transpose(1, 0, 2)
    hval, cs, ns = _mlstm_fwd(proj, gcol, grow, t, wc, dh)
    y = _hnorm_fwd(hval, proj, gain, y, t, wc, dh)
    tok = wx.forward("w_out", y)
    w_out = wx.get("w_out", tok)
    mix = _matmul(y, w_out, "nn", F32, "out_proj", tm=512, tn=1024, tk=wc, a_blocked=True, after=tok)
    xhat1, rstd1, x1_b = _ln1_fwd(x, mix, ln1_g, ln1_b, after=wx.forward("w_up", mix))
    w_up = wx.get("w_up", x1_b)
    wsl = w_up.shape[2]
    hid0 = _matmul(x1_b, w_up, "nn", F32, "ffn_up", tm=1024, tn=wsl, tk=d, b_blocked=True)
    act = _ffn_act_fwd(hid0, w_fc, b_fc, t, dff, after=wx.forward("w_down", hid0))
    w_down = wx.get("w_down", act)
    ff = _matmul(act, w_down, "nn", F32, "ffn_down", tm=1024, tn=512, tk=dff)
    dz2, dz2_b, d_ln2_g, d_ln2_b, loss = _ln2_loss(xhat1, ln1_g, ln1_b, ff, target, ln2_g, ln2_b)

    if gx is None:
        gx = _Kept()
    d_w_down = _matmul(act, dz2_b, "tn", BF16, "ffn_down_dw", tm=1408, tn=1024, tk=t)
    d_act = _matmul(dz2_b, w_down, "nt", F32, "ffn_down_dx", tm=2048, tn=512, tk=d, after=gx.start("w_down", d_w_down))
    *d_hid0, dwv, dwg, dbv, dbg = _ffn_act_bwd(d_act, hid0, w_fc, b_fc, t, dff, after=gx.relay("w_down", d_act))
    d_w_fc = jnp.concatenate([dwv, dwg], axis=1)
    d_b_fc = jnp.concatenate([dbv, dbg], axis=1)
    d_hid0 = tuple(d_hid0[:2])
    d_w_up = _matmul(x1_b, d_hid0, "tn", BF16, "ffn_up_dw", tm=1024, tn=wsl, tk=t, o_width=wsl)
    d_x1_ffn = _matmul(d_hid0, w_up, "nt", F32, "ffn_up_dx", tm=1024, tn=1024, tk=wsl, b_blocked=True,
                       after=gx.start("w_up", d_w_up))
    dz1, dz1_b, d_ln1_g, d_ln1_b = _ln1_bwd(dz2, d_x1_ffn, xhat1, rstd1, ln1_g, after=gx.relay("w_up", d_x1_ffn))

    d_w_out = _matmul(y, dz1_b, "tn", BF16, "out_proj_dw", tm=1024, tn=1024, tk=t, a_blocked=True)
    dy = _matmul(dz1_b, w_out, "nt", F32, "out_proj_dx", tm=1024, tn=1024, tk=d, after=gx.start("w_out", d_w_out))
    dcb, dcc, dch, d_w_sc = _sconv_bwd(dy, proj, w_sc, t, wc, after=gx.relay("w_out", dy))
    d_proj, d_hval, d_gain = _hnorm_bwd(dy, hval, proj, gain, t, wc, dh)
    d_proj, dgate = _mlstm_bwd(proj, gcol, grow, hval, d_hval, cs, ns, d_proj, t, wc, dh)
    d_proj, d_b_gates = _gates_bwd(dgate, proj, bias_tile, d_proj, t, gate_tile)
    for part, grad in enumerate((dcb, dcc, dch)):
        d_proj = lax.dynamic_update_slice(d_proj, grad, (0, part * wc))
    d_w_in = _matmul(d_proj, x_b, "tn", BF16, "proj_dw", tm=2432, tn=1024, tk=t)
    small = dict(b_gates=d_b_gates[:, :2 * NH], w_sc_conv=d_w_sc, mh_gain=d_gain, ln1_g=d_ln1_g, ln1_b=d_ln1_b,
                 w_ffn_conv=d_w_fc, b_ffn_conv=d_b_fc, ln2_g=d_ln2_g, ln2_b=d_ln2_b)
    token = gx.start("w_in", d_w_in)
    token = gx.relay("w_in", gx.meanwhile(small, loss, token))
    grad_x = _matmul(d_proj, w_in, "nn", F32, "proj_dx", tm=512, tn=512, tk=ninp, add=dz1, add_scale=ALPHA, after=token)
    return loss, grad_x, small, gx


HBM = pl.BlockSpec(memory_space=pltpu.HBM)


def _place():
    return lax.axis_index("x"), lax.axis_index("y"), lax.axis_index("c")


def _index(p):
    return 4 * p[0] + 2 * p[1] + p[2]


def _all_gather(arrs, name):
    n = len(arrs)

    def body(*refs):
        ins, outs = refs[:n], refs[n:2 * n]
        send_sems, recv_sems, local_sems = refs[2 * n:]
        x, y, c = _place()
        me, sibling = (x, y, c), (x, y, 1 - c)
        chips = [(1 - x, y), (x, 1 - y), (1 - x, 1 - y)]

        def copy(a, k, block, to, own=False):
            dst = outs[a].at[_index(block)]
            return pltpu.make_async_remote_copy(
                src_ref=ins[a] if own else dst, dst_ref=dst,
                send_sem=send_sems.at[k * n + a], recv_sem=recv_sems.at[k * n + a],
                device_id=to, device_id_type=MESH)

        mine = [pltpu.make_async_copy(ins[a], outs[a].at[_index(me)], local_sems.at[a]) for a in range(n)]
        for cp in mine:
            cp.start()
        first = []
        for a in range(n):
            first.append(copy(a, 0, me, sibling, own=True))
            first += [copy(a, 1 + j, me, (*chip, c), own=True) for j, chip in enumerate(chips)]
        for cp in first:
            cp.start()
        passed = []
        for j, chip in enumerate(chips):
            for a in range(n):
                copy(a, 1 + j, (*chip, c), me).wait_recv()
                cp = copy(a, 4 + j, (*chip, c), sibling)
                cp.start()
                passed.append(cp)
        for a in range(n):
            copy(a, 0, sibling, me).wait_recv()
            for j, chip in enumerate(chips):
                copy(a, 4 + j, (*chip, 1 - c), me).wait_recv()
        for cp in first + passed:
            cp.wait_send()
        for cp in mine:
            cp.wait()

    return pl.pallas_call(
        body, name=name, in_specs=[HBM] * n, out_specs=[HBM] * n,
        out_shape=[jax.ShapeDtypeStruct((N_DEV,) + a.shape, a.dtype) for a in arrs],
        scratch_shapes=[pltpu.SemaphoreType.DMA((7 * n,)), pltpu.SemaphoreType.DMA((7 * n,)),
                        pltpu.SemaphoreType.DMA((n,))],
    )(*arrs)


SEM = pl.BlockSpec(memory_space=pltpu.SEMAPHORE)
EFFECT = pltpu.SideEffectType.DATAFLOW_SIDE_EFFECTING


def _chips(x, y):
    return [(1 - x, y), (x, 1 - y), (1 - x, 1 - y)]


N_CHIP = N_DEV // 2


def _pair_route(x, y, c):
    return [((x, y, 1 - c), 2 * q + (1 - c), q, q) for q in range(N_CHIP)]


def _chip_route(x, y, c):
    mine = 2 * x + y
    return [((*chip, c), 2 * chip[0] + chip[1], mine, 2 * chip[0] + chip[1]) for chip in _chips(x, y)]


def _exchange_pieces(g_ref, land_ref, width, tail):
    if not tail:
        return [(lambda i: g_ref.at[i], lambda s: land_ref.at[s])]
    rows = lambda i, n: pl.ds(pl.multiple_of(i * width, IN_TAIL), n)
    return [(lambda i: g_ref.at[rows(i, width), :], lambda s: land_ref.at[s, pl.ds(0, width), :]),
            (lambda i: g_ref.at[rows(i + 1, IN_TAIL), :], lambda s: land_ref.at[s, pl.ds(width, IN_TAIL), :])]


def _chip_slot(x, y, c):
    return 2 * x + y


def _exchange_start(grad, route, tail, name, own_slot=None):
    width = IN_SLAB if tail else grad.shape[1]
    n_p = 2 if tail else 1
    n_c = len(route(0, 0, 0))
    land_shape = (N_CHIP, width + (IN_TAIL if tail else 0), grad.shape[-1])
    assert not (tail and own_slot)

    def body(g_ref, land_ref, send_sems, recv_sems, g_thru, land_thru, token):
        for j, (peer, slab, slot, _) in enumerate(route(*_place())):
            for p, (src, dst) in enumerate(_exchange_pieces(g_ref, land_ref, width, tail)):
                pltpu.make_async_remote_copy(src_ref=src(slab), dst_ref=dst(slot), send_sem=send_sems.at[j * n_p + p],
                                             recv_sem=recv_sems.at[j * n_p + p], device_id=peer,
                                             device_id_type=MESH).start()
        if own_slot:
            mine = own_slot(*_place())
            pltpu.make_async_copy(g_ref.at[mine], land_ref.at[mine], send_sems.at[n_c * n_p]).start()
        token[...] = jnp.zeros_like(token)

    return pl.pallas_call(
        body, name=name,
        out_shape=(pltpu.SemaphoreType.DMA((n_c * n_p + bool(own_slot),)), pltpu.SemaphoreType.DMA((n_c * n_p,)),
                   pltpu.HBM(grad.shape, grad.dtype), pltpu.HBM(land_shape, grad.dtype),
                   jax.ShapeDtypeStruct((8, LANE), F32)),
        in_specs=(HBM, HBM), out_specs=(SEM, SEM, HBM, HBM, pl.BlockSpec(memory_space=pltpu.VMEM)),
        input_output_aliases={0: 2, 1: 3},
        compiler_params=pltpu.CompilerParams(has_side_effects=EFFECT),
    )(pltpu.with_memory_space_constraint(grad, pltpu.HBM),
      pltpu.with_memory_space_constraint(lax.empty(land_shape, grad.dtype), pltpu.HBM))


def _exchange_wait(send_sems, recv_sems, g_thru, land_thru, after, route, tail, name, own_slot=None):
    width = IN_SLAB if tail else g_thru.shape[1]
    n_p = 2 if tail else 1

    def body(g_ref, land_ref, send_sems, recv_sems, after_ref, g_dead, got_ref):
        places = route(*_place())
        for j, (peer, slab, _, slot) in enumerate(places):
            for p, (src, dst) in enumerate(_exchange_pieces(g_ref, land_ref, width, tail)):
                cp = pltpu.make_async_remote_copy(src_ref=src(slab), dst_ref=dst(slot),
                                                  send_sem=send_sems.at[j * n_p + p], recv_sem=recv_sems.at[j * n_p + p],
                                                  device_id=peer, device_id_type=MESH)
                cp.wait_send()
                cp.wait_recv()
        if own_slot:
            mine = own_slot(*_place())
            pltpu.make_async_copy(g_ref.at[mine], land_ref.at[mine], send_sems.at[len(places) * n_p]).wait()

    return pl.pallas_call(
        body, name=name,
        out_shape=(pltpu.HBM(g_thru.shape, g_thru.dtype), pltpu.HBM(land_thru.shape, land_thru.dtype)),
        in_specs=(HBM, HBM, SEM, SEM, pl.BlockSpec(memory_space=pl.ANY)), out_specs=(HBM, HBM),
        input_output_aliases={0: 0, 1: 1},
        compiler_params=pltpu.CompilerParams(has_side_effects=EFFECT),
    )(g_thru, land_thru, send_sems, recv_sems, after)


def _pair_add(grad, pair, core, tail, name):
    rows, cols = (IN_SLAB if tail else grad.shape[1]), grad.shape[-1]
    total = pair.shape[1]

    def body(core_ref, *refs):
        if tail:
            g_ref, t_ref, p_ref, o_ref = refs
            o_ref[0:rows, :] = (g_ref[...].astype(F32) + p_ref[0:rows, :].astype(F32)).astype(BF16)
            o_ref[rows:total, :] = (t_ref[...].astype(F32) + p_ref[rows:total, :].astype(F32)).astype(BF16)
        else:
            g_ref, p_ref, o_ref = refs
            o_ref[...] = (g_ref[...].astype(F32) + p_ref[...].astype(F32)).astype(BF16)

    if tail:
        tc = _fit(cols, 512)
        grid = (N_CHIP, cols // tc)
        slab = pl.BlockSpec((None, total, tc), lambda q, i, core_ref: (q, 0, i))
        per = IN_SLAB // IN_TAIL
        in_specs = [pl.BlockSpec((rows, tc), lambda q, i, core_ref: (2 * q + core_ref[0], i)),
                    pl.BlockSpec((IN_TAIL, tc), lambda q, i, core_ref: ((2 * q + core_ref[0] + 1) * per, i))]
    else:
        tr = _rows(rows, 1024)
        grid = (N_CHIP, rows // tr)
        slab = pl.BlockSpec((None, tr, cols), lambda q, i, core_ref: (q, i, 0))
        in_specs = [pl.BlockSpec((None, tr, cols), lambda q, i, core_ref: (2 * q + core_ref[0], i, 0))]
    return pl.pallas_call(
        body, name=name,
        grid_spec=pltpu.PrefetchScalarGridSpec(num_scalar_prefetch=1, grid=grid,
                                               in_specs=in_specs + [slab], out_specs=slab),
        out_shape=jax.ShapeDtypeStruct(pair.shape, BF16),
        compiler_params=_params("parallel", "parallel"),
    )(core, *([grad, grad] if tail else [grad]), pair)


def _relay_places(x, y, c):
    came_from = (c * (1 - x) + (1 - c) * x, c * y + (1 - c) * (1 - y), c)
    pass_to = (c * x + (1 - c) * (1 - x), c * (1 - y) + (1 - c) * y, c)
    return 2 - c, came_from, pass_to, pass_to


OWN = 4


def _gather_start(blocks, after, name, spare=(), relayed=False):
    n = len(blocks)
    lands = [(N_DEV + (a in spare),) + b.shape for a, b in enumerate(blocks)]

    def body(*refs):
        b_refs, land_refs = refs[:n], refs[n:2 * n]
        send_sems, recv_sems = refs[2 * n + 1:3 * n + 1], refs[3 * n + 1:4 * n + 1]
        token = refs[-1]
        x, y, c = _place()
        me = _index((x, y, c))
        for a in range(n):
            targets = [(x, y, 1 - c)] + [(*chip, c) for chip in _chips(x, y)]
            for k, to in enumerate(targets[:3] if relayed else targets):
                pltpu.make_async_remote_copy(src_ref=b_refs[a], dst_ref=land_refs[a].at[me], send_sem=send_sems[a].at[k],
                                             recv_sem=recv_sems[a].at[k], device_id=to, device_id_type=MESH).start()
        for a in range(n):
            pltpu.make_async_copy(b_refs[a], land_refs[a].at[me], send_sems[a].at[OWN]).start()
        token[...] = jnp.zeros_like(token)

    sems = [pltpu.SemaphoreType.DMA((OWN + 1,))] * n
    out = pl.pallas_call(
        body, name=name,
        out_shape=(*sems, *sems, *[pltpu.HBM(b.shape, b.dtype) for b in blocks],
                   *[pltpu.HBM(s, b.dtype) for s, b in zip(lands, blocks)], jax.ShapeDtypeStruct((8, LANE), F32)),
        in_specs=(*[HBM] * (2 * n), pl.BlockSpec(memory_space=pl.ANY)),
        out_specs=(*[SEM] * (2 * n), *[HBM] * (2 * n), pl.BlockSpec(memory_space=pltpu.VMEM)),
        input_output_aliases={i: 2 * n + i for i in range(2 * n)},
        compiler_params=pltpu.CompilerParams(has_side_effects=EFFECT),
    )(*[pltpu.with_memory_space_constraint(b, pltpu.HBM) for b in blocks],
      *[pltpu.with_memory_space_constraint(lax.empty(s, b.dtype), pltpu.HBM) for s, b in zip(lands, blocks)], after)
    return [(out[a], out[n + a], out[2 * n + a], out[3 * n + a]) for a in range(n)], out[-1]


def _gather_relay(states, after, name):
    n, first_out = len(states), 3 * len(states) + len(after)

    def body(*refs):
        land_refs, send_sems, recv_sems = refs[:n], refs[n:2 * n], refs[2 * n:3 * n]
        pass_send, pass_recv = refs[first_out + n:first_out + 2 * n], refs[first_out + 2 * n:first_out + 3 * n]
        k_in, came_from, pass_to, _ = _relay_places(*_place())
        for a in range(n):
            slot = land_refs[a].at[_index(came_from)]
            pltpu.make_async_remote_copy(src_ref=slot, dst_ref=slot, send_sem=send_sems[a].at[k_in],
                                         recv_sem=recv_sems[a].at[k_in], device_id=came_from,
                                         device_id_type=MESH).wait_recv()
        for a in range(n):
            slot = land_refs[a].at[_index(came_from)]
            pltpu.make_async_remote_copy(src_ref=slot, dst_ref=slot, send_sem=pass_send[a].at[0],
                                         recv_sem=pass_recv[a].at[0], device_id=pass_to, device_id_type=MESH).start()
        refs[-1][...] = jnp.zeros_like(refs[-1])

    lands = [st[3] for st in states]
    pair = [pltpu.SemaphoreType.DMA((1,))] * n
    out = pl.pallas_call(
        body, name=name,
        out_shape=(*[pltpu.HBM(l.shape, l.dtype) for l in lands], *pair, *pair, jax.ShapeDtypeStruct((8, LANE), F32)),
        in_specs=(*[HBM] * n, *[SEM] * (2 * n), *[pl.BlockSpec(memory_space=pl.ANY)] * len(after)),
        out_specs=(*[HBM] * n, *[SEM] * (2 * n), pl.BlockSpec(memory_space=pltpu.VMEM)),
        input_output_aliases={a: a for a in range(n)},
        compiler_params=pltpu.CompilerParams(has_side_effects=EFFECT),
    )(*lands, *[st[0] for st in states], *[st[1] for st in states], *after)
    return [(st[0], st[1], st[2], out[a], (out[n + a], out[2 * n + a])) for a, st in enumerate(states)], out[-1]


def _gather_forward(send_sems, recv_sems, b_thru, land_thru, after, name, passed=None):
    relayed = passed is not None

    def body(b_ref, land_ref, send_sems, recv_sems, *rest):
        pass_send, pass_recv = rest[:2] if relayed else (None, None)
        send2, recv2, token = rest[-3:]
        x, y, c = _place()
        sibling = (x, y, 1 - c)
        arrivals = [sibling] + [(*chip, c) for chip in _chips(x, y)]
        waits = [(send_sems.at[k], recv_sems.at[k], frm) for k, frm in enumerate(arrivals)]
        sends = [send_sems.at[k] for k in range(4)]
        if relayed:
            k_in, _, _, other = _relay_places(x, y, c)
            waits = [waits[0], (send_sems.at[3 - k_in], recv_sems.at[3 - k_in], other),
                     (pass_send.at[0], pass_recv.at[0], arrivals[3])]
            sends[3] = pass_send.at[0]
        for sem in sends:
            pltpu.make_async_remote_copy(src_ref=b_ref, dst_ref=land_ref.at[0], send_sem=sem, recv_sem=recv_sems.at[0],
                                         device_id=sibling, device_id_type=MESH).wait_send()
        pltpu.make_async_copy(b_ref, land_ref.at[_index((x, y, c))], send_sems.at[OWN]).wait()
        for send_sem, recv_sem, frm in waits:
            pltpu.make_async_remote_copy(src_ref=b_ref, dst_ref=land_ref.at[_index(frm)], send_sem=send_sem,
                                         recv_sem=recv_sem, device_id=frm, device_id_type=MESH).wait_recv()
        for j, chip in enumerate(_chips(x, y)):
            slot = land_ref.at[_index((*chip, c))]
            pltpu.make_async_remote_copy(src_ref=slot, dst_ref=slot, send_sem=send2.at[j], recv_sem=recv2.at[j],
                                         device_id=sibling, device_id_type=MESH).start()
        token[...] = jnp.zeros_like(token)

    extra = list(passed) if relayed else []
    return pl.pallas_call(
        body, name=name,
        out_shape=(pltpu.HBM(b_thru.shape, b_thru.dtype), pltpu.HBM(land_thru.shape, land_thru.dtype),
                   pltpu.SemaphoreType.DMA((3,)), pltpu.SemaphoreType.DMA((3,)), jax.ShapeDtypeStruct((8, LANE), F32)),
        in_specs=(HBM, HBM, SEM, SEM, *[SEM] * len(extra), pl.BlockSpec(memory_space=pl.ANY)),
        out_specs=(HBM, HBM, SEM, SEM, pl.BlockSpec(memory_space=pltpu.VMEM)),
        input_output_aliases={0: 0, 1: 1},
        compiler_params=pltpu.CompilerParams(has_side_effects=EFFECT),
    )(b_thru, land_thru, send_sems, recv_sems, *extra, after)


def _gather_finish(land_thru, send2, recv2, after, name):
    def body(land_ref, send2, recv2, after_ref, land_out):
        x, y, c = _place()
        for j, chip in enumerate(_chips(x, y)):
            cp = pltpu.make_async_remote_copy(src_ref=land_ref.at[_index((*chip, c))],
                                              dst_ref=land_ref.at[_index((*chip, 1 - c))], send_sem=send2.at[j],
                                              recv_sem=recv2.at[j], device_id=(x, y, 1 - c), device_id_type=MESH)
            cp.wait_send()
            cp.wait_recv()

    return pl.pallas_call(
        body, name=name, out_shape=pltpu.HBM(land_thru.shape, land_thru.dtype),
        in_specs=(HBM, SEM, SEM, pl.BlockSpec(memory_space=pl.ANY)), out_specs=HBM,
        input_output_aliases={0: 0},
        compiler_params=pltpu.CompilerParams(has_side_effects=EFFECT),
    )(land_thru, send2, recv2, after)


class _Gathering:
    def __init__(self, ahead, later, me):
        cast = [a.astype(BF16) for a in ahead.values()]
        started, self.token = _gather_start(cast, cast[0], "gather1_ahead", relayed=True)
        self.me, self.state, self.relayed, self.later = me, dict(zip(ahead, started)), tuple(ahead), later

    def start_first(self, first):
        started, self.token = _gather_start(list(first.values()), self.token, "gather1_first", spare=(0,), relayed=True)
        self.state.update(zip(first, started))
        self.relayed += tuple(first)

    def begin(self, after):
        return self.token

    def relay(self, *after):
        states, token = _gather_relay([self.state[n] for n in self.relayed], after, "gather_relay")
        self.state.update(zip(self.relayed, states))
        cast = [_behind(a, token).astype(BF16) for a in self.later.values()]
        started, self.token = _gather_start(cast, token, "gather1_later")
        self.state.update(zip(self.later, started))
        return self.token

    def forward(self, name, after):
        first_leg, passed = self.state[name][:4], (self.state[name][4:] or (None,))[0]
        *self.state[name], token = _gather_forward(*first_leg, after, "gather2_" + name, passed=passed)
        return token

    def get(self, name, after):
        _, land, send2, recv2 = self.state[name]
        land = _gather_finish(land, send2, recv2, after, "gather3_" + name)
        return land if name not in ("w_out", "w_down") else land.reshape(-1, land.shape[2])


class _Reducing:
    def __init__(self, core, chip, gather_small):
        self.core, self.chip, self.state, self.token, self.gather_small = core, chip, {}, None, gather_small

    def meanwhile(self, small, loss, after):
        self.small_sum = self.gather_small(small, loss, after)
        return self.small_sum

    def start(self, name, grad):
        tail = name == "w_in"
        g = grad if tail or grad.ndim == 3 else grad.reshape(N_DEV, grad.shape[0] // N_DEV, grad.shape[1])
        *self.state[name], token = _exchange_start(g, _pair_route, tail, "pair_send_" + name)
        return token

    def relay(self, name, after):
        tail = name == "w_in"
        grad, pair = _exchange_wait(*self.state[name], after, _pair_route, tail, "pair_recv_" + name)
        total = _pair_add(grad, pair, self.core, tail, "pair_add_" + name)
        *self.state[name], self.token = _exchange_start(total, _chip_route, False, "chip_send_" + name,
                                                        own_slot=_chip_slot)
        return self.token

    def finish(self, name, after):
        _, land = _exchange_wait(*self.state[name], after, _chip_route, False, "chip_recv_" + name, own_slot=_chip_slot)
        return land


def _carry_w_in(main, tail):
    slabs, _, d = main.shape
    tc = _fit(d, 2048)
    assert slabs == N_DEV + 1 and tail.shape[:2] == (N_DEV, IN_TAIL), (main.shape, tail.shape)
    top = lambda off: pl.BlockSpec((None, IN_TAIL, tc), lambda s, j: (s + off, 0, j))

    def carry(m_ref, t_ref, o_ref):
        o_ref[...] = m_ref[...] + t_ref[...]

    main = pl.pallas_call(
        carry, name="carry_w_in", grid=(N_DEV - 1, d // tc), in_specs=[top(1), top(0)], out_specs=top(1),
        out_shape=jax.ShapeDtypeStruct(main.shape, main.dtype), input_output_aliases={0: 0},
        compiler_params=_params("parallel", "parallel"),
    )(main, tail)

    def last(m_ref, t_ref, o_ref):
        o_ref[...] = jnp.zeros_like(o_ref)
        o_ref[0:IN_TAIL, :] = t_ref[...]

    return pl.pallas_call(
        last, name="last_slab_w_in", grid=(d // tc,),
        in_specs=[pl.BlockSpec(memory_space=pl.ANY), pl.BlockSpec((None, IN_TAIL, tc), lambda j: (N_DEV - 1, 0, j))],
        out_specs=pl.BlockSpec((None, LANE, tc), lambda j: (N_DEV, 0, j)),
        out_shape=jax.ShapeDtypeStruct(main.shape, main.dtype), input_output_aliases={0: 0},
        compiler_params=_params("parallel"),
    )(main, tail)


def _rows(n, want):
    t = min(n, want)
    t -= t % 16
    while n % t:
        t -= 16
    return t


def _adam_math(w, g, m, v):
    m2 = ADAM_B1 * m + (1.0 - ADAM_B1) * g
    v2 = ADAM_B2 * v + (1.0 - ADAM_B2) * (g * g)
    m_hat = m2 * (1.0 / (1.0 - ADAM_B1 ** ADAM_STEP))
    v_hat = v2 * (1.0 / (1.0 - ADAM_B2 ** ADAM_STEP))
    return -ADAM_LR * (m_hat / (jnp.sqrt(v_hat) + ADAM_EPS) + ADAM_WD * w), m2, v2


def _slot_sum(r_ref):
    acc = r_ref[0].astype(F32)
    for i in range(1, r_ref.shape[0]):
        acc = acc + r_ref[i].astype(F32)
    return acc


def _shift_w_in(w):
    ws, d = w.shape
    tc = _fit(d, 256)

    def body(w_ref, main_ref, tail_ref, tall):
        tall[...] = jnp.zeros_like(tall)
        tall[0:ws, :] = w_ref[...]
        moved = pltpu.roll(tall[...], _index(_place()), 0).astype(BF16)
        main_ref[...] = moved[0:IN_SLAB]
        tail_ref[...] = moved[IN_SLAB:]

    return pl.pallas_call(
        body, name="shift_w_in", grid=(d // tc,),
        in_specs=[pl.BlockSpec((ws, tc), lambda j: (0, j))],
        out_specs=[pl.BlockSpec((IN_SLAB, tc), lambda j: (0, j)), pl.BlockSpec((IN_TAIL, tc), lambda j: (0, j))],
        out_shape=[jax.ShapeDtypeStruct((IN_SLAB, d), BF16), jax.ShapeDtypeStruct((IN_TAIL, d), BF16)],
        scratch_shapes=[pltpu.VMEM((IN_SLAB + IN_TAIL, tc), F32)], compiler_params=_params("parallel"),
    )(w)


def _sum_adamw_shifted(r, w, m, v, name):
    _, ph, d = r.shape
    ws = w.shape[0]
    tc = _fit(d, 256)

    def body(r_ref, w_ref, m_ref, v_ref, g_ref, d_ref, m2_ref, v2_ref, tall):
        tall[...] = pltpu.roll(_slot_sum(r_ref), lax.rem(ph - _index(_place()), ph), 0)
        g = tall[0:ws, :]
        g_ref[...] = g
        d_ref[...], m2_ref[...], v2_ref[...] = _adam_math(w_ref[...], g, m_ref[...], v_ref[...])

    blk = pl.BlockSpec((ws, tc), lambda j: (0, j))
    out = jax.ShapeDtypeStruct(w.shape, F32)
    return pl.pallas_call(
        body, name=name, grid=(d // tc,),
        in_specs=[pl.BlockSpec((r.shape[0], ph, tc), lambda j: (0, 0, j)), blk, blk, blk],
        out_specs=[blk] * 4, out_shape=[out] * 4,
        scratch_shapes=[pltpu.VMEM((ph, tc), F32)], compiler_params=_params("parallel"),
    )(r, w, m, v)


def _sum_slots(r, name, tr=128):
    _, rows, cols = r.shape
    tr = _rows(rows, tr)

    def body(r_ref, g_ref):
        g_ref[...] = _slot_sum(r_ref)

    return pl.pallas_call(
        body, name=name, grid=(rows // tr,),
        in_specs=[pl.BlockSpec((r.shape[0], tr, cols), lambda i: (0, i, 0))],
        out_specs=pl.BlockSpec((tr, cols), lambda i: (i, 0)),
        out_shape=jax.ShapeDtypeStruct((rows, cols), F32),
        compiler_params=_params("parallel"),
    )(r)


def _adamw(w, g, m, v, name, tr=256):
    rows, cols = w.shape
    tr = _rows(rows, tr)

    def body(w_ref, g_ref, m_ref, v_ref, d_ref, m2_ref, v2_ref):
        d_ref[...], m2_ref[...], v2_ref[...] = _adam_math(w_ref[...], g_ref[...], m_ref[...], v_ref[...])

    blk = pl.BlockSpec((tr, cols), lambda i: (i, 0))
    out = jax.ShapeDtypeStruct((rows, cols), F32)
    return pl.pallas_call(
        body, name=name, grid=(rows // tr,), in_specs=[blk] * 4, out_specs=[blk] * 3, out_shape=[out] * 3,
        compiler_params=_params("parallel"),
    )(w, g, m, v)


def _sum_adamw(r, w, m, v, name, tr=256):
    rows, cols = w.shape
    tr = _rows(rows, tr)

    def body(r_ref, w_ref, m_ref, v_ref, g_ref, d_ref, m2_ref, v2_ref):
        g = _slot_sum(r_ref)
        g_ref[...] = g
        d_ref[...], m2_ref[...], v2_ref[...] = _adam_math(w_ref[...], g, m_ref[...], v_ref[...])

    blk = pl.BlockSpec((tr, cols), lambda i: (i, 0))
    out = jax.ShapeDtypeStruct((rows, cols), F32)
    return pl.pallas_call(
        body, name=name, grid=(rows // tr,),
        in_specs=[pl.BlockSpec((r.shape[0], tr, cols), lambda i: (0, i, 0)), blk, blk, blk],
        out_specs=[blk] * 4, out_shape=[out] * 4,
        compiler_params=_params("parallel"),
    )(r, w, m, v)


def _pack(pieces, sizes):
    flat = [jnp.pad(p.reshape(-1).astype(F32), (0, s - p.size)) for p, s in zip(pieces, sizes)]
    total = sum(sizes)
    padded = -(-total // (16 * LANE)) * (16 * LANE)
    return jnp.pad(jnp.concatenate(flat), (0, padded - total)).reshape(-1, LANE)


def _unpack(packed, shapes, sizes):
    flat = packed.reshape(-1)
    out, off = [], 0
    for shp, s in zip(shapes, sizes):
        n = 1
        for k in shp:
            n *= k
        out.append(flat[off:off + n].reshape(shp))
        off += s
    return out


def _lanes(n):
    return -(-n // LANE) * LANE


WEIGHTS = ("w_in", "b_gates", "w_sc_conv", "mh_gain", "w_out", "ln1_g", "ln1_b", "w_up", "w_ffn_conv", "b_ffn_conv",
           "w_down", "ln2_g", "ln2_b")
BIG = ("w_in", "w_out", "w_up", "w_down")
SMALL = tuple(n for n in WEIGHTS if n not in BIG)


def kernel(x, w_in, b_gates, w_sc_conv, mh_gain, w_out, ln1_g, ln1_b, w_up, w_ffn_conv, b_ffn_conv, w_down, ln2_g, ln2_b, loss_target, m_w_in, m_b_gates, m_w_sc_conv, m_mh_gain, m_w_out, m_ln1_g, m_ln1_b, m_w_up, m_w_ffn_conv, m_b_ffn_conv, m_w_down, m_ln2_g, m_ln2_b, v_w_in, v_b_gates, v_w_sc_conv, v_mh_gain, v_w_out, v_ln1_g, v_ln1_b, v_w_up, v_w_ffn_conv, v_b_ffn_conv, v_w_down, v_ln2_g, v_ln2_b):
    w = dict(zip(WEIGHTS, (w_in, b_gates, w_sc_conv, mh_gain, w_out, ln1_g, ln1_b, w_up, w_ffn_conv, b_ffn_conv,
                           w_down, ln2_g, ln2_b)))
    m = dict(zip(WEIGHTS, (m_w_in, m_b_gates, m_w_sc_conv, m_mh_gain, m_w_out, m_ln1_g, m_ln1_b, m_w_up,
                           m_w_ffn_conv, m_b_ffn_conv, m_w_down, m_ln2_g, m_ln2_b)))
    v = dict(zip(WEIGHTS, (v_w_in, v_b_gates, v_w_sc_conv, v_mh_gain, v_w_out, v_ln1_g, v_ln1_b, v_w_up,
                           v_w_ffn_conv, v_b_ffn_conv, v_w_down, v_ln2_g, v_ln2_b)))
    me = _index(_place())
    d = x.shape[2]
    ws_in = w_in.shape[2]
    assert ws_in == IN_SLAB + 1 and N_DEV <= LANE, w_in.shape
    ninp = (N_DEV + 1) * IN_SLAB
    ws_sc, ws_fc = w_sc_conv.shape[2], w_ffn_conv.shape[2]

    wx = _Gathering({"w_out": w_out[0]}, {n: w[n][0] for n in ("w_up", "w_down")}, me)
    w_in_t = jnp.transpose(_behind(w_in[0], wx.token))
    w_in_main, w_in_tail = _shift_w_in(w_in_t)
    taps8 = lambda a: jnp.pad(a[0], ((0, 5), (0, 0)))
    at_once = ("w_in", "w_tail", "w_sc", "w_fc")
    wx.start_first(dict(zip(at_once, (w_in_main, w_in_tail, taps8(w_sc_conv), taps8(w_ffn_conv)))))
    x_b = _behind(x[0], wx.begin(None)).astype(BF16)
    m_in_t, v_in_t = (jnp.transpose(_behind(a[0], wx.begin(None))) for a in (m_w_in, v_w_in))
    token = wx.relay(x_b, m_in_t, v_in_t)
    for n in at_once:
        token = wx.forward(n, token)
    g_in, g_tail, g_sc, g_fc = (wx.get(n, token) for n in at_once)
    w_in_full = _carry_w_in(g_in, g_tail).reshape(ninp, d)
    w_sc_full = g_sc[:, :3].transpose(1, 0, 2).reshape(3, N_DEV * ws_sc)
    w_fc_full = g_fc[:, :3].transpose(1, 0, 2).reshape(3, N_DEV * ws_fc)

    xi, yi, ci = _place()
    names = ("loss",) + SMALL
    pieces = {}

    def gather_small(small, loss_t, after):
        pieces.update(small, loss=loss_t[0, :1])
        sizes = [_lanes(pieces[n].size) for n in names]
        (g_small,) = _all_gather([_behind(_pack([pieces[n] for n in names], sizes), after)], "gather_small")
        return _sum_slots(g_small, "sum_small", tr=g_small.shape[1])

    gx = _Reducing(jnp.reshape(ci, (1,)).astype(jnp.int32), 2 * xi + yi, gather_small)
    loss_t, grad_x, small, _ = _local_step(
        x[0], loss_target[0], w_in_full, b_gates, w_sc_full, mh_gain, None, ln1_g, ln1_b, None,
        w_fc_full, b_ffn_conv, None, ln2_g, ln2_b, gx=gx, wx=wx, x_b=x_b)

    grads, deltas, new_m, new_v = {}, {}, {}, {}
    for name in ("w_down", "w_up", "w_out"):
        grads[name], deltas[name], new_m[name], new_v[name] = _sum_adamw(
            gx.finish(name, gx.token), w[name][0], m[name][0], v[name][0], "adamw_" + name)

    summed = _unpack(gx.small_sum, [pieces[n].shape for n in names], [_lanes(pieces[n].size) for n in names])
    full = dict(zip(names, summed))
    full["w_sc_conv"] = lax.dynamic_slice(full["w_sc_conv"], (0, me * ws_sc), (3, ws_sc))
    full["w_ffn_conv"] = lax.dynamic_slice(full["w_ffn_conv"], (0, me * ws_fc), (3, ws_fc))
    for n in SMALL:
        grads[n] = full[n].reshape(w[n].shape)
    sizes = [_lanes(w[n].size) for n in SMALL]
    shapes = [w[n].shape for n in SMALL]
    packed = [_pack([t[n] for n in SMALL], sizes) for t in (w, grads, m, v)]
    small_out = _adamw(*packed, "adamw_small")
    for res, t in zip(small_out, (deltas, new_m, new_v)):
        t.update(zip(SMALL, _unpack(res, shapes, sizes)))

    done = sum(t[0:1, 0:1] for t in (grad_x, deltas["w_down"], deltas["w_up"], deltas["w_out"], small_out[0]))
    grads["w_in"], deltas["w_in"], new_m["w_in"], new_v["w_in"] = (
        jnp.transpose(a)[None] for a in _sum_adamw_shifted(gx.finish("w_in", done), w_in_t, m_in_t, v_in_t, "adamw_w_in"))

    big = lambda t: {n: (t[n].reshape(w[n].shape) if n in BIG else t[n]) for n in WEIGHTS}
    grads, deltas, new_m, new_v = big(grads), big(deltas), big(new_m), big(new_v)
    return (full["loss"].reshape(()), grad_x[None], *[grads[n] for n in WEIGHTS], *[deltas[n] for n in WEIGHTS],
            *[new_m[n] for n in WEIGHTS], *[new_v[n] for n in WEIGHTS])
```

```python
import functools

import jax
import jax.numpy as jnp
from jax import lax
from jax.experimental import pallas as pl
from jax.experimental.pallas import tpu as pltpu

F32 = jnp.float32
BF16 = jnp.bfloat16
MESH = pl.DeviceIdType.MESH

N_DEV = 8
NH = 4
CHUNK = 64
LN_EPS = 1e-5
HN_EPS = 1e-6
ALPHA = 2.0 ** 0.25
LANE = 128
IN_SLAB = 7 * LANE
IN_TAIL = 16
VMEM_LIMIT = 56 * 1024 * 1024
ADAM_LR, ADAM_B1, ADAM_B2, ADAM_EPS, ADAM_WD, ADAM_STEP = 0.001, 0.9, 0.999, 1e-08, 0.01, 10

_NN = (((1,), (0,)), ((), ()))
_NT = (((1,), (1,)), ((), ()))
_TN = (((0,), (0,)), ((), ()))


def _dot(a, b, dn=_NN):
    return lax.dot_general(a, b, dn, preferred_element_type=F32)


def _params(*sem):
    return pltpu.CompilerParams(dimension_semantics=sem if sem else None, vmem_limit_bytes=VMEM_LIMIT)


def _iota(shape, axis):
    return lax.broadcasted_iota(jnp.int32, shape, axis)


def _fit(n, want):
    if n <= want:
        return n
    t = want - want % LANE
    while n % t:
        t -= LANE
    return t


def _placed(after, body, in_specs, args):
    if after is None:
        return body, in_specs, args
    return (lambda after_ref, *refs: body(*refs)), [pl.BlockSpec(memory_space=pl.ANY)] + in_specs, (after,) + args


def _matmul(a, b, mode, out_dtype, name, tm=1024, tn=512, tk=1024, add=None, add_scale=1.0,
            a_blocked=False, b_blocked=False, o_width=None, after=None, n=None):
    a_parts = a if isinstance(a, tuple) else None
    b_parts = b if isinstance(b, tuple) else None
    if a_parts:
        a_blocked, (a_rows, wa), na = True, a[0].shape, len(a)
        kd, m = (a_rows, na * wa) if mode == "tn" else (na * wa, a_rows)
    elif a_blocked:
        na, a_rows, wa = a.shape
        kd, m = (a_rows, na * wa) if mode == "tn" else (na * wa, a_rows)
    elif mode == "tn":
        kd, m = a.shape
    else:
        m, kd = a.shape
    if b_parts:
        b_blocked, (rows, w), nb = True, b[0].shape, len(b)
    elif b_blocked:
        nb, rows, w = b.shape
    if b_blocked:
        n = rows if mode == "nt" else nb * w
        assert (nb * w if mode == "nt" else rows) == kd, (name, kd)
    else:
        n = n or (b.shape[0] if mode == "nt" else b.shape[1])
    tm, tn, tk = _fit(m, tm), _fit(n, tn), _fit(kd, tk)
    if a_blocked and mode == "tn":
        tm = _fit(wa, tm)
    if a_blocked and mode != "tn":
        tk = _fit(wa, tk)
    if b_blocked and mode != "nt":
        tn = _fit(w, tn)
    if b_blocked and mode == "nt":
        tk = _fit(w, tk)
    if o_width is not None:
        tn = _fit(o_width, tn)
    assert m % tm == 0 and n % tn == 0 and kd % tk == 0, (name, m, n, kd, tm, tn, tk)
    assert not (a_blocked and mode != "tn" and wa % tk) and not (b_blocked and mode == "nt" and w % tk), (name, tk)
    nk = kd // tk
    dn = {"nn": _NN, "nt": _NT, "tn": _TN}[mode]
    if a_blocked and mode == "tn":
        a_per = wa // tm
        a_spec = pl.BlockSpec((None, tk, tm), lambda i, j, k: (i // a_per, k, i % a_per))
    elif a_blocked:
        a_per = wa // tk
        a_spec = pl.BlockSpec((None, tm, tk), lambda i, j, k: (k // a_per, i, k % a_per))
    elif mode == "tn":
        a_spec = pl.BlockSpec((tk, tm), lambda i, j, k: (k, i))
    else:
        a_spec = pl.BlockSpec((tm, tk), lambda i, j, k: (i, k))
    if b_blocked and mode != "nt":
        per = w // tn
        b_spec = pl.BlockSpec((None, tk, tn), lambda i, j, k: (j // per, k, j % per))
    elif b_blocked:
        per = w // tk
        b_spec = pl.BlockSpec((None, tn, tk), lambda i, j, k: (k // per, j, k % per))
    elif mode == "nt":
        b_spec = pl.BlockSpec((tn, tk), lambda i, j, k: (j, k))
    else:
        b_spec = pl.BlockSpec((tk, tn), lambda i, j, k: (k, j))
    if o_width is None:
        o_spec = pl.BlockSpec((tm, tn), lambda i, j, k: (i, j))
        o_shape = (m, n)
    else:
        oper = o_width // tn
        o_spec = pl.BlockSpec((None, tm, tn), lambda i, j, k: (j // oper, i, j % oper))
        o_shape = (n // o_width, m, o_width)
    a_list, a_specs = [a], [a_spec]
    if a_parts:
        hold = lambda x, s: jnp.clip(x - s * a_per, 0, a_per - 1)
        a_list = list(a_parts)
        a_specs = [(pl.BlockSpec((tk, tm), lambda i, j, k, s=s: (k, hold(i, s))) if mode == "tn"
                    else pl.BlockSpec((tm, tk), lambda i, j, k, s=s: (i, hold(k, s)))) for s in range(na)]
    b_list, b_specs = [b], [b_spec]
    if b_parts:
        hold_b = lambda x, s: jnp.clip(x - s * per, 0, per - 1)
        b_list = list(b_parts)
        b_specs = [(pl.BlockSpec((tn, tk), lambda i, j, k, s=s: (j, hold_b(k, s))) if mode == "nt"
                    else pl.BlockSpec((tk, tn), lambda i, j, k, s=s: (k, hold_b(j, s)))) for s in range(nb)]
    n_a, n_b = len(a_list), len(b_list)
    has_add = add is not None
    n_in = n_a + n_b + has_add + (after is not None)
    in_place = nk > 1 and out_dtype == F32

    def body(*refs):
        add_ref = refs[n_a + n_b] if has_add else None
        o_ref = refs[n_in]
        i, j, k = pl.program_id(0), pl.program_id(1), pl.program_id(2)

        def finish(r):
            if has_add:
                r = r + add_scale * add_ref[...]
            o_ref[...] = r.astype(out_dtype)

        def step(a_ref, b_ref):
            if nk == 1:
                finish(_dot(a_ref[...], b_ref[...], dn))
                return
            acc = o_ref if in_place else refs[-1]

            @pl.when(k == 0)
            def _():
                acc[...] = _dot(a_ref[...], b_ref[...], dn)

            @pl.when(k > 0)
            def _():
                acc[...] += _dot(a_ref[...], b_ref[...], dn)

        if n_a == 1 and n_b == 1:
            step(refs[0], refs[1])
        else:
            slab_a = ((i if mode == "tn" else k) // a_per) if n_a > 1 else 0
            slab_b = ((k if mode == "nt" else j) // per) if n_b > 1 else 0
            for sa in range(n_a):
                for sb in range(n_b):
                    pl.when((slab_a == sa) & (slab_b == sb))(functools.partial(step, refs[sa], refs[n_a + sb]))
        if nk > 1 and not (in_place and not has_add):
            @pl.when(k == nk - 1)
            def _():
                finish((o_ref if in_place else refs[-1])[...])

    in_specs = a_specs + b_specs + ([pl.BlockSpec((tm, tn), lambda i, j, k: (i, j))] if has_add else [])
    args = (*a_list, *b_list) + ((add,) if has_add else ())
    if after is not None:
        in_specs.append(pl.BlockSpec(memory_space=pl.ANY))
        args += (after,)
    return pl.pallas_call(
        body, name=name, grid=(m // tm, n // tn, nk),
        in_specs=in_specs, out_specs=o_spec,
        out_shape=jax.ShapeDtypeStruct(o_shape, out_dtype),
        scratch_shapes=[pltpu.VMEM((tm, tn), F32)] if nk > 1 and not in_place else [],
        compiler_params=_params("parallel", "parallel", "arbitrary"),
    )(*args)


def _shift_down(u, s):
    return jnp.where(_iota(u.shape, 0) >= s, pltpu.roll(u, s, 0), 0.0)


def _shift_up(u, s):
    t = u.shape[0]
    return jnp.where(_iota(u.shape, 0) < t - s, pltpu.roll(u, t - s, 0), 0.0)


SLAB = 8


def _rolled(u):
    return pltpu.roll(u, 2, 0), pltpu.roll(u, 1, 0)


def _conv(u, w, rolled=None):
    u2, u1 = _rolled(u) if rolled is None else rolled
    raw = w[0:1] * u2 + w[1:2] * u1 + w[2:3] * u
    head = u[0:SLAB]
    mended = w[0:1] * _shift_down(head, 2) + w[1:2] * _shift_down(head, 1) + w[2:3] * head
    return jnp.concatenate([mended, raw[SLAB:]], axis=0)


def _conv_t(dy, w):
    t = dy.shape[0]
    raw = w[2:3] * dy + w[1:2] * pltpu.roll(dy, t - 1, 0) + w[0:1] * pltpu.roll(dy, t - 2, 0)
    tail = dy[t - SLAB:]
    mended = w[2:3] * tail + w[1:2] * _shift_up(tail, 1) + w[0:1] * _shift_up(tail, 2)
    return jnp.concatenate([raw[:t - SLAB], mended], axis=0)


def _conv_dw(dy, u, rolled=None):
    t = dy.shape[0]
    u2, u1 = _rolled(u) if rolled is None else rolled
    head, tail = dy[0:SLAB], u[t - SLAB:]
    r = _iota(head.shape, 0)
    wrap2 = jnp.sum(jnp.where(r < 2, head * pltpu.roll(tail, 2, 0), 0.0), axis=0, keepdims=True)
    wrap1 = jnp.sum(jnp.where(r < 1, head * pltpu.roll(tail, 1, 0), 0.0), axis=0, keepdims=True)
    d0 = jnp.sum(dy * u2, axis=0, keepdims=True) - wrap2
    d1 = jnp.sum(dy * u1, axis=0, keepdims=True) - wrap1
    d2 = jnp.sum(dy * u, axis=0, keepdims=True)
    r3 = _iota((3, dy.shape[1]), 0)
    return jnp.where(r3 == 0, d0, jnp.where(r3 == 1, d1, d2))


def _sigmoid(x):
    return 0.5 * jnp.tanh(0.5 * x) + 0.5


def _sconv_fwd(proj, w_sc, t, wc):
    nb = wc // LANE

    def body(cb_ref, cc_ref, ch_ref, w_ref, y_ref):
        u = cc_ref[...] * ch_ref[...]
        y_ref[...] = (cb_ref[...] * _conv(u, w_ref[...])).astype(BF16)

    col = lambda off: pl.BlockSpec((t, LANE), lambda j: (0, j + off))
    return pl.pallas_call(
        body, name="sconv_fwd", grid=(nb,),
        in_specs=[col(0), col(nb), col(2 * nb), pl.BlockSpec((3, LANE), lambda j: (0, j))],
        out_specs=pl.BlockSpec((None, t, LANE), lambda j: (0, 0, j)),
        out_shape=jax.ShapeDtypeStruct((2, t, wc), BF16),
        compiler_params=_params("parallel"),
    )(proj, proj, proj, w_sc)


def _sconv_bwd(dy, proj, w_sc, t, wc, after=None):
    nb = wc // LANE

    def body(dy_ref, cb_ref, cc_ref, ch_ref, w_ref, dcb_ref, dcc_ref, dch_ref, dw_ref):
        cc, ch, w, d = cc_ref[...], ch_ref[...], w_ref[...], dy_ref[...]
        u = cc * ch
        ru = _rolled(u)
        dcb_ref[...] = (d * _conv(u, w, ru)).astype(BF16)
        dcu = d * cb_ref[...]
        dw_ref[...] = _conv_dw(dcu, u, ru)
        du = _conv_t(dcu, w)
        dcc_ref[...] = (du * ch).astype(BF16)
        dch_ref[...] = (du * cc).astype(BF16)

    col = lambda off: pl.BlockSpec((t, LANE), lambda j: (0, j + off))
    act = jax.ShapeDtypeStruct((t, wc), BF16)
    body, in_specs, args = _placed(
        after, body, [col(0), col(0), col(nb), col(2 * nb), pl.BlockSpec((3, LANE), lambda j: (0, j))],
        (dy, proj, proj, proj, w_sc))
    return pl.pallas_call(
        body, name="sconv_bwd", grid=(nb,),
        in_specs=in_specs,
        out_specs=[col(0), col(0), col(0), pl.BlockSpec((3, LANE), lambda j: (0, j))],
        out_shape=[act, act, act, jax.ShapeDtypeStruct((3, wc), F32)],
        compiler_params=_params("parallel"),
    )(*args)


def _gates_prep(proj, bias_tile, t, gate_tile):
    def body(g_ref, b_ref, o_ref):
        g = g_ref[...] + b_ref[...]
        lane = _iota(g.shape, 1)
        is_f = (lane >= NH) & (lane < 2 * NH)
        lf = jnp.minimum(g, 0.0) - jnp.log(1.0 + jnp.exp(-jnp.abs(g)))
        c = jnp.where(is_f, lf, 0.0)
        r = _iota(g.shape, 0) % CHUNK
        s = 1
        while s < CHUNK:
            c = c + jnp.where(r >= s, pltpu.roll(c, s, 0), 0.0)
            s *= 2
        o_ref[...] = jnp.where(is_f, c, jnp.where(lane < NH, g, 0.0))

    return pl.pallas_call(
        body, name="gates_prep", grid=(1,),
        in_specs=[pl.BlockSpec((t, LANE), lambda i: (0, gate_tile)), pl.BlockSpec((1, LANE), lambda i: (0, 0))],
        out_specs=pl.BlockSpec((t, LANE), lambda i: (0, 0)),
        out_shape=jax.ShapeDtypeStruct((t, LANE), F32),
        compiler_params=_params("arbitrary"),
    )(proj, bias_tile)


def _gates_bwd(dgate, proj, bias_tile, d_proj, t, gate_tile):
    def body(dg_ref, g_ref, b_ref, d_proj_in, o_ref, s_ref):
        g = g_ref[...] + b_ref[...]
        lane = _iota(g.shape, 1)
        r = _iota(g.shape, 0) % CHUNK
        dsig = 1.0 - _sigmoid(g)
        out = jnp.zeros(g.shape, F32)
        for h in range(NH):
            d = dg_ref[h]
            c = d
            s = 1
            while s < CHUNK:
                c = c + jnp.where(r + s < CHUNK, pltpu.roll(c, t - s, 0), 0.0)
                s *= 2
            di = jnp.broadcast_to(d[:, 0:1], g.shape)
            db = jnp.broadcast_to(c[:, 1:2], g.shape)
            out = out + jnp.where(lane == h, di, 0.0) + jnp.where(lane == NH + h, db * dsig, 0.0)
        o_ref[...] = out.astype(BF16)
        s_ref[...] = jnp.sum(out, axis=0, keepdims=True)

    return pl.pallas_call(
        body, name="gates_bwd", grid=(1,),
        in_specs=[pl.BlockSpec((NH, t, LANE), lambda i: (0, 0, 0)),
                  pl.BlockSpec((t, LANE), lambda i: (0, gate_tile)), pl.BlockSpec((1, LANE), lambda i: (0, 0)),
                  pl.BlockSpec(memory_space=pl.ANY)],
        out_specs=[pl.BlockSpec((t, LANE), lambda i: (0, gate_tile)), pl.BlockSpec((1, LANE), lambda i: (0, 0))],
        out_shape=[jax.ShapeDtypeStruct(d_proj.shape, d_proj.dtype), jax.ShapeDtypeStruct((1, LANE), F32)],
        input_output_aliases={3: 0},
        compiler_params=_params("arbitrary"),
    )(dgate, proj, bias_tile, d_proj)


def _in_turn(heads):
    while heads:
        heads = [g for g in heads if next(g, heads) is not heads]


def _chunk_gates(gc, gr, h, mprev):
    L = CHUNK
    icol, bcol = gc[:, h:h + 1], gc[:, h + NH:h + NH + 1]
    irow, brow = gr[h:h + 1, :], gr[h + NH:h + NH + 1, :]
    tri = _iota((L, L), 0) >= _iota((L, L), 1)
    log_d = jnp.where(tri, bcol - brow + irow, -jnp.inf)
    inter = bcol + mprev
    mt = jnp.maximum(inter, jnp.max(log_d, axis=1, keepdims=True))
    dw = jnp.exp(log_d - mt)
    iw = jnp.exp(inter - mt)
    g = brow[:, L - 1:L]
    wlog_col = g - bcol + icol
    wlog_row = g - brow + irow
    mnew = jnp.maximum(g + mprev, jnp.max(wlog_row, axis=1, keepdims=True))
    wcol = jnp.exp(wlog_col - mnew)
    decay = jnp.exp(g + mprev - mnew)
    return dw, iw, mt, wcol, decay, mnew


def _mlstm_fwd(proj, gcol, grow, t, wc, dh):
    nc = t // CHUNK
    wm = NH * dh
    assert wc == wm, (wc, wm)
    qoff = 3 * wc // wm
    scale = dh ** -0.5

    def body(q_ref, k_ref, v_ref, gc_ref, gr_ref, h_ref, cs_ref, ns_ref, c_s, n_s, m_s):
        @pl.when(pl.program_id(0) == 0)
        def _():
            c_s[...] = jnp.zeros_like(c_s)
            n_s[...] = jnp.zeros_like(n_s)
            m_s[...] = jnp.zeros_like(m_s)

        gc, gr = gc_ref[...], gr_ref[0]
        done = [None] * NH

        def head(h):
            cols = slice(h * dh, (h + 1) * dh)
            mprev = m_s[h, 0:1, 0:1]
            cprev = c_s[h]
            n8 = n_s[h]
            nprev = n8[0:1]
            qs = q_ref[:, cols] * scale
            k = k_ref[:, cols]
            qs_b, k_b, v_b = qs.astype(BF16), k.astype(BF16), v_ref[:, cols].astype(BF16)
            qk = _dot(qs_b, k_b, _NT)
            yield
            q_c = _dot(qs_b, cprev.astype(BF16))
            yield
            dw, iw, mt, wcol, decay, mnew = _chunk_gates(gc, gr, h, mprev)
            yield
            s = qk * dw
            wk = wcol * k
            num = _dot(s.astype(BF16), v_b) + iw * q_c
            yield
            c_new = decay * cprev + _dot(wk.astype(BF16), v_b, _TN)
            yield
            den = jnp.sum(s, axis=1, keepdims=True) + iw * jnp.sum(qs * nprev, axis=1, keepdims=True)
            done[h] = (cprev, jnp.where(_iota(n8.shape, 0) == 1, mprev, n8),
                       num / jnp.maximum(jnp.abs(den), jnp.exp(-mt)), c_new,
                       decay * n8 + jnp.sum(wk, axis=0, keepdims=True), mnew)

        _in_turn([head(h) for h in range(NH)])
        for h, (c_old, n_old, h_out, c_new, n_new, m_new) in enumerate(done):
            cs_ref[h] = c_old
            ns_ref[h] = n_old
            h_ref[:, h * dh:(h + 1) * dh] = h_out
            c_s[h] = c_new
            n_s[h] = n_new
            m_s[h] = jnp.broadcast_to(m_new, m_s.shape[1:])

    grp = lambda off: pl.BlockSpec((CHUNK, wm), lambda c: (c, qoff + off))
    return pl.pallas_call(
        body, name="mlstm_fwd", grid=(nc,),
        in_specs=[grp(0), grp(1), grp(2),
                  pl.BlockSpec((CHUNK, LANE), lambda c: (c, 0)),
                  pl.BlockSpec((1, 8, CHUNK), lambda c: (c, 0, 0))],
        out_specs=[pl.BlockSpec((CHUNK, wm), lambda c: (c, 0)),
                   pl.BlockSpec((NH, None, dh, dh), lambda c: (0, c, 0, 0)),
                   pl.BlockSpec((NH, None, 8, dh), lambda c: (0, c, 0, 0))],
        out_shape=[jax.ShapeDtypeStruct((t, wm), F32),
                   jax.ShapeDtypeStruct((NH, nc, dh, dh), F32),
                   jax.ShapeDtypeStruct((NH, nc, 8, dh), F32)],
        scratch_shapes=[pltpu.VMEM((NH, dh, dh), F32), pltpu.VMEM((NH, 8, dh), F32), pltpu.VMEM((NH, 8, LANE), F32)],
        compiler_params=_params("arbitrary"),
    )(proj, proj, proj, gcol, grow)


def _mlstm_bwd(proj, gcol, grow, hval, dh_in, cs, ns, d_proj, t, wc, dh):
    nc = t // CHUNK
    wm = NH * dh
    assert wc == wm, (wc, wm)
    qoff = 3 * wc // wm
    scale = dh ** -0.5
    L = CHUNK

    def body(q_ref, k_ref, v_ref, gc_ref, gr_ref, h_ref, dh_ref, cs_ref, ns_ref, d_proj_in,
             dqkv_ref, dg_ref, dc_s, dn_s):
        @pl.when(pl.program_id(0) == 0)
        def _():
            dc_s[...] = jnp.zeros_like(dc_s)
            dn_s[...] = jnp.zeros_like(dn_s)

        gc, gr = gc_ref[...], gr_ref[0]
        eye = _iota((L, L), 0) == _iota((L, L), 1)
        lane = _iota((L, LANE), 1)
        last = _iota((L, 1), 0) == L - 1
        done = [None] * NH

        def head(h):
            cols = slice(h * dh, (h + 1) * dh)
            ns8 = ns_ref[h]
            nprev = ns8[0:1]
            mprev = ns8[1:2, 0:1]
            cprev = cs_ref[h]
            dcn = dc_s[h]
            dn8 = dn_s[h]
            dnn = dn8[0:1]

            qs = q_ref[:, cols] * scale
            k = k_ref[:, cols]
            qs_b, k_b, v_b = qs.astype(BF16), k.astype(BF16), v_ref[:, cols].astype(BF16)
            qk = _dot(qs_b, k_b, _NT)
            yield
            dw, iw, mt, wcol, decay, _ = _chunk_gates(gc, gr, h, mprev)
            yield
            s = qk * dw
            den = jnp.sum(s, axis=1, keepdims=True) + iw * jnp.sum(qs * nprev, axis=1, keepdims=True)
            emt = jnp.exp(-mt)
            r = 1.0 / jnp.maximum(jnp.abs(den), emt)
            dout = dh_ref[:, cols]
            dnum = dout * r
            dden = (-jnp.sum(dout * h_ref[:, cols], axis=1, keepdims=True) * r
                    * jnp.where(jnp.abs(den) > emt, jnp.sign(den), 0.0))
            dnum_b = dnum.astype(BF16)
            cprev_b = cprev.astype(BF16)
            dcn_b = dcn.astype(BF16)
            yield

            g_raw = _dot(dnum_b, v_b, _NT)
            yield
            q_inter = _dot(dnum_b, cprev_b, _NT)
            yield
            k_raw = _dot(v_b, dcn_b, _NT)
            yield
            gd = (g_raw + dden) * dw
            gd_b = gd.astype(BF16)
            dqs_inter = iw * (q_inter + dden * nprev)
            dk_inter = wcol * (k_raw + dnn)
            wk = wcol * k
            iq = iw * qs
            dqs = _dot(gd_b, k_b) + dqs_inter
            yield
            dk = _dot(gd_b, qs_b, _TN) + dk_inter
            yield
            dv = _dot(s.astype(BF16), dnum_b, _TN) + _dot(wk.astype(BF16), dcn_b)
            yield
            dc_new = decay * dcn + _dot(iq.astype(BF16), dnum_b, _TN)
            yield

            e = gd * qk
            e_cols = jnp.sum(jnp.where(eye, jnp.sum(e, axis=0, keepdims=True), 0.0), axis=1, keepdims=True)
            yield
            k_inter = jnp.sum(k * dk_inter, axis=1, keepdims=True)
            rq = jnp.sum(e, axis=1, keepdims=True) + jnp.sum(qs * dqs_inter, axis=1, keepdims=True)
            rk = e_cols + k_inter
            hsum = jnp.sum(k_inter, axis=0, keepdims=True)
            jdec = decay * (jnp.sum(jnp.sum(dcn * cprev, axis=1, keepdims=True), axis=0, keepdims=True)
                            + jnp.sum(dnn * nprev, axis=1, keepdims=True))
            db = rq - rk + jnp.where(last, hsum + jdec, 0.0)
            done[h] = (jnp.where(lane == 0, rk, jnp.where(lane == 1, db, 0.0)),
                       (dqs * scale).astype(BF16), dk.astype(BF16), dv.astype(BF16), dc_new,
                       decay * dn8 + jnp.sum(iq * dden, axis=0, keepdims=True))

        _in_turn([head(h) for h in range(NH)])
        for h, (dgate, dq, dk, dv, dc_new, dn_new) in enumerate(done):
            dg_ref[h] = dgate
            for part, grad in enumerate((dq, dk, dv)):
                dqkv_ref[:, part * wm + h * dh:part * wm + (h + 1) * dh] = grad
            dc_s[h] = dc_new
            dn_s[h] = dn_new

    rc = lambda c: nc - 1 - c
    grp = lambda off: pl.BlockSpec((L, wm), lambda c: (rc(c), qoff + off))
    hm = pl.BlockSpec((L, wm), lambda c: (rc(c), 0))
    assert qoff % 3 == 0, qoff
    return pl.pallas_call(
        body, name="mlstm_bwd", grid=(nc,),
        in_specs=[grp(0), grp(1), grp(2),
                  pl.BlockSpec((L, LANE), lambda c: (rc(c), 0)),
                  pl.BlockSpec((1, 8, L), lambda c: (rc(c), 0, 0)),
                  hm, hm,
                  pl.BlockSpec((NH, None, dh, dh), lambda c: (0, rc(c), 0, 0)),
                  pl.BlockSpec((NH, None, 8, dh), lambda c: (0, rc(c), 0, 0)),
                  pl.BlockSpec(memory_space=pl.ANY)],
        out_specs=[pl.BlockSpec((L, 3 * wm), lambda c: (rc(c), qoff // 3)),
                   pl.BlockSpec((NH, L, LANE), lambda c: (0, rc(c), 0))],
        out_shape=[jax.ShapeDtypeStruct(d_proj.shape, d_proj.dtype), jax.ShapeDtypeStruct((NH, t, LANE), F32)],
        input_output_aliases={9: 0},
        scratch_shapes=[pltpu.VMEM((NH, dh, dh), F32), pltpu.VMEM((NH, 8, dh), F32)],
        compiler_params=_params("arbitrary"),
    )(proj, proj, proj, gcol, grow, hval, dh_in, cs, ns, d_proj)


def _head_norm(hv):
    mu = jnp.mean(hv, axis=1, keepdims=True)
    hc = hv - mu
    rstd = lax.rsqrt(jnp.mean(hc * hc, axis=1, keepdims=True) + HN_EPS)
    return hc * rstd, rstd


def _hnorm_fwd(hval, proj, gain, y, t, wc, dh, tr=512):
    ooff = 3 * wc // dh + 3 * NH
    tr = min(tr, t)

    def body(h_ref, o_ref, g_ref, y_in, y_ref):
        hhat, _ = _head_norm(h_ref[...])
        y_ref[...] = (_sigmoid(o_ref[...]) * hhat * g_ref[...]).astype(BF16)

    return pl.pallas_call(
        body, name="hnorm_fwd", grid=(t // tr, NH),
        in_specs=[pl.BlockSpec((tr, dh), lambda i, h: (i, h)),
                  pl.BlockSpec((tr, dh), lambda i, h: (i, ooff + h)),
                  pl.BlockSpec((1, dh), lambda i, h: (0, h)),
                  pl.BlockSpec(memory_space=pl.ANY)],
        out_specs=pl.BlockSpec((None, tr, dh), lambda i, h: (1, i, h)),
        out_shape=jax.ShapeDtypeStruct(y.shape, BF16),
        input_output_aliases={3: 0},
        compiler_params=_params("parallel", "parallel"),
    )(hval, proj, gain, y)


def _hnorm_bwd(dy, hval, proj, gain, t, wc, dh, tr=512):
    ooff = 3 * wc // dh + 3 * NH
    tr = min(tr, t)
    yoff = wc // dh

    def body(dy_ref, h_ref, o_ref, g_ref, do_ref, dh_ref, dg_ref):
        i = pl.program_id(1)
        hhat, rstd = _head_norm(h_ref[...])
        gain_v = g_ref[...]
        sig = _sigmoid(o_ref[...])
        d = dy_ref[...]
        do_ref[...] = (d * hhat * gain_v * sig * (1.0 - sig)).astype(BF16)
        dhn = d * sig
        part = jnp.sum(dhn * hhat, axis=0, keepdims=True)

        @pl.when(i == 0)
        def _():
            dg_ref[...] = part

        @pl.when(i > 0)
        def _():
            dg_ref[...] += part

        dhat = dhn * gain_v
        dh_ref[...] = rstd * (dhat - jnp.mean(dhat, axis=1, keepdims=True)
                              - hhat * jnp.mean(dhat * hhat, axis=1, keepdims=True))

    blk = lambda off: pl.BlockSpec((tr, dh), lambda h, i: (i, off + h))
    return pl.pallas_call(
        body, name="hnorm_bwd", grid=(NH, t // tr),
        in_specs=[blk(yoff), blk(0), blk(ooff), pl.BlockSpec((1, dh), lambda h, i: (0, h))],
        out_specs=[blk(ooff), blk(0), pl.BlockSpec((1, dh), lambda h, i: (0, h))],
        out_shape=[jax.ShapeDtypeStruct(proj.shape, BF16), jax.ShapeDtypeStruct((t, NH * dh), F32),
                   jax.ShapeDtypeStruct((1, NH * dh), F32)],
        compiler_params=_params("parallel", "arbitrary"),
    )(dy, hval, proj, gain)


def _ln_stats(z):
    mu = jnp.mean(z, axis=1, keepdims=True)
    zc = z - mu
    rstd = lax.rsqrt(jnp.mean(zc * zc, axis=1, keepdims=True) + LN_EPS)
    return zc * rstd, rstd


def _ln_bwd(dy, xhat, rstd, g):
    dxh = dy * g
    return rstd * (dxh - jnp.mean(dxh, axis=1, keepdims=True) - xhat * jnp.mean(dxh * xhat, axis=1, keepdims=True))


def _accum(ref, i, part):
    @pl.when(i == 0)
    def _():
        ref[...] = part

    @pl.when(i > 0)
    def _():
        ref[...] += part


def _ln1_fwd(x, mix, g, b, tr=256, after=None):
    t, d = x.shape

    def body(x_ref, m_ref, g_ref, b_ref, xh_ref, rs_ref, xb_ref):
        xhat, rstd = _ln_stats(ALPHA * x_ref[...] + m_ref[...])
        xh_ref[...] = xhat
        rs_ref[...] = rstd
        xb_ref[...] = (xhat * g_ref[...] + b_ref[...]).astype(BF16)

    row = pl.BlockSpec((tr, d), lambda i: (i, 0))
    vec = pl.BlockSpec((1, d), lambda i: (0, 0))
    body, in_specs, args = _placed(after, body, [row, row, vec, vec], (x, mix, g, b))
    return pl.pallas_call(
        body, name="ln1_fwd", grid=(t // tr,),
        in_specs=in_specs,
        out_specs=[row, pl.BlockSpec((tr, 1), lambda i: (i, 0)), row],
        out_shape=[jax.ShapeDtypeStruct((t, d), F32), jax.ShapeDtypeStruct((t, 1), F32),
                   jax.ShapeDtypeStruct((t, d), BF16)],
        compiler_params=_params("parallel"),
    )(*args)


def _ln2_loss(xhat1, g1, b1, ff, target, g2, b2, tr=256):
    t, d = ff.shape

    def body(xh_ref, g1_ref, b1_ref, f_ref, t_ref, g_ref, b_ref, dz_ref, dzb_ref, dg_ref, db_ref, l_ref):
        i = pl.program_id(0)
        x1 = xh_ref[...] * g1_ref[...] + b1_ref[...]
        xhat, rstd = _ln_stats(ALPHA * x1 + f_ref[...])
        gv = g_ref[...]
        e = xhat * gv + b_ref[...] - t_ref[...]
        lsum = jnp.sum(jnp.sum(e * e, axis=1, keepdims=True), axis=0, keepdims=True) * (0.5 / d)
        dy = e * (1.0 / d)
        _accum(dg_ref, i, jnp.sum(dy * xhat, axis=0, keepdims=True))
        _accum(db_ref, i, jnp.sum(dy, axis=0, keepdims=True))
        _accum(l_ref, i, jnp.broadcast_to(lsum, l_ref.shape))
        dz = _ln_bwd(dy, xhat, rstd, gv)
        dz_ref[...] = dz
        dzb_ref[...] = dz.astype(BF16)

    row = pl.BlockSpec((tr, d), lambda i: (i, 0))
    vec = pl.BlockSpec((1, d), lambda i: (0, 0))
    return pl.pallas_call(
        body, name="ln2_loss", grid=(t // tr,),
        in_specs=[row, vec, vec, row, row, vec, vec],
        out_specs=[row, row, vec, vec, pl.BlockSpec((8, LANE), lambda i: (0, 0))],
        out_shape=[jax.ShapeDtypeStruct((t, d), F32), jax.ShapeDtypeStruct((t, d), BF16),
                   jax.ShapeDtypeStruct((1, d), F32), jax.ShapeDtypeStruct((1, d), F32),
                   jax.ShapeDtypeStruct((8, LANE), F32)],
        compiler_params=_params("arbitrary"),
    )(xhat1, g1, b1, ff, target, g2, b2)


def _ln1_bwd(dz2, dffn, xhat1, rstd1, g1, tr=256, after=None):
    t, d = dz2.shape

    def body(a_ref, f_ref, xh_ref, rs_ref, g_ref, dz_ref, dzb_ref, dg_ref, db_ref):
        i = pl.program_id(0)
        dy = ALPHA * a_ref[...] + f_ref[...]
        xhat = xh_ref[...]
        _accum(dg_ref, i, jnp.sum(dy * xhat, axis=0, keepdims=True))
        _accum(db_ref, i, jnp.sum(dy, axis=0, keepdims=True))
        dz = _ln_bwd(dy, xhat, rs_ref[...], g_ref[...])
        dz_ref[...] = dz
        dzb_ref[...] = dz.astype(BF16)

    row = pl.BlockSpec((tr, d), lambda i: (i, 0))
    vec = pl.BlockSpec((1, d), lambda i: (0, 0))
    body, in_specs, args = _placed(after, body, [row, row, row, pl.BlockSpec((tr, 1), lambda i: (i, 0)), vec],
                                   (dz2, dffn, xhat1, rstd1, g1))
    return pl.pallas_call(
        body, name="ln1_bwd", grid=(t // tr,),
        in_specs=in_specs,
        out_specs=[row, row, vec, vec],
        out_shape=[jax.ShapeDtypeStruct((t, d), F32), jax.ShapeDtypeStruct((t, d), BF16),
                   jax.ShapeDtypeStruct((1, d), F32), jax.ShapeDtypeStruct((1, d), F32)],
        compiler_params=_params("arbitrary"),
    )(*args)


def _ffn_act_fwd(hid0, w_fc, b_fc, t, dff, after=None):
    nb = dff // LANE

    def body(hv_ref, hg_ref, wv_ref, wg_ref, bv_ref, bg_ref, a_ref):
        val = _conv(hv_ref[...], wv_ref[...]) + bv_ref[...]
        gate = _conv(hg_ref[...], wg_ref[...]) + bg_ref[...]
        a_ref[...] = (gate * _sigmoid(gate) * val).astype(BF16)

    col = lambda off: pl.BlockSpec((t, LANE), lambda j: (0, j + off))
    w3 = lambda off: pl.BlockSpec((3, LANE), lambda j: (0, j + off))
    w1 = lambda off: pl.BlockSpec((1, LANE), lambda j: (0, j + off))
    body, in_specs, args = _placed(after, body, [col(0), col(nb), w3(0), w3(nb), w1(0), w1(nb)],
                                   (hid0, hid0, w_fc, w_fc, b_fc, b_fc))
    return pl.pallas_call(
        body, name="ffn_act_fwd", grid=(nb,),
        in_specs=in_specs,
        out_specs=col(0),
        out_shape=jax.ShapeDtypeStruct((t, dff), BF16),
        compiler_params=_params("parallel"),
    )(*args)


def _ffn_act_bwd(da, hid0, w_fc, b_fc, t, dff, after=None):
    nb = dff // LANE

    def body(da_ref, hv_ref, hg_ref, wv_ref, wg_ref, bv_ref, bg_ref,
             dhv_ref, dhg_ref, dwv_ref, dwg_ref, dbv_ref, dbg_ref):
        hv, hg, wv, wg = hv_ref[...], hg_ref[...], wv_ref[...], wg_ref[...]
        rv, rg = _rolled(hv), _rolled(hg)
        val = _conv(hv, wv, rv) + bv_ref[...]
        gate = _conv(hg, wg, rg) + bg_ref[...]
        sig = _sigmoid(gate)
        d = da_ref[...]
        dsig = d * sig
        dval = dsig * gate
        dgate = dsig * val * (1.0 + gate * (1.0 - sig))
        dhv_ref[...] = _conv_t(dval, wv).astype(BF16)
        dhg_ref[...] = _conv_t(dgate, wg).astype(BF16)
        dwv_ref[...] = _conv_dw(dval, hv, rv)
        dwg_ref[...] = _conv_dw(dgate, hg, rg)
        dbv_ref[...] = jnp.sum(dval, axis=0, keepdims=True)
        dbg_ref[...] = jnp.sum(dgate, axis=0, keepdims=True)

    col = lambda off: pl.BlockSpec((t, LANE), lambda j: (0, j + off))
    w3 = lambda off: pl.BlockSpec((3, LANE), lambda j: (0, j + off))
    w1 = lambda off: pl.BlockSpec((1, LANE), lambda j: (0, j + off))
    s3 = jax.ShapeDtypeStruct((3, dff), F32)
    s1 = jax.ShapeDtypeStruct((1, dff), F32)
    body, in_specs, args = _placed(after, body, [col(0), col(0), col(nb), w3(0), w3(nb), w1(0), w1(nb)],
                                   (da, hid0, hid0, w_fc, w_fc, b_fc, b_fc))
    return pl.pallas_call(
        body, name="ffn_act_bwd", grid=(nb,),
        in_specs=in_specs,
        out_specs=[col(0), col(0), w3(0), w3(0), w1(0), w1(0)],
        out_shape=[jax.ShapeDtypeStruct((t, dff), BF16)] * 2 + [s3, s3, s1, s1],
        compiler_params=_params("parallel"),
    )(*args)


class _Ready:
    def __init__(self, **weights):
        self.weights = weights

    def begin(self, after):
        return None

    def forward(self, name, after):
        return None

    def get(self, name, after):
        return self.weights[name]


class _Kept:
    def __init__(self):
        self.grads = {}

    def start(self, name, grad):
        self.grads[name] = grad
        return None

    def relay(self, name, after):
        return None

    def meanwhile(self, small, loss, after):
        return None


def _behind(a, token):
    return a if token is None else a + token[0:1, 0:1].reshape((1,) * a.ndim)


def _local_step(x, target, w_in, b_gates, w_sc, gain, w_out, ln1_g, ln1_b, w_up, w_fc, b_fc, w_down, ln2_g, ln2_b,
                gx=None, wx=None, x_b=None):
    t, d = x.shape
    wc = d // 2
    dh = (d - wc) // NH
    wm = NH * dh
    dff = w_fc.shape[1] // 2
    if wx is None:
        wx = _Ready(w_out=w_out, w_up=w_up, w_down=w_down)
    ninp = 3 * wc + 4 * wm + LANE
    nin = 3 * wc + 4 * wm
    gate_tile = nin // LANE
    nc = t // CHUNK
    bias_tile = jnp.pad(b_gates, ((0, 0), (0, LANE - 2 * NH)))

    if x_b is None:
        x_b = x.astype(BF16)
    proj = _matmul(x_b, w_in, "nt", F32, "proj", tm=512, tn=2432, tk=d, n=ninp, after=wx.begin(w_in))
    y = _sconv_fwd(proj, w_sc, t, wc)
    gcol = _gates_prep(proj, bias_tile, t, gate_tile)
    grow = gcol[:, :8].T.reshape(8, nc, CHUNK).transpose(1, 0, 2)
    hval, cs, ns = _mlstm_fwd(proj, gcol, grow, t, wc, dh)
    y = _hnorm_fwd(hval, proj, gain, y, t, wc, dh)
    tok = wx.forward("w_out", y)
    w_out = wx.get("w_out", tok)
    mix = _matmul(y, w_out, "nn", F32, "out_proj", tm=512, tn=1024, tk=wc, a_blocked=True, after=tok)
    xhat1, rstd1, x1_b = _ln1_fwd(x, mix, ln1_g, ln1_b, after=wx.forward("w_up", mix))
    w_up = wx.get("w_up", x1_b)
    wsl = w_up.shape[2]
    hid0 = _matmul(x1_b, w_up, "nn", F32, "ffn_up", tm=1024, tn=wsl, tk=d, b_blocked=True)
    act = _ffn_act_fwd(hid0, w_fc, b_fc, t, dff, after=wx.forward("w_down", hid0))
    w_down = wx.get("w_down", act)
    ff = _matmul(act, w_down, "nn", F32, "ffn_down", tm=1024, tn=512, tk=dff)
    dz2, dz2_b, d_ln2_g, d_ln2_b, loss = _ln2_loss(xhat1, ln1_g, ln1_b, ff, target, ln2_g, ln2_b)

    if gx is None:
        gx = _Kept()
    d_w_down = _matmul(act, dz2_b, "tn", BF16, "ffn_down_dw", tm=1408, tn=1024, tk=t)
    d_act = _matmul(dz2_b, w_down, "nt", F32, "ffn_down_dx", tm=2048, tn=512, tk=d, after=gx.start("w_down", d_w_down))
    *d_hid0, dwv, dwg, dbv, dbg = _ffn_act_bwd(d_act, hid0, w_fc, b_fc, t, dff, after=gx.relay("w_down", d_act))
    d_w_fc = jnp.concatenate([dwv, dwg], axis=1)
    d_b_fc = jnp.concatenate([dbv, dbg], axis=1)
    d_hid0 = tuple(d_hid0[:2])
    d_w_up = _matmul(x1_b, d_hid0, "tn", BF16, "ffn_up_dw", tm=1024, tn=wsl, tk=t, o_width=wsl)
    d_x1_ffn = _matmul(d_hid0, w_up, "nt", F32, "ffn_up_dx", tm=1024, tn=1024, tk=wsl, b_blocked=True,
                       after=gx.start("w_up", d_w_up))
    dz1, dz1_b, d_ln1_g, d_ln1_b = _ln1_bwd(dz2, d_x1_ffn, xhat1, rstd1, ln1_g, after=gx.relay("w_up", d_x1_ffn))

    d_w_out = _matmul(y, dz1_b, "tn", BF16, "out_proj_dw", tm=1024, tn=1024, tk=t, a_blocked=True)
    dy = _matmul(dz1_b, w_out, "nt", F32, "out_proj_dx", tm=1024, tn=1024, tk=d, after=gx.start("w_out", d_w_out))
    dcb, dcc, dch, d_w_sc = _sconv_bwd(dy, proj, w_sc, t, wc, after=gx.relay("w_out", dy))
    d_proj, d_hval, d_gain = _hnorm_bwd(dy, hval, proj, gain, t, wc, dh)
    d_proj, dgate = _mlstm_bwd(proj, gcol, grow, hval, d_hval, cs, ns, d_proj, t, wc, dh)
    d_proj, d_b_gates = _gates_bwd(dgate, proj, bias_tile, d_proj, t, gate_tile)
    for part, grad in enumerate((dcb, dcc, dch)):
        d_proj = lax.dynamic_update_slice(d_proj, grad, (0, part * wc))
    d_w_in = _matmul(d_proj, x_b, "tn", BF16, "proj_dw", tm=2432, tn=1024, tk=t)
    small = dict(b_gates=d_b_gates[:, :2 * NH], w_sc_conv=d_w_sc, mh_gain=d_gain, ln1_g=d_ln1_g, ln1_b=d_ln1_b,
                 w_ffn_conv=d_w_fc, b_ffn_conv=d_b_fc, ln2_g=d_ln2_g, ln2_b=d_ln2_b)
    token = gx.start("w_in", d_w_in)
    token = gx.relay("w_in", gx.meanwhile(small, loss, token))
    grad_x = _matmul(d_proj, w_in, "nn", F32, "proj_dx", tm=512, tn=512, tk=ninp, add=dz1, add_scale=ALPHA, after=token)
    return loss, grad_x, small, gx


HBM = pl.BlockSpec(memory_space=pltpu.HBM)


def _place():
    return lax.axis_index("x"), lax.axis_index("y"), lax.axis_index("c")


def _index(p):
    return 4 * p[0] + 2 * p[1] + p[2]


def _all_gather(arrs, name):
    n = len(arrs)

    def body(*refs):
        ins, outs = refs[:n], refs[n:2 * n]
        send_sems, recv_sems, local_sems = refs[2 * n:]
        x, y, c = _place()
        me, sibling = (x, y, c), (x, y, 1 - c)
        chips = [(1 - x, y), (x, 1 - y), (1 - x, 1 - y)]

        def copy(a, k, block, to, own=False):
            dst = outs[a].at[_index(block)]
            return pltpu.make_async_remote_copy(
                src_ref=ins[a] if own else dst, dst_ref=dst,
                send_sem=send_sems.at[k * n + a], recv_sem=recv_sems.at[k * n + a],
                device_id=to, device_id_type=MESH)

        mine = [pltpu.make_async_copy(ins[a], outs[a].at[_index(me)], local_sems.at[a]) for a in range(n)]
        for cp in mine:
            cp.start()
        first = []
        for a in range(n):
            first.append(copy(a, 0, me, sibling, own=True))
            first += [copy(a, 1 + j, me, (*chip, c), own=True) for j, chip in enumerate(chips)]
        for cp in first:
            cp.start()
        passed = []
        for j, chip in enumerate(chips):
            for a in range(n):
                copy(a, 1 + j, (*chip, c), me).wait_recv()
                cp = copy(a, 4 + j, (*chip, c), sibling)
                cp.start()
                passed.append(cp)
        for a in range(n):
            copy(a, 0, sibling, me).wait_recv()
            for j, chip in enumerate(chips):
                copy(a, 4 + j, (*chip, 1 - c), me).wait_recv()
        for cp in first + passed:
            cp.wait_send()
        for cp in mine:
            cp.wait()

    return pl.pallas_call(
        body, name=name, in_specs=[HBM] * n, out_specs=[HBM] * n,
        out_shape=[jax.ShapeDtypeStruct((N_DEV,) + a.shape, a.dtype) for a in arrs],
        scratch_shapes=[pltpu.SemaphoreType.DMA((7 * n,)), pltpu.SemaphoreType.DMA((7 * n,)),
                        pltpu.SemaphoreType.DMA((n,))],
    )(*arrs)


SEM = pl.BlockSpec(memory_space=pltpu.SEMAPHORE)
EFFECT = pltpu.SideEffectType.DATAFLOW_SIDE_EFFECTING


def _chips(x, y):
    return [(1 - x, y), (x, 1 - y), (1 - x, 1 - y)]


N_CHIP = N_DEV // 2


def _pair_route(x, y, c):
    return [((x, y, 1 - c), 2 * q + (1 - c), q, q) for q in range(N_CHIP)]


def _chip_route(x, y, c):
    mine = 2 * x + y
    return [((*chip, c), 2 * chip[0] + chip[1], mine, 2 * chip[0] + chip[1]) for chip in _chips(x, y)]


def _exchange_pieces(g_ref, land_ref, width, tail):
    if not tail:
        return [(lambda i: g_ref.at[i], lambda s: land_ref.at[s])]
    rows = lambda i, n: pl.ds(pl.multiple_of(i * width, IN_TAIL), n)
    return [(lambda i: g_ref.at[rows(i, width), :], lambda s: land_ref.at[s, pl.ds(0, width), :]),
            (lambda i: g_ref.at[rows(i + 1, IN_TAIL), :], lambda s: land_ref.at[s, pl.ds(width, IN_TAIL), :])]


def _chip_slot(x, y, c):
    return 2 * x + y


def _exchange_start(grad, route, tail, name, own_slot=None):
    width = IN_SLAB if tail else grad.shape[1]
    n_p = 2 if tail else 1
    n_c = len(route(0, 0, 0))
    land_shape = (N_CHIP, width + (IN_TAIL if tail else 0), grad.shape[-1])
    assert not (tail and own_slot)

    def body(g_ref, land_ref, send_sems, recv_sems, g_thru, land_thru, token):
        for j, (peer, slab, slot, _) in enumerate(route(*_place())):
            for p, (src, dst) in enumerate(_exchange_pieces(g_ref, land_ref, width, tail)):
                pltpu.make_async_remote_copy(src_ref=src(slab), dst_ref=dst(slot), send_sem=send_sems.at[j * n_p + p],
                                             recv_sem=recv_sems.at[j * n_p + p], device_id=peer,
                                             device_id_type=MESH).start()
        if own_slot:
            mine = own_slot(*_place())
            pltpu.make_async_copy(g_ref.at[mine], land_ref.at[mine], send_sems.at[n_c * n_p]).start()
        token[...] = jnp.zeros_like(token)

    return pl.pallas_call(
        body, name=name,
        out_shape=(pltpu.SemaphoreType.DMA((n_c * n_p + bool(own_slot),)), pltpu.SemaphoreType.DMA((n_c * n_p,)),
                   pltpu.HBM(grad.shape, grad.dtype), pltpu.HBM(land_shape, grad.dtype),
                   jax.ShapeDtypeStruct((8, LANE), F32)),
        in_specs=(HBM, HBM), out_specs=(SEM, SEM, HBM, HBM, pl.BlockSpec(memory_space=pltpu.VMEM)),
        input_output_aliases={0: 2, 1: 3},
        compiler_params=pltpu.CompilerParams(has_side_effects=EFFECT),
    )(pltpu.with_memory_space_constraint(grad, pltpu.HBM),
      pltpu.with_memory_space_constraint(lax.empty(land_shape, grad.dtype), pltpu.HBM))


def _exchange_wait(send_sems, recv_sems, g_thru, land_thru, after, route, tail, name, own_slot=None):
    width = IN_SLAB if tail else g_thru.shape[1]
    n_p = 2 if tail else 1

    def body(g_ref, land_ref, send_sems, recv_sems, after_ref, g_dead, got_ref):
        places = route(*_place())
        for j, (peer, slab, _, slot) in enumerate(places):
            for p, (src, dst) in enumerate(_exchange_pieces(g_ref, land_ref, width, tail)):
                cp = pltpu.make_async_remote_copy(src_ref=src(slab), dst_ref=dst(slot),
                                                  send_sem=send_sems.at[j * n_p + p], recv_sem=recv_sems.at[j * n_p + p],
                                                  device_id=peer, device_id_type=MESH)
                cp.wait_send()
                cp.wait_recv()
        if own_slot:
            mine = own_slot(*_place())
            pltpu.make_async_copy(g_ref.at[mine], land_ref.at[mine], send_sems.at[len(places) * n_p]).wait()

    return pl.pallas_call(
        body, name=name,
        out_shape=(pltpu.HBM(g_thru.shape, g_thru.dtype), pltpu.HBM(land_thru.shape, land_thru.dtype)),
        in_specs=(HBM, HBM, SEM, SEM, pl.BlockSpec(memory_space=pl.ANY)), out_specs=(HBM, HBM),
        input_output_aliases={0: 0, 1: 1},
        compiler_params=pltpu.CompilerParams(has_side_effects=EFFECT),
    )(g_thru, land_thru, send_sems, recv_sems, after)


def _pair_add(grad, pair, core, tail, name):
    rows, cols = (IN_SLAB if tail else grad.shape[1]), grad.shape[-1]
    total = pair.shape[1]

    def body(core_ref, *refs):
        if tail:
            g_ref, t_ref, p_ref, o_ref = refs
            o_ref[0:rows, :] = (g_ref[...].astype(F32) + p_ref[0:rows, :].astype(F32)).astype(BF16)
            o_ref[rows:total, :] = (t_ref[...].astype(F32) + p_ref[rows:total, :].astype(F32)).astype(BF16)
        else:
            g_ref, p_ref, o_ref = refs
            o_ref[...] = (g_ref[...].astype(F32) + p_ref[...].astype(F32)).astype(BF16)

    if tail:
        tc = _fit(cols, 512)
        grid = (N_CHIP, cols // tc)
        slab = pl.BlockSpec((None, total, tc), lambda q, i, core_ref: (q, 0, i))
        per = IN_SLAB // IN_TAIL
        in_specs = [pl.BlockSpec((rows, tc), lambda q, i, core_ref: (2 * q + core_ref[0], i)),
                    pl.BlockSpec((IN_TAIL, tc), lambda q, i, core_ref: ((2 * q + core_ref[0] + 1) * per, i))]
    else:
        tr = _rows(rows, 1024)
        grid = (N_CHIP, rows // tr)
        slab = pl.BlockSpec((None, tr, cols), lambda q, i, core_ref: (q, i, 0))
        in_specs = [pl.BlockSpec((None, tr, cols), lambda q, i, core_ref: (2 * q + core_ref[0], i, 0))]
    return pl.pallas_call(
        body, name=name,
        grid_spec=pltpu.PrefetchScalarGridSpec(num_scalar_prefetch=1, grid=grid,
                                               in_specs=in_specs + [slab], out_specs=slab),
        out_shape=jax.ShapeDtypeStruct(pair.shape, BF16),
        compiler_params=_params("parallel", "parallel"),
    )(core, *([grad, grad] if tail else [grad]), pair)


def _relay_places(x, y, c):
    came_from = (c * (1 - x) + (1 - c) * x, c * y + (1 - c) * (1 - y), c)
    pass_to = (c * x + (1 - c) * (1 - x), c * (1 - y) + (1 - c) * y, c)
    return 2 - c, came_from, pass_to, pass_to


OWN = 4


def _gather_start(blocks, after, name, spare=(), relayed=False):
    n = len(blocks)
    lands = [(N_DEV + (a in spare),) + b.shape for a, b in enumerate(blocks)]

    def body(*refs):
        b_refs, land_refs = refs[:n], refs[n:2 * n]
        send_sems, recv_sems = refs[2 * n + 1:3 * n + 1], refs[3 * n + 1:4 * n + 1]
        token = refs[-1]
        x, y, c = _place()
        me = _index((x, y, c))
        for a in range(n):
            targets = [(x, y, 1 - c)] + [(*chip, c) for chip in _chips(x, y)]
            for k, to in enumerate(targets[:3] if relayed else targets):
                pltpu.make_async_remote_copy(src_ref=b_refs[a], dst_ref=land_refs[a].at[me], send_sem=send_sems[a].at[k],
                                             recv_sem=recv_sems[a].at[k], device_id=to, device_id_type=MESH).start()
        for a in range(n):
            pltpu.make_async_copy(b_refs[a], land_refs[a].at[me], send_sems[a].at[OWN]).start()
        token[...] = jnp.zeros_like(token)

    sems = [pltpu.SemaphoreType.DMA((OWN + 1,))] * n
    out = pl.pallas_call(
        body, name=name,
        out_shape=(*sems, *sems, *[pltpu.HBM(b.shape, b.dtype) for b in blocks],
                   *[pltpu.HBM(s, b.dtype) for s, b in zip(lands, blocks)], jax.ShapeDtypeStruct((8, LANE), F32)),
        in_specs=(*[HBM] * (2 * n), pl.BlockSpec(memory_space=pl.ANY)),
        out_specs=(*[SEM] * (2 * n), *[HBM] * (2 * n), pl.BlockSpec(memory_space=pltpu.VMEM)),
        input_output_aliases={i: 2 * n + i for i in range(2 * n)},
        compiler_params=pltpu.CompilerParams(has_side_effects=EFFECT),
    )(*[pltpu.with_memory_space_constraint(b, pltpu.HBM) for b in blocks],
      *[pltpu.with_memory_space_constraint(lax.empty(s, b.dtype), pltpu.HBM) for s, b in zip(lands, blocks)], after)
    return [(out[a], out[n + a], out[2 * n + a], out[3 * n + a]) for a in range(n)], out[-1]


def _gather_relay(states, after, name):
    n, first_out = len(states), 3 * len(states) + len(after)

    def body(*refs):
        land_refs, send_sems, recv_sems = refs[:n], refs[n:2 * n], refs[2 * n:3 * n]
        pass_send, pass_recv = refs[first_out + n:first_out + 2 * n], refs[first_out + 2 * n:first_out + 3 * n]
        k_in, came_from, pass_to, _ = _relay_places(*_place())
        for a in range(n):
            slot = land_refs[a].at[_index(came_from)]
            pltpu.make_async_remote_copy(src_ref=slot, dst_ref=slot, send_sem=send_sems[a].at[k_in],
                                         recv_sem=recv_sems[a].at[k_in], device_id=came_from,
                                         device_id_type=MESH).wait_recv()
            pltpu.make_async_remote_copy(src_ref=slot, dst_ref=slot, send_sem=pass_send[a].at[0],
                                         recv_sem=pass_recv[a].at[0], device_id=pass_to, device_id_type=MESH).start()
        refs[-1][...] = jnp.zeros_like(refs[-1])

    lands = [st[3] for st in states]
    pair = [pltpu.SemaphoreType.DMA((1,))] * n
    out = pl.pallas_call(
        body, name=name,
        out_shape=(*[pltpu.HBM(l.shape, l.dtype) for l in lands], *pair, *pair, jax.ShapeDtypeStruct((8, LANE), F32)),
        in_specs=(*[HBM] * n, *[SEM] * (2 * n), *[pl.BlockSpec(memory_space=pl.ANY)] * len(after)),
        out_specs=(*[HBM] * n, *[SEM] * (2 * n), pl.BlockSpec(memory_space=pltpu.VMEM)),
        input_output_aliases={a: a for a in range(n)},
        compiler_params=pltpu.CompilerParams(has_side_effects=EFFECT),
    )(*lands, *[st[0] for st in states], *[st[1] for st in states], *after)
    return [(st[0], st[1], st[2], out[a], (out[n + a], out[2 * n + a])) for a, st in enumerate(states)], out[-1]


def _gather_forward(send_sems, recv_sems, b_thru, land_thru, after, name, passed=None):
    relayed = passed is not None

    def body(b_ref, land_ref, send_sems, recv_sems, *rest):
        pass_send, pass_recv = rest[:2] if relayed else (None, None)
        send2, recv2, token = rest[-3:]
        x, y, c = _place()
        sibling = (x, y, 1 - c)
        arrivals = [sibling] + [(*chip, c) for chip in _chips(x, y)]
        waits = [(send_sems.at[k], recv_sems.at[k], frm) for k, frm in enumerate(arrivals)]
        sends = [send_sems.at[k] for k in range(4)]
        if relayed:
            k_in, _, _, other = _relay_places(x, y, c)
            waits = [waits[0], (send_sems.at[3 - k_in], recv_sems.at[3 - k_in], other),
                     (pass_send.at[0], pass_recv.at[0], arrivals[3])]
            sends[3] = pass_send.at[0]
        for sem in sends:
            pltpu.make_async_remote_copy(src_ref=b_ref, dst_ref=land_ref.at[0], send_sem=sem, recv_sem=recv_sems.at[0],
                                         device_id=sibling, device_id_type=MESH).wait_send()
        pltpu.make_async_copy(b_ref, land_ref.at[_index((x, y, c))], send_sems.at[OWN]).wait()
        for send_sem, recv_sem, frm in waits:
            pltpu.make_async_remote_copy(src_ref=b_ref, dst_ref=land_ref.at[_index(frm)], send_sem=send_sem,
                                         recv_sem=recv_sem, device_id=frm, device_id_type=MESH).wait_recv()
        for j, chip in enumerate(_chips(x, y)):
            slot = land_ref.at[_index((*chip, c))]
            pltpu.make_async_remote_copy(src_ref=slot, dst_ref=slot, send_sem=send2.at[j], recv_sem=recv2.at[j],
                                         device_id=sibling, device_id_type=MESH).start()
        token[...] = jnp.zeros_like(token)

    extra = list(passed) if relayed else []
    return pl.pallas_call(
        body, name=name,
        out_shape=(pltpu.HBM(b_thru.shape, b_thru.dtype), pltpu.HBM(land_thru.shape, land_thru.dtype),
                   pltpu.SemaphoreType.DMA((3,)), pltpu.SemaphoreType.DMA((3,)), jax.ShapeDtypeStruct((8, LANE), F32)),
        in_specs=(HBM, HBM, SEM, SEM, *[SEM] * len(extra), pl.BlockSpec(memory_space=pl.ANY)),
        out_specs=(HBM, HBM, SEM, SEM, pl.BlockSpec(memory_space=pltpu.VMEM)),
        input_output_aliases={0: 0, 1: 1},
        compiler_params=pltpu.CompilerParams(has_side_effects=EFFECT),
    )(b_thru, land_thru, send_sems, recv_sems, *extra, after)


def _gather_finish(land_thru, send2, recv2, after, name):
    def body(land_ref, send2, recv2, after_ref, land_out):
        x, y, c = _place()
        for j, chip in enumerate(_chips(x, y)):
            cp = pltpu.make_async_remote_copy(src_ref=land_ref.at[_index((*chip, c))],
                                              dst_ref=land_ref.at[_index((*chip, 1 - c))], send_sem=send2.at[j],
                                              recv_sem=recv2.at[j], device_id=(x, y, 1 - c), device_id_type=MESH)
            cp.wait_send()
            cp.wait_recv()

    return pl.pallas_call(
        body, name=name, out_shape=pltpu.HBM(land_thru.shape, land_thru.dtype),
        in_specs=(HBM, SEM, SEM, pl.BlockSpec(memory_space=pl.ANY)), out_specs=HBM,
        input_output_aliases={0: 0},
        compiler_params=pltpu.CompilerParams(has_side_effects=EFFECT),
    )(land_thru, send2, recv2, after)


class _Gathering:
    def __init__(self, ahead, later, me):
        cast = [a.astype(BF16) for a in ahead.values()]
        started, self.token = _gather_start(cast, cast[0], "gather1_ahead", relayed=True)
        self.me, self.state, self.relayed, self.later = me, dict(zip(ahead, started)), tuple(ahead), later

    def start_first(self, first, spare):
        started, self.token = _gather_start(list(first.values()), self.token, "gather1_first",
                                            spare=(list(first).index(spare),), relayed=True)
        self.state.update(zip(first, started))
        self.relayed += tuple(first)

    def begin(self, after):
        return self.token

    def relay(self, *after):
        states, token = _gather_relay([self.state[n] for n in self.relayed], after, "gather_relay")
        self.state.update(zip(self.relayed, states))
        cast = [_behind(a, token).astype(BF16) for a in self.later.values()]
        started, self.token = _gather_start(cast, token, "gather1_later")
        self.state.update(zip(self.later, started))
        return self.token

    def forward(self, name, after):
        first_leg, passed = self.state[name][:4], (self.state[name][4:] or (None,))[0]
        *self.state[name], token = _gather_forward(*first_leg, after, "gather2_" + name, passed=passed)
        return token

    def get(self, name, after):
        _, land, send2, recv2 = self.state[name]
        land = _gather_finish(land, send2, recv2, after, "gather3_" + name)
        return land if name not in ("w_out", "w_down") else land.reshape(-1, land.shape[2])


class _Reducing:
    def __init__(self, core, chip, gather_small):
        self.core, self.chip, self.state, self.token, self.gather_small = core, chip, {}, None, gather_small

    def meanwhile(self, small, loss, after):
        self.small_sum = self.gather_small(small, loss, after)
        return self.small_sum

    def start(self, name, grad):
        tail = name == "w_in"
        g = grad if tail or grad.ndim == 3 else grad.reshape(N_DEV, grad.shape[0] // N_DEV, grad.shape[1])
        *self.state[name], token = _exchange_start(g, _pair_route, tail, "pair_send_" + name)
        return token

    def relay(self, name, after):
        tail = name == "w_in"
        grad, pair = _exchange_wait(*self.state[name], after, _pair_route, tail, "pair_recv_" + name)
        total = _pair_add(grad, pair, self.core, tail, "pair_add_" + name)
        *self.state[name], self.token = _exchange_start(total, _chip_route, False, "chip_send_" + name,
                                                        own_slot=_chip_slot)
        return self.token

    def finish(self, name, after):
        _, land = _exchange_wait(*self.state[name], after, _chip_route, False, "chip_recv_" + name, own_slot=_chip_slot)
        return land


def _carry_w_in(main, tail):
    slabs, _, d = main.shape
    tc = _fit(d, 2048)
    assert slabs == N_DEV + 1 and tail.shape[:2] == (N_DEV, IN_TAIL), (main.shape, tail.shape)
    top = lambda off: pl.BlockSpec((None, IN_TAIL, tc), lambda s, j: (s + off, 0, j))

    def carry(m_ref, t_ref, o_ref):
        o_ref[...] = m_ref[...] + t_ref[...]

    main = pl.pallas_call(
        carry, name="carry_w_in", grid=(N_DEV - 1, d // tc), in_specs=[top(1), top(0)], out_specs=top(1),
        out_shape=jax.ShapeDtypeStruct(main.shape, main.dtype), input_output_aliases={0: 0},
        compiler_params=_params("parallel", "parallel"),
    )(main, tail)

    def last(m_ref, t_ref, o_ref):
        o_ref[...] = jnp.zeros_like(o_ref)
        o_ref[0:IN_TAIL, :] = t_ref[...]

    return pl.pallas_call(
        last, name="last_slab_w_in", grid=(d // tc,),
        in_specs=[pl.BlockSpec(memory_space=pl.ANY), pl.BlockSpec((None, IN_TAIL, tc), lambda j: (N_DEV - 1, 0, j))],
        out_specs=pl.BlockSpec((None, LANE, tc), lambda j: (N_DEV, 0, j)),
        out_shape=jax.ShapeDtypeStruct(main.shape, main.dtype), input_output_aliases={0: 0},
        compiler_params=_params("parallel"),
    )(main, tail)


def _rows(n, want):
    t = min(n, want)
    t -= t % 16
    while n % t:
        t -= 16
    return t


def _adam_math(w, g, m, v):
    m2 = ADAM_B1 * m + (1.0 - ADAM_B1) * g
    v2 = ADAM_B2 * v + (1.0 - ADAM_B2) * (g * g)
    m_hat = m2 * (1.0 / (1.0 - ADAM_B1 ** ADAM_STEP))
    v_hat = v2 * (1.0 / (1.0 - ADAM_B2 ** ADAM_STEP))
    return -ADAM_LR * (m_hat / (jnp.sqrt(v_hat) + ADAM_EPS) + ADAM_WD * w), m2, v2


def _slot_sum(r_ref):
    acc = r_ref[0].astype(F32)
    for i in range(1, r_ref.shape[0]):
        acc = acc + r_ref[i].astype(F32)
    return acc


def _shift_w_in(w):
    ws, d = w.shape
    tc = _fit(d, 256)

    def body(w_ref, main_ref, tail_ref, tall):
        tall[...] = jnp.zeros_like(tall)
        tall[0:ws, :] = w_ref[...]
        moved = pltpu.roll(tall[...], _index(_place()), 0).astype(BF16)
        main_ref[...] = moved[0:IN_SLAB]
        tail_ref[...] = moved[IN_SLAB:]

    return pl.pallas_call(
        body, name="shift_w_in", grid=(d // tc,),
        in_specs=[pl.BlockSpec((ws, tc), lambda j: (0, j))],
        out_specs=[pl.BlockSpec((IN_SLAB, tc), lambda j: (0, j)), pl.BlockSpec((IN_TAIL, tc), lambda j: (0, j))],
        out_shape=[jax.ShapeDtypeStruct((IN_SLAB, d), BF16), jax.ShapeDtypeStruct((IN_TAIL, d), BF16)],
        scratch_shapes=[pltpu.VMEM((IN_SLAB + IN_TAIL, tc), F32)], compiler_params=_params("parallel"),
    )(w)


def _sum_adamw_shifted(r, w, m, v, name):
    _, ph, d = r.shape
    ws = w.shape[0]
    tc = _fit(d, 256)

    def body(r_ref, w_ref, m_ref, v_ref, g_ref, d_ref, m2_ref, v2_ref, tall):
        tall[...] = pltpu.roll(_slot_sum(r_ref), lax.rem(ph - _index(_place()), ph), 0)
        g = tall[0:ws, :]
        g_ref[...] = g
        d_ref[...], m2_ref[...], v2_ref[...] = _adam_math(w_ref[...], g, m_ref[...], v_ref[...])

    blk = pl.BlockSpec((ws, tc), lambda j: (0, j))
    out = jax.ShapeDtypeStruct(w.shape, F32)
    return pl.pallas_call(
        body, name=name, grid=(d // tc,),
        in_specs=[pl.BlockSpec((r.shape[0], ph, tc), lambda j: (0, 0, j)), blk, blk, blk],
        out_specs=[blk] * 4, out_shape=[out] * 4,
        scratch_shapes=[pltpu.VMEM((ph, tc), F32)], compiler_params=_params("parallel"),
    )(r, w, m, v)


def _sum_slots(r, name, tr=128):
    _, rows, cols = r.shape
    tr = _rows(rows, tr)

    def body(r_ref, g_ref):
        g_ref[...] = _slot_sum(r_ref)

    return pl.pallas_call(
        body, name=name, grid=(rows // tr,),
        in_specs=[pl.BlockSpec((r.shape[0], tr, cols), lambda i: (0, i, 0))],
        out_specs=pl.BlockSpec((tr, cols), lambda i: (i, 0)),
        out_shape=jax.ShapeDtypeStruct((rows, cols), F32),
        compiler_params=_params("parallel"),
    )(r)


def _adamw(w, g, m, v, name, tr=256):
    rows, cols = w.shape
    tr = _rows(rows, tr)

    def body(w_ref, g_ref, m_ref, v_ref, d_ref, m2_ref, v2_ref):
        d_ref[...], m2_ref[...], v2_ref[...] = _adam_math(w_ref[...], g_ref[...], m_ref[...], v_ref[...])

    blk = pl.BlockSpec((tr, cols), lambda i: (i, 0))
    out = jax.ShapeDtypeStruct((rows, cols), F32)
    return pl.pallas_call(
        body, name=name, grid=(rows // tr,), in_specs=[blk] * 4, out_specs=[blk] * 3, out_shape=[out] * 3,
        compiler_params=_params("parallel"),
    )(w, g, m, v)


def _sum_adamw(r, w, m, v, name, tr=256):
    rows, cols = w.shape
    tr = _rows(rows, tr)

    def body(r_ref, w_ref, m_ref, v_ref, g_ref, d_ref, m2_ref, v2_ref):
        g = _slot_sum(r_ref)
        g_ref[...] = g
        d_ref[...], m2_ref[...], v2_ref[...] = _adam_math(w_ref[...], g, m_ref[...], v_ref[...])

    blk = pl.BlockSpec((tr, cols), lambda i: (i, 0))
    out = jax.ShapeDtypeStruct((rows, cols), F32)
    return pl.pallas_call(
        body, name=name, grid=(rows // tr,),
        in_specs=[pl.BlockSpec((r.shape[0], tr, cols), lambda i: (0, i, 0)), blk, blk, blk],
        out_specs=[blk] * 4, out_shape=[out] * 4,
        compiler_params=_params("parallel"),
    )(r, w, m, v)


def _pack(pieces, sizes):
    flat = [jnp.pad(p.reshape(-1).astype(F32), (0, s - p.size)) for p, s in zip(pieces, sizes)]
    total = sum(sizes)
    padded = -(-total // (16 * LANE)) * (16 * LANE)
    return jnp.pad(jnp.concatenate(flat), (0, padded - total)).reshape(-1, LANE)


def _unpack(packed, shapes, sizes):
    flat = packed.reshape(-1)
    out, off = [], 0
    for shp, s in zip(shapes, sizes):
        n = 1
        for k in shp:
            n *= k
        out.append(flat[off:off + n].reshape(shp))
        off += s
    return out


def _lanes(n):
    return -(-n // LANE) * LANE


WEIGHTS = ("w_in", "b_gates", "w_sc_conv", "mh_gain", "w_out", "ln1_g", "ln1_b", "w_up", "w_ffn_conv", "b_ffn_conv",
           "w_down", "ln2_g", "ln2_b")
BIG = ("w_in", "w_out", "w_up", "w_down")
SMALL = tuple(n for n in WEIGHTS if n not in BIG)


def kernel(x, w_in, b_gates, w_sc_conv, mh_gain, w_out, ln1_g, ln1_b, w_up, w_ffn_conv, b_ffn_conv, w_down, ln2_g, ln2_b, loss_target, m_w_in, m_b_gates, m_w_sc_conv, m_mh_gain, m_w_out, m_ln1_g, m_ln1_b, m_w_up, m_w_ffn_conv, m_b_ffn_conv, m_w_down, m_ln2_g, m_ln2_b, v_w_in, v_b_gates, v_w_sc_conv, v_mh_gain, v_w_out, v_ln1_g, v_ln1_b, v_w_up, v_w_ffn_conv, v_b_ffn_conv, v_w_down, v_ln2_g, v_ln2_b):
    w = dict(zip(WEIGHTS, (w_in, b_gates, w_sc_conv, mh_gain, w_out, ln1_g, ln1_b, w_up, w_ffn_conv, b_ffn_conv,
                           w_down, ln2_g, ln2_b)))
    m = dict(zip(WEIGHTS, (m_w_in, m_b_gates, m_w_sc_conv, m_mh_gain, m_w_out, m_ln1_g, m_ln1_b, m_w_up,
                           m_w_ffn_conv, m_b_ffn_conv, m_w_down, m_ln2_g, m_ln2_b)))
    v = dict(zip(WEIGHTS, (v_w_in, v_b_gates, v_w_sc_conv, v_mh_gain, v_w_out, v_ln1_g, v_ln1_b, v_w_up,
                           v_w_ffn_conv, v_b_ffn_conv, v_w_down, v_ln2_g, v_ln2_b)))
    me = _index(_place())
    d = x.shape[2]
    ws_in = w_in.shape[2]
    assert ws_in == IN_SLAB + 1 and N_DEV <= LANE, w_in.shape
    ninp = (N_DEV + 1) * IN_SLAB
    ws_sc, ws_fc = w_sc_conv.shape[2], w_ffn_conv.shape[2]

    wx = _Gathering({"w_out": w_out[0]}, {n: w[n][0] for n in ("w_up", "w_down")}, me)
    w_in_t = jnp.transpose(_behind(w_in[0], wx.token))
    w_in_main, w_in_tail = _shift_w_in(w_in_t)
    taps8 = lambda a: jnp.pad(a[0], ((0, 5), (0, 0)))
    at_once = ("w_sc", "w_fc", "w_tail", "w_in")
    wx.start_first(dict(zip(at_once, (taps8(w_sc_conv), taps8(w_ffn_conv), w_in_tail, w_in_main))), spare="w_in")
    x_b = _behind(x[0], wx.begin(None)).astype(BF16)
    m_in_t, v_in_t = (jnp.transpose(_behind(a[0], wx.begin(None))) for a in (m_w_in, v_w_in))
    token = wx.relay(x_b, m_in_t, v_in_t)
    for n in at_once:
        token = wx.forward(n, token)
    g_sc, g_fc, g_tail, g_in = (wx.get(n, token) for n in at_once)
    w_in_full = _carry_w_in(g_in, g_tail).reshape(ninp, d)
    w_sc_full = g_sc[:, :3].transpose(1, 0, 2).reshape(3, N_DEV * ws_sc)
    w_fc_full = g_fc[:, :3].transpose(1, 0, 2).reshape(3, N_DEV * ws_fc)

    xi, yi, ci = _place()
    names = ("loss",) + SMALL
    pieces = {}

    def gather_small(small, loss_t, after):
        pieces.update(small, loss=loss_t[0, :1])
        sizes = [_lanes(pieces[n].size) for n in names]
        (g_small,) = _all_gather([_behind(_pack([pieces[n] for n in names], sizes), after)], "gather_small")
        return _sum_slots(g_small, "sum_small", tr=g_small.shape[1])

    gx = _Reducing(jnp.reshape(ci, (1,)).astype(jnp.int32), 2 * xi + yi, gather_small)
    loss_t, grad_x, small, _ = _local_step(
        x[0], loss_target[0], w_in_full, b_gates, w_sc_full, mh_gain, None, ln1_g, ln1_b, None,
        w_fc_full, b_ffn_conv, None, ln2_g, ln2_b, gx=gx, wx=wx, x_b=x_b)

    grads, deltas, new_m, new_v = {}, {}, {}, {}
    for name in ("w_down", "w_up", "w_out"):
        grads[name], deltas[name], new_m[name], new_v[name] = _sum_adamw(
            gx.finish(name, gx.token), w[name][0], m[name][0], v[name][0], "adamw_" + name)

    summed = _unpack(gx.small_sum, [pieces[n].shape for n in names], [_lanes(pieces[n].size) for n in names])
    full = dict(zip(names, summed))
    full["w_sc_conv"] = lax.dynamic_slice(full["w_sc_conv"], (0, me * ws_sc), (3, ws_sc))
    full["w_ffn_conv"] = lax.dynamic_slice(full["w_ffn_conv"], (0, me * ws_fc), (3, ws_fc))
    for n in SMALL:
        grads[n] = full[n].reshape(w[n].shape)
    sizes = [_lanes(w[n].size) for n in SMALL]
    shapes = [w[n].shape for n in SMALL]
    packed = [_pack([t[n] for n in SMALL], sizes) for t in (w, grads, m, v)]
    small_out = _adamw(*packed, "adamw_small")
    for res, t in zip(small_out, (deltas, new_m, new_v)):
        t.update(zip(SMALL, _unpack(res, shapes, sizes)))

    done = sum(t[0:1, 0:1] for t in (grad_x, deltas["w_down"], deltas["w_up"], deltas["w_out"], small_out[0]))
    grads["w_in"], deltas["w_in"], new_m["w_in"], new_v["w_in"] = (
        jnp.transpose(a)[None] for a in _sum_adamw_shifted(gx.finish("w_in", done), w_in_t, m_in_t, v_in_t, "adamw_w_in"))

    big = lambda t: {n: (t[n].reshape(w[n].shape) if n in BIG else t[n]) for n in WEIGHTS}
    grads, deltas, new_m, new_v = big(grads), big(deltas), big(new_m), big(new_v)
    return (full["loss"].reshape(()), grad_x[None], *[grads[n] for n in WEIGHTS], *[deltas[n] for n in WEIGHTS],
            *[new_m[n] for n in WEIGHTS], *[new_v[n] for n in WEIGHTS])
```

```python
import functools

import jax
import jax.numpy as jnp
from jax import lax
from jax.experimental import pallas as pl
from jax.experimental.pallas import tpu as pltpu

F32 = jnp.float32
BF16 = jnp.bfloat16
MESH = pl.DeviceIdType.MESH

N_DEV = 8
NH = 4
CHUNK = 64
LN_EPS = 1e-5
HN_EPS = 1e-6
ALPHA = 2.0 ** 0.25
LANE = 128
IN_SLAB = 7 * LANE
IN_TAIL = 16
VMEM_LIMIT = 56 * 1024 * 1024
ADAM_LR, ADAM_B1, ADAM_B2, ADAM_EPS, ADAM_WD, ADAM_STEP = 0.001, 0.9, 0.999, 1e-08, 0.01, 10

_NN = (((1,), (0,)), ((), ()))
_NT = (((1,), (1,)), ((), ()))
_TN = (((0,), (0,)), ((), ()))


def _dot(a, b, dn=_NN):
    return lax.dot_general(a, b, dn, preferred_element_type=F32)


def _params(*sem):
    return pltpu.CompilerParams(dimension_semantics=sem if sem else None, vmem_limit_bytes=VMEM_LIMIT)


def _iota(shape, axis):
    return lax.broadcasted_iota(jnp.int32, shape, axis)


def _fit(n, want):
    if n <= want:
        return n
    t = want - want % LANE
    while n % t:
        t -= LANE
    return t


def _placed(after, body, in_specs, args):
    if after is None:
        return body, in_specs, args
    return (lambda after_ref, *refs: body(*refs)), [pl.BlockSpec(memory_space=pl.ANY)] + in_specs, (after,) + args


def _matmul(a, b, mode, out_dtype, name, tm=1024, tn=512, tk=1024, add=None, add_scale=1.0,
            a_blocked=False, b_blocked=False, o_width=None, after=None, n=None):
    a_parts = a if isinstance(a, tuple) else None
    b_parts = b if isinstance(b, tuple) else None
    if a_parts:
        a_blocked, (a_rows, wa), na = True, a[0].shape, len(a)
        kd, m = (a_rows, na * wa) if mode == "tn" else (na * wa, a_rows)
    elif a_blocked:
        na, a_rows, wa = a.shape
        kd, m = (a_rows, na * wa) if mode == "tn" else (na * wa, a_rows)
    elif mode == "tn":
        kd, m = a.shape
    else:
        m, kd = a.shape
    if b_parts:
        b_blocked, (rows, w), nb = True, b[0].shape, len(b)
    elif b_blocked:
        nb, rows, w = b.shape
    if b_blocked:
        n = rows if mode == "nt" else nb * w
        assert (nb * w if mode == "nt" else rows) == kd, (name, kd)
    else:
        n = n or (b.shape[0] if mode == "nt" else b.shape[1])
    tm, tn, tk = _fit(m, tm), _fit(n, tn), _fit(kd, tk)
    if a_blocked and mode == "tn":
        tm = _fit(wa, tm)
    if a_blocked and mode != "tn":
        tk = _fit(wa, tk)
    if b_blocked and mode != "nt":
        tn = _fit(w, tn)
    if b_blocked and mode == "nt":
        tk = _fit(w, tk)
    if o_width is not None:
        tn = _fit(o_width, tn)
    assert m % tm == 0 and n % tn == 0 and kd % tk == 0, (name, m, n, kd, tm, tn, tk)
    assert not (a_blocked and mode != "tn" and wa % tk) and not (b_blocked and mode == "nt" and w % tk), (name, tk)
    nk = kd // tk
    dn = {"nn": _NN, "nt": _NT, "tn": _TN}[mode]
    if a_blocked and mode == "tn":
        a_per = wa // tm
        a_spec = pl.BlockSpec((None, tk, tm), lambda i, j, k: (i // a_per, k, i % a_per))
    elif a_blocked:
        a_per = wa // tk
        a_spec = pl.BlockSpec((None, tm, tk), lambda i, j, k: (k // a_per, i, k % a_per))
    elif mode == "tn":
        a_spec = pl.BlockSpec((tk, tm), lambda i, j, k: (k, i))
    else:
        a_spec = pl.BlockSpec((tm, tk), lambda i, j, k: (i, k))
    if b_blocked and mode != "nt":
        per = w // tn
        b_spec = pl.BlockSpec((None, tk, tn), lambda i, j, k: (j // per, k, j % per))
    elif b_blocked:
        per = w // tk
        b_spec = pl.BlockSpec((None, tn, tk), lambda i, j, k: (k // per, j, k % per))
    elif mode == "nt":
        b_spec = pl.BlockSpec((tn, tk), lambda i, j, k: (j, k))
    else:
        b_spec = pl.BlockSpec((tk, tn), lambda i, j, k: (k, j))
    if o_width is None:
        o_spec = pl.BlockSpec((tm, tn), lambda i, j, k: (i, j))
        o_shape = (m, n)
    else:
        oper = o_width // tn
        o_spec = pl.BlockSpec((None, tm, tn), lambda i, j, k: (j // oper, i, j % oper))
        o_shape = (n // o_width, m, o_width)
    a_list, a_specs = [a], [a_spec]
    if a_parts:
        hold = lambda x, s: jnp.clip(x - s * a_per, 0, a_per - 1)
        a_list = list(a_parts)
        a_specs = [(pl.BlockSpec((tk, tm), lambda i, j, k, s=s: (k, hold(i, s))) if mode == "tn"
                    else pl.BlockSpec((tm, tk), lambda i, j, k, s=s: (i, hold(k, s)))) for s in range(na)]
    b_list, b_specs = [b], [b_spec]
    if b_parts:
        hold_b = lambda x, s: jnp.clip(x - s * per, 0, per - 1)
        b_list = list(b_parts)
        b_specs = [(pl.BlockSpec((tn, tk), lambda i, j, k, s=s: (j, hold_b(k, s))) if mode == "nt"
                    else pl.BlockSpec((tk, tn), lambda i, j, k, s=s: (k, hold_b(j, s)))) for s in range(nb)]
    n_a, n_b = len(a_list), len(b_list)
    has_add = add is not None
    n_in = n_a + n_b + has_add + (after is not None)
    in_place = nk > 1 and out_dtype == F32

    def body(*refs):
        add_ref = refs[n_a + n_b] if has_add else None
        o_ref = refs[n_in]
        i, j, k = pl.program_id(0), pl.program_id(1), pl.program_id(2)

        def finish(r):
            if has_add:
                r = r + add_scale * add_ref[...]
            o_ref[...] = r.astype(out_dtype)

        def step(a_ref, b_ref):
            if nk == 1:
                finish(_dot(a_ref[...], b_ref[...], dn))
                return
            acc = o_ref if in_place else refs[-1]

            @pl.when(k == 0)
            def _():
                acc[...] = _dot(a_ref[...], b_ref[...], dn)

            @pl.when(k > 0)
            def _():
                acc[...] += _dot(a_ref[...], b_ref[...], dn)

        if n_a == 1 and n_b == 1:
            step(refs[0], refs[1])
        else:
            slab_a = ((i if mode == "tn" else k) // a_per) if n_a > 1 else 0
            slab_b = ((k if mode == "nt" else j) // per) if n_b > 1 else 0
            for sa in range(n_a):
                for sb in range(n_b):
                    pl.when((slab_a == sa) & (slab_b == sb))(functools.partial(step, refs[sa], refs[n_a + sb]))
        if nk > 1 and not (in_place and not has_add):
            @pl.when(k == nk - 1)
            def _():
                finish((o_ref if in_place else refs[-1])[...])

    in_specs = a_specs + b_specs + ([pl.BlockSpec((tm, tn), lambda i, j, k: (i, j))] if has_add else [])
    args = (*a_list, *b_list) + ((add,) if has_add else ())
    if after is not None:
        in_specs.append(pl.BlockSpec(memory_space=pl.ANY))
        args += (after,)
    return pl.pallas_call(
        body, name=name, grid=(m // tm, n // tn, nk),
        in_specs=in_specs, out_specs=o_spec,
        out_shape=jax.ShapeDtypeStruct(o_shape, out_dtype),
        scratch_shapes=[pltpu.VMEM((tm, tn), F32)] if nk > 1 and not in_place else [],
        compiler_params=_params("parallel", "parallel", "arbitrary"),
    )(*args)


def _shift_down(u, s):
    return jnp.where(_iota(u.shape, 0) >= s, pltpu.roll(u, s, 0), 0.0)


def _shift_up(u, s):
    t = u.shape[0]
    return jnp.where(_iota(u.shape, 0) < t - s, pltpu.roll(u, t - s, 0), 0.0)


SLAB = 8


def _rolled(u):
    return pltpu.roll(u, 2, 0), pltpu.roll(u, 1, 0)


def _conv(u, w, rolled=None):
    u2, u1 = _rolled(u) if rolled is None else rolled
    raw = w[0:1] * u2 + w[1:2] * u1 + w[2:3] * u
    head = u[0:SLAB]
    mended = w[0:1] * _shift_down(head, 2) + w[1:2] * _shift_down(head, 1) + w[2:3] * head
    return jnp.concatenate([mended, raw[SLAB:]], axis=0)


def _conv_t(dy, w):
    t = dy.shape[0]
    raw = w[2:3] * dy + w[1:2] * pltpu.roll(dy, t - 1, 0) + w[0:1] * pltpu.roll(dy, t - 2, 0)
    tail = dy[t - SLAB:]
    mended = w[2:3] * tail + w[1:2] * _shift_up(tail, 1) + w[0:1] * _shift_up(tail, 2)
    return jnp.concatenate([raw[:t - SLAB], mended], axis=0)


def _conv_dw(dy, u, rolled=None):
    t = dy.shape[0]
    u2, u1 = _rolled(u) if rolled is None else rolled
    head, tail = dy[0:SLAB], u[t - SLAB:]
    r = _iota(head.shape, 0)
    wrap2 = jnp.sum(jnp.where(r < 2, head * pltpu.roll(tail, 2, 0), 0.0), axis=0, keepdims=True)
    wrap1 = jnp.sum(jnp.where(r < 1, head * pltpu.roll(tail, 1, 0), 0.0), axis=0, keepdims=True)
    d0 = jnp.sum(dy * u2, axis=0, keepdims=True) - wrap2
    d1 = jnp.sum(dy * u1, axis=0, keepdims=True) - wrap1
    d2 = jnp.sum(dy * u, axis=0, keepdims=True)
    r3 = _iota((3, dy.shape[1]), 0)
    return jnp.where(r3 == 0, d0, jnp.where(r3 == 1, d1, d2))


def _sigmoid(x):
    return 0.5 * jnp.tanh(0.5 * x) + 0.5


def _sconv_fwd(proj, w_sc, t, wc):
    nb = wc // LANE

    def body(cb_ref, cc_ref, ch_ref, w_ref, y_ref):
        u = cc_ref[...] * ch_ref[...]
        y_ref[...] = (cb_ref[...] * _conv(u, w_ref[...])).astype(BF16)

    col = lambda off: pl.BlockSpec((t, LANE), lambda j: (0, j + off))
    return pl.pallas_call(
        body, name="sconv_fwd", grid=(nb,),
        in_specs=[col(0), col(nb), col(2 * nb), pl.BlockSpec((3, LANE), lambda j: (0, j))],
        out_specs=pl.BlockSpec((None, t, LANE), lambda j: (0, 0, j)),
        out_shape=jax.ShapeDtypeStruct((2, t, wc), BF16),
        compiler_params=_params("parallel"),
    )(proj, proj, proj, w_sc)


def _sconv_bwd(dy, proj, w_sc, t, wc, after=None):
    nb = wc // LANE

    def body(dy_ref, cb_ref, cc_ref, ch_ref, w_ref, dcb_ref, dcc_ref, dch_ref, dw_ref):
        cc, ch, w, d = cc_ref[...], ch_ref[...], w_ref[...], dy_ref[...]
        u = cc * ch
        ru = _rolled(u)
        dcb_ref[...] = (d * _conv(u, w, ru)).astype(BF16)
        dcu = d * cb_ref[...]
        dw_ref[...] = _conv_dw(dcu, u, ru)
        du = _conv_t(dcu, w)
        dcc_ref[...] = (du * ch).astype(BF16)
        dch_ref[...] = (du * cc).astype(BF16)

    col = lambda off: pl.BlockSpec((t, LANE), lambda j: (0, j + off))
    act = jax.ShapeDtypeStruct((t, wc), BF16)
    body, in_specs, args = _placed(
        after, body, [col(0), col(0), col(nb), col(2 * nb), pl.BlockSpec((3, LANE), lambda j: (0, j))],
        (dy, proj, proj, proj, w_sc))
    return pl.pallas_call(
        body, name="sconv_bwd", grid=(nb,),
        in_specs=in_specs,
        out_specs=[col(0), col(0), col(0), pl.BlockSpec((3, LANE), lambda j: (0, j))],
        out_shape=[act, act, act, jax.ShapeDtypeStruct((3, wc), F32)],
        compiler_params=_params("parallel"),
    )(*args)


def _gates_prep(proj, bias_tile, t, gate_tile):
    def body(g_ref, b_ref, o_ref):
        g = g_ref[...] + b_ref[...]
        lane = _iota(g.shape, 1)
        is_f = (lane >= NH) & (lane < 2 * NH)
        lf = jnp.minimum(g, 0.0) - jnp.log(1.0 + jnp.exp(-jnp.abs(g)))
        c = jnp.where(is_f, lf, 0.0)
        r = _iota(g.shape, 0) % CHUNK
        s = 1
        while s < CHUNK:
            c = c + jnp.where(r >= s, pltpu.roll(c, s, 0), 0.0)
            s *= 2
        o_ref[...] = jnp.where(is_f, c, jnp.where(lane < NH, g, 0.0))

    return pl.pallas_call(
        body, name="gates_prep", grid=(1,),
        in_specs=[pl.BlockSpec((t, LANE), lambda i: (0, gate_tile)), pl.BlockSpec((1, LANE), lambda i: (0, 0))],
        out_specs=pl.BlockSpec((t, LANE), lambda i: (0, 0)),
        out_shape=jax.ShapeDtypeStruct((t, LANE), F32),
        compiler_params=_params("arbitrary"),
    )(proj, bias_tile)


def _gates_bwd(dgate, proj, bias_tile, d_proj, t, gate_tile):
    def body(dg_ref, g_ref, b_ref, d_proj_in, o_ref, s_ref):
        g = g_ref[...] + b_ref[...]
        lane = _iota(g.shape, 1)
        r = _iota(g.shape, 0) % CHUNK
        dsig = 1.0 - _sigmoid(g)
        out = jnp.zeros(g.shape, F32)
        for h in range(NH):
            d = dg_ref[h]
            c = d
            s = 1
            while s < CHUNK:
                c = c + jnp.where(r + s < CHUNK, pltpu.roll(c, t - s, 0), 0.0)
                s *= 2
            di = jnp.broadcast_to(d[:, 0:1], g.shape)
            db = jnp.broadcast_to(c[:, 1:2], g.shape)
            out = out + jnp.where(lane == h, di, 0.0) + jnp.where(lane == NH + h, db * dsig, 0.0)
        o_ref[...] = out.astype(BF16)
        s_ref[...] = jnp.sum(out, axis=0, keepdims=True)

    return pl.pallas_call(
        body, name="gates_bwd", grid=(1,),
        in_specs=[pl.BlockSpec((NH, t, LANE), lambda i: (0, 0, 0)),
                  pl.BlockSpec((t, LANE), lambda i: (0, gate_tile)), pl.BlockSpec((1, LANE), lambda i: (0, 0)),
                  pl.BlockSpec(memory_space=pl.ANY)],
        out_specs=[pl.BlockSpec((t, LANE), lambda i: (0, gate_tile)), pl.BlockSpec((1, LANE), lambda i: (0, 0))],
        out_shape=[jax.ShapeDtypeStruct(d_proj.shape, d_proj.dtype), jax.ShapeDtypeStruct((1, LANE), F32)],
        input_output_aliases={3: 0},
        compiler_params=_params("arbitrary"),
    )(dgate, proj, bias_tile, d_proj)


def _in_turn(heads):
    while heads:
        heads = [g for g in heads if next(g, heads) is not heads]


def _chunk_gates(gc, gr, h, mprev):
    L = CHUNK
    icol, bcol = gc[:, h:h + 1], gc[:, h + NH:h + NH + 1]
    irow, brow = gr[h:h + 1, :], gr[h + NH:h + NH + 1, :]
    tri = _iota((L, L), 0) >= _iota((L, L), 1)
    log_d = jnp.where(tri, bcol - brow + irow, -jnp.inf)
    inter = bcol + mprev
    mt = jnp.maximum(inter, jnp.max(log_d, axis=1, keepdims=True))
    dw = jnp.exp(log_d - mt)
    iw = jnp.exp(inter - mt)
    g = brow[:, L - 1:L]
    wlog_col = g - bcol + icol
    wlog_row = g - brow + irow
    mnew = jnp.maximum(g + mprev, jnp.max(wlog_row, axis=1, keepdims=True))
    wcol = jnp.exp(wlog_col - mnew)
    decay = jnp.exp(g + mprev - mnew)
    return dw, iw, mt, wcol, decay, mnew


def _mlstm_fwd(proj, gcol, grow, t, wc, dh):
    nc = t // CHUNK
    wm = NH * dh
    assert wc == wm, (wc, wm)
    qoff = 3 * wc // wm
    scale = dh ** -0.5

    def body(q_ref, k_ref, v_ref, gc_ref, gr_ref, h_ref, cs_ref, ns_ref, c_s, n_s, m_s):
        @pl.when(pl.program_id(0) == 0)
        def _():
            c_s[...] = jnp.zeros_like(c_s)
            n_s[...] = jnp.zeros_like(n_s)
            m_s[...] = jnp.zeros_like(m_s)

        gc, gr = gc_ref[...], gr_ref[0]
        done = [None] * NH

        def head(h):
            cols = slice(h * dh, (h + 1) * dh)
            mprev = m_s[h, 0:1, 0:1]
            cprev = c_s[h]
            n8 = n_s[h]
            nprev = n8[0:1]
            qs = q_ref[:, cols] * scale
            k = k_ref[:, cols]
            qs_b, k_b, v_b = qs.astype(BF16), k.astype(BF16), v_ref[:, cols].astype(BF16)
            qk = _dot(qs_b, k_b, _NT)
            yield
            q_c = _dot(qs_b, cprev.astype(BF16))
            yield
            dw, iw, mt, wcol, decay, mnew = _chunk_gates(gc, gr, h, mprev)
            yield
            s = qk * dw
            wk = wcol * k
            num = _dot(s.astype(BF16), v_b) + iw * q_c
            yield
            c_new = decay * cprev + _dot(wk.astype(BF16), v_b, _TN)
            yield
            den = jnp.sum(s, axis=1, keepdims=True) + iw * jnp.sum(qs * nprev, axis=1, keepdims=True)
            done[h] = (cprev, jnp.where(_iota(n8.shape, 0) == 1, mprev, n8),
                       num / jnp.maximum(jnp.abs(den), jnp.exp(-mt)), c_new,
                       decay * n8 + jnp.sum(wk, axis=0, keepdims=True), mnew)

        _in_turn([head(h) for h in range(NH)])
        for h, (c_old, n_old, h_out, c_new, n_new, m_new) in enumerate(done):
            cs_ref[h] = c_old
            ns_ref[h] = n_old
            h_ref[:, h * dh:(h + 1) * dh] = h_out
            c_s[h] = c_new
            n_s[h] = n_new
            m_s[h] = jnp.broadcast_to(m_new, m_s.shape[1:])

    grp = lambda off: pl.BlockSpec((CHUNK, wm), lambda c: (c, qoff + off))
    return pl.pallas_call(
        body, name="mlstm_fwd", grid=(nc,),
        in_specs=[grp(0), grp(1), grp(2),
                  pl.BlockSpec((CHUNK, LANE), lambda c: (c, 0)),
                  pl.BlockSpec((1, 8, CHUNK), lambda c: (c, 0, 0))],
        out_specs=[pl.BlockSpec((CHUNK, wm), lambda c: (c, 0)),
                   pl.BlockSpec((NH, None, dh, dh), lambda c: (0, c, 0, 0)),
                   pl.BlockSpec((NH, None, 8, dh), lambda c: (0, c, 0, 0))],
        out_shape=[jax.ShapeDtypeStruct((t, wm), F32),
                   jax.ShapeDtypeStruct((NH, nc, dh, dh), F32),
                   jax.ShapeDtypeStruct((NH, nc, 8, dh), F32)],
        scratch_shapes=[pltpu.VMEM((NH, dh, dh), F32), pltpu.VMEM((NH, 8, dh), F32), pltpu.VMEM((NH, 8, LANE), F32)],
        compiler_params=_params("arbitrary"),
    )(proj, proj, proj, gcol, grow)


def _mlstm_bwd(proj, gcol, grow, hval, dh_in, cs, ns, d_proj, t, wc, dh):
    nc = t // CHUNK
    wm = NH * dh
    assert wc == wm, (wc, wm)
    qoff = 3 * wc // wm
    scale = dh ** -0.5
    L = CHUNK

    def body(q_ref, k_ref, v_ref, gc_ref, gr_ref, h_ref, dh_ref, cs_ref, ns_ref, d_proj_in,
             dqkv_ref, dg_ref, dc_s, dn_s):
        @pl.when(pl.program_id(0) == 0)
        def _():
            dc_s[...] = jnp.zeros_like(dc_s)
            dn_s[...] = jnp.zeros_like(dn_s)

        gc, gr = gc_ref[...], gr_ref[0]
        eye = _iota((L, L), 0) == _iota((L, L), 1)
        lane = _iota((L, LANE), 1)
        last = _iota((L, 1), 0) == L - 1
        done = [None] * NH

        def head(h):
            cols = slice(h * dh, (h + 1) * dh)
            ns8 = ns_ref[h]
            nprev = ns8[0:1]
            mprev = ns8[1:2, 0:1]
            cprev = cs_ref[h]
            dcn = dc_s[h]
            dn8 = dn_s[h]
            dnn = dn8[0:1]

            qs = q_ref[:, cols] * scale
            k = k_ref[:, cols]
            qs_b, k_b, v_b = qs.astype(BF16), k.astype(BF16), v_ref[:, cols].astype(BF16)
            qk = _dot(qs_b, k_b, _NT)
            yield
            dw, iw, mt, wcol, decay, _ = _chunk_gates(gc, gr, h, mprev)
            yield
            s = qk * dw
            den = jnp.sum(s, axis=1, keepdims=True) + iw * jnp.sum(qs * nprev, axis=1, keepdims=True)
            emt = jnp.exp(-mt)
            r = 1.0 / jnp.maximum(jnp.abs(den), emt)
            dout = dh_ref[:, cols]
            dnum = dout * r
            dden = (-jnp.sum(dout * h_ref[:, cols], axis=1, keepdims=True) * r
                    * jnp.where(jnp.abs(den) > emt, jnp.sign(den), 0.0))
            dnum_b = dnum.astype(BF16)
            cprev_b = cprev.astype(BF16)
            dcn_b = dcn.astype(BF16)
            yield

            g_raw = _dot(dnum_b, v_b, _NT)
            yield
            q_inter = _dot(dnum_b, cprev_b, _NT)
            yield
            k_raw = _dot(v_b, dcn_b, _NT)
            yield
            gd = (g_raw + dden) * dw
            gd_b = gd.astype(BF16)
            dqs_inter = iw * (q_inter + dden * nprev)
            dk_inter = wcol * (k_raw + dnn)
            wk = wcol * k
            iq = iw * qs
            dqs = _dot(gd_b, k_b) + dqs_inter
            yield
            dk = _dot(gd_b, qs_b, _TN) + dk_inter
            yield
            dv = _dot(s.astype(BF16), dnum_b, _TN) + _dot(wk.astype(BF16), dcn_b)
            yield
            dc_new = decay * dcn + _dot(iq.astype(BF16), dnum_b, _TN)
            yield

            e = gd * qk
            e_cols = jnp.sum(jnp.where(eye, jnp.sum(e, axis=0, keepdims=True), 0.0), axis=1, keepdims=True)
            yield
            k_inter = jnp.sum(k * dk_inter, axis=1, keepdims=True)
            rq = jnp.sum(e, axis=1, keepdims=True) + jnp.sum(qs * dqs_inter, axis=1, keepdims=True)
            rk = e_cols + k_inter
            hsum = jnp.sum(k_inter, axis=0, keepdims=True)
            jdec = decay * (jnp.sum(jnp.sum(dcn * cprev, axis=1, keepdims=True), axis=0, keepdims=True)
                            + jnp.sum(dnn * nprev, axis=1, keepdims=True))
            db = rq - rk + jnp.where(last, hsum + jdec, 0.0)
            done[h] = (jnp.where(lane == 0, rk, jnp.where(lane == 1, db, 0.0)),
                       (dqs * scale).astype(BF16), dk.astype(BF16), dv.astype(BF16), dc_new,
                       decay * dn8 + jnp.sum(iq * dden, axis=0, keepdims=True))

        _in_turn([head(h) for h in range(NH)])
        for h, (dgate, dq, dk, dv, dc_new, dn_new) in enumerate(done):
            dg_ref[h] = dgate
            for part, grad in enumerate((dq, dk, dv)):
                dqkv_ref[:, part * wm + h * dh:part * wm + (h + 1) * dh] = grad
            dc_s[h] = dc_new
            dn_s[h] = dn_new

    rc = lambda c: nc - 1 - c
    grp = lambda off: pl.BlockSpec((L, wm), lambda c: (rc(c), qoff + off))
    hm = pl.BlockSpec((L, wm), lambda c: (rc(c), 0))
    assert qoff % 3 == 0, qoff
    return pl.pallas_call(
        body, name="mlstm_bwd", grid=(nc,),
        in_specs=[grp(0), grp(1), grp(2),
                  pl.BlockSpec((L, LANE), lambda c: (rc(c), 0)),
                  pl.BlockSpec((1, 8, L), lambda c: (rc(c), 0, 0)),
                  hm, hm,
                  pl.BlockSpec((NH, None, dh, dh), lambda c: (0, rc(c), 0, 0)),
                  pl.BlockSpec((NH, None, 8, dh), lambda c: (0, rc(c), 0, 0)),
                  pl.BlockSpec(memory_space=pl.ANY)],
        out_specs=[pl.BlockSpec((L, 3 * wm), lambda c: (rc(c), qoff // 3)),
                   pl.BlockSpec((NH, L, LANE), lambda c: (0, rc(c), 0))],
        out_shape=[jax.ShapeDtypeStruct(d_proj.shape, d_proj.dtype), jax.ShapeDtypeStruct((NH, t, LANE), F32)],
        input_output_aliases={9: 0},
        scratch_shapes=[pltpu.VMEM((NH, dh, dh), F32), pltpu.VMEM((NH, 8, dh), F32)],
        compiler_params=_params("arbitrary"),
    )(proj, proj, proj, gcol, grow, hval, dh_in, cs, ns, d_proj)


def _head_norm(hv):
    mu = jnp.mean(hv, axis=1, keepdims=True)
    hc = hv - mu
    rstd = lax.rsqrt(jnp.mean(hc * hc, axis=1, keepdims=True) + HN_EPS)
    return hc * rstd, rstd


def _hnorm_fwd(hval, proj, gain, y, t, wc, dh, tr=512):
    ooff = 3 * wc // dh + 3 * NH
    tr = min(tr, t)

    def body(h_ref, o_ref, g_ref, y_in, y_ref):
        hhat, _ = _head_norm(h_ref[...])
        y_ref[...] = (_sigmoid(o_ref[...]) * hhat * g_ref[...]).astype(BF16)

    return pl.pallas_call(
        body, name="hnorm_fwd", grid=(t // tr, NH),
        in_specs=[pl.BlockSpec((tr, dh), lambda i, h: (i, h)),
                  pl.BlockSpec((tr, dh), lambda i, h: (i, ooff + h)),
                  pl.BlockSpec((1, dh), lambda i, h: (0, h)),
                  pl.BlockSpec(memory_space=pl.ANY)],
        out_specs=pl.BlockSpec((None, tr, dh), lambda i, h: (1, i, h)),
        out_shape=jax.ShapeDtypeStruct(y.shape, BF16),
        input_output_aliases={3: 0},
        compiler_params=_params("parallel", "parallel"),
    )(hval, proj, gain, y)


def _hnorm_bwd(dy, hval, proj, gain, t, wc, dh, tr=512):
    ooff = 3 * wc // dh + 3 * NH
    tr = min(tr, t)
    yoff = wc // dh

    def body(dy_ref, h_ref, o_ref, g_ref, do_ref, dh_ref, dg_ref):
        i = pl.program_id(1)
        hhat, rstd = _head_norm(h_ref[...])
        gain_v = g_ref[...]
        sig = _sigmoid(o_ref[...])
        d = dy_ref[...]
        do_ref[...] = (d * hhat * gain_v * sig * (1.0 - sig)).astype(BF16)
        dhn = d * sig
        part = jnp.sum(dhn * hhat, axis=0, keepdims=True)

        @pl.when(i == 0)
        def _():
            dg_ref[...] = part

        @pl.when(i > 0)
        def _():
            dg_ref[...] += part

        dhat = dhn * gain_v
        dh_ref[...] = rstd * (dhat - jnp.mean(dhat, axis=1, keepdims=True)
                              - hhat * jnp.mean(dhat * hhat, axis=1, keepdims=True))

    blk = lambda off: pl.BlockSpec((tr, dh), lambda h, i: (i, off + h))
    return pl.pallas_call(
        body, name="hnorm_bwd", grid=(NH, t // tr),
        in_specs=[blk(yoff), blk(0), blk(ooff), pl.BlockSpec((1, dh), lambda h, i: (0, h))],
        out_specs=[blk(ooff), blk(0), pl.BlockSpec((1, dh), lambda h, i: (0, h))],
        out_shape=[jax.ShapeDtypeStruct(proj.shape, BF16), jax.ShapeDtypeStruct((t, NH * dh), F32),
                   jax.ShapeDtypeStruct((1, NH * dh), F32)],
        compiler_params=_params("parallel", "arbitrary"),
    )(dy, hval, proj, gain)


def _ln_stats(z):
    mu = jnp.mean(z, axis=1, keepdims=True)
    zc = z - mu
    rstd = lax.rsqrt(jnp.mean(zc * zc, axis=1, keepdims=True) + LN_EPS)
    return zc * rstd, rstd


def _ln_bwd(dy, xhat, rstd, g):
    dxh = dy * g
    return rstd * (dxh - jnp.mean(dxh, axis=1, keepdims=True) - xhat * jnp.mean(dxh * xhat, axis=1, keepdims=True))


def _accum(ref, i, part):
    @pl.when(i == 0)
    def _():
        ref[...] = part

    @pl.when(i > 0)
    def _():
        ref[...] += part


def _ln1_fwd(x, mix, g, b, tr=256, after=None):
    t, d = x.shape

    def body(x_ref, m_ref, g_ref, b_ref, xh_ref, rs_ref, xb_ref):
        xhat, rstd = _ln_stats(ALPHA * x_ref[...] + m_ref[...])
        xh_ref[...] = xhat
        rs_ref[...] = rstd
        xb_ref[...] = (xhat * g_ref[...] + b_ref[...]).astype(BF16)

    row = pl.BlockSpec((tr, d), lambda i: (i, 0))
    vec = pl.BlockSpec((1, d), lambda i: (0, 0))
    body, in_specs, args = _placed(after, body, [row, row, vec, vec], (x, mix, g, b))
    return pl.pallas_call(
        body, name="ln1_fwd", grid=(t // tr,),
        in_specs=in_specs,
        out_specs=[row, pl.BlockSpec((tr, 1), lambda i: (i, 0)), row],
        out_shape=[jax.ShapeDtypeStruct((t, d), F32), jax.ShapeDtypeStruct((t, 1), F32),
                   jax.ShapeDtypeStruct((t, d), BF16)],
        compiler_params=_params("parallel"),
    )(*args)


def _ln2_loss(xhat1, g1, b1, ff, target, g2, b2, tr=256):
    t, d = ff.shape

    def body(xh_ref, g1_ref, b1_ref, f_ref, t_ref, g_ref, b_ref, dz_ref, dzb_ref, dg_ref, db_ref, l_ref):
        i = pl.program_id(0)
        x1 = xh_ref[...] * g1_ref[...] + b1_ref[...]
        xhat, rstd = _ln_stats(ALPHA * x1 + f_ref[...])
        gv = g_ref[...]
        e = xhat * gv + b_ref[...] - t_ref[...]
        lsum = jnp.sum(jnp.sum(e * e, axis=1, keepdims=True), axis=0, keepdims=True) * (0.5 / d)
        dy = e * (1.0 / d)
        _accum(dg_ref, i, jnp.sum(dy * xhat, axis=0, keepdims=True))
        _accum(db_ref, i, jnp.sum(dy, axis=0, keepdims=True))
        _accum(l_ref, i, jnp.broadcast_to(lsum, l_ref.shape))
        dz = _ln_bwd(dy, xhat, rstd, gv)
        dz_ref[...] = dz
        dzb_ref[...] = dz.astype(BF16)

    row = pl.BlockSpec((tr, d), lambda i: (i, 0))
    vec = pl.BlockSpec((1, d), lambda i: (0, 0))
    return pl.pallas_call(
        body, name="ln2_loss", grid=(t // tr,),
        in_specs=[row, vec, vec, row, row, vec, vec],
        out_specs=[row, row, vec, vec, pl.BlockSpec((8, LANE), lambda i: (0, 0))],
        out_shape=[jax.ShapeDtypeStruct((t, d), F32), jax.ShapeDtypeStruct((t, d), BF16),
                   jax.ShapeDtypeStruct((1, d), F32), jax.ShapeDtypeStruct((1, d), F32),
                   jax.ShapeDtypeStruct((8, LANE), F32)],
        compiler_params=_params("arbitrary"),
    )(xhat1, g1, b1, ff, target, g2, b2)


def _ln1_bwd(dz2, dffn, xhat1, rstd1, g1, tr=256, after=None):
    t, d = dz2.shape

    def body(a_ref, f_ref, xh_ref, rs_ref, g_ref, dz_ref, dzb_ref, dg_ref, db_ref):
        i = pl.program_id(0)
        dy = ALPHA * a_ref[...] + f_ref[...]
        xhat = xh_ref[...]
        _accum(dg_ref, i, jnp.sum(dy * xhat, axis=0, keepdims=True))
        _accum(db_ref, i, jnp.sum(dy, axis=0, keepdims=True))
        dz = _ln_bwd(dy, xhat, rs_ref[...], g_ref[...])
        dz_ref[...] = dz
        dzb_ref[...] = dz.astype(BF16)

    row = pl.BlockSpec((tr, d), lambda i: (i, 0))
    vec = pl.BlockSpec((1, d), lambda i: (0, 0))
    body, in_specs, args = _placed(after, body, [row, row, row, pl.BlockSpec((tr, 1), lambda i: (i, 0)), vec],
                                   (dz2, dffn, xhat1, rstd1, g1))
    return pl.pallas_call(
        body, name="ln1_bwd", grid=(t // tr,),
        in_specs=in_specs,
        out_specs=[row, row, vec, vec],
        out_shape=[jax.ShapeDtypeStruct((t, d), F32), jax.ShapeDtypeStruct((t, d), BF16),
                   jax.ShapeDtypeStruct((1, d), F32), jax.ShapeDtypeStruct((1, d), F32)],
        compiler_params=_params("arbitrary"),
    )(*args)


def _ffn_act_fwd(hid0, w_fc, b_fc, t, dff, after=None):
    nb = dff // LANE

    def body(hv_ref, hg_ref, wv_ref, wg_ref, bv_ref, bg_ref, a_ref):
        val = _conv(hv_ref[...], wv_ref[...]) + bv_ref[...]
        gate = _conv(hg_ref[...], wg_ref[...]) + bg_ref[...]
        a_ref[...] = (gate * _sigmoid(gate) * val).astype(BF16)

    col = lambda off: pl.BlockSpec((t, LANE), lambda j: (0, j + off))
    w3 = lambda off: pl.BlockSpec((3, LANE), lambda j: (0, j + off))
    w1 = lambda off: pl.BlockSpec((1, LANE), lambda j: (0, j + off))
    body, in_specs, args = _placed(after, body, [col(0), col(nb), w3(0), w3(nb), w1(0), w1(nb)],
                                   (hid0, hid0, w_fc, w_fc, b_fc, b_fc))
    return pl.pallas_call(
        body, name="ffn_act_fwd", grid=(nb,),
        in_specs=in_specs,
        out_specs=col(0),
        out_shape=jax.ShapeDtypeStruct((t, dff), BF16),
        compiler_params=_params("parallel"),
    )(*args)


def _ffn_act_bwd(da, hid0, w_fc, b_fc, t, dff, after=None):
    nb = dff // LANE

    def body(da_ref, hv_ref, hg_ref, wv_ref, wg_ref, bv_ref, bg_ref,
             dhv_ref, dhg_ref, dwv_ref, dwg_ref, dbv_ref, dbg_ref):
        hv, hg, wv, wg = hv_ref[...], hg_ref[...], wv_ref[...], wg_ref[...]
        rv, rg = _rolled(hv), _rolled(hg)
        val = _conv(hv, wv, rv) + bv_ref[...]
        gate = _conv(hg, wg, rg) + bg_ref[...]
        sig = _sigmoid(gate)
        d = da_ref[...]
        dsig = d * sig
        dval = dsig * gate
        dgate = dsig * val * (1.0 + gate * (1.0 - sig))
        dhv_ref[...] = _conv_t(dval, wv).astype(BF16)
        dhg_ref[...] = _conv_t(dgate, wg).astype(BF16)
        dwv_ref[...] = _conv_dw(dval, hv, rv)
        dwg_ref[...] = _conv_dw(dgate, hg, rg)
        dbv_ref[...] = jnp.sum(dval, axis=0, keepdims=True)
        dbg_ref[...] = jnp.sum(dgate, axis=0, keepdims=True)

    col = lambda off: pl.BlockSpec((t, LANE), lambda j: (0, j + off))
    w3 = lambda off: pl.BlockSpec((3, LANE), lambda j: (0, j + off))
    w1 = lambda off: pl.BlockSpec((1, LANE), lambda j: (0, j + off))
    s3 = jax.ShapeDtypeStruct((3, dff), F32)
    s1 = jax.ShapeDtypeStruct((1, dff), F32)
    body, in_specs, args = _placed(after, body, [col(0), col(0), col(nb), w3(0), w3(nb), w1(0), w1(nb)],
                                   (da, hid0, hid0, w_fc, w_fc, b_fc, b_fc))
    return pl.pallas_call(
        body, name="ffn_act_bwd", grid=(nb,),
        in_specs=in_specs,
        out_specs=[col(0), col(0), w3(0), w3(0), w1(0), w1(0)],
        out_shape=[jax.ShapeDtypeStruct((t, dff), BF16)] * 2 + [s3, s3, s1, s1],
        compiler_params=_params("parallel"),
    )(*args)


class _Ready:
    def __init__(self, **weights):
        self.weights = weights

    def begin(self, after):
        return None

    def forward(self, name, after):
        return None

    def get(self, name, after):
        return self.weights[name]


class _Kept:
    def __init__(self):
        self.grads = {}

    def start(self, name, grad):
        self.grads[name] = grad
        return None

    def relay(self, name, after):
        return None

    def meanwhile(self, small, loss, after):
        return None


def _behind(a, token):
    return a if token is None else a + token[0:1, 0:1].reshape((1,) * a.ndim)


def _local_step(x, target, w_in, b_gates, w_sc, gain, w_out, ln1_g, ln1_b, w_up, w_fc, b_fc, w_down, ln2_g, ln2_b,
                gx=None, wx=None, x_b=None):
    t, d = x.shape
    wc = d // 2
    dh = (d - wc) // NH
    wm = NH * dh
    dff = w_fc.shape[1] // 2
    if wx is None:
        wx = _Ready(w_out=w_out, w_up=w_up, w_down=w_down)
    ninp = 3 * wc + 4 * wm + LANE
    nin = 3 * wc + 4 * wm
    gate_tile = nin // LANE
    nc = t // CHUNK
    bias_tile = jnp.pad(b_gates, ((0, 0), (0, LANE - 2 * NH)))

    if x_b is None:
        x_b = x.astype(BF16)
    proj = _matmul(x_b, w_in, "nt", F32, "proj", tm=512, tn=2432, tk=d, n=ninp, after=wx.begin(w_in))
    y = _sconv_fwd(proj, w_sc, t, wc)
    gcol = _gates_prep(proj, bias_tile, t, gate_tile)
    grow = gcol[:, :8].T.reshape(8, nc, CHUNK).transpose(1, 0, 2)
    hval, cs, ns = _mlstm_fwd(proj, gcol, grow, t, wc, dh)
    y = _hnorm_fwd(hval, proj, gain, y, t, wc, dh)
    tok = wx.forward("w_out", y)
    w_out = wx.get("w_out", tok)
    mix = _matmul(y, w_out, "nn", F32, "out_proj", tm=512, tn=1024, tk=wc, a_blocked=True, after=tok)
    xhat1, rstd1, x1_b = _ln1_fwd(x, mix, ln1_g, ln1_b, after=wx.forward("w_up", mix))
    w_up = wx.get("w_up", x1_b)
    wsl = w_up.shape[2]
    hid0 = _matmul(x1_b, w_up, "nn", F32, "ffn_up", tm=1024, tn=wsl, tk=d, b_blocked=True)
    act = _ffn_act_fwd(hid0, w_fc, b_fc, t, dff, after=wx.forward("w_down", hid0))
    w_down = wx.get("w_down", act)
    ff = _matmul(act, w_down, "nn", F32, "ffn_down", tm=1024, tn=512, tk=dff)
    dz2, dz2_b, d_ln2_g, d_ln2_b, loss = _ln2_loss(xhat1, ln1_g, ln1_b, ff, target, ln2_g, ln2_b)

    if gx is None:
        gx = _Kept()
    d_w_down = _matmul(act, dz2_b, "tn", BF16, "ffn_down_dw", tm=1408, tn=1024, tk=t)
    d_act = _matmul(dz2_b, w_down, "nt", F32, "ffn_down_dx", tm=2048, tn=512, tk=d, after=gx.start("w_down", d_w_down))
    *d_hid0, dwv, dwg, dbv, dbg = _ffn_act_bwd(d_act, hid0, w_fc, b_fc, t, dff, after=gx.relay("w_down", d_act))
    d_w_fc = jnp.concatenate([dwv, dwg], axis=1)
    d_b_fc = jnp.concatenate([dbv, dbg], axis=1)
    d_hid0 = tuple(d_hid0[:2])
    d_w_up = _matmul(x1_b, d_hid0, "tn", BF16, "ffn_up_dw", tm=1024, tn=wsl, tk=t, o_width=wsl)
    d_x1_ffn = _matmul(d_hid0, w_up, "nt", F32, "ffn_up_dx", tm=1024, tn=1024, tk=wsl, b_blocked=True,
                       after=gx.start("w_up", d_w_up))
    dz1, dz1_b, d_ln1_g, d_ln1_b = _ln1_bwd(dz2, d_x1_ffn, xhat1, rstd1, ln1_g, after=gx.relay("w_up", d_x1_ffn))

    d_w_out = _matmul(y, dz1_b, "tn", BF16, "out_proj_dw", tm=1024, tn=1024, tk=t, a_blocked=True)
    dy = _matmul(dz1_b, w_out, "nt", F32, "out_proj_dx", tm=1024, tn=1024, tk=d, after=gx.start("w_out", d_w_out))
    dcb, dcc, dch, d_w_sc = _sconv_bwd(dy, proj, w_sc, t, wc, after=gx.relay("w_out", dy))
    d_proj, d_hval, d_gain = _hnorm_bwd(dy, hval, proj, gain, t, wc, dh)
    d_proj, dgate = _mlstm_bwd(proj, gcol, grow, hval, d_hval, cs, ns, d_proj, t, wc, dh)
    d_proj, d_b_gates = _gates_bwd(dgate, proj, bias_tile, d_proj, t, gate_tile)
    for part, grad in enumerate((dcb, dcc, dch)):
        d_proj = lax.dynamic_update_slice(d_proj, grad, (0, part * wc))
    d_w_in = _matmul(d_proj, x_b, "tn", BF16, "proj_dw", tm=2432, tn=1024, tk=t)
    small = dict(b_gates=d_b_gates[:, :2 * NH], w_sc_conv=d_w_sc, mh_gain=d_gain, ln1_g=d_ln1_g, ln1_b=d_ln1_b,
                 w_ffn_conv=d_w_fc, b_ffn_conv=d_b_fc, ln2_g=d_ln2_g, ln2_b=d_ln2_b)
    token = gx.start("w_in", d_w_in)
    token = gx.relay("w_in", gx.meanwhile(small, loss, token))
    grad_x = _matmul(d_proj, w_in, "nn", F32, "proj_dx", tm=512, tn=512, tk=ninp, add=dz1, add_scale=ALPHA, after=token)
    return loss, grad_x, small, gx


HBM = pl.BlockSpec(memory_space=pltpu.HBM)


def _place():
    return lax.axis_index("x"), lax.axis_index("y"), lax.axis_index("c")


def _index(p):
    return 4 * p[0] + 2 * p[1] + p[2]


def _all_gather(arrs, name):
    n = len(arrs)

    def body(*refs):
        ins, outs = refs[:n], refs[n:2 * n]
        send_sems, recv_sems, local_sems = refs[2 * n:]
        x, y, c = _place()
        me, sibling = (x, y, c), (x, y, 1 - c)
        chips = [(1 - x, y), (x, 1 - y), (1 - x, 1 - y)]

        def copy(a, k, block, to, own=False):
            dst = outs[a].at[_index(block)]
            return pltpu.make_async_remote_copy(
                src_ref=ins[a] if own else dst, dst_ref=dst,
                send_sem=send_sems.at[k * n + a], recv_sem=recv_sems.at[k * n + a],
                device_id=to, device_id_type=MESH)

        mine = [pltpu.make_async_copy(ins[a], outs[a].at[_index(me)], local_sems.at[a]) for a in range(n)]
        for cp in mine:
            cp.start()
        first = []
        for a in range(n):
            first.append(copy(a, 0, me, sibling, own=True))
            first += [copy(a, 1 + j, me, (*chip, c), own=True) for j, chip in enumerate(chips)]
        for cp in first:
            cp.start()
        passed = []
        for j, chip in enumerate(chips):
            for a in range(n):
                copy(a, 1 + j, (*chip, c), me).wait_recv()
                cp = copy(a, 4 + j, (*chip, c), sibling)
                cp.start()
                passed.append(cp)
        for a in range(n):
            copy(a, 0, sibling, me).wait_recv()
            for j, chip in enumerate(chips):
                copy(a, 4 + j, (*chip, 1 - c), me).wait_recv()
        for cp in first + passed:
            cp.wait_send()
        for cp in mine:
            cp.wait()

    return pl.pallas_call(
        body, name=name, in_specs=[HBM] * n, out_specs=[HBM] * n,
        out_shape=[jax.ShapeDtypeStruct((N_DEV,) + a.shape, a.dtype) for a in arrs],
        scratch_shapes=[pltpu.SemaphoreType.DMA((7 * n,)), pltpu.SemaphoreType.DMA((7 * n,)),
                        pltpu.SemaphoreType.DMA((n,))],
    )(*arrs)


SEM = pl.BlockSpec(memory_space=pltpu.SEMAPHORE)
EFFECT = pltpu.SideEffectType.DATAFLOW_SIDE_EFFECTING


def _chips(x, y):
    return [(1 - x, y), (x, 1 - y), (1 - x, 1 - y)]


N_CHIP = N_DEV // 2


def _pair_route(x, y, c):
    return [((x, y, 1 - c), 2 * q + (1 - c), q, q) for q in range(N_CHIP)]


def _chip_route(x, y, c):
    mine = 2 * x + y
    return [((*chip, c), 2 * chip[0] + chip[1], mine, 2 * chip[0] + chip[1]) for chip in _chips(x, y)]


def _exchange_pieces(g_ref, land_ref, width, tail):
    if not tail:
        return [(lambda i: g_ref.at[i], lambda s: land_ref.at[s])]
    rows = lambda i, n: pl.ds(pl.multiple_of(i * width, IN_TAIL), n)
    return [(lambda i: g_ref.at[rows(i, width), :], lambda s: land_ref.at[s, pl.ds(0, width), :]),
            (lambda i: g_ref.at[rows(i + 1, IN_TAIL), :], lambda s: land_ref.at[s, pl.ds(width, IN_TAIL), :])]


def _chip_slot(x, y, c):
    return 2 * x + y


def _exchange_start(grad, route, tail, name, own_slot=None):
    width = IN_SLAB if tail else grad.shape[1]
    n_p = 2 if tail else 1
    n_c = len(route(0, 0, 0))
    land_shape = (N_CHIP, width + (IN_TAIL if tail else 0), grad.shape[-1])
    assert not (tail and own_slot)

    def body(g_ref, land_ref, send_sems, recv_sems, g_thru, land_thru, token):
        for j, (peer, slab, slot, _) in enumerate(route(*_place())):
            for p, (src, dst) in enumerate(_exchange_pieces(g_ref, land_ref, width, tail)):
                pltpu.make_async_remote_copy(src_ref=src(slab), dst_ref=dst(slot), send_sem=send_sems.at[j * n_p + p],
                                             recv_sem=recv_sems.at[j * n_p + p], device_id=peer,
                                             device_id_type=MESH).start()
        if own_slot:
            mine = own_slot(*_place())
            pltpu.make_async_copy(g_ref.at[mine], land_ref.at[mine], send_sems.at[n_c * n_p]).start()
        token[...] = jnp.zeros_like(token)

    return pl.pallas_call(
        body, name=name,
        out_shape=(pltpu.SemaphoreType.DMA((n_c * n_p + bool(own_slot),)), pltpu.SemaphoreType.DMA((n_c * n_p,)),
                   pltpu.HBM(grad.shape, grad.dtype), pltpu.HBM(land_shape, grad.dtype),
                   jax.ShapeDtypeStruct((8, LANE), F32)),
        in_specs=(HBM, HBM), out_specs=(SEM, SEM, HBM, HBM, pl.BlockSpec(memory_space=pltpu.VMEM)),
        input_output_aliases={0: 2, 1: 3},
        compiler_params=pltpu.CompilerParams(has_side_effects=EFFECT),
    )(pltpu.with_memory_space_constraint(grad, pltpu.HBM),
      pltpu.with_memory_space_constraint(lax.empty(land_shape, grad.dtype), pltpu.HBM))


def _exchange_wait(send_sems, recv_sems, g_thru, land_thru, after, route, tail, name, own_slot=None):
    width = IN_SLAB if tail else g_thru.shape[1]
    n_p = 2 if tail else 1

    def body(g_ref, land_ref, send_sems, recv_sems, after_ref, g_dead, got_ref):
        places = route(*_place())
        for j, (peer, slab, _, slot) in enumerate(places):
            for p, (src, dst) in enumerate(_exchange_pieces(g_ref, land_ref, width, tail)):
                cp = pltpu.make_async_remote_copy(src_ref=src(slab), dst_ref=dst(slot),
                                                  send_sem=send_sems.at[j * n_p + p], recv_sem=recv_sems.at[j * n_p + p],
                                                  device_id=peer, device_id_type=MESH)
                cp.wait_send()
                cp.wait_recv()
        if own_slot:
            mine = own_slot(*_place())
            pltpu.make_async_copy(g_ref.at[mine], land_ref.at[mine], send_sems.at[len(places) * n_p]).wait()

    return pl.pallas_call(
        body, name=name,
        out_shape=(pltpu.HBM(g_thru.shape, g_thru.dtype), pltpu.HBM(land_thru.shape, land_thru.dtype)),
        in_specs=(HBM, HBM, SEM, SEM, pl.BlockSpec(memory_space=pl.ANY)), out_specs=(HBM, HBM),
        input_output_aliases={0: 0, 1: 1},
        compiler_params=pltpu.CompilerParams(has_side_effects=EFFECT),
    )(g_thru, land_thru, send_sems, recv_sems, after)


def _pair_add(grad, pair, core, tail, name):
    rows, cols = (IN_SLAB if tail else grad.shape[1]), grad.shape[-1]
    total = pair.shape[1]

    def body(core_ref, *refs):
        if tail:
            g_ref, t_ref, p_ref, o_ref = refs
            o_ref[0:rows, :] = (g_ref[...].astype(F32) + p_ref[0:rows, :].astype(F32)).astype(BF16)
            o_ref[rows:total, :] = (t_ref[...].astype(F32) + p_ref[rows:total, :].astype(F32)).astype(BF16)
        else:
            g_ref, p_ref, o_ref = refs
            o_ref[...] = (g_ref[...].astype(F32) + p_ref[...].astype(F32)).astype(BF16)

    if tail:
        tc = _fit(cols, 512)
        grid = (N_CHIP, cols // tc)
        slab = pl.BlockSpec((None, total, tc), lambda q, i, core_ref: (q, 0, i))
        per = IN_SLAB // IN_TAIL
        in_specs = [pl.BlockSpec((rows, tc), lambda q, i, core_ref: (2 * q + core_ref[0], i)),
                    pl.BlockSpec((IN_TAIL, tc), lambda q, i, core_ref: ((2 * q + core_ref[0] + 1) * per, i))]
    else:
        tr = _rows(rows, 1024)
        grid = (N_CHIP, rows // tr)
        slab = pl.BlockSpec((None, tr, cols), lambda q, i, core_ref: (q, i, 0))
        in_specs = [pl.BlockSpec((None, tr, cols), lambda q, i, core_ref: (2 * q + core_ref[0], i, 0))]
    return pl.pallas_call(
        body, name=name,
        grid_spec=pltpu.PrefetchScalarGridSpec(num_scalar_prefetch=1, grid=grid,
                                               in_specs=in_specs + [slab], out_specs=slab),
        out_shape=jax.ShapeDtypeStruct(pair.shape, BF16),
        compiler_params=_params("parallel", "parallel"),
    )(core, *([grad, grad] if tail else [grad]), pair)


def _relay_places(x, y, c):
    came_from = (c * (1 - x) + (1 - c) * x, c * y + (1 - c) * (1 - y), c)
    pass_to = (c * x + (1 - c) * (1 - x), c * (1 - y) + (1 - c) * y, c)
    return 2 - c, came_from, pass_to, pass_to


OWN = 4


def _gather_start(blocks, after, name, spare=(), relayed=False):
    n = len(blocks)
    lands = [(N_DEV + (a in spare),) + b.shape for a, b in enumerate(blocks)]

    def body(*refs):
        b_refs, land_refs = refs[:n], refs[n:2 * n]
        send_sems, recv_sems = refs[2 * n + 1:3 * n + 1], refs[3 * n + 1:4 * n + 1]
        token = refs[-1]
        x, y, c = _place()
        me = _index((x, y, c))
        for a in range(n):
            targets = [(x, y, 1 - c)] + [(*chip, c) for chip in _chips(x, y)]
            for k, to in enumerate(targets[:3] if relayed else targets):
                pltpu.make_async_remote_copy(src_ref=b_refs[a], dst_ref=land_refs[a].at[me], send_sem=send_sems[a].at[k],
                                             recv_sem=recv_sems[a].at[k], device_id=to, device_id_type=MESH).start()
        for a in range(n):
            pltpu.make_async_copy(b_refs[a], land_refs[a].at[me], send_sems[a].at[OWN]).start()
        token[...] = jnp.zeros_like(token)

    sems = [pltpu.SemaphoreType.DMA((OWN + 1,))] * n
    out = pl.pallas_call(
        body, name=name,
        out_shape=(*sems, *sems, *[pltpu.HBM(b.shape, b.dtype) for b in blocks],
                   *[pltpu.HBM(s, b.dtype) for s, b in zip(lands, blocks)], jax.ShapeDtypeStruct((8, LANE), F32)),
        in_specs=(*[HBM] * (2 * n), pl.BlockSpec(memory_space=pl.ANY)),
        out_specs=(*[SEM] * (2 * n), *[HBM] * (2 * n), pl.BlockSpec(memory_space=pltpu.VMEM)),
        input_output_aliases={i: 2 * n + i for i in range(2 * n)},
        compiler_params=pltpu.CompilerParams(has_side_effects=EFFECT),
    )(*[pltpu.with_memory_space_constraint(b, pltpu.HBM) for b in blocks],
      *[pltpu.with_memory_space_constraint(lax.empty(s, b.dtype), pltpu.HBM) for s, b in zip(lands, blocks)], after)
    return [(out[a], out[n + a], out[2 * n + a], out[3 * n + a]) for a in range(n)], out[-1]


def _gather_relay(states, after, name):
    n, first_out = len(states), 3 * len(states) + len(after)

    def body(*refs):
        land_refs, send_sems, recv_sems = refs[:n], refs[n:2 * n], refs[2 * n:3 * n]
        pass_send, pass_recv = refs[first_out + n:first_out + 2 * n], refs[first_out + 2 * n:first_out + 3 * n]
        k_in, came_from, pass_to, _ = _relay_places(*_place())
        for a in range(n):
            slot = land_refs[a].at[_index(came_from)]
            pltpu.make_async_remote_copy(src_ref=slot, dst_ref=slot, send_sem=send_sems[a].at[k_in],
                                         recv_sem=recv_sems[a].at[k_in], device_id=came_from,
                                         device_id_type=MESH).wait_recv()
            pltpu.make_async_remote_copy(src_ref=slot, dst_ref=slot, send_sem=pass_send[a].at[0],
                                         recv_sem=pass_recv[a].at[0], device_id=pass_to, device_id_type=MESH).start()
        refs[-1][...] = jnp.zeros_like(refs[-1])

    lands = [st[3] for st in states]
    pair = [pltpu.SemaphoreType.DMA((1,))] * n
    out = pl.pallas_call(
        body, name=name,
        out_shape=(*[pltpu.HBM(l.shape, l.dtype) for l in lands], *pair, *pair, jax.ShapeDtypeStruct((8, LANE), F32)),
        in_specs=(*[HBM] * n, *[SEM] * (2 * n), *[pl.BlockSpec(memory_space=pl.ANY)] * len(after)),
        out_specs=(*[HBM] * n, *[SEM] * (2 * n), pl.BlockSpec(memory_space=pltpu.VMEM)),
        input_output_aliases={a: a for a in range(n)},
        compiler_params=pltpu.CompilerParams(has_side_effects=EFFECT),
    )(*lands, *[st[0] for st in states], *[st[1] for st in states], *after)
    return [(st[0], st[1], st[2], out[a], (out[n + a], out[2 * n + a])) for a, st in enumerate(states)], out[-1]


def _gather_forward(send_sems, recv_sems, b_thru, land_thru, after, name, passed=None):
    relayed = passed is not None

    def body(b_ref, land_ref, send_sems, recv_sems, *rest):
        pass_send, pass_recv = rest[:2] if relayed else (None, None)
        send2, recv2, token = rest[-3:]
        x, y, c = _place()
        sibling = (x, y, 1 - c)
        chips = [(*chip, c) for chip in _chips(x, y)]
        arrivals = [((send_sems.at[j + 1], recv_sems.at[j + 1]), frm, j) for j, frm in enumerate(chips)]
        sends = [send_sems.at[k] for k in range(4)]
        if relayed:
            k_in, came_from, _, other = _relay_places(x, y, c)
            arrivals = [(None, came_from, k_in - 1), ((send_sems.at[3 - k_in], recv_sems.at[3 - k_in]), other, 2 - k_in),
                        ((pass_send.at[0], pass_recv.at[0]), chips[2], 2)]
            sends[3] = pass_send.at[0]
        for sems, frm, j in arrivals:
            slot = land_ref.at[_index(frm)]
            if sems:
                pltpu.make_async_remote_copy(src_ref=b_ref, dst_ref=slot, send_sem=sems[0], recv_sem=sems[1],
                                             device_id=frm, device_id_type=MESH).wait_recv()
            pltpu.make_async_remote_copy(src_ref=slot, dst_ref=slot, send_sem=send2.at[j], recv_sem=recv2.at[j],
                                         device_id=sibling, device_id_type=MESH).start()
        pltpu.make_async_remote_copy(src_ref=b_ref, dst_ref=land_ref.at[_index(sibling)], send_sem=send_sems.at[0],
                                     recv_sem=recv_sems.at[0], device_id=sibling, device_id_type=MESH).wait_recv()
        for sem in sends:
            pltpu.make_async_remote_copy(src_ref=b_ref, dst_ref=land_ref.at[0], send_sem=sem, recv_sem=recv_sems.at[0],
                                         device_id=sibling, device_id_type=MESH).wait_send()
        pltpu.make_async_copy(b_ref, land_ref.at[_index((x, y, c))], send_sems.at[OWN]).wait()
        token[...] = jnp.zeros_like(token)

    extra = list(passed) if relayed else []
    return pl.pallas_call(
        body, name=name,
        out_shape=(pltpu.HBM(b_thru.shape, b_thru.dtype), pltpu.HBM(land_thru.shape, land_thru.dtype),
                   pltpu.SemaphoreType.DMA((3,)), pltpu.SemaphoreType.DMA((3,)), jax.ShapeDtypeStruct((8, LANE), F32)),
        in_specs=(HBM, HBM, SEM, SEM, *[SEM] * len(extra), pl.BlockSpec(memory_space=pl.ANY)),
        out_specs=(HBM, HBM, SEM, SEM, pl.BlockSpec(memory_space=pltpu.VMEM)),
        input_output_aliases={0: 0, 1: 1},
        compiler_params=pltpu.CompilerParams(has_side_effects=EFFECT),
    )(b_thru, land_thru, send_sems, recv_sems, *extra, after)


def _gather_finish(land_thru, send2, recv2, after, name):
    def body(land_ref, send2, recv2, after_ref, land_out):
        x, y, c = _place()
        for j, chip in enumerate(_chips(x, y)):
            cp = pltpu.make_async_remote_copy(src_ref=land_ref.at[_index((*chip, c))],
                                              dst_ref=land_ref.at[_index((*chip, 1 - c))], send_sem=send2.at[j],
                                              recv_sem=recv2.at[j], device_id=(x, y, 1 - c), device_id_type=MESH)
            cp.wait_send()
            cp.wait_recv()

    return pl.pallas_call(
        body, name=name, out_shape=pltpu.HBM(land_thru.shape, land_thru.dtype),
        in_specs=(HBM, SEM, SEM, pl.BlockSpec(memory_space=pl.ANY)), out_specs=HBM,
        input_output_aliases={0: 0},
        compiler_params=pltpu.CompilerParams(has_side_effects=EFFECT),
    )(land_thru, send2, recv2, after)


class _Gathering:
    def __init__(self, ahead, later, me):
        cast = [a.astype(BF16) for a in ahead.values()]
        started, self.token = _gather_start(cast, cast[0], "gather1_ahead", relayed=True)
        self.me, self.state, self.relayed, self.later = me, dict(zip(ahead, started)), tuple(ahead), later

    def start_first(self, first, spare):
        started, self.token = _gather_start(list(first.values()), self.token, "gather1_first",
                                            spare=(list(first).index(spare),), relayed=True)
        self.state.update(zip(first, started))
        self.relayed += tuple(first)

    def begin(self, after):
        return self.token

    def relay(self, *after):
        states, token = _gather_relay([self.state[n] for n in self.relayed], after, "gather_relay")
        self.state.update(zip(self.relayed, states))
        cast = [_behind(a, token).astype(BF16) for a in self.later.values()]
        started, self.token = _gather_start(cast, token, "gather1_later")
        self.state.update(zip(self.later, started))
        return self.token

    def forward(self, name, after):
        first_leg, passed = self.state[name][:4], (self.state[name][4:] or (None,))[0]
        *self.state[name], token = _gather_forward(*first_leg, after, "gather2_" + name, passed=passed)
        return token

    def get(self, name, after):
        _, land, send2, recv2 = self.state[name]
        land = _gather_finish(land, send2, recv2, after, "gather3_" + name)
        return land if name not in ("w_out", "w_down") else land.reshape(-1, land.shape[2])


class _Reducing:
    def __init__(self, core, chip, gather_small):
        self.core, self.chip, self.state, self.token, self.gather_small = core, chip, {}, None, gather_small

    def meanwhile(self, small, loss, after):
        self.small_sum = self.gather_small(small, loss, after)
        return self.small_sum

    def start(self, name, grad):
        tail = name == "w_in"
        g = grad if tail or grad.ndim == 3 else grad.reshape(N_DEV, grad.shape[0] // N_DEV, grad.shape[1])
        *self.state[name], token = _exchange_start(g, _pair_route, tail, "pair_send_" + name)
        return token

    def relay(self, name, after):
        tail = name == "w_in"
        grad, pair = _exchange_wait(*self.state[name], after, _pair_route, tail, "pair_recv_" + name)
        total = _pair_add(grad, pair, self.core, tail, "pair_add_" + name)
        *self.state[name], self.token = _exchange_start(total, _chip_route, False, "chip_send_" + name,
                                                        own_slot=_chip_slot)
        return self.token

    def finish(self, name, after):
        _, land = _exchange_wait(*self.state[name], after, _chip_route, False, "chip_recv_" + name, own_slot=_chip_slot)
        return land


def _carry_w_in(main, tail):
    slabs, _, d = main.shape
    tc = _fit(d, 2048)
    assert slabs == N_DEV + 1 and tail.shape[:2] == (N_DEV, IN_TAIL), (main.shape, tail.shape)
    top = lambda off: pl.BlockSpec((None, IN_TAIL, tc), lambda s, j: (s + off, 0, j))

    def carry(m_ref, t_ref, o_ref):
        o_ref[...] = m_ref[...] + t_ref[...]

    main = pl.pallas_call(
        carry, name="carry_w_in", grid=(N_DEV - 1, d // tc), in_specs=[top(1), top(0)], out_specs=top(1),
        out_shape=jax.ShapeDtypeStruct(main.shape, main.dtype), input_output_aliases={0: 0},
        compiler_params=_params("parallel", "parallel"),
    )(main, tail)

    def last(m_ref, t_ref, o_ref):
        o_ref[...] = jnp.zeros_like(o_ref)
        o_ref[0:IN_TAIL, :] = t_ref[...]

    return pl.pallas_call(
        last, name="last_slab_w_in", grid=(d // tc,),
        in_specs=[pl.BlockSpec(memory_space=pl.ANY), pl.BlockSpec((None, IN_TAIL, tc), lambda j: (N_DEV - 1, 0, j))],
        out_specs=pl.BlockSpec((None, LANE, tc), lambda j: (N_DEV, 0, j)),
        out_shape=jax.ShapeDtypeStruct(main.shape, main.dtype), input_output_aliases={0: 0},
        compiler_params=_params("parallel"),
    )(main, tail)


def _rows(n, want):
    t = min(n, want)
    t -= t % 16
    while n % t:
        t -= 16
    return t


def _adam_math(w, g, m, v):
    m2 = ADAM_B1 * m + (1.0 - ADAM_B1) * g
    v2 = ADAM_B2 * v + (1.0 - ADAM_B2) * (g * g)
    m_hat = m2 * (1.0 / (1.0 - ADAM_B1 ** ADAM_STEP))
    v_hat = v2 * (1.0 / (1.0 - ADAM_B2 ** ADAM_STEP))
    return -ADAM_LR * (m_hat / (jnp.sqrt(v_hat) + ADAM_EPS) + ADAM_WD * w), m2, v2


def _slot_sum(r_ref):
    acc = r_ref[0].astype(F32)
    for i in range(1, r_ref.shape[0]):
        acc = acc + r_ref[i].astype(F32)
    return acc


def _shift_w_in(w):
    ws, d = w.shape
    tc = _fit(d, 256)

    def body(w_ref, main_ref, tail_ref, tall):
        tall[...] = jnp.zeros_like(tall)
        tall[0:ws, :] = w_ref[...]
        moved = pltpu.roll(tall[...], _index(_place()), 0).astype(BF16)
        main_ref[...] = moved[0:IN_SLAB]
        tail_ref[...] = moved[IN_SLAB:]

    return pl.pallas_call(
        body, name="shift_w_in", grid=(d // tc,),
        in_specs=[pl.BlockSpec((ws, tc), lambda j: (0, j))],
        out_specs=[pl.BlockSpec((IN_SLAB, tc), lambda j: (0, j)), pl.BlockSpec((IN_TAIL, tc), lambda j: (0, j))],
        out_shape=[jax.ShapeDtypeStruct((IN_SLAB, d), BF16), jax.ShapeDtypeStruct((IN_TAIL, d), BF16)],
        scratch_shapes=[pltpu.VMEM((IN_SLAB + IN_TAIL, tc), F32)], compiler_params=_params("parallel"),
    )(w)


def _sum_adamw_shifted(r, w, m, v, name):
    _, ph, d = r.shape
    ws = w.shape[0]
    tc = _fit(d, 256)

    def body(r_ref, w_ref, m_ref, v_ref, g_ref, d_ref, m2_ref, v2_ref, tall):
        tall[...] = pltpu.roll(_slot_sum(r_ref), lax.rem(ph - _index(_place()), ph), 0)
        g = tall[0:ws, :]
        g_ref[...] = g
        d_ref[...], m2_ref[...], v2_ref[...] = _adam_math(w_ref[...], g, m_ref[...], v_ref[...])

    blk = pl.BlockSpec((ws, tc), lambda j: (0, j))
    out = jax.ShapeDtypeStruct(w.shape, F32)
    return pl.pallas_call(
        body, name=name, grid=(d // tc,),
        in_specs=[pl.BlockSpec((r.shape[0], ph, tc), lambda j: (0, 0, j)), blk, blk, blk],
        out_specs=[blk] * 4, out_shape=[out] * 4,
        scratch_shapes=[pltpu.VMEM((ph, tc), F32)], compiler_params=_params("parallel"),
    )(r, w, m, v)


def _sum_slots(r, name, tr=128):
    _, rows, cols = r.shape
    tr = _rows(rows, tr)

    def body(r_ref, g_ref):
        g_ref[...] = _slot_sum(r_ref)

    return pl.pallas_call(
        body, name=name, grid=(rows // tr,),
        in_specs=[pl.BlockSpec((r.shape[0], tr, cols), lambda i: (0, i, 0))],
        out_specs=pl.BlockSpec((tr, cols), lambda i: (i, 0)),
        out_shape=jax.ShapeDtypeStruct((rows, cols), F32),
        compiler_params=_params("parallel"),
    )(r)


def _adamw(w, g, m, v, name, tr=256):
    rows, cols = w.shape
    tr = _rows(rows, tr)

    def body(w_ref, g_ref, m_ref, v_ref, d_ref, m2_ref, v2_ref):
        d_ref[...], m2_ref[...], v2_ref[...] = _adam_math(w_ref[...], g_ref[...], m_ref[...], v_ref[...])

    blk = pl.BlockSpec((tr, cols), lambda i: (i, 0))
    out = jax.ShapeDtypeStruct((rows, cols), F32)
    return pl.pallas_call(
        body, name=name, grid=(rows // tr,), in_specs=[blk] * 4, out_specs=[blk] * 3, out_shape=[out] * 3,
        compiler_params=_params("parallel"),
    )(w, g, m, v)


def _sum_adamw(r, w, m, v, name, tr=256):
    rows, cols = w.shape
    tr = _rows(rows, tr)

    def body(r_ref, w_ref, m_ref, v_ref, g_ref, d_ref, m2_ref, v2_ref):
        g = _slot_sum(r_ref)
        g_ref[...] = g
        d_ref[...], m2_ref[...], v2_ref[...] = _adam_math(w_ref[...], g, m_ref[...], v_ref[...])

    blk = pl.BlockSpec((tr, cols), lambda i: (i, 0))
    out = jax.ShapeDtypeStruct((rows, cols), F32)
    return pl.pallas_call(
        body, name=name, grid=(rows // tr,),
        in_specs=[pl.BlockSpec((r.shape[0], tr, cols), lambda i: (0, i, 0)), blk, blk, blk],
        out_specs=[blk] * 4, out_shape=[out] * 4,
        compiler_params=_params("parallel"),
    )(r, w, m, v)


def _pack(pieces, sizes):
    flat = [jnp.pad(p.reshape(-1).astype(F32), (0, s - p.size)) for p, s in zip(pieces, sizes)]
    total = sum(sizes)
    padded = -(-total // (16 * LANE)) * (16 * LANE)
    return jnp.pad(jnp.concatenate(flat), (0, padded - total)).reshape(-1, LANE)


def _unpack(packed, shapes, sizes):
    flat = packed.reshape(-1)
    out, off = [], 0
    for shp, s in zip(shapes, sizes):
        n = 1
        for k in shp:
            n *= k
        out.append(flat[off:off + n].reshape(shp))
        off += s
    return out


def _lanes(n):
    return -(-n // LANE) * LANE


WEIGHTS = ("w_in", "b_gates", "w_sc_conv", "mh_gain", "w_out", "ln1_g", "ln1_b", "w_up", "w_ffn_conv", "b_ffn_conv",
           "w_down", "ln2_g", "ln2_b")
BIG = ("w_in", "w_out", "w_up", "w_down")
SMALL = tuple(n for n in WEIGHTS if n not in BIG)


def kernel(x, w_in, b_gates, w_sc_conv, mh_gain, w_out, ln1_g, ln1_b, w_up, w_ffn_conv, b_ffn_conv, w_down, ln2_g, ln2_b, loss_target, m_w_in, m_b_gates, m_w_sc_conv, m_mh_gain, m_w_out, m_ln1_g, m_ln1_b, m_w_up, m_w_ffn_conv, m_b_ffn_conv, m_w_down, m_ln2_g, m_ln2_b, v_w_in, v_b_gates, v_w_sc_conv, v_mh_gain, v_w_out, v_ln1_g, v_ln1_b, v_w_up, v_w_ffn_conv, v_b_ffn_conv, v_w_down, v_ln2_g, v_ln2_b):
    w = dict(zip(WEIGHTS, (w_in, b_gates, w_sc_conv, mh_gain, w_out, ln1_g, ln1_b, w_up, w_ffn_conv, b_ffn_conv,
                           w_down, ln2_g, ln2_b)))
    m = dict(zip(WEIGHTS, (m_w_in, m_b_gates, m_w_sc_conv, m_mh_gain, m_w_out, m_ln1_g, m_ln1_b, m_w_up,
                           m_w_ffn_conv, m_b_ffn_conv, m_w_down, m_ln2_g, m_ln2_b)))
    v = dict(zip(WEIGHTS, (v_w_in, v_b_gates, v_w_sc_conv, v_mh_gain, v_w_out, v_ln1_g, v_ln1_b, v_w_up,
                           v_w_ffn_conv, v_b_ffn_conv, v_w_down, v_ln2_g, v_ln2_b)))
    me = _index(_place())
    d = x.shape[2]
    ws_in = w_in.shape[2]
    assert ws_in == IN_SLAB + 1 and N_DEV <= LANE, w_in.shape
    ninp = (N_DEV + 1) * IN_SLAB
    ws_sc, ws_fc = w_sc_conv.shape[2], w_ffn_conv.shape[2]

    wx = _Gathering({"w_out": w_out[0]}, {n: w[n][0] for n in ("w_up", "w_down")}, me)
    w_in_t = jnp.transpose(_behind(w_in[0], wx.token))
    w_in_main, w_in_tail = _shift_w_in(w_in_t)
    taps8 = lambda a: jnp.pad(a[0], ((0, 5), (0, 0)))
    at_once = ("w_sc", "w_fc", "w_tail", "w_in")
    wx.start_first(dict(zip(at_once, (taps8(w_sc_conv), taps8(w_ffn_conv), w_in_tail, w_in_main))), spare="w_in")
    x_b = _behind(x[0], wx.begin(None)).astype(BF16)
    m_in_t, v_in_t = (jnp.transpose(_behind(a[0], wx.begin(None))) for a in (m_w_in, v_w_in))
    token = wx.relay(x_b, m_in_t, v_in_t)
    for n in at_once:
        token = wx.forward(n, token)
    g_sc, g_fc, g_tail, g_in = (wx.get(n, token) for n in at_once)
    w_in_full = _carry_w_in(g_in, g_tail).reshape(ninp, d)
    w_sc_full = g_sc[:, :3].transpose(1, 0, 2).reshape(3, N_DEV * ws_sc)
    w_fc_full = g_fc[:, :3].transpose(1, 0, 2).reshape(3, N_DEV * ws_fc)

    xi, yi, ci = _place()
    names = ("loss",) + SMALL
    pieces = {}

    def gather_small(small, loss_t, after):
        pieces.update(small, loss=loss_t[0, :1])
        sizes = [_lanes(pieces[n].size) for n in names]
        (g_small,) = _all_gather([_behind(_pack([pieces[n] for n in names], sizes), after)], "gather_small")
        return _sum_slots(g_small, "sum_small", tr=g_small.shape[1])

    gx = _Reducing(jnp.reshape(ci, (1,)).astype(jnp.int32), 2 * xi + yi, gather_small)
    loss_t, grad_x, small, _ = _local_step(
        x[0], loss_target[0], w_in_full, b_gates, w_sc_full, mh_gain, None, ln1_g, ln1_b, None,
        w_fc_full, b_ffn_conv, None, ln2_g, ln2_b, gx=gx, wx=wx, x_b=x_b)

    grads, deltas, new_m, new_v = {}, {}, {}, {}
    for name in ("w_down", "w_up", "w_out"):
        grads[name], deltas[name], new_m[name], new_v[name] = _sum_adamw(
            gx.finish(name, gx.token), w[name][0], m[name][0], v[name][0], "adamw_" + name)

    summed = _unpack(gx.small_sum, [pieces[n].shape for n in names], [_lanes(pieces[n].size) for n in names])
    full = dict(zip(names, summed))
    full["w_sc_conv"] = lax.dynamic_slice(full["w_sc_conv"], (0, me * ws_sc), (3, ws_sc))
    full["w_ffn_conv"] = lax.dynamic_slice(full["w_ffn_conv"], (0, me * ws_fc), (3, ws_fc))
    for n in SMALL:
        grads[n] = full[n].reshape(w[n].shape)
    sizes = [_lanes(w[n].size) for n in SMALL]
    shapes = [w[n].shape for n in SMALL]
    packed = [_pack([t[n] for n in SMALL], sizes) for t in (w, grads, m, v)]
    small_out = _adamw(*packed, "adamw_small")
    for res, t in zip(small_out, (deltas, new_m, new_v)):
        t.update(zip(SMALL, _unpack(res, shapes, sizes)))

    done = sum(t[0:1, 0:1] for t in (grad_x, deltas["w_down"], deltas["w_up"], deltas["w_out"], small_out[0]))
    grads["w_in"], deltas["w_in"], new_m["w_in"], new_v["w_in"] = (
        jnp.transpose(a)[None] for a in _sum_adamw_shifted(gx.finish("w_in", done), w_in_t, m_in_t, v_in_t, "adamw_w_in"))

    big = lambda t: {n: (t[n].reshape(w[n].shape) if n in BIG else t[n]) for n in WEIGHTS}
    grads, deltas, new_m, new_v = big(grads), big(deltas), big(new_m), big(new_v)
    return (full["loss"].reshape(()), grad_x[None], *[grads[n] for n in WEIGHTS], *[deltas[n] for n in WEIGHTS],
            *[new_m[n] for n in WEIGHTS], *[new_v[n] for n in WEIGHTS])
```

```python
import functools

import jax
import jax.numpy as jnp
from jax import lax
from jax.experimental import pallas as pl
from jax.experimental.pallas import tpu as pltpu

F32 = jnp.float32
BF16 = jnp.bfloat16
MESH = pl.DeviceIdType.MESH

N_DEV = 8
NH = 4
CHUNK = 64
LN_EPS = 1e-5
HN_EPS = 1e-6
ALPHA = 2.0 ** 0.25
LANE = 128
IN_SLAB = 7 * LANE
IN_TAIL = 16
VMEM_LIMIT = 56 * 1024 * 1024
ADAM_LR, ADAM_B1, ADAM_B2, ADAM_EPS, ADAM_WD, ADAM_STEP = 0.001, 0.9, 0.999, 1e-08, 0.01, 10

_NN = (((1,), (0,)), ((), ()))
_NT = (((1,), (1,)), ((), ()))
_TN = (((0,), (0,)), ((), ()))


def _dot(a, b, dn=_NN):
    return lax.dot_general(a, b, dn, preferred_element_type=F32)


def _params(*sem):
    return pltpu.CompilerParams(dimension_semantics=sem if sem else None, vmem_limit_bytes=VMEM_LIMIT)


def _iota(shape, axis):
    return lax.broadcasted_iota(jnp.int32, shape, axis)


def _fit(n, want):
    if n <= want:
        return n
    t = want - want % LANE
    while n % t:
        t -= LANE
    return t


def _placed(after, body, in_specs, args):
    if after is None:
        return body, in_specs, args
    return (lambda after_ref, *refs: body(*refs)), [pl.BlockSpec(memory_space=pl.ANY)] + in_specs, (after,) + args


def _matmul(a, b, mode, out_dtype, name, tm=1024, tn=512, tk=1024, add=None, add_scale=1.0,
            a_blocked=False, b_blocked=False, o_width=None, after=None, n=None):
    a_parts = a if isinstance(a, tuple) else None
    b_parts = b if isinstance(b, tuple) else None
    if a_parts:
        a_blocked, (a_rows, wa), na = True, a[0].shape, len(a)
        kd, m = (a_rows, na * wa) if mode == "tn" else (na * wa, a_rows)
    elif a_blocked:
        na, a_rows, wa = a.shape
        kd, m = (a_rows, na * wa) if mode == "tn" else (na * wa, a_rows)
    elif mode == "tn":
        kd, m = a.shape
    else:
        m, kd = a.shape
    if b_parts:
        b_blocked, (rows, w), nb = True, b[0].shape, len(b)
    elif b_blocked:
        nb, rows, w = b.shape
    if b_blocked:
        n = rows if mode == "nt" else nb * w
        assert (nb * w if mode == "nt" else rows) == kd, (name, kd)
    else:
        n = n or (b.shape[0] if mode == "nt" else b.shape[1])
    tm, tn, tk = _fit(m, tm), _fit(n, tn), _fit(kd, tk)
    if a_blocked and mode == "tn":
        tm = _fit(wa, tm)
    if a_blocked and mode != "tn":
        tk = _fit(wa, tk)
    if b_blocked and mode != "nt":
        tn = _fit(w, tn)
    if b_blocked and mode == "nt":
        tk = _fit(w, tk)
    if o_width is not None:
        tn = _fit(o_width, tn)
    assert m % tm == 0 and n % tn == 0 and kd % tk == 0, (name, m, n, kd, tm, tn, tk)
    assert not (a_blocked and mode != "tn" and wa % tk) and not (b_blocked and mode == "nt" and w % tk), (name, tk)
    nk = kd // tk
    dn = {"nn": _NN, "nt": _NT, "tn": _TN}[mode]
    if a_blocked and mode == "tn":
        a_per = wa // tm
        a_spec = pl.BlockSpec((None, tk, tm), lambda i, j, k: (i // a_per, k, i % a_per))
    elif a_blocked:
        a_per = wa // tk
        a_spec = pl.BlockSpec((None, tm, tk), lambda i, j, k: (k // a_per, i, k % a_per))
    elif mode == "tn":
        a_spec = pl.BlockSpec((tk, tm), lambda i, j, k: (k, i))
    else:
        a_spec = pl.BlockSpec((tm, tk), lambda i, j, k: (i, k))
    if b_blocked and mode != "nt":
        per = w // tn
        b_spec = pl.BlockSpec((None, tk, tn), lambda i, j, k: (j // per, k, j % per))
    elif b_blocked:
        per = w // tk
        b_spec = pl.BlockSpec((None, tn, tk), lambda i, j, k: (k // per, j, k % per))
    elif mode == "nt":
        b_spec = pl.BlockSpec((tn, tk), lambda i, j, k: (j, k))
    else:
        b_spec = pl.BlockSpec((tk, tn), lambda i, j, k: (k, j))
    if o_width is None:
        o_spec = pl.BlockSpec((tm, tn), lambda i, j, k: (i, j))
        o_shape = (m, n)
    else:
        oper = o_width // tn
        o_spec = pl.BlockSpec((None, tm, tn), lambda i, j, k: (j // oper, i, j % oper))
        o_shape = (n // o_width, m, o_width)
    a_list, a_specs = [a], [a_spec]
    if a_parts:
        hold = lambda x, s: jnp.clip(x - s * a_per, 0, a_per - 1)
        a_list = list(a_parts)
        a_specs = [(pl.BlockSpec((tk, tm), lambda i, j, k, s=s: (k, hold(i, s))) if mode == "tn"
                    else pl.BlockSpec((tm, tk), lambda i, j, k, s=s: (i, hold(k, s)))) for s in range(na)]
    b_list, b_specs = [b], [b_spec]
    if b_parts:
        hold_b = lambda x, s: jnp.clip(x - s * per, 0, per - 1)
        b_list = list(b_parts)
        b_specs = [(pl.BlockSpec((tn, tk), lambda i, j, k, s=s: (j, hold_b(k, s))) if mode == "nt"
                    else pl.BlockSpec((tk, tn), lambda i, j, k, s=s: (k, hold_b(j, s)))) for s in range(nb)]
    n_a, n_b = len(a_list), len(b_list)
    has_add = add is not None
    n_in = n_a + n_b + has_add + (after is not None)
    in_place = nk > 1 and out_dtype == F32

    def body(*refs):
        add_ref = refs[n_a + n_b] if has_add else None
        o_ref = refs[n_in]
        i, j, k = pl.program_id(0), pl.program_id(1), pl.program_id(2)

        def finish(r):
            if has_add:
                r = r + add_scale * add_ref[...]
            o_ref[...] = r.astype(out_dtype)

        def step(a_ref, b_ref):
            if nk == 1:
                finish(_dot(a_ref[...], b_ref[...], dn))
                return
            acc = o_ref if in_place else refs[-1]

            @pl.when(k == 0)
            def _():
                acc[...] = _dot(a_ref[...], b_ref[...], dn)

            @pl.when(k > 0)
            def _():
                acc[...] += _dot(a_ref[...], b_ref[...], dn)

        if n_a == 1 and n_b == 1:
            step(refs[0], refs[1])
        else:
            slab_a = ((i if mode == "tn" else k) // a_per) if n_a > 1 else 0
            slab_b = ((k if mode == "nt" else j) // per) if n_b > 1 else 0
            for sa in range(n_a):
                for sb in range(n_b):
                    pl.when((slab_a == sa) & (slab_b == sb))(functools.partial(step, refs[sa], refs[n_a + sb]))
        if nk > 1 and not (in_place and not has_add):
            @pl.when(k == nk - 1)
            def _():
                finish((o_ref if in_place else refs[-1])[...])

    in_specs = a_specs + b_specs + ([pl.BlockSpec((tm, tn), lambda i, j, k: (i, j))] if has_add else [])
    args = (*a_list, *b_list) + ((add,) if has_add else ())
    if after is not None:
        in_specs.append(pl.BlockSpec(memory_space=pl.ANY))
        args += (after,)
    return pl.pallas_call(
        body, name=name, grid=(m // tm, n // tn, nk),
        in_specs=in_specs, out_specs=o_spec,
        out_shape=jax.ShapeDtypeStruct(o_shape, out_dtype),
        scratch_shapes=[pltpu.VMEM((tm, tn), F32)] if nk > 1 and not in_place else [],
        compiler_params=_params("parallel", "parallel", "arbitrary"),
    )(*args)


def _shift_down(u, s):
    return jnp.where(_iota(u.shape, 0) >= s, pltpu.roll(u, s, 0), 0.0)


def _shift_up(u, s):
    t = u.shape[0]
    return jnp.where(_iota(u.shape, 0) < t - s, pltpu.roll(u, t - s, 0), 0.0)


SLAB = 8


def _rolled(u):
    return pltpu.roll(u, 2, 0), pltpu.roll(u, 1, 0)


def _conv(u, w, rolled=None):
    u2, u1 = _rolled(u) if rolled is None else rolled
    raw = w[0:1] * u2 + w[1:2] * u1 + w[2:3] * u
    head = u[0:SLAB]
    mended = w[0:1] * _shift_down(head, 2) + w[1:2] * _shift_down(head, 1) + w[2:3] * head
    return jnp.concatenate([mended, raw[SLAB:]], axis=0)


def _conv_t(dy, w):
    t = dy.shape[0]
    raw = w[2:3] * dy + w[1:2] * pltpu.roll(dy, t - 1, 0) + w[0:1] * pltpu.roll(dy, t - 2, 0)
    tail = dy[t - SLAB:]
    mended = w[2:3] * tail + w[1:2] * _shift_up(tail, 1) + w[0:1] * _shift_up(tail, 2)
    return jnp.concatenate([raw[:t - SLAB], mended], axis=0)


def _conv_dw(dy, u, rolled=None):
    t = dy.shape[0]
    u2, u1 = _rolled(u) if rolled is None else rolled
    head, tail = dy[0:SLAB], u[t - SLAB:]
    r = _iota(head.shape, 0)
    wrap2 = jnp.sum(jnp.where(r < 2, head * pltpu.roll(tail, 2, 0), 0.0), axis=0, keepdims=True)
    wrap1 = jnp.sum(jnp.where(r < 1, head * pltpu.roll(tail, 1, 0), 0.0), axis=0, keepdims=True)
    d0 = jnp.sum(dy * u2, axis=0, keepdims=True) - wrap2
    d1 = jnp.sum(dy * u1, axis=0, keepdims=True) - wrap1
    d2 = jnp.sum(dy * u, axis=0, keepdims=True)
    r3 = _iota((3, dy.shape[1]), 0)
    return jnp.where(r3 == 0, d0, jnp.where(r3 == 1, d1, d2))


def _sigmoid(x):
    return 0.5 * jnp.tanh(0.5 * x) + 0.5


def _sconv_fwd(proj, w_sc, t, wc):
    nb = wc // LANE

    def body(cb_ref, cc_ref, ch_ref, w_ref, y_ref):
        u = cc_ref[...] * ch_ref[...]
        y_ref[...] = (cb_ref[...] * _conv(u, w_ref[...])).astype(BF16)

    col = lambda off: pl.BlockSpec((t, LANE), lambda j: (0, j + off))
    return pl.pallas_call(
        body, name="sconv_fwd", grid=(nb,),
        in_specs=[col(0), col(nb), col(2 * nb), pl.BlockSpec((3, LANE), lambda j: (0, j))],
        out_specs=pl.BlockSpec((None, t, LANE), lambda j: (0, 0, j)),
        out_shape=jax.ShapeDtypeStruct((2, t, wc), BF16),
        compiler_params=_params("parallel"),
    )(proj, proj, proj, w_sc)


def _sconv_bwd(dy, proj, w_sc, d_proj, t, wc, after=None):
    nb = wc // LANE
    assert nb >= 2, nb

    def body(dy_ref, cb_ref, cc_ref, ch_ref, w_ref, d_proj_in, d_proj_ref, dw_ref, out_s, sems):
        j = pl.program_id(0)
        slot = j % 2

        def copies(step, slot):
            cols = lambda part: pl.ds(pl.multiple_of((step + part * nb) * LANE, LANE), LANE)
            return [pltpu.make_async_copy(out_s.at[slot, part], d_proj_ref.at[:, cols(part)], sems.at[slot, part])
                    for part in range(3)]

        @pl.when(j >= 2)
        def _():
            for cp in copies(j - 2, slot):
                cp.wait()

        cc, ch, w, d = cc_ref[...], ch_ref[...], w_ref[...], dy_ref[...]
        u = cc * ch
        ru = _rolled(u)
        out_s[slot, 0] = (d * _conv(u, w, ru)).astype(BF16)
        dcu = d * cb_ref[...]
        dw_ref[...] = _conv_dw(dcu, u, ru)
        du = _conv_t(dcu, w)
        out_s[slot, 1] = (du * ch).astype(BF16)
        out_s[slot, 2] = (du * cc).astype(BF16)
        for cp in copies(j, slot):
            cp.start()

        @pl.when(j == nb - 1)
        def _():
            for cp in copies(j - 1, 1 - slot) + copies(j, slot):
                cp.wait()

    col = lambda off: pl.BlockSpec((t, LANE), lambda j: (0, j + off))
    body, in_specs, args = _placed(
        after, body, [col(0), col(0), col(nb), col(2 * nb), pl.BlockSpec((3, LANE), lambda j: (0, j)),
                      pl.BlockSpec(memory_space=pl.ANY)],
        (dy, proj, proj, proj, w_sc, d_proj))
    return pl.pallas_call(
        body, name="sconv_bwd", grid=(nb,),
        in_specs=in_specs,
        out_specs=[pl.BlockSpec(memory_space=pl.ANY), pl.BlockSpec((3, LANE), lambda j: (0, j))],
        out_shape=[jax.ShapeDtypeStruct(d_proj.shape, d_proj.dtype), jax.ShapeDtypeStruct((3, wc), F32)],
        input_output_aliases={len(args) - 1: 0},
        scratch_shapes=[pltpu.VMEM((2, 3, t, LANE), BF16), pltpu.SemaphoreType.DMA((2, 3))],
        compiler_params=_params("arbitrary"),
    )(*args)


def _gates_prep(proj, bias_tile, t, gate_tile):
    def body(g_ref, b_ref, o_ref):
        g = g_ref[...] + b_ref[...]
        lane = _iota(g.shape, 1)
        is_f = (lane >= NH) & (lane < 2 * NH)
        lf = jnp.minimum(g, 0.0) - jnp.log(1.0 + jnp.exp(-jnp.abs(g)))
        c = jnp.where(is_f, lf, 0.0)
        r = _iota(g.shape, 0) % CHUNK
        s = 1
        while s < CHUNK:
            c = c + jnp.where(r >= s, pltpu.roll(c, s, 0), 0.0)
            s *= 2
        o_ref[...] = jnp.where(is_f, c, jnp.where(lane < NH, g, 0.0))

    return pl.pallas_call(
        body, name="gates_prep", grid=(1,),
        in_specs=[pl.BlockSpec((t, LANE), lambda i: (0, gate_tile)), pl.BlockSpec((1, LANE), lambda i: (0, 0))],
        out_specs=pl.BlockSpec((t, LANE), lambda i: (0, 0)),
        out_shape=jax.ShapeDtypeStruct((t, LANE), F32),
        compiler_params=_params("arbitrary"),
    )(proj, bias_tile)


def _gates_bwd(dgate, proj, bias_tile, d_proj, t, gate_tile):
    def body(dg_ref, g_ref, b_ref, d_proj_in, o_ref, s_ref):
        g = g_ref[...] + b_ref[...]
        lane = _iota(g.shape, 1)
        r = _iota(g.shape, 0) % CHUNK
        dsig = 1.0 - _sigmoid(g)
        out = jnp.zeros(g.shape, F32)
        for h in range(NH):
            d = dg_ref[h]
            c = d
            s = 1
            while s < CHUNK:
                c = c + jnp.where(r + s < CHUNK, pltpu.roll(c, t - s, 0), 0.0)
                s *= 2
            di = jnp.broadcast_to(d[:, 0:1], g.shape)
            db = jnp.broadcast_to(c[:, 1:2], g.shape)
            out = out + jnp.where(lane == h, di, 0.0) + jnp.where(lane == NH + h, db * dsig, 0.0)
        o_ref[...] = out.astype(BF16)
        s_ref[...] = jnp.sum(out, axis=0, keepdims=True)

    return pl.pallas_call(
        body, name="gates_bwd", grid=(1,),
        in_specs=[pl.BlockSpec((NH, t, LANE), lambda i: (0, 0, 0)),
                  pl.BlockSpec((t, LANE), lambda i: (0, gate_tile)), pl.BlockSpec((1, LANE), lambda i: (0, 0)),
                  pl.BlockSpec(memory_space=pl.ANY)],
        out_specs=[pl.BlockSpec((t, LANE), lambda i: (0, gate_tile)), pl.BlockSpec((1, LANE), lambda i: (0, 0))],
        out_shape=[jax.ShapeDtypeStruct(d_proj.shape, d_proj.dtype), jax.ShapeDtypeStruct((1, LANE), F32)],
        input_output_aliases={3: 0},
        compiler_params=_params("arbitrary"),
    )(dgate, proj, bias_tile, d_proj)


def _in_turn(heads):
    while heads:
        heads = [g for g in heads if next(g, heads) is not heads]


def _chunk_gates(gc, gr, h, mprev):
    L = CHUNK
    icol, bcol = gc[:, h:h + 1], gc[:, h + NH:h + NH + 1]
    irow, brow = gr[h:h + 1, :], gr[h + NH:h + NH + 1, :]
    tri = _iota((L, L), 0) >= _iota((L, L), 1)
    log_d = jnp.where(tri, bcol - brow + irow, -jnp.inf)
    inter = bcol + mprev
    mt = jnp.maximum(inter, jnp.max(log_d, axis=1, keepdims=True))
    dw = jnp.exp(log_d - mt)
    iw = jnp.exp(inter - mt)
    g = brow[:, L - 1:L]
    wlog_col = g - bcol + icol
    wlog_row = g - brow + irow
    mnew = jnp.maximum(g + mprev, jnp.max(wlog_row, axis=1, keepdims=True))
    wcol = jnp.exp(wlog_col - mnew)
    decay = jnp.exp(g + mprev - mnew)
    return dw, iw, mt, wcol, decay, mnew


def _mlstm_fwd(proj, gcol, grow, t, wc, dh):
    nc = t // CHUNK
    wm = NH * dh
    assert wc == wm, (wc, wm)
    qoff = 3 * wc // wm
    scale = dh ** -0.5

    def body(q_ref, k_ref, v_ref, gc_ref, gr_ref, h_ref, cs_ref, ns_ref, c_s, n_s, m_s):
        @pl.when(pl.program_id(0) == 0)
        def _():
            c_s[...] = jnp.zeros_like(c_s)
            n_s[...] = jnp.zeros_like(n_s)
            m_s[...] = jnp.zeros_like(m_s)

        gc, gr = gc_ref[...], gr_ref[0]
        done = [None] * NH

        def head(h):
            cols = slice(h * dh, (h + 1) * dh)
            mprev = m_s[h, 0:1, 0:1]
            cprev = c_s[h]
            n8 = n_s[h]
            nprev = n8[0:1]
            qs = q_ref[:, cols] * scale
            k = k_ref[:, cols]
            qs_b, k_b, v_b = qs.astype(BF16), k.astype(BF16), v_ref[:, cols].astype(BF16)
            qk = _dot(qs_b, k_b, _NT)
            yield
            q_c = _dot(qs_b, cprev.astype(BF16))
            yield
            dw, iw, mt, wcol, decay, mnew = _chunk_gates(gc, gr, h, mprev)
            yield
            s = qk * dw
            wk = wcol * k
            num = _dot(s.astype(BF16), v_b) + iw * q_c
            yield
            c_new = decay * cprev + _dot(wk.astype(BF16), v_b, _TN)
            yield
            den = jnp.sum(s, axis=1, keepdims=True) + iw * jnp.sum(qs * nprev, axis=1, keepdims=True)
            done[h] = (cprev, jnp.where(_iota(n8.shape, 0) == 1, mprev, n8),
                       num / jnp.maximum(jnp.abs(den), jnp.exp(-mt)), c_new,
                       decay * n8 + jnp.sum(wk, axis=0, keepdims=True), mnew)

        _in_turn([head(h) for h in range(NH)])
        for h, (c_old, n_old, h_out, c_new, n_new, m_new) in enumerate(done):
            cs_ref[h] = c_old
            ns_ref[h] = n_old
            h_ref[:, h * dh:(h + 1) * dh] = h_out
            c_s[h] = c_new
            n_s[h] = n_new
            m_s[h] = jnp.broadcast_to(m_new, m_s.shape[1:])

    grp = lambda off: pl.BlockSpec((CHUNK, wm), lambda c: (c, qoff + off))
    return pl.pallas_call(
        body, name="mlstm_fwd", grid=(nc,),
        in_specs=[grp(0), grp(1), grp(2),
                  pl.BlockSpec((CHUNK, LANE), lambda c: (c, 0)),
                  pl.BlockSpec((1, 8, CHUNK), lambda c: (c, 0, 0))],
        out_specs=[pl.BlockSpec((CHUNK, wm), lambda c: (c, 0)),
                   pl.BlockSpec((NH, None, dh, dh), lambda c: (0, c, 0, 0)),
                   pl.BlockSpec((NH, None, 8, dh), lambda c: (0, c, 0, 0))],
        out_shape=[jax.ShapeDtypeStruct((t, wm), F32),
                   jax.ShapeDtypeStruct((NH, nc, dh, dh), F32),
                   jax.ShapeDtypeStruct((NH, nc, 8, dh), F32)],
        scratch_shapes=[pltpu.VMEM((NH, dh, dh), F32), pltpu.VMEM((NH, 8, dh), F32), pltpu.VMEM((NH, 8, LANE), F32)],
        compiler_params=_params("arbitrary"),
    )(proj, proj, proj, gcol, grow)


def _mlstm_bwd(proj, gcol, grow, hval, dh_in, cs, ns, d_proj, t, wc, dh):
    nc = t // CHUNK
    wm = NH * dh
    assert wc == wm, (wc, wm)
    qoff = 3 * wc // wm
    scale = dh ** -0.5
    L = CHUNK

    def body(q_ref, k_ref, v_ref, gc_ref, gr_ref, h_ref, dh_ref, cs_ref, ns_ref, d_proj_in,
             dqkv_ref, dg_ref, dc_s, dn_s):
        @pl.when(pl.program_id(0) == 0)
        def _():
            dc_s[...] = jnp.zeros_like(dc_s)
            dn_s[...] = jnp.zeros_like(dn_s)

        gc, gr = gc_ref[...], gr_ref[0]
        eye = _iota((L, L), 0) == _iota((L, L), 1)
        lane = _iota((L, LANE), 1)
        last = _iota((L, 1), 0) == L - 1
        done = [None] * NH

        def head(h):
            cols = slice(h * dh, (h + 1) * dh)
            ns8 = ns_ref[h]
            nprev = ns8[0:1]
            mprev = ns8[1:2, 0:1]
            cprev = cs_ref[h]
            dcn = dc_s[h]
            dn8 = dn_s[h]
            dnn = dn8[0:1]

            qs = q_ref[:, cols] * scale
            k = k_ref[:, cols]
            qs_b, k_b, v_b = qs.astype(BF16), k.astype(BF16), v_ref[:, cols].astype(BF16)
            qk = _dot(qs_b, k_b, _NT)
            yield
            dw, iw, mt, wcol, decay, _ = _chunk_gates(gc, gr, h, mprev)
            yield
            s = qk * dw
            den = jnp.sum(s, axis=1, keepdims=True) + iw * jnp.sum(qs * nprev, axis=1, keepdims=True)
            emt = jnp.exp(-mt)
            r = 1.0 / jnp.maximum(jnp.abs(den), emt)
            dout = dh_ref[:, cols]
            dnum = dout * r
            dden = (-jnp.sum(dout * h_ref[:, cols], axis=1, keepdims=True) * r
                    * jnp.where(jnp.abs(den) > emt, jnp.sign(den), 0.0))
            dnum_b = dnum.astype(BF16)
            cprev_b = cprev.astype(BF16)
            dcn_b = dcn.astype(BF16)
            yield

            g_raw = _dot(dnum_b, v_b, _NT)
            yield
            q_inter = _dot(dnum_b, cprev_b, _NT)
            yield
            k_raw = _dot(v_b, dcn_b, _NT)
            yield
            gd = (g_raw + dden) * dw
            gd_b = gd.astype(BF16)
            dqs_inter = iw * (q_inter + dden * nprev)
            dk_inter = wcol * (k_raw + dnn)
            wk = wcol * k
            iq = iw * qs
            dqs = _dot(gd_b, k_b) + dqs_inter
            yield
            dk = _dot(gd_b, qs_b, _TN) + dk_inter
            yield
            dv = _dot(s.astype(BF16), dnum_b, _TN) + _dot(wk.astype(BF16), dcn_b)
            yield
            dc_new = decay * dcn + _dot(iq.astype(BF16), dnum_b, _TN)
            yield

            e = gd * qk
            e_cols = jnp.sum(jnp.where(eye, jnp.sum(e, axis=0, keepdims=True), 0.0), axis=1, keepdims=True)
            yield
            k_inter = jnp.sum(k * dk_inter, axis=1, keepdims=True)
            rq = jnp.sum(e, axis=1, keepdims=True) + jnp.sum(qs * dqs_inter, axis=1, keepdims=True)
            rk = e_cols + k_inter
            hsum = jnp.sum(k_inter, axis=0, keepdims=True)
            jdec = decay * (jnp.sum(jnp.sum(dcn * cprev, axis=1, keepdims=True), axis=0, keepdims=True)
                            + jnp.sum(dnn * nprev, axis=1, keepdims=True))
            db = rq - rk + jnp.where(last, hsum + jdec, 0.0)
            done[h] = (jnp.where(lane == 0, rk, jnp.where(lane == 1, db, 0.0)),
                       (dqs * scale).astype(BF16), dk.astype(BF16), dv.astype(BF16), dc_new,
                       decay * dn8 + jnp.sum(iq * dden, axis=0, keepdims=True))

        _in_turn([head(h) for h in range(NH)])
        for h, (dgate, dq, dk, dv, dc_new, dn_new) in enumerate(done):
            dg_ref[h] = dgate
            for part, grad in enumerate((dq, dk, dv)):
                dqkv_ref[:, part * wm + h * dh:part * wm + (h + 1) * dh] = grad
            dc_s[h] = dc_new
            dn_s[h] = dn_new

    rc = lambda c: nc - 1 - c
    grp = lambda off: pl.BlockSpec((L, wm), lambda c: (rc(c), qoff + off))
    hm = pl.BlockSpec((L, wm), lambda c: (rc(c), 0))
    assert qoff % 3 == 0, qoff
    return pl.pallas_call(
        body, name="mlstm_bwd", grid=(nc,),
        in_specs=[grp(0), grp(1), grp(2),
                  pl.BlockSpec((L, LANE), lambda c: (rc(c), 0)),
                  pl.BlockSpec((1, 8, L), lambda c: (rc(c), 0, 0)),
                  hm, hm,
                  pl.BlockSpec((NH, None, dh, dh), lambda c: (0, rc(c), 0, 0)),
                  pl.BlockSpec((NH, None, 8, dh), lambda c: (0, rc(c), 0, 0)),
                  pl.BlockSpec(memory_space=pl.ANY)],
        out_specs=[pl.BlockSpec((L, 3 * wm), lambda c: (rc(c), qoff // 3)),
                   pl.BlockSpec((NH, L, LANE), lambda c: (0, rc(c), 0))],
        out_shape=[jax.ShapeDtypeStruct(d_proj.shape, d_proj.dtype), jax.ShapeDtypeStruct((NH, t, LANE), F32)],
        input_output_aliases={9: 0},
        scratch_shapes=[pltpu.VMEM((NH, dh, dh), F32), pltpu.VMEM((NH, 8, dh), F32)],
        compiler_params=_params("arbitrary"),
    )(proj, proj, proj, gcol, grow, hval, dh_in, cs, ns, d_proj)


def _head_norm(hv):
    mu = jnp.mean(hv, axis=1, keepdims=True)
    hc = hv - mu
    rstd = lax.rsqrt(jnp.mean(hc * hc, axis=1, keepdims=True) + HN_EPS)
    return hc * rstd, rstd


def _hnorm_fwd(hval, proj, gain, y, t, wc, dh, tr=512):
    ooff = 3 * wc // dh + 3 * NH
    tr = min(tr, t)

    def body(h_ref, o_ref, g_ref, y_in, y_ref):
        hhat, _ = _head_norm(h_ref[...])
        y_ref[...] = (_sigmoid(o_ref[...]) * hhat * g_ref[...]).astype(BF16)

    return pl.pallas_call(
        body, name="hnorm_fwd", grid=(t // tr, NH),
        in_specs=[pl.BlockSpec((tr, dh), lambda i, h: (i, h)),
                  pl.BlockSpec((tr, dh), lambda i, h: (i, ooff + h)),
                  pl.BlockSpec((1, dh), lambda i, h: (0, h)),
                  pl.BlockSpec(memory_space=pl.ANY)],
        out_specs=pl.BlockSpec((None, tr, dh), lambda i, h: (1, i, h)),
        out_shape=jax.ShapeDtypeStruct(y.shape, BF16),
        input_output_aliases={3: 0},
        compiler_params=_params("parallel", "parallel"),
    )(hval, proj, gain, y)


def _hnorm_bwd(dy, hval, proj, gain, t, wc, dh, tr=512):
    ooff = 3 * wc // dh + 3 * NH
    tr = min(tr, t)
    yoff = wc // dh

    def body(dy_ref, h_ref, o_ref, g_ref, do_ref, dh_ref, dg_ref):
        i = pl.program_id(1)
        hhat, rstd = _head_norm(h_ref[...])
        gain_v = g_ref[...]
        sig = _sigmoid(o_ref[...])
        d = dy_ref[...]
        do_ref[...] = (d * hhat * gain_v * sig * (1.0 - sig)).astype(BF16)
        dhn = d * sig
        part = jnp.sum(dhn * hhat, axis=0, keepdims=True)

        @pl.when(i == 0)
        def _():
            dg_ref[...] = part

        @pl.when(i > 0)
        def _():
            dg_ref[...] += part

        dhat = dhn * gain_v
        dh_ref[...] = rstd * (dhat - jnp.mean(dhat, axis=1, keepdims=True)
                              - hhat * jnp.mean(dhat * hhat, axis=1, keepdims=True))

    blk = lambda off: pl.BlockSpec((tr, dh), lambda h, i: (i, off + h))
    return pl.pallas_call(
        body, name="hnorm_bwd", grid=(NH, t // tr),
        in_specs=[blk(yoff), blk(0), blk(ooff), pl.BlockSpec((1, dh), lambda h, i: (0, h))],
        out_specs=[blk(ooff), blk(0), pl.BlockSpec((1, dh), lambda h, i: (0, h))],
        out_shape=[jax.ShapeDtypeStruct(proj.shape, BF16), jax.ShapeDtypeStruct((t, NH * dh), F32),
                   jax.ShapeDtypeStruct((1, NH * dh), F32)],
        compiler_params=_params("parallel", "arbitrary"),
    )(dy, hval, proj, gain)


def _ln_stats(z):
    mu = jnp.mean(z, axis=1, keepdims=True)
    zc = z - mu
    rstd = lax.rsqrt(jnp.mean(zc * zc, axis=1, keepdims=True) + LN_EPS)
    return zc * rstd, rstd


def _ln_bwd(dy, xhat, rstd, g):
    dxh = dy * g
    return rstd * (dxh - jnp.mean(dxh, axis=1, keepdims=True) - xhat * jnp.mean(dxh * xhat, axis=1, keepdims=True))


def _accum(ref, i, part):
    @pl.when(i == 0)
    def _():
        ref[...] = part

    @pl.when(i > 0)
    def _():
        ref[...] += part


def _ln1_fwd(x, mix, g, b, tr=256, after=None):
    t, d = x.shape

    def body(x_ref, m_ref, g_ref, b_ref, xh_ref, rs_ref, xb_ref):
        xhat, rstd = _ln_stats(ALPHA * x_ref[...] + m_ref[...])
        xh_ref[...] = xhat
        rs_ref[...] = rstd
        xb_ref[...] = (xhat * g_ref[...] + b_ref[...]).astype(BF16)

    row = pl.BlockSpec((tr, d), lambda i: (i, 0))
    vec = pl.BlockSpec((1, d), lambda i: (0, 0))
    body, in_specs, args = _placed(after, body, [row, row, vec, vec], (x, mix, g, b))
    return pl.pallas_call(
        body, name="ln1_fwd", grid=(t // tr,),
        in_specs=in_specs,
        out_specs=[row, pl.BlockSpec((tr, 1), lambda i: (i, 0)), row],
        out_shape=[jax.ShapeDtypeStruct((t, d), F32), jax.ShapeDtypeStruct((t, 1), F32),
                   jax.ShapeDtypeStruct((t, d), BF16)],
        compiler_params=_params("parallel"),
    )(*args)


def _ln2_loss(xhat1, g1, b1, ff, target, g2, b2, tr=256):
    t, d = ff.shape

    def body(xh_ref, g1_ref, b1_ref, f_ref, t_ref, g_ref, b_ref, dz_ref, dzb_ref, dg_ref, db_ref, l_ref):
        i = pl.program_id(0)
        x1 = xh_ref[...] * g1_ref[...] + b1_ref[...]
        xhat, rstd = _ln_stats(ALPHA * x1 + f_ref[...])
        gv = g_ref[...]
        e = xhat * gv + b_ref[...] - t_ref[...]
        lsum = jnp.sum(jnp.sum(e * e, axis=1, keepdims=True), axis=0, keepdims=True) * (0.5 / d)
        dy = e * (1.0 / d)
        _accum(dg_ref, i, jnp.sum(dy * xhat, axis=0, keepdims=True))
        _accum(db_ref, i, jnp.sum(dy, axis=0, keepdims=True))
        _accum(l_ref, i, jnp.broadcast_to(lsum, l_ref.shape))
        dz = _ln_bwd(dy, xhat, rstd, gv)
        dz_ref[...] = dz
        dzb_ref[...] = dz.astype(BF16)

    row = pl.BlockSpec((tr, d), lambda i: (i, 0))
    vec = pl.BlockSpec((1, d), lambda i: (0, 0))
    return pl.pallas_call(
        body, name="ln2_loss", grid=(t // tr,),
        in_specs=[row, vec, vec, row, row, vec, vec],
        out_specs=[row, row, vec, vec, pl.BlockSpec((8, LANE), lambda i: (0, 0))],
        out_shape=[jax.ShapeDtypeStruct((t, d), F32), jax.ShapeDtypeStruct((t, d), BF16),
                   jax.ShapeDtypeStruct((1, d), F32), jax.ShapeDtypeStruct((1, d), F32),
                   jax.ShapeDtypeStruct((8, LANE), F32)],
        compiler_params=_params("arbitrary"),
    )(xhat1, g1, b1, ff, target, g2, b2)


def _ln1_bwd(dz2, dffn, xhat1, rstd1, g1, tr=256, after=None):
    t, d = dz2.shape

    def body(a_ref, f_ref, xh_ref, rs_ref, g_ref, dz_ref, dzb_ref, dg_ref, db_ref):
        i = pl.program_id(0)
        dy = ALPHA * a_ref[...] + f_ref[...]
        xhat = xh_ref[...]
        _accum(dg_ref, i, jnp.sum(dy * xhat, axis=0, keepdims=True))
        _accum(db_ref, i, jnp.sum(dy, axis=0, keepdims=True))
        dz = _ln_bwd(dy, xhat, rs_ref[...], g_ref[...])
        dz_ref[...] = dz
        dzb_ref[...] = dz.astype(BF16)

    row = pl.BlockSpec((tr, d), lambda i: (i, 0))
    vec = pl.BlockSpec((1, d), lambda i: (0, 0))
    body, in_specs, args = _placed(after, body, [row, row, row, pl.BlockSpec((tr, 1), lambda i: (i, 0)), vec],
                                   (dz2, dffn, xhat1, rstd1, g1))
    return pl.pallas_call(
        body, name="ln1_bwd", grid=(t // tr,),
        in_specs=in_specs,
        out_specs=[row, row, vec, vec],
        out_shape=[jax.ShapeDtypeStruct((t, d), F32), jax.ShapeDtypeStruct((t, d), BF16),
                   jax.ShapeDtypeStruct((1, d), F32), jax.ShapeDtypeStruct((1, d), F32)],
        compiler_params=_params("arbitrary"),
    )(*args)


def _ffn_act_fwd(hid0, w_fc, b_fc, t, dff, after=None):
    nb = dff // LANE

    def body(hv_ref, hg_ref, wv_ref, wg_ref, bv_ref, bg_ref, a_ref):
        val = _conv(hv_ref[...], wv_ref[...]) + bv_ref[...]
        gate = _conv(hg_ref[...], wg_ref[...]) + bg_ref[...]
        a_ref[...] = (gate * _sigmoid(gate) * val).astype(BF16)

    col = lambda off: pl.BlockSpec((t, LANE), lambda j: (0, j + off))
    w3 = lambda off: pl.BlockSpec((3, LANE), lambda j: (0, j + off))
    w1 = lambda off: pl.BlockSpec((1, LANE), lambda j: (0, j + off))
    body, in_specs, args = _placed(after, body, [col(0), col(nb), w3(0), w3(nb), w1(0), w1(nb)],
                                   (hid0, hid0, w_fc, w_fc, b_fc, b_fc))
    return pl.pallas_call(
        body, name="ffn_act_fwd", grid=(nb,),
        in_specs=in_specs,
        out_specs=col(0),
        out_shape=jax.ShapeDtypeStruct((t, dff), BF16),
        compiler_params=_params("parallel"),
    )(*args)


def _ffn_act_bwd(da, hid0, w_fc, b_fc, t, dff, after=None):
    nb = dff // LANE

    def body(da_ref, hv_ref, hg_ref, wv_ref, wg_ref, bv_ref, bg_ref,
             dhv_ref, dhg_ref, dwv_ref, dwg_ref, dbv_ref, dbg_ref):
        hv, hg, wv, wg = hv_ref[...], hg_ref[...], wv_ref[...], wg_ref[...]
        rv, rg = _rolled(hv), _rolled(hg)
        val = _conv(hv, wv, rv) + bv_ref[...]
        gate = _conv(hg, wg, rg) + bg_ref[...]
        sig = _sigmoid(gate)
        d = da_ref[...]
        dsig = d * sig
        dval = dsig * gate
        dgate = dsig * val * (1.0 + gate * (1.0 - sig))
        dhv_ref[...] = _conv_t(dval, wv).astype(BF16)
        dhg_ref[...] = _conv_t(dgate, wg).astype(BF16)
        dwv_ref[...] = _conv_dw(dval, hv, rv)
        dwg_ref[...] = _conv_dw(dgate, hg, rg)
        dbv_ref[...] = jnp.sum(dval, axis=0, keepdims=True)
        dbg_ref[...] = jnp.sum(dgate, axis=0, keepdims=True)

    col = lambda off: pl.BlockSpec((t, LANE), lambda j: (0, j + off))
    w3 = lambda off: pl.BlockSpec((3, LANE), lambda j: (0, j + off))
    w1 = lambda off: pl.BlockSpec((1, LANE), lambda j: (0, j + off))
    s3 = jax.ShapeDtypeStruct((3, dff), F32)
    s1 = jax.ShapeDtypeStruct((1, dff), F32)
    body, in_specs, args = _placed(after, body, [col(0), col(0), col(nb), w3(0), w3(nb), w1(0), w1(nb)],
                                   (da, hid0, hid0, w_fc, w_fc, b_fc, b_fc))
    return pl.pallas_call(
        body, name="ffn_act_bwd", grid=(nb,),
        in_specs=in_specs,
        out_specs=[col(0), col(0), w3(0), w3(0), w1(0), w1(0)],
        out_shape=[jax.ShapeDtypeStruct((t, dff), BF16)] * 2 + [s3, s3, s1, s1],
        compiler_params=_params("parallel"),
    )(*args)


class _Ready:
    def __init__(self, **weights):
        self.weights = weights

    def begin(self, after):
        return None

    def forward(self, name, after):
        return None

    def get(self, name, after):
        return self.weights[name]


class _Kept:
    def __init__(self):
        self.grads = {}

    def start(self, name, grad):
        self.grads[name] = grad
        return None

    def relay(self, name, after):
        return None

    def meanwhile(self, small, loss, after):
        return None


def _behind(a, token):
    return a if token is None else a + token[0:1, 0:1].reshape((1,) * a.ndim)


def _local_step(x, target, w_in, b_gates, w_sc, gain, w_out, ln1_g, ln1_b, w_up, w_fc, b_fc, w_down, ln2_g, ln2_b,
                gx=None, wx=None, x_b=None):
    t, d = x.shape
    wc = d // 2
    dh = (d - wc) // NH
    wm = NH * dh
    dff = w_fc.shape[1] // 2
    if wx is None:
        wx = _Ready(w_out=w_out, w_up=w_up, w_down=w_down)
    ninp = 3 * wc + 4 * wm + LANE
    nin = 3 * wc + 4 * wm
    gate_tile = nin // LANE
    nc = t // CHUNK
    bias_tile = jnp.pad(b_gates, ((0, 0), (0, LANE - 2 * NH)))

    if x_b is None:
        x_b = x.astype(BF16)
    proj = _matmul(x_b, w_in, "nt", F32, "proj", tm=512, tn=2432, tk=d, n=ninp, after=wx.begin(w_in))
    y = _sconv_fwd(proj, w_sc, t, wc)
    gcol = _gates_prep(proj, bias_tile, t, gate_tile)
    grow = gcol[:, :8].T.reshape(8, nc, CHUNK).transpose(1, 0, 2)
    hval, cs, ns = _mlstm_fwd(proj, gcol, grow, t, wc, dh)
    y = _hnorm_fwd(hval, proj, gain, y, t, wc, dh)
    tok = wx.forward("w_out", y)
    w_out = wx.get("w_out", tok)
    mix = _matmul(y, w_out, "nn", F32, "out_proj", tm=512, tn=1024, tk=wc, a_blocked=True, after=tok)
    xhat1, rstd1, x1_b = _ln1_fwd(x, mix, ln1_g, ln1_b, after=wx.forward("w_up", mix))
    w_up = wx.get("w_up", x1_b)
    wsl = w_up.shape[2]
    hid0 = _matmul(x1_b, w_up, "nn", F32, "ffn_up", tm=1024, tn=wsl, tk=d, b_blocked=True)
    act = _ffn_act_fwd(hid0, w_fc, b_fc, t, dff, after=wx.forward("w_down", hid0))
    w_down = wx.get("w_down", act)
    ff = _matmul(act, w_down, "nn", F32, "ffn_down", tm=1024, tn=512, tk=dff)
    dz2, dz2_b, d_ln2_g, d_ln2_b, loss = _ln2_loss(xhat1, ln1_g, ln1_b, ff, target, ln2_g, ln2_b)

    if gx is None:
        gx = _Kept()
    d_w_down = _matmul(act, dz2_b, "tn", BF16, "ffn_down_dw", tm=1408, tn=1024, tk=t)
    d_act = _matmul(dz2_b, w_down, "nt", F32, "ffn_down_dx", tm=2048, tn=512, tk=d, after=gx.start("w_down", d_w_down))
    *d_hid0, dwv, dwg, dbv, dbg = _ffn_act_bwd(d_act, hid0, w_fc, b_fc, t, dff, after=gx.relay("w_down", d_act))
    d_w_fc = jnp.concatenate([dwv, dwg], axis=1)
    d_b_fc = jnp.concatenate([dbv, dbg], axis=1)
    d_hid0 = tuple(d_hid0[:2])
    d_w_up = _matmul(x1_b, d_hid0, "tn", BF16, "ffn_up_dw", tm=1024, tn=wsl, tk=t, o_width=wsl)
    d_x1_ffn = _matmul(d_hid0, w_up, "nt", F32, "ffn_up_dx", tm=1024, tn=1024, tk=wsl, b_blocked=True,
                       after=gx.start("w_up", d_w_up))
    dz1, dz1_b, d_ln1_g, d_ln1_b = _ln1_bwd(dz2, d_x1_ffn, xhat1, rstd1, ln1_g, after=gx.relay("w_up", d_x1_ffn))

    d_w_out = _matmul(y, dz1_b, "tn", BF16, "out_proj_dw", tm=1024, tn=1024, tk=t, a_blocked=True)
    dy = _matmul(dz1_b, w_out, "nt", F32, "out_proj_dx", tm=1024, tn=1024, tk=d, after=gx.start("w_out", d_w_out))
    d_proj, d_hval, d_gain = _hnorm_bwd(dy, hval, proj, gain, t, wc, dh)
    d_proj, d_w_sc = _sconv_bwd(dy, proj, w_sc, d_proj, t, wc, after=gx.relay("w_out", dy))
    d_proj, dgate = _mlstm_bwd(proj, gcol, grow, hval, d_hval, cs, ns, d_proj, t, wc, dh)
    d_proj, d_b_gates = _gates_bwd(dgate, proj, bias_tile, d_proj, t, gate_tile)
    d_w_in = _matmul(d_proj, x_b, "tn", BF16, "proj_dw", tm=2432, tn=1024, tk=t)
    small = dict(b_gates=d_b_gates[:, :2 * NH], w_sc_conv=d_w_sc, mh_gain=d_gain, ln1_g=d_ln1_g, ln1_b=d_ln1_b,
                 w_ffn_conv=d_w_fc, b_ffn_conv=d_b_fc, ln2_g=d_ln2_g, ln2_b=d_ln2_b)
    token = gx.start("w_in", d_w_in)
    token = gx.relay("w_in", gx.meanwhile(small, loss, token))
    grad_x = _matmul(d_proj, w_in, "nn", F32, "proj_dx", tm=512, tn=512, tk=ninp, add=dz1, add_scale=ALPHA, after=token)
    return loss, grad_x, small, gx


HBM = pl.BlockSpec(memory_space=pltpu.HBM)


def _place():
    return lax.axis_index("x"), lax.axis_index("y"), lax.axis_index("c")


def _index(p):
    return 4 * p[0] + 2 * p[1] + p[2]


def _all_gather(arrs, name):
    n = len(arrs)

    def body(*refs):
        ins, outs = refs[:n], refs[n:2 * n]
        send_sems, recv_sems, local_sems = refs[2 * n:]
        x, y, c = _place()
        me, sibling = (x, y, c), (x, y, 1 - c)
        chips = [(1 - x, y), (x, 1 - y), (1 - x, 1 - y)]

        def copy(a, k, block, to, own=False):
            dst = outs[a].at[_index(block)]
            return pltpu.make_async_remote_copy(
                src_ref=ins[a] if own else dst, dst_ref=dst,
                send_sem=send_sems.at[k * n + a], recv_sem=recv_sems.at[k * n + a],
                device_id=to, device_id_type=MESH)

        mine = [pltpu.make_async_copy(ins[a], outs[a].at[_index(me)], local_sems.at[a]) for a in range(n)]
        for cp in mine:
            cp.start()
        first = []
        for a in range(n):
            first.append(copy(a, 0, me, sibling, own=True))
            first += [copy(a, 1 + j, me, (*chip, c), own=True) for j, chip in enumerate(chips)]
        for cp in first:
            cp.start()
        passed = []
        for j, chip in enumerate(chips):
            for a in range(n):
                copy(a, 1 + j, (*chip, c), me).wait_recv()
                cp = copy(a, 4 + j, (*chip, c), sibling)
                cp.start()
                passed.append(cp)
        for a in range(n):
            copy(a, 0, sibling, me).wait_recv()
            for j, chip in enumerate(chips):
                copy(a, 4 + j, (*chip, 1 - c), me).wait_recv()
        for cp in first + passed:
            cp.wait_send()
        for cp in mine:
            cp.wait()

    return pl.pallas_call(
        body, name=name, in_specs=[HBM] * n, out_specs=[HBM] * n,
        out_shape=[jax.ShapeDtypeStruct((N_DEV,) + a.shape, a.dtype) for a in arrs],
        scratch_shapes=[pltpu.SemaphoreType.DMA((7 * n,)), pltpu.SemaphoreType.DMA((7 * n,)),
                        pltpu.SemaphoreType.DMA((n,))],
    )(*arrs)


SEM = pl.BlockSpec(memory_space=pltpu.SEMAPHORE)
EFFECT = pltpu.SideEffectType.DATAFLOW_SIDE_EFFECTING


def _chips(x, y):
    return [(1 - x, y), (x, 1 - y), (1 - x, 1 - y)]


N_CHIP = N_DEV // 2


def _pair_route(x, y, c):
    return [((x, y, 1 - c), 2 * q + (1 - c), q, q) for q in range(N_CHIP)]


def _chip_route(x, y, c):
    mine = 2 * x + y
    return [((*chip, c), 2 * chip[0] + chip[1], mine, 2 * chip[0] + chip[1]) for chip in _chips(x, y)]


def _exchange_pieces(g_ref, land_ref, width, tail):
    if not tail:
        return [(lambda i: g_ref.at[i], lambda s: land_ref.at[s])]
    rows = lambda i, n: pl.ds(pl.multiple_of(i * width, IN_TAIL), n)
    return [(lambda i: g_ref.at[rows(i, width), :], lambda s: land_ref.at[s, pl.ds(0, width), :]),
            (lambda i: g_ref.at[rows(i + 1, IN_TAIL), :], lambda s: land_ref.at[s, pl.ds(width, IN_TAIL), :])]


def _chip_slot(x, y, c):
    return 2 * x + y


def _exchange_start(grad, route, tail, name, own_slot=None):
    width = IN_SLAB if tail else grad.shape[1]
    n_p = 2 if tail else 1
    n_c = len(route(0, 0, 0))
    land_shape = (N_CHIP, width + (IN_TAIL if tail else 0), grad.shape[-1])
    assert not (tail and own_slot)

    def body(g_ref, land_ref, send_sems, recv_sems, g_thru, land_thru, token):
        for j, (peer, slab, slot, _) in enumerate(route(*_place())):
            for p, (src, dst) in enumerate(_exchange_pieces(g_ref, land_ref, width, tail)):
                pltpu.make_async_remote_copy(src_ref=src(slab), dst_ref=dst(slot), send_sem=send_sems.at[j * n_p + p],
                                             recv_sem=recv_sems.at[j * n_p + p], device_id=peer,
                                             device_id_type=MESH).start()
        if own_slot:
            mine = own_slot(*_place())
            pltpu.make_async_copy(g_ref.at[mine], land_ref.at[mine], send_sems.at[n_c * n_p]).start()
        token[...] = jnp.zeros_like(token)

    return pl.pallas_call(
        body, name=name,
        out_shape=(pltpu.SemaphoreType.DMA((n_c * n_p + bool(own_slot),)), pltpu.SemaphoreType.DMA((n_c * n_p,)),
                   pltpu.HBM(grad.shape, grad.dtype), pltpu.HBM(land_shape, grad.dtype),
                   jax.ShapeDtypeStruct((8, LANE), F32)),
        in_specs=(HBM, HBM), out_specs=(SEM, SEM, HBM, HBM, pl.BlockSpec(memory_space=pltpu.VMEM)),
        input_output_aliases={0: 2, 1: 3},
        compiler_params=pltpu.CompilerParams(has_side_effects=EFFECT),
    )(pltpu.with_memory_space_constraint(grad, pltpu.HBM),
      pltpu.with_memory_space_constraint(lax.empty(land_shape, grad.dtype), pltpu.HBM))


def _exchange_wait(send_sems, recv_sems, g_thru, land_thru, after, route, tail, name, own_slot=None):
    width = IN_SLAB if tail else g_thru.shape[1]
    n_p = 2 if tail else 1

    def body(g_ref, land_ref, send_sems, recv_sems, after_ref, g_dead, got_ref):
        places = route(*_place())
        for j, (peer, slab, _, slot) in enumerate(places):
            for p, (src, dst) in enumerate(_exchange_pieces(g_ref, land_ref, width, tail)):
                cp = pltpu.make_async_remote_copy(src_ref=src(slab), dst_ref=dst(slot),
                                                  send_sem=send_sems.at[j * n_p + p], recv_sem=recv_sems.at[j * n_p + p],
                                                  device_id=peer, device_id_type=MESH)
                cp.wait_send()
                cp.wait_recv()
        if own_slot:
            mine = own_slot(*_place())
            pltpu.make_async_copy(g_ref.at[mine], land_ref.at[mine], send_sems.at[len(places) * n_p]).wait()

    return pl.pallas_call(
        body, name=name,
        out_shape=(pltpu.HBM(g_thru.shape, g_thru.dtype), pltpu.HBM(land_thru.shape, land_thru.dtype)),
        in_specs=(HBM, HBM, SEM, SEM, pl.BlockSpec(memory_space=pl.ANY)), out_specs=(HBM, HBM),
        input_output_aliases={0: 0, 1: 1},
        compiler_params=pltpu.CompilerParams(has_side_effects=EFFECT),
    )(g_thru, land_thru, send_sems, recv_sems, after)


def _pair_add(grad, pair, core, tail, name):
    rows, cols = (IN_SLAB if tail else grad.shape[1]), grad.shape[-1]
    total = pair.shape[1]

    def body(core_ref, *refs):
        if tail:
            g_ref, t_ref, p_ref, o_ref = refs
            o_ref[0:rows, :] = (g_ref[...].astype(F32) + p_ref[0:rows, :].astype(F32)).astype(BF16)
            o_ref[rows:total, :] = (t_ref[...].astype(F32) + p_ref[rows:total, :].astype(F32)).astype(BF16)
        else:
            g_ref, p_ref, o_ref = refs
            o_ref[...] = (g_ref[...].astype(F32) + p_ref[...].astype(F32)).astype(BF16)

    if tail:
        tc = _fit(cols, 512)
        grid = (N_CHIP, cols // tc)
        slab = pl.BlockSpec((None, total, tc), lambda q, i, core_ref: (q, 0, i))
        per = IN_SLAB // IN_TAIL
        in_specs = [pl.BlockSpec((rows, tc), lambda q, i, core_ref: (2 * q + core_ref[0], i)),
                    pl.BlockSpec((IN_TAIL, tc), lambda q, i, core_ref: ((2 * q + core_ref[0] + 1) * per, i))]
    else:
        tr = _rows(rows, 1024)
        grid = (N_CHIP, rows // tr)
        slab = pl.BlockSpec((None, tr, cols), lambda q, i, core_ref: (q, i, 0))
        in_specs = [pl.BlockSpec((None, tr, cols), lambda q, i, core_ref: (2 * q + core_ref[0], i, 0))]
    return pl.pallas_call(
        body, name=name,
        grid_spec=pltpu.PrefetchScalarGridSpec(num_scalar_prefetch=1, grid=grid,
                                               in_specs=in_specs + [slab], out_specs=slab),
        out_shape=jax.ShapeDtypeStruct(pair.shape, BF16),
        compiler_params=_params("parallel", "parallel"),
    )(core, *([grad, grad] if tail else [grad]), pair)


def _relay_places(x, y, c):
    came_from = (c * (1 - x) + (1 - c) * x, c * y + (1 - c) * (1 - y), c)
    pass_to = (c * x + (1 - c) * (1 - x), c * (1 - y) + (1 - c) * y, c)
    return 2 - c, came_from, pass_to, pass_to


OWN = 4


def _gather_start(blocks, after, name, spare=(), relayed=False):
    n = len(blocks)
    lands = [(N_DEV + (a in spare),) + b.shape for a, b in enumerate(blocks)]

    def body(*refs):
        b_refs, land_refs = refs[:n], refs[n:2 * n]
        send_sems, recv_sems = refs[2 * n + 1:3 * n + 1], refs[3 * n + 1:4 * n + 1]
        token = refs[-1]
        x, y, c = _place()
        me = _index((x, y, c))
        for a in range(n):
            targets = [(x, y, 1 - c)] + [(*chip, c) for chip in _chips(x, y)]
            for k, to in enumerate(targets[:3] if relayed else targets):
                pltpu.make_async_remote_copy(src_ref=b_refs[a], dst_ref=land_refs[a].at[me], send_sem=send_sems[a].at[k],
                                             recv_sem=recv_sems[a].at[k], device_id=to, device_id_type=MESH).start()
        for a in range(n):
            pltpu.make_async_copy(b_refs[a], land_refs[a].at[me], send_sems[a].at[OWN]).start()
        token[...] = jnp.zeros_like(token)

    sems = [pltpu.SemaphoreType.DMA((OWN + 1,))] * n
    out = pl.pallas_call(
        body, name=name,
        out_shape=(*sems, *sems, *[pltpu.HBM(b.shape, b.dtype) for b in blocks],
                   *[pltpu.HBM(s, b.dtype) for s, b in zip(lands, blocks)], jax.ShapeDtypeStruct((8, LANE), F32)),
        in_specs=(*[HBM] * (2 * n), pl.BlockSpec(memory_space=pl.ANY)),
        out_specs=(*[SEM] * (2 * n), *[HBM] * (2 * n), pl.BlockSpec(memory_space=pltpu.VMEM)),
        input_output_aliases={i: 2 * n + i for i in range(2 * n)},
        compiler_params=pltpu.CompilerParams(has_side_effects=EFFECT),
    )(*[pltpu.with_memory_space_constraint(b, pltpu.HBM) for b in blocks],
      *[pltpu.with_memory_space_constraint(lax.empty(s, b.dtype), pltpu.HBM) for s, b in zip(lands, blocks)], after)
    return [(out[a], out[n + a], out[2 * n + a], out[3 * n + a]) for a in range(n)], out[-1]


def _gather_relay(states, after, name):
    n, first_out = len(states), 3 * len(states) + len(after)

    def body(*refs):
        land_refs, send_sems, recv_sems = refs[:n], refs[n:2 * n], refs[2 * n:3 * n]
        pass_send, pass_recv = refs[first_out + n:first_out + 2 * n], refs[first_out + 2 * n:first_out + 3 * n]
        k_in, came_from, pass_to, _ = _relay_places(*_place())
        for a in range(n):
            slot = land_refs[a].at[_index(came_from)]
            pltpu.make_async_remote_copy(src_ref=slot, dst_ref=slot, send_sem=send_sems[a].at[k_in],
                                         recv_sem=recv_sems[a].at[k_in], device_id=came_from,
                                         device_id_type=MESH).wait_recv()
            pltpu.make_async_remote_copy(src_ref=slot, dst_ref=slot, send_sem=pass_send[a].at[0],
                                         recv_sem=pass_recv[a].at[0], device_id=pass_to, device_id_type=MESH).start()
        refs[-1][...] = jnp.zeros_like(refs[-1])

    lands = [st[3] for st in states]
    pair = [pltpu.SemaphoreType.DMA((1,))] * n
    out = pl.pallas_call(
        body, name=name,
        out_shape=(*[pltpu.HBM(l.shape, l.dtype) for l in lands], *pair, *pair, jax.ShapeDtypeStruct((8, LANE), F32)),
        in_specs=(*[HBM] * n, *[SEM] * (2 * n), *[pl.BlockSpec(memory_space=pl.ANY)] * len(after)),
        out_specs=(*[HBM] * n, *[SEM] * (2 * n), pl.BlockSpec(memory_space=pltpu.VMEM)),
        input_output_aliases={a: a for a in range(n)},
        compiler_params=pltpu.CompilerParams(has_side_effects=EFFECT),
    )(*lands, *[st[0] for st in states], *[st[1] for st in states], *after)
    return [(st[0], st[1], st[2], out[a], (out[n + a], out[2 * n + a])) for a, st in enumerate(states)], out[-1]


def _gather_forward(send_sems, recv_sems, b_thru, land_thru, after, name, passed=None):
    relayed = passed is not None

    def body(b_ref, land_ref, send_sems, recv_sems, *rest):
        pass_send, pass_recv = rest[:2] if relayed else (None, None)
        send2, recv2, token = rest[-3:]
        x, y, c = _place()
        sibling = (x, y, 1 - c)
        chips = [(*chip, c) for chip in _chips(x, y)]
        arrivals = [((send_sems.at[j + 1], recv_sems.at[j + 1]), frm, j) for j, frm in enumerate(chips)]
        sends = [send_sems.at[k] for k in range(4)]
        if relayed:
            k_in, came_from, _, other = _relay_places(x, y, c)
            arrivals = [(None, came_from, k_in - 1), ((send_sems.at[3 - k_in], recv_sems.at[3 - k_in]), other, 2 - k_in),
                        ((pass_send.at[0], pass_recv.at[0]), chips[2], 2)]
            sends[3] = pass_send.at[0]
        for sems, frm, j in arrivals:
            slot = land_ref.at[_index(frm)]
            if sems:
                pltpu.make_async_remote_copy(src_ref=b_ref, dst_ref=slot, send_sem=sems[0], recv_sem=sems[1],
                                             device_id=frm, device_id_type=MESH).wait_recv()
            pltpu.make_async_remote_copy(src_ref=slot, dst_ref=slot, send_sem=send2.at[j], recv_sem=recv2.at[j],
                                         device_id=sibling, device_id_type=MESH).start()
        pltpu.make_async_remote_copy(src_ref=b_ref, dst_ref=land_ref.at[_index(sibling)], send_sem=send_sems.at[0],
                                     recv_sem=recv_sems.at[0], device_id=sibling, device_id_type=MESH).wait_recv()
        for sem in sends:
            pltpu.make_async_remote_copy(src_ref=b_ref, dst_ref=land_ref.at[0], send_sem=sem, recv_sem=recv_sems.at[0],
                                         device_id=sibling, device_id_type=MESH).wait_send()
        pltpu.make_async_copy(b_ref, land_ref.at[_index((x, y, c))], send_sems.at[OWN]).wait()
        token[...] = jnp.zeros_like(token)

    extra = list(passed) if relayed else []
    return pl.pallas_call(
        body, name=name,
        out_shape=(pltpu.HBM(b_thru.shape, b_thru.dtype), pltpu.HBM(land_thru.shape, land_thru.dtype),
                   pltpu.SemaphoreType.DMA((3,)), pltpu.SemaphoreType.DMA((3,)), jax.ShapeDtypeStruct((8, LANE), F32)),
        in_specs=(HBM, HBM, SEM, SEM, *[SEM] * len(extra), pl.BlockSpec(memory_space=pl.ANY)),
        out_specs=(HBM, HBM, SEM, SEM, pl.BlockSpec(memory_space=pltpu.VMEM)),
        input_output_aliases={0: 0, 1: 1},
        compiler_params=pltpu.CompilerParams(has_side_effects=EFFECT),
    )(b_thru, land_thru, send_sems, recv_sems, *extra, after)


def _gather_finish(land_thru, send2, recv2, after, name):
    def body(land_ref, send2, recv2, after_ref, land_out):
        x, y, c = _place()
        for j, chip in enumerate(_chips(x, y)):
            cp = pltpu.make_async_remote_copy(src_ref=land_ref.at[_index((*chip, c))],
                                              dst_ref=land_ref.at[_index((*chip, 1 - c))], send_sem=send2.at[j],
                                              recv_sem=recv2.at[j], device_id=(x, y, 1 - c), device_id_type=MESH)
            cp.wait_send()
            cp.wait_recv()

    return pl.pallas_call(
        body, name=name, out_shape=pltpu.HBM(land_thru.shape, land_thru.dtype),
        in_specs=(HBM, SEM, SEM, pl.BlockSpec(memory_space=pl.ANY)), out_specs=HBM,
        input_output_aliases={0: 0},
        compiler_params=pltpu.CompilerParams(has_side_effects=EFFECT),
    )(land_thru, send2, recv2, after)


class _Gathering:
    def __init__(self, ahead, later, me):
        cast = [a.astype(BF16) for a in ahead.values()]
        started, self.token = _gather_start(cast, cast[0], "gather1_ahead", relayed=True)
        self.me, self.state, self.relayed, self.later = me, dict(zip(ahead, started)), tuple(ahead), later

    def start_first(self, first, spare):
        started, self.token = _gather_start(list(first.values()), self.token, "gather1_first",
                                            spare=(list(first).index(spare),), relayed=True)
        self.state.update(zip(first, started))
        self.relayed += tuple(first)

    def begin(self, after):
        return self.token

    def relay(self, *after):
        states, token = _gather_relay([self.state[n] for n in self.relayed], after, "gather_relay")
        self.state.update(zip(self.relayed, states))
        cast = [_behind(a, token).astype(BF16) for a in self.later.values()]
        started, self.token = _gather_start(cast, token, "gather1_later")
        self.state.update(zip(self.later, started))
        return self.token

    def forward(self, name, after):
        first_leg, passed = self.state[name][:4], (self.state[name][4:] or (None,))[0]
        *self.state[name], token = _gather_forward(*first_leg, after, "gather2_" + name, passed=passed)
        return token

    def get(self, name, after):
        _, land, send2, recv2 = self.state[name]
        land = _gather_finish(land, send2, recv2, after, "gather3_" + name)
        return land if name not in ("w_out", "w_down") else land.reshape(-1, land.shape[2])


class _Reducing:
    def __init__(self, core, chip, gather_small):
        self.core, self.chip, self.state, self.token, self.gather_small = core, chip, {}, None, gather_small

    def meanwhile(self, small, loss, after):
        self.small_sum = self.gather_small(small, loss, after)
        return self.small_sum

    def start(self, name, grad):
        tail = name == "w_in"
        g = grad if tail or grad.ndim == 3 else grad.reshape(N_DEV, grad.shape[0] // N_DEV, grad.shape[1])
        *self.state[name], token = _exchange_start(g, _pair_route, tail, "pair_send_" + name)
        return token

    def relay(self, name, after):
        tail = name == "w_in"
        grad, pair = _exchange_wait(*self.state[name], after, _pair_route, tail, "pair_recv_" + name)
        total = _pair_add(grad, pair, self.core, tail, "pair_add_" + name)
        *self.state[name], self.token = _exchange_start(total, _chip_route, False, "chip_send_" + name,
                                                        own_slot=_chip_slot)
        return self.token

    def finish(self, name, after):
        _, land = _exchange_wait(*self.state[name], after, _chip_route, False, "chip_recv_" + name, own_slot=_chip_slot)
        return land


def _carry_w_in(main, tail):
    slabs, _, d = main.shape
    tc = _fit(d, 2048)
    assert slabs == N_DEV + 1 and tail.shape[:2] == (N_DEV, IN_TAIL), (main.shape, tail.shape)
    top = lambda off: pl.BlockSpec((None, IN_TAIL, tc), lambda s, j: (s + off, 0, j))

    def carry(m_ref, t_ref, o_ref):
        o_ref[...] = m_ref[...] + t_ref[...]

    main = pl.pallas_call(
        carry, name="carry_w_in", grid=(N_DEV - 1, d // tc), in_specs=[top(1), top(0)], out_specs=top(1),
        out_shape=jax.ShapeDtypeStruct(main.shape, main.dtype), input_output_aliases={0: 0},
        compiler_params=_params("parallel", "parallel"),
    )(main, tail)

    def last(m_ref, t_ref, o_ref):
        o_ref[...] = jnp.zeros_like(o_ref)
        o_ref[0:IN_TAIL, :] = t_ref[...]

    return pl.pallas_call(
        last, name="last_slab_w_in", grid=(d // tc,),
        in_specs=[pl.BlockSpec(memory_space=pl.ANY), pl.BlockSpec((None, IN_TAIL, tc), lambda j: (N_DEV - 1, 0, j))],
        out_specs=pl.BlockSpec((None, LANE, tc), lambda j: (N_DEV, 0, j)),
        out_shape=jax.ShapeDtypeStruct(main.shape, main.dtype), input_output_aliases={0: 0},
        compiler_params=_params("parallel"),
    )(main, tail)


def _rows(n, want):
    t = min(n, want)
    t -= t % 16
    while n % t:
        t -= 16
    return t


def _adam_math(w, g, m, v):
    m2 = ADAM_B1 * m + (1.0 - ADAM_B1) * g
    v2 = ADAM_B2 * v + (1.0 - ADAM_B2) * (g * g)
    m_hat = m2 * (1.0 / (1.0 - ADAM_B1 ** ADAM_STEP))
    v_hat = v2 * (1.0 / (1.0 - ADAM_B2 ** ADAM_STEP))
    return -ADAM_LR * (m_hat / (jnp.sqrt(v_hat) + ADAM_EPS) + ADAM_WD * w), m2, v2


def _slot_sum(r_ref):
    acc = r_ref[0].astype(F32)
    for i in range(1, r_ref.shape[0]):
        acc = acc + r_ref[i].astype(F32)
    return acc


def _shift_w_in(w):
    ws, d = w.shape
    tc = _fit(d, 256)

    def body(w_ref, main_ref, tail_ref, tall):
        tall[...] = jnp.zeros_like(tall)
        tall[0:ws, :] = w_ref[...]
        moved = pltpu.roll(tall[...], _index(_place()), 0).astype(BF16)
        main_ref[...] = moved[0:IN_SLAB]
        tail_ref[...] = moved[IN_SLAB:]

    return pl.pallas_call(
        body, name="shift_w_in", grid=(d // tc,),
        in_specs=[pl.BlockSpec((ws, tc), lambda j: (0, j))],
        out_specs=[pl.BlockSpec((IN_SLAB, tc), lambda j: (0, j)), pl.BlockSpec((IN_TAIL, tc), lambda j: (0, j))],
        out_shape=[jax.ShapeDtypeStruct((IN_SLAB, d), BF16), jax.ShapeDtypeStruct((IN_TAIL, d), BF16)],
        scratch_shapes=[pltpu.VMEM((IN_SLAB + IN_TAIL, tc), F32)], compiler_params=_params("parallel"),
    )(w)


def _sum_adamw_shifted(r, w, m, v, name):
    _, ph, d = r.shape
    ws = w.shape[0]
    tc = _fit(d, 256)

    def body(r_ref, w_ref, m_ref, v_ref, g_ref, d_ref, m2_ref, v2_ref, tall):
        tall[...] = pltpu.roll(_slot_sum(r_ref), lax.rem(ph - _index(_place()), ph), 0)
        g = tall[0:ws, :]
        g_ref[...] = g
        d_ref[...], m2_ref[...], v2_ref[...] = _adam_math(w_ref[...], g, m_ref[...], v_ref[...])

    blk = pl.BlockSpec((ws, tc), lambda j: (0, j))
    out = jax.ShapeDtypeStruct(w.shape, F32)
    return pl.pallas_call(
        body, name=name, grid=(d // tc,),
        in_specs=[pl.BlockSpec((r.shape[0], ph, tc), lambda j: (0, 0, j)), blk, blk, blk],
        out_specs=[blk] * 4, out_shape=[out] * 4,
        scratch_shapes=[pltpu.VMEM((ph, tc), F32)], compiler_params=_params("parallel"),
    )(r, w, m, v)


def _sum_slots(r, name, tr=128):
    _, rows, cols = r.shape
    tr = _rows(rows, tr)

    def body(r_ref, g_ref):
        g_ref[...] = _slot_sum(r_ref)

    return pl.pallas_call(
        body, name=name, grid=(rows // tr,),
        in_specs=[pl.BlockSpec((r.shape[0], tr, cols), lambda i: (0, i, 0))],
        out_specs=pl.BlockSpec((tr, cols), lambda i: (i, 0)),
        out_shape=jax.ShapeDtypeStruct((rows, cols), F32),
        compiler_params=_params("parallel"),
    )(r)


def _adamw(w, g, m, v, name, tr=256):
    rows, cols = w.shape
    tr = _rows(rows, tr)

    def body(w_ref, g_ref, m_ref, v_ref, d_ref, m2_ref, v2_ref):
        d_ref[...], m2_ref[...], v2_ref[...] = _adam_math(w_ref[...], g_ref[...], m_ref[...], v_ref[...])

    blk = pl.BlockSpec((tr, cols), lambda i: (i, 0))
    out = jax.ShapeDtypeStruct((rows, cols), F32)
    return pl.pallas_call(
        body, name=name, grid=(rows // tr,), in_specs=[blk] * 4, out_specs=[blk] * 3, out_shape=[out] * 3,
        compiler_params=_params("parallel"),
    )(w, g, m, v)


def _sum_adamw(r, w, m, v, name, tr=256):
    rows, cols = w.shape
    tr = _rows(rows, tr)

    def body(r_ref, w_ref, m_ref, v_ref, g_ref, d_ref, m2_ref, v2_ref):
        g = _slot_sum(r_ref)
        g_ref[...] = g
        d_ref[...], m2_ref[...], v2_ref[...] = _adam_math(w_ref[...], g, m_ref[...], v_ref[...])

    blk = pl.BlockSpec((tr, cols), lambda i: (i, 0))
    out = jax.ShapeDtypeStruct((rows, cols), F32)
    return pl.pallas_call(
        body, name=name, grid=(rows // tr,),
        in_specs=[pl.BlockSpec((r.shape[0], tr, cols), lambda i: (0, i, 0)), blk, blk, blk],
        out_specs=[blk] * 4, out_shape=[out] * 4,
        compiler_params=_params("parallel"),
    )(r, w, m, v)


def _pack(pieces, sizes):
    flat = [jnp.pad(p.reshape(-1).astype(F32), (0, s - p.size)) for p, s in zip(pieces, sizes)]
    total = sum(sizes)
    padded = -(-total // (16 * LANE)) * (16 * LANE)
    return jnp.pad(jnp.concatenate(flat), (0, padded - total)).reshape(-1, LANE)


def _unpack(packed, shapes, sizes):
    flat = packed.reshape(-1)
    out, off = [], 0
    for shp, s in zip(shapes, sizes):
        n = 1
        for k in shp:
            n *= k
        out.append(flat[off:off + n].reshape(shp))
        off += s
    return out


def _lanes(n):
    return -(-n // LANE) * LANE


WEIGHTS = ("w_in", "b_gates", "w_sc_conv", "mh_gain", "w_out", "ln1_g", "ln1_b", "w_up", "w_ffn_conv", "b_ffn_conv",
           "w_down", "ln2_g", "ln2_b")
BIG = ("w_in", "w_out", "w_up", "w_down")
SMALL = tuple(n for n in WEIGHTS if n not in BIG)


def kernel(x, w_in, b_gates, w_sc_conv, mh_gain, w_out, ln1_g, ln1_b, w_up, w_ffn_conv, b_ffn_conv, w_down, ln2_g, ln2_b, loss_target, m_w_in, m_b_gates, m_w_sc_conv, m_mh_gain, m_w_out, m_ln1_g, m_ln1_b, m_w_up, m_w_ffn_conv, m_b_ffn_conv, m_w_down, m_ln2_g, m_ln2_b, v_w_in, v_b_gates, v_w_sc_conv, v_mh_gain, v_w_out, v_ln1_g, v_ln1_b, v_w_up, v_w_ffn_conv, v_b_ffn_conv, v_w_down, v_ln2_g, v_ln2_b):
    w = dict(zip(WEIGHTS, (w_in, b_gates, w_sc_conv, mh_gain, w_out, ln1_g, ln1_b, w_up, w_ffn_conv, b_ffn_conv,
                           w_down, ln2_g, ln2_b)))
    m = dict(zip(WEIGHTS, (m_w_in, m_b_gates, m_w_sc_conv, m_mh_gain, m_w_out, m_ln1_g, m_ln1_b, m_w_up,
                           m_w_ffn_conv, m_b_ffn_conv, m_w_down, m_ln2_g, m_ln2_b)))
    v = dict(zip(WEIGHTS, (v_w_in, v_b_gates, v_w_sc_conv, v_mh_gain, v_w_out, v_ln1_g, v_ln1_b, v_w_up,
                           v_w_ffn_conv, v_b_ffn_conv, v_w_down, v_ln2_g, v_ln2_b)))
    me = _index(_place())
    d = x.shape[2]
    ws_in = w_in.shape[2]
    assert ws_in == IN_SLAB + 1 and N_DEV <= LANE, w_in.shape
    ninp = (N_DEV + 1) * IN_SLAB
    ws_sc, ws_fc = w_sc_conv.shape[2], w_ffn_conv.shape[2]

    wx = _Gathering({"w_out": w_out[0]}, {n: w[n][0] for n in ("w_up", "w_down")}, me)
    w_in_t = jnp.transpose(_behind(w_in[0], wx.token))
    w_in_main, w_in_tail = _shift_w_in(w_in_t)
    taps8 = lambda a: jnp.pad(a[0], ((0, 5), (0, 0)))
    at_once = ("w_sc", "w_fc", "w_tail", "w_in")
    wx.start_first(dict(zip(at_once, (taps8(w_sc_conv), taps8(w_ffn_conv), w_in_tail, w_in_main))), spare="w_in")
    x_b = _behind(x[0], wx.begin(None)).astype(BF16)
    m_in_t, v_in_t = (jnp.transpose(_behind(a[0], wx.begin(None))) for a in (m_w_in, v_w_in))
    token = wx.relay(x_b, m_in_t, v_in_t)
    for n in at_once:
        token = wx.forward(n, token)
    g_sc, g_fc, g_tail, g_in = (wx.get(n, token) for n in at_once)
    w_in_full = _carry_w_in(g_in, g_tail).reshape(ninp, d)
    w_sc_full = g_sc[:, :3].transpose(1, 0, 2).reshape(3, N_DEV * ws_sc)
    w_fc_full = g_fc[:, :3].transpose(1, 0, 2).reshape(3, N_DEV * ws_fc)

    xi, yi, ci = _place()
    names = ("loss",) + SMALL
    pieces = {}

    def gather_small(small, loss_t, after):
        pieces.update(small, loss=loss_t[0, :1])
        sizes = [_lanes(pieces[n].size) for n in names]
        (g_small,) = _all_gather([_behind(_pack([pieces[n] for n in names], sizes), after)], "gather_small")
        return _sum_slots(g_small, "sum_small", tr=g_small.shape[1])

    gx = _Reducing(jnp.reshape(ci, (1,)).astype(jnp.int32), 2 * xi + yi, gather_small)
    loss_t, grad_x, small, _ = _local_step(
        x[0], loss_target[0], w_in_full, b_gates, w_sc_full, mh_gain, None, ln1_g, ln1_b, None,
        w_fc_full, b_ffn_conv, None, ln2_g, ln2_b, gx=gx, wx=wx, x_b=x_b)

    grads, deltas, new_m, new_v = {}, {}, {}, {}
    for name in ("w_down", "w_up", "w_out"):
        grads[name], deltas[name], new_m[name], new_v[name] = _sum_adamw(
            gx.finish(name, gx.token), w[name][0], m[name][0], v[name][0], "adamw_" + name)

    summed = _unpack(gx.small_sum, [pieces[n].shape for n in names], [_lanes(pieces[n].size) for n in names])
    full = dict(zip(names, summed))
    full["w_sc_conv"] = lax.dynamic_slice(full["w_sc_conv"], (0, me * ws_sc), (3, ws_sc))
    full["w_ffn_conv"] = lax.dynamic_slice(full["w_ffn_conv"], (0, me * ws_fc), (3, ws_fc))
    for n in SMALL:
        grads[n] = full[n].reshape(w[n].shape)
    sizes = [_lanes(w[n].size) for n in SMALL]
    shapes = [w[n].shape for n in SMALL]
    packed = [_pack([t[n] for n in SMALL], sizes) for t in (w, grads, m, v)]
    small_out = _adamw(*packed, "adamw_small")
    for res, t in zip(small_out, (deltas, new_m, new_v)):
        t.update(zip(SMALL, _unpack(res, shapes, sizes)))

    done = sum(t[0:1, 0:1] for t in (grad_x, deltas["w_down"], deltas["w_up"], deltas["w_out"], small_out[0]))
    grads["w_in"], deltas["w_in"], new_m["w_in"], new_v["w_in"] = (
        jnp.transpose(a)[None] for a in _sum_adamw_shifted(gx.finish("w_in", done), w_in_t, m_in_t, v_in_t, "adamw_w_in"))

    big = lambda t: {n: (t[n].reshape(w[n].shape) if n in BIG else t[n]) for n in WEIGHTS}
    grads, deltas, new_m, new_v = big(grads), big(deltas), big(new_m), big(new_v)
    return (full["loss"].reshape(()), grad_x[None], *[grads[n] for n in WEIGHTS], *[deltas[n] for n in WEIGHTS],
            *[new_m[n] for n in WEIGHTS], *[new_v[n] for n in WEIGHTS])
```

```python
import functools

import jax
import jax.numpy as jnp
from jax import lax
from jax.experimental import pallas as pl
from jax.experimental.pallas import tpu as pltpu

F32 = jnp.float32
BF16 = jnp.bfloat16
MESH = pl.DeviceIdType.MESH

N_DEV = 8
NH = 4
CHUNK = 64
LN_EPS = 1e-5
HN_EPS = 1e-6
ALPHA = 2.0 ** 0.25
LANE = 128
IN_SLAB = 7 * LANE
IN_TAIL = 16
VMEM_LIMIT = 56 * 1024 * 1024
ADAM_LR, ADAM_B1, ADAM_B2, ADAM_EPS, ADAM_WD, ADAM_STEP = 0.001, 0.9, 0.999, 1e-08, 0.01, 10

_NN = (((1,), (0,)), ((), ()))
_NT = (((1,), (1,)), ((), ()))
_TN = (((0,), (0,)), ((), ()))


def _dot(a, b, dn=_NN):
    return lax.dot_general(a, b, dn, preferred_element_type=F32)


def _params(*sem):
    return pltpu.CompilerParams(dimension_semantics=sem if sem else None, vmem_limit_bytes=VMEM_LIMIT)


def _iota(shape, axis):
    return lax.broadcasted_iota(jnp.int32, shape, axis)


def _fit(n, want):
    if n <= want:
        return n
    t = want - want % LANE
    while n % t:
        t -= LANE
    return t


def _placed(after, body, in_specs, args):
    if after is None:
        return body, in_specs, args
    return (lambda after_ref, *refs: body(*refs)), [pl.BlockSpec(memory_space=pl.ANY)] + in_specs, (after,) + args


def _matmul(a, b, mode, out_dtype, name, tm=1024, tn=512, tk=1024, add=None, add_scale=1.0,
            a_blocked=False, b_blocked=False, o_width=None, after=None, n=None):
    a_parts = a if isinstance(a, tuple) else None
    b_parts = b if isinstance(b, tuple) else None
    if a_parts:
        a_blocked, (a_rows, wa), na = True, a[0].shape, len(a)
        kd, m = (a_rows, na * wa) if mode == "tn" else (na * wa, a_rows)
    elif a_blocked:
        na, a_rows, wa = a.shape
        kd, m = (a_rows, na * wa) if mode == "tn" else (na * wa, a_rows)
    elif mode == "tn":
        kd, m = a.shape
    else:
        m, kd = a.shape
    if b_parts:
        b_blocked, (rows, w), nb = True, b[0].shape, len(b)
    elif b_blocked:
        nb, rows, w = b.shape
    if b_blocked:
        n = rows if mode == "nt" else nb * w
        assert (nb * w if mode == "nt" else rows) == kd, (name, kd)
    else:
        n = n or (b.shape[0] if mode == "nt" else b.shape[1])
    tm, tn, tk = _fit(m, tm), _fit(n, tn), _fit(kd, tk)
    if a_blocked and mode == "tn":
        tm = _fit(wa, tm)
    if a_blocked and mode != "tn":
        tk = _fit(wa, tk)
    if b_blocked and mode != "nt":
        tn = _fit(w, tn)
    if b_blocked and mode == "nt":
        tk = _fit(w, tk)
    if o_width is not None:
        tn = _fit(o_width, tn)
    assert m % tm == 0 and n % tn == 0 and kd % tk == 0, (name, m, n, kd, tm, tn, tk)
    assert not (a_blocked and mode != "tn" and wa % tk) and not (b_blocked and mode == "nt" and w % tk), (name, tk)
    nk = kd // tk
    dn = {"nn": _NN, "nt": _NT, "tn": _TN}[mode]
    if a_blocked and mode == "tn":
        a_per = wa // tm
        a_spec = pl.BlockSpec((None, tk, tm), lambda i, j, k: (i // a_per, k, i % a_per))
    elif a_blocked:
        a_per = wa // tk
        a_spec = pl.BlockSpec((None, tm, tk), lambda i, j, k: (k // a_per, i, k % a_per))
    elif mode == "tn":
        a_spec = pl.BlockSpec((tk, tm), lambda i, j, k: (k, i))
    else:
        a_spec = pl.BlockSpec((tm, tk), lambda i, j, k: (i, k))
    if b_blocked and mode != "nt":
        per = w // tn
        b_spec = pl.BlockSpec((None, tk, tn), lambda i, j, k: (j // per, k, j % per))
    elif b_blocked:
        per = w // tk
        b_spec = pl.BlockSpec((None, tn, tk), lambda i, j, k: (k // per, j, k % per))
    elif mode == "nt":
        b_spec = pl.BlockSpec((tn, tk), lambda i, j, k: (j, k))
    else:
        b_spec = pl.BlockSpec((tk, tn), lambda i, j, k: (k, j))
    if o_width is None:
        o_spec = pl.BlockSpec((tm, tn), lambda i, j, k: (i, j))
        o_shape = (m, n)
    else:
        oper = o_width // tn
        o_spec = pl.BlockSpec((None, tm, tn), lambda i, j, k: (j // oper, i, j % oper))
        o_shape = (n // o_width, m, o_width)
    a_list, a_specs = [a], [a_spec]
    if a_parts:
        hold = lambda x, s: jnp.clip(x - s * a_per, 0, a_per - 1)
        a_list = list(a_parts)
        a_specs = [(pl.BlockSpec((tk, tm), lambda i, j, k, s=s: (k, hold(i, s))) if mode == "tn"
                    else pl.BlockSpec((tm, tk), lambda i, j, k, s=s: (i, hold(k, s)))) for s in range(na)]
    b_list, b_specs = [b], [b_spec]
    if b_parts:
        hold_b = lambda x, s: jnp.clip(x - s * per, 0, per - 1)
        b_list = list(b_parts)
        b_specs = [(pl.BlockSpec((tn, tk), lambda i, j, k, s=s: (j, hold_b(k, s))) if mode == "nt"
                    else pl.BlockSpec((tk, tn), lambda i, j, k, s=s: (k, hold_b(j, s)))) for s in range(nb)]
    n_a, n_b = len(a_list), len(b_list)
    has_add = add is not None
    n_in = n_a + n_b + has_add + (after is not None)
    in_place = nk > 1 and out_dtype == F32

    def body(*refs):
        add_ref = refs[n_a + n_b] if has_add else None
        o_ref = refs[n_in]
        i, j, k = pl.program_id(0), pl.program_id(1), pl.program_id(2)

        def finish(r):
            if has_add:
                r = r + add_scale * add_ref[...]
            o_ref[...] = r.astype(out_dtype)

        def step(a_ref, b_ref):
            if nk == 1:
                finish(_dot(a_ref[...], b_ref[...], dn))
                return
            acc = o_ref if in_place else refs[-1]

            @pl.when(k == 0)
            def _():
                acc[...] = _dot(a_ref[...], b_ref[...], dn)

            @pl.when(k > 0)
            def _():
                acc[...] += _dot(a_ref[...], b_ref[...], dn)

        if n_a == 1 and n_b == 1:
            step(refs[0], refs[1])
        else:
            slab_a = ((i if mode == "tn" else k) // a_per) if n_a > 1 else 0
            slab_b = ((k if mode == "nt" else j) // per) if n_b > 1 else 0
            for sa in range(n_a):
                for sb in range(n_b):
                    pl.when((slab_a == sa) & (slab_b == sb))(functools.partial(step, refs[sa], refs[n_a + sb]))
        if nk > 1 and not (in_place and not has_add):
            @pl.when(k == nk - 1)
            def _():
                finish((o_ref if in_place else refs[-1])[...])

    in_specs = a_specs + b_specs + ([pl.BlockSpec((tm, tn), lambda i, j, k: (i, j))] if has_add else [])
    args = (*a_list, *b_list) + ((add,) if has_add else ())
    if after is not None:
        in_specs.append(pl.BlockSpec(memory_space=pl.ANY))
        args += (after,)
    return pl.pallas_call(
        body, name=name, grid=(m // tm, n // tn, nk),
        in_specs=in_specs, out_specs=o_spec,
        out_shape=jax.ShapeDtypeStruct(o_shape, out_dtype),
        scratch_shapes=[pltpu.VMEM((tm, tn), F32)] if nk > 1 and not in_place else [],
        compiler_params=_params("parallel", "parallel", "arbitrary"),
    )(*args)


def _shift_down(u, s):
    return jnp.where(_iota(u.shape, 0) >= s, pltpu.roll(u, s, 0), 0.0)


def _shift_up(u, s):
    t = u.shape[0]
    return jnp.where(_iota(u.shape, 0) < t - s, pltpu.roll(u, t - s, 0), 0.0)


SLAB = 8


def _rolled(u):
    return pltpu.roll(u, 2, 0), pltpu.roll(u, 1, 0)


def _conv(u, w, rolled=None):
    u2, u1 = _rolled(u) if rolled is None else rolled
    raw = w[0:1] * u2 + w[1:2] * u1 + w[2:3] * u
    head = u[0:SLAB]
    mended = w[0:1] * _shift_down(head, 2) + w[1:2] * _shift_down(head, 1) + w[2:3] * head
    return jnp.concatenate([mended, raw[SLAB:]], axis=0)


def _conv_t(dy, w):
    t = dy.shape[0]
    raw = w[2:3] * dy + w[1:2] * pltpu.roll(dy, t - 1, 0) + w[0:1] * pltpu.roll(dy, t - 2, 0)
    tail = dy[t - SLAB:]
    mended = w[2:3] * tail + w[1:2] * _shift_up(tail, 1) + w[0:1] * _shift_up(tail, 2)
    return jnp.concatenate([raw[:t - SLAB], mended], axis=0)


def _conv_dw(dy, u, rolled=None):
    t = dy.shape[0]
    u2, u1 = _rolled(u) if rolled is None else rolled
    head, tail = dy[0:SLAB], u[t - SLAB:]
    r = _iota(head.shape, 0)
    wrap2 = jnp.sum(jnp.where(r < 2, head * pltpu.roll(tail, 2, 0), 0.0), axis=0, keepdims=True)
    wrap1 = jnp.sum(jnp.where(r < 1, head * pltpu.roll(tail, 1, 0), 0.0), axis=0, keepdims=True)
    d0 = jnp.sum(dy * u2, axis=0, keepdims=True) - wrap2
    d1 = jnp.sum(dy * u1, axis=0, keepdims=True) - wrap1
    d2 = jnp.sum(dy * u, axis=0, keepdims=True)
    r3 = _iota((3, dy.shape[1]), 0)
    return jnp.where(r3 == 0, d0, jnp.where(r3 == 1, d1, d2))


def _sigmoid(x):
    return 0.5 * jnp.tanh(0.5 * x) + 0.5


def _sconv_fwd(proj, w_sc, t, wc):
    nb = wc // LANE

    def body(cb_ref, cc_ref, ch_ref, w_ref, y_ref):
        u = cc_ref[...] * ch_ref[...]
        y_ref[...] = (cb_ref[...] * _conv(u, w_ref[...])).astype(BF16)

    col = lambda off: pl.BlockSpec((t, LANE), lambda j: (0, j + off))
    return pl.pallas_call(
        body, name="sconv_fwd", grid=(nb,),
        in_specs=[col(0), col(nb), col(2 * nb), pl.BlockSpec((3, LANE), lambda j: (0, j))],
        out_specs=pl.BlockSpec((None, t, LANE), lambda j: (0, 0, j)),
        out_shape=jax.ShapeDtypeStruct((2, t, wc), BF16),
        compiler_params=_params("parallel"),
    )(proj, proj, proj, w_sc)


def _sconv_bwd(dy, proj, w_sc, d_proj, t, wc, after=None):
    nb = wc // LANE
    assert nb >= 2, nb

    def body(dy_ref, cb_ref, cc_ref, ch_ref, w_ref, d_proj_in, d_proj_ref, dw_ref, out_s, sems):
        j = pl.program_id(0)
        slot = j % 2

        def copies(step, slot):
            cols = lambda part: pl.ds(pl.multiple_of((step + part * nb) * LANE, LANE), LANE)
            return [pltpu.make_async_copy(out_s.at[slot, part], d_proj_ref.at[:, cols(part)], sems.at[slot, part])
                    for part in range(3)]

        @pl.when(j >= 2)
        def _():
            for cp in copies(j - 2, slot):
                cp.wait()

        cc, ch, w, d = cc_ref[...], ch_ref[...], w_ref[...], dy_ref[...]
        u = cc * ch
        ru = _rolled(u)
        out_s[slot, 0] = (d * _conv(u, w, ru)).astype(BF16)
        dcu = d * cb_ref[...]
        dw_ref[...] = _conv_dw(dcu, u, ru)
        du = _conv_t(dcu, w)
        out_s[slot, 1] = (du * ch).astype(BF16)
        out_s[slot, 2] = (du * cc).astype(BF16)
        for cp in copies(j, slot):
            cp.start()

        @pl.when(j == nb - 1)
        def _():
            for cp in copies(j - 1, 1 - slot) + copies(j, slot):
                cp.wait()

    col = lambda off: pl.BlockSpec((t, LANE), lambda j: (0, j + off))
    body, in_specs, args = _placed(
        after, body, [col(0), col(0), col(nb), col(2 * nb), pl.BlockSpec((3, LANE), lambda j: (0, j)),
                      pl.BlockSpec(memory_space=pl.ANY)],
        (dy, proj, proj, proj, w_sc, d_proj))
    return pl.pallas_call(
        body, name="sconv_bwd", grid=(nb,),
        in_specs=in_specs,
        out_specs=[pl.BlockSpec(memory_space=pl.ANY), pl.BlockSpec((3, LANE), lambda j: (0, j))],
        out_shape=[jax.ShapeDtypeStruct(d_proj.shape, d_proj.dtype), jax.ShapeDtypeStruct((3, wc), F32)],
        input_output_aliases={len(args) - 1: 0},
        scratch_shapes=[pltpu.VMEM((2, 3, t, LANE), BF16), pltpu.SemaphoreType.DMA((2, 3))],
        compiler_params=_params("arbitrary"),
    )(*args)


def _gates_prep(proj, bias_tile, t, gate_tile):
    def body(g_ref, b_ref, o_ref):
        g = g_ref[...] + b_ref[...]
        lane = _iota(g.shape, 1)
        is_f = (lane >= NH) & (lane < 2 * NH)
        lf = jnp.minimum(g, 0.0) - jnp.log(1.0 + jnp.exp(-jnp.abs(g)))
        c = jnp.where(is_f, lf, 0.0)
        r = _iota(g.shape, 0) % CHUNK
        s = 1
        while s < CHUNK:
            c = c + jnp.where(r >= s, pltpu.roll(c, s, 0), 0.0)
            s *= 2
        o_ref[...] = jnp.where(is_f, c, jnp.where(lane < NH, g, 0.0))

    return pl.pallas_call(
        body, name="gates_prep", grid=(1,),
        in_specs=[pl.BlockSpec((t, LANE), lambda i: (0, gate_tile)), pl.BlockSpec((1, LANE), lambda i: (0, 0))],
        out_specs=pl.BlockSpec((t, LANE), lambda i: (0, 0)),
        out_shape=jax.ShapeDtypeStruct((t, LANE), F32),
        compiler_params=_params("arbitrary"),
    )(proj, bias_tile)


def _gates_bwd(dgate, proj, bias_tile, d_proj, t, gate_tile):
    def body(dg_ref, g_ref, b_ref, d_proj_in, o_ref, s_ref):
        g = g_ref[...] + b_ref[...]
        lane = _iota(g.shape, 1)
        r = _iota(g.shape, 0) % CHUNK
        dsig = 1.0 - _sigmoid(g)
        out = jnp.zeros(g.shape, F32)
        for h in range(NH):
            d = dg_ref[h]
            c = d
            s = 1
            while s < CHUNK:
                c = c + jnp.where(r + s < CHUNK, pltpu.roll(c, t - s, 0), 0.0)
                s *= 2
            di = jnp.broadcast_to(d[:, 0:1], g.shape)
            db = jnp.broadcast_to(c[:, 1:2], g.shape)
            out = out + jnp.where(lane == h, di, 0.0) + jnp.where(lane == NH + h, db * dsig, 0.0)
        o_ref[...] = out.astype(BF16)
        s_ref[...] = jnp.sum(out, axis=0, keepdims=True)

    return pl.pallas_call(
        body, name="gates_bwd", grid=(1,),
        in_specs=[pl.BlockSpec((NH, t, LANE), lambda i: (0, 0, 0)),
                  pl.BlockSpec((t, LANE), lambda i: (0, gate_tile)), pl.BlockSpec((1, LANE), lambda i: (0, 0)),
                  pl.BlockSpec(memory_space=pl.ANY)],
        out_specs=[pl.BlockSpec((t, LANE), lambda i: (0, gate_tile)), pl.BlockSpec((1, LANE), lambda i: (0, 0))],
        out_shape=[jax.ShapeDtypeStruct(d_proj.shape, d_proj.dtype), jax.ShapeDtypeStruct((1, LANE), F32)],
        input_output_aliases={3: 0},
        compiler_params=_params("arbitrary"),
    )(dgate, proj, bias_tile, d_proj)


def _in_turn(heads):
    while heads:
        heads = [g for g in heads if next(g, heads) is not heads]


def _chunk_gates(gc, gr, h, mprev):
    L = CHUNK
    icol, bcol = gc[:, h:h + 1], gc[:, h + NH:h + NH + 1]
    irow, brow = gr[h:h + 1, :], gr[h + NH:h + NH + 1, :]
    tri = _iota((L, L), 0) >= _iota((L, L), 1)
    log_d = jnp.where(tri, bcol - brow + irow, -jnp.inf)
    inter = bcol + mprev
    mt = jnp.maximum(inter, jnp.max(log_d, axis=1, keepdims=True))
    dw = jnp.exp(log_d - mt)
    iw = jnp.exp(inter - mt)
    g = brow[:, L - 1:L]
    wlog_col = g - bcol + icol
    wlog_row = g - brow + irow
    mnew = jnp.maximum(g + mprev, jnp.max(wlog_row, axis=1, keepdims=True))
    wcol = jnp.exp(wlog_col - mnew)
    decay = jnp.exp(g + mprev - mnew)
    return dw, iw, mt, wcol, decay, mnew


def _mlstm_fwd(proj, gcol, grow, t, wc, dh):
    nc = t // CHUNK
    wm = NH * dh
    assert wc == wm, (wc, wm)
    qoff = 3 * wc // wm
    scale = dh ** -0.5

    def body(q_ref, k_ref, v_ref, gc_ref, gr_ref, h_ref, cs_ref, ns_ref, c_s, n_s, m_s):
        @pl.when(pl.program_id(0) == 0)
        def _():
            c_s[...] = jnp.zeros_like(c_s)
            n_s[...] = jnp.zeros_like(n_s)
            m_s[...] = jnp.zeros_like(m_s)

        gc, gr = gc_ref[...], gr_ref[0]
        done = [None] * NH

        def head(h):
            cols = slice(h * dh, (h + 1) * dh)
            mprev = m_s[h, 0:1, 0:1]
            cprev = c_s[h]
            n8 = n_s[h]
            nprev = n8[0:1]
            qs = q_ref[:, cols] * scale
            k = k_ref[:, cols]
            qs_b, k_b, v_b = qs.astype(BF16), k.astype(BF16), v_ref[:, cols].astype(BF16)
            qk = _dot(qs_b, k_b, _NT)
            yield
            q_c = _dot(qs_b, cprev.astype(BF16))
            yield
            dw, iw, mt, wcol, decay, mnew = _chunk_gates(gc, gr, h, mprev)
            yield
            s = qk * dw
            wk = wcol * k
            num = _dot(s.astype(BF16), v_b) + iw * q_c
            yield
            c_new = decay * cprev + _dot(wk.astype(BF16), v_b, _TN)
            yield
            den = jnp.sum(s, axis=1, keepdims=True) + iw * jnp.sum(qs * nprev, axis=1, keepdims=True)
            done[h] = (cprev, jnp.where(_iota(n8.shape, 0) == 1, mprev, n8),
                       num / jnp.maximum(jnp.abs(den), jnp.exp(-mt)), c_new,
                       decay * n8 + jnp.sum(wk, axis=0, keepdims=True), mnew)

        _in_turn([head(h) for h in range(NH)])
        for h, (c_old, n_old, h_out, c_new, n_new, m_new) in enumerate(done):
            cs_ref[h] = c_old
            ns_ref[h] = n_old
            h_ref[:, h * dh:(h + 1) * dh] = h_out
            c_s[h] = c_new
            n_s[h] = n_new
            m_s[h] = jnp.broadcast_to(m_new, m_s.shape[1:])

    grp = lambda off: pl.BlockSpec((CHUNK, wm), lambda c: (c, qoff + off))
    return pl.pallas_call(
        body, name="mlstm_fwd", grid=(nc,),
        in_specs=[grp(0), grp(1), grp(2),
                  pl.BlockSpec((CHUNK, LANE), lambda c: (c, 0)),
                  pl.BlockSpec((1, 8, CHUNK), lambda c: (c, 0, 0))],
        out_specs=[pl.BlockSpec((CHUNK, wm), lambda c: (c, 0)),
                   pl.BlockSpec((NH, None, dh, dh), lambda c: (0, c, 0, 0)),
                   pl.BlockSpec((NH, None, 8, dh), lambda c: (0, c, 0, 0))],
        out_shape=[jax.ShapeDtypeStruct((t, wm), F32),
                   jax.ShapeDtypeStruct((NH, nc, dh, dh), F32),
                   jax.ShapeDtypeStruct((NH, nc, 8, dh), F32)],
        scratch_shapes=[pltpu.VMEM((NH, dh, dh), F32), pltpu.VMEM((NH, 8, dh), F32), pltpu.VMEM((NH, 8, LANE), F32)],
        compiler_params=_params("arbitrary"),
    )(proj, proj, proj, gcol, grow)


def _mlstm_bwd(proj, gcol, grow, hval, dh_in, cs, ns, d_proj, t, wc, dh):
    nc = t // CHUNK
    wm = NH * dh
    assert wc == wm, (wc, wm)
    qoff = 3 * wc // wm
    scale = dh ** -0.5
    L = CHUNK

    def body(q_ref, k_ref, v_ref, gc_ref, gr_ref, h_ref, dh_ref, cs_ref, ns_ref, d_proj_in,
             dqkv_ref, dg_ref, dc_s, dn_s):
        @pl.when(pl.program_id(0) == 0)
        def _():
            dc_s[...] = jnp.zeros_like(dc_s)
            dn_s[...] = jnp.zeros_like(dn_s)

        gc, gr = gc_ref[...], gr_ref[0]
        eye = _iota((L, L), 0) == _iota((L, L), 1)
        lane = _iota((L, LANE), 1)
        last = _iota((L, 1), 0) == L - 1
        done = [None] * NH

        def head(h):
            cols = slice(h * dh, (h + 1) * dh)
            ns8 = ns_ref[h]
            nprev = ns8[0:1]
            mprev = ns8[1:2, 0:1]
            cprev = cs_ref[h]
            dcn = dc_s[h]
            dn8 = dn_s[h]
            dnn = dn8[0:1]

            qs = q_ref[:, cols] * scale
            k = k_ref[:, cols]
            qs_b, k_b, v_b = qs.astype(BF16), k.astype(BF16), v_ref[:, cols].astype(BF16)
            qk = _dot(qs_b, k_b, _NT)
            yield
            dw, iw, mt, wcol, decay, _ = _chunk_gates(gc, gr, h, mprev)
            yield
            s = qk * dw
            den = jnp.sum(s, axis=1, keepdims=True) + iw * jnp.sum(qs * nprev, axis=1, keepdims=True)
            emt = jnp.exp(-mt)
            r = 1.0 / jnp.maximum(jnp.abs(den), emt)
            dout = dh_ref[:, cols]
            dnum = dout * r
            dden = (-jnp.sum(dout * h_ref[:, cols], axis=1, keepdims=True) * r
                    * jnp.where(jnp.abs(den) > emt, jnp.sign(den), 0.0))
            dnum_b = dnum.astype(BF16)
            cprev_b = cprev.astype(BF16)
            dcn_b = dcn.astype(BF16)
            yield

            g_raw = _dot(dnum_b, v_b, _NT)
            yield
            q_inter = _dot(dnum_b, cprev_b, _NT)
            yield
            k_raw = _dot(v_b, dcn_b, _NT)
            yield
            gd = (g_raw + dden) * dw
            gd_b = gd.astype(BF16)
            dqs_inter = iw * (q_inter + dden * nprev)
            dk_inter = wcol * (k_raw + dnn)
            wk = wcol * k
            iq = iw * qs
            dqs = _dot(gd_b, k_b) + dqs_inter
            yield
            dk = _dot(gd_b, qs_b, _TN) + dk_inter
            yield
            dv = _dot(s.astype(BF16), dnum_b, _TN) + _dot(wk.astype(BF16), dcn_b)
            yield
            dc_new = decay * dcn + _dot(iq.astype(BF16), dnum_b, _TN)
            yield

            e = gd * qk
            e_cols = jnp.sum(jnp.where(eye, jnp.sum(e, axis=0, keepdims=True), 0.0), axis=1, keepdims=True)
            yield
            k_inter = jnp.sum(k * dk_inter, axis=1, keepdims=True)
            rq = jnp.sum(e, axis=1, keepdims=True) + jnp.sum(qs * dqs_inter, axis=1, keepdims=True)
            rk = e_cols + k_inter
            hsum = jnp.sum(k_inter, axis=0, keepdims=True)
            jdec = decay * (jnp.sum(jnp.sum(dcn * cprev, axis=1, keepdims=True), axis=0, keepdims=True)
                            + jnp.sum(dnn * nprev, axis=1, keepdims=True))
            db = rq - rk + jnp.where(last, hsum + jdec, 0.0)
            done[h] = (jnp.where(lane == 0, rk, jnp.where(lane == 1, db, 0.0)),
                       (dqs * scale).astype(BF16), dk.astype(BF16), dv.astype(BF16), dc_new,
                       decay * dn8 + jnp.sum(iq * dden, axis=0, keepdims=True))

        _in_turn([head(h) for h in range(NH)])
        for h, (dgate, dq, dk, dv, dc_new, dn_new) in enumerate(done):
            dg_ref[h] = dgate
            for part, grad in enumerate((dq, dk, dv)):
                dqkv_ref[:, part * wm + h * dh:part * wm + (h + 1) * dh] = grad
            dc_s[h] = dc_new
            dn_s[h] = dn_new

    rc = lambda c: nc - 1 - c
    grp = lambda off: pl.BlockSpec((L, wm), lambda c: (rc(c), qoff + off))
    hm = pl.BlockSpec((L, wm), lambda c: (rc(c), 0))
    assert qoff % 3 == 0, qoff
    return pl.pallas_call(
        body, name="mlstm_bwd", grid=(nc,),
        in_specs=[grp(0), grp(1), grp(2),
                  pl.BlockSpec((L, LANE), lambda c: (rc(c), 0)),
                  pl.BlockSpec((1, 8, L), lambda c: (rc(c), 0, 0)),
                  hm, hm,
                  pl.BlockSpec((NH, None, dh, dh), lambda c: (0, rc(c), 0, 0)),
                  pl.BlockSpec((NH, None, 8, dh), lambda c: (0, rc(c), 0, 0)),
                  pl.BlockSpec(memory_space=pl.ANY)],
        out_specs=[pl.BlockSpec((L, 3 * wm), lambda c: (rc(c), qoff // 3)),
                   pl.BlockSpec((NH, L, LANE), lambda c: (0, rc(c), 0))],
        out_shape=[jax.ShapeDtypeStruct(d_proj.shape, d_proj.dtype), jax.ShapeDtypeStruct((NH, t, LANE), F32)],
        input_output_aliases={9: 0},
        scratch_shapes=[pltpu.VMEM((NH, dh, dh), F32), pltpu.VMEM((NH, 8, dh), F32)],
        compiler_params=_params("arbitrary"),
    )(proj, proj, proj, gcol, grow, hval, dh_in, cs, ns, d_proj)


def _head_norm(hv):
    mu = jnp.mean(hv, axis=1, keepdims=True)
    hc = hv - mu
    rstd = lax.rsqrt(jnp.mean(hc * hc, axis=1, keepdims=True) + HN_EPS)
    return hc * rstd, rstd


def _hnorm_fwd(hval, proj, gain, y, t, wc, dh, tr=512):
    ooff = 3 * wc // dh + 3 * NH
    tr = min(tr, t)

    def body(h_ref, o_ref, g_ref, y_in, y_ref):
        hhat, _ = _head_norm(h_ref[...])
        y_ref[...] = (_sigmoid(o_ref[...]) * hhat * g_ref[...]).astype(BF16)

    return pl.pallas_call(
        body, name="hnorm_fwd", grid=(t // tr, NH),
        in_specs=[pl.BlockSpec((tr, dh), lambda i, h: (i, h)),
                  pl.BlockSpec((tr, dh), lambda i, h: (i, ooff + h)),
                  pl.BlockSpec((1, dh), lambda i, h: (0, h)),
                  pl.BlockSpec(memory_space=pl.ANY)],
        out_specs=pl.BlockSpec((None, tr, dh), lambda i, h: (1, i, h)),
        out_shape=jax.ShapeDtypeStruct(y.shape, BF16),
        input_output_aliases={3: 0},
        compiler_params=_params("parallel", "parallel"),
    )(hval, proj, gain, y)


def _hnorm_bwd(dy, hval, proj, gain, t, wc, dh, tr=512):
    ooff = 3 * wc // dh + 3 * NH
    tr = min(tr, t)
    yoff = wc // dh

    def body(dy_ref, h_ref, o_ref, g_ref, do_ref, dh_ref, dg_ref):
        i = pl.program_id(1)
        hhat, rstd = _head_norm(h_ref[...])
        gain_v = g_ref[...]
        sig = _sigmoid(o_ref[...])
        d = dy_ref[...]
        do_ref[...] = (d * hhat * gain_v * sig * (1.0 - sig)).astype(BF16)
        dhn = d * sig
        part = jnp.sum(dhn * hhat, axis=0, keepdims=True)

        @pl.when(i == 0)
        def _():
            dg_ref[...] = part

        @pl.when(i > 0)
        def _():
            dg_ref[...] += part

        dhat = dhn * gain_v
        dh_ref[...] = rstd * (dhat - jnp.mean(dhat, axis=1, keepdims=True)
                              - hhat * jnp.mean(dhat * hhat, axis=1, keepdims=True))

    blk = lambda off: pl.BlockSpec((tr, dh), lambda h, i: (i, off + h))
    return pl.pallas_call(
        body, name="hnorm_bwd", grid=(NH, t // tr),
        in_specs=[blk(yoff), blk(0), blk(ooff), pl.BlockSpec((1, dh), lambda h, i: (0, h))],
        out_specs=[blk(ooff), blk(0), pl.BlockSpec((1, dh), lambda h, i: (0, h))],
        out_shape=[jax.ShapeDtypeStruct(proj.shape, BF16), jax.ShapeDtypeStruct((t, NH * dh), F32),
                   jax.ShapeDtypeStruct((1, NH * dh), F32)],
        compiler_params=_params("parallel", "arbitrary"),
    )(dy, hval, proj, gain)


def _ln_stats(z):
    mu = jnp.mean(z, axis=1, keepdims=True)
    zc = z - mu
    rstd = lax.rsqrt(jnp.mean(zc * zc, axis=1, keepdims=True) + LN_EPS)
    return zc * rstd, rstd


def _ln_bwd(dy, xhat, rstd, g):
    dxh = dy * g
    return rstd * (dxh - jnp.mean(dxh, axis=1, keepdims=True) - xhat * jnp.mean(dxh * xhat, axis=1, keepdims=True))


def _accum(ref, i, part):
    @pl.when(i == 0)
    def _():
        ref[...] = part

    @pl.when(i > 0)
    def _():
        ref[...] += part


def _ln1_fwd(x, mix, g, b, tr=256, after=None):
    t, d = x.shape

    def body(x_ref, m_ref, g_ref, b_ref, xh_ref, rs_ref, xb_ref):
        xhat, rstd = _ln_stats(ALPHA * x_ref[...] + m_ref[...])
        xh_ref[...] = xhat
        rs_ref[...] = rstd
        xb_ref[...] = (xhat * g_ref[...] + b_ref[...]).astype(BF16)

    row = pl.BlockSpec((tr, d), lambda i: (i, 0))
    vec = pl.BlockSpec((1, d), lambda i: (0, 0))
    body, in_specs, args = _placed(after, body, [row, row, vec, vec], (x, mix, g, b))
    return pl.pallas_call(
        body, name="ln1_fwd", grid=(t // tr,),
        in_specs=in_specs,
        out_specs=[row, pl.BlockSpec((tr, 1), lambda i: (i, 0)), row],
        out_shape=[jax.ShapeDtypeStruct((t, d), F32), jax.ShapeDtypeStruct((t, 1), F32),
                   jax.ShapeDtypeStruct((t, d), BF16)],
        compiler_params=_params("parallel"),
    )(*args)


def _ln2_loss(xhat1, g1, b1, ff, target, g2, b2, tr=256):
    t, d = ff.shape

    def body(xh_ref, g1_ref, b1_ref, f_ref, t_ref, g_ref, b_ref, dz_ref, dzb_ref, dg_ref, db_ref, l_ref):
        i = pl.program_id(0)
        x1 = xh_ref[...] * g1_ref[...] + b1_ref[...]
        xhat, rstd = _ln_stats(ALPHA * x1 + f_ref[...])
        gv = g_ref[...]
        e = xhat * gv + b_ref[...] - t_ref[...]
        lsum = jnp.sum(jnp.sum(e * e, axis=1, keepdims=True), axis=0, keepdims=True) * (0.5 / d)
        dy = e * (1.0 / d)
        _accum(dg_ref, i, jnp.sum(dy * xhat, axis=0, keepdims=True))
        _accum(db_ref, i, jnp.sum(dy, axis=0, keepdims=True))
        _accum(l_ref, i, jnp.broadcast_to(lsum, l_ref.shape))
        dz = _ln_bwd(dy, xhat, rstd, gv)
        dz_ref[...] = dz
        dzb_ref[...] = dz.astype(BF16)

    row = pl.BlockSpec((tr, d), lambda i: (i, 0))
    vec = pl.BlockSpec((1, d), lambda i: (0, 0))
    return pl.pallas_call(
        body, name="ln2_loss", grid=(t // tr,),
        in_specs=[row, vec, vec, row, row, vec, vec],
        out_specs=[row, row, vec, vec, pl.BlockSpec((8, LANE), lambda i: (0, 0))],
        out_shape=[jax.ShapeDtypeStruct((t, d), F32), jax.ShapeDtypeStruct((t, d), BF16),
                   jax.ShapeDtypeStruct((1, d), F32), jax.ShapeDtypeStruct((1, d), F32),
                   jax.ShapeDtypeStruct((8, LANE), F32)],
        compiler_params=_params("arbitrary"),
    )(xhat1, g1, b1, ff, target, g2, b2)


def _ln1_bwd(dz2, dffn, xhat1, rstd1, g1, tr=256, after=None):
    t, d = dz2.shape

    def body(a_ref, f_ref, xh_ref, rs_ref, g_ref, dz_ref, dzb_ref, dg_ref, db_ref):
        i = pl.program_id(0)
        dy = ALPHA * a_ref[...] + f_ref[...]
        xhat = xh_ref[...]
        _accum(dg_ref, i, jnp.sum(dy * xhat, axis=0, keepdims=True))
        _accum(db_ref, i, jnp.sum(dy, axis=0, keepdims=True))
        dz = _ln_bwd(dy, xhat, rs_ref[...], g_ref[...])
        dz_ref[...] = dz
        dzb_ref[...] = dz.astype(BF16)

    row = pl.BlockSpec((tr, d), lambda i: (i, 0))
    vec = pl.BlockSpec((1, d), lambda i: (0, 0))
    body, in_specs, args = _placed(after, body, [row, row, row, pl.BlockSpec((tr, 1), lambda i: (i, 0)), vec],
                                   (dz2, dffn, xhat1, rstd1, g1))
    return pl.pallas_call(
        body, name="ln1_bwd", grid=(t // tr,),
        in_specs=in_specs,
        out_specs=[row, row, vec, vec],
        out_shape=[jax.ShapeDtypeStruct((t, d), F32), jax.ShapeDtypeStruct((t, d), BF16),
                   jax.ShapeDtypeStruct((1, d), F32), jax.ShapeDtypeStruct((1, d), F32)],
        compiler_params=_params("arbitrary"),
    )(*args)


def _ffn_act_fwd(hid0, w_fc, b_fc, t, dff, after=None):
    nb = dff // LANE

    def body(hv_ref, hg_ref, wv_ref, wg_ref, bv_ref, bg_ref, a_ref):
        val = _conv(hv_ref[...], wv_ref[...]) + bv_ref[...]
        gate = _conv(hg_ref[...], wg_ref[...]) + bg_ref[...]
        a_ref[...] = (gate * _sigmoid(gate) * val).astype(BF16)

    col = lambda off: pl.BlockSpec((t, LANE), lambda j: (0, j + off))
    w3 = lambda off: pl.BlockSpec((3, LANE), lambda j: (0, j + off))
    w1 = lambda off: pl.BlockSpec((1, LANE), lambda j: (0, j + off))
    body, in_specs, args = _placed(after, body, [col(0), col(nb), w3(0), w3(nb), w1(0), w1(nb)],
                                   (hid0, hid0, w_fc, w_fc, b_fc, b_fc))
    return pl.pallas_call(
        body, name="ffn_act_fwd", grid=(nb,),
        in_specs=in_specs,
        out_specs=col(0),
        out_shape=jax.ShapeDtypeStruct((t, dff), BF16),
        compiler_params=_params("parallel"),
    )(*args)


def _ffn_act_bwd(da, hid0, w_fc, b_fc, t, dff, after=None):
    nb = dff // LANE

    def body(da_ref, hv_ref, hg_ref, wv_ref, wg_ref, bv_ref, bg_ref,
             dhv_ref, dhg_ref, dwv_ref, dwg_ref, dbv_ref, dbg_ref):
        hv, hg, wv, wg = hv_ref[...], hg_ref[...], wv_ref[...], wg_ref[...]
        rv, rg = _rolled(hv), _rolled(hg)
        val = _conv(hv, wv, rv) + bv_ref[...]
        gate = _conv(hg, wg, rg) + bg_ref[...]
        sig = _sigmoid(gate)
        d = da_ref[...]
        dsig = d * sig
        dval = dsig * gate
        dgate = dsig * val * (1.0 + gate * (1.0 - sig))
        dhv_ref[...] = _conv_t(dval, wv).astype(BF16)
        dhg_ref[...] = _conv_t(dgate, wg).astype(BF16)
        dwv_ref[...] = _conv_dw(dval, hv, rv)
        dwg_ref[...] = _conv_dw(dgate, hg, rg)
        dbv_ref[...] = jnp.sum(dval, axis=0, keepdims=True)
        dbg_ref[...] = jnp.sum(dgate, axis=0, keepdims=True)

    col = lambda off: pl.BlockSpec((t, LANE), lambda j: (0, j + off))
    w3 = lambda off: pl.BlockSpec((3, LANE), lambda j: (0, j + off))
    w1 = lambda off: pl.BlockSpec((1, LANE), lambda j: (0, j + off))
    s3 = jax.ShapeDtypeStruct((3, dff), F32)
    s1 = jax.ShapeDtypeStruct((1, dff), F32)
    body, in_specs, args = _placed(after, body, [col(0), col(0), col(nb), w3(0), w3(nb), w1(0), w1(nb)],
                                   (da, hid0, hid0, w_fc, w_fc, b_fc, b_fc))
    return pl.pallas_call(
        body, name="ffn_act_bwd", grid=(nb,),
        in_specs=in_specs,
        out_specs=[col(0), col(0), w3(0), w3(0), w1(0), w1(0)],
        out_shape=[jax.ShapeDtypeStruct((t, dff), BF16)] * 2 + [s3, s3, s1, s1],
        compiler_params=_params("parallel"),
    )(*args)


class _Ready:
    def __init__(self, **weights):
        self.weights = weights

    def begin(self, after):
        return None

    def forward(self, name, after):
        return None

    def get(self, name, after):
        return self.weights[name]


class _Kept:
    def __init__(self):
        self.grads = {}

    def start(self, name, grad):
        self.grads[name] = grad
        return None

    def relay(self, name, after):
        return None

    def meanwhile(self, small, loss, after):
        return None


def _behind(a, token):
    return a if token is None else a + token[0:1, 0:1].reshape((1,) * a.ndim)


def _local_step(x, target, w_in, b_gates, w_sc, gain, w_out, ln1_g, ln1_b, w_up, w_fc, b_fc, w_down, ln2_g, ln2_b,
                gx=None, wx=None, x_b=None):
    t, d = x.shape
    wc = d // 2
    dh = (d - wc) // NH
    wm = NH * dh
    dff = w_fc.shape[1] // 2
    if wx is None:
        wx = _Ready(w_out=w_out, w_up=w_up, w_down=w_down)
    ninp = 3 * wc + 4 * wm + LANE
    nin = 3 * wc + 4 * wm
    gate_tile = nin // LANE
    nc = t // CHUNK
    bias_tile = jnp.pad(b_gates, ((0, 0), (0, LANE - 2 * NH)))

    if x_b is None:
        x_b = x.astype(BF16)
    proj = _matmul(x_b, w_in, "nt", F32, "proj", tm=512, tn=2432, tk=d, n=ninp, after=wx.begin(w_in))
    y = _sconv_fwd(proj, w_sc, t, wc)
    gcol = _gates_prep(proj, bias_tile, t, gate_tile)
    grow = gcol[:, :8].T.reshape(8, nc, CHUNK).transpose(1, 0, 2)
    hval, cs, ns = _mlstm_fwd(proj, gcol, grow, t, wc, dh)
    y = _hnorm_fwd(hval, proj, gain, y, t, wc, dh)
    tok = wx.forward("w_out", y)
    w_out = wx.get("w_out", tok)
    mix = _matmul(y, w_out, "nn", F32, "out_proj", tm=512, tn=1024, tk=wc, a_blocked=True, after=tok)
    xhat1, rstd1, x1_b = _ln1_fwd(x, mix, ln1_g, ln1_b, after=wx.forward("w_up", mix))
    w_up = wx.get("w_up", x1_b)
    wsl = w_up.shape[2]
    hid0 = _matmul(x1_b, w_up, "nn", F32, "ffn_up", tm=1024, tn=wsl, tk=d, b_blocked=True)
    act = _ffn_act_fwd(hid0, w_fc, b_fc, t, dff, after=wx.forward("w_down", hid0))
    w_down = wx.get("w_down", act)
    ff = _matmul(act, w_down, "nn", F32, "ffn_down", tm=1024, tn=512, tk=dff)
    dz2, dz2_b, d_ln2_g, d_ln2_b, loss = _ln2_loss(xhat1, ln1_g, ln1_b, ff, target, ln2_g, ln2_b)

    if gx is None:
        gx = _Kept()
    d_w_down = _matmul(act, dz2_b, "tn", BF16, "ffn_down_dw", tm=1408, tn=1024, tk=t)
    d_act = _matmul(dz2_b, w_down, "nt", F32, "ffn_down_dx", tm=2048, tn=512, tk=d, after=gx.start("w_down", d_w_down))
    *d_hid0, dwv, dwg, dbv, dbg = _ffn_act_bwd(d_act, hid0, w_fc, b_fc, t, dff, after=gx.relay("w_down", d_act))
    d_w_fc = jnp.concatenate([dwv, dwg], axis=1)
    d_b_fc = jnp.concatenate([dbv, dbg], axis=1)
    d_hid0 = tuple(d_hid0[:2])
    d_w_up = _matmul(x1_b, d_hid0, "tn", BF16, "ffn_up_dw", tm=1024, tn=wsl, tk=t, o_width=wsl)
    d_x1_ffn = _matmul(d_hid0, w_up, "nt", F32, "ffn_up_dx", tm=1024, tn=1024, tk=wsl, b_blocked=True,
                       after=gx.start("w_up", d_w_up))
    dz1, dz1_b, d_ln1_g, d_ln1_b = _ln1_bwd(dz2, d_x1_ffn, xhat1, rstd1, ln1_g, after=gx.relay("w_up", d_x1_ffn))

    d_w_out = _matmul(y, dz1_b, "tn", BF16, "out_proj_dw", tm=1024, tn=1024, tk=t, a_blocked=True)
    dy = _matmul(dz1_b, w_out, "nt", F32, "out_proj_dx", tm=1024, tn=1024, tk=d, after=gx.start("w_out", d_w_out))
    d_proj, d_hval, d_gain = _hnorm_bwd(dy, hval, proj, gain, t, wc, dh)
    d_proj, d_w_sc = _sconv_bwd(dy, proj, w_sc, d_proj, t, wc, after=gx.relay("w_out", dy))
    d_proj, dgate = _mlstm_bwd(proj, gcol, grow, hval, d_hval, cs, ns, d_proj, t, wc, dh)
    d_proj, d_b_gates = _gates_bwd(dgate, proj, bias_tile, d_proj, t, gate_tile)
    d_w_in = _matmul(d_proj, x_b, "tn", BF16, "proj_dw", tm=2432, tn=1024, tk=t)
    small = dict(b_gates=d_b_gates[:, :2 * NH], w_sc_conv=d_w_sc, mh_gain=d_gain, ln1_g=d_ln1_g, ln1_b=d_ln1_b,
                 w_ffn_conv=d_w_fc, b_ffn_conv=d_b_fc, ln2_g=d_ln2_g, ln2_b=d_ln2_b)
    token = gx.start("w_in", d_w_in)
    token = gx.relay("w_in", gx.meanwhile(small, loss, token))
    grad_x = _matmul(d_proj, w_in, "nn", F32, "proj_dx", tm=512, tn=512, tk=ninp, add=dz1, add_scale=ALPHA, after=token)
    return loss, grad_x, small, gx


HBM = pl.BlockSpec(memory_space=pltpu.HBM)


def _place():
    return lax.axis_index("x"), lax.axis_index("y"), lax.axis_index("c")


def _index(p):
    return 4 * p[0] + 2 * p[1] + p[2]


def _all_gather(arrs, name):
    n = len(arrs)

    def body(*refs):
        ins, outs = refs[:n], refs[n:2 * n]
        send_sems, recv_sems, local_sems = refs[2 * n:]
        x, y, c = _place()
        me, sibling = (x, y, c), (x, y, 1 - c)
        chips = [(1 - x, y), (x, 1 - y), (1 - x, 1 - y)]

        def copy(a, k, block, to, own=False):
            dst = outs[a].at[_index(block)]
            return pltpu.make_async_remote_copy(
                src_ref=ins[a] if own else dst, dst_ref=dst,
                send_sem=send_sems.at[k * n + a], recv_sem=recv_sems.at[k * n + a],
                device_id=to, device_id_type=MESH)

        mine = [pltpu.make_async_copy(ins[a], outs[a].at[_index(me)], local_sems.at[a]) for a in range(n)]
        for cp in mine:
            cp.start()
        first = []
        for a in range(n):
            first.append(copy(a, 0, me, sibling, own=True))
            first += [copy(a, 1 + j, me, (*chip, c), own=True) for j, chip in enumerate(chips)]
        for cp in first:
            cp.start()
        passed = []
        for j, chip in enumerate(chips):
            for a in range(n):
                copy(a, 1 + j, (*chip, c), me).wait_recv()
                cp = copy(a, 4 + j, (*chip, c), sibling)
                cp.start()
                passed.append(cp)
        for a in range(n):
            copy(a, 0, sibling, me).wait_recv()
            for j, chip in enumerate(chips):
                copy(a, 4 + j, (*chip, 1 - c), me).wait_recv()
        for cp in first + passed:
            cp.wait_send()
        for cp in mine:
            cp.wait()

    return pl.pallas_call(
        body, name=name, in_specs=[HBM] * n, out_specs=[HBM] * n,
        out_shape=[jax.ShapeDtypeStruct((N_DEV,) + a.shape, a.dtype) for a in arrs],
        scratch_shapes=[pltpu.SemaphoreType.DMA((7 * n,)), pltpu.SemaphoreType.DMA((7 * n,)),
                        pltpu.SemaphoreType.DMA((n,))],
    )(*arrs)


SEM = pl.BlockSpec(memory_space=pltpu.SEMAPHORE)
EFFECT = pltpu.SideEffectType.DATAFLOW_SIDE_EFFECTING


def _chips(x, y):
    return [(1 - x, y), (x, 1 - y), (1 - x, 1 - y)]


N_CHIP = N_DEV // 2


def _pair_route(x, y, c):
    return [((x, y, 1 - c), 2 * q + (1 - c), q, q) for q in range(N_CHIP)]


def _chip_route(x, y, c):
    mine = 2 * x + y
    return [((*chip, c), 2 * chip[0] + chip[1], mine, 2 * chip[0] + chip[1]) for chip in _chips(x, y)]


def _exchange_pieces(g_ref, land_ref, width, tail):
    if not tail:
        return [(lambda i: g_ref.at[i], lambda s: land_ref.at[s])]
    rows = lambda i, n: pl.ds(pl.multiple_of(i * width, IN_TAIL), n)
    return [(lambda i: g_ref.at[rows(i, width), :], lambda s: land_ref.at[s, pl.ds(0, width), :]),
            (lambda i: g_ref.at[rows(i + 1, IN_TAIL), :], lambda s: land_ref.at[s, pl.ds(width, IN_TAIL), :])]


def _chip_slot(x, y, c):
    return 2 * x + y


def _exchange_start(grad, route, tail, name, own_slot=None):
    width = IN_SLAB if tail else grad.shape[1]
    n_p = 2 if tail else 1
    n_c = len(route(0, 0, 0))
    land_shape = (N_CHIP, width + (IN_TAIL if tail else 0), grad.shape[-1])
    assert not (tail and own_slot)

    def body(g_ref, land_ref, send_sems, recv_sems, g_thru, land_thru, token):
        for j, (peer, slab, slot, _) in enumerate(route(*_place())):
            for p, (src, dst) in enumerate(_exchange_pieces(g_ref, land_ref, width, tail)):
                pltpu.make_async_remote_copy(src_ref=src(slab), dst_ref=dst(slot), send_sem=send_sems.at[j * n_p + p],
                                             recv_sem=recv_sems.at[j * n_p + p], device_id=peer,
                                             device_id_type=MESH).start()
        if own_slot:
            mine = own_slot(*_place())
            pltpu.make_async_copy(g_ref.at[mine], land_ref.at[mine], send_sems.at[n_c * n_p]).start()
        token[...] = jnp.zeros_like(token)

    return pl.pallas_call(
        body, name=name,
        out_shape=(pltpu.SemaphoreType.DMA((n_c * n_p + bool(own_slot),)), pltpu.SemaphoreType.DMA((n_c * n_p,)),
                   pltpu.HBM(grad.shape, grad.dtype), pltpu.HBM(land_shape, grad.dtype),
                   jax.ShapeDtypeStruct((8, LANE), F32)),
        in_specs=(HBM, HBM), out_specs=(SEM, SEM, HBM, HBM, pl.BlockSpec(memory_space=pltpu.VMEM)),
        input_output_aliases={0: 2, 1: 3},
        compiler_params=pltpu.CompilerParams(has_side_effects=EFFECT),
    )(pltpu.with_memory_space_constraint(grad, pltpu.HBM),
      pltpu.with_memory_space_constraint(lax.empty(land_shape, grad.dtype), pltpu.HBM))


def _exchange_wait(send_sems, recv_sems, g_thru, land_thru, after, route, tail, name, own_slot=None):
    width = IN_SLAB if tail else g_thru.shape[1]
    n_p = 2 if tail else 1

    def body(g_ref, land_ref, send_sems, recv_sems, after_ref, g_dead, got_ref):
        places = route(*_place())
        for j, (peer, slab, _, slot) in enumerate(places):
            for p, (src, dst) in enumerate(_exchange_pieces(g_ref, land_ref, width, tail)):
                cp = pltpu.make_async_remote_copy(src_ref=src(slab), dst_ref=dst(slot),
                                                  send_sem=send_sems.at[j * n_p + p], recv_sem=recv_sems.at[j * n_p + p],
                                                  device_id=peer, device_id_type=MESH)
                cp.wait_send()
                cp.wait_recv()
        if own_slot:
            mine = own_slot(*_place())
            pltpu.make_async_copy(g_ref.at[mine], land_ref.at[mine], send_sems.at[len(places) * n_p]).wait()

    return pl.pallas_call(
        body, name=name,
        out_shape=(pltpu.HBM(g_thru.shape, g_thru.dtype), pltpu.HBM(land_thru.shape, land_thru.dtype)),
        in_specs=(HBM, HBM, SEM, SEM, pl.BlockSpec(memory_space=pl.ANY)), out_specs=(HBM, HBM),
        input_output_aliases={0: 0, 1: 1},
        compiler_params=pltpu.CompilerParams(has_side_effects=EFFECT),
    )(g_thru, land_thru, send_sems, recv_sems, after)


def _pair_add(grad, pair, core, tail, name):
    rows, cols = (IN_SLAB if tail else grad.shape[1]), grad.shape[-1]
    total = pair.shape[1]

    def body(core_ref, *refs):
        if tail:
            g_ref, t_ref, p_ref, o_ref = refs
            o_ref[0:rows, :] = (g_ref[...].astype(F32) + p_ref[0:rows, :].astype(F32)).astype(BF16)
            o_ref[rows:total, :] = (t_ref[...].astype(F32) + p_ref[rows:total, :].astype(F32)).astype(BF16)
        else:
            g_ref, p_ref, o_ref = refs
            o_ref[...] = (g_ref[...].astype(F32) + p_ref[...].astype(F32)).astype(BF16)

    if tail:
        tc = _fit(cols, 512)
        grid = (N_CHIP, cols // tc)
        slab = pl.BlockSpec((None, total, tc), lambda q, i, core_ref: (q, 0, i))
        per = IN_SLAB // IN_TAIL
        in_specs = [pl.BlockSpec((rows, tc), lambda q, i, core_ref: (2 * q + core_ref[0], i)),
                    pl.BlockSpec((IN_TAIL, tc), lambda q, i, core_ref: ((2 * q + core_ref[0] + 1) * per, i))]
    else:
        tr = _rows(rows, 1024)
        grid = (N_CHIP, rows // tr)
        slab = pl.BlockSpec((None, tr, cols), lambda q, i, core_ref: (q, i, 0))
        in_specs = [pl.BlockSpec((None, tr, cols), lambda q, i, core_ref: (2 * q + core_ref[0], i, 0))]
    return pl.pallas_call(
        body, name=name,
        grid_spec=pltpu.PrefetchScalarGridSpec(num_scalar_prefetch=1, grid=grid,
                                               in_specs=in_specs + [slab], out_specs=slab),
        out_shape=jax.ShapeDtypeStruct(pair.shape, BF16),
        compiler_params=_params("parallel", "parallel"),
    )(core, *([grad, grad] if tail else [grad]), pair)


def _relay_places(x, y, c):
    came_from = (c * (1 - x) + (1 - c) * x, c * y + (1 - c) * (1 - y), c)
    pass_to = (c * x + (1 - c) * (1 - x), c * (1 - y) + (1 - c) * y, c)
    return 2 - c, came_from, pass_to, pass_to


OWN = 4


def _gather_start(blocks, after, name, spare=(), relayed=False):
    n = len(blocks)
    lands = [(N_DEV + (a in spare),) + b.shape for a, b in enumerate(blocks)]

    def body(*refs):
        b_refs, land_refs = refs[:n], refs[n:2 * n]
        send_sems, recv_sems = refs[2 * n + 1:3 * n + 1], refs[3 * n + 1:4 * n + 1]
        token = refs[-1]
        x, y, c = _place()
        me = _index((x, y, c))
        for a in range(n):
            targets = [(x, y, 1 - c)] + [(*chip, c) for chip in _chips(x, y)]
            for k, to in enumerate(targets[:3] if relayed else targets):
                pltpu.make_async_remote_copy(src_ref=b_refs[a], dst_ref=land_refs[a].at[me], send_sem=send_sems[a].at[k],
                                             recv_sem=recv_sems[a].at[k], device_id=to, device_id_type=MESH).start()
        for a in range(n):
            pltpu.make_async_copy(b_refs[a], land_refs[a].at[me], send_sems[a].at[OWN]).start()
        token[...] = jnp.zeros_like(token)

    sems = [pltpu.SemaphoreType.DMA((OWN + 1,))] * n
    out = pl.pallas_call(
        body, name=name,
        out_shape=(*sems, *sems, *[pltpu.HBM(b.shape, b.dtype) for b in blocks],
                   *[pltpu.HBM(s, b.dtype) for s, b in zip(lands, blocks)], jax.ShapeDtypeStruct((8, LANE), F32)),
        in_specs=(*[HBM] * (2 * n), pl.BlockSpec(memory_space=pl.ANY)),
        out_specs=(*[SEM] * (2 * n), *[HBM] * (2 * n), pl.BlockSpec(memory_space=pltpu.VMEM)),
        input_output_aliases={i: 2 * n + i for i in range(2 * n)},
        compiler_params=pltpu.CompilerParams(has_side_effects=EFFECT),
    )(*[pltpu.with_memory_space_constraint(b, pltpu.HBM) for b in blocks],
      *[pltpu.with_memory_space_constraint(lax.empty(s, b.dtype), pltpu.HBM) for s, b in zip(lands, blocks)], after)
    return [(out[a], out[n + a], out[2 * n + a], out[3 * n + a]) for a in range(n)], out[-1]


def _gather_relay(states, after, name):
    n, first_out = len(states), 3 * len(states) + len(after)

    def body(*refs):
        land_refs, send_sems, recv_sems = refs[:n], refs[n:2 * n], refs[2 * n:3 * n]
        pass_send, pass_recv = refs[first_out + n:first_out + 2 * n], refs[first_out + 2 * n:first_out + 3 * n]
        k_in, came_from, pass_to, _ = _relay_places(*_place())
        for a in range(n):
            slot = land_refs[a].at[_index(came_from)]
            pltpu.make_async_remote_copy(src_ref=slot, dst_ref=slot, send_sem=send_sems[a].at[k_in],
                                         recv_sem=recv_sems[a].at[k_in], device_id=came_from,
                                         device_id_type=MESH).wait_recv()
            pltpu.make_async_remote_copy(src_ref=slot, dst_ref=slot, send_sem=pass_send[a].at[0],
                                         recv_sem=pass_recv[a].at[0], device_id=pass_to, device_id_type=MESH).start()
        refs[-1][...] = jnp.zeros_like(refs[-1])

    lands = [st[3] for st in states]
    pair = [pltpu.SemaphoreType.DMA((1,))] * n
    out = pl.pallas_call(
        body, name=name,
        out_shape=(*[pltpu.HBM(l.shape, l.dtype) for l in lands], *pair, *pair, jax.ShapeDtypeStruct((8, LANE), F32)),
        in_specs=(*[HBM] * n, *[SEM] * (2 * n), *[pl.BlockSpec(memory_space=pl.ANY)] * len(after)),
        out_specs=(*[HBM] * n, *[SEM] * (2 * n), pl.BlockSpec(memory_space=pltpu.VMEM)),
        input_output_aliases={a: a for a in range(n)},
        compiler_params=pltpu.CompilerParams(has_side_effects=EFFECT),
    )(*lands, *[st[0] for st in states], *[st[1] for st in states], *after)
    return [(st[0], st[1], st[2], out[a], (out[n + a], out[2 * n + a])) for a, st in enumerate(states)], out[-1]


def _gather_forward(send_sems, recv_sems, b_thru, land_thru, after, name, passed=None):
    relayed = passed is not None

    def body(b_ref, land_ref, send_sems, recv_sems, *rest):
        pass_send, pass_recv = rest[:2] if relayed else (None, None)
        send2, recv2, token = rest[-3:]
        x, y, c = _place()
        sibling = (x, y, 1 - c)
        chips = [(*chip, c) for chip in _chips(x, y)]
        arrivals = [((send_sems.at[j + 1], recv_sems.at[j + 1]), frm, j) for j, frm in enumerate(chips)]
        sends = [send_sems.at[k] for k in range(4)]
        if relayed:
            k_in, came_from, _, other = _relay_places(x, y, c)
            arrivals = [(None, came_from, k_in - 1), ((send_sems.at[3 - k_in], recv_sems.at[3 - k_in]), other, 2 - k_in),
                        ((pass_send.at[0], pass_recv.at[0]), chips[2], 2)]
            sends[3] = pass_send.at[0]
        for sems, frm, j in arrivals:
            slot = land_ref.at[_index(frm)]
            if sems:
                pltpu.make_async_remote_copy(src_ref=b_ref, dst_ref=slot, send_sem=sems[0], recv_sem=sems[1],
                                             device_id=frm, device_id_type=MESH).wait_recv()
            pltpu.make_async_remote_copy(src_ref=slot, dst_ref=slot, send_sem=send2.at[j], recv_sem=recv2.at[j],
                                         device_id=sibling, device_id_type=MESH).start()
        pltpu.make_async_remote_copy(src_ref=b_ref, dst_ref=land_ref.at[_index(sibling)], send_sem=send_sems.at[0],
                                     recv_sem=recv_sems.at[0], device_id=sibling, device_id_type=MESH).wait_recv()
        for sem in sends:
            pltpu.make_async_remote_copy(src_ref=b_ref, dst_ref=land_ref.at[0], send_sem=sem, recv_sem=recv_sems.at[0],
                                         device_id=sibling, device_id_type=MESH).wait_send()
        pltpu.make_async_copy(b_ref, land_ref.at[_index((x, y, c))], send_sems.at[OWN]).wait()
        token[...] = jnp.zeros_like(token)

    extra = list(passed) if relayed else []
    return pl.pallas_call(
        body, name=name,
        out_shape=(pltpu.HBM(b_thru.shape, b_thru.dtype), pltpu.HBM(land_thru.shape, land_thru.dtype),
                   pltpu.SemaphoreType.DMA((3,)), pltpu.SemaphoreType.DMA((3,)), jax.ShapeDtypeStruct((8, LANE), F32)),
        in_specs=(HBM, HBM, SEM, SEM, *[SEM] * len(extra), pl.BlockSpec(memory_space=pl.ANY)),
        out_specs=(HBM, HBM, SEM, SEM, pl.BlockSpec(memory_space=pltpu.VMEM)),
        input_output_aliases={0: 0, 1: 1},
        compiler_params=pltpu.CompilerParams(has_side_effects=EFFECT),
    )(b_thru, land_thru, send_sems, recv_sems, *extra, after)


def _gather_finish(land_thru, send2, recv2, after, name):
    def body(land_ref, send2, recv2, after_ref, land_out):
        x, y, c = _place()
        for j, chip in enumerate(_chips(x, y)):
            cp = pltpu.make_async_remote_copy(src_ref=land_ref.at[_index((*chip, c))],
                                              dst_ref=land_ref.at[_index((*chip, 1 - c))], send_sem=send2.at[j],
                                              recv_sem=recv2.at[j], device_id=(x, y, 1 - c), device_id_type=MESH)
            cp.wait_send()
            cp.wait_recv()

    return pl.pallas_call(
        body, name=name, out_shape=pltpu.HBM(land_thru.shape, land_thru.dtype),
        in_specs=(HBM, SEM, SEM, pl.BlockSpec(memory_space=pl.ANY)), out_specs=HBM,
        input_output_aliases={0: 0},
        compiler_params=pltpu.CompilerParams(has_side_effects=EFFECT),
    )(land_thru, send2, recv2, after)


class _Gathering:
    def __init__(self, ahead, later, me):
        cast = [a.astype(BF16) for a in ahead.values()]
        started, self.token = _gather_start(cast, cast[0], "gather1_ahead", relayed=True)
        self.me, self.state, self.relayed, self.later = me, dict(zip(ahead, started)), tuple(ahead), later

    def start_first(self, first, spare):
        started, self.token = _gather_start(list(first.values()), self.token, "gather1_first",
                                            spare=(list(first).index(spare),), relayed=True)
        self.state.update(zip(first, started))
        self.relayed += tuple(first)

    def begin(self, after):
        return self.token

    def relay(self, *after):
        states, token = _gather_relay([self.state[n] for n in self.relayed], after, "gather_relay")
        self.state.update(zip(self.relayed, states))
        cast = [_behind(a, token).astype(BF16) for a in self.later.values()]
        started, self.token = _gather_start(cast, token, "gather1_later")
        self.state.update(zip(self.later, started))
        return self.token

    def forward(self, name, after):
        first_leg, passed = self.state[name][:4], (self.state[name][4:] or (None,))[0]
        *self.state[name], token = _gather_forward(*first_leg, after, "gather2_" + name, passed=passed)
        return token

    def get(self, name, after):
        _, land, send2, recv2 = self.state[name]
        land = _gather_finish(land, send2, recv2, after, "gather3_" + name)
        return land if name not in ("w_out", "w_down") else land.reshape(-1, land.shape[2])


class _Reducing:
    def __init__(self, core, chip, gather_small):
        self.core, self.chip, self.state, self.token, self.gather_small = core, chip, {}, None, gather_small

    def meanwhile(self, small, loss, after):
        self.small_sum = self.gather_small(small, loss, after)
        return self.small_sum

    def start(self, name, grad):
        tail = name == "w_in"
        g = grad if tail or grad.ndim == 3 else grad.reshape(N_DEV, grad.shape[0] // N_DEV, grad.shape[1])
        *self.state[name], token = _exchange_start(g, _pair_route, tail, "pair_send_" + name)
        return token

    def relay(self, name, after):
        tail = name == "w_in"
        grad, pair = _exchange_wait(*self.state[name], after, _pair_route, tail, "pair_recv_" + name)
        total = _pair_add(grad, pair, self.core, tail, "pair_add_" + name)
        *self.state[name], self.token = _exchange_start(total, _chip_route, False, "chip_send_" + name,
                                                        own_slot=_chip_slot)
        return self.token

    def finish(self, name, after):
        _, land = _exchange_wait(*self.state[name], after, _chip_route, False, "chip_recv_" + name, own_slot=_chip_slot)
        return land


def _carry_w_in(main, tail):
    slabs, _, d = main.shape
    tc = _fit(d, 2048)
    assert slabs == N_DEV + 1 and tail.shape[:2] == (N_DEV, IN_TAIL), (main.shape, tail.shape)
    top = lambda off: pl.BlockSpec((None, IN_TAIL, tc), lambda s, j: (s + off, 0, j))

    def carry(m_ref, t_ref, o_ref):
        o_ref[...] = m_ref[...] + t_ref[...]

    main = pl.pallas_call(
        carry, name="carry_w_in", grid=(N_DEV - 1, d // tc), in_specs=[top(1), top(0)], out_specs=top(1),
        out_shape=jax.ShapeDtypeStruct(main.shape, main.dtype), input_output_aliases={0: 0},
        compiler_params=_params("parallel", "parallel"),
    )(main, tail)

    def last(m_ref, t_ref, o_ref):
        o_ref[...] = jnp.zeros_like(o_ref)
        o_ref[0:IN_TAIL, :] = t_ref[...]

    return pl.pallas_call(
        last, name="last_slab_w_in", grid=(d // tc,),
        in_specs=[pl.BlockSpec(memory_space=pl.ANY), pl.BlockSpec((None, IN_TAIL, tc), lambda j: (N_DEV - 1, 0, j))],
        out_specs=pl.BlockSpec((None, LANE, tc), lambda j: (N_DEV, 0, j)),
        out_shape=jax.ShapeDtypeStruct(main.shape, main.dtype), input_output_aliases={0: 0},
        compiler_params=_params("parallel"),
    )(main, tail)


def _rows(n, want):
    t = min(n, want)
    t -= t % 16
    while n % t:
        t -= 16
    return t


def _adam_math(w, g, m, v):
    m2 = ADAM_B1 * m + (1.0 - ADAM_B1) * g
    v2 = ADAM_B2 * v + (1.0 - ADAM_B2) * (g * g)
    m_hat = m2 * (1.0 / (1.0 - ADAM_B1 ** ADAM_STEP))
    v_hat = v2 * (1.0 / (1.0 - ADAM_B2 ** ADAM_STEP))
    return -ADAM_LR * (m_hat / (jnp.sqrt(v_hat) + ADAM_EPS) + ADAM_WD * w), m2, v2


def _slot_sum(r_ref):
    acc = r_ref[0].astype(F32)
    for i in range(1, r_ref.shape[0]):
        acc = acc + r_ref[i].astype(F32)
    return acc


def _shift_w_in(w):
    ws, d = w.shape
    tc = _fit(d, 256)

    def body(w_ref, main_ref, tail_ref, tall):
        tall[...] = jnp.zeros_like(tall)
        tall[0:ws, :] = w_ref[...]
        moved = pltpu.roll(tall[...], _index(_place()), 0).astype(BF16)
        main_ref[...] = moved[0:IN_SLAB]
        tail_ref[...] = moved[IN_SLAB:]

    return pl.pallas_call(
        body, name="shift_w_in", grid=(d // tc,),
        in_specs=[pl.BlockSpec((ws, tc), lambda j: (0, j))],
        out_specs=[pl.BlockSpec((IN_SLAB, tc), lambda j: (0, j)), pl.BlockSpec((IN_TAIL, tc), lambda j: (0, j))],
        out_shape=[jax.ShapeDtypeStruct((IN_SLAB, d), BF16), jax.ShapeDtypeStruct((IN_TAIL, d), BF16)],
        scratch_shapes=[pltpu.VMEM((IN_SLAB + IN_TAIL, tc), F32)], compiler_params=_params("parallel"),
    )(w)


def _sum_adamw_shifted(r, w, m, v, name):
    _, ph, d = r.shape
    ws = w.shape[0]
    tc = _fit(d, 256)

    def body(r_ref, w_ref, m_ref, v_ref, g_ref, d_ref, m2_ref, v2_ref, tall):
        tall[...] = pltpu.roll(_slot_sum(r_ref), lax.rem(ph - _index(_place()), ph), 0)
        g = tall[0:ws, :]
        g_ref[...] = g
        d_ref[...], m2_ref[...], v2_ref[...] = _adam_math(w_ref[...], g, m_ref[...], v_ref[...])

    blk = pl.BlockSpec((ws, tc), lambda j: (0, j))
    out = jax.ShapeDtypeStruct(w.shape, F32)
    return pl.pallas_call(
        body, name=name, grid=(d // tc,),
        in_specs=[pl.BlockSpec((r.shape[0], ph, tc), lambda j: (0, 0, j)), blk, blk, blk],
        out_specs=[blk] * 4, out_shape=[out] * 4,
        scratch_shapes=[pltpu.VMEM((ph, tc), F32)], compiler_params=_params("parallel"),
    )(r, w, m, v)


def _sum_slots(r, name, tr=128):
    _, rows, cols = r.shape
    tr = _rows(rows, tr)

    def body(r_ref, g_ref):
        g_ref[...] = _slot_sum(r_ref)

    return pl.pallas_call(
        body, name=name, grid=(rows // tr,),
        in_specs=[pl.BlockSpec((r.shape[0], tr, cols), lambda i: (0, i, 0))],
        out_specs=pl.BlockSpec((tr, cols), lambda i: (i, 0)),
        out_shape=jax.ShapeDtypeStruct((rows, cols), F32),
        compiler_params=_params("parallel"),
    )(r)


def _adamw(w, g, m, v, name, tr=256):
    rows, cols = w.shape
    tr = _rows(rows, tr)

    def body(w_ref, g_ref, m_ref, v_ref, d_ref, m2_ref, v2_ref):
        d_ref[...], m2_ref[...], v2_ref[...] = _adam_math(w_ref[...], g_ref[...], m_ref[...], v_ref[...])

    blk = pl.BlockSpec((tr, cols), lambda i: (i, 0))
    out = jax.ShapeDtypeStruct((rows, cols), F32)
    return pl.pallas_call(
        body, name=name, grid=(rows // tr,), in_specs=[blk] * 4, out_specs=[blk] * 3, out_shape=[out] * 3,
        compiler_params=_params("parallel"),
    )(w, g, m, v)


def _sum_adamw(r, w, m, v, name, tr=256):
    rows, cols = w.shape
    tr = _rows(rows, tr)

    def body(r_ref, w_ref, m_ref, v_ref, g_ref, d_ref, m2_ref, v2_ref):
        g = _slot_sum(r_ref)
        g_ref[...] = g
        d_ref[...], m2_ref[...], v2_ref[...] = _adam_math(w_ref[...], g, m_ref[...], v_ref[...])

    blk = pl.BlockSpec((tr, cols), lambda i: (i, 0))
    out = jax.ShapeDtypeStruct((rows, cols), F32)
    return pl.pallas_call(
        body, name=name, grid=(rows // tr,),
        in_specs=[pl.BlockSpec((r.shape[0], tr, cols), lambda i: (0, i, 0)), blk, blk, blk],
        out_specs=[blk] * 4, out_shape=[out] * 4,
        compiler_params=_params("parallel"),
    )(r, w, m, v)


def _pack(pieces, sizes):
    flat = [jnp.pad(p.reshape(-1).astype(F32), (0, s - p.size)) for p, s in zip(pieces, sizes)]
    total = sum(sizes)
    padded = -(-total // (16 * LANE)) * (16 * LANE)
    return jnp.pad(jnp.concatenate(flat), (0, padded - total)).reshape(-1, LANE)


def _unpack(packed, shapes, sizes):
    flat = packed.reshape(-1)
    out, off = [], 0
    for shp, s in zip(shapes, sizes):
        n = 1
        for k in shp:
            n *= k
        out.append(flat[off:off + n].reshape(shp))
        off += s
    return out


def _lanes(n):
    return -(-n // LANE) * LANE


WEIGHTS = ("w_in", "b_gates", "w_sc_conv", "mh_gain", "w_out", "ln1_g", "ln1_b", "w_up", "w_ffn_conv", "b_ffn_conv",
           "w_down", "ln2_g", "ln2_b")
BIG = ("w_in", "w_out", "w_up", "w_down")
SMALL = tuple(n for n in WEIGHTS if n not in BIG)


def kernel(x, w_in, b_gates, w_sc_conv, mh_gain, w_out, ln1_g, ln1_b, w_up, w_ffn_conv, b_ffn_conv, w_down, ln2_g, ln2_b, loss_target, m_w_in, m_b_gates, m_w_sc_conv, m_mh_gain, m_w_out, m_ln1_g, m_ln1_b, m_w_up, m_w_ffn_conv, m_b_ffn_conv, m_w_down, m_ln2_g, m_ln2_b, v_w_in, v_b_gates, v_w_sc_conv, v_mh_gain, v_w_out, v_ln1_g, v_ln1_b, v_w_up, v_w_ffn_conv, v_b_ffn_conv, v_w_down, v_ln2_g, v_ln2_b):
    w = dict(zip(WEIGHTS, (w_in, b_gates, w_sc_conv, mh_gain, w_out, ln1_g, ln1_b, w_up, w_ffn_conv, b_ffn_conv,
                           w_down, ln2_g, ln2_b)))
    m = dict(zip(WEIGHTS, (m_w_in, m_b_gates, m_w_sc_conv, m_mh_gain, m_w_out, m_ln1_g, m_ln1_b, m_w_up,
                           m_w_ffn_conv, m_b_ffn_conv, m_w_down, m_ln2_g, m_ln2_b)))
    v = dict(zip(WEIGHTS, (v_w_in, v_b_gates, v_w_sc_conv, v_mh_gain, v_w_out, v_ln1_g, v_ln1_b, v_w_up,
                           v_w_ffn_conv, v_b_ffn_conv, v_w_down, v_ln2_g, v_ln2_b)))
    me = _index(_place())
    d = x.shape[2]
    ws_in = w_in.shape[2]
    assert ws_in == IN_SLAB + 1 and N_DEV <= LANE, w_in.shape
    ninp = (N_DEV + 1) * IN_SLAB
    ws_sc, ws_fc = w_sc_conv.shape[2], w_ffn_conv.shape[2]

    wx = _Gathering({"w_out": w_out[0]}, {n: w[n][0] for n in ("w_up", "w_down")}, me)
    w_in_t = jnp.transpose(_behind(w_in[0], wx.token))
    w_in_main, w_in_tail = _shift_w_in(w_in_t)
    taps8 = lambda a: jnp.pad(a[0], ((0, 5), (0, 0)))
    at_once = ("w_sc", "w_fc", "w_tail", "w_in")
    wx.start_first(dict(zip(at_once, (taps8(w_sc_conv), taps8(w_ffn_conv), w_in_tail, w_in_main))), spare="w_in")
    x_b = _behind(x[0], wx.begin(None)).astype(BF16)
    m_in_t, v_in_t = (jnp.transpose(_behind(a[0], wx.begin(None))) for a in (m_w_in, v_w_in))
    token = wx.relay(x_b, m_in_t, v_in_t)
    for n in at_once:
        token = wx.forward(n, token)
    g_sc, g_fc, g_tail, g_in = (wx.get(n, token) for n in at_once)
    w_in_full = _carry_w_in(g_in, g_tail).reshape(ninp, d)
    w_sc_full = g_sc[:, :3].transpose(1, 0, 2).reshape(3, N_DEV * ws_sc)
    w_fc_full = g_fc[:, :3].transpose(1, 0, 2).reshape(3, N_DEV * ws_fc)

    xi, yi, ci = _place()
    names = ("loss",) + SMALL
    pieces = {}

    def gather_small(small, loss_t, after):
        pieces.update(small, loss=loss_t[0, :1])
        sizes = [_lanes(pieces[n].size) for n in names]
        (g_small,) = _all_gather([_behind(_pack([pieces[n] for n in names], sizes), after)], "gather_small")
        return _sum_slots(g_small, "sum_small", tr=g_small.shape[1])

    gx = _Reducing(jnp.reshape(ci, (1,)).astype(jnp.int32), 2 * xi + yi, gather_small)
    loss_t, grad_x, small, _ = _local_step(
        x[0], loss_target[0], w_in_full, b_gates, w_sc_full, mh_gain, None, ln1_g, ln1_b, None,
        w_fc_full, b_ffn_conv, None, ln2_g, ln2_b, gx=gx, wx=wx, x_b=x_b)

    grads, deltas, new_m, new_v = {}, {}, {}, {}
    for name in ("w_down", "w_up", "w_out"):
        grads[name], deltas[name], new_m[name], new_v[name] = _sum_adamw(
            gx.finish(name, gx.token), w[name][0], m[name][0], v[name][0], "adamw_" + name)

    summed = _unpack(_behind(gx.small_sum, gx.token), [pieces[n].shape for n in names],
                     [_lanes(pieces[n].size) for n in names])
    full = dict(zip(names, summed))
    full["w_sc_conv"] = lax.dynamic_slice(full["w_sc_conv"], (0, me * ws_sc), (3, ws_sc))
    full["w_ffn_conv"] = lax.dynamic_slice(full["w_ffn_conv"], (0, me * ws_fc), (3, ws_fc))
    for n in SMALL:
        grads[n] = full[n].reshape(w[n].shape)
    sizes = [_lanes(w[n].size) for n in SMALL]
    shapes = [w[n].shape for n in SMALL]
    packed = [_pack([t[n] for n in SMALL], sizes) for t in (w, grads, m, v)]
    small_out = _adamw(*packed, "adamw_small")
    for res, t in zip(small_out, (deltas, new_m, new_v)):
        t.update(zip(SMALL, _unpack(res, shapes, sizes)))

    done = sum(t[0:1, 0:1] for t in (grad_x, deltas["w_down"], deltas["w_up"], deltas["w_out"], small_out[0]))
    grads["w_in"], deltas["w_in"], new_m["w_in"], new_v["w_in"] = (
        jnp.transpose(a)[None] for a in _sum_adamw_shifted(gx.finish("w_in", done), w_in_t, m_in_t, v_in_t, "adamw_w_in"))

    big = lambda t: {n: (t[n].reshape(w[n].shape) if n in BIG else t[n]) for n in WEIGHTS}
    grads, deltas, new_m, new_v = big(grads), big(deltas), big(new_m), big(new_v)
    return (full["loss"].reshape(()), grad_x[None], *[grads[n] for n in WEIGHTS], *[deltas[n] for n in WEIGHTS],
            *[new_m[n] for n in WEIGHTS], *[new_v[n] for n in WEIGHTS])
```

```python
import functools

import jax
import jax.numpy as jnp
from jax import lax
from jax.experimental import pallas as pl
from jax.experimental.pallas import tpu as pltpu

F32 = jnp.float32
BF16 = jnp.bfloat16
MESH = pl.DeviceIdType.MESH

N_DEV = 8
NH = 4
CHUNK = 64
LN_EPS = 1e-5
HN_EPS = 1e-6
ALPHA = 2.0 ** 0.25
LANE = 128
IN_SLAB = 7 * LANE
IN_TAIL = 16
VMEM_LIMIT = 56 * 1024 * 1024
ADAM_LR, ADAM_B1, ADAM_B2, ADAM_EPS, ADAM_WD, ADAM_STEP = 0.001, 0.9, 0.999, 1e-08, 0.01, 10

_NN = (((1,), (0,)), ((), ()))
_NT = (((1,), (1,)), ((), ()))
_TN = (((0,), (0,)), ((), ()))


def _dot(a, b, dn=_NN):
    return lax.dot_general(a, b, dn, preferred_element_type=F32)


def _params(*sem):
    return pltpu.CompilerParams(dimension_semantics=sem if sem else None, vmem_limit_bytes=VMEM_LIMIT)


def _iota(shape, axis):
    return lax.broadcasted_iota(jnp.int32, shape, axis)


def _fit(n, want):
    if n <= want:
        return n
    t = want - want % LANE
    while n % t:
        t -= LANE
    return t


def _placed(after, body, in_specs, args):
    if after is None:
        return body, in_specs, args
    return (lambda after_ref, *refs: body(*refs)), [pl.BlockSpec(memory_space=pl.ANY)] + in_specs, (after,) + args


def _matmul(a, b, mode, out_dtype, name, tm=1024, tn=512, tk=1024, add=None, add_scale=1.0,
            a_blocked=False, b_blocked=False, o_width=None, after=None, n=None):
    a_parts = a if isinstance(a, tuple) else None
    b_parts = b if isinstance(b, tuple) else None
    if a_parts:
        a_blocked, (a_rows, wa), na = True, a[0].shape, len(a)
        kd, m = (a_rows, na * wa) if mode == "tn" else (na * wa, a_rows)
    elif a_blocked:
        na, a_rows, wa = a.shape
        kd, m = (a_rows, na * wa) if mode == "tn" else (na * wa, a_rows)
    elif mode == "tn":
        kd, m = a.shape
    else:
        m, kd = a.shape
    if b_parts:
        b_blocked, (rows, w), nb = True, b[0].shape, len(b)
    elif b_blocked:
        nb, rows, w = b.shape
    if b_blocked:
        n = rows if mode == "nt" else nb * w
        assert (nb * w if mode == "nt" else rows) == kd, (name, kd)
    else:
        n = n or (b.shape[0] if mode == "nt" else b.shape[1])
    tm, tn, tk = _fit(m, tm), _fit(n, tn), _fit(kd, tk)
    if a_blocked and mode == "tn":
        tm = _fit(wa, tm)
    if a_blocked and mode != "tn":
        tk = _fit(wa, tk)
    if b_blocked and mode != "nt":
        tn = _fit(w, tn)
    if b_blocked and mode == "nt":
        tk = _fit(w, tk)
    if o_width is not None:
        tn = _fit(o_width, tn)
    assert m % tm == 0 and n % tn == 0 and kd % tk == 0, (name, m, n, kd, tm, tn, tk)
    assert not (a_blocked and mode != "tn" and wa % tk) and not (b_blocked and mode == "nt" and w % tk), (name, tk)
    nk = kd // tk
    dn = {"nn": _NN, "nt": _NT, "tn": _TN}[mode]
    if a_blocked and mode == "tn":
        a_per = wa // tm
        a_spec = pl.BlockSpec((None, tk, tm), lambda i, j, k: (i // a_per, k, i % a_per))
    elif a_blocked:
        a_per = wa // tk
        a_spec = pl.BlockSpec((None, tm, tk), lambda i, j, k: (k // a_per, i, k % a_per))
    elif mode == "tn":
        a_spec = pl.BlockSpec((tk, tm), lambda i, j, k: (k, i))
    else:
        a_spec = pl.BlockSpec((tm, tk), lambda i, j, k: (i, k))
    if b_blocked and mode != "nt":
        per = w // tn
        b_spec = pl.BlockSpec((None, tk, tn), lambda i, j, k: (j // per, k, j % per))
    elif b_blocked:
        per = w // tk
        b_spec = pl.BlockSpec((None, tn, tk), lambda i, j, k: (k // per, j, k % per))
    elif mode == "nt":
        b_spec = pl.BlockSpec((tn, tk), lambda i, j, k: (j, k))
    else:
        b_spec = pl.BlockSpec((tk, tn), lambda i, j, k: (k, j))
    if o_width is None:
        o_spec = pl.BlockSpec((tm, tn), lambda i, j, k: (i, j))
        o_shape = (m, n)
    else:
        oper = o_width // tn
        o_spec = pl.BlockSpec((None, tm, tn), lambda i, j, k: (j // oper, i, j % oper))
        o_shape = (n // o_width, m, o_width)
    a_list, a_specs = [a], [a_spec]
    if a_parts:
        hold = lambda x, s: jnp.clip(x - s * a_per, 0, a_per - 1)
        a_list = list(a_parts)
        a_specs = [(pl.BlockSpec((tk, tm), lambda i, j, k, s=s: (k, hold(i, s))) if mode == "tn"
                    else pl.BlockSpec((tm, tk), lambda i, j, k, s=s: (i, hold(k, s)))) for s in range(na)]
    b_list, b_specs = [b], [b_spec]
    if b_parts:
        hold_b = lambda x, s: jnp.clip(x - s * per, 0, per - 1)
        b_list = list(b_parts)
        b_specs = [(pl.BlockSpec((tn, tk), lambda i, j, k, s=s: (j, hold_b(k, s))) if mode == "nt"
                    else pl.BlockSpec((tk, tn), lambda i, j, k, s=s: (k, hold_b(j, s)))) for s in range(nb)]
    n_a, n_b = len(a_list), len(b_list)
    has_add = add is not None
    n_in = n_a + n_b + has_add + (after is not None)
    in_place = nk > 1 and out_dtype == F32

    def body(*refs):
        add_ref = refs[n_a + n_b] if has_add else None
        o_ref = refs[n_in]
        i, j, k = pl.program_id(0), pl.program_id(1), pl.program_id(2)

        def finish(r):
            if has_add:
                r = r + add_scale * add_ref[...]
            o_ref[...] = r.astype(out_dtype)

        def step(a_ref, b_ref):
            if nk == 1:
                finish(_dot(a_ref[...], b_ref[...], dn))
                return
            acc = o_ref if in_place else refs[-1]

            @pl.when(k == 0)
            def _():
                acc[...] = _dot(a_ref[...], b_ref[...], dn)

            @pl.when(k > 0)
            def _():
                acc[...] += _dot(a_ref[...], b_ref[...], dn)

        if n_a == 1 and n_b == 1:
            step(refs[0], refs[1])
        else:
            slab_a = ((i if mode == "tn" else k) // a_per) if n_a > 1 else 0
            slab_b = ((k if mode == "nt" else j) // per) if n_b > 1 else 0
            for sa in range(n_a):
                for sb in range(n_b):
                    pl.when((slab_a == sa) & (slab_b == sb))(functools.partial(step, refs[sa], refs[n_a + sb]))
        if nk > 1 and not (in_place and not has_add):
            @pl.when(k == nk - 1)
            def _():
                finish((o_ref if in_place else refs[-1])[...])

    in_specs = a_specs + b_specs + ([pl.BlockSpec((tm, tn), lambda i, j, k: (i, j))] if has_add else [])
    args = (*a_list, *b_list) + ((add,) if has_add else ())
    if after is not None:
        in_specs.append(pl.BlockSpec(memory_space=pl.ANY))
        args += (after,)
    return pl.pallas_call(
        body, name=name, grid=(m // tm, n // tn, nk),
        in_specs=in_specs, out_specs=o_spec,
        out_shape=jax.ShapeDtypeStruct(o_shape, out_dtype),
        scratch_shapes=[pltpu.VMEM((tm, tn), F32)] if nk > 1 and not in_place else [],
        compiler_params=_params("parallel", "parallel", "arbitrary"),
    )(*args)


def _shift_down(u, s):
    return jnp.where(_iota(u.shape, 0) >= s, pltpu.roll(u, s, 0), 0.0)


def _shift_up(u, s):
    t = u.shape[0]
    return jnp.where(_iota(u.shape, 0) < t - s, pltpu.roll(u, t - s, 0), 0.0)


SLAB = 8


def _rolled(u):
    return pltpu.roll(u, 2, 0), pltpu.roll(u, 1, 0)


def _conv(u, w, rolled=None):
    u2, u1 = _rolled(u) if rolled is None else rolled
    raw = w[0:1] * u2 + w[1:2] * u1 + w[2:3] * u
    head = u[0:SLAB]
    mended = w[0:1] * _shift_down(head, 2) + w[1:2] * _shift_down(head, 1) + w[2:3] * head
    return jnp.concatenate([mended, raw[SLAB:]], axis=0)


def _conv_t(dy, w):
    t = dy.shape[0]
    raw = w[2:3] * dy + w[1:2] * pltpu.roll(dy, t - 1, 0) + w[0:1] * pltpu.roll(dy, t - 2, 0)
    tail = dy[t - SLAB:]
    mended = w[2:3] * tail + w[1:2] * _shift_up(tail, 1) + w[0:1] * _shift_up(tail, 2)
    return jnp.concatenate([raw[:t - SLAB], mended], axis=0)


def _conv_dw(dy, u, rolled=None):
    t = dy.shape[0]
    u2, u1 = _rolled(u) if rolled is None else rolled
    head, tail = dy[0:SLAB], u[t - SLAB:]
    r = _iota(head.shape, 0)
    wrap2 = jnp.sum(jnp.where(r < 2, head * pltpu.roll(tail, 2, 0), 0.0), axis=0, keepdims=True)
    wrap1 = jnp.sum(jnp.where(r < 1, head * pltpu.roll(tail, 1, 0), 0.0), axis=0, keepdims=True)
    d0 = jnp.sum(dy * u2, axis=0, keepdims=True) - wrap2
    d1 = jnp.sum(dy * u1, axis=0, keepdims=True) - wrap1
    d2 = jnp.sum(dy * u, axis=0, keepdims=True)
    r3 = _iota((3, dy.shape[1]), 0)
    return jnp.where(r3 == 0, d0, jnp.where(r3 == 1, d1, d2))


def _sigmoid(x):
    return 0.5 * jnp.tanh(0.5 * x) + 0.5


def _sconv_fwd(proj, w_sc, t, wc):
    nb = wc // LANE

    def body(cb_ref, cc_ref, ch_ref, w_ref, y_ref):
        u = cc_ref[...] * ch_ref[...]
        y_ref[...] = (cb_ref[...] * _conv(u, w_ref[...])).astype(BF16)

    col = lambda off: pl.BlockSpec((t, LANE), lambda j: (0, j + off))
    return pl.pallas_call(
        body, name="sconv_fwd", grid=(nb,),
        in_specs=[col(0), col(nb), col(2 * nb), pl.BlockSpec((3, LANE), lambda j: (0, j))],
        out_specs=pl.BlockSpec((None, t, LANE), lambda j: (0, 0, j)),
        out_shape=jax.ShapeDtypeStruct((2, t, wc), BF16),
        compiler_params=_params("parallel"),
    )(proj, proj, proj, w_sc)


def _sconv_bwd(dy, proj, w_sc, d_proj, t, wc, after=None):
    nb = wc // LANE
    assert nb >= 2, nb

    def body(dy_ref, cb_ref, cc_ref, ch_ref, w_ref, d_proj_in, d_proj_ref, dw_ref, out_s, sems):
        j = pl.program_id(0)
        slot = j % 2

        def copies(step, slot):
            cols = lambda part: pl.ds(pl.multiple_of((step + part * nb) * LANE, LANE), LANE)
            return [pltpu.make_async_copy(out_s.at[slot, part], d_proj_ref.at[:, cols(part)], sems.at[slot, part])
                    for part in range(3)]

        @pl.when(j >= 2)
        def _():
            for cp in copies(j - 2, slot):
                cp.wait()

        cc, ch, w, d = cc_ref[...], ch_ref[...], w_ref[...], dy_ref[...]
        u = cc * ch
        ru = _rolled(u)
        out_s[slot, 0] = (d * _conv(u, w, ru)).astype(BF16)
        dcu = d * cb_ref[...]
        dw_ref[...] = _conv_dw(dcu, u, ru)
        du = _conv_t(dcu, w)
        out_s[slot, 1] = (du * ch).astype(BF16)
        out_s[slot, 2] = (du * cc).astype(BF16)
        for cp in copies(j, slot):
            cp.start()

        @pl.when(j == nb - 1)
        def _():
            for cp in copies(j - 1, 1 - slot) + copies(j, slot):
                cp.wait()

    col = lambda off: pl.BlockSpec((t, LANE), lambda j: (0, j + off))
    body, in_specs, args = _placed(
        after, body, [col(0), col(0), col(nb), col(2 * nb), pl.BlockSpec((3, LANE), lambda j: (0, j)),
                      pl.BlockSpec(memory_space=pl.ANY)],
        (dy, proj, proj, proj, w_sc, d_proj))
    return pl.pallas_call(
        body, name="sconv_bwd", grid=(nb,),
        in_specs=in_specs,
        out_specs=[pl.BlockSpec(memory_space=pl.ANY), pl.BlockSpec((3, LANE), lambda j: (0, j))],
        out_shape=[jax.ShapeDtypeStruct(d_proj.shape, d_proj.dtype), jax.ShapeDtypeStruct((3, wc), F32)],
        input_output_aliases={len(args) - 1: 0},
        scratch_shapes=[pltpu.VMEM((2, 3, t, LANE), BF16), pltpu.SemaphoreType.DMA((2, 3))],
        compiler_params=_params("arbitrary"),
    )(*args)


def _gates_prep(proj, bias_tile, t, gate_tile):
    def body(g_ref, b_ref, o_ref):
        g = g_ref[...] + b_ref[...]
        lane = _iota(g.shape, 1)
        is_f = (lane >= NH) & (lane < 2 * NH)
        lf = jnp.minimum(g, 0.0) - jnp.log(1.0 + jnp.exp(-jnp.abs(g)))
        c = jnp.where(is_f, lf, 0.0)
        r = _iota(g.shape, 0) % CHUNK
        s = 1
        while s < CHUNK:
            c = c + jnp.where(r >= s, pltpu.roll(c, s, 0), 0.0)
            s *= 2
        o_ref[...] = jnp.where(is_f, c, jnp.where(lane < NH, g, 0.0))

    return pl.pallas_call(
        body, name="gates_prep", grid=(1,),
        in_specs=[pl.BlockSpec((t, LANE), lambda i: (0, gate_tile)), pl.BlockSpec((1, LANE), lambda i: (0, 0))],
        out_specs=pl.BlockSpec((t, LANE), lambda i: (0, 0)),
        out_shape=jax.ShapeDtypeStruct((t, LANE), F32),
        compiler_params=_params("arbitrary"),
    )(proj, bias_tile)


def _gates_bwd(dgate, proj, bias_tile, d_proj, t, gate_tile):
    def body(dg_ref, g_ref, b_ref, d_proj_in, o_ref, s_ref):
        g = g_ref[...] + b_ref[...]
        lane = _iota(g.shape, 1)
        r = _iota(g.shape, 0) % CHUNK
        dsig = 1.0 - _sigmoid(g)
        out = jnp.zeros(g.shape, F32)
        for h in range(NH):
            d = dg_ref[h]
            c = d
            s = 1
            while s < CHUNK:
                c = c + jnp.where(r + s < CHUNK, pltpu.roll(c, t - s, 0), 0.0)
                s *= 2
            di = jnp.broadcast_to(d[:, 0:1], g.shape)
            db = jnp.broadcast_to(c[:, 1:2], g.shape)
            out = out + jnp.where(lane == h, di, 0.0) + jnp.where(lane == NH + h, db * dsig, 0.0)
        o_ref[...] = out.astype(BF16)
        s_ref[...] = jnp.sum(out, axis=0, keepdims=True)

    return pl.pallas_call(
        body, name="gates_bwd", grid=(1,),
        in_specs=[pl.BlockSpec((NH, t, LANE), lambda i: (0, 0, 0)),
                  pl.BlockSpec((t, LANE), lambda i: (0, gate_tile)), pl.BlockSpec((1, LANE), lambda i: (0, 0)),
                  pl.BlockSpec(memory_space=pl.ANY)],
        out_specs=[pl.BlockSpec((t, LANE), lambda i: (0, gate_tile)), pl.BlockSpec((1, LANE), lambda i: (0, 0))],
        out_shape=[jax.ShapeDtypeStruct(d_proj.shape, d_proj.dtype), jax.ShapeDtypeStruct((1, LANE), F32)],
        input_output_aliases={3: 0},
        compiler_params=_params("arbitrary"),
    )(dgate, proj, bias_tile, d_proj)


def _in_turn(heads):
    while heads:
        heads = [g for g in heads if next(g, heads) is not heads]


def _chunk_gates(gc, gr, h, mprev):
    L = CHUNK
    icol, bcol = gc[:, h:h + 1], gc[:, h + NH:h + NH + 1]
    irow, brow = gr[h:h + 1, :], gr[h + NH:h + NH + 1, :]
    tri = _iota((L, L), 0) >= _iota((L, L), 1)
    log_d = jnp.where(tri, bcol - brow + irow, -jnp.inf)
    inter = bcol + mprev
    mt = jnp.maximum(inter, jnp.max(log_d, axis=1, keepdims=True))
    dw = jnp.exp(log_d - mt)
    iw = jnp.exp(inter - mt)
    g = brow[:, L - 1:L]
    wlog_col = g - bcol + icol
    wlog_row = g - brow + irow
    mnew = jnp.maximum(g + mprev, jnp.max(wlog_row, axis=1, keepdims=True))
    wcol = jnp.exp(wlog_col - mnew)
    decay = jnp.exp(g + mprev - mnew)
    return dw, iw, mt, wcol, decay, mnew


def _mlstm_fwd(proj, gcol, grow, t, wc, dh):
    nc = t // CHUNK
    wm = NH * dh
    assert wc == wm, (wc, wm)
    qoff = 3 * wc // wm
    scale = dh ** -0.5

    def body(q_ref, k_ref, v_ref, gc_ref, gr_ref, h_ref, cs_ref, ns_ref, c_s, n_s, m_s):
        @pl.when(pl.program_id(0) == 0)
        def _():
            c_s[...] = jnp.zeros_like(c_s)
            n_s[...] = jnp.zeros_like(n_s)
            m_s[...] = jnp.zeros_like(m_s)

        gc, gr = gc_ref[...], gr_ref[0]
        done = [None] * NH

        def head(h):
            cols = slice(h * dh, (h + 1) * dh)
            mprev = m_s[h, 0:1, 0:1]
            cprev = c_s[h]
            n8 = n_s[h]
            nprev = n8[0:1]
            qs = q_ref[:, cols] * scale
            k = k_ref[:, cols]
            qs_b, k_b, v_b = qs.astype(BF16), k.astype(BF16), v_ref[:, cols].astype(BF16)
            qk = _dot(qs_b, k_b, _NT)
            yield
            q_c = _dot(qs_b, cprev.astype(BF16))
            yield
            dw, iw, mt, wcol, decay, mnew = _chunk_gates(gc, gr, h, mprev)
            yield
            s = qk * dw
            wk = wcol * k
            num = _dot(s.astype(BF16), v_b) + iw * q_c
            yield
            c_new = decay * cprev + _dot(wk.astype(BF16), v_b, _TN)
            yield
            den = jnp.sum(s, axis=1, keepdims=True) + iw * jnp.sum(qs * nprev, axis=1, keepdims=True)
            done[h] = (cprev, jnp.where(_iota(n8.shape, 0) == 1, mprev, n8),
                       num / jnp.maximum(jnp.abs(den), jnp.exp(-mt)), c_new,
                       decay * n8 + jnp.sum(wk, axis=0, keepdims=True), mnew)

        _in_turn([head(h) for h in range(NH)])
        for h, (c_old, n_old, h_out, c_new, n_new, m_new) in enumerate(done):
            cs_ref[h] = c_old
            ns_ref[h] = n_old
            h_ref[:, h * dh:(h + 1) * dh] = h_out
            c_s[h] = c_new
            n_s[h] = n_new
            m_s[h] = jnp.broadcast_to(m_new, m_s.shape[1:])

    grp = lambda off: pl.BlockSpec((CHUNK, wm), lambda c: (c, qoff + off))
    return pl.pallas_call(
        body, name="mlstm_fwd", grid=(nc,),
        in_specs=[grp(0), grp(1), grp(2),
                  pl.BlockSpec((CHUNK, LANE), lambda c: (c, 0)),
                  pl.BlockSpec((1, 8, CHUNK), lambda c: (c, 0, 0))],
        out_specs=[pl.BlockSpec((CHUNK, wm), lambda c: (c, 0)),
                   pl.BlockSpec((NH, None, dh, dh), lambda c: (0, c, 0, 0)),
                   pl.BlockSpec((NH, None, 8, dh), lambda c: (0, c, 0, 0))],
        out_shape=[jax.ShapeDtypeStruct((t, wm), F32),
                   jax.ShapeDtypeStruct((NH, nc, dh, dh), F32),
                   jax.ShapeDtypeStruct((NH, nc, 8, dh), F32)],
        scratch_shapes=[pltpu.VMEM((NH, dh, dh), F32), pltpu.VMEM((NH, 8, dh), F32), pltpu.VMEM((NH, 8, LANE), F32)],
        compiler_params=_params("arbitrary"),
    )(proj, proj, proj, gcol, grow)


def _mlstm_bwd(proj, gcol, grow, hval, dh_in, cs, ns, d_proj, t, wc, dh):
    nc = t // CHUNK
    wm = NH * dh
    assert wc == wm, (wc, wm)
    qoff = 3 * wc // wm
    scale = dh ** -0.5
    L = CHUNK

    def body(q_ref, k_ref, v_ref, gc_ref, gr_ref, h_ref, dh_ref, cs_ref, ns_ref, d_proj_in,
             dqkv_ref, dg_ref, dc_s, dn_s):
        @pl.when(pl.program_id(0) == 0)
        def _():
            dc_s[...] = jnp.zeros_like(dc_s)
            dn_s[...] = jnp.zeros_like(dn_s)

        gc, gr = gc_ref[...], gr_ref[0]
        eye = _iota((L, L), 0) == _iota((L, L), 1)
        lane = _iota((L, LANE), 1)
        last = _iota((L, 1), 0) == L - 1
        done = [None] * NH

        def head(h):
            cols = slice(h * dh, (h + 1) * dh)
            ns8 = ns_ref[h]
            nprev = ns8[0:1]
            mprev = ns8[1:2, 0:1]
            cprev = cs_ref[h]
            dcn = dc_s[h]
            dn8 = dn_s[h]
            dnn = dn8[0:1]

            qs = q_ref[:, cols] * scale
            k = k_ref[:, cols]
            qs_b, k_b, v_b = qs.astype(BF16), k.astype(BF16), v_ref[:, cols].astype(BF16)
            qk = _dot(qs_b, k_b, _NT)
            yield
            dw, iw, mt, wcol, decay, _ = _chunk_gates(gc, gr, h, mprev)
            yield
            s = qk * dw
            den = jnp.sum(s, axis=1, keepdims=True) + iw * jnp.sum(qs * nprev, axis=1, keepdims=True)
            emt = jnp.exp(-mt)
            r = 1.0 / jnp.maximum(jnp.abs(den), emt)
            dout = dh_ref[:, cols]
            dnum = dout * r
            dden = (-jnp.sum(dout * h_ref[:, cols], axis=1, keepdims=True) * r
                    * jnp.where(jnp.abs(den) > emt, jnp.sign(den), 0.0))
            dnum_b = dnum.astype(BF16)
            cprev_b = cprev.astype(BF16)
            dcn_b = dcn.astype(BF16)
            yield

            g_raw = _dot(dnum_b, v_b, _NT)
            yield
            q_inter = _dot(dnum_b, cprev_b, _NT)
            yield
            k_raw = _dot(v_b, dcn_b, _NT)
            yield
            gd = (g_raw + dden) * dw
            gd_b = gd.astype(BF16)
            dqs_inter = iw * (q_inter + dden * nprev)
            dk_inter = wcol * (k_raw + dnn)
            wk = wcol * k
            iq = iw * qs
            dqs = _dot(gd_b, k_b) + dqs_inter
            yield
            dk = _dot(gd_b, qs_b, _TN) + dk_inter
            yield
            dv = _dot(s.astype(BF16), dnum_b, _TN) + _dot(wk.astype(BF16), dcn_b)
            yield
            dc_new = decay * dcn + _dot(iq.astype(BF16), dnum_b, _TN)
            yield

            e = gd * qk
            e_cols = jnp.sum(jnp.where(eye, jnp.sum(e, axis=0, keepdims=True), 0.0), axis=1, keepdims=True)
            yield
            k_inter = jnp.sum(k * dk_inter, axis=1, keepdims=True)
            rq = jnp.sum(e, axis=1, keepdims=True) + jnp.sum(qs * dqs_inter, axis=1, keepdims=True)
            rk = e_cols + k_inter
            hsum = jnp.sum(k_inter, axis=0, keepdims=True)
            jdec = decay * (jnp.sum(jnp.sum(dcn * cprev, axis=1, keepdims=True), axis=0, keepdims=True)
                            + jnp.sum(dnn * nprev, axis=1, keepdims=True))
            db = rq - rk + jnp.where(last, hsum + jdec, 0.0)
            done[h] = (jnp.where(lane == 0, rk, jnp.where(lane == 1, db, 0.0)),
                       (dqs * scale).astype(BF16), dk.astype(BF16), dv.astype(BF16), dc_new,
                       decay * dn8 + jnp.sum(iq * dden, axis=0, keepdims=True))

        _in_turn([head(h) for h in range(NH)])
        for h, (dgate, dq, dk, dv, dc_new, dn_new) in enumerate(done):
            dg_ref[h] = dgate
            for part, grad in enumerate((dq, dk, dv)):
                dqkv_ref[:, part * wm + h * dh:part * wm + (h + 1) * dh] = grad
            dc_s[h] = dc_new
            dn_s[h] = dn_new

    rc = lambda c: nc - 1 - c
    grp = lambda off: pl.BlockSpec((L, wm), lambda c: (rc(c), qoff + off))
    hm = pl.BlockSpec((L, wm), lambda c: (rc(c), 0))
    assert qoff % 3 == 0, qoff
    return pl.pallas_call(
        body, name="mlstm_bwd", grid=(nc,),
        in_specs=[grp(0), grp(1), grp(2),
                  pl.BlockSpec((L, LANE), lambda c: (rc(c), 0)),
                  pl.BlockSpec((1, 8, L), lambda c: (rc(c), 0, 0)),
                  hm, hm,
                  pl.BlockSpec((NH, None, dh, dh), lambda c: (0, rc(c), 0, 0)),
                  pl.BlockSpec((NH, None, 8, dh), lambda c: (0, rc(c), 0, 0)),
                  pl.BlockSpec(memory_space=pl.ANY)],
        out_specs=[pl.BlockSpec((L, 3 * wm), lambda c: (rc(c), qoff // 3)),
                   pl.BlockSpec((NH, L, LANE), lambda c: (0, rc(c), 0))],
        out_shape=[jax.ShapeDtypeStruct(d_proj.shape, d_proj.dtype), jax.ShapeDtypeStruct((NH, t, LANE), F32)],
        input_output_aliases={9: 0},
        scratch_shapes=[pltpu.VMEM((NH, dh, dh), F32), pltpu.VMEM((NH, 8, dh), F32)],
        compiler_params=_params("arbitrary"),
    )(proj, proj, proj, gcol, grow, hval, dh_in, cs, ns, d_proj)


def _head_norm(hv):
    mu = jnp.mean(hv, axis=1, keepdims=True)
    hc = hv - mu
    rstd = lax.rsqrt(jnp.mean(hc * hc, axis=1, keepdims=True) + HN_EPS)
    return hc * rstd, rstd


def _hnorm_fwd(hval, proj, gain, y, t, wc, dh, tr=512):
    ooff = 3 * wc // dh + 3 * NH
    tr = min(tr, t)

    def body(h_ref, o_ref, g_ref, y_in, y_ref):
        hhat, _ = _head_norm(h_ref[...])
        y_ref[...] = (_sigmoid(o_ref[...]) * hhat * g_ref[...]).astype(BF16)

    return pl.pallas_call(
        body, name="hnorm_fwd", grid=(t // tr, NH),
        in_specs=[pl.BlockSpec((tr, dh), lambda i, h: (i, h)),
                  pl.BlockSpec((tr, dh), lambda i, h: (i, ooff + h)),
                  pl.BlockSpec((1, dh), lambda i, h: (0, h)),
                  pl.BlockSpec(memory_space=pl.ANY)],
        out_specs=pl.BlockSpec((None, tr, dh), lambda i, h: (1, i, h)),
        out_shape=jax.ShapeDtypeStruct(y.shape, BF16),
        input_output_aliases={3: 0},
        compiler_params=_params("parallel", "parallel"),
    )(hval, proj, gain, y)


def _hnorm_bwd(dy, hval, proj, gain, t, wc, dh, tr=512):
    ooff = 3 * wc // dh + 3 * NH
    tr = min(tr, t)
    yoff = wc // dh

    def body(dy_ref, h_ref, o_ref, g_ref, do_ref, dh_ref, dg_ref):
        i = pl.program_id(1)
        hhat, rstd = _head_norm(h_ref[...])
        gain_v = g_ref[...]
        sig = _sigmoid(o_ref[...])
        d = dy_ref[...]
        do_ref[...] = (d * hhat * gain_v * sig * (1.0 - sig)).astype(BF16)
        dhn = d * sig
        part = jnp.sum(dhn * hhat, axis=0, keepdims=True)

        @pl.when(i == 0)
        def _():
            dg_ref[...] = part

        @pl.when(i > 0)
        def _():
            dg_ref[...] += part

        dhat = dhn * gain_v
        dh_ref[...] = rstd * (dhat - jnp.mean(dhat, axis=1, keepdims=True)
                              - hhat * jnp.mean(dhat * hhat, axis=1, keepdims=True))

    blk = lambda off: pl.BlockSpec((tr, dh), lambda h, i: (i, off + h))
    return pl.pallas_call(
        body, name="hnorm_bwd", grid=(NH, t // tr),
        in_specs=[blk(yoff), blk(0), blk(ooff), pl.BlockSpec((1, dh), lambda h, i: (0, h))],
        out_specs=[blk(ooff), blk(0), pl.BlockSpec((1, dh), lambda h, i: (0, h))],
        out_shape=[jax.ShapeDtypeStruct(proj.shape, BF16), jax.ShapeDtypeStruct((t, NH * dh), F32),
                   jax.ShapeDtypeStruct((1, NH * dh), F32)],
        compiler_params=_params("parallel", "arbitrary"),
    )(dy, hval, proj, gain)


def _ln_stats(z):
    mu = jnp.mean(z, axis=1, keepdims=True)
    zc = z - mu
    rstd = lax.rsqrt(jnp.mean(zc * zc, axis=1, keepdims=True) + LN_EPS)
    return zc * rstd, rstd


def _ln_bwd(dy, xhat, rstd, g):
    dxh = dy * g
    return rstd * (dxh - jnp.mean(dxh, axis=1, keepdims=True) - xhat * jnp.mean(dxh * xhat, axis=1, keepdims=True))


def _accum(ref, i, part):
    @pl.when(i == 0)
    def _():
        ref[...] = part

    @pl.when(i > 0)
    def _():
        ref[...] += part


def _ln1_fwd(x, mix, g, b, tr=256, after=None):
    t, d = x.shape

    def body(x_ref, m_ref, g_ref, b_ref, xh_ref, rs_ref, xb_ref):
        xhat, rstd = _ln_stats(ALPHA * x_ref[...] + m_ref[...])
        xh_ref[...] = xhat
        rs_ref[...] = rstd
        xb_ref[...] = (xhat * g_ref[...] + b_ref[...]).astype(BF16)

    row = pl.BlockSpec((tr, d), lambda i: (i, 0))
    vec = pl.BlockSpec((1, d), lambda i: (0, 0))
    body, in_specs, args = _placed(after, body, [row, row, vec, vec], (x, mix, g, b))
    return pl.pallas_call(
        body, name="ln1_fwd", grid=(t // tr,),
        in_specs=in_specs,
        out_specs=[row, pl.BlockSpec((tr, 1), lambda i: (i, 0)), row],
        out_shape=[jax.ShapeDtypeStruct((t, d), F32), jax.ShapeDtypeStruct((t, 1), F32),
                   jax.ShapeDtypeStruct((t, d), BF16)],
        compiler_params=_params("parallel"),
    )(*args)


def _ln2_loss(xhat1, g1, b1, ff, target, g2, b2, tr=256):
    t, d = ff.shape

    def body(xh_ref, g1_ref, b1_ref, f_ref, t_ref, g_ref, b_ref, dz_ref, dzb_ref, dg_ref, db_ref, l_ref):
        i = pl.program_id(0)
        x1 = xh_ref[...] * g1_ref[...] + b1_ref[...]
        xhat, rstd = _ln_stats(ALPHA * x1 + f_ref[...])
        gv = g_ref[...]
        e = xhat * gv + b_ref[...] - t_ref[...]
        lsum = jnp.sum(jnp.sum(e * e, axis=1, keepdims=True), axis=0, keepdims=True) * (0.5 / d)
        dy = e * (1.0 / d)
        _accum(dg_ref, i, jnp.sum(dy * xhat, axis=0, keepdims=True))
        _accum(db_ref, i, jnp.sum(dy, axis=0, keepdims=True))
        _accum(l_ref, i, jnp.broadcast_to(lsum, l_ref.shape))
        dz = _ln_bwd(dy, xhat, rstd, gv)
        dz_ref[...] = dz
        dzb_ref[...] = dz.astype(BF16)

    row = pl.BlockSpec((tr, d), lambda i: (i, 0))
    vec = pl.BlockSpec((1, d), lambda i: (0, 0))
    return pl.pallas_call(
        body, name="ln2_loss", grid=(t // tr,),
        in_specs=[row, vec, vec, row, row, vec, vec],
        out_specs=[row, row, vec, vec, pl.BlockSpec((8, LANE), lambda i: (0, 0))],
        out_shape=[jax.ShapeDtypeStruct((t, d), F32), jax.ShapeDtypeStruct((t, d), BF16),
                   jax.ShapeDtypeStruct((1, d), F32), jax.ShapeDtypeStruct((1, d), F32),
                   jax.ShapeDtypeStruct((8, LANE), F32)],
        compiler_params=_params("arbitrary"),
    )(xhat1, g1, b1, ff, target, g2, b2)


def _ln1_bwd(dz2, dffn, xhat1, rstd1, g1, tr=256, after=None):
    t, d = dz2.shape

    def body(a_ref, f_ref, xh_ref, rs_ref, g_ref, dz_ref, dzb_ref, dg_ref, db_ref):
        i = pl.program_id(0)
        dy = ALPHA * a_ref[...] + f_ref[...]
        xhat = xh_ref[...]
        _accum(dg_ref, i, jnp.sum(dy * xhat, axis=0, keepdims=True))
        _accum(db_ref, i, jnp.sum(dy, axis=0, keepdims=True))
        dz = _ln_bwd(dy, xhat, rs_ref[...], g_ref[...])
        dz_ref[...] = dz
        dzb_ref[...] = dz.astype(BF16)

    row = pl.BlockSpec((tr, d), lambda i: (i, 0))
    vec = pl.BlockSpec((1, d), lambda i: (0, 0))
    body, in_specs, args = _placed(after, body, [row, row, row, pl.BlockSpec((tr, 1), lambda i: (i, 0)), vec],
                                   (dz2, dffn, xhat1, rstd1, g1))
    return pl.pallas_call(
        body, name="ln1_bwd", grid=(t // tr,),
        in_specs=in_specs,
        out_specs=[row, row, vec, vec],
        out_shape=[jax.ShapeDtypeStruct((t, d), F32), jax.ShapeDtypeStruct((t, d), BF16),
                   jax.ShapeDtypeStruct((1, d), F32), jax.ShapeDtypeStruct((1, d), F32)],
        compiler_params=_params("arbitrary"),
    )(*args)


def _ffn_act_fwd(hid0, w_fc, b_fc, t, dff, after=None):
    nb = dff // LANE

    def body(hv_ref, hg_ref, wv_ref, wg_ref, bv_ref, bg_ref, a_ref):
        val = _conv(hv_ref[...], wv_ref[...]) + bv_ref[...]
        gate = _conv(hg_ref[...], wg_ref[...]) + bg_ref[...]
        a_ref[...] = (gate * _sigmoid(gate) * val).astype(BF16)

    col = lambda off: pl.BlockSpec((t, LANE), lambda j: (0, j + off))
    w3 = lambda off: pl.BlockSpec((3, LANE), lambda j: (0, j + off))
    w1 = lambda off: pl.BlockSpec((1, LANE), lambda j: (0, j + off))
    body, in_specs, args = _placed(after, body, [col(0), col(nb), w3(0), w3(nb), w1(0), w1(nb)],
                                   (hid0, hid0, w_fc, w_fc, b_fc, b_fc))
    return pl.pallas_call(
        body, name="ffn_act_fwd", grid=(nb,),
        in_specs=in_specs,
        out_specs=col(0),
        out_shape=jax.ShapeDtypeStruct((t, dff), BF16),
        compiler_params=_params("parallel"),
    )(*args)


def _ffn_act_bwd(da, hid0, w_fc, b_fc, t, dff, after=None):
    nb = dff // LANE

    def body(da_ref, hv_ref, hg_ref, wv_ref, wg_ref, bv_ref, bg_ref,
             dhv_ref, dhg_ref, dwv_ref, dwg_ref, dbv_ref, dbg_ref):
        hv, hg, wv, wg = hv_ref[...], hg_ref[...], wv_ref[...], wg_ref[...]
        rv, rg = _rolled(hv), _rolled(hg)
        val = _conv(hv, wv, rv) + bv_ref[...]
        gate = _conv(hg, wg, rg) + bg_ref[...]
        sig = _sigmoid(gate)
        d = da_ref[...]
        dsig = d * sig
        dval = dsig * gate
        dgate = dsig * val * (1.0 + gate * (1.0 - sig))
        dhv_ref[...] = _conv_t(dval, wv).astype(BF16)
        dhg_ref[...] = _conv_t(dgate, wg).astype(BF16)
        dwv_ref[...] = _conv_dw(dval, hv, rv)
        dwg_ref[...] = _conv_dw(dgate, hg, rg)
        dbv_ref[...] = jnp.sum(dval, axis=0, keepdims=True)
        dbg_ref[...] = jnp.sum(dgate, axis=0, keepdims=True)

    col = lambda off: pl.BlockSpec((t, LANE), lambda j: (0, j + off))
    w3 = lambda off: pl.BlockSpec((3, LANE), lambda j: (0, j + off))
    w1 = lambda off: pl.BlockSpec((1, LANE), lambda j: (0, j + off))
    s3 = jax.ShapeDtypeStruct((3, dff), F32)
    s1 = jax.ShapeDtypeStruct((1, dff), F32)
    body, in_specs, args = _placed(after, body, [col(0), col(0), col(nb), w3(0), w3(nb), w1(0), w1(nb)],
                                   (da, hid0, hid0, w_fc, w_fc, b_fc, b_fc))
    return pl.pallas_call(
        body, name="ffn_act_bwd", grid=(nb,),
        in_specs=in_specs,
        out_specs=[col(0), col(0), w3(0), w3(0), w1(0), w1(0)],
        out_shape=[jax.ShapeDtypeStruct((t, dff), BF16)] * 2 + [s3, s3, s1, s1],
        compiler_params=_params("parallel"),
    )(*args)


class _Ready:
    def __init__(self, **weights):
        self.weights = weights

    def begin(self, after):
        return None

    def forward(self, name, after):
        return None

    def get(self, name, after):
        return self.weights[name]


class _Kept:
    def __init__(self):
        self.grads = {}

    def start(self, name, grad):
        self.grads[name] = grad
        return None

    def relay(self, name, after):
        return None

    def early(self, small, loss, after):
        return None

    def meanwhile(self, after):
        return None


def _behind(a, token):
    return a if token is None else a + token[0:1, 0:1].reshape((1,) * a.ndim)


def _local_step(x, target, w_in, b_gates, w_sc, gain, w_out, ln1_g, ln1_b, w_up, w_fc, b_fc, w_down, ln2_g, ln2_b,
                gx=None, wx=None, x_b=None):
    t, d = x.shape
    wc = d // 2
    dh = (d - wc) // NH
    wm = NH * dh
    dff = w_fc.shape[1] // 2
    if wx is None:
        wx = _Ready(w_out=w_out, w_up=w_up, w_down=w_down)
    ninp = 3 * wc + 4 * wm + LANE
    nin = 3 * wc + 4 * wm
    gate_tile = nin // LANE
    nc = t // CHUNK
    bias_tile = jnp.pad(b_gates, ((0, 0), (0, LANE - 2 * NH)))

    if x_b is None:
        x_b = x.astype(BF16)
    proj = _matmul(x_b, w_in, "nt", F32, "proj", tm=512, tn=2432, tk=d, n=ninp, after=wx.begin(w_in))
    y = _sconv_fwd(proj, w_sc, t, wc)
    gcol = _gates_prep(proj, bias_tile, t, gate_tile)
    grow = gcol[:, :8].T.reshape(8, nc, CHUNK).transpose(1, 0, 2)
    hval, cs, ns = _mlstm_fwd(proj, gcol, grow, t, wc, dh)
    y = _hnorm_fwd(hval, proj, gain, y, t, wc, dh)
    tok = wx.forward("w_out", y)
    w_out = wx.get("w_out", tok)
    mix = _matmul(y, w_out, "nn", F32, "out_proj", tm=512, tn=1024, tk=wc, a_blocked=True, after=tok)
    xhat1, rstd1, x1_b = _ln1_fwd(x, mix, ln1_g, ln1_b, after=wx.forward("w_up", mix))
    w_up = wx.get("w_up", x1_b)
    wsl = w_up.shape[2]
    hid0 = _matmul(x1_b, w_up, "nn", F32, "ffn_up", tm=1024, tn=wsl, tk=d, b_blocked=True)
    act = _ffn_act_fwd(hid0, w_fc, b_fc, t, dff, after=wx.forward("w_down", hid0))
    w_down = wx.get("w_down", act)
    ff = _matmul(act, w_down, "nn", F32, "ffn_down", tm=1024, tn=512, tk=dff)
    dz2, dz2_b, d_ln2_g, d_ln2_b, loss = _ln2_loss(xhat1, ln1_g, ln1_b, ff, target, ln2_g, ln2_b)

    if gx is None:
        gx = _Kept()
    d_w_down = _matmul(act, dz2_b, "tn", BF16, "ffn_down_dw", tm=1408, tn=1024, tk=t)
    d_act = _matmul(dz2_b, w_down, "nt", F32, "ffn_down_dx", tm=2048, tn=512, tk=d, after=gx.start("w_down", d_w_down))
    *d_hid0, dwv, dwg, dbv, dbg = _ffn_act_bwd(d_act, hid0, w_fc, b_fc, t, dff, after=gx.relay("w_down", d_act))
    d_w_fc = jnp.concatenate([dwv, dwg], axis=1)
    d_b_fc = jnp.concatenate([dbv, dbg], axis=1)
    d_hid0 = tuple(d_hid0[:2])
    d_w_up = _matmul(x1_b, d_hid0, "tn", BF16, "ffn_up_dw", tm=1024, tn=wsl, tk=t, o_width=wsl)
    d_x1_ffn = _matmul(d_hid0, w_up, "nt", F32, "ffn_up_dx", tm=1024, tn=1024, tk=wsl, b_blocked=True,
                       after=gx.start("w_up", d_w_up))
    dz1, dz1_b, d_ln1_g, d_ln1_b = _ln1_bwd(dz2, d_x1_ffn, xhat1, rstd1, ln1_g, after=gx.relay("w_up", d_x1_ffn))

    d_w_out = _matmul(y, dz1_b, "tn", BF16, "out_proj_dw", tm=1024, tn=1024, tk=t, a_blocked=True)
    dy = _matmul(dz1_b, w_out, "nt", F32, "out_proj_dx", tm=1024, tn=1024, tk=d, after=gx.start("w_out", d_w_out))
    d_proj, d_hval, d_gain = _hnorm_bwd(dy, hval, proj, gain, t, wc, dh)
    d_proj, d_w_sc = _sconv_bwd(dy, proj, w_sc, d_proj, t, wc, after=gx.relay("w_out", dy))
    d_proj, dgate = _mlstm_bwd(proj, gcol, grow, hval, d_hval, cs, ns, d_proj, t, wc, dh)
    d_proj, d_b_gates = _gates_bwd(dgate, proj, bias_tile, d_proj, t, gate_tile)
    small = dict(b_gates=d_b_gates[:, :2 * NH], w_sc_conv=d_w_sc, mh_gain=d_gain, ln1_g=d_ln1_g, ln1_b=d_ln1_b,
                 w_ffn_conv=d_w_fc, b_ffn_conv=d_b_fc, ln2_g=d_ln2_g, ln2_b=d_ln2_b)
    d_w_in = _matmul(d_proj, x_b, "tn", BF16, "proj_dw", tm=2432, tn=1024, tk=t,
                     after=gx.early(small, loss, d_proj))
    token = gx.start("w_in", d_w_in)
    token = gx.relay("w_in", gx.meanwhile(token))
    grad_x = _matmul(d_proj, w_in, "nn", F32, "proj_dx", tm=512, tn=512, tk=ninp, add=dz1, add_scale=ALPHA, after=token)
    return loss, grad_x, small, gx


HBM = pl.BlockSpec(memory_space=pltpu.HBM)


def _place():
    return lax.axis_index("x"), lax.axis_index("y"), lax.axis_index("c")


def _index(p):
    return 4 * p[0] + 2 * p[1] + p[2]


def _all_gather(arrs, name):
    n = len(arrs)

    def body(*refs):
        ins, outs = refs[:n], refs[n:2 * n]
        send_sems, recv_sems, local_sems = refs[2 * n:]
        x, y, c = _place()
        me, sibling = (x, y, c), (x, y, 1 - c)
        chips = [(1 - x, y), (x, 1 - y), (1 - x, 1 - y)]

        def copy(a, k, block, to, own=False):
            dst = outs[a].at[_index(block)]
            return pltpu.make_async_remote_copy(
                src_ref=ins[a] if own else dst, dst_ref=dst,
                send_sem=send_sems.at[k * n + a], recv_sem=recv_sems.at[k * n + a],
                device_id=to, device_id_type=MESH)

        mine = [pltpu.make_async_copy(ins[a], outs[a].at[_index(me)], local_sems.at[a]) for a in range(n)]
        for cp in mine:
            cp.start()
        first = []
        for a in range(n):
            first.append(copy(a, 0, me, sibling, own=True))
            first += [copy(a, 1 + j, me, (*chip, c), own=True) for j, chip in enumerate(chips)]
        for cp in first:
            cp.start()
        passed = []
        for j, chip in enumerate(chips):
            for a in range(n):
                copy(a, 1 + j, (*chip, c), me).wait_recv()
                cp = copy(a, 4 + j, (*chip, c), sibling)
                cp.start()
                passed.append(cp)
        for a in range(n):
            copy(a, 0, sibling, me).wait_recv()
            for j, chip in enumerate(chips):
                copy(a, 4 + j, (*chip, 1 - c), me).wait_recv()
        for cp in first + passed:
            cp.wait_send()
        for cp in mine:
            cp.wait()

    return pl.pallas_call(
        body, name=name, in_specs=[HBM] * n, out_specs=[HBM] * n,
        out_shape=[jax.ShapeDtypeStruct((N_DEV,) + a.shape, a.dtype) for a in arrs],
        scratch_shapes=[pltpu.SemaphoreType.DMA((7 * n,)), pltpu.SemaphoreType.DMA((7 * n,)),
                        pltpu.SemaphoreType.DMA((n,))],
    )(*arrs)


SEM = pl.BlockSpec(memory_space=pltpu.SEMAPHORE)
EFFECT = pltpu.SideEffectType.DATAFLOW_SIDE_EFFECTING


def _chips(x, y):
    return [(1 - x, y), (x, 1 - y), (1 - x, 1 - y)]


N_CHIP = N_DEV // 2


def _pair_route(x, y, c):
    return [((x, y, 1 - c), 2 * q + (1 - c), q, q) for q in range(N_CHIP)]


def _chip_route(x, y, c):
    mine = 2 * x + y
    return [((*chip, c), 2 * chip[0] + chip[1], mine, 2 * chip[0] + chip[1]) for chip in _chips(x, y)]


def _exchange_pieces(g_ref, land_ref, width, tail):
    if not tail:
        return [(lambda i: g_ref.at[i], lambda s: land_ref.at[s])]
    rows = lambda i, n: pl.ds(pl.multiple_of(i * width, IN_TAIL), n)
    return [(lambda i: g_ref.at[rows(i, width), :], lambda s: land_ref.at[s, pl.ds(0, width), :]),
            (lambda i: g_ref.at[rows(i + 1, IN_TAIL), :], lambda s: land_ref.at[s, pl.ds(width, IN_TAIL), :])]


def _chip_slot(x, y, c):
    return 2 * x + y


def _exchange_start(grad, route, tail, name, own_slot=None):
    width = IN_SLAB if tail else grad.shape[1]
    n_p = 2 if tail else 1
    n_c = len(route(0, 0, 0))
    land_shape = (N_CHIP, width + (IN_TAIL if tail else 0), grad.shape[-1])
    assert not (tail and own_slot)

    def body(g_ref, land_ref, send_sems, recv_sems, g_thru, land_thru, token):
        for j, (peer, slab, slot, _) in enumerate(route(*_place())):
            for p, (src, dst) in enumerate(_exchange_pieces(g_ref, land_ref, width, tail)):
                pltpu.make_async_remote_copy(src_ref=src(slab), dst_ref=dst(slot), send_sem=send_sems.at[j * n_p + p],
                                             recv_sem=recv_sems.at[j * n_p + p], device_id=peer,
                                             device_id_type=MESH).start()
        if own_slot:
            mine = own_slot(*_place())
            pltpu.make_async_copy(g_ref.at[mine], land_ref.at[mine], send_sems.at[n_c * n_p]).start()
        token[...] = jnp.zeros_like(token)

    return pl.pallas_call(
        body, name=name,
        out_shape=(pltpu.SemaphoreType.DMA((n_c * n_p + bool(own_slot),)), pltpu.SemaphoreType.DMA((n_c * n_p,)),
                   pltpu.HBM(grad.shape, grad.dtype), pltpu.HBM(land_shape, grad.dtype),
                   jax.ShapeDtypeStruct((8, LANE), F32)),
        in_specs=(HBM, HBM), out_specs=(SEM, SEM, HBM, HBM, pl.BlockSpec(memory_space=pltpu.VMEM)),
        input_output_aliases={0: 2, 1: 3},
        compiler_params=pltpu.CompilerParams(has_side_effects=EFFECT),
    )(pltpu.with_memory_space_constraint(grad, pltpu.HBM),
      pltpu.with_memory_space_constraint(lax.empty(land_shape, grad.dtype), pltpu.HBM))


def _exchange_wait(send_sems, recv_sems, g_thru, land_thru, after, route, tail, name, own_slot=None):
    width = IN_SLAB if tail else g_thru.shape[1]
    n_p = 2 if tail else 1

    def body(g_ref, land_ref, send_sems, recv_sems, after_ref, g_dead, got_ref):
        places = route(*_place())
        for j, (peer, slab, _, slot) in enumerate(places):
            for p, (src, dst) in enumerate(_exchange_pieces(g_ref, land_ref, width, tail)):
                cp = pltpu.make_async_remote_copy(src_ref=src(slab), dst_ref=dst(slot),
                                                  send_sem=send_sems.at[j * n_p + p], recv_sem=recv_sems.at[j * n_p + p],
                                                  device_id=peer, device_id_type=MESH)
                cp.wait_send()
                cp.wait_recv()
        if own_slot:
            mine = own_slot(*_place())
            pltpu.make_async_copy(g_ref.at[mine], land_ref.at[mine], send_sems.at[len(places) * n_p]).wait()

    return pl.pallas_call(
        body, name=name,
        out_shape=(pltpu.HBM(g_thru.shape, g_thru.dtype), pltpu.HBM(land_thru.shape, land_thru.dtype)),
        in_specs=(HBM, HBM, SEM, SEM, pl.BlockSpec(memory_space=pl.ANY)), out_specs=(HBM, HBM),
        input_output_aliases={0: 0, 1: 1},
        compiler_params=pltpu.CompilerParams(has_side_effects=EFFECT),
    )(g_thru, land_thru, send_sems, recv_sems, after)


def _pair_add(grad, pair, core, tail, name):
    rows, cols = (IN_SLAB if tail else grad.shape[1]), grad.shape[-1]
    total = pair.shape[1]

    def body(core_ref, *refs):
        if tail:
            g_ref, t_ref, p_ref, o_ref = refs
            o_ref[0:rows, :] = (g_ref[...].astype(F32) + p_ref[0:rows, :].astype(F32)).astype(BF16)
            o_ref[rows:total, :] = (t_ref[...].astype(F32) + p_ref[rows:total, :].astype(F32)).astype(BF16)
        else:
            g_ref, p_ref, o_ref = refs
            o_ref[...] = (g_ref[...].astype(F32) + p_ref[...].astype(F32)).astype(BF16)

    if tail:
        tc = _fit(cols, 512)
        grid = (N_CHIP, cols // tc)
        slab = pl.BlockSpec((None, total, tc), lambda q, i, core_ref: (q, 0, i))
        per = IN_SLAB // IN_TAIL
        in_specs = [pl.BlockSpec((rows, tc), lambda q, i, core_ref: (2 * q + core_ref[0], i)),
                    pl.BlockSpec((IN_TAIL, tc), lambda q, i, core_ref: ((2 * q + core_ref[0] + 1) * per, i))]
    else:
        tr = _rows(rows, 1024)
        grid = (N_CHIP, rows // tr)
        slab = pl.BlockSpec((None, tr, cols), lambda q, i, core_ref: (q, i, 0))
        in_specs = [pl.BlockSpec((None, tr, cols), lambda q, i, core_ref: (2 * q + core_ref[0], i, 0))]
    return pl.pallas_call(
        body, name=name,
        grid_spec=pltpu.PrefetchScalarGridSpec(num_scalar_prefetch=1, grid=grid,
                                               in_specs=in_specs + [slab], out_specs=slab),
        out_shape=jax.ShapeDtypeStruct(pair.shape, BF16),
        compiler_params=_params("parallel", "parallel"),
    )(core, *([grad, grad] if tail else [grad]), pair)


def _relay_places(x, y, c):
    came_from = (c * (1 - x) + (1 - c) * x, c * y + (1 - c) * (1 - y), c)
    pass_to = (c * x + (1 - c) * (1 - x), c * (1 - y) + (1 - c) * y, c)
    return 2 - c, came_from, pass_to, pass_to


OWN = 4


def _gather_start(blocks, after, name, spare=(), relayed=False):
    n = len(blocks)
    lands = [(N_DEV + (a in spare),) + b.shape for a, b in enumerate(blocks)]

    def body(*refs):
        b_refs, land_refs = refs[:n], refs[n:2 * n]
        send_sems, recv_sems = refs[2 * n + 1:3 * n + 1], refs[3 * n + 1:4 * n + 1]
        token = refs[-1]
        x, y, c = _place()
        me = _index((x, y, c))
        for a in range(n):
            targets = [(x, y, 1 - c)] + [(*chip, c) for chip in _chips(x, y)]
            for k, to in enumerate(targets[:3] if relayed else targets):
                pltpu.make_async_remote_copy(src_ref=b_refs[a], dst_ref=land_refs[a].at[me], send_sem=send_sems[a].at[k],
                                             recv_sem=recv_sems[a].at[k], device_id=to, device_id_type=MESH).start()
        for a in range(n):
            pltpu.make_async_copy(b_refs[a], land_refs[a].at[me], send_sems[a].at[OWN]).start()
        token[...] = jnp.zeros_like(token)

    sems = [pltpu.SemaphoreType.DMA((OWN + 1,))] * n
    out = pl.pallas_call(
        body, name=name,
        out_shape=(*sems, *sems, *[pltpu.HBM(b.shape, b.dtype) for b in blocks],
                   *[pltpu.HBM(s, b.dtype) for s, b in zip(lands, blocks)], jax.ShapeDtypeStruct((8, LANE), F32)),
        in_specs=(*[HBM] * (2 * n), pl.BlockSpec(memory_space=pl.ANY)),
        out_specs=(*[SEM] * (2 * n), *[HBM] * (2 * n), pl.BlockSpec(memory_space=pltpu.VMEM)),
        input_output_aliases={i: 2 * n + i for i in range(2 * n)},
        compiler_params=pltpu.CompilerParams(has_side_effects=EFFECT),
    )(*[pltpu.with_memory_space_constraint(b, pltpu.HBM) for b in blocks],
      *[pltpu.with_memory_space_constraint(lax.empty(s, b.dtype), pltpu.HBM) for s, b in zip(lands, blocks)], after)
    return [(out[a], out[n + a], out[2 * n + a], out[3 * n + a]) for a in range(n)], out[-1]


def _gather_relay(states, after, name):
    n, first_out = len(states), 3 * len(states) + len(after)

    def body(*refs):
        land_refs, send_sems, recv_sems = refs[:n], refs[n:2 * n], refs[2 * n:3 * n]
        pass_send, pass_recv = refs[first_out + n:first_out + 2 * n], refs[first_out + 2 * n:first_out + 3 * n]
        k_in, came_from, pass_to, _ = _relay_places(*_place())
        for a in range(n):
            slot = land_refs[a].at[_index(came_from)]
            pltpu.make_async_remote_copy(src_ref=slot, dst_ref=slot, send_sem=send_sems[a].at[k_in],
                                         recv_sem=recv_sems[a].at[k_in], device_id=came_from,
                                         device_id_type=MESH).wait_recv()
            pltpu.make_async_remote_copy(src_ref=slot, dst_ref=slot, send_sem=pass_send[a].at[0],
                                         recv_sem=pass_recv[a].at[0], device_id=pass_to, device_id_type=MESH).start()
        refs[-1][...] = jnp.zeros_like(refs[-1])

    lands = [st[3] for st in states]
    pair = [pltpu.SemaphoreType.DMA((1,))] * n
    out = pl.pallas_call(
        body, name=name,
        out_shape=(*[pltpu.HBM(l.shape, l.dtype) for l in lands], *pair, *pair, jax.ShapeDtypeStruct((8, LANE), F32)),
        in_specs=(*[HBM] * n, *[SEM] * (2 * n), *[pl.BlockSpec(memory_space=pl.ANY)] * len(after)),
        out_specs=(*[HBM] * n, *[SEM] * (2 * n), pl.BlockSpec(memory_space=pltpu.VMEM)),
        input_output_aliases={a: a for a in range(n)},
        compiler_params=pltpu.CompilerParams(has_side_effects=EFFECT),
    )(*lands, *[st[0] for st in states], *[st[1] for st in states], *after)
    return [(st[0], st[1], st[2], out[a], (out[n + a], out[2 * n + a])) for a, st in enumerate(states)], out[-1]


def _gather_forward(send_sems, recv_sems, b_thru, land_thru, after, name, passed=None):
    relayed = passed is not None

    def body(b_ref, land_ref, send_sems, recv_sems, *rest):
        pass_send, pass_recv = rest[:2] if relayed else (None, None)
        send2, recv2, token = rest[-3:]
        x, y, c = _place()
        sibling = (x, y, 1 - c)
        chips = [(*chip, c) for chip in _chips(x, y)]
        arrivals = [((send_sems.at[j + 1], recv_sems.at[j + 1]), frm, j) for j, frm in enumerate(chips)]
        sends = [send_sems.at[k] for k in range(4)]
        if relayed:
            k_in, came_from, _, other = _relay_places(x, y, c)
            arrivals = [(None, came_from, k_in - 1), ((send_sems.at[3 - k_in], recv_sems.at[3 - k_in]), other, 2 - k_in),
                        ((pass_send.at[0], pass_recv.at[0]), chips[2], 2)]
            sends[3] = pass_send.at[0]
        for sems, frm, j in arrivals:
            slot = land_ref.at[_index(frm)]
            if sems:
                pltpu.make_async_remote_copy(src_ref=b_ref, dst_ref=slot, send_sem=sems[0], recv_sem=sems[1],
                                             device_id=frm, device_id_type=MESH).wait_recv()
            pltpu.make_async_remote_copy(src_ref=slot, dst_ref=slot, send_sem=send2.at[j], recv_sem=recv2.at[j],
                                         device_id=sibling, device_id_type=MESH).start()
        pltpu.make_async_remote_copy(src_ref=b_ref, dst_ref=land_ref.at[_index(sibling)], send_sem=send_sems.at[0],
                                     recv_sem=recv_sems.at[0], device_id=sibling, device_id_type=MESH).wait_recv()
        for sem in sends:
            pltpu.make_async_remote_copy(src_ref=b_ref, dst_ref=land_ref.at[0], send_sem=sem, recv_sem=recv_sems.at[0],
                                         device_id=sibling, device_id_type=MESH).wait_send()
        pltpu.make_async_copy(b_ref, land_ref.at[_index((x, y, c))], send_sems.at[OWN]).wait()
        token[...] = jnp.zeros_like(token)

    extra = list(passed) if relayed else []
    return pl.pallas_call(
        body, name=name,
        out_shape=(pltpu.HBM(b_thru.shape, b_thru.dtype), pltpu.HBM(land_thru.shape, land_thru.dtype),
                   pltpu.SemaphoreType.DMA((3,)), pltpu.SemaphoreType.DMA((3,)), jax.ShapeDtypeStruct((8, LANE), F32)),
        in_specs=(HBM, HBM, SEM, SEM, *[SEM] * len(extra), pl.BlockSpec(memory_space=pl.ANY)),
        out_specs=(HBM, HBM, SEM, SEM, pl.BlockSpec(memory_space=pltpu.VMEM)),
        input_output_aliases={0: 0, 1: 1},
        compiler_params=pltpu.CompilerParams(has_side_effects=EFFECT),
    )(b_thru, land_thru, send_sems, recv_sems, *extra, after)


def _gather_finish(land_thru, send2, recv2, after, name):
    def body(land_ref, send2, recv2, after_ref, land_out):
        x, y, c = _place()
        for j, chip in enumerate(_chips(x, y)):
            cp = pltpu.make_async_remote_copy(src_ref=land_ref.at[_index((*chip, c))],
                                              dst_ref=land_ref.at[_index((*chip, 1 - c))], send_sem=send2.at[j],
                                              recv_sem=recv2.at[j], device_id=(x, y, 1 - c), device_id_type=MESH)
            cp.wait_send()
            cp.wait_recv()

    return pl.pallas_call(
        body, name=name, out_shape=pltpu.HBM(land_thru.shape, land_thru.dtype),
        in_specs=(HBM, SEM, SEM, pl.BlockSpec(memory_space=pl.ANY)), out_specs=HBM,
        input_output_aliases={0: 0},
        compiler_params=pltpu.CompilerParams(has_side_effects=EFFECT),
    )(land_thru, send2, recv2, after)


class _Gathering:
    def __init__(self, ahead, later, me):
        cast = [a.astype(BF16) for a in ahead.values()]
        started, self.token = _gather_start(cast, cast[0], "gather1_ahead", relayed=True)
        self.me, self.state, self.relayed, self.later = me, dict(zip(ahead, started)), tuple(ahead), later

    def start_first(self, first, spare):
        started, self.token = _gather_start(list(first.values()), self.token, "gather1_first",
                                            spare=(list(first).index(spare),), relayed=True)
        self.state.update(zip(first, started))
        self.relayed += tuple(first)

    def begin(self, after):
        return self.token

    def relay(self, *after):
        states, token = _gather_relay([self.state[n] for n in self.relayed], after, "gather_relay")
        self.state.update(zip(self.relayed, states))
        cast = [_behind(a, token).astype(BF16) for a in self.later.values()]
        started, self.token = _gather_start(cast, token, "gather1_later")
        self.state.update(zip(self.later, started))
        return self.token

    def forward(self, name, after):
        first_leg, passed = self.state[name][:4], (self.state[name][4:] or (None,))[0]
        *self.state[name], token = _gather_forward(*first_leg, after, "gather2_" + name, passed=passed)
        return token

    def get(self, name, after):
        _, land, send2, recv2 = self.state[name]
        land = _gather_finish(land, send2, recv2, after, "gather3_" + name)
        return land if name not in ("w_out", "w_down") else land.reshape(-1, land.shape[2])


class _Reducing:
    def __init__(self, core, chip, small_start, small_finish):
        self.core, self.chip, self.state, self.token = core, chip, {}, None
        self.small_start, self.small_finish = small_start, small_finish

    def early(self, small, loss, after):
        self.small_state, token = self.small_start(small, loss, after)
        return token

    def meanwhile(self, after):
        self.small_sum = self.small_finish(self.small_state, after)
        return self.small_sum

    def start(self, name, grad):
        tail = name == "w_in"
        g = grad if tail or grad.ndim == 3 else grad.reshape(N_DEV, grad.shape[0] // N_DEV, grad.shape[1])
        *self.state[name], token = _exchange_start(g, _pair_route, tail, "pair_send_" + name)
        return token

    def relay(self, name, after):
        tail = name == "w_in"
        grad, pair = _exchange_wait(*self.state[name], after, _pair_route, tail, "pair_recv_" + name)
        total = _pair_add(grad, pair, self.core, tail, "pair_add_" + name)
        *self.state[name], self.token = _exchange_start(total, _chip_route, False, "chip_send_" + name,
                                                        own_slot=_chip_slot)
        return self.token

    def finish(self, name, after):
        _, land = _exchange_wait(*self.state[name], after, _chip_route, False, "chip_recv_" + name, own_slot=_chip_slot)
        return land


def _carry_w_in(main, tail):
    slabs, _, d = main.shape
    tc = _fit(d, 2048)
    assert slabs == N_DEV + 1 and tail.shape[:2] == (N_DEV, IN_TAIL), (main.shape, tail.shape)
    top = lambda off: pl.BlockSpec((None, IN_TAIL, tc), lambda s, j: (s + off, 0, j))

    def carry(m_ref, t_ref, o_ref):
        o_ref[...] = m_ref[...] + t_ref[...]

    main = pl.pallas_call(
        carry, name="carry_w_in", grid=(N_DEV - 1, d // tc), in_specs=[top(1), top(0)], out_specs=top(1),
        out_shape=jax.ShapeDtypeStruct(main.shape, main.dtype), input_output_aliases={0: 0},
        compiler_params=_params("parallel", "parallel"),
    )(main, tail)

    def last(m_ref, t_ref, o_ref):
        o_ref[...] = jnp.zeros_like(o_ref)
        o_ref[0:IN_TAIL, :] = t_ref[...]

    return pl.pallas_call(
        last, name="last_slab_w_in", grid=(d // tc,),
        in_specs=[pl.BlockSpec(memory_space=pl.ANY), pl.BlockSpec((None, IN_TAIL, tc), lambda j: (N_DEV - 1, 0, j))],
        out_specs=pl.BlockSpec((None, LANE, tc), lambda j: (N_DEV, 0, j)),
        out_shape=jax.ShapeDtypeStruct(main.shape, main.dtype), input_output_aliases={0: 0},
        compiler_params=_params("parallel"),
    )(main, tail)


def _rows(n, want):
    t = min(n, want)
    t -= t % 16
    while n % t:
        t -= 16
    return t


def _adam_math(w, g, m, v):
    m2 = ADAM_B1 * m + (1.0 - ADAM_B1) * g
    v2 = ADAM_B2 * v + (1.0 - ADAM_B2) * (g * g)
    m_hat = m2 * (1.0 / (1.0 - ADAM_B1 ** ADAM_STEP))
    v_hat = v2 * (1.0 / (1.0 - ADAM_B2 ** ADAM_STEP))
    return -ADAM_LR * (m_hat / (jnp.sqrt(v_hat) + ADAM_EPS) + ADAM_WD * w), m2, v2


def _slot_sum(r_ref):
    acc = r_ref[0].astype(F32)
    for i in range(1, r_ref.shape[0]):
        acc = acc + r_ref[i].astype(F32)
    return acc


def _shift_w_in(w):
    ws, d = w.shape
    tc = _fit(d, 256)

    def body(w_ref, main_ref, tail_ref, tall):
        tall[...] = jnp.zeros_like(tall)
        tall[0:ws, :] = w_ref[...]
        moved = pltpu.roll(tall[...], _index(_place()), 0).astype(BF16)
        main_ref[...] = moved[0:IN_SLAB]
        tail_ref[...] = moved[IN_SLAB:]

    return pl.pallas_call(
        body, name="shift_w_in", grid=(d // tc,),
        in_specs=[pl.BlockSpec((ws, tc), lambda j: (0, j))],
        out_specs=[pl.BlockSpec((IN_SLAB, tc), lambda j: (0, j)), pl.BlockSpec((IN_TAIL, tc), lambda j: (0, j))],
        out_shape=[jax.ShapeDtypeStruct((IN_SLAB, d), BF16), jax.ShapeDtypeStruct((IN_TAIL, d), BF16)],
        scratch_shapes=[pltpu.VMEM((IN_SLAB + IN_TAIL, tc), F32)], compiler_params=_params("parallel"),
    )(w)


def _sum_adamw_shifted(r, w, m, v, name):
    _, ph, d = r.shape
    ws = w.shape[0]
    tc = _fit(d, 256)

    def body(r_ref, w_ref, m_ref, v_ref, g_ref, d_ref, m2_ref, v2_ref, tall):
        tall[...] = pltpu.roll(_slot_sum(r_ref), lax.rem(ph - _index(_place()), ph), 0)
        g = tall[0:ws, :]
        g_ref[...] = g
        d_ref[...], m2_ref[...], v2_ref[...] = _adam_math(w_ref[...], g, m_ref[...], v_ref[...])

    blk = pl.BlockSpec((ws, tc), lambda j: (0, j))
    out = jax.ShapeDtypeStruct(w.shape, F32)
    return pl.pallas_call(
        body, name=name, grid=(d // tc,),
        in_specs=[pl.BlockSpec((r.shape[0], ph, tc), lambda j: (0, 0, j)), blk, blk, blk],
        out_specs=[blk] * 4, out_shape=[out] * 4,
        scratch_shapes=[pltpu.VMEM((ph, tc), F32)], compiler_params=_params("parallel"),
    )(r, w, m, v)


def _sum_slots(r, name, tr=128):
    _, rows, cols = r.shape
    tr = _rows(rows, tr)

    def body(r_ref, g_ref):
        g_ref[...] = _slot_sum(r_ref)

    return pl.pallas_call(
        body, name=name, grid=(rows // tr,),
        in_specs=[pl.BlockSpec((r.shape[0], tr, cols), lambda i: (0, i, 0))],
        out_specs=pl.BlockSpec((tr, cols), lambda i: (i, 0)),
        out_shape=jax.ShapeDtypeStruct((rows, cols), F32),
        compiler_params=_params("parallel"),
    )(r)


def _adamw(w, g, m, v, name, tr=256):
    rows, cols = w.shape
    tr = _rows(rows, tr)

    def body(w_ref, g_ref, m_ref, v_ref, d_ref, m2_ref, v2_ref):
        d_ref[...], m2_ref[...], v2_ref[...] = _adam_math(w_ref[...], g_ref[...], m_ref[...], v_ref[...])

    blk = pl.BlockSpec((tr, cols), lambda i: (i, 0))
    out = jax.ShapeDtypeStruct((rows, cols), F32)
    return pl.pallas_call(
        body, name=name, grid=(rows // tr,), in_specs=[blk] * 4, out_specs=[blk] * 3, out_shape=[out] * 3,
        compiler_params=_params("parallel"),
    )(w, g, m, v)


def _sum_adamw(r, w, m, v, name, tr=256):
    rows, cols = w.shape
    tr = _rows(rows, tr)

    def body(r_ref, w_ref, m_ref, v_ref, g_ref, d_ref, m2_ref, v2_ref):
        g = _slot_sum(r_ref)
        g_ref[...] = g
        d_ref[...], m2_ref[...], v2_ref[...] = _adam_math(w_ref[...], g, m_ref[...], v_ref[...])

    blk = pl.BlockSpec((tr, cols), lambda i: (i, 0))
    out = jax.ShapeDtypeStruct((rows, cols), F32)
    return pl.pallas_call(
        body, name=name, grid=(rows // tr,),
        in_specs=[pl.BlockSpec((r.shape[0], tr, cols), lambda i: (0, i, 0)), blk, blk, blk],
        out_specs=[blk] * 4, out_shape=[out] * 4,
        compiler_params=_params("parallel"),
    )(r, w, m, v)


def _pack(pieces, sizes):
    flat = [jnp.pad(p.reshape(-1).astype(F32), (0, s - p.size)) for p, s in zip(pieces, sizes)]
    total = sum(sizes)
    padded = -(-total // (16 * LANE)) * (16 * LANE)
    return jnp.pad(jnp.concatenate(flat), (0, padded - total)).reshape(-1, LANE)


def _unpack(packed, shapes, sizes):
    flat = packed.reshape(-1)
    out, off = [], 0
    for shp, s in zip(shapes, sizes):
        n = 1
        for k in shp:
            n *= k
        out.append(flat[off:off + n].reshape(shp))
        off += s
    return out


def _lanes(n):
    return -(-n // LANE) * LANE


WEIGHTS = ("w_in", "b_gates", "w_sc_conv", "mh_gain", "w_out", "ln1_g", "ln1_b", "w_up", "w_ffn_conv", "b_ffn_conv",
           "w_down", "ln2_g", "ln2_b")
BIG = ("w_in", "w_out", "w_up", "w_down")
SMALL = tuple(n for n in WEIGHTS if n not in BIG)


def kernel(x, w_in, b_gates, w_sc_conv, mh_gain, w_out, ln1_g, ln1_b, w_up, w_ffn_conv, b_ffn_conv, w_down, ln2_g, ln2_b, loss_target, m_w_in, m_b_gates, m_w_sc_conv, m_mh_gain, m_w_out, m_ln1_g, m_ln1_b, m_w_up, m_w_ffn_conv, m_b_ffn_conv, m_w_down, m_ln2_g, m_ln2_b, v_w_in, v_b_gates, v_w_sc_conv, v_mh_gain, v_w_out, v_ln1_g, v_ln1_b, v_w_up, v_w_ffn_conv, v_b_ffn_conv, v_w_down, v_ln2_g, v_ln2_b):
    w = dict(zip(WEIGHTS, (w_in, b_gates, w_sc_conv, mh_gain, w_out, ln1_g, ln1_b, w_up, w_ffn_conv, b_ffn_conv,
                           w_down, ln2_g, ln2_b)))
    m = dict(zip(WEIGHTS, (m_w_in, m_b_gates, m_w_sc_conv, m_mh_gain, m_w_out, m_ln1_g, m_ln1_b, m_w_up,
                           m_w_ffn_conv, m_b_ffn_conv, m_w_down, m_ln2_g, m_ln2_b)))
    v = dict(zip(WEIGHTS, (v_w_in, v_b_gates, v_w_sc_conv, v_mh_gain, v_w_out, v_ln1_g, v_ln1_b, v_w_up,
                           v_w_ffn_conv, v_b_ffn_conv, v_w_down, v_ln2_g, v_ln2_b)))
    me = _index(_place())
    d = x.shape[2]
    ws_in = w_in.shape[2]
    assert ws_in == IN_SLAB + 1 and N_DEV <= LANE, w_in.shape
    ninp = (N_DEV + 1) * IN_SLAB
    ws_sc, ws_fc = w_sc_conv.shape[2], w_ffn_conv.shape[2]

    wx = _Gathering({"w_out": w_out[0]}, {n: w[n][0] for n in ("w_up", "w_down")}, me)
    w_in_t = jnp.transpose(_behind(w_in[0], wx.token))
    w_in_main, w_in_tail = _shift_w_in(w_in_t)
    taps8 = lambda a: jnp.pad(a[0], ((0, 5), (0, 0)))
    at_once = ("w_sc", "w_fc", "w_tail", "w_in")
    wx.start_first(dict(zip(at_once, (taps8(w_sc_conv), taps8(w_ffn_conv), w_in_tail, w_in_main))), spare="w_in")
    x_b = _behind(x[0], wx.begin(None)).astype(BF16)
    m_in_t, v_in_t = (jnp.transpose(_behind(a[0], wx.begin(None))) for a in (m_w_in, v_w_in))
    token = wx.relay(x_b, m_in_t, v_in_t)
    for n in at_once:
        token = wx.forward(n, token)
    g_sc, g_fc, g_tail, g_in = (wx.get(n, token) for n in at_once)
    w_in_full = _carry_w_in(g_in, g_tail).reshape(ninp, d)
    w_sc_full = g_sc[:, :3].transpose(1, 0, 2).reshape(3, N_DEV * ws_sc)
    w_fc_full = g_fc[:, :3].transpose(1, 0, 2).reshape(3, N_DEV * ws_fc)

    xi, yi, ci = _place()
    names = ("loss",) + SMALL
    pieces = {}

    def small_start(small, loss_t, after):
        pieces.update(small, loss=loss_t[0, :1])
        sizes = [_lanes(pieces[n].size) for n in names]
        (state,), token = _gather_start([_pack([pieces[n] for n in names], sizes)], after, "gather1_small")
        return state, token

    def small_finish(state, after):
        _, land, send2, recv2, token = _gather_forward(*state, after, "gather2_small")
        land = _gather_finish(land, send2, recv2, token, "gather3_small")
        return _sum_slots(land, "sum_small", tr=land.shape[1])

    gx = _Reducing(jnp.reshape(ci, (1,)).astype(jnp.int32), 2 * xi + yi, small_start, small_finish)
    loss_t, grad_x, small, _ = _local_step(
        x[0], loss_target[0], w_in_full, b_gates, w_sc_full, mh_gain, None, ln1_g, ln1_b, None,
        w_fc_full, b_ffn_conv, None, ln2_g, ln2_b, gx=gx, wx=wx, x_b=x_b)

    grads, deltas, new_m, new_v = {}, {}, {}, {}
    for name in ("w_down", "w_up", "w_out"):
        grads[name], deltas[name], new_m[name], new_v[name] = _sum_adamw(
            gx.finish(name, gx.token), w[name][0], m[name][0], v[name][0], "adamw_" + name)

    summed = _unpack(_behind(gx.small_sum, gx.token), [pieces[n].shape for n in names],
                     [_lanes(pieces[n].size) for n in names])
    full = dict(zip(names, summed))
    full["w_sc_conv"] = lax.dynamic_slice(full["w_sc_conv"], (0, me * ws_sc), (3, ws_sc))
    full["w_ffn_conv"] = lax.dynamic_slice(full["w_ffn_conv"], (0, me * ws_fc), (3, ws_fc))
    for n in SMALL:
        grads[n] = full[n].reshape(w[n].shape)
    sizes = [_lanes(w[n].size) for n in SMALL]
    shapes = [w[n].shape for n in SMALL]
    packed = [_pack([t[n] for n in SMALL], sizes) for t in (w, grads, m, v)]
    small_out = _adamw(*packed, "adamw_small")
    for res, t in zip(small_out, (deltas, new_m, new_v)):
        t.update(zip(SMALL, _unpack(res, shapes, sizes)))

    done = sum(t[0:1, 0:1] for t in (grad_x, deltas["w_down"], deltas["w_up"], deltas["w_out"], small_out[0]))
    grads["w_in"], deltas["w_in"], new_m["w_in"], new_v["w_in"] = (
        jnp.transpose(a)[None] for a in _sum_adamw_shifted(gx.finish("w_in", done), w_in_t, m_in_t, v_in_t, "adamw_w_in"))

    big = lambda t: {n: (t[n].reshape(w[n].shape) if n in BIG else t[n]) for n in WEIGHTS}
    grads, deltas, new_m, new_v = big(grads), big(deltas), big(new_m), big(new_v)
    return (full["loss"].reshape(()), grad_x[None], *[grads[n] for n in WEIGHTS], *[deltas[n] for n in WEIGHTS],
            *[new_m[n] for n in WEIGHTS], *[new_v[n] for n in WEIGHTS])
```
